```python
import math
import jax
import jax.numpy as jnp
from jax import lax
import numpy as np

D_MODEL = 1024
BATCH = 8
SEQ = 4096
DEPTH = 1

HEAD_DIM = 64
A_Q_HEADS = 8
A_KV_HEADS = 2
A_WINDOW = 128
B_GROUPS = ((128, 1), (512, 4), (2048, 16))
B_HEADS_PER_GROUP = 8
A_WIDTH = A_Q_HEADS * HEAD_DIM
B_WIDTH = B_HEADS_PER_GROUP * HEAD_DIM
N_BRANCHES = 2
N_BUCKETS = 32
MAX_DISTANCE = 1024
REL_HEADS = A_Q_HEADS + len(B_GROUPS) * B_HEADS_PER_GROUP
PROJ_SIZES = (A_WIDTH, A_KV_HEADS * HEAD_DIM, A_KV_HEADS * HEAD_DIM,
              len(B_GROUPS) * B_WIDTH, len(B_GROUPS) * B_WIDTH, len(B_GROUPS) * B_WIDTH,
              A_WIDTH, B_WIDTH, N_BRANCHES * D_MODEL)
IN_WIDTH = sum(PROJ_SIZES)
EPS = 1e-6
NEG_INF = -1e30

kernel_name = "hybrid_gated_window_dilated_attention_block"


def rms_norm(x, gain):
    xf = x.astype(jnp.float32)
    y = xf * lax.rsqrt(jnp.mean(xf * xf, axis=-1, keepdims=True) + EPS)
    return (y * gain.astype(jnp.float32)).astype(x.dtype)


def t5_bucket(rel):
    half = N_BUCKETS // 2
    max_exact = half // 2
    ret = (rel > 0).astype(jnp.int32) * half
    n = jnp.abs(rel)
    nf = jnp.maximum(n, max_exact).astype(jnp.float32)
    large = max_exact + (jnp.log(nf / max_exact) / math.log(MAX_DISTANCE / max_exact)
                         * (half - max_exact)).astype(jnp.int32)
    large = jnp.minimum(large, half - 1)
    return ret + jnp.where(n < max_exact, n, large)


def banded_attention(q, k, v, half_window, stride, bias_table, sink):
    bsz, L, H, dh = q.shape
    KV = k.shape[2]
    G = H // KV
    blk = half_window
    nb = -(-L // blk)
    Lp = nb * blk
    pad = Lp - L
    qb = jnp.pad(q, ((0, 0), (0, pad), (0, 0), (0, 0))).reshape(bsz, nb, blk, KV, G, dh)

    def windows(t):
        tp = jnp.pad(t, ((0, 0), (blk, blk + pad), (0, 0), (0, 0))).reshape(bsz, nb + 2, blk, KV, dh)
        return jnp.concatenate([tp[:, :-2], tp[:, 1:-1], tp[:, 2:]], axis=2)

    kw = windows(k)
    vw = windows(v)
    scores = jnp.einsum('bnqkgd,bnmkd->bkgnqm', qb, kw).astype(jnp.float32)

    qi = jnp.arange(blk)
    mi = jnp.arange(3 * blk)
    rel = mi[None, :] - blk - qi[:, None]
    bias = bias_table[t5_bucket(rel * stride)]
    bias = bias.transpose(2, 0, 1).reshape(KV, G, 1, blk, 3 * blk).astype(jnp.float32)
    blocks = jnp.arange(nb)[:, None, None] * blk
    qpos = blocks + qi[None, :, None]
    kpos = blocks - blk + mi[None, None, :]
    valid = (kpos >= 0) & (kpos < L) & (jnp.abs(kpos - qpos) <= half_window)

    logits = jnp.where(valid, scores + bias, NEG_INF)
    lse = jax.nn.logsumexp(logits, axis=-1)
    if sink is not None:
        lse = jnp.logaddexp(lse, sink.astype(jnp.float32).reshape(KV, G, 1, 1))
    probs = jnp.exp(logits - lse[..., None])
    out = jnp.einsum('bkgnqm,bnmkd->bnqkgd', probs.astype(v.dtype), vw)
    out = out.reshape(bsz, Lp, H, dh)[:, :L]
    lse = lse.transpose(0, 3, 4, 1, 2).reshape(bsz, Lp, H)[:, :L]
    return out, lse


def dilated_group(q, k, v, window, dilation, bias_table):
    bsz, S, H, dh = q.shape
    L = S // dilation

    def to_sub(t):
        return t.reshape(bsz, L, dilation, H, dh).transpose(0, 2, 1, 3, 4).reshape(bsz * dilation, L, H, dh)

    out, lse = banded_attention(to_sub(q), to_sub(k), to_sub(v), window // (2 * dilation),
                                dilation, bias_table, None)
    out = out.reshape(bsz, dilation, L, H, dh).transpose(0, 2, 1, 3, 4).reshape(bsz, S, H, dh)
    lse = lse.reshape(bsz, dilation, L, H).transpose(0, 2, 1, 3).reshape(bsz, S, H)
    return out, lse


def head_rms_norm(t, gain):
    return rms_norm(t, gain)


def _fwd_setup_inputs(seed: int = 0) -> dict:
    key = jax.random.key(seed)
    ks = jax.random.split(key, 16)
    f32 = jnp.float32
    x = jax.random.normal(ks[0], (BATCH, SEQ, D_MODEL), f32)
    norm_gain = 1.0 + 0.02 * jax.random.normal(ks[1], (DEPTH, D_MODEL), f32)
    w_in = jax.random.normal(ks[2], (DEPTH, D_MODEL, IN_WIDTH), f32) * D_MODEL ** -0.5
    q_norm_a = 1.0 + 0.02 * jax.random.normal(ks[3], (DEPTH, HEAD_DIM), f32)
    k_norm_a = 1.0 + 0.02 * jax.random.normal(ks[4], (DEPTH, HEAD_DIM), f32)
    q_norm_b = 1.0 + 0.02 * jax.random.normal(ks[5], (DEPTH, HEAD_DIM), f32)
    k_norm_b = 1.0 + 0.02 * jax.random.normal(ks[6], (DEPTH, HEAD_DIM), f32)
    sink_a = 0.5 * jax.random.normal(ks[7], (DEPTH, A_Q_HEADS), f32)
    rel_bias = 0.5 * jax.random.normal(ks[8], (N_BUCKETS, REL_HEADS), f32)
    w_branch_a = jax.random.normal(ks[9], (DEPTH, A_WIDTH, D_MODEL), f32) * A_WIDTH ** -0.5
    w_branch_b = jax.random.normal(ks[10], (DEPTH, B_WIDTH, D_MODEL), f32) * B_WIDTH ** -0.5
    b_merge = 0.1 * jax.random.normal(ks[11], (DEPTH, N_BRANCHES, D_MODEL), f32)
    w_out = jax.random.normal(ks[12], (DEPTH, D_MODEL, D_MODEL), f32) * D_MODEL ** -0.5
    return {"x": x, "norm_gain": norm_gain, "w_in": w_in, "q_norm_a": q_norm_a,
            "k_norm_a": k_norm_a, "q_norm_b": q_norm_b, "k_norm_b": k_norm_b,
            "sink_a": sink_a, "rel_bias": rel_bias, "w_branch_a": w_branch_a,
            "w_branch_b": w_branch_b, "b_merge": b_merge, "w_out": w_out}


def _fwd_reference(x, norm_gain, w_in, q_norm_a, k_norm_a, q_norm_b, k_norm_b, sink_a,
              rel_bias, w_branch_a, w_branch_b, b_merge, w_out):
    bsz, S, D = x.shape
    n_groups = len(B_GROUPS)
    scale = HEAD_DIM ** -0.5
    split_idx = [int(i) for i in np.cumsum(PROJ_SIZES)[:-1]]
    for layer in range(DEPTH):
        h = rms_norm(x, norm_gain[layer])
        proj = h @ w_in[layer]
        qa, ka, va, qb, kb, vb, ga, gb, mg = jnp.split(proj, split_idx, axis=-1)

        qa = head_rms_norm(qa.reshape(bsz, S, A_Q_HEADS, HEAD_DIM), q_norm_a[layer]) * scale
        ka = head_rms_norm(ka.reshape(bsz, S, A_KV_HEADS, HEAD_DIM), k_norm_a[layer])
        va = va.reshape(bsz, S, A_KV_HEADS, HEAD_DIM)
        ya, _ = banded_attention(qa, ka, va, A_WINDOW, 1, rel_bias[:, :A_Q_HEADS], sink_a[layer])
        ya = ya.reshape(bsz, S, A_WIDTH) * jax.nn.silu(ga)

        qb = head_rms_norm(qb.reshape(bsz, S, n_groups, B_HEADS_PER_GROUP, HEAD_DIM), q_norm_b[layer]) * scale
        kb = head_rms_norm(kb.reshape(bsz, S, n_groups, B_HEADS_PER_GROUP, HEAD_DIM), k_norm_b[layer])
        vb = vb.reshape(bsz, S, n_groups, B_HEADS_PER_GROUP, HEAD_DIM)
        outs = []
        lses = []
        for g, (window, dilation) in enumerate(B_GROUPS):
            c0 = A_Q_HEADS + g * B_HEADS_PER_GROUP
            o, l = dilated_group(qb[:, :, g], kb[:, :, g], vb[:, :, g], window, dilation,
                                 rel_bias[:, c0:c0 + B_HEADS_PER_GROUP])
            outs.append(o)
            lses.append(l)
        alpha = jax.nn.softmax(jnp.stack(lses, axis=0), axis=0)
        yb = jnp.sum(alpha[..., None].astype(x.dtype) * jnp.stack(outs, axis=0), axis=0)
        yb = yb.reshape(bsz, S, B_WIDTH) * jax.nn.silu(gb)

        br_a = ya @ w_branch_a[layer]
        br_b = yb @ w_branch_b[layer]
        gates = jax.nn.sigmoid(mg.reshape(bsz, S, N_BRANCHES, D).astype(jnp.float32)
                               + b_merge[layer].astype(jnp.float32)).astype(x.dtype)
        merged = gates[:, :, 0] * br_a + gates[:, :, 1] * br_b
        x = x + merged @ w_out[layer]
    return x


import jax as _jax
import jax.numpy as _jnp

TWIN_FORMAT = 'train_step'
FWD_PARAMS = ['x', 'norm_gain', 'w_in', 'q_norm_a', 'k_norm_a', 'q_norm_b', 'k_norm_b', 'sink_a', 'rel_bias', 'w_branch_a', 'w_branch_b', 'b_merge', 'w_out']
TWIN_WEIGHTS = ['norm_gain', 'w_in', 'q_norm_a', 'k_norm_a', 'q_norm_b', 'k_norm_b', 'sink_a', 'rel_bias', 'w_branch_a', 'w_branch_b', 'b_merge', 'w_out']
TWIN_DIFF_INPUT = 'x'
TWIN_INPUTS = ['x', 'norm_gain', 'w_in', 'q_norm_a', 'k_norm_a', 'q_norm_b', 'k_norm_b', 'sink_a', 'rel_bias', 'w_branch_a', 'w_branch_b', 'b_merge', 'w_out', 'loss_target', 'm_norm_gain', 'm_w_in', 'm_q_norm_a', 'm_k_norm_a', 'm_q_norm_b', 'm_k_norm_b', 'm_sink_a', 'm_rel_bias', 'm_w_branch_a', 'm_w_branch_b', 'm_b_merge', 'm_w_out', 'v_norm_gain', 'v_w_in', 'v_q_norm_a', 'v_k_norm_a', 'v_q_norm_b', 'v_k_norm_b', 'v_sink_a', 'v_rel_bias', 'v_w_branch_a', 'v_w_branch_b', 'v_b_merge', 'v_w_out']
TWIN_OUTPUTS = ['loss', 'grad_x', 'grad_norm_gain', 'grad_w_in', 'grad_q_norm_a', 'grad_k_norm_a', 'grad_q_norm_b', 'grad_k_norm_b', 'grad_sink_a', 'grad_rel_bias', 'grad_w_branch_a', 'grad_w_branch_b', 'grad_b_merge', 'grad_w_out', 'delta_norm_gain', 'delta_w_in', 'delta_q_norm_a', 'delta_k_norm_a', 'delta_q_norm_b', 'delta_k_norm_b', 'delta_sink_a', 'delta_rel_bias', 'delta_w_branch_a', 'delta_w_branch_b', 'delta_b_merge', 'delta_w_out', 'new_m_norm_gain', 'new_m_w_in', 'new_m_q_norm_a', 'new_m_k_norm_a', 'new_m_q_norm_b', 'new_m_k_norm_b', 'new_m_sink_a', 'new_m_rel_bias', 'new_m_w_branch_a', 'new_m_w_branch_b', 'new_m_b_merge', 'new_m_w_out', 'new_v_norm_gain', 'new_v_w_in', 'new_v_q_norm_a', 'new_v_k_norm_a', 'new_v_q_norm_b', 'new_v_k_norm_b', 'new_v_sink_a', 'new_v_rel_bias', 'new_v_w_branch_a', 'new_v_w_branch_b', 'new_v_b_merge', 'new_v_w_out']
TWIN_LEAF_KINDS = {'loss': 'loss', 'grad_x': 'grad_x', 'grad_norm_gain': 'grad_w', 'grad_w_in': 'grad_w', 'grad_q_norm_a': 'grad_w', 'grad_k_norm_a': 'grad_w', 'grad_q_norm_b': 'grad_w', 'grad_k_norm_b': 'grad_w', 'grad_sink_a': 'grad_w', 'grad_rel_bias': 'grad_w', 'grad_w_branch_a': 'grad_w', 'grad_w_branch_b': 'grad_w', 'grad_b_merge': 'grad_w', 'grad_w_out': 'grad_w', 'delta_norm_gain': 'delta_w', 'delta_w_in': 'delta_w', 'delta_q_norm_a': 'delta_w', 'delta_k_norm_a': 'delta_w', 'delta_q_norm_b': 'delta_w', 'delta_k_norm_b': 'delta_w', 'delta_sink_a': 'delta_w', 'delta_rel_bias': 'delta_w', 'delta_w_branch_a': 'delta_w', 'delta_w_branch_b': 'delta_w', 'delta_b_merge': 'delta_w', 'delta_w_out': 'delta_w', 'new_m_norm_gain': 'new_m', 'new_m_w_in': 'new_m', 'new_m_q_norm_a': 'new_m', 'new_m_k_norm_a': 'new_m', 'new_m_q_norm_b': 'new_m', 'new_m_k_norm_b': 'new_m', 'new_m_sink_a': 'new_m', 'new_m_rel_bias': 'new_m', 'new_m_w_branch_a': 'new_m', 'new_m_w_branch_b': 'new_m', 'new_m_b_merge': 'new_m', 'new_m_w_out': 'new_m', 'new_v_norm_gain': 'new_v', 'new_v_w_in': 'new_v', 'new_v_q_norm_a': 'new_v', 'new_v_k_norm_a': 'new_v', 'new_v_q_norm_b': 'new_v', 'new_v_k_norm_b': 'new_v', 'new_v_sink_a': 'new_v', 'new_v_rel_bias': 'new_v', 'new_v_w_branch_a': 'new_v', 'new_v_w_branch_b': 'new_v', 'new_v_b_merge': 'new_v', 'new_v_w_out': 'new_v'}


def _forward(args):
    return _fwd_reference(*[args[k] for k in FWD_PARAMS])


def _output_shape():
    out = _jax.eval_shape(lambda: _forward(_fwd_setup_inputs(0)))
    return out.shape, out.dtype

N_MICROBATCH = 1
ADAM_LR = 0.001
ADAM_B1 = 0.9
ADAM_B2 = 0.999
ADAM_EPS = 1e-08
ADAM_WD = 0.01
ADAM_STEP = 10
PER_EXAMPLE_BATCH_AXIS = {'x': 0, 'loss_target': 0}
SHARED_INPUTS = []
_WEIGHT_DTYPES = {'norm_gain': _jnp.float32, 'w_in': _jnp.float32, 'q_norm_a': _jnp.float32, 'k_norm_a': _jnp.float32, 'q_norm_b': _jnp.float32, 'k_norm_b': _jnp.float32, 'sink_a': _jnp.float32, 'rel_bias': _jnp.float32, 'w_branch_a': _jnp.float32, 'w_branch_b': _jnp.float32, 'b_merge': _jnp.float32, 'w_out': _jnp.float32}
MOMENT_SCALE = {'norm_gain': 1.027707e-01, 'w_in': 1.057557e-02, 'q_norm_a': 4.741677e-01, 'k_norm_a': 4.744837e-01, 'q_norm_b': 2.674065e-01, 'k_norm_b': 2.670910e-01, 'sink_a': 5.842438e-03, 'rel_bias': 4.164830e-02, 'w_branch_a': 1.062124e-02, 'w_branch_b': 1.049599e-02, 'b_merge': 7.822669e-03, 'w_out': 1.491332e-02}


def _to_microbatches(a, axis):
    t = _jnp.moveaxis(a, axis, 0)
    t = t.reshape((N_MICROBATCH, t.shape[0] // N_MICROBATCH) + t.shape[1:])
    return _jnp.moveaxis(t, 1, axis + 1)


def setup_inputs(seed: int = 0) -> dict:
    inp = _fwd_setup_inputs(seed)
    key = _jax.random.fold_in(_jax.random.key(seed), 7919)
    shape, _ = _output_shape()
    out = dict(inp)
    out["loss_target"] = _jax.random.normal(_jax.random.fold_in(key, 0), shape, _jnp.float32)
    for i, name in enumerate(TWIN_WEIGHTS):
        w = inp[name].astype(_jnp.float32)
        if MOMENT_SCALE is None:
            s = _jnp.sqrt(_jnp.mean(_jnp.square(w)) + 1e-30)
        else:
            s = MOMENT_SCALE[name]
        km, kv = _jax.random.split(_jax.random.fold_in(key, i + 1))
        out[name] = w
        out["m_" + name] = s * _jax.random.normal(km, w.shape, _jnp.float32)
        out["v_" + name] = (s * s) * _jax.random.uniform(kv, w.shape, _jnp.float32, 0.5, 1.5)
    if N_MICROBATCH > 1:
        for name, axis in PER_EXAMPLE_BATCH_AXIS.items():
            out[name] = _to_microbatches(out[name], axis)
    return {'x': out['x'], 'norm_gain': out['norm_gain'], 'w_in': out['w_in'], 'q_norm_a': out['q_norm_a'], 'k_norm_a': out['k_norm_a'], 'q_norm_b': out['q_norm_b'], 'k_norm_b': out['k_norm_b'], 'sink_a': out['sink_a'], 'rel_bias': out['rel_bias'], 'w_branch_a': out['w_branch_a'], 'w_branch_b': out['w_branch_b'], 'b_merge': out['b_merge'], 'w_out': out['w_out'], 'loss_target': out['loss_target'], 'm_norm_gain': out['m_norm_gain'], 'm_w_in': out['m_w_in'], 'm_q_norm_a': out['m_q_norm_a'], 'm_k_norm_a': out['m_k_norm_a'], 'm_q_norm_b': out['m_q_norm_b'], 'm_k_norm_b': out['m_k_norm_b'], 'm_sink_a': out['m_sink_a'], 'm_rel_bias': out['m_rel_bias'], 'm_w_branch_a': out['m_w_branch_a'], 'm_w_branch_b': out['m_w_branch_b'], 'm_b_merge': out['m_b_merge'], 'm_w_out': out['m_w_out'], 'v_norm_gain': out['v_norm_gain'], 'v_w_in': out['v_w_in'], 'v_q_norm_a': out['v_q_norm_a'], 'v_k_norm_a': out['v_k_norm_a'], 'v_q_norm_b': out['v_q_norm_b'], 'v_k_norm_b': out['v_k_norm_b'], 'v_sink_a': out['v_sink_a'], 'v_rel_bias': out['v_rel_bias'], 'v_w_branch_a': out['v_w_branch_a'], 'v_w_branch_b': out['v_w_branch_b'], 'v_b_merge': out['v_b_merge'], 'v_w_out': out['v_w_out']}


def _loss(weights, diff, rest, loss_target):
    with _jax.named_scope("forward"):
        args = {**rest, TWIN_DIFF_INPUT: diff, **{k: w.astype(_WEIGHT_DTYPES[k]) for k, w in weights.items()}}
        y = _forward(args)
    with _jax.named_scope("loss_head"):
        err = _jnp.square(y.astype(_jnp.float32) - loss_target)
        return 0.5 * _jnp.sum(_jnp.mean(err, axis=-1)) if err.ndim else 0.5 * err


def _adamw(w, g, m, v):
    m = ADAM_B1 * m + (1.0 - ADAM_B1) * g
    v = ADAM_B2 * v + (1.0 - ADAM_B2) * _jnp.square(g)
    m_hat = m / (1.0 - ADAM_B1 ** ADAM_STEP)
    v_hat = v / (1.0 - ADAM_B2 ** ADAM_STEP)
    delta = -ADAM_LR * (m_hat / (_jnp.sqrt(v_hat) + ADAM_EPS) + ADAM_WD * w)
    return delta, m, v


def reference(x, norm_gain, w_in, q_norm_a, k_norm_a, q_norm_b, k_norm_b, sink_a, rel_bias, w_branch_a, w_branch_b, b_merge, w_out, loss_target, m_norm_gain, m_w_in, m_q_norm_a, m_k_norm_a, m_q_norm_b, m_k_norm_b, m_sink_a, m_rel_bias, m_w_branch_a, m_w_branch_b, m_b_merge, m_w_out, v_norm_gain, v_w_in, v_q_norm_a, v_k_norm_a, v_q_norm_b, v_k_norm_b, v_sink_a, v_rel_bias, v_w_branch_a, v_w_branch_b, v_b_merge, v_w_out):
    given = dict(x=x, norm_gain=norm_gain, w_in=w_in, q_norm_a=q_norm_a, k_norm_a=k_norm_a, q_norm_b=q_norm_b, k_norm_b=k_norm_b, sink_a=sink_a, rel_bias=rel_bias, w_branch_a=w_branch_a, w_branch_b=w_branch_b, b_merge=b_merge, w_out=w_out, loss_target=loss_target, m_norm_gain=m_norm_gain, m_w_in=m_w_in, m_q_norm_a=m_q_norm_a, m_k_norm_a=m_k_norm_a, m_q_norm_b=m_q_norm_b, m_k_norm_b=m_k_norm_b, m_sink_a=m_sink_a, m_rel_bias=m_rel_bias, m_w_branch_a=m_w_branch_a, m_w_branch_b=m_w_branch_b, m_b_merge=m_b_merge, m_w_out=m_w_out, v_norm_gain=v_norm_gain, v_w_in=v_w_in, v_q_norm_a=v_q_norm_a, v_k_norm_a=v_k_norm_a, v_q_norm_b=v_q_norm_b, v_k_norm_b=v_k_norm_b, v_sink_a=v_sink_a, v_rel_bias=v_rel_bias, v_w_branch_a=v_w_branch_a, v_w_branch_b=v_w_branch_b, v_b_merge=v_b_merge, v_w_out=v_w_out)
    weights = {n: given[n] for n in TWIN_WEIGHTS}
    shared = {n: given[n] for n in SHARED_INPUTS}
    per_example = {n: given[n] for n in ['x']}
    grad_fn = _jax.value_and_grad(_loss, argnums=(0, 1))

    def one_microbatch(ex, loss_target):
        ex = dict(ex)
        diff = ex.pop(TWIN_DIFF_INPUT)
        return grad_fn(weights, diff, {**shared, **ex}, loss_target)

    if N_MICROBATCH == 1:
        loss, (grad_w, grad_x) = one_microbatch(per_example, given["loss_target"])
    else:
        def body(carry, xs):
            loss_sum, grad_sum = carry
            l_k, (gw_k, gx_k) = one_microbatch(xs[0], xs[1])
            with _jax.named_scope("update"):
                return (loss_sum + l_k, _jax.tree.map(_jnp.add, grad_sum, gw_k)), gx_k

        init = (_jnp.zeros((), _jnp.float32), _jax.tree.map(_jnp.zeros_like, weights))
        (loss, grad_w), grad_x = _jax.lax.scan(body, init, (per_example, given["loss_target"]))
    with _jax.named_scope("update"):
        delta_w, new_m, new_v = {}, {}, {}
        for n in TWIN_WEIGHTS:
            delta_w[n], new_m[n], new_v[n] = _adamw(weights[n], grad_w[n], given["m_" + n], given["v_" + n])
    return (loss, grad_x, *[grad_w[n] for n in TWIN_WEIGHTS], *[delta_w[n] for n in TWIN_WEIGHTS],
            *[new_m[n] for n in TWIN_WEIGHTS], *[new_v[n] for n in TWIN_WEIGHTS])
```

```python
import math

import numpy as np
import jax
import jax.numpy as jnp
from jax import lax
from jax.experimental import pallas as pl
from jax.experimental.pallas import tpu as pltpu

f32 = jnp.float32
bf16 = jnp.bfloat16

S = 4096
D = 1024
NA = 5376
NT = 3072
NW = NA + NT
HD = 64
LANES = 128
EPS = 1e-6
NEG = -1e30
SCALE = HD ** -0.5
TQ = 128
PAD = 128
SP = S + 2 * PAD
NDEV = 8
GROUPS = ((128, 1, 0), (64, 1, 8), (64, 4, 16), (64, 16, 24))
CHUNK = 256
TN = 768

ADAM_LR, ADAM_B1, ADAM_B2, ADAM_EPS, ADAM_WD, ADAM_STEP = 0.001, 0.9, 0.999, 1e-08, 0.01, 10

MIB = 1024 * 1024
NT_DIMS = (((1,), (1,)), ((), ()))
TN_DIMS = (((0,), (0,)), ((), ()))


def _params(sem=None, vmem_mib=48):
    return pltpu.CompilerParams(dimension_semantics=sem, vmem_limit_bytes=vmem_mib * MIB)


def _lo():
    return lax.broadcasted_iota(jnp.int32, (1, LANES), 1) < HD


def _seg_sum(x):
    lo = _lo()
    outs = []
    for b in range(x.shape[1] // LANES):
        xb = x[:, b * LANES:(b + 1) * LANES]
        s0 = jnp.sum(jnp.where(lo, xb, 0.0), axis=1, keepdims=True)
        s1 = jnp.sum(jnp.where(lo, 0.0, xb), axis=1, keepdims=True)
        outs.append(jnp.where(lo, s0, s1))
    return outs[0] if len(outs) == 1 else jnp.concatenate(outs, axis=1)


def _bucket_np(blk, stride):
    w = TQ + 2 * blk
    rel = np.arange(w)[None, :] - blk - np.arange(TQ)[:, None]
    band = np.abs(rel) <= blk
    r = rel * stride
    n = np.abs(r)
    nf = np.maximum(n, 8).astype(np.float32)
    large = 8 + (np.log(nf / np.float32(8)) / np.float32(math.log(128.0)) * np.float32(8)).astype(np.int32)
    large = np.minimum(large, 15)
    b = (r > 0).astype(np.int32) * 16 + np.where(n < 8, n, large)
    return np.where(band, b, -1).astype(np.int32)


def _rms_inproj(x, gain, w):
    ts = 512
    na, nt = NA // TN, NT // TN

    def body(x_ref, g_ref, w_ref, pa_ref, pt_ref, h_ref, r_ref, hs):
        j = pl.program_id(1)

        @pl.when(j == 0)
        def _():
            xv = x_ref[...]
            r = lax.rsqrt(jnp.mean(xv * xv, axis=-1, keepdims=True) + EPS)
            hs[...] = ((xv * r) * g_ref[...]).astype(bf16)
            h_ref[...] = hs[...]
            r_ref[...] = r

        acc = jnp.dot(hs[...], w_ref[...], preferred_element_type=f32)

        @pl.when(j < na)
        def _():
            pa_ref[...] = acc

        @pl.when(j >= na)
        def _():
            pt_ref[...] = acc

    return pl.pallas_call(
        body,
        grid=(S // ts, na + nt),
        in_specs=[
            pl.BlockSpec((ts, D), lambda i, j: (i, 0)),
            pl.BlockSpec((1, D), lambda i, j: (0, 0)),
            pl.BlockSpec((D, TN), lambda i, j: (0, j)),
        ],
        out_specs=[
            pl.BlockSpec((ts, TN), lambda i, j: (i, jnp.minimum(j, na - 1))),
            pl.BlockSpec((ts, TN), lambda i, j: (i, jnp.maximum(j - na, 0))),
            pl.BlockSpec((ts, D), lambda i, j: (i, 0)),
            pl.BlockSpec((ts, 1), lambda i, j: (i, 0)),
        ],
        out_shape=[
            jax.ShapeDtypeStruct((S, NA), f32),
            jax.ShapeDtypeStruct((S, NT), f32),
            jax.ShapeDtypeStruct((S, D), bf16),
            jax.ShapeDtypeStruct((S, 1), f32),
        ],
        scratch_shapes=[pltpu.VMEM((ts, D), bf16)],
        compiler_params=_params(("arbitrary", "arbitrary")),
        name="rms_inproj",
    )(x, gain, w)


def _bias_expand(table, bucket, c0, name):
    tq, w = bucket.shape

    def body(tab_ref, bk_ref, o_ref):
        h = pl.program_id(0)
        bk = bk_ref[...]

        def step(b, acc):
            return jnp.where(bk == b, tab_ref[b, c0 + h], acc)

        o_ref[...] = lax.fori_loop(0, 32, step, jnp.full((tq, w), NEG, f32))

    return pl.pallas_call(
        body,
        grid=(8,),
        in_specs=[pl.BlockSpec(memory_space=pltpu.SMEM), pl.BlockSpec((tq, w), lambda h: (0, 0))],
        out_specs=pl.BlockSpec((None, tq, w), lambda h: (h, 0, 0)),
        out_shape=jax.ShapeDtypeStruct((8, tq, w), f32),
        compiler_params=_params(("arbitrary",)),
        name=name,
    )(table, bucket)


def _bias_reduce(dbias, bucket, name):
    tq, w = bucket.shape

    def body(db_ref, bk_ref, o_ref):
        bk = bk_ref[...]
        db = db_ref[...]
        lane = lax.broadcasted_iota(jnp.int32, (8, LANES), 1)

        def step(b, acc):
            s = jnp.sum(jnp.where(bk == b, db, 0.0), axis=1, keepdims=True)
            s = jnp.sum(s, axis=0, keepdims=True)
            return jnp.where(lane == b, s, acc)

        o_ref[...] = lax.fori_loop(0, 32, step, jnp.zeros((8, LANES), f32))

    return pl.pallas_call(
        body,
        grid=(8,),
        in_specs=[pl.BlockSpec((None, tq, w), lambda h: (h, 0, 0)), pl.BlockSpec((tq, w), lambda h: (0, 0))],
        out_specs=pl.BlockSpec((None, 8, LANES), lambda h: (h, 0, 0)),
        out_shape=jax.ShapeDtypeStruct((8, 8, LANES), f32),
        compiler_params=_params(("arbitrary",)),
        name=name,
    )(dbias, bucket)


def _col_block(g, j):
    kind = j // 4
    hp = j % 4
    a = jnp.where(kind == 0, hp, 3 + kind)
    b = 6 + 12 * kind + 4 * (g - 1) + hp
    return jnp.where(g == 0, a, b)


def _half_sums(x):
    lo = _lo()
    s0 = jnp.sum(jnp.where(lo, x, 0.0), axis=1, keepdims=True)
    s1 = jnp.sum(jnp.where(lo, 0.0, x), axis=1, keepdims=True)
    return jnp.where(lo, s0, s1)


def _prep(proj_a, gains):
    def body(p_ref, g_ref, o_ref):
        g = pl.program_id(0)
        j = pl.program_id(1)
        kind = j // 4
        lo = _lo()
        half = jnp.where(lo, 0, 1)
        take = (kind == 0) | (half == (j % 4) // 2)
        gain = g_ref[...]
        o_ref[0:PAD, :] = jnp.zeros((PAD, LANES), bf16)
        o_ref[PAD + S:SP, :] = jnp.zeros((PAD, LANES), bf16)

        def norm_store(xv, dst, dup):
            if dup:
                xv = jnp.where(take, xv, pltpu.roll(xv, HD, 1))
            r = lax.rsqrt(_half_sums(xv * xv) * (1.0 / HD) + EPS)
            r = jnp.where(kind == 2, 1.0, r)
            yv = (xv * r) * gain
            yv = jnp.where(kind == 0, yv * SCALE, yv)
            o_ref[PAD + dst:PAD + dst + CHUNK, :] = yv.astype(bf16)

        for gi, (_, d, _) in enumerate(GROUPS):
            @pl.when(g == gi)
            def _():
                seq = S // d
                for c in range(d):
                    for i in range(seq // CHUNK):
                        if d == 1:
                            xv = p_ref[i * CHUNK:(i + 1) * CHUNK, :]
                        else:
                            xv = p_ref[pl.ds(c + i * CHUNK * d, CHUNK, stride=d), :]
                        norm_store(xv, c * seq + i * CHUNK, gi == 0)

    return pl.pallas_call(
        body,
        grid=(4, 12),
        in_specs=[
            pl.BlockSpec((S, LANES), lambda g, j: (0, _col_block(g, j))),
            pl.BlockSpec((None, None, 1, LANES), lambda g, j: (g, j // 4, 0, 0)),
        ],
        out_specs=pl.BlockSpec((None, None, SP, LANES), lambda g, j: (g, j, 0, 0)),
        out_shape=jax.ShapeDtypeStruct((4, 12, SP, LANES), bf16),
        compiler_params=_params(("arbitrary", "arbitrary")),
        name="prep",
    )(proj_a, gains)


def _seq_len(g, ng):
    return S if ng == 1 else jnp.right_shift(S, 2 * g)


def _attn_fwd(gl, bias, sink, g0, ng, blk, name):
    w = TQ + 2 * blk
    use_sink = sink is not None

    def body(*refs):
        if use_sink:
            sink_ref, q_ref, k_ref, v_ref, b_ref, o_ref, l_ref = refs
        else:
            q_ref, k_ref, v_ref, b_ref, o_ref, l_ref = refs
        g = pl.program_id(0)
        hp = pl.program_id(1)
        lo = _lo()
        seq = _seq_len(g, ng)
        mi = lax.broadcasted_iota(jnp.int32, (1, w), 1)

        def tile(t, carry):
            f0 = pl.multiple_of(t * TQ, TQ)
            m0 = jnp.bitwise_and(f0, seq - 1)
            q = q_ref[pl.ds(PAD + f0, TQ), :]
            kw = k_ref[pl.ds(PAD - blk + f0, w), :]
            vw = v_ref[pl.ds(PAD - blk + f0, w), :]
            inside = (mi >= blk - m0) & (mi < seq + blk - m0)
            outs, lses = [], []
            for h in range(2):
                msk = lo if h == 0 else jnp.logical_not(lo)
                qh = jnp.where(msk, q, jnp.zeros_like(q))
                s = lax.dot_general(qh, kw, NT_DIMS, preferred_element_type=f32)
                logit = jnp.where(inside, s + b_ref[h], NEG)
                m = jnp.max(logit, axis=1, keepdims=True)
                e = jnp.exp(logit - m)
                lse = m + jnp.log(jnp.sum(e, axis=1, keepdims=True))
                if use_sink:
                    sk = sink_ref[2 * hp + h]
                    mx = jnp.maximum(lse, sk)
                    lse = mx + jnp.log(jnp.exp(lse - mx) + jnp.exp(sk - mx))
                p = e * jnp.exp(m - lse)
                outs.append(jnp.dot(p.astype(bf16), vw, preferred_element_type=f32))
                lses.append(lse)
            o_ref[pl.ds(f0, TQ), :] = jnp.where(lo, outs[0], outs[1])
            l_ref[pl.ds(f0, TQ), :] = jnp.where(lo, lses[0], lses[1])
            return carry

        lax.fori_loop(0, S // TQ, tile, 0)

    col = pl.BlockSpec((None, None, SP, LANES), lambda g, hp: (g0 + g, hp, 0, 0))
    in_specs = [
        col,
        pl.BlockSpec((None, None, SP, LANES), lambda g, hp: (g0 + g, 4 + hp, 0, 0)),
        pl.BlockSpec((None, None, SP, LANES), lambda g, hp: (g0 + g, 8 + hp, 0, 0)),
        pl.BlockSpec((None, 2, TQ, w), lambda g, hp: (g, hp, 0, 0)),
    ]
    args = [gl, gl, gl, bias]
    if use_sink:
        in_specs = [pl.BlockSpec(memory_space=pltpu.SMEM)] + in_specs
        args = [sink] + args
    out = pl.BlockSpec((None, None, S, LANES), lambda g, hp: (g, hp, 0, 0))
    return pl.pallas_call(
        body,
        grid=(ng, 4),
        in_specs=in_specs,
        out_specs=[out, out],
        out_shape=[jax.ShapeDtypeStruct((ng, 4, S, LANES), f32)] * 2,
        compiler_params=_params(("arbitrary", "arbitrary")),
        name=name,
    )(*args)


def _attn_bwd(gl, bias, do, lse, dd, g0, ng, blk, name):
    w = TQ + 2 * blk

    def body(q_ref, k_ref, v_ref, b_ref, do_ref, l_ref, d_ref, dq_ref, dk_ref, dv_ref, db_ref):
        g = pl.program_id(0)
        lo = _lo()
        seq = _seq_len(g, ng)
        mi = lax.broadcasted_iota(jnp.int32, (1, w), 1)
        dk_ref[...] = jnp.zeros((SP, LANES), f32)
        dv_ref[...] = jnp.zeros((SP, LANES), f32)
        db_ref[...] = jnp.zeros((2, TQ, w), f32)

        def tile(t, carry):
            f0 = pl.multiple_of(t * TQ, TQ)
            m0 = jnp.bitwise_and(f0, seq - 1)
            q = q_ref[pl.ds(PAD + f0, TQ), :]
            kw = k_ref[pl.ds(PAD - blk + f0, w), :]
            vw = v_ref[pl.ds(PAD - blk + f0, w), :]
            dob = do_ref[pl.ds(f0, TQ), :]
            lb = l_ref[pl.ds(f0, TQ), :]
            ddb = d_ref[pl.ds(f0, TQ), :]
            inside = (mi >= blk - m0) & (mi < seq + blk - m0)
            dqs = []
            dkw = jnp.zeros((w, LANES), f32)
            dvw = jnp.zeros((w, LANES), f32)
            for h in range(2):
                msk = lo if h == 0 else jnp.logical_not(lo)
                qh = jnp.where(msk, q, jnp.zeros_like(q))
                doh = jnp.where(msk, dob, jnp.zeros_like(dob))
                lh = jnp.max(jnp.where(msk, lb, -jnp.inf), axis=1, keepdims=True)
                dh = jnp.max(jnp.where(msk, ddb, -jnp.inf), axis=1, keepdims=True)
                s = lax.dot_general(qh, kw, NT_DIMS, preferred_element_type=f32)
                logit = jnp.where(inside, s + b_ref[h], NEG)
                p = jnp.exp(logit - lh)
                dp = lax.dot_general(doh, vw, NT_DIMS, preferred_element_type=f32)
                ds = p * (dp - dh)
                db_ref[h] += ds
                dsb = ds.astype(bf16)
                dqs.append(jnp.dot(dsb, kw, preferred_element_type=f32))
                dkw = dkw + lax.dot_general(dsb, qh, TN_DIMS, preferred_element_type=f32)
                dvw = dvw + lax.dot_general(p.astype(bf16), doh, TN_DIMS, preferred_element_type=f32)
            dq_ref[pl.ds(f0, TQ), :] = jnp.where(lo, dqs[0], dqs[1])
            dk_ref[pl.ds(PAD - blk + f0, w), :] += dkw
            dv_ref[pl.ds(PAD - blk + f0, w), :] += dvw
            return carry

        lax.fori_loop(0, S // TQ, tile, 0)

    def gcol(off):
        return pl.BlockSpec((None, None, SP, LANES), lambda g, hp: (g0 + g, off + hp, 0, 0))

    row = pl.BlockSpec((None, None, S, LANES), lambda g, hp: (g, hp, 0, 0))
    rowp = pl.BlockSpec((None, None, SP, LANES), lambda g, hp: (g, hp, 0, 0))
    bsp = pl.BlockSpec((None, 2, TQ, w), lambda g, hp: (g, hp, 0, 0))
    return pl.pallas_call(
        body,
        grid=(ng, 4),
        in_specs=[gcol(0), gcol(4), gcol(8), bsp, row, row, row],
        out_specs=[row, rowp, rowp, bsp],
        out_shape=[
            jax.ShapeDtypeStruct((ng, 4, S, LANES), f32),
            jax.ShapeDtypeStruct((ng, 4, SP, LANES), f32),
            jax.ShapeDtypeStruct((ng, 4, SP, LANES), f32),
            jax.ShapeDtypeStruct((ng, 8, TQ, w), f32),
        ],
        compiler_params=_params(("arbitrary", "arbitrary"), vmem_mib=56),
        name=name,
    )(gl, gl, gl, bias, do, lse, dd)


def _b_to_natural(o_gl, l_gl):
    def body(o_ref, l_ref, on_ref, ln_ref):
        g = pl.program_id(0)
        for gi in range(3):
            d = GROUPS[gi + 1][1]

            @pl.when(g == gi)
            def _():
                seq = S // d
                for c in range(d):
                    for i in range(seq // CHUNK):
                        src = slice(c * seq + i * CHUNK, c * seq + (i + 1) * CHUNK)
                        if d == 1:
                            on_ref[src, :] = o_ref[src, :]
                            ln_ref[src, :] = l_ref[src, :]
                        else:
                            dst = pl.ds(c + i * CHUNK * d, CHUNK, stride=d)
                            on_ref[dst, :] = o_ref[src, :]
                            ln_ref[dst, :] = l_ref[src, :]

    col = pl.BlockSpec((None, None, S, LANES), lambda g, hp: (g, hp, 0, 0))
    nat = pl.BlockSpec((S, LANES), lambda g, hp: (0, 4 * g + hp))
    return pl.pallas_call(
        body,
        grid=(3, 4),
        in_specs=[col, col],
        out_specs=[nat, nat],
        out_shape=[jax.ShapeDtypeStruct((S, 3 * 512), f32)] * 2,
        compiler_params=_params(("arbitrary", "arbitrary")),
        name="b_to_natural",
    )(o_gl, l_gl)


def _b_from_natural(do_n, dd_n):
    def body(do_ref, dd_ref, dog_ref, ddg_ref):
        g = pl.program_id(0)
        for gi in range(3):
            d = GROUPS[gi + 1][1]

            @pl.when(g == gi)
            def _():
                seq = S // d
                for c in range(d):
                    for i in range(seq // CHUNK):
                        dst = slice(c * seq + i * CHUNK, c * seq + (i + 1) * CHUNK)
                        if d == 1:
                            a, b = do_ref[dst, :], dd_ref[dst, :]
                        else:
                            src = pl.ds(c + i * CHUNK * d, CHUNK, stride=d)
                            a, b = do_ref[src, :], dd_ref[src, :]
                        dog_ref[dst, :] = a.astype(bf16)
                        ddg_ref[dst, :] = b

    col = pl.BlockSpec((None, None, S, LANES), lambda g, hp: (g, hp, 0, 0))
    nat = pl.BlockSpec((S, LANES), lambda g, hp: (0, 4 * g + hp))
    return pl.pallas_call(
        body,
        grid=(3, 4),
        in_specs=[nat, nat],
        out_specs=[col, col],
        out_shape=[jax.ShapeDtypeStruct((3, 4, S, LANES), bf16), jax.ShapeDtypeStruct((3, 4, S, LANES), f32)],
        compiler_params=_params(("arbitrary", "arbitrary")),
        name="b_from_natural",
    )(do_n, dd_n)


def _sigmoid(z):
    return 1.0 / (1.0 + jnp.exp(-z))


def _tail(x, tgt, o_a, l_a, o_b, l_b, proj_t, bm, w_a, w_b, w_o, sink_b):
    ts = 128

    def body(x_ref, t_ref, oa_ref, la_ref, ob_ref, lb_ref, ga_ref, gb_ref, m0_ref, m1_ref, bm_ref,
             wa_ref, wb_ref, wo_ref, sk_ref,
             dy_ref, dyb_ref, dt_ref, doa_ref, dda_ref, dob_ref, ddb_ref, ya_ref, yb_ref, mg_ref, dbra_ref, dbrb_ref,
             loss_ref, dbm_ref, dsk_ref):
        i = pl.program_id(0)

        @pl.when(i == 0)
        def _():
            loss_ref[...] = jnp.zeros_like(loss_ref)
            dbm_ref[...] = jnp.zeros_like(dbm_ref)
            dsk_ref[...] = jnp.zeros_like(dsk_ref)

        ga = ga_ref[...]
        sa = _sigmoid(ga)
        silu_a = ga * sa
        oa = oa_ref[...]
        ya = oa * silu_a
        gb = gb_ref[...]
        sb = _sigmoid(gb)
        silu_b = gb * sb
        ob = [ob_ref[:, k * 512:(k + 1) * 512] for k in range(3)]
        lb = [lb_ref[:, k * 512:(k + 1) * 512] for k in range(3)]
        mx = jnp.maximum(jnp.maximum(lb[0], lb[1]), lb[2])
        ex = [jnp.exp(v - mx) for v in lb]
        den = ex[0] + ex[1] + ex[2]
        alpha = [e / den for e in ex]
        ybc = alpha[0] * ob[0] + alpha[1] * ob[1] + alpha[2] * ob[2]
        yb = ybc * silu_b
        yab = ya.astype(bf16)
        ybb = yb.astype(bf16)
        br_a = jnp.dot(yab, wa_ref[...], preferred_element_type=f32)
        br_b = jnp.dot(ybb, wb_ref[...], preferred_element_type=f32)
        g0 = _sigmoid(m0_ref[...] + bm_ref[0:1, :])
        g1 = _sigmoid(m1_ref[...] + bm_ref[1:2, :])
        merged = g0 * br_a + g1 * br_b
        mgb = merged.astype(bf16)
        y = x_ref[...] + jnp.dot(mgb, wo_ref[...], preferred_element_type=f32)
        err = y - t_ref[...]
        part = jnp.sum(jnp.sum(err * err, axis=1, keepdims=True), axis=0, keepdims=True)
        loss_ref[...] += part * (0.5 / D)
        dy = err * (1.0 / D)
        dyb = dy.astype(bf16)
        dmerged = lax.dot_general(dyb, wo_ref[...], NT_DIMS, preferred_element_type=f32)
        dbr_a = (dmerged * g0).astype(bf16)
        dbr_b = (dmerged * g1).astype(bf16)
        dm0 = dmerged * br_a * (g0 * (1.0 - g0))
        dm1 = dmerged * br_b * (g1 * (1.0 - g1))
        dbm_ref[0:1, :] += jnp.sum(dm0, axis=0, keepdims=True)
        dbm_ref[1:2, :] += jnp.sum(dm1, axis=0, keepdims=True)
        dya = lax.dot_general(dbr_a, wa_ref[...], NT_DIMS, preferred_element_type=f32)
        dyb2 = lax.dot_general(dbr_b, wb_ref[...], NT_DIMS, preferred_element_type=f32)
        do_a = dya * silu_a
        dga = dya * oa * (sa * (1.0 + ga * (1.0 - sa)))
        delta_a = _seg_sum(do_a * oa)
        dsk_ref[...] -= jnp.sum(delta_a * jnp.exp(sk_ref[...] - la_ref[...]), axis=0, keepdims=True)
        dybc = dyb2 * silu_b
        dgb = dyb2 * ybc * (sb * (1.0 + gb * (1.0 - sb)))
        dbar = _seg_sum(dybc * ybc)
        dy_ref[...] = dy
        dyb_ref[...] = dyb
        dt_ref[:, 0:512] = dga.astype(bf16)
        dt_ref[:, 512:1024] = dgb.astype(bf16)
        dt_ref[:, 1024:2048] = dm0.astype(bf16)
        dt_ref[:, 2048:3072] = dm1.astype(bf16)
        doa_ref[...] = do_a.astype(bf16)
        dda_ref[...] = delta_a
        for k in range(3):
            dob_ref[:, k * 512:(k + 1) * 512] = alpha[k] * dybc
            ddb_ref[:, k * 512:(k + 1) * 512] = alpha[k] * dbar
        ya_ref[...] = yab
        yb_ref[...] = ybb
        mg_ref[...] = mgb
        dbra_ref[...] = dbr_a
        dbrb_ref[...] = dbr_b

    def rows(n, blk=0):
        return pl.BlockSpec((ts, n), lambda i: (i, blk))

    def whole(r, c):
        return pl.BlockSpec((r, c), lambda i: (0, 0))

    outs = [
        ((S, D), f32, rows(D)), ((S, D), bf16, rows(D)), ((S, NT), bf16, rows(NT)),
        ((S, 512), bf16, rows(512)), ((S, 512), f32, rows(512)),
        ((S, 1536), f32, rows(1536)), ((S, 1536), f32, rows(1536)),
        ((S, 512), bf16, rows(512)), ((S, 512), bf16, rows(512)), ((S, D), bf16, rows(D)),
        ((S, D), bf16, rows(D)), ((S, D), bf16, rows(D)),
        ((1, 1), f32, whole(1, 1)), ((2, D), f32, whole(2, D)), ((1, 512), f32, whole(1, 512)),
    ]
    return pl.pallas_call(
        body,
        grid=(S // ts,),
        in_specs=[
            rows(D), rows(D), rows(512), rows(512), rows(1536), rows(1536),
            rows(512, 0), rows(512, 1), rows(D, 1), rows(D, 2), whole(2, D),
            whole(512, D), whole(512, D), whole(D, D), whole(1, 512),
        ],
        out_specs=[o[2] for o in outs],
        out_shape=[jax.ShapeDtypeStruct(o[0], o[1]) for o in outs],
        compiler_params=_params(("arbitrary",)),
        name="tail",
    )(x, tgt, o_a, l_a, o_b, l_b, proj_t, proj_t, proj_t, proj_t, bm, w_a, w_b, w_o, sink_b)


def _norm_bwd(xv, dyv, gain, kind):
    r = lax.rsqrt(_half_sums(xv * xv) * (1.0 / HD) + EPS)
    yv = xv * r
    up = jnp.where(kind == 0, dyv * SCALE, dyv)
    u = up * gain
    dxv = r * (u - yv * (_half_sums(u * yv) * (1.0 / HD)))
    dxv = jnp.where(kind == 2, dyv, dxv)
    dg = jnp.where(kind == 2, 0.0, jnp.sum(up * yv, axis=0, keepdims=True))
    return dxv, dg


def _post_b(dq, dk, dv, proj_a, gains, dproj):
    def body(dq_ref, dk_ref, dv_ref, p_ref, g_ref, alias_ref, o_ref, dg_ref, nat):
        del alias_ref
        g = pl.program_id(0)
        j = pl.program_id(1)
        kind = j // 4
        gain = g_ref[...]

        @pl.when(j % 4 == 0)
        def _():
            dg_ref[...] = jnp.zeros_like(dg_ref)

        for gi in range(3):
            d = GROUPS[gi + 1][1]

            @pl.when(g == gi)
            def _():
                seq = S // d
                for c in range(d):
                    for i in range(seq // CHUNK):
                        src = c * seq + i * CHUNK
                        a = dq_ref[src:src + CHUNK, :]
                        b = dk_ref[PAD + src:PAD + src + CHUNK, :]
                        e = dv_ref[PAD + src:PAD + src + CHUNK, :]
                        dyv = jnp.where(kind == 0, a, jnp.where(kind == 1, b, e))
                        if d == 1:
                            idx = slice(src, src + CHUNK)
                        else:
                            idx = pl.ds(c + i * CHUNK * d, CHUNK, stride=d)
                        dxv, dg = _norm_bwd(p_ref[idx, :], dyv, gain, kind)
                        nat[idx, :] = dxv
                        dg_ref[...] += dg

        for i in range(S // CHUNK):
            o_ref[i * CHUNK:(i + 1) * CHUNK, :] = nat[i * CHUNK:(i + 1) * CHUNK, :].astype(bf16)

    def gcol(n, off):
        return pl.BlockSpec((None, None, n, LANES), lambda g, j: (g, jnp.clip(j - off, 0, 3), 0, 0))

    return pl.pallas_call(
        body,
        grid=(3, 12),
        in_specs=[
            gcol(S, 0), gcol(SP, 4), gcol(SP, 8),
            pl.BlockSpec((S, LANES), lambda g, j: (0, _col_block(g + 1, j))),
            pl.BlockSpec((None, None, 1, LANES), lambda g, j: (g + 1, j // 4, 0, 0)),
            pl.BlockSpec(memory_space=pl.ANY),
        ],
        out_specs=[
            pl.BlockSpec((S, LANES), lambda g, j: (0, _col_block(g + 1, j))),
            pl.BlockSpec((None, None, 1, LANES), lambda g, j: (g, j // 4, 0, 0)),
        ],
        out_shape=[jax.ShapeDtypeStruct((S, NA), bf16), jax.ShapeDtypeStruct((3, 3, 1, LANES), f32)],
        scratch_shapes=[pltpu.VMEM((S, LANES), f32)],
        input_output_aliases={5: 0},
        compiler_params=_params(("arbitrary", "arbitrary")),
        name="post_b",
    )(dq, dk, dv, proj_a, gains, dproj)


def _post_a(dq, dk, dv, proj_a, gains):
    def body(dq_ref, dk_ref, dv_ref, p_ref, g_ref, o_ref, dg_ref):
        j = pl.program_id(0)
        kind = jnp.maximum(j - 3, 0)
        gain = g_ref[...]
        lo = _lo()

        @pl.when((j == 0) | (j >= 4))
        def _():
            dg_ref[...] = jnp.zeros_like(dg_ref)

        def fold(ref, r0):
            t0 = ref[0, PAD + r0:PAD + r0 + CHUNK, :] + ref[1, PAD + r0:PAD + r0 + CHUNK, :]
            t1 = ref[2, PAD + r0:PAD + r0 + CHUNK, :] + ref[3, PAD + r0:PAD + r0 + CHUNK, :]
            return jnp.where(lo, t0 + pltpu.roll(t0, HD, 1), t1 + pltpu.roll(t1, HD, 1))

        for i in range(S // CHUNK):
            r0 = i * CHUNK
            a = dq_ref[r0:r0 + CHUNK, :]
            dyv = jnp.where(kind == 0, a, jnp.where(kind == 1, fold(dk_ref, r0), fold(dv_ref, r0)))
            dxv, dg = _norm_bwd(p_ref[r0:r0 + CHUNK, :], dyv, gain, kind)
            o_ref[r0:r0 + CHUNK, :] = dxv.astype(bf16)
            dg_ref[...] += dg

    exp = pl.BlockSpec((None, 4, SP, LANES), lambda j: (0, 0, 0, 0))
    return pl.pallas_call(
        body,
        grid=(6,),
        in_specs=[
            pl.BlockSpec((None, None, S, LANES), lambda j: (0, jnp.minimum(j, 3), 0, 0)),
            exp, exp,
            pl.BlockSpec((S, LANES), lambda j: (0, j)),
            pl.BlockSpec((None, None, 1, LANES), lambda j: (0, jnp.maximum(j - 3, 0), 0, 0)),
        ],
        out_specs=[
            pl.BlockSpec((S, LANES), lambda j: (0, j)),
            pl.BlockSpec((None, 1, LANES), lambda j: (jnp.maximum(j - 3, 0), 0, 0)),
        ],
        out_shape=[jax.ShapeDtypeStruct((S, NA), bf16), jax.ShapeDtypeStruct((3, 1, LANES), f32)],
        compiler_params=_params(("arbitrary",)),
        name="post_a",
    )(dq, dk, dv, proj_a, gains)


def _dh_norm_bwd(dproj_a, dproj_t, w, x, rstd, gain, dy):
    ts = 512
    na, nt = NA // TN, NT // TN

    def body(da_ref, dt_ref, w_ref, x_ref, r_ref, g_ref, dy_ref, gx_ref, dgn_ref, acc):
        i = pl.program_id(0)
        k = pl.program_id(1)

        @pl.when((i == 0) & (k == 0))
        def _():
            dgn_ref[...] = jnp.zeros_like(dgn_ref)

        @pl.when(k == 0)
        def _():
            acc[...] = jnp.zeros_like(acc)

        @pl.when(k < na)
        def _():
            acc[...] += lax.dot_general(da_ref[...], w_ref[...], NT_DIMS, preferred_element_type=f32)

        @pl.when(k >= na)
        def _():
            acc[...] += lax.dot_general(dt_ref[...], w_ref[...], NT_DIMS, preferred_element_type=f32)

        @pl.when(k == na + nt - 1)
        def _():
            dh = acc[...]
            xh = x_ref[...] * r_ref[...]
            u = dh * g_ref[...]
            dx = r_ref[...] * (u - xh * jnp.mean(u * xh, axis=-1, keepdims=True))
            gx_ref[...] = dy_ref[...] + dx
            dgn_ref[...] += jnp.sum(dh * xh, axis=0, keepdims=True)

    return pl.pallas_call(
        body,
        grid=(S // ts, na + nt),
        in_specs=[
            pl.BlockSpec((ts, TN), lambda i, k: (i, jnp.minimum(k, na - 1))),
            pl.BlockSpec((ts, TN), lambda i, k: (i, jnp.maximum(k - na, 0))),
            pl.BlockSpec((D, TN), lambda i, k: (0, k)),
            pl.BlockSpec((ts, D), lambda i, k: (i, 0)),
            pl.BlockSpec((ts, 1), lambda i, k: (i, 0)),
            pl.BlockSpec((1, D), lambda i, k: (0, 0)),
            pl.BlockSpec((ts, D), lambda i, k: (i, 0)),
        ],
        out_specs=[pl.BlockSpec((ts, D), lambda i, k: (i, 0)), pl.BlockSpec((1, D), lambda i, k: (0, 0))],
        out_shape=[jax.ShapeDtypeStruct((S, D), f32), jax.ShapeDtypeStruct((1, D), f32)],
        scratch_shapes=[pltpu.VMEM((ts, D), f32)],
        compiler_params=_params(("arbitrary", "arbitrary")),
        name="dh_norm_bwd",
    )(dproj_a, dproj_t, w, x, rstd, gain, dy)


def _matmul_tn(a, b, name):
    m, n = a.shape[1], b.shape[1]
    tn = TN if n % TN == 0 else 512
    tk = 512

    def body(a_ref, b_ref, o_ref):
        @pl.when(pl.program_id(1) == 0)
        def _():
            o_ref[...] = jnp.zeros_like(o_ref)

        o_ref[...] += lax.dot_general(a_ref[...], b_ref[...], TN_DIMS, preferred_element_type=f32)

    return pl.pallas_call(
        body,
        grid=(n // tn, S // tk),
        in_specs=[pl.BlockSpec((tk, m), lambda j, k: (k, 0)), pl.BlockSpec((tk, tn), lambda j, k: (k, j))],
        out_specs=pl.BlockSpec((m, tn), lambda j, k: (0, j)),
        out_shape=jax.ShapeDtypeStruct((m, n), f32),
        compiler_params=_params(("arbitrary", "arbitrary")),
        name=name,
    )(a, b)


def _exchange(scatter, gather, name):
    arrs = list(scatter) + list(gather)
    n = len(arrs)
    ns = len(scatter)

    def body(*refs):
        ins, outs = refs[:n], refs[n:2 * n]
        send_sems, recv_sems, local_sems = refs[2 * n:]
        x, y, c = lax.axis_index("x"), lax.axis_index("y"), lax.axis_index("c")
        me = 4 * x + 2 * y + c
        local, remote = [], []
        for a in range(n):
            lc = pltpu.make_async_copy(ins[a].at[me] if a < ns else ins[a], outs[a].at[me], local_sems.at[a])
            lc.start()
            local.append(lc)
            for r in range(1, NDEV):
                px = 1 - x if r & 4 else x
                py = 1 - y if r & 2 else y
                pc = 1 - c if r & 1 else c
                cp = pltpu.make_async_remote_copy(
                    src_ref=ins[a].at[4 * px + 2 * py + pc] if a < ns else ins[a],
                    dst_ref=outs[a].at[me],
                    send_sem=send_sems.at[a, r - 1],
                    recv_sem=recv_sems.at[a, r - 1],
                    device_id=(px, py, pc),
                    device_id_type=pl.DeviceIdType.MESH,
                )
                cp.start()
                remote.append(cp)
        for cp in remote:
            cp.wait_recv()
        for cp in remote:
            cp.wait_send()
        for lc in local:
            lc.wait()

    out_shape = [jax.ShapeDtypeStruct(a.shape if i < ns else (NDEV,) + a.shape, a.dtype) for i, a in enumerate(arrs)]
    return pl.pallas_call(
        body,
        in_specs=[pl.BlockSpec(memory_space=pl.ANY)] * n,
        out_specs=[pl.BlockSpec(memory_space=pl.ANY)] * n,
        out_shape=out_shape,
        scratch_shapes=[
            pltpu.SemaphoreType.DMA((n, NDEV - 1)),
            pltpu.SemaphoreType.DMA((n, NDEV - 1)),
            pltpu.SemaphoreType.DMA((n,)),
        ],
        compiler_params=pltpu.CompilerParams(has_side_effects=True),
        name=name,
    )(*arrs)


def _adamw(w, slots, m, v, name):
    r, c = w.shape
    tr = 128 if r % 128 == 0 else r

    def body(w_ref, s_ref, m_ref, v_ref, g_ref, d_ref, nm_ref, nv_ref):
        g = s_ref[0]
        for k in range(1, NDEV):
            g = g + s_ref[k]
        mm = ADAM_B1 * m_ref[...] + (1.0 - ADAM_B1) * g
        vv = ADAM_B2 * v_ref[...] + (1.0 - ADAM_B2) * (g * g)
        m_hat = mm / (1.0 - ADAM_B1 ** ADAM_STEP)
        v_hat = vv / (1.0 - ADAM_B2 ** ADAM_STEP)
        g_ref[...] = g
        d_ref[...] = -ADAM_LR * (m_hat / (jnp.sqrt(v_hat) + ADAM_EPS) + ADAM_WD * w_ref[...])
        nm_ref[...] = mm
        nv_ref[...] = vv

    blk = pl.BlockSpec((tr, c), lambda i: (i, 0))
    return pl.pallas_call(
        body,
        grid=(r // tr,),
        in_specs=[blk, pl.BlockSpec((NDEV, tr, c), lambda i: (0, i, 0)), blk, blk],
        out_specs=[blk] * 4,
        out_shape=[jax.ShapeDtypeStruct((r, c), f32)] * 4,
        compiler_params=_params(("arbitrary",)),
        name=name,
    )(w, slots, m, v)


def _local_step(x, tgt, norm_gain, w_in, qn_a, kn_a, qn_b, kn_b, sink_a, rel_bias, w_a, w_b, b_merge, w_o):
    two = lambda t: jnp.concatenate([t, t], axis=-1).reshape(1, LANES)
    ones = jnp.ones((1, LANES), f32)
    gains = jnp.stack([
        jnp.stack([two(qn_a), two(kn_a), ones]),
        jnp.stack([two(qn_b), two(kn_b), ones]),
        jnp.stack([two(qn_b), two(kn_b), ones]),
        jnp.stack([two(qn_b), two(kn_b), ones]),
    ])
    buckets = [jnp.asarray(_bucket_np(blk, d)) for blk, d, _ in GROUPS]
    bias_a = _bias_expand(rel_bias, buckets[0], 0, "bias_expand_a")[None]
    bias_b = jnp.stack([_bias_expand(rel_bias, buckets[k], GROUPS[k][2], "bias_expand_b%d" % k) for k in (1, 2, 3)])

    proj_a, proj_t, hb, rstd = _rms_inproj(x, norm_gain, w_in)
    gl = _prep(proj_a, gains)
    o_a, l_a = _attn_fwd(gl, bias_a, sink_a.reshape(8), 0, 1, 128, "attn_fwd_a")
    o_bg, l_bg = _attn_fwd(gl, bias_b, None, 1, 3, 64, "attn_fwd_b")
    o_b, l_b = _b_to_natural(o_bg, l_bg)
    o_an = o_a[0].transpose(1, 0, 2).reshape(S, 512)
    l_an = l_a[0].transpose(1, 0, 2).reshape(S, 512)
    sink_b = jnp.repeat(sink_a.reshape(8), HD).reshape(1, 512)

    (dy, dyb, dproj_t, do_a, dd_a, do_b, dd_b, ya, yb, mg, dbr_a, dbr_b, loss, dbm, dsk) = _tail(
        x, tgt, o_an, l_an, o_b, l_b, proj_t, b_merge, w_a, w_b, w_o, sink_b)

    do_ag = do_a.reshape(S, 4, LANES).transpose(1, 0, 2)[None]
    dd_ag = dd_a.reshape(S, 4, LANES).transpose(1, 0, 2)[None]
    dq_a, dk_a, dv_a, dbias_a = _attn_bwd(gl, bias_a, do_ag, l_a, dd_ag, 0, 1, 128, "attn_bwd_a")
    do_bg, dd_bg = _b_from_natural(do_b, dd_b)
    dq_b, dk_b, dv_b, dbias_b = _attn_bwd(gl, bias_b, do_bg, l_bg, dd_bg, 1, 3, 64, "attn_bwd_b")

    dproj_a, dg_a = _post_a(dq_a, dk_a, dv_a, proj_a, gains)
    dproj_a, dg_b = _post_b(dq_b, dk_b, dv_b, proj_a, gains, dproj_a)

    grad_x, d_norm_gain = _dh_norm_bwd(dproj_a, dproj_t, w_in, x, rstd, norm_gain, dy)
    dw_in = jnp.concatenate([_matmul_tn(hb, dproj_a, "dw_in_a"), _matmul_tn(hb, dproj_t, "dw_in_t")], axis=1)
    dw_o = _matmul_tn(mg, dyb, "dw_out")
    dw_a = _matmul_tn(ya, dbr_a, "dw_branch_a")
    dw_b = _matmul_tn(yb, dbr_b, "dw_branch_b")

    fold = lambda t: t[..., :HD] + t[..., HD:]
    d_qn_a = fold(dg_a[0, 0])
    d_kn_a = fold(dg_a[1, 0])
    d_qn_b = fold(dg_b[:, 0, 0].sum(axis=0))
    d_kn_b = fold(dg_b[:, 1, 0].sum(axis=0))
    d_sink = dsk.reshape(8, HD)[:, 0]
    red = [_bias_reduce(dbias_a[0], buckets[0], "bias_reduce_a")] + [
        _bias_reduce(dbias_b[k - 1], buckets[k], "bias_reduce_b%d" % k) for k in (1, 2, 3)]
    d_rel = jnp.concatenate([r[:, 0, :32].T for r in red], axis=1)
    return dict(loss=loss, grad_x=grad_x, norm_gain=d_norm_gain, w_in=dw_in, q_norm_a=d_qn_a, k_norm_a=d_kn_a,
                q_norm_b=d_qn_b, k_norm_b=d_kn_b, sink_a=d_sink, rel_bias=d_rel, w_branch_a=dw_a, w_branch_b=dw_b,
                b_merge=dbm, w_out=dw_o)


SMALL = (("norm_gain", D), ("q_norm_a", HD), ("k_norm_a", HD), ("q_norm_b", HD), ("k_norm_b", HD), ("sink_a", 8),
         ("rel_bias", 1024))
SMALL_PAD = 2432


def _pack_small(parts):
    flat = jnp.concatenate([parts[n].reshape(-1) for n, _ in SMALL])
    return jnp.pad(flat, (0, SMALL_PAD - flat.shape[0])).reshape(1, SMALL_PAD)


def _unpack_small(flat, shapes):
    out, off = {}, 0
    for n, sz in SMALL:
        out[n] = flat[0, off:off + sz].reshape(shapes[n])
        off += sz
    return out


def kernel(x, norm_gain, w_in, q_norm_a, k_norm_a, q_norm_b, k_norm_b, sink_a, rel_bias, w_branch_a, w_branch_b, b_merge, w_out, loss_target, m_norm_gain, m_w_in, m_q_norm_a, m_k_norm_a, m_q_norm_b, m_k_norm_b, m_sink_a, m_rel_bias, m_w_branch_a, m_w_branch_b, m_b_merge, m_w_out, v_norm_gain, v_w_in, v_q_norm_a, v_k_norm_a, v_q_norm_b, v_k_norm_b, v_sink_a, v_rel_bias, v_w_branch_a, v_w_branch_b, v_b_merge, v_w_out):
    wsh = NW // NDEV
    csh = D // NDEV
    g_in, g_a, g_b, g_o, g_bm = _exchange(
        [], [w_in[0].astype(bf16), w_branch_a[0].astype(bf16), w_branch_b[0].astype(bf16), w_out[0].astype(bf16),
             b_merge[0]], "gather_weights")
    w_in_full = g_in.transpose(1, 0, 2).reshape(D, NW)
    w_a_full = g_a.transpose(1, 0, 2).reshape(512, D)
    w_b_full = g_b.transpose(1, 0, 2).reshape(512, D)
    w_o_full = g_o.reshape(D, D)
    bm_full = g_bm.transpose(1, 0, 2).reshape(2, D)

    loc = _local_step(x[0], loss_target[0], norm_gain, w_in_full, q_norm_a, k_norm_a, q_norm_b, k_norm_b, sink_a,
                      rel_bias, w_a_full, w_b_full, bm_full, w_o_full)

    small_shapes = dict(norm_gain=(1, D), q_norm_a=(1, HD), k_norm_a=(1, HD), q_norm_b=(1, HD), k_norm_b=(1, HD),
                        sink_a=(1, 8), rel_bias=(32, 32))
    r_in, r_a, r_b, r_o, r_bm, r_small = _exchange(
        [loc["w_in"].reshape(D, NDEV, wsh).transpose(1, 0, 2),
         loc["w_branch_a"].reshape(512, NDEV, csh).transpose(1, 0, 2),
         loc["w_branch_b"].reshape(512, NDEV, csh).transpose(1, 0, 2),
         loc["w_out"].reshape(NDEV, csh, D),
         loc["b_merge"].reshape(2, NDEV, csh).transpose(1, 0, 2)],
        [_pack_small(loc)], "scatter_grads")

    given = dict(norm_gain=norm_gain, q_norm_a=q_norm_a, k_norm_a=k_norm_a, q_norm_b=q_norm_b, k_norm_b=k_norm_b,
                 sink_a=sink_a, rel_bias=rel_bias)
    m_small = dict(norm_gain=m_norm_gain, q_norm_a=m_q_norm_a, k_norm_a=m_k_norm_a, q_norm_b=m_q_norm_b,
                   k_norm_b=m_k_norm_b, sink_a=m_sink_a, rel_bias=m_rel_bias)
    v_small = dict(norm_gain=v_norm_gain, q_norm_a=v_q_norm_a, k_norm_a=v_k_norm_a, q_norm_b=v_q_norm_b,
                   k_norm_b=v_k_norm_b, sink_a=v_sink_a, rel_bias=v_rel_bias)
    res = {
        "small": _adamw(_pack_small(given), r_small, _pack_small(m_small), _pack_small(v_small), "adamw_small"),
        "w_in": _adamw(w_in[0], r_in, m_w_in[0], v_w_in[0], "adamw_w_in"),
        "w_branch_a": _adamw(w_branch_a[0], r_a, m_w_branch_a[0], v_w_branch_a[0], "adamw_w_branch_a"),
        "w_branch_b": _adamw(w_branch_b[0], r_b, m_w_branch_b[0], v_w_branch_b[0], "adamw_w_branch_b"),
        "b_merge": _adamw(b_merge[0], r_bm, m_b_merge[0], v_b_merge[0], "adamw_b_merge"),
        "w_out": _adamw(w_out[0], r_o, m_w_out[0], v_w_out[0], "adamw_w_out"),
    }
    order = ["norm_gain", "w_in", "q_norm_a", "k_norm_a", "q_norm_b", "k_norm_b", "sink_a", "rel_bias", "w_branch_a",
             "w_branch_b", "b_merge", "w_out"]
    outs = []
    for k in range(4):
        small = _unpack_small(res["small"][k], small_shapes)
        for n in order:
            outs.append(small[n] if n in small else res[n][k][None])
    loss = lax.psum(loc["loss"][0, 0], ("x", "y", "c"))
    return (loss, loc["grad_x"][None], *outs)
```

```python
import math

import numpy as np
import jax
import jax.numpy as jnp
from jax import lax
from jax.experimental import pallas as pl
from jax.experimental.pallas import tpu as pltpu

f32 = jnp.float32
bf16 = jnp.bfloat16

S = 4096
D = 1024
NA = 5376
NT = 3072
NW = NA + NT
HD = 64
LANES = 128
EPS = 1e-6
NEG = -1e30
SCALE = HD ** -0.5
TQ = 128
PAD = 128
SP = S + 2 * PAD
NDEV = 8
GROUPS = ((128, 1, 0), (64, 1, 8), (64, 4, 16), (64, 16, 24))
CHUNK = 256
TN = 768

ADAM_LR, ADAM_B1, ADAM_B2, ADAM_EPS, ADAM_WD, ADAM_STEP = 0.001, 0.9, 0.999, 1e-08, 0.01, 10

MIB = 1024 * 1024
NT_DIMS = (((1,), (1,)), ((), ()))
TN_DIMS = (((0,), (0,)), ((), ()))


def _params(sem=None, vmem_mib=48):
    return pltpu.CompilerParams(dimension_semantics=sem, vmem_limit_bytes=vmem_mib * MIB)


def _lo():
    return lax.broadcasted_iota(jnp.int32, (1, LANES), 1) < HD


def _seg_sum(x):
    lo = _lo()
    outs = []
    for b in range(x.shape[1] // LANES):
        xb = x[:, b * LANES:(b + 1) * LANES]
        s0 = jnp.sum(jnp.where(lo, xb, 0.0), axis=1, keepdims=True)
        s1 = jnp.sum(jnp.where(lo, 0.0, xb), axis=1, keepdims=True)
        outs.append(jnp.where(lo, s0, s1))
    return outs[0] if len(outs) == 1 else jnp.concatenate(outs, axis=1)


def _bucket_np(blk, stride):
    w = TQ + 2 * blk
    rel = np.arange(w)[None, :] - blk - np.arange(TQ)[:, None]
    band = np.abs(rel) <= blk
    r = rel * stride
    n = np.abs(r)
    nf = np.maximum(n, 8).astype(np.float32)
    large = 8 + (np.log(nf / np.float32(8)) / np.float32(math.log(128.0)) * np.float32(8)).astype(np.int32)
    large = np.minimum(large, 15)
    b = (r > 0).astype(np.int32) * 16 + np.where(n < 8, n, large)
    return np.where(band, b, -1).astype(np.int32)


def _rms_inproj(x, gain, w):
    ts = 512
    na, nt = NA // TN, NT // TN

    def body(x_ref, g_ref, w_ref, pa_ref, pt_ref, h_ref, r_ref, hs):
        j = pl.program_id(1)

        @pl.when(j == 0)
        def _():
            xv = x_ref[...]
            r = lax.rsqrt(jnp.mean(xv * xv, axis=-1, keepdims=True) + EPS)
            hs[...] = ((xv * r) * g_ref[...]).astype(bf16)
            h_ref[...] = hs[...]
            r_ref[...] = r

        acc = jnp.dot(hs[...], w_ref[...], preferred_element_type=f32)

        @pl.when(j < na)
        def _():
            pa_ref[...] = acc

        @pl.when(j >= na)
        def _():
            pt_ref[...] = acc

    return pl.pallas_call(
        body,
        grid=(S // ts, na + nt),
        in_specs=[
            pl.BlockSpec((ts, D), lambda i, j: (i, 0)),
            pl.BlockSpec((1, D), lambda i, j: (0, 0)),
            pl.BlockSpec((D, TN), lambda i, j: (0, j)),
        ],
        out_specs=[
            pl.BlockSpec((ts, TN), lambda i, j: (i, jnp.minimum(j, na - 1))),
            pl.BlockSpec((ts, TN), lambda i, j: (i, jnp.maximum(j - na, 0))),
            pl.BlockSpec((ts, D), lambda i, j: (i, 0)),
            pl.BlockSpec((ts, 1), lambda i, j: (i, 0)),
        ],
        out_shape=[
            jax.ShapeDtypeStruct((S, NA), f32),
            jax.ShapeDtypeStruct((S, NT), f32),
            jax.ShapeDtypeStruct((S, D), bf16),
            jax.ShapeDtypeStruct((S, 1), f32),
        ],
        scratch_shapes=[pltpu.VMEM((ts, D), bf16)],
        compiler_params=_params(("arbitrary", "arbitrary")),
        name="rms_inproj",
    )(x, gain, w)


def _bias_expand(table, bucket, c0, name):
    tq, w = bucket.shape

    def body(tab_ref, bk_ref, o_ref):
        h = pl.program_id(0)
        bk = bk_ref[...]

        def step(b, acc):
            return jnp.where(bk == b, tab_ref[b, c0 + h], acc)

        o_ref[...] = lax.fori_loop(0, 32, step, jnp.full((tq, w), NEG, f32))

    return pl.pallas_call(
        body,
        grid=(8,),
        in_specs=[pl.BlockSpec(memory_space=pltpu.SMEM), pl.BlockSpec((tq, w), lambda h: (0, 0))],
        out_specs=pl.BlockSpec((None, tq, w), lambda h: (h, 0, 0)),
        out_shape=jax.ShapeDtypeStruct((8, tq, w), f32),
        compiler_params=_params(("arbitrary",)),
        name=name,
    )(table, bucket)


def _col_block(g, j):
    kind = j // 4
    hp = j % 4
    a = jnp.where(kind == 0, hp, 3 + kind)
    b = 6 + 12 * kind + 4 * (g - 1) + hp
    return jnp.where(g == 0, a, b)


def _half_sums(x):
    lo = _lo()
    s0 = jnp.sum(jnp.where(lo, x, 0.0), axis=1, keepdims=True)
    s1 = jnp.sum(jnp.where(lo, 0.0, x), axis=1, keepdims=True)
    return jnp.where(lo, s0, s1)


def _prep(proj_a, gains):
    def body(p_ref, g_ref, o_ref):
        g = pl.program_id(0)
        j = pl.program_id(1)
        kind = j // 4
        lo = _lo()
        half = jnp.where(lo, 0, 1)
        take = (kind == 0) | (half == (j % 4) // 2)
        gain = g_ref[...]
        o_ref[0:PAD, :] = jnp.zeros((PAD, LANES), bf16)
        o_ref[PAD + S:SP, :] = jnp.zeros((PAD, LANES), bf16)

        def norm_store(xv, dst, dup):
            if dup:
                xv = jnp.where(take, xv, pltpu.roll(xv, HD, 1))
            r = lax.rsqrt(_half_sums(xv * xv) * (1.0 / HD) + EPS)
            r = jnp.where(kind == 2, 1.0, r)
            yv = (xv * r) * gain
            yv = jnp.where(kind == 0, yv * SCALE, yv)
            o_ref[PAD + dst:PAD + dst + CHUNK, :] = yv.astype(bf16)

        for gi, (_, d, _) in enumerate(GROUPS):
            @pl.when(g == gi)
            def _():
                seq = S // d
                for c in range(d):
                    for i in range(seq // CHUNK):
                        if d == 1:
                            xv = p_ref[i * CHUNK:(i + 1) * CHUNK, :]
                        else:
                            xv = p_ref[pl.ds(c + i * CHUNK * d, CHUNK, stride=d), :]
                        norm_store(xv, c * seq + i * CHUNK, gi == 0)

    return pl.pallas_call(
        body,
        grid=(4, 12),
        in_specs=[
            pl.BlockSpec((S, LANES), lambda g, j: (0, _col_block(g, j))),
            pl.BlockSpec((None, None, 1, LANES), lambda g, j: (g, j // 4, 0, 0)),
        ],
        out_specs=pl.BlockSpec((None, None, SP, LANES), lambda g, j: (g, j, 0, 0)),
        out_shape=jax.ShapeDtypeStruct((4, 12, SP, LANES), bf16),
        compiler_params=_params(("arbitrary", "arbitrary")),
        name="prep",
    )(proj_a, gains)


def _seq_len(g, ng):
    return S if ng == 1 else jnp.right_shift(S, 2 * g)


def _stack_heads(t, lo):
    z = jnp.zeros_like(t)
    return jnp.concatenate([jnp.where(lo, t, z), jnp.where(lo, z, t)], axis=0)


def _unstack_heads(t2, lo):
    return jnp.where(lo, t2[:TQ], t2[TQ:])


def _row_spec(natural, ng):
    if natural:
        return pl.BlockSpec((S, LANES), lambda g, hp: (0, hp)), (S, 4 * LANES)
    return pl.BlockSpec((None, None, S, LANES), lambda g, hp: (g, hp, 0, 0)), (ng, 4, S, LANES)


def _attn_fwd(gl, bias, sink, g0, ng, blk, natural, name):
    w = TQ + 2 * blk
    use_sink = sink is not None

    def body(*refs):
        if use_sink:
            sink_ref, q_ref, k_ref, v_ref, b_ref, o_ref, l_ref = refs
        else:
            q_ref, k_ref, v_ref, b_ref, o_ref, l_ref = refs
        g = pl.program_id(0)
        hp = pl.program_id(1)
        lo = _lo()
        seq = _seq_len(g, ng)
        mi = lax.broadcasted_iota(jnp.int32, (1, w), 1)
        if use_sink:
            top = lax.broadcasted_iota(jnp.int32, (2 * TQ, 1), 0) < TQ
            sk = jnp.where(top, sink_ref[2 * hp], sink_ref[2 * hp + 1])

        def tile(t, carry):
            f0 = pl.multiple_of(t * TQ, TQ)
            m0 = jnp.bitwise_and(f0, seq - 1)
            q2 = _stack_heads(q_ref[pl.ds(PAD + f0, TQ), :], lo)
            kw = k_ref[pl.ds(PAD - blk + f0, w), :]
            vw = v_ref[pl.ds(PAD - blk + f0, w), :]
            inside = (mi >= blk - m0) & (mi < seq + blk - m0)
            s = lax.dot_general(q2, kw, NT_DIMS, preferred_element_type=f32)
            logit = jnp.where(inside, s + b_ref[...].reshape(2 * TQ, w), NEG)
            m = jnp.max(logit, axis=1, keepdims=True)
            e = jnp.exp(logit - m)
            lse = m + jnp.log(jnp.sum(e, axis=1, keepdims=True))
            if use_sink:
                mx = jnp.maximum(lse, sk)
                lse = mx + jnp.log(jnp.exp(lse - mx) + jnp.exp(sk - mx))
            p = e * jnp.exp(m - lse)
            o2 = jnp.dot(p.astype(bf16), vw, preferred_element_type=f32)
            o_ref[pl.ds(f0, TQ), :] = _unstack_heads(o2, lo)
            l_ref[pl.ds(f0, TQ), :] = jnp.where(lo, lse[:TQ], lse[TQ:])
            return carry

        lax.fori_loop(0, S // TQ, tile, 0, unroll=2)

    in_specs = [
        pl.BlockSpec((None, None, SP, LANES), lambda g, hp: (g0 + g, hp, 0, 0)),
        pl.BlockSpec((None, None, SP, LANES), lambda g, hp: (g0 + g, 4 + hp, 0, 0)),
        pl.BlockSpec((None, None, SP, LANES), lambda g, hp: (g0 + g, 8 + hp, 0, 0)),
        pl.BlockSpec((None, 2, TQ, w), lambda g, hp: (g, hp, 0, 0)),
    ]
    args = [gl, gl, gl, bias]
    if use_sink:
        in_specs = [pl.BlockSpec(memory_space=pltpu.SMEM)] + in_specs
        args = [sink] + args
    out, shape = _row_spec(natural, ng)
    return pl.pallas_call(
        body,
        grid=(ng, 4),
        in_specs=in_specs,
        out_specs=[out, out],
        out_shape=[jax.ShapeDtypeStruct(shape, f32)] * 2,
        compiler_params=_params(("arbitrary", "arbitrary")),
        name=name,
    )(*args)


def _attn_bwd(gl, bias, bucket, do, lse, dd, g0, ng, blk, natural, name):
    w = TQ + 2 * blk

    def body(q_ref, k_ref, v_ref, b_ref, bk_ref, do_ref, l_ref, d_ref, dq_ref, dk_ref, dv_ref, dbk_ref, db_acc):
        g = pl.program_id(0)
        lo = _lo()
        hi = jnp.logical_not(lo)
        seq = _seq_len(g, ng)
        mi = lax.broadcasted_iota(jnp.int32, (1, w), 1)
        dk_ref[...] = jnp.zeros((SP, LANES), f32)
        dv_ref[...] = jnp.zeros((SP, LANES), f32)
        db_acc[...] = jnp.zeros((2 * TQ, w), f32)

        def head_rows(tb):
            return jnp.concatenate([
                jnp.max(jnp.where(lo, tb, -jnp.inf), axis=1, keepdims=True),
                jnp.max(jnp.where(hi, tb, -jnp.inf), axis=1, keepdims=True)], axis=0)

        def tile(t, carry):
            f0 = pl.multiple_of(t * TQ, TQ)
            m0 = jnp.bitwise_and(f0, seq - 1)
            q2 = _stack_heads(q_ref[pl.ds(PAD + f0, TQ), :], lo)
            do2 = _stack_heads(do_ref[pl.ds(f0, TQ), :], lo)
            kw = k_ref[pl.ds(PAD - blk + f0, w), :]
            vw = v_ref[pl.ds(PAD - blk + f0, w), :]
            lh = head_rows(l_ref[pl.ds(f0, TQ), :])
            dh = head_rows(d_ref[pl.ds(f0, TQ), :])
            inside = (mi >= blk - m0) & (mi < seq + blk - m0)
            s = lax.dot_general(q2, kw, NT_DIMS, preferred_element_type=f32)
            dp = lax.dot_general(do2, vw, NT_DIMS, preferred_element_type=f32)
            logit = jnp.where(inside, s + b_ref[...].reshape(2 * TQ, w), NEG)
            p = jnp.exp(logit - lh)
            ds = p * (dp - dh)
            db_acc[...] += ds
            dsb = ds.astype(bf16)
            dq2 = jnp.dot(dsb, kw, preferred_element_type=f32)
            dkw = lax.dot_general(dsb, q2, TN_DIMS, preferred_element_type=f32)
            dvw = lax.dot_general(p.astype(bf16), do2, TN_DIMS, preferred_element_type=f32)
            dq_ref[pl.ds(f0, TQ), :] = _unstack_heads(dq2, lo)
            dk_ref[pl.ds(PAD - blk + f0, w), :] += dkw
            dv_ref[pl.ds(PAD - blk + f0, w), :] += dvw
            return carry

        lax.fori_loop(0, S // TQ, tile, 0, unroll=2)

        bk = bk_ref[...]
        lane = lax.broadcasted_iota(jnp.int32, (8, LANES), 1)
        for h in range(2):
            db = db_acc[h * TQ:(h + 1) * TQ, :]
            acc = jnp.zeros((8, LANES), f32)
            for b in range(32):
                part = jnp.where(bk == b, db, 0.0).reshape(TQ // 8, 8, w).sum(axis=0)
                tot = jnp.sum(jnp.sum(part, axis=1, keepdims=True), axis=0, keepdims=True)
                acc = jnp.where(lane == b, tot, acc)
            dbk_ref[h] = acc

    def gcol(off):
        return pl.BlockSpec((None, None, SP, LANES), lambda g, hp: (g0 + g, off + hp, 0, 0))

    row, row_shape = _row_spec(natural, ng)
    rowp = pl.BlockSpec((None, None, SP, LANES), lambda g, hp: (g, hp, 0, 0))
    return pl.pallas_call(
        body,
        grid=(ng, 4),
        in_specs=[gcol(0), gcol(4), gcol(8), pl.BlockSpec((None, 2, TQ, w), lambda g, hp: (g, hp, 0, 0)),
                  pl.BlockSpec((None, TQ, w), lambda g, hp: (g, 0, 0)), row, row, row],
        out_specs=[row, rowp, rowp, pl.BlockSpec((None, 2, 8, LANES), lambda g, hp: (g, hp, 0, 0))],
        out_shape=[
            jax.ShapeDtypeStruct(row_shape, f32),
            jax.ShapeDtypeStruct((ng, 4, SP, LANES), f32),
            jax.ShapeDtypeStruct((ng, 4, SP, LANES), f32),
            jax.ShapeDtypeStruct((ng, 8, 8, LANES), f32),
        ],
        scratch_shapes=[pltpu.VMEM((2 * TQ, w), f32)],
        compiler_params=_params(("arbitrary", "arbitrary"), vmem_mib=56),
        name=name,
    )(gl, gl, gl, bias, bucket, do, lse, dd)


def _b_to_natural(o_gl, l_gl):
    def body(o_ref, l_ref, on_ref, ln_ref):
        g = pl.program_id(0)
        for gi in range(3):
            d = GROUPS[gi + 1][1]

            @pl.when(g == gi)
            def _():
                seq = S // d
                for c in range(d):
                    for i in range(seq // CHUNK):
                        src = slice(c * seq + i * CHUNK, c * seq + (i + 1) * CHUNK)
                        if d == 1:
                            on_ref[src, :] = o_ref[src, :]
                            ln_ref[src, :] = l_ref[src, :]
                        else:
                            dst = pl.ds(c + i * CHUNK * d, CHUNK, stride=d)
                            on_ref[dst, :] = o_ref[src, :]
                            ln_ref[dst, :] = l_ref[src, :]

    col = pl.BlockSpec((None, None, S, LANES), lambda g, hp: (g, hp, 0, 0))
    nat = pl.BlockSpec((S, LANES), lambda g, hp: (0, 4 * g + hp))
    return pl.pallas_call(
        body,
        grid=(3, 4),
        in_specs=[col, col],
        out_specs=[nat, nat],
        out_shape=[jax.ShapeDtypeStruct((S, 3 * 512), f32)] * 2,
        compiler_params=_params(("arbitrary", "arbitrary")),
        name="b_to_natural",
    )(o_gl, l_gl)


def _b_from_natural(do_n, dd_n):
    def body(do_ref, dd_ref, dog_ref, ddg_ref):
        g = pl.program_id(0)
        for gi in range(3):
            d = GROUPS[gi + 1][1]

            @pl.when(g == gi)
            def _():
                seq = S // d
                for c in range(d):
                    for i in range(seq // CHUNK):
                        dst = slice(c * seq + i * CHUNK, c * seq + (i + 1) * CHUNK)
                        if d == 1:
                            a, b = do_ref[dst, :], dd_ref[dst, :]
                        else:
                            src = pl.ds(c + i * CHUNK * d, CHUNK, stride=d)
                            a, b = do_ref[src, :], dd_ref[src, :]
                        dog_ref[dst, :] = a.astype(bf16)
                        ddg_ref[dst, :] = b

    col = pl.BlockSpec((None, None, S, LANES), lambda g, hp: (g, hp, 0, 0))
    nat = pl.BlockSpec((S, LANES), lambda g, hp: (0, 4 * g + hp))
    return pl.pallas_call(
        body,
        grid=(3, 4),
        in_specs=[nat, nat],
        out_specs=[col, col],
        out_shape=[jax.ShapeDtypeStruct((3, 4, S, LANES), bf16), jax.ShapeDtypeStruct((3, 4, S, LANES), f32)],
        compiler_params=_params(("arbitrary", "arbitrary")),
        name="b_from_natural",
    )(do_n, dd_n)


def _sigmoid(z):
    return 1.0 / (1.0 + jnp.exp(-z))


def _tail(x, tgt, o_a, l_a, o_b, l_b, proj_t, bm, w_a, w_b, w_o, sink_b):
    ts = 128

    def body(x_ref, t_ref, oa_ref, la_ref, ob_ref, lb_ref, ga_ref, gb_ref, m0_ref, m1_ref, bm_ref,
             wa_ref, wb_ref, wo_ref, sk_ref,
             dy_ref, dyb_ref, dt_ref, doa_ref, dda_ref, dob_ref, ddb_ref, ya_ref, yb_ref, mg_ref, dbra_ref, dbrb_ref,
             loss_ref, dbm_ref, dsk_ref):
        i = pl.program_id(0)

        @pl.when(i == 0)
        def _():
            loss_ref[...] = jnp.zeros_like(loss_ref)
            dbm_ref[...] = jnp.zeros_like(dbm_ref)
            dsk_ref[...] = jnp.zeros_like(dsk_ref)

        ga = ga_ref[...]
        sa = _sigmoid(ga)
        silu_a = ga * sa
        oa = oa_ref[...]
        ya = oa * silu_a
        gb = gb_ref[...]
        sb = _sigmoid(gb)
        silu_b = gb * sb
        ob = [ob_ref[:, k * 512:(k + 1) * 512] for k in range(3)]
        lb = [lb_ref[:, k * 512:(k + 1) * 512] for k in range(3)]
        mx = jnp.maximum(jnp.maximum(lb[0], lb[1]), lb[2])
        ex = [jnp.exp(v - mx) for v in lb]
        den = ex[0] + ex[1] + ex[2]
        alpha = [e / den for e in ex]
        ybc = alpha[0] * ob[0] + alpha[1] * ob[1] + alpha[2] * ob[2]
        yb = ybc * silu_b
        yab = ya.astype(bf16)
        ybb = yb.astype(bf16)
        br_a = jnp.dot(yab, wa_ref[...], preferred_element_type=f32)
        br_b = jnp.dot(ybb, wb_ref[...], preferred_element_type=f32)
        g0 = _sigmoid(m0_ref[...] + bm_ref[0:1, :])
        g1 = _sigmoid(m1_ref[...] + bm_ref[1:2, :])
        merged = g0 * br_a + g1 * br_b
        mgb = merged.astype(bf16)
        y = x_ref[...] + jnp.dot(mgb, wo_ref[...], preferred_element_type=f32)
        err = y - t_ref[...]
        part = jnp.sum(jnp.sum(err * err, axis=1, keepdims=True), axis=0, keepdims=True)
        loss_ref[...] += part * (0.5 / D)
        dy = err * (1.0 / D)
        dyb = dy.astype(bf16)
        dmerged = lax.dot_general(dyb, wo_ref[...], NT_DIMS, preferred_element_type=f32)
        dbr_a = (dmerged * g0).astype(bf16)
        dbr_b = (dmerged * g1).astype(bf16)
        dm0 = dmerged * br_a * (g0 * (1.0 - g0))
        dm1 = dmerged * br_b * (g1 * (1.0 - g1))
        dbm_ref[0:1, :] += jnp.sum(dm0, axis=0, keepdims=True)
        dbm_ref[1:2, :] += jnp.sum(dm1, axis=0, keepdims=True)
        dya = lax.dot_general(dbr_a, wa_ref[...], NT_DIMS, preferred_element_type=f32)
        dyb2 = lax.dot_general(dbr_b, wb_ref[...], NT_DIMS, preferred_element_type=f32)
        do_a = dya * silu_a
        dga = dya * oa * (sa * (1.0 + ga * (1.0 - sa)))
        delta_a = _seg_sum(do_a * oa)
        dsk_ref[...] -= jnp.sum(delta_a * jnp.exp(sk_ref[...] - la_ref[...]), axis=0, keepdims=True)
        dybc = dyb2 * silu_b
        dgb = dyb2 * ybc * (sb * (1.0 + gb * (1.0 - sb)))
        dbar = _seg_sum(dybc * ybc)
        dy_ref[...] = dy
        dyb_ref[...] = dyb
        dt_ref[:, 0:512] = dga.astype(bf16)
        dt_ref[:, 512:1024] = dgb.astype(bf16)
        dt_ref[:, 1024:2048] = dm0.astype(bf16)
        dt_ref[:, 2048:3072] = dm1.astype(bf16)
        doa_ref[...] = do_a.astype(bf16)
        dda_ref[...] = delta_a
        for k in range(3):
            dob_ref[:, k * 512:(k + 1) * 512] = alpha[k] * dybc
            ddb_ref[:, k * 512:(k + 1) * 512] = alpha[k] * dbar
        ya_ref[...] = yab
        yb_ref[...] = ybb
        mg_ref[...] = mgb
        dbra_ref[...] = dbr_a
        dbrb_ref[...] = dbr_b

    def rows(n, blk=0):
        return pl.BlockSpec((ts, n), lambda i: (i, blk))

    def whole(r, c):
        return pl.BlockSpec((r, c), lambda i: (0, 0))

    outs = [
        ((S, D), f32, rows(D)), ((S, D), bf16, rows(D)), ((S, NT), bf16, rows(NT)),
        ((S, 512), bf16, rows(512)), ((S, 512), f32, rows(512)),
        ((S, 1536), f32, rows(1536)), ((S, 1536), f32, rows(1536)),
        ((S, 512), bf16, rows(512)), ((S, 512), bf16, rows(512)), ((S, D), bf16, rows(D)),
        ((S, D), bf16, rows(D)), ((S, D), bf16, rows(D)),
        ((1, 1), f32, whole(1, 1)), ((2, D), f32, whole(2, D)), ((1, 512), f32, whole(1, 512)),
    ]
    return pl.pallas_call(
        body,
        grid=(S // ts,),
        in_specs=[
            rows(D), rows(D), rows(512), rows(512), rows(1536), rows(1536),
            rows(512, 0), rows(512, 1), rows(D, 1), rows(D, 2), whole(2, D),
            whole(512, D), whole(512, D), whole(D, D), whole(1, 512),
        ],
        out_specs=[o[2] for o in outs],
        out_shape=[jax.ShapeDtypeStruct(o[0], o[1]) for o in outs],
        compiler_params=_params(("arbitrary",)),
        name="tail",
    )(x, tgt, o_a, l_a, o_b, l_b, proj_t, proj_t, proj_t, proj_t, bm, w_a, w_b, w_o, sink_b)


def _norm_bwd(xv, dyv, gain, kind):
    r = lax.rsqrt(_half_sums(xv * xv) * (1.0 / HD) + EPS)
    yv = xv * r
    up = jnp.where(kind == 0, dyv * SCALE, dyv)
    u = up * gain
    dxv = r * (u - yv * (_half_sums(u * yv) * (1.0 / HD)))
    dxv = jnp.where(kind == 2, dyv, dxv)
    dg = jnp.where(kind == 2, 0.0, jnp.sum(up * yv, axis=0, keepdims=True))
    return dxv, dg


def _post_b(dq, dk, dv, proj_a, gains, dproj):
    def body(dq_ref, dk_ref, dv_ref, p_ref, g_ref, alias_ref, o_ref, dg_ref, nat):
        del alias_ref
        g = pl.program_id(0)
        j = pl.program_id(1)
        kind = j // 4
        gain = g_ref[...]

        @pl.when(j % 4 == 0)
        def _():
            dg_ref[...] = jnp.zeros_like(dg_ref)

        for gi in range(3):
            d = GROUPS[gi + 1][1]

            @pl.when(g == gi)
            def _():
                seq = S // d
                for c in range(d):
                    for i in range(seq // CHUNK):
                        src = c * seq + i * CHUNK
                        a = dq_ref[src:src + CHUNK, :]
                        b = dk_ref[PAD + src:PAD + src + CHUNK, :]
                        e = dv_ref[PAD + src:PAD + src + CHUNK, :]
                        dyv = jnp.where(kind == 0, a, jnp.where(kind == 1, b, e))
                        if d == 1:
                            idx = slice(src, src + CHUNK)
                        else:
                            idx = pl.ds(c + i * CHUNK * d, CHUNK, stride=d)
                        dxv, dg = _norm_bwd(p_ref[idx, :], dyv, gain, kind)
                        nat[idx, :] = dxv
                        dg_ref[...] += dg

        for i in range(S // CHUNK):
            o_ref[i * CHUNK:(i + 1) * CHUNK, :] = nat[i * CHUNK:(i + 1) * CHUNK, :].astype(bf16)

    def gcol(n, off):
        return pl.BlockSpec((None, None, n, LANES), lambda g, j: (g, jnp.clip(j - off, 0, 3), 0, 0))

    return pl.pallas_call(
        body,
        grid=(3, 12),
        in_specs=[
            gcol(S, 0), gcol(SP, 4), gcol(SP, 8),
            pl.BlockSpec((S, LANES), lambda g, j: (0, _col_block(g + 1, j))),
            pl.BlockSpec((None, None, 1, LANES), lambda g, j: (g + 1, j // 4, 0, 0)),
            pl.BlockSpec(memory_space=pl.ANY),
        ],
        out_specs=[
            pl.BlockSpec((S, LANES), lambda g, j: (0, _col_block(g + 1, j))),
            pl.BlockSpec((None, None, 1, LANES), lambda g, j: (g, j // 4, 0, 0)),
        ],
        out_shape=[jax.ShapeDtypeStruct((S, NA), bf16), jax.ShapeDtypeStruct((3, 3, 1, LANES), f32)],
        scratch_shapes=[pltpu.VMEM((S, LANES), f32)],
        input_output_aliases={5: 0},
        compiler_params=_params(("arbitrary", "arbitrary")),
        name="post_b",
    )(dq, dk, dv, proj_a, gains, dproj)


def _post_a(dq, dk, dv, proj_a, gains):
    def body(dq_ref, dk_ref, dv_ref, p_ref, g_ref, o_ref, dg_ref):
        j = pl.program_id(0)
        kind = jnp.maximum(j - 3, 0)
        gain = g_ref[...]
        lo = _lo()

        @pl.when((j == 0) | (j >= 4))
        def _():
            dg_ref[...] = jnp.zeros_like(dg_ref)

        def fold(ref, r0):
            t0 = ref[0, PAD + r0:PAD + r0 + CHUNK, :] + ref[1, PAD + r0:PAD + r0 + CHUNK, :]
            t1 = ref[2, PAD + r0:PAD + r0 + CHUNK, :] + ref[3, PAD + r0:PAD + r0 + CHUNK, :]
            return jnp.where(lo, t0 + pltpu.roll(t0, HD, 1), t1 + pltpu.roll(t1, HD, 1))

        for i in range(S // CHUNK):
            r0 = i * CHUNK
            a = dq_ref[r0:r0 + CHUNK, :]
            dyv = jnp.where(kind == 0, a, jnp.where(kind == 1, fold(dk_ref, r0), fold(dv_ref, r0)))
            dxv, dg = _norm_bwd(p_ref[r0:r0 + CHUNK, :], dyv, gain, kind)
            o_ref[r0:r0 + CHUNK, :] = dxv.astype(bf16)
            dg_ref[...] += dg

    exp = pl.BlockSpec((None, 4, SP, LANES), lambda j: (0, 0, 0, 0))
    return pl.pallas_call(
        body,
        grid=(6,),
        in_specs=[
            pl.BlockSpec((S, LANES), lambda j: (0, jnp.minimum(j, 3))),
            exp, exp,
            pl.BlockSpec((S, LANES), lambda j: (0, j)),
            pl.BlockSpec((None, None, 1, LANES), lambda j: (0, jnp.maximum(j - 3, 0), 0, 0)),
        ],
        out_specs=[
            pl.BlockSpec((S, LANES), lambda j: (0, j)),
            pl.BlockSpec((None, 1, LANES), lambda j: (jnp.maximum(j - 3, 0), 0, 0)),
        ],
        out_shape=[jax.ShapeDtypeStruct((S, NA), bf16), jax.ShapeDtypeStruct((3, 1, LANES), f32)],
        compiler_params=_params(("arbitrary",)),
        name="post_a",
    )(dq, dk, dv, proj_a, gains)


def _dh_norm_bwd(dproj_a, dproj_t, w, x, rstd, gain, dy):
    ts = 512
    na, nt = NA // TN, NT // TN

    def body(da_ref, dt_ref, w_ref, x_ref, r_ref, g_ref, dy_ref, gx_ref, dgn_ref, acc):
        i = pl.program_id(0)
        k = pl.program_id(1)

        @pl.when((i == 0) & (k == 0))
        def _():
            dgn_ref[...] = jnp.zeros_like(dgn_ref)

        @pl.when(k == 0)
        def _():
            acc[...] = jnp.zeros_like(acc)

        @pl.when(k < na)
        def _():
            acc[...] += lax.dot_general(da_ref[...], w_ref[...], NT_DIMS, preferred_element_type=f32)

        @pl.when(k >= na)
        def _():
            acc[...] += lax.dot_general(dt_ref[...], w_ref[...], NT_DIMS, preferred_element_type=f32)

        @pl.when(k == na + nt - 1)
        def _():
            dh = acc[...]
            xh = x_ref[...] * r_ref[...]
            u = dh * g_ref[...]
            dx = r_ref[...] * (u - xh * jnp.mean(u * xh, axis=-1, keepdims=True))
            gx_ref[...] = dy_ref[...] + dx
            dgn_ref[...] += jnp.sum(dh * xh, axis=0, keepdims=True)

    return pl.pallas_call(
        body,
        grid=(S // ts, na + nt),
        in_specs=[
            pl.BlockSpec((ts, TN), lambda i, k: (i, jnp.minimum(k, na - 1))),
            pl.BlockSpec((ts, TN), lambda i, k: (i, jnp.maximum(k - na, 0))),
            pl.BlockSpec((D, TN), lambda i, k: (0, k)),
            pl.BlockSpec((ts, D), lambda i, k: (i, 0)),
            pl.BlockSpec((ts, 1), lambda i, k: (i, 0)),
            pl.BlockSpec((1, D), lambda i, k: (0, 0)),
            pl.BlockSpec((ts, D), lambda i, k: (i, 0)),
        ],
        out_specs=[pl.BlockSpec((ts, D), lambda i, k: (i, 0)), pl.BlockSpec((1, D), lambda i, k: (0, 0))],
        out_shape=[jax.ShapeDtypeStruct((S, D), f32), jax.ShapeDtypeStruct((1, D), f32)],
        scratch_shapes=[pltpu.VMEM((ts, D), f32)],
        compiler_params=_params(("arbitrary", "arbitrary")),
        name="dh_norm_bwd",
    )(dproj_a, dproj_t, w, x, rstd, gain, dy)


def _matmul_tn(a, b, name):
    m, n = a.shape[1], b.shape[1]
    tn = TN if n % TN == 0 else 512
    tk = 512

    def body(a_ref, b_ref, o_ref):
        @pl.when(pl.program_id(1) == 0)
        def _():
            o_ref[...] = jnp.zeros_like(o_ref)

        o_ref[...] += lax.dot_general(a_ref[...], b_ref[...], TN_DIMS, preferred_element_type=f32)

    return pl.pallas_call(
        body,
        grid=(n // tn, S // tk),
        in_specs=[pl.BlockSpec((tk, m), lambda j, k: (k, 0)), pl.BlockSpec((tk, tn), lambda j, k: (k, j))],
        out_specs=pl.BlockSpec((m, tn), lambda j, k: (0, j)),
        out_shape=jax.ShapeDtypeStruct((m, n), f32),
        compiler_params=_params(("arbitrary", "arbitrary")),
        name=name,
    )(a, b)


def _exchange(scatter, gather, name):
    arrs = list(scatter) + list(gather)
    n = len(arrs)
    ns = len(scatter)

    def body(*refs):
        ins, outs = refs[:n], refs[n:2 * n]
        send_sems, recv_sems, local_sems = refs[2 * n:]
        x, y, c = lax.axis_index("x"), lax.axis_index("y"), lax.axis_index("c")
        me = 4 * x + 2 * y + c
        local, remote = [], []
        for a in range(n):
            lc = pltpu.make_async_copy(ins[a].at[me] if a < ns else ins[a], outs[a].at[me], local_sems.at[a])
            lc.start()
            local.append(lc)
            for r in range(1, NDEV):
                px = 1 - x if r & 4 else x
                py = 1 - y if r & 2 else y
                pc = 1 - c if r & 1 else c
                cp = pltpu.make_async_remote_copy(
                    src_ref=ins[a].at[4 * px + 2 * py + pc] if a < ns else ins[a],
                    dst_ref=outs[a].at[me],
                    send_sem=send_sems.at[a, r - 1],
                    recv_sem=recv_sems.at[a, r - 1],
                    device_id=(px, py, pc),
                    device_id_type=pl.DeviceIdType.MESH,
                )
                cp.start()
                remote.append(cp)
        for cp in remote:
            cp.wait_recv()
        for cp in remote:
            cp.wait_send()
        for lc in local:
            lc.wait()

    out_shape = [jax.ShapeDtypeStruct(a.shape if i < ns else (NDEV,) + a.shape, a.dtype) for i, a in enumerate(arrs)]
    return pl.pallas_call(
        body,
        in_specs=[pl.BlockSpec(memory_space=pl.ANY)] * n,
        out_specs=[pl.BlockSpec(memory_space=pl.ANY)] * n,
        out_shape=out_shape,
        scratch_shapes=[
            pltpu.SemaphoreType.DMA((n, NDEV - 1)),
            pltpu.SemaphoreType.DMA((n, NDEV - 1)),
            pltpu.SemaphoreType.DMA((n,)),
        ],
        compiler_params=pltpu.CompilerParams(has_side_effects=True),
        name=name,
    )(*arrs)


def _adamw(w, slots, m, v, name):
    r, c = w.shape
    tr = 128 if r % 128 == 0 else r

    def body(w_ref, s_ref, m_ref, v_ref, g_ref, d_ref, nm_ref, nv_ref):
        g = s_ref[0].astype(f32)
        for k in range(1, NDEV):
            g = g + s_ref[k].astype(f32)
        mm = ADAM_B1 * m_ref[...] + (1.0 - ADAM_B1) * g
        vv = ADAM_B2 * v_ref[...] + (1.0 - ADAM_B2) * (g * g)
        m_hat = mm / (1.0 - ADAM_B1 ** ADAM_STEP)
        v_hat = vv / (1.0 - ADAM_B2 ** ADAM_STEP)
        g_ref[...] = g
        d_ref[...] = -ADAM_LR * (m_hat / (jnp.sqrt(v_hat) + ADAM_EPS) + ADAM_WD * w_ref[...])
        nm_ref[...] = mm
        nv_ref[...] = vv

    blk = pl.BlockSpec((tr, c), lambda i: (i, 0))
    return pl.pallas_call(
        body,
        grid=(r // tr,),
        in_specs=[blk, pl.BlockSpec((NDEV, tr, c), lambda i: (0, i, 0)), blk, blk],
        out_specs=[blk] * 4,
        out_shape=[jax.ShapeDtypeStruct((r, c), f32)] * 4,
        compiler_params=_params(("arbitrary",)),
        name=name,
    )(w, slots, m, v)


def _local_step(x, tgt, norm_gain, w_in, qn_a, kn_a, qn_b, kn_b, sink_a, rel_bias, w_a, w_b, b_merge, w_o):
    two = lambda t: jnp.concatenate([t, t], axis=-1).reshape(1, LANES)
    ones = jnp.ones((1, LANES), f32)
    gains = jnp.stack([
        jnp.stack([two(qn_a), two(kn_a), ones]),
        jnp.stack([two(qn_b), two(kn_b), ones]),
        jnp.stack([two(qn_b), two(kn_b), ones]),
        jnp.stack([two(qn_b), two(kn_b), ones]),
    ])
    buckets = [jnp.asarray(_bucket_np(blk, d)) for blk, d, _ in GROUPS]
    bias_a = _bias_expand(rel_bias, buckets[0], 0, "bias_expand_a")[None]
    bias_b = jnp.stack([_bias_expand(rel_bias, buckets[k], GROUPS[k][2], "bias_expand_b%d" % k) for k in (1, 2, 3)])

    proj_a, proj_t, hb, rstd = _rms_inproj(x, norm_gain, w_in)
    gl = _prep(proj_a, gains)
    o_a, l_a = _attn_fwd(gl, bias_a, sink_a.reshape(8), 0, 1, 128, True, "attn_fwd_a")
    o_bg, l_bg = _attn_fwd(gl, bias_b, None, 1, 3, 64, False, "attn_fwd_b")
    o_b, l_b = _b_to_natural(o_bg, l_bg)
    sink_b = jnp.repeat(sink_a.reshape(8), HD).reshape(1, 512)

    (dy, dyb, dproj_t, do_a, dd_a, do_b, dd_b, ya, yb, mg, dbr_a, dbr_b, loss, dbm, dsk) = _tail(
        x, tgt, o_a, l_a, o_b, l_b, proj_t, b_merge, w_a, w_b, w_o, sink_b)

    dq_a, dk_a, dv_a, dbk_a = _attn_bwd(gl, bias_a, buckets[0][None], do_a, l_a, dd_a, 0, 1, 128, True, "attn_bwd_a")
    do_bg, dd_bg = _b_from_natural(do_b, dd_b)
    dq_b, dk_b, dv_b, dbk_b = _attn_bwd(gl, bias_b, jnp.stack(buckets[1:]), do_bg, l_bg, dd_bg, 1, 3, 64, False,
                                        "attn_bwd_b")

    dproj_a, dg_a = _post_a(dq_a, dk_a, dv_a, proj_a, gains)
    dproj_a, dg_b = _post_b(dq_b, dk_b, dv_b, proj_a, gains, dproj_a)

    grad_x, d_norm_gain = _dh_norm_bwd(dproj_a, dproj_t, w_in, x, rstd, norm_gain, dy)
    dw_in = jnp.concatenate([_matmul_tn(hb, dproj_a, "dw_in_a"), _matmul_tn(hb, dproj_t, "dw_in_t")], axis=1)
    dw_o = _matmul_tn(mg, dyb, "dw_out")
    dw_a = _matmul_tn(ya, dbr_a, "dw_branch_a")
    dw_b = _matmul_tn(yb, dbr_b, "dw_branch_b")

    fold = lambda t: t[..., :HD] + t[..., HD:]
    d_qn_a = fold(dg_a[0, 0])
    d_kn_a = fold(dg_a[1, 0])
    d_qn_b = fold(dg_b[:, 0, 0].sum(axis=0))
    d_kn_b = fold(dg_b[:, 1, 0].sum(axis=0))
    d_sink = dsk.reshape(8, HD)[:, 0]
    red = jnp.concatenate([dbk_a, dbk_b])
    d_rel = red[:, :, 0, :32].reshape(32, 32).T
    return dict(loss=loss, grad_x=grad_x, norm_gain=d_norm_gain, w_in=dw_in, q_norm_a=d_qn_a, k_norm_a=d_kn_a,
                q_norm_b=d_qn_b, k_norm_b=d_kn_b, sink_a=d_sink, rel_bias=d_rel, w_branch_a=dw_a, w_branch_b=dw_b,
                b_merge=dbm, w_out=dw_o)


SMALL = (("norm_gain", D), ("q_norm_a", HD), ("k_norm_a", HD), ("q_norm_b", HD), ("k_norm_b", HD), ("sink_a", 8),
         ("rel_bias", 1024))
SMALL_PAD = 2432


SMALL_USED = sum(sz for _, sz in SMALL)


def _pack_small(parts, loss=None):
    tail = jnp.zeros((SMALL_PAD - SMALL_USED,), f32)
    if loss is not None:
        tail = tail.at[0].set(loss.reshape(()))
    return jnp.concatenate([parts[n].reshape(-1) for n, _ in SMALL] + [tail]).reshape(1, SMALL_PAD)


def _unpack_small(flat, shapes):
    out, off = {}, 0
    for n, sz in SMALL:
        out[n] = flat[0, off:off + sz].reshape(shapes[n])
        off += sz
    return out


def kernel(x, norm_gain, w_in, q_norm_a, k_norm_a, q_norm_b, k_norm_b, sink_a, rel_bias, w_branch_a, w_branch_b, b_merge, w_out, loss_target, m_norm_gain, m_w_in, m_q_norm_a, m_k_norm_a, m_q_norm_b, m_k_norm_b, m_sink_a, m_rel_bias, m_w_branch_a, m_w_branch_b, m_b_merge, m_w_out, v_norm_gain, v_w_in, v_q_norm_a, v_k_norm_a, v_q_norm_b, v_k_norm_b, v_sink_a, v_rel_bias, v_w_branch_a, v_w_branch_b, v_b_merge, v_w_out):
    wsh = NW // NDEV
    csh = D // NDEV
    g_in, g_a, g_b, g_o, g_bm = _exchange(
        [], [w_in[0].astype(bf16), w_branch_a[0].astype(bf16), w_branch_b[0].astype(bf16), w_out[0].astype(bf16),
             b_merge[0]], "gather_weights")
    w_in_full = g_in.transpose(1, 0, 2).reshape(D, NW)
    w_a_full = g_a.transpose(1, 0, 2).reshape(512, D)
    w_b_full = g_b.transpose(1, 0, 2).reshape(512, D)
    w_o_full = g_o.reshape(D, D)
    bm_full = g_bm.transpose(1, 0, 2).reshape(2, D)

    loc = _local_step(x[0], loss_target[0], norm_gain, w_in_full, q_norm_a, k_norm_a, q_norm_b, k_norm_b, sink_a,
                      rel_bias, w_a_full, w_b_full, bm_full, w_o_full)

    small_shapes = dict(norm_gain=(1, D), q_norm_a=(1, HD), k_norm_a=(1, HD), q_norm_b=(1, HD), k_norm_b=(1, HD),
                        sink_a=(1, 8), rel_bias=(32, 32))
    r_in, r_a, r_b, r_o, r_bm, r_small = _exchange(
        [loc["w_in"].reshape(D, NDEV, wsh).transpose(1, 0, 2).astype(bf16),
         loc["w_branch_a"].reshape(512, NDEV, csh).transpose(1, 0, 2).astype(bf16),
         loc["w_branch_b"].reshape(512, NDEV, csh).transpose(1, 0, 2).astype(bf16),
         loc["w_out"].reshape(NDEV, csh, D).astype(bf16),
         loc["b_merge"].reshape(2, NDEV, csh).transpose(1, 0, 2)],
        [_pack_small(loc, loc["loss"])], "scatter_grads")

    given = dict(norm_gain=norm_gain, q_norm_a=q_norm_a, k_norm_a=k_norm_a, q_norm_b=q_norm_b, k_norm_b=k_norm_b,
                 sink_a=sink_a, rel_bias=rel_bias)
    m_small = dict(norm_gain=m_norm_gain, q_norm_a=m_q_norm_a, k_norm_a=m_k_norm_a, q_norm_b=m_q_norm_b,
                   k_norm_b=m_k_norm_b, sink_a=m_sink_a, rel_bias=m_rel_bias)
    v_small = dict(norm_gain=v_norm_gain, q_norm_a=v_q_norm_a, k_norm_a=v_k_norm_a, q_norm_b=v_q_norm_b,
                   k_norm_b=v_k_norm_b, sink_a=v_sink_a, rel_bias=v_rel_bias)
    res = {
        "small": _adamw(_pack_small(given), r_small, _pack_small(m_small), _pack_small(v_small), "adamw_small"),
        "w_in": _adamw(w_in[0], r_in, m_w_in[0], v_w_in[0], "adamw_w_in"),
        "w_branch_a": _adamw(w_branch_a[0], r_a, m_w_branch_a[0], v_w_branch_a[0], "adamw_w_branch_a"),
        "w_branch_b": _adamw(w_branch_b[0], r_b, m_w_branch_b[0], v_w_branch_b[0], "adamw_w_branch_b"),
        "b_merge": _adamw(b_merge[0], r_bm, m_b_merge[0], v_b_merge[0], "adamw_b_merge"),
        "w_out": _adamw(w_out[0], r_o, m_w_out[0], v_w_out[0], "adamw_w_out"),
    }
    order = ["norm_gain", "w_in", "q_norm_a", "k_norm_a", "q_norm_b", "k_norm_b", "sink_a", "rel_bias", "w_branch_a",
             "w_branch_b", "b_merge", "w_out"]
    outs = []
    for k in range(4):
        small = _unpack_small(res["small"][k], small_shapes)
        for n in order:
            outs.append(small[n] if n in small else res[n][k][None])
    loss = res["small"][0][0, SMALL_USED]
    return (loss, loc["grad_x"][None], *outs)
```

```python
import math

import numpy as np
import jax
import jax.numpy as jnp
from jax import lax
from jax.experimental import pallas as pl
from jax.experimental.pallas import tpu as pltpu

f32 = jnp.float32
bf16 = jnp.bfloat16

S = 4096
D = 1024
NA = 5376
NT = 3072
NW = NA + NT
HD = 64
LANES = 128
EPS = 1e-6
NEG = -1e30
SCALE = HD ** -0.5
TQ = 128
PAD = 128
SP = S + 2 * PAD
NDEV = 8
GROUPS = ((128, 1, 0), (64, 1, 8), (64, 4, 16), (64, 16, 24))
CHUNK = 256
TN = 768

ADAM_LR, ADAM_B1, ADAM_B2, ADAM_EPS, ADAM_WD, ADAM_STEP = 0.001, 0.9, 0.999, 1e-08, 0.01, 10

MIB = 1024 * 1024
NT_DIMS = (((1,), (1,)), ((), ()))
TN_DIMS = (((0,), (0,)), ((), ()))


def _params(sem=None, vmem_mib=48):
    return pltpu.CompilerParams(dimension_semantics=sem, vmem_limit_bytes=vmem_mib * MIB)


def _lo():
    return lax.broadcasted_iota(jnp.int32, (1, LANES), 1) < HD


def _seg_sum(x):
    lo = _lo()
    outs = []
    for b in range(x.shape[1] // LANES):
        xb = x[:, b * LANES:(b + 1) * LANES]
        s0 = jnp.sum(jnp.where(lo, xb, 0.0), axis=1, keepdims=True)
        s1 = jnp.sum(jnp.where(lo, 0.0, xb), axis=1, keepdims=True)
        outs.append(jnp.where(lo, s0, s1))
    return outs[0] if len(outs) == 1 else jnp.concatenate(outs, axis=1)


def _bucket_np(blk, stride):
    w = TQ + 2 * blk
    rel = np.arange(w)[None, :] - blk - np.arange(TQ)[:, None]
    band = np.abs(rel) <= blk
    r = rel * stride
    n = np.abs(r)
    nf = np.maximum(n, 8).astype(np.float32)
    large = 8 + (np.log(nf / np.float32(8)) / np.float32(math.log(128.0)) * np.float32(8)).astype(np.int32)
    large = np.minimum(large, 15)
    b = (r > 0).astype(np.int32) * 16 + np.where(n < 8, n, large)
    return np.where(band, b, -1).astype(np.int32)


def _rms_inproj(x, gain, w):
    ts = 512
    na, nt = NA // TN, NT // TN

    def body(x_ref, g_ref, w_ref, pa_ref, pt_ref, h_ref, r_ref, hs):
        j = pl.program_id(1)

        @pl.when(j == 0)
        def _():
            xv = x_ref[...]
            r = lax.rsqrt(jnp.mean(xv * xv, axis=-1, keepdims=True) + EPS)
            hs[...] = ((xv * r) * g_ref[...]).astype(bf16)
            h_ref[...] = hs[...]
            r_ref[...] = r

        acc = jnp.dot(hs[...], w_ref[...], preferred_element_type=f32)

        @pl.when(j < na)
        def _():
            pa_ref[...] = acc

        @pl.when(j >= na)
        def _():
            pt_ref[...] = acc

    return pl.pallas_call(
        body,
        grid=(S // ts, na + nt),
        in_specs=[
            pl.BlockSpec((ts, D), lambda i, j: (i, 0)),
            pl.BlockSpec((1, D), lambda i, j: (0, 0)),
            pl.BlockSpec((D, TN), lambda i, j: (0, j)),
        ],
        out_specs=[
            pl.BlockSpec((ts, TN), lambda i, j: (i, jnp.minimum(j, na - 1))),
            pl.BlockSpec((ts, TN), lambda i, j: (i, jnp.maximum(j - na, 0))),
            pl.BlockSpec((ts, D), lambda i, j: (i, 0)),
            pl.BlockSpec((ts, 1), lambda i, j: (i, 0)),
        ],
        out_shape=[
            jax.ShapeDtypeStruct((S, NA), f32),
            jax.ShapeDtypeStruct((S, NT), f32),
            jax.ShapeDtypeStruct((S, D), bf16),
            jax.ShapeDtypeStruct((S, 1), f32),
        ],
        scratch_shapes=[pltpu.VMEM((ts, D), bf16)],
        compiler_params=_params(("arbitrary", "arbitrary")),
        name="rms_inproj",
    )(x, gain, w)


def _bias_expand(table, bucket, c0, name):
    tq, w = bucket.shape

    def body(tab_ref, bk_ref, o_ref):
        h = pl.program_id(0)
        bk = bk_ref[...]

        def step(b, acc):
            return jnp.where(bk == b, tab_ref[b, c0 + h], acc)

        o_ref[...] = lax.fori_loop(0, 32, step, jnp.full((tq, w), NEG, f32))

    return pl.pallas_call(
        body,
        grid=(8,),
        in_specs=[pl.BlockSpec(memory_space=pltpu.SMEM), pl.BlockSpec((tq, w), lambda h: (0, 0))],
        out_specs=pl.BlockSpec((None, tq, w), lambda h: (h, 0, 0)),
        out_shape=jax.ShapeDtypeStruct((8, tq, w), f32),
        compiler_params=_params(("arbitrary",)),
        name=name,
    )(table, bucket)


def _col_block(g, j):
    kind = j // 4
    hp = j % 4
    a = jnp.where(kind == 0, hp, 3 + kind)
    b = 6 + 12 * kind + 4 * (g - 1) + hp
    return jnp.where(g == 0, a, b)


def _half_sums(x):
    lo = _lo()
    s0 = jnp.sum(jnp.where(lo, x, 0.0), axis=1, keepdims=True)
    s1 = jnp.sum(jnp.where(lo, 0.0, x), axis=1, keepdims=True)
    return jnp.where(lo, s0, s1)


def _prep(proj_a, gains):
    def body(p_ref, g_ref, o_ref):
        g = pl.program_id(0)
        j = pl.program_id(1)
        kind = j // 4
        lo = _lo()
        half = jnp.where(lo, 0, 1)
        take = (kind == 0) | (half == (j % 4) // 2)
        gain = g_ref[...]
        o_ref[0:PAD, :] = jnp.zeros((PAD, LANES), bf16)
        o_ref[PAD + S:SP, :] = jnp.zeros((PAD, LANES), bf16)

        def norm_store(xv, dst, dup):
            if dup:
                xv = jnp.where(take, xv, pltpu.roll(xv, HD, 1))
            r = lax.rsqrt(_half_sums(xv * xv) * (1.0 / HD) + EPS)
            r = jnp.where(kind == 2, 1.0, r)
            yv = (xv * r) * gain
            yv = jnp.where(kind == 0, yv * SCALE, yv)
            o_ref[PAD + dst:PAD + dst + CHUNK, :] = yv.astype(bf16)

        for gi, (_, d, _) in enumerate(GROUPS):
            @pl.when(g == gi)
            def _():
                seq = S // d
                for c in range(d):
                    for i in range(seq // CHUNK):
                        if d == 1:
                            xv = p_ref[i * CHUNK:(i + 1) * CHUNK, :]
                        else:
                            xv = p_ref[pl.ds(c + i * CHUNK * d, CHUNK, stride=d), :]
                        norm_store(xv, c * seq + i * CHUNK, gi == 0)

    return pl.pallas_call(
        body,
        grid=(4, 12),
        in_specs=[
            pl.BlockSpec((S, LANES), lambda g, j: (0, _col_block(g, j))),
            pl.BlockSpec((None, None, 1, LANES), lambda g, j: (g, j // 4, 0, 0)),
        ],
        out_specs=pl.BlockSpec((None, None, SP, LANES), lambda g, j: (g, j, 0, 0)),
        out_shape=jax.ShapeDtypeStruct((4, 12, SP, LANES), bf16),
        compiler_params=_params(("arbitrary", "arbitrary")),
        name="prep",
    )(proj_a, gains)


def _seq_len(g, ng):
    return S if ng == 1 else jnp.right_shift(S, 2 * g)


def _stack_heads(t, lo):
    z = jnp.zeros_like(t)
    return jnp.concatenate([jnp.where(lo, t, z), jnp.where(lo, z, t)], axis=0)


def _unstack_heads(t2, lo):
    return jnp.where(lo, t2[:TQ], t2[TQ:])


def _row_spec(natural, ng):
    if natural:
        return pl.BlockSpec((S, LANES), lambda g, hp: (0, hp)), (S, 4 * LANES)
    return pl.BlockSpec((None, None, S, LANES), lambda g, hp: (g, hp, 0, 0)), (ng, 4, S, LANES)


def _attn_fwd(gl, bias, sink, g0, ng, blk, natural, name):
    w = TQ + 2 * blk
    use_sink = sink is not None

    def body(*refs):
        if use_sink:
            sink_ref, q_ref, k_ref, v_ref, b_ref, o_ref, l_ref = refs
        else:
            q_ref, k_ref, v_ref, b_ref, o_ref, l_ref = refs
        g = pl.program_id(0)
        hp = pl.program_id(1)
        lo = _lo()
        seq = _seq_len(g, ng)
        mi = lax.broadcasted_iota(jnp.int32, (1, w), 1)
        if use_sink:
            top = lax.broadcasted_iota(jnp.int32, (2 * TQ, 1), 0) < TQ
            sk = jnp.where(top, sink_ref[2 * hp], sink_ref[2 * hp + 1])

        def tile(t, carry):
            f0 = pl.multiple_of(t * TQ, TQ)
            m0 = jnp.bitwise_and(f0, seq - 1)
            q2 = _stack_heads(q_ref[pl.ds(PAD + f0, TQ), :], lo)
            kw = k_ref[pl.ds(PAD - blk + f0, w), :]
            vw = v_ref[pl.ds(PAD - blk + f0, w), :]
            inside = (mi >= blk - m0) & (mi < seq + blk - m0)
            s = lax.dot_general(q2, kw, NT_DIMS, preferred_element_type=f32)
            logit = jnp.where(inside, s + b_ref[...].reshape(2 * TQ, w), NEG)
            m = jnp.max(logit, axis=1, keepdims=True)
            e = jnp.exp(logit - m)
            lse = m + jnp.log(jnp.sum(e, axis=1, keepdims=True))
            if use_sink:
                mx = jnp.maximum(lse, sk)
                lse = mx + jnp.log(jnp.exp(lse - mx) + jnp.exp(sk - mx))
            p = e * jnp.exp(m - lse)
            o2 = jnp.dot(p.astype(bf16), vw, preferred_element_type=f32)
            o_ref[pl.ds(f0, TQ), :] = _unstack_heads(o2, lo)
            l_ref[pl.ds(f0, TQ), :] = jnp.where(lo, lse[:TQ], lse[TQ:])
            return carry

        lax.fori_loop(0, S // TQ, tile, 0, unroll=2)

    in_specs = [
        pl.BlockSpec((None, None, SP, LANES), lambda g, hp: (g0 + g, hp, 0, 0)),
        pl.BlockSpec((None, None, SP, LANES), lambda g, hp: (g0 + g, 4 + hp, 0, 0)),
        pl.BlockSpec((None, None, SP, LANES), lambda g, hp: (g0 + g, 8 + hp, 0, 0)),
        pl.BlockSpec((None, 2, TQ, w), lambda g, hp: (g, hp, 0, 0)),
    ]
    args = [gl, gl, gl, bias]
    if use_sink:
        in_specs = [pl.BlockSpec(memory_space=pltpu.SMEM)] + in_specs
        args = [sink] + args
    out, shape = _row_spec(natural, ng)
    return pl.pallas_call(
        body,
        grid=(ng, 4),
        in_specs=in_specs,
        out_specs=[out, out],
        out_shape=[jax.ShapeDtypeStruct(shape, f32)] * 2,
        compiler_params=_params(("arbitrary", "arbitrary")),
        name=name,
    )(*args)


def _attn_bwd(gl, bias, bucket, do, lse, dd, g0, ng, blk, natural, name):
    w = TQ + 2 * blk

    def body(q_ref, k_ref, v_ref, b_ref, bk_ref, do_ref, l_ref, d_ref, dq_ref, dk_ref, dv_ref, dbk_ref, db_acc):
        g = pl.program_id(0)
        lo = _lo()
        hi = jnp.logical_not(lo)
        seq = _seq_len(g, ng)
        mi = lax.broadcasted_iota(jnp.int32, (1, w), 1)
        dk_ref[...] = jnp.zeros((SP, LANES), f32)
        dv_ref[...] = jnp.zeros((SP, LANES), f32)
        db_acc[...] = jnp.zeros((2 * TQ, w), f32)

        def head_rows(tb):
            return jnp.concatenate([
                jnp.max(jnp.where(lo, tb, -jnp.inf), axis=1, keepdims=True),
                jnp.max(jnp.where(hi, tb, -jnp.inf), axis=1, keepdims=True)], axis=0)

        def tile(t, carry):
            f0 = pl.multiple_of(t * TQ, TQ)
            m0 = jnp.bitwise_and(f0, seq - 1)
            q2 = _stack_heads(q_ref[pl.ds(PAD + f0, TQ), :], lo)
            do2 = _stack_heads(do_ref[pl.ds(f0, TQ), :], lo)
            kw = k_ref[pl.ds(PAD - blk + f0, w), :]
            vw = v_ref[pl.ds(PAD - blk + f0, w), :]
            lh = head_rows(l_ref[pl.ds(f0, TQ), :])
            dh = head_rows(d_ref[pl.ds(f0, TQ), :])
            inside = (mi >= blk - m0) & (mi < seq + blk - m0)
            s = lax.dot_general(q2, kw, NT_DIMS, preferred_element_type=f32)
            dp = lax.dot_general(do2, vw, NT_DIMS, preferred_element_type=f32)
            logit = jnp.where(inside, s + b_ref[...].reshape(2 * TQ, w), NEG)
            p = jnp.exp(logit - lh)
            ds = p * (dp - dh)
            db_acc[...] += ds
            dsb = ds.astype(bf16)
            dq2 = jnp.dot(dsb, kw, preferred_element_type=f32)
            dkw = lax.dot_general(dsb, q2, TN_DIMS, preferred_element_type=f32)
            dvw = lax.dot_general(p.astype(bf16), do2, TN_DIMS, preferred_element_type=f32)
            dq_ref[pl.ds(f0, TQ), :] = _unstack_heads(dq2, lo)
            dk_ref[pl.ds(PAD - blk + f0, w), :] += dkw
            dv_ref[pl.ds(PAD - blk + f0, w), :] += dvw
            return carry

        lax.fori_loop(0, S // TQ, tile, 0, unroll=2)

        bk = bk_ref[...]
        lane = lax.broadcasted_iota(jnp.int32, (8, LANES), 1)
        for h in range(2):
            db = db_acc[h * TQ:(h + 1) * TQ, :]
            acc = jnp.zeros((8, LANES), f32)
            for b in range(32):
                part = jnp.where(bk == b, db, 0.0).reshape(TQ // 8, 8, w).sum(axis=0)
                tot = jnp.sum(jnp.sum(part, axis=1, keepdims=True), axis=0, keepdims=True)
                acc = jnp.where(lane == b, tot, acc)
            dbk_ref[h] = acc

    def gcol(off):
        return pl.BlockSpec((None, None, SP, LANES), lambda g, hp: (g0 + g, off + hp, 0, 0))

    row, row_shape = _row_spec(natural, ng)
    rowp = pl.BlockSpec((None, None, SP, LANES), lambda g, hp: (g, hp, 0, 0))
    return pl.pallas_call(
        body,
        grid=(ng, 4),
        in_specs=[gcol(0), gcol(4), gcol(8), pl.BlockSpec((None, 2, TQ, w), lambda g, hp: (g, hp, 0, 0)),
                  pl.BlockSpec((None, TQ, w), lambda g, hp: (g, 0, 0)), row, row, row],
        out_specs=[row, rowp, rowp, pl.BlockSpec((None, 2, 8, LANES), lambda g, hp: (g, hp, 0, 0))],
        out_shape=[
            jax.ShapeDtypeStruct(row_shape, f32),
            jax.ShapeDtypeStruct((ng, 4, SP, LANES), f32),
            jax.ShapeDtypeStruct((ng, 4, SP, LANES), f32),
            jax.ShapeDtypeStruct((ng, 8, 8, LANES), f32),
        ],
        scratch_shapes=[pltpu.VMEM((2 * TQ, w), f32)],
        compiler_params=_params(("arbitrary", "arbitrary"), vmem_mib=56),
        name=name,
    )(gl, gl, gl, bias, bucket, do, lse, dd)


def _b_to_natural(o_gl, l_gl):
    def body(o_ref, l_ref, on_ref, ln_ref):
        g = pl.program_id(0)
        for gi in range(3):
            d = GROUPS[gi + 1][1]

            @pl.when(g == gi)
            def _():
                seq = S // d
                for c in range(d):
                    for i in range(seq // CHUNK):
                        src = slice(c * seq + i * CHUNK, c * seq + (i + 1) * CHUNK)
                        if d == 1:
                            on_ref[src, :] = o_ref[src, :]
                            ln_ref[src, :] = l_ref[src, :]
                        else:
                            dst = pl.ds(c + i * CHUNK * d, CHUNK, stride=d)
                            on_ref[dst, :] = o_ref[src, :]
                            ln_ref[dst, :] = l_ref[src, :]

    col = pl.BlockSpec((None, None, S, LANES), lambda g, hp: (g, hp, 0, 0))
    nat = pl.BlockSpec((S, LANES), lambda g, hp: (0, 4 * g + hp))
    return pl.pallas_call(
        body,
        grid=(3, 4),
        in_specs=[col, col],
        out_specs=[nat, nat],
        out_shape=[jax.ShapeDtypeStruct((S, 3 * 512), f32)] * 2,
        compiler_params=_params(("arbitrary", "arbitrary")),
        name="b_to_natural",
    )(o_gl, l_gl)


def _b_from_natural(do_n, dd_n):
    def body(do_ref, dd_ref, dog_ref, ddg_ref):
        g = pl.program_id(0)
        for gi in range(3):
            d = GROUPS[gi + 1][1]

            @pl.when(g == gi)
            def _():
                seq = S // d
                for c in range(d):
                    for i in range(seq // CHUNK):
                        dst = slice(c * seq + i * CHUNK, c * seq + (i + 1) * CHUNK)
                        if d == 1:
                            a, b = do_ref[dst, :], dd_ref[dst, :]
                        else:
                            src = pl.ds(c + i * CHUNK * d, CHUNK, stride=d)
                            a, b = do_ref[src, :], dd_ref[src, :]
                        dog_ref[dst, :] = a.astype(bf16)
                        ddg_ref[dst, :] = b

    col = pl.BlockSpec((None, None, S, LANES), lambda g, hp: (g, hp, 0, 0))
    nat = pl.BlockSpec((S, LANES), lambda g, hp: (0, 4 * g + hp))
    return pl.pallas_call(
        body,
        grid=(3, 4),
        in_specs=[nat, nat],
        out_specs=[col, col],
        out_shape=[jax.ShapeDtypeStruct((3, 4, S, LANES), bf16), jax.ShapeDtypeStruct((3, 4, S, LANES), f32)],
        compiler_params=_params(("arbitrary", "arbitrary")),
        name="b_from_natural",
    )(do_n, dd_n)


def _sigmoid(z):
    return 1.0 / (1.0 + jnp.exp(-z))


def _tail(x, tgt, o_a, l_a, o_b, l_b, proj_t, bm, w_a, w_b, w_o, sink_b):
    ts = 128

    def body(x_ref, t_ref, oa_ref, la_ref, ob_ref, lb_ref, ga_ref, gb_ref, m0_ref, m1_ref, bm_ref,
             wa_ref, wb_ref, wo_ref, sk_ref,
             dy_ref, dyb_ref, dt_ref, doa_ref, dda_ref, dob_ref, ddb_ref, ya_ref, yb_ref, mg_ref, dbra_ref, dbrb_ref,
             loss_ref, dbm_ref, dsk_ref):
        i = pl.program_id(0)

        @pl.when(i == 0)
        def _():
            loss_ref[...] = jnp.zeros_like(loss_ref)
            dbm_ref[...] = jnp.zeros_like(dbm_ref)
            dsk_ref[...] = jnp.zeros_like(dsk_ref)

        ga = ga_ref[...]
        sa = _sigmoid(ga)
        silu_a = ga * sa
        oa = oa_ref[...]
        ya = oa * silu_a
        gb = gb_ref[...]
        sb = _sigmoid(gb)
        silu_b = gb * sb
        ob = [ob_ref[:, k * 512:(k + 1) * 512] for k in range(3)]
        lb = [lb_ref[:, k * 512:(k + 1) * 512] for k in range(3)]
        mx = jnp.maximum(jnp.maximum(lb[0], lb[1]), lb[2])
        ex = [jnp.exp(v - mx) for v in lb]
        den = ex[0] + ex[1] + ex[2]
        alpha = [e / den for e in ex]
        ybc = alpha[0] * ob[0] + alpha[1] * ob[1] + alpha[2] * ob[2]
        yb = ybc * silu_b
        yab = ya.astype(bf16)
        ybb = yb.astype(bf16)
        br_a = jnp.dot(yab, wa_ref[...], preferred_element_type=f32)
        br_b = jnp.dot(ybb, wb_ref[...], preferred_element_type=f32)
        g0 = _sigmoid(m0_ref[...] + bm_ref[0:1, :])
        g1 = _sigmoid(m1_ref[...] + bm_ref[1:2, :])
        merged = g0 * br_a + g1 * br_b
        mgb = merged.astype(bf16)
        y = x_ref[...] + jnp.dot(mgb, wo_ref[...], preferred_element_type=f32)
        err = y - t_ref[...]
        part = jnp.sum(jnp.sum(err * err, axis=1, keepdims=True), axis=0, keepdims=True)
        loss_ref[...] += part * (0.5 / D)
        dy = err * (1.0 / D)
        dyb = dy.astype(bf16)
        dmerged = lax.dot_general(dyb, wo_ref[...], NT_DIMS, preferred_element_type=f32)
        dbr_a = (dmerged * g0).astype(bf16)
        dbr_b = (dmerged * g1).astype(bf16)
        dm0 = dmerged * br_a * (g0 * (1.0 - g0))
        dm1 = dmerged * br_b * (g1 * (1.0 - g1))
        dbm_ref[0:1, :] += jnp.sum(dm0, axis=0, keepdims=True)
        dbm_ref[1:2, :] += jnp.sum(dm1, axis=0, keepdims=True)
        dya = lax.dot_general(dbr_a, wa_ref[...], NT_DIMS, preferred_element_type=f32)
        dyb2 = lax.dot_general(dbr_b, wb_ref[...], NT_DIMS, preferred_element_type=f32)
        do_a = dya * silu_a
        dga = dya * oa * (sa * (1.0 + ga * (1.0 - sa)))
        delta_a = _seg_sum(do_a * oa)
        dsk_ref[...] -= jnp.sum(delta_a * jnp.exp(sk_ref[...] - la_ref[...]), axis=0, keepdims=True)
        dybc = dyb2 * silu_b
        dgb = dyb2 * ybc * (sb * (1.0 + gb * (1.0 - sb)))
        dbar = _seg_sum(dybc * ybc)
        dy_ref[...] = dy
        dyb_ref[...] = dyb
        dt_ref[:, 0:512] = dga.astype(bf16)
        dt_ref[:, 512:1024] = dgb.astype(bf16)
        dt_ref[:, 1024:2048] = dm0.astype(bf16)
        dt_ref[:, 2048:3072] = dm1.astype(bf16)
        doa_ref[...] = do_a.astype(bf16)
        dda_ref[...] = delta_a
        for k in range(3):
            dob_ref[:, k * 512:(k + 1) * 512] = alpha[k] * dybc
            ddb_ref[:, k * 512:(k + 1) * 512] = alpha[k] * dbar
        ya_ref[...] = yab
        yb_ref[...] = ybb
        mg_ref[...] = mgb
        dbra_ref[...] = dbr_a
        dbrb_ref[...] = dbr_b

    def rows(n, blk=0):
        return pl.BlockSpec((ts, n), lambda i: (i, blk))

    def whole(r, c):
        return pl.BlockSpec((r, c), lambda i: (0, 0))

    outs = [
        ((S, D), f32, rows(D)), ((S, D), bf16, rows(D)), ((S, NT), bf16, rows(NT)),
        ((S, 512), bf16, rows(512)), ((S, 512), f32, rows(512)),
        ((S, 1536), f32, rows(1536)), ((S, 1536), f32, rows(1536)),
        ((S, 512), bf16, rows(512)), ((S, 512), bf16, rows(512)), ((S, D), bf16, rows(D)),
        ((S, D), bf16, rows(D)), ((S, D), bf16, rows(D)),
        ((1, 1), f32, whole(1, 1)), ((2, D), f32, whole(2, D)), ((1, 512), f32, whole(1, 512)),
    ]
    return pl.pallas_call(
        body,
        grid=(S // ts,),
        in_specs=[
            rows(D), rows(D), rows(512), rows(512), rows(1536), rows(1536),
            rows(512, 0), rows(512, 1), rows(D, 1), rows(D, 2), whole(2, D),
            whole(512, D), whole(512, D), whole(D, D), whole(1, 512),
        ],
        out_specs=[o[2] for o in outs],
        out_shape=[jax.ShapeDtypeStruct(o[0], o[1]) for o in outs],
        compiler_params=_params(("arbitrary",)),
        name="tail",
    )(x, tgt, o_a, l_a, o_b, l_b, proj_t, proj_t, proj_t, proj_t, bm, w_a, w_b, w_o, sink_b)


def _norm_bwd(xv, dyv, gain, kind):
    r = lax.rsqrt(_half_sums(xv * xv) * (1.0 / HD) + EPS)
    yv = xv * r
    up = jnp.where(kind == 0, dyv * SCALE, dyv)
    u = up * gain
    dxv = r * (u - yv * (_half_sums(u * yv) * (1.0 / HD)))
    dxv = jnp.where(kind == 2, dyv, dxv)
    dg = jnp.where(kind == 2, 0.0, jnp.sum(up * yv, axis=0, keepdims=True))
    return dxv, dg


def _post_b(dq, dk, dv, proj_a, gains, dproj):
    def body(dq_ref, dk_ref, dv_ref, p_ref, g_ref, alias_ref, o_ref, dg_ref, nat):
        del alias_ref
        g = pl.program_id(0)
        j = pl.program_id(1)
        kind = j // 4
        gain = g_ref[...]

        @pl.when(j % 4 == 0)
        def _():
            dg_ref[...] = jnp.zeros_like(dg_ref)

        for gi in range(3):
            d = GROUPS[gi + 1][1]

            @pl.when(g == gi)
            def _():
                seq = S // d
                for c in range(d):
                    for i in range(seq // CHUNK):
                        src = c * seq + i * CHUNK
                        a = dq_ref[src:src + CHUNK, :]
                        b = dk_ref[PAD + src:PAD + src + CHUNK, :]
                        e = dv_ref[PAD + src:PAD + src + CHUNK, :]
                        dyv = jnp.where(kind == 0, a, jnp.where(kind == 1, b, e))
                        if d == 1:
                            idx = slice(src, src + CHUNK)
                        else:
                            idx = pl.ds(c + i * CHUNK * d, CHUNK, stride=d)
                        dxv, dg = _norm_bwd(p_ref[idx, :], dyv, gain, kind)
                        nat[idx, :] = dxv
                        dg_ref[...] += dg

        for i in range(S // CHUNK):
            o_ref[i * CHUNK:(i + 1) * CHUNK, :] = nat[i * CHUNK:(i + 1) * CHUNK, :].astype(bf16)

    def gcol(n, off):
        return pl.BlockSpec((None, None, n, LANES), lambda g, j: (g, jnp.clip(j - off, 0, 3), 0, 0))

    return pl.pallas_call(
        body,
        grid=(3, 12),
        in_specs=[
            gcol(S, 0), gcol(SP, 4), gcol(SP, 8),
            pl.BlockSpec((S, LANES), lambda g, j: (0, _col_block(g + 1, j))),
            pl.BlockSpec((None, None, 1, LANES), lambda g, j: (g + 1, j // 4, 0, 0)),
            pl.BlockSpec(memory_space=pl.ANY),
        ],
        out_specs=[
            pl.BlockSpec((S, LANES), lambda g, j: (0, _col_block(g + 1, j))),
            pl.BlockSpec((None, None, 1, LANES), lambda g, j: (g, j // 4, 0, 0)),
        ],
        out_shape=[jax.ShapeDtypeStruct((S, NA), bf16), jax.ShapeDtypeStruct((3, 3, 1, LANES), f32)],
        scratch_shapes=[pltpu.VMEM((S, LANES), f32)],
        input_output_aliases={5: 0},
        compiler_params=_params(("arbitrary", "arbitrary")),
        name="post_b",
    )(dq, dk, dv, proj_a, gains, dproj)


def _post_a(dq, dk, dv, proj_a, gains):
    def body(dq_ref, dk_ref, dv_ref, p_ref, g_ref, o_ref, dg_ref):
        j = pl.program_id(0)
        kind = jnp.maximum(j - 3, 0)
        gain = g_ref[...]
        lo = _lo()

        @pl.when((j == 0) | (j >= 4))
        def _():
            dg_ref[...] = jnp.zeros_like(dg_ref)

        def fold(ref, r0):
            t0 = ref[0, PAD + r0:PAD + r0 + CHUNK, :] + ref[1, PAD + r0:PAD + r0 + CHUNK, :]
            t1 = ref[2, PAD + r0:PAD + r0 + CHUNK, :] + ref[3, PAD + r0:PAD + r0 + CHUNK, :]
            return jnp.where(lo, t0 + pltpu.roll(t0, HD, 1), t1 + pltpu.roll(t1, HD, 1))

        for i in range(S // CHUNK):
            r0 = i * CHUNK
            a = dq_ref[r0:r0 + CHUNK, :]
            dyv = jnp.where(kind == 0, a, jnp.where(kind == 1, fold(dk_ref, r0), fold(dv_ref, r0)))
            dxv, dg = _norm_bwd(p_ref[r0:r0 + CHUNK, :], dyv, gain, kind)
            o_ref[r0:r0 + CHUNK, :] = dxv.astype(bf16)
            dg_ref[...] += dg

    exp = pl.BlockSpec((None, 4, SP, LANES), lambda j: (0, 0, 0, 0))
    return pl.pallas_call(
        body,
        grid=(6,),
        in_specs=[
            pl.BlockSpec((S, LANES), lambda j: (0, jnp.minimum(j, 3))),
            exp, exp,
            pl.BlockSpec((S, LANES), lambda j: (0, j)),
            pl.BlockSpec((None, None, 1, LANES), lambda j: (0, jnp.maximum(j - 3, 0), 0, 0)),
        ],
        out_specs=[
            pl.BlockSpec((S, LANES), lambda j: (0, j)),
            pl.BlockSpec((None, 1, LANES), lambda j: (jnp.maximum(j - 3, 0), 0, 0)),
        ],
        out_shape=[jax.ShapeDtypeStruct((S, NA), bf16), jax.ShapeDtypeStruct((3, 1, LANES), f32)],
        compiler_params=_params(("arbitrary",)),
        name="post_a",
    )(dq, dk, dv, proj_a, gains)


def _dh_norm_bwd(dproj_a, dproj_t, w, x, rstd, gain, dy):
    ts = 512
    na, nt = NA // TN, NT // TN

    def body(da_ref, dt_ref, w_ref, x_ref, r_ref, g_ref, dy_ref, gx_ref, dgn_ref, acc):
        i = pl.program_id(0)
        k = pl.program_id(1)

        @pl.when((i == 0) & (k == 0))
        def _():
            dgn_ref[...] = jnp.zeros_like(dgn_ref)

        @pl.when(k == 0)
        def _():
            acc[...] = jnp.zeros_like(acc)

        @pl.when(k < na)
        def _():
            acc[...] += lax.dot_general(da_ref[...], w_ref[...], NT_DIMS, preferred_element_type=f32)

        @pl.when(k >= na)
        def _():
            acc[...] += lax.dot_general(dt_ref[...], w_ref[...], NT_DIMS, preferred_element_type=f32)

        @pl.when(k == na + nt - 1)
        def _():
            dh = acc[...]
            xh = x_ref[...] * r_ref[...]
            u = dh * g_ref[...]
            dx = r_ref[...] * (u - xh * jnp.mean(u * xh, axis=-1, keepdims=True))
            gx_ref[...] = dy_ref[...] + dx
            dgn_ref[...] += jnp.sum(dh * xh, axis=0, keepdims=True)

    return pl.pallas_call(
        body,
        grid=(S // ts, na + nt),
        in_specs=[
            pl.BlockSpec((ts, TN), lambda i, k: (i, jnp.minimum(k, na - 1))),
            pl.BlockSpec((ts, TN), lambda i, k: (i, jnp.maximum(k - na, 0))),
            pl.BlockSpec((D, TN), lambda i, k: (0, k)),
            pl.BlockSpec((ts, D), lambda i, k: (i, 0)),
            pl.BlockSpec((ts, 1), lambda i, k: (i, 0)),
            pl.BlockSpec((1, D), lambda i, k: (0, 0)),
            pl.BlockSpec((ts, D), lambda i, k: (i, 0)),
        ],
        out_specs=[pl.BlockSpec((ts, D), lambda i, k: (i, 0)), pl.BlockSpec((1, D), lambda i, k: (0, 0))],
        out_shape=[jax.ShapeDtypeStruct((S, D), f32), jax.ShapeDtypeStruct((1, D), f32)],
        scratch_shapes=[pltpu.VMEM((ts, D), f32)],
        compiler_params=_params(("arbitrary", "arbitrary")),
        name="dh_norm_bwd",
    )(dproj_a, dproj_t, w, x, rstd, gain, dy)


def _matmul_tn(a, b, name):
    m, n = a.shape[1], b.shape[1]
    tn = TN if n % TN == 0 else 512
    tk = 512

    def body(a_ref, b_ref, o_ref):
        @pl.when(pl.program_id(1) == 0)
        def _():
            o_ref[...] = jnp.zeros_like(o_ref)

        o_ref[...] += lax.dot_general(a_ref[...], b_ref[...], TN_DIMS, preferred_element_type=f32)

    return pl.pallas_call(
        body,
        grid=(n // tn, S // tk),
        in_specs=[pl.BlockSpec((tk, m), lambda j, k: (k, 0)), pl.BlockSpec((tk, tn), lambda j, k: (k, j))],
        out_specs=pl.BlockSpec((m, tn), lambda j, k: (0, j)),
        out_shape=jax.ShapeDtypeStruct((m, n), f32),
        compiler_params=_params(("arbitrary", "arbitrary")),
        name=name,
    )(a, b)


def _exchange(scatter, gather, name):
    arrs = list(scatter) + list(gather)
    n = len(arrs)
    ns = len(scatter)

    def body(*refs):
        ins, outs = refs[:n], refs[n:2 * n]
        send_sems, recv_sems, local_sems = refs[2 * n:]
        x, y, c = lax.axis_index("x"), lax.axis_index("y"), lax.axis_index("c")
        me = 4 * x + 2 * y + c
        local, remote = [], []
        for a in range(n):
            lc = pltpu.make_async_copy(ins[a].at[me] if a < ns else ins[a], outs[a].at[me], local_sems.at[a])
            lc.start()
            local.append(lc)
            for r in range(1, NDEV):
                px = 1 - x if r & 4 else x
                py = 1 - y if r & 2 else y
                pc = 1 - c if r & 1 else c
                cp = pltpu.make_async_remote_copy(
                    src_ref=ins[a].at[4 * px + 2 * py + pc] if a < ns else ins[a],
                    dst_ref=outs[a].at[me],
                    send_sem=send_sems.at[a, r - 1],
                    recv_sem=recv_sems.at[a, r - 1],
                    device_id=(px, py, pc),
                    device_id_type=pl.DeviceIdType.MESH,
                )
                cp.start()
                remote.append(cp)
        for cp in remote:
            cp.wait_recv()
        for cp in remote:
            cp.wait_send()
        for lc in local:
            lc.wait()

    out_shape = [jax.ShapeDtypeStruct(a.shape if i < ns else (NDEV,) + a.shape, a.dtype) for i, a in enumerate(arrs)]
    return pl.pallas_call(
        body,
        in_specs=[pl.BlockSpec(memory_space=pl.ANY)] * n,
        out_specs=[pl.BlockSpec(memory_space=pl.ANY)] * n,
        out_shape=out_shape,
        scratch_shapes=[
            pltpu.SemaphoreType.DMA((n, NDEV - 1)),
            pltpu.SemaphoreType.DMA((n, NDEV - 1)),
            pltpu.SemaphoreType.DMA((n,)),
        ],
        compiler_params=pltpu.CompilerParams(has_side_effects=True),
        name=name,
    )(*arrs)


def _gather_two_level(arrs, name):
    n = len(arrs)

    def body(*refs):
        ins, outs = refs[:n], refs[n:2 * n]
        send_sems, recv_sems, local_sems = refs[2 * n:]
        x, y, c = lax.axis_index("x"), lax.axis_index("y"), lax.axis_index("c")
        me, sibling = (x, y, c), (x, y, 1 - c)
        chips = [(1 - x, y), (x, 1 - y), (1 - x, 1 - y)]

        def copy(a, k, block, to, src=None):
            slot = outs[a].at[4 * block[0] + 2 * block[1] + block[2]]
            return pltpu.make_async_remote_copy(
                src_ref=slot if src is None else src, dst_ref=slot, send_sem=send_sems.at[a, k],
                recv_sem=recv_sems.at[a, k], device_id=to, device_id_type=pl.DeviceIdType.MESH)

        mine, first, passed = [], [], []
        for a in range(n):
            lc = pltpu.make_async_copy(ins[a], outs[a].at[4 * x + 2 * y + c], local_sems.at[a])
            lc.start()
            mine.append(lc)
            first.append(copy(a, 0, me, sibling, src=ins[a]))
            first += [copy(a, 1 + j, me, (*chip, c), src=ins[a]) for j, chip in enumerate(chips)]
        for cp in first:
            cp.start()
        for j, chip in enumerate(chips):
            for a in range(n):
                copy(a, 1 + j, (*chip, c), me).wait_recv()
                fwd = copy(a, 4 + j, (*chip, c), sibling)
                fwd.start()
                passed.append(fwd)
        for a in range(n):
            copy(a, 0, sibling, me).wait_recv()
        for j, chip in enumerate(chips):
            for a in range(n):
                copy(a, 4 + j, (*chip, 1 - c), me).wait_recv()
        for cp in first + passed:
            cp.wait_send()
        for lc in mine:
            lc.wait()

    return pl.pallas_call(
        body,
        in_specs=[pl.BlockSpec(memory_space=pl.ANY)] * n,
        out_specs=[pl.BlockSpec(memory_space=pl.ANY)] * n,
        out_shape=[jax.ShapeDtypeStruct((NDEV,) + a.shape, a.dtype) for a in arrs],
        scratch_shapes=[
            pltpu.SemaphoreType.DMA((n, NDEV - 1)),
            pltpu.SemaphoreType.DMA((n, NDEV - 1)),
            pltpu.SemaphoreType.DMA((n,)),
        ],
        compiler_params=pltpu.CompilerParams(has_side_effects=True),
        name=name,
    )(*arrs)


_HBM = pl.BlockSpec(memory_space=pltpu.HBM)
_SEM = pl.BlockSpec(memory_space=pltpu.SEMAPHORE)
_EFFECT = pltpu.SideEffectType.DATAFLOW_SIDE_EFFECTING


def _scatter_start(arrs, name):
    n = len(arrs)

    def body(*refs):
        src, land = refs[:n], refs[n:2 * n]
        send_sems, recv_sems = refs[2 * n:3 * n], refs[3 * n:4 * n]
        token = refs[6 * n]
        x, y, c = lax.axis_index("x"), lax.axis_index("y"), lax.axis_index("c")
        me = 4 * x + 2 * y + c
        for a in range(n):
            for r in range(1, NDEV):
                px = 1 - x if r & 4 else x
                py = 1 - y if r & 2 else y
                pc = 1 - c if r & 1 else c
                pltpu.make_async_remote_copy(
                    src_ref=src[a].at[4 * px + 2 * py + pc], dst_ref=land[a].at[me], send_sem=send_sems[a],
                    recv_sem=recv_sems[a], device_id=(px, py, pc), device_id_type=pl.DeviceIdType.MESH).start()
        token[...] = jnp.zeros_like(token)

    hbm = [pltpu.HBM(a.shape, a.dtype) for a in arrs]
    ops = [pltpu.with_memory_space_constraint(a, pltpu.HBM) for a in arrs]
    outs = pl.pallas_call(
        body,
        out_shape=tuple([pltpu.SemaphoreType.DMA(())] * (2 * n) + hbm + hbm + [jax.ShapeDtypeStruct((8, LANES), f32)]),
        in_specs=[_HBM] * (2 * n),
        out_specs=tuple([_SEM] * (2 * n) + [_HBM] * (2 * n) + [pl.BlockSpec(memory_space=pltpu.VMEM)]),
        input_output_aliases={i: 2 * n + i for i in range(2 * n)},
        compiler_params=pltpu.CompilerParams(has_side_effects=_EFFECT),
        name=name,
    )(*ops, *ops)
    return outs[:n], outs[n:2 * n], outs[2 * n:3 * n], outs[3 * n:4 * n], outs[4 * n]


def _scatter_wait(send_sems, recv_sems, srcs, lands, after, name):
    n = len(srcs)

    def body(*refs):
        land = refs[n:2 * n]
        ssem, rsem = refs[2 * n:3 * n], refs[3 * n:4 * n]
        x, y, c = lax.axis_index("x"), lax.axis_index("y"), lax.axis_index("c")
        for a in range(n):
            seven = land[a].at[pl.ds(0, NDEV - 1)]
            done = pltpu.make_async_remote_copy(
                src_ref=seven, dst_ref=seven, send_sem=ssem[a], recv_sem=rsem[a], device_id=(x, y, c),
                device_id_type=pl.DeviceIdType.MESH)
            done.wait_send()
            done.wait_recv()

    hbm = [pltpu.HBM(a.shape, a.dtype) for a in srcs]
    outs = pl.pallas_call(
        body,
        out_shape=tuple(hbm + hbm),
        in_specs=[_HBM] * (2 * n) + [_SEM] * (2 * n) + [pl.BlockSpec(memory_space=pl.ANY)],
        out_specs=tuple([_HBM] * (2 * n)),
        input_output_aliases={i: i for i in range(2 * n)},
        compiler_params=pltpu.CompilerParams(has_side_effects=_EFFECT),
        name=name,
    )(*srcs, *lands, *send_sems, *recv_sems, after)
    return outs[n:]


def _adamw(w, slots, m, v, name):
    r, c = w.shape
    tr = 128 if r % 128 == 0 else r

    def body(w_ref, s_ref, m_ref, v_ref, g_ref, d_ref, nm_ref, nv_ref):
        g = s_ref[0].astype(f32)
        for k in range(1, NDEV):
            g = g + s_ref[k].astype(f32)
        mm = ADAM_B1 * m_ref[...] + (1.0 - ADAM_B1) * g
        vv = ADAM_B2 * v_ref[...] + (1.0 - ADAM_B2) * (g * g)
        m_hat = mm / (1.0 - ADAM_B1 ** ADAM_STEP)
        v_hat = vv / (1.0 - ADAM_B2 ** ADAM_STEP)
        g_ref[...] = g
        d_ref[...] = -ADAM_LR * (m_hat / (jnp.sqrt(v_hat) + ADAM_EPS) + ADAM_WD * w_ref[...])
        nm_ref[...] = mm
        nv_ref[...] = vv

    blk = pl.BlockSpec((tr, c), lambda i: (i, 0))
    return pl.pallas_call(
        body,
        grid=(r // tr,),
        in_specs=[blk, pl.BlockSpec((NDEV, tr, c), lambda i: (0, i, 0)), blk, blk],
        out_specs=[blk] * 4,
        out_shape=[jax.ShapeDtypeStruct((r, c), f32)] * 4,
        compiler_params=_params(("arbitrary",)),
        name=name,
    )(w, slots, m, v)


def _local_step(x, tgt, norm_gain, w_in, qn_a, kn_a, qn_b, kn_b, sink_a, rel_bias, w_a, w_b, b_merge, w_o,
                on_weight_grads=None):
    two = lambda t: jnp.concatenate([t, t], axis=-1).reshape(1, LANES)
    ones = jnp.ones((1, LANES), f32)
    gains = jnp.stack([
        jnp.stack([two(qn_a), two(kn_a), ones]),
        jnp.stack([two(qn_b), two(kn_b), ones]),
        jnp.stack([two(qn_b), two(kn_b), ones]),
        jnp.stack([two(qn_b), two(kn_b), ones]),
    ])
    buckets = [jnp.asarray(_bucket_np(blk, d)) for blk, d, _ in GROUPS]
    bias_a = _bias_expand(rel_bias, buckets[0], 0, "bias_expand_a")[None]
    bias_b = jnp.stack([_bias_expand(rel_bias, buckets[k], GROUPS[k][2], "bias_expand_b%d" % k) for k in (1, 2, 3)])

    proj_a, proj_t, hb, rstd = _rms_inproj(x, norm_gain, w_in)
    gl = _prep(proj_a, gains)
    o_a, l_a = _attn_fwd(gl, bias_a, sink_a.reshape(8), 0, 1, 128, True, "attn_fwd_a")
    o_bg, l_bg = _attn_fwd(gl, bias_b, None, 1, 3, 64, False, "attn_fwd_b")
    o_b, l_b = _b_to_natural(o_bg, l_bg)
    sink_b = jnp.repeat(sink_a.reshape(8), HD).reshape(1, 512)

    (dy, dyb, dproj_t, do_a, dd_a, do_b, dd_b, ya, yb, mg, dbr_a, dbr_b, loss, dbm, dsk) = _tail(
        x, tgt, o_a, l_a, o_b, l_b, proj_t, b_merge, w_a, w_b, w_o, sink_b)

    dq_a, dk_a, dv_a, dbk_a = _attn_bwd(gl, bias_a, buckets[0][None], do_a, l_a, dd_a, 0, 1, 128, True, "attn_bwd_a")
    do_bg, dd_bg = _b_from_natural(do_b, dd_b)
    dq_b, dk_b, dv_b, dbk_b = _attn_bwd(gl, bias_b, jnp.stack(buckets[1:]), do_bg, l_bg, dd_bg, 1, 3, 64, False,
                                        "attn_bwd_b")

    dproj_a, dg_a = _post_a(dq_a, dk_a, dv_a, proj_a, gains)
    dproj_a, dg_b = _post_b(dq_b, dk_b, dv_b, proj_a, gains, dproj_a)

    dw_in = jnp.concatenate([_matmul_tn(hb, dproj_a, "dw_in_a"), _matmul_tn(hb, dproj_t, "dw_in_t")], axis=1)
    dw_o = _matmul_tn(mg, dyb, "dw_out")
    dw_a = _matmul_tn(ya, dbr_a, "dw_branch_a")
    dw_b = _matmul_tn(yb, dbr_b, "dw_branch_b")
    token = jnp.zeros((), f32) if on_weight_grads is None else on_weight_grads(
        dict(w_in=dw_in, w_branch_a=dw_a, w_branch_b=dw_b, b_merge=dbm, w_out=dw_o))
    grad_x, d_norm_gain = _dh_norm_bwd(dproj_a, dproj_t, w_in, x, rstd, norm_gain + token, dy)

    fold = lambda t: t[..., :HD] + t[..., HD:]
    d_qn_a = fold(dg_a[0, 0])
    d_kn_a = fold(dg_a[1, 0])
    d_qn_b = fold(dg_b[:, 0, 0].sum(axis=0))
    d_kn_b = fold(dg_b[:, 1, 0].sum(axis=0))
    d_sink = dsk.reshape(8, HD)[:, 0]
    red = jnp.concatenate([dbk_a, dbk_b])
    d_rel = red[:, :, 0, :32].reshape(32, 32).T
    return dict(loss=loss, grad_x=grad_x, norm_gain=d_norm_gain, w_in=dw_in, q_norm_a=d_qn_a, k_norm_a=d_kn_a,
                q_norm_b=d_qn_b, k_norm_b=d_kn_b, sink_a=d_sink, rel_bias=d_rel, w_branch_a=dw_a, w_branch_b=dw_b,
                b_merge=dbm, w_out=dw_o)


SMALL = (("norm_gain", D), ("q_norm_a", HD), ("k_norm_a", HD), ("q_norm_b", HD), ("k_norm_b", HD), ("sink_a", 8),
         ("rel_bias", 1024))
SMALL_PAD = 2432


SMALL_USED = sum(sz for _, sz in SMALL)


def _pack_small(parts, loss=None):
    tail = jnp.zeros((SMALL_PAD - SMALL_USED,), f32)
    if loss is not None:
        tail = tail.at[0].set(loss.reshape(()))
    return jnp.concatenate([parts[n].reshape(-1) for n, _ in SMALL] + [tail]).reshape(1, SMALL_PAD)


def _unpack_small(flat, shapes):
    out, off = {}, 0
    for n, sz in SMALL:
        out[n] = flat[0, off:off + sz].reshape(shapes[n])
        off += sz
    return out


def kernel(x, norm_gain, w_in, q_norm_a, k_norm_a, q_norm_b, k_norm_b, sink_a, rel_bias, w_branch_a, w_branch_b, b_merge, w_out, loss_target, m_norm_gain, m_w_in, m_q_norm_a, m_k_norm_a, m_q_norm_b, m_k_norm_b, m_sink_a, m_rel_bias, m_w_branch_a, m_w_branch_b, m_b_merge, m_w_out, v_norm_gain, v_w_in, v_q_norm_a, v_k_norm_a, v_q_norm_b, v_k_norm_b, v_sink_a, v_rel_bias, v_w_branch_a, v_w_branch_b, v_b_merge, v_w_out):
    wsh = NW // NDEV
    csh = D // NDEV
    g_in, g_a, g_b, g_o, g_bm = _gather_two_level(
        [w_in[0].astype(bf16), w_branch_a[0].astype(bf16), w_branch_b[0].astype(bf16), w_out[0].astype(bf16),
         b_merge[0]], "gather_weights")
    w_in_full = g_in.transpose(1, 0, 2).reshape(D, NW)
    w_a_full = g_a.transpose(1, 0, 2).reshape(512, D)
    w_b_full = g_b.transpose(1, 0, 2).reshape(512, D)
    w_o_full = g_o.reshape(D, D)
    bm_full = g_bm.transpose(1, 0, 2).reshape(2, D)

    pending = {}

    def start_exchange(gw):
        blocks = [gw["w_in"].reshape(D, NDEV, wsh).transpose(1, 0, 2).astype(bf16),
                  gw["w_branch_a"].reshape(512, NDEV, csh).transpose(1, 0, 2).astype(bf16),
                  gw["w_branch_b"].reshape(512, NDEV, csh).transpose(1, 0, 2).astype(bf16),
                  gw["w_out"].reshape(NDEV, csh, D).astype(bf16),
                  gw["b_merge"].reshape(2, NDEV, csh).transpose(1, 0, 2)]
        pending["started"] = _scatter_start(blocks, "scatter_grads_start")
        return pending["started"][4][0, 0]

    loc = _local_step(x[0], loss_target[0], norm_gain, w_in_full, q_norm_a, k_norm_a, q_norm_b, k_norm_b, sink_a,
                      rel_bias, w_a_full, w_b_full, bm_full, w_o_full, on_weight_grads=start_exchange)

    small_shapes = dict(norm_gain=(1, D), q_norm_a=(1, HD), k_norm_a=(1, HD), q_norm_b=(1, HD), k_norm_b=(1, HD),
                        sink_a=(1, 8), rel_bias=(32, 32))
    (r_small,) = _exchange([], [_pack_small(loc, loc["loss"])], "gather_small_grads")
    send_sems, recv_sems, srcs, lands, _ = pending["started"]
    r_in, r_a, r_b, r_o, r_bm = _scatter_wait(send_sems, recv_sems, srcs, lands, r_small, "scatter_grads_wait")

    given = dict(norm_gain=norm_gain, q_norm_a=q_norm_a, k_norm_a=k_norm_a, q_norm_b=q_norm_b, k_norm_b=k_norm_b,
                 sink_a=sink_a, rel_bias=rel_bias)
    m_small = dict(norm_gain=m_norm_gain, q_norm_a=m_q_norm_a, k_norm_a=m_k_norm_a, q_norm_b=m_q_norm_b,
                   k_norm_b=m_k_norm_b, sink_a=m_sink_a, rel_bias=m_rel_bias)
    v_small = dict(norm_gain=v_norm_gain, q_norm_a=v_q_norm_a, k_norm_a=v_k_norm_a, q_norm_b=v_q_norm_b,
                   k_norm_b=v_k_norm_b, sink_a=v_sink_a, rel_bias=v_rel_bias)
    res = {
        "small": _adamw(_pack_small(given), r_small, _pack_small(m_small), _pack_small(v_small), "adamw_small"),
        "w_in": _adamw(w_in[0], r_in, m_w_in[0], v_w_in[0], "adamw_w_in"),
        "w_branch_a": _adamw(w_branch_a[0], r_a, m_w_branch_a[0], v_w_branch_a[0], "adamw_w_branch_a"),
        "w_branch_b": _adamw(w_branch_b[0], r_b, m_w_branch_b[0], v_w_branch_b[0], "adamw_w_branch_b"),
        "b_merge": _adamw(b_merge[0], r_bm, m_b_merge[0], v_b_merge[0], "adamw_b_merge"),
        "w_out": _adamw(w_out[0], r_o, m_w_out[0], v_w_out[0], "adamw_w_out"),
    }
    order = ["norm_gain", "w_in", "q_norm_a", "k_norm_a", "q_norm_b", "k_norm_b", "sink_a", "rel_bias", "w_branch_a",
             "w_branch_b", "b_merge", "w_out"]
    outs = []
    for k in range(4):
        small = _unpack_small(res["small"][k], small_shapes)
        for n in order:
            outs.append(small[n] if n in small else res[n][k][None])
    loss = res["small"][0][0, SMALL_USED]
    return (loss, loc["grad_x"][None], *outs)
```

```python
import math

import numpy as np
import jax
import jax.numpy as jnp
from jax import lax
from jax.experimental import pallas as pl
from jax.experimental.pallas import tpu as pltpu

f32 = jnp.float32
bf16 = jnp.bfloat16

S = 4096
D = 1024
NA = 5376
NT = 3072
NW = NA + NT
HD = 64
LANES = 128
EPS = 1e-6
NEG = -1e30
SCALE = HD ** -0.5
TQ = 128
PAD = 128
SP = S + 2 * PAD
NDEV = 8
GROUPS = ((128, 1, 0), (64, 1, 8), (64, 4, 16), (64, 16, 24))
CHUNK = 256
PCHUNK = 128
RC = 64
TN = 768

ADAM_LR, ADAM_B1, ADAM_B2, ADAM_EPS, ADAM_WD, ADAM_STEP = 0.001, 0.9, 0.999, 1e-08, 0.01, 10

MIB = 1024 * 1024
NT_DIMS = (((1,), (1,)), ((), ()))
TN_DIMS = (((0,), (0,)), ((), ()))


def _params(sem=None, vmem_mib=48):
    return pltpu.CompilerParams(dimension_semantics=sem, vmem_limit_bytes=vmem_mib * MIB)


def _lo():
    return lax.broadcasted_iota(jnp.int32, (1, LANES), 1) < HD


def _head_ones():
    r = lax.broadcasted_iota(jnp.int32, (LANES, LANES), 0) // HD
    c = lax.broadcasted_iota(jnp.int32, (LANES, LANES), 1) // HD
    return jnp.where(r == c, 1.0, 0.0).astype(bf16)


def _half_sums(x, ones):
    hi = x.astype(bf16)
    mid = (x - hi.astype(f32)).astype(bf16)
    return (jnp.dot(hi, ones, preferred_element_type=f32) + jnp.dot(mid, ones, preferred_element_type=f32))


def _seg_sum(x, ones):
    outs = [_half_sums(x[:, b * LANES:(b + 1) * LANES], ones) for b in range(x.shape[1] // LANES)]
    return outs[0] if len(outs) == 1 else jnp.concatenate(outs, axis=1)


def _bucket_np(blk, stride):
    w = TQ + 2 * blk
    rel = np.arange(w)[None, :] - blk - np.arange(TQ)[:, None]
    band = np.abs(rel) <= blk
    r = rel * stride
    n = np.abs(r)
    nf = np.maximum(n, 8).astype(np.float32)
    large = 8 + (np.log(nf / np.float32(8)) / np.float32(math.log(128.0)) * np.float32(8)).astype(np.int32)
    large = np.minimum(large, 15)
    b = (r > 0).astype(np.int32) * 16 + np.where(n < 8, n, large)
    return np.where(band, b, -1).astype(np.int32)


def _rms_inproj(x, gain, w):
    ts = 512
    na, nt = NA // TN, NT // TN

    def body(x_ref, g_ref, w_ref, pa_ref, pt_ref, h_ref, r_ref, hs):
        j = pl.program_id(1)

        @pl.when(j == 0)
        def _():
            xv = x_ref[...]
            r = lax.rsqrt(jnp.mean(xv * xv, axis=-1, keepdims=True) + EPS)
            hs[...] = ((xv * r) * g_ref[...]).astype(bf16)
            h_ref[...] = hs[...]
            r_ref[...] = r

        acc = jnp.dot(hs[...], w_ref[...], preferred_element_type=f32)

        @pl.when(j < na)
        def _():
            pa_ref[...] = acc

        @pl.when(j >= na)
        def _():
            pt_ref[...] = acc

    return pl.pallas_call(
        body,
        grid=(S // ts, na + nt),
        in_specs=[
            pl.BlockSpec((ts, D), lambda i, j: (i, 0)),
            pl.BlockSpec((1, D), lambda i, j: (0, 0)),
            pl.BlockSpec((D, TN), lambda i, j: (0, j)),
        ],
        out_specs=[
            pl.BlockSpec((ts, TN), lambda i, j: (i, jnp.minimum(j, na - 1))),
            pl.BlockSpec((ts, TN), lambda i, j: (i, jnp.maximum(j - na, 0))),
            pl.BlockSpec((ts, D), lambda i, j: (i, 0)),
            pl.BlockSpec((ts, 1), lambda i, j: (i, 0)),
        ],
        out_shape=[
            jax.ShapeDtypeStruct((S, NA), f32),
            jax.ShapeDtypeStruct((S, NT), f32),
            jax.ShapeDtypeStruct((S, D), bf16),
            jax.ShapeDtypeStruct((S, 1), f32),
        ],
        scratch_shapes=[pltpu.VMEM((ts, D), bf16)],
        compiler_params=_params(("arbitrary", "arbitrary")),
        name="rms_inproj",
    )(x, gain, w)


def _bias_expand(table, bucket, c0, name):
    tq, w = bucket.shape

    def body(tab_ref, bk_ref, o_ref):
        h = pl.program_id(0)
        bk = bk_ref[...]

        def step(b, acc):
            return jnp.where(bk == b, tab_ref[b, c0 + h], acc)

        o_ref[...] = lax.fori_loop(0, 32, step, jnp.full((tq, w), NEG, f32))

    return pl.pallas_call(
        body,
        grid=(8,),
        in_specs=[pl.BlockSpec(memory_space=pltpu.SMEM), pl.BlockSpec((tq, w), lambda h: (0, 0))],
        out_specs=pl.BlockSpec((None, tq, w), lambda h: (h, 0, 0)),
        out_shape=jax.ShapeDtypeStruct((8, tq, w), f32),
        compiler_params=_params(("arbitrary",)),
        name=name,
    )(table, bucket)


def _col_block(g, j):
    kind = j // 4
    hp = j % 4
    a = jnp.where(kind == 0, hp, 3 + kind)
    b = 6 + 12 * kind + 4 * (g - 1) + hp
    return jnp.where(g == 0, a, b)


def _prep(proj_a, gains):
    def body(p_ref, g_ref, o_ref):
        g = pl.program_id(0)
        j = pl.program_id(1)
        kind = j // 4
        lo = _lo()
        ones = _head_ones()
        half = jnp.where(lo, 0, 1)
        take = (kind == 0) | (half == (j % 4) // 2)
        gain = g_ref[...]
        o_ref[0:PAD, :] = jnp.zeros((PAD, LANES), bf16)
        o_ref[PAD + S:SP, :] = jnp.zeros((PAD, LANES), bf16)

        def norm_store(xv, dst, dup):
            if dup:
                xv = jnp.where(take, xv, pltpu.roll(xv, HD, 1))
            r = lax.rsqrt(_half_sums(xv * xv, ones) * (1.0 / HD) + EPS)
            r = jnp.where(kind == 2, 1.0, r)
            yv = (xv * r) * gain
            yv = jnp.where(kind == 0, yv * SCALE, yv)
            o_ref[PAD + dst:PAD + dst + CHUNK, :] = yv.astype(bf16)

        for gi, (_, d, _) in enumerate(GROUPS):
            @pl.when(g == gi)
            def _():
                seq = S // d
                for c in range(d):
                    for i in range(seq // CHUNK):
                        if d == 1:
                            xv = p_ref[i * CHUNK:(i + 1) * CHUNK, :]
                        else:
                            xv = p_ref[pl.ds(c + i * CHUNK * d, CHUNK, stride=d), :]
                        norm_store(xv, c * seq + i * CHUNK, gi == 0)

    return pl.pallas_call(
        body,
        grid=(4, 12),
        in_specs=[
            pl.BlockSpec((S, LANES), lambda g, j: (0, _col_block(g, j))),
            pl.BlockSpec((None, None, 1, LANES), lambda g, j: (g, j // 4, 0, 0)),
        ],
        out_specs=pl.BlockSpec((None, None, SP, LANES), lambda g, j: (g, j, 0, 0)),
        out_shape=jax.ShapeDtypeStruct((4, 12, SP, LANES), bf16),
        compiler_params=_params(("arbitrary", "arbitrary")),
        name="prep",
    )(proj_a, gains)


def _seq_len(g, ng):
    return S if ng == 1 else jnp.right_shift(S, 2 * g)


def _stack_heads(t, lo):
    z = jnp.zeros_like(t)
    return jnp.concatenate([jnp.where(lo, t, z), jnp.where(lo, z, t)], axis=0)


def _unstack_heads(t2, lo):
    return jnp.where(lo, t2[:TQ], t2[TQ:])


def _row_spec(natural, ng):
    if natural:
        return pl.BlockSpec((S, LANES), lambda g, hp: (0, hp)), (S, 4 * LANES)
    return pl.BlockSpec((None, None, S, LANES), lambda g, hp: (g, hp, 0, 0)), (ng, 4, S, LANES)


def _attn_fwd(gl, bias, sink, g0, ng, blk, natural, name):
    w = TQ + 2 * blk
    use_sink = sink is not None

    def body(*refs):
        if use_sink:
            sink_ref, q_ref, k_ref, v_ref, b_ref, o_ref, l_ref, s0, s1, p0, p1, lse_scr = refs
        else:
            q_ref, k_ref, v_ref, b_ref, o_ref, l_ref, s0, s1, p0, p1, lse_scr = refs
        g = pl.program_id(0)
        hp = pl.program_id(1)
        lo = _lo()
        seq = _seq_len(g, ng)
        mi = lax.broadcasted_iota(jnp.int32, (1, w), 1)
        s_bufs, p_bufs = (s0, s1), (p0, p1)

        def scores(t, slot):
            f0 = pl.multiple_of(t * TQ, TQ)
            q2 = _stack_heads(q_ref[pl.ds(PAD + f0, TQ), :], lo)
            kw = k_ref[pl.ds(PAD - blk + f0, w), :]
            s_bufs[slot][...] = lax.dot_general(q2, kw, NT_DIMS, preferred_element_type=f32)

        def softmax(t, slot):
            f0 = pl.multiple_of(t * TQ, TQ)
            m0 = jnp.bitwise_and(f0, seq - 1)
            inside = (mi >= blk - m0) & (mi < seq + blk - m0)
            for h in range(2):
                for r in range(TQ // RC):
                    rows = slice(h * TQ + r * RC, h * TQ + (r + 1) * RC)
                    logit = jnp.where(inside, s_bufs[slot][rows, :] + b_ref[h, r * RC:(r + 1) * RC, :], NEG)
                    m = jnp.max(logit, axis=1, keepdims=True)
                    e = jnp.exp(logit - m)
                    lse = m + jnp.log(jnp.sum(e, axis=1, keepdims=True))
                    if use_sink:
                        sk = sink_ref[2 * hp + h]
                        mx = jnp.maximum(lse, sk)
                        lse = mx + jnp.log(jnp.exp(lse - mx) + jnp.exp(sk - mx))
                    p_bufs[slot][rows, :] = (e * jnp.exp(m - lse)).astype(bf16)
                    lse_scr[rows, :] = jnp.broadcast_to(lse, (RC, LANES))
            l_ref[pl.ds(f0, TQ), :] = jnp.where(lo, lse_scr[0:TQ, :], lse_scr[TQ:2 * TQ, :])

        def values(t, slot):
            f0 = pl.multiple_of(t * TQ, TQ)
            vw = v_ref[pl.ds(PAD - blk + f0, w), :]
            o2 = jnp.dot(p_bufs[slot][...], vw, preferred_element_type=f32)
            o_ref[pl.ds(f0, TQ), :] = _unstack_heads(o2, lo)

        nt = S // TQ
        scores(0, 0)
        scores(1, 1)
        softmax(0, 0)

        def pair(k, carry):
            t = 2 * k + 2
            scores(t, 0)
            softmax(t - 1, 1)
            values(t - 2, 0)
            scores(t + 1, 1)
            softmax(t, 0)
            values(t - 1, 1)
            return carry

        lax.fori_loop(0, (nt - 2) // 2, pair, 0)
        softmax(nt - 1, 1)
        values(nt - 2, 0)
        values(nt - 1, 1)

    in_specs = [
        pl.BlockSpec((None, None, SP, LANES), lambda g, hp: (g0 + g, hp, 0, 0)),
        pl.BlockSpec((None, None, SP, LANES), lambda g, hp: (g0 + g, 4 + hp, 0, 0)),
        pl.BlockSpec((None, None, SP, LANES), lambda g, hp: (g0 + g, 8 + hp, 0, 0)),
        pl.BlockSpec((None, 2, TQ, w), lambda g, hp: (g, hp, 0, 0)),
    ]
    args = [gl, gl, gl, bias]
    if use_sink:
        in_specs = [pl.BlockSpec(memory_space=pltpu.SMEM)] + in_specs
        args = [sink] + args
    out, shape = _row_spec(natural, ng)
    return pl.pallas_call(
        body,
        grid=(ng, 4),
        in_specs=in_specs,
        out_specs=[out, out],
        out_shape=[jax.ShapeDtypeStruct(shape, f32)] * 2,
        scratch_shapes=[pltpu.VMEM((2 * TQ, w), f32), pltpu.VMEM((2 * TQ, w), f32),
                        pltpu.VMEM((2 * TQ, w), bf16), pltpu.VMEM((2 * TQ, w), bf16),
                        pltpu.VMEM((2 * TQ, LANES), f32)],
        compiler_params=_params(("arbitrary", "arbitrary")),
        name=name,
    )(*args)


def _attn_bwd(gl, bias, bucket, do, lse, dd, g0, ng, blk, natural, name):
    w = TQ + 2 * blk

    def body(q_ref, k_ref, v_ref, b_ref, bk_ref, do_ref, l_ref, d_ref, dqkv_ref, dbk_ref,
             db_acc, s0, s1, dp0, dp1, pb0, pb1, ds0, ds1):
        g = pl.program_id(0)
        lo = _lo()
        hi = jnp.logical_not(lo)
        seq = _seq_len(g, ng)
        mi = lax.broadcasted_iota(jnp.int32, (1, w), 1)
        dqkv_ref[1] = jnp.zeros((SP, LANES), f32)
        dqkv_ref[2] = jnp.zeros((SP, LANES), f32)
        db_acc[...] = jnp.zeros((2 * TQ, w), f32)
        s_bufs, dp_bufs, pb_bufs, ds_bufs = (s0, s1), (dp0, dp1), (pb0, pb1), (ds0, ds1)

        def stacked(t):
            f0 = pl.multiple_of(t * TQ, TQ)
            q2 = _stack_heads(q_ref[pl.ds(PAD + f0, TQ), :], lo)
            do2 = _stack_heads(do_ref[pl.ds(f0, TQ), :], lo)
            return f0, q2, do2

        def scores(t, slot):
            f0, q2, do2 = stacked(t)
            win = pl.ds(PAD - blk + f0, w)
            s_bufs[slot][...] = lax.dot_general(q2, k_ref[win, :], NT_DIMS, preferred_element_type=f32)
            dp_bufs[slot][...] = lax.dot_general(do2, v_ref[win, :], NT_DIMS, preferred_element_type=f32)

        def grads(t, slot):
            f0 = pl.multiple_of(t * TQ, TQ)
            m0 = jnp.bitwise_and(f0, seq - 1)
            inside = (mi >= blk - m0) & (mi < seq + blk - m0)
            for h in range(2):
                msk = lo if h == 0 else hi
                for r in range(TQ // RC):
                    rows = slice(h * TQ + r * RC, h * TQ + (r + 1) * RC)
                    src = pl.ds(f0 + r * RC, RC)
                    lh = jnp.max(jnp.where(msk, l_ref[src, :], -jnp.inf), axis=1, keepdims=True)
                    dh = jnp.max(jnp.where(msk, d_ref[src, :], -jnp.inf), axis=1, keepdims=True)
                    logit = jnp.where(inside, s_bufs[slot][rows, :] + b_ref[h, r * RC:(r + 1) * RC, :], NEG)
                    p = jnp.exp(logit - lh)
                    ds = p * (dp_bufs[slot][rows, :] - dh)
                    db_acc[rows, :] += ds
                    pb_bufs[slot][rows, :] = p.astype(bf16)
                    ds_bufs[slot][rows, :] = ds.astype(bf16)

        def accumulate(t, slot):
            f0, q2, do2 = stacked(t)
            win = pl.ds(PAD - blk + f0, w)
            dsb = ds_bufs[slot][...]
            dq2 = jnp.dot(dsb, k_ref[win, :], preferred_element_type=f32)
            dqkv_ref[0, pl.ds(PAD + f0, TQ), :] = _unstack_heads(dq2, lo)
            dqkv_ref[1, win, :] += lax.dot_general(dsb, q2, TN_DIMS, preferred_element_type=f32)
            dqkv_ref[2, win, :] += lax.dot_general(pb_bufs[slot][...], do2, TN_DIMS, preferred_element_type=f32)

        nt = S // TQ
        scores(0, 0)
        scores(1, 1)
        grads(0, 0)

        def pair(k, carry):
            t = 2 * k + 2
            scores(t, 0)
            grads(t - 1, 1)
            accumulate(t - 2, 0)
            scores(t + 1, 1)
            grads(t, 0)
            accumulate(t - 1, 1)
            return carry

        lax.fori_loop(0, (nt - 2) // 2, pair, 0)
        grads(nt - 1, 1)
        accumulate(nt - 2, 0)
        accumulate(nt - 1, 1)

        bk = bk_ref[...]
        lane = lax.broadcasted_iota(jnp.int32, (8, LANES), 1)
        for h in range(2):
            db = db_acc[h * TQ:(h + 1) * TQ, :]
            acc = jnp.zeros((8, LANES), f32)
            for b in range(32):
                part = jnp.where(bk == b, db, 0.0).reshape(TQ // 8, 8, w).sum(axis=0)
                tot = jnp.sum(jnp.sum(part, axis=1, keepdims=True), axis=0, keepdims=True)
                acc = jnp.where(lane == b, tot, acc)
            dbk_ref[h] = acc

    def gcol(off):
        return pl.BlockSpec((None, None, SP, LANES), lambda g, hp: (g0 + g, off + hp, 0, 0))

    row, _ = _row_spec(natural, ng)
    return pl.pallas_call(
        body,
        grid=(ng, 4),
        in_specs=[gcol(0), gcol(4), gcol(8), pl.BlockSpec((None, 2, TQ, w), lambda g, hp: (g, hp, 0, 0)),
                  pl.BlockSpec((None, TQ, w), lambda g, hp: (g, 0, 0)), row, row, row],
        out_specs=[pl.BlockSpec((None, 3, None, SP, LANES), lambda g, hp: (g, 0, hp, 0, 0)),
                   pl.BlockSpec((None, 2, 8, LANES), lambda g, hp: (g, hp, 0, 0))],
        out_shape=[
            jax.ShapeDtypeStruct((ng, 3, 4, SP, LANES), f32),
            jax.ShapeDtypeStruct((ng, 8, 8, LANES), f32),
        ],
        scratch_shapes=[pltpu.VMEM((2 * TQ, w), f32)] * 5 + [pltpu.VMEM((2 * TQ, w), bf16)] * 4,
        compiler_params=_params(("arbitrary", "arbitrary"), vmem_mib=56),
        name=name,
    )(gl, gl, gl, bias, bucket, do, lse, dd)


def _b_to_natural(o_gl, l_gl):
    def body(o_ref, l_ref, on_ref, ln_ref):
        g = pl.program_id(0)
        for gi in range(3):
            d = GROUPS[gi + 1][1]

            @pl.when(g == gi)
            def _():
                seq = S // d
                for c in range(d):
                    for i in range(seq // CHUNK):
                        src = slice(c * seq + i * CHUNK, c * seq + (i + 1) * CHUNK)
                        if d == 1:
                            on_ref[src, :] = o_ref[src, :]
                            ln_ref[src, :] = l_ref[src, :]
                        else:
                            dst = pl.ds(c + i * CHUNK * d, CHUNK, stride=d)
                            on_ref[dst, :] = o_ref[src, :]
                            ln_ref[dst, :] = l_ref[src, :]

    col = pl.BlockSpec((None, None, S, LANES), lambda g, hp: (g, hp, 0, 0))
    nat = pl.BlockSpec((S, LANES), lambda g, hp: (0, 4 * g + hp))
    return pl.pallas_call(
        body,
        grid=(3, 4),
        in_specs=[col, col],
        out_specs=[nat, nat],
        out_shape=[jax.ShapeDtypeStruct((S, 3 * 512), f32)] * 2,
        compiler_params=_params(("arbitrary", "arbitrary")),
        name="b_to_natural",
    )(o_gl, l_gl)


def _b_from_natural(do_n, dd_n):
    def body(do_ref, dd_ref, dog_ref, ddg_ref):
        g = pl.program_id(0)
        for gi in range(3):
            d = GROUPS[gi + 1][1]

            @pl.when(g == gi)
            def _():
                seq = S // d
                for c in range(d):
                    for i in range(seq // CHUNK):
                        dst = slice(c * seq + i * CHUNK, c * seq + (i + 1) * CHUNK)
                        if d == 1:
                            a, b = do_ref[dst, :], dd_ref[dst, :]
                        else:
                            src = pl.ds(c + i * CHUNK * d, CHUNK, stride=d)
                            a, b = do_ref[src, :], dd_ref[src, :]
                        dog_ref[dst, :] = a.astype(bf16)
                        ddg_ref[dst, :] = b

    col = pl.BlockSpec((None, None, S, LANES), lambda g, hp: (g, hp, 0, 0))
    nat = pl.BlockSpec((S, LANES), lambda g, hp: (0, 4 * g + hp))
    return pl.pallas_call(
        body,
        grid=(3, 4),
        in_specs=[nat, nat],
        out_specs=[col, col],
        out_shape=[jax.ShapeDtypeStruct((3, 4, S, LANES), bf16), jax.ShapeDtypeStruct((3, 4, S, LANES), f32)],
        compiler_params=_params(("arbitrary", "arbitrary")),
        name="b_from_natural",
    )(do_n, dd_n)


def _sigmoid(z):
    return 1.0 / (1.0 + jnp.exp(-z))


def _tail(x, tgt, o_a, l_a, o_b, l_b, proj_t, bm, w_a, w_b, w_o, sink_b):
    ts = 128

    def body(x_ref, t_ref, oa_ref, la_ref, ob_ref, lb_ref, ga_ref, gb_ref, m0_ref, m1_ref, bm_ref,
             wa_ref, wb_ref, wo_ref, sk_ref,
             dy_ref, dyb_ref, dt_ref, doa_ref, dda_ref, dob_ref, ddb_ref, ya_ref, yb_ref, mg_ref, dbra_ref, dbrb_ref,
             loss_ref, dbm_ref, dsk_ref):
        i = pl.program_id(0)

        @pl.when(i == 0)
        def _():
            loss_ref[...] = jnp.zeros_like(loss_ref)
            dbm_ref[...] = jnp.zeros_like(dbm_ref)
            dsk_ref[...] = jnp.zeros_like(dsk_ref)

        ga = ga_ref[...]
        sa = _sigmoid(ga)
        silu_a = ga * sa
        oa = oa_ref[...]
        ya = oa * silu_a
        gb = gb_ref[...]
        sb = _sigmoid(gb)
        silu_b = gb * sb
        ob = [ob_ref[:, k * 512:(k + 1) * 512] for k in range(3)]
        lb = [lb_ref[:, k * 512:(k + 1) * 512] for k in range(3)]
        mx = jnp.maximum(jnp.maximum(lb[0], lb[1]), lb[2])
        ex = [jnp.exp(v - mx) for v in lb]
        den = ex[0] + ex[1] + ex[2]
        alpha = [e / den for e in ex]
        ybc = alpha[0] * ob[0] + alpha[1] * ob[1] + alpha[2] * ob[2]
        yb = ybc * silu_b
        yab = ya.astype(bf16)
        ybb = yb.astype(bf16)
        br_a = jnp.dot(yab, wa_ref[...], preferred_element_type=f32)
        br_b = jnp.dot(ybb, wb_ref[...], preferred_element_type=f32)
        g0 = _sigmoid(m0_ref[...] + bm_ref[0:1, :])
        g1 = _sigmoid(m1_ref[...] + bm_ref[1:2, :])
        merged = g0 * br_a + g1 * br_b
        mgb = merged.astype(bf16)
        y = x_ref[...] + jnp.dot(mgb, wo_ref[...], preferred_element_type=f32)
        err = y - t_ref[...]
        part = jnp.sum(jnp.sum(err * err, axis=1, keepdims=True), axis=0, keepdims=True)
        loss_ref[...] += part * (0.5 / D)
        dy = err * (1.0 / D)
        dyb = dy.astype(bf16)
        dmerged = lax.dot_general(dyb, wo_ref[...], NT_DIMS, preferred_element_type=f32)
        dbr_a = (dmerged * g0).astype(bf16)
        dbr_b = (dmerged * g1).astype(bf16)
        dm0 = dmerged * br_a * (g0 * (1.0 - g0))
        dm1 = dmerged * br_b * (g1 * (1.0 - g1))
        dbm_ref[0:1, :] += jnp.sum(dm0, axis=0, keepdims=True)
        dbm_ref[1:2, :] += jnp.sum(dm1, axis=0, keepdims=True)
        dya = lax.dot_general(dbr_a, wa_ref[...], NT_DIMS, preferred_element_type=f32)
        dyb2 = lax.dot_general(dbr_b, wb_ref[...], NT_DIMS, preferred_element_type=f32)
        do_a = dya * silu_a
        dga = dya * oa * (sa * (1.0 + ga * (1.0 - sa)))
        ones = _head_ones()
        delta_a = _seg_sum(do_a * oa, ones)
        dsk_ref[...] -= jnp.sum(delta_a * jnp.exp(sk_ref[...] - la_ref[...]), axis=0, keepdims=True)
        dybc = dyb2 * silu_b
        dgb = dyb2 * ybc * (sb * (1.0 + gb * (1.0 - sb)))
        dbar = _seg_sum(dybc * ybc, ones)
        dy_ref[...] = dy
        dyb_ref[...] = dyb
        dt_ref[:, 0:512] = dga.astype(bf16)
        dt_ref[:, 512:1024] = dgb.astype(bf16)
        dt_ref[:, 1024:2048] = dm0.astype(bf16)
        dt_ref[:, 2048:3072] = dm1.astype(bf16)
        doa_ref[...] = do_a.astype(bf16)
        dda_ref[...] = delta_a
        for k in range(3):
            dob_ref[:, k * 512:(k + 1) * 512] = alpha[k] * dybc
            ddb_ref[:, k * 512:(k + 1) * 512] = alpha[k] * dbar
        ya_ref[...] = yab
        yb_ref[...] = ybb
        mg_ref[...] = mgb
        dbra_ref[...] = dbr_a
        dbrb_ref[...] = dbr_b

    def rows(n, blk=0):
        return pl.BlockSpec((ts, n), lambda i: (i, blk))

    def whole(r, c):
        return pl.BlockSpec((r, c), lambda i: (0, 0))

    outs = [
        ((S, D), f32, rows(D)), ((S, D), bf16, rows(D)), ((S, NT), bf16, rows(NT)),
        ((S, 512), bf16, rows(512)), ((S, 512), f32, rows(512)),
        ((S, 1536), f32, rows(1536)), ((S, 1536), f32, rows(1536)),
        ((S, 512), bf16, rows(512)), ((S, 512), bf16, rows(512)), ((S, D), bf16, rows(D)),
        ((S, D), bf16, rows(D)), ((S, D), bf16, rows(D)),
        ((1, 1), f32, whole(1, 1)), ((2, D), f32, whole(2, D)), ((1, 512), f32, whole(1, 512)),
    ]
    return pl.pallas_call(
        body,
        grid=(S // ts,),
        in_specs=[
            rows(D), rows(D), rows(512), rows(512), rows(1536), rows(1536),
            rows(512, 0), rows(512, 1), rows(D, 1), rows(D, 2), whole(2, D),
            whole(512, D), whole(512, D), whole(D, D), whole(1, 512),
        ],
        out_specs=[o[2] for o in outs],
        out_shape=[jax.ShapeDtypeStruct(o[0], o[1]) for o in outs],
        compiler_params=_params(("arbitrary",)),
        name="tail",
    )(x, tgt, o_a, l_a, o_b, l_b, proj_t, proj_t, proj_t, proj_t, bm, w_a, w_b, w_o, sink_b)


def _norm_bwd(xv, dyv, gain, kind, ones):
    r = lax.rsqrt(_half_sums(xv * xv, ones) * (1.0 / HD) + EPS)
    yv = xv * r
    up = jnp.where(kind == 0, dyv * SCALE, dyv)
    u = up * gain
    dxv = r * (u - yv * (_half_sums(u * yv, ones) * (1.0 / HD)))
    dxv = jnp.where(kind == 2, dyv, dxv)
    dg = jnp.where(kind == 2, 0.0, jnp.sum(up * yv, axis=0, keepdims=True))
    return dxv, dg


def _post_b(dqkv, proj_a, gains, dproj):
    def body(d_ref, p_ref, g_ref, alias_ref, o_ref, dg_ref, nat):
        del alias_ref
        g = pl.program_id(0)
        j = pl.program_id(1)
        kind = j // 4
        gain = g_ref[...]
        ones = _head_ones()

        @pl.when(j % 4 == 0)
        def _():
            dg_ref[...] = jnp.zeros_like(dg_ref)

        for gi in range(3):
            d = GROUPS[gi + 1][1]

            @pl.when(g == gi)
            def _():
                seq = S // d
                for c in range(d):
                    for i in range(seq // PCHUNK):
                        src = c * seq + i * PCHUNK
                        if d == 1:
                            idx = slice(src, src + PCHUNK)
                        else:
                            idx = pl.ds(c + i * PCHUNK * d, PCHUNK, stride=d)
                        dxv, dg = _norm_bwd(p_ref[idx, :], d_ref[PAD + src:PAD + src + PCHUNK, :], gain, kind, ones)
                        nat[idx, :] = dxv
                        dg_ref[...] += dg

        for i in range(S // CHUNK):
            o_ref[i * CHUNK:(i + 1) * CHUNK, :] = nat[i * CHUNK:(i + 1) * CHUNK, :].astype(bf16)

    return pl.pallas_call(
        body,
        grid=(3, 12),
        in_specs=[
            pl.BlockSpec((None, None, None, SP, LANES), lambda g, j: (g, j // 4, j % 4, 0, 0)),
            pl.BlockSpec((S, LANES), lambda g, j: (0, _col_block(g + 1, j))),
            pl.BlockSpec((None, None, 1, LANES), lambda g, j: (g + 1, j // 4, 0, 0)),
            pl.BlockSpec(memory_space=pl.ANY),
        ],
        out_specs=[
            pl.BlockSpec((S, LANES), lambda g, j: (0, _col_block(g + 1, j))),
            pl.BlockSpec((None, None, 1, LANES), lambda g, j: (g, j // 4, 0, 0)),
        ],
        out_shape=[jax.ShapeDtypeStruct((S, NA), bf16), jax.ShapeDtypeStruct((3, 3, 1, LANES), f32)],
        scratch_shapes=[pltpu.VMEM((S, LANES), f32)],
        input_output_aliases={3: 0},
        compiler_params=_params(("arbitrary", "arbitrary")),
        name="post_b",
    )(dqkv, proj_a, gains, dproj)


def _post_a(dqkv, proj_a, gains):
    def body(q_ref, e_ref, p_ref, g_ref, o_ref, dg_ref):
        j = pl.program_id(0)
        kind = jnp.maximum(j - 3, 0)
        gain = g_ref[...]
        lo = _lo()
        ones = _head_ones()

        @pl.when((j == 0) | (j >= 4))
        def _():
            dg_ref[...] = jnp.zeros_like(dg_ref)

        for i in range(S // PCHUNK):
            r0 = i * PCHUNK
            rows = slice(PAD + r0, PAD + r0 + PCHUNK)
            t0 = e_ref[0, rows, :] + e_ref[1, rows, :]
            t1 = e_ref[2, rows, :] + e_ref[3, rows, :]
            folded = jnp.where(lo, t0 + pltpu.roll(t0, HD, 1), t1 + pltpu.roll(t1, HD, 1))
            dyv = jnp.where(kind == 0, q_ref[rows, :], folded)
            dxv, dg = _norm_bwd(p_ref[r0:r0 + PCHUNK, :], dyv, gain, kind, ones)
            o_ref[r0:r0 + PCHUNK, :] = dxv.astype(bf16)
            dg_ref[...] += dg

    return pl.pallas_call(
        body,
        grid=(6,),
        in_specs=[
            pl.BlockSpec((None, None, None, SP, LANES), lambda j: (0, 0, jnp.minimum(j, 3), 0, 0)),
            pl.BlockSpec((None, None, 4, SP, LANES), lambda j: (0, jnp.clip(j - 3, 1, 2), 0, 0, 0)),
            pl.BlockSpec((S, LANES), lambda j: (0, j)),
            pl.BlockSpec((None, None, 1, LANES), lambda j: (0, jnp.maximum(j - 3, 0), 0, 0)),
        ],
        out_specs=[
            pl.BlockSpec((S, LANES), lambda j: (0, j)),
            pl.BlockSpec((None, 1, LANES), lambda j: (jnp.maximum(j - 3, 0), 0, 0)),
        ],
        out_shape=[jax.ShapeDtypeStruct((S, NA), bf16), jax.ShapeDtypeStruct((3, 1, LANES), f32)],
        compiler_params=_params(("arbitrary",)),
        name="post_a",
    )(dqkv, dqkv, proj_a, gains)


def _dh_norm_bwd(dproj_a, dproj_t, w, x, rstd, gain, dy):
    ts = 512
    na, nt = NA // TN, NT // TN

    def body(da_ref, dt_ref, w_ref, x_ref, r_ref, g_ref, dy_ref, gx_ref, dgn_ref, acc):
        i = pl.program_id(0)
        k = pl.program_id(1)

        @pl.when((i == 0) & (k == 0))
        def _():
            dgn_ref[...] = jnp.zeros_like(dgn_ref)

        @pl.when(k == 0)
        def _():
            acc[...] = jnp.zeros_like(acc)

        @pl.when(k < na)
        def _():
            acc[...] += lax.dot_general(da_ref[...], w_ref[...], NT_DIMS, preferred_element_type=f32)

        @pl.when(k >= na)
        def _():
            acc[...] += lax.dot_general(dt_ref[...], w_ref[...], NT_DIMS, preferred_element_type=f32)

        @pl.when(k == na + nt - 1)
        def _():
            dh = acc[...]
            xh = x_ref[...] * r_ref[...]
            u = dh * g_ref[...]
            dx = r_ref[...] * (u - xh * jnp.mean(u * xh, axis=-1, keepdims=True))
            gx_ref[...] = dy_ref[...] + dx
            dgn_ref[...] += jnp.sum(dh * xh, axis=0, keepdims=True)

    return pl.pallas_call(
        body,
        grid=(S // ts, na + nt),
        in_specs=[
            pl.BlockSpec((ts, TN), lambda i, k: (i, jnp.minimum(k, na - 1))),
            pl.BlockSpec((ts, TN), lambda i, k: (i, jnp.maximum(k - na, 0))),
            pl.BlockSpec((D, TN), lambda i, k: (0, k)),
            pl.BlockSpec((ts, D), lambda i, k: (i, 0)),
            pl.BlockSpec((ts, 1), lambda i, k: (i, 0)),
            pl.BlockSpec((1, D), lambda i, k: (0, 0)),
            pl.BlockSpec((ts, D), lambda i, k: (i, 0)),
        ],
        out_specs=[pl.BlockSpec((ts, D), lambda i, k: (i, 0)), pl.BlockSpec((1, D), lambda i, k: (0, 0))],
        out_shape=[jax.ShapeDtypeStruct((S, D), f32), jax.ShapeDtypeStruct((1, D), f32)],
        scratch_shapes=[pltpu.VMEM((ts, D), f32)],
        compiler_params=_params(("arbitrary", "arbitrary")),
        name="dh_norm_bwd",
    )(dproj_a, dproj_t, w, x, rstd, gain, dy)


def _matmul_tn(a, b, name):
    m, n = a.shape[1], b.shape[1]
    tn = TN if n % TN == 0 else 512
    tk = 512

    def body(a_ref, b_ref, o_ref):
        @pl.when(pl.program_id(1) == 0)
        def _():
            o_ref[...] = jnp.zeros_like(o_ref)

        o_ref[...] += lax.dot_general(a_ref[...], b_ref[...], TN_DIMS, preferred_element_type=f32)

    return pl.pallas_call(
        body,
        grid=(n // tn, S // tk),
        in_specs=[pl.BlockSpec((tk, m), lambda j, k: (k, 0)), pl.BlockSpec((tk, tn), lambda j, k: (k, j))],
        out_specs=pl.BlockSpec((m, tn), lambda j, k: (0, j)),
        out_shape=jax.ShapeDtypeStruct((m, n), f32),
        compiler_params=_params(("arbitrary", "arbitrary")),
        name=name,
    )(a, b)


def _exchange(scatter, gather, name):
    arrs = list(scatter) + list(gather)
    n = len(arrs)
    ns = len(scatter)

    def body(*refs):
        ins, outs = refs[:n], refs[n:2 * n]
        send_sems, recv_sems, local_sems = refs[2 * n:]
        x, y, c = lax.axis_index("x"), lax.axis_index("y"), lax.axis_index("c")
        me = 4 * x + 2 * y + c
        local, remote = [], []
        for a in range(n):
            lc = pltpu.make_async_copy(ins[a].at[me] if a < ns else ins[a], outs[a].at[me], local_sems.at[a])
            lc.start()
            local.append(lc)
            for r in range(1, NDEV):
                px = 1 - x if r & 4 else x
                py = 1 - y if r & 2 else y
                pc = 1 - c if r & 1 else c
                cp = pltpu.make_async_remote_copy(
                    src_ref=ins[a].at[4 * px + 2 * py + pc] if a < ns else ins[a],
                    dst_ref=outs[a].at[me],
                    send_sem=send_sems.at[a, r - 1],
                    recv_sem=recv_sems.at[a, r - 1],
                    device_id=(px, py, pc),
                    device_id_type=pl.DeviceIdType.MESH,
                )
                cp.start()
                remote.append(cp)
        for cp in remote:
            cp.wait_recv()
        for cp in remote:
            cp.wait_send()
        for lc in local:
            lc.wait()

    out_shape = [jax.ShapeDtypeStruct(a.shape if i < ns else (NDEV,) + a.shape, a.dtype) for i, a in enumerate(arrs)]
    return pl.pallas_call(
        body,
        in_specs=[pl.BlockSpec(memory_space=pl.ANY)] * n,
        out_specs=[pl.BlockSpec(memory_space=pl.ANY)] * n,
        out_shape=out_shape,
        scratch_shapes=[
            pltpu.SemaphoreType.DMA((n, NDEV - 1)),
            pltpu.SemaphoreType.DMA((n, NDEV - 1)),
            pltpu.SemaphoreType.DMA((n,)),
        ],
        compiler_params=pltpu.CompilerParams(has_side_effects=True),
        name=name,
    )(*arrs)


def _gather_two_level(arrs, name):
    n = len(arrs)

    def body(*refs):
        ins, outs = refs[:n], refs[n:2 * n]
        send_sems, recv_sems, local_sems = refs[2 * n:]
        x, y, c = lax.axis_index("x"), lax.axis_index("y"), lax.axis_index("c")
        me, sibling = (x, y, c), (x, y, 1 - c)
        chips = [(1 - x, y), (x, 1 - y), (1 - x, 1 - y)]

        def copy(a, k, block, to, src=None):
            slot = outs[a].at[4 * block[0] + 2 * block[1] + block[2]]
            return pltpu.make_async_remote_copy(
                src_ref=slot if src is None else src, dst_ref=slot, send_sem=send_sems.at[a, k],
                recv_sem=recv_sems.at[a, k], device_id=to, device_id_type=pl.DeviceIdType.MESH)

        mine, first, passed = [], [], []
        for a in range(n):
            lc = pltpu.make_async_copy(ins[a], outs[a].at[4 * x + 2 * y + c], local_sems.at[a])
            lc.start()
            mine.append(lc)
            first.append(copy(a, 0, me, sibling, src=ins[a]))
            first += [copy(a, 1 + j, me, (*chip, c), src=ins[a]) for j, chip in enumerate(chips)]
        for cp in first:
            cp.start()
        for j, chip in enumerate(chips):
            for a in range(n):
                copy(a, 1 + j, (*chip, c), me).wait_recv()
                fwd = copy(a, 4 + j, (*chip, c), sibling)
                fwd.start()
                passed.append(fwd)
        for a in range(n):
            copy(a, 0, sibling, me).wait_recv()
        for j, chip in enumerate(chips):
            for a in range(n):
                copy(a, 4 + j, (*chip, 1 - c), me).wait_recv()
        for cp in first + passed:
            cp.wait_send()
        for lc in mine:
            lc.wait()

    return pl.pallas_call(
        body,
        in_specs=[pl.BlockSpec(memory_space=pl.ANY)] * n,
        out_specs=[pl.BlockSpec(memory_space=pl.ANY)] * n,
        out_shape=[jax.ShapeDtypeStruct((NDEV,) + a.shape, a.dtype) for a in arrs],
        scratch_shapes=[
            pltpu.SemaphoreType.DMA((n, NDEV - 1)),
            pltpu.SemaphoreType.DMA((n, NDEV - 1)),
            pltpu.SemaphoreType.DMA((n,)),
        ],
        compiler_params=pltpu.CompilerParams(has_side_effects=True),
        name=name,
    )(*arrs)


_HBM = pl.BlockSpec(memory_space=pltpu.HBM)
_SEM = pl.BlockSpec(memory_space=pltpu.SEMAPHORE)
_EFFECT = pltpu.SideEffectType.DATAFLOW_SIDE_EFFECTING


def _scatter_start(arrs, name):
    n = len(arrs)

    def body(*refs):
        src, land = refs[:n], refs[n:2 * n]
        send_sems, recv_sems = refs[2 * n:3 * n], refs[3 * n:4 * n]
        token = refs[6 * n]
        x, y, c = lax.axis_index("x"), lax.axis_index("y"), lax.axis_index("c")
        me = 4 * x + 2 * y + c
        for a in range(n):
            for r in range(1, NDEV):
                px = 1 - x if r & 4 else x
                py = 1 - y if r & 2 else y
                pc = 1 - c if r & 1 else c
                pltpu.make_async_remote_copy(
                    src_ref=src[a].at[4 * px + 2 * py + pc], dst_ref=land[a].at[me], send_sem=send_sems[a],
                    recv_sem=recv_sems[a], device_id=(px, py, pc), device_id_type=pl.DeviceIdType.MESH).start()
        token[...] = jnp.zeros_like(token)

    hbm = [pltpu.HBM(a.shape, a.dtype) for a in arrs]
    ops = [pltpu.with_memory_space_constraint(a, pltpu.HBM) for a in arrs]
    outs = pl.pallas_call(
        body,
        out_shape=tuple([pltpu.SemaphoreType.DMA(())] * (2 * n) + hbm + hbm + [jax.ShapeDtypeStruct((8, LANES), f32)]),
        in_specs=[_HBM] * (2 * n),
        out_specs=tuple([_SEM] * (2 * n) + [_HBM] * (2 * n) + [pl.BlockSpec(memory_space=pltpu.VMEM)]),
        input_output_aliases={i: 2 * n + i for i in range(2 * n)},
        compiler_params=pltpu.CompilerParams(has_side_effects=_EFFECT),
        name=name,
    )(*ops, *ops)
    return outs[:n], outs[n:2 * n], outs[2 * n:3 * n], outs[3 * n:4 * n], outs[4 * n]


def _scatter_wait(send_sems, recv_sems, srcs, lands, after, name):
    n = len(srcs)

    def body(*refs):
        land = refs[n:2 * n]
        ssem, rsem = refs[2 * n:3 * n], refs[3 * n:4 * n]
        x, y, c = lax.axis_index("x"), lax.axis_index("y"), lax.axis_index("c")
        for a in range(n):
            seven = land[a].at[pl.ds(0, NDEV - 1)]
            done = pltpu.make_async_remote_copy(
                src_ref=seven, dst_ref=seven, send_sem=ssem[a], recv_sem=rsem[a], device_id=(x, y, c),
                device_id_type=pl.DeviceIdType.MESH)
            done.wait_send()
            done.wait_recv()

    hbm = [pltpu.HBM(a.shape, a.dtype) for a in srcs]
    outs = pl.pallas_call(
        body,
        out_shape=tuple(hbm + hbm),
        in_specs=[_HBM] * (2 * n) + [_SEM] * (2 * n) + [pl.BlockSpec(memory_space=pl.ANY)],
        out_specs=tuple([_HBM] * (2 * n)),
        input_output_aliases={i: i for i in range(2 * n)},
        compiler_params=pltpu.CompilerParams(has_side_effects=_EFFECT),
        name=name,
    )(*srcs, *lands, *send_sems, *recv_sems, after)
    return outs[n:]


def _adamw(w, slots, m, v, name):
    r, c = w.shape
    tr = 128 if r % 128 == 0 else r

    def body(w_ref, s_ref, m_ref, v_ref, g_ref, d_ref, nm_ref, nv_ref):
        g = s_ref[0].astype(f32)
        for k in range(1, NDEV):
            g = g + s_ref[k].astype(f32)
        mm = ADAM_B1 * m_ref[...] + (1.0 - ADAM_B1) * g
        vv = ADAM_B2 * v_ref[...] + (1.0 - ADAM_B2) * (g * g)
        m_hat = mm / (1.0 - ADAM_B1 ** ADAM_STEP)
        v_hat = vv / (1.0 - ADAM_B2 ** ADAM_STEP)
        g_ref[...] = g
        d_ref[...] = -ADAM_LR * (m_hat / (jnp.sqrt(v_hat) + ADAM_EPS) + ADAM_WD * w_ref[...])
        nm_ref[...] = mm
        nv_ref[...] = vv

    blk = pl.BlockSpec((tr, c), lambda i: (i, 0))
    return pl.pallas_call(
        body,
        grid=(r // tr,),
        in_specs=[blk, pl.BlockSpec((NDEV, tr, c), lambda i: (0, i, 0)), blk, blk],
        out_specs=[blk] * 4,
        out_shape=[jax.ShapeDtypeStruct((r, c), f32)] * 4,
        compiler_params=_params(("arbitrary",)),
        name=name,
    )(w, slots, m, v)


def _local_step(x, tgt, norm_gain, w_in, qn_a, kn_a, qn_b, kn_b, sink_a, rel_bias, w_a, w_b, b_merge, w_o,
                on_weight_grads=None):
    two = lambda t: jnp.concatenate([t, t], axis=-1).reshape(1, LANES)
    ones = jnp.ones((1, LANES), f32)
    gains = jnp.stack([
        jnp.stack([two(qn_a), two(kn_a), ones]),
        jnp.stack([two(qn_b), two(kn_b), ones]),
        jnp.stack([two(qn_b), two(kn_b), ones]),
        jnp.stack([two(qn_b), two(kn_b), ones]),
    ])
    buckets = [jnp.asarray(_bucket_np(blk, d)) for blk, d, _ in GROUPS]
    bias_a = _bias_expand(rel_bias, buckets[0], 0, "bias_expand_a")[None]
    bias_b = jnp.stack([_bias_expand(rel_bias, buckets[k], GROUPS[k][2], "bias_expand_b%d" % k) for k in (1, 2, 3)])

    proj_a, proj_t, hb, rstd = _rms_inproj(x, norm_gain, w_in)
    gl = _prep(proj_a, gains)
    o_a, l_a = _attn_fwd(gl, bias_a, sink_a.reshape(8), 0, 1, 128, True, "attn_fwd_a")
    o_bg, l_bg = _attn_fwd(gl, bias_b, None, 1, 3, 64, False, "attn_fwd_b")
    o_b, l_b = _b_to_natural(o_bg, l_bg)
    sink_b = jnp.repeat(sink_a.reshape(8), HD).reshape(1, 512)

    (dy, dyb, dproj_t, do_a, dd_a, do_b, dd_b, ya, yb, mg, dbr_a, dbr_b, loss, dbm, dsk) = _tail(
        x, tgt, o_a, l_a, o_b, l_b, proj_t, b_merge, w_a, w_b, w_o, sink_b)

    dqkv_a, dbk_a = _attn_bwd(gl, bias_a, buckets[0][None], do_a, l_a, dd_a, 0, 1, 128, True, "attn_bwd_a")
    do_bg, dd_bg = _b_from_natural(do_b, dd_b)
    dqkv_b, dbk_b = _attn_bwd(gl, bias_b, jnp.stack(buckets[1:]), do_bg, l_bg, dd_bg, 1, 3, 64, False, "attn_bwd_b")

    dproj_a, dg_a = _post_a(dqkv_a, proj_a, gains)
    dproj_a, dg_b = _post_b(dqkv_b, proj_a, gains, dproj_a)

    dw_in = jnp.concatenate([_matmul_tn(hb, dproj_a, "dw_in_a"), _matmul_tn(hb, dproj_t, "dw_in_t")], axis=1)
    dw_o = _matmul_tn(mg, dyb, "dw_out")
    dw_a = _matmul_tn(ya, dbr_a, "dw_branch_a")
    dw_b = _matmul_tn(yb, dbr_b, "dw_branch_b")
    token = jnp.zeros((), f32) if on_weight_grads is None else on_weight_grads(
        dict(w_in=dw_in, w_branch_a=dw_a, w_branch_b=dw_b, b_merge=dbm, w_out=dw_o))
    grad_x, d_norm_gain = _dh_norm_bwd(dproj_a, dproj_t, w_in, x, rstd, norm_gain + token, dy)

    fold = lambda t: t[..., :HD] + t[..., HD:]
    d_qn_a = fold(dg_a[0, 0])
    d_kn_a = fold(dg_a[1, 0])
    d_qn_b = fold(dg_b[:, 0, 0].sum(axis=0))
    d_kn_b = fold(dg_b[:, 1, 0].sum(axis=0))
    d_sink = dsk.reshape(8, HD)[:, 0]
    red = jnp.concatenate([dbk_a, dbk_b])
    d_rel = red[:, :, 0, :32].reshape(32, 32).T
    return dict(loss=loss, grad_x=grad_x, norm_gain=d_norm_gain, w_in=dw_in, q_norm_a=d_qn_a, k_norm_a=d_kn_a,
                q_norm_b=d_qn_b, k_norm_b=d_kn_b, sink_a=d_sink, rel_bias=d_rel, w_branch_a=dw_a, w_branch_b=dw_b,
                b_merge=dbm, w_out=dw_o)


SMALL = (("norm_gain", D), ("q_norm_a", HD), ("k_norm_a", HD), ("q_norm_b", HD), ("k_norm_b", HD), ("sink_a", 8),
         ("rel_bias", 1024))
SMALL_PAD = 2432


SMALL_USED = sum(sz for _, sz in SMALL)


def _pack_small(parts, loss=None):
    tail = jnp.zeros((SMALL_PAD - SMALL_USED,), f32)
    if loss is not None:
        tail = tail.at[0].set(loss.reshape(()))
    return jnp.concatenate([parts[n].reshape(-1) for n, _ in SMALL] + [tail]).reshape(1, SMALL_PAD)


def _unpack_small(flat, shapes):
    out, off = {}, 0
    for n, sz in SMALL:
        out[n] = flat[0, off:off + sz].reshape(shapes[n])
        off += sz
    return out


def kernel(x, norm_gain, w_in, q_norm_a, k_norm_a, q_norm_b, k_norm_b, sink_a, rel_bias, w_branch_a, w_branch_b, b_merge, w_out, loss_target, m_norm_gain, m_w_in, m_q_norm_a, m_k_norm_a, m_q_norm_b, m_k_norm_b, m_sink_a, m_rel_bias, m_w_branch_a, m_w_branch_b, m_b_merge, m_w_out, v_norm_gain, v_w_in, v_q_norm_a, v_k_norm_a, v_q_norm_b, v_k_norm_b, v_sink_a, v_rel_bias, v_w_branch_a, v_w_branch_b, v_b_merge, v_w_out):
    wsh = NW // NDEV
    csh = D // NDEV
    g_in, g_a, g_b, g_o, g_bm = _gather_two_level(
        [w_in[0].astype(bf16), w_branch_a[0].astype(bf16), w_branch_b[0].astype(bf16), w_out[0].astype(bf16),
         b_merge[0]], "gather_weights")
    w_in_full = g_in.transpose(1, 0, 2).reshape(D, NW)
    w_a_full = g_a.transpose(1, 0, 2).reshape(512, D)
    w_b_full = g_b.transpose(1, 0, 2).reshape(512, D)
    w_o_full = g_o.reshape(D, D)
    bm_full = g_bm.transpose(1, 0, 2).reshape(2, D)

    pending = {}

    def start_exchange(gw):
        blocks = [gw["w_in"].reshape(D, NDEV, wsh).transpose(1, 0, 2).astype(bf16),
                  gw["w_branch_a"].reshape(512, NDEV, csh).transpose(1, 0, 2).astype(bf16),
                  gw["w_branch_b"].reshape(512, NDEV, csh).transpose(1, 0, 2).astype(bf16),
                  gw["w_out"].reshape(NDEV, csh, D).astype(bf16),
                  gw["b_merge"].reshape(2, NDEV, csh).transpose(1, 0, 2)]
        pending["started"] = _scatter_start(blocks, "scatter_grads_start")
        return pending["started"][4][0, 0]

    loc = _local_step(x[0], loss_target[0], norm_gain, w_in_full, q_norm_a, k_norm_a, q_norm_b, k_norm_b, sink_a,
                      rel_bias, w_a_full, w_b_full, bm_full, w_o_full, on_weight_grads=start_exchange)

    small_shapes = dict(norm_gain=(1, D), q_norm_a=(1, HD), k_norm_a=(1, HD), q_norm_b=(1, HD), k_norm_b=(1, HD),
                        sink_a=(1, 8), rel_bias=(32, 32))
    (r_small,) = _exchange([], [_pack_small(loc, loc["loss"])], "gather_small_grads")
    send_sems, recv_sems, srcs, lands, _ = pending["started"]
    r_in, r_a, r_b, r_o, r_bm = _scatter_wait(send_sems, recv_sems, srcs, lands, r_small, "scatter_grads_wait")

    given = dict(norm_gain=norm_gain, q_norm_a=q_norm_a, k_norm_a=k_norm_a, q_norm_b=q_norm_b, k_norm_b=k_norm_b,
                 sink_a=sink_a, rel_bias=rel_bias)
    m_small = dict(norm_gain=m_norm_gain, q_norm_a=m_q_norm_a, k_norm_a=m_k_norm_a, q_norm_b=m_q_norm_b,
                   k_norm_b=m_k_norm_b, sink_a=m_sink_a, rel_bias=m_rel_bias)
    v_small = dict(norm_gain=v_norm_gain, q_norm_a=v_q_norm_a, k_norm_a=v_k_norm_a, q_norm_b=v_q_norm_b,
                   k_norm_b=v_k_norm_b, sink_a=v_sink_a, rel_bias=v_rel_bias)
    res = {
        "small": _adamw(_pack_small(given), r_small, _pack_small(m_small), _pack_small(v_small), "adamw_small"),
        "w_in": _adamw(w_in[0], r_in, m_w_in[0], v_w_in[0], "adamw_w_in"),
        "w_branch_a": _adamw(w_branch_a[0], r_a, m_w_branch_a[0], v_w_branch_a[0], "adamw_w_branch_a"),
        "w_branch_b": _adamw(w_branch_b[0], r_b, m_w_branch_b[0], v_w_branch_b[0], "adamw_w_branch_b"),
        "b_merge": _adamw(b_merge[0], r_bm, m_b_merge[0], v_b_merge[0], "adamw_b_merge"),
        "w_out": _adamw(w_out[0], r_o, m_w_out[0], v_w_out[0], "adamw_w_out"),
    }
    order = ["norm_gain", "w_in", "q_norm_a", "k_norm_a", "q_norm_b", "k_norm_b", "sink_a", "rel_bias", "w_branch_a",
             "w_branch_b", "b_merge", "w_out"]
    outs = []
    for k in range(4):
        small = _unpack_small(res["small"][k], small_shapes)
        for n in order:
            outs.append(small[n] if n in small else res[n][k][None])
    loss = res["small"][0][0, SMALL_USED]
    return (loss, loc["grad_x"][None], *outs)
```

```python
import math

import numpy as np
import jax
import jax.numpy as jnp
from jax import lax
from jax.experimental import pallas as pl
from jax.experimental.pallas import tpu as pltpu

f32 = jnp.float32
bf16 = jnp.bfloat16

S = 4096
D = 1024
NA = 5376
NT = 3072
NW = NA + NT
WSH = NW // 8
HD = 64
LANES = 128
EPS = 1e-6
NEG = -1e30
SCALE = HD ** -0.5
TQ = 128
PAD = 128
SP = S + 2 * PAD
NDEV = 8
GROUPS = ((128, 1, 0), (64, 1, 8), (64, 4, 16), (64, 16, 24))
CHUNK = 256
PCHUNK = 128
RC = 64
TN = 768

ADAM_LR, ADAM_B1, ADAM_B2, ADAM_EPS, ADAM_WD, ADAM_STEP = 0.001, 0.9, 0.999, 1e-08, 0.01, 10

MIB = 1024 * 1024
NT_DIMS = (((1,), (1,)), ((), ()))
TN_DIMS = (((0,), (0,)), ((), ()))


def _params(sem=None, vmem_mib=48):
    return pltpu.CompilerParams(dimension_semantics=sem, vmem_limit_bytes=vmem_mib * MIB)


def _lo():
    return lax.broadcasted_iota(jnp.int32, (1, LANES), 1) < HD


def _head_ones():
    r = lax.broadcasted_iota(jnp.int32, (LANES, LANES), 0) // HD
    c = lax.broadcasted_iota(jnp.int32, (LANES, LANES), 1) // HD
    return jnp.where(r == c, 1.0, 0.0).astype(bf16)


def _half_sums(x, ones):
    hi = x.astype(bf16)
    mid = (x - hi.astype(f32)).astype(bf16)
    return (jnp.dot(hi, ones, preferred_element_type=f32) + jnp.dot(mid, ones, preferred_element_type=f32))


def _seg_sum(x, ones):
    outs = [_half_sums(x[:, b * LANES:(b + 1) * LANES], ones) for b in range(x.shape[1] // LANES)]
    return outs[0] if len(outs) == 1 else jnp.concatenate(outs, axis=1)


def _bucket_np(blk, stride):
    w = TQ + 2 * blk
    rel = np.arange(w)[None, :] - blk - np.arange(TQ)[:, None]
    band = np.abs(rel) <= blk
    r = rel * stride
    n = np.abs(r)
    nf = np.maximum(n, 8).astype(np.float32)
    large = 8 + (np.log(nf / np.float32(8)) / np.float32(math.log(128.0)) * np.float32(8)).astype(np.int32)
    large = np.minimum(large, 15)
    b = (r > 0).astype(np.int32) * 16 + np.where(n < 8, n, large)
    return np.where(band, b, -1).astype(np.int32)


def _rms(x, gain):
    ts = 512

    def body(x_ref, g_ref, h_ref, ht_ref, r_ref):
        xv = x_ref[...]
        r = lax.rsqrt(jnp.mean(xv * xv, axis=-1, keepdims=True) + EPS)
        h = (xv * r) * g_ref[...]
        h_ref[...] = h.astype(bf16)
        ht_ref[...] = h.T.astype(bf16)
        r_ref[...] = r

    return pl.pallas_call(
        body,
        grid=(S // ts,),
        in_specs=[pl.BlockSpec((ts, D), lambda i: (i, 0)), pl.BlockSpec((1, D), lambda i: (0, 0))],
        out_specs=[pl.BlockSpec((ts, D), lambda i: (i, 0)), pl.BlockSpec((D, ts), lambda i: (0, i)),
                   pl.BlockSpec((ts, 1), lambda i: (i, 0))],
        out_shape=[jax.ShapeDtypeStruct((S, D), bf16), jax.ShapeDtypeStruct((D, S), bf16),
                   jax.ShapeDtypeStruct((S, 1), f32)],
        compiler_params=_params(("arbitrary",)),
        name="rms",
    )(x, gain)


def _inproj(hb, w_shards):
    ts = 1024

    def body(h_ref, wa_ref, wb_ref, p_ref, wf_ref, w_scr):
        n = pl.program_id(0)

        @pl.when(pl.program_id(1) == 0)
        def _():
            for nn in range(NW // TN):
                j0 = (TN * nn) // WSH
                a = TN * nn - WSH * j0
                len1 = min(TN, WSH - a)

                @pl.when(n == nn)
                def _():
                    w_scr[:, 0:len1] = wa_ref[:, a:a + len1]
                    if len1 < TN:
                        w_scr[:, len1:TN] = wb_ref[:, 0:TN - len1]

            wf_ref[...] = w_scr[...]

        p_ref[...] = jnp.dot(h_ref[...], w_scr[...], preferred_element_type=f32)

    return pl.pallas_call(
        body,
        grid=(NW // TN, S // ts),
        in_specs=[
            pl.BlockSpec((ts, D), lambda n, i: (i, 0)),
            pl.BlockSpec((None, D, WSH), lambda n, i: ((TN * n) // WSH, 0, 0)),
            pl.BlockSpec((None, D, WSH), lambda n, i: (jnp.minimum((TN * n) // WSH + 1, NDEV - 1), 0, 0)),
        ],
        out_specs=[pl.BlockSpec((ts, TN), lambda n, i: (i, n)), pl.BlockSpec((D, TN), lambda n, i: (0, n))],
        out_shape=[jax.ShapeDtypeStruct((S, NW), f32), jax.ShapeDtypeStruct((D, NW), bf16)],
        scratch_shapes=[pltpu.VMEM((D, TN), bf16)],
        compiler_params=_params(("arbitrary", "arbitrary")),
        name="inproj",
    )(hb, w_shards, w_shards)


def _bias_expand(table, bucket, c0, name):
    tq, w = bucket.shape

    def body(tab_ref, bk_ref, o_ref):
        h = pl.program_id(0)
        bk = bk_ref[...]

        def step(b, acc):
            return jnp.where(bk == b, tab_ref[b, c0 + h], acc)

        o_ref[...] = lax.fori_loop(0, 32, step, jnp.full((tq, w), NEG, f32))

    return pl.pallas_call(
        body,
        grid=(8,),
        in_specs=[pl.BlockSpec(memory_space=pltpu.SMEM), pl.BlockSpec((tq, w), lambda h: (0, 0))],
        out_specs=pl.BlockSpec((None, tq, w), lambda h: (h, 0, 0)),
        out_shape=jax.ShapeDtypeStruct((8, tq, w), f32),
        compiler_params=_params(("arbitrary",)),
        name=name,
    )(table, bucket)


def _col_block(g, j):
    kind = j // 4
    hp = j % 4
    a = jnp.where(kind == 0, hp, 3 + kind)
    b = 6 + 12 * kind + 4 * (g - 1) + hp
    return jnp.where(g == 0, a, b)


def _prep(proj_a, gains):
    def body(p_ref, g_ref, o_ref):
        g = pl.program_id(0)
        j = pl.program_id(1)
        kind = j // 4
        lo = _lo()
        ones = _head_ones()
        half = jnp.where(lo, 0, 1)
        take = (kind == 0) | (half == (j % 4) // 2)
        gain = g_ref[...]
        o_ref[0:PAD, :] = jnp.zeros((PAD, LANES), bf16)
        o_ref[PAD + S:SP, :] = jnp.zeros((PAD, LANES), bf16)

        def norm_store(xv, dst, dup):
            if dup:
                xv = jnp.where(take, xv, pltpu.roll(xv, HD, 1))
            r = lax.rsqrt(_half_sums(xv * xv, ones) * (1.0 / HD) + EPS)
            r = jnp.where(kind == 2, 1.0, r)
            yv = (xv * r) * gain
            yv = jnp.where(kind == 0, yv * SCALE, yv)
            o_ref[PAD + dst:PAD + dst + CHUNK, :] = yv.astype(bf16)

        for gi, (_, d, _) in enumerate(GROUPS):
            @pl.when(g == gi)
            def _():
                seq = S // d
                for c in range(d):
                    for i in range(seq // CHUNK):
                        if d == 1:
                            xv = p_ref[i * CHUNK:(i + 1) * CHUNK, :]
                        else:
                            xv = p_ref[pl.ds(c + i * CHUNK * d, CHUNK, stride=d), :]
                        norm_store(xv, c * seq + i * CHUNK, gi == 0)

    return pl.pallas_call(
        body,
        grid=(4, 12),
        in_specs=[
            pl.BlockSpec((S, LANES), lambda g, j: (0, _col_block(g, j))),
            pl.BlockSpec((None, None, 1, LANES), lambda g, j: (g, j // 4, 0, 0)),
        ],
        out_specs=pl.BlockSpec((None, None, SP, LANES), lambda g, j: (g, j, 0, 0)),
        out_shape=jax.ShapeDtypeStruct((4, 12, SP, LANES), bf16),
        compiler_params=_params(("arbitrary", "arbitrary")),
        name="prep",
    )(proj_a, gains)


def _seq_len(g, ng):
    return S if ng == 1 else jnp.right_shift(S, 2 * g)


def _stack_heads(t, lo):
    z = jnp.zeros_like(t)
    return jnp.concatenate([jnp.where(lo, t, z), jnp.where(lo, z, t)], axis=0)


def _unstack_heads(t2, lo):
    return jnp.where(lo, t2[:TQ], t2[TQ:])


def _row_spec(natural, ng):
    if natural:
        return pl.BlockSpec((S, LANES), lambda g, hp: (0, hp)), (S, 4 * LANES)
    return pl.BlockSpec((None, None, S, LANES), lambda g, hp: (g, hp, 0, 0)), (ng, 4, S, LANES)


def _attn_fwd(gl, bias, sink, g0, ng, blk, natural, name):
    w = TQ + 2 * blk
    use_sink = sink is not None

    def body(*refs):
        if use_sink:
            sink_ref, q_ref, k_ref, v_ref, b_ref, o_ref, l_ref, s0, s1, p0, p1, lse_scr = refs
        else:
            q_ref, k_ref, v_ref, b_ref, o_ref, l_ref, s0, s1, p0, p1, lse_scr = refs
        g = pl.program_id(0)
        hp = pl.program_id(1)
        lo = _lo()
        seq = _seq_len(g, ng)
        mi = lax.broadcasted_iota(jnp.int32, (1, w), 1)
        s_bufs, p_bufs = (s0, s1), (p0, p1)

        def scores(t, slot):
            f0 = pl.multiple_of(t * TQ, TQ)
            q2 = _stack_heads(q_ref[pl.ds(PAD + f0, TQ), :], lo)
            kw = k_ref[pl.ds(PAD - blk + f0, w), :]
            s_bufs[slot][...] = lax.dot_general(q2, kw, NT_DIMS, preferred_element_type=f32)

        def softmax(t, slot):
            f0 = pl.multiple_of(t * TQ, TQ)
            m0 = jnp.bitwise_and(f0, seq - 1)
            inside = (mi >= blk - m0) & (mi < seq + blk - m0)
            for h in range(2):
                for r in range(TQ // RC):
                    rows = slice(h * TQ + r * RC, h * TQ + (r + 1) * RC)
                    logit = jnp.where(inside, s_bufs[slot][rows, :] + b_ref[h, r * RC:(r + 1) * RC, :], NEG)
                    m = jnp.max(logit, axis=1, keepdims=True)
                    e = jnp.exp(logit - m)
                    lse = m + jnp.log(jnp.sum(e, axis=1, keepdims=True))
                    if use_sink:
                        sk = sink_ref[2 * hp + h]
                        mx = jnp.maximum(lse, sk)
                        lse = mx + jnp.log(jnp.exp(lse - mx) + jnp.exp(sk - mx))
                    p_bufs[slot][rows, :] = (e * jnp.exp(m - lse)).astype(bf16)
                    lse_scr[rows, :] = jnp.broadcast_to(lse, (RC, LANES))
            l_ref[pl.ds(f0, TQ), :] = jnp.where(lo, lse_scr[0:TQ, :], lse_scr[TQ:2 * TQ, :])

        def values(t, slot):
            f0 = pl.multiple_of(t * TQ, TQ)
            vw = v_ref[pl.ds(PAD - blk + f0, w), :]
            o2 = jnp.dot(p_bufs[slot][...], vw, preferred_element_type=f32)
            o_ref[pl.ds(f0, TQ), :] = _unstack_heads(o2, lo)

        nt = S // TQ
        scores(0, 0)
        scores(1, 1)
        softmax(0, 0)

        def pair(k, carry):
            t = 2 * k + 2
            scores(t, 0)
            softmax(t - 1, 1)
            values(t - 2, 0)
            scores(t + 1, 1)
            softmax(t, 0)
            values(t - 1, 1)
            return carry

        lax.fori_loop(0, (nt - 2) // 2, pair, 0)
        softmax(nt - 1, 1)
        values(nt - 2, 0)
        values(nt - 1, 1)

    in_specs = [
        pl.BlockSpec((None, None, SP, LANES), lambda g, hp: (g0 + g, hp, 0, 0)),
        pl.BlockSpec((None, None, SP, LANES), lambda g, hp: (g0 + g, 4 + hp, 0, 0)),
        pl.BlockSpec((None, None, SP, LANES), lambda g, hp: (g0 + g, 8 + hp, 0, 0)),
        pl.BlockSpec((None, 2, TQ, w), lambda g, hp: (g, hp, 0, 0)),
    ]
    args = [gl, gl, gl, bias]
    if use_sink:
        in_specs = [pl.BlockSpec(memory_space=pltpu.SMEM)] + in_specs
        args = [sink] + args
    out, shape = _row_spec(natural, ng)
    return pl.pallas_call(
        body,
        grid=(ng, 4),
        in_specs=in_specs,
        out_specs=[out, out],
        out_shape=[jax.ShapeDtypeStruct(shape, f32)] * 2,
        scratch_shapes=[pltpu.VMEM((2 * TQ, w), f32), pltpu.VMEM((2 * TQ, w), f32),
                        pltpu.VMEM((2 * TQ, w), bf16), pltpu.VMEM((2 * TQ, w), bf16),
                        pltpu.VMEM((2 * TQ, LANES), f32)],
        compiler_params=_params(("arbitrary", "arbitrary")),
        name=name,
    )(*args)


def _attn_bwd(gl, bias, bucket, do, lse, dd, g0, ng, blk, natural, name):
    w = TQ + 2 * blk

    def body(q_ref, k_ref, v_ref, b_ref, bk_ref, do_ref, l_ref, d_ref, dqkv_ref, dbk_ref,
             db_acc, s0, s1, dp0, dp1, pb0, pb1, ds0, ds1):
        g = pl.program_id(0)
        lo = _lo()
        hi = jnp.logical_not(lo)
        seq = _seq_len(g, ng)
        mi = lax.broadcasted_iota(jnp.int32, (1, w), 1)
        dqkv_ref[1] = jnp.zeros((SP, LANES), f32)
        dqkv_ref[2] = jnp.zeros((SP, LANES), f32)
        db_acc[...] = jnp.zeros((2 * TQ, w), f32)
        s_bufs, dp_bufs, pb_bufs, ds_bufs = (s0, s1), (dp0, dp1), (pb0, pb1), (ds0, ds1)

        def stacked(t):
            f0 = pl.multiple_of(t * TQ, TQ)
            q2 = _stack_heads(q_ref[pl.ds(PAD + f0, TQ), :], lo)
            do2 = _stack_heads(do_ref[pl.ds(f0, TQ), :], lo)
            return f0, q2, do2

        def scores(t, slot):
            f0, q2, do2 = stacked(t)
            win = pl.ds(PAD - blk + f0, w)
            s_bufs[slot][...] = lax.dot_general(q2, k_ref[win, :], NT_DIMS, preferred_element_type=f32)
            dp_bufs[slot][...] = lax.dot_general(do2, v_ref[win, :], NT_DIMS, preferred_element_type=f32)

        def grads(t, slot):
            f0 = pl.multiple_of(t * TQ, TQ)
            m0 = jnp.bitwise_and(f0, seq - 1)
            inside = (mi >= blk - m0) & (mi < seq + blk - m0)
            for h in range(2):
                msk = lo if h == 0 else hi
                for r in range(TQ // RC):
                    rows = slice(h * TQ + r * RC, h * TQ + (r + 1) * RC)
                    src = pl.ds(f0 + r * RC, RC)
                    lh = jnp.max(jnp.where(msk, l_ref[src, :], -jnp.inf), axis=1, keepdims=True)
                    dh = jnp.max(jnp.where(msk, d_ref[src, :], -jnp.inf), axis=1, keepdims=True)
                    logit = jnp.where(inside, s_bufs[slot][rows, :] + b_ref[h, r * RC:(r + 1) * RC, :], NEG)
                    p = jnp.exp(logit - lh)
                    ds = p * (dp_bufs[slot][rows, :] - dh)
                    db_acc[rows, :] += ds
                    pb_bufs[slot][rows, :] = p.astype(bf16)
                    ds_bufs[slot][rows, :] = ds.astype(bf16)

        def accumulate(t, slot):
            f0, q2, do2 = stacked(t)
            win = pl.ds(PAD - blk + f0, w)
            dsb = ds_bufs[slot][...]
            dq2 = jnp.dot(dsb, k_ref[win, :], preferred_element_type=f32)
            dqkv_ref[0, pl.ds(PAD + f0, TQ), :] = _unstack_heads(dq2, lo)
            dqkv_ref[1, win, :] += lax.dot_general(dsb, q2, TN_DIMS, preferred_element_type=f32)
            dqkv_ref[2, win, :] += lax.dot_general(pb_bufs[slot][...], do2, TN_DIMS, preferred_element_type=f32)

        nt = S // TQ
        scores(0, 0)
        scores(1, 1)
        grads(0, 0)

        def pair(k, carry):
            t = 2 * k + 2
            scores(t, 0)
            grads(t - 1, 1)
            accumulate(t - 2, 0)
            scores(t + 1, 1)
            grads(t, 0)
            accumulate(t - 1, 1)
            return carry

        lax.fori_loop(0, (nt - 2) // 2, pair, 0)
        grads(nt - 1, 1)
        accumulate(nt - 2, 0)
        accumulate(nt - 1, 1)

        bk = bk_ref[...]
        lane = lax.broadcasted_iota(jnp.int32, (8, LANES), 1)
        for h in range(2):
            db = db_acc[h * TQ:(h + 1) * TQ, :]
            acc = jnp.zeros((8, LANES), f32)
            for b in range(32):
                part = jnp.where(bk == b, db, 0.0).reshape(TQ // 8, 8, w).sum(axis=0)
                tot = jnp.sum(jnp.sum(part, axis=1, keepdims=True), axis=0, keepdims=True)
                acc = jnp.where(lane == b, tot, acc)
            dbk_ref[h] = acc

    def gcol(off):
        return pl.BlockSpec((None, None, SP, LANES), lambda g, hp: (g0 + g, off + hp, 0, 0))

    row, _ = _row_spec(natural, ng)
    return pl.pallas_call(
        body,
        grid=(ng, 4),
        in_specs=[gcol(0), gcol(4), gcol(8), pl.BlockSpec((None, 2, TQ, w), lambda g, hp: (g, hp, 0, 0)),
                  pl.BlockSpec((None, TQ, w), lambda g, hp: (g, 0, 0)), row, row, row],
        out_specs=[pl.BlockSpec((None, 3, None, SP, LANES), lambda g, hp: (g, 0, hp, 0, 0)),
                   pl.BlockSpec((None, 2, 8, LANES), lambda g, hp: (g, hp, 0, 0))],
        out_shape=[
            jax.ShapeDtypeStruct((ng, 3, 4, SP, LANES), f32),
            jax.ShapeDtypeStruct((ng, 8, 8, LANES), f32),
        ],
        scratch_shapes=[pltpu.VMEM((2 * TQ, w), f32)] * 5 + [pltpu.VMEM((2 * TQ, w), bf16)] * 4,
        compiler_params=_params(("arbitrary", "arbitrary"), vmem_mib=56),
        name=name,
    )(gl, gl, gl, bias, bucket, do, lse, dd)


def _b_to_natural(o_gl, l_gl):
    def body(o_ref, l_ref, on_ref, ln_ref):
        g = pl.program_id(0)
        for gi in range(3):
            d = GROUPS[gi + 1][1]

            @pl.when(g == gi)
            def _():
                seq = S // d
                for c in range(d):
                    for i in range(seq // CHUNK):
                        src = slice(c * seq + i * CHUNK, c * seq + (i + 1) * CHUNK)
                        if d == 1:
                            on_ref[src, :] = o_ref[src, :]
                            ln_ref[src, :] = l_ref[src, :]
                        else:
                            dst = pl.ds(c + i * CHUNK * d, CHUNK, stride=d)
                            on_ref[dst, :] = o_ref[src, :]
                            ln_ref[dst, :] = l_ref[src, :]

    col = pl.BlockSpec((None, None, S, LANES), lambda g, hp: (g, hp, 0, 0))
    nat = pl.BlockSpec((S, LANES), lambda g, hp: (0, 4 * g + hp))
    return pl.pallas_call(
        body,
        grid=(3, 4),
        in_specs=[col, col],
        out_specs=[nat, nat],
        out_shape=[jax.ShapeDtypeStruct((S, 3 * 512), f32)] * 2,
        compiler_params=_params(("arbitrary", "arbitrary")),
        name="b_to_natural",
    )(o_gl, l_gl)


def _b_from_natural(do_n, dd_n):
    def body(do_ref, dd_ref, dog_ref, ddg_ref):
        g = pl.program_id(0)
        for gi in range(3):
            d = GROUPS[gi + 1][1]

            @pl.when(g == gi)
            def _():
                seq = S // d
                for c in range(d):
                    for i in range(seq // CHUNK):
                        dst = slice(c * seq + i * CHUNK, c * seq + (i + 1) * CHUNK)
                        if d == 1:
                            a, b = do_ref[dst, :], dd_ref[dst, :]
                        else:
                            src = pl.ds(c + i * CHUNK * d, CHUNK, stride=d)
                            a, b = do_ref[src, :], dd_ref[src, :]
                        dog_ref[dst, :] = a.astype(bf16)
                        ddg_ref[dst, :] = b

    col = pl.BlockSpec((None, None, S, LANES), lambda g, hp: (g, hp, 0, 0))
    nat = pl.BlockSpec((S, LANES), lambda g, hp: (0, 4 * g + hp))
    return pl.pallas_call(
        body,
        grid=(3, 4),
        in_specs=[nat, nat],
        out_specs=[col, col],
        out_shape=[jax.ShapeDtypeStruct((3, 4, S, LANES), bf16), jax.ShapeDtypeStruct((3, 4, S, LANES), f32)],
        compiler_params=_params(("arbitrary", "arbitrary")),
        name="b_from_natural",
    )(do_n, dd_n)


def _sigmoid(z):
    return 1.0 / (1.0 + jnp.exp(-z))


def _tail(x, tgt, o_a, l_a, o_b, l_b, proj, bm, w_a, w_b, w_o, sink_b):
    ts = 128

    def body(x_ref, t_ref, oa_ref, la_ref, ob_ref, lb_ref, ga_ref, gb_ref, m0_ref, m1_ref, bm_ref,
             wa_ref, wb_ref, wo_ref, sk_ref,
             dy_ref, dyb_ref, dt_ref, doa_ref, dda_ref, dob_ref, ddb_ref, ya_ref, yb_ref, mg_ref, dbra_ref, dbrb_ref,
             loss_ref, dbm_ref, dsk_ref):
        i = pl.program_id(0)

        @pl.when(i == 0)
        def _():
            loss_ref[...] = jnp.zeros_like(loss_ref)
            dbm_ref[...] = jnp.zeros_like(dbm_ref)
            dsk_ref[...] = jnp.zeros_like(dsk_ref)

        ga = ga_ref[...]
        sa = _sigmoid(ga)
        silu_a = ga * sa
        oa = oa_ref[...]
        ya = oa * silu_a
        gb = gb_ref[...]
        sb = _sigmoid(gb)
        silu_b = gb * sb
        ob = [ob_ref[:, k * 512:(k + 1) * 512] for k in range(3)]
        lb = [lb_ref[:, k * 512:(k + 1) * 512] for k in range(3)]
        mx = jnp.maximum(jnp.maximum(lb[0], lb[1]), lb[2])
        ex = [jnp.exp(v - mx) for v in lb]
        den = ex[0] + ex[1] + ex[2]
        alpha = [e / den for e in ex]
        ybc = alpha[0] * ob[0] + alpha[1] * ob[1] + alpha[2] * ob[2]
        yb = ybc * silu_b
        yab = ya.astype(bf16)
        ybb = yb.astype(bf16)
        br_a = jnp.dot(yab, wa_ref[...], preferred_element_type=f32)
        br_b = jnp.dot(ybb, wb_ref[...], preferred_element_type=f32)
        g0 = _sigmoid(m0_ref[...] + bm_ref[0:1, :])
        g1 = _sigmoid(m1_ref[...] + bm_ref[1:2, :])
        merged = g0 * br_a + g1 * br_b
        mgb = merged.astype(bf16)
        y = x_ref[...] + jnp.dot(mgb, wo_ref[...], preferred_element_type=f32)
        err = y - t_ref[...]
        part = jnp.sum(jnp.sum(err * err, axis=1, keepdims=True), axis=0, keepdims=True)
        loss_ref[...] += part * (0.5 / D)
        dy = err * (1.0 / D)
        dyb = dy.astype(bf16)
        dmerged = lax.dot_general(dyb, wo_ref[...], NT_DIMS, preferred_element_type=f32)
        dbr_a = (dmerged * g0).astype(bf16)
        dbr_b = (dmerged * g1).astype(bf16)
        dm0 = dmerged * br_a * (g0 * (1.0 - g0))
        dm1 = dmerged * br_b * (g1 * (1.0 - g1))
        dbm_ref[0:1, :] += jnp.sum(dm0, axis=0, keepdims=True)
        dbm_ref[1:2, :] += jnp.sum(dm1, axis=0, keepdims=True)
        dya = lax.dot_general(dbr_a, wa_ref[...], NT_DIMS, preferred_element_type=f32)
        dyb2 = lax.dot_general(dbr_b, wb_ref[...], NT_DIMS, preferred_element_type=f32)
        do_a = dya * silu_a
        dga = dya * oa * (sa * (1.0 + ga * (1.0 - sa)))
        ones = _head_ones()
        delta_a = _seg_sum(do_a * oa, ones)
        dsk_ref[...] -= jnp.sum(delta_a * jnp.exp(sk_ref[...] - la_ref[...]), axis=0, keepdims=True)
        dybc = dyb2 * silu_b
        dgb = dyb2 * ybc * (sb * (1.0 + gb * (1.0 - sb)))
        dbar = _seg_sum(dybc * ybc, ones)
        dy_ref[...] = dy
        dyb_ref[...] = dyb
        dt_ref[:, 0:512] = dga.astype(bf16)
        dt_ref[:, 512:1024] = dgb.astype(bf16)
        dt_ref[:, 1024:2048] = dm0.astype(bf16)
        dt_ref[:, 2048:3072] = dm1.astype(bf16)
        doa_ref[...] = do_a.astype(bf16)
        dda_ref[...] = delta_a
        for k in range(3):
            dob_ref[:, k * 512:(k + 1) * 512] = alpha[k] * dybc
            ddb_ref[:, k * 512:(k + 1) * 512] = alpha[k] * dbar
        ya_ref[...] = yab
        yb_ref[...] = ybb
        mg_ref[...] = mgb
        dbra_ref[...] = dbr_a
        dbrb_ref[...] = dbr_b

    def rows(n, blk=0):
        return pl.BlockSpec((ts, n), lambda i: (i, blk))

    def whole(r, c):
        return pl.BlockSpec((r, c), lambda i: (0, 0))

    def gate_cols(n, col):
        return pl.BlockSpec((pl.Element(ts), pl.Element(n)), lambda i: (i * ts, NA + col))

    outs = [
        ((S, D), f32, rows(D)), ((S, D), bf16, rows(D)), ((S, NW), bf16, gate_cols(NT, 0)),
        ((S, 512), bf16, rows(512)), ((S, 512), f32, rows(512)),
        ((S, 1536), f32, rows(1536)), ((S, 1536), f32, rows(1536)),
        ((S, 512), bf16, rows(512)), ((S, 512), bf16, rows(512)), ((S, D), bf16, rows(D)),
        ((S, D), bf16, rows(D)), ((S, D), bf16, rows(D)),
        ((1, 1), f32, whole(1, 1)), ((2, D), f32, whole(2, D)), ((1, 512), f32, whole(1, 512)),
    ]
    return pl.pallas_call(
        body,
        grid=(S // ts,),
        in_specs=[
            rows(D), rows(D), rows(512), rows(512), rows(1536), rows(1536),
            gate_cols(512, 0), gate_cols(512, 512), gate_cols(D, 1024), gate_cols(D, 2048), whole(2, D),
            whole(512, D), whole(512, D), whole(D, D), whole(1, 512),
        ],
        out_specs=[o[2] for o in outs],
        out_shape=[jax.ShapeDtypeStruct(o[0], o[1]) for o in outs],
        compiler_params=_params(("arbitrary",)),
        name="tail",
    )(x, tgt, o_a, l_a, o_b, l_b, proj, proj, proj, proj, bm, w_a, w_b, w_o, sink_b)


def _norm_bwd(xv, dyv, gain, kind, ones):
    r = lax.rsqrt(_half_sums(xv * xv, ones) * (1.0 / HD) + EPS)
    yv = xv * r
    up = jnp.where(kind == 0, dyv * SCALE, dyv)
    u = up * gain
    dxv = r * (u - yv * (_half_sums(u * yv, ones) * (1.0 / HD)))
    dxv = jnp.where(kind == 2, dyv, dxv)
    dg = jnp.where(kind == 2, 0.0, jnp.sum(up * yv, axis=0, keepdims=True))
    return dxv, dg


def _post_b(dqkv, proj_a, gains, dproj):
    def body(d_ref, p_ref, g_ref, alias_ref, o_ref, dg_ref, nat):
        del alias_ref
        g = pl.program_id(0)
        j = pl.program_id(1)
        kind = j // 4
        gain = g_ref[...]
        ones = _head_ones()

        @pl.when(j % 4 == 0)
        def _():
            dg_ref[...] = jnp.zeros_like(dg_ref)

        for gi in range(3):
            d = GROUPS[gi + 1][1]

            @pl.when(g == gi)
            def _():
                seq = S // d
                for c in range(d):
                    for i in range(seq // PCHUNK):
                        src = c * seq + i * PCHUNK
                        if d == 1:
                            idx = slice(src, src + PCHUNK)
                        else:
                            idx = pl.ds(c + i * PCHUNK * d, PCHUNK, stride=d)
                        dxv, dg = _norm_bwd(p_ref[idx, :], d_ref[PAD + src:PAD + src + PCHUNK, :], gain, kind, ones)
                        nat[idx, :] = dxv
                        dg_ref[...] += dg

        for i in range(S // CHUNK):
            o_ref[i * CHUNK:(i + 1) * CHUNK, :] = nat[i * CHUNK:(i + 1) * CHUNK, :].astype(bf16)

    return pl.pallas_call(
        body,
        grid=(3, 12),
        in_specs=[
            pl.BlockSpec((None, None, None, SP, LANES), lambda g, j: (g, j // 4, j % 4, 0, 0)),
            pl.BlockSpec((S, LANES), lambda g, j: (0, _col_block(g + 1, j))),
            pl.BlockSpec((None, None, 1, LANES), lambda g, j: (g + 1, j // 4, 0, 0)),
            pl.BlockSpec(memory_space=pl.ANY),
        ],
        out_specs=[
            pl.BlockSpec((S, LANES), lambda g, j: (0, _col_block(g + 1, j))),
            pl.BlockSpec((None, None, 1, LANES), lambda g, j: (g, j // 4, 0, 0)),
        ],
        out_shape=[jax.ShapeDtypeStruct((S, NW), bf16), jax.ShapeDtypeStruct((3, 3, 1, LANES), f32)],
        scratch_shapes=[pltpu.VMEM((S, LANES), f32)],
        input_output_aliases={3: 0},
        compiler_params=_params(("arbitrary", "arbitrary")),
        name="post_b",
    )(dqkv, proj_a, gains, dproj)


def _post_a(dqkv, proj_a, gains, dproj):
    def body(q_ref, e_ref, p_ref, g_ref, alias_ref, o_ref, dg_ref):
        del alias_ref
        j = pl.program_id(0)
        kind = jnp.maximum(j - 3, 0)
        gain = g_ref[...]
        lo = _lo()
        ones = _head_ones()

        @pl.when((j == 0) | (j >= 4))
        def _():
            dg_ref[...] = jnp.zeros_like(dg_ref)

        for i in range(S // PCHUNK):
            r0 = i * PCHUNK
            rows = slice(PAD + r0, PAD + r0 + PCHUNK)
            t0 = e_ref[0, rows, :] + e_ref[1, rows, :]
            t1 = e_ref[2, rows, :] + e_ref[3, rows, :]
            folded = jnp.where(lo, t0 + pltpu.roll(t0, HD, 1), t1 + pltpu.roll(t1, HD, 1))
            dyv = jnp.where(kind == 0, q_ref[rows, :], folded)
            dxv, dg = _norm_bwd(p_ref[r0:r0 + PCHUNK, :], dyv, gain, kind, ones)
            o_ref[r0:r0 + PCHUNK, :] = dxv.astype(bf16)
            dg_ref[...] += dg

    return pl.pallas_call(
        body,
        grid=(6,),
        in_specs=[
            pl.BlockSpec((None, None, None, SP, LANES), lambda j: (0, 0, jnp.minimum(j, 3), 0, 0)),
            pl.BlockSpec((None, None, 4, SP, LANES), lambda j: (0, jnp.clip(j - 3, 1, 2), 0, 0, 0)),
            pl.BlockSpec((S, LANES), lambda j: (0, j)),
            pl.BlockSpec((None, None, 1, LANES), lambda j: (0, jnp.maximum(j - 3, 0), 0, 0)),
            pl.BlockSpec(memory_space=pl.ANY),
        ],
        out_specs=[
            pl.BlockSpec((S, LANES), lambda j: (0, j)),
            pl.BlockSpec((None, 1, LANES), lambda j: (jnp.maximum(j - 3, 0), 0, 0)),
        ],
        out_shape=[jax.ShapeDtypeStruct((S, NW), bf16), jax.ShapeDtypeStruct((3, 1, LANES), f32)],
        input_output_aliases={4: 0},
        compiler_params=_params(("arbitrary",)),
        name="post_a",
    )(dqkv, dqkv, proj_a, gains, dproj)


def _dh_norm_bwd(dproj, w, x, rstd, gain, dy):
    ts = 1024
    nk = NW // TN

    def body(d_ref, w_ref, x_ref, r_ref, g_ref, dy_ref, gx_ref, dgn_ref, acc):
        i = pl.program_id(0)
        k = pl.program_id(1)

        @pl.when((i == 0) & (k == 0))
        def _():
            dgn_ref[...] = jnp.zeros_like(dgn_ref)

        @pl.when(k == 0)
        def _():
            acc[...] = jnp.zeros_like(acc)

        acc[...] += lax.dot_general(d_ref[...], w_ref[...], NT_DIMS, preferred_element_type=f32)

        @pl.when(k == nk - 1)
        def _():
            dh = acc[...]
            xh = x_ref[...] * r_ref[...]
            u = dh * g_ref[...]
            dx = r_ref[...] * (u - xh * jnp.mean(u * xh, axis=-1, keepdims=True))
            gx_ref[...] = dy_ref[...] + dx
            dgn_ref[...] += jnp.sum(dh * xh, axis=0, keepdims=True)

    return pl.pallas_call(
        body,
        grid=(S // ts, nk),
        in_specs=[
            pl.BlockSpec((ts, TN), lambda i, k: (i, k)),
            pl.BlockSpec((D, TN), lambda i, k: (0, k)),
            pl.BlockSpec((ts, D), lambda i, k: (i, 0)),
            pl.BlockSpec((ts, 1), lambda i, k: (i, 0)),
            pl.BlockSpec((1, D), lambda i, k: (0, 0)),
            pl.BlockSpec((ts, D), lambda i, k: (i, 0)),
        ],
        out_specs=[pl.BlockSpec((ts, D), lambda i, k: (i, 0)), pl.BlockSpec((1, D), lambda i, k: (0, 0))],
        out_shape=[jax.ShapeDtypeStruct((S, D), f32), jax.ShapeDtypeStruct((1, D), f32)],
        scratch_shapes=[pltpu.VMEM((ts, D), f32)],
        compiler_params=_params(("arbitrary", "arbitrary"), vmem_mib=56),
        name="dh_norm_bwd",
    )(dproj, w, x, rstd, gain, dy)


def _dw_in(hbt, dproj):
    tk = 1024
    win = WSH + 96

    def body(a_ref, b_ref, o_ref, acc):
        j = pl.program_id(0)
        k = pl.program_id(1)

        @pl.when(k == 0)
        def _():
            acc[...] = jnp.zeros_like(acc)

        for jj in range(NDEV):
            off = (WSH * jj) % LANES

            @pl.when(j == jj)
            def _():
                acc[...] += jnp.dot(a_ref[...], b_ref[:, off:off + WSH], preferred_element_type=f32)

        @pl.when(k == S // tk - 1)
        def _():
            o_ref[...] = acc[...].astype(bf16)

    return pl.pallas_call(
        body,
        grid=(NDEV, S // tk),
        in_specs=[
            pl.BlockSpec((D, tk), lambda j, k: (0, k)),
            pl.BlockSpec((pl.Element(tk), pl.Element(win)), lambda j, k: (k * tk, (WSH * j) // LANES * LANES)),
        ],
        out_specs=pl.BlockSpec((None, D, WSH), lambda j, k: (j, 0, 0)),
        out_shape=jax.ShapeDtypeStruct((NDEV, D, WSH), bf16),
        scratch_shapes=[pltpu.VMEM((D, WSH), f32)],
        compiler_params=_params(("arbitrary", "arbitrary")),
        name="dw_in",
    )(hbt, dproj)


def _matmul_tn(a, b, name):
    m, n = a.shape[1], b.shape[1]
    tn = TN if n % TN == 0 else 512
    tk = 512

    def body(a_ref, b_ref, o_ref):
        @pl.when(pl.program_id(1) == 0)
        def _():
            o_ref[...] = jnp.zeros_like(o_ref)

        o_ref[...] += lax.dot_general(a_ref[...], b_ref[...], TN_DIMS, preferred_element_type=f32)

    return pl.pallas_call(
        body,
        grid=(n // tn, S // tk),
        in_specs=[pl.BlockSpec((tk, m), lambda j, k: (k, 0)), pl.BlockSpec((tk, tn), lambda j, k: (k, j))],
        out_specs=pl.BlockSpec((m, tn), lambda j, k: (0, j)),
        out_shape=jax.ShapeDtypeStruct((m, n), f32),
        compiler_params=_params(("arbitrary", "arbitrary")),
        name=name,
    )(a, b)


def _exchange(scatter, gather, name):
    arrs = list(scatter) + list(gather)
    n = len(arrs)
    ns = len(scatter)

    def body(*refs):
        ins, outs = refs[:n], refs[n:2 * n]
        send_sems, recv_sems, local_sems = refs[2 * n:]
        x, y, c = lax.axis_index("x"), lax.axis_index("y"), lax.axis_index("c")
        me = 4 * x + 2 * y + c
        local, remote = [], []
        for a in range(n):
            lc = pltpu.make_async_copy(ins[a].at[me] if a < ns else ins[a], outs[a].at[me], local_sems.at[a])
            lc.start()
            local.append(lc)
            for r in range(1, NDEV):
                px = 1 - x if r & 4 else x
                py = 1 - y if r & 2 else y
                pc = 1 - c if r & 1 else c
                cp = pltpu.make_async_remote_copy(
                    src_ref=ins[a].at[4 * px + 2 * py + pc] if a < ns else ins[a],
                    dst_ref=outs[a].at[me],
                    send_sem=send_sems.at[a, r - 1],
                    recv_sem=recv_sems.at[a, r - 1],
                    device_id=(px, py, pc),
                    device_id_type=pl.DeviceIdType.MESH,
                )
                cp.start()
                remote.append(cp)
        for cp in remote:
            cp.wait_recv()
        for cp in remote:
            cp.wait_send()
        for lc in local:
            lc.wait()

    out_shape = [jax.ShapeDtypeStruct(a.shape if i < ns else (NDEV,) + a.shape, a.dtype) for i, a in enumerate(arrs)]
    return pl.pallas_call(
        body,
        in_specs=[pl.BlockSpec(memory_space=pl.ANY)] * n,
        out_specs=[pl.BlockSpec(memory_space=pl.ANY)] * n,
        out_shape=out_shape,
        scratch_shapes=[
            pltpu.SemaphoreType.DMA((n, NDEV - 1)),
            pltpu.SemaphoreType.DMA((n, NDEV - 1)),
            pltpu.SemaphoreType.DMA((n,)),
        ],
        compiler_params=pltpu.CompilerParams(has_side_effects=True),
        name=name,
    )(*arrs)


def _gather_two_level(arrs, name):
    n = len(arrs)

    def body(*refs):
        ins, outs = refs[:n], refs[n:2 * n]
        send_sems, recv_sems, local_sems = refs[2 * n:]
        x, y, c = lax.axis_index("x"), lax.axis_index("y"), lax.axis_index("c")
        me, sibling = (x, y, c), (x, y, 1 - c)
        chips = [(1 - x, y), (x, 1 - y), (1 - x, 1 - y)]

        def copy(a, k, block, to, src=None):
            slot = outs[a].at[4 * block[0] + 2 * block[1] + block[2]]
            return pltpu.make_async_remote_copy(
                src_ref=slot if src is None else src, dst_ref=slot, send_sem=send_sems.at[a, k],
                recv_sem=recv_sems.at[a, k], device_id=to, device_id_type=pl.DeviceIdType.MESH)

        mine, first, passed = [], [], []
        for a in range(n):
            lc = pltpu.make_async_copy(ins[a], outs[a].at[4 * x + 2 * y + c], local_sems.at[a])
            lc.start()
            mine.append(lc)
            first.append(copy(a, 0, me, sibling, src=ins[a]))
            first += [copy(a, 1 + j, me, (*chip, c), src=ins[a]) for j, chip in enumerate(chips)]
        for cp in first:
            cp.start()
        for j, chip in enumerate(chips):
            for a in range(n):
                copy(a, 1 + j, (*chip, c), me).wait_recv()
                fwd = copy(a, 4 + j, (*chip, c), sibling)
                fwd.start()
                passed.append(fwd)
        for a in range(n):
            copy(a, 0, sibling, me).wait_recv()
        for j, chip in enumerate(chips):
            for a in range(n):
                copy(a, 4 + j, (*chip, 1 - c), me).wait_recv()
        for cp in first + passed:
            cp.wait_send()
        for lc in mine:
            lc.wait()

    return pl.pallas_call(
        body,
        in_specs=[pl.BlockSpec(memory_space=pl.ANY)] * n,
        out_specs=[pl.BlockSpec(memory_space=pl.ANY)] * n,
        out_shape=[jax.ShapeDtypeStruct((NDEV,) + a.shape, a.dtype) for a in arrs],
        scratch_shapes=[
            pltpu.SemaphoreType.DMA((n, NDEV - 1)),
            pltpu.SemaphoreType.DMA((n, NDEV - 1)),
            pltpu.SemaphoreType.DMA((n,)),
        ],
        compiler_params=pltpu.CompilerParams(has_side_effects=True),
        name=name,
    )(*arrs)


_HBM = pl.BlockSpec(memory_space=pltpu.HBM)
_SEM = pl.BlockSpec(memory_space=pltpu.SEMAPHORE)
_EFFECT = pltpu.SideEffectType.DATAFLOW_SIDE_EFFECTING


def _scatter_start(arrs, name):
    n = len(arrs)

    def body(*refs):
        src, land = refs[:n], refs[n:2 * n]
        send_sems, recv_sems = refs[2 * n:3 * n], refs[3 * n:4 * n]
        token = refs[6 * n]
        x, y, c = lax.axis_index("x"), lax.axis_index("y"), lax.axis_index("c")
        me = 4 * x + 2 * y + c
        for a in range(n):
            for r in range(1, NDEV):
                px = 1 - x if r & 4 else x
                py = 1 - y if r & 2 else y
                pc = 1 - c if r & 1 else c
                pltpu.make_async_remote_copy(
                    src_ref=src[a].at[4 * px + 2 * py + pc], dst_ref=land[a].at[me], send_sem=send_sems[a],
                    recv_sem=recv_sems[a], device_id=(px, py, pc), device_id_type=pl.DeviceIdType.MESH).start()
        token[...] = jnp.zeros_like(token)

    hbm = [pltpu.HBM(a.shape, a.dtype) for a in arrs]
    ops = [pltpu.with_memory_space_constraint(a, pltpu.HBM) for a in arrs]
    outs = pl.pallas_call(
        body,
        out_shape=tuple([pltpu.SemaphoreType.DMA(())] * (2 * n) + hbm + hbm + [jax.ShapeDtypeStruct((8, LANES), f32)]),
        in_specs=[_HBM] * (2 * n),
        out_specs=tuple([_SEM] * (2 * n) + [_HBM] * (2 * n) + [pl.BlockSpec(memory_space=pltpu.VMEM)]),
        input_output_aliases={i: 2 * n + i for i in range(2 * n)},
        compiler_params=pltpu.CompilerParams(has_side_effects=_EFFECT),
        name=name,
    )(*ops, *ops)
    return outs[:n], outs[n:2 * n], outs[2 * n:3 * n], outs[3 * n:4 * n], outs[4 * n]


def _scatter_wait(send_sems, recv_sems, srcs, lands, after, name):
    n = len(srcs)

    def body(*refs):
        land = refs[n:2 * n]
        ssem, rsem = refs[2 * n:3 * n], refs[3 * n:4 * n]
        x, y, c = lax.axis_index("x"), lax.axis_index("y"), lax.axis_index("c")
        for a in range(n):
            seven = land[a].at[pl.ds(0, NDEV - 1)]
            done = pltpu.make_async_remote_copy(
                src_ref=seven, dst_ref=seven, send_sem=ssem[a], recv_sem=rsem[a], device_id=(x, y, c),
                device_id_type=pl.DeviceIdType.MESH)
            done.wait_send()
            done.wait_recv()

    hbm = [pltpu.HBM(a.shape, a.dtype) for a in srcs]
    outs = pl.pallas_call(
        body,
        out_shape=tuple(hbm + hbm),
        in_specs=[_HBM] * (2 * n) + [_SEM] * (2 * n) + [pl.BlockSpec(memory_space=pl.ANY)],
        out_specs=tuple([_HBM] * (2 * n)),
        input_output_aliases={i: i for i in range(2 * n)},
        compiler_params=pltpu.CompilerParams(has_side_effects=_EFFECT),
        name=name,
    )(*srcs, *lands, *send_sems, *recv_sems, after)
    return outs[n:]


def _adamw(w, slots, m, v, name):
    r, c = w.shape[-2:]
    tr = 128 if r % 128 == 0 else r

    def body(w_ref, s_ref, m_ref, v_ref, g_ref, d_ref, nm_ref, nv_ref):
        g = s_ref[0].astype(f32)
        for k in range(1, NDEV):
            g = g + s_ref[k].astype(f32)
        mm = ADAM_B1 * m_ref[...] + (1.0 - ADAM_B1) * g
        vv = ADAM_B2 * v_ref[...] + (1.0 - ADAM_B2) * (g * g)
        m_hat = mm / (1.0 - ADAM_B1 ** ADAM_STEP)
        v_hat = vv / (1.0 - ADAM_B2 ** ADAM_STEP)
        g_ref[...] = g
        d_ref[...] = -ADAM_LR * (m_hat / (jnp.sqrt(v_hat) + ADAM_EPS) + ADAM_WD * w_ref[...])
        nm_ref[...] = mm
        nv_ref[...] = vv

    if w.ndim == 3:
        blk = pl.BlockSpec((None, tr, c), lambda i: (0, i, 0))
    else:
        blk = pl.BlockSpec((tr, c), lambda i: (i, 0))
    return pl.pallas_call(
        body,
        grid=(r // tr,),
        in_specs=[blk, pl.BlockSpec((NDEV, tr, c), lambda i: (0, i, 0)), blk, blk],
        out_specs=[blk] * 4,
        out_shape=[jax.ShapeDtypeStruct(w.shape, f32)] * 4,
        compiler_params=_params(("arbitrary",)),
        name=name,
    )(w, slots, m, v)


def _local_step(x, tgt, norm_gain, w_shards, qn_a, kn_a, qn_b, kn_b, sink_a, rel_bias, w_a, w_b, b_merge, w_o,
                on_weight_grads=None):
    two = lambda t: jnp.concatenate([t, t], axis=-1).reshape(1, LANES)
    ones = jnp.ones((1, LANES), f32)
    gains = jnp.stack([
        jnp.stack([two(qn_a), two(kn_a), ones]),
        jnp.stack([two(qn_b), two(kn_b), ones]),
        jnp.stack([two(qn_b), two(kn_b), ones]),
        jnp.stack([two(qn_b), two(kn_b), ones]),
    ])
    buckets = [jnp.asarray(_bucket_np(blk, d)) for blk, d, _ in GROUPS]
    bias_a = _bias_expand(rel_bias, buckets[0], 0, "bias_expand_a")[None]
    bias_b = jnp.stack([_bias_expand(rel_bias, buckets[k], GROUPS[k][2], "bias_expand_b%d" % k) for k in (1, 2, 3)])

    hb, hbt, rstd = _rms(x, norm_gain)
    proj, w_in = _inproj(hb, w_shards)
    gl = _prep(proj, gains)
    o_a, l_a = _attn_fwd(gl, bias_a, sink_a.reshape(8), 0, 1, 128, True, "attn_fwd_a")
    o_bg, l_bg = _attn_fwd(gl, bias_b, None, 1, 3, 64, False, "attn_fwd_b")
    o_b, l_b = _b_to_natural(o_bg, l_bg)
    sink_b = jnp.repeat(sink_a.reshape(8), HD).reshape(1, 512)

    (dy, dyb, dproj, do_a, dd_a, do_b, dd_b, ya, yb, mg, dbr_a, dbr_b, loss, dbm, dsk) = _tail(
        x, tgt, o_a, l_a, o_b, l_b, proj, b_merge, w_a, w_b, w_o, sink_b)

    dqkv_a, dbk_a = _attn_bwd(gl, bias_a, buckets[0][None], do_a, l_a, dd_a, 0, 1, 128, True, "attn_bwd_a")
    do_bg, dd_bg = _b_from_natural(do_b, dd_b)
    dqkv_b, dbk_b = _attn_bwd(gl, bias_b, jnp.stack(buckets[1:]), do_bg, l_bg, dd_bg, 1, 3, 64, False, "attn_bwd_b")

    dproj, dg_a = _post_a(dqkv_a, proj, gains, dproj)
    dproj, dg_b = _post_b(dqkv_b, proj, gains, dproj)

    dw_in = _dw_in(hbt, dproj)
    dw_o = _matmul_tn(mg, dyb, "dw_out")
    dw_a = _matmul_tn(ya, dbr_a, "dw_branch_a")
    dw_b = _matmul_tn(yb, dbr_b, "dw_branch_b")
    token = jnp.zeros((), f32) if on_weight_grads is None else on_weight_grads(
        dict(w_in=dw_in, w_branch_a=dw_a, w_branch_b=dw_b, b_merge=dbm, w_out=dw_o))
    grad_x, d_norm_gain = _dh_norm_bwd(dproj, w_in, x, rstd, norm_gain + token, dy)

    fold = lambda t: t[..., :HD] + t[..., HD:]
    d_qn_a = fold(dg_a[0, 0])
    d_kn_a = fold(dg_a[1, 0])
    d_qn_b = fold(dg_b[:, 0, 0].sum(axis=0))
    d_kn_b = fold(dg_b[:, 1, 0].sum(axis=0))
    d_sink = dsk.reshape(8, HD)[:, 0]
    red = jnp.concatenate([dbk_a, dbk_b])
    d_rel = red[:, :, 0, :32].reshape(32, 32).T
    return dict(loss=loss, grad_x=grad_x, norm_gain=d_norm_gain, w_in=dw_in, q_norm_a=d_qn_a, k_norm_a=d_kn_a,
                q_norm_b=d_qn_b, k_norm_b=d_kn_b, sink_a=d_sink, rel_bias=d_rel, w_branch_a=dw_a, w_branch_b=dw_b,
                b_merge=dbm, w_out=dw_o)


SMALL = (("norm_gain", D), ("q_norm_a", HD), ("k_norm_a", HD), ("q_norm_b", HD), ("k_norm_b", HD), ("sink_a", 8),
         ("rel_bias", 1024))
SMALL_PAD = 2432


SMALL_USED = sum(sz for _, sz in SMALL)


def _pack_small(parts, loss=None):
    tail = jnp.zeros((SMALL_PAD - SMALL_USED,), f32)
    if loss is not None:
        tail = tail.at[0].set(loss.reshape(()))
    return jnp.concatenate([parts[n].reshape(-1) for n, _ in SMALL] + [tail]).reshape(1, SMALL_PAD)


def _unpack_small(flat, shapes):
    out, off = {}, 0
    for n, sz in SMALL:
        out[n] = flat[0, off:off + sz].reshape(shapes[n])
        off += sz
    return out


def kernel(x, norm_gain, w_in, q_norm_a, k_norm_a, q_norm_b, k_norm_b, sink_a, rel_bias, w_branch_a, w_branch_b, b_merge, w_out, loss_target, m_norm_gain, m_w_in, m_q_norm_a, m_k_norm_a, m_q_norm_b, m_k_norm_b, m_sink_a, m_rel_bias, m_w_branch_a, m_w_branch_b, m_b_merge, m_w_out, v_norm_gain, v_w_in, v_q_norm_a, v_k_norm_a, v_q_norm_b, v_k_norm_b, v_sink_a, v_rel_bias, v_w_branch_a, v_w_branch_b, v_b_merge, v_w_out):
    wsh = NW // NDEV
    csh = D // NDEV
    g_in, g_a, g_b, g_o, g_bm = _gather_two_level(
        [w_in[0].astype(bf16), w_branch_a[0].astype(bf16), w_branch_b[0].astype(bf16), w_out[0].astype(bf16),
         b_merge[0]], "gather_weights")
    w_a_full = g_a.transpose(1, 0, 2).reshape(512, D)
    w_b_full = g_b.transpose(1, 0, 2).reshape(512, D)
    w_o_full = g_o.reshape(D, D)
    bm_full = g_bm.transpose(1, 0, 2).reshape(2, D)

    pending = {}

    def start_exchange(gw):
        blocks = [gw["w_in"],
                  gw["w_branch_a"].reshape(512, NDEV, csh).transpose(1, 0, 2).astype(bf16),
                  gw["w_branch_b"].reshape(512, NDEV, csh).transpose(1, 0, 2).astype(bf16),
                  gw["w_out"].reshape(NDEV, csh, D).astype(bf16),
                  gw["b_merge"].reshape(2, NDEV, csh).transpose(1, 0, 2)]
        pending["started"] = _scatter_start(blocks, "scatter_grads_start")
        return pending["started"][4][0, 0]

    loc = _local_step(x[0], loss_target[0], norm_gain, g_in, q_norm_a, k_norm_a, q_norm_b, k_norm_b, sink_a,
                      rel_bias, w_a_full, w_b_full, bm_full, w_o_full, on_weight_grads=start_exchange)

    small_shapes = dict(norm_gain=(1, D), q_norm_a=(1, HD), k_norm_a=(1, HD), q_norm_b=(1, HD), k_norm_b=(1, HD),
                        sink_a=(1, 8), rel_bias=(32, 32))
    (r_small,) = _exchange([], [_pack_small(loc, loc["loss"])], "gather_small_grads")
    send_sems, recv_sems, srcs, lands, _ = pending["started"]
    r_in, r_a, r_b, r_o, r_bm = _scatter_wait(send_sems, recv_sems, srcs, lands, r_small, "scatter_grads_wait")

    given = dict(norm_gain=norm_gain, q_norm_a=q_norm_a, k_norm_a=k_norm_a, q_norm_b=q_norm_b, k_norm_b=k_norm_b,
                 sink_a=sink_a, rel_bias=rel_bias)
    m_small = dict(norm_gain=m_norm_gain, q_norm_a=m_q_norm_a, k_norm_a=m_k_norm_a, q_norm_b=m_q_norm_b,
                   k_norm_b=m_k_norm_b, sink_a=m_sink_a, rel_bias=m_rel_bias)
    v_small = dict(norm_gain=v_norm_gain, q_norm_a=v_q_norm_a, k_norm_a=v_k_norm_a, q_norm_b=v_q_norm_b,
                   k_norm_b=v_k_norm_b, sink_a=v_sink_a, rel_bias=v_rel_bias)
    res = {
        "small": _adamw(_pack_small(given), r_small, _pack_small(m_small), _pack_small(v_small), "adamw_small"),
        "w_in": _adamw(w_in, r_in, m_w_in, v_w_in, "adamw_w_in"),
        "w_branch_a": _adamw(w_branch_a, r_a, m_w_branch_a, v_w_branch_a, "adamw_w_branch_a"),
        "w_branch_b": _adamw(w_branch_b, r_b, m_w_branch_b, v_w_branch_b, "adamw_w_branch_b"),
        "b_merge": _adamw(b_merge, r_bm, m_b_merge, v_b_merge, "adamw_b_merge"),
        "w_out": _adamw(w_out, r_o, m_w_out, v_w_out, "adamw_w_out"),
    }
    order = ["norm_gain", "w_in", "q_norm_a", "k_norm_a", "q_norm_b", "k_norm_b", "sink_a", "rel_bias", "w_branch_a",
             "w_branch_b", "b_merge", "w_out"]
    outs = []
    for k in range(4):
        small = _unpack_small(res["small"][k], small_shapes)
        for n in order:
            outs.append(small[n] if n in small else res[n][k])
    loss = res["small"][0][0, SMALL_USED]
    return (loss, loc["grad_x"][None], *outs)
```

```python
import math

import numpy as np
import jax
import jax.numpy as jnp
from jax import lax
from jax.experimental import pallas as pl
from jax.experimental.pallas import tpu as pltpu

f32 = jnp.float32
bf16 = jnp.bfloat16

S = 4096
D = 1024
NA = 5376
NT = 3072
NW = NA + NT
WSH = NW // 8
HD = 64
LANES = 128
EPS = 1e-6
NEG = -1e30
SCALE = HD ** -0.5
TQ = 128
PAD = 128
SP = S + 2 * PAD
NDEV = 8
GROUPS = ((128, 1, 0), (64, 1, 8), (64, 4, 16), (64, 16, 24))
CHUNK = 256
PCHUNK = 128
RC = 64
TN = 768

ADAM_LR, ADAM_B1, ADAM_B2, ADAM_EPS, ADAM_WD, ADAM_STEP = 0.001, 0.9, 0.999, 1e-08, 0.01, 10

MIB = 1024 * 1024
NT_DIMS = (((1,), (1,)), ((), ()))
TN_DIMS = (((0,), (0,)), ((), ()))


def _params(sem=None, vmem_mib=48):
    return pltpu.CompilerParams(dimension_semantics=sem, vmem_limit_bytes=vmem_mib * MIB)


def _lo():
    return lax.broadcasted_iota(jnp.int32, (1, LANES), 1) < HD


def _head_ones():
    r = lax.broadcasted_iota(jnp.int32, (LANES, LANES), 0) // HD
    c = lax.broadcasted_iota(jnp.int32, (LANES, LANES), 1) // HD
    return jnp.where(r == c, 1.0, 0.0).astype(bf16)


def _half_sums(x, ones):
    hi = x.astype(bf16)
    mid = (x - hi.astype(f32)).astype(bf16)
    return (jnp.dot(hi, ones, preferred_element_type=f32) + jnp.dot(mid, ones, preferred_element_type=f32))


def _seg_sum(x, ones):
    outs = [_half_sums(x[:, b * LANES:(b + 1) * LANES], ones) for b in range(x.shape[1] // LANES)]
    return outs[0] if len(outs) == 1 else jnp.concatenate(outs, axis=1)


def _bucket_np(blk, stride):
    w = TQ + 2 * blk
    rel = np.arange(w)[None, :] - blk - np.arange(TQ)[:, None]
    band = np.abs(rel) <= blk
    r = rel * stride
    n = np.abs(r)
    nf = np.maximum(n, 8).astype(np.float32)
    large = 8 + (np.log(nf / np.float32(8)) / np.float32(math.log(128.0)) * np.float32(8)).astype(np.int32)
    large = np.minimum(large, 15)
    b = (r > 0).astype(np.int32) * 16 + np.where(n < 8, n, large)
    return np.where(band, b, -1).astype(np.int32)


def _rms(x, gain):
    ts = 512

    def body(x_ref, g_ref, h_ref, ht_ref, r_ref):
        xv = x_ref[...]
        r = lax.rsqrt(jnp.mean(xv * xv, axis=-1, keepdims=True) + EPS)
        h = (xv * r) * g_ref[...]
        h_ref[...] = h.astype(bf16)
        ht_ref[...] = h.T.astype(bf16)
        r_ref[...] = r

    return pl.pallas_call(
        body,
        grid=(S // ts,),
        in_specs=[pl.BlockSpec((ts, D), lambda i: (i, 0)), pl.BlockSpec((1, D), lambda i: (0, 0))],
        out_specs=[pl.BlockSpec((ts, D), lambda i: (i, 0)), pl.BlockSpec((D, ts), lambda i: (0, i)),
                   pl.BlockSpec((ts, 1), lambda i: (i, 0))],
        out_shape=[jax.ShapeDtypeStruct((S, D), bf16), jax.ShapeDtypeStruct((D, S), bf16),
                   jax.ShapeDtypeStruct((S, 1), f32)],
        compiler_params=_params(("arbitrary",)),
        name="rms",
    )(x, gain)


def _inproj(hb, w_shards):
    ts = 1024

    def body(h_ref, wa_ref, wb_ref, p_ref, wf_ref, w_scr):
        n = pl.program_id(0)

        @pl.when(pl.program_id(1) == 0)
        def _():
            for nn in range(NW // TN):
                j0 = (TN * nn) // WSH
                a = TN * nn - WSH * j0
                len1 = min(TN, WSH - a)

                @pl.when(n == nn)
                def _():
                    w_scr[:, 0:len1] = wa_ref[:, a:a + len1]
                    if len1 < TN:
                        w_scr[:, len1:TN] = wb_ref[:, 0:TN - len1]

            wf_ref[...] = w_scr[...]

        p_ref[...] = jnp.dot(h_ref[...], w_scr[...], preferred_element_type=f32)

    return pl.pallas_call(
        body,
        grid=(NW // TN, S // ts),
        in_specs=[
            pl.BlockSpec((ts, D), lambda n, i: (i, 0)),
            pl.BlockSpec((None, D, WSH), lambda n, i: ((TN * n) // WSH, 0, 0)),
            pl.BlockSpec((None, D, WSH), lambda n, i: (jnp.minimum((TN * n) // WSH + 1, NDEV - 1), 0, 0)),
        ],
        out_specs=[pl.BlockSpec((ts, TN), lambda n, i: (i, n)), pl.BlockSpec((D, TN), lambda n, i: (0, n))],
        out_shape=[jax.ShapeDtypeStruct((S, NW), f32), jax.ShapeDtypeStruct((D, NW), bf16)],
        scratch_shapes=[pltpu.VMEM((D, TN), bf16)],
        compiler_params=_params(("arbitrary", "arbitrary")),
        name="inproj",
    )(hb, w_shards, w_shards)


def _bias_expand(table, bucket, c0, name):
    tq, w = bucket.shape

    def body(tab_ref, bk_ref, o_ref):
        h = pl.program_id(0)
        bk = bk_ref[...]

        def step(b, acc):
            return jnp.where(bk == b, tab_ref[b, c0 + h], acc)

        o_ref[...] = lax.fori_loop(0, 32, step, jnp.full((tq, w), NEG, f32))

    return pl.pallas_call(
        body,
        grid=(8,),
        in_specs=[pl.BlockSpec(memory_space=pltpu.SMEM), pl.BlockSpec((tq, w), lambda h: (0, 0))],
        out_specs=pl.BlockSpec((None, tq, w), lambda h: (h, 0, 0)),
        out_shape=jax.ShapeDtypeStruct((8, tq, w), f32),
        compiler_params=_params(("arbitrary",)),
        name=name,
    )(table, bucket)


def _col_block(g, j):
    kind = j // 4
    hp = j % 4
    a = jnp.where(kind == 0, hp, 3 + kind)
    b = 6 + 12 * kind + 4 * (g - 1) + hp
    return jnp.where(g == 0, a, b)


def _prep(proj_a, gains):
    def body(p_ref, g_ref, o_ref):
        g = pl.program_id(0)
        j = pl.program_id(1)
        kind = j // 4
        lo = _lo()
        ones = _head_ones()
        half = jnp.where(lo, 0, 1)
        take = (kind == 0) | (half == (j % 4) // 2)
        gain = g_ref[...]
        o_ref[0:PAD, :] = jnp.zeros((PAD, LANES), bf16)
        o_ref[PAD + S:SP, :] = jnp.zeros((PAD, LANES), bf16)

        def norm_store(xv, dst, dup):
            if dup:
                xv = jnp.where(take, xv, pltpu.roll(xv, HD, 1))
            r = lax.rsqrt(_half_sums(xv * xv, ones) * (1.0 / HD) + EPS)
            r = jnp.where(kind == 2, 1.0, r)
            yv = (xv * r) * gain
            yv = jnp.where(kind == 0, yv * SCALE, yv)
            o_ref[PAD + dst:PAD + dst + CHUNK, :] = yv.astype(bf16)

        for gi, (_, d, _) in enumerate(GROUPS):
            @pl.when(g == gi)
            def _():
                seq = S // d
                for c in range(d):
                    for i in range(seq // CHUNK):
                        if d == 1:
                            xv = p_ref[i * CHUNK:(i + 1) * CHUNK, :]
                        else:
                            xv = p_ref[pl.ds(c + i * CHUNK * d, CHUNK, stride=d), :]
                        norm_store(xv, c * seq + i * CHUNK, gi == 0)

    return pl.pallas_call(
        body,
        grid=(4, 12),
        in_specs=[
            pl.BlockSpec((S, LANES), lambda g, j: (0, _col_block(g, j))),
            pl.BlockSpec((None, None, 1, LANES), lambda g, j: (g, j // 4, 0, 0)),
        ],
        out_specs=pl.BlockSpec((None, None, SP, LANES), lambda g, j: (g, j, 0, 0)),
        out_shape=jax.ShapeDtypeStruct((4, 12, SP, LANES), bf16),
        compiler_params=_params(("arbitrary", "arbitrary")),
        name="prep",
    )(proj_a, gains)


def _seq_len(g, ng):
    return S if ng == 1 else jnp.right_shift(S, 2 * g)


def _stack_heads(t, lo):
    z = jnp.zeros_like(t)
    return jnp.concatenate([jnp.where(lo, t, z), jnp.where(lo, z, t)], axis=0)


def _unstack_heads(t2, lo):
    return jnp.where(lo, t2[:TQ], t2[TQ:])


def _row_spec(natural, ng):
    if natural:
        return pl.BlockSpec((S, LANES), lambda g, hp: (0, hp)), (S, 4 * LANES)
    return pl.BlockSpec((None, None, S, LANES), lambda g, hp: (g, hp, 0, 0)), (ng, 4, S, LANES)


def _attn_fwd(gl, bias, sink, g0, ng, blk, natural, name):
    w = TQ + 2 * blk
    use_sink = sink is not None

    def body(*refs):
        if use_sink:
            sink_ref, q_ref, k_ref, v_ref, b_ref, o_ref, l_ref, s0, s1, p0, p1, lse_scr = refs
        else:
            q_ref, k_ref, v_ref, b_ref, o_ref, l_ref, s0, s1, p0, p1, lse_scr = refs
        g = pl.program_id(0)
        hp = pl.program_id(1)
        lo = _lo()
        seq = _seq_len(g, ng)
        mi = lax.broadcasted_iota(jnp.int32, (1, w), 1)
        s_bufs, p_bufs = (s0, s1), (p0, p1)

        def scores(t, slot):
            f0 = pl.multiple_of(t * TQ, TQ)
            q2 = _stack_heads(q_ref[pl.ds(PAD + f0, TQ), :], lo)
            kw = k_ref[pl.ds(PAD - blk + f0, w), :]
            s_bufs[slot][...] = lax.dot_general(q2, kw, NT_DIMS, preferred_element_type=f32)

        def softmax(t, slot):
            f0 = pl.multiple_of(t * TQ, TQ)
            m0 = jnp.bitwise_and(f0, seq - 1)
            inside = (mi >= blk - m0) & (mi < seq + blk - m0)
            for h in range(2):
                for r in range(TQ // RC):
                    rows = slice(h * TQ + r * RC, h * TQ + (r + 1) * RC)
                    logit = jnp.where(inside, s_bufs[slot][rows, :] + b_ref[h, r * RC:(r + 1) * RC, :], NEG)
                    m = jnp.max(logit, axis=1, keepdims=True)
                    e = jnp.exp(logit - m)
                    lse = m + jnp.log(jnp.sum(e, axis=1, keepdims=True))
                    if use_sink:
                        sk = sink_ref[2 * hp + h]
                        mx = jnp.maximum(lse, sk)
                        lse = mx + jnp.log(jnp.exp(lse - mx) + jnp.exp(sk - mx))
                    p_bufs[slot][rows, :] = (e * jnp.exp(m - lse)).astype(bf16)
                    lse_scr[rows, :] = jnp.broadcast_to(lse, (RC, LANES))
            l_ref[pl.ds(f0, TQ), :] = jnp.where(lo, lse_scr[0:TQ, :], lse_scr[TQ:2 * TQ, :])

        def values(t, slot):
            f0 = pl.multiple_of(t * TQ, TQ)
            vw = v_ref[pl.ds(PAD - blk + f0, w), :]
            o2 = jnp.dot(p_bufs[slot][...], vw, preferred_element_type=f32)
            o_ref[pl.ds(f0, TQ), :] = _unstack_heads(o2, lo)

        nt = S // TQ
        scores(0, 0)
        scores(1, 1)
        softmax(0, 0)

        def pair(k, carry):
            t = 2 * k + 2
            scores(t, 0)
            softmax(t - 1, 1)
            values(t - 2, 0)
            scores(t + 1, 1)
            softmax(t, 0)
            values(t - 1, 1)
            return carry

        lax.fori_loop(0, (nt - 2) // 2, pair, 0)
        softmax(nt - 1, 1)
        values(nt - 2, 0)
        values(nt - 1, 1)

    in_specs = [
        pl.BlockSpec((None, None, SP, LANES), lambda g, hp: (g0 + g, hp, 0, 0)),
        pl.BlockSpec((None, None, SP, LANES), lambda g, hp: (g0 + g, 4 + hp, 0, 0)),
        pl.BlockSpec((None, None, SP, LANES), lambda g, hp: (g0 + g, 8 + hp, 0, 0)),
        pl.BlockSpec((None, 2, TQ, w), lambda g, hp: (g, hp, 0, 0)),
    ]
    args = [gl, gl, gl, bias]
    if use_sink:
        in_specs = [pl.BlockSpec(memory_space=pltpu.SMEM)] + in_specs
        args = [sink] + args
    out, shape = _row_spec(natural, ng)
    return pl.pallas_call(
        body,
        grid=(ng, 4),
        in_specs=in_specs,
        out_specs=[out, out],
        out_shape=[jax.ShapeDtypeStruct(shape, f32)] * 2,
        scratch_shapes=[pltpu.VMEM((2 * TQ, w), f32), pltpu.VMEM((2 * TQ, w), f32),
                        pltpu.VMEM((2 * TQ, w), bf16), pltpu.VMEM((2 * TQ, w), bf16),
                        pltpu.VMEM((2 * TQ, LANES), f32)],
        compiler_params=_params(("arbitrary", "arbitrary")),
        name=name,
    )(*args)


def _attn_bwd(gl, bias, bucket, do, lse, dd, g0, ng, blk, natural, name):
    w = TQ + 2 * blk

    def body(q_ref, k_ref, v_ref, b_ref, bk_ref, do_ref, l_ref, d_ref, dqkv_ref, dbk_ref,
             db_acc, s0, s1, dp0, dp1, pb0, pb1, ds0, ds1):
        g = pl.program_id(0)
        lo = _lo()
        hi = jnp.logical_not(lo)
        seq = _seq_len(g, ng)
        mi = lax.broadcasted_iota(jnp.int32, (1, w), 1)
        dqkv_ref[1] = jnp.zeros((SP, LANES), f32)
        dqkv_ref[2] = jnp.zeros((SP, LANES), f32)
        db_acc[...] = jnp.zeros((2 * TQ, w), f32)
        s_bufs, dp_bufs, pb_bufs, ds_bufs = (s0, s1), (dp0, dp1), (pb0, pb1), (ds0, ds1)

        def stacked(t):
            f0 = pl.multiple_of(t * TQ, TQ)
            q2 = _stack_heads(q_ref[pl.ds(PAD + f0, TQ), :], lo)
            do2 = _stack_heads(do_ref[pl.ds(f0, TQ), :], lo)
            return f0, q2, do2

        def scores(t, slot):
            f0, q2, do2 = stacked(t)
            win = pl.ds(PAD - blk + f0, w)
            s_bufs[slot][...] = lax.dot_general(q2, k_ref[win, :], NT_DIMS, preferred_element_type=f32)
            dp_bufs[slot][...] = lax.dot_general(do2, v_ref[win, :], NT_DIMS, preferred_element_type=f32)

        def grads(t, slot):
            f0 = pl.multiple_of(t * TQ, TQ)
            m0 = jnp.bitwise_and(f0, seq - 1)
            inside = (mi >= blk - m0) & (mi < seq + blk - m0)
            for h in range(2):
                msk = lo if h == 0 else hi
                for r in range(TQ // RC):
                    rows = slice(h * TQ + r * RC, h * TQ + (r + 1) * RC)
                    src = pl.ds(f0 + r * RC, RC)
                    lh = jnp.max(jnp.where(msk, l_ref[src, :], -jnp.inf), axis=1, keepdims=True)
                    dh = jnp.max(jnp.where(msk, d_ref[src, :], -jnp.inf), axis=1, keepdims=True)
                    logit = jnp.where(inside, s_bufs[slot][rows, :] + b_ref[h, r * RC:(r + 1) * RC, :], NEG)
                    p = jnp.exp(logit - lh)
                    ds = p * (dp_bufs[slot][rows, :] - dh)
                    db_acc[rows, :] += ds
                    pb_bufs[slot][rows, :] = p.astype(bf16)
                    ds_bufs[slot][rows, :] = ds.astype(bf16)

        def accumulate(t, slot):
            f0, q2, do2 = stacked(t)
            win = pl.ds(PAD - blk + f0, w)
            dsb = ds_bufs[slot][...]
            dq2 = jnp.dot(dsb, k_ref[win, :], preferred_element_type=f32)
            dqkv_ref[0, pl.ds(PAD + f0, TQ), :] = _unstack_heads(dq2, lo)
            dqkv_ref[1, win, :] += lax.dot_general(dsb, q2, TN_DIMS, preferred_element_type=f32)
            dqkv_ref[2, win, :] += lax.dot_general(pb_bufs[slot][...], do2, TN_DIMS, preferred_element_type=f32)

        nt = S // TQ
        scores(0, 0)
        scores(1, 1)
        grads(0, 0)

        def pair(k, carry):
            t = 2 * k + 2
            scores(t, 0)
            grads(t - 1, 1)
            accumulate(t - 2, 0)
            scores(t + 1, 1)
            grads(t, 0)
            accumulate(t - 1, 1)
            return carry

        lax.fori_loop(0, (nt - 2) // 2, pair, 0)
        grads(nt - 1, 1)
        accumulate(nt - 2, 0)
        accumulate(nt - 1, 1)

        bk = bk_ref[...]
        lane = lax.broadcasted_iota(jnp.int32, (8, LANES), 1)
        for h in range(2):
            db = db_acc[h * TQ:(h + 1) * TQ, :]
            acc = jnp.zeros((8, LANES), f32)
            for b in range(32):
                part = jnp.where(bk == b, db, 0.0).reshape(TQ // 8, 8, w).sum(axis=0)
                tot = jnp.sum(jnp.sum(part, axis=1, keepdims=True), axis=0, keepdims=True)
                acc = jnp.where(lane == b, tot, acc)
            dbk_ref[h] = acc

    def gcol(off):
        return pl.BlockSpec((None, None, SP, LANES), lambda g, hp: (g0 + g, off + hp, 0, 0))

    row, _ = _row_spec(natural, ng)
    return pl.pallas_call(
        body,
        grid=(ng, 4),
        in_specs=[gcol(0), gcol(4), gcol(8), pl.BlockSpec((None, 2, TQ, w), lambda g, hp: (g, hp, 0, 0)),
                  pl.BlockSpec((None, TQ, w), lambda g, hp: (g, 0, 0)), row, row, row],
        out_specs=[pl.BlockSpec((None, 3, None, SP, LANES), lambda g, hp: (g, 0, hp, 0, 0)),
                   pl.BlockSpec((None, 2, 8, LANES), lambda g, hp: (g, hp, 0, 0))],
        out_shape=[
            jax.ShapeDtypeStruct((ng, 3, 4, SP, LANES), f32),
            jax.ShapeDtypeStruct((ng, 8, 8, LANES), f32),
        ],
        scratch_shapes=[pltpu.VMEM((2 * TQ, w), f32)] * 5 + [pltpu.VMEM((2 * TQ, w), bf16)] * 4,
        compiler_params=_params(("arbitrary", "arbitrary"), vmem_mib=56),
        name=name,
    )(gl, gl, gl, bias, bucket, do, lse, dd)


def _b_to_natural(o_gl, l_gl):
    def body(o_ref, l_ref, on_ref, ln_ref):
        g = pl.program_id(0)
        for gi in range(3):
            d = GROUPS[gi + 1][1]

            @pl.when(g == gi)
            def _():
                seq = S // d
                for c in range(d):
                    for i in range(seq // CHUNK):
                        src = slice(c * seq + i * CHUNK, c * seq + (i + 1) * CHUNK)
                        if d == 1:
                            on_ref[src, :] = o_ref[src, :]
                            ln_ref[src, :] = l_ref[src, :]
                        else:
                            dst = pl.ds(c + i * CHUNK * d, CHUNK, stride=d)
                            on_ref[dst, :] = o_ref[src, :]
                            ln_ref[dst, :] = l_ref[src, :]

    col = pl.BlockSpec((None, None, S, LANES), lambda g, hp: (g, hp, 0, 0))
    nat = pl.BlockSpec((S, LANES), lambda g, hp: (0, 4 * g + hp))
    return pl.pallas_call(
        body,
        grid=(3, 4),
        in_specs=[col, col],
        out_specs=[nat, nat],
        out_shape=[jax.ShapeDtypeStruct((S, 3 * 512), f32)] * 2,
        compiler_params=_params(("arbitrary", "arbitrary")),
        name="b_to_natural",
    )(o_gl, l_gl)


def _b_from_natural(do_n, dd_n):
    def body(do_ref, dd_ref, dog_ref, ddg_ref):
        g = pl.program_id(0)
        for gi in range(3):
            d = GROUPS[gi + 1][1]

            @pl.when(g == gi)
            def _():
                seq = S // d
                for c in range(d):
                    for i in range(seq // CHUNK):
                        dst = slice(c * seq + i * CHUNK, c * seq + (i + 1) * CHUNK)
                        if d == 1:
                            a, b = do_ref[dst, :], dd_ref[dst, :]
                        else:
                            src = pl.ds(c + i * CHUNK * d, CHUNK, stride=d)
                            a, b = do_ref[src, :], dd_ref[src, :]
                        dog_ref[dst, :] = a.astype(bf16)
                        ddg_ref[dst, :] = b

    col = pl.BlockSpec((None, None, S, LANES), lambda g, hp: (g, hp, 0, 0))
    nat = pl.BlockSpec((S, LANES), lambda g, hp: (0, 4 * g + hp))
    return pl.pallas_call(
        body,
        grid=(3, 4),
        in_specs=[nat, nat],
        out_specs=[col, col],
        out_shape=[jax.ShapeDtypeStruct((3, 4, S, LANES), bf16), jax.ShapeDtypeStruct((3, 4, S, LANES), f32)],
        compiler_params=_params(("arbitrary", "arbitrary")),
        name="b_from_natural",
    )(do_n, dd_n)


def _sigmoid(z):
    return 1.0 / (1.0 + jnp.exp(-z))


def _tail(x, tgt, o_a, l_a, o_b, l_b, proj, bm, w_a, w_b, w_o, sink_b):
    ts = 128

    def body(x_ref, t_ref, oa_ref, la_ref, ob_ref, lb_ref, ga_ref, gb_ref, m0_ref, m1_ref, bm_ref,
             wa_ref, wb_ref, wo_ref, sk_ref,
             dy_ref, dyb_ref, dt_ref, doa_ref, dda_ref, dob_ref, ddb_ref, ya_ref, yb_ref, mg_ref, dbra_ref, dbrb_ref,
             loss_ref, dbm_ref, dsk_ref):
        i = pl.program_id(0)

        @pl.when(i == 0)
        def _():
            loss_ref[...] = jnp.zeros_like(loss_ref)
            dbm_ref[...] = jnp.zeros_like(dbm_ref)
            dsk_ref[...] = jnp.zeros_like(dsk_ref)

        ga = ga_ref[...]
        sa = _sigmoid(ga)
        silu_a = ga * sa
        oa = oa_ref[...]
        ya = oa * silu_a
        gb = gb_ref[...]
        sb = _sigmoid(gb)
        silu_b = gb * sb
        ob = [ob_ref[:, k * 512:(k + 1) * 512] for k in range(3)]
        lb = [lb_ref[:, k * 512:(k + 1) * 512] for k in range(3)]
        mx = jnp.maximum(jnp.maximum(lb[0], lb[1]), lb[2])
        ex = [jnp.exp(v - mx) for v in lb]
        den = ex[0] + ex[1] + ex[2]
        alpha = [e / den for e in ex]
        ybc = alpha[0] * ob[0] + alpha[1] * ob[1] + alpha[2] * ob[2]
        yb = ybc * silu_b
        yab = ya.astype(bf16)
        ybb = yb.astype(bf16)
        br_a = jnp.dot(yab, wa_ref[...], preferred_element_type=f32)
        br_b = jnp.dot(ybb, wb_ref[...], preferred_element_type=f32)
        g0 = _sigmoid(m0_ref[...] + bm_ref[0:1, :])
        g1 = _sigmoid(m1_ref[...] + bm_ref[1:2, :])
        merged = g0 * br_a + g1 * br_b
        mgb = merged.astype(bf16)
        y = x_ref[...] + jnp.dot(mgb, wo_ref[...], preferred_element_type=f32)
        err = y - t_ref[...]
        part = jnp.sum(jnp.sum(err * err, axis=1, keepdims=True), axis=0, keepdims=True)
        loss_ref[...] += part * (0.5 / D)
        dy = err * (1.0 / D)
        dyb = dy.astype(bf16)
        dmerged = lax.dot_general(dyb, wo_ref[...], NT_DIMS, preferred_element_type=f32)
        dbr_a = (dmerged * g0).astype(bf16)
        dbr_b = (dmerged * g1).astype(bf16)
        dm0 = dmerged * br_a * (g0 * (1.0 - g0))
        dm1 = dmerged * br_b * (g1 * (1.0 - g1))
        dbm_ref[0:1, :] += jnp.sum(dm0, axis=0, keepdims=True)
        dbm_ref[1:2, :] += jnp.sum(dm1, axis=0, keepdims=True)
        dya = lax.dot_general(dbr_a, wa_ref[...], NT_DIMS, preferred_element_type=f32)
        dyb2 = lax.dot_general(dbr_b, wb_ref[...], NT_DIMS, preferred_element_type=f32)
        do_a = dya * silu_a
        dga = dya * oa * (sa * (1.0 + ga * (1.0 - sa)))
        ones = _head_ones()
        delta_a = _seg_sum(do_a * oa, ones)
        dsk_ref[...] -= jnp.sum(delta_a * jnp.exp(sk_ref[...] - la_ref[...]), axis=0, keepdims=True)
        dybc = dyb2 * silu_b
        dgb = dyb2 * ybc * (sb * (1.0 + gb * (1.0 - sb)))
        dbar = _seg_sum(dybc * ybc, ones)
        dy_ref[...] = dy
        dyb_ref[...] = dyb
        dt_ref[:, 0:512] = dga.astype(bf16)
        dt_ref[:, 512:1024] = dgb.astype(bf16)
        dt_ref[:, 1024:2048] = dm0.astype(bf16)
        dt_ref[:, 2048:3072] = dm1.astype(bf16)
        doa_ref[...] = do_a.astype(bf16)
        dda_ref[...] = delta_a
        for k in range(3):
            dob_ref[:, k * 512:(k + 1) * 512] = alpha[k] * dybc
            ddb_ref[:, k * 512:(k + 1) * 512] = alpha[k] * dbar
        ya_ref[...] = yab
        yb_ref[...] = ybb
        mg_ref[...] = mgb
        dbra_ref[...] = dbr_a
        dbrb_ref[...] = dbr_b

    def rows(n, blk=0):
        return pl.BlockSpec((ts, n), lambda i: (i, blk))

    def whole(r, c):
        return pl.BlockSpec((r, c), lambda i: (0, 0))

    def gate_cols(n, col):
        return pl.BlockSpec((pl.Element(ts), pl.Element(n)), lambda i: (i * ts, NA + col))

    outs = [
        ((S, D), f32, rows(D)), ((S, D), bf16, rows(D)), ((S, NW), bf16, gate_cols(NT, 0)),
        ((S, 512), bf16, rows(512)), ((S, 512), f32, rows(512)),
        ((S, 1536), f32, rows(1536)), ((S, 1536), f32, rows(1536)),
        ((S, 512), bf16, rows(512)), ((S, 512), bf16, rows(512)), ((S, D), bf16, rows(D)),
        ((S, D), bf16, rows(D)), ((S, D), bf16, rows(D)),
        ((1, 1), f32, whole(1, 1)), ((2, D), f32, whole(2, D)), ((1, 512), f32, whole(1, 512)),
    ]
    return pl.pallas_call(
        body,
        grid=(S // ts,),
        in_specs=[
            rows(D), rows(D), rows(512), rows(512), rows(1536), rows(1536),
            gate_cols(512, 0), gate_cols(512, 512), gate_cols(D, 1024), gate_cols(D, 2048), whole(2, D),
            whole(512, D), whole(512, D), whole(D, D), whole(1, 512),
        ],
        out_specs=[o[2] for o in outs],
        out_shape=[jax.ShapeDtypeStruct(o[0], o[1]) for o in outs],
        compiler_params=_params(("arbitrary",)),
        name="tail",
    )(x, tgt, o_a, l_a, o_b, l_b, proj, proj, proj, proj, bm, w_a, w_b, w_o, sink_b)


def _norm_bwd(xv, dyv, gain, kind, ones):
    r = lax.rsqrt(_half_sums(xv * xv, ones) * (1.0 / HD) + EPS)
    yv = xv * r
    up = jnp.where(kind == 0, dyv * SCALE, dyv)
    u = up * gain
    dxv = r * (u - yv * (_half_sums(u * yv, ones) * (1.0 / HD)))
    dxv = jnp.where(kind == 2, dyv, dxv)
    dg = jnp.where(kind == 2, 0.0, jnp.sum(up * yv, axis=0, keepdims=True))
    return dxv, dg


def _post_b(dqkv, proj_a, gains, dproj):
    def body(d_ref, p_ref, g_ref, alias_ref, o_ref, dg_ref, nat):
        del alias_ref
        g = pl.program_id(0)
        j = pl.program_id(1)
        kind = j // 4
        gain = g_ref[...]
        ones = _head_ones()

        @pl.when(j % 4 == 0)
        def _():
            dg_ref[...] = jnp.zeros_like(dg_ref)

        for gi in range(3):
            d = GROUPS[gi + 1][1]

            @pl.when(g == gi)
            def _():
                seq = S // d
                for c in range(d):
                    for i in range(seq // PCHUNK):
                        src = c * seq + i * PCHUNK
                        if d == 1:
                            idx = slice(src, src + PCHUNK)
                        else:
                            idx = pl.ds(c + i * PCHUNK * d, PCHUNK, stride=d)
                        dxv, dg = _norm_bwd(p_ref[idx, :], d_ref[PAD + src:PAD + src + PCHUNK, :], gain, kind, ones)
                        nat[idx, :] = dxv
                        dg_ref[...] += dg

        for i in range(S // CHUNK):
            o_ref[i * CHUNK:(i + 1) * CHUNK, :] = nat[i * CHUNK:(i + 1) * CHUNK, :].astype(bf16)

    return pl.pallas_call(
        body,
        grid=(3, 12),
        in_specs=[
            pl.BlockSpec((None, None, None, SP, LANES), lambda g, j: (g, j // 4, j % 4, 0, 0)),
            pl.BlockSpec((S, LANES), lambda g, j: (0, _col_block(g + 1, j))),
            pl.BlockSpec((None, None, 1, LANES), lambda g, j: (g + 1, j // 4, 0, 0)),
            pl.BlockSpec(memory_space=pl.ANY),
        ],
        out_specs=[
            pl.BlockSpec((S, LANES), lambda g, j: (0, _col_block(g + 1, j))),
            pl.BlockSpec((None, None, 1, LANES), lambda g, j: (g, j // 4, 0, 0)),
        ],
        out_shape=[jax.ShapeDtypeStruct((S, NW), bf16), jax.ShapeDtypeStruct((3, 3, 1, LANES), f32)],
        scratch_shapes=[pltpu.VMEM((S, LANES), f32)],
        input_output_aliases={3: 0},
        compiler_params=_params(("arbitrary", "arbitrary")),
        name="post_b",
    )(dqkv, proj_a, gains, dproj)


def _post_a(dqkv, proj_a, gains, dproj):
    def body(q_ref, e_ref, p_ref, g_ref, alias_ref, o_ref, dg_ref):
        del alias_ref
        j = pl.program_id(0)
        kind = jnp.maximum(j - 3, 0)
        gain = g_ref[...]
        lo = _lo()
        ones = _head_ones()

        @pl.when((j == 0) | (j >= 4))
        def _():
            dg_ref[...] = jnp.zeros_like(dg_ref)

        for i in range(S // PCHUNK):
            r0 = i * PCHUNK
            rows = slice(PAD + r0, PAD + r0 + PCHUNK)
            t0 = e_ref[0, rows, :] + e_ref[1, rows, :]
            t1 = e_ref[2, rows, :] + e_ref[3, rows, :]
            folded = jnp.where(lo, t0 + pltpu.roll(t0, HD, 1), t1 + pltpu.roll(t1, HD, 1))
            dyv = jnp.where(kind == 0, q_ref[rows, :], folded)
            dxv, dg = _norm_bwd(p_ref[r0:r0 + PCHUNK, :], dyv, gain, kind, ones)
            o_ref[r0:r0 + PCHUNK, :] = dxv.astype(bf16)
            dg_ref[...] += dg

    return pl.pallas_call(
        body,
        grid=(6,),
        in_specs=[
            pl.BlockSpec((None, None, None, SP, LANES), lambda j: (0, 0, jnp.minimum(j, 3), 0, 0)),
            pl.BlockSpec((None, None, 4, SP, LANES), lambda j: (0, jnp.clip(j - 3, 1, 2), 0, 0, 0)),
            pl.BlockSpec((S, LANES), lambda j: (0, j)),
            pl.BlockSpec((None, None, 1, LANES), lambda j: (0, jnp.maximum(j - 3, 0), 0, 0)),
            pl.BlockSpec(memory_space=pl.ANY),
        ],
        out_specs=[
            pl.BlockSpec((S, LANES), lambda j: (0, j)),
            pl.BlockSpec((None, 1, LANES), lambda j: (jnp.maximum(j - 3, 0), 0, 0)),
        ],
        out_shape=[jax.ShapeDtypeStruct((S, NW), bf16), jax.ShapeDtypeStruct((3, 1, LANES), f32)],
        input_output_aliases={4: 0},
        compiler_params=_params(("arbitrary",)),
        name="post_a",
    )(dqkv, dqkv, proj_a, gains, dproj)


def _dh_norm_bwd(dproj, w, x, rstd, gain, dy):
    ts = 1024
    nk = NW // TN

    def body(d_ref, w_ref, x_ref, r_ref, g_ref, dy_ref, gx_ref, dgn_ref, acc):
        i = pl.program_id(0)
        k = pl.program_id(1)

        @pl.when((i == 0) & (k == 0))
        def _():
            dgn_ref[...] = jnp.zeros_like(dgn_ref)

        @pl.when(k == 0)
        def _():
            acc[...] = jnp.zeros_like(acc)

        acc[...] += lax.dot_general(d_ref[...], w_ref[...], NT_DIMS, preferred_element_type=f32)

        @pl.when(k == nk - 1)
        def _():
            dh = acc[...]
            xh = x_ref[...] * r_ref[...]
            u = dh * g_ref[...]
            dx = r_ref[...] * (u - xh * jnp.mean(u * xh, axis=-1, keepdims=True))
            gx_ref[...] = dy_ref[...] + dx
            dgn_ref[...] += jnp.sum(dh * xh, axis=0, keepdims=True)

    return pl.pallas_call(
        body,
        grid=(S // ts, nk),
        in_specs=[
            pl.BlockSpec((ts, TN), lambda i, k: (i, k)),
            pl.BlockSpec((D, TN), lambda i, k: (0, k)),
            pl.BlockSpec((ts, D), lambda i, k: (i, 0)),
            pl.BlockSpec((ts, 1), lambda i, k: (i, 0)),
            pl.BlockSpec((1, D), lambda i, k: (0, 0)),
            pl.BlockSpec((ts, D), lambda i, k: (i, 0)),
        ],
        out_specs=[pl.BlockSpec((ts, D), lambda i, k: (i, 0)), pl.BlockSpec((1, D), lambda i, k: (0, 0))],
        out_shape=[jax.ShapeDtypeStruct((S, D), f32), jax.ShapeDtypeStruct((1, D), f32)],
        scratch_shapes=[pltpu.VMEM((ts, D), f32)],
        compiler_params=_params(("arbitrary", "arbitrary"), vmem_mib=56),
        name="dh_norm_bwd",
    )(dproj, w, x, rstd, gain, dy)


def _dw_in(hbt, dproj):
    tk = 1024
    win = WSH + 96

    def body(a_ref, b_ref, o_ref, acc):
        j = pl.program_id(0)
        k = pl.program_id(1)

        @pl.when(k == 0)
        def _():
            acc[...] = jnp.zeros_like(acc)

        for jj in range(NDEV):
            off = (WSH * jj) % LANES

            @pl.when(j == jj)
            def _():
                acc[...] += jnp.dot(a_ref[...], b_ref[:, off:off + WSH], preferred_element_type=f32)

        @pl.when(k == S // tk - 1)
        def _():
            o_ref[...] = acc[...].astype(bf16)

    return pl.pallas_call(
        body,
        grid=(NDEV, S // tk),
        in_specs=[
            pl.BlockSpec((D, tk), lambda j, k: (0, k)),
            pl.BlockSpec((pl.Element(tk), pl.Element(win)), lambda j, k: (k * tk, (WSH * j) // LANES * LANES)),
        ],
        out_specs=pl.BlockSpec((None, D, WSH), lambda j, k: (j, 0, 0)),
        out_shape=jax.ShapeDtypeStruct((NDEV, D, WSH), bf16),
        scratch_shapes=[pltpu.VMEM((D, WSH), f32)],
        compiler_params=_params(("arbitrary", "arbitrary")),
        name="dw_in",
    )(hbt, dproj)


def _matmul_tn(a, b, name):
    m, n = a.shape[1], b.shape[1]
    tn = TN if n % TN == 0 else 512
    tk = 512

    def body(a_ref, b_ref, o_ref):
        @pl.when(pl.program_id(1) == 0)
        def _():
            o_ref[...] = jnp.zeros_like(o_ref)

        o_ref[...] += lax.dot_general(a_ref[...], b_ref[...], TN_DIMS, preferred_element_type=f32)

    return pl.pallas_call(
        body,
        grid=(n // tn, S // tk),
        in_specs=[pl.BlockSpec((tk, m), lambda j, k: (k, 0)), pl.BlockSpec((tk, tn), lambda j, k: (k, j))],
        out_specs=pl.BlockSpec((m, tn), lambda j, k: (0, j)),
        out_shape=jax.ShapeDtypeStruct((m, n), f32),
        compiler_params=_params(("arbitrary", "arbitrary")),
        name=name,
    )(a, b)


def _exchange(scatter, gather, name):
    arrs = list(scatter) + list(gather)
    n = len(arrs)
    ns = len(scatter)

    def body(*refs):
        ins, outs = refs[:n], refs[n:2 * n]
        send_sems, recv_sems, local_sems = refs[2 * n:]
        x, y, c = lax.axis_index("x"), lax.axis_index("y"), lax.axis_index("c")
        me = 4 * x + 2 * y + c
        local, remote = [], []
        for a in range(n):
            lc = pltpu.make_async_copy(ins[a].at[me] if a < ns else ins[a], outs[a].at[me], local_sems.at[a])
            lc.start()
            local.append(lc)
            for r in range(1, NDEV):
                px = 1 - x if r & 4 else x
                py = 1 - y if r & 2 else y
                pc = 1 - c if r & 1 else c
                cp = pltpu.make_async_remote_copy(
                    src_ref=ins[a].at[4 * px + 2 * py + pc] if a < ns else ins[a],
                    dst_ref=outs[a].at[me],
                    send_sem=send_sems.at[a, r - 1],
                    recv_sem=recv_sems.at[a, r - 1],
                    device_id=(px, py, pc),
                    device_id_type=pl.DeviceIdType.MESH,
                )
                cp.start()
                remote.append(cp)
        for cp in remote:
            cp.wait_recv()
        for cp in remote:
            cp.wait_send()
        for lc in local:
            lc.wait()

    out_shape = [jax.ShapeDtypeStruct(a.shape if i < ns else (NDEV,) + a.shape, a.dtype) for i, a in enumerate(arrs)]
    return pl.pallas_call(
        body,
        in_specs=[pl.BlockSpec(memory_space=pl.ANY)] * n,
        out_specs=[pl.BlockSpec(memory_space=pl.ANY)] * n,
        out_shape=out_shape,
        scratch_shapes=[
            pltpu.SemaphoreType.DMA((n, NDEV - 1)),
            pltpu.SemaphoreType.DMA((n, NDEV - 1)),
            pltpu.SemaphoreType.DMA((n,)),
        ],
        compiler_params=pltpu.CompilerParams(has_side_effects=True),
        name=name,
    )(*arrs)


def _gather_two_level(arrs, name):
    n = len(arrs)

    def body(*refs):
        ins, outs = refs[:n], refs[n:2 * n]
        send_sems, recv_sems, local_sems = refs[2 * n:]
        x, y, c = lax.axis_index("x"), lax.axis_index("y"), lax.axis_index("c")
        me, sibling = (x, y, c), (x, y, 1 - c)
        chips = [(1 - x, y), (x, 1 - y), (1 - x, 1 - y)]

        def copy(a, k, block, to, src=None):
            slot = outs[a].at[4 * block[0] + 2 * block[1] + block[2]]
            return pltpu.make_async_remote_copy(
                src_ref=slot if src is None else src, dst_ref=slot, send_sem=send_sems.at[a, k],
                recv_sem=recv_sems.at[a, k], device_id=to, device_id_type=pl.DeviceIdType.MESH)

        mine, first, passed = [], [], []
        for a in range(n):
            lc = pltpu.make_async_copy(ins[a], outs[a].at[4 * x + 2 * y + c], local_sems.at[a])
            lc.start()
            mine.append(lc)
            first.append(copy(a, 0, me, sibling, src=ins[a]))
            first += [copy(a, 1 + j, me, (*chip, c), src=ins[a]) for j, chip in enumerate(chips)]
        for cp in first:
            cp.start()
        for j, chip in enumerate(chips):
            for a in range(n):
                copy(a, 1 + j, (*chip, c), me).wait_recv()
                fwd = copy(a, 4 + j, (*chip, c), sibling)
                fwd.start()
                passed.append(fwd)
        for a in range(n):
            copy(a, 0, sibling, me).wait_recv()
        for j, chip in enumerate(chips):
            for a in range(n):
                copy(a, 4 + j, (*chip, 1 - c), me).wait_recv()
        for cp in first + passed:
            cp.wait_send()
        for lc in mine:
            lc.wait()

    return pl.pallas_call(
        body,
        in_specs=[pl.BlockSpec(memory_space=pl.ANY)] * n,
        out_specs=[pl.BlockSpec(memory_space=pl.ANY)] * n,
        out_shape=[jax.ShapeDtypeStruct((NDEV,) + a.shape, a.dtype) for a in arrs],
        scratch_shapes=[
            pltpu.SemaphoreType.DMA((n, NDEV - 1)),
            pltpu.SemaphoreType.DMA((n, NDEV - 1)),
            pltpu.SemaphoreType.DMA((n,)),
        ],
        compiler_params=pltpu.CompilerParams(has_side_effects=True),
        name=name,
    )(*arrs)


_HBM = pl.BlockSpec(memory_space=pltpu.HBM)
_SEM = pl.BlockSpec(memory_space=pltpu.SEMAPHORE)
_EFFECT = pltpu.SideEffectType.DATAFLOW_SIDE_EFFECTING


def _sibling_exchange(g, name):
    def body(in_ref, out_ref, send_sems, recv_sems):
        x, y, c = lax.axis_index("x"), lax.axis_index("y"), lax.axis_index("c")
        copies = []
        for q in range(4):
            cp = pltpu.make_async_remote_copy(
                src_ref=in_ref.at[2 * q + (1 - c)], dst_ref=out_ref.at[q], send_sem=send_sems.at[q],
                recv_sem=recv_sems.at[q], device_id=(x, y, 1 - c), device_id_type=pl.DeviceIdType.MESH)
            cp.start()
            copies.append(cp)
        for cp in copies:
            cp.wait_recv()
        for cp in copies:
            cp.wait_send()

    return pl.pallas_call(
        body,
        in_specs=[pl.BlockSpec(memory_space=pl.ANY)],
        out_specs=pl.BlockSpec(memory_space=pl.ANY),
        out_shape=jax.ShapeDtypeStruct((4,) + g.shape[1:], g.dtype),
        scratch_shapes=[pltpu.SemaphoreType.DMA((4,)), pltpu.SemaphoreType.DMA((4,))],
        compiler_params=pltpu.CompilerParams(has_side_effects=True),
        name=name,
    )(g)


def _pair_sum(g, r, core, name):
    _, rows, cols = g.shape
    tr = 256

    def body(c_ref, g_ref, r_ref, o_ref):
        del c_ref
        o_ref[...] = (g_ref[...].astype(f32) + r_ref[...].astype(f32)).astype(bf16)

    return pl.pallas_call(
        body,
        grid_spec=pltpu.PrefetchScalarGridSpec(
            num_scalar_prefetch=1,
            grid=(4, rows // tr),
            in_specs=[pl.BlockSpec((None, tr, cols), lambda q, i, c_ref: (2 * q + c_ref[0], i, 0)),
                      pl.BlockSpec((None, tr, cols), lambda q, i, c_ref: (q, i, 0))],
            out_specs=pl.BlockSpec((None, tr, cols), lambda q, i, c_ref: (q, i, 0)),
        ),
        out_shape=jax.ShapeDtypeStruct((4, rows, cols), bf16),
        compiler_params=_params(("arbitrary", "arbitrary")),
        name=name,
    )(core, g, r)


def _scatter_start(chip_arrs, all_arrs, name):
    arrs = list(chip_arrs) + list(all_arrs)
    n, nc = len(arrs), len(chip_arrs)
    lands = [lax.empty(((3 if i < nc else NDEV - 1),) + a.shape[1:], a.dtype) for i, a in enumerate(arrs)]

    def body(*refs):
        src, land = refs[:n], refs[n:2 * n]
        send_sems, recv_sems = refs[2 * n:3 * n], refs[3 * n:4 * n]
        token = refs[6 * n]
        x, y, c = lax.axis_index("x"), lax.axis_index("y"), lax.axis_index("c")
        for a in range(n):
            for r in range(1, 4 if a < nc else NDEV):
                if a < nc:
                    px, py, pc = (1 - x if r & 2 else x), (1 - y if r & 1 else y), c
                    block = 2 * px + py
                else:
                    px, py, pc = (1 - x if r & 4 else x), (1 - y if r & 2 else y), (1 - c if r & 1 else c)
                    block = 4 * px + 2 * py + pc
                pltpu.make_async_remote_copy(
                    src_ref=src[a].at[block], dst_ref=land[a].at[r - 1], send_sem=send_sems[a],
                    recv_sem=recv_sems[a], device_id=(px, py, pc), device_id_type=pl.DeviceIdType.MESH).start()
        token[...] = jnp.zeros_like(token)

    hbm = [pltpu.HBM(a.shape, a.dtype) for a in arrs + lands]
    ops = [pltpu.with_memory_space_constraint(a, pltpu.HBM) for a in arrs + lands]
    outs = pl.pallas_call(
        body,
        out_shape=tuple([pltpu.SemaphoreType.DMA(())] * (2 * n) + hbm + [jax.ShapeDtypeStruct((8, LANES), f32)]),
        in_specs=[_HBM] * (2 * n),
        out_specs=tuple([_SEM] * (2 * n) + [_HBM] * (2 * n) + [pl.BlockSpec(memory_space=pltpu.VMEM)]),
        input_output_aliases={i: 2 * n + i for i in range(2 * n)},
        compiler_params=pltpu.CompilerParams(has_side_effects=_EFFECT),
        name=name,
    )(*ops)
    return outs[:n], outs[n:2 * n], outs[2 * n:3 * n], outs[3 * n:4 * n], outs[4 * n]


def _scatter_wait(send_sems, recv_sems, srcs, lands, after, name):
    n = len(srcs)

    def body(*refs):
        land = refs[n:2 * n]
        ssem, rsem = refs[2 * n:3 * n], refs[3 * n:4 * n]
        x, y, c = lax.axis_index("x"), lax.axis_index("y"), lax.axis_index("c")
        for a in range(n):
            done = pltpu.make_async_remote_copy(
                src_ref=land[a], dst_ref=land[a], send_sem=ssem[a], recv_sem=rsem[a], device_id=(x, y, c),
                device_id_type=pl.DeviceIdType.MESH)
            done.wait_send()
            done.wait_recv()

    hbm = [pltpu.HBM(a.shape, a.dtype) for a in list(srcs) + list(lands)]
    outs = pl.pallas_call(
        body,
        out_shape=tuple(hbm),
        in_specs=[_HBM] * (2 * n) + [_SEM] * (2 * n) + [pl.BlockSpec(memory_space=pl.ANY)],
        out_specs=tuple([_HBM] * (2 * n)),
        input_output_aliases={i: i for i in range(2 * n)},
        compiler_params=pltpu.CompilerParams(has_side_effects=_EFFECT),
        name=name,
    )(*srcs, *lands, *send_sems, *recv_sems, after)
    return outs[:n], outs[n:]


def _adam_update(g, w_ref, m_ref, v_ref, g_ref, d_ref, nm_ref, nv_ref):
    mm = ADAM_B1 * m_ref[...] + (1.0 - ADAM_B1) * g
    vv = ADAM_B2 * v_ref[...] + (1.0 - ADAM_B2) * (g * g)
    m_hat = mm / (1.0 - ADAM_B1 ** ADAM_STEP)
    v_hat = vv / (1.0 - ADAM_B2 ** ADAM_STEP)
    g_ref[...] = g
    d_ref[...] = -ADAM_LR * (m_hat / (jnp.sqrt(v_hat) + ADAM_EPS) + ADAM_WD * w_ref[...])
    nm_ref[...] = mm
    nv_ref[...] = vv


def _adamw_own(w, own, own_idx, slots, m, v, name):
    r, c = w.shape[-2:]
    tr = 128 if r % 128 == 0 else r
    k = slots.shape[0]

    def body(i_ref, w_ref, o_ref, s_ref, m_ref, v_ref, g_ref, d_ref, nm_ref, nv_ref):
        del i_ref
        g = o_ref[...].astype(f32)
        for j in range(k):
            g = g + s_ref[j].astype(f32)
        _adam_update(g, w_ref, m_ref, v_ref, g_ref, d_ref, nm_ref, nv_ref)

    blk = pl.BlockSpec((None, tr, c), lambda i, ix: (0, i, 0))
    return pl.pallas_call(
        body,
        grid_spec=pltpu.PrefetchScalarGridSpec(
            num_scalar_prefetch=1,
            grid=(r // tr,),
            in_specs=[blk, pl.BlockSpec((None, tr, c), lambda i, ix: (ix[0], i, 0)),
                      pl.BlockSpec((k, tr, c), lambda i, ix: (0, i, 0)), blk, blk],
            out_specs=[blk] * 4,
        ),
        out_shape=[jax.ShapeDtypeStruct(w.shape, f32)] * 4,
        compiler_params=_params(("arbitrary",)),
        name=name,
    )(own_idx, w, own, slots, m, v)


def _adamw(w, slots, m, v, name):
    r, c = w.shape[-2:]
    tr = 128 if r % 128 == 0 else r

    def body(w_ref, s_ref, m_ref, v_ref, g_ref, d_ref, nm_ref, nv_ref):
        g = s_ref[0].astype(f32)
        for k in range(1, NDEV):
            g = g + s_ref[k].astype(f32)
        _adam_update(g, w_ref, m_ref, v_ref, g_ref, d_ref, nm_ref, nv_ref)

    if w.ndim == 3:
        blk = pl.BlockSpec((None, tr, c), lambda i: (0, i, 0))
    else:
        blk = pl.BlockSpec((tr, c), lambda i: (i, 0))
    return pl.pallas_call(
        body,
        grid=(r // tr,),
        in_specs=[blk, pl.BlockSpec((NDEV, tr, c), lambda i: (0, i, 0)), blk, blk],
        out_specs=[blk] * 4,
        out_shape=[jax.ShapeDtypeStruct(w.shape, f32)] * 4,
        compiler_params=_params(("arbitrary",)),
        name=name,
    )(w, slots, m, v)


def _local_step(x, tgt, norm_gain, w_shards, qn_a, kn_a, qn_b, kn_b, sink_a, rel_bias, w_a, w_b, b_merge, w_o,
                on_weight_grads=None):
    two = lambda t: jnp.concatenate([t, t], axis=-1).reshape(1, LANES)
    ones = jnp.ones((1, LANES), f32)
    gains = jnp.stack([
        jnp.stack([two(qn_a), two(kn_a), ones]),
        jnp.stack([two(qn_b), two(kn_b), ones]),
        jnp.stack([two(qn_b), two(kn_b), ones]),
        jnp.stack([two(qn_b), two(kn_b), ones]),
    ])
    buckets = [jnp.asarray(_bucket_np(blk, d)) for blk, d, _ in GROUPS]
    bias_a = _bias_expand(rel_bias, buckets[0], 0, "bias_expand_a")[None]
    bias_b = jnp.stack([_bias_expand(rel_bias, buckets[k], GROUPS[k][2], "bias_expand_b%d" % k) for k in (1, 2, 3)])

    hb, hbt, rstd = _rms(x, norm_gain)
    proj, w_in = _inproj(hb, w_shards)
    gl = _prep(proj, gains)
    o_a, l_a = _attn_fwd(gl, bias_a, sink_a.reshape(8), 0, 1, 128, True, "attn_fwd_a")
    o_bg, l_bg = _attn_fwd(gl, bias_b, None, 1, 3, 64, False, "attn_fwd_b")
    o_b, l_b = _b_to_natural(o_bg, l_bg)
    sink_b = jnp.repeat(sink_a.reshape(8), HD).reshape(1, 512)

    (dy, dyb, dproj, do_a, dd_a, do_b, dd_b, ya, yb, mg, dbr_a, dbr_b, loss, dbm, dsk) = _tail(
        x, tgt, o_a, l_a, o_b, l_b, proj, b_merge, w_a, w_b, w_o, sink_b)

    dqkv_a, dbk_a = _attn_bwd(gl, bias_a, buckets[0][None], do_a, l_a, dd_a, 0, 1, 128, True, "attn_bwd_a")
    do_bg, dd_bg = _b_from_natural(do_b, dd_b)
    dqkv_b, dbk_b = _attn_bwd(gl, bias_b, jnp.stack(buckets[1:]), do_bg, l_bg, dd_bg, 1, 3, 64, False, "attn_bwd_b")

    dproj, dg_a = _post_a(dqkv_a, proj, gains, dproj)
    dproj, dg_b = _post_b(dqkv_b, proj, gains, dproj)

    dw_in = _dw_in(hbt, dproj)
    dw_o = _matmul_tn(mg, dyb, "dw_out")
    dw_a = _matmul_tn(ya, dbr_a, "dw_branch_a")
    dw_b = _matmul_tn(yb, dbr_b, "dw_branch_b")
    token = jnp.zeros((), f32) if on_weight_grads is None else on_weight_grads(
        dict(w_in=dw_in, w_branch_a=dw_a, w_branch_b=dw_b, b_merge=dbm, w_out=dw_o))
    grad_x, d_norm_gain = _dh_norm_bwd(dproj, w_in, x, rstd, norm_gain + token, dy)

    fold = lambda t: t[..., :HD] + t[..., HD:]
    d_qn_a = fold(dg_a[0, 0])
    d_kn_a = fold(dg_a[1, 0])
    d_qn_b = fold(dg_b[:, 0, 0].sum(axis=0))
    d_kn_b = fold(dg_b[:, 1, 0].sum(axis=0))
    d_sink = dsk.reshape(8, HD)[:, 0]
    red = jnp.concatenate([dbk_a, dbk_b])
    d_rel = red[:, :, 0, :32].reshape(32, 32).T
    return dict(loss=loss, grad_x=grad_x, norm_gain=d_norm_gain, w_in=dw_in, q_norm_a=d_qn_a, k_norm_a=d_kn_a,
                q_norm_b=d_qn_b, k_norm_b=d_kn_b, sink_a=d_sink, rel_bias=d_rel, w_branch_a=dw_a, w_branch_b=dw_b,
                b_merge=dbm, w_out=dw_o)


SMALL = (("norm_gain", D), ("q_norm_a", HD), ("k_norm_a", HD), ("q_norm_b", HD), ("k_norm_b", HD), ("sink_a", 8),
         ("rel_bias", 1024))
SMALL_PAD = 2432


SMALL_USED = sum(sz for _, sz in SMALL)


def _pack_small(parts, loss=None):
    tail = jnp.zeros((SMALL_PAD - SMALL_USED,), f32)
    if loss is not None:
        tail = tail.at[0].set(loss.reshape(()))
    return jnp.concatenate([parts[n].reshape(-1) for n, _ in SMALL] + [tail]).reshape(1, SMALL_PAD)


def _unpack_small(flat, shapes):
    out, off = {}, 0
    for n, sz in SMALL:
        out[n] = flat[0, off:off + sz].reshape(shapes[n])
        off += sz
    return out


def kernel(x, norm_gain, w_in, q_norm_a, k_norm_a, q_norm_b, k_norm_b, sink_a, rel_bias, w_branch_a, w_branch_b, b_merge, w_out, loss_target, m_norm_gain, m_w_in, m_q_norm_a, m_k_norm_a, m_q_norm_b, m_k_norm_b, m_sink_a, m_rel_bias, m_w_branch_a, m_w_branch_b, m_b_merge, m_w_out, v_norm_gain, v_w_in, v_q_norm_a, v_k_norm_a, v_q_norm_b, v_k_norm_b, v_sink_a, v_rel_bias, v_w_branch_a, v_w_branch_b, v_b_merge, v_w_out):
    wsh = NW // NDEV
    csh = D // NDEV
    g_in, g_a, g_b, g_o, g_bm = _gather_two_level(
        [w_in[0].astype(bf16), w_branch_a[0].astype(bf16), w_branch_b[0].astype(bf16), w_out[0].astype(bf16),
         b_merge[0]], "gather_weights")
    w_a_full = g_a.transpose(1, 0, 2).reshape(512, D)
    w_b_full = g_b.transpose(1, 0, 2).reshape(512, D)
    w_o_full = g_o.reshape(D, D)
    bm_full = g_bm.transpose(1, 0, 2).reshape(2, D)

    pending = {}
    core = lax.axis_index("c").astype(jnp.int32).reshape(1)
    chip = (2 * lax.axis_index("x") + lax.axis_index("y")).astype(jnp.int32).reshape(1)
    me = (2 * chip + core).astype(jnp.int32)

    def start_exchange(gw):
        from_sibling = _sibling_exchange(gw["w_in"], "grad_sibling_exchange")
        chip_sums = _pair_sum(gw["w_in"], from_sibling, core, "grad_pair_sum")
        blocks = [gw["w_branch_a"].reshape(512, NDEV, csh).transpose(1, 0, 2).astype(bf16),
                  gw["w_branch_b"].reshape(512, NDEV, csh).transpose(1, 0, 2).astype(bf16),
                  gw["w_out"].reshape(NDEV, csh, D).astype(bf16),
                  gw["b_merge"].reshape(2, NDEV, csh).transpose(1, 0, 2)]
        pending["started"] = _scatter_start([chip_sums], blocks, "scatter_grads_start")
        return pending["started"][4][0, 0]

    loc = _local_step(x[0], loss_target[0], norm_gain, g_in, q_norm_a, k_norm_a, q_norm_b, k_norm_b, sink_a,
                      rel_bias, w_a_full, w_b_full, bm_full, w_o_full, on_weight_grads=start_exchange)

    small_shapes = dict(norm_gain=(1, D), q_norm_a=(1, HD), k_norm_a=(1, HD), q_norm_b=(1, HD), k_norm_b=(1, HD),
                        sink_a=(1, 8), rel_bias=(32, 32))
    (r_small,) = _exchange([], [_pack_small(loc, loc["loss"])], "gather_small_grads")
    send_sems, recv_sems, srcs, lands, _ = pending["started"]
    (s_in, s_a, s_b, s_o, s_bm), (r_in, r_a, r_b, r_o, r_bm) = _scatter_wait(
        send_sems, recv_sems, srcs, lands, r_small, "scatter_grads_wait")

    given = dict(norm_gain=norm_gain, q_norm_a=q_norm_a, k_norm_a=k_norm_a, q_norm_b=q_norm_b, k_norm_b=k_norm_b,
                 sink_a=sink_a, rel_bias=rel_bias)
    m_small = dict(norm_gain=m_norm_gain, q_norm_a=m_q_norm_a, k_norm_a=m_k_norm_a, q_norm_b=m_q_norm_b,
                   k_norm_b=m_k_norm_b, sink_a=m_sink_a, rel_bias=m_rel_bias)
    v_small = dict(norm_gain=v_norm_gain, q_norm_a=v_q_norm_a, k_norm_a=v_k_norm_a, q_norm_b=v_q_norm_b,
                   k_norm_b=v_k_norm_b, sink_a=v_sink_a, rel_bias=v_rel_bias)
    res = {
        "small": _adamw(_pack_small(given), r_small, _pack_small(m_small), _pack_small(v_small), "adamw_small"),
        "w_in": _adamw_own(w_in, s_in, chip, r_in, m_w_in, v_w_in, "adamw_w_in"),
        "w_branch_a": _adamw_own(w_branch_a, s_a, me, r_a, m_w_branch_a, v_w_branch_a, "adamw_w_branch_a"),
        "w_branch_b": _adamw_own(w_branch_b, s_b, me, r_b, m_w_branch_b, v_w_branch_b, "adamw_w_branch_b"),
        "b_merge": _adamw_own(b_merge, s_bm, me, r_bm, m_b_merge, v_b_merge, "adamw_b_merge"),
        "w_out": _adamw_own(w_out, s_o, me, r_o, m_w_out, v_w_out, "adamw_w_out"),
    }
    order = ["norm_gain", "w_in", "q_norm_a", "k_norm_a", "q_norm_b", "k_norm_b", "sink_a", "rel_bias", "w_branch_a",
             "w_branch_b", "b_merge", "w_out"]
    outs = []
    for k in range(4):
        small = _unpack_small(res["small"][k], small_shapes)
        for n in order:
            outs.append(small[n] if n in small else res[n][k])
    loss = res["small"][0][0, SMALL_USED]
    return (loss, loc["grad_x"][None], *outs)
```

```python
import math

import numpy as np
import jax
import jax.numpy as jnp
from jax import lax
from jax.experimental import pallas as pl
from jax.experimental.pallas import tpu as pltpu

f32 = jnp.float32
bf16 = jnp.bfloat16

S = 4096
D = 1024
NA = 5376
NT = 3072
NW = NA + NT
WSH = NW // 8
HD = 64
LANES = 128
EPS = 1e-6
NEG = -1e30
SCALE = HD ** -0.5
TQ = 128
PAD = 128
SP = S + 2 * PAD
NDEV = 8
GROUPS = ((128, 1, 0), (64, 1, 8), (64, 4, 16), (64, 16, 24))
CHUNK = 256
PCHUNK = 128
RC = 64
TN = 768

ADAM_LR, ADAM_B1, ADAM_B2, ADAM_EPS, ADAM_WD, ADAM_STEP = 0.001, 0.9, 0.999, 1e-08, 0.01, 10

MIB = 1024 * 1024
NT_DIMS = (((1,), (1,)), ((), ()))
TN_DIMS = (((0,), (0,)), ((), ()))


def _params(sem=None, vmem_mib=48):
    return pltpu.CompilerParams(dimension_semantics=sem, vmem_limit_bytes=vmem_mib * MIB)


def _lo():
    return lax.broadcasted_iota(jnp.int32, (1, LANES), 1) < HD


def _head_ones():
    r = lax.broadcasted_iota(jnp.int32, (LANES, LANES), 0) // HD
    c = lax.broadcasted_iota(jnp.int32, (LANES, LANES), 1) // HD
    return jnp.where(r == c, 1.0, 0.0).astype(bf16)


def _half_sums(x, ones):
    hi = x.astype(bf16)
    mid = (x - hi.astype(f32)).astype(bf16)
    return (jnp.dot(hi, ones, preferred_element_type=f32) + jnp.dot(mid, ones, preferred_element_type=f32))


def _seg_sum(x, ones):
    outs = [_half_sums(x[:, b * LANES:(b + 1) * LANES], ones) for b in range(x.shape[1] // LANES)]
    return outs[0] if len(outs) == 1 else jnp.concatenate(outs, axis=1)


def _bucket_np(blk, stride):
    w = TQ + 2 * blk
    rel = np.arange(w)[None, :] - blk - np.arange(TQ)[:, None]
    band = np.abs(rel) <= blk
    r = rel * stride
    n = np.abs(r)
    nf = np.maximum(n, 8).astype(np.float32)
    large = 8 + (np.log(nf / np.float32(8)) / np.float32(math.log(128.0)) * np.float32(8)).astype(np.int32)
    large = np.minimum(large, 15)
    b = (r > 0).astype(np.int32) * 16 + np.where(n < 8, n, large)
    return np.where(band, b, -1).astype(np.int32)


def _rms(x, gain):
    ts = 512

    def body(x_ref, g_ref, h_ref, ht_ref, r_ref):
        xv = x_ref[...]
        r = lax.rsqrt(jnp.mean(xv * xv, axis=-1, keepdims=True) + EPS)
        h = (xv * r) * g_ref[...]
        h_ref[...] = h.astype(bf16)
        ht_ref[...] = h.T.astype(bf16)
        r_ref[...] = r

    return pl.pallas_call(
        body,
        grid=(S // ts,),
        in_specs=[pl.BlockSpec((ts, D), lambda i: (i, 0)), pl.BlockSpec((1, D), lambda i: (0, 0))],
        out_specs=[pl.BlockSpec((ts, D), lambda i: (i, 0)), pl.BlockSpec((D, ts), lambda i: (0, i)),
                   pl.BlockSpec((ts, 1), lambda i: (i, 0))],
        out_shape=[jax.ShapeDtypeStruct((S, D), bf16), jax.ShapeDtypeStruct((D, S), bf16),
                   jax.ShapeDtypeStruct((S, 1), f32)],
        compiler_params=_params(("arbitrary",)),
        name="rms",
    )(x, gain)


def _assemble_w(w_shards):
    def body(wa_ref, wb_ref, o_ref):
        n = pl.program_id(0)
        for nn in range(NW // TN):
            j0 = (TN * nn) // WSH
            a = TN * nn - WSH * j0
            len1 = min(TN, WSH - a)

            @pl.when(n == nn)
            def _():
                o_ref[:, 0:len1] = wa_ref[:, a:a + len1]
                if len1 < TN:
                    o_ref[:, len1:TN] = wb_ref[:, 0:TN - len1]

    return pl.pallas_call(
        body,
        grid=(NW // TN,),
        in_specs=[
            pl.BlockSpec((None, D, WSH), lambda n: ((TN * n) // WSH, 0, 0)),
            pl.BlockSpec((None, D, WSH), lambda n: (jnp.minimum((TN * n) // WSH + 1, NDEV - 1), 0, 0)),
        ],
        out_specs=pl.BlockSpec((D, TN), lambda n: (0, n)),
        out_shape=jax.ShapeDtypeStruct((D, NW), bf16),
        compiler_params=_params(("arbitrary",)),
        name="assemble_w",
    )(w_shards, w_shards)


def _inproj(hb, w):
    ts = 2048

    def body(h_ref, w_ref, p_ref):
        p_ref[...] = jnp.dot(h_ref[...], w_ref[...], preferred_element_type=f32)

    return pl.pallas_call(
        body,
        grid=(S // ts, NW // TN),
        in_specs=[pl.BlockSpec((ts, D), lambda i, n: (i, 0)), pl.BlockSpec((D, TN), lambda i, n: (0, n))],
        out_specs=pl.BlockSpec((ts, TN), lambda i, n: (i, n)),
        out_shape=jax.ShapeDtypeStruct((S, NW), f32),
        compiler_params=_params(("arbitrary", "arbitrary")),
        name="inproj",
    )(hb, w)


def _bias_expand(table, bucket, c0, name):
    tq, w = bucket.shape

    def body(tab_ref, bk_ref, o_ref):
        h = pl.program_id(0)
        bk = bk_ref[...]

        def step(b, acc):
            return jnp.where(bk == b, tab_ref[b, c0 + h], acc)

        o_ref[...] = lax.fori_loop(0, 32, step, jnp.full((tq, w), NEG, f32))

    return pl.pallas_call(
        body,
        grid=(8,),
        in_specs=[pl.BlockSpec(memory_space=pltpu.SMEM), pl.BlockSpec((tq, w), lambda h: (0, 0))],
        out_specs=pl.BlockSpec((None, tq, w), lambda h: (h, 0, 0)),
        out_shape=jax.ShapeDtypeStruct((8, tq, w), f32),
        compiler_params=_params(("arbitrary",)),
        name=name,
    )(table, bucket)


def _col_block(g, j):
    kind = j // 4
    hp = j % 4
    a = jnp.where(kind == 0, hp, 3 + kind)
    b = 6 + 12 * kind + 4 * (g - 1) + hp
    return jnp.where(g == 0, a, b)


def _prep(proj_a, gains):
    def body(p_ref, g_ref, o_ref):
        g = pl.program_id(0)
        j = pl.program_id(1)
        kind = j // 4
        lo = _lo()
        ones = _head_ones()
        half = jnp.where(lo, 0, 1)
        take = (kind == 0) | (half == (j % 4) // 2)
        gain = g_ref[...]
        o_ref[0:PAD, :] = jnp.zeros((PAD, LANES), bf16)
        o_ref[PAD + S:SP, :] = jnp.zeros((PAD, LANES), bf16)

        def norm_store(xv, dst, dup):
            if dup:
                xv = jnp.where(take, xv, pltpu.roll(xv, HD, 1))
            r = lax.rsqrt(_half_sums(xv * xv, ones) * (1.0 / HD) + EPS)
            r = jnp.where(kind == 2, 1.0, r)
            yv = (xv * r) * gain
            yv = jnp.where(kind == 0, yv * SCALE, yv)
            o_ref[PAD + dst:PAD + dst + CHUNK, :] = yv.astype(bf16)

        for gi, (_, d, _) in enumerate(GROUPS):
            @pl.when(g == gi)
            def _():
                seq = S // d
                for c in range(d):
                    for i in range(seq // CHUNK):
                        if d == 1:
                            xv = p_ref[i * CHUNK:(i + 1) * CHUNK, :]
                        else:
                            xv = p_ref[pl.ds(c + i * CHUNK * d, CHUNK, stride=d), :]
                        norm_store(xv, c * seq + i * CHUNK, gi == 0)

    return pl.pallas_call(
        body,
        grid=(4, 12),
        in_specs=[
            pl.BlockSpec((S, LANES), lambda g, j: (0, _col_block(g, j))),
            pl.BlockSpec((None, None, 1, LANES), lambda g, j: (g, j // 4, 0, 0)),
        ],
        out_specs=pl.BlockSpec((None, None, SP, LANES), lambda g, j: (g, j, 0, 0)),
        out_shape=jax.ShapeDtypeStruct((4, 12, SP, LANES), bf16),
        compiler_params=_params(("arbitrary", "arbitrary")),
        name="prep",
    )(proj_a, gains)


def _token_rows(t, r0, n, d):
    if d == 1:
        return pl.ds(pl.multiple_of(t * TQ, TQ) + r0, n)
    per = S // d // TQ
    return pl.ds(((t % per) * TQ + r0) * d + t // per, n, stride=d)


def _stack_heads(t, lo):
    z = jnp.zeros_like(t)
    return jnp.concatenate([jnp.where(lo, t, z), jnp.where(lo, z, t)], axis=0)


def _unstack_heads(t2, lo):
    return jnp.where(lo, t2[:TQ], t2[TQ:])


def _attn_fwd(gl, bias, sink, g, blk, d, name):
    w = TQ + 2 * blk
    seq = S // d
    use_sink = sink is not None

    def body(*refs):
        if use_sink:
            sink_ref, q_ref, k_ref, v_ref, b_ref, o_ref, l_ref, s0, s1, p0, p1, lse_scr = refs
        else:
            q_ref, k_ref, v_ref, b_ref, o_ref, l_ref, s0, s1, p0, p1, lse_scr = refs
        hp = pl.program_id(0)
        lo = _lo()
        mi = lax.broadcasted_iota(jnp.int32, (1, w), 1)
        s_bufs, p_bufs = (s0, s1), (p0, p1)

        def scores(t, slot):
            f0 = pl.multiple_of(t * TQ, TQ)
            q2 = _stack_heads(q_ref[pl.ds(PAD + f0, TQ), :], lo)
            kw = k_ref[pl.ds(PAD - blk + f0, w), :]
            s_bufs[slot][...] = lax.dot_general(q2, kw, NT_DIMS, preferred_element_type=f32)

        def softmax(t, slot):
            f0 = pl.multiple_of(t * TQ, TQ)
            m0 = jnp.bitwise_and(f0, seq - 1)
            inside = (mi >= blk - m0) & (mi < seq + blk - m0)
            for h in range(2):
                for r in range(TQ // RC):
                    rows = slice(h * TQ + r * RC, h * TQ + (r + 1) * RC)
                    logit = jnp.where(inside, s_bufs[slot][rows, :] + b_ref[h, r * RC:(r + 1) * RC, :], NEG)
                    m = jnp.max(logit, axis=1, keepdims=True)
                    e = jnp.exp(logit - m)
                    lse = m + jnp.log(jnp.sum(e, axis=1, keepdims=True))
                    if use_sink:
                        sk = sink_ref[2 * hp + h]
                        mx = jnp.maximum(lse, sk)
                        lse = mx + jnp.log(jnp.exp(lse - mx) + jnp.exp(sk - mx))
                    p_bufs[slot][rows, :] = (e * jnp.exp(m - lse)).astype(bf16)
                    lse_scr[rows, :] = jnp.broadcast_to(lse, (RC, LANES))
            l_ref[_token_rows(t, 0, TQ, d), :] = jnp.where(lo, lse_scr[0:TQ, :], lse_scr[TQ:2 * TQ, :])

        def values(t, slot):
            f0 = pl.multiple_of(t * TQ, TQ)
            vw = v_ref[pl.ds(PAD - blk + f0, w), :]
            o2 = jnp.dot(p_bufs[slot][...], vw, preferred_element_type=f32)
            o_ref[_token_rows(t, 0, TQ, d), :] = _unstack_heads(o2, lo)

        nt = S // TQ
        scores(0, 0)
        scores(1, 1)
        softmax(0, 0)

        def pair(k, carry):
            t = 2 * k + 2
            scores(t, 0)
            softmax(t - 1, 1)
            values(t - 2, 0)
            scores(t + 1, 1)
            softmax(t, 0)
            values(t - 1, 1)
            return carry

        lax.fori_loop(0, (nt - 2) // 2, pair, 0)
        softmax(nt - 1, 1)
        values(nt - 2, 0)
        values(nt - 1, 1)

    in_specs = [
        pl.BlockSpec((None, None, SP, LANES), lambda hp: (g, hp, 0, 0)),
        pl.BlockSpec((None, None, SP, LANES), lambda hp: (g, 4 + hp, 0, 0)),
        pl.BlockSpec((None, None, SP, LANES), lambda hp: (g, 8 + hp, 0, 0)),
        pl.BlockSpec((2, TQ, w), lambda hp: (hp, 0, 0)),
    ]
    args = [gl, gl, gl, bias]
    if use_sink:
        in_specs = [pl.BlockSpec(memory_space=pltpu.SMEM)] + in_specs
        args = [sink] + args
    out = pl.BlockSpec((S, LANES), lambda hp: (0, hp))
    return pl.pallas_call(
        body,
        grid=(4,),
        in_specs=in_specs,
        out_specs=[out, out],
        out_shape=[jax.ShapeDtypeStruct((S, 4 * LANES), f32)] * 2,
        scratch_shapes=[pltpu.VMEM((2 * TQ, w), f32), pltpu.VMEM((2 * TQ, w), f32),
                        pltpu.VMEM((2 * TQ, w), bf16), pltpu.VMEM((2 * TQ, w), bf16),
                        pltpu.VMEM((2 * TQ, LANES), f32)],
        compiler_params=_params(("arbitrary",)),
        name=name,
    )(*args)


def _attn_bwd(gl, bias, bucket, do, lse, dd, g, blk, d, name):
    w = TQ + 2 * blk
    seq = S // d

    def body(q_ref, k_ref, v_ref, b_ref, bk_ref, do_ref, l_ref, d_ref, dqkv_ref, dbk_ref,
             db_acc, s0, s1, dp0, dp1, pb0, pb1, ds0, ds1):
        lo = _lo()
        hi = jnp.logical_not(lo)
        mi = lax.broadcasted_iota(jnp.int32, (1, w), 1)
        dqkv_ref[1] = jnp.zeros((SP, LANES), f32)
        dqkv_ref[2] = jnp.zeros((SP, LANES), f32)
        db_acc[...] = jnp.zeros((2 * TQ, w), f32)
        s_bufs, dp_bufs, pb_bufs, ds_bufs = (s0, s1), (dp0, dp1), (pb0, pb1), (ds0, ds1)

        def stacked(t):
            f0 = pl.multiple_of(t * TQ, TQ)
            q2 = _stack_heads(q_ref[pl.ds(PAD + f0, TQ), :], lo)
            do2 = _stack_heads(do_ref[_token_rows(t, 0, TQ, d), :].astype(bf16), lo)
            return f0, q2, do2

        def scores(t, slot):
            f0, q2, do2 = stacked(t)
            win = pl.ds(PAD - blk + f0, w)
            s_bufs[slot][...] = lax.dot_general(q2, k_ref[win, :], NT_DIMS, preferred_element_type=f32)
            dp_bufs[slot][...] = lax.dot_general(do2, v_ref[win, :], NT_DIMS, preferred_element_type=f32)

        def grads(t, slot):
            f0 = pl.multiple_of(t * TQ, TQ)
            m0 = jnp.bitwise_and(f0, seq - 1)
            inside = (mi >= blk - m0) & (mi < seq + blk - m0)
            for h in range(2):
                msk = lo if h == 0 else hi
                for r in range(TQ // RC):
                    rows = slice(h * TQ + r * RC, h * TQ + (r + 1) * RC)
                    src = _token_rows(t, r * RC, RC, d)
                    lh =jnp.max(jnp.where(msk, l_ref[src, :], -jnp.inf), axis=1, keepdims=True)
                    dh = jnp.max(jnp.where(msk, d_ref[src, :], -jnp.inf), axis=1, keepdims=True)
                    logit = jnp.where(inside, s_bufs[slot][rows, :] + b_ref[h, r * RC:(r + 1) * RC, :], NEG)
                    p = jnp.exp(logit - lh)
                    ds = p * (dp_bufs[slot][rows, :] - dh)
                    db_acc[rows, :] += ds
                    pb_bufs[slot][rows, :] = p.astype(bf16)
                    ds_bufs[slot][rows, :] = ds.astype(bf16)

        def accumulate(t, slot):
            f0, q2, do2 = stacked(t)
            win = pl.ds(PAD - blk + f0, w)
            dsb = ds_bufs[slot][...]
            dq2 = jnp.dot(dsb, k_ref[win, :], preferred_element_type=f32)
            dqkv_ref[0, pl.ds(PAD + f0, TQ), :] = _unstack_heads(dq2, lo)
            dqkv_ref[1, win, :] += lax.dot_general(dsb, q2, TN_DIMS, preferred_element_type=f32)
            dqkv_ref[2, win, :] += lax.dot_general(pb_bufs[slot][...], do2, TN_DIMS, preferred_element_type=f32)

        nt = S // TQ
        scores(0, 0)
        scores(1, 1)
        grads(0, 0)

        def pair(k, carry):
            t = 2 * k + 2
            scores(t, 0)
            grads(t - 1, 1)
            accumulate(t - 2, 0)
            scores(t + 1, 1)
            grads(t, 0)
            accumulate(t - 1, 1)
            return carry

        lax.fori_loop(0, (nt - 2) // 2, pair, 0)
        grads(nt - 1, 1)
        accumulate(nt - 2, 0)
        accumulate(nt - 1, 1)

        bk = bk_ref[...]
        lane = lax.broadcasted_iota(jnp.int32, (8, LANES), 1)
        for h in range(2):
            db = db_acc[h * TQ:(h + 1) * TQ, :]
            acc = jnp.zeros((8, LANES), f32)
            for b in range(32):
                part = jnp.where(bk == b, db, 0.0).reshape(TQ // 8, 8, w).sum(axis=0)
                tot = jnp.sum(jnp.sum(part, axis=1, keepdims=True), axis=0, keepdims=True)
                acc = jnp.where(lane == b, tot, acc)
            dbk_ref[h] = acc

    def gcol(off):
        return pl.BlockSpec((None, None, SP, LANES), lambda hp: (g, off + hp, 0, 0))

    row = pl.BlockSpec((S, LANES), lambda hp: (0, hp))
    return pl.pallas_call(
        body,
        grid=(4,),
        in_specs=[gcol(0), gcol(4), gcol(8), pl.BlockSpec((2, TQ, w), lambda hp: (hp, 0, 0)),
                  pl.BlockSpec((TQ, w), lambda hp: (0, 0)), row, row, row],
        out_specs=[pl.BlockSpec((3, None, SP, LANES), lambda hp: (0, hp, 0, 0)),
                   pl.BlockSpec((2, 8, LANES), lambda hp: (hp, 0, 0))],
        out_shape=[
            jax.ShapeDtypeStruct((3, 4, SP, LANES), f32),
            jax.ShapeDtypeStruct((8, 8, LANES), f32),
        ],
        scratch_shapes=[pltpu.VMEM((2 * TQ, w), f32)] * 5 + [pltpu.VMEM((2 * TQ, w), bf16)] * 4,
        compiler_params=_params(("arbitrary",), vmem_mib=56),
        name=name,
    )(gl, gl, gl, bias, bucket, do, lse, dd)


def _sigmoid(z):
    return 1.0 / (1.0 + jnp.exp(-z))


def _tail(x, tgt, o_a, l_a, o_b, l_b, proj, bm, w_a, w_b, w_o, sink_b):
    ts = 128

    def body(x_ref, t_ref, oa_ref, la_ref, ob0_ref, ob1_ref, ob2_ref, lb0_ref, lb1_ref, lb2_ref,
             ga_ref, gb_ref, m0_ref, m1_ref, bm_ref, wa_ref, wb_ref, wo_ref, sk_ref,
             dy_ref, dyb_ref, dt_ref, doa_ref, dda_ref, dob0_ref, dob1_ref, dob2_ref, ddb0_ref, ddb1_ref, ddb2_ref,
             ya_ref, yb_ref, mg_ref, dbra_ref, dbrb_ref, loss_ref, dbm_ref, dsk_ref):
        i = pl.program_id(0)

        @pl.when(i == 0)
        def _():
            loss_ref[...] = jnp.zeros_like(loss_ref)
            dbm_ref[...] = jnp.zeros_like(dbm_ref)
            dsk_ref[...] = jnp.zeros_like(dsk_ref)

        ga = ga_ref[...]
        sa = _sigmoid(ga)
        silu_a = ga * sa
        oa = oa_ref[...]
        ya = oa * silu_a
        gb = gb_ref[...]
        sb = _sigmoid(gb)
        silu_b = gb * sb
        ob = [ob0_ref[...], ob1_ref[...], ob2_ref[...]]
        lb = [lb0_ref[...], lb1_ref[...], lb2_ref[...]]
        mx = jnp.maximum(jnp.maximum(lb[0], lb[1]), lb[2])
        ex = [jnp.exp(v - mx) for v in lb]
        den = ex[0] + ex[1] + ex[2]
        alpha = [e / den for e in ex]
        ybc = alpha[0] * ob[0] + alpha[1] * ob[1] + alpha[2] * ob[2]
        yb = ybc * silu_b
        yab = ya.astype(bf16)
        ybb = yb.astype(bf16)
        br_a = jnp.dot(yab, wa_ref[...], preferred_element_type=f32)
        br_b = jnp.dot(ybb, wb_ref[...], preferred_element_type=f32)
        g0 = _sigmoid(m0_ref[...] + bm_ref[0:1, :])
        g1 = _sigmoid(m1_ref[...] + bm_ref[1:2, :])
        merged = g0 * br_a + g1 * br_b
        mgb = merged.astype(bf16)
        y = x_ref[...] + jnp.dot(mgb, wo_ref[...], preferred_element_type=f32)
        err = y - t_ref[...]
        part = jnp.sum(jnp.sum(err * err, axis=1, keepdims=True), axis=0, keepdims=True)
        loss_ref[...] += part * (0.5 / D)
        dy = err * (1.0 / D)
        dyb = dy.astype(bf16)
        dmerged = lax.dot_general(dyb, wo_ref[...], NT_DIMS, preferred_element_type=f32)
        dbr_a = (dmerged * g0).astype(bf16)
        dbr_b = (dmerged * g1).astype(bf16)
        dm0 = dmerged * br_a * (g0 * (1.0 - g0))
        dm1 = dmerged * br_b * (g1 * (1.0 - g1))
        dbm_ref[0:1, :] += jnp.sum(dm0, axis=0, keepdims=True)
        dbm_ref[1:2, :] += jnp.sum(dm1, axis=0, keepdims=True)
        dya = lax.dot_general(dbr_a, wa_ref[...], NT_DIMS, preferred_element_type=f32)
        dyb2 = lax.dot_general(dbr_b, wb_ref[...], NT_DIMS, preferred_element_type=f32)
        do_a = dya * silu_a
        dga = dya * oa * (sa * (1.0 + ga * (1.0 - sa)))
        ones = _head_ones()
        delta_a = _seg_sum(do_a * oa, ones)
        dsk_ref[...] -= jnp.sum(delta_a * jnp.exp(sk_ref[...] - la_ref[...]), axis=0, keepdims=True)
        dybc = dyb2 * silu_b
        dgb = dyb2 * ybc * (sb * (1.0 + gb * (1.0 - sb)))
        dbar = _seg_sum(dybc * ybc, ones)
        dy_ref[...] = dy
        dyb_ref[...] = dyb
        dt_ref[:, 0:512] = dga.astype(bf16)
        dt_ref[:, 512:1024] = dgb.astype(bf16)
        dt_ref[:, 1024:2048] = dm0.astype(bf16)
        dt_ref[:, 2048:3072] = dm1.astype(bf16)
        doa_ref[...] = do_a.astype(bf16)
        dda_ref[...] = delta_a
        for k, (dob_ref, ddb_ref) in enumerate(((dob0_ref, ddb0_ref), (dob1_ref, ddb1_ref), (dob2_ref, ddb2_ref))):
            dob_ref[...] = alpha[k] * dybc
            ddb_ref[...] = alpha[k] * dbar
        ya_ref[...] = yab
        yb_ref[...] = ybb
        mg_ref[...] = mgb
        dbra_ref[...] = dbr_a
        dbrb_ref[...] = dbr_b

    def rows(n, blk=0):
        return pl.BlockSpec((ts, n), lambda i: (i, blk))

    def whole(r, c):
        return pl.BlockSpec((r, c), lambda i: (0, 0))

    def gate_cols(n, col):
        return pl.BlockSpec((pl.Element(ts), pl.Element(n)), lambda i: (i * ts, NA + col))

    outs = [
        ((S, D), f32, rows(D)), ((S, D), bf16, rows(D)), ((S, NW), bf16, gate_cols(NT, 0)),
        ((S, 512), bf16, rows(512)), ((S, 512), f32, rows(512)),
        ((S, 512), f32, rows(512)), ((S, 512), f32, rows(512)), ((S, 512), f32, rows(512)),
        ((S, 512), f32, rows(512)), ((S, 512), f32, rows(512)), ((S, 512), f32, rows(512)),
        ((S, 512), bf16, rows(512)), ((S, 512), bf16, rows(512)), ((S, D), bf16, rows(D)),
        ((S, D), bf16, rows(D)), ((S, D), bf16, rows(D)),
        ((1, 1), f32, whole(1, 1)), ((2, D), f32, whole(2, D)), ((1, 512), f32, whole(1, 512)),
    ]
    return pl.pallas_call(
        body,
        grid=(S // ts,),
        in_specs=[
            rows(D), rows(D), rows(512), rows(512), rows(512), rows(512), rows(512), rows(512), rows(512), rows(512),
            gate_cols(512, 0), gate_cols(512, 512), gate_cols(D, 1024), gate_cols(D, 2048), whole(2, D),
            whole(512, D), whole(512, D), whole(D, D), whole(1, 512),
        ],
        out_specs=[o[2] for o in outs],
        out_shape=[jax.ShapeDtypeStruct(o[0], o[1]) for o in outs],
        compiler_params=_params(("arbitrary",)),
        name="tail",
    )(x, tgt, o_a, l_a, *o_b, *l_b, proj, proj, proj, proj, bm, w_a, w_b, w_o, sink_b)


def _norm_bwd(xv, dyv, gain, kind, ones):
    r = lax.rsqrt(_half_sums(xv * xv, ones) * (1.0 / HD) + EPS)
    yv = xv * r
    up = jnp.where(kind == 0, dyv * SCALE, dyv)
    u = up * gain
    dxv = r * (u - yv * (_half_sums(u * yv, ones) * (1.0 / HD)))
    dxv = jnp.where(kind == 2, dyv, dxv)
    dg = jnp.where(kind == 2, 0.0, jnp.sum(up * yv, axis=0, keepdims=True))
    return dxv, dg


def _post_b(g, dqkv, proj_a, gains, dproj):
    d = GROUPS[g][1]
    seq = S // d

    def body(d_ref, p_ref, g_ref, alias_ref, o_ref, dg_ref, nat):
        del alias_ref
        j = pl.program_id(0)
        kind = j // 4
        gain = g_ref[...]
        ones = _head_ones()

        @pl.when(j % 4 == 0)
        def _():
            dg_ref[...] = jnp.zeros_like(dg_ref)

        for c in range(d):
            for i in range(seq // PCHUNK):
                src = c * seq + i * PCHUNK
                if d == 1:
                    idx = slice(src, src + PCHUNK)
                else:
                    idx = pl.ds(c + i * PCHUNK * d, PCHUNK, stride=d)
                dxv, dg = _norm_bwd(p_ref[idx, :], d_ref[PAD + src:PAD + src + PCHUNK, :], gain, kind, ones)
                nat[idx, :] = dxv
                dg_ref[...] += dg

        for i in range(S // CHUNK):
            o_ref[i * CHUNK:(i + 1) * CHUNK, :] = nat[i * CHUNK:(i + 1) * CHUNK, :].astype(bf16)

    return pl.pallas_call(
        body,
        grid=(12,),
        in_specs=[
            pl.BlockSpec((None, None, SP, LANES), lambda j: (j // 4, j % 4, 0, 0)),
            pl.BlockSpec((S, LANES), lambda j: (0, _col_block(g, j))),
            pl.BlockSpec((None, None, 1, LANES), lambda j: (g, j // 4, 0, 0)),
            pl.BlockSpec(memory_space=pl.ANY),
        ],
        out_specs=[
            pl.BlockSpec((S, LANES), lambda j: (0, _col_block(g, j))),
            pl.BlockSpec((None, 1, LANES), lambda j: (j // 4, 0, 0)),
        ],
        out_shape=[jax.ShapeDtypeStruct((S, NW), bf16), jax.ShapeDtypeStruct((3, 1, LANES), f32)],
        scratch_shapes=[pltpu.VMEM((S, LANES), f32)],
        input_output_aliases={3: 0},
        compiler_params=_params(("arbitrary",)),
        name="post_b%d" % g,
    )(dqkv, proj_a, gains, dproj)


def _post_a(dqkv, proj_a, gains, dproj):
    def body(q_ref, e_ref, p_ref, g_ref, alias_ref, o_ref, dg_ref):
        del alias_ref
        j = pl.program_id(0)
        kind = jnp.maximum(j - 3, 0)
        gain = g_ref[...]
        lo = _lo()
        ones = _head_ones()

        @pl.when((j == 0) | (j >= 4))
        def _():
            dg_ref[...] = jnp.zeros_like(dg_ref)

        for i in range(S // PCHUNK):
            r0 = i * PCHUNK
            rows = slice(PAD + r0, PAD + r0 + PCHUNK)
            t0 = e_ref[0, rows, :] + e_ref[1, rows, :]
            t1 = e_ref[2, rows, :] + e_ref[3, rows, :]
            folded = jnp.where(lo, t0 + pltpu.roll(t0, HD, 1), t1 + pltpu.roll(t1, HD, 1))
            dyv = jnp.where(kind == 0, q_ref[rows, :], folded)
            dxv, dg = _norm_bwd(p_ref[r0:r0 + PCHUNK, :], dyv, gain, kind, ones)
            o_ref[r0:r0 + PCHUNK, :] = dxv.astype(bf16)
            dg_ref[...] += dg

    return pl.pallas_call(
        body,
        grid=(6,),
        in_specs=[
            pl.BlockSpec((None, None, SP, LANES), lambda j: (0, jnp.minimum(j, 3), 0, 0)),
            pl.BlockSpec((None, 4, SP, LANES), lambda j: (jnp.clip(j - 3, 1, 2), 0, 0, 0)),
            pl.BlockSpec((S, LANES), lambda j: (0, j)),
            pl.BlockSpec((None, None, 1, LANES), lambda j: (0, jnp.maximum(j - 3, 0), 0, 0)),
            pl.BlockSpec(memory_space=pl.ANY),
        ],
        out_specs=[
            pl.BlockSpec((S, LANES), lambda j: (0, j)),
            pl.BlockSpec((None, 1, LANES), lambda j: (jnp.maximum(j - 3, 0), 0, 0)),
        ],
        out_shape=[jax.ShapeDtypeStruct((S, NW), bf16), jax.ShapeDtypeStruct((3, 1, LANES), f32)],
        input_output_aliases={4: 0},
        compiler_params=_params(("arbitrary",)),
        name="post_a",
    )(dqkv, dqkv, proj_a, gains, dproj)


def _dh_norm_bwd(dproj, w, x, rstd, gain, dy):
    ts = 1024
    tk = NW // 6
    nk = NW // tk

    def body(d_ref, w_ref, x_ref, r_ref, g_ref, dy_ref, gx_ref, dgn_ref, acc):
        i = pl.program_id(0)
        k = pl.program_id(1)

        @pl.when((i == 0) & (k == 0))
        def _():
            dgn_ref[...] = jnp.zeros_like(dgn_ref)

        @pl.when(k == 0)
        def _():
            acc[...] = jnp.zeros_like(acc)

        acc[...] += lax.dot_general(d_ref[...], w_ref[...], NT_DIMS, preferred_element_type=f32)

        @pl.when(k == nk - 1)
        def _():
            dh = acc[...]
            xh = x_ref[...] * r_ref[...]
            u = dh * g_ref[...]
            dx = r_ref[...] * (u - xh * jnp.mean(u * xh, axis=-1, keepdims=True))
            gx_ref[...] = dy_ref[...] + dx
            dgn_ref[...] += jnp.sum(dh * xh, axis=0, keepdims=True)

    return pl.pallas_call(
        body,
        grid=(S // ts, nk),
        in_specs=[
            pl.BlockSpec((ts, tk), lambda i, k: (i, k)),
            pl.BlockSpec((D, tk), lambda i, k: (0, k)),
            pl.BlockSpec((ts, D), lambda i, k: (i, 0)),
            pl.BlockSpec((ts, 1), lambda i, k: (i, 0)),
            pl.BlockSpec((1, D), lambda i, k: (0, 0)),
            pl.BlockSpec((ts, D), lambda i, k: (i, 0)),
        ],
        out_specs=[pl.BlockSpec((ts, D), lambda i, k: (i, 0)), pl.BlockSpec((1, D), lambda i, k: (0, 0))],
        out_shape=[jax.ShapeDtypeStruct((S, D), f32), jax.ShapeDtypeStruct((1, D), f32)],
        scratch_shapes=[pltpu.VMEM((ts, D), f32)],
        compiler_params=_params(("arbitrary", "arbitrary"), vmem_mib=56),
        name="dh_norm_bwd",
    )(dproj, w, x, rstd, gain, dy)


def _dw_in(hbt, dproj):
    tk = 2048
    win = WSH + 96

    def body(a_ref, b_ref, o_ref, acc):
        j = pl.program_id(0)
        k = pl.program_id(1)

        @pl.when(k == 0)
        def _():
            acc[...] = jnp.zeros_like(acc)

        for jj in range(NDEV):
            off = (WSH * jj) % LANES

            @pl.when(j == jj)
            def _():
                acc[...] += jnp.dot(a_ref[...], b_ref[:, off:off + WSH], preferred_element_type=f32)

        @pl.when(k == S // tk - 1)
        def _():
            o_ref[...] = acc[...].astype(bf16)

    return pl.pallas_call(
        body,
        grid=(NDEV, S // tk),
        in_specs=[
            pl.BlockSpec((D, tk), lambda j, k: (0, k)),
            pl.BlockSpec((pl.Element(tk), pl.Element(win)), lambda j, k: (k * tk, (WSH * j) // LANES * LANES)),
        ],
        out_specs=pl.BlockSpec((None, D, WSH), lambda j, k: (j, 0, 0)),
        out_shape=jax.ShapeDtypeStruct((NDEV, D, WSH), bf16),
        scratch_shapes=[pltpu.VMEM((D, WSH), f32)],
        compiler_params=_params(("arbitrary", "arbitrary")),
        name="dw_in",
    )(hbt, dproj)


def _matmul_tn(a, b, name):
    m, n = a.shape[1], b.shape[1]
    tn = TN if n % TN == 0 else 512
    tk = 512

    def body(a_ref, b_ref, o_ref):
        @pl.when(pl.program_id(1) == 0)
        def _():
            o_ref[...] = jnp.zeros_like(o_ref)

        o_ref[...] += lax.dot_general(a_ref[...], b_ref[...], TN_DIMS, preferred_element_type=f32)

    return pl.pallas_call(
        body,
        grid=(n // tn, S // tk),
        in_specs=[pl.BlockSpec((tk, m), lambda j, k: (k, 0)), pl.BlockSpec((tk, tn), lambda j, k: (k, j))],
        out_specs=pl.BlockSpec((m, tn), lambda j, k: (0, j)),
        out_shape=jax.ShapeDtypeStruct((m, n), f32),
        compiler_params=_params(("arbitrary", "arbitrary")),
        name=name,
    )(a, b)


def _exchange(scatter, gather, name):
    arrs = list(scatter) + list(gather)
    n = len(arrs)
    ns = len(scatter)

    def body(*refs):
        ins, outs = refs[:n], refs[n:2 * n]
        send_sems, recv_sems, local_sems = refs[2 * n:]
        x, y, c = lax.axis_index("x"), lax.axis_index("y"), lax.axis_index("c")
        me = 4 * x + 2 * y + c
        local, remote = [], []
        for a in range(n):
            lc = pltpu.make_async_copy(ins[a].at[me] if a < ns else ins[a], outs[a].at[me], local_sems.at[a])
            lc.start()
            local.append(lc)
            for r in range(1, NDEV):
                px = 1 - x if r & 4 else x
                py = 1 - y if r & 2 else y
                pc = 1 - c if r & 1 else c
                cp = pltpu.make_async_remote_copy(
                    src_ref=ins[a].at[4 * px + 2 * py + pc] if a < ns else ins[a],
                    dst_ref=outs[a].at[me],
                    send_sem=send_sems.at[a, r - 1],
                    recv_sem=recv_sems.at[a, r - 1],
                    device_id=(px, py, pc),
                    device_id_type=pl.DeviceIdType.MESH,
                )
                cp.start()
                remote.append(cp)
        for cp in remote:
            cp.wait_recv()
        for cp in remote:
            cp.wait_send()
        for lc in local:
            lc.wait()

    out_shape = [jax.ShapeDtypeStruct(a.shape if i < ns else (NDEV,) + a.shape, a.dtype) for i, a in enumerate(arrs)]
    return pl.pallas_call(
        body,
        in_specs=[pl.BlockSpec(memory_space=pl.ANY)] * n,
        out_specs=[pl.BlockSpec(memory_space=pl.ANY)] * n,
        out_shape=out_shape,
        scratch_shapes=[
            pltpu.SemaphoreType.DMA((n, NDEV - 1)),
            pltpu.SemaphoreType.DMA((n, NDEV - 1)),
            pltpu.SemaphoreType.DMA((n,)),
        ],
        compiler_params=pltpu.CompilerParams(has_side_effects=True),
        name=name,
    )(*arrs)


def _gather_two_level(arrs, name):
    n = len(arrs)

    def body(*refs):
        ins, outs = refs[:n], refs[n:2 * n]
        send_sems, recv_sems, local_sems = refs[2 * n:]
        x, y, c = lax.axis_index("x"), lax.axis_index("y"), lax.axis_index("c")
        me, sibling = (x, y, c), (x, y, 1 - c)
        chips = [(1 - x, y), (x, 1 - y), (1 - x, 1 - y)]

        def copy(a, k, block, to, src=None):
            slot = outs[a].at[4 * block[0] + 2 * block[1] + block[2]]
            return pltpu.make_async_remote_copy(
                src_ref=slot if src is None else src, dst_ref=slot, send_sem=send_sems.at[a, k],
                recv_sem=recv_sems.at[a, k], device_id=to, device_id_type=pl.DeviceIdType.MESH)

        mine, first, passed = [], [], []
        for a in range(n):
            lc = pltpu.make_async_copy(ins[a], outs[a].at[4 * x + 2 * y + c], local_sems.at[a])
            lc.start()
            mine.append(lc)
            first.append(copy(a, 0, me, sibling, src=ins[a]))
            first += [copy(a, 1 + j, me, (*chip, c), src=ins[a]) for j, chip in enumerate(chips)]
        for cp in first:
            cp.start()
        for j, chip in enumerate(chips):
            for a in range(n):
                copy(a, 1 + j, (*chip, c), me).wait_recv()
                fwd = copy(a, 4 + j, (*chip, c), sibling)
                fwd.start()
                passed.append(fwd)
        for a in range(n):
            copy(a, 0, sibling, me).wait_recv()
        for j, chip in enumerate(chips):
            for a in range(n):
                copy(a, 4 + j, (*chip, 1 - c), me).wait_recv()
        for cp in first + passed:
            cp.wait_send()
        for lc in mine:
            lc.wait()

    return pl.pallas_call(
        body,
        in_specs=[pl.BlockSpec(memory_space=pl.ANY)] * n,
        out_specs=[pl.BlockSpec(memory_space=pl.ANY)] * n,
        out_shape=[jax.ShapeDtypeStruct((NDEV,) + a.shape, a.dtype) for a in arrs],
        scratch_shapes=[
            pltpu.SemaphoreType.DMA((n, NDEV - 1)),
            pltpu.SemaphoreType.DMA((n, NDEV - 1)),
            pltpu.SemaphoreType.DMA((n,)),
        ],
        compiler_params=pltpu.CompilerParams(has_side_effects=True),
        name=name,
    )(*arrs)


_HBM = pl.BlockSpec(memory_space=pltpu.HBM)
_SEM = pl.BlockSpec(memory_space=pltpu.SEMAPHORE)
_EFFECT = pltpu.SideEffectType.DATAFLOW_SIDE_EFFECTING


def _sibling_exchange(g, name):
    def body(in_ref, out_ref, send_sems, recv_sems):
        x, y, c = lax.axis_index("x"), lax.axis_index("y"), lax.axis_index("c")
        copies = []
        for q in range(4):
            cp = pltpu.make_async_remote_copy(
                src_ref=in_ref.at[2 * q + (1 - c)], dst_ref=out_ref.at[q], send_sem=send_sems.at[q],
                recv_sem=recv_sems.at[q], device_id=(x, y, 1 - c), device_id_type=pl.DeviceIdType.MESH)
            cp.start()
            copies.append(cp)
        for cp in copies:
            cp.wait_recv()
        for cp in copies:
            cp.wait_send()

    return pl.pallas_call(
        body,
        in_specs=[pl.BlockSpec(memory_space=pl.ANY)],
        out_specs=pl.BlockSpec(memory_space=pl.ANY),
        out_shape=jax.ShapeDtypeStruct((4,) + g.shape[1:], g.dtype),
        scratch_shapes=[pltpu.SemaphoreType.DMA((4,)), pltpu.SemaphoreType.DMA((4,))],
        compiler_params=pltpu.CompilerParams(has_side_effects=True),
        name=name,
    )(g)


def _pair_sum(g, r, core, name):
    _, rows, cols = g.shape
    tr = 256

    def body(c_ref, g_ref, r_ref, o_ref):
        del c_ref
        o_ref[...] = (g_ref[...].astype(f32) + r_ref[...].astype(f32)).astype(bf16)

    return pl.pallas_call(
        body,
        grid_spec=pltpu.PrefetchScalarGridSpec(
            num_scalar_prefetch=1,
            grid=(4, rows // tr),
            in_specs=[pl.BlockSpec((None, tr, cols), lambda q, i, c_ref: (2 * q + c_ref[0], i, 0)),
                      pl.BlockSpec((None, tr, cols), lambda q, i, c_ref: (q, i, 0))],
            out_specs=pl.BlockSpec((None, tr, cols), lambda q, i, c_ref: (q, i, 0)),
        ),
        out_shape=jax.ShapeDtypeStruct((4, rows, cols), bf16),
        compiler_params=_params(("arbitrary", "arbitrary")),
        name=name,
    )(core, g, r)


def _scatter_start(chip_arrs, all_arrs, name):
    arrs = list(chip_arrs) + list(all_arrs)
    n, nc = len(arrs), len(chip_arrs)
    lands = [lax.empty(((3 if i < nc else NDEV - 1),) + a.shape[1:], a.dtype) for i, a in enumerate(arrs)]

    def body(*refs):
        src, land = refs[:n], refs[n:2 * n]
        send_sems, recv_sems = refs[2 * n:3 * n], refs[3 * n:4 * n]
        token = refs[6 * n]
        x, y, c = lax.axis_index("x"), lax.axis_index("y"), lax.axis_index("c")
        for a in range(n):
            for r in range(1, 4 if a < nc else NDEV):
                if a < nc:
                    px, py, pc = (1 - x if r & 2 else x), (1 - y if r & 1 else y), c
                    block = 2 * px + py
                else:
                    px, py, pc = (1 - x if r & 4 else x), (1 - y if r & 2 else y), (1 - c if r & 1 else c)
                    block = 4 * px + 2 * py + pc
                pltpu.make_async_remote_copy(
                    src_ref=src[a].at[block], dst_ref=land[a].at[r - 1], send_sem=send_sems[a],
                    recv_sem=recv_sems[a], device_id=(px, py, pc), device_id_type=pl.DeviceIdType.MESH).start()
        token[...] = jnp.zeros_like(token)

    hbm = [pltpu.HBM(a.shape, a.dtype) for a in arrs + lands]
    ops = [pltpu.with_memory_space_constraint(a, pltpu.HBM) for a in arrs + lands]
    outs = pl.pallas_call(
        body,
        out_shape=tuple([pltpu.SemaphoreType.DMA(())] * (2 * n) + hbm + [jax.ShapeDtypeStruct((8, LANES), f32)]),
        in_specs=[_HBM] * (2 * n),
        out_specs=tuple([_SEM] * (2 * n) + [_HBM] * (2 * n) + [pl.BlockSpec(memory_space=pltpu.VMEM)]),
        input_output_aliases={i: 2 * n + i for i in range(2 * n)},
        compiler_params=pltpu.CompilerParams(has_side_effects=_EFFECT),
        name=name,
    )(*ops)
    return outs[:n], outs[n:2 * n], outs[2 * n:3 * n], outs[3 * n:4 * n], outs[4 * n]


def _scatter_wait(send_sems, recv_sems, srcs, lands, after, name):
    n = len(srcs)

    def body(*refs):
        land = refs[n:2 * n]
        ssem, rsem = refs[2 * n:3 * n], refs[3 * n:4 * n]
        x, y, c = lax.axis_index("x"), lax.axis_index("y"), lax.axis_index("c")
        for a in range(n):
            done = pltpu.make_async_remote_copy(
                src_ref=land[a], dst_ref=land[a], send_sem=ssem[a], recv_sem=rsem[a], device_id=(x, y, c),
                device_id_type=pl.DeviceIdType.MESH)
            done.wait_send()
            done.wait_recv()

    hbm = [pltpu.HBM(a.shape, a.dtype) for a in list(srcs) + list(lands)]
    outs = pl.pallas_call(
        body,
        out_shape=tuple(hbm),
        in_specs=[_HBM] * (2 * n) + [_SEM] * (2 * n) + [pl.BlockSpec(memory_space=pl.ANY)],
        out_specs=tuple([_HBM] * (2 * n)),
        input_output_aliases={i: i for i in range(2 * n)},
        compiler_params=pltpu.CompilerParams(has_side_effects=_EFFECT),
        name=name,
    )(*srcs, *lands, *send_sems, *recv_sems, after)
    return outs[:n], outs[n:]


def _adam_update(g, w_ref, m_ref, v_ref, g_ref, d_ref, nm_ref, nv_ref):
    mm = ADAM_B1 * m_ref[...] + (1.0 - ADAM_B1) * g
    vv = ADAM_B2 * v_ref[...] + (1.0 - ADAM_B2) * (g * g)
    m_hat = mm / (1.0 - ADAM_B1 ** ADAM_STEP)
    v_hat = vv / (1.0 - ADAM_B2 ** ADAM_STEP)
    g_ref[...] = g
    d_ref[...] = -ADAM_LR * (m_hat / (jnp.sqrt(v_hat) + ADAM_EPS) + ADAM_WD * w_ref[...])
    nm_ref[...] = mm
    nv_ref[...] = vv


def _adamw_own(w, own, own_idx, slots, m, v, name):
    r, c = w.shape[-2:]
    tr = 128 if r % 128 == 0 else r
    k = slots.shape[0]

    def body(i_ref, w_ref, o_ref, s_ref, m_ref, v_ref, g_ref, d_ref, nm_ref, nv_ref):
        del i_ref
        g = o_ref[...].astype(f32)
        for j in range(k):
            g = g + s_ref[j].astype(f32)
        _adam_update(g, w_ref, m_ref, v_ref, g_ref, d_ref, nm_ref, nv_ref)

    blk = pl.BlockSpec((None, tr, c), lambda i, ix: (0, i, 0))
    return pl.pallas_call(
        body,
        grid_spec=pltpu.PrefetchScalarGridSpec(
            num_scalar_prefetch=1,
            grid=(r // tr,),
            in_specs=[blk, pl.BlockSpec((None, tr, c), lambda i, ix: (ix[0], i, 0)),
                      pl.BlockSpec((k, tr, c), lambda i, ix: (0, i, 0)), blk, blk],
            out_specs=[blk] * 4,
        ),
        out_shape=[jax.ShapeDtypeStruct(w.shape, f32)] * 4,
        compiler_params=_params(("arbitrary",)),
        name=name,
    )(own_idx, w, own, slots, m, v)


def _adamw(w, slots, m, v, name):
    r, c = w.shape[-2:]
    tr = 128 if r % 128 == 0 else r

    def body(w_ref, s_ref, m_ref, v_ref, g_ref, d_ref, nm_ref, nv_ref):
        g = s_ref[0].astype(f32)
        for k in range(1, NDEV):
            g = g + s_ref[k].astype(f32)
        _adam_update(g, w_ref, m_ref, v_ref, g_ref, d_ref, nm_ref, nv_ref)

    if w.ndim == 3:
        blk = pl.BlockSpec((None, tr, c), lambda i: (0, i, 0))
    else:
        blk = pl.BlockSpec((tr, c), lambda i: (i, 0))
    return pl.pallas_call(
        body,
        grid=(r // tr,),
        in_specs=[blk, pl.BlockSpec((NDEV, tr, c), lambda i: (0, i, 0)), blk, blk],
        out_specs=[blk] * 4,
        out_shape=[jax.ShapeDtypeStruct(w.shape, f32)] * 4,
        compiler_params=_params(("arbitrary",)),
        name=name,
    )(w, slots, m, v)


def _local_step(x, tgt, norm_gain, w_shards, qn_a, kn_a, qn_b, kn_b, sink_a, rel_bias, w_a, w_b, b_merge, w_o,
                on_weight_grads=None):
    two = lambda t: jnp.concatenate([t, t], axis=-1).reshape(1, LANES)
    ones = jnp.ones((1, LANES), f32)
    gains = jnp.stack([
        jnp.stack([two(qn_a), two(kn_a), ones]),
        jnp.stack([two(qn_b), two(kn_b), ones]),
        jnp.stack([two(qn_b), two(kn_b), ones]),
        jnp.stack([two(qn_b), two(kn_b), ones]),
    ])
    buckets = [jnp.asarray(_bucket_np(blk, d)) for blk, d, _ in GROUPS]
    bias = [_bias_expand(rel_bias, buckets[k], GROUPS[k][2], "bias_expand_%d" % k) for k in range(4)]

    hb, hbt, rstd = _rms(x, norm_gain)
    w_in = _assemble_w(w_shards)
    proj = _inproj(hb, w_in)
    gl = _prep(proj, gains)
    o_a, l_a = _attn_fwd(gl, bias[0], sink_a.reshape(8), 0, 128, 1, "attn_fwd_a")
    fwd_b = [_attn_fwd(gl, bias[k], None, k, GROUPS[k][0], GROUPS[k][1], "attn_fwd_b%d" % k) for k in (1, 2, 3)]
    sink_b = jnp.repeat(sink_a.reshape(8), HD).reshape(1, 512)

    (dy, dyb, dproj, do_a, dd_a, do_b0, do_b1, do_b2, dd_b0, dd_b1, dd_b2, ya, yb, mg, dbr_a, dbr_b, loss, dbm,
     dsk) = _tail(x, tgt, o_a, l_a, [f[0] for f in fwd_b], [f[1] for f in fwd_b], proj, b_merge, w_a, w_b, w_o, sink_b)

    dqkv_a, dbk_a = _attn_bwd(gl, bias[0], buckets[0], do_a, l_a, dd_a, 0, 128, 1, "attn_bwd_a")
    dproj, dg_a = _post_a(dqkv_a, proj, gains, dproj)
    dbk_b, dg_b = [], []
    for k, do_k, dd_k in ((1, do_b0, dd_b0), (2, do_b1, dd_b1), (3, do_b2, dd_b2)):
        dqkv, dbk = _attn_bwd(gl, bias[k], buckets[k], do_k, fwd_b[k - 1][1], dd_k, k, GROUPS[k][0], GROUPS[k][1],
                              "attn_bwd_b%d" % k)
        dproj, dg = _post_b(k, dqkv, proj, gains, dproj)
        dbk_b.append(dbk)
        dg_b.append(dg)
    dg_b = jnp.stack(dg_b)

    dw_in = _dw_in(hbt, dproj)
    dw_o = _matmul_tn(mg, dyb, "dw_out")
    dw_a = _matmul_tn(ya, dbr_a, "dw_branch_a")
    dw_b = _matmul_tn(yb, dbr_b, "dw_branch_b")
    token = jnp.zeros((), f32) if on_weight_grads is None else on_weight_grads(
        dict(w_in=dw_in, w_branch_a=dw_a, w_branch_b=dw_b, b_merge=dbm, w_out=dw_o))
    grad_x, d_norm_gain = _dh_norm_bwd(dproj, w_in, x, rstd, norm_gain + token, dy)

    fold = lambda t: t[..., :HD] + t[..., HD:]
    d_qn_a = fold(dg_a[0, 0])
    d_kn_a = fold(dg_a[1, 0])
    d_qn_b = fold(dg_b[:, 0, 0].sum(axis=0))
    d_kn_b = fold(dg_b[:, 1, 0].sum(axis=0))
    d_sink = dsk.reshape(8, HD)[:, 0]
    red = jnp.stack([dbk_a] + dbk_b)
    d_rel = red[:, :, 0, :32].reshape(32, 32).T
    return dict(loss=loss, grad_x=grad_x, norm_gain=d_norm_gain, w_in=dw_in, q_norm_a=d_qn_a, k_norm_a=d_kn_a,
                q_norm_b=d_qn_b, k_norm_b=d_kn_b, sink_a=d_sink, rel_bias=d_rel, w_branch_a=dw_a, w_branch_b=dw_b,
                b_merge=dbm, w_out=dw_o)


SMALL = (("norm_gain", D), ("q_norm_a", HD), ("k_norm_a", HD), ("q_norm_b", HD), ("k_norm_b", HD), ("sink_a", 8),
         ("rel_bias", 1024))
SMALL_PAD = 2432


SMALL_USED = sum(sz for _, sz in SMALL)


def _pack_small(parts, loss=None):
    tail = jnp.zeros((SMALL_PAD - SMALL_USED,), f32)
    if loss is not None:
        tail = tail.at[0].set(loss.reshape(()))
    return jnp.concatenate([parts[n].reshape(-1) for n, _ in SMALL] + [tail]).reshape(1, SMALL_PAD)


def _unpack_small(flat, shapes):
    out, off = {}, 0
    for n, sz in SMALL:
        out[n] = flat[0, off:off + sz].reshape(shapes[n])
        off += sz
    return out


def kernel(x, norm_gain, w_in, q_norm_a, k_norm_a, q_norm_b, k_norm_b, sink_a, rel_bias, w_branch_a, w_branch_b, b_merge, w_out, loss_target, m_norm_gain, m_w_in, m_q_norm_a, m_k_norm_a, m_q_norm_b, m_k_norm_b, m_sink_a, m_rel_bias, m_w_branch_a, m_w_branch_b, m_b_merge, m_w_out, v_norm_gain, v_w_in, v_q_norm_a, v_k_norm_a, v_q_norm_b, v_k_norm_b, v_sink_a, v_rel_bias, v_w_branch_a, v_w_branch_b, v_b_merge, v_w_out):
    wsh = NW // NDEV
    csh = D // NDEV
    g_in, g_a, g_b, g_o, g_bm = _gather_two_level(
        [w_in[0].astype(bf16), w_branch_a[0].astype(bf16), w_branch_b[0].astype(bf16), w_out[0].astype(bf16),
         b_merge[0]], "gather_weights")
    w_a_full = g_a.transpose(1, 0, 2).reshape(512, D)
    w_b_full = g_b.transpose(1, 0, 2).reshape(512, D)
    w_o_full = g_o.reshape(D, D)
    bm_full = g_bm.transpose(1, 0, 2).reshape(2, D)

    pending = {}
    core = lax.axis_index("c").astype(jnp.int32).reshape(1)
    chip = (2 * lax.axis_index("x") + lax.axis_index("y")).astype(jnp.int32).reshape(1)
    me = (2 * chip + core).astype(jnp.int32)

    def start_exchange(gw):
        from_sibling = _sibling_exchange(gw["w_in"], "grad_sibling_exchange")
        chip_sums = _pair_sum(gw["w_in"], from_sibling, core, "grad_pair_sum")
        blocks = [gw["w_branch_a"].reshape(512, NDEV, csh).transpose(1, 0, 2).astype(bf16),
                  gw["w_branch_b"].reshape(512, NDEV, csh).transpose(1, 0, 2).astype(bf16),
                  gw["w_out"].reshape(NDEV, csh, D).astype(bf16),
                  gw["b_merge"].reshape(2, NDEV, csh).transpose(1, 0, 2)]
        pending["started"] = _scatter_start([chip_sums], blocks, "scatter_grads_start")
        return pending["started"][4][0, 0]

    loc = _local_step(x[0], loss_target[0], norm_gain, g_in, q_norm_a, k_norm_a, q_norm_b, k_norm_b, sink_a,
                      rel_bias, w_a_full, w_b_full, bm_full, w_o_full, on_weight_grads=start_exchange)

    small_shapes = dict(norm_gain=(1, D), q_norm_a=(1, HD), k_norm_a=(1, HD), q_norm_b=(1, HD), k_norm_b=(1, HD),
                        sink_a=(1, 8), rel_bias=(32, 32))
    (r_small,) = _exchange([], [_pack_small(loc, loc["loss"])], "gather_small_grads")
    send_sems, recv_sems, srcs, lands, _ = pending["started"]
    (s_in, s_a, s_b, s_o, s_bm), (r_in, r_a, r_b, r_o, r_bm) = _scatter_wait(
        send_sems, recv_sems, srcs, lands, r_small, "scatter_grads_wait")

    given = dict(norm_gain=norm_gain, q_norm_a=q_norm_a, k_norm_a=k_norm_a, q_norm_b=q_norm_b, k_norm_b=k_norm_b,
                 sink_a=sink_a, rel_bias=rel_bias)
    m_small = dict(norm_gain=m_norm_gain, q_norm_a=m_q_norm_a, k_norm_a=m_k_norm_a, q_norm_b=m_q_norm_b,
                   k_norm_b=m_k_norm_b, sink_a=m_sink_a, rel_bias=m_rel_bias)
    v_small = dict(norm_gain=v_norm_gain, q_norm_a=v_q_norm_a, k_norm_a=v_k_norm_a, q_norm_b=v_q_norm_b,
                   k_norm_b=v_k_norm_b, sink_a=v_sink_a, rel_bias=v_rel_bias)
    res = {
        "small": _adamw(_pack_small(given), r_small, _pack_small(m_small), _pack_small(v_small), "adamw_small"),
        "w_in": _adamw_own(w_in, s_in, chip, r_in, m_w_in, v_w_in, "adamw_w_in"),
        "w_branch_a": _adamw_own(w_branch_a, s_a, me, r_a, m_w_branch_a, v_w_branch_a, "adamw_w_branch_a"),
        "w_branch_b": _adamw_own(w_branch_b, s_b, me, r_b, m_w_branch_b, v_w_branch_b, "adamw_w_branch_b"),
        "b_merge": _adamw_own(b_merge, s_bm, me, r_bm, m_b_merge, v_b_merge, "adamw_b_merge"),
        "w_out": _adamw_own(w_out, s_o, me, r_o, m_w_out, v_w_out, "adamw_w_out"),
    }
    order = ["norm_gain", "w_in", "q_norm_a", "k_norm_a", "q_norm_b", "k_norm_b", "sink_a", "rel_bias", "w_branch_a",
             "w_branch_b", "b_merge", "w_out"]
    outs = []
    for k in range(4):
        small = _unpack_small(res["small"][k], small_shapes)
        for n in order:
            outs.append(small[n] if n in small else res[n][k])
    loss = res["small"][0][0, SMALL_USED]
    return (loss, loc["grad_x"][None], *outs)
```

```python
import math

import numpy as np
import jax
import jax.numpy as jnp
from jax import lax
from jax.experimental import pallas as pl
from jax.experimental.pallas import tpu as pltpu

f32 = jnp.float32
bf16 = jnp.bfloat16

S = 4096
D = 1024
NA = 5376
NT = 3072
NW = NA + NT
WSH = NW // 8
HD = 64
LANES = 128
EPS = 1e-6
NEG = -1e30
SCALE = HD ** -0.5
TQ = 128
PAD = 128
SP = S + 2 * PAD
NDEV = 8
GROUPS = ((128, 1, 0), (64, 1, 8), (64, 4, 16), (64, 16, 24))
CHUNK = 256
PCHUNK = 128
RC = 64
TN = 768

ADAM_LR, ADAM_B1, ADAM_B2, ADAM_EPS, ADAM_WD, ADAM_STEP = 0.001, 0.9, 0.999, 1e-08, 0.01, 10

MIB = 1024 * 1024
NT_DIMS = (((1,), (1,)), ((), ()))
TN_DIMS = (((0,), (0,)), ((), ()))


def _params(sem=None, vmem_mib=48):
    return pltpu.CompilerParams(dimension_semantics=sem, vmem_limit_bytes=vmem_mib * MIB)


def _lo():
    return lax.broadcasted_iota(jnp.int32, (1, LANES), 1) < HD


def _head_ones():
    r = lax.broadcasted_iota(jnp.int32, (LANES, LANES), 0) // HD
    c = lax.broadcasted_iota(jnp.int32, (LANES, LANES), 1) // HD
    return jnp.where(r == c, 1.0, 0.0).astype(bf16)


def _half_sums(x, ones):
    hi = x.astype(bf16)
    mid = (x - hi.astype(f32)).astype(bf16)
    return (jnp.dot(hi, ones, preferred_element_type=f32) + jnp.dot(mid, ones, preferred_element_type=f32))


def _seg_sum(x, ones):
    outs = [_half_sums(x[:, b * LANES:(b + 1) * LANES], ones) for b in range(x.shape[1] // LANES)]
    return outs[0] if len(outs) == 1 else jnp.concatenate(outs, axis=1)


def _bucket_np(blk, stride):
    w = TQ + 2 * blk
    rel = np.arange(w)[None, :] - blk - np.arange(TQ)[:, None]
    band = np.abs(rel) <= blk
    r = rel * stride
    n = np.abs(r)
    nf = np.maximum(n, 8).astype(np.float32)
    large = 8 + (np.log(nf / np.float32(8)) / np.float32(math.log(128.0)) * np.float32(8)).astype(np.int32)
    large = np.minimum(large, 15)
    b = (r > 0).astype(np.int32) * 16 + np.where(n < 8, n, large)
    return np.where(band, b, -1).astype(np.int32)


def _rms(x, gain):
    ts = 512

    def body(x_ref, g_ref, h_ref, ht_ref, r_ref):
        xv = x_ref[...]
        r = lax.rsqrt(jnp.mean(xv * xv, axis=-1, keepdims=True) + EPS)
        h = (xv * r) * g_ref[...]
        h_ref[...] = h.astype(bf16)
        ht_ref[...] = h.T.astype(bf16)
        r_ref[...] = r

    return pl.pallas_call(
        body,
        grid=(S // ts,),
        in_specs=[pl.BlockSpec((ts, D), lambda i: (i, 0)), pl.BlockSpec((1, D), lambda i: (0, 0))],
        out_specs=[pl.BlockSpec((ts, D), lambda i: (i, 0)), pl.BlockSpec((D, ts), lambda i: (0, i)),
                   pl.BlockSpec((ts, 1), lambda i: (i, 0))],
        out_shape=[jax.ShapeDtypeStruct((S, D), bf16), jax.ShapeDtypeStruct((D, S), bf16),
                   jax.ShapeDtypeStruct((S, 1), f32)],
        compiler_params=_params(("arbitrary",)),
        name="rms",
    )(x, gain)


def _assemble_w(w_shards):
    def body(wa_ref, wb_ref, o_ref):
        n = pl.program_id(0)
        for nn in range(NW // TN):
            j0 = (TN * nn) // WSH
            a = TN * nn - WSH * j0
            len1 = min(TN, WSH - a)

            @pl.when(n == nn)
            def _():
                o_ref[:, 0:len1] = wa_ref[:, a:a + len1]
                if len1 < TN:
                    o_ref[:, len1:TN] = wb_ref[:, 0:TN - len1]

    return pl.pallas_call(
        body,
        grid=(NW // TN,),
        in_specs=[
            pl.BlockSpec((None, D, WSH), lambda n: ((TN * n) // WSH, 0, 0)),
            pl.BlockSpec((None, D, WSH), lambda n: (jnp.minimum((TN * n) // WSH + 1, NDEV - 1), 0, 0)),
        ],
        out_specs=pl.BlockSpec((D, TN), lambda n: (0, n)),
        out_shape=jax.ShapeDtypeStruct((D, NW), bf16),
        compiler_params=_params(("arbitrary",)),
        name="assemble_w",
    )(w_shards, w_shards)


def _inproj(hb, w):
    ts = 2048

    def body(h_ref, w_ref, p_ref):
        p_ref[...] = jnp.dot(h_ref[...], w_ref[...], preferred_element_type=f32)

    return pl.pallas_call(
        body,
        grid=(S // ts, NW // TN),
        in_specs=[pl.BlockSpec((ts, D), lambda i, n: (i, 0)), pl.BlockSpec((D, TN), lambda i, n: (0, n))],
        out_specs=pl.BlockSpec((ts, TN), lambda i, n: (i, n)),
        out_shape=jax.ShapeDtypeStruct((S, NW), f32),
        compiler_params=_params(("arbitrary", "arbitrary")),
        name="inproj",
    )(hb, w)


def _bias_expand(table, bucket, c0, name):
    tq, w = bucket.shape

    def body(tab_ref, bk_ref, o_ref):
        h = pl.program_id(0)
        bk = bk_ref[...]

        def step(b, acc):
            return jnp.where(bk == b, tab_ref[b, c0 + h], acc)

        o_ref[...] = lax.fori_loop(0, 32, step, jnp.full((tq, w), NEG, f32))

    return pl.pallas_call(
        body,
        grid=(8,),
        in_specs=[pl.BlockSpec(memory_space=pltpu.SMEM), pl.BlockSpec((tq, w), lambda h: (0, 0))],
        out_specs=pl.BlockSpec((None, tq, w), lambda h: (h, 0, 0)),
        out_shape=jax.ShapeDtypeStruct((8, tq, w), f32),
        compiler_params=_params(("arbitrary",)),
        name=name,
    )(table, bucket)


def _col_block(g, j):
    kind = j // 4
    hp = j % 4
    a = jnp.where(kind == 0, hp, 3 + kind)
    b = 6 + 12 * kind + 4 * (g - 1) + hp
    return jnp.where(g == 0, a, b)


def _prep(proj_a, gains):
    def body(p_ref, g_ref, o_ref):
        g = pl.program_id(0)
        j = pl.program_id(1)
        kind = j // 4
        lo = _lo()
        ones = _head_ones()
        half = jnp.where(lo, 0, 1)
        take = (kind == 0) | (half == (j % 4) // 2)
        gain = g_ref[...]
        o_ref[0:PAD, :] = jnp.zeros((PAD, LANES), bf16)
        o_ref[PAD + S:SP, :] = jnp.zeros((PAD, LANES), bf16)

        def norm_store(xv, dst, dup):
            if dup:
                xv = jnp.where(take, xv, pltpu.roll(xv, HD, 1))
            r = lax.rsqrt(_half_sums(xv * xv, ones) * (1.0 / HD) + EPS)
            r = jnp.where(kind == 2, 1.0, r)
            yv = (xv * r) * gain
            yv = jnp.where(kind == 0, yv * SCALE, yv)
            o_ref[PAD + dst:PAD + dst + CHUNK, :] = yv.astype(bf16)

        for gi, (_, d, _) in enumerate(GROUPS):
            @pl.when(g == gi)
            def _():
                seq = S // d
                for c in range(d):
                    for i in range(seq // CHUNK):
                        if d == 1:
                            xv = p_ref[i * CHUNK:(i + 1) * CHUNK, :]
                        else:
                            xv = p_ref[pl.ds(c + i * CHUNK * d, CHUNK, stride=d), :]
                        norm_store(xv, c * seq + i * CHUNK, gi == 0)

    return pl.pallas_call(
        body,
        grid=(4, 12),
        in_specs=[
            pl.BlockSpec((S, LANES), lambda g, j: (0, _col_block(g, j))),
            pl.BlockSpec((None, None, 1, LANES), lambda g, j: (g, j // 4, 0, 0)),
        ],
        out_specs=pl.BlockSpec((None, None, SP, LANES), lambda g, j: (g, j, 0, 0)),
        out_shape=jax.ShapeDtypeStruct((4, 12, SP, LANES), bf16),
        compiler_params=_params(("arbitrary", "arbitrary")),
        name="prep",
    )(proj_a, gains)


def _token_rows(t, r0, n, d):
    if d == 1:
        return pl.ds(pl.multiple_of(t * TQ, TQ) + r0, n)
    per = S // d // TQ
    return pl.ds(((t % per) * TQ + r0) * d + t // per, n, stride=d)


def _stack_heads(t, lo):
    z = jnp.zeros_like(t)
    return jnp.concatenate([jnp.where(lo, t, z), jnp.where(lo, z, t)], axis=0)


def _unstack_heads(t2, lo):
    return jnp.where(lo, t2[:TQ], t2[TQ:])


def _attn_fwd(gl, bias, sink, g, blk, d, name):
    w = TQ + 2 * blk
    seq = S // d
    use_sink = sink is not None

    def body(*refs):
        if use_sink:
            sink_ref, q_ref, k_ref, v_ref, b_ref, o_ref, l_ref, s0, s1, p0, p1, lse_scr = refs
        else:
            q_ref, k_ref, v_ref, b_ref, o_ref, l_ref, s0, s1, p0, p1, lse_scr = refs
        hp = pl.program_id(0)
        lo = _lo()
        mi = lax.broadcasted_iota(jnp.int32, (1, w), 1)
        s_bufs, p_bufs = (s0, s1), (p0, p1)

        def scores(p, slot):
            for u in range(2):
                f0 = pl.multiple_of((2 * p + u) * TQ, TQ)
                q2 = _stack_heads(q_ref[pl.ds(PAD + f0, TQ), :], lo)
                kw = k_ref[pl.ds(PAD - blk + f0, w), :]
                s_bufs[slot][u] = lax.dot_general(q2, kw, NT_DIMS, preferred_element_type=f32)

        def softmax(p, slot):
            for u in range(2):
                t = 2 * p + u
                m0 = jnp.bitwise_and(pl.multiple_of(t * TQ, TQ), seq - 1)
                inside = (mi >= blk - m0) & (mi < seq + blk - m0)
                for h in range(2):
                    for r in range(TQ // RC):
                        rows = slice(h * TQ + r * RC, h * TQ + (r + 1) * RC)
                        logit = jnp.where(inside, s_bufs[slot][u, rows, :] + b_ref[h, r * RC:(r + 1) * RC, :], NEG)
                        m = jnp.max(logit, axis=1, keepdims=True)
                        e = jnp.exp(logit - m)
                        lse = m + jnp.log(jnp.sum(e, axis=1, keepdims=True))
                        if use_sink:
                            sk = sink_ref[2 * hp + h]
                            mx = jnp.maximum(lse, sk)
                            lse = mx + jnp.log(jnp.exp(lse - mx) + jnp.exp(sk - mx))
                        p_bufs[slot][u, rows, :] = (e * jnp.exp(m - lse)).astype(bf16)
                        lse_scr[u, rows, :] = jnp.broadcast_to(lse, (RC, LANES))
                l_ref[_token_rows(t, 0, TQ, d), :] = jnp.where(lo, lse_scr[u, 0:TQ, :], lse_scr[u, TQ:2 * TQ, :])

        def values(p, slot):
            for u in range(2):
                t = 2 * p + u
                vw = v_ref[pl.ds(PAD - blk + pl.multiple_of(t * TQ, TQ), w), :]
                o2 = jnp.dot(p_bufs[slot][u], vw, preferred_element_type=f32)
                o_ref[_token_rows(t, 0, TQ, d), :] = _unstack_heads(o2, lo)

        npair = S // TQ // 2
        scores(0, 0)
        scores(1, 1)
        softmax(0, 0)

        def steady(k, carry):
            p = 2 * k + 2
            scores(p, 0)
            softmax(p - 1, 1)
            values(p - 2, 0)
            scores(p + 1, 1)
            softmax(p, 0)
            values(p - 1, 1)
            return carry

        lax.fori_loop(0, (npair - 2) // 2, steady, 0)
        softmax(npair - 1, 1)
        values(npair - 2, 0)
        values(npair - 1, 1)

    in_specs = [
        pl.BlockSpec((None, None, SP, LANES), lambda hp: (g, hp, 0, 0)),
        pl.BlockSpec((None, None, SP, LANES), lambda hp: (g, 4 + hp, 0, 0)),
        pl.BlockSpec((None, None, SP, LANES), lambda hp: (g, 8 + hp, 0, 0)),
        pl.BlockSpec((2, TQ, w), lambda hp: (hp, 0, 0)),
    ]
    args = [gl, gl, gl, bias]
    if use_sink:
        in_specs = [pl.BlockSpec(memory_space=pltpu.SMEM)] + in_specs
        args = [sink] + args
    out = pl.BlockSpec((S, LANES), lambda hp: (0, hp))
    return pl.pallas_call(
        body,
        grid=(4,),
        in_specs=in_specs,
        out_specs=[out, out],
        out_shape=[jax.ShapeDtypeStruct((S, 4 * LANES), f32)] * 2,
        scratch_shapes=[pltpu.VMEM((2, 2 * TQ, w), f32), pltpu.VMEM((2, 2 * TQ, w), f32),
                        pltpu.VMEM((2, 2 * TQ, w), bf16), pltpu.VMEM((2, 2 * TQ, w), bf16),
                        pltpu.VMEM((2, 2 * TQ, LANES), f32)],
        compiler_params=_params(("arbitrary",)),
        name=name,
    )(*args)


def _attn_bwd(gl, bias, bucket, do, lse, dd, g, blk, d, name):
    w = TQ + 2 * blk
    seq = S // d

    def body(q_ref, k_ref, v_ref, b_ref, bk_ref, do_ref, l_ref, d_ref, dqkv_ref, dbk_ref,
             db_acc, s0, s1, dp0, dp1, pb0, pb1, ds0, ds1):
        lo = _lo()
        hi = jnp.logical_not(lo)
        mi = lax.broadcasted_iota(jnp.int32, (1, w), 1)
        dqkv_ref[1] = jnp.zeros((SP, LANES), f32)
        dqkv_ref[2] = jnp.zeros((SP, LANES), f32)
        db_acc[...] = jnp.zeros((2 * TQ, w), f32)
        s_bufs, dp_bufs, pb_bufs, ds_bufs = (s0, s1), (dp0, dp1), (pb0, pb1), (ds0, ds1)

        def stacked(t):
            f0 = pl.multiple_of(t * TQ, TQ)
            q2 = _stack_heads(q_ref[pl.ds(PAD + f0, TQ), :], lo)
            do2 = _stack_heads(do_ref[_token_rows(t, 0, TQ, d), :].astype(bf16), lo)
            return f0, q2, do2

        def scores(p, slot):
            for u in range(2):
                f0, q2, do2 = stacked(2 * p + u)
                win = pl.ds(PAD - blk + f0, w)
                s_bufs[slot][u] = lax.dot_general(q2, k_ref[win, :], NT_DIMS, preferred_element_type=f32)
                dp_bufs[slot][u] = lax.dot_general(do2, v_ref[win, :], NT_DIMS, preferred_element_type=f32)

        def grads(p, slot):
            for u in range(2):
                t = 2 * p + u
                m0 = jnp.bitwise_and(pl.multiple_of(t * TQ, TQ), seq - 1)
                inside = (mi >= blk - m0) & (mi < seq + blk - m0)
                for h in range(2):
                    msk = lo if h == 0 else hi
                    for r in range(TQ // RC):
                        rows = slice(h * TQ + r * RC, h * TQ + (r + 1) * RC)
                        src = _token_rows(t, r * RC, RC, d)
                        lh = jnp.max(jnp.where(msk, l_ref[src, :], -jnp.inf), axis=1, keepdims=True)
                        dh = jnp.max(jnp.where(msk, d_ref[src, :], -jnp.inf), axis=1, keepdims=True)
                        logit = jnp.where(inside, s_bufs[slot][u, rows, :] + b_ref[h, r * RC:(r + 1) * RC, :], NEG)
                        pr = jnp.exp(logit - lh)
                        ds = pr * (dp_bufs[slot][u, rows, :] - dh)
                        db_acc[rows, :] += ds
                        pb_bufs[slot][u, rows, :] = pr.astype(bf16)
                        ds_bufs[slot][u, rows, :] = ds.astype(bf16)

        def accumulate(p, slot):
            for u in range(2):
                f0, q2, do2 = stacked(2 * p + u)
                win = pl.ds(PAD - blk + f0, w)
                dsb = ds_bufs[slot][u]
                dq2 = jnp.dot(dsb, k_ref[win, :], preferred_element_type=f32)
                dqkv_ref[0, pl.ds(PAD + f0, TQ), :] = _unstack_heads(dq2, lo)
                dqkv_ref[1, win, :] += lax.dot_general(dsb, q2, TN_DIMS, preferred_element_type=f32)
                dqkv_ref[2, win, :] += lax.dot_general(pb_bufs[slot][u], do2, TN_DIMS, preferred_element_type=f32)

        npair = S // TQ // 2
        scores(0, 0)
        scores(1, 1)
        grads(0, 0)

        def steady(k, carry):
            p = 2 * k + 2
            scores(p, 0)
            grads(p - 1, 1)
            accumulate(p - 2, 0)
            scores(p + 1, 1)
            grads(p, 0)
            accumulate(p - 1, 1)
            return carry

        lax.fori_loop(0, (npair - 2) // 2, steady, 0)
        grads(npair - 1, 1)
        accumulate(npair - 2, 0)
        accumulate(npair - 1, 1)

        bk = bk_ref[...]
        lane = lax.broadcasted_iota(jnp.int32, (8, LANES), 1)
        for h in range(2):
            db = db_acc[h * TQ:(h + 1) * TQ, :]
            acc = jnp.zeros((8, LANES), f32)
            for b in range(32):
                part = jnp.where(bk == b, db, 0.0).reshape(TQ // 8, 8, w).sum(axis=0)
                tot = jnp.sum(jnp.sum(part, axis=1, keepdims=True), axis=0, keepdims=True)
                acc = jnp.where(lane == b, tot, acc)
            dbk_ref[h] = acc

    def gcol(off):
        return pl.BlockSpec((None, None, SP, LANES), lambda hp: (g, off + hp, 0, 0))

    row = pl.BlockSpec((S, LANES), lambda hp: (0, hp))
    return pl.pallas_call(
        body,
        grid=(4,),
        in_specs=[gcol(0), gcol(4), gcol(8), pl.BlockSpec((2, TQ, w), lambda hp: (hp, 0, 0)),
                  pl.BlockSpec((TQ, w), lambda hp: (0, 0)), row, row, row],
        out_specs=[pl.BlockSpec((3, None, SP, LANES), lambda hp: (0, hp, 0, 0)),
                   pl.BlockSpec((2, 8, LANES), lambda hp: (hp, 0, 0))],
        out_shape=[
            jax.ShapeDtypeStruct((3, 4, SP, LANES), f32),
            jax.ShapeDtypeStruct((8, 8, LANES), f32),
        ],
        scratch_shapes=([pltpu.VMEM((2 * TQ, w), f32)] + [pltpu.VMEM((2, 2 * TQ, w), f32)] * 4
                        + [pltpu.VMEM((2, 2 * TQ, w), bf16)] * 4),
        compiler_params=_params(("arbitrary",), vmem_mib=56),
        name=name,
    )(gl, gl, gl, bias, bucket, do, lse, dd)


def _sigmoid(z):
    return 1.0 / (1.0 + jnp.exp(-z))


def _tail(x, tgt, o_a, l_a, o_b, l_b, proj, bm, w_a, w_b, w_o, sink_b):
    ts = 128

    def body(x_ref, t_ref, oa_ref, la_ref, ob0_ref, ob1_ref, ob2_ref, lb0_ref, lb1_ref, lb2_ref,
             ga_ref, gb_ref, m0_ref, m1_ref, bm_ref, wa_ref, wb_ref, wo_ref, sk_ref,
             dy_ref, dyb_ref, dt_ref, doa_ref, dda_ref, dob0_ref, dob1_ref, dob2_ref, ddb0_ref, ddb1_ref, ddb2_ref,
             ya_ref, yb_ref, mg_ref, dbra_ref, dbrb_ref, loss_ref, dbm_ref, dsk_ref):
        i = pl.program_id(0)

        @pl.when(i == 0)
        def _():
            loss_ref[...] = jnp.zeros_like(loss_ref)
            dbm_ref[...] = jnp.zeros_like(dbm_ref)
            dsk_ref[...] = jnp.zeros_like(dsk_ref)

        ga = ga_ref[...]
        sa = _sigmoid(ga)
        silu_a = ga * sa
        oa = oa_ref[...]
        ya = oa * silu_a
        gb = gb_ref[...]
        sb = _sigmoid(gb)
        silu_b = gb * sb
        ob = [ob0_ref[...], ob1_ref[...], ob2_ref[...]]
        lb = [lb0_ref[...], lb1_ref[...], lb2_ref[...]]
        mx = jnp.maximum(jnp.maximum(lb[0], lb[1]), lb[2])
        ex = [jnp.exp(v - mx) for v in lb]
        den = ex[0] + ex[1] + ex[2]
        alpha = [e / den for e in ex]
        ybc = alpha[0] * ob[0] + alpha[1] * ob[1] + alpha[2] * ob[2]
        yb = ybc * silu_b
        yab = ya.astype(bf16)
        ybb = yb.astype(bf16)
        br_a = jnp.dot(yab, wa_ref[...], preferred_element_type=f32)
        br_b = jnp.dot(ybb, wb_ref[...], preferred_element_type=f32)
        g0 = _sigmoid(m0_ref[...] + bm_ref[0:1, :])
        g1 = _sigmoid(m1_ref[...] + bm_ref[1:2, :])
        merged = g0 * br_a + g1 * br_b
        mgb = merged.astype(bf16)
        y = x_ref[...] + jnp.dot(mgb, wo_ref[...], preferred_element_type=f32)
        err = y - t_ref[...]
        part = jnp.sum(jnp.sum(err * err, axis=1, keepdims=True), axis=0, keepdims=True)
        loss_ref[...] += part * (0.5 / D)
        dy = err * (1.0 / D)
        dyb = dy.astype(bf16)
        dmerged = lax.dot_general(dyb, wo_ref[...], NT_DIMS, preferred_element_type=f32)
        dbr_a = (dmerged * g0).astype(bf16)
        dbr_b = (dmerged * g1).astype(bf16)
        dm0 = dmerged * br_a * (g0 * (1.0 - g0))
        dm1 = dmerged * br_b * (g1 * (1.0 - g1))
        dbm_ref[0:1, :] += jnp.sum(dm0, axis=0, keepdims=True)
        dbm_ref[1:2, :] += jnp.sum(dm1, axis=0, keepdims=True)
        dya = lax.dot_general(dbr_a, wa_ref[...], NT_DIMS, preferred_element_type=f32)
        dyb2 = lax.dot_general(dbr_b, wb_ref[...], NT_DIMS, preferred_element_type=f32)
        do_a = dya * silu_a
        dga = dya * oa * (sa * (1.0 + ga * (1.0 - sa)))
        ones = _head_ones()
        delta_a = _seg_sum(do_a * oa, ones)
        dsk_ref[...] -= jnp.sum(delta_a * jnp.exp(sk_ref[...] - la_ref[...]), axis=0, keepdims=True)
        dybc = dyb2 * silu_b
        dgb = dyb2 * ybc * (sb * (1.0 + gb * (1.0 - sb)))
        dbar = _seg_sum(dybc * ybc, ones)
        dy_ref[...] = dy
        dyb_ref[...] = dyb
        dt_ref[:, 0:512] = dga.astype(bf16)
        dt_ref[:, 512:1024] = dgb.astype(bf16)
        dt_ref[:, 1024:2048] = dm0.astype(bf16)
        dt_ref[:, 2048:3072] = dm1.astype(bf16)
        doa_ref[...] = do_a.astype(bf16)
        dda_ref[...] = delta_a
        for k, (dob_ref, ddb_ref) in enumerate(((dob0_ref, ddb0_ref), (dob1_ref, ddb1_ref), (dob2_ref, ddb2_ref))):
            dob_ref[...] = alpha[k] * dybc
            ddb_ref[...] = alpha[k] * dbar
        ya_ref[...] = yab
        yb_ref[...] = ybb
        mg_ref[...] = mgb
        dbra_ref[...] = dbr_a
        dbrb_ref[...] = dbr_b

    def rows(n, blk=0):
        return pl.BlockSpec((ts, n), lambda i: (i, blk))

    def whole(r, c):
        return pl.BlockSpec((r, c), lambda i: (0, 0))

    def gate_cols(n, col):
        return pl.BlockSpec((pl.Element(ts), pl.Element(n)), lambda i: (i * ts, NA + col))

    outs = [
        ((S, D), f32, rows(D)), ((S, D), bf16, rows(D)), ((S, NW), bf16, gate_cols(NT, 0)),
        ((S, 512), bf16, rows(512)), ((S, 512), f32, rows(512)),
        ((S, 512), f32, rows(512)), ((S, 512), f32, rows(512)), ((S, 512), f32, rows(512)),
        ((S, 512), f32, rows(512)), ((S, 512), f32, rows(512)), ((S, 512), f32, rows(512)),
        ((S, 512), bf16, rows(512)), ((S, 512), bf16, rows(512)), ((S, D), bf16, rows(D)),
        ((S, D), bf16, rows(D)), ((S, D), bf16, rows(D)),
        ((1, 1), f32, whole(1, 1)), ((2, D), f32, whole(2, D)), ((1, 512), f32, whole(1, 512)),
    ]
    return pl.pallas_call(
        body,
        grid=(S // ts,),
        in_specs=[
            rows(D), rows(D), rows(512), rows(512), rows(512), rows(512), rows(512), rows(512), rows(512), rows(512),
            gate_cols(512, 0), gate_cols(512, 512), gate_cols(D, 1024), gate_cols(D, 2048), whole(2, D),
            whole(512, D), whole(512, D), whole(D, D), whole(1, 512),
        ],
        out_specs=[o[2] for o in outs],
        out_shape=[jax.ShapeDtypeStruct(o[0], o[1]) for o in outs],
        compiler_params=_params(("arbitrary",)),
        name="tail",
    )(x, tgt, o_a, l_a, *o_b, *l_b, proj, proj, proj, proj, bm, w_a, w_b, w_o, sink_b)


def _norm_bwd(xv, dyv, gain, kind, ones):
    r = lax.rsqrt(_half_sums(xv * xv, ones) * (1.0 / HD) + EPS)
    yv = xv * r
    up = jnp.where(kind == 0, dyv * SCALE, dyv)
    u = up * gain
    dxv = r * (u - yv * (_half_sums(u * yv, ones) * (1.0 / HD)))
    dxv = jnp.where(kind == 2, dyv, dxv)
    dg = jnp.where(kind == 2, 0.0, jnp.sum(up * yv, axis=0, keepdims=True))
    return dxv, dg


def _post_b(g, dqkv, proj_a, gains, dproj):
    d = GROUPS[g][1]
    seq = S // d

    def body(d_ref, p_ref, g_ref, alias_ref, o_ref, dg_ref, nat):
        del alias_ref
        j = pl.program_id(0)
        kind = j // 4
        gain = g_ref[...]
        ones = _head_ones()

        @pl.when(j % 4 == 0)
        def _():
            dg_ref[...] = jnp.zeros_like(dg_ref)

        for c in range(d):
            for i in range(seq // PCHUNK):
                src = c * seq + i * PCHUNK
                if d == 1:
                    idx = slice(src, src + PCHUNK)
                else:
                    idx = pl.ds(c + i * PCHUNK * d, PCHUNK, stride=d)
                dxv, dg = _norm_bwd(p_ref[idx, :], d_ref[PAD + src:PAD + src + PCHUNK, :], gain, kind, ones)
                nat[idx, :] = dxv
                dg_ref[...] += dg

        for i in range(S // CHUNK):
            o_ref[i * CHUNK:(i + 1) * CHUNK, :] = nat[i * CHUNK:(i + 1) * CHUNK, :].astype(bf16)

    return pl.pallas_call(
        body,
        grid=(12,),
        in_specs=[
            pl.BlockSpec((None, None, SP, LANES), lambda j: (j // 4, j % 4, 0, 0)),
            pl.BlockSpec((S, LANES), lambda j: (0, _col_block(g, j))),
            pl.BlockSpec((None, None, 1, LANES), lambda j: (g, j // 4, 0, 0)),
            pl.BlockSpec(memory_space=pl.ANY),
        ],
        out_specs=[
            pl.BlockSpec((S, LANES), lambda j: (0, _col_block(g, j))),
            pl.BlockSpec((None, 1, LANES), lambda j: (j // 4, 0, 0)),
        ],
        out_shape=[jax.ShapeDtypeStruct((S, NW), bf16), jax.ShapeDtypeStruct((3, 1, LANES), f32)],
        scratch_shapes=[pltpu.VMEM((S, LANES), f32)],
        input_output_aliases={3: 0},
        compiler_params=_params(("arbitrary",)),
        name="post_b%d" % g,
    )(dqkv, proj_a, gains, dproj)


def _post_a(dqkv, proj_a, gains, dproj):
    def body(q_ref, e_ref, p_ref, g_ref, alias_ref, o_ref, dg_ref):
        del alias_ref
        j = pl.program_id(0)
        kind = jnp.maximum(j - 3, 0)
        gain = g_ref[...]
        lo = _lo()
        ones = _head_ones()

        @pl.when((j == 0) | (j >= 4))
        def _():
            dg_ref[...] = jnp.zeros_like(dg_ref)

        for i in range(S // PCHUNK):
            r0 = i * PCHUNK
            rows = slice(PAD + r0, PAD + r0 + PCHUNK)
            t0 = e_ref[0, rows, :] + e_ref[1, rows, :]
            t1 = e_ref[2, rows, :] + e_ref[3, rows, :]
            folded = jnp.where(lo, t0 + pltpu.roll(t0, HD, 1), t1 + pltpu.roll(t1, HD, 1))
            dyv = jnp.where(kind == 0, q_ref[rows, :], folded)
            dxv, dg = _norm_bwd(p_ref[r0:r0 + PCHUNK, :], dyv, gain, kind, ones)
            o_ref[r0:r0 + PCHUNK, :] = dxv.astype(bf16)
            dg_ref[...] += dg

    return pl.pallas_call(
        body,
        grid=(6,),
        in_specs=[
            pl.BlockSpec((None, None, SP, LANES), lambda j: (0, jnp.minimum(j, 3), 0, 0)),
            pl.BlockSpec((None, 4, SP, LANES), lambda j: (jnp.clip(j - 3, 1, 2), 0, 0, 0)),
            pl.BlockSpec((S, LANES), lambda j: (0, j)),
            pl.BlockSpec((None, None, 1, LANES), lambda j: (0, jnp.maximum(j - 3, 0), 0, 0)),
            pl.BlockSpec(memory_space=pl.ANY),
        ],
        out_specs=[
            pl.BlockSpec((S, LANES), lambda j: (0, j)),
            pl.BlockSpec((None, 1, LANES), lambda j: (jnp.maximum(j - 3, 0), 0, 0)),
        ],
        out_shape=[jax.ShapeDtypeStruct((S, NW), bf16), jax.ShapeDtypeStruct((3, 1, LANES), f32)],
        input_output_aliases={4: 0},
        compiler_params=_params(("arbitrary",)),
        name="post_a",
    )(dqkv, dqkv, proj_a, gains, dproj)


def _dh_norm_bwd(dproj, w, x, rstd, gain, dy):
    ts = 1024
    tk = NW // 6
    nk = NW // tk

    def body(d_ref, w_ref, x_ref, r_ref, g_ref, dy_ref, gx_ref, dgn_ref, acc):
        i = pl.program_id(0)
        k = pl.program_id(1)

        @pl.when((i == 0) & (k == 0))
        def _():
            dgn_ref[...] = jnp.zeros_like(dgn_ref)

        @pl.when(k == 0)
        def _():
            acc[...] = jnp.zeros_like(acc)

        acc[...] += lax.dot_general(d_ref[...], w_ref[...], NT_DIMS, preferred_element_type=f32)

        @pl.when(k == nk - 1)
        def _():
            dh = acc[...]
            xh = x_ref[...] * r_ref[...]
            u = dh * g_ref[...]
            dx = r_ref[...] * (u - xh * jnp.mean(u * xh, axis=-1, keepdims=True))
            gx_ref[...] = dy_ref[...] + dx
            dgn_ref[...] += jnp.sum(dh * xh, axis=0, keepdims=True)

    return pl.pallas_call(
        body,
        grid=(S // ts, nk),
        in_specs=[
            pl.BlockSpec((ts, tk), lambda i, k: (i, k)),
            pl.BlockSpec((D, tk), lambda i, k: (0, k)),
            pl.BlockSpec((ts, D), lambda i, k: (i, 0)),
            pl.BlockSpec((ts, 1), lambda i, k: (i, 0)),
            pl.BlockSpec((1, D), lambda i, k: (0, 0)),
            pl.BlockSpec((ts, D), lambda i, k: (i, 0)),
        ],
        out_specs=[pl.BlockSpec((ts, D), lambda i, k: (i, 0)), pl.BlockSpec((1, D), lambda i, k: (0, 0))],
        out_shape=[jax.ShapeDtypeStruct((S, D), f32), jax.ShapeDtypeStruct((1, D), f32)],
        scratch_shapes=[pltpu.VMEM((ts, D), f32)],
        compiler_params=_params(("arbitrary", "arbitrary"), vmem_mib=56),
        name="dh_norm_bwd",
    )(dproj, w, x, rstd, gain, dy)


def _dw_in(hbt, dproj):
    tk = 1024
    win = WSH + 96

    def body(a_ref, b_ref, o_ref, acc):
        j = pl.program_id(0)
        k = pl.program_id(1)

        @pl.when(k == 0)
        def _():
            acc[...] = jnp.zeros_like(acc)

        for jj in range(NDEV):
            off = (WSH * jj) % LANES

            @pl.when(j == jj)
            def _():
                acc[...] += jnp.dot(a_ref[...], b_ref[:, off:off + WSH], preferred_element_type=f32)

        @pl.when(k == S // tk - 1)
        def _():
            o_ref[...] = acc[...].astype(bf16)

    return pl.pallas_call(
        body,
        grid=(NDEV, S // tk),
        in_specs=[
            pl.BlockSpec((D, tk), lambda j, k: (0, k)),
            pl.BlockSpec((pl.Element(tk), pl.Element(win)), lambda j, k: (k * tk, (WSH * j) // LANES * LANES)),
        ],
        out_specs=pl.BlockSpec((None, D, WSH), lambda j, k: (j, 0, 0)),
        out_shape=jax.ShapeDtypeStruct((NDEV, D, WSH), bf16),
        scratch_shapes=[pltpu.VMEM((D, WSH), f32)],
        compiler_params=_params(("arbitrary", "arbitrary")),
        name="dw_in",
    )(hbt, dproj)


def _matmul_tn(a, b, name):
    m, n = a.shape[1], b.shape[1]
    tn = TN if n % TN == 0 else 512
    tk = 512

    def body(a_ref, b_ref, o_ref):
        @pl.when(pl.program_id(1) == 0)
        def _():
            o_ref[...] = jnp.zeros_like(o_ref)

        o_ref[...] += lax.dot_general(a_ref[...], b_ref[...], TN_DIMS, preferred_element_type=f32)

    return pl.pallas_call(
        body,
        grid=(n // tn, S // tk),
        in_specs=[pl.BlockSpec((tk, m), lambda j, k: (k, 0)), pl.BlockSpec((tk, tn), lambda j, k: (k, j))],
        out_specs=pl.BlockSpec((m, tn), lambda j, k: (0, j)),
        out_shape=jax.ShapeDtypeStruct((m, n), f32),
        compiler_params=_params(("arbitrary", "arbitrary")),
        name=name,
    )(a, b)


def _exchange(scatter, gather, name):
    arrs = list(scatter) + list(gather)
    n = len(arrs)
    ns = len(scatter)

    def body(*refs):
        ins, outs = refs[:n], refs[n:2 * n]
        send_sems, recv_sems, local_sems = refs[2 * n:]
        x, y, c = lax.axis_index("x"), lax.axis_index("y"), lax.axis_index("c")
        me = 4 * x + 2 * y + c
        local, remote = [], []
        for a in range(n):
            lc = pltpu.make_async_copy(ins[a].at[me] if a < ns else ins[a], outs[a].at[me], local_sems.at[a])
            lc.start()
            local.append(lc)
            for r in range(1, NDEV):
                px = 1 - x if r & 4 else x
                py = 1 - y if r & 2 else y
                pc = 1 - c if r & 1 else c
                cp = pltpu.make_async_remote_copy(
                    src_ref=ins[a].at[4 * px + 2 * py + pc] if a < ns else ins[a],
                    dst_ref=outs[a].at[me],
                    send_sem=send_sems.at[a, r - 1],
                    recv_sem=recv_sems.at[a, r - 1],
                    device_id=(px, py, pc),
                    device_id_type=pl.DeviceIdType.MESH,
                )
                cp.start()
                remote.append(cp)
        for cp in remote:
            cp.wait_recv()
        for cp in remote:
            cp.wait_send()
        for lc in local:
            lc.wait()

    out_shape = [jax.ShapeDtypeStruct(a.shape if i < ns else (NDEV,) + a.shape, a.dtype) for i, a in enumerate(arrs)]
    return pl.pallas_call(
        body,
        in_specs=[pl.BlockSpec(memory_space=pl.ANY)] * n,
        out_specs=[pl.BlockSpec(memory_space=pl.ANY)] * n,
        out_shape=out_shape,
        scratch_shapes=[
            pltpu.SemaphoreType.DMA((n, NDEV - 1)),
            pltpu.SemaphoreType.DMA((n, NDEV - 1)),
            pltpu.SemaphoreType.DMA((n,)),
        ],
        compiler_params=pltpu.CompilerParams(has_side_effects=True),
        name=name,
    )(*arrs)


def _gather_two_level(arrs, name):
    n = len(arrs)

    def body(*refs):
        ins, outs = refs[:n], refs[n:2 * n]
        send_sems, recv_sems, local_sems = refs[2 * n:]
        x, y, c = lax.axis_index("x"), lax.axis_index("y"), lax.axis_index("c")
        me, sibling = (x, y, c), (x, y, 1 - c)
        chips = [(1 - x, y), (x, 1 - y), (1 - x, 1 - y)]

        def copy(a, k, block, to, src=None):
            slot = outs[a].at[4 * block[0] + 2 * block[1] + block[2]]
            return pltpu.make_async_remote_copy(
                src_ref=slot if src is None else src, dst_ref=slot, send_sem=send_sems.at[a, k],
                recv_sem=recv_sems.at[a, k], device_id=to, device_id_type=pl.DeviceIdType.MESH)

        mine, first, passed = [], [], []
        for a in range(n):
            lc = pltpu.make_async_copy(ins[a], outs[a].at[4 * x + 2 * y + c], local_sems.at[a])
            lc.start()
            mine.append(lc)
            first.append(copy(a, 0, me, sibling, src=ins[a]))
            first += [copy(a, 1 + j, me, (*chip, c), src=ins[a]) for j, chip in enumerate(chips)]
        for cp in first:
            cp.start()
        for j, chip in enumerate(chips):
            for a in range(n):
                copy(a, 1 + j, (*chip, c), me).wait_recv()
                fwd = copy(a, 4 + j, (*chip, c), sibling)
                fwd.start()
                passed.append(fwd)
        for a in range(n):
            copy(a, 0, sibling, me).wait_recv()
        for j, chip in enumerate(chips):
            for a in range(n):
                copy(a, 4 + j, (*chip, 1 - c), me).wait_recv()
        for cp in first + passed:
            cp.wait_send()
        for lc in mine:
            lc.wait()

    return pl.pallas_call(
        body,
        in_specs=[pl.BlockSpec(memory_space=pl.ANY)] * n,
        out_specs=[pl.BlockSpec(memory_space=pl.ANY)] * n,
        out_shape=[jax.ShapeDtypeStruct((NDEV,) + a.shape, a.dtype) for a in arrs],
        scratch_shapes=[
            pltpu.SemaphoreType.DMA((n, NDEV - 1)),
            pltpu.SemaphoreType.DMA((n, NDEV - 1)),
            pltpu.SemaphoreType.DMA((n,)),
        ],
        compiler_params=pltpu.CompilerParams(has_side_effects=True),
        name=name,
    )(*arrs)


_HBM = pl.BlockSpec(memory_space=pltpu.HBM)
_SEM = pl.BlockSpec(memory_space=pltpu.SEMAPHORE)
_EFFECT = pltpu.SideEffectType.DATAFLOW_SIDE_EFFECTING


def _sibling_exchange(g, name):
    def body(in_ref, out_ref, send_sems, recv_sems):
        x, y, c = lax.axis_index("x"), lax.axis_index("y"), lax.axis_index("c")
        copies = []
        for q in range(4):
            cp = pltpu.make_async_remote_copy(
                src_ref=in_ref.at[2 * q + (1 - c)], dst_ref=out_ref.at[q], send_sem=send_sems.at[q],
                recv_sem=recv_sems.at[q], device_id=(x, y, 1 - c), device_id_type=pl.DeviceIdType.MESH)
            cp.start()
            copies.append(cp)
        for cp in copies:
            cp.wait_recv()
        for cp in copies:
            cp.wait_send()

    return pl.pallas_call(
        body,
        in_specs=[pl.BlockSpec(memory_space=pl.ANY)],
        out_specs=pl.BlockSpec(memory_space=pl.ANY),
        out_shape=jax.ShapeDtypeStruct((4,) + g.shape[1:], g.dtype),
        scratch_shapes=[pltpu.SemaphoreType.DMA((4,)), pltpu.SemaphoreType.DMA((4,))],
        compiler_params=pltpu.CompilerParams(has_side_effects=True),
        name=name,
    )(g)


def _pair_sum(g, r, core, name):
    _, rows, cols = g.shape
    tr = 256

    def body(c_ref, g_ref, r_ref, o_ref):
        del c_ref
        o_ref[...] = (g_ref[...].astype(f32) + r_ref[...].astype(f32)).astype(bf16)

    return pl.pallas_call(
        body,
        grid_spec=pltpu.PrefetchScalarGridSpec(
            num_scalar_prefetch=1,
            grid=(4, rows // tr),
            in_specs=[pl.BlockSpec((None, tr, cols), lambda q, i, c_ref: (2 * q + c_ref[0], i, 0)),
                      pl.BlockSpec((None, tr, cols), lambda q, i, c_ref: (q, i, 0))],
            out_specs=pl.BlockSpec((None, tr, cols), lambda q, i, c_ref: (q, i, 0)),
        ),
        out_shape=jax.ShapeDtypeStruct((4, rows, cols), bf16),
        compiler_params=_params(("arbitrary", "arbitrary")),
        name=name,
    )(core, g, r)


def _scatter_start(chip_arrs, all_arrs, name):
    arrs = list(chip_arrs) + list(all_arrs)
    n, nc = len(arrs), len(chip_arrs)
    lands = [lax.empty(((3 if i < nc else NDEV - 1),) + a.shape[1:], a.dtype) for i, a in enumerate(arrs)]

    def body(*refs):
        src, land = refs[:n], refs[n:2 * n]
        send_sems, recv_sems = refs[2 * n:3 * n], refs[3 * n:4 * n]
        token = refs[6 * n]
        x, y, c = lax.axis_index("x"), lax.axis_index("y"), lax.axis_index("c")
        for a in range(n):
            for r in range(1, 4 if a < nc else NDEV):
                if a < nc:
                    px, py, pc = (1 - x if r & 2 else x), (1 - y if r & 1 else y), c
                    block = 2 * px + py
                else:
                    px, py, pc = (1 - x if r & 4 else x), (1 - y if r & 2 else y), (1 - c if r & 1 else c)
                    block = 4 * px + 2 * py + pc
                pltpu.make_async_remote_copy(
                    src_ref=src[a].at[block], dst_ref=land[a].at[r - 1], send_sem=send_sems[a],
                    recv_sem=recv_sems[a], device_id=(px, py, pc), device_id_type=pl.DeviceIdType.MESH).start()
        token[...] = jnp.zeros_like(token)

    hbm = [pltpu.HBM(a.shape, a.dtype) for a in arrs + lands]
    ops = [pltpu.with_memory_space_constraint(a, pltpu.HBM) for a in arrs + lands]
    outs = pl.pallas_call(
        body,
        out_shape=tuple([pltpu.SemaphoreType.DMA(())] * (2 * n) + hbm + [jax.ShapeDtypeStruct((8, LANES), f32)]),
        in_specs=[_HBM] * (2 * n),
        out_specs=tuple([_SEM] * (2 * n) + [_HBM] * (2 * n) + [pl.BlockSpec(memory_space=pltpu.VMEM)]),
        input_output_aliases={i: 2 * n + i for i in range(2 * n)},
        compiler_params=pltpu.CompilerParams(has_side_effects=_EFFECT),
        name=name,
    )(*ops)
    return outs[:n], outs[n:2 * n], outs[2 * n:3 * n], outs[3 * n:4 * n], outs[4 * n]


def _scatter_wait(send_sems, recv_sems, srcs, lands, after, name):
    n = len(srcs)

    def body(*refs):
        land = refs[n:2 * n]
        ssem, rsem = refs[2 * n:3 * n], refs[3 * n:4 * n]
        x, y, c = lax.axis_index("x"), lax.axis_index("y"), lax.axis_index("c")
        for a in range(n):
            done = pltpu.make_async_remote_copy(
                src_ref=land[a], dst_ref=land[a], send_sem=ssem[a], recv_sem=rsem[a], device_id=(x, y, c),
                device_id_type=pl.DeviceIdType.MESH)
            done.wait_send()
            done.wait_recv()

    hbm = [pltpu.HBM(a.shape, a.dtype) for a in list(srcs) + list(lands)]
    outs = pl.pallas_call(
        body,
        out_shape=tuple(hbm),
        in_specs=[_HBM] * (2 * n) + [_SEM] * (2 * n) + [pl.BlockSpec(memory_space=pl.ANY)],
        out_specs=tuple([_HBM] * (2 * n)),
        input_output_aliases={i: i for i in range(2 * n)},
        compiler_params=pltpu.CompilerParams(has_side_effects=_EFFECT),
        name=name,
    )(*srcs, *lands, *send_sems, *recv_sems, after)
    return outs[:n], outs[n:]


def _adam_update(g, w_ref, m_ref, v_ref, g_ref, d_ref, nm_ref, nv_ref):
    mm = ADAM_B1 * m_ref[...] + (1.0 - ADAM_B1) * g
    vv = ADAM_B2 * v_ref[...] + (1.0 - ADAM_B2) * (g * g)
    m_hat = mm / (1.0 - ADAM_B1 ** ADAM_STEP)
    v_hat = vv / (1.0 - ADAM_B2 ** ADAM_STEP)
    g_ref[...] = g
    d_ref[...] = -ADAM_LR * (m_hat / (jnp.sqrt(v_hat) + ADAM_EPS) + ADAM_WD * w_ref[...])
    nm_ref[...] = mm
    nv_ref[...] = vv


def _adamw_own(w, own, own_idx, slots, m, v, name):
    r, c = w.shape[-2:]
    tr = 128 if r % 128 == 0 else r
    k = slots.shape[0]

    def body(i_ref, w_ref, o_ref, s_ref, m_ref, v_ref, g_ref, d_ref, nm_ref, nv_ref):
        del i_ref
        g = o_ref[...].astype(f32)
        for j in range(k):
            g = g + s_ref[j].astype(f32)
        _adam_update(g, w_ref, m_ref, v_ref, g_ref, d_ref, nm_ref, nv_ref)

    blk = pl.BlockSpec((None, tr, c), lambda i, ix: (0, i, 0))
    return pl.pallas_call(
        body,
        grid_spec=pltpu.PrefetchScalarGridSpec(
            num_scalar_prefetch=1,
            grid=(r // tr,),
            in_specs=[blk, pl.BlockSpec((None, tr, c), lambda i, ix: (ix[0], i, 0)),
                      pl.BlockSpec((k, tr, c), lambda i, ix: (0, i, 0)), blk, blk],
            out_specs=[blk] * 4,
        ),
        out_shape=[jax.ShapeDtypeStruct(w.shape, f32)] * 4,
        compiler_params=_params(("arbitrary",)),
        name=name,
    )(own_idx, w, own, slots, m, v)


def _adamw(w, slots, m, v, name):
    r, c = w.shape[-2:]
    tr = 128 if r % 128 == 0 else r

    def body(w_ref, s_ref, m_ref, v_ref, g_ref, d_ref, nm_ref, nv_ref):
        g = s_ref[0].astype(f32)
        for k in range(1, NDEV):
            g = g + s_ref[k].astype(f32)
        _adam_update(g, w_ref, m_ref, v_ref, g_ref, d_ref, nm_ref, nv_ref)

    if w.ndim == 3:
        blk = pl.BlockSpec((None, tr, c), lambda i: (0, i, 0))
    else:
        blk = pl.BlockSpec((tr, c), lambda i: (i, 0))
    return pl.pallas_call(
        body,
        grid=(r // tr,),
        in_specs=[blk, pl.BlockSpec((NDEV, tr, c), lambda i: (0, i, 0)), blk, blk],
        out_specs=[blk] * 4,
        out_shape=[jax.ShapeDtypeStruct(w.shape, f32)] * 4,
        compiler_params=_params(("arbitrary",)),
        name=name,
    )(w, slots, m, v)


def _local_step(x, tgt, norm_gain, w_shards, qn_a, kn_a, qn_b, kn_b, sink_a, rel_bias, w_a, w_b, b_merge, w_o,
                on_weight_grads=None):
    two = lambda t: jnp.concatenate([t, t], axis=-1).reshape(1, LANES)
    ones = jnp.ones((1, LANES), f32)
    gains = jnp.stack([
        jnp.stack([two(qn_a), two(kn_a), ones]),
        jnp.stack([two(qn_b), two(kn_b), ones]),
        jnp.stack([two(qn_b), two(kn_b), ones]),
        jnp.stack([two(qn_b), two(kn_b), ones]),
    ])
    buckets = [jnp.asarray(_bucket_np(blk, d)) for blk, d, _ in GROUPS]
    bias = [_bias_expand(rel_bias, buckets[k], GROUPS[k][2], "bias_expand_%d" % k) for k in range(4)]

    hb, hbt, rstd = _rms(x, norm_gain)
    w_in = _assemble_w(w_shards)
    proj = _inproj(hb, w_in)
    gl = _prep(proj, gains)
    o_a, l_a = _attn_fwd(gl, bias[0], sink_a.reshape(8), 0, 128, 1, "attn_fwd_a")
    fwd_b = [_attn_fwd(gl, bias[k], None, k, GROUPS[k][0], GROUPS[k][1], "attn_fwd_b%d" % k) for k in (1, 2, 3)]
    sink_b = jnp.repeat(sink_a.reshape(8), HD).reshape(1, 512)

    (dy, dyb, dproj, do_a, dd_a, do_b0, do_b1, do_b2, dd_b0, dd_b1, dd_b2, ya, yb, mg, dbr_a, dbr_b, loss, dbm,
     dsk) = _tail(x, tgt, o_a, l_a, [f[0] for f in fwd_b], [f[1] for f in fwd_b], proj, b_merge, w_a, w_b, w_o, sink_b)

    dqkv_a, dbk_a = _attn_bwd(gl, bias[0], buckets[0], do_a, l_a, dd_a, 0, 128, 1, "attn_bwd_a")
    dproj, dg_a = _post_a(dqkv_a, proj, gains, dproj)
    dbk_b, dg_b = [], []
    for k, do_k, dd_k in ((1, do_b0, dd_b0), (2, do_b1, dd_b1), (3, do_b2, dd_b2)):
        dqkv, dbk = _attn_bwd(gl, bias[k], buckets[k], do_k, fwd_b[k - 1][1], dd_k, k, GROUPS[k][0], GROUPS[k][1],
                              "attn_bwd_b%d" % k)
        dproj, dg = _post_b(k, dqkv, proj, gains, dproj)
        dbk_b.append(dbk)
        dg_b.append(dg)
    dg_b = jnp.stack(dg_b)

    dw_in = _dw_in(hbt, dproj)
    dw_o = _matmul_tn(mg, dyb, "dw_out")
    dw_a = _matmul_tn(ya, dbr_a, "dw_branch_a")
    dw_b = _matmul_tn(yb, dbr_b, "dw_branch_b")
    token = jnp.zeros((), f32) if on_weight_grads is None else on_weight_grads(
        dict(w_in=dw_in, w_branch_a=dw_a, w_branch_b=dw_b, b_merge=dbm, w_out=dw_o))
    grad_x, d_norm_gain = _dh_norm_bwd(dproj, w_in, x, rstd, norm_gain + token, dy)

    fold = lambda t: t[..., :HD] + t[..., HD:]
    d_qn_a = fold(dg_a[0, 0])
    d_kn_a = fold(dg_a[1, 0])
    d_qn_b = fold(dg_b[:, 0, 0].sum(axis=0))
    d_kn_b = fold(dg_b[:, 1, 0].sum(axis=0))
    d_sink = dsk.reshape(8, HD)[:, 0]
    red = jnp.stack([dbk_a] + dbk_b)
    d_rel = red[:, :, 0, :32].reshape(32, 32).T
    return dict(loss=loss, grad_x=grad_x, norm_gain=d_norm_gain, w_in=dw_in, q_norm_a=d_qn_a, k_norm_a=d_kn_a,
                q_norm_b=d_qn_b, k_norm_b=d_kn_b, sink_a=d_sink, rel_bias=d_rel, w_branch_a=dw_a, w_branch_b=dw_b,
                b_merge=dbm, w_out=dw_o)


SMALL = (("norm_gain", D), ("q_norm_a", HD), ("k_norm_a", HD), ("q_norm_b", HD), ("k_norm_b", HD), ("sink_a", 8),
         ("rel_bias", 1024))
SMALL_PAD = 2432


SMALL_USED = sum(sz for _, sz in SMALL)


def _pack_small(parts, loss=None):
    tail = jnp.zeros((SMALL_PAD - SMALL_USED,), f32)
    if loss is not None:
        tail = tail.at[0].set(loss.reshape(()))
    return jnp.concatenate([parts[n].reshape(-1) for n, _ in SMALL] + [tail]).reshape(1, SMALL_PAD)


def _unpack_small(flat, shapes):
    out, off = {}, 0
    for n, sz in SMALL:
        out[n] = flat[0, off:off + sz].reshape(shapes[n])
        off += sz
    return out


def kernel(x, norm_gain, w_in, q_norm_a, k_norm_a, q_norm_b, k_norm_b, sink_a, rel_bias, w_branch_a, w_branch_b, b_merge, w_out, loss_target, m_norm_gain, m_w_in, m_q_norm_a, m_k_norm_a, m_q_norm_b, m_k_norm_b, m_sink_a, m_rel_bias, m_w_branch_a, m_w_branch_b, m_b_merge, m_w_out, v_norm_gain, v_w_in, v_q_norm_a, v_k_norm_a, v_q_norm_b, v_k_norm_b, v_sink_a, v_rel_bias, v_w_branch_a, v_w_branch_b, v_b_merge, v_w_out):
    wsh = NW // NDEV
    csh = D // NDEV
    g_in, g_a, g_b, g_o, g_bm = _gather_two_level(
        [w_in[0].astype(bf16), w_branch_a[0].astype(bf16), w_branch_b[0].astype(bf16), w_out[0].astype(bf16),
         b_merge[0]], "gather_weights")
    w_a_full = g_a.transpose(1, 0, 2).reshape(512, D)
    w_b_full = g_b.transpose(1, 0, 2).reshape(512, D)
    w_o_full = g_o.reshape(D, D)
    bm_full = g_bm.transpose(1, 0, 2).reshape(2, D)

    pending = {}
    core = lax.axis_index("c").astype(jnp.int32).reshape(1)
    chip = (2 * lax.axis_index("x") + lax.axis_index("y")).astype(jnp.int32).reshape(1)
    me = (2 * chip + core).astype(jnp.int32)

    def start_exchange(gw):
        from_sibling = _sibling_exchange(gw["w_in"], "grad_sibling_exchange")
        chip_sums = _pair_sum(gw["w_in"], from_sibling, core, "grad_pair_sum")
        blocks = [gw["w_branch_a"].reshape(512, NDEV, csh).transpose(1, 0, 2).astype(bf16),
                  gw["w_branch_b"].reshape(512, NDEV, csh).transpose(1, 0, 2).astype(bf16),
                  gw["w_out"].reshape(NDEV, csh, D).astype(bf16),
                  gw["b_merge"].reshape(2, NDEV, csh).transpose(1, 0, 2)]
        pending["started"] = _scatter_start([chip_sums], blocks, "scatter_grads_start")
        return pending["started"][4][0, 0]

    loc = _local_step(x[0], loss_target[0], norm_gain, g_in, q_norm_a, k_norm_a, q_norm_b, k_norm_b, sink_a,
                      rel_bias, w_a_full, w_b_full, bm_full, w_o_full, on_weight_grads=start_exchange)

    small_shapes = dict(norm_gain=(1, D), q_norm_a=(1, HD), k_norm_a=(1, HD), q_norm_b=(1, HD), k_norm_b=(1, HD),
                        sink_a=(1, 8), rel_bias=(32, 32))
    (r_small,) = _exchange([], [_pack_small(loc, loc["loss"])], "gather_small_grads")
    send_sems, recv_sems, srcs, lands, _ = pending["started"]
    (s_in, s_a, s_b, s_o, s_bm), (r_in, r_a, r_b, r_o, r_bm) = _scatter_wait(
        send_sems, recv_sems, srcs, lands, r_small, "scatter_grads_wait")

    given = dict(norm_gain=norm_gain, q_norm_a=q_norm_a, k_norm_a=k_norm_a, q_norm_b=q_norm_b, k_norm_b=k_norm_b,
                 sink_a=sink_a, rel_bias=rel_bias)
    m_small = dict(norm_gain=m_norm_gain, q_norm_a=m_q_norm_a, k_norm_a=m_k_norm_a, q_norm_b=m_q_norm_b,
                   k_norm_b=m_k_norm_b, sink_a=m_sink_a, rel_bias=m_rel_bias)
    v_small = dict(norm_gain=v_norm_gain, q_norm_a=v_q_norm_a, k_norm_a=v_k_norm_a, q_norm_b=v_q_norm_b,
                   k_norm_b=v_k_norm_b, sink_a=v_sink_a, rel_bias=v_rel_bias)
    res = {
        "small": _adamw(_pack_small(given), r_small, _pack_small(m_small), _pack_small(v_small), "adamw_small"),
        "w_in": _adamw_own(w_in, s_in, chip, r_in, m_w_in, v_w_in, "adamw_w_in"),
        "w_branch_a": _adamw_own(w_branch_a, s_a, me, r_a, m_w_branch_a, v_w_branch_a, "adamw_w_branch_a"),
        "w_branch_b": _adamw_own(w_branch_b, s_b, me, r_b, m_w_branch_b, v_w_branch_b, "adamw_w_branch_b"),
        "b_merge": _adamw_own(b_merge, s_bm, me, r_bm, m_b_merge, v_b_merge, "adamw_b_merge"),
        "w_out": _adamw_own(w_out, s_o, me, r_o, m_w_out, v_w_out, "adamw_w_out"),
    }
    order = ["norm_gain", "w_in", "q_norm_a", "k_norm_a", "q_norm_b", "k_norm_b", "sink_a", "rel_bias", "w_branch_a",
             "w_branch_b", "b_merge", "w_out"]
    outs = []
    for k in range(4):
        small = _unpack_small(res["small"][k], small_shapes)
        for n in order:
            outs.append(small[n] if n in small else res[n][k])
    loss = res["small"][0][0, SMALL_USED]
    return (loss, loc["grad_x"][None], *outs)
```

```python
import math

import numpy as np
import jax
import jax.numpy as jnp
from jax import lax
from jax.experimental import pallas as pl
from jax.experimental.pallas import tpu as pltpu

f32 = jnp.float32
bf16 = jnp.bfloat16

S = 4096
D = 1024
NA = 5376
NT = 3072
NW = NA + NT
WSH = NW // 8
HD = 64
LANES = 128
EPS = 1e-6
NEG = -1e30
SCALE = HD ** -0.5
TQ = 128
PAD = 128
SP = S + 2 * PAD
NDEV = 8
GROUPS = ((128, 1, 0), (64, 1, 8), (64, 4, 16), (64, 16, 24))
CHUNK = 256
PCHUNK = 128
RC = 64
TN = 768

ADAM_LR, ADAM_B1, ADAM_B2, ADAM_EPS, ADAM_WD, ADAM_STEP = 0.001, 0.9, 0.999, 1e-08, 0.01, 10

MIB = 1024 * 1024
NT_DIMS = (((1,), (1,)), ((), ()))
TN_DIMS = (((0,), (0,)), ((), ()))


def _params(sem=None, vmem_mib=48):
    return pltpu.CompilerParams(dimension_semantics=sem, vmem_limit_bytes=vmem_mib * MIB)


def _lo():
    return lax.broadcasted_iota(jnp.int32, (1, LANES), 1) < HD


def _head_ones():
    r = lax.broadcasted_iota(jnp.int32, (LANES, LANES), 0) // HD
    c = lax.broadcasted_iota(jnp.int32, (LANES, LANES), 1) // HD
    return jnp.where(r == c, 1.0, 0.0).astype(bf16)


def _half_sums(x, ones):
    hi = x.astype(bf16)
    mid = (x - hi.astype(f32)).astype(bf16)
    return (jnp.dot(hi, ones, preferred_element_type=f32) + jnp.dot(mid, ones, preferred_element_type=f32))


def _seg_sum(x, ones):
    outs = [_half_sums(x[:, b * LANES:(b + 1) * LANES], ones) for b in range(x.shape[1] // LANES)]
    return outs[0] if len(outs) == 1 else jnp.concatenate(outs, axis=1)


def _bucket_np(blk, stride):
    w = TQ + 2 * blk
    rel = np.arange(w)[None, :] - blk - np.arange(TQ)[:, None]
    band = np.abs(rel) <= blk
    r = rel * stride
    n = np.abs(r)
    nf = np.maximum(n, 8).astype(np.float32)
    large = 8 + (np.log(nf / np.float32(8)) / np.float32(math.log(128.0)) * np.float32(8)).astype(np.int32)
    large = np.minimum(large, 15)
    b = (r > 0).astype(np.int32) * 16 + np.where(n < 8, n, large)
    return np.where(band, b, -1).astype(np.int32)


def _rms(x, gain):
    ts = 512

    def body(x_ref, g_ref, h_ref, ht_ref, r_ref):
        xv = x_ref[...]
        r = lax.rsqrt(jnp.mean(xv * xv, axis=-1, keepdims=True) + EPS)
        h = (xv * r) * g_ref[...]
        h_ref[...] = h.astype(bf16)
        ht_ref[...] = h.T.astype(bf16)
        r_ref[...] = r

    return pl.pallas_call(
        body,
        grid=(S // ts,),
        in_specs=[pl.BlockSpec((ts, D), lambda i: (i, 0)), pl.BlockSpec((1, D), lambda i: (0, 0))],
        out_specs=[pl.BlockSpec((ts, D), lambda i: (i, 0)), pl.BlockSpec((D, ts), lambda i: (0, i)),
                   pl.BlockSpec((ts, 1), lambda i: (i, 0))],
        out_shape=[jax.ShapeDtypeStruct((S, D), bf16), jax.ShapeDtypeStruct((D, S), bf16),
                   jax.ShapeDtypeStruct((S, 1), f32)],
        compiler_params=_params(("arbitrary",)),
        name="rms",
    )(x, gain)


def _assemble_w(w_shards):
    def body(wa_ref, wb_ref, o_ref):
        n = pl.program_id(0)
        for nn in range(NW // TN):
            j0 = (TN * nn) // WSH
            a = TN * nn - WSH * j0
            len1 = min(TN, WSH - a)

            @pl.when(n == nn)
            def _():
                o_ref[:, 0:len1] = wa_ref[:, a:a + len1]
                if len1 < TN:
                    o_ref[:, len1:TN] = wb_ref[:, 0:TN - len1]

    return pl.pallas_call(
        body,
        grid=(NW // TN,),
        in_specs=[
            pl.BlockSpec((None, D, WSH), lambda n: ((TN * n) // WSH, 0, 0)),
            pl.BlockSpec((None, D, WSH), lambda n: (jnp.minimum((TN * n) // WSH + 1, NDEV - 1), 0, 0)),
        ],
        out_specs=pl.BlockSpec((D, TN), lambda n: (0, n)),
        out_shape=jax.ShapeDtypeStruct((D, NW), bf16),
        compiler_params=_params(("arbitrary",)),
        name="assemble_w",
    )(w_shards, w_shards)


def _inproj(hb, w):
    ts = 2048

    def body(h_ref, w_ref, p_ref):
        p_ref[...] = jnp.dot(h_ref[...], w_ref[...], preferred_element_type=f32)

    return pl.pallas_call(
        body,
        grid=(S // ts, NW // TN),
        in_specs=[pl.BlockSpec((ts, D), lambda i, n: (i, 0)), pl.BlockSpec((D, TN), lambda i, n: (0, n))],
        out_specs=pl.BlockSpec((ts, TN), lambda i, n: (i, n)),
        out_shape=jax.ShapeDtypeStruct((S, NW), f32),
        compiler_params=_params(("arbitrary", "arbitrary")),
        name="inproj",
    )(hb, w)


def _bias_expand(table, bucket, c0, name):
    tq, w = bucket.shape

    def body(tab_ref, bk_ref, o_ref):
        h = pl.program_id(0)
        bk = bk_ref[...]

        def step(b, acc):
            return jnp.where(bk == b, tab_ref[b, c0 + h], acc)

        o_ref[...] = lax.fori_loop(0, 32, step, jnp.full((tq, w), NEG, f32))

    return pl.pallas_call(
        body,
        grid=(8,),
        in_specs=[pl.BlockSpec(memory_space=pltpu.SMEM), pl.BlockSpec((tq, w), lambda h: (0, 0))],
        out_specs=pl.BlockSpec((None, tq, w), lambda h: (h, 0, 0)),
        out_shape=jax.ShapeDtypeStruct((8, tq, w), f32),
        compiler_params=_params(("arbitrary",)),
        name=name,
    )(table, bucket)


def _col_block(g, j):
    kind = j // 4
    hp = j % 4
    a = jnp.where(kind == 0, hp, 3 + kind)
    b = 6 + 12 * kind + 4 * (g - 1) + hp
    return jnp.where(g == 0, a, b)


def _prep(proj_a, gains):
    def body(p_ref, g_ref, o_ref):
        g = pl.program_id(0)
        j = pl.program_id(1)
        kind = j // 4
        lo = _lo()
        ones = _head_ones()
        half = jnp.where(lo, 0, 1)
        take = (kind == 0) | (half == (j % 4) // 2)
        gain = g_ref[...]
        o_ref[0:PAD, :] = jnp.zeros((PAD, LANES), bf16)
        o_ref[PAD + S:SP, :] = jnp.zeros((PAD, LANES), bf16)

        def norm_store(xv, dst, dup):
            if dup:
                xv = jnp.where(take, xv, pltpu.roll(xv, HD, 1))
            r = lax.rsqrt(_half_sums(xv * xv, ones) * (1.0 / HD) + EPS)
            r = jnp.where(kind == 2, 1.0, r)
            yv = (xv * r) * gain
            yv = jnp.where(kind == 0, yv * SCALE, yv)
            o_ref[PAD + dst:PAD + dst + CHUNK, :] = yv.astype(bf16)

        for gi, (_, d, _) in enumerate(GROUPS):
            @pl.when(g == gi)
            def _():
                seq = S // d
                for c in range(d):
                    for i in range(seq // CHUNK):
                        if d == 1:
                            xv = p_ref[i * CHUNK:(i + 1) * CHUNK, :]
                        else:
                            xv = p_ref[pl.ds(c + i * CHUNK * d, CHUNK, stride=d), :]
                        norm_store(xv, c * seq + i * CHUNK, gi == 0)

    return pl.pallas_call(
        body,
        grid=(4, 12),
        in_specs=[
            pl.BlockSpec((S, LANES), lambda g, j: (0, _col_block(g, j))),
            pl.BlockSpec((None, None, 1, LANES), lambda g, j: (g, j // 4, 0, 0)),
        ],
        out_specs=pl.BlockSpec((None, None, SP, LANES), lambda g, j: (g, j, 0, 0)),
        out_shape=jax.ShapeDtypeStruct((4, 12, SP, LANES), bf16),
        compiler_params=_params(("arbitrary", "arbitrary")),
        name="prep",
    )(proj_a, gains)


def _token_rows(t, r0, n, d):
    if d == 1:
        return pl.ds(pl.multiple_of(t * TQ, TQ) + r0, n)
    per = S // d // TQ
    return pl.ds(((t % per) * TQ + r0) * d + t // per, n, stride=d)


def _stack_heads(t, lo):
    z = jnp.zeros_like(t)
    return jnp.concatenate([jnp.where(lo, t, z), jnp.where(lo, z, t)], axis=0)


def _unstack_heads(t2, lo):
    return jnp.where(lo, t2[:TQ], t2[TQ:])


def _attn_fwd(gl, bias, sink, g, blk, d, name):
    w = TQ + 2 * blk
    seq = S // d
    use_sink = sink is not None

    def body(*refs):
        if use_sink:
            sink_ref, q_ref, k_ref, v_ref, b_ref, o_ref, l_ref, s0, s1, p0, p1, lse_scr = refs
        else:
            q_ref, k_ref, v_ref, b_ref, o_ref, l_ref, s0, s1, p0, p1, lse_scr = refs
        hp = pl.program_id(0)
        lo = _lo()
        mi = lax.broadcasted_iota(jnp.int32, (1, w), 1)
        s_bufs, p_bufs = (s0, s1), (p0, p1)

        def scores(p, slot):
            for u in range(2):
                f0 = pl.multiple_of((2 * p + u) * TQ, TQ)
                q2 = _stack_heads(q_ref[pl.ds(PAD + f0, TQ), :], lo)
                kw = k_ref[pl.ds(PAD - blk + f0, w), :]
                s_bufs[slot][u] = lax.dot_general(q2, kw, NT_DIMS, preferred_element_type=f32)

        def softmax(p, slot):
            for u in range(2):
                t = 2 * p + u
                m0 = jnp.bitwise_and(pl.multiple_of(t * TQ, TQ), seq - 1)
                inside = (mi >= blk - m0) & (mi < seq + blk - m0)
                for h in range(2):
                    for r in range(TQ // RC):
                        rows = slice(h * TQ + r * RC, h * TQ + (r + 1) * RC)
                        logit = jnp.where(inside, s_bufs[slot][u, rows, :] + b_ref[h, r * RC:(r + 1) * RC, :], NEG)
                        m = jnp.max(logit, axis=1, keepdims=True)
                        e = jnp.exp(logit - m)
                        lse = m + jnp.log(jnp.sum(e, axis=1, keepdims=True))
                        if use_sink:
                            sk = sink_ref[2 * hp + h]
                            mx = jnp.maximum(lse, sk)
                            lse = mx + jnp.log(jnp.exp(lse - mx) + jnp.exp(sk - mx))
                        p_bufs[slot][u, rows, :] = (e * jnp.exp(m - lse)).astype(bf16)
                        lse_scr[u, rows, :] = jnp.broadcast_to(lse, (RC, LANES))
                l_ref[_token_rows(t, 0, TQ, d), :] = jnp.where(lo, lse_scr[u, 0:TQ, :], lse_scr[u, TQ:2 * TQ, :])

        def values(p, slot):
            for u in range(2):
                t = 2 * p + u
                vw = v_ref[pl.ds(PAD - blk + pl.multiple_of(t * TQ, TQ), w), :]
                o2 = jnp.dot(p_bufs[slot][u], vw, preferred_element_type=f32)
                o_ref[_token_rows(t, 0, TQ, d), :] = _unstack_heads(o2, lo)

        npair = S // TQ // 2
        scores(0, 0)
        scores(1, 1)
        softmax(0, 0)

        def steady(k, carry):
            p = 2 * k + 2
            scores(p, 0)
            softmax(p - 1, 1)
            values(p - 2, 0)
            scores(p + 1, 1)
            softmax(p, 0)
            values(p - 1, 1)
            return carry

        lax.fori_loop(0, (npair - 2) // 2, steady, 0)
        softmax(npair - 1, 1)
        values(npair - 2, 0)
        values(npair - 1, 1)

    in_specs = [
        pl.BlockSpec((None, None, SP, LANES), lambda hp: (g, hp, 0, 0)),
        pl.BlockSpec((None, None, SP, LANES), lambda hp: (g, 4 + hp, 0, 0)),
        pl.BlockSpec((None, None, SP, LANES), lambda hp: (g, 8 + hp, 0, 0)),
        pl.BlockSpec((2, TQ, w), lambda hp: (hp, 0, 0)),
    ]
    args = [gl, gl, gl, bias]
    if use_sink:
        in_specs = [pl.BlockSpec(memory_space=pltpu.SMEM)] + in_specs
        args = [sink] + args
    out = pl.BlockSpec((S, LANES), lambda hp: (0, hp))
    return pl.pallas_call(
        body,
        grid=(4,),
        in_specs=in_specs,
        out_specs=[out, out],
        out_shape=[jax.ShapeDtypeStruct((S, 4 * LANES), f32)] * 2,
        scratch_shapes=[pltpu.VMEM((2, 2 * TQ, w), f32), pltpu.VMEM((2, 2 * TQ, w), f32),
                        pltpu.VMEM((2, 2 * TQ, w), bf16), pltpu.VMEM((2, 2 * TQ, w), bf16),
                        pltpu.VMEM((2, 2 * TQ, LANES), f32)],
        compiler_params=_params(("arbitrary",)),
        name=name,
    )(*args)


def _attn_bwd(gl, bias, bucket, do, lse, dd, g, blk, d, name):
    w = TQ + 2 * blk
    seq = S // d

    def body(q_ref, k_ref, v_ref, b_ref, bk_ref, do_ref, l_ref, d_ref, dqkv_ref, dbk_ref,
             db_acc, s0, s1, dp0, dp1, pb0, pb1, ds0, ds1):
        lo = _lo()
        hi = jnp.logical_not(lo)
        mi = lax.broadcasted_iota(jnp.int32, (1, w), 1)
        dqkv_ref[1] = jnp.zeros((SP, LANES), f32)
        dqkv_ref[2] = jnp.zeros((SP, LANES), f32)
        db_acc[...] = jnp.zeros((2 * TQ, w), f32)
        s_bufs, dp_bufs, pb_bufs, ds_bufs = (s0, s1), (dp0, dp1), (pb0, pb1), (ds0, ds1)

        def stacked(t):
            f0 = pl.multiple_of(t * TQ, TQ)
            q2 = _stack_heads(q_ref[pl.ds(PAD + f0, TQ), :], lo)
            do2 = _stack_heads(do_ref[_token_rows(t, 0, TQ, d), :].astype(bf16), lo)
            return f0, q2, do2

        def scores(p, slot):
            for u in range(2):
                f0, q2, do2 = stacked(2 * p + u)
                win = pl.ds(PAD - blk + f0, w)
                s_bufs[slot][u] = lax.dot_general(q2, k_ref[win, :], NT_DIMS, preferred_element_type=f32)
                dp_bufs[slot][u] = lax.dot_general(do2, v_ref[win, :], NT_DIMS, preferred_element_type=f32)

        def grads(p, slot):
            for u in range(2):
                t = 2 * p + u
                m0 = jnp.bitwise_and(pl.multiple_of(t * TQ, TQ), seq - 1)
                inside = (mi >= blk - m0) & (mi < seq + blk - m0)
                for h in range(2):
                    msk = lo if h == 0 else hi
                    for r in range(TQ // RC):
                        rows = slice(h * TQ + r * RC, h * TQ + (r + 1) * RC)
                        src = _token_rows(t, r * RC, RC, d)
                        lh = jnp.max(jnp.where(msk, l_ref[src, :], -jnp.inf), axis=1, keepdims=True)
                        dh = jnp.max(jnp.where(msk, d_ref[src, :], -jnp.inf), axis=1, keepdims=True)
                        logit = jnp.where(inside, s_bufs[slot][u, rows, :] + b_ref[h, r * RC:(r + 1) * RC, :], NEG)
                        pr = jnp.exp(logit - lh)
                        ds = pr * (dp_bufs[slot][u, rows, :] - dh)
                        db_acc[rows, :] += ds
                        pb_bufs[slot][u, rows, :] = pr.astype(bf16)
                        ds_bufs[slot][u, rows, :] = ds.astype(bf16)

        def accumulate(p, slot):
            for u in range(2):
                f0, q2, do2 = stacked(2 * p + u)
                win = pl.ds(PAD - blk + f0, w)
                dsb = ds_bufs[slot][u]
                dq2 = jnp.dot(dsb, k_ref[win, :], preferred_element_type=f32)
                dqkv_ref[0, pl.ds(PAD + f0, TQ), :] = _unstack_heads(dq2, lo)
                dqkv_ref[1, win, :] += lax.dot_general(dsb, q2, TN_DIMS, preferred_element_type=f32)
                dqkv_ref[2, win, :] += lax.dot_general(pb_bufs[slot][u], do2, TN_DIMS, preferred_element_type=f32)

        npair = S // TQ // 2
        scores(0, 0)
        scores(1, 1)
        grads(0, 0)

        def steady(k, carry):
            p = 2 * k + 2
            scores(p, 0)
            grads(p - 1, 1)
            accumulate(p - 2, 0)
            scores(p + 1, 1)
            grads(p, 0)
            accumulate(p - 1, 1)
            return carry

        lax.fori_loop(0, (npair - 2) // 2, steady, 0)
        grads(npair - 1, 1)
        accumulate(npair - 2, 0)
        accumulate(npair - 1, 1)

        bk = bk_ref[...]
        lane = lax.broadcasted_iota(jnp.int32, (8, LANES), 1)
        for h in range(2):
            db = db_acc[h * TQ:(h + 1) * TQ, :]
            acc = jnp.zeros((8, LANES), f32)
            for b in range(32):
                part = jnp.where(bk == b, db, 0.0).reshape(TQ // 8, 8, w).sum(axis=0)
                tot = jnp.sum(jnp.sum(part, axis=1, keepdims=True), axis=0, keepdims=True)
                acc = jnp.where(lane == b, tot, acc)
            dbk_ref[h] = acc

    def gcol(off):
        return pl.BlockSpec((None, None, SP, LANES), lambda hp: (g, off + hp, 0, 0))

    row = pl.BlockSpec((S, LANES), lambda hp: (0, hp))
    return pl.pallas_call(
        body,
        grid=(4,),
        in_specs=[gcol(0), gcol(4), gcol(8), pl.BlockSpec((2, TQ, w), lambda hp: (hp, 0, 0)),
                  pl.BlockSpec((TQ, w), lambda hp: (0, 0)), row, row, row],
        out_specs=[pl.BlockSpec((3, None, SP, LANES), lambda hp: (0, hp, 0, 0)),
                   pl.BlockSpec((2, 8, LANES), lambda hp: (hp, 0, 0))],
        out_shape=[
            jax.ShapeDtypeStruct((3, 4, SP, LANES), f32),
            jax.ShapeDtypeStruct((8, 8, LANES), f32),
        ],
        scratch_shapes=([pltpu.VMEM((2 * TQ, w), f32)] + [pltpu.VMEM((2, 2 * TQ, w), f32)] * 4
                        + [pltpu.VMEM((2, 2 * TQ, w), bf16)] * 4),
        compiler_params=_params(("arbitrary",), vmem_mib=56),
        name=name,
    )(gl, gl, gl, bias, bucket, do, lse, dd)


def _sigmoid(z):
    return 1.0 / (1.0 + jnp.exp(-z))


def _tail(x, tgt, o_a, l_a, o_b, l_b, proj, bm, w_a, w_b, w_o, sink_b):
    ts = 128

    def body(x_ref, t_ref, oa_ref, la_ref, ob0_ref, ob1_ref, ob2_ref, lb0_ref, lb1_ref, lb2_ref,
             ga_ref, gb_ref, m0_ref, m1_ref, bm_ref, wa_ref, wb_ref, wo_ref, sk_ref,
             dy_ref, dyb_ref, dt_ref, doa_ref, dda_ref, dob0_ref, dob1_ref, dob2_ref, ddb0_ref, ddb1_ref, ddb2_ref,
             ya_ref, yb_ref, mg_ref, dbra_ref, dbrb_ref, loss_ref, dbm_ref, dsk_ref):
        i = pl.program_id(0)

        @pl.when(i == 0)
        def _():
            loss_ref[...] = jnp.zeros_like(loss_ref)
            dbm_ref[...] = jnp.zeros_like(dbm_ref)
            dsk_ref[...] = jnp.zeros_like(dsk_ref)

        ga = ga_ref[...]
        sa = _sigmoid(ga)
        silu_a = ga * sa
        oa = oa_ref[...]
        ya = oa * silu_a
        gb = gb_ref[...]
        sb = _sigmoid(gb)
        silu_b = gb * sb
        ob = [ob0_ref[...], ob1_ref[...], ob2_ref[...]]
        lb = [lb0_ref[...], lb1_ref[...], lb2_ref[...]]
        mx = jnp.maximum(jnp.maximum(lb[0], lb[1]), lb[2])
        ex = [jnp.exp(v - mx) for v in lb]
        den = ex[0] + ex[1] + ex[2]
        alpha = [e / den for e in ex]
        ybc = alpha[0] * ob[0] + alpha[1] * ob[1] + alpha[2] * ob[2]
        yb = ybc * silu_b
        yab = ya.astype(bf16)
        ybb = yb.astype(bf16)
        br_a = jnp.dot(yab, wa_ref[...], preferred_element_type=f32)
        br_b = jnp.dot(ybb, wb_ref[...], preferred_element_type=f32)
        g0 = _sigmoid(m0_ref[...] + bm_ref[0:1, :])
        g1 = _sigmoid(m1_ref[...] + bm_ref[1:2, :])
        merged = g0 * br_a + g1 * br_b
        mgb = merged.astype(bf16)
        y = x_ref[...] + jnp.dot(mgb, wo_ref[...], preferred_element_type=f32)
        err = y - t_ref[...]
        part = jnp.sum(jnp.sum(err * err, axis=1, keepdims=True), axis=0, keepdims=True)
        loss_ref[...] += part * (0.5 / D)
        dy = err * (1.0 / D)
        dyb = dy.astype(bf16)
        dmerged = lax.dot_general(dyb, wo_ref[...], NT_DIMS, preferred_element_type=f32)
        dbr_a = (dmerged * g0).astype(bf16)
        dbr_b = (dmerged * g1).astype(bf16)
        dm0 = dmerged * br_a * (g0 * (1.0 - g0))
        dm1 = dmerged * br_b * (g1 * (1.0 - g1))
        dbm_ref[0:1, :] += jnp.sum(dm0, axis=0, keepdims=True)
        dbm_ref[1:2, :] += jnp.sum(dm1, axis=0, keepdims=True)
        dya = lax.dot_general(dbr_a, wa_ref[...], NT_DIMS, preferred_element_type=f32)
        dyb2 = lax.dot_general(dbr_b, wb_ref[...], NT_DIMS, preferred_element_type=f32)
        do_a = dya * silu_a
        dga = dya * oa * (sa * (1.0 + ga * (1.0 - sa)))
        ones = _head_ones()
        delta_a = _seg_sum(do_a * oa, ones)
        dsk_ref[...] -= jnp.sum(delta_a * jnp.exp(sk_ref[...] - la_ref[...]), axis=0, keepdims=True)
        dybc = dyb2 * silu_b
        dgb = dyb2 * ybc * (sb * (1.0 + gb * (1.0 - sb)))
        dbar = _seg_sum(dybc * ybc, ones)
        dy_ref[...] = dy
        dyb_ref[...] = dyb
        dt_ref[:, 0:512] = dga.astype(bf16)
        dt_ref[:, 512:1024] = dgb.astype(bf16)
        dt_ref[:, 1024:2048] = dm0.astype(bf16)
        dt_ref[:, 2048:3072] = dm1.astype(bf16)
        doa_ref[...] = do_a.astype(bf16)
        dda_ref[...] = delta_a
        for k, (dob_ref, ddb_ref) in enumerate(((dob0_ref, ddb0_ref), (dob1_ref, ddb1_ref), (dob2_ref, ddb2_ref))):
            dob_ref[...] = alpha[k] * dybc
            ddb_ref[...] = alpha[k] * dbar
        ya_ref[...] = ya.T.astype(bf16)
        yb_ref[...] = yb.T.astype(bf16)
        mg_ref[...] = merged.T.astype(bf16)
        dbra_ref[...] = dbr_a
        dbrb_ref[...] = dbr_b

    def rows(n, blk=0):
        return pl.BlockSpec((ts, n), lambda i: (i, blk))

    def whole(r, c):
        return pl.BlockSpec((r, c), lambda i: (0, 0))

    def cols(n):
        return pl.BlockSpec((n, ts), lambda i: (0, i))

    def gate_cols(n, col):
        return pl.BlockSpec((pl.Element(ts), pl.Element(n)), lambda i: (i * ts, NA + col))

    outs = [
        ((S, D), f32, rows(D)), ((S, D), bf16, rows(D)), ((S, NW), bf16, gate_cols(NT, 0)),
        ((S, 512), bf16, rows(512)), ((S, 512), f32, rows(512)),
        ((S, 512), f32, rows(512)), ((S, 512), f32, rows(512)), ((S, 512), f32, rows(512)),
        ((S, 512), f32, rows(512)), ((S, 512), f32, rows(512)), ((S, 512), f32, rows(512)),
        ((512, S), bf16, cols(512)), ((512, S), bf16, cols(512)), ((D, S), bf16, cols(D)),
        ((S, D), bf16, rows(D)), ((S, D), bf16, rows(D)),
        ((1, 1), f32, whole(1, 1)), ((2, D), f32, whole(2, D)), ((1, 512), f32, whole(1, 512)),
    ]
    return pl.pallas_call(
        body,
        grid=(S // ts,),
        in_specs=[
            rows(D), rows(D), rows(512), rows(512), rows(512), rows(512), rows(512), rows(512), rows(512), rows(512),
            gate_cols(512, 0), gate_cols(512, 512), gate_cols(D, 1024), gate_cols(D, 2048), whole(2, D),
            whole(512, D), whole(512, D), whole(D, D), whole(1, 512),
        ],
        out_specs=[o[2] for o in outs],
        out_shape=[jax.ShapeDtypeStruct(o[0], o[1]) for o in outs],
        compiler_params=_params(("arbitrary",)),
        name="tail",
    )(x, tgt, o_a, l_a, *o_b, *l_b, proj, proj, proj, proj, bm, w_a, w_b, w_o, sink_b)


def _norm_bwd(xv, dyv, gain, kind, ones):
    r = lax.rsqrt(_half_sums(xv * xv, ones) * (1.0 / HD) + EPS)
    yv = xv * r
    up = jnp.where(kind == 0, dyv * SCALE, dyv)
    u = up * gain
    dxv = r * (u - yv * (_half_sums(u * yv, ones) * (1.0 / HD)))
    dxv = jnp.where(kind == 2, dyv, dxv)
    dg = jnp.where(kind == 2, 0.0, jnp.sum(up * yv, axis=0, keepdims=True))
    return dxv, dg


def _post_b(g, dqkv, proj_a, gains, dproj):
    d = GROUPS[g][1]
    seq = S // d

    def body(d_ref, p_ref, g_ref, alias_ref, o_ref, dg_ref, nat):
        del alias_ref
        j = pl.program_id(0)
        kind = j // 4
        gain = g_ref[...]
        ones = _head_ones()

        @pl.when(j % 4 == 0)
        def _():
            dg_ref[...] = jnp.zeros_like(dg_ref)

        for c in range(d):
            for i in range(seq // PCHUNK):
                src = c * seq + i * PCHUNK
                if d == 1:
                    idx = slice(src, src + PCHUNK)
                else:
                    idx = pl.ds(c + i * PCHUNK * d, PCHUNK, stride=d)
                dxv, dg = _norm_bwd(p_ref[idx, :], d_ref[PAD + src:PAD + src + PCHUNK, :], gain, kind, ones)
                nat[idx, :] = dxv
                dg_ref[...] += dg

        for i in range(S // CHUNK):
            o_ref[i * CHUNK:(i + 1) * CHUNK, :] = nat[i * CHUNK:(i + 1) * CHUNK, :].astype(bf16)

    return pl.pallas_call(
        body,
        grid=(12,),
        in_specs=[
            pl.BlockSpec((None, None, SP, LANES), lambda j: (j // 4, j % 4, 0, 0)),
            pl.BlockSpec((S, LANES), lambda j: (0, _col_block(g, j))),
            pl.BlockSpec((None, None, 1, LANES), lambda j: (g, j // 4, 0, 0)),
            pl.BlockSpec(memory_space=pl.ANY),
        ],
        out_specs=[
            pl.BlockSpec((S, LANES), lambda j: (0, _col_block(g, j))),
            pl.BlockSpec((None, 1, LANES), lambda j: (j // 4, 0, 0)),
        ],
        out_shape=[jax.ShapeDtypeStruct((S, NW), bf16), jax.ShapeDtypeStruct((3, 1, LANES), f32)],
        scratch_shapes=[pltpu.VMEM((S, LANES), f32)],
        input_output_aliases={3: 0},
        compiler_params=_params(("arbitrary",)),
        name="post_b%d" % g,
    )(dqkv, proj_a, gains, dproj)


def _post_a(dqkv, proj_a, gains, dproj):
    def body(q_ref, e_ref, p_ref, g_ref, alias_ref, o_ref, dg_ref):
        del alias_ref
        j = pl.program_id(0)
        kind = jnp.maximum(j - 3, 0)
        gain = g_ref[...]
        lo = _lo()
        ones = _head_ones()

        @pl.when((j == 0) | (j >= 4))
        def _():
            dg_ref[...] = jnp.zeros_like(dg_ref)

        for i in range(S // PCHUNK):
            r0 = i * PCHUNK
            rows = slice(PAD + r0, PAD + r0 + PCHUNK)
            t0 = e_ref[0, rows, :] + e_ref[1, rows, :]
            t1 = e_ref[2, rows, :] + e_ref[3, rows, :]
            folded = jnp.where(lo, t0 + pltpu.roll(t0, HD, 1), t1 + pltpu.roll(t1, HD, 1))
            dyv = jnp.where(kind == 0, q_ref[rows, :], folded)
            dxv, dg = _norm_bwd(p_ref[r0:r0 + PCHUNK, :], dyv, gain, kind, ones)
            o_ref[r0:r0 + PCHUNK, :] = dxv.astype(bf16)
            dg_ref[...] += dg

    return pl.pallas_call(
        body,
        grid=(6,),
        in_specs=[
            pl.BlockSpec((None, None, SP, LANES), lambda j: (0, jnp.minimum(j, 3), 0, 0)),
            pl.BlockSpec((None, 4, SP, LANES), lambda j: (jnp.clip(j - 3, 1, 2), 0, 0, 0)),
            pl.BlockSpec((S, LANES), lambda j: (0, j)),
            pl.BlockSpec((None, None, 1, LANES), lambda j: (0, jnp.maximum(j - 3, 0), 0, 0)),
            pl.BlockSpec(memory_space=pl.ANY),
        ],
        out_specs=[
            pl.BlockSpec((S, LANES), lambda j: (0, j)),
            pl.BlockSpec((None, 1, LANES), lambda j: (jnp.maximum(j - 3, 0), 0, 0)),
        ],
        out_shape=[jax.ShapeDtypeStruct((S, NW), bf16), jax.ShapeDtypeStruct((3, 1, LANES), f32)],
        input_output_aliases={4: 0},
        compiler_params=_params(("arbitrary",)),
        name="post_a",
    )(dqkv, dqkv, proj_a, gains, dproj)


def _dh_norm_bwd(dproj, w, x, rstd, gain, dy):
    ts = 1024
    tk = NW // 6
    nk = NW // tk

    def body(d_ref, w_ref, x_ref, r_ref, g_ref, dy_ref, gx_ref, dgn_ref, acc):
        i = pl.program_id(0)
        k = pl.program_id(1)

        @pl.when((i == 0) & (k == 0))
        def _():
            dgn_ref[...] = jnp.zeros_like(dgn_ref)

        @pl.when(k == 0)
        def _():
            acc[...] = jnp.zeros_like(acc)

        acc[...] += lax.dot_general(d_ref[...], w_ref[...], NT_DIMS, preferred_element_type=f32)

        @pl.when(k == nk - 1)
        def _():
            dh = acc[...]
            xh = x_ref[...] * r_ref[...]
            u = dh * g_ref[...]
            dx = r_ref[...] * (u - xh * jnp.mean(u * xh, axis=-1, keepdims=True))
            gx_ref[...] = dy_ref[...] + dx
            dgn_ref[...] += jnp.sum(dh * xh, axis=0, keepdims=True)

    return pl.pallas_call(
        body,
        grid=(S // ts, nk),
        in_specs=[
            pl.BlockSpec((ts, tk), lambda i, k: (i, k)),
            pl.BlockSpec((D, tk), lambda i, k: (0, k)),
            pl.BlockSpec((ts, D), lambda i, k: (i, 0)),
            pl.BlockSpec((ts, 1), lambda i, k: (i, 0)),
            pl.BlockSpec((1, D), lambda i, k: (0, 0)),
            pl.BlockSpec((ts, D), lambda i, k: (i, 0)),
        ],
        out_specs=[pl.BlockSpec((ts, D), lambda i, k: (i, 0)), pl.BlockSpec((1, D), lambda i, k: (0, 0))],
        out_shape=[jax.ShapeDtypeStruct((S, D), f32), jax.ShapeDtypeStruct((1, D), f32)],
        scratch_shapes=[pltpu.VMEM((ts, D), f32)],
        compiler_params=_params(("arbitrary", "arbitrary"), vmem_mib=56),
        name="dh_norm_bwd",
    )(dproj, w, x, rstd, gain, dy)


def _dw_in(hbt, dproj):
    tk = 1024
    win = WSH + 96

    def body(a_ref, b_ref, o_ref, acc):
        j = pl.program_id(0)
        k = pl.program_id(1)

        @pl.when(k == 0)
        def _():
            acc[...] = jnp.zeros_like(acc)

        for jj in range(NDEV):
            off = (WSH * jj) % LANES

            @pl.when(j == jj)
            def _():
                acc[...] += jnp.dot(a_ref[...], b_ref[:, off:off + WSH], preferred_element_type=f32)

        @pl.when(k == S // tk - 1)
        def _():
            o_ref[...] = acc[...].astype(bf16)

    return pl.pallas_call(
        body,
        grid=(NDEV, S // tk),
        in_specs=[
            pl.BlockSpec((D, tk), lambda j, k: (0, k)),
            pl.BlockSpec((pl.Element(tk), pl.Element(win)), lambda j, k: (k * tk, (WSH * j) // LANES * LANES)),
        ],
        out_specs=pl.BlockSpec((None, D, WSH), lambda j, k: (j, 0, 0)),
        out_shape=jax.ShapeDtypeStruct((NDEV, D, WSH), bf16),
        scratch_shapes=[pltpu.VMEM((D, WSH), f32)],
        compiler_params=_params(("arbitrary", "arbitrary")),
        name="dw_in",
    )(hbt, dproj)


def _matmul_tokens(at, b, name):
    m, n = at.shape[0], b.shape[1]
    tn = 512
    tk = 1024

    def body(a_ref, b_ref, o_ref):
        @pl.when(pl.program_id(1) == 0)
        def _():
            o_ref[...] = jnp.zeros_like(o_ref)

        o_ref[...] += jnp.dot(a_ref[...], b_ref[...], preferred_element_type=f32)

    return pl.pallas_call(
        body,
        grid=(n // tn, S // tk),
        in_specs=[pl.BlockSpec((m, tk), lambda j, k: (0, k)), pl.BlockSpec((tk, tn), lambda j, k: (k, j))],
        out_specs=pl.BlockSpec((m, tn), lambda j, k: (0, j)),
        out_shape=jax.ShapeDtypeStruct((m, n), f32),
        compiler_params=_params(("arbitrary", "arbitrary")),
        name=name,
    )(at, b)


def _exchange(scatter, gather, name):
    arrs = list(scatter) + list(gather)
    n = len(arrs)
    ns = len(scatter)

    def body(*refs):
        ins, outs = refs[:n], refs[n:2 * n]
        send_sems, recv_sems, local_sems = refs[2 * n:]
        x, y, c = lax.axis_index("x"), lax.axis_index("y"), lax.axis_index("c")
        me = 4 * x + 2 * y + c
        local, remote = [], []
        for a in range(n):
            lc = pltpu.make_async_copy(ins[a].at[me] if a < ns else ins[a], outs[a].at[me], local_sems.at[a])
            lc.start()
            local.append(lc)
            for r in range(1, NDEV):
                px = 1 - x if r & 4 else x
                py = 1 - y if r & 2 else y
                pc = 1 - c if r & 1 else c
                cp = pltpu.make_async_remote_copy(
                    src_ref=ins[a].at[4 * px + 2 * py + pc] if a < ns else ins[a],
                    dst_ref=outs[a].at[me],
                    send_sem=send_sems.at[a, r - 1],
                    recv_sem=recv_sems.at[a, r - 1],
                    device_id=(px, py, pc),
                    device_id_type=pl.DeviceIdType.MESH,
                )
                cp.start()
                remote.append(cp)
        for cp in remote:
            cp.wait_recv()
        for cp in remote:
            cp.wait_send()
        for lc in local:
            lc.wait()

    out_shape = [jax.ShapeDtypeStruct(a.shape if i < ns else (NDEV,) + a.shape, a.dtype) for i, a in enumerate(arrs)]
    return pl.pallas_call(
        body,
        in_specs=[pl.BlockSpec(memory_space=pl.ANY)] * n,
        out_specs=[pl.BlockSpec(memory_space=pl.ANY)] * n,
        out_shape=out_shape,
        scratch_shapes=[
            pltpu.SemaphoreType.DMA((n, NDEV - 1)),
            pltpu.SemaphoreType.DMA((n, NDEV - 1)),
            pltpu.SemaphoreType.DMA((n,)),
        ],
        compiler_params=pltpu.CompilerParams(has_side_effects=True),
        name=name,
    )(*arrs)


def _gather_two_level(arrs, name):
    n = len(arrs)

    def body(*refs):
        ins, outs = refs[:n], refs[n:2 * n]
        send_sems, recv_sems, local_sems = refs[2 * n:]
        x, y, c = lax.axis_index("x"), lax.axis_index("y"), lax.axis_index("c")
        me, sibling = (x, y, c), (x, y, 1 - c)
        xn, yn, dg = (1 - x, y, c), (x, 1 - y, c), (1 - x, 1 - y, c)
        relay_origin = (jnp.bitwise_xor(x, c), jnp.bitwise_xor(y, 1 - c), c)
        relay_target = (jnp.bitwise_xor(x, 1 - c), jnp.bitwise_xor(y, c), c)

        def copy(a, k, block, to, src=None):
            slot = outs[a].at[4 * block[0] + 2 * block[1] + block[2]]
            return pltpu.make_async_remote_copy(
                src_ref=slot if src is None else src, dst_ref=slot, send_sem=send_sems.at[a, k],
                recv_sem=recv_sems.at[a, k], device_id=to, device_id_type=pl.DeviceIdType.MESH)

        def other_core(block):
            return (block[0], block[1], 1 - c)

        mine, sent = [], []
        for a in range(n):
            lc = pltpu.make_async_copy(ins[a], outs[a].at[4 * x + 2 * y + c], local_sems.at[a])
            lc.start()
            mine.append(lc)
            sent += [copy(a, 0, me, sibling, src=ins[a]), copy(a, 1, me, xn, src=ins[a]),
                     copy(a, 2, me, yn, src=ins[a])]
        for cp in sent:
            cp.start()
        later = []
        for a in range(n):
            copy(a, 1, xn, me).wait_recv()
            copy(a, 2, yn, me).wait_recv()
            later += [copy(a, 3, relay_origin, relay_target), copy(a, 4, xn, sibling), copy(a, 5, yn, sibling)]
            for cp in later[-3:]:
                cp.start()
        for a in range(n):
            copy(a, 3, dg, me).wait_recv()
            later.append(copy(a, 6, dg, sibling))
            later[-1].start()
        for a in range(n):
            copy(a, 0, sibling, me).wait_recv()
            for k, block in ((4, xn), (5, yn), (6, dg)):
                copy(a, k, other_core(block), me).wait_recv()
        for cp in sent + later:
            cp.wait_send()
        for lc in mine:
            lc.wait()

    return pl.pallas_call(
        body,
        in_specs=[pl.BlockSpec(memory_space=pl.ANY)] * n,
        out_specs=[pl.BlockSpec(memory_space=pl.ANY)] * n,
        out_shape=[jax.ShapeDtypeStruct((NDEV,) + a.shape, a.dtype) for a in arrs],
        scratch_shapes=[
            pltpu.SemaphoreType.DMA((n, NDEV - 1)),
            pltpu.SemaphoreType.DMA((n, NDEV - 1)),
            pltpu.SemaphoreType.DMA((n,)),
        ],
        compiler_params=pltpu.CompilerParams(has_side_effects=True),
        name=name,
    )(*arrs)


_HBM = pl.BlockSpec(memory_space=pltpu.HBM)
_SEM = pl.BlockSpec(memory_space=pltpu.SEMAPHORE)
_EFFECT = pltpu.SideEffectType.DATAFLOW_SIDE_EFFECTING


def _sibling_exchange(g, name):
    def body(in_ref, out_ref, send_sems, recv_sems):
        x, y, c = lax.axis_index("x"), lax.axis_index("y"), lax.axis_index("c")
        copies = []
        for q in range(4):
            cp = pltpu.make_async_remote_copy(
                src_ref=in_ref.at[2 * q + (1 - c)], dst_ref=out_ref.at[q], send_sem=send_sems.at[q],
                recv_sem=recv_sems.at[q], device_id=(x, y, 1 - c), device_id_type=pl.DeviceIdType.MESH)
            cp.start()
            copies.append(cp)
        for cp in copies:
            cp.wait_recv()
        for cp in copies:
            cp.wait_send()

    return pl.pallas_call(
        body,
        in_specs=[pl.BlockSpec(memory_space=pl.ANY)],
        out_specs=pl.BlockSpec(memory_space=pl.ANY),
        out_shape=jax.ShapeDtypeStruct((4,) + g.shape[1:], g.dtype),
        scratch_shapes=[pltpu.SemaphoreType.DMA((4,)), pltpu.SemaphoreType.DMA((4,))],
        compiler_params=pltpu.CompilerParams(has_side_effects=True),
        name=name,
    )(g)


def _pair_sum(g, r, core, name):
    _, rows, cols = g.shape
    tr = 256

    def body(c_ref, g_ref, r_ref, o_ref):
        del c_ref
        o_ref[...] = (g_ref[...].astype(f32) + r_ref[...].astype(f32)).astype(bf16)

    return pl.pallas_call(
        body,
        grid_spec=pltpu.PrefetchScalarGridSpec(
            num_scalar_prefetch=1,
            grid=(4, rows // tr),
            in_specs=[pl.BlockSpec((None, tr, cols), lambda q, i, c_ref: (2 * q + c_ref[0], i, 0)),
                      pl.BlockSpec((None, tr, cols), lambda q, i, c_ref: (q, i, 0))],
            out_specs=pl.BlockSpec((None, tr, cols), lambda q, i, c_ref: (q, i, 0)),
        ),
        out_shape=jax.ShapeDtypeStruct((4, rows, cols), bf16),
        compiler_params=_params(("arbitrary", "arbitrary")),
        name=name,
    )(core, g, r)


def _scatter_start(chip_arrs, all_arrs, name):
    arrs = list(chip_arrs) + list(all_arrs)
    n, nc = len(arrs), len(chip_arrs)
    lands = [lax.empty(((3 if i < nc else NDEV - 1),) + a.shape[1:], a.dtype) for i, a in enumerate(arrs)]

    def body(*refs):
        src, land = refs[:n], refs[n:2 * n]
        send_sems, recv_sems = refs[2 * n:3 * n], refs[3 * n:4 * n]
        token = refs[6 * n]
        x, y, c = lax.axis_index("x"), lax.axis_index("y"), lax.axis_index("c")
        for a in range(n):
            for r in range(1, 4 if a < nc else NDEV):
                if a < nc:
                    px, py, pc = (1 - x if r & 2 else x), (1 - y if r & 1 else y), c
                    block = 2 * px + py
                else:
                    px, py, pc = (1 - x if r & 4 else x), (1 - y if r & 2 else y), (1 - c if r & 1 else c)
                    block = 4 * px + 2 * py + pc
                pltpu.make_async_remote_copy(
                    src_ref=src[a].at[block], dst_ref=land[a].at[r - 1], send_sem=send_sems[a],
                    recv_sem=recv_sems[a], device_id=(px, py, pc), device_id_type=pl.DeviceIdType.MESH).start()
        token[...] = jnp.zeros_like(token)

    hbm = [pltpu.HBM(a.shape, a.dtype) for a in arrs + lands]
    ops = [pltpu.with_memory_space_constraint(a, pltpu.HBM) for a in arrs + lands]
    outs = pl.pallas_call(
        body,
        out_shape=tuple([pltpu.SemaphoreType.DMA(())] * (2 * n) + hbm + [jax.ShapeDtypeStruct((8, LANES), f32)]),
        in_specs=[_HBM] * (2 * n),
        out_specs=tuple([_SEM] * (2 * n) + [_HBM] * (2 * n) + [pl.BlockSpec(memory_space=pltpu.VMEM)]),
        input_output_aliases={i: 2 * n + i for i in range(2 * n)},
        compiler_params=pltpu.CompilerParams(has_side_effects=_EFFECT),
        name=name,
    )(*ops)
    return outs[:n], outs[n:2 * n], outs[2 * n:3 * n], outs[3 * n:4 * n], outs[4 * n]


def _scatter_wait(send_sems, recv_sems, srcs, lands, after, name):
    n = len(srcs)

    def body(*refs):
        land = refs[n:2 * n]
        ssem, rsem = refs[2 * n:3 * n], refs[3 * n:4 * n]
        x, y, c = lax.axis_index("x"), lax.axis_index("y"), lax.axis_index("c")
        for a in range(n):
            done = pltpu.make_async_remote_copy(
                src_ref=land[a], dst_ref=land[a], send_sem=ssem[a], recv_sem=rsem[a], device_id=(x, y, c),
                device_id_type=pl.DeviceIdType.MESH)
            done.wait_send()
            done.wait_recv()

    hbm = [pltpu.HBM(a.shape, a.dtype) for a in list(srcs) + list(lands)]
    outs = pl.pallas_call(
        body,
        out_shape=tuple(hbm),
        in_specs=[_HBM] * (2 * n) + [_SEM] * (2 * n) + [pl.BlockSpec(memory_space=pl.ANY)],
        out_specs=tuple([_HBM] * (2 * n)),
        input_output_aliases={i: i for i in range(2 * n)},
        compiler_params=pltpu.CompilerParams(has_side_effects=_EFFECT),
        name=name,
    )(*srcs, *lands, *send_sems, *recv_sems, after)
    return outs[:n], outs[n:]


def _adam_update(g, w_ref, m_ref, v_ref, g_ref, d_ref, nm_ref, nv_ref):
    mm = ADAM_B1 * m_ref[...] + (1.0 - ADAM_B1) * g
    vv = ADAM_B2 * v_ref[...] + (1.0 - ADAM_B2) * (g * g)
    m_hat = mm / (1.0 - ADAM_B1 ** ADAM_STEP)
    v_hat = vv / (1.0 - ADAM_B2 ** ADAM_STEP)
    g_ref[...] = g
    d_ref[...] = -ADAM_LR * (m_hat / (jnp.sqrt(v_hat) + ADAM_EPS) + ADAM_WD * w_ref[...])
    nm_ref[...] = mm
    nv_ref[...] = vv


def _adamw_own(w, own, own_idx, slots, m, v, name):
    r, c = w.shape[-2:]
    tr = 128 if r % 128 == 0 else r
    k = slots.shape[0]

    def body(i_ref, w_ref, o_ref, s_ref, m_ref, v_ref, g_ref, d_ref, nm_ref, nv_ref):
        del i_ref
        g = o_ref[...].astype(f32)
        for j in range(k):
            g = g + s_ref[j].astype(f32)
        _adam_update(g, w_ref, m_ref, v_ref, g_ref, d_ref, nm_ref, nv_ref)

    blk = pl.BlockSpec((None, tr, c), lambda i, ix: (0, i, 0))
    return pl.pallas_call(
        body,
        grid_spec=pltpu.PrefetchScalarGridSpec(
            num_scalar_prefetch=1,
            grid=(r // tr,),
            in_specs=[blk, pl.BlockSpec((None, tr, c), lambda i, ix: (ix[0], i, 0)),
                      pl.BlockSpec((k, tr, c), lambda i, ix: (0, i, 0)), blk, blk],
            out_specs=[blk] * 4,
        ),
        out_shape=[jax.ShapeDtypeStruct(w.shape, f32)] * 4,
        compiler_params=_params(("arbitrary",)),
        name=name,
    )(own_idx, w, own, slots, m, v)


def _adamw(w, slots, m, v, name):
    r, c = w.shape[-2:]
    tr = 128 if r % 128 == 0 else r

    def body(w_ref, s_ref, m_ref, v_ref, g_ref, d_ref, nm_ref, nv_ref):
        g = s_ref[0].astype(f32)
        for k in range(1, NDEV):
            g = g + s_ref[k].astype(f32)
        _adam_update(g, w_ref, m_ref, v_ref, g_ref, d_ref, nm_ref, nv_ref)

    if w.ndim == 3:
        blk = pl.BlockSpec((None, tr, c), lambda i: (0, i, 0))
    else:
        blk = pl.BlockSpec((tr, c), lambda i: (i, 0))
    return pl.pallas_call(
        body,
        grid=(r // tr,),
        in_specs=[blk, pl.BlockSpec((NDEV, tr, c), lambda i: (0, i, 0)), blk, blk],
        out_specs=[blk] * 4,
        out_shape=[jax.ShapeDtypeStruct(w.shape, f32)] * 4,
        compiler_params=_params(("arbitrary",)),
        name=name,
    )(w, slots, m, v)


def _local_step(x, tgt, norm_gain, w_shards, qn_a, kn_a, qn_b, kn_b, sink_a, rel_bias, w_a, w_b, b_merge, w_o,
                on_weight_grads=None):
    two = lambda t: jnp.concatenate([t, t], axis=-1).reshape(1, LANES)
    ones = jnp.ones((1, LANES), f32)
    gains = jnp.stack([
        jnp.stack([two(qn_a), two(kn_a), ones]),
        jnp.stack([two(qn_b), two(kn_b), ones]),
        jnp.stack([two(qn_b), two(kn_b), ones]),
        jnp.stack([two(qn_b), two(kn_b), ones]),
    ])
    buckets = [jnp.asarray(_bucket_np(blk, d)) for blk, d, _ in GROUPS]
    bias = [_bias_expand(rel_bias, buckets[k], GROUPS[k][2], "bias_expand_%d" % k) for k in range(4)]

    hb, hbt, rstd = _rms(x, norm_gain)
    w_in = _assemble_w(w_shards)
    proj = _inproj(hb, w_in)
    gl = _prep(proj, gains)
    o_a, l_a = _attn_fwd(gl, bias[0], sink_a.reshape(8), 0, 128, 1, "attn_fwd_a")
    fwd_b = [_attn_fwd(gl, bias[k], None, k, GROUPS[k][0], GROUPS[k][1], "attn_fwd_b%d" % k) for k in (1, 2, 3)]
    sink_b = jnp.repeat(sink_a.reshape(8), HD).reshape(1, 512)

    (dy, dyb, dproj, do_a, dd_a, do_b0, do_b1, do_b2, dd_b0, dd_b1, dd_b2, ya, yb, mg, dbr_a, dbr_b, loss, dbm,
     dsk) = _tail(x, tgt, o_a, l_a, [f[0] for f in fwd_b], [f[1] for f in fwd_b], proj, b_merge, w_a, w_b, w_o, sink_b)

    dqkv_a, dbk_a = _attn_bwd(gl, bias[0], buckets[0], do_a, l_a, dd_a, 0, 128, 1, "attn_bwd_a")
    dproj, dg_a = _post_a(dqkv_a, proj, gains, dproj)
    dbk_b, dg_b = [], []
    for k, do_k, dd_k in ((1, do_b0, dd_b0), (2, do_b1, dd_b1), (3, do_b2, dd_b2)):
        dqkv, dbk = _attn_bwd(gl, bias[k], buckets[k], do_k, fwd_b[k - 1][1], dd_k, k, GROUPS[k][0], GROUPS[k][1],
                              "attn_bwd_b%d" % k)
        dproj, dg = _post_b(k, dqkv, proj, gains, dproj)
        dbk_b.append(dbk)
        dg_b.append(dg)
    dg_b = jnp.stack(dg_b)

    dw_in = _dw_in(hbt, dproj)
    dw_o = _matmul_tokens(mg, dyb, "dw_out")
    dw_a = _matmul_tokens(ya, dbr_a, "dw_branch_a")
    dw_b = _matmul_tokens(yb, dbr_b, "dw_branch_b")
    token = jnp.zeros((), f32) if on_weight_grads is None else on_weight_grads(
        dict(w_in=dw_in, w_branch_a=dw_a, w_branch_b=dw_b, b_merge=dbm, w_out=dw_o))
    grad_x, d_norm_gain = _dh_norm_bwd(dproj, w_in, x, rstd, norm_gain + token, dy)

    fold = lambda t: t[..., :HD] + t[..., HD:]
    d_qn_a = fold(dg_a[0, 0])
    d_kn_a = fold(dg_a[1, 0])
    d_qn_b = fold(dg_b[:, 0, 0].sum(axis=0))
    d_kn_b = fold(dg_b[:, 1, 0].sum(axis=0))
    d_sink = dsk.reshape(8, HD)[:, 0]
    red = jnp.stack([dbk_a] + dbk_b)
    d_rel = red[:, :, 0, :32].reshape(32, 32).T
    return dict(loss=loss, grad_x=grad_x, norm_gain=d_norm_gain, w_in=dw_in, q_norm_a=d_qn_a, k_norm_a=d_kn_a,
                q_norm_b=d_qn_b, k_norm_b=d_kn_b, sink_a=d_sink, rel_bias=d_rel, w_branch_a=dw_a, w_branch_b=dw_b,
                b_merge=dbm, w_out=dw_o)


SMALL = (("norm_gain", D), ("q_norm_a", HD), ("k_norm_a", HD), ("q_norm_b", HD), ("k_norm_b", HD), ("sink_a", 8),
         ("rel_bias", 1024))
SMALL_PAD = 2432


SMALL_USED = sum(sz for _, sz in SMALL)


def _pack_small(parts, loss=None):
    tail = jnp.zeros((SMALL_PAD - SMALL_USED,), f32)
    if loss is not None:
        tail = tail.at[0].set(loss.reshape(()))
    return jnp.concatenate([parts[n].reshape(-1) for n, _ in SMALL] + [tail]).reshape(1, SMALL_PAD)


def _unpack_small(flat, shapes):
    out, off = {}, 0
    for n, sz in SMALL:
        out[n] = flat[0, off:off + sz].reshape(shapes[n])
        off += sz
    return out


def kernel(x, norm_gain, w_in, q_norm_a, k_norm_a, q_norm_b, k_norm_b, sink_a, rel_bias, w_branch_a, w_branch_b, b_merge, w_out, loss_target, m_norm_gain, m_w_in, m_q_norm_a, m_k_norm_a, m_q_norm_b, m_k_norm_b, m_sink_a, m_rel_bias, m_w_branch_a, m_w_branch_b, m_b_merge, m_w_out, v_norm_gain, v_w_in, v_q_norm_a, v_k_norm_a, v_q_norm_b, v_k_norm_b, v_sink_a, v_rel_bias, v_w_branch_a, v_w_branch_b, v_b_merge, v_w_out):
    wsh = NW // NDEV
    csh = D // NDEV
    g_in, g_a, g_b, g_o, g_bm = _gather_two_level(
        [w_in[0].astype(bf16), w_branch_a[0].astype(bf16), w_branch_b[0].astype(bf16), w_out[0].astype(bf16),
         b_merge[0]], "gather_weights")
    w_a_full = g_a.transpose(1, 0, 2).reshape(512, D)
    w_b_full = g_b.transpose(1, 0, 2).reshape(512, D)
    w_o_full = g_o.reshape(D, D)
    bm_full = g_bm.transpose(1, 0, 2).reshape(2, D)

    pending = {}
    core = lax.axis_index("c").astype(jnp.int32).reshape(1)
    chip = (2 * lax.axis_index("x") + lax.axis_index("y")).astype(jnp.int32).reshape(1)
    me = (2 * chip + core).astype(jnp.int32)

    def start_exchange(gw):
        from_sibling = _sibling_exchange(gw["w_in"], "grad_sibling_exchange")
        chip_sums = _pair_sum(gw["w_in"], from_sibling, core, "grad_pair_sum")
        blocks = [gw["w_branch_a"].reshape(512, NDEV, csh).transpose(1, 0, 2).astype(bf16),
                  gw["w_branch_b"].reshape(512, NDEV, csh).transpose(1, 0, 2).astype(bf16),
                  gw["w_out"].reshape(NDEV, csh, D).astype(bf16),
                  gw["b_merge"].reshape(2, NDEV, csh).transpose(1, 0, 2)]
        pending["started"] = _scatter_start([chip_sums], blocks, "scatter_grads_start")
        return pending["started"][4][0, 0]

    loc = _local_step(x[0], loss_target[0], norm_gain, g_in, q_norm_a, k_norm_a, q_norm_b, k_norm_b, sink_a,
                      rel_bias, w_a_full, w_b_full, bm_full, w_o_full, on_weight_grads=start_exchange)

    small_shapes = dict(norm_gain=(1, D), q_norm_a=(1, HD), k_norm_a=(1, HD), q_norm_b=(1, HD), k_norm_b=(1, HD),
                        sink_a=(1, 8), rel_bias=(32, 32))
    (r_small,) = _exchange([], [_pack_small(loc, loc["loss"])], "gather_small_grads")
    send_sems, recv_sems, srcs, lands, _ = pending["started"]
    (s_in, s_a, s_b, s_o, s_bm), (r_in, r_a, r_b, r_o, r_bm) = _scatter_wait(
        send_sems, recv_sems, srcs, lands, r_small, "scatter_grads_wait")

    given = dict(norm_gain=norm_gain, q_norm_a=q_norm_a, k_norm_a=k_norm_a, q_norm_b=q_norm_b, k_norm_b=k_norm_b,
                 sink_a=sink_a, rel_bias=rel_bias)
    m_small = dict(norm_gain=m_norm_gain, q_norm_a=m_q_norm_a, k_norm_a=m_k_norm_a, q_norm_b=m_q_norm_b,
                   k_norm_b=m_k_norm_b, sink_a=m_sink_a, rel_bias=m_rel_bias)
    v_small = dict(norm_gain=v_norm_gain, q_norm_a=v_q_norm_a, k_norm_a=v_k_norm_a, q_norm_b=v_q_norm_b,
                   k_norm_b=v_k_norm_b, sink_a=v_sink_a, rel_bias=v_rel_bias)
    res = {
        "small": _adamw(_pack_small(given), r_small, _pack_small(m_small), _pack_small(v_small), "adamw_small"),
        "w_in": _adamw_own(w_in, s_in, chip, r_in, m_w_in, v_w_in, "adamw_w_in"),
        "w_branch_a": _adamw_own(w_branch_a, s_a, me, r_a, m_w_branch_a, v_w_branch_a, "adamw_w_branch_a"),
        "w_branch_b": _adamw_own(w_branch_b, s_b, me, r_b, m_w_branch_b, v_w_branch_b, "adamw_w_branch_b"),
        "b_merge": _adamw_own(b_merge, s_bm, me, r_bm, m_b_merge, v_b_merge, "adamw_b_merge"),
        "w_out": _adamw_own(w_out, s_o, me, r_o, m_w_out, v_w_out, "adamw_w_out"),
    }
    order = ["norm_gain", "w_in", "q_norm_a", "k_norm_a", "q_norm_b", "k_norm_b", "sink_a", "rel_bias", "w_branch_a",
             "w_branch_b", "b_merge", "w_out"]
    outs = []
    for k in range(4):
        small = _unpack_small(res["small"][k], small_shapes)
        for n in order:
            outs.append(small[n] if n in small else res[n][k])
    loss = res["small"][0][0, SMALL_USED]
    return (loss, loc["grad_x"][None], *outs)
```

```python
import math

import numpy as np
import jax
import jax.numpy as jnp
from jax import lax
from jax.experimental import pallas as pl
from jax.experimental.pallas import tpu as pltpu

f32 = jnp.float32
bf16 = jnp.bfloat16

S = 4096
D = 1024
NA = 5376
NT = 3072
NW = NA + NT
WSH = NW // 8
HD = 64
LANES = 128
EPS = 1e-6
NEG = -1e30
SCALE = HD ** -0.5
TQ = 128
PAD = 128
SP = S + 2 * PAD
NDEV = 8
GROUPS = ((128, 1, 0), (64, 1, 8), (64, 4, 16), (64, 16, 24))
CHUNK = 256
PCHUNK = 128
RC = 64
TN = 768

ADAM_LR, ADAM_B1, ADAM_B2, ADAM_EPS, ADAM_WD, ADAM_STEP = 0.001, 0.9, 0.999, 1e-08, 0.01, 10

MIB = 1024 * 1024
NT_DIMS = (((1,), (1,)), ((), ()))
TN_DIMS = (((0,), (0,)), ((), ()))


def _params(sem=None, vmem_mib=48):
    return pltpu.CompilerParams(dimension_semantics=sem, vmem_limit_bytes=vmem_mib * MIB)


def _lo():
    return lax.broadcasted_iota(jnp.int32, (1, LANES), 1) < HD


def _head_ones():
    r = lax.broadcasted_iota(jnp.int32, (LANES, LANES), 0) // HD
    c = lax.broadcasted_iota(jnp.int32, (LANES, LANES), 1) // HD
    return jnp.where(r == c, 1.0, 0.0).astype(bf16)


def _half_sums(x, ones):
    hi = x.astype(bf16)
    mid = (x - hi.astype(f32)).astype(bf16)
    return (jnp.dot(hi, ones, preferred_element_type=f32) + jnp.dot(mid, ones, preferred_element_type=f32))


def _seg_sum(x, ones):
    outs = [_half_sums(x[:, b * LANES:(b + 1) * LANES], ones) for b in range(x.shape[1] // LANES)]
    return outs[0] if len(outs) == 1 else jnp.concatenate(outs, axis=1)


def _bucket_np(blk, stride):
    w = TQ + 2 * blk
    rel = np.arange(w)[None, :] - blk - np.arange(TQ)[:, None]
    band = np.abs(rel) <= blk
    r = rel * stride
    n = np.abs(r)
    nf = np.maximum(n, 8).astype(np.float32)
    large = 8 + (np.log(nf / np.float32(8)) / np.float32(math.log(128.0)) * np.float32(8)).astype(np.int32)
    large = np.minimum(large, 15)
    b = (r > 0).astype(np.int32) * 16 + np.where(n < 8, n, large)
    return np.where(band, b, -1).astype(np.int32)


def _rms(x, gain):
    ts = 512

    def body(x_ref, g_ref, h_ref, ht_ref, r_ref):
        xv = x_ref[...]
        r = lax.rsqrt(jnp.mean(xv * xv, axis=-1, keepdims=True) + EPS)
        h = (xv * r) * g_ref[...]
        h_ref[...] = h.astype(bf16)
        ht_ref[...] = h.T.astype(bf16)
        r_ref[...] = r

    return pl.pallas_call(
        body,
        grid=(S // ts,),
        in_specs=[pl.BlockSpec((ts, D), lambda i: (i, 0)), pl.BlockSpec((1, D), lambda i: (0, 0))],
        out_specs=[pl.BlockSpec((ts, D), lambda i: (i, 0)), pl.BlockSpec((D, ts), lambda i: (0, i)),
                   pl.BlockSpec((ts, 1), lambda i: (i, 0))],
        out_shape=[jax.ShapeDtypeStruct((S, D), bf16), jax.ShapeDtypeStruct((D, S), bf16),
                   jax.ShapeDtypeStruct((S, 1), f32)],
        compiler_params=_params(("arbitrary",)),
        name="rms",
    )(x, gain)


def _inproj(hb, w_t):
    ts = 2048

    def body(h_ref, w_ref, p_ref):
        p_ref[...] = lax.dot_general(h_ref[...], w_ref[...], NT_DIMS, preferred_element_type=f32)

    return pl.pallas_call(
        body,
        grid=(S // ts, NW // TN),
        in_specs=[pl.BlockSpec((ts, D), lambda i, n: (i, 0)), pl.BlockSpec((TN, D), lambda i, n: (n, 0))],
        out_specs=pl.BlockSpec((ts, TN), lambda i, n: (i, n)),
        out_shape=jax.ShapeDtypeStruct((S, NW), f32),
        compiler_params=_params(("arbitrary", "arbitrary")),
        name="inproj",
    )(hb, w_t)


def _bias_expand(table, bucket, c0, name):
    tq, w = bucket.shape

    def body(tab_ref, bk_ref, o_ref):
        h = pl.program_id(0)
        bk = bk_ref[...]

        def step(b, acc):
            return jnp.where(bk == b, tab_ref[b, c0 + h], acc)

        o_ref[...] = lax.fori_loop(0, 32, step, jnp.full((tq, w), NEG, f32))

    return pl.pallas_call(
        body,
        grid=(8,),
        in_specs=[pl.BlockSpec(memory_space=pltpu.SMEM), pl.BlockSpec((tq, w), lambda h: (0, 0))],
        out_specs=pl.BlockSpec((None, tq, w), lambda h: (h, 0, 0)),
        out_shape=jax.ShapeDtypeStruct((8, tq, w), f32),
        compiler_params=_params(("arbitrary",)),
        name=name,
    )(table, bucket)


def _col_block(g, j):
    kind = j // 4
    hp = j % 4
    a = jnp.where(kind == 0, hp, 3 + kind)
    b = 6 + 12 * kind + 4 * (g - 1) + hp
    return jnp.where(g == 0, a, b)


def _prep(proj_a, gains):
    def body(p_ref, g_ref, o_ref):
        g = pl.program_id(0)
        j = pl.program_id(1)
        kind = j // 4
        lo = _lo()
        ones = _head_ones()
        half = jnp.where(lo, 0, 1)
        take = (kind == 0) | (half == (j % 4) // 2)
        gain = g_ref[...]
        o_ref[0:PAD, :] = jnp.zeros((PAD, LANES), bf16)
        o_ref[PAD + S:SP, :] = jnp.zeros((PAD, LANES), bf16)

        def norm_store(xv, dst, dup):
            if dup:
                xv = jnp.where(take, xv, pltpu.roll(xv, HD, 1))
            r = lax.rsqrt(_half_sums(xv * xv, ones) * (1.0 / HD) + EPS)
            r = jnp.where(kind == 2, 1.0, r)
            yv = (xv * r) * gain
            yv = jnp.where(kind == 0, yv * SCALE, yv)
            o_ref[PAD + dst:PAD + dst + CHUNK, :] = yv.astype(bf16)

        for gi, (_, d, _) in enumerate(GROUPS):
            @pl.when(g == gi)
            def _():
                seq = S // d
                for c in range(d):
                    for i in range(seq // CHUNK):
                        if d == 1:
                            xv = p_ref[i * CHUNK:(i + 1) * CHUNK, :]
                        else:
                            xv = p_ref[pl.ds(c + i * CHUNK * d, CHUNK, stride=d), :]
                        norm_store(xv, c * seq + i * CHUNK, gi == 0)

    return pl.pallas_call(
        body,
        grid=(4, 12),
        in_specs=[
            pl.BlockSpec((S, LANES), lambda g, j: (0, _col_block(g, j))),
            pl.BlockSpec((None, None, 1, LANES), lambda g, j: (g, j // 4, 0, 0)),
        ],
        out_specs=pl.BlockSpec((None, None, SP, LANES), lambda g, j: (g, j, 0, 0)),
        out_shape=jax.ShapeDtypeStruct((4, 12, SP, LANES), bf16),
        compiler_params=_params(("arbitrary", "arbitrary")),
        name="prep",
    )(proj_a, gains)


def _token_rows(t, r0, n, d):
    if d == 1:
        return pl.ds(pl.multiple_of(t * TQ, TQ) + r0, n)
    per = S // d // TQ
    return pl.ds(((t % per) * TQ + r0) * d + t // per, n, stride=d)


def _stack_heads(t, lo):
    z = jnp.zeros_like(t)
    return jnp.concatenate([jnp.where(lo, t, z), jnp.where(lo, z, t)], axis=0)


def _unstack_heads(t2, lo):
    return jnp.where(lo, t2[:TQ], t2[TQ:])


def _attn_fwd(gl, bias, sink, g, blk, d, name):
    w = TQ + 2 * blk
    seq = S // d
    use_sink = sink is not None

    def body(*refs):
        if use_sink:
            sink_ref, q_ref, k_ref, v_ref, b_ref, o_ref, l_ref, s0, s1, p0, p1, lse_scr = refs
        else:
            q_ref, k_ref, v_ref, b_ref, o_ref, l_ref, s0, s1, p0, p1, lse_scr = refs
        hp = pl.program_id(0)
        lo = _lo()
        mi = lax.broadcasted_iota(jnp.int32, (1, w), 1)
        s_bufs, p_bufs = (s0, s1), (p0, p1)

        def scores(p, slot):
            for u in range(2):
                f0 = pl.multiple_of((2 * p + u) * TQ, TQ)
                q2 = _stack_heads(q_ref[pl.ds(PAD + f0, TQ), :], lo)
                kw = k_ref[pl.ds(PAD - blk + f0, w), :]
                s_bufs[slot][u] = lax.dot_general(q2, kw, NT_DIMS, preferred_element_type=f32)

        def softmax(p, slot):
            for u in range(2):
                t = 2 * p + u
                m0 = jnp.bitwise_and(pl.multiple_of(t * TQ, TQ), seq - 1)
                inside = (mi >= blk - m0) & (mi < seq + blk - m0)
                for h in range(2):
                    for r in range(TQ // RC):
                        rows = slice(h * TQ + r * RC, h * TQ + (r + 1) * RC)
                        logit = jnp.where(inside, s_bufs[slot][u, rows, :] + b_ref[h, r * RC:(r + 1) * RC, :], NEG)
                        m = jnp.max(logit, axis=1, keepdims=True)
                        e = jnp.exp(logit - m)
                        lse = m + jnp.log(jnp.sum(e, axis=1, keepdims=True))
                        if use_sink:
                            sk = sink_ref[2 * hp + h]
                            mx = jnp.maximum(lse, sk)
                            lse = mx + jnp.log(jnp.exp(lse - mx) + jnp.exp(sk - mx))
                        p_bufs[slot][u, rows, :] = (e * jnp.exp(m - lse)).astype(bf16)
                        lse_scr[u, rows, :] = jnp.broadcast_to(lse, (RC, LANES))
                l_ref[_token_rows(t, 0, TQ, d), :] = jnp.where(lo, lse_scr[u, 0:TQ, :], lse_scr[u, TQ:2 * TQ, :])

        def values(p, slot):
            for u in range(2):
                t = 2 * p + u
                vw = v_ref[pl.ds(PAD - blk + pl.multiple_of(t * TQ, TQ), w), :]
                o2 = jnp.dot(p_bufs[slot][u], vw, preferred_element_type=f32)
                o_ref[_token_rows(t, 0, TQ, d), :] = _unstack_heads(o2, lo)

        npair = S // TQ // 2
        scores(0, 0)
        scores(1, 1)
        softmax(0, 0)

        def steady(k, carry):
            p = 2 * k + 2
            scores(p, 0)
            softmax(p - 1, 1)
            values(p - 2, 0)
            scores(p + 1, 1)
            softmax(p, 0)
            values(p - 1, 1)
            return carry

        lax.fori_loop(0, (npair - 2) // 2, steady, 0)
        softmax(npair - 1, 1)
        values(npair - 2, 0)
        values(npair - 1, 1)

    in_specs = [
        pl.BlockSpec((None, None, SP, LANES), lambda hp: (g, hp, 0, 0)),
        pl.BlockSpec((None, None, SP, LANES), lambda hp: (g, 4 + hp, 0, 0)),
        pl.BlockSpec((None, None, SP, LANES), lambda hp: (g, 8 + hp, 0, 0)),
        pl.BlockSpec((2, TQ, w), lambda hp: (hp, 0, 0)),
    ]
    args = [gl, gl, gl, bias]
    if use_sink:
        in_specs = [pl.BlockSpec(memory_space=pltpu.SMEM)] + in_specs
        args = [sink] + args
    out = pl.BlockSpec((S, LANES), lambda hp: (0, hp))
    return pl.pallas_call(
        body,
        grid=(4,),
        in_specs=in_specs,
        out_specs=[out, out],
        out_shape=[jax.ShapeDtypeStruct((S, 4 * LANES), f32)] * 2,
        scratch_shapes=[pltpu.VMEM((2, 2 * TQ, w), f32), pltpu.VMEM((2, 2 * TQ, w), f32),
                        pltpu.VMEM((2, 2 * TQ, w), bf16), pltpu.VMEM((2, 2 * TQ, w), bf16),
                        pltpu.VMEM((2, 2 * TQ, LANES), f32)],
        compiler_params=_params(("arbitrary",)),
        name=name,
    )(*args)


def _attn_bwd(gl, bias, bucket, do, lse, dd, g, blk, d, name):
    w = TQ + 2 * blk
    seq = S // d

    def body(q_ref, k_ref, v_ref, b_ref, bk_ref, do_ref, l_ref, d_ref, dqkv_ref, dbk_ref,
             db_acc, s0, s1, dp0, dp1, pb0, pb1, ds0, ds1):
        lo = _lo()
        hi = jnp.logical_not(lo)
        mi = lax.broadcasted_iota(jnp.int32, (1, w), 1)
        dqkv_ref[1] = jnp.zeros((SP, LANES), f32)
        dqkv_ref[2] = jnp.zeros((SP, LANES), f32)
        db_acc[...] = jnp.zeros((2 * TQ, w), f32)
        s_bufs, dp_bufs, pb_bufs, ds_bufs = (s0, s1), (dp0, dp1), (pb0, pb1), (ds0, ds1)

        def stacked(t):
            f0 = pl.multiple_of(t * TQ, TQ)
            q2 = _stack_heads(q_ref[pl.ds(PAD + f0, TQ), :], lo)
            do2 = _stack_heads(do_ref[_token_rows(t, 0, TQ, d), :].astype(bf16), lo)
            return f0, q2, do2

        def scores(p, slot):
            for u in range(2):
                f0, q2, do2 = stacked(2 * p + u)
                win = pl.ds(PAD - blk + f0, w)
                s_bufs[slot][u] = lax.dot_general(q2, k_ref[win, :], NT_DIMS, preferred_element_type=f32)
                dp_bufs[slot][u] = lax.dot_general(do2, v_ref[win, :], NT_DIMS, preferred_element_type=f32)

        def grads(p, slot):
            for u in range(2):
                t = 2 * p + u
                m0 = jnp.bitwise_and(pl.multiple_of(t * TQ, TQ), seq - 1)
                inside = (mi >= blk - m0) & (mi < seq + blk - m0)
                for h in range(2):
                    msk = lo if h == 0 else hi
                    for r in range(TQ // RC):
                        rows = slice(h * TQ + r * RC, h * TQ + (r + 1) * RC)
                        src = _token_rows(t, r * RC, RC, d)
                        lh = jnp.max(jnp.where(msk, l_ref[src, :], -jnp.inf), axis=1, keepdims=True)
                        dh = jnp.max(jnp.where(msk, d_ref[src, :], -jnp.inf), axis=1, keepdims=True)
                        logit = jnp.where(inside, s_bufs[slot][u, rows, :] + b_ref[h, r * RC:(r + 1) * RC, :], NEG)
                        pr = jnp.exp(logit - lh)
                        ds = pr * (dp_bufs[slot][u, rows, :] - dh)
                        db_acc[rows, :] += ds
                        pb_bufs[slot][u, rows, :] = pr.astype(bf16)
                        ds_bufs[slot][u, rows, :] = ds.astype(bf16)

        def accumulate(p, slot):
            for u in range(2):
                f0, q2, do2 = stacked(2 * p + u)
                win = pl.ds(PAD - blk + f0, w)
                dsb = ds_bufs[slot][u]
                dq2 = jnp.dot(dsb, k_ref[win, :], preferred_element_type=f32)
                dqkv_ref[0, pl.ds(PAD + f0, TQ), :] = _unstack_heads(dq2, lo)
                dqkv_ref[1, win, :] += lax.dot_general(dsb, q2, TN_DIMS, preferred_element_type=f32)
                dqkv_ref[2, win, :] += lax.dot_general(pb_bufs[slot][u], do2, TN_DIMS, preferred_element_type=f32)

        npair = S // TQ // 2
        scores(0, 0)
        scores(1, 1)
        grads(0, 0)

        def steady(k, carry):
            p = 2 * k + 2
            scores(p, 0)
            grads(p - 1, 1)
            accumulate(p - 2, 0)
            scores(p + 1, 1)
            grads(p, 0)
            accumulate(p - 1, 1)
            return carry

        lax.fori_loop(0, (npair - 2) // 2, steady, 0)
        grads(npair - 1, 1)
        accumulate(npair - 2, 0)
        accumulate(npair - 1, 1)

        bk = bk_ref[...]
        lane = lax.broadcasted_iota(jnp.int32, (8, LANES), 1)
        for h in range(2):
            db = db_acc[h * TQ:(h + 1) * TQ, :]
            acc = jnp.zeros((8, LANES), f32)
            for b in range(32):
                part = jnp.where(bk == b, db, 0.0).reshape(TQ // 8, 8, w).sum(axis=0)
                tot = jnp.sum(jnp.sum(part, axis=1, keepdims=True), axis=0, keepdims=True)
                acc = jnp.where(lane == b, tot, acc)
            dbk_ref[h] = acc

    def gcol(off):
        return pl.BlockSpec((None, None, SP, LANES), lambda hp: (g, off + hp, 0, 0))

    row = pl.BlockSpec((S, LANES), lambda hp: (0, hp))
    return pl.pallas_call(
        body,
        grid=(4,),
        in_specs=[gcol(0), gcol(4), gcol(8), pl.BlockSpec((2, TQ, w), lambda hp: (hp, 0, 0)),
                  pl.BlockSpec((TQ, w), lambda hp: (0, 0)), row, row, row],
        out_specs=[pl.BlockSpec((3, None, SP, LANES), lambda hp: (0, hp, 0, 0)),
                   pl.BlockSpec((2, 8, LANES), lambda hp: (hp, 0, 0))],
        out_shape=[
            jax.ShapeDtypeStruct((3, 4, SP, LANES), f32),
            jax.ShapeDtypeStruct((8, 8, LANES), f32),
        ],
        scratch_shapes=([pltpu.VMEM((2 * TQ, w), f32)] + [pltpu.VMEM((2, 2 * TQ, w), f32)] * 4
                        + [pltpu.VMEM((2, 2 * TQ, w), bf16)] * 4),
        compiler_params=_params(("arbitrary",), vmem_mib=56),
        name=name,
    )(gl, gl, gl, bias, bucket, do, lse, dd)


def _sigmoid(z):
    return 1.0 / (1.0 + jnp.exp(-z))


def _tail(x, tgt, o_a, l_a, o_b, l_b, proj, bm, w_a, w_b, w_o, sink_b):
    ts = 128

    def body(x_ref, t_ref, oa_ref, la_ref, ob0_ref, ob1_ref, ob2_ref, lb0_ref, lb1_ref, lb2_ref,
             ga_ref, gb_ref, m0_ref, m1_ref, bm_ref, wa_ref, wb_ref, wo_ref, sk_ref,
             dy_ref, dyb_ref, dt_ref, doa_ref, dda_ref, dob0_ref, dob1_ref, dob2_ref, ddb0_ref, ddb1_ref, ddb2_ref,
             ya_ref, yb_ref, mg_ref, dbra_ref, dbrb_ref, loss_ref, dbm_ref, dsk_ref):
        i = pl.program_id(0)

        @pl.when(i == 0)
        def _():
            loss_ref[...] = jnp.zeros_like(loss_ref)
            dbm_ref[...] = jnp.zeros_like(dbm_ref)
            dsk_ref[...] = jnp.zeros_like(dsk_ref)

        ga = ga_ref[...]
        sa = _sigmoid(ga)
        silu_a = ga * sa
        oa = oa_ref[...]
        ya = oa * silu_a
        gb = gb_ref[...]
        sb = _sigmoid(gb)
        silu_b = gb * sb
        ob = [ob0_ref[...], ob1_ref[...], ob2_ref[...]]
        lb = [lb0_ref[...], lb1_ref[...], lb2_ref[...]]
        mx = jnp.maximum(jnp.maximum(lb[0], lb[1]), lb[2])
        ex = [jnp.exp(v - mx) for v in lb]
        den = ex[0] + ex[1] + ex[2]
        alpha = [e / den for e in ex]
        ybc = alpha[0] * ob[0] + alpha[1] * ob[1] + alpha[2] * ob[2]
        yb = ybc * silu_b
        yab = ya.astype(bf16)
        ybb = yb.astype(bf16)
        br_a = jnp.dot(yab, wa_ref[...], preferred_element_type=f32)
        br_b = jnp.dot(ybb, wb_ref[...], preferred_element_type=f32)
        g0 = _sigmoid(m0_ref[...] + bm_ref[0:1, :])
        g1 = _sigmoid(m1_ref[...] + bm_ref[1:2, :])
        merged = g0 * br_a + g1 * br_b
        mgb = merged.astype(bf16)
        y = x_ref[...] + jnp.dot(mgb, wo_ref[...], preferred_element_type=f32)
        err = y - t_ref[...]
        part = jnp.sum(jnp.sum(err * err, axis=1, keepdims=True), axis=0, keepdims=True)
        loss_ref[...] += part * (0.5 / D)
        dy = err * (1.0 / D)
        dyb = dy.astype(bf16)
        dmerged = lax.dot_general(dyb, wo_ref[...], NT_DIMS, preferred_element_type=f32)
        dbr_a = (dmerged * g0).astype(bf16)
        dbr_b = (dmerged * g1).astype(bf16)
        dm0 = dmerged * br_a * (g0 * (1.0 - g0))
        dm1 = dmerged * br_b * (g1 * (1.0 - g1))
        dbm_ref[0:1, :] += jnp.sum(dm0, axis=0, keepdims=True)
        dbm_ref[1:2, :] += jnp.sum(dm1, axis=0, keepdims=True)
        dya = lax.dot_general(dbr_a, wa_ref[...], NT_DIMS, preferred_element_type=f32)
        dyb2 = lax.dot_general(dbr_b, wb_ref[...], NT_DIMS, preferred_element_type=f32)
        do_a = dya * silu_a
        dga = dya * oa * (sa * (1.0 + ga * (1.0 - sa)))
        ones = _head_ones()
        delta_a = _seg_sum(do_a * oa, ones)
        dsk_ref[...] -= jnp.sum(delta_a * jnp.exp(sk_ref[...] - la_ref[...]), axis=0, keepdims=True)
        dybc = dyb2 * silu_b
        dgb = dyb2 * ybc * (sb * (1.0 + gb * (1.0 - sb)))
        dbar = _seg_sum(dybc * ybc, ones)
        dy_ref[...] = dy
        dyb_ref[...] = dyb
        dt_ref[:, 0:512] = dga.astype(bf16)
        dt_ref[:, 512:1024] = dgb.astype(bf16)
        dt_ref[:, 1024:2048] = dm0.astype(bf16)
        dt_ref[:, 2048:3072] = dm1.astype(bf16)
        doa_ref[...] = do_a.astype(bf16)
        dda_ref[...] = delta_a
        for k, (dob_ref, ddb_ref) in enumerate(((dob0_ref, ddb0_ref), (dob1_ref, ddb1_ref), (dob2_ref, ddb2_ref))):
            dob_ref[...] = alpha[k] * dybc
            ddb_ref[...] = alpha[k] * dbar
        ya_ref[...] = ya.T.astype(bf16)
        yb_ref[...] = yb.T.astype(bf16)
        mg_ref[...] = merged.T.astype(bf16)
        dbra_ref[...] = dbr_a
        dbrb_ref[...] = dbr_b

    def rows(n, blk=0):
        return pl.BlockSpec((ts, n), lambda i: (i, blk))

    def whole(r, c):
        return pl.BlockSpec((r, c), lambda i: (0, 0))

    def cols(n):
        return pl.BlockSpec((n, ts), lambda i: (0, i))

    def gate_cols(n, col):
        return pl.BlockSpec((pl.Element(ts), pl.Element(n)), lambda i: (i * ts, NA + col))

    outs = [
        ((S, D), f32, rows(D)), ((S, D), bf16, rows(D)), ((S, NW), bf16, gate_cols(NT, 0)),
        ((S, 512), bf16, rows(512)), ((S, 512), f32, rows(512)),
        ((S, 512), f32, rows(512)), ((S, 512), f32, rows(512)), ((S, 512), f32, rows(512)),
        ((S, 512), f32, rows(512)), ((S, 512), f32, rows(512)), ((S, 512), f32, rows(512)),
        ((512, S), bf16, cols(512)), ((512, S), bf16, cols(512)), ((D, S), bf16, cols(D)),
        ((S, D), bf16, rows(D)), ((S, D), bf16, rows(D)),
        ((1, 1), f32, whole(1, 1)), ((2, D), f32, whole(2, D)), ((1, 512), f32, whole(1, 512)),
    ]
    return pl.pallas_call(
        body,
        grid=(S // ts,),
        in_specs=[
            rows(D), rows(D), rows(512), rows(512), rows(512), rows(512), rows(512), rows(512), rows(512), rows(512),
            gate_cols(512, 0), gate_cols(512, 512), gate_cols(D, 1024), gate_cols(D, 2048), whole(2, D),
            whole(512, D), whole(512, D), whole(D, D), whole(1, 512),
        ],
        out_specs=[o[2] for o in outs],
        out_shape=[jax.ShapeDtypeStruct(o[0], o[1]) for o in outs],
        compiler_params=_params(("arbitrary",)),
        name="tail",
    )(x, tgt, o_a, l_a, *o_b, *l_b, proj, proj, proj, proj, bm, w_a, w_b, w_o, sink_b)


def _norm_bwd(xv, dyv, gain, kind, ones):
    r = lax.rsqrt(_half_sums(xv * xv, ones) * (1.0 / HD) + EPS)
    yv = xv * r
    up = jnp.where(kind == 0, dyv * SCALE, dyv)
    u = up * gain
    dxv = r * (u - yv * (_half_sums(u * yv, ones) * (1.0 / HD)))
    dxv = jnp.where(kind == 2, dyv, dxv)
    dg = jnp.where(kind == 2, 0.0, jnp.sum(up * yv, axis=0, keepdims=True))
    return dxv, dg


def _post_b(g, dqkv, proj_a, gains, dproj):
    d = GROUPS[g][1]
    seq = S // d

    def body(d_ref, p_ref, g_ref, alias_ref, o_ref, dg_ref, nat):
        del alias_ref
        j = pl.program_id(0)
        kind = j // 4
        gain = g_ref[...]
        ones = _head_ones()

        @pl.when(j % 4 == 0)
        def _():
            dg_ref[...] = jnp.zeros_like(dg_ref)

        for c in range(d):
            for i in range(seq // PCHUNK):
                src = c * seq + i * PCHUNK
                if d == 1:
                    idx = slice(src, src + PCHUNK)
                else:
                    idx = pl.ds(c + i * PCHUNK * d, PCHUNK, stride=d)
                dxv, dg = _norm_bwd(p_ref[idx, :], d_ref[PAD + src:PAD + src + PCHUNK, :], gain, kind, ones)
                nat[idx, :] = dxv
                dg_ref[...] += dg

        for i in range(S // CHUNK):
            o_ref[i * CHUNK:(i + 1) * CHUNK, :] = nat[i * CHUNK:(i + 1) * CHUNK, :].astype(bf16)

    return pl.pallas_call(
        body,
        grid=(12,),
        in_specs=[
            pl.BlockSpec((None, None, SP, LANES), lambda j: (j // 4, j % 4, 0, 0)),
            pl.BlockSpec((S, LANES), lambda j: (0, _col_block(g, j))),
            pl.BlockSpec((None, None, 1, LANES), lambda j: (g, j // 4, 0, 0)),
            pl.BlockSpec(memory_space=pl.ANY),
        ],
        out_specs=[
            pl.BlockSpec((S, LANES), lambda j: (0, _col_block(g, j))),
            pl.BlockSpec((None, 1, LANES), lambda j: (j // 4, 0, 0)),
        ],
        out_shape=[jax.ShapeDtypeStruct((S, NW), bf16), jax.ShapeDtypeStruct((3, 1, LANES), f32)],
        scratch_shapes=[pltpu.VMEM((S, LANES), f32)],
        input_output_aliases={3: 0},
        compiler_params=_params(("arbitrary",)),
        name="post_b%d" % g,
    )(dqkv, proj_a, gains, dproj)


def _post_a(dqkv, proj_a, gains, dproj):
    def body(q_ref, e_ref, p_ref, g_ref, alias_ref, o_ref, dg_ref):
        del alias_ref
        j = pl.program_id(0)
        kind = jnp.maximum(j - 3, 0)
        gain = g_ref[...]
        lo = _lo()
        ones = _head_ones()

        @pl.when((j == 0) | (j >= 4))
        def _():
            dg_ref[...] = jnp.zeros_like(dg_ref)

        for i in range(S // PCHUNK):
            r0 = i * PCHUNK
            rows = slice(PAD + r0, PAD + r0 + PCHUNK)
            t0 = e_ref[0, rows, :] + e_ref[1, rows, :]
            t1 = e_ref[2, rows, :] + e_ref[3, rows, :]
            folded = jnp.where(lo, t0 + pltpu.roll(t0, HD, 1), t1 + pltpu.roll(t1, HD, 1))
            dyv = jnp.where(kind == 0, q_ref[rows, :], folded)
            dxv, dg = _norm_bwd(p_ref[r0:r0 + PCHUNK, :], dyv, gain, kind, ones)
            o_ref[r0:r0 + PCHUNK, :] = dxv.astype(bf16)
            dg_ref[...] += dg

    return pl.pallas_call(
        body,
        grid=(6,),
        in_specs=[
            pl.BlockSpec((None, None, SP, LANES), lambda j: (0, jnp.minimum(j, 3), 0, 0)),
            pl.BlockSpec((None, 4, SP, LANES), lambda j: (jnp.clip(j - 3, 1, 2), 0, 0, 0)),
            pl.BlockSpec((S, LANES), lambda j: (0, j)),
            pl.BlockSpec((None, None, 1, LANES), lambda j: (0, jnp.maximum(j - 3, 0), 0, 0)),
            pl.BlockSpec(memory_space=pl.ANY),
        ],
        out_specs=[
            pl.BlockSpec((S, LANES), lambda j: (0, j)),
            pl.BlockSpec((None, 1, LANES), lambda j: (jnp.maximum(j - 3, 0), 0, 0)),
        ],
        out_shape=[jax.ShapeDtypeStruct((S, NW), bf16), jax.ShapeDtypeStruct((3, 1, LANES), f32)],
        input_output_aliases={4: 0},
        compiler_params=_params(("arbitrary",)),
        name="post_a",
    )(dqkv, dqkv, proj_a, gains, dproj)


def _dh_norm_bwd(dproj, w, x, rstd, gain, dy):
    ts = 1024
    tk = NW // 6
    nk = NW // tk

    def body(d_ref, w_ref, x_ref, r_ref, g_ref, dy_ref, gx_ref, dgn_ref, acc):
        i = pl.program_id(0)
        k = pl.program_id(1)

        @pl.when((i == 0) & (k == 0))
        def _():
            dgn_ref[...] = jnp.zeros_like(dgn_ref)

        @pl.when(k == 0)
        def _():
            acc[...] = jnp.zeros_like(acc)

        acc[...] += jnp.dot(d_ref[...], w_ref[...], preferred_element_type=f32)

        @pl.when(k == nk - 1)
        def _():
            dh = acc[...]
            xh = x_ref[...] * r_ref[...]
            u = dh * g_ref[...]
            dx = r_ref[...] * (u - xh * jnp.mean(u * xh, axis=-1, keepdims=True))
            gx_ref[...] = dy_ref[...] + dx
            dgn_ref[...] += jnp.sum(dh * xh, axis=0, keepdims=True)

    return pl.pallas_call(
        body,
        grid=(S // ts, nk),
        in_specs=[
            pl.BlockSpec((ts, tk), lambda i, k: (i, k)),
            pl.BlockSpec((tk, D), lambda i, k: (k, 0)),
            pl.BlockSpec((ts, D), lambda i, k: (i, 0)),
            pl.BlockSpec((ts, 1), lambda i, k: (i, 0)),
            pl.BlockSpec((1, D), lambda i, k: (0, 0)),
            pl.BlockSpec((ts, D), lambda i, k: (i, 0)),
        ],
        out_specs=[pl.BlockSpec((ts, D), lambda i, k: (i, 0)), pl.BlockSpec((1, D), lambda i, k: (0, 0))],
        out_shape=[jax.ShapeDtypeStruct((S, D), f32), jax.ShapeDtypeStruct((1, D), f32)],
        scratch_shapes=[pltpu.VMEM((ts, D), f32)],
        compiler_params=_params(("arbitrary", "arbitrary"), vmem_mib=56),
        name="dh_norm_bwd",
    )(dproj, w, x, rstd, gain, dy)


def _dw_in(hbt, dproj):
    tk = 1024
    win = WSH + 96

    def body(a_ref, b_ref, o_ref, acc):
        j = pl.program_id(0)
        k = pl.program_id(1)

        @pl.when(k == 0)
        def _():
            acc[...] = jnp.zeros_like(acc)

        acc[...] += jnp.dot(a_ref[...], b_ref[...], preferred_element_type=f32)

        @pl.when(k == S // tk - 1)
        def _():
            acc_t = acc[...].T
            for jj in range(NDEV):
                off = (WSH * jj) % LANES

                @pl.when(j == jj)
                def _():
                    o_ref[...] = acc_t[off:off + WSH, :].astype(bf16)

    return pl.pallas_call(
        body,
        grid=(NDEV, S // tk),
        in_specs=[
            pl.BlockSpec((D, tk), lambda j, k: (0, k)),
            pl.BlockSpec((pl.Element(tk), pl.Element(win)), lambda j, k: (k * tk, (WSH * j) // LANES * LANES)),
        ],
        out_specs=pl.BlockSpec((None, WSH, D), lambda j, k: (j, 0, 0)),
        out_shape=jax.ShapeDtypeStruct((NDEV, WSH, D), bf16),
        scratch_shapes=[pltpu.VMEM((D, win), f32)],
        compiler_params=_params(("arbitrary", "arbitrary")),
        name="dw_in",
    )(hbt, dproj)


def _matmul_tokens(at, b, name):
    m, n = at.shape[0], b.shape[1]
    tn = 512
    tk = 1024

    def body(a_ref, b_ref, o_ref):
        @pl.when(pl.program_id(1) == 0)
        def _():
            o_ref[...] = jnp.zeros_like(o_ref)

        o_ref[...] += jnp.dot(a_ref[...], b_ref[...], preferred_element_type=f32)

    return pl.pallas_call(
        body,
        grid=(n // tn, S // tk),
        in_specs=[pl.BlockSpec((m, tk), lambda j, k: (0, k)), pl.BlockSpec((tk, tn), lambda j, k: (k, j))],
        out_specs=pl.BlockSpec((m, tn), lambda j, k: (0, j)),
        out_shape=jax.ShapeDtypeStruct((m, n), f32),
        compiler_params=_params(("arbitrary", "arbitrary")),
        name=name,
    )(at, b)


def _exchange(scatter, gather, name):
    arrs = list(scatter) + list(gather)
    n = len(arrs)
    ns = len(scatter)

    def body(*refs):
        ins, outs = refs[:n], refs[n:2 * n]
        send_sems, recv_sems, local_sems = refs[2 * n:]
        x, y, c = lax.axis_index("x"), lax.axis_index("y"), lax.axis_index("c")
        me = 4 * x + 2 * y + c
        local, remote = [], []
        for a in range(n):
            lc = pltpu.make_async_copy(ins[a].at[me] if a < ns else ins[a], outs[a].at[me], local_sems.at[a])
            lc.start()
            local.append(lc)
            for r in range(1, NDEV):
                px = 1 - x if r & 4 else x
                py = 1 - y if r & 2 else y
                pc = 1 - c if r & 1 else c
                cp = pltpu.make_async_remote_copy(
                    src_ref=ins[a].at[4 * px + 2 * py + pc] if a < ns else ins[a],
                    dst_ref=outs[a].at[me],
                    send_sem=send_sems.at[a, r - 1],
                    recv_sem=recv_sems.at[a, r - 1],
                    device_id=(px, py, pc),
                    device_id_type=pl.DeviceIdType.MESH,
                )
                cp.start()
                remote.append(cp)
        for cp in remote:
            cp.wait_recv()
        for cp in remote:
            cp.wait_send()
        for lc in local:
            lc.wait()

    out_shape = [jax.ShapeDtypeStruct(a.shape if i < ns else (NDEV,) + a.shape, a.dtype) for i, a in enumerate(arrs)]
    return pl.pallas_call(
        body,
        in_specs=[pl.BlockSpec(memory_space=pl.ANY)] * n,
        out_specs=[pl.BlockSpec(memory_space=pl.ANY)] * n,
        out_shape=out_shape,
        scratch_shapes=[
            pltpu.SemaphoreType.DMA((n, NDEV - 1)),
            pltpu.SemaphoreType.DMA((n, NDEV - 1)),
            pltpu.SemaphoreType.DMA((n,)),
        ],
        compiler_params=pltpu.CompilerParams(has_side_effects=True),
        name=name,
    )(*arrs)


def _gather_two_level(arrs, name):
    n = len(arrs)

    def body(*refs):
        ins, outs = refs[:n], refs[n:2 * n]
        send_sems, recv_sems, local_sems = refs[2 * n:]
        x, y, c = lax.axis_index("x"), lax.axis_index("y"), lax.axis_index("c")
        me, sibling = (x, y, c), (x, y, 1 - c)
        xn, yn, dg = (1 - x, y, c), (x, 1 - y, c), (1 - x, 1 - y, c)
        relay_origin = (jnp.bitwise_xor(x, c), jnp.bitwise_xor(y, 1 - c), c)
        relay_target = (jnp.bitwise_xor(x, 1 - c), jnp.bitwise_xor(y, c), c)

        def copy(a, k, block, to, src=None):
            slot = outs[a].at[4 * block[0] + 2 * block[1] + block[2]]
            return pltpu.make_async_remote_copy(
                src_ref=slot if src is None else src, dst_ref=slot, send_sem=send_sems.at[a, k],
                recv_sem=recv_sems.at[a, k], device_id=to, device_id_type=pl.DeviceIdType.MESH)

        def other_core(block):
            return (block[0], block[1], 1 - c)

        mine, sent = [], []
        for a in range(n):
            lc = pltpu.make_async_copy(ins[a], outs[a].at[4 * x + 2 * y + c], local_sems.at[a])
            lc.start()
            mine.append(lc)
            sent += [copy(a, 0, me, sibling, src=ins[a]), copy(a, 1, me, xn, src=ins[a]),
                     copy(a, 2, me, yn, src=ins[a])]
        for cp in sent:
            cp.start()
        later = []
        for a in range(n):
            copy(a, 1, xn, me).wait_recv()
            copy(a, 2, yn, me).wait_recv()
            later += [copy(a, 3, relay_origin, relay_target), copy(a, 4, xn, sibling), copy(a, 5, yn, sibling)]
            for cp in later[-3:]:
                cp.start()
        for a in range(n):
            copy(a, 3, dg, me).wait_recv()
            later.append(copy(a, 6, dg, sibling))
            later[-1].start()
        for a in range(n):
            copy(a, 0, sibling, me).wait_recv()
            for k, block in ((4, xn), (5, yn), (6, dg)):
                copy(a, k, other_core(block), me).wait_recv()
        for cp in sent + later:
            cp.wait_send()
        for lc in mine:
            lc.wait()

    return pl.pallas_call(
        body,
        in_specs=[pl.BlockSpec(memory_space=pl.ANY)] * n,
        out_specs=[pl.BlockSpec(memory_space=pl.ANY)] * n,
        out_shape=[jax.ShapeDtypeStruct((NDEV,) + a.shape, a.dtype) for a in arrs],
        scratch_shapes=[
            pltpu.SemaphoreType.DMA((n, NDEV - 1)),
            pltpu.SemaphoreType.DMA((n, NDEV - 1)),
            pltpu.SemaphoreType.DMA((n,)),
        ],
        compiler_params=pltpu.CompilerParams(has_side_effects=True),
        name=name,
    )(*arrs)


_HBM = pl.BlockSpec(memory_space=pltpu.HBM)
_SEM = pl.BlockSpec(memory_space=pltpu.SEMAPHORE)
_EFFECT = pltpu.SideEffectType.DATAFLOW_SIDE_EFFECTING


def _sibling_exchange(g, name):
    def body(in_ref, out_ref, send_sems, recv_sems):
        x, y, c = lax.axis_index("x"), lax.axis_index("y"), lax.axis_index("c")
        copies = []
        for q in range(4):
            cp = pltpu.make_async_remote_copy(
                src_ref=in_ref.at[2 * q + (1 - c)], dst_ref=out_ref.at[q], send_sem=send_sems.at[q],
                recv_sem=recv_sems.at[q], device_id=(x, y, 1 - c), device_id_type=pl.DeviceIdType.MESH)
            cp.start()
            copies.append(cp)
        for cp in copies:
            cp.wait_recv()
        for cp in copies:
            cp.wait_send()

    return pl.pallas_call(
        body,
        in_specs=[pl.BlockSpec(memory_space=pl.ANY)],
        out_specs=pl.BlockSpec(memory_space=pl.ANY),
        out_shape=jax.ShapeDtypeStruct((4,) + g.shape[1:], g.dtype),
        scratch_shapes=[pltpu.SemaphoreType.DMA((4,)), pltpu.SemaphoreType.DMA((4,))],
        compiler_params=pltpu.CompilerParams(has_side_effects=True),
        name=name,
    )(g)


def _row_tile(rows, limit=256):
    fits = [t for t in range(16, limit + 1, 16) if rows % t == 0]
    return fits[-1] if fits else rows


def _pair_sum(g, r, core, name):
    _, rows, cols = g.shape
    tr = _row_tile(rows)

    def body(c_ref, g_ref, r_ref, o_ref):
        del c_ref
        o_ref[...] = (g_ref[...].astype(f32) + r_ref[...].astype(f32)).astype(bf16)

    return pl.pallas_call(
        body,
        grid_spec=pltpu.PrefetchScalarGridSpec(
            num_scalar_prefetch=1,
            grid=(4, rows // tr),
            in_specs=[pl.BlockSpec((None, tr, cols), lambda q, i, c_ref: (2 * q + c_ref[0], i, 0)),
                      pl.BlockSpec((None, tr, cols), lambda q, i, c_ref: (q, i, 0))],
            out_specs=pl.BlockSpec((None, tr, cols), lambda q, i, c_ref: (q, i, 0)),
        ),
        out_shape=jax.ShapeDtypeStruct((4, rows, cols), bf16),
        compiler_params=_params(("arbitrary", "arbitrary")),
        name=name,
    )(core, g, r)


def _scatter_start(chip_arrs, all_arrs, name):
    arrs = list(chip_arrs) + list(all_arrs)
    n, nc = len(arrs), len(chip_arrs)
    lands = [lax.empty(((3 if i < nc else NDEV - 1),) + a.shape[1:], a.dtype) for i, a in enumerate(arrs)]

    def body(*refs):
        src, land = refs[:n], refs[n:2 * n]
        send_sems, recv_sems = refs[2 * n:3 * n], refs[3 * n:4 * n]
        token = refs[6 * n]
        x, y, c = lax.axis_index("x"), lax.axis_index("y"), lax.axis_index("c")
        for a in range(n):
            for r in range(1, 4 if a < nc else NDEV):
                if a < nc:
                    px, py, pc = (1 - x if r & 2 else x), (1 - y if r & 1 else y), c
                    block = 2 * px + py
                else:
                    px, py, pc = (1 - x if r & 4 else x), (1 - y if r & 2 else y), (1 - c if r & 1 else c)
                    block = 4 * px + 2 * py + pc
                pltpu.make_async_remote_copy(
                    src_ref=src[a].at[block], dst_ref=land[a].at[r - 1], send_sem=send_sems[a],
                    recv_sem=recv_sems[a], device_id=(px, py, pc), device_id_type=pl.DeviceIdType.MESH).start()
        token[...] = jnp.zeros_like(token)

    hbm = [pltpu.HBM(a.shape, a.dtype) for a in arrs + lands]
    ops = [pltpu.with_memory_space_constraint(a, pltpu.HBM) for a in arrs + lands]
    outs = pl.pallas_call(
        body,
        out_shape=tuple([pltpu.SemaphoreType.DMA(())] * (2 * n) + hbm + [jax.ShapeDtypeStruct((8, LANES), f32)]),
        in_specs=[_HBM] * (2 * n),
        out_specs=tuple([_SEM] * (2 * n) + [_HBM] * (2 * n) + [pl.BlockSpec(memory_space=pltpu.VMEM)]),
        input_output_aliases={i: 2 * n + i for i in range(2 * n)},
        compiler_params=pltpu.CompilerParams(has_side_effects=_EFFECT),
        name=name,
    )(*ops)
    return outs[:n], outs[n:2 * n], outs[2 * n:3 * n], outs[3 * n:4 * n], outs[4 * n]


def _scatter_wait(send_sems, recv_sems, srcs, lands, after, name):
    n = len(srcs)

    def body(*refs):
        land = refs[n:2 * n]
        ssem, rsem = refs[2 * n:3 * n], refs[3 * n:4 * n]
        x, y, c = lax.axis_index("x"), lax.axis_index("y"), lax.axis_index("c")
        for a in range(n):
            done = pltpu.make_async_remote_copy(
                src_ref=land[a], dst_ref=land[a], send_sem=ssem[a], recv_sem=rsem[a], device_id=(x, y, c),
                device_id_type=pl.DeviceIdType.MESH)
            done.wait_send()
            done.wait_recv()

    hbm = [pltpu.HBM(a.shape, a.dtype) for a in list(srcs) + list(lands)]
    outs = pl.pallas_call(
        body,
        out_shape=tuple(hbm),
        in_specs=[_HBM] * (2 * n) + [_SEM] * (2 * n) + [pl.BlockSpec(memory_space=pl.ANY)],
        out_specs=tuple([_HBM] * (2 * n)),
        input_output_aliases={i: i for i in range(2 * n)},
        compiler_params=pltpu.CompilerParams(has_side_effects=_EFFECT),
        name=name,
    )(*srcs, *lands, *send_sems, *recv_sems, after)
    return outs[:n], outs[n:]


def _adam_update(g, w_ref, m_ref, v_ref, g_ref, d_ref, nm_ref, nv_ref):
    mm = ADAM_B1 * m_ref[...] + (1.0 - ADAM_B1) * g
    vv = ADAM_B2 * v_ref[...] + (1.0 - ADAM_B2) * (g * g)
    m_hat = mm / (1.0 - ADAM_B1 ** ADAM_STEP)
    v_hat = vv / (1.0 - ADAM_B2 ** ADAM_STEP)
    g_ref[...] = g
    d_ref[...] = -ADAM_LR * (m_hat / (jnp.sqrt(v_hat) + ADAM_EPS) + ADAM_WD * w_ref[...])
    nm_ref[...] = mm
    nv_ref[...] = vv


def _adamw_own(w, own, own_idx, slots, m, v, name):
    r, c = w.shape[-2:]
    tr = _row_tile(r, 128)
    k = slots.shape[0]

    def body(i_ref, w_ref, o_ref, s_ref, m_ref, v_ref, g_ref, d_ref, nm_ref, nv_ref):
        del i_ref
        g = o_ref[...].astype(f32)
        for j in range(k):
            g = g + s_ref[j].astype(f32)
        _adam_update(g, w_ref, m_ref, v_ref, g_ref, d_ref, nm_ref, nv_ref)

    blk = pl.BlockSpec((None, tr, c), lambda i, ix: (0, i, 0))
    return pl.pallas_call(
        body,
        grid_spec=pltpu.PrefetchScalarGridSpec(
            num_scalar_prefetch=1,
            grid=(r // tr,),
            in_specs=[blk, pl.BlockSpec((None, tr, c), lambda i, ix: (ix[0], i, 0)),
                      pl.BlockSpec((k, tr, c), lambda i, ix: (0, i, 0)), blk, blk],
            out_specs=[blk] * 4,
        ),
        out_shape=[jax.ShapeDtypeStruct(w.shape, f32)] * 4,
        compiler_params=_params(("arbitrary",)),
        name=name,
    )(own_idx, w, own, slots, m, v)


def _adamw(w, slots, m, v, name):
    r, c = w.shape[-2:]
    tr = _row_tile(r, 128)

    def body(w_ref, s_ref, m_ref, v_ref, g_ref, d_ref, nm_ref, nv_ref):
        g = s_ref[0].astype(f32)
        for k in range(1, NDEV):
            g = g + s_ref[k].astype(f32)
        _adam_update(g, w_ref, m_ref, v_ref, g_ref, d_ref, nm_ref, nv_ref)

    if w.ndim == 3:
        blk = pl.BlockSpec((None, tr, c), lambda i: (0, i, 0))
    else:
        blk = pl.BlockSpec((tr, c), lambda i: (i, 0))
    return pl.pallas_call(
        body,
        grid=(r // tr,),
        in_specs=[blk, pl.BlockSpec((NDEV, tr, c), lambda i: (0, i, 0)), blk, blk],
        out_specs=[blk] * 4,
        out_shape=[jax.ShapeDtypeStruct(w.shape, f32)] * 4,
        compiler_params=_params(("arbitrary",)),
        name=name,
    )(w, slots, m, v)


def _local_step(x, tgt, norm_gain, w_t, qn_a, kn_a, qn_b, kn_b, sink_a, rel_bias, w_a, w_b, b_merge, w_o,
                on_weight_grads=None):
    two = lambda t: jnp.concatenate([t, t], axis=-1).reshape(1, LANES)
    ones = jnp.ones((1, LANES), f32)
    gains = jnp.stack([
        jnp.stack([two(qn_a), two(kn_a), ones]),
        jnp.stack([two(qn_b), two(kn_b), ones]),
        jnp.stack([two(qn_b), two(kn_b), ones]),
        jnp.stack([two(qn_b), two(kn_b), ones]),
    ])
    buckets = [jnp.asarray(_bucket_np(blk, d)) for blk, d, _ in GROUPS]
    bias = [_bias_expand(rel_bias, buckets[k], GROUPS[k][2], "bias_expand_%d" % k) for k in range(4)]

    hb, hbt, rstd = _rms(x, norm_gain)
    proj = _inproj(hb, w_t)
    gl = _prep(proj, gains)
    o_a, l_a = _attn_fwd(gl, bias[0], sink_a.reshape(8), 0, 128, 1, "attn_fwd_a")
    fwd_b = [_attn_fwd(gl, bias[k], None, k, GROUPS[k][0], GROUPS[k][1], "attn_fwd_b%d" % k) for k in (1, 2, 3)]
    sink_b = jnp.repeat(sink_a.reshape(8), HD).reshape(1, 512)

    (dy, dyb, dproj, do_a, dd_a, do_b0, do_b1, do_b2, dd_b0, dd_b1, dd_b2, ya, yb, mg, dbr_a, dbr_b, loss, dbm,
     dsk) = _tail(x, tgt, o_a, l_a, [f[0] for f in fwd_b], [f[1] for f in fwd_b], proj, b_merge, w_a, w_b, w_o, sink_b)

    dqkv_a, dbk_a = _attn_bwd(gl, bias[0], buckets[0], do_a, l_a, dd_a, 0, 128, 1, "attn_bwd_a")
    dproj, dg_a = _post_a(dqkv_a, proj, gains, dproj)
    dbk_b, dg_b = [], []
    for k, do_k, dd_k in ((1, do_b0, dd_b0), (2, do_b1, dd_b1), (3, do_b2, dd_b2)):
        dqkv, dbk = _attn_bwd(gl, bias[k], buckets[k], do_k, fwd_b[k - 1][1], dd_k, k, GROUPS[k][0], GROUPS[k][1],
                              "attn_bwd_b%d" % k)
        dproj, dg = _post_b(k, dqkv, proj, gains, dproj)
        dbk_b.append(dbk)
        dg_b.append(dg)
    dg_b = jnp.stack(dg_b)

    dw_in = _dw_in(hbt, dproj)
    dw_o = _matmul_tokens(mg, dyb, "dw_out")
    dw_a = _matmul_tokens(ya, dbr_a, "dw_branch_a")
    dw_b = _matmul_tokens(yb, dbr_b, "dw_branch_b")
    token = jnp.zeros((), f32) if on_weight_grads is None else on_weight_grads(
        dict(w_in=dw_in, w_branch_a=dw_a, w_branch_b=dw_b, b_merge=dbm, w_out=dw_o))
    grad_x, d_norm_gain = _dh_norm_bwd(dproj, w_t, x, rstd, norm_gain + token, dy)

    fold = lambda t: t[..., :HD] + t[..., HD:]
    d_qn_a = fold(dg_a[0, 0])
    d_kn_a = fold(dg_a[1, 0])
    d_qn_b = fold(dg_b[:, 0, 0].sum(axis=0))
    d_kn_b = fold(dg_b[:, 1, 0].sum(axis=0))
    d_sink = dsk.reshape(8, HD)[:, 0]
    red = jnp.stack([dbk_a] + dbk_b)
    d_rel = red[:, :, 0, :32].reshape(32, 32).T
    return dict(loss=loss, grad_x=grad_x, norm_gain=d_norm_gain, w_in=dw_in, q_norm_a=d_qn_a, k_norm_a=d_kn_a,
                q_norm_b=d_qn_b, k_norm_b=d_kn_b, sink_a=d_sink, rel_bias=d_rel, w_branch_a=dw_a, w_branch_b=dw_b,
                b_merge=dbm, w_out=dw_o)


SMALL = (("norm_gain", D), ("q_norm_a", HD), ("k_norm_a", HD), ("q_norm_b", HD), ("k_norm_b", HD), ("sink_a", 8),
         ("rel_bias", 1024))
SMALL_PAD = 2432


SMALL_USED = sum(sz for _, sz in SMALL)


def _pack_small(parts, loss=None):
    tail = jnp.zeros((SMALL_PAD - SMALL_USED,), f32)
    if loss is not None:
        tail = tail.at[0].set(loss.reshape(()))
    return jnp.concatenate([parts[n].reshape(-1) for n, _ in SMALL] + [tail]).reshape(1, SMALL_PAD)


def _unpack_small(flat, shapes):
    out, off = {}, 0
    for n, sz in SMALL:
        out[n] = flat[0, off:off + sz].reshape(shapes[n])
        off += sz
    return out


def kernel(x, norm_gain, w_in, q_norm_a, k_norm_a, q_norm_b, k_norm_b, sink_a, rel_bias, w_branch_a, w_branch_b, b_merge, w_out, loss_target, m_norm_gain, m_w_in, m_q_norm_a, m_k_norm_a, m_q_norm_b, m_k_norm_b, m_sink_a, m_rel_bias, m_w_branch_a, m_w_branch_b, m_b_merge, m_w_out, v_norm_gain, v_w_in, v_q_norm_a, v_k_norm_a, v_q_norm_b, v_k_norm_b, v_sink_a, v_rel_bias, v_w_branch_a, v_w_branch_b, v_b_merge, v_w_out):
    csh = D // NDEV
    w_in_t, m_w_in_t, v_w_in_t = (jnp.swapaxes(t, 1, 2) for t in (w_in, m_w_in, v_w_in))
    g_in, g_a, g_b, g_o, g_bm = _gather_two_level(
        [w_in_t[0].astype(bf16), w_branch_a[0].astype(bf16), w_branch_b[0].astype(bf16), w_out[0].astype(bf16),
         b_merge[0]], "gather_weights")
    w_t_full = g_in.reshape(NW, D)
    w_a_full = g_a.transpose(1, 0, 2).reshape(512, D)
    w_b_full = g_b.transpose(1, 0, 2).reshape(512, D)
    w_o_full = g_o.reshape(D, D)
    bm_full = g_bm.transpose(1, 0, 2).reshape(2, D)

    pending = {}
    core = lax.axis_index("c").astype(jnp.int32).reshape(1)
    chip = (2 * lax.axis_index("x") + lax.axis_index("y")).astype(jnp.int32).reshape(1)
    me = (2 * chip + core).astype(jnp.int32)

    def start_exchange(gw):
        from_sibling = _sibling_exchange(gw["w_in"], "grad_sibling_exchange")
        chip_sums = _pair_sum(gw["w_in"], from_sibling, core, "grad_pair_sum")
        blocks = [gw["w_branch_a"].reshape(512, NDEV, csh).transpose(1, 0, 2).astype(bf16),
                  gw["w_branch_b"].reshape(512, NDEV, csh).transpose(1, 0, 2).astype(bf16),
                  gw["w_out"].reshape(NDEV, csh, D).astype(bf16),
                  gw["b_merge"].reshape(2, NDEV, csh).transpose(1, 0, 2)]
        pending["started"] = _scatter_start([chip_sums], blocks, "scatter_grads_start")
        return pending["started"][4][0, 0]

    loc = _local_step(x[0], loss_target[0], norm_gain, w_t_full, q_norm_a, k_norm_a, q_norm_b, k_norm_b, sink_a,
                      rel_bias, w_a_full, w_b_full, bm_full, w_o_full, on_weight_grads=start_exchange)

    small_shapes = dict(norm_gain=(1, D), q_norm_a=(1, HD), k_norm_a=(1, HD), q_norm_b=(1, HD), k_norm_b=(1, HD),
                        sink_a=(1, 8), rel_bias=(32, 32))
    (r_small,) = _exchange([], [_pack_small(loc, loc["loss"])], "gather_small_grads")
    send_sems, recv_sems, srcs, lands, _ = pending["started"]
    (s_in, s_a, s_b, s_o, s_bm), (r_in, r_a, r_b, r_o, r_bm) = _scatter_wait(
        send_sems, recv_sems, srcs, lands, r_small, "scatter_grads_wait")

    given = dict(norm_gain=norm_gain, q_norm_a=q_norm_a, k_norm_a=k_norm_a, q_norm_b=q_norm_b, k_norm_b=k_norm_b,
                 sink_a=sink_a, rel_bias=rel_bias)
    m_small = dict(norm_gain=m_norm_gain, q_norm_a=m_q_norm_a, k_norm_a=m_k_norm_a, q_norm_b=m_q_norm_b,
                   k_norm_b=m_k_norm_b, sink_a=m_sink_a, rel_bias=m_rel_bias)
    v_small = dict(norm_gain=v_norm_gain, q_norm_a=v_q_norm_a, k_norm_a=v_k_norm_a, q_norm_b=v_q_norm_b,
                   k_norm_b=v_k_norm_b, sink_a=v_sink_a, rel_bias=v_rel_bias)
    res = {
        "small": _adamw(_pack_small(given), r_small, _pack_small(m_small), _pack_small(v_small), "adamw_small"),
        "w_in": [jnp.swapaxes(t, 1, 2) for t in
                 _adamw_own(w_in_t, s_in, chip, r_in, m_w_in_t, v_w_in_t, "adamw_w_in")],
        "w_branch_a": _adamw_own(w_branch_a, s_a, me, r_a, m_w_branch_a, v_w_branch_a, "adamw_w_branch_a"),
        "w_branch_b": _adamw_own(w_branch_b, s_b, me, r_b, m_w_branch_b, v_w_branch_b, "adamw_w_branch_b"),
        "b_merge": _adamw_own(b_merge, s_bm, me, r_bm, m_b_merge, v_b_merge, "adamw_b_merge"),
        "w_out": _adamw_own(w_out, s_o, me, r_o, m_w_out, v_w_out, "adamw_w_out"),
    }
    order = ["norm_gain", "w_in", "q_norm_a", "k_norm_a", "q_norm_b", "k_norm_b", "sink_a", "rel_bias", "w_branch_a",
             "w_branch_b", "b_merge", "w_out"]
    outs = []
    for k in range(4):
        small = _unpack_small(res["small"][k], small_shapes)
        for n in order:
            outs.append(small[n] if n in small else res[n][k])
    loss = res["small"][0][0, SMALL_USED]
    return (loss, loc["grad_x"][None], *outs)
```

```python
import math

import numpy as np
import jax
import jax.numpy as jnp
from jax import lax
from jax.experimental import pallas as pl
from jax.experimental.pallas import tpu as pltpu

f32 = jnp.float32
bf16 = jnp.bfloat16

S = 4096
D = 1024
NA = 5376
NT = 3072
NW = NA + NT
WSH = NW // 8
HD = 64
LANES = 128
EPS = 1e-6
NEG = -1e30
SCALE = HD ** -0.5
TQ = 128
PAD = 128
SP = S + 2 * PAD
NDEV = 8
GROUPS = ((128, 1, 0), (64, 1, 8), (64, 4, 16), (64, 16, 24))
CHUNK = 256
PCHUNK = 128
RC = 64
TN = 768

ADAM_LR, ADAM_B1, ADAM_B2, ADAM_EPS, ADAM_WD, ADAM_STEP = 0.001, 0.9, 0.999, 1e-08, 0.01, 10

MIB = 1024 * 1024
NT_DIMS = (((1,), (1,)), ((), ()))
TN_DIMS = (((0,), (0,)), ((), ()))


def _params(sem=None, vmem_mib=48):
    return pltpu.CompilerParams(dimension_semantics=sem, vmem_limit_bytes=vmem_mib * MIB)


def _lo():
    return lax.broadcasted_iota(jnp.int32, (1, LANES), 1) < HD


def _head_ones():
    r = lax.broadcasted_iota(jnp.int32, (LANES, LANES), 0) // HD
    c = lax.broadcasted_iota(jnp.int32, (LANES, LANES), 1) // HD
    return jnp.where(r == c, 1.0, 0.0).astype(bf16)


def _half_sums(x, ones):
    hi = x.astype(bf16)
    mid = (x - hi.astype(f32)).astype(bf16)
    return (jnp.dot(hi, ones, preferred_element_type=f32) + jnp.dot(mid, ones, preferred_element_type=f32))


def _seg_sum(x, ones):
    outs = [_half_sums(x[:, b * LANES:(b + 1) * LANES], ones) for b in range(x.shape[1] // LANES)]
    return outs[0] if len(outs) == 1 else jnp.concatenate(outs, axis=1)


def _bucket_np(blk, stride):
    w = TQ + 2 * blk
    rel = np.arange(w)[None, :] - blk - np.arange(TQ)[:, None]
    band = np.abs(rel) <= blk
    r = rel * stride
    n = np.abs(r)
    nf = np.maximum(n, 8).astype(np.float32)
    large = 8 + (np.log(nf / np.float32(8)) / np.float32(math.log(128.0)) * np.float32(8)).astype(np.int32)
    large = np.minimum(large, 15)
    b = (r > 0).astype(np.int32) * 16 + np.where(n < 8, n, large)
    return np.where(band, b, -1).astype(np.int32)


def _rms(x, gain):
    ts = 512

    def body(x_ref, g_ref, h_ref, ht_ref, r_ref):
        xv = x_ref[...]
        r = lax.rsqrt(jnp.mean(xv * xv, axis=-1, keepdims=True) + EPS)
        h = (xv * r) * g_ref[...]
        h_ref[...] = h.astype(bf16)
        ht_ref[...] = h.T.astype(bf16)
        r_ref[...] = r

    return pl.pallas_call(
        body,
        grid=(S // ts,),
        in_specs=[pl.BlockSpec((ts, D), lambda i: (i, 0)), pl.BlockSpec((1, D), lambda i: (0, 0))],
        out_specs=[pl.BlockSpec((ts, D), lambda i: (i, 0)), pl.BlockSpec((D, ts), lambda i: (0, i)),
                   pl.BlockSpec((ts, 1), lambda i: (i, 0))],
        out_shape=[jax.ShapeDtypeStruct((S, D), bf16), jax.ShapeDtypeStruct((D, S), bf16),
                   jax.ShapeDtypeStruct((S, 1), f32)],
        compiler_params=_params(("arbitrary",)),
        name="rms",
    )(x, gain)


def _inproj(hb, w_t):
    ts = 2048

    def body(h_ref, w_ref, p_ref):
        p_ref[...] = lax.dot_general(h_ref[...], w_ref[...], NT_DIMS, preferred_element_type=f32)

    return pl.pallas_call(
        body,
        grid=(S // ts, NW // TN),
        in_specs=[pl.BlockSpec((ts, D), lambda i, n: (i, 0)), pl.BlockSpec((TN, D), lambda i, n: (n, 0))],
        out_specs=pl.BlockSpec((ts, TN), lambda i, n: (i, n)),
        out_shape=jax.ShapeDtypeStruct((S, NW), f32),
        compiler_params=_params(("arbitrary", "arbitrary")),
        name="inproj",
    )(hb, w_t)


def _bias_expand(table, bucket, c0, name):
    tq, w = bucket.shape
    blk = (w - tq) // 2

    def body(tab_ref, bk_ref, o_ref):
        h = pl.program_id(0)
        bk = bk_ref[...]

        def step(b, acc):
            return jnp.where(bk == b, tab_ref[b, c0 + h], acc)

        inner = lax.fori_loop(0, 32, step, jnp.full((tq, w), NEG, f32))
        col = lax.broadcasted_iota(jnp.int32, (1, w), 1)
        o_ref[0] = jnp.where(col < blk, NEG, inner)
        o_ref[1] = inner
        o_ref[2] = jnp.where(col >= tq + blk, NEG, inner)

    return pl.pallas_call(
        body,
        grid=(8,),
        in_specs=[pl.BlockSpec(memory_space=pltpu.SMEM), pl.BlockSpec((tq, w), lambda h: (0, 0))],
        out_specs=pl.BlockSpec((3, None, tq, w), lambda h: (0, h, 0, 0)),
        out_shape=jax.ShapeDtypeStruct((3, 8, tq, w), f32),
        compiler_params=_params(("arbitrary",)),
        name=name,
    )(table, bucket)


def _tile_kind(t, seq):
    m0 = jnp.bitwise_and(t * TQ, seq - 1)
    return jnp.where(m0 == 0, 0, jnp.where(m0 == seq - TQ, 2, 1))


def _col_block(g, j):
    kind = j // 4
    hp = j % 4
    a = jnp.where(kind == 0, hp, 3 + kind)
    b = 6 + 12 * kind + 4 * (g - 1) + hp
    return jnp.where(g == 0, a, b)


def _prep(proj_a, gains):
    def body(p_ref, g_ref, o_ref):
        g = pl.program_id(0)
        j = pl.program_id(1)
        kind = j // 4
        lo = _lo()
        ones = _head_ones()
        half = jnp.where(lo, 0, 1)
        take = (kind == 0) | (half == (j % 4) // 2)
        gain = g_ref[...]
        o_ref[0:PAD, :] = jnp.zeros((PAD, LANES), bf16)
        o_ref[PAD + S:SP, :] = jnp.zeros((PAD, LANES), bf16)

        def norm_store(xv, dst, dup):
            if dup:
                xv = jnp.where(take, xv, pltpu.roll(xv, HD, 1))
            r = lax.rsqrt(_half_sums(xv * xv, ones) * (1.0 / HD) + EPS)
            r = jnp.where(kind == 2, 1.0, r)
            yv = (xv * r) * gain
            yv = jnp.where(kind == 0, yv * SCALE, yv)
            o_ref[PAD + dst:PAD + dst + CHUNK, :] = yv.astype(bf16)

        for gi, (_, d, _) in enumerate(GROUPS):
            @pl.when(g == gi)
            def _():
                seq = S // d
                for c in range(d):
                    for i in range(seq // CHUNK):
                        if d == 1:
                            xv = p_ref[i * CHUNK:(i + 1) * CHUNK, :]
                        else:
                            xv = p_ref[pl.ds(c + i * CHUNK * d, CHUNK, stride=d), :]
                        norm_store(xv, c * seq + i * CHUNK, gi == 0)

    return pl.pallas_call(
        body,
        grid=(4, 12),
        in_specs=[
            pl.BlockSpec((S, LANES), lambda g, j: (0, _col_block(g, j))),
            pl.BlockSpec((None, None, 1, LANES), lambda g, j: (g, j // 4, 0, 0)),
        ],
        out_specs=pl.BlockSpec((None, None, SP, LANES), lambda g, j: (g, j, 0, 0)),
        out_shape=jax.ShapeDtypeStruct((4, 12, SP, LANES), bf16),
        compiler_params=_params(("arbitrary", "arbitrary")),
        name="prep",
    )(proj_a, gains)


def _token_rows(t, r0, n, d):
    if d == 1:
        return pl.ds(pl.multiple_of(t * TQ, TQ) + r0, n)
    per = S // d // TQ
    return pl.ds(((t % per) * TQ + r0) * d + t // per, n, stride=d)


def _stack_heads(t, lo):
    z = jnp.zeros_like(t)
    return jnp.concatenate([jnp.where(lo, t, z), jnp.where(lo, z, t)], axis=0)


def _unstack_heads(t2, lo):
    return jnp.where(lo, t2[:TQ], t2[TQ:])


def _attn_fwd(gl, bias, sink, g, blk, d, name):
    w = TQ + 2 * blk
    seq = S // d
    use_sink = sink is not None

    def body(*refs):
        if use_sink:
            sink_ref, q_ref, k_ref, v_ref, b_ref, o_ref, l_ref, s0, s1, p0, p1, lse_scr = refs
        else:
            q_ref, k_ref, v_ref, b_ref, o_ref, l_ref, s0, s1, p0, p1, lse_scr = refs
        hp = pl.program_id(0)
        lo = _lo()
        s_bufs, p_bufs = (s0, s1), (p0, p1)

        def scores(p, slot):
            for u in range(2):
                f0 = pl.multiple_of((2 * p + u) * TQ, TQ)
                q2 = _stack_heads(q_ref[pl.ds(PAD + f0, TQ), :], lo)
                kw = k_ref[pl.ds(PAD - blk + f0, w), :]
                s_bufs[slot][u] = lax.dot_general(q2, kw, NT_DIMS, preferred_element_type=f32)

        def softmax(p, slot):
            for u in range(2):
                t = 2 * p + u
                kind = _tile_kind(t, seq)
                for h in range(2):
                    for r in range(TQ // RC):
                        rows = slice(h * TQ + r * RC, h * TQ + (r + 1) * RC)
                        logit = s_bufs[slot][u, rows, :] + b_ref[kind, h, r * RC:(r + 1) * RC, :]
                        m = jnp.max(logit, axis=1, keepdims=True)
                        e = jnp.exp(logit - m)
                        lse = m + jnp.log(jnp.sum(e, axis=1, keepdims=True))
                        if use_sink:
                            sk = sink_ref[2 * hp + h]
                            mx = jnp.maximum(lse, sk)
                            lse = mx + jnp.log(jnp.exp(lse - mx) + jnp.exp(sk - mx))
                        p_bufs[slot][u, rows, :] = (e * jnp.exp(m - lse)).astype(bf16)
                        lse_scr[u, rows, :] = jnp.broadcast_to(lse, (RC, LANES))
                l_ref[_token_rows(t, 0, TQ, d), :] = jnp.where(lo, lse_scr[u, 0:TQ, :], lse_scr[u, TQ:2 * TQ, :])

        def values(p, slot):
            for u in range(2):
                t = 2 * p + u
                vw = v_ref[pl.ds(PAD - blk + pl.multiple_of(t * TQ, TQ), w), :]
                o2 = jnp.dot(p_bufs[slot][u], vw, preferred_element_type=f32)
                o_ref[_token_rows(t, 0, TQ, d), :] = _unstack_heads(o2, lo)

        npair = S // TQ // 2
        scores(0, 0)
        scores(1, 1)
        softmax(0, 0)

        def steady(k, carry):
            p = 2 * k + 2
            scores(p, 0)
            softmax(p - 1, 1)
            values(p - 2, 0)
            scores(p + 1, 1)
            softmax(p, 0)
            values(p - 1, 1)
            return carry

        lax.fori_loop(0, (npair - 2) // 2, steady, 0)
        softmax(npair - 1, 1)
        values(npair - 2, 0)
        values(npair - 1, 1)

    in_specs = [
        pl.BlockSpec((None, None, SP, LANES), lambda hp: (g, hp, 0, 0)),
        pl.BlockSpec((None, None, SP, LANES), lambda hp: (g, 4 + hp, 0, 0)),
        pl.BlockSpec((None, None, SP, LANES), lambda hp: (g, 8 + hp, 0, 0)),
        pl.BlockSpec((3, 2, TQ, w), lambda hp: (0, hp, 0, 0)),
    ]
    args = [gl, gl, gl, bias]
    if use_sink:
        in_specs = [pl.BlockSpec(memory_space=pltpu.SMEM)] + in_specs
        args = [sink] + args
    out = pl.BlockSpec((S, LANES), lambda hp: (0, hp))
    return pl.pallas_call(
        body,
        grid=(4,),
        in_specs=in_specs,
        out_specs=[out, out],
        out_shape=[jax.ShapeDtypeStruct((S, 4 * LANES), f32)] * 2,
        scratch_shapes=[pltpu.VMEM((2, 2 * TQ, w), f32), pltpu.VMEM((2, 2 * TQ, w), f32),
                        pltpu.VMEM((2, 2 * TQ, w), bf16), pltpu.VMEM((2, 2 * TQ, w), bf16),
                        pltpu.VMEM((2, 2 * TQ, LANES), f32)],
        compiler_params=_params(("arbitrary",)),
        name=name,
    )(*args)


def _attn_bwd(gl, bias, bucket, do, lse, dd, g, blk, d, name):
    w = TQ + 2 * blk
    seq = S // d

    def body(q_ref, k_ref, v_ref, b_ref, bk_ref, do_ref, l_ref, d_ref, dqkv_ref, dbk_ref,
             db_acc, s0, s1, dp0, dp1, pb0, pb1, ds0, ds1):
        lo = _lo()
        hi = jnp.logical_not(lo)
        dqkv_ref[1] = jnp.zeros((SP, LANES), f32)
        dqkv_ref[2] = jnp.zeros((SP, LANES), f32)
        db_acc[...] = jnp.zeros((2 * TQ, w), f32)
        s_bufs, dp_bufs, pb_bufs, ds_bufs = (s0, s1), (dp0, dp1), (pb0, pb1), (ds0, ds1)

        def stacked(t):
            f0 = pl.multiple_of(t * TQ, TQ)
            q2 = _stack_heads(q_ref[pl.ds(PAD + f0, TQ), :], lo)
            do2 = _stack_heads(do_ref[_token_rows(t, 0, TQ, d), :].astype(bf16), lo)
            return f0, q2, do2

        def scores(p, slot):
            for u in range(2):
                f0, q2, do2 = stacked(2 * p + u)
                win = pl.ds(PAD - blk + f0, w)
                s_bufs[slot][u] = lax.dot_general(q2, k_ref[win, :], NT_DIMS, preferred_element_type=f32)
                dp_bufs[slot][u] = lax.dot_general(do2, v_ref[win, :], NT_DIMS, preferred_element_type=f32)

        def grads(p, slot):
            for u in range(2):
                t = 2 * p + u
                kind = _tile_kind(t, seq)
                for h in range(2):
                    msk = lo if h == 0 else hi
                    for r in range(TQ // RC):
                        rows = slice(h * TQ + r * RC, h * TQ + (r + 1) * RC)
                        src = _token_rows(t, r * RC, RC, d)
                        lh = jnp.max(jnp.where(msk, l_ref[src, :], -jnp.inf), axis=1, keepdims=True)
                        dh = jnp.max(jnp.where(msk, d_ref[src, :], -jnp.inf), axis=1, keepdims=True)
                        logit = s_bufs[slot][u, rows, :] + b_ref[kind, h, r * RC:(r + 1) * RC, :]
                        pr = jnp.exp(logit - lh)
                        ds = pr * (dp_bufs[slot][u, rows, :] - dh)
                        db_acc[rows, :] += ds
                        pb_bufs[slot][u, rows, :] = pr.astype(bf16)
                        ds_bufs[slot][u, rows, :] = ds.astype(bf16)

        def accumulate(p, slot):
            for u in range(2):
                f0, q2, do2 = stacked(2 * p + u)
                win = pl.ds(PAD - blk + f0, w)
                dsb = ds_bufs[slot][u]
                dq2 = jnp.dot(dsb, k_ref[win, :], preferred_element_type=f32)
                dqkv_ref[0, pl.ds(PAD + f0, TQ), :] = _unstack_heads(dq2, lo)
                dqkv_ref[1, win, :] += lax.dot_general(dsb, q2, TN_DIMS, preferred_element_type=f32)
                dqkv_ref[2, win, :] += lax.dot_general(pb_bufs[slot][u], do2, TN_DIMS, preferred_element_type=f32)

        npair = S // TQ // 2
        scores(0, 0)
        scores(1, 1)
        grads(0, 0)

        def steady(k, carry):
            p = 2 * k + 2
            scores(p, 0)
            grads(p - 1, 1)
            accumulate(p - 2, 0)
            scores(p + 1, 1)
            grads(p, 0)
            accumulate(p - 1, 1)
            return carry

        lax.fori_loop(0, (npair - 2) // 2, steady, 0)
        grads(npair - 1, 1)
        accumulate(npair - 2, 0)
        accumulate(npair - 1, 1)

        bk = bk_ref[...]
        lane = lax.broadcasted_iota(jnp.int32, (8, LANES), 1)
        for h in range(2):
            db = db_acc[h * TQ:(h + 1) * TQ, :]
            acc = jnp.zeros((8, LANES), f32)
            for b in range(32):
                part = jnp.where(bk == b, db, 0.0).reshape(TQ // 8, 8, w).sum(axis=0)
                tot = jnp.sum(jnp.sum(part, axis=1, keepdims=True), axis=0, keepdims=True)
                acc = jnp.where(lane == b, tot, acc)
            dbk_ref[h] = acc

    def gcol(off):
        return pl.BlockSpec((None, None, SP, LANES), lambda hp: (g, off + hp, 0, 0))

    row = pl.BlockSpec((S, LANES), lambda hp: (0, hp))
    return pl.pallas_call(
        body,
        grid=(4,),
        in_specs=[gcol(0), gcol(4), gcol(8), pl.BlockSpec((3, 2, TQ, w), lambda hp: (0, hp, 0, 0)),
                  pl.BlockSpec((TQ, w), lambda hp: (0, 0)), row, row, row],
        out_specs=[pl.BlockSpec((3, None, SP, LANES), lambda hp: (0, hp, 0, 0)),
                   pl.BlockSpec((2, 8, LANES), lambda hp: (hp, 0, 0))],
        out_shape=[
            jax.ShapeDtypeStruct((3, 4, SP, LANES), f32),
            jax.ShapeDtypeStruct((8, 8, LANES), f32),
        ],
        scratch_shapes=([pltpu.VMEM((2 * TQ, w), f32)] + [pltpu.VMEM((2, 2 * TQ, w), f32)] * 4
                        + [pltpu.VMEM((2, 2 * TQ, w), bf16)] * 4),
        compiler_params=_params(("arbitrary",), vmem_mib=56),
        name=name,
    )(gl, gl, gl, bias, bucket, do, lse, dd)


def _sigmoid(z):
    return 1.0 / (1.0 + jnp.exp(-z))


def _tail(x, tgt, o_a, l_a, o_b, l_b, proj, bm, w_a, w_b, w_o, sink_b):
    ts = 256

    def body(x_ref, t_ref, oa_ref, la_ref, ob0_ref, ob1_ref, ob2_ref, lb0_ref, lb1_ref, lb2_ref,
             ga_ref, gb_ref, m0_ref, m1_ref, bm_ref, wa_ref, wb_ref, wo_ref, sk_ref,
             dy_ref, dyb_ref, dt_ref, doa_ref, dda_ref, dob0_ref, dob1_ref, dob2_ref, ddb0_ref, ddb1_ref, ddb2_ref,
             ya_ref, yb_ref, mg_ref, dbra_ref, dbrb_ref, loss_ref, dbm_ref, dsk_ref):
        i = pl.program_id(0)

        @pl.when(i == 0)
        def _():
            loss_ref[...] = jnp.zeros_like(loss_ref)
            dbm_ref[...] = jnp.zeros_like(dbm_ref)
            dsk_ref[...] = jnp.zeros_like(dsk_ref)

        ga = ga_ref[...]
        sa = _sigmoid(ga)
        silu_a = ga * sa
        oa = oa_ref[...]
        ya = oa * silu_a
        gb = gb_ref[...]
        sb = _sigmoid(gb)
        silu_b = gb * sb
        ob = [ob0_ref[...], ob1_ref[...], ob2_ref[...]]
        lb = [lb0_ref[...], lb1_ref[...], lb2_ref[...]]
        mx = jnp.maximum(jnp.maximum(lb[0], lb[1]), lb[2])
        ex = [jnp.exp(v - mx) for v in lb]
        den = ex[0] + ex[1] + ex[2]
        alpha = [e / den for e in ex]
        ybc = alpha[0] * ob[0] + alpha[1] * ob[1] + alpha[2] * ob[2]
        yb = ybc * silu_b
        yab = ya.astype(bf16)
        ybb = yb.astype(bf16)
        br_a = jnp.dot(yab, wa_ref[...], preferred_element_type=f32)
        br_b = jnp.dot(ybb, wb_ref[...], preferred_element_type=f32)
        g0 = _sigmoid(m0_ref[...] + bm_ref[0:1, :])
        g1 = _sigmoid(m1_ref[...] + bm_ref[1:2, :])
        merged = g0 * br_a + g1 * br_b
        mgb = merged.astype(bf16)
        y = x_ref[...] + jnp.dot(mgb, wo_ref[...], preferred_element_type=f32)
        err = y - t_ref[...]
        part = jnp.sum(jnp.sum(err * err, axis=1, keepdims=True), axis=0, keepdims=True)
        loss_ref[...] += part * (0.5 / D)
        dy = err * (1.0 / D)
        dyb = dy.astype(bf16)
        dmerged = lax.dot_general(dyb, wo_ref[...], NT_DIMS, preferred_element_type=f32)
        dbr_a = (dmerged * g0).astype(bf16)
        dbr_b = (dmerged * g1).astype(bf16)
        dm0 = dmerged * br_a * (g0 * (1.0 - g0))
        dm1 = dmerged * br_b * (g1 * (1.0 - g1))
        dbm_ref[0:1, :] += jnp.sum(dm0, axis=0, keepdims=True)
        dbm_ref[1:2, :] += jnp.sum(dm1, axis=0, keepdims=True)
        dya = lax.dot_general(dbr_a, wa_ref[...], NT_DIMS, preferred_element_type=f32)
        dyb2 = lax.dot_general(dbr_b, wb_ref[...], NT_DIMS, preferred_element_type=f32)
        do_a = dya * silu_a
        dga = dya * oa * (sa * (1.0 + ga * (1.0 - sa)))
        ones = _head_ones()
        delta_a = _seg_sum(do_a * oa, ones)
        dsk_ref[...] -= jnp.sum(delta_a * jnp.exp(sk_ref[...] - la_ref[...]), axis=0, keepdims=True)
        dybc = dyb2 * silu_b
        dgb = dyb2 * ybc * (sb * (1.0 + gb * (1.0 - sb)))
        dbar = _seg_sum(dybc * ybc, ones)
        dy_ref[...] = dy
        dyb_ref[...] = dyb
        dt_ref[:, 0:512] = dga.astype(bf16)
        dt_ref[:, 512:1024] = dgb.astype(bf16)
        dt_ref[:, 1024:2048] = dm0.astype(bf16)
        dt_ref[:, 2048:3072] = dm1.astype(bf16)
        doa_ref[...] = do_a.astype(bf16)
        dda_ref[...] = delta_a
        for k, (dob_ref, ddb_ref) in enumerate(((dob0_ref, ddb0_ref), (dob1_ref, ddb1_ref), (dob2_ref, ddb2_ref))):
            dob_ref[...] = alpha[k] * dybc
            ddb_ref[...] = alpha[k] * dbar
        ya_ref[...] = ya.T.astype(bf16)
        yb_ref[...] = yb.T.astype(bf16)
        mg_ref[...] = merged.T.astype(bf16)
        dbra_ref[...] = dbr_a
        dbrb_ref[...] = dbr_b

    def rows(n, blk=0):
        return pl.BlockSpec((ts, n), lambda i: (i, blk))

    def whole(r, c):
        return pl.BlockSpec((r, c), lambda i: (0, 0))

    def cols(n):
        return pl.BlockSpec((n, ts), lambda i: (0, i))

    def gate_cols(n, col):
        return pl.BlockSpec((pl.Element(ts), pl.Element(n)), lambda i: (i * ts, NA + col))

    outs = [
        ((S, D), f32, rows(D)), ((S, D), bf16, rows(D)), ((S, NW), bf16, gate_cols(NT, 0)),
        ((S, 512), bf16, rows(512)), ((S, 512), f32, rows(512)),
        ((S, 512), f32, rows(512)), ((S, 512), f32, rows(512)), ((S, 512), f32, rows(512)),
        ((S, 512), f32, rows(512)), ((S, 512), f32, rows(512)), ((S, 512), f32, rows(512)),
        ((512, S), bf16, cols(512)), ((512, S), bf16, cols(512)), ((D, S), bf16, cols(D)),
        ((S, D), bf16, rows(D)), ((S, D), bf16, rows(D)),
        ((1, 1), f32, whole(1, 1)), ((2, D), f32, whole(2, D)), ((1, 512), f32, whole(1, 512)),
    ]
    return pl.pallas_call(
        body,
        grid=(S // ts,),
        in_specs=[
            rows(D), rows(D), rows(512), rows(512), rows(512), rows(512), rows(512), rows(512), rows(512), rows(512),
            gate_cols(512, 0), gate_cols(512, 512), gate_cols(D, 1024), gate_cols(D, 2048), whole(2, D),
            whole(512, D), whole(512, D), whole(D, D), whole(1, 512),
        ],
        out_specs=[o[2] for o in outs],
        out_shape=[jax.ShapeDtypeStruct(o[0], o[1]) for o in outs],
        compiler_params=_params(("arbitrary",), vmem_mib=60),
        name="tail",
    )(x, tgt, o_a, l_a, *o_b, *l_b, proj, proj, proj, proj, bm, w_a, w_b, w_o, sink_b)


def _norm_bwd(xv, dyv, gain, kind, ones):
    r = lax.rsqrt(_half_sums(xv * xv, ones) * (1.0 / HD) + EPS)
    yv = xv * r
    up = jnp.where(kind == 0, dyv * SCALE, dyv)
    u = up * gain
    dxv = r * (u - yv * (_half_sums(u * yv, ones) * (1.0 / HD)))
    dxv = jnp.where(kind == 2, dyv, dxv)
    dg = jnp.where(kind == 2, 0.0, jnp.sum(up * yv, axis=0, keepdims=True))
    return dxv, dg


def _post_b(g, dqkv, proj_a, gains, dproj):
    d = GROUPS[g][1]
    seq = S // d

    def body(d_ref, p_ref, g_ref, alias_ref, o_ref, dg_ref, nat):
        del alias_ref
        j = pl.program_id(0)
        kind = j // 4
        gain = g_ref[...]
        ones = _head_ones()

        @pl.when(j % 4 == 0)
        def _():
            dg_ref[...] = jnp.zeros_like(dg_ref)

        for c in range(d):
            for i in range(seq // PCHUNK):
                src = c * seq + i * PCHUNK
                if d == 1:
                    idx = slice(src, src + PCHUNK)
                else:
                    idx = pl.ds(c + i * PCHUNK * d, PCHUNK, stride=d)
                dxv, dg = _norm_bwd(p_ref[idx, :], d_ref[PAD + src:PAD + src + PCHUNK, :], gain, kind, ones)
                nat[idx, :] = dxv
                dg_ref[...] += dg

        for i in range(S // CHUNK):
            o_ref[i * CHUNK:(i + 1) * CHUNK, :] = nat[i * CHUNK:(i + 1) * CHUNK, :].astype(bf16)

    return pl.pallas_call(
        body,
        grid=(12,),
        in_specs=[
            pl.BlockSpec((None, None, SP, LANES), lambda j: (j // 4, j % 4, 0, 0)),
            pl.BlockSpec((S, LANES), lambda j: (0, _col_block(g, j))),
            pl.BlockSpec((None, None, 1, LANES), lambda j: (g, j // 4, 0, 0)),
            pl.BlockSpec(memory_space=pl.ANY),
        ],
        out_specs=[
            pl.BlockSpec((S, LANES), lambda j: (0, _col_block(g, j))),
            pl.BlockSpec((None, 1, LANES), lambda j: (j // 4, 0, 0)),
        ],
        out_shape=[jax.ShapeDtypeStruct((S, NW), bf16), jax.ShapeDtypeStruct((3, 1, LANES), f32)],
        scratch_shapes=[pltpu.VMEM((S, LANES), f32)],
        input_output_aliases={3: 0},
        compiler_params=_params(("arbitrary",)),
        name="post_b%d" % g,
    )(dqkv, proj_a, gains, dproj)


def _post_a(dqkv, proj_a, gains, dproj):
    def body(q_ref, e_ref, p_ref, g_ref, alias_ref, o_ref, dg_ref):
        del alias_ref
        j = pl.program_id(0)
        kind = jnp.maximum(j - 3, 0)
        gain = g_ref[...]
        lo = _lo()
        ones = _head_ones()

        @pl.when((j == 0) | (j >= 4))
        def _():
            dg_ref[...] = jnp.zeros_like(dg_ref)

        for i in range(S // PCHUNK):
            r0 = i * PCHUNK
            rows = slice(PAD + r0, PAD + r0 + PCHUNK)
            t0 = e_ref[0, rows, :] + e_ref[1, rows, :]
            t1 = e_ref[2, rows, :] + e_ref[3, rows, :]
            folded = jnp.where(lo, t0 + pltpu.roll(t0, HD, 1), t1 + pltpu.roll(t1, HD, 1))
            dyv = jnp.where(kind == 0, q_ref[rows, :], folded)
            dxv, dg = _norm_bwd(p_ref[r0:r0 + PCHUNK, :], dyv, gain, kind, ones)
            o_ref[r0:r0 + PCHUNK, :] = dxv.astype(bf16)
            dg_ref[...] += dg

    return pl.pallas_call(
        body,
        grid=(6,),
        in_specs=[
            pl.BlockSpec((None, None, SP, LANES), lambda j: (0, jnp.minimum(j, 3), 0, 0)),
            pl.BlockSpec((None, 4, SP, LANES), lambda j: (jnp.clip(j - 3, 1, 2), 0, 0, 0)),
            pl.BlockSpec((S, LANES), lambda j: (0, j)),
            pl.BlockSpec((None, None, 1, LANES), lambda j: (0, jnp.maximum(j - 3, 0), 0, 0)),
            pl.BlockSpec(memory_space=pl.ANY),
        ],
        out_specs=[
            pl.BlockSpec((S, LANES), lambda j: (0, j)),
            pl.BlockSpec((None, 1, LANES), lambda j: (jnp.maximum(j - 3, 0), 0, 0)),
        ],
        out_shape=[jax.ShapeDtypeStruct((S, NW), bf16), jax.ShapeDtypeStruct((3, 1, LANES), f32)],
        input_output_aliases={4: 0},
        compiler_params=_params(("arbitrary",)),
        name="post_a",
    )(dqkv, dqkv, proj_a, gains, dproj)


def _dh_norm_bwd(dproj, w, x, rstd, gain, dy):
    ts = 1024
    tk = NW // 6
    nk = NW // tk

    def body(d_ref, w_ref, x_ref, r_ref, g_ref, dy_ref, gx_ref, dgn_ref, acc):
        i = pl.program_id(0)
        k = pl.program_id(1)

        @pl.when((i == 0) & (k == 0))
        def _():
            dgn_ref[...] = jnp.zeros_like(dgn_ref)

        @pl.when(k == 0)
        def _():
            acc[...] = jnp.zeros_like(acc)

        acc[...] += jnp.dot(d_ref[...], w_ref[...], preferred_element_type=f32)

        @pl.when(k == nk - 1)
        def _():
            dh = acc[...]
            xh = x_ref[...] * r_ref[...]
            u = dh * g_ref[...]
            dx = r_ref[...] * (u - xh * jnp.mean(u * xh, axis=-1, keepdims=True))
            gx_ref[...] = dy_ref[...] + dx
            dgn_ref[...] += jnp.sum(dh * xh, axis=0, keepdims=True)

    return pl.pallas_call(
        body,
        grid=(S // ts, nk),
        in_specs=[
            pl.BlockSpec((ts, tk), lambda i, k: (i, k)),
            pl.BlockSpec((tk, D), lambda i, k: (k, 0)),
            pl.BlockSpec((ts, D), lambda i, k: (i, 0)),
            pl.BlockSpec((ts, 1), lambda i, k: (i, 0)),
            pl.BlockSpec((1, D), lambda i, k: (0, 0)),
            pl.BlockSpec((ts, D), lambda i, k: (i, 0)),
        ],
        out_specs=[pl.BlockSpec((ts, D), lambda i, k: (i, 0)), pl.BlockSpec((1, D), lambda i, k: (0, 0))],
        out_shape=[jax.ShapeDtypeStruct((S, D), f32), jax.ShapeDtypeStruct((1, D), f32)],
        scratch_shapes=[pltpu.VMEM((ts, D), f32)],
        compiler_params=_params(("arbitrary", "arbitrary"), vmem_mib=56),
        name="dh_norm_bwd",
    )(dproj, w, x, rstd, gain, dy)


def _dw_in(hbt, dproj):
    tk = 1024
    win = WSH + 96

    def body(a_ref, b_ref, o_ref, acc):
        j = pl.program_id(0)
        k = pl.program_id(1)

        @pl.when(k == 0)
        def _():
            acc[...] = jnp.zeros_like(acc)

        acc[...] += jnp.dot(a_ref[...], b_ref[...], preferred_element_type=f32)

        @pl.when(k == S // tk - 1)
        def _():
            acc_t = acc[...].T
            for jj in range(NDEV):
                off = (WSH * jj) % LANES

                @pl.when(j == jj)
                def _():
                    o_ref[...] = acc_t[off:off + WSH, :].astype(bf16)

    return pl.pallas_call(
        body,
        grid=(NDEV, S // tk),
        in_specs=[
            pl.BlockSpec((D, tk), lambda j, k: (0, k)),
            pl.BlockSpec((pl.Element(tk), pl.Element(win)), lambda j, k: (k * tk, (WSH * j) // LANES * LANES)),
        ],
        out_specs=pl.BlockSpec((None, WSH, D), lambda j, k: (j, 0, 0)),
        out_shape=jax.ShapeDtypeStruct((NDEV, WSH, D), bf16),
        scratch_shapes=[pltpu.VMEM((D, win), f32)],
        compiler_params=_params(("arbitrary", "arbitrary")),
        name="dw_in",
    )(hbt, dproj)


def _matmul_tokens(at, b, name):
    m, n = at.shape[0], b.shape[1]
    tn = 512
    tk = 1024

    def body(a_ref, b_ref, o_ref):
        @pl.when(pl.program_id(1) == 0)
        def _():
            o_ref[...] = jnp.zeros_like(o_ref)

        o_ref[...] += jnp.dot(a_ref[...], b_ref[...], preferred_element_type=f32)

    return pl.pallas_call(
        body,
        grid=(n // tn, S // tk),
        in_specs=[pl.BlockSpec((m, tk), lambda j, k: (0, k)), pl.BlockSpec((tk, tn), lambda j, k: (k, j))],
        out_specs=pl.BlockSpec((m, tn), lambda j, k: (0, j)),
        out_shape=jax.ShapeDtypeStruct((m, n), f32),
        compiler_params=_params(("arbitrary", "arbitrary")),
        name=name,
    )(at, b)


def _exchange(scatter, gather, name):
    arrs = list(scatter) + list(gather)
    n = len(arrs)
    ns = len(scatter)

    def body(*refs):
        ins, outs = refs[:n], refs[n:2 * n]
        send_sems, recv_sems, local_sems = refs[2 * n:]
        x, y, c = lax.axis_index("x"), lax.axis_index("y"), lax.axis_index("c")
        me = 4 * x + 2 * y + c
        local, remote = [], []
        for a in range(n):
            lc = pltpu.make_async_copy(ins[a].at[me] if a < ns else ins[a], outs[a].at[me], local_sems.at[a])
            lc.start()
            local.append(lc)
            for r in range(1, NDEV):
                px = 1 - x if r & 4 else x
                py = 1 - y if r & 2 else y
                pc = 1 - c if r & 1 else c
                cp = pltpu.make_async_remote_copy(
                    src_ref=ins[a].at[4 * px + 2 * py + pc] if a < ns else ins[a],
                    dst_ref=outs[a].at[me],
                    send_sem=send_sems.at[a, r - 1],
                    recv_sem=recv_sems.at[a, r - 1],
                    device_id=(px, py, pc),
                    device_id_type=pl.DeviceIdType.MESH,
                )
                cp.start()
                remote.append(cp)
        for cp in remote:
            cp.wait_recv()
        for cp in remote:
            cp.wait_send()
        for lc in local:
            lc.wait()

    out_shape = [jax.ShapeDtypeStruct(a.shape if i < ns else (NDEV,) + a.shape, a.dtype) for i, a in enumerate(arrs)]
    return pl.pallas_call(
        body,
        in_specs=[pl.BlockSpec(memory_space=pl.ANY)] * n,
        out_specs=[pl.BlockSpec(memory_space=pl.ANY)] * n,
        out_shape=out_shape,
        scratch_shapes=[
            pltpu.SemaphoreType.DMA((n, NDEV - 1)),
            pltpu.SemaphoreType.DMA((n, NDEV - 1)),
            pltpu.SemaphoreType.DMA((n,)),
        ],
        compiler_params=pltpu.CompilerParams(has_side_effects=True),
        name=name,
    )(*arrs)


def _gather_two_level(arrs, name):
    n = len(arrs)

    def body(*refs):
        ins, outs = refs[:n], refs[n:2 * n]
        send_sems, recv_sems, local_sems = refs[2 * n:]
        x, y, c = lax.axis_index("x"), lax.axis_index("y"), lax.axis_index("c")
        me, sibling = (x, y, c), (x, y, 1 - c)
        xn, yn, dg = (1 - x, y, c), (x, 1 - y, c), (1 - x, 1 - y, c)
        relay_origin = (jnp.bitwise_xor(x, c), jnp.bitwise_xor(y, 1 - c), c)
        relay_target = (jnp.bitwise_xor(x, 1 - c), jnp.bitwise_xor(y, c), c)

        def copy(a, k, block, to, src=None):
            slot = outs[a].at[4 * block[0] + 2 * block[1] + block[2]]
            return pltpu.make_async_remote_copy(
                src_ref=slot if src is None else src, dst_ref=slot, send_sem=send_sems.at[a, k],
                recv_sem=recv_sems.at[a, k], device_id=to, device_id_type=pl.DeviceIdType.MESH)

        def other_core(block):
            return (block[0], block[1], 1 - c)

        mine, sent = [], []
        for a in range(n):
            lc = pltpu.make_async_copy(ins[a], outs[a].at[4 * x + 2 * y + c], local_sems.at[a])
            lc.start()
            mine.append(lc)
            sent += [copy(a, 0, me, sibling, src=ins[a]), copy(a, 1, me, xn, src=ins[a]),
                     copy(a, 2, me, yn, src=ins[a])]
        for cp in sent:
            cp.start()
        later = []
        for a in range(n):
            copy(a, 1, xn, me).wait_recv()
            copy(a, 2, yn, me).wait_recv()
            later += [copy(a, 3, relay_origin, relay_target), copy(a, 4, xn, sibling), copy(a, 5, yn, sibling)]
            for cp in later[-3:]:
                cp.start()
        for a in range(n):
            copy(a, 3, dg, me).wait_recv()
            later.append(copy(a, 6, dg, sibling))
            later[-1].start()
        for a in range(n):
            copy(a, 0, sibling, me).wait_recv()
            for k, block in ((4, xn), (5, yn), (6, dg)):
                copy(a, k, other_core(block), me).wait_recv()
        for cp in sent + later:
            cp.wait_send()
        for lc in mine:
            lc.wait()

    return pl.pallas_call(
        body,
        in_specs=[pl.BlockSpec(memory_space=pl.ANY)] * n,
        out_specs=[pl.BlockSpec(memory_space=pl.ANY)] * n,
        out_shape=[jax.ShapeDtypeStruct((NDEV,) + a.shape, a.dtype) for a in arrs],
        scratch_shapes=[
            pltpu.SemaphoreType.DMA((n, NDEV - 1)),
            pltpu.SemaphoreType.DMA((n, NDEV - 1)),
            pltpu.SemaphoreType.DMA((n,)),
        ],
        compiler_params=pltpu.CompilerParams(has_side_effects=True),
        name=name,
    )(*arrs)


_HBM = pl.BlockSpec(memory_space=pltpu.HBM)
_SEM = pl.BlockSpec(memory_space=pltpu.SEMAPHORE)
_EFFECT = pltpu.SideEffectType.DATAFLOW_SIDE_EFFECTING


def _sibling_exchange(g, name):
    def body(in_ref, out_ref, send_sems, recv_sems):
        x, y, c = lax.axis_index("x"), lax.axis_index("y"), lax.axis_index("c")
        copies = []
        for q in range(4):
            cp = pltpu.make_async_remote_copy(
                src_ref=in_ref.at[2 * q + (1 - c)], dst_ref=out_ref.at[q], send_sem=send_sems.at[q],
                recv_sem=recv_sems.at[q], device_id=(x, y, 1 - c), device_id_type=pl.DeviceIdType.MESH)
            cp.start()
            copies.append(cp)
        for cp in copies:
            cp.wait_recv()
        for cp in copies:
            cp.wait_send()

    return pl.pallas_call(
        body,
        in_specs=[pl.BlockSpec(memory_space=pl.ANY)],
        out_specs=pl.BlockSpec(memory_space=pl.ANY),
        out_shape=jax.ShapeDtypeStruct((4,) + g.shape[1:], g.dtype),
        scratch_shapes=[pltpu.SemaphoreType.DMA((4,)), pltpu.SemaphoreType.DMA((4,))],
        compiler_params=pltpu.CompilerParams(has_side_effects=True),
        name=name,
    )(g)


def _row_tile(rows, limit=256):
    fits = [t for t in range(16, limit + 1, 16) if rows % t == 0]
    return fits[-1] if fits else rows


def _pair_sum(g, r, core, name):
    _, rows, cols = g.shape
    tr = _row_tile(rows)

    def body(c_ref, g_ref, r_ref, o_ref):
        del c_ref
        o_ref[...] = (g_ref[...].astype(f32) + r_ref[...].astype(f32)).astype(bf16)

    return pl.pallas_call(
        body,
        grid_spec=pltpu.PrefetchScalarGridSpec(
            num_scalar_prefetch=1,
            grid=(4, rows // tr),
            in_specs=[pl.BlockSpec((None, tr, cols), lambda q, i, c_ref: (2 * q + c_ref[0], i, 0)),
                      pl.BlockSpec((None, tr, cols), lambda q, i, c_ref: (q, i, 0))],
            out_specs=pl.BlockSpec((None, tr, cols), lambda q, i, c_ref: (q, i, 0)),
        ),
        out_shape=jax.ShapeDtypeStruct((4, rows, cols), bf16),
        compiler_params=_params(("arbitrary", "arbitrary")),
        name=name,
    )(core, g, r)


def _scatter_start(chip_arrs, all_arrs, name):
    arrs = list(chip_arrs) + list(all_arrs)
    n, nc = len(arrs), len(chip_arrs)
    lands = [lax.empty(((3 if i < nc else NDEV - 1),) + a.shape[1:], a.dtype) for i, a in enumerate(arrs)]

    def body(*refs):
        src, land = refs[:n], refs[n:2 * n]
        send_sems, recv_sems = refs[2 * n:3 * n], refs[3 * n:4 * n]
        token = refs[6 * n]
        x, y, c = lax.axis_index("x"), lax.axis_index("y"), lax.axis_index("c")
        for a in range(n):
            for r in range(1, 4 if a < nc else NDEV):
                if a < nc:
                    px, py, pc = (1 - x if r & 2 else x), (1 - y if r & 1 else y), c
                    block = 2 * px + py
                else:
                    px, py, pc = (1 - x if r & 4 else x), (1 - y if r & 2 else y), (1 - c if r & 1 else c)
                    block = 4 * px + 2 * py + pc
                pltpu.make_async_remote_copy(
                    src_ref=src[a].at[block], dst_ref=land[a].at[r - 1], send_sem=send_sems[a],
                    recv_sem=recv_sems[a], device_id=(px, py, pc), device_id_type=pl.DeviceIdType.MESH).start()
        token[...] = jnp.zeros_like(token)

    hbm = [pltpu.HBM(a.shape, a.dtype) for a in arrs + lands]
    ops = [pltpu.with_memory_space_constraint(a, pltpu.HBM) for a in arrs + lands]
    outs = pl.pallas_call(
        body,
        out_shape=tuple([pltpu.SemaphoreType.DMA(())] * (2 * n) + hbm + [jax.ShapeDtypeStruct((8, LANES), f32)]),
        in_specs=[_HBM] * (2 * n),
        out_specs=tuple([_SEM] * (2 * n) + [_HBM] * (2 * n) + [pl.BlockSpec(memory_space=pltpu.VMEM)]),
        input_output_aliases={i: 2 * n + i for i in range(2 * n)},
        compiler_params=pltpu.CompilerParams(has_side_effects=_EFFECT),
        name=name,
    )(*ops)
    return outs[:n], outs[n:2 * n], outs[2 * n:3 * n], outs[3 * n:4 * n], outs[4 * n]


def _scatter_wait(send_sems, recv_sems, srcs, lands, after, name):
    n = len(srcs)

    def body(*refs):
        land = refs[n:2 * n]
        ssem, rsem = refs[2 * n:3 * n], refs[3 * n:4 * n]
        x, y, c = lax.axis_index("x"), lax.axis_index("y"), lax.axis_index("c")
        for a in range(n):
            done = pltpu.make_async_remote_copy(
                src_ref=land[a], dst_ref=land[a], send_sem=ssem[a], recv_sem=rsem[a], device_id=(x, y, c),
                device_id_type=pl.DeviceIdType.MESH)
            done.wait_send()
            done.wait_recv()

    hbm = [pltpu.HBM(a.shape, a.dtype) for a in list(srcs) + list(lands)]
    outs = pl.pallas_call(
        body,
        out_shape=tuple(hbm),
        in_specs=[_HBM] * (2 * n) + [_SEM] * (2 * n) + [pl.BlockSpec(memory_space=pl.ANY)],
        out_specs=tuple([_HBM] * (2 * n)),
        input_output_aliases={i: i for i in range(2 * n)},
        compiler_params=pltpu.CompilerParams(has_side_effects=_EFFECT),
        name=name,
    )(*srcs, *lands, *send_sems, *recv_sems, after)
    return outs[:n], outs[n:]


def _adam_update(g, w_ref, m_ref, v_ref, g_ref, d_ref, nm_ref, nv_ref):
    mm = ADAM_B1 * m_ref[...] + (1.0 - ADAM_B1) * g
    vv = ADAM_B2 * v_ref[...] + (1.0 - ADAM_B2) * (g * g)
    m_hat = mm / (1.0 - ADAM_B1 ** ADAM_STEP)
    v_hat = vv / (1.0 - ADAM_B2 ** ADAM_STEP)
    g_ref[...] = g
    d_ref[...] = -ADAM_LR * (m_hat / (jnp.sqrt(v_hat) + ADAM_EPS) + ADAM_WD * w_ref[...])
    nm_ref[...] = mm
    nv_ref[...] = vv


def _adamw_own(w, own, own_idx, slots, m, v, name):
    r, c = w.shape[-2:]
    tr = _row_tile(r, 128)
    k = slots.shape[0]

    def body(i_ref, w_ref, o_ref, s_ref, m_ref, v_ref, g_ref, d_ref, nm_ref, nv_ref):
        del i_ref
        g = o_ref[...].astype(f32)
        for j in range(k):
            g = g + s_ref[j].astype(f32)
        _adam_update(g, w_ref, m_ref, v_ref, g_ref, d_ref, nm_ref, nv_ref)

    blk = pl.BlockSpec((None, tr, c), lambda i, ix: (0, i, 0))
    return pl.pallas_call(
        body,
        grid_spec=pltpu.PrefetchScalarGridSpec(
            num_scalar_prefetch=1,
            grid=(r // tr,),
            in_specs=[blk, pl.BlockSpec((None, tr, c), lambda i, ix: (ix[0], i, 0)),
                      pl.BlockSpec((k, tr, c), lambda i, ix: (0, i, 0)), blk, blk],
            out_specs=[blk] * 4,
        ),
        out_shape=[jax.ShapeDtypeStruct(w.shape, f32)] * 4,
        compiler_params=_params(("arbitrary",)),
        name=name,
    )(own_idx, w, own, slots, m, v)


def _adamw(w, slots, m, v, name):
    r, c = w.shape[-2:]
    tr = _row_tile(r, 128)

    def body(w_ref, s_ref, m_ref, v_ref, g_ref, d_ref, nm_ref, nv_ref):
        g = s_ref[0].astype(f32)
        for k in range(1, NDEV):
            g = g + s_ref[k].astype(f32)
        _adam_update(g, w_ref, m_ref, v_ref, g_ref, d_ref, nm_ref, nv_ref)

    if w.ndim == 3:
        blk = pl.BlockSpec((None, tr, c), lambda i: (0, i, 0))
    else:
        blk = pl.BlockSpec((tr, c), lambda i: (i, 0))
    return pl.pallas_call(
        body,
        grid=(r // tr,),
        in_specs=[blk, pl.BlockSpec((NDEV, tr, c), lambda i: (0, i, 0)), blk, blk],
        out_specs=[blk] * 4,
        out_shape=[jax.ShapeDtypeStruct(w.shape, f32)] * 4,
        compiler_params=_params(("arbitrary",)),
        name=name,
    )(w, slots, m, v)


def _local_step(x, tgt, norm_gain, w_t, qn_a, kn_a, qn_b, kn_b, sink_a, rel_bias, w_a, w_b, b_merge, w_o,
                on_weight_grads=None):
    two = lambda t: jnp.concatenate([t, t], axis=-1).reshape(1, LANES)
    ones = jnp.ones((1, LANES), f32)
    gains = jnp.stack([
        jnp.stack([two(qn_a), two(kn_a), ones]),
        jnp.stack([two(qn_b), two(kn_b), ones]),
        jnp.stack([two(qn_b), two(kn_b), ones]),
        jnp.stack([two(qn_b), two(kn_b), ones]),
    ])
    buckets = [jnp.asarray(_bucket_np(blk, d)) for blk, d, _ in GROUPS]
    bias = [_bias_expand(rel_bias, buckets[k], GROUPS[k][2], "bias_expand_%d" % k) for k in range(4)]

    hb, hbt, rstd = _rms(x, norm_gain)
    proj = _inproj(hb, w_t)
    gl = _prep(proj, gains)
    o_a, l_a = _attn_fwd(gl, bias[0], sink_a.reshape(8), 0, 128, 1, "attn_fwd_a")
    fwd_b = [_attn_fwd(gl, bias[k], None, k, GROUPS[k][0], GROUPS[k][1], "attn_fwd_b%d" % k) for k in (1, 2, 3)]
    sink_b = jnp.repeat(sink_a.reshape(8), HD).reshape(1, 512)

    (dy, dyb, dproj, do_a, dd_a, do_b0, do_b1, do_b2, dd_b0, dd_b1, dd_b2, ya, yb, mg, dbr_a, dbr_b, loss, dbm,
     dsk) = _tail(x, tgt, o_a, l_a, [f[0] for f in fwd_b], [f[1] for f in fwd_b], proj, b_merge, w_a, w_b, w_o, sink_b)

    dqkv_a, dbk_a = _attn_bwd(gl, bias[0], buckets[0], do_a, l_a, dd_a, 0, 128, 1, "attn_bwd_a")
    dproj, dg_a = _post_a(dqkv_a, proj, gains, dproj)
    dbk_b, dg_b = [], []
    for k, do_k, dd_k in ((1, do_b0, dd_b0), (2, do_b1, dd_b1), (3, do_b2, dd_b2)):
        dqkv, dbk = _attn_bwd(gl, bias[k], buckets[k], do_k, fwd_b[k - 1][1], dd_k, k, GROUPS[k][0], GROUPS[k][1],
                              "attn_bwd_b%d" % k)
        dproj, dg = _post_b(k, dqkv, proj, gains, dproj)
        dbk_b.append(dbk)
        dg_b.append(dg)
    dg_b = jnp.stack(dg_b)

    dw_in = _dw_in(hbt, dproj)
    dw_o = _matmul_tokens(mg, dyb, "dw_out")
    dw_a = _matmul_tokens(ya, dbr_a, "dw_branch_a")
    dw_b = _matmul_tokens(yb, dbr_b, "dw_branch_b")
    token = jnp.zeros((), f32) if on_weight_grads is None else on_weight_grads(
        dict(w_in=dw_in, w_branch_a=dw_a, w_branch_b=dw_b, b_merge=dbm, w_out=dw_o))
    grad_x, d_norm_gain = _dh_norm_bwd(dproj, w_t, x, rstd, norm_gain + token, dy)

    fold = lambda t: t[..., :HD] + t[..., HD:]
    d_qn_a = fold(dg_a[0, 0])
    d_kn_a = fold(dg_a[1, 0])
    d_qn_b = fold(dg_b[:, 0, 0].sum(axis=0))
    d_kn_b = fold(dg_b[:, 1, 0].sum(axis=0))
    d_sink = dsk.reshape(8, HD)[:, 0]
    red = jnp.stack([dbk_a] + dbk_b)
    d_rel = red[:, :, 0, :32].reshape(32, 32).T
    return dict(loss=loss, grad_x=grad_x, norm_gain=d_norm_gain, w_in=dw_in, q_norm_a=d_qn_a, k_norm_a=d_kn_a,
                q_norm_b=d_qn_b, k_norm_b=d_kn_b, sink_a=d_sink, rel_bias=d_rel, w_branch_a=dw_a, w_branch_b=dw_b,
                b_merge=dbm, w_out=dw_o)


SMALL = (("norm_gain", D), ("q_norm_a", HD), ("k_norm_a", HD), ("q_norm_b", HD), ("k_norm_b", HD), ("sink_a", 8),
         ("rel_bias", 1024))
SMALL_PAD = 2432


SMALL_USED = sum(sz for _, sz in SMALL)


def _pack_small(parts, loss=None):
    tail = jnp.zeros((SMALL_PAD - SMALL_USED,), f32)
    if loss is not None:
        tail = tail.at[0].set(loss.reshape(()))
    return jnp.concatenate([parts[n].reshape(-1) for n, _ in SMALL] + [tail]).reshape(1, SMALL_PAD)


def _unpack_small(flat, shapes):
    out, off = {}, 0
    for n, sz in SMALL:
        out[n] = flat[0, off:off + sz].reshape(shapes[n])
        off += sz
    return out


def kernel(x, norm_gain, w_in, q_norm_a, k_norm_a, q_norm_b, k_norm_b, sink_a, rel_bias, w_branch_a, w_branch_b, b_merge, w_out, loss_target, m_norm_gain, m_w_in, m_q_norm_a, m_k_norm_a, m_q_norm_b, m_k_norm_b, m_sink_a, m_rel_bias, m_w_branch_a, m_w_branch_b, m_b_merge, m_w_out, v_norm_gain, v_w_in, v_q_norm_a, v_k_norm_a, v_q_norm_b, v_k_norm_b, v_sink_a, v_rel_bias, v_w_branch_a, v_w_branch_b, v_b_merge, v_w_out):
    csh = D // NDEV
    w_in_t, m_w_in_t, v_w_in_t = (jnp.swapaxes(t, 1, 2) for t in (w_in, m_w_in, v_w_in))
    g_in, g_a, g_b, g_o, g_bm = _gather_two_level(
        [w_in_t[0].astype(bf16), w_branch_a[0].astype(bf16), w_branch_b[0].astype(bf16), w_out[0].astype(bf16),
         b_merge[0]], "gather_weights")
    w_t_full = g_in.reshape(NW, D)
    w_a_full = g_a.transpose(1, 0, 2).reshape(512, D)
    w_b_full = g_b.transpose(1, 0, 2).reshape(512, D)
    w_o_full = g_o.reshape(D, D)
    bm_full = g_bm.transpose(1, 0, 2).reshape(2, D)

    pending = {}
    core = lax.axis_index("c").astype(jnp.int32).reshape(1)
    chip = (2 * lax.axis_index("x") + lax.axis_index("y")).astype(jnp.int32).reshape(1)
    me = (2 * chip + core).astype(jnp.int32)

    def start_exchange(gw):
        from_sibling = _sibling_exchange(gw["w_in"], "grad_sibling_exchange")
        chip_sums = _pair_sum(gw["w_in"], from_sibling, core, "grad_pair_sum")
        blocks = [gw["w_branch_a"].reshape(512, NDEV, csh).transpose(1, 0, 2).astype(bf16),
                  gw["w_branch_b"].reshape(512, NDEV, csh).transpose(1, 0, 2).astype(bf16),
                  gw["w_out"].reshape(NDEV, csh, D).astype(bf16),
                  gw["b_merge"].reshape(2, NDEV, csh).transpose(1, 0, 2)]
        pending["started"] = _scatter_start([chip_sums], blocks, "scatter_grads_start")
        return pending["started"][4][0, 0]

    loc = _local_step(x[0], loss_target[0], norm_gain, w_t_full, q_norm_a, k_norm_a, q_norm_b, k_norm_b, sink_a,
                      rel_bias, w_a_full, w_b_full, bm_full, w_o_full, on_weight_grads=start_exchange)

    small_shapes = dict(norm_gain=(1, D), q_norm_a=(1, HD), k_norm_a=(1, HD), q_norm_b=(1, HD), k_norm_b=(1, HD),
                        sink_a=(1, 8), rel_bias=(32, 32))
    (r_small,) = _exchange([], [_pack_small(loc, loc["loss"])], "gather_small_grads")
    send_sems, recv_sems, srcs, lands, _ = pending["started"]
    (s_in, s_a, s_b, s_o, s_bm), (r_in, r_a, r_b, r_o, r_bm) = _scatter_wait(
        send_sems, recv_sems, srcs, lands, r_small, "scatter_grads_wait")

    given = dict(norm_gain=norm_gain, q_norm_a=q_norm_a, k_norm_a=k_norm_a, q_norm_b=q_norm_b, k_norm_b=k_norm_b,
                 sink_a=sink_a, rel_bias=rel_bias)
    m_small = dict(norm_gain=m_norm_gain, q_norm_a=m_q_norm_a, k_norm_a=m_k_norm_a, q_norm_b=m_q_norm_b,
                   k_norm_b=m_k_norm_b, sink_a=m_sink_a, rel_bias=m_rel_bias)
    v_small = dict(norm_gain=v_norm_gain, q_norm_a=v_q_norm_a, k_norm_a=v_k_norm_a, q_norm_b=v_q_norm_b,
                   k_norm_b=v_k_norm_b, sink_a=v_sink_a, rel_bias=v_rel_bias)
    res = {
        "small": _adamw(_pack_small(given), r_small, _pack_small(m_small), _pack_small(v_small), "adamw_small"),
        "w_in": [jnp.swapaxes(t, 1, 2) for t in
                 _adamw_own(w_in_t, s_in, chip, r_in, m_w_in_t, v_w_in_t, "adamw_w_in")],
        "w_branch_a": _adamw_own(w_branch_a, s_a, me, r_a, m_w_branch_a, v_w_branch_a, "adamw_w_branch_a"),
        "w_branch_b": _adamw_own(w_branch_b, s_b, me, r_b, m_w_branch_b, v_w_branch_b, "adamw_w_branch_b"),
        "b_merge": _adamw_own(b_merge, s_bm, me, r_bm, m_b_merge, v_b_merge, "adamw_b_merge"),
        "w_out": _adamw_own(w_out, s_o, me, r_o, m_w_out, v_w_out, "adamw_w_out"),
    }
    order = ["norm_gain", "w_in", "q_norm_a", "k_norm_a", "q_norm_b", "k_norm_b", "sink_a", "rel_bias", "w_branch_a",
             "w_branch_b", "b_merge", "w_out"]
    outs = []
    for k in range(4):
        small = _unpack_small(res["small"][k], small_shapes)
        for n in order:
            outs.append(small[n] if n in small else res[n][k])
    loss = res["small"][0][0, SMALL_USED]
    return (loss, loc["grad_x"][None], *outs)
```

```python
import math

import numpy as np
import jax
import jax.numpy as jnp
from jax import lax
from jax.experimental import pallas as pl
from jax.experimental.pallas import tpu as pltpu

f32 = jnp.float32
bf16 = jnp.bfloat16

S = 4096
D = 1024
NA = 5376
NT = 3072
NW = NA + NT
WSH = NW // 8
HD = 64
LANES = 128
EPS = 1e-6
NEG = -1e30
SCALE = HD ** -0.5
TQ = 128
PAD = 128
SP = S + 2 * PAD
NDEV = 8
GROUPS = ((128, 1, 0), (64, 1, 8), (64, 4, 16), (64, 16, 24))
CHUNK = 256
PCHUNK = 128
RC = 64
TN = 768

ADAM_LR, ADAM_B1, ADAM_B2, ADAM_EPS, ADAM_WD, ADAM_STEP = 0.001, 0.9, 0.999, 1e-08, 0.01, 10

MIB = 1024 * 1024
NT_DIMS = (((1,), (1,)), ((), ()))
TN_DIMS = (((0,), (0,)), ((), ()))


def _params(sem=None, vmem_mib=48):
    return pltpu.CompilerParams(dimension_semantics=sem, vmem_limit_bytes=vmem_mib * MIB)


def _lo():
    return lax.broadcasted_iota(jnp.int32, (1, LANES), 1) < HD


def _head_ones():
    r = lax.broadcasted_iota(jnp.int32, (LANES, LANES), 0) // HD
    c = lax.broadcasted_iota(jnp.int32, (LANES, LANES), 1) // HD
    return jnp.where(r == c, 1.0, 0.0).astype(bf16)


def _half_sums(x, ones):
    hi = x.astype(bf16)
    mid = (x - hi.astype(f32)).astype(bf16)
    return (jnp.dot(hi, ones, preferred_element_type=f32) + jnp.dot(mid, ones, preferred_element_type=f32))


def _seg_sum(x, ones):
    outs = [_half_sums(x[:, b * LANES:(b + 1) * LANES], ones) for b in range(x.shape[1] // LANES)]
    return outs[0] if len(outs) == 1 else jnp.concatenate(outs, axis=1)


def _bucket_np(blk, stride):
    w = TQ + 2 * blk
    rel = np.arange(w)[None, :] - blk - np.arange(TQ)[:, None]
    band = np.abs(rel) <= blk
    r = rel * stride
    n = np.abs(r)
    nf = np.maximum(n, 8).astype(np.float32)
    large = 8 + (np.log(nf / np.float32(8)) / np.float32(math.log(128.0)) * np.float32(8)).astype(np.int32)
    large = np.minimum(large, 15)
    b = (r > 0).astype(np.int32) * 16 + np.where(n < 8, n, large)
    return np.where(band, b, -1).astype(np.int32)


def _rms(x, gain):
    ts = 512

    def body(x_ref, g_ref, h_ref, ht_ref, r_ref):
        xv = x_ref[...]
        r = lax.rsqrt(jnp.mean(xv * xv, axis=-1, keepdims=True) + EPS)
        h = (xv * r) * g_ref[...]
        h_ref[...] = h.astype(bf16)
        ht_ref[...] = h.T.astype(bf16)
        r_ref[...] = r

    return pl.pallas_call(
        body,
        grid=(S // ts,),
        in_specs=[pl.BlockSpec((ts, D), lambda i: (i, 0)), pl.BlockSpec((1, D), lambda i: (0, 0))],
        out_specs=[pl.BlockSpec((ts, D), lambda i: (i, 0)), pl.BlockSpec((D, ts), lambda i: (0, i)),
                   pl.BlockSpec((ts, 1), lambda i: (i, 0))],
        out_shape=[jax.ShapeDtypeStruct((S, D), bf16), jax.ShapeDtypeStruct((D, S), bf16),
                   jax.ShapeDtypeStruct((S, 1), f32)],
        compiler_params=_params(("arbitrary",)),
        name="rms",
    )(x, gain)


def _inproj_half(hb, w_t, half, proj, name):
    ts = 1024
    tn = NW // 6
    per = NW // 2 // tn

    def body(h_idx, h_ref, w_ref, *rest):
        del h_idx
        rest[-1][...] = lax.dot_general(h_ref[...], w_ref[...], NT_DIMS, preferred_element_type=f32)

    in_specs = [pl.BlockSpec((ts, D), lambda i, n, hf: (i, 0)),
                pl.BlockSpec((tn, D), lambda i, n, hf: (hf[0] * per + n, 0))]
    args = [half, hb, w_t]
    aliases = {}
    if proj is not None:
        in_specs.append(pl.BlockSpec(memory_space=pl.ANY))
        args.append(proj)
        aliases = {3: 0}
    return pl.pallas_call(
        body,
        grid_spec=pltpu.PrefetchScalarGridSpec(
            num_scalar_prefetch=1,
            grid=(S // ts, per),
            in_specs=in_specs,
            out_specs=pl.BlockSpec((ts, tn), lambda i, n, hf: (i, hf[0] * per + n)),
        ),
        out_shape=jax.ShapeDtypeStruct((S, NW), f32),
        input_output_aliases=aliases,
        compiler_params=_params(("arbitrary", "arbitrary")),
        name=name,
    )(*args)


def _bias_expand(table, bucket, c0, name):
    tq, w = bucket.shape
    blk = (w - tq) // 2

    def body(tab_ref, bk_ref, o_ref):
        h = pl.program_id(0)
        bk = bk_ref[...]

        def step(b, acc):
            return jnp.where(bk == b, tab_ref[b, c0 + h], acc)

        inner = lax.fori_loop(0, 32, step, jnp.full((tq, w), NEG, f32))
        col = lax.broadcasted_iota(jnp.int32, (1, w), 1)
        o_ref[0] = jnp.where(col < blk, NEG, inner)
        o_ref[1] = inner
        o_ref[2] = jnp.where(col >= tq + blk, NEG, inner)

    return pl.pallas_call(
        body,
        grid=(8,),
        in_specs=[pl.BlockSpec(memory_space=pltpu.SMEM), pl.BlockSpec((tq, w), lambda h: (0, 0))],
        out_specs=pl.BlockSpec((3, None, tq, w), lambda h: (0, h, 0, 0)),
        out_shape=jax.ShapeDtypeStruct((3, 8, tq, w), f32),
        compiler_params=_params(("arbitrary",)),
        name=name,
    )(table, bucket)


def _tile_kind(t, seq):
    m0 = jnp.bitwise_and(t * TQ, seq - 1)
    return jnp.where(m0 == 0, 0, jnp.where(m0 == seq - TQ, 2, 1))


def _col_block(g, j):
    kind = j // 4
    hp = j % 4
    a = jnp.where(kind == 0, hp, 3 + kind)
    b = 6 + 12 * kind + 4 * (g - 1) + hp
    return jnp.where(g == 0, a, b)


def _prep(proj_a, gains):
    def body(p_ref, g_ref, o_ref):
        g = pl.program_id(0)
        j = pl.program_id(1)
        kind = j // 4
        lo = _lo()
        ones = _head_ones()
        half = jnp.where(lo, 0, 1)
        take = (kind == 0) | (half == (j % 4) // 2)
        gain = g_ref[...]
        o_ref[0:PAD, :] = jnp.zeros((PAD, LANES), bf16)
        o_ref[PAD + S:SP, :] = jnp.zeros((PAD, LANES), bf16)

        def norm_store(xv, dst, dup):
            if dup:
                xv = jnp.where(take, xv, pltpu.roll(xv, HD, 1))
            r = lax.rsqrt(_half_sums(xv * xv, ones) * (1.0 / HD) + EPS)
            r = jnp.where(kind == 2, 1.0, r)
            yv = (xv * r) * gain
            yv = jnp.where(kind == 0, yv * SCALE, yv)
            o_ref[PAD + dst:PAD + dst + CHUNK, :] = yv.astype(bf16)

        for gi, (_, d, _) in enumerate(GROUPS):
            @pl.when(g == gi)
            def _():
                seq = S // d
                for c in range(d):
                    for i in range(seq // CHUNK):
                        if d == 1:
                            xv = p_ref[i * CHUNK:(i + 1) * CHUNK, :]
                        else:
                            xv = p_ref[pl.ds(c + i * CHUNK * d, CHUNK, stride=d), :]
                        norm_store(xv, c * seq + i * CHUNK, gi == 0)

    return pl.pallas_call(
        body,
        grid=(4, 12),
        in_specs=[
            pl.BlockSpec((S, LANES), lambda g, j: (0, _col_block(g, j))),
            pl.BlockSpec((None, None, 1, LANES), lambda g, j: (g, j // 4, 0, 0)),
        ],
        out_specs=pl.BlockSpec((None, None, SP, LANES), lambda g, j: (g, j, 0, 0)),
        out_shape=jax.ShapeDtypeStruct((4, 12, SP, LANES), bf16),
        compiler_params=_params(("arbitrary", "arbitrary")),
        name="prep",
    )(proj_a, gains)


def _token_rows(t, r0, n, d):
    if d == 1:
        return pl.ds(pl.multiple_of(t * TQ, TQ) + r0, n)
    per = S // d // TQ
    return pl.ds(((t % per) * TQ + r0) * d + t // per, n, stride=d)


def _stack_heads(t, lo):
    z = jnp.zeros_like(t)
    return jnp.concatenate([jnp.where(lo, t, z), jnp.where(lo, z, t)], axis=0)


def _unstack_heads(t2, lo):
    return jnp.where(lo, t2[:TQ], t2[TQ:])


def _attn_fwd(gl, bias, sink, g, blk, d, name):
    w = TQ + 2 * blk
    seq = S // d
    use_sink = sink is not None

    def body(*refs):
        if use_sink:
            sink_ref, q_ref, k_ref, v_ref, b_ref, o_ref, l_ref, s0, s1, p0, p1, lse_scr = refs
        else:
            q_ref, k_ref, v_ref, b_ref, o_ref, l_ref, s0, s1, p0, p1, lse_scr = refs
        hp = pl.program_id(0)
        lo = _lo()
        s_bufs, p_bufs = (s0, s1), (p0, p1)

        def scores(p, slot):
            for u in range(2):
                f0 = pl.multiple_of((2 * p + u) * TQ, TQ)
                q2 = _stack_heads(q_ref[pl.ds(PAD + f0, TQ), :], lo)
                kw = k_ref[pl.ds(PAD - blk + f0, w), :]
                s_bufs[slot][u] = lax.dot_general(q2, kw, NT_DIMS, preferred_element_type=f32)

        def softmax(p, slot):
            for u in range(2):
                t = 2 * p + u
                kind = _tile_kind(t, seq)
                for h in range(2):
                    for r in range(TQ // RC):
                        rows = slice(h * TQ + r * RC, h * TQ + (r + 1) * RC)
                        logit = s_bufs[slot][u, rows, :] + b_ref[kind, h, r * RC:(r + 1) * RC, :]
                        m = jnp.max(logit, axis=1, keepdims=True)
                        e = jnp.exp(logit - m)
                        lse = m + jnp.log(jnp.sum(e, axis=1, keepdims=True))
                        if use_sink:
                            sk = sink_ref[2 * hp + h]
                            mx = jnp.maximum(lse, sk)
                            lse = mx + jnp.log(jnp.exp(lse - mx) + jnp.exp(sk - mx))
                        p_bufs[slot][u, rows, :] = (e * jnp.exp(m - lse)).astype(bf16)
                        lse_scr[u, rows, :] = jnp.broadcast_to(lse, (RC, LANES))
                l_ref[_token_rows(t, 0, TQ, d), :] = jnp.where(lo, lse_scr[u, 0:TQ, :], lse_scr[u, TQ:2 * TQ, :])

        def values(p, slot):
            for u in range(2):
                t = 2 * p + u
                vw = v_ref[pl.ds(PAD - blk + pl.multiple_of(t * TQ, TQ), w), :]
                o2 = jnp.dot(p_bufs[slot][u], vw, preferred_element_type=f32)
                o_ref[_token_rows(t, 0, TQ, d), :] = _unstack_heads(o2, lo)

        npair = S // TQ // 2
        scores(0, 0)
        scores(1, 1)
        softmax(0, 0)

        def steady(k, carry):
            p = 2 * k + 2
            scores(p, 0)
            softmax(p - 1, 1)
            values(p - 2, 0)
            scores(p + 1, 1)
            softmax(p, 0)
            values(p - 1, 1)
            return carry

        lax.fori_loop(0, (npair - 2) // 2, steady, 0)
        softmax(npair - 1, 1)
        values(npair - 2, 0)
        values(npair - 1, 1)

    in_specs = [
        pl.BlockSpec((None, None, SP, LANES), lambda hp: (g, hp, 0, 0)),
        pl.BlockSpec((None, None, SP, LANES), lambda hp: (g, 4 + hp, 0, 0)),
        pl.BlockSpec((None, None, SP, LANES), lambda hp: (g, 8 + hp, 0, 0)),
        pl.BlockSpec((3, 2, TQ, w), lambda hp: (0, hp, 0, 0)),
    ]
    args = [gl, gl, gl, bias]
    if use_sink:
        in_specs = [pl.BlockSpec(memory_space=pltpu.SMEM)] + in_specs
        args = [sink] + args
    out = pl.BlockSpec((S, LANES), lambda hp: (0, hp))
    return pl.pallas_call(
        body,
        grid=(4,),
        in_specs=in_specs,
        out_specs=[out, out],
        out_shape=[jax.ShapeDtypeStruct((S, 4 * LANES), f32)] * 2,
        scratch_shapes=[pltpu.VMEM((2, 2 * TQ, w), f32), pltpu.VMEM((2, 2 * TQ, w), f32),
                        pltpu.VMEM((2, 2 * TQ, w), bf16), pltpu.VMEM((2, 2 * TQ, w), bf16),
                        pltpu.VMEM((2, 2 * TQ, LANES), f32)],
        compiler_params=_params(("arbitrary",)),
        name=name,
    )(*args)


def _attn_bwd(gl, bias, bucket, do, lse, dd, g, blk, d, name):
    w = TQ + 2 * blk
    seq = S // d

    def body(q_ref, k_ref, v_ref, b_ref, bk_ref, do_ref, l_ref, d_ref, dqkv_ref, dbk_ref,
             db_acc, s0, s1, dp0, dp1, pb0, pb1, ds0, ds1):
        lo = _lo()
        hi = jnp.logical_not(lo)
        dqkv_ref[1] = jnp.zeros((SP, LANES), f32)
        dqkv_ref[2] = jnp.zeros((SP, LANES), f32)
        db_acc[...] = jnp.zeros((2 * TQ, w), f32)
        s_bufs, dp_bufs, pb_bufs, ds_bufs = (s0, s1), (dp0, dp1), (pb0, pb1), (ds0, ds1)

        def stacked(t):
            f0 = pl.multiple_of(t * TQ, TQ)
            q2 = _stack_heads(q_ref[pl.ds(PAD + f0, TQ), :], lo)
            do2 = _stack_heads(do_ref[_token_rows(t, 0, TQ, d), :].astype(bf16), lo)
            return f0, q2, do2

        def scores(p, slot):
            for u in range(2):
                f0, q2, do2 = stacked(2 * p + u)
                win = pl.ds(PAD - blk + f0, w)
                s_bufs[slot][u] = lax.dot_general(q2, k_ref[win, :], NT_DIMS, preferred_element_type=f32)
                dp_bufs[slot][u] = lax.dot_general(do2, v_ref[win, :], NT_DIMS, preferred_element_type=f32)

        def grads(p, slot):
            for u in range(2):
                t = 2 * p + u
                kind = _tile_kind(t, seq)
                for h in range(2):
                    msk = lo if h == 0 else hi
                    for r in range(TQ // RC):
                        rows = slice(h * TQ + r * RC, h * TQ + (r + 1) * RC)
                        src = _token_rows(t, r * RC, RC, d)
                        lh = jnp.max(jnp.where(msk, l_ref[src, :], -jnp.inf), axis=1, keepdims=True)
                        dh = jnp.max(jnp.where(msk, d_ref[src, :], -jnp.inf), axis=1, keepdims=True)
                        logit = s_bufs[slot][u, rows, :] + b_ref[kind, h, r * RC:(r + 1) * RC, :]
                        pr = jnp.exp(logit - lh)
                        ds = pr * (dp_bufs[slot][u, rows, :] - dh)
                        db_acc[rows, :] += ds
                        pb_bufs[slot][u, rows, :] = pr.astype(bf16)
                        ds_bufs[slot][u, rows, :] = ds.astype(bf16)

        def accumulate(p, slot):
            for u in range(2):
                f0, q2, do2 = stacked(2 * p + u)
                win = pl.ds(PAD - blk + f0, w)
                dsb = ds_bufs[slot][u]
                dq2 = jnp.dot(dsb, k_ref[win, :], preferred_element_type=f32)
                dqkv_ref[0, pl.ds(PAD + f0, TQ), :] = _unstack_heads(dq2, lo)
                dqkv_ref[1, win, :] += lax.dot_general(dsb, q2, TN_DIMS, preferred_element_type=f32)
                dqkv_ref[2, win, :] += lax.dot_general(pb_bufs[slot][u], do2, TN_DIMS, preferred_element_type=f32)

        npair = S // TQ // 2
        scores(0, 0)
        scores(1, 1)
        grads(0, 0)

        def steady(k, carry):
            p = 2 * k + 2
            scores(p, 0)
            grads(p - 1, 1)
            accumulate(p - 2, 0)
            scores(p + 1, 1)
            grads(p, 0)
            accumulate(p - 1, 1)
            return carry

        lax.fori_loop(0, (npair - 2) // 2, steady, 0)
        grads(npair - 1, 1)
        accumulate(npair - 2, 0)
        accumulate(npair - 1, 1)

        bk = bk_ref[...]
        lane = lax.broadcasted_iota(jnp.int32, (8, LANES), 1)
        for h in range(2):
            db = db_acc[h * TQ:(h + 1) * TQ, :]
            acc = jnp.zeros((8, LANES), f32)
            for b in range(32):
                part = jnp.where(bk == b, db, 0.0).reshape(TQ // 8, 8, w).sum(axis=0)
                tot = jnp.sum(jnp.sum(part, axis=1, keepdims=True), axis=0, keepdims=True)
                acc = jnp.where(lane == b, tot, acc)
            dbk_ref[h] = acc

    def gcol(off):
        return pl.BlockSpec((None, None, SP, LANES), lambda hp: (g, off + hp, 0, 0))

    row = pl.BlockSpec((S, LANES), lambda hp: (0, hp))
    return pl.pallas_call(
        body,
        grid=(4,),
        in_specs=[gcol(0), gcol(4), gcol(8), pl.BlockSpec((3, 2, TQ, w), lambda hp: (0, hp, 0, 0)),
                  pl.BlockSpec((TQ, w), lambda hp: (0, 0)), row, row, row],
        out_specs=[pl.BlockSpec((3, None, SP, LANES), lambda hp: (0, hp, 0, 0)),
                   pl.BlockSpec((2, 8, LANES), lambda hp: (hp, 0, 0))],
        out_shape=[
            jax.ShapeDtypeStruct((3, 4, SP, LANES), f32),
            jax.ShapeDtypeStruct((8, 8, LANES), f32),
        ],
        scratch_shapes=([pltpu.VMEM((2 * TQ, w), f32)] + [pltpu.VMEM((2, 2 * TQ, w), f32)] * 4
                        + [pltpu.VMEM((2, 2 * TQ, w), bf16)] * 4),
        compiler_params=_params(("arbitrary",), vmem_mib=56),
        name=name,
    )(gl, gl, gl, bias, bucket, do, lse, dd)


def _sigmoid(z):
    return 1.0 / (1.0 + jnp.exp(-z))


def _tail(x, tgt, o_a, l_a, o_b, l_b, proj, bm, w_a, w_b, w_o, sink_b):
    ts = 256

    def body(x_ref, t_ref, oa_ref, la_ref, ob0_ref, ob1_ref, ob2_ref, lb0_ref, lb1_ref, lb2_ref,
             ga_ref, gb_ref, m0_ref, m1_ref, bm_ref, wa_ref, wb_ref, wo_ref, sk_ref,
             dy_ref, dyb_ref, dt_ref, doa_ref, dda_ref, dob0_ref, dob1_ref, dob2_ref, ddb0_ref, ddb1_ref, ddb2_ref,
             ya_ref, yb_ref, mg_ref, dbra_ref, dbrb_ref, loss_ref, dbm_ref, dsk_ref):
        i = pl.program_id(0)

        @pl.when(i == 0)
        def _():
            loss_ref[...] = jnp.zeros_like(loss_ref)
            dbm_ref[...] = jnp.zeros_like(dbm_ref)
            dsk_ref[...] = jnp.zeros_like(dsk_ref)

        ga = ga_ref[...]
        sa = _sigmoid(ga)
        silu_a = ga * sa
        oa = oa_ref[...]
        ya = oa * silu_a
        gb = gb_ref[...]
        sb = _sigmoid(gb)
        silu_b = gb * sb
        ob = [ob0_ref[...], ob1_ref[...], ob2_ref[...]]
        lb = [lb0_ref[...], lb1_ref[...], lb2_ref[...]]
        mx = jnp.maximum(jnp.maximum(lb[0], lb[1]), lb[2])
        ex = [jnp.exp(v - mx) for v in lb]
        den = ex[0] + ex[1] + ex[2]
        alpha = [e / den for e in ex]
        ybc = alpha[0] * ob[0] + alpha[1] * ob[1] + alpha[2] * ob[2]
        yb = ybc * silu_b
        yab = ya.astype(bf16)
        ybb = yb.astype(bf16)
        br_a = jnp.dot(yab, wa_ref[...], preferred_element_type=f32)
        br_b = jnp.dot(ybb, wb_ref[...], preferred_element_type=f32)
        g0 = _sigmoid(m0_ref[...] + bm_ref[0:1, :])
        g1 = _sigmoid(m1_ref[...] + bm_ref[1:2, :])
        merged = g0 * br_a + g1 * br_b
        mgb = merged.astype(bf16)
        y = x_ref[...] + jnp.dot(mgb, wo_ref[...], preferred_element_type=f32)
        err = y - t_ref[...]
        part = jnp.sum(jnp.sum(err * err, axis=1, keepdims=True), axis=0, keepdims=True)
        loss_ref[...] += part * (0.5 / D)
        dy = err * (1.0 / D)
        dyb = dy.astype(bf16)
        dmerged = lax.dot_general(dyb, wo_ref[...], NT_DIMS, preferred_element_type=f32)
        dbr_a = (dmerged * g0).astype(bf16)
        dbr_b = (dmerged * g1).astype(bf16)
        dm0 = dmerged * br_a * (g0 * (1.0 - g0))
        dm1 = dmerged * br_b * (g1 * (1.0 - g1))
        dbm_ref[0:1, :] += jnp.sum(dm0, axis=0, keepdims=True)
        dbm_ref[1:2, :] += jnp.sum(dm1, axis=0, keepdims=True)
        dya = lax.dot_general(dbr_a, wa_ref[...], NT_DIMS, preferred_element_type=f32)
        dyb2 = lax.dot_general(dbr_b, wb_ref[...], NT_DIMS, preferred_element_type=f32)
        do_a = dya * silu_a
        dga = dya * oa * (sa * (1.0 + ga * (1.0 - sa)))
        ones = _head_ones()
        delta_a = _seg_sum(do_a * oa, ones)
        dsk_ref[...] -= jnp.sum(delta_a * jnp.exp(sk_ref[...] - la_ref[...]), axis=0, keepdims=True)
        dybc = dyb2 * silu_b
        dgb = dyb2 * ybc * (sb * (1.0 + gb * (1.0 - sb)))
        dbar = _seg_sum(dybc * ybc, ones)
        dy_ref[...] = dy
        dyb_ref[...] = dyb
        dt_ref[:, 0:512] = dga.astype(bf16)
        dt_ref[:, 512:1024] = dgb.astype(bf16)
        dt_ref[:, 1024:2048] = dm0.astype(bf16)
        dt_ref[:, 2048:3072] = dm1.astype(bf16)
        doa_ref[...] = do_a.astype(bf16)
        dda_ref[...] = delta_a
        for k, (dob_ref, ddb_ref) in enumerate(((dob0_ref, ddb0_ref), (dob1_ref, ddb1_ref), (dob2_ref, ddb2_ref))):
            dob_ref[...] = alpha[k] * dybc
            ddb_ref[...] = alpha[k] * dbar
        ya_ref[...] = ya.T.astype(bf16)
        yb_ref[...] = yb.T.astype(bf16)
        mg_ref[...] = merged.T.astype(bf16)
        dbra_ref[...] = dbr_a
        dbrb_ref[...] = dbr_b

    def rows(n, blk=0):
        return pl.BlockSpec((ts, n), lambda i: (i, blk))

    def whole(r, c):
        return pl.BlockSpec((r, c), lambda i: (0, 0))

    def cols(n):
        return pl.BlockSpec((n, ts), lambda i: (0, i))

    def gate_cols(n, col):
        return pl.BlockSpec((pl.Element(ts), pl.Element(n)), lambda i: (i * ts, NA + col))

    outs = [
        ((S, D), f32, rows(D)), ((S, D), bf16, rows(D)), ((S, NW), bf16, gate_cols(NT, 0)),
        ((S, 512), bf16, rows(512)), ((S, 512), f32, rows(512)),
        ((S, 512), f32, rows(512)), ((S, 512), f32, rows(512)), ((S, 512), f32, rows(512)),
        ((S, 512), f32, rows(512)), ((S, 512), f32, rows(512)), ((S, 512), f32, rows(512)),
        ((512, S), bf16, cols(512)), ((512, S), bf16, cols(512)), ((D, S), bf16, cols(D)),
        ((S, D), bf16, rows(D)), ((S, D), bf16, rows(D)),
        ((1, 1), f32, whole(1, 1)), ((2, D), f32, whole(2, D)), ((1, 512), f32, whole(1, 512)),
    ]
    return pl.pallas_call(
        body,
        grid=(S // ts,),
        in_specs=[
            rows(D), rows(D), rows(512), rows(512), rows(512), rows(512), rows(512), rows(512), rows(512), rows(512),
            gate_cols(512, 0), gate_cols(512, 512), gate_cols(D, 1024), gate_cols(D, 2048), whole(2, D),
            whole(512, D), whole(512, D), whole(D, D), whole(1, 512),
        ],
        out_specs=[o[2] for o in outs],
        out_shape=[jax.ShapeDtypeStruct(o[0], o[1]) for o in outs],
        compiler_params=_params(("arbitrary",), vmem_mib=60),
        name="tail",
    )(x, tgt, o_a, l_a, *o_b, *l_b, proj, proj, proj, proj, bm, w_a, w_b, w_o, sink_b)


def _norm_bwd(xv, dyv, gain, kind, ones):
    r = lax.rsqrt(_half_sums(xv * xv, ones) * (1.0 / HD) + EPS)
    yv = xv * r
    up = jnp.where(kind == 0, dyv * SCALE, dyv)
    u = up * gain
    dxv = r * (u - yv * (_half_sums(u * yv, ones) * (1.0 / HD)))
    dxv = jnp.where(kind == 2, dyv, dxv)
    dg = jnp.where(kind == 2, 0.0, jnp.sum(up * yv, axis=0, keepdims=True))
    return dxv, dg


def _post_b(g, dqkv, proj_a, gains, dproj):
    d = GROUPS[g][1]
    seq = S // d

    def body(d_ref, p_ref, g_ref, alias_ref, o_ref, dg_ref, nat):
        del alias_ref
        j = pl.program_id(0)
        kind = j // 4
        gain = g_ref[...]
        ones = _head_ones()

        @pl.when(j % 4 == 0)
        def _():
            dg_ref[...] = jnp.zeros_like(dg_ref)

        for c in range(d):
            for i in range(seq // PCHUNK):
                src = c * seq + i * PCHUNK
                if d == 1:
                    idx = slice(src, src + PCHUNK)
                else:
                    idx = pl.ds(c + i * PCHUNK * d, PCHUNK, stride=d)
                dxv, dg = _norm_bwd(p_ref[idx, :], d_ref[PAD + src:PAD + src + PCHUNK, :], gain, kind, ones)
                nat[idx, :] = dxv
                dg_ref[...] += dg

        for i in range(S // CHUNK):
            o_ref[i * CHUNK:(i + 1) * CHUNK, :] = nat[i * CHUNK:(i + 1) * CHUNK, :].astype(bf16)

    return pl.pallas_call(
        body,
        grid=(12,),
        in_specs=[
            pl.BlockSpec((None, None, SP, LANES), lambda j: (j // 4, j % 4, 0, 0)),
            pl.BlockSpec((S, LANES), lambda j: (0, _col_block(g, j))),
            pl.BlockSpec((None, None, 1, LANES), lambda j: (g, j // 4, 0, 0)),
            pl.BlockSpec(memory_space=pl.ANY),
        ],
        out_specs=[
            pl.BlockSpec((S, LANES), lambda j: (0, _col_block(g, j))),
            pl.BlockSpec((None, 1, LANES), lambda j: (j // 4, 0, 0)),
        ],
        out_shape=[jax.ShapeDtypeStruct((S, NW), bf16), jax.ShapeDtypeStruct((3, 1, LANES), f32)],
        scratch_shapes=[pltpu.VMEM((S, LANES), f32)],
        input_output_aliases={3: 0},
        compiler_params=_params(("arbitrary",)),
        name="post_b%d" % g,
    )(dqkv, proj_a, gains, dproj)


def _post_a(dqkv, proj_a, gains, dproj):
    def body(q_ref, e_ref, p_ref, g_ref, alias_ref, o_ref, dg_ref):
        del alias_ref
        j = pl.program_id(0)
        kind = jnp.maximum(j - 3, 0)
        gain = g_ref[...]
        lo = _lo()
        ones = _head_ones()

        @pl.when((j == 0) | (j >= 4))
        def _():
            dg_ref[...] = jnp.zeros_like(dg_ref)

        for i in range(S // PCHUNK):
            r0 = i * PCHUNK
            rows = slice(PAD + r0, PAD + r0 + PCHUNK)
            t0 = e_ref[0, rows, :] + e_ref[1, rows, :]
            t1 = e_ref[2, rows, :] + e_ref[3, rows, :]
            folded = jnp.where(lo, t0 + pltpu.roll(t0, HD, 1), t1 + pltpu.roll(t1, HD, 1))
            dyv = jnp.where(kind == 0, q_ref[rows, :], folded)
            dxv, dg = _norm_bwd(p_ref[r0:r0 + PCHUNK, :], dyv, gain, kind, ones)
            o_ref[r0:r0 + PCHUNK, :] = dxv.astype(bf16)
            dg_ref[...] += dg

    return pl.pallas_call(
        body,
        grid=(6,),
        in_specs=[
            pl.BlockSpec((None, None, SP, LANES), lambda j: (0, jnp.minimum(j, 3), 0, 0)),
            pl.BlockSpec((None, 4, SP, LANES), lambda j: (jnp.clip(j - 3, 1, 2), 0, 0, 0)),
            pl.BlockSpec((S, LANES), lambda j: (0, j)),
            pl.BlockSpec((None, None, 1, LANES), lambda j: (0, jnp.maximum(j - 3, 0), 0, 0)),
            pl.BlockSpec(memory_space=pl.ANY),
        ],
        out_specs=[
            pl.BlockSpec((S, LANES), lambda j: (0, j)),
            pl.BlockSpec((None, 1, LANES), lambda j: (jnp.maximum(j - 3, 0), 0, 0)),
        ],
        out_shape=[jax.ShapeDtypeStruct((S, NW), bf16), jax.ShapeDtypeStruct((3, 1, LANES), f32)],
        input_output_aliases={4: 0},
        compiler_params=_params(("arbitrary",)),
        name="post_a",
    )(dqkv, dqkv, proj_a, gains, dproj)


def _dh_norm_bwd(dproj, w, x, rstd, gain, dy):
    ts = 1024
    tk = NW // 6
    nk = NW // tk

    def body(d_ref, w_ref, x_ref, r_ref, g_ref, dy_ref, gx_ref, dgn_ref, acc):
        i = pl.program_id(0)
        k = pl.program_id(1)

        @pl.when((i == 0) & (k == 0))
        def _():
            dgn_ref[...] = jnp.zeros_like(dgn_ref)

        @pl.when(k == 0)
        def _():
            acc[...] = jnp.zeros_like(acc)

        acc[...] += jnp.dot(d_ref[...], w_ref[...], preferred_element_type=f32)

        @pl.when(k == nk - 1)
        def _():
            dh = acc[...]
            xh = x_ref[...] * r_ref[...]
            u = dh * g_ref[...]
            dx = r_ref[...] * (u - xh * jnp.mean(u * xh, axis=-1, keepdims=True))
            gx_ref[...] = dy_ref[...] + dx
            dgn_ref[...] += jnp.sum(dh * xh, axis=0, keepdims=True)

    return pl.pallas_call(
        body,
        grid=(S // ts, nk),
        in_specs=[
            pl.BlockSpec((ts, tk), lambda i, k: (i, k)),
            pl.BlockSpec((tk, D), lambda i, k: (k, 0)),
            pl.BlockSpec((ts, D), lambda i, k: (i, 0)),
            pl.BlockSpec((ts, 1), lambda i, k: (i, 0)),
            pl.BlockSpec((1, D), lambda i, k: (0, 0)),
            pl.BlockSpec((ts, D), lambda i, k: (i, 0)),
        ],
        out_specs=[pl.BlockSpec((ts, D), lambda i, k: (i, 0)), pl.BlockSpec((1, D), lambda i, k: (0, 0))],
        out_shape=[jax.ShapeDtypeStruct((S, D), f32), jax.ShapeDtypeStruct((1, D), f32)],
        scratch_shapes=[pltpu.VMEM((ts, D), f32)],
        compiler_params=_params(("arbitrary", "arbitrary"), vmem_mib=56),
        name="dh_norm_bwd",
    )(dproj, w, x, rstd, gain, dy)


def _dw_in(hbt, dproj):
    tk = 1024
    win = WSH + 96

    def body(a_ref, b_ref, o_ref, acc):
        j = pl.program_id(0)
        k = pl.program_id(1)

        @pl.when(k == 0)
        def _():
            acc[...] = jnp.zeros_like(acc)

        acc[...] += jnp.dot(a_ref[...], b_ref[...], preferred_element_type=f32)

        @pl.when(k == S // tk - 1)
        def _():
            acc_t = acc[...].T
            for jj in range(NDEV):
                off = (WSH * jj) % LANES

                @pl.when(j == jj)
                def _():
                    o_ref[...] = acc_t[off:off + WSH, :].astype(bf16)

    return pl.pallas_call(
        body,
        grid=(NDEV, S // tk),
        in_specs=[
            pl.BlockSpec((D, tk), lambda j, k: (0, k)),
            pl.BlockSpec((pl.Element(tk), pl.Element(win)), lambda j, k: (k * tk, (WSH * j) // LANES * LANES)),
        ],
        out_specs=pl.BlockSpec((None, WSH, D), lambda j, k: (j, 0, 0)),
        out_shape=jax.ShapeDtypeStruct((NDEV, WSH, D), bf16),
        scratch_shapes=[pltpu.VMEM((D, win), f32)],
        compiler_params=_params(("arbitrary", "arbitrary")),
        name="dw_in",
    )(hbt, dproj)


def _matmul_tokens(at, b, name):
    m, n = at.shape[0], b.shape[1]
    tn = 512
    tk = 1024

    def body(a_ref, b_ref, o_ref):
        @pl.when(pl.program_id(1) == 0)
        def _():
            o_ref[...] = jnp.zeros_like(o_ref)

        o_ref[...] += jnp.dot(a_ref[...], b_ref[...], preferred_element_type=f32)

    return pl.pallas_call(
        body,
        grid=(n // tn, S // tk),
        in_specs=[pl.BlockSpec((m, tk), lambda j, k: (0, k)), pl.BlockSpec((tk, tn), lambda j, k: (k, j))],
        out_specs=pl.BlockSpec((m, tn), lambda j, k: (0, j)),
        out_shape=jax.ShapeDtypeStruct((m, n), f32),
        compiler_params=_params(("arbitrary", "arbitrary")),
        name=name,
    )(at, b)


def _exchange(scatter, gather, name):
    arrs = list(scatter) + list(gather)
    n = len(arrs)
    ns = len(scatter)

    def body(*refs):
        ins, outs = refs[:n], refs[n:2 * n]
        send_sems, recv_sems, local_sems = refs[2 * n:]
        x, y, c = lax.axis_index("x"), lax.axis_index("y"), lax.axis_index("c")
        me = 4 * x + 2 * y + c
        local, remote = [], []
        for a in range(n):
            lc = pltpu.make_async_copy(ins[a].at[me] if a < ns else ins[a], outs[a].at[me], local_sems.at[a])
            lc.start()
            local.append(lc)
            for r in range(1, NDEV):
                px = 1 - x if r & 4 else x
                py = 1 - y if r & 2 else y
                pc = 1 - c if r & 1 else c
                cp = pltpu.make_async_remote_copy(
                    src_ref=ins[a].at[4 * px + 2 * py + pc] if a < ns else ins[a],
                    dst_ref=outs[a].at[me],
                    send_sem=send_sems.at[a, r - 1],
                    recv_sem=recv_sems.at[a, r - 1],
                    device_id=(px, py, pc),
                    device_id_type=pl.DeviceIdType.MESH,
                )
                cp.start()
                remote.append(cp)
        for cp in remote:
            cp.wait_recv()
        for cp in remote:
            cp.wait_send()
        for lc in local:
            lc.wait()

    out_shape = [jax.ShapeDtypeStruct(a.shape if i < ns else (NDEV,) + a.shape, a.dtype) for i, a in enumerate(arrs)]
    return pl.pallas_call(
        body,
        in_specs=[pl.BlockSpec(memory_space=pl.ANY)] * n,
        out_specs=[pl.BlockSpec(memory_space=pl.ANY)] * n,
        out_shape=out_shape,
        scratch_shapes=[
            pltpu.SemaphoreType.DMA((n, NDEV - 1)),
            pltpu.SemaphoreType.DMA((n, NDEV - 1)),
            pltpu.SemaphoreType.DMA((n,)),
        ],
        compiler_params=pltpu.CompilerParams(has_side_effects=True),
        name=name,
    )(*arrs)


def _gather_two_level(arrs, name):
    n = len(arrs)

    def body(*refs):
        ins, outs = refs[:n], refs[n:2 * n]
        send_sems, recv_sems, local_sems = refs[2 * n:]
        x, y, c = lax.axis_index("x"), lax.axis_index("y"), lax.axis_index("c")
        me, sibling = (x, y, c), (x, y, 1 - c)
        xn, yn, dg = (1 - x, y, c), (x, 1 - y, c), (1 - x, 1 - y, c)
        relay_origin = (jnp.bitwise_xor(x, c), jnp.bitwise_xor(y, 1 - c), c)
        relay_target = (jnp.bitwise_xor(x, 1 - c), jnp.bitwise_xor(y, c), c)

        def copy(a, k, block, to, src=None):
            slot = outs[a].at[4 * block[0] + 2 * block[1] + block[2]]
            return pltpu.make_async_remote_copy(
                src_ref=slot if src is None else src, dst_ref=slot, send_sem=send_sems.at[a, k],
                recv_sem=recv_sems.at[a, k], device_id=to, device_id_type=pl.DeviceIdType.MESH)

        def other_core(block):
            return (block[0], block[1], 1 - c)

        mine, sent = [], []
        for a in range(n):
            lc = pltpu.make_async_copy(ins[a], outs[a].at[4 * x + 2 * y + c], local_sems.at[a])
            lc.start()
            mine.append(lc)
            sent += [copy(a, 0, me, sibling, src=ins[a]), copy(a, 1, me, xn, src=ins[a]),
                     copy(a, 2, me, yn, src=ins[a])]
        for cp in sent:
            cp.start()
        later = []
        for a in range(n):
            copy(a, 1, xn, me).wait_recv()
            copy(a, 2, yn, me).wait_recv()
            later += [copy(a, 3, relay_origin, relay_target), copy(a, 4, xn, sibling), copy(a, 5, yn, sibling)]
            for cp in later[-3:]:
                cp.start()
        for a in range(n):
            copy(a, 3, dg, me).wait_recv()
            later.append(copy(a, 6, dg, sibling))
            later[-1].start()
        for a in range(n):
            copy(a, 0, sibling, me).wait_recv()
            for k, block in ((4, xn), (5, yn), (6, dg)):
                copy(a, k, other_core(block), me).wait_recv()
        for cp in sent + later:
            cp.wait_send()
        for lc in mine:
            lc.wait()

    return pl.pallas_call(
        body,
        in_specs=[pl.BlockSpec(memory_space=pl.ANY)] * n,
        out_specs=[pl.BlockSpec(memory_space=pl.ANY)] * n,
        out_shape=[jax.ShapeDtypeStruct((NDEV,) + a.shape, a.dtype) for a in arrs],
        scratch_shapes=[
            pltpu.SemaphoreType.DMA((n, NDEV - 1)),
            pltpu.SemaphoreType.DMA((n, NDEV - 1)),
            pltpu.SemaphoreType.DMA((n,)),
        ],
        compiler_params=pltpu.CompilerParams(has_side_effects=True),
        name=name,
    )(*arrs)


_HBM = pl.BlockSpec(memory_space=pltpu.HBM)
_SEM = pl.BlockSpec(memory_space=pltpu.SEMAPHORE)
_EFFECT = pltpu.SideEffectType.DATAFLOW_SIDE_EFFECTING


def _comm_step(name, body_fn, lands, srcs=(), wait_sems=(), n_new=0, after=()):
    n, ns, nw, na = len(lands), len(srcs), len(wait_sems), len(after)

    def body(*refs):
        src, land = refs[:ns], refs[ns:ns + n]
        waits = refs[ns + n:ns + n + nw]
        new = refs[ns + n + nw + na:ns + n + nw + na + n_new]
        body_fn(src, land, waits, new)

    hbm = [pltpu.HBM(a.shape, a.dtype) for a in lands]
    ops = [pltpu.with_memory_space_constraint(a, pltpu.HBM) for a in list(srcs) + list(lands)]
    outs = pl.pallas_call(
        body,
        out_shape=tuple([pltpu.SemaphoreType.DMA(())] * n_new + hbm),
        in_specs=[_HBM] * (ns + n) + [_SEM] * nw + [pl.BlockSpec(memory_space=pl.ANY)] * na,
        out_specs=tuple([_SEM] * n_new + [_HBM] * n),
        input_output_aliases={ns + i: n_new + i for i in range(n)},
        compiler_params=pltpu.CompilerParams(has_side_effects=_EFFECT),
        name=name,
    )(*ops, *wait_sems, *after)
    return list(outs[:n_new]), list(outs[n_new:])


class _GatheredWeights:
    def __init__(self, shards):
        self.n = n = len(shards)
        x, y, c = lax.axis_index("x"), lax.axis_index("y"), lax.axis_index("c")
        self.x = x
        me = 4 * x + 2 * y + c
        lands = [lax.dynamic_update_slice(lax.empty((NDEV,) + s.shape, s.dtype), s[None], (me,) + (0,) * s.ndim)
                 for s in shards]

        def start_own(src, land, waits, new):
            p = self._peers()
            for a in range(n):
                for k, to in ((0, p["sibling"]), (1, p["xn"]), (2, p["yn"])):
                    self._copy(land[a], new, a, k, 3, p["me"], to).start()

        self.sems, self.lands = {}, None
        new, self.lands = _comm_step("gather_start", start_own, lands, n_new=6 * n)
        self._keep(new, (0, 1, 2))

    @staticmethod
    def _peers():
        x, y, c = lax.axis_index("x"), lax.axis_index("y"), lax.axis_index("c")
        return dict(
            me=(x, y, c), sibling=(x, y, 1 - c), xn=(1 - x, y, c), yn=(x, 1 - y, c), dg=(1 - x, 1 - y, c),
            relay_origin=(jnp.bitwise_xor(x, c), jnp.bitwise_xor(y, 1 - c), c),
            relay_target=(jnp.bitwise_xor(x, 1 - c), jnp.bitwise_xor(y, c), c))

    def _keep(self, new, ks):
        half = len(new) // 2
        i = 0
        for a in range(self.n):
            for k in ks:
                self.sems[a, k] = (new[i], new[half + i])
                i += 1

    @staticmethod
    def _copy(land, sem_refs, a, k, nk, block, to, src=None, ks=None):
        ks = tuple(range(nk)) if ks is None else ks
        half = len(sem_refs) // 2
        i = a * len(ks) + ks.index(k)
        slot = land.at[4 * block[0] + 2 * block[1] + block[2]]
        return pltpu.make_async_remote_copy(
            src_ref=slot if src is None else src, dst_ref=slot, send_sem=sem_refs[i], recv_sem=sem_refs[half + i],
            device_id=to, device_id_type=pl.DeviceIdType.MESH)

    def _sem_list(self, ks):
        return ([self.sems[a, k][0] for a in range(self.n) for k in ks]
                + [self.sems[a, k][1] for a in range(self.n) for k in ks])

    def first_half(self, after):
        n = self.n

        def relay(src, land, waits, new):
            p = self._peers()
            for a in range(n):
                self._copy(land[a], waits, a, 1, 0, p["xn"], p["me"], ks=(1, 2)).wait_recv()
                self._copy(land[a], waits, a, 2, 0, p["yn"], p["me"], ks=(1, 2)).wait_recv()
                self._copy(land[a], new, a, 3, 0, p["relay_origin"], p["relay_target"], ks=(3, 4, 5)).start()
                self._copy(land[a], new, a, 4, 0, p["xn"], p["sibling"], ks=(3, 4, 5)).start()
                self._copy(land[a], new, a, 5, 0, p["yn"], p["sibling"], ks=(3, 4, 5)).start()

        new, self.lands = _comm_step("gather_relay", relay, self.lands, wait_sems=self._sem_list((1, 2)),
                                     n_new=6 * n, after=after)
        self._keep(new, (3, 4, 5))

        def from_sibling(src, land, waits, new):
            p = self._peers()
            other = lambda b: (b[0], b[1], 1 - b[2])
            for a in range(n):
                self._copy(land[a], waits, a, 0, 0, other(p["me"]), p["me"], ks=(0, 4, 5)).wait_recv()
                self._copy(land[a], waits, a, 4, 0, other(p["xn"]), p["me"], ks=(0, 4, 5)).wait_recv()
                self._copy(land[a], waits, a, 5, 0, other(p["yn"]), p["me"], ks=(0, 4, 5)).wait_recv()

        _, self.lands = _comm_step("gather_wait_sibling", from_sibling, self.lands,
                                   wait_sems=self._sem_list((0, 4, 5)))
        return self.lands[0].reshape(NW, D), self.x.astype(jnp.int32).reshape(1)

    def second_half(self, after):
        n = self.n

        def forward_diagonal(src, land, waits, new):
            p = self._peers()
            for a in range(n):
                self._copy(land[a], waits, a, 3, 0, p["dg"], p["me"], ks=(3,)).wait_recv()
                self._copy(land[a], new, a, 6, 0, p["dg"], p["sibling"], ks=(6,)).start()

        new, self.lands = _comm_step("gather_forward_diagonal", forward_diagonal, self.lands,
                                     wait_sems=self._sem_list((3,)), n_new=2 * n, after=after)
        self._keep(new, (6,))

        def finish(src, land, waits, new):
            p = self._peers()
            ks = tuple(range(7))
            for a in range(n):
                self._copy(land[a], waits, a, 6, 0, (p["dg"][0], p["dg"][1], 1 - p["dg"][2]), p["me"], ks=ks).wait_recv()
                for k in ks:
                    self._copy(land[a], waits, a, k, 0, p["me"], p["me"], ks=ks).wait_send()

        _, self.lands = _comm_step("gather_finish", finish, self.lands, wait_sems=self._sem_list(tuple(range(7))))
        return self.lands[0].reshape(NW, D), (1 - self.x).astype(jnp.int32).reshape(1)

    def rest(self):
        g_a, g_b, g_o, g_bm = self.lands[1:]
        return (g_a.transpose(1, 0, 2).reshape(512, D), g_b.transpose(1, 0, 2).reshape(512, D),
                g_bm.transpose(1, 0, 2).reshape(2, D), g_o.reshape(D, D))


def _sibling_exchange(g, name):
    def body(in_ref, out_ref, send_sems, recv_sems):
        x, y, c = lax.axis_index("x"), lax.axis_index("y"), lax.axis_index("c")
        copies = []
        for q in range(4):
            cp = pltpu.make_async_remote_copy(
                src_ref=in_ref.at[2 * q + (1 - c)], dst_ref=out_ref.at[q], send_sem=send_sems.at[q],
                recv_sem=recv_sems.at[q], device_id=(x, y, 1 - c), device_id_type=pl.DeviceIdType.MESH)
            cp.start()
            copies.append(cp)
        for cp in copies:
            cp.wait_recv()
        for cp in copies:
            cp.wait_send()

    return pl.pallas_call(
        body,
        in_specs=[pl.BlockSpec(memory_space=pl.ANY)],
        out_specs=pl.BlockSpec(memory_space=pl.ANY),
        out_shape=jax.ShapeDtypeStruct((4,) + g.shape[1:], g.dtype),
        scratch_shapes=[pltpu.SemaphoreType.DMA((4,)), pltpu.SemaphoreType.DMA((4,))],
        compiler_params=pltpu.CompilerParams(has_side_effects=True),
        name=name,
    )(g)


def _row_tile(rows, limit=256):
    fits = [t for t in range(16, limit + 1, 16) if rows % t == 0]
    return fits[-1] if fits else rows


def _pair_sum(g, r, core, name):
    _, rows, cols = g.shape
    tr = _row_tile(rows)

    def body(c_ref, g_ref, r_ref, o_ref):
        del c_ref
        o_ref[...] = (g_ref[...].astype(f32) + r_ref[...].astype(f32)).astype(bf16)

    return pl.pallas_call(
        body,
        grid_spec=pltpu.PrefetchScalarGridSpec(
            num_scalar_prefetch=1,
            grid=(4, rows // tr),
            in_specs=[pl.BlockSpec((None, tr, cols), lambda q, i, c_ref: (2 * q + c_ref[0], i, 0)),
                      pl.BlockSpec((None, tr, cols), lambda q, i, c_ref: (q, i, 0))],
            out_specs=pl.BlockSpec((None, tr, cols), lambda q, i, c_ref: (q, i, 0)),
        ),
        out_shape=jax.ShapeDtypeStruct((4, rows, cols), bf16),
        compiler_params=_params(("arbitrary", "arbitrary")),
        name=name,
    )(core, g, r)


def _scatter_start(chip_arrs, all_arrs, name):
    arrs = list(chip_arrs) + list(all_arrs)
    n, nc = len(arrs), len(chip_arrs)
    lands = [lax.empty(((3 if i < nc else NDEV - 1),) + a.shape[1:], a.dtype) for i, a in enumerate(arrs)]

    def body(*refs):
        src, land = refs[:n], refs[n:2 * n]
        send_sems, recv_sems = refs[2 * n:3 * n], refs[3 * n:4 * n]
        token = refs[6 * n]
        x, y, c = lax.axis_index("x"), lax.axis_index("y"), lax.axis_index("c")
        for a in range(n):
            for r in range(1, 4 if a < nc else NDEV):
                if a < nc:
                    px, py, pc = (1 - x if r & 2 else x), (1 - y if r & 1 else y), c
                    block = 2 * px + py
                else:
                    px, py, pc = (1 - x if r & 4 else x), (1 - y if r & 2 else y), (1 - c if r & 1 else c)
                    block = 4 * px + 2 * py + pc
                pltpu.make_async_remote_copy(
                    src_ref=src[a].at[block], dst_ref=land[a].at[r - 1], send_sem=send_sems[a],
                    recv_sem=recv_sems[a], device_id=(px, py, pc), device_id_type=pl.DeviceIdType.MESH).start()
        token[...] = jnp.zeros_like(token)

    hbm = [pltpu.HBM(a.shape, a.dtype) for a in arrs + lands]
    ops = [pltpu.with_memory_space_constraint(a, pltpu.HBM) for a in arrs + lands]
    outs = pl.pallas_call(
        body,
        out_shape=tuple([pltpu.SemaphoreType.DMA(())] * (2 * n) + hbm + [jax.ShapeDtypeStruct((8, LANES), f32)]),
        in_specs=[_HBM] * (2 * n),
        out_specs=tuple([_SEM] * (2 * n) + [_HBM] * (2 * n) + [pl.BlockSpec(memory_space=pltpu.VMEM)]),
        input_output_aliases={i: 2 * n + i for i in range(2 * n)},
        compiler_params=pltpu.CompilerParams(has_side_effects=_EFFECT),
        name=name,
    )(*ops)
    return outs[:n], outs[n:2 * n], outs[2 * n:3 * n], outs[3 * n:4 * n], outs[4 * n]


def _scatter_wait(send_sems, recv_sems, srcs, lands, after, name):
    n = len(srcs)

    def body(*refs):
        land = refs[n:2 * n]
        ssem, rsem = refs[2 * n:3 * n], refs[3 * n:4 * n]
        x, y, c = lax.axis_index("x"), lax.axis_index("y"), lax.axis_index("c")
        for a in range(n):
            done = pltpu.make_async_remote_copy(
                src_ref=land[a], dst_ref=land[a], send_sem=ssem[a], recv_sem=rsem[a], device_id=(x, y, c),
                device_id_type=pl.DeviceIdType.MESH)
            done.wait_send()
            done.wait_recv()

    hbm = [pltpu.HBM(a.shape, a.dtype) for a in list(srcs) + list(lands)]
    outs = pl.pallas_call(
        body,
        out_shape=tuple(hbm),
        in_specs=[_HBM] * (2 * n) + [_SEM] * (2 * n) + [pl.BlockSpec(memory_space=pl.ANY)],
        out_specs=tuple([_HBM] * (2 * n)),
        input_output_aliases={i: i for i in range(2 * n)},
        compiler_params=pltpu.CompilerParams(has_side_effects=_EFFECT),
        name=name,
    )(*srcs, *lands, *send_sems, *recv_sems, after)
    return outs[:n], outs[n:]


def _adam_update(g, w_ref, m_ref, v_ref, g_ref, d_ref, nm_ref, nv_ref):
    mm = ADAM_B1 * m_ref[...] + (1.0 - ADAM_B1) * g
    vv = ADAM_B2 * v_ref[...] + (1.0 - ADAM_B2) * (g * g)
    m_hat = mm / (1.0 - ADAM_B1 ** ADAM_STEP)
    v_hat = vv / (1.0 - ADAM_B2 ** ADAM_STEP)
    g_ref[...] = g
    d_ref[...] = -ADAM_LR * (m_hat / (jnp.sqrt(v_hat) + ADAM_EPS) + ADAM_WD * w_ref[...])
    nm_ref[...] = mm
    nv_ref[...] = vv


def _adamw_own(w, own, own_idx, slots, m, v, name):
    r, c = w.shape[-2:]
    tr = _row_tile(r, 128)
    k = slots.shape[0]

    def body(i_ref, w_ref, o_ref, s_ref, m_ref, v_ref, g_ref, d_ref, nm_ref, nv_ref):
        del i_ref
        g = o_ref[...].astype(f32)
        for j in range(k):
            g = g + s_ref[j].astype(f32)
        _adam_update(g, w_ref, m_ref, v_ref, g_ref, d_ref, nm_ref, nv_ref)

    blk = pl.BlockSpec((None, tr, c), lambda i, ix: (0, i, 0))
    return pl.pallas_call(
        body,
        grid_spec=pltpu.PrefetchScalarGridSpec(
            num_scalar_prefetch=1,
            grid=(r // tr,),
            in_specs=[blk, pl.BlockSpec((None, tr, c), lambda i, ix: (ix[0], i, 0)),
                      pl.BlockSpec((k, tr, c), lambda i, ix: (0, i, 0)), blk, blk],
            out_specs=[blk] * 4,
        ),
        out_shape=[jax.ShapeDtypeStruct(w.shape, f32)] * 4,
        compiler_params=_params(("arbitrary",)),
        name=name,
    )(own_idx, w, own, slots, m, v)


def _adamw(w, slots, m, v, name):
    r, c = w.shape[-2:]
    tr = _row_tile(r, 128)

    def body(w_ref, s_ref, m_ref, v_ref, g_ref, d_ref, nm_ref, nv_ref):
        g = s_ref[0].astype(f32)
        for k in range(1, NDEV):
            g = g + s_ref[k].astype(f32)
        _adam_update(g, w_ref, m_ref, v_ref, g_ref, d_ref, nm_ref, nv_ref)

    if w.ndim == 3:
        blk = pl.BlockSpec((None, tr, c), lambda i: (0, i, 0))
    else:
        blk = pl.BlockSpec((tr, c), lambda i: (i, 0))
    return pl.pallas_call(
        body,
        grid=(r // tr,),
        in_specs=[blk, pl.BlockSpec((NDEV, tr, c), lambda i: (0, i, 0)), blk, blk],
        out_specs=[blk] * 4,
        out_shape=[jax.ShapeDtypeStruct(w.shape, f32)] * 4,
        compiler_params=_params(("arbitrary",)),
        name=name,
    )(w, slots, m, v)


class _Weights:
    def __init__(self, w_t, w_a, w_b, b_merge, w_o):
        self._w_t, self._rest = w_t, (w_a, w_b, b_merge, w_o)

    def first_half(self, after):
        del after
        return self._w_t, jnp.zeros((1,), jnp.int32)

    def second_half(self, after):
        del after
        return self._w_t, jnp.ones((1,), jnp.int32)

    def rest(self):
        return self._rest


def _local_step(x, tgt, norm_gain, weights, qn_a, kn_a, qn_b, kn_b, sink_a, rel_bias, on_weight_grads=None):
    two = lambda t: jnp.concatenate([t, t], axis=-1).reshape(1, LANES)
    ones = jnp.ones((1, LANES), f32)
    gains = jnp.stack([
        jnp.stack([two(qn_a), two(kn_a), ones]),
        jnp.stack([two(qn_b), two(kn_b), ones]),
        jnp.stack([two(qn_b), two(kn_b), ones]),
        jnp.stack([two(qn_b), two(kn_b), ones]),
    ])
    buckets = [jnp.asarray(_bucket_np(blk, d)) for blk, d, _ in GROUPS]
    bias = [_bias_expand(rel_bias, buckets[k], GROUPS[k][2], "bias_expand_%d" % k) for k in range(4)]

    hb, hbt, rstd = _rms(x, norm_gain)
    w_t, half = weights.first_half([hb] + bias)
    proj = _inproj_half(hb, w_t, half, None, "inproj_1")
    w_t, half = weights.second_half([proj])
    proj = _inproj_half(hb, w_t, half, proj, "inproj_2")
    w_a, w_b, b_merge, w_o = weights.rest()
    gl = _prep(proj, gains)
    o_a, l_a = _attn_fwd(gl, bias[0], sink_a.reshape(8), 0, 128, 1, "attn_fwd_a")
    fwd_b = [_attn_fwd(gl, bias[k], None, k, GROUPS[k][0], GROUPS[k][1], "attn_fwd_b%d" % k) for k in (1, 2, 3)]
    sink_b = jnp.repeat(sink_a.reshape(8), HD).reshape(1, 512)

    (dy, dyb, dproj, do_a, dd_a, do_b0, do_b1, do_b2, dd_b0, dd_b1, dd_b2, ya, yb, mg, dbr_a, dbr_b, loss, dbm,
     dsk) = _tail(x, tgt, o_a, l_a, [f[0] for f in fwd_b], [f[1] for f in fwd_b], proj, b_merge, w_a, w_b, w_o, sink_b)

    dqkv_a, dbk_a = _attn_bwd(gl, bias[0], buckets[0], do_a, l_a, dd_a, 0, 128, 1, "attn_bwd_a")
    dproj, dg_a = _post_a(dqkv_a, proj, gains, dproj)
    dbk_b, dg_b = [], []
    for k, do_k, dd_k in ((1, do_b0, dd_b0), (2, do_b1, dd_b1), (3, do_b2, dd_b2)):
        dqkv, dbk = _attn_bwd(gl, bias[k], buckets[k], do_k, fwd_b[k - 1][1], dd_k, k, GROUPS[k][0], GROUPS[k][1],
                              "attn_bwd_b%d" % k)
        dproj, dg = _post_b(k, dqkv, proj, gains, dproj)
        dbk_b.append(dbk)
        dg_b.append(dg)
    dg_b = jnp.stack(dg_b)

    dw_in = _dw_in(hbt, dproj)
    dw_o = _matmul_tokens(mg, dyb, "dw_out")
    dw_a = _matmul_tokens(ya, dbr_a, "dw_branch_a")
    dw_b = _matmul_tokens(yb, dbr_b, "dw_branch_b")
    token = jnp.zeros((), f32) if on_weight_grads is None else on_weight_grads(
        dict(w_in=dw_in, w_branch_a=dw_a, w_branch_b=dw_b, b_merge=dbm, w_out=dw_o))
    grad_x, d_norm_gain = _dh_norm_bwd(dproj, w_t, x, rstd, norm_gain + token, dy)

    fold = lambda t: t[..., :HD] + t[..., HD:]
    d_qn_a = fold(dg_a[0, 0])
    d_kn_a = fold(dg_a[1, 0])
    d_qn_b = fold(dg_b[:, 0, 0].sum(axis=0))
    d_kn_b = fold(dg_b[:, 1, 0].sum(axis=0))
    d_sink = dsk.reshape(8, HD)[:, 0]
    red = jnp.stack([dbk_a] + dbk_b)
    d_rel = red[:, :, 0, :32].reshape(32, 32).T
    return dict(loss=loss, grad_x=grad_x, norm_gain=d_norm_gain, w_in=dw_in, q_norm_a=d_qn_a, k_norm_a=d_kn_a,
                q_norm_b=d_qn_b, k_norm_b=d_kn_b, sink_a=d_sink, rel_bias=d_rel, w_branch_a=dw_a, w_branch_b=dw_b,
                b_merge=dbm, w_out=dw_o)


SMALL = (("norm_gain", D), ("q_norm_a", HD), ("k_norm_a", HD), ("q_norm_b", HD), ("k_norm_b", HD), ("sink_a", 8),
         ("rel_bias", 1024))
SMALL_PAD = 2432


SMALL_USED = sum(sz for _, sz in SMALL)


def _pack_small(parts, loss=None):
    tail = jnp.zeros((SMALL_PAD - SMALL_USED,), f32)
    if loss is not None:
        tail = tail.at[0].set(loss.reshape(()))
    return jnp.concatenate([parts[n].reshape(-1) for n, _ in SMALL] + [tail]).reshape(1, SMALL_PAD)


def _unpack_small(flat, shapes):
    out, off = {}, 0
    for n, sz in SMALL:
        out[n] = flat[0, off:off + sz].reshape(shapes[n])
        off += sz
    return out


def kernel(x, norm_gain, w_in, q_norm_a, k_norm_a, q_norm_b, k_norm_b, sink_a, rel_bias, w_branch_a, w_branch_b, b_merge, w_out, loss_target, m_norm_gain, m_w_in, m_q_norm_a, m_k_norm_a, m_q_norm_b, m_k_norm_b, m_sink_a, m_rel_bias, m_w_branch_a, m_w_branch_b, m_b_merge, m_w_out, v_norm_gain, v_w_in, v_q_norm_a, v_k_norm_a, v_q_norm_b, v_k_norm_b, v_sink_a, v_rel_bias, v_w_branch_a, v_w_branch_b, v_b_merge, v_w_out):
    csh = D // NDEV
    w_in_t, m_w_in_t, v_w_in_t = (jnp.swapaxes(t, 1, 2) for t in (w_in, m_w_in, v_w_in))
    weights = _GatheredWeights([w_in_t[0].astype(bf16), w_branch_a[0].astype(bf16), w_branch_b[0].astype(bf16),
                                w_out[0].astype(bf16), b_merge[0]])

    pending = {}
    core = lax.axis_index("c").astype(jnp.int32).reshape(1)
    chip = (2 * lax.axis_index("x") + lax.axis_index("y")).astype(jnp.int32).reshape(1)
    me = (2 * chip + core).astype(jnp.int32)

    def start_exchange(gw):
        from_sibling = _sibling_exchange(gw["w_in"], "grad_sibling_exchange")
        chip_sums = _pair_sum(gw["w_in"], from_sibling, core, "grad_pair_sum")
        blocks = [gw["w_branch_a"].reshape(512, NDEV, csh).transpose(1, 0, 2).astype(bf16),
                  gw["w_branch_b"].reshape(512, NDEV, csh).transpose(1, 0, 2).astype(bf16),
                  gw["w_out"].reshape(NDEV, csh, D).astype(bf16),
                  gw["b_merge"].reshape(2, NDEV, csh).transpose(1, 0, 2)]
        pending["started"] = _scatter_start([chip_sums], blocks, "scatter_grads_start")
        return pending["started"][4][0, 0]

    loc = _local_step(x[0], loss_target[0], norm_gain, weights, q_norm_a, k_norm_a, q_norm_b, k_norm_b, sink_a,
                      rel_bias, on_weight_grads=start_exchange)

    small_shapes = dict(norm_gain=(1, D), q_norm_a=(1, HD), k_norm_a=(1, HD), q_norm_b=(1, HD), k_norm_b=(1, HD),
                        sink_a=(1, 8), rel_bias=(32, 32))
    (r_small,) = _exchange([], [_pack_small(loc, loc["loss"])], "gather_small_grads")
    send_sems, recv_sems, srcs, lands, _ = pending["started"]
    (s_in, s_a, s_b, s_o, s_bm), (r_in, r_a, r_b, r_o, r_bm) = _scatter_wait(
        send_sems, recv_sems, srcs, lands, r_small, "scatter_grads_wait")

    given = dict(norm_gain=norm_gain, q_norm_a=q_norm_a, k_norm_a=k_norm_a, q_norm_b=q_norm_b, k_norm_b=k_norm_b,
                 sink_a=sink_a, rel_bias=rel_bias)
    m_small = dict(norm_gain=m_norm_gain, q_norm_a=m_q_norm_a, k_norm_a=m_k_norm_a, q_norm_b=m_q_norm_b,
                   k_norm_b=m_k_norm_b, sink_a=m_sink_a, rel_bias=m_rel_bias)
    v_small = dict(norm_gain=v_norm_gain, q_norm_a=v_q_norm_a, k_norm_a=v_k_norm_a, q_norm_b=v_q_norm_b,
                   k_norm_b=v_k_norm_b, sink_a=v_sink_a, rel_bias=v_rel_bias)
    res = {
        "small": _adamw(_pack_small(given), r_small, _pack_small(m_small), _pack_small(v_small), "adamw_small"),
        "w_in": [jnp.swapaxes(t, 1, 2) for t in
                 _adamw_own(w_in_t, s_in, chip, r_in, m_w_in_t, v_w_in_t, "adamw_w_in")],
        "w_branch_a": _adamw_own(w_branch_a, s_a, me, r_a, m_w_branch_a, v_w_branch_a, "adamw_w_branch_a"),
        "w_branch_b": _adamw_own(w_branch_b, s_b, me, r_b, m_w_branch_b, v_w_branch_b, "adamw_w_branch_b"),
        "b_merge": _adamw_own(b_merge, s_bm, me, r_bm, m_b_merge, v_b_merge, "adamw_b_merge"),
        "w_out": _adamw_own(w_out, s_o, me, r_o, m_w_out, v_w_out, "adamw_w_out"),
    }
    order = ["norm_gain", "w_in", "q_norm_a", "k_norm_a", "q_norm_b", "k_norm_b", "sink_a", "rel_bias", "w_branch_a",
             "w_branch_b", "b_merge", "w_out"]
    outs = []
    for k in range(4):
        small = _unpack_small(res["small"][k], small_shapes)
        for n in order:
            outs.append(small[n] if n in small else res[n][k])
    loss = res["small"][0][0, SMALL_USED]
    return (loss, loc["grad_x"][None], *outs)
```

```python
import math

import numpy as np
import jax
import jax.numpy as jnp
from jax import lax
from jax.experimental import pallas as pl
from jax.experimental.pallas import tpu as pltpu

f32 = jnp.float32
bf16 = jnp.bfloat16

S = 4096
D = 1024
NA = 5376
NT = 3072
NW = NA + NT
WSH = NW // 8
HD = 64
LANES = 128
EPS = 1e-6
NEG = -1e30
SCALE = HD ** -0.5
TQ = 128
PAD = 128
SP = S + 2 * PAD
NDEV = 8
GROUPS = ((128, 1, 0), (64, 1, 8), (64, 4, 16), (64, 16, 24))
CHUNK = 256
PCHUNK = 128
RC = 64
TN = 768

ADAM_LR, ADAM_B1, ADAM_B2, ADAM_EPS, ADAM_WD, ADAM_STEP = 0.001, 0.9, 0.999, 1e-08, 0.01, 10

MIB = 1024 * 1024
NT_DIMS = (((1,), (1,)), ((), ()))
TN_DIMS = (((0,), (0,)), ((), ()))


def _params(sem=None, vmem_mib=48):
    return pltpu.CompilerParams(dimension_semantics=sem, vmem_limit_bytes=vmem_mib * MIB)


def _lo():
    return lax.broadcasted_iota(jnp.int32, (1, LANES), 1) < HD


def _head_ones():
    r = lax.broadcasted_iota(jnp.int32, (LANES, LANES), 0) // HD
    c = lax.broadcasted_iota(jnp.int32, (LANES, LANES), 1) // HD
    return jnp.where(r == c, 1.0, 0.0).astype(bf16)


def _half_sums(x, ones):
    hi = x.astype(bf16)
    mid = (x - hi.astype(f32)).astype(bf16)
    return (jnp.dot(hi, ones, preferred_element_type=f32) + jnp.dot(mid, ones, preferred_element_type=f32))


def _seg_sum(x, ones):
    outs = [_half_sums(x[:, b * LANES:(b + 1) * LANES], ones) for b in range(x.shape[1] // LANES)]
    return outs[0] if len(outs) == 1 else jnp.concatenate(outs, axis=1)


def _bucket_np(blk, stride):
    w = TQ + 2 * blk
    rel = np.arange(w)[None, :] - blk - np.arange(TQ)[:, None]
    band = np.abs(rel) <= blk
    r = rel * stride
    n = np.abs(r)
    nf = np.maximum(n, 8).astype(np.float32)
    large = 8 + (np.log(nf / np.float32(8)) / np.float32(math.log(128.0)) * np.float32(8)).astype(np.int32)
    large = np.minimum(large, 15)
    b = (r > 0).astype(np.int32) * 16 + np.where(n < 8, n, large)
    return np.where(band, b, -1).astype(np.int32)


def _rms(x, gain):
    ts = 512

    def body(x_ref, g_ref, h_ref, ht_ref, r_ref):
        xv = x_ref[...]
        r = lax.rsqrt(jnp.mean(xv * xv, axis=-1, keepdims=True) + EPS)
        h = (xv * r) * g_ref[...]
        h_ref[...] = h.astype(bf16)
        ht_ref[...] = h.T.astype(bf16)
        r_ref[...] = r

    return pl.pallas_call(
        body,
        grid=(S // ts,),
        in_specs=[pl.BlockSpec((ts, D), lambda i: (i, 0)), pl.BlockSpec((1, D), lambda i: (0, 0))],
        out_specs=[pl.BlockSpec((ts, D), lambda i: (i, 0)), pl.BlockSpec((D, ts), lambda i: (0, i)),
                   pl.BlockSpec((ts, 1), lambda i: (i, 0))],
        out_shape=[jax.ShapeDtypeStruct((S, D), bf16), jax.ShapeDtypeStruct((D, S), bf16),
                   jax.ShapeDtypeStruct((S, 1), f32)],
        compiler_params=_params(("arbitrary",)),
        name="rms",
    )(x, gain)


def _inproj_half(hb, w_t, half, proj, name):
    ts = 1024
    tn = NW // 6
    per = NW // 2 // tn

    def body(h_idx, h_ref, w_ref, *rest):
        del h_idx
        rest[-1][...] = lax.dot_general(h_ref[...], w_ref[...], NT_DIMS, preferred_element_type=f32)

    in_specs = [pl.BlockSpec((ts, D), lambda i, n, hf: (i, 0)),
                pl.BlockSpec((tn, D), lambda i, n, hf: (hf[0] * per + n, 0))]
    args = [half, hb, w_t]
    aliases = {}
    if proj is not None:
        in_specs.append(pl.BlockSpec(memory_space=pl.ANY))
        args.append(proj)
        aliases = {3: 0}
    return pl.pallas_call(
        body,
        grid_spec=pltpu.PrefetchScalarGridSpec(
            num_scalar_prefetch=1,
            grid=(S // ts, per),
            in_specs=in_specs,
            out_specs=pl.BlockSpec((ts, tn), lambda i, n, hf: (i, hf[0] * per + n)),
        ),
        out_shape=jax.ShapeDtypeStruct((S, NW), f32),
        input_output_aliases=aliases,
        compiler_params=_params(("arbitrary", "arbitrary")),
        name=name,
    )(*args)


def _bias_expand(table, bucket, c0, name):
    tq, w = bucket.shape
    blk = (w - tq) // 2

    def body(tab_ref, bk_ref, o_ref):
        h = pl.program_id(0)
        bk = bk_ref[...]

        def step(b, acc):
            return jnp.where(bk == b, tab_ref[b, c0 + h], acc)

        inner = lax.fori_loop(0, 32, step, jnp.full((tq, w), NEG, f32))
        col = lax.broadcasted_iota(jnp.int32, (1, w), 1)
        o_ref[0] = jnp.where(col < blk, NEG, inner)
        o_ref[1] = inner
        o_ref[2] = jnp.where(col >= tq + blk, NEG, inner)

    return pl.pallas_call(
        body,
        grid=(8,),
        in_specs=[pl.BlockSpec(memory_space=pltpu.SMEM), pl.BlockSpec((tq, w), lambda h: (0, 0))],
        out_specs=pl.BlockSpec((3, None, tq, w), lambda h: (0, h, 0, 0)),
        out_shape=jax.ShapeDtypeStruct((3, 8, tq, w), f32),
        compiler_params=_params(("arbitrary",)),
        name=name,
    )(table, bucket)


def _tile_kind(t, seq):
    m0 = jnp.bitwise_and(t * TQ, seq - 1)
    return jnp.where(m0 == 0, 0, jnp.where(m0 == seq - TQ, 2, 1))


def _col_block(g, j):
    kind = j // 4
    hp = j % 4
    a = jnp.where(kind == 0, hp, 3 + kind)
    b = 6 + 12 * kind + 4 * (g - 1) + hp
    return jnp.where(g == 0, a, b)


def _prep(proj_a, gains):
    def body(p_ref, g_ref, o_ref):
        g = pl.program_id(0)
        j = pl.program_id(1)
        kind = j // 4
        lo = _lo()
        ones = _head_ones()
        half = jnp.where(lo, 0, 1)
        take = (kind == 0) | (half == (j % 4) // 2)
        gain = g_ref[...]
        o_ref[0:PAD, :] = jnp.zeros((PAD, LANES), bf16)
        o_ref[PAD + S:SP, :] = jnp.zeros((PAD, LANES), bf16)

        def norm_store(xv, dst, dup):
            if dup:
                xv = jnp.where(take, xv, pltpu.roll(xv, HD, 1))
            r = lax.rsqrt(_half_sums(xv * xv, ones) * (1.0 / HD) + EPS)
            r = jnp.where(kind == 2, 1.0, r)
            yv = (xv * r) * gain
            yv = jnp.where(kind == 0, yv * SCALE, yv)
            o_ref[PAD + dst:PAD + dst + CHUNK, :] = yv.astype(bf16)

        for gi, (_, d, _) in enumerate(GROUPS):
            @pl.when(g == gi)
            def _():
                seq = S // d
                for c in range(d):
                    for i in range(seq // CHUNK):
                        if d == 1:
                            xv = p_ref[i * CHUNK:(i + 1) * CHUNK, :]
                        else:
                            xv = p_ref[pl.ds(c + i * CHUNK * d, CHUNK, stride=d), :]
                        norm_store(xv, c * seq + i * CHUNK, gi == 0)

    return pl.pallas_call(
        body,
        grid=(4, 12),
        in_specs=[
            pl.BlockSpec((S, LANES), lambda g, j: (0, _col_block(g, j))),
            pl.BlockSpec((None, None, 1, LANES), lambda g, j: (g, j // 4, 0, 0)),
        ],
        out_specs=pl.BlockSpec((None, None, SP, LANES), lambda g, j: (g, j, 0, 0)),
        out_shape=jax.ShapeDtypeStruct((4, 12, SP, LANES), bf16),
        compiler_params=_params(("arbitrary", "arbitrary")),
        name="prep",
    )(proj_a, gains)


def _token_rows(t, r0, n, d):
    if d == 1:
        return pl.ds(pl.multiple_of(t * TQ, TQ) + r0, n)
    per = S // d // TQ
    return pl.ds(((t % per) * TQ + r0) * d + t // per, n, stride=d)


def _stack_heads(t, lo):
    z = jnp.zeros_like(t)
    return jnp.concatenate([jnp.where(lo, t, z), jnp.where(lo, z, t)], axis=0)


def _unstack_heads(t2, lo):
    return jnp.where(lo, t2[:TQ], t2[TQ:])


def _attn_fwd(gl, bias, sink, g, blk, d, name):
    w = TQ + 2 * blk
    seq = S // d
    use_sink = sink is not None

    def body(*refs):
        if use_sink:
            sink_ref, q_ref, k_ref, v_ref, b_ref, o_ref, l_ref, s0, s1, p0, p1, lse_scr = refs
        else:
            q_ref, k_ref, v_ref, b_ref, o_ref, l_ref, s0, s1, p0, p1, lse_scr = refs
        hp = pl.program_id(0)
        lo = _lo()
        s_bufs, p_bufs = (s0, s1), (p0, p1)

        def scores(p, slot):
            for u in range(2):
                f0 = pl.multiple_of((2 * p + u) * TQ, TQ)
                q2 = _stack_heads(q_ref[pl.ds(PAD + f0, TQ), :], lo)
                kw = k_ref[pl.ds(PAD - blk + f0, w), :]
                s_bufs[slot][u] = lax.dot_general(q2, kw, NT_DIMS, preferred_element_type=f32)

        def softmax(p, slot):
            for u in range(2):
                t = 2 * p + u
                kind = _tile_kind(t, seq)
                for h in range(2):
                    for r in range(TQ // RC):
                        rows = slice(h * TQ + r * RC, h * TQ + (r + 1) * RC)
                        logit = s_bufs[slot][u, rows, :] + b_ref[kind, h, r * RC:(r + 1) * RC, :]
                        m = jnp.max(logit, axis=1, keepdims=True)
                        e = jnp.exp(logit - m)
                        lse = m + jnp.log(jnp.sum(e, axis=1, keepdims=True))
                        if use_sink:
                            sk = sink_ref[2 * hp + h]
                            mx = jnp.maximum(lse, sk)
                            lse = mx + jnp.log(jnp.exp(lse - mx) + jnp.exp(sk - mx))
                        p_bufs[slot][u, rows, :] = (e * jnp.exp(m - lse)).astype(bf16)
                        lse_scr[u, rows, :] = jnp.broadcast_to(lse, (RC, LANES))
                l_ref[_token_rows(t, 0, TQ, d), :] = jnp.where(lo, lse_scr[u, 0:TQ, :], lse_scr[u, TQ:2 * TQ, :])

        def values(p, slot):
            for u in range(2):
                t = 2 * p + u
                vw = v_ref[pl.ds(PAD - blk + pl.multiple_of(t * TQ, TQ), w), :]
                o2 = jnp.dot(p_bufs[slot][u], vw, preferred_element_type=f32)
                o_ref[_token_rows(t, 0, TQ, d), :] = _unstack_heads(o2, lo)

        npair = S // TQ // 2
        scores(0, 0)
        scores(1, 1)
        softmax(0, 0)

        def steady(k, carry):
            p = 2 * k + 2
            scores(p, 0)
            softmax(p - 1, 1)
            values(p - 2, 0)
            scores(p + 1, 1)
            softmax(p, 0)
            values(p - 1, 1)
            return carry

        lax.fori_loop(0, (npair - 2) // 2, steady, 0)
        softmax(npair - 1, 1)
        values(npair - 2, 0)
        values(npair - 1, 1)

    in_specs = [
        pl.BlockSpec((None, None, SP, LANES), lambda hp: (g, hp, 0, 0)),
        pl.BlockSpec((None, None, SP, LANES), lambda hp: (g, 4 + hp, 0, 0)),
        pl.BlockSpec((None, None, SP, LANES), lambda hp: (g, 8 + hp, 0, 0)),
        pl.BlockSpec((3, 2, TQ, w), lambda hp: (0, hp, 0, 0)),
    ]
    args = [gl, gl, gl, bias]
    if use_sink:
        in_specs = [pl.BlockSpec(memory_space=pltpu.SMEM)] + in_specs
        args = [sink] + args
    out = pl.BlockSpec((S, LANES), lambda hp: (0, hp))
    return pl.pallas_call(
        body,
        grid=(4,),
        in_specs=in_specs,
        out_specs=[out, out],
        out_shape=[jax.ShapeDtypeStruct((S, 4 * LANES), f32)] * 2,
        scratch_shapes=[pltpu.VMEM((2, 2 * TQ, w), f32), pltpu.VMEM((2, 2 * TQ, w), f32),
                        pltpu.VMEM((2, 2 * TQ, w), bf16), pltpu.VMEM((2, 2 * TQ, w), bf16),
                        pltpu.VMEM((2, 2 * TQ, LANES), f32)],
        compiler_params=_params(("arbitrary",)),
        name=name,
    )(*args)


def _attn_bwd(gl, bias, bucket, do, lse, dd, g, blk, d, name):
    w = TQ + 2 * blk
    seq = S // d

    def body(q_ref, k_ref, v_ref, b_ref, bk_ref, do_ref, l_ref, d_ref, dqkv_ref, dbk_ref,
             db_acc, s0, s1, dp0, dp1, pb0, pb1, ds0, ds1):
        lo = _lo()
        hi = jnp.logical_not(lo)
        dqkv_ref[1] = jnp.zeros((SP, LANES), f32)
        dqkv_ref[2] = jnp.zeros((SP, LANES), f32)
        db_acc[...] = jnp.zeros((2 * TQ, w), f32)
        s_bufs, dp_bufs, pb_bufs, ds_bufs = (s0, s1), (dp0, dp1), (pb0, pb1), (ds0, ds1)

        def stacked(t):
            f0 = pl.multiple_of(t * TQ, TQ)
            q2 = _stack_heads(q_ref[pl.ds(PAD + f0, TQ), :], lo)
            do2 = _stack_heads(do_ref[_token_rows(t, 0, TQ, d), :].astype(bf16), lo)
            return f0, q2, do2

        def scores(p, slot):
            for u in range(2):
                f0, q2, do2 = stacked(2 * p + u)
                win = pl.ds(PAD - blk + f0, w)
                s_bufs[slot][u] = lax.dot_general(q2, k_ref[win, :], NT_DIMS, preferred_element_type=f32)
                dp_bufs[slot][u] = lax.dot_general(do2, v_ref[win, :], NT_DIMS, preferred_element_type=f32)

        def grads(p, slot):
            for u in range(2):
                t = 2 * p + u
                kind = _tile_kind(t, seq)
                for h in range(2):
                    msk = lo if h == 0 else hi
                    for r in range(TQ // RC):
                        rows = slice(h * TQ + r * RC, h * TQ + (r + 1) * RC)
                        src = _token_rows(t, r * RC, RC, d)
                        lh = jnp.max(jnp.where(msk, l_ref[src, :], -jnp.inf), axis=1, keepdims=True)
                        dh = jnp.max(jnp.where(msk, d_ref[src, :], -jnp.inf), axis=1, keepdims=True)
                        logit = s_bufs[slot][u, rows, :] + b_ref[kind, h, r * RC:(r + 1) * RC, :]
                        pr = jnp.exp(logit - lh)
                        ds = pr * (dp_bufs[slot][u, rows, :] - dh)
                        db_acc[rows, :] += ds
                        pb_bufs[slot][u, rows, :] = pr.astype(bf16)
                        ds_bufs[slot][u, rows, :] = ds.astype(bf16)

        def accumulate(p, slot):
            for u in range(2):
                f0, q2, do2 = stacked(2 * p + u)
                win = pl.ds(PAD - blk + f0, w)
                dsb = ds_bufs[slot][u]
                dq2 = jnp.dot(dsb, k_ref[win, :], preferred_element_type=f32)
                dqkv_ref[0, pl.ds(PAD + f0, TQ), :] = _unstack_heads(dq2, lo)
                dqkv_ref[1, win, :] += lax.dot_general(dsb, q2, TN_DIMS, preferred_element_type=f32)
                dqkv_ref[2, win, :] += lax.dot_general(pb_bufs[slot][u], do2, TN_DIMS, preferred_element_type=f32)

        npair = S // TQ // 2
        scores(0, 0)
        scores(1, 1)
        grads(0, 0)

        def steady(k, carry):
            p = 2 * k + 2
            scores(p, 0)
            grads(p - 1, 1)
            accumulate(p - 2, 0)
            scores(p + 1, 1)
            grads(p, 0)
            accumulate(p - 1, 1)
            return carry

        lax.fori_loop(0, (npair - 2) // 2, steady, 0)
        grads(npair - 1, 1)
        accumulate(npair - 2, 0)
        accumulate(npair - 1, 1)

        bk = bk_ref[...]
        lane = lax.broadcasted_iota(jnp.int32, (8, LANES), 1)
        for h in range(2):
            db = db_acc[h * TQ:(h + 1) * TQ, :]
            acc = jnp.zeros((8, LANES), f32)
            for b in range(32):
                part = jnp.where(bk == b, db, 0.0).reshape(TQ // 8, 8, w).sum(axis=0)
                tot = jnp.sum(jnp.sum(part, axis=1, keepdims=True), axis=0, keepdims=True)
                acc = jnp.where(lane == b, tot, acc)
            dbk_ref[h] = acc

    def gcol(off):
        return pl.BlockSpec((None, None, SP, LANES), lambda hp: (g, off + hp, 0, 0))

    row = pl.BlockSpec((S, LANES), lambda hp: (0, hp))
    return pl.pallas_call(
        body,
        grid=(4,),
        in_specs=[gcol(0), gcol(4), gcol(8), pl.BlockSpec((3, 2, TQ, w), lambda hp: (0, hp, 0, 0)),
                  pl.BlockSpec((TQ, w), lambda hp: (0, 0)), row, row, row],
        out_specs=[pl.BlockSpec((3, None, SP, LANES), lambda hp: (0, hp, 0, 0)),
                   pl.BlockSpec((2, 8, LANES), lambda hp: (hp, 0, 0))],
        out_shape=[
            jax.ShapeDtypeStruct((3, 4, SP, LANES), f32),
            jax.ShapeDtypeStruct((8, 8, LANES), f32),
        ],
        scratch_shapes=([pltpu.VMEM((2 * TQ, w), f32)] + [pltpu.VMEM((2, 2 * TQ, w), f32)] * 4
                        + [pltpu.VMEM((2, 2 * TQ, w), bf16)] * 4),
        compiler_params=_params(("arbitrary",), vmem_mib=56),
        name=name,
    )(gl, gl, gl, bias, bucket, do, lse, dd)


def _sigmoid(z):
    return 1.0 / (1.0 + jnp.exp(-z))


def _tail(x, tgt, o_a, l_a, o_b, l_b, proj, bm, w_a, w_b, w_o, sink_b):
    ts = 256

    def body(x_ref, t_ref, oa_ref, la_ref, ob0_ref, ob1_ref, ob2_ref, lb0_ref, lb1_ref, lb2_ref,
             ga_ref, gb_ref, m0_ref, m1_ref, bm_ref, wa_ref, wb_ref, wo_ref, sk_ref,
             dy_ref, dyb_ref, dt_ref, doa_ref, dda_ref, dob0_ref, dob1_ref, dob2_ref, ddb0_ref, ddb1_ref, ddb2_ref,
             ya_ref, yb_ref, mg_ref, dbra_ref, dbrb_ref, loss_ref, dbm_ref, dsk_ref):
        i = pl.program_id(0)

        @pl.when(i == 0)
        def _():
            loss_ref[...] = jnp.zeros_like(loss_ref)
            dbm_ref[...] = jnp.zeros_like(dbm_ref)
            dsk_ref[...] = jnp.zeros_like(dsk_ref)

        ga = ga_ref[...]
        sa = _sigmoid(ga)
        silu_a = ga * sa
        oa = oa_ref[...]
        ya = oa * silu_a
        gb = gb_ref[...]
        sb = _sigmoid(gb)
        silu_b = gb * sb
        ob = [ob0_ref[...], ob1_ref[...], ob2_ref[...]]
        lb = [lb0_ref[...], lb1_ref[...], lb2_ref[...]]
        mx = jnp.maximum(jnp.maximum(lb[0], lb[1]), lb[2])
        ex = [jnp.exp(v - mx) for v in lb]
        den = ex[0] + ex[1] + ex[2]
        alpha = [e / den for e in ex]
        ybc = alpha[0] * ob[0] + alpha[1] * ob[1] + alpha[2] * ob[2]
        yb = ybc * silu_b
        yab = ya.astype(bf16)
        ybb = yb.astype(bf16)
        br_a = jnp.dot(yab, wa_ref[...], preferred_element_type=f32)
        br_b = jnp.dot(ybb, wb_ref[...], preferred_element_type=f32)
        g0 = _sigmoid(m0_ref[...] + bm_ref[0:1, :])
        g1 = _sigmoid(m1_ref[...] + bm_ref[1:2, :])
        merged = g0 * br_a + g1 * br_b
        mgb = merged.astype(bf16)
        y = x_ref[...] + jnp.dot(mgb, wo_ref[...], preferred_element_type=f32)
        err = y - t_ref[...]
        part = jnp.sum(jnp.sum(err * err, axis=1, keepdims=True), axis=0, keepdims=True)
        loss_ref[...] += part * (0.5 / D)
        dy = err * (1.0 / D)
        dyb = dy.astype(bf16)
        dmerged = lax.dot_general(dyb, wo_ref[...], NT_DIMS, preferred_element_type=f32)
        dbr_a = (dmerged * g0).astype(bf16)
        dbr_b = (dmerged * g1).astype(bf16)
        dm0 = dmerged * br_a * (g0 * (1.0 - g0))
        dm1 = dmerged * br_b * (g1 * (1.0 - g1))
        dbm_ref[0:1, :] += jnp.sum(dm0, axis=0, keepdims=True)
        dbm_ref[1:2, :] += jnp.sum(dm1, axis=0, keepdims=True)
        dya = lax.dot_general(dbr_a, wa_ref[...], NT_DIMS, preferred_element_type=f32)
        dyb2 = lax.dot_general(dbr_b, wb_ref[...], NT_DIMS, preferred_element_type=f32)
        do_a = dya * silu_a
        dga = dya * oa * (sa * (1.0 + ga * (1.0 - sa)))
        ones = _head_ones()
        delta_a = _seg_sum(do_a * oa, ones)
        dsk_ref[...] -= jnp.sum(delta_a * jnp.exp(sk_ref[...] - la_ref[...]), axis=0, keepdims=True)
        dybc = dyb2 * silu_b
        dgb = dyb2 * ybc * (sb * (1.0 + gb * (1.0 - sb)))
        dbar = _seg_sum(dybc * ybc, ones)
        dy_ref[...] = dy
        dyb_ref[...] = dyb
        dt_ref[:, 0:512] = dga.astype(bf16)
        dt_ref[:, 512:1024] = dgb.astype(bf16)
        dt_ref[:, 1024:2048] = dm0.astype(bf16)
        dt_ref[:, 2048:3072] = dm1.astype(bf16)
        doa_ref[...] = do_a.astype(bf16)
        dda_ref[...] = delta_a
        for k, (dob_ref, ddb_ref) in enumerate(((dob0_ref, ddb0_ref), (dob1_ref, ddb1_ref), (dob2_ref, ddb2_ref))):
            dob_ref[...] = alpha[k] * dybc
            ddb_ref[...] = alpha[k] * dbar
        ya_ref[...] = ya.T.astype(bf16)
        yb_ref[...] = yb.T.astype(bf16)
        mg_ref[...] = merged.T.astype(bf16)
        dbra_ref[...] = dbr_a
        dbrb_ref[...] = dbr_b

    def rows(n, blk=0):
        return pl.BlockSpec((ts, n), lambda i: (i, blk))

    def whole(r, c):
        return pl.BlockSpec((r, c), lambda i: (0, 0))

    def cols(n):
        return pl.BlockSpec((n, ts), lambda i: (0, i))

    def gate_cols(n, col):
        return pl.BlockSpec((pl.Element(ts), pl.Element(n)), lambda i: (i * ts, NA + col))

    outs = [
        ((S, D), f32, rows(D)), ((S, D), bf16, rows(D)), ((S, NW), bf16, gate_cols(NT, 0)),
        ((S, 512), bf16, rows(512)), ((S, 512), f32, rows(512)),
        ((S, 512), f32, rows(512)), ((S, 512), f32, rows(512)), ((S, 512), f32, rows(512)),
        ((S, 512), f32, rows(512)), ((S, 512), f32, rows(512)), ((S, 512), f32, rows(512)),
        ((512, S), bf16, cols(512)), ((512, S), bf16, cols(512)), ((D, S), bf16, cols(D)),
        ((S, D), bf16, rows(D)), ((S, D), bf16, rows(D)),
        ((1, 1), f32, whole(1, 1)), ((2, D), f32, whole(2, D)), ((1, 512), f32, whole(1, 512)),
    ]
    return pl.pallas_call(
        body,
        grid=(S // ts,),
        in_specs=[
            rows(D), rows(D), rows(512), rows(512), rows(512), rows(512), rows(512), rows(512), rows(512), rows(512),
            gate_cols(512, 0), gate_cols(512, 512), gate_cols(D, 1024), gate_cols(D, 2048), whole(2, D),
            whole(512, D), whole(512, D), whole(D, D), whole(1, 512),
        ],
        out_specs=[o[2] for o in outs],
        out_shape=[jax.ShapeDtypeStruct(o[0], o[1]) for o in outs],
        compiler_params=_params(("arbitrary",), vmem_mib=60),
        name="tail",
    )(x, tgt, o_a, l_a, *o_b, *l_b, proj, proj, proj, proj, bm, w_a, w_b, w_o, sink_b)


def _norm_bwd(xv, dyv, gain, kind, ones):
    r = lax.rsqrt(_half_sums(xv * xv, ones) * (1.0 / HD) + EPS)
    yv = xv * r
    up = jnp.where(kind == 0, dyv * SCALE, dyv)
    u = up * gain
    dxv = r * (u - yv * (_half_sums(u * yv, ones) * (1.0 / HD)))
    dxv = jnp.where(kind == 2, dyv, dxv)
    dg = jnp.where(kind == 2, 0.0, jnp.sum(up * yv, axis=0, keepdims=True))
    return dxv, dg


def _post_b(g, dqkv, proj_a, gains, dproj):
    d = GROUPS[g][1]
    seq = S // d

    def body(d_ref, p_ref, g_ref, alias_ref, o_ref, dg_ref, nat):
        del alias_ref
        j = pl.program_id(0)
        kind = j // 4
        gain = g_ref[...]
        ones = _head_ones()

        @pl.when(j % 4 == 0)
        def _():
            dg_ref[...] = jnp.zeros_like(dg_ref)

        for c in range(d):
            for i in range(seq // PCHUNK):
                src = c * seq + i * PCHUNK
                if d == 1:
                    idx = slice(src, src + PCHUNK)
                else:
                    idx = pl.ds(c + i * PCHUNK * d, PCHUNK, stride=d)
                dxv, dg = _norm_bwd(p_ref[idx, :], d_ref[PAD + src:PAD + src + PCHUNK, :], gain, kind, ones)
                nat[idx, :] = dxv
                dg_ref[...] += dg

        for i in range(S // CHUNK):
            o_ref[i * CHUNK:(i + 1) * CHUNK, :] = nat[i * CHUNK:(i + 1) * CHUNK, :].astype(bf16)

    return pl.pallas_call(
        body,
        grid=(12,),
        in_specs=[
            pl.BlockSpec((None, None, SP, LANES), lambda j: (j // 4, j % 4, 0, 0)),
            pl.BlockSpec((S, LANES), lambda j: (0, _col_block(g, j))),
            pl.BlockSpec((None, None, 1, LANES), lambda j: (g, j // 4, 0, 0)),
            pl.BlockSpec(memory_space=pl.ANY),
        ],
        out_specs=[
            pl.BlockSpec((S, LANES), lambda j: (0, _col_block(g, j))),
            pl.BlockSpec((None, 1, LANES), lambda j: (j // 4, 0, 0)),
        ],
        out_shape=[jax.ShapeDtypeStruct((S, NW), bf16), jax.ShapeDtypeStruct((3, 1, LANES), f32)],
        scratch_shapes=[pltpu.VMEM((S, LANES), f32)],
        input_output_aliases={3: 0},
        compiler_params=_params(("arbitrary",)),
        name="post_b%d" % g,
    )(dqkv, proj_a, gains, dproj)


def _post_a(dqkv, proj_a, gains, dproj):
    def body(q_ref, e_ref, p_ref, g_ref, alias_ref, o_ref, dg_ref):
        del alias_ref
        j = pl.program_id(0)
        kind = jnp.maximum(j - 3, 0)
        gain = g_ref[...]
        lo = _lo()
        ones = _head_ones()

        @pl.when((j == 0) | (j >= 4))
        def _():
            dg_ref[...] = jnp.zeros_like(dg_ref)

        for i in range(S // PCHUNK):
            r0 = i * PCHUNK
            rows = slice(PAD + r0, PAD + r0 + PCHUNK)
            t0 = e_ref[0, rows, :] + e_ref[1, rows, :]
            t1 = e_ref[2, rows, :] + e_ref[3, rows, :]
            folded = jnp.where(lo, t0 + pltpu.roll(t0, HD, 1), t1 + pltpu.roll(t1, HD, 1))
            dyv = jnp.where(kind == 0, q_ref[rows, :], folded)
            dxv, dg = _norm_bwd(p_ref[r0:r0 + PCHUNK, :], dyv, gain, kind, ones)
            o_ref[r0:r0 + PCHUNK, :] = dxv.astype(bf16)
            dg_ref[...] += dg

    return pl.pallas_call(
        body,
        grid=(6,),
        in_specs=[
            pl.BlockSpec((None, None, SP, LANES), lambda j: (0, jnp.minimum(j, 3), 0, 0)),
            pl.BlockSpec((None, 4, SP, LANES), lambda j: (jnp.clip(j - 3, 1, 2), 0, 0, 0)),
            pl.BlockSpec((S, LANES), lambda j: (0, j)),
            pl.BlockSpec((None, None, 1, LANES), lambda j: (0, jnp.maximum(j - 3, 0), 0, 0)),
            pl.BlockSpec(memory_space=pl.ANY),
        ],
        out_specs=[
            pl.BlockSpec((S, LANES), lambda j: (0, j)),
            pl.BlockSpec((None, 1, LANES), lambda j: (jnp.maximum(j - 3, 0), 0, 0)),
        ],
        out_shape=[jax.ShapeDtypeStruct((S, NW), bf16), jax.ShapeDtypeStruct((3, 1, LANES), f32)],
        input_output_aliases={4: 0},
        compiler_params=_params(("arbitrary",)),
        name="post_a",
    )(dqkv, dqkv, proj_a, gains, dproj)


def _dh_norm_bwd(dproj, w, x, rstd, gain, dy):
    ts = 1024
    tk = NW // 6
    nk = NW // tk

    def body(d_ref, w_ref, x_ref, r_ref, g_ref, dy_ref, gx_ref, dgn_ref, acc):
        i = pl.program_id(0)
        k = pl.program_id(1)

        @pl.when((i == 0) & (k == 0))
        def _():
            dgn_ref[...] = jnp.zeros_like(dgn_ref)

        @pl.when(k == 0)
        def _():
            acc[...] = jnp.zeros_like(acc)

        acc[...] += jnp.dot(d_ref[...], w_ref[...], preferred_element_type=f32)

        @pl.when(k == nk - 1)
        def _():
            dh = acc[...]
            xh = x_ref[...] * r_ref[...]
            u = dh * g_ref[...]
            dx = r_ref[...] * (u - xh * jnp.mean(u * xh, axis=-1, keepdims=True))
            gx_ref[...] = dy_ref[...] + dx
            dgn_ref[...] += jnp.sum(dh * xh, axis=0, keepdims=True)

    return pl.pallas_call(
        body,
        grid=(S // ts, nk),
        in_specs=[
            pl.BlockSpec((ts, tk), lambda i, k: (i, k)),
            pl.BlockSpec((tk, D), lambda i, k: (k, 0)),
            pl.BlockSpec((ts, D), lambda i, k: (i, 0)),
            pl.BlockSpec((ts, 1), lambda i, k: (i, 0)),
            pl.BlockSpec((1, D), lambda i, k: (0, 0)),
            pl.BlockSpec((ts, D), lambda i, k: (i, 0)),
        ],
        out_specs=[pl.BlockSpec((ts, D), lambda i, k: (i, 0)), pl.BlockSpec((1, D), lambda i, k: (0, 0))],
        out_shape=[jax.ShapeDtypeStruct((S, D), f32), jax.ShapeDtypeStruct((1, D), f32)],
        scratch_shapes=[pltpu.VMEM((ts, D), f32)],
        compiler_params=_params(("arbitrary", "arbitrary"), vmem_mib=56),
        name="dh_norm_bwd",
    )(dproj, w, x, rstd, gain, dy)


def _dw_in(hbt, dproj):
    tk = 1024
    win = WSH + 96

    def body(a_ref, b_ref, o_ref, acc):
        j = pl.program_id(0)
        k = pl.program_id(1)

        @pl.when(k == 0)
        def _():
            acc[...] = jnp.zeros_like(acc)

        acc[...] += jnp.dot(a_ref[...], b_ref[...], preferred_element_type=f32)

        @pl.when(k == S // tk - 1)
        def _():
            acc_t = acc[...].T
            for jj in range(NDEV):
                off = (WSH * jj) % LANES

                @pl.when(j == jj)
                def _():
                    o_ref[...] = acc_t[off:off + WSH, :].astype(bf16)

    return pl.pallas_call(
        body,
        grid=(NDEV, S // tk),
        in_specs=[
            pl.BlockSpec((D, tk), lambda j, k: (0, k)),
            pl.BlockSpec((pl.Element(tk), pl.Element(win)), lambda j, k: (k * tk, (WSH * j) // LANES * LANES)),
        ],
        out_specs=pl.BlockSpec((None, WSH, D), lambda j, k: (j, 0, 0)),
        out_shape=jax.ShapeDtypeStruct((NDEV, WSH, D), bf16),
        scratch_shapes=[pltpu.VMEM((D, win), f32)],
        compiler_params=_params(("arbitrary", "arbitrary")),
        name="dw_in",
    )(hbt, dproj)


def _matmul_tokens(at, b, name):
    m, n = at.shape[0], b.shape[1]
    tn = 512
    tk = 1024

    def body(a_ref, b_ref, o_ref):
        @pl.when(pl.program_id(1) == 0)
        def _():
            o_ref[...] = jnp.zeros_like(o_ref)

        o_ref[...] += jnp.dot(a_ref[...], b_ref[...], preferred_element_type=f32)

    return pl.pallas_call(
        body,
        grid=(n // tn, S // tk),
        in_specs=[pl.BlockSpec((m, tk), lambda j, k: (0, k)), pl.BlockSpec((tk, tn), lambda j, k: (k, j))],
        out_specs=pl.BlockSpec((m, tn), lambda j, k: (0, j)),
        out_shape=jax.ShapeDtypeStruct((m, n), f32),
        compiler_params=_params(("arbitrary", "arbitrary")),
        name=name,
    )(at, b)


def _exchange(scatter, gather, name):
    arrs = list(scatter) + list(gather)
    n = len(arrs)
    ns = len(scatter)

    def body(*refs):
        ins, outs = refs[:n], refs[n:2 * n]
        send_sems, recv_sems, local_sems = refs[2 * n:]
        x, y, c = lax.axis_index("x"), lax.axis_index("y"), lax.axis_index("c")
        me = 4 * x + 2 * y + c
        local, remote = [], []
        for a in range(n):
            lc = pltpu.make_async_copy(ins[a].at[me] if a < ns else ins[a], outs[a].at[me], local_sems.at[a])
            lc.start()
            local.append(lc)
            for r in range(1, NDEV):
                px = 1 - x if r & 4 else x
                py = 1 - y if r & 2 else y
                pc = 1 - c if r & 1 else c
                cp = pltpu.make_async_remote_copy(
                    src_ref=ins[a].at[4 * px + 2 * py + pc] if a < ns else ins[a],
                    dst_ref=outs[a].at[me],
                    send_sem=send_sems.at[a, r - 1],
                    recv_sem=recv_sems.at[a, r - 1],
                    device_id=(px, py, pc),
                    device_id_type=pl.DeviceIdType.MESH,
                )
                cp.start()
                remote.append(cp)
        for cp in remote:
            cp.wait_recv()
        for cp in remote:
            cp.wait_send()
        for lc in local:
            lc.wait()

    out_shape = [jax.ShapeDtypeStruct(a.shape if i < ns else (NDEV,) + a.shape, a.dtype) for i, a in enumerate(arrs)]
    return pl.pallas_call(
        body,
        in_specs=[pl.BlockSpec(memory_space=pl.ANY)] * n,
        out_specs=[pl.BlockSpec(memory_space=pl.ANY)] * n,
        out_shape=out_shape,
        scratch_shapes=[
            pltpu.SemaphoreType.DMA((n, NDEV - 1)),
            pltpu.SemaphoreType.DMA((n, NDEV - 1)),
            pltpu.SemaphoreType.DMA((n,)),
        ],
        compiler_params=pltpu.CompilerParams(has_side_effects=True),
        name=name,
    )(*arrs)


def _gather_two_level(arrs, name):
    n = len(arrs)

    def body(*refs):
        ins, outs = refs[:n], refs[n:2 * n]
        send_sems, recv_sems, local_sems = refs[2 * n:]
        x, y, c = lax.axis_index("x"), lax.axis_index("y"), lax.axis_index("c")
        me, sibling = (x, y, c), (x, y, 1 - c)
        xn, yn, dg = (1 - x, y, c), (x, 1 - y, c), (1 - x, 1 - y, c)
        relay_origin = (jnp.bitwise_xor(x, c), jnp.bitwise_xor(y, 1 - c), c)
        relay_target = (jnp.bitwise_xor(x, 1 - c), jnp.bitwise_xor(y, c), c)

        def copy(a, k, block, to, src=None):
            slot = outs[a].at[4 * block[0] + 2 * block[1] + block[2]]
            return pltpu.make_async_remote_copy(
                src_ref=slot if src is None else src, dst_ref=slot, send_sem=send_sems.at[a, k],
                recv_sem=recv_sems.at[a, k], device_id=to, device_id_type=pl.DeviceIdType.MESH)

        def other_core(block):
            return (block[0], block[1], 1 - c)

        mine, sent = [], []
        for a in range(n):
            lc = pltpu.make_async_copy(ins[a], outs[a].at[4 * x + 2 * y + c], local_sems.at[a])
            lc.start()
            mine.append(lc)
            sent += [copy(a, 0, me, sibling, src=ins[a]), copy(a, 1, me, xn, src=ins[a]),
                     copy(a, 2, me, yn, src=ins[a])]
        for cp in sent:
            cp.start()
        later = []
        for a in range(n):
            copy(a, 1, xn, me).wait_recv()
            copy(a, 2, yn, me).wait_recv()
            later += [copy(a, 3, relay_origin, relay_target), copy(a, 4, xn, sibling), copy(a, 5, yn, sibling)]
            for cp in later[-3:]:
                cp.start()
        for a in range(n):
            copy(a, 3, dg, me).wait_recv()
            later.append(copy(a, 6, dg, sibling))
            later[-1].start()
        for a in range(n):
            copy(a, 0, sibling, me).wait_recv()
            for k, block in ((4, xn), (5, yn), (6, dg)):
                copy(a, k, other_core(block), me).wait_recv()
        for cp in sent + later:
            cp.wait_send()
        for lc in mine:
            lc.wait()

    return pl.pallas_call(
        body,
        in_specs=[pl.BlockSpec(memory_space=pl.ANY)] * n,
        out_specs=[pl.BlockSpec(memory_space=pl.ANY)] * n,
        out_shape=[jax.ShapeDtypeStruct((NDEV,) + a.shape, a.dtype) for a in arrs],
        scratch_shapes=[
            pltpu.SemaphoreType.DMA((n, NDEV - 1)),
            pltpu.SemaphoreType.DMA((n, NDEV - 1)),
            pltpu.SemaphoreType.DMA((n,)),
        ],
        compiler_params=pltpu.CompilerParams(has_side_effects=True),
        name=name,
    )(*arrs)


_HBM = pl.BlockSpec(memory_space=pltpu.HBM)
_SEM = pl.BlockSpec(memory_space=pltpu.SEMAPHORE)
_EFFECT = pltpu.SideEffectType.DATAFLOW_SIDE_EFFECTING


def _comm_step(name, body_fn, lands, srcs=(), wait_sems=(), n_new=0, after=()):
    n, ns, nw, na = len(lands), len(srcs), len(wait_sems), len(after)

    def body(*refs):
        src, land = refs[:ns], refs[ns:ns + n]
        waits = refs[ns + n:ns + n + nw]
        new = refs[ns + n + nw + na:ns + n + nw + na + n_new]
        body_fn(src, land, waits, new)

    hbm = [pltpu.HBM(a.shape, a.dtype) for a in lands]
    ops = [pltpu.with_memory_space_constraint(a, pltpu.HBM) for a in list(srcs) + list(lands)]
    outs = pl.pallas_call(
        body,
        out_shape=tuple([pltpu.SemaphoreType.DMA(())] * n_new + hbm),
        in_specs=[_HBM] * (ns + n) + [_SEM] * nw + [pl.BlockSpec(memory_space=pl.ANY)] * na,
        out_specs=tuple([_SEM] * n_new + [_HBM] * n),
        input_output_aliases={ns + i: n_new + i for i in range(n)},
        compiler_params=pltpu.CompilerParams(has_side_effects=_EFFECT),
        name=name,
    )(*ops, *wait_sems, *after)
    return list(outs[:n_new]), list(outs[n_new:])


class _GatheredWeights:
    def __init__(self, shards):
        self.n = n = len(shards)
        x, y, c = lax.axis_index("x"), lax.axis_index("y"), lax.axis_index("c")
        self.x = x
        me = 4 * x + 2 * y + c
        lands = [lax.dynamic_update_slice(lax.empty((NDEV,) + s.shape, s.dtype), s[None], (me,) + (0,) * s.ndim)
                 for s in shards]

        def start_own(src, land, waits, new):
            p = self._peers()
            for a in range(n):
                for k, to in ((0, p["sibling"]), (1, p["xn"]), (2, p["yn"])):
                    self._copy(land[a], new, a, k, 3, p["me"], to).start()

        self.sems, self.lands = {}, None
        new, self.lands = _comm_step("gather_start", start_own, lands, n_new=6 * n)
        self._keep(new, (0, 1, 2))

    @staticmethod
    def _peers():
        x, y, c = lax.axis_index("x"), lax.axis_index("y"), lax.axis_index("c")
        return dict(
            me=(x, y, c), sibling=(x, y, 1 - c), xn=(1 - x, y, c), yn=(x, 1 - y, c), dg=(1 - x, 1 - y, c),
            relay_origin=(jnp.bitwise_xor(x, c), jnp.bitwise_xor(y, 1 - c), c),
            relay_target=(jnp.bitwise_xor(x, 1 - c), jnp.bitwise_xor(y, c), c))

    def _keep(self, new, ks):
        half = len(new) // 2
        i = 0
        for a in range(self.n):
            for k in ks:
                self.sems[a, k] = (new[i], new[half + i])
                i += 1

    @staticmethod
    def _copy(land, sem_refs, a, k, nk, block, to, src=None, ks=None):
        ks = tuple(range(nk)) if ks is None else ks
        half = len(sem_refs) // 2
        i = a * len(ks) + ks.index(k)
        slot = land.at[4 * block[0] + 2 * block[1] + block[2]]
        return pltpu.make_async_remote_copy(
            src_ref=slot if src is None else src, dst_ref=slot, send_sem=sem_refs[i], recv_sem=sem_refs[half + i],
            device_id=to, device_id_type=pl.DeviceIdType.MESH)

    def _sem_list(self, ks):
        return ([self.sems[a, k][0] for a in range(self.n) for k in ks]
                + [self.sems[a, k][1] for a in range(self.n) for k in ks])

    def first_half(self, after):
        n = self.n

        def relay(src, land, waits, new):
            p = self._peers()
            for a in range(n):
                self._copy(land[a], waits, a, 1, 0, p["xn"], p["me"], ks=(1, 2)).wait_recv()
                self._copy(land[a], waits, a, 2, 0, p["yn"], p["me"], ks=(1, 2)).wait_recv()
                self._copy(land[a], new, a, 3, 0, p["relay_origin"], p["relay_target"], ks=(3, 4, 5)).start()
                self._copy(land[a], new, a, 4, 0, p["xn"], p["sibling"], ks=(3, 4, 5)).start()
                self._copy(land[a], new, a, 5, 0, p["yn"], p["sibling"], ks=(3, 4, 5)).start()

        new, self.lands = _comm_step("gather_relay", relay, self.lands, wait_sems=self._sem_list((1, 2)),
                                     n_new=6 * n, after=after)
        self._keep(new, (3, 4, 5))

        def from_sibling(src, land, waits, new):
            p = self._peers()
            other = lambda b: (b[0], b[1], 1 - b[2])
            for a in range(n):
                self._copy(land[a], waits, a, 0, 0, other(p["me"]), p["me"], ks=(0, 4, 5)).wait_recv()
                self._copy(land[a], waits, a, 4, 0, other(p["xn"]), p["me"], ks=(0, 4, 5)).wait_recv()
                self._copy(land[a], waits, a, 5, 0, other(p["yn"]), p["me"], ks=(0, 4, 5)).wait_recv()

        _, self.lands = _comm_step("gather_wait_sibling", from_sibling, self.lands,
                                   wait_sems=self._sem_list((0, 4, 5)))
        return self.lands[0].reshape(NW, D), self.x.astype(jnp.int32).reshape(1)

    def second_half(self, after):
        n = self.n

        def forward_diagonal(src, land, waits, new):
            p = self._peers()
            for a in range(n):
                self._copy(land[a], waits, a, 3, 0, p["dg"], p["me"], ks=(3,)).wait_recv()
                self._copy(land[a], new, a, 6, 0, p["dg"], p["sibling"], ks=(6,)).start()

        new, self.lands = _comm_step("gather_forward_diagonal", forward_diagonal, self.lands,
                                     wait_sems=self._sem_list((3,)), n_new=2 * n, after=after)
        self._keep(new, (6,))

        def finish(src, land, waits, new):
            p = self._peers()
            ks = tuple(range(7))
            for a in range(n):
                self._copy(land[a], waits, a, 6, 0, (p["dg"][0], p["dg"][1], 1 - p["dg"][2]), p["me"], ks=ks).wait_recv()
                for k in ks:
                    self._copy(land[a], waits, a, k, 0, p["me"], p["me"], ks=ks).wait_send()

        _, self.lands = _comm_step("gather_finish", finish, self.lands, wait_sems=self._sem_list(tuple(range(7))))
        return self.lands[0].reshape(NW, D), (1 - self.x).astype(jnp.int32).reshape(1)

    def rest(self):
        g_a, g_b, g_o, g_bm = self.lands[1:]
        return (g_a.transpose(1, 0, 2).reshape(512, D), g_b.transpose(1, 0, 2).reshape(512, D),
                g_bm.transpose(1, 0, 2).reshape(2, D), g_o.reshape(D, D))


def _sibling_exchange(g, name):
    def body(in_ref, out_ref, send_sems, recv_sems):
        x, y, c = lax.axis_index("x"), lax.axis_index("y"), lax.axis_index("c")
        copies = []
        for q in range(4):
            cp = pltpu.make_async_remote_copy(
                src_ref=in_ref.at[2 * q + (1 - c)], dst_ref=out_ref.at[q], send_sem=send_sems.at[q],
                recv_sem=recv_sems.at[q], device_id=(x, y, 1 - c), device_id_type=pl.DeviceIdType.MESH)
            cp.start()
            copies.append(cp)
        for cp in copies:
            cp.wait_recv()
        for cp in copies:
            cp.wait_send()

    return pl.pallas_call(
        body,
        in_specs=[pl.BlockSpec(memory_space=pl.ANY)],
        out_specs=pl.BlockSpec(memory_space=pl.ANY),
        out_shape=jax.ShapeDtypeStruct((4,) + g.shape[1:], g.dtype),
        scratch_shapes=[pltpu.SemaphoreType.DMA((4,)), pltpu.SemaphoreType.DMA((4,))],
        compiler_params=pltpu.CompilerParams(has_side_effects=True),
        name=name,
    )(g)


def _row_tile(rows, limit=256):
    fits = [t for t in range(16, limit + 1, 16) if rows % t == 0]
    return fits[-1] if fits else rows


def _pair_sum(g, r, core, name):
    _, rows, cols = g.shape
    tr = _row_tile(rows)

    def body(c_ref, g_ref, r_ref, o_ref):
        del c_ref
        o_ref[...] = (g_ref[...].astype(f32) + r_ref[...].astype(f32)).astype(bf16)

    return pl.pallas_call(
        body,
        grid_spec=pltpu.PrefetchScalarGridSpec(
            num_scalar_prefetch=1,
            grid=(4, rows // tr),
            in_specs=[pl.BlockSpec((None, tr, cols), lambda q, i, c_ref: (2 * q + c_ref[0], i, 0)),
                      pl.BlockSpec((None, tr, cols), lambda q, i, c_ref: (q, i, 0))],
            out_specs=pl.BlockSpec((None, tr, cols), lambda q, i, c_ref: (q, i, 0)),
        ),
        out_shape=jax.ShapeDtypeStruct((4, rows, cols), bf16),
        compiler_params=_params(("arbitrary", "arbitrary")),
        name=name,
    )(core, g, r)


def _scatter_start(chip_arrs, all_arrs, name):
    arrs = list(chip_arrs) + list(all_arrs)
    n, nc = len(arrs), len(chip_arrs)
    lands = [lax.empty(((3 if i < nc else NDEV - 1),) + a.shape[1:], a.dtype) for i, a in enumerate(arrs)]

    def body(*refs):
        src, land = refs[:n], refs[n:2 * n]
        send_sems, recv_sems = refs[2 * n:3 * n], refs[3 * n:4 * n]
        token = refs[6 * n]
        x, y, c = lax.axis_index("x"), lax.axis_index("y"), lax.axis_index("c")
        for a in range(n):
            for r in range(1, 4 if a < nc else NDEV):
                if a < nc:
                    px, py, pc = (1 - x if r & 2 else x), (1 - y if r & 1 else y), c
                    block = 2 * px + py
                else:
                    px, py, pc = (1 - x if r & 4 else x), (1 - y if r & 2 else y), (1 - c if r & 1 else c)
                    block = 4 * px + 2 * py + pc
                pltpu.make_async_remote_copy(
                    src_ref=src[a].at[block], dst_ref=land[a].at[r - 1], send_sem=send_sems[a],
                    recv_sem=recv_sems[a], device_id=(px, py, pc), device_id_type=pl.DeviceIdType.MESH).start()
        token[...] = jnp.zeros_like(token)

    hbm = [pltpu.HBM(a.shape, a.dtype) for a in arrs + lands]
    ops = [pltpu.with_memory_space_constraint(a, pltpu.HBM) for a in arrs + lands]
    outs = pl.pallas_call(
        body,
        out_shape=tuple([pltpu.SemaphoreType.DMA(())] * (2 * n) + hbm + [jax.ShapeDtypeStruct((8, LANES), f32)]),
        in_specs=[_HBM] * (2 * n),
        out_specs=tuple([_SEM] * (2 * n) + [_HBM] * (2 * n) + [pl.BlockSpec(memory_space=pltpu.VMEM)]),
        input_output_aliases={i: 2 * n + i for i in range(2 * n)},
        compiler_params=pltpu.CompilerParams(has_side_effects=_EFFECT),
        name=name,
    )(*ops)
    return outs[:n], outs[n:2 * n], outs[2 * n:3 * n], outs[3 * n:4 * n], outs[4 * n]


def _scatter_wait(send_sems, recv_sems, srcs, lands, after, name):
    n = len(srcs)

    def body(*refs):
        land = refs[n:2 * n]
        ssem, rsem = refs[2 * n:3 * n], refs[3 * n:4 * n]
        x, y, c = lax.axis_index("x"), lax.axis_index("y"), lax.axis_index("c")
        for a in range(n):
            done = pltpu.make_async_remote_copy(
                src_ref=land[a], dst_ref=land[a], send_sem=ssem[a], recv_sem=rsem[a], device_id=(x, y, c),
                device_id_type=pl.DeviceIdType.MESH)
            done.wait_send()
            done.wait_recv()

    hbm = [pltpu.HBM(a.shape, a.dtype) for a in list(srcs) + list(lands)]
    outs = pl.pallas_call(
        body,
        out_shape=tuple(hbm),
        in_specs=[_HBM] * (2 * n) + [_SEM] * (2 * n) + [pl.BlockSpec(memory_space=pl.ANY)],
        out_specs=tuple([_HBM] * (2 * n)),
        input_output_aliases={i: i for i in range(2 * n)},
        compiler_params=pltpu.CompilerParams(has_side_effects=_EFFECT),
        name=name,
    )(*srcs, *lands, *send_sems, *recv_sems, after)
    return outs[:n], outs[n:]


def _adam_update(g, w_ref, m_ref, v_ref, g_ref, d_ref, nm_ref, nv_ref):
    mm = ADAM_B1 * m_ref[...] + (1.0 - ADAM_B1) * g
    vv = ADAM_B2 * v_ref[...] + (1.0 - ADAM_B2) * (g * g)
    m_hat = mm / (1.0 - ADAM_B1 ** ADAM_STEP)
    v_hat = vv / (1.0 - ADAM_B2 ** ADAM_STEP)
    g_ref[...] = g
    d_ref[...] = -ADAM_LR * (m_hat / (jnp.sqrt(v_hat) + ADAM_EPS) + ADAM_WD * w_ref[...])
    nm_ref[...] = mm
    nv_ref[...] = vv


def _adamw_own(w, own, own_idx, slots, m, v, name):
    r, c = w.shape[-2:]
    tr = _row_tile(r, 128)
    k = slots.shape[0]

    def body(i_ref, w_ref, o_ref, s_ref, m_ref, v_ref, g_ref, d_ref, nm_ref, nv_ref):
        del i_ref
        g = o_ref[...].astype(f32)
        for j in range(k):
            g = g + s_ref[j].astype(f32)
        _adam_update(g, w_ref, m_ref, v_ref, g_ref, d_ref, nm_ref, nv_ref)

    blk = pl.BlockSpec((None, tr, c), lambda i, ix: (0, i, 0))
    return pl.pallas_call(
        body,
        grid_spec=pltpu.PrefetchScalarGridSpec(
            num_scalar_prefetch=1,
            grid=(r // tr,),
            in_specs=[blk, pl.BlockSpec((None, tr, c), lambda i, ix: (ix[0], i, 0)),
                      pl.BlockSpec((k, tr, c), lambda i, ix: (0, i, 0)), blk, blk],
            out_specs=[blk] * 4,
        ),
        out_shape=[jax.ShapeDtypeStruct(w.shape, f32)] * 4,
        compiler_params=_params(("arbitrary",)),
        name=name,
    )(own_idx, w, own, slots, m, v)


def _adamw(w, slots, m, v, name):
    r, c = w.shape[-2:]
    tr = _row_tile(r, 128)

    def body(w_ref, s_ref, m_ref, v_ref, g_ref, d_ref, nm_ref, nv_ref):
        g = s_ref[0].astype(f32)
        for k in range(1, NDEV):
            g = g + s_ref[k].astype(f32)
        _adam_update(g, w_ref, m_ref, v_ref, g_ref, d_ref, nm_ref, nv_ref)

    if w.ndim == 3:
        blk = pl.BlockSpec((None, tr, c), lambda i: (0, i, 0))
    else:
        blk = pl.BlockSpec((tr, c), lambda i: (i, 0))
    return pl.pallas_call(
        body,
        grid=(r // tr,),
        in_specs=[blk, pl.BlockSpec((NDEV, tr, c), lambda i: (0, i, 0)), blk, blk],
        out_specs=[blk] * 4,
        out_shape=[jax.ShapeDtypeStruct(w.shape, f32)] * 4,
        compiler_params=_params(("arbitrary",)),
        name=name,
    )(w, slots, m, v)


class _Weights:
    def __init__(self, w_t, w_a, w_b, b_merge, w_o):
        self._w_t, self._rest = w_t, (w_a, w_b, b_merge, w_o)

    def first_half(self, after):
        del after
        return self._w_t, jnp.zeros((1,), jnp.int32)

    def second_half(self, after):
        del after
        return self._w_t, jnp.ones((1,), jnp.int32)

    def rest(self):
        return self._rest


def _local_step(x, tgt, norm_gain, weights, qn_a, kn_a, qn_b, kn_b, sink_a, rel_bias, on_weight_grads=None):
    two = lambda t: jnp.concatenate([t, t], axis=-1).reshape(1, LANES)
    ones = jnp.ones((1, LANES), f32)
    gains = jnp.stack([
        jnp.stack([two(qn_a), two(kn_a), ones]),
        jnp.stack([two(qn_b), two(kn_b), ones]),
        jnp.stack([two(qn_b), two(kn_b), ones]),
        jnp.stack([two(qn_b), two(kn_b), ones]),
    ])
    buckets = [jnp.asarray(_bucket_np(blk, d)) for blk, d, _ in GROUPS]
    bias = [_bias_expand(rel_bias, buckets[k], GROUPS[k][2], "bias_expand_%d" % k) for k in range(4)]

    hb, hbt, rstd = _rms(x, norm_gain)
    w_t, half = weights.first_half([hb] + bias)
    proj = _inproj_half(hb, w_t, half, None, "inproj_1")
    w_t, half = weights.second_half([proj])
    proj = _inproj_half(hb, w_t, half, proj, "inproj_2")
    w_a, w_b, b_merge, w_o = weights.rest()
    gl = _prep(proj, gains)
    o_a, l_a = _attn_fwd(gl, bias[0], sink_a.reshape(8), 0, 128, 1, "attn_fwd_a")
    fwd_b = [_attn_fwd(gl, bias[k], None, k, GROUPS[k][0], GROUPS[k][1], "attn_fwd_b%d" % k) for k in (1, 2, 3)]
    sink_b = jnp.repeat(sink_a.reshape(8), HD).reshape(1, 512)

    (dy, dyb, dproj, do_a, dd_a, do_b0, do_b1, do_b2, dd_b0, dd_b1, dd_b2, ya, yb, mg, dbr_a, dbr_b, loss, dbm,
     dsk) = _tail(x, tgt, o_a, l_a, [f[0] for f in fwd_b], [f[1] for f in fwd_b], proj, b_merge, w_a, w_b, w_o, sink_b)

    dw_o = _matmul_tokens(mg, dyb, "dw_out")
    dw_a = _matmul_tokens(ya, dbr_a, "dw_branch_a")
    dw_b = _matmul_tokens(yb, dbr_b, "dw_branch_b")
    if on_weight_grads is not None:
        early = on_weight_grads(dict(w_branch_a=dw_a, w_branch_b=dw_b, b_merge=dbm, w_out=dw_o))
        buckets = [buckets[0] + early.astype(jnp.int32)] + buckets[1:]

    dqkv_a, dbk_a = _attn_bwd(gl, bias[0], buckets[0], do_a, l_a, dd_a, 0, 128, 1, "attn_bwd_a")
    dproj, dg_a = _post_a(dqkv_a, proj, gains, dproj)
    dbk_b, dg_b = [], []
    for k, do_k, dd_k in ((1, do_b0, dd_b0), (2, do_b1, dd_b1), (3, do_b2, dd_b2)):
        dqkv, dbk = _attn_bwd(gl, bias[k], buckets[k], do_k, fwd_b[k - 1][1], dd_k, k, GROUPS[k][0], GROUPS[k][1],
                              "attn_bwd_b%d" % k)
        dproj, dg = _post_b(k, dqkv, proj, gains, dproj)
        dbk_b.append(dbk)
        dg_b.append(dg)
    dg_b = jnp.stack(dg_b)

    dw_in = _dw_in(hbt, dproj)
    token = jnp.zeros((), f32) if on_weight_grads is None else on_weight_grads(dict(w_in=dw_in))
    grad_x, d_norm_gain = _dh_norm_bwd(dproj, w_t, x, rstd, norm_gain + token, dy)

    fold = lambda t: t[..., :HD] + t[..., HD:]
    d_qn_a = fold(dg_a[0, 0])
    d_kn_a = fold(dg_a[1, 0])
    d_qn_b = fold(dg_b[:, 0, 0].sum(axis=0))
    d_kn_b = fold(dg_b[:, 1, 0].sum(axis=0))
    d_sink = dsk.reshape(8, HD)[:, 0]
    red = jnp.stack([dbk_a] + dbk_b)
    d_rel = red[:, :, 0, :32].reshape(32, 32).T
    return dict(loss=loss, grad_x=grad_x, norm_gain=d_norm_gain, w_in=dw_in, q_norm_a=d_qn_a, k_norm_a=d_kn_a,
                q_norm_b=d_qn_b, k_norm_b=d_kn_b, sink_a=d_sink, rel_bias=d_rel, w_branch_a=dw_a, w_branch_b=dw_b,
                b_merge=dbm, w_out=dw_o)


SMALL = (("norm_gain", D), ("q_norm_a", HD), ("k_norm_a", HD), ("q_norm_b", HD), ("k_norm_b", HD), ("sink_a", 8),
         ("rel_bias", 1024))
SMALL_PAD = 2432


SMALL_USED = sum(sz for _, sz in SMALL)


def _pack_small(parts, loss=None):
    tail = jnp.zeros((SMALL_PAD - SMALL_USED,), f32)
    if loss is not None:
        tail = tail.at[0].set(loss.reshape(()))
    return jnp.concatenate([parts[n].reshape(-1) for n, _ in SMALL] + [tail]).reshape(1, SMALL_PAD)


def _unpack_small(flat, shapes):
    out, off = {}, 0
    for n, sz in SMALL:
        out[n] = flat[0, off:off + sz].reshape(shapes[n])
        off += sz
    return out


def kernel(x, norm_gain, w_in, q_norm_a, k_norm_a, q_norm_b, k_norm_b, sink_a, rel_bias, w_branch_a, w_branch_b, b_merge, w_out, loss_target, m_norm_gain, m_w_in, m_q_norm_a, m_k_norm_a, m_q_norm_b, m_k_norm_b, m_sink_a, m_rel_bias, m_w_branch_a, m_w_branch_b, m_b_merge, m_w_out, v_norm_gain, v_w_in, v_q_norm_a, v_k_norm_a, v_q_norm_b, v_k_norm_b, v_sink_a, v_rel_bias, v_w_branch_a, v_w_branch_b, v_b_merge, v_w_out):
    csh = D // NDEV
    w_in_t, m_w_in_t, v_w_in_t = (jnp.swapaxes(t, 1, 2) for t in (w_in, m_w_in, v_w_in))
    weights = _GatheredWeights([w_in_t[0].astype(bf16), w_branch_a[0].astype(bf16), w_branch_b[0].astype(bf16),
                                w_out[0].astype(bf16), b_merge[0]])

    pending = {}
    core = lax.axis_index("c").astype(jnp.int32).reshape(1)
    chip = (2 * lax.axis_index("x") + lax.axis_index("y")).astype(jnp.int32).reshape(1)
    me = (2 * chip + core).astype(jnp.int32)

    def start_exchange(gw):
        if "w_in" in gw:
            from_sibling = _sibling_exchange(gw["w_in"], "grad_sibling_exchange")
            chip_sums = _pair_sum(gw["w_in"], from_sibling, core, "grad_pair_sum")
            pending["w_in"] = _scatter_start([chip_sums], [], "scatter_w_in_start")
            return pending["w_in"][4][0, 0]
        blocks = [gw["w_branch_a"].reshape(512, NDEV, csh).transpose(1, 0, 2).astype(bf16),
                  gw["w_branch_b"].reshape(512, NDEV, csh).transpose(1, 0, 2).astype(bf16),
                  gw["w_out"].reshape(NDEV, csh, D).astype(bf16),
                  gw["b_merge"].reshape(2, NDEV, csh).transpose(1, 0, 2)]
        pending["rest"] = _scatter_start([], blocks, "scatter_rest_start")
        return pending["rest"][4][0, 0]

    loc = _local_step(x[0], loss_target[0], norm_gain, weights, q_norm_a, k_norm_a, q_norm_b, k_norm_b, sink_a,
                      rel_bias, on_weight_grads=start_exchange)

    small_shapes = dict(norm_gain=(1, D), q_norm_a=(1, HD), k_norm_a=(1, HD), q_norm_b=(1, HD), k_norm_b=(1, HD),
                        sink_a=(1, 8), rel_bias=(32, 32))
    (r_small,) = _exchange([], [_pack_small(loc, loc["loss"])], "gather_small_grads")
    send_sems, recv_sems, srcs, lands, _ = pending["rest"]
    (s_a, s_b, s_o, s_bm), (r_a, r_b, r_o, r_bm) = _scatter_wait(
        send_sems, recv_sems, srcs, lands, r_small, "scatter_rest_wait")
    send_sems, recv_sems, srcs, lands, _ = pending["w_in"]
    (s_in,), (r_in,) = _scatter_wait(send_sems, recv_sems, srcs, lands, r_small, "scatter_w_in_wait")

    given = dict(norm_gain=norm_gain, q_norm_a=q_norm_a, k_norm_a=k_norm_a, q_norm_b=q_norm_b, k_norm_b=k_norm_b,
                 sink_a=sink_a, rel_bias=rel_bias)
    m_small = dict(norm_gain=m_norm_gain, q_norm_a=m_q_norm_a, k_norm_a=m_k_norm_a, q_norm_b=m_q_norm_b,
                   k_norm_b=m_k_norm_b, sink_a=m_sink_a, rel_bias=m_rel_bias)
    v_small = dict(norm_gain=v_norm_gain, q_norm_a=v_q_norm_a, k_norm_a=v_k_norm_a, q_norm_b=v_q_norm_b,
                   k_norm_b=v_k_norm_b, sink_a=v_sink_a, rel_bias=v_rel_bias)
    res = {
        "small": _adamw(_pack_small(given), r_small, _pack_small(m_small), _pack_small(v_small), "adamw_small"),
        "w_in": [jnp.swapaxes(t, 1, 2) for t in
                 _adamw_own(w_in_t, s_in, chip, r_in, m_w_in_t, v_w_in_t, "adamw_w_in")],
        "w_branch_a": _adamw_own(w_branch_a, s_a, me, r_a, m_w_branch_a, v_w_branch_a, "adamw_w_branch_a"),
        "w_branch_b": _adamw_own(w_branch_b, s_b, me, r_b, m_w_branch_b, v_w_branch_b, "adamw_w_branch_b"),
        "b_merge": _adamw_own(b_merge, s_bm, me, r_bm, m_b_merge, v_b_merge, "adamw_b_merge"),
        "w_out": _adamw_own(w_out, s_o, me, r_o, m_w_out, v_w_out, "adamw_w_out"),
    }
    order = ["norm_gain", "w_in", "q_norm_a", "k_norm_a", "q_norm_b", "k_norm_b", "sink_a", "rel_bias", "w_branch_a",
             "w_branch_b", "b_merge", "w_out"]
    outs = []
    for k in range(4):
        small = _unpack_small(res["small"][k], small_shapes)
        for n in order:
            outs.append(small[n] if n in small else res[n][k])
    loss = res["small"][0][0, SMALL_USED]
    return (loss, loc["grad_x"][None], *outs)
```

```python
import math

import numpy as np
import jax
import jax.numpy as jnp
from jax import lax
from jax.experimental import pallas as pl
from jax.experimental.pallas import tpu as pltpu

f32 = jnp.float32
bf16 = jnp.bfloat16

S = 4096
D = 1024
NA = 5376
NT = 3072
NW = NA + NT
WSH = NW // 8
HD = 64
LANES = 128
EPS = 1e-6
NEG = -1e30
SCALE = HD ** -0.5
TQ = 128
PAD = 128
SP = S + 2 * PAD
NDEV = 8
GROUPS = ((128, 1, 0), (64, 1, 8), (64, 4, 16), (64, 16, 24))
CHUNK = 256
PCHUNK = 128
RC = 64
TN = 768

ADAM_LR, ADAM_B1, ADAM_B2, ADAM_EPS, ADAM_WD, ADAM_STEP = 0.001, 0.9, 0.999, 1e-08, 0.01, 10

MIB = 1024 * 1024
NT_DIMS = (((1,), (1,)), ((), ()))
TN_DIMS = (((0,), (0,)), ((), ()))


def _params(sem=None, vmem_mib=48):
    return pltpu.CompilerParams(dimension_semantics=sem, vmem_limit_bytes=vmem_mib * MIB)


def _lo():
    return lax.broadcasted_iota(jnp.int32, (1, LANES), 1) < HD


def _head_ones():
    r = lax.broadcasted_iota(jnp.int32, (LANES, LANES), 0) // HD
    c = lax.broadcasted_iota(jnp.int32, (LANES, LANES), 1) // HD
    return jnp.where(r == c, 1.0, 0.0).astype(bf16)


def _half_sums(x, ones):
    hi = x.astype(bf16)
    mid = (x - hi.astype(f32)).astype(bf16)
    return (jnp.dot(hi, ones, preferred_element_type=f32) + jnp.dot(mid, ones, preferred_element_type=f32))


def _seg_sum(x, ones):
    outs = [_half_sums(x[:, b * LANES:(b + 1) * LANES], ones) for b in range(x.shape[1] // LANES)]
    return outs[0] if len(outs) == 1 else jnp.concatenate(outs, axis=1)


def _bucket_np(blk, stride):
    w = TQ + 2 * blk
    rel = np.arange(w)[None, :] - blk - np.arange(TQ)[:, None]
    band = np.abs(rel) <= blk
    r = rel * stride
    n = np.abs(r)
    nf = np.maximum(n, 8).astype(np.float32)
    large = 8 + (np.log(nf / np.float32(8)) / np.float32(math.log(128.0)) * np.float32(8)).astype(np.int32)
    large = np.minimum(large, 15)
    b = (r > 0).astype(np.int32) * 16 + np.where(n < 8, n, large)
    return np.where(band, b, -1).astype(np.int32)


def _rms(x, gain):
    ts = 512

    def body(x_ref, g_ref, h_ref, ht_ref, r_ref):
        xv = x_ref[...]
        r = lax.rsqrt(jnp.mean(xv * xv, axis=-1, keepdims=True) + EPS)
        h = (xv * r) * g_ref[...]
        h_ref[...] = h.astype(bf16)
        ht_ref[...] = h.T.astype(bf16)
        r_ref[...] = r

    return pl.pallas_call(
        body,
        grid=(S // ts,),
        in_specs=[pl.BlockSpec((ts, D), lambda i: (i, 0)), pl.BlockSpec((1, D), lambda i: (0, 0))],
        out_specs=[pl.BlockSpec((ts, D), lambda i: (i, 0)), pl.BlockSpec((D, ts), lambda i: (0, i)),
                   pl.BlockSpec((ts, 1), lambda i: (i, 0))],
        out_shape=[jax.ShapeDtypeStruct((S, D), bf16), jax.ShapeDtypeStruct((D, S), bf16),
                   jax.ShapeDtypeStruct((S, 1), f32)],
        compiler_params=_params(("arbitrary",)),
        name="rms",
    )(x, gain)


def _inproj_half(hb, w_t, half, proj, name):
    ts = 1024
    tn = NW // 6
    per = NW // 2 // tn

    def body(h_idx, h_ref, w_ref, *rest):
        del h_idx
        rest[-1][...] = lax.dot_general(h_ref[...], w_ref[...], NT_DIMS, preferred_element_type=f32)

    in_specs = [pl.BlockSpec((ts, D), lambda i, n, hf: (i, 0)),
                pl.BlockSpec((tn, D), lambda i, n, hf: (hf[0] * per + n, 0))]
    args = [half, hb, w_t]
    aliases = {}
    if proj is not None:
        in_specs.append(pl.BlockSpec(memory_space=pl.ANY))
        args.append(proj)
        aliases = {3: 0}
    return pl.pallas_call(
        body,
        grid_spec=pltpu.PrefetchScalarGridSpec(
            num_scalar_prefetch=1,
            grid=(S // ts, per),
            in_specs=in_specs,
            out_specs=pl.BlockSpec((ts, tn), lambda i, n, hf: (i, hf[0] * per + n)),
        ),
        out_shape=jax.ShapeDtypeStruct((S, NW), f32),
        input_output_aliases=aliases,
        compiler_params=_params(("arbitrary", "arbitrary")),
        name=name,
    )(*args)


def _bias_expand(table, bucket, c0, name):
    tq, w = bucket.shape
    blk = (w - tq) // 2

    def body(tab_ref, bk_ref, o_ref):
        h = pl.program_id(0)
        bk = bk_ref[...]

        def step(b, acc):
            return jnp.where(bk == b, tab_ref[b, c0 + h], acc)

        inner = lax.fori_loop(0, 32, step, jnp.full((tq, w), NEG, f32))
        col = lax.broadcasted_iota(jnp.int32, (1, w), 1)
        o_ref[0] = jnp.where(col < blk, NEG, inner)
        o_ref[1] = inner
        o_ref[2] = jnp.where(col >= tq + blk, NEG, inner)

    return pl.pallas_call(
        body,
        grid=(8,),
        in_specs=[pl.BlockSpec(memory_space=pltpu.SMEM), pl.BlockSpec((tq, w), lambda h: (0, 0))],
        out_specs=pl.BlockSpec((3, None, tq, w), lambda h: (0, h, 0, 0)),
        out_shape=jax.ShapeDtypeStruct((3, 8, tq, w), f32),
        compiler_params=_params(("arbitrary",)),
        name=name,
    )(table, bucket)


def _tile_kind(t, seq):
    m0 = jnp.bitwise_and(t * TQ, seq - 1)
    return jnp.where(m0 == 0, 0, jnp.where(m0 == seq - TQ, 2, 1))


def _col_block(g, j):
    kind = j // 4
    hp = j % 4
    a = jnp.where(kind == 0, hp, 3 + kind)
    b = 6 + 12 * kind + 4 * (g - 1) + hp
    return jnp.where(g == 0, a, b)


def _prep(proj_a, gains):
    def body(p_ref, g_ref, o_ref):
        g = pl.program_id(0)
        j = pl.program_id(1)
        kind = j // 4
        lo = _lo()
        ones = _head_ones()
        half = jnp.where(lo, 0, 1)
        take = (kind == 0) | (half == (j % 4) // 2)
        gain = g_ref[...]
        o_ref[0:PAD, :] = jnp.zeros((PAD, LANES), bf16)
        o_ref[PAD + S:SP, :] = jnp.zeros((PAD, LANES), bf16)

        def norm_store(xv, dst, dup):
            if dup:
                xv = jnp.where(take, xv, pltpu.roll(xv, HD, 1))
            r = lax.rsqrt(_half_sums(xv * xv, ones) * (1.0 / HD) + EPS)
            r = jnp.where(kind == 2, 1.0, r)
            yv = (xv * r) * gain
            yv = jnp.where(kind == 0, yv * SCALE, yv)
            o_ref[PAD + dst:PAD + dst + CHUNK, :] = yv.astype(bf16)

        for gi, (_, d, _) in enumerate(GROUPS):
            @pl.when(g == gi)
            def _():
                seq = S // d
                for c in range(d):
                    for i in range(seq // CHUNK):
                        if d == 1:
                            xv = p_ref[i * CHUNK:(i + 1) * CHUNK, :]
                        else:
                            xv = p_ref[pl.ds(c + i * CHUNK * d, CHUNK, stride=d), :]
                        norm_store(xv, c * seq + i * CHUNK, gi == 0)

    return pl.pallas_call(
        body,
        grid=(4, 12),
        in_specs=[
            pl.BlockSpec((S, LANES), lambda g, j: (0, _col_block(g, j))),
            pl.BlockSpec((None, None, 1, LANES), lambda g, j: (g, j // 4, 0, 0)),
        ],
        out_specs=pl.BlockSpec((None, None, SP, LANES), lambda g, j: (g, j, 0, 0)),
        out_shape=jax.ShapeDtypeStruct((4, 12, SP, LANES), bf16),
        compiler_params=_params(("arbitrary", "arbitrary")),
        name="prep",
    )(proj_a, gains)


def _token_rows(t, r0, n, d):
    if d == 1:
        return pl.ds(pl.multiple_of(t * TQ, TQ) + r0, n)
    per = S // d // TQ
    return pl.ds(((t % per) * TQ + r0) * d + t // per, n, stride=d)


def _stack_heads(t, lo):
    z = jnp.zeros_like(t)
    return jnp.concatenate([jnp.where(lo, t, z), jnp.where(lo, z, t)], axis=0)


def _unstack_heads(t2, lo):
    return jnp.where(lo, t2[:TQ], t2[TQ:])


def _attn_fwd(gl, bias, sink, g, blk, d, name):
    w = TQ + 2 * blk
    seq = S // d
    use_sink = sink is not None

    def body(*refs):
        if use_sink:
            sink_ref, q_ref, k_ref, v_ref, b_ref, o_ref, l_ref, s0, s1, p0, p1, lse_scr = refs
        else:
            q_ref, k_ref, v_ref, b_ref, o_ref, l_ref, s0, s1, p0, p1, lse_scr = refs
        hp = pl.program_id(0)
        lo = _lo()
        s_bufs, p_bufs = (s0, s1), (p0, p1)

        def scores(p, slot):
            for u in range(2):
                f0 = pl.multiple_of((2 * p + u) * TQ, TQ)
                q2 = _stack_heads(q_ref[pl.ds(PAD + f0, TQ), :], lo)
                kw = k_ref[pl.ds(PAD - blk + f0, w), :]
                s_bufs[slot][u] = lax.dot_general(q2, kw, NT_DIMS, preferred_element_type=f32)

        def softmax(p, slot):
            for u in range(2):
                t = 2 * p + u
                kind = _tile_kind(t, seq)
                for h in range(2):
                    for r in range(TQ // RC):
                        rows = slice(h * TQ + r * RC, h * TQ + (r + 1) * RC)
                        logit = s_bufs[slot][u, rows, :] + b_ref[kind, h, r * RC:(r + 1) * RC, :]
                        m = jnp.max(logit, axis=1, keepdims=True)
                        e = jnp.exp(logit - m)
                        lse = m + jnp.log(jnp.sum(e, axis=1, keepdims=True))
                        if use_sink:
                            sk = sink_ref[2 * hp + h]
                            mx = jnp.maximum(lse, sk)
                            lse = mx + jnp.log(jnp.exp(lse - mx) + jnp.exp(sk - mx))
                        p_bufs[slot][u, rows, :] = (e * jnp.exp(m - lse)).astype(bf16)
                        lse_scr[u, rows, :] = jnp.broadcast_to(lse, (RC, LANES))
                l_ref[_token_rows(t, 0, TQ, d), :] = jnp.where(lo, lse_scr[u, 0:TQ, :], lse_scr[u, TQ:2 * TQ, :])

        def values(p, slot):
            for u in range(2):
                t = 2 * p + u
                vw = v_ref[pl.ds(PAD - blk + pl.multiple_of(t * TQ, TQ), w), :]
                o2 = jnp.dot(p_bufs[slot][u], vw, preferred_element_type=f32)
                o_ref[_token_rows(t, 0, TQ, d), :] = _unstack_heads(o2, lo)

        npair = S // TQ // 2
        scores(0, 0)
        scores(1, 1)
        softmax(0, 0)

        def steady(k, carry):
            p = 2 * k + 2
            scores(p, 0)
            softmax(p - 1, 1)
            values(p - 2, 0)
            scores(p + 1, 1)
            softmax(p, 0)
            values(p - 1, 1)
            return carry

        lax.fori_loop(0, (npair - 2) // 2, steady, 0)
        softmax(npair - 1, 1)
        values(npair - 2, 0)
        values(npair - 1, 1)

    in_specs = [
        pl.BlockSpec((None, None, SP, LANES), lambda hp: (g, hp, 0, 0)),
        pl.BlockSpec((None, None, SP, LANES), lambda hp: (g, 4 + hp, 0, 0)),
        pl.BlockSpec((None, None, SP, LANES), lambda hp: (g, 8 + hp, 0, 0)),
        pl.BlockSpec((3, 2, TQ, w), lambda hp: (0, hp, 0, 0)),
    ]
    args = [gl, gl, gl, bias]
    if use_sink:
        in_specs = [pl.BlockSpec(memory_space=pltpu.SMEM)] + in_specs
        args = [sink] + args
    out = pl.BlockSpec((S, LANES), lambda hp: (0, hp))
    return pl.pallas_call(
        body,
        grid=(4,),
        in_specs=in_specs,
        out_specs=[out, out],
        out_shape=[jax.ShapeDtypeStruct((S, 4 * LANES), f32)] * 2,
        scratch_shapes=[pltpu.VMEM((2, 2 * TQ, w), f32), pltpu.VMEM((2, 2 * TQ, w), f32),
                        pltpu.VMEM((2, 2 * TQ, w), bf16), pltpu.VMEM((2, 2 * TQ, w), bf16),
                        pltpu.VMEM((2, 2 * TQ, LANES), f32)],
        compiler_params=_params(("arbitrary",)),
        name=name,
    )(*args)


def _attn_bwd(gl, bias, bucket, do, lse, dd, g, blk, d, name):
    w = TQ + 2 * blk
    seq = S // d

    def body(q_ref, k_ref, v_ref, b_ref, bk_ref, do_ref, l_ref, d_ref, dqkv_ref, dbk_ref,
             db_acc, s0, s1, dp0, dp1, pb0, pb1, ds0, ds1, dk_acc, dv_acc):
        lo = _lo()
        hi = jnp.logical_not(lo)
        dk_acc[...] = jnp.zeros((SP, LANES), f32)
        dv_acc[...] = jnp.zeros((SP, LANES), f32)
        db_acc[...] = jnp.zeros((2 * TQ, w), f32)
        s_bufs, dp_bufs, pb_bufs, ds_bufs = (s0, s1), (dp0, dp1), (pb0, pb1), (ds0, ds1)

        def stacked(t):
            f0 = pl.multiple_of(t * TQ, TQ)
            q2 = _stack_heads(q_ref[pl.ds(PAD + f0, TQ), :], lo)
            do2 = _stack_heads(do_ref[_token_rows(t, 0, TQ, d), :].astype(bf16), lo)
            return f0, q2, do2

        def scores(p, slot):
            for u in range(2):
                f0, q2, do2 = stacked(2 * p + u)
                win = pl.ds(PAD - blk + f0, w)
                s_bufs[slot][u] = lax.dot_general(q2, k_ref[win, :], NT_DIMS, preferred_element_type=f32)
                dp_bufs[slot][u] = lax.dot_general(do2, v_ref[win, :], NT_DIMS, preferred_element_type=f32)

        def grads(p, slot):
            for u in range(2):
                t = 2 * p + u
                kind = _tile_kind(t, seq)
                for h in range(2):
                    msk = lo if h == 0 else hi
                    for r in range(TQ // RC):
                        rows = slice(h * TQ + r * RC, h * TQ + (r + 1) * RC)
                        src = _token_rows(t, r * RC, RC, d)
                        lh = jnp.max(jnp.where(msk, l_ref[src, :], -jnp.inf), axis=1, keepdims=True)
                        dh = jnp.max(jnp.where(msk, d_ref[src, :], -jnp.inf), axis=1, keepdims=True)
                        logit = s_bufs[slot][u, rows, :] + b_ref[kind, h, r * RC:(r + 1) * RC, :]
                        pr = jnp.exp(logit - lh)
                        ds = pr * (dp_bufs[slot][u, rows, :] - dh)
                        db_acc[rows, :] += ds
                        pb_bufs[slot][u, rows, :] = pr.astype(bf16)
                        ds_bufs[slot][u, rows, :] = ds.astype(bf16)

        def accumulate(p, slot):
            for u in range(2):
                f0, q2, do2 = stacked(2 * p + u)
                win = pl.ds(PAD - blk + f0, w)
                dsb = ds_bufs[slot][u]
                dq2 = jnp.dot(dsb, k_ref[win, :], preferred_element_type=f32)
                dqkv_ref[0, pl.ds(PAD + f0, TQ), :] = _unstack_heads(dq2, lo).astype(bf16)
                dk_acc[win, :] += lax.dot_general(dsb, q2, TN_DIMS, preferred_element_type=f32)
                dv_acc[win, :] += lax.dot_general(pb_bufs[slot][u], do2, TN_DIMS, preferred_element_type=f32)

        npair = S // TQ // 2
        scores(0, 0)
        scores(1, 1)
        grads(0, 0)

        def steady(k, carry):
            p = 2 * k + 2
            scores(p, 0)
            grads(p - 1, 1)
            accumulate(p - 2, 0)
            scores(p + 1, 1)
            grads(p, 0)
            accumulate(p - 1, 1)
            return carry

        lax.fori_loop(0, (npair - 2) // 2, steady, 0)
        grads(npair - 1, 1)
        accumulate(npair - 2, 0)
        accumulate(npair - 1, 1)
        for i in range(SP // CHUNK):
            rows = slice(i * CHUNK, (i + 1) * CHUNK)
            dqkv_ref[1, rows, :] = dk_acc[rows, :].astype(bf16)
            dqkv_ref[2, rows, :] = dv_acc[rows, :].astype(bf16)

        bk = bk_ref[...]
        lane = lax.broadcasted_iota(jnp.int32, (8, LANES), 1)
        for h in range(2):
            db = db_acc[h * TQ:(h + 1) * TQ, :]
            acc = jnp.zeros((8, LANES), f32)
            for b in range(32):
                part = jnp.where(bk == b, db, 0.0).reshape(TQ // 8, 8, w).sum(axis=0)
                tot = jnp.sum(jnp.sum(part, axis=1, keepdims=True), axis=0, keepdims=True)
                acc = jnp.where(lane == b, tot, acc)
            dbk_ref[h] = acc

    def gcol(off):
        return pl.BlockSpec((None, None, SP, LANES), lambda hp: (g, off + hp, 0, 0))

    row = pl.BlockSpec((S, LANES), lambda hp: (0, hp))
    return pl.pallas_call(
        body,
        grid=(4,),
        in_specs=[gcol(0), gcol(4), gcol(8), pl.BlockSpec((3, 2, TQ, w), lambda hp: (0, hp, 0, 0)),
                  pl.BlockSpec((TQ, w), lambda hp: (0, 0)), row, row, row],
        out_specs=[pl.BlockSpec((3, None, SP, LANES), lambda hp: (0, hp, 0, 0)),
                   pl.BlockSpec((2, 8, LANES), lambda hp: (hp, 0, 0))],
        out_shape=[
            jax.ShapeDtypeStruct((3, 4, SP, LANES), bf16),
            jax.ShapeDtypeStruct((8, 8, LANES), f32),
        ],
        scratch_shapes=([pltpu.VMEM((2 * TQ, w), f32)] + [pltpu.VMEM((2, 2 * TQ, w), f32)] * 4
                        + [pltpu.VMEM((2, 2 * TQ, w), bf16)] * 4 + [pltpu.VMEM((SP, LANES), f32)] * 2),
        compiler_params=_params(("arbitrary",), vmem_mib=56),
        name=name,
    )(gl, gl, gl, bias, bucket, do, lse, dd)


def _sigmoid(z):
    return 1.0 / (1.0 + jnp.exp(-z))


def _tail(x, tgt, o_a, l_a, o_b, l_b, proj, bm, w_a, w_b, w_o, sink_b):
    ts = 256

    def body(x_ref, t_ref, oa_ref, la_ref, ob0_ref, ob1_ref, ob2_ref, lb0_ref, lb1_ref, lb2_ref,
             ga_ref, gb_ref, m0_ref, m1_ref, bm_ref, wa_ref, wb_ref, wo_ref, sk_ref,
             dy_ref, dyb_ref, dt_ref, doa_ref, dda_ref, dob0_ref, dob1_ref, dob2_ref, ddb0_ref, ddb1_ref, ddb2_ref,
             ya_ref, yb_ref, mg_ref, dbra_ref, dbrb_ref, loss_ref, dbm_ref, dsk_ref):
        i = pl.program_id(0)

        @pl.when(i == 0)
        def _():
            loss_ref[...] = jnp.zeros_like(loss_ref)
            dbm_ref[...] = jnp.zeros_like(dbm_ref)
            dsk_ref[...] = jnp.zeros_like(dsk_ref)

        ga = ga_ref[...]
        sa = _sigmoid(ga)
        silu_a = ga * sa
        oa = oa_ref[...]
        ya = oa * silu_a
        gb = gb_ref[...]
        sb = _sigmoid(gb)
        silu_b = gb * sb
        ob = [ob0_ref[...], ob1_ref[...], ob2_ref[...]]
        lb = [lb0_ref[...], lb1_ref[...], lb2_ref[...]]
        mx = jnp.maximum(jnp.maximum(lb[0], lb[1]), lb[2])
        ex = [jnp.exp(v - mx) for v in lb]
        den = ex[0] + ex[1] + ex[2]
        alpha = [e / den for e in ex]
        ybc = alpha[0] * ob[0] + alpha[1] * ob[1] + alpha[2] * ob[2]
        yb = ybc * silu_b
        yab = ya.astype(bf16)
        ybb = yb.astype(bf16)
        br_a = jnp.dot(yab, wa_ref[...], preferred_element_type=f32)
        br_b = jnp.dot(ybb, wb_ref[...], preferred_element_type=f32)
        g0 = _sigmoid(m0_ref[...] + bm_ref[0:1, :])
        g1 = _sigmoid(m1_ref[...] + bm_ref[1:2, :])
        merged = g0 * br_a + g1 * br_b
        mgb = merged.astype(bf16)
        y = x_ref[...] + jnp.dot(mgb, wo_ref[...], preferred_element_type=f32)
        err = y - t_ref[...]
        part = jnp.sum(jnp.sum(err * err, axis=1, keepdims=True), axis=0, keepdims=True)
        loss_ref[...] += part * (0.5 / D)
        dy = err * (1.0 / D)
        dyb = dy.astype(bf16)
        dmerged = lax.dot_general(dyb, wo_ref[...], NT_DIMS, preferred_element_type=f32)
        dbr_a = (dmerged * g0).astype(bf16)
        dbr_b = (dmerged * g1).astype(bf16)
        dm0 = dmerged * br_a * (g0 * (1.0 - g0))
        dm1 = dmerged * br_b * (g1 * (1.0 - g1))
        dbm_ref[0:1, :] += jnp.sum(dm0, axis=0, keepdims=True)
        dbm_ref[1:2, :] += jnp.sum(dm1, axis=0, keepdims=True)
        dya = lax.dot_general(dbr_a, wa_ref[...], NT_DIMS, preferred_element_type=f32)
        dyb2 = lax.dot_general(dbr_b, wb_ref[...], NT_DIMS, preferred_element_type=f32)
        do_a = dya * silu_a
        dga = dya * oa * (sa * (1.0 + ga * (1.0 - sa)))
        ones = _head_ones()
        delta_a = _seg_sum(do_a * oa, ones)
        dsk_ref[...] -= jnp.sum(delta_a * jnp.exp(sk_ref[...] - la_ref[...]), axis=0, keepdims=True)
        dybc = dyb2 * silu_b
        dgb = dyb2 * ybc * (sb * (1.0 + gb * (1.0 - sb)))
        dbar = _seg_sum(dybc * ybc, ones)
        dy_ref[...] = dy
        dyb_ref[...] = dyb
        dt_ref[:, 0:512] = dga.astype(bf16)
        dt_ref[:, 512:1024] = dgb.astype(bf16)
        dt_ref[:, 1024:2048] = dm0.astype(bf16)
        dt_ref[:, 2048:3072] = dm1.astype(bf16)
        doa_ref[...] = do_a.astype(bf16)
        dda_ref[...] = delta_a
        for k, (dob_ref, ddb_ref) in enumerate(((dob0_ref, ddb0_ref), (dob1_ref, ddb1_ref), (dob2_ref, ddb2_ref))):
            dob_ref[...] = alpha[k] * dybc
            ddb_ref[...] = alpha[k] * dbar
        ya_ref[...] = ya.T.astype(bf16)
        yb_ref[...] = yb.T.astype(bf16)
        mg_ref[...] = merged.T.astype(bf16)
        dbra_ref[...] = dbr_a
        dbrb_ref[...] = dbr_b

    def rows(n, blk=0):
        return pl.BlockSpec((ts, n), lambda i: (i, blk))

    def whole(r, c):
        return pl.BlockSpec((r, c), lambda i: (0, 0))

    def cols(n):
        return pl.BlockSpec((n, ts), lambda i: (0, i))

    def gate_cols(n, col):
        return pl.BlockSpec((pl.Element(ts), pl.Element(n)), lambda i: (i * ts, NA + col))

    outs = [
        ((S, D), f32, rows(D)), ((S, D), bf16, rows(D)), ((S, NW), bf16, gate_cols(NT, 0)),
        ((S, 512), bf16, rows(512)), ((S, 512), f32, rows(512)),
        ((S, 512), f32, rows(512)), ((S, 512), f32, rows(512)), ((S, 512), f32, rows(512)),
        ((S, 512), f32, rows(512)), ((S, 512), f32, rows(512)), ((S, 512), f32, rows(512)),
        ((512, S), bf16, cols(512)), ((512, S), bf16, cols(512)), ((D, S), bf16, cols(D)),
        ((S, D), bf16, rows(D)), ((S, D), bf16, rows(D)),
        ((1, 1), f32, whole(1, 1)), ((2, D), f32, whole(2, D)), ((1, 512), f32, whole(1, 512)),
    ]
    return pl.pallas_call(
        body,
        grid=(S // ts,),
        in_specs=[
            rows(D), rows(D), rows(512), rows(512), rows(512), rows(512), rows(512), rows(512), rows(512), rows(512),
            gate_cols(512, 0), gate_cols(512, 512), gate_cols(D, 1024), gate_cols(D, 2048), whole(2, D),
            whole(512, D), whole(512, D), whole(D, D), whole(1, 512),
        ],
        out_specs=[o[2] for o in outs],
        out_shape=[jax.ShapeDtypeStruct(o[0], o[1]) for o in outs],
        compiler_params=_params(("arbitrary",), vmem_mib=60),
        name="tail",
    )(x, tgt, o_a, l_a, *o_b, *l_b, proj, proj, proj, proj, bm, w_a, w_b, w_o, sink_b)


def _norm_bwd(xv, dyv, gain, kind, ones):
    r = lax.rsqrt(_half_sums(xv * xv, ones) * (1.0 / HD) + EPS)
    yv = xv * r
    up = jnp.where(kind == 0, dyv * SCALE, dyv)
    u = up * gain
    dxv = r * (u - yv * (_half_sums(u * yv, ones) * (1.0 / HD)))
    dxv = jnp.where(kind == 2, dyv, dxv)
    dg = jnp.where(kind == 2, 0.0, jnp.sum(up * yv, axis=0, keepdims=True))
    return dxv, dg


def _post_b(g, dqkv, proj_a, gains, dproj):
    d = GROUPS[g][1]
    seq = S // d

    def body(d_ref, p_ref, g_ref, alias_ref, o_ref, dg_ref, nat):
        del alias_ref
        j = pl.program_id(0)
        kind = j // 4
        gain = g_ref[...]
        ones = _head_ones()

        @pl.when(j % 4 == 0)
        def _():
            dg_ref[...] = jnp.zeros_like(dg_ref)

        for c in range(d):
            for i in range(seq // PCHUNK):
                src = c * seq + i * PCHUNK
                if d == 1:
                    idx = slice(src, src + PCHUNK)
                else:
                    idx = pl.ds(c + i * PCHUNK * d, PCHUNK, stride=d)
                dyv = d_ref[PAD + src:PAD + src + PCHUNK, :].astype(f32)
                dxv, dg = _norm_bwd(p_ref[idx, :], dyv, gain, kind, ones)
                nat[idx, :] = dxv
                dg_ref[...] += dg

        for i in range(S // CHUNK):
            o_ref[i * CHUNK:(i + 1) * CHUNK, :] = nat[i * CHUNK:(i + 1) * CHUNK, :].astype(bf16)

    return pl.pallas_call(
        body,
        grid=(12,),
        in_specs=[
            pl.BlockSpec((None, None, SP, LANES), lambda j: (j // 4, j % 4, 0, 0)),
            pl.BlockSpec((S, LANES), lambda j: (0, _col_block(g, jnp.minimum(j, 7)))),
            pl.BlockSpec((None, None, 1, LANES), lambda j: (g, j // 4, 0, 0)),
            pl.BlockSpec(memory_space=pl.ANY),
        ],
        out_specs=[
            pl.BlockSpec((S, LANES), lambda j: (0, _col_block(g, j))),
            pl.BlockSpec((None, 1, LANES), lambda j: (j // 4, 0, 0)),
        ],
        out_shape=[jax.ShapeDtypeStruct((S, NW), bf16), jax.ShapeDtypeStruct((3, 1, LANES), f32)],
        scratch_shapes=[pltpu.VMEM((S, LANES), f32)],
        input_output_aliases={3: 0},
        compiler_params=_params(("arbitrary",)),
        name="post_b%d" % g,
    )(dqkv, proj_a, gains, dproj)


def _post_a(dqkv, proj_a, gains, dproj):
    def body(q_ref, e_ref, p_ref, g_ref, alias_ref, o_ref, dg_ref):
        del alias_ref
        j = pl.program_id(0)
        kind = jnp.maximum(j - 3, 0)
        gain = g_ref[...]
        lo = _lo()
        ones = _head_ones()

        @pl.when((j == 0) | (j >= 4))
        def _():
            dg_ref[...] = jnp.zeros_like(dg_ref)

        for i in range(S // PCHUNK):
            r0 = i * PCHUNK
            rows = slice(PAD + r0, PAD + r0 + PCHUNK)
            t0 = e_ref[0, rows, :].astype(f32) + e_ref[1, rows, :].astype(f32)
            t1 = e_ref[2, rows, :].astype(f32) + e_ref[3, rows, :].astype(f32)
            folded = jnp.where(lo, t0 + pltpu.roll(t0, HD, 1), t1 + pltpu.roll(t1, HD, 1))
            dyv = jnp.where(kind == 0, q_ref[rows, :].astype(f32), folded)
            dxv, dg = _norm_bwd(p_ref[r0:r0 + PCHUNK, :], dyv, gain, kind, ones)
            o_ref[r0:r0 + PCHUNK, :] = dxv.astype(bf16)
            dg_ref[...] += dg

    return pl.pallas_call(
        body,
        grid=(6,),
        in_specs=[
            pl.BlockSpec((None, None, SP, LANES), lambda j: (0, jnp.minimum(j, 3), 0, 0)),
            pl.BlockSpec((None, 4, SP, LANES), lambda j: (jnp.clip(j - 3, 1, 2), 0, 0, 0)),
            pl.BlockSpec((S, LANES), lambda j: (0, jnp.minimum(j, 4))),
            pl.BlockSpec((None, None, 1, LANES), lambda j: (0, jnp.maximum(j - 3, 0), 0, 0)),
            pl.BlockSpec(memory_space=pl.ANY),
        ],
        out_specs=[
            pl.BlockSpec((S, LANES), lambda j: (0, j)),
            pl.BlockSpec((None, 1, LANES), lambda j: (jnp.maximum(j - 3, 0), 0, 0)),
        ],
        out_shape=[jax.ShapeDtypeStruct((S, NW), bf16), jax.ShapeDtypeStruct((3, 1, LANES), f32)],
        input_output_aliases={4: 0},
        compiler_params=_params(("arbitrary",)),
        name="post_a",
    )(dqkv, dqkv, proj_a, gains, dproj)


def _dh_norm_bwd(dproj, w, x, rstd, gain, dy):
    ts = 1024
    tk = NW // 6
    nk = NW // tk

    def body(d_ref, w_ref, x_ref, r_ref, g_ref, dy_ref, gx_ref, dgn_ref, acc):
        i = pl.program_id(0)
        k = pl.program_id(1)

        @pl.when((i == 0) & (k == 0))
        def _():
            dgn_ref[...] = jnp.zeros_like(dgn_ref)

        @pl.when(k == 0)
        def _():
            acc[...] = jnp.zeros_like(acc)

        acc[...] += jnp.dot(d_ref[...], w_ref[...], preferred_element_type=f32)

        @pl.when(k == nk - 1)
        def _():
            dh = acc[...]
            xh = x_ref[...] * r_ref[...]
            u = dh * g_ref[...]
            dx = r_ref[...] * (u - xh * jnp.mean(u * xh, axis=-1, keepdims=True))
            gx_ref[...] = dy_ref[...] + dx
            dgn_ref[...] += jnp.sum(dh * xh, axis=0, keepdims=True)

    return pl.pallas_call(
        body,
        grid=(S // ts, nk),
        in_specs=[
            pl.BlockSpec((ts, tk), lambda i, k: (i, k)),
            pl.BlockSpec((tk, D), lambda i, k: (k, 0)),
            pl.BlockSpec((ts, D), lambda i, k: (i, 0)),
            pl.BlockSpec((ts, 1), lambda i, k: (i, 0)),
            pl.BlockSpec((1, D), lambda i, k: (0, 0)),
            pl.BlockSpec((ts, D), lambda i, k: (i, 0)),
        ],
        out_specs=[pl.BlockSpec((ts, D), lambda i, k: (i, 0)), pl.BlockSpec((1, D), lambda i, k: (0, 0))],
        out_shape=[jax.ShapeDtypeStruct((S, D), f32), jax.ShapeDtypeStruct((1, D), f32)],
        scratch_shapes=[pltpu.VMEM((ts, D), f32)],
        compiler_params=_params(("arbitrary", "arbitrary"), vmem_mib=56),
        name="dh_norm_bwd",
    )(dproj, w, x, rstd, gain, dy)


def _dw_in(hbt, dproj):
    tk = 1024
    win = WSH + 96

    def body(a_ref, b_ref, o_ref, acc):
        j = pl.program_id(0)
        k = pl.program_id(1)

        @pl.when(k == 0)
        def _():
            acc[...] = jnp.zeros_like(acc)

        acc[...] += jnp.dot(a_ref[...], b_ref[...], preferred_element_type=f32)

        @pl.when(k == S // tk - 1)
        def _():
            acc_t = acc[...].T
            for jj in range(NDEV):
                off = (WSH * jj) % LANES

                @pl.when(j == jj)
                def _():
                    o_ref[...] = acc_t[off:off + WSH, :].astype(bf16)

    return pl.pallas_call(
        body,
        grid=(NDEV, S // tk),
        in_specs=[
            pl.BlockSpec((D, tk), lambda j, k: (0, k)),
            pl.BlockSpec((pl.Element(tk), pl.Element(win)), lambda j, k: (k * tk, (WSH * j) // LANES * LANES)),
        ],
        out_specs=pl.BlockSpec((None, WSH, D), lambda j, k: (j, 0, 0)),
        out_shape=jax.ShapeDtypeStruct((NDEV, WSH, D), bf16),
        scratch_shapes=[pltpu.VMEM((D, win), f32)],
        compiler_params=_params(("arbitrary", "arbitrary")),
        name="dw_in",
    )(hbt, dproj)


def _matmul_tokens(at, b, name):
    m, n = at.shape[0], b.shape[1]
    tn = 512
    tk = 1024

    def body(a_ref, b_ref, o_ref):
        @pl.when(pl.program_id(1) == 0)
        def _():
            o_ref[...] = jnp.zeros_like(o_ref)

        o_ref[...] += jnp.dot(a_ref[...], b_ref[...], preferred_element_type=f32)

    return pl.pallas_call(
        body,
        grid=(n // tn, S // tk),
        in_specs=[pl.BlockSpec((m, tk), lambda j, k: (0, k)), pl.BlockSpec((tk, tn), lambda j, k: (k, j))],
        out_specs=pl.BlockSpec((m, tn), lambda j, k: (0, j)),
        out_shape=jax.ShapeDtypeStruct((m, n), f32),
        compiler_params=_params(("arbitrary", "arbitrary")),
        name=name,
    )(at, b)


def _exchange(scatter, gather, name):
    arrs = list(scatter) + list(gather)
    n = len(arrs)
    ns = len(scatter)

    def body(*refs):
        ins, outs = refs[:n], refs[n:2 * n]
        send_sems, recv_sems, local_sems = refs[2 * n:]
        x, y, c = lax.axis_index("x"), lax.axis_index("y"), lax.axis_index("c")
        me = 4 * x + 2 * y + c
        local, remote = [], []
        for a in range(n):
            lc = pltpu.make_async_copy(ins[a].at[me] if a < ns else ins[a], outs[a].at[me], local_sems.at[a])
            lc.start()
            local.append(lc)
            for r in range(1, NDEV):
                px = 1 - x if r & 4 else x
                py = 1 - y if r & 2 else y
                pc = 1 - c if r & 1 else c
                cp = pltpu.make_async_remote_copy(
                    src_ref=ins[a].at[4 * px + 2 * py + pc] if a < ns else ins[a],
                    dst_ref=outs[a].at[me],
                    send_sem=send_sems.at[a, r - 1],
                    recv_sem=recv_sems.at[a, r - 1],
                    device_id=(px, py, pc),
                    device_id_type=pl.DeviceIdType.MESH,
                )
                cp.start()
                remote.append(cp)
        for cp in remote:
            cp.wait_recv()
        for cp in remote:
            cp.wait_send()
        for lc in local:
            lc.wait()

    out_shape = [jax.ShapeDtypeStruct(a.shape if i < ns else (NDEV,) + a.shape, a.dtype) for i, a in enumerate(arrs)]
    return pl.pallas_call(
        body,
        in_specs=[pl.BlockSpec(memory_space=pl.ANY)] * n,
        out_specs=[pl.BlockSpec(memory_space=pl.ANY)] * n,
        out_shape=out_shape,
        scratch_shapes=[
            pltpu.SemaphoreType.DMA((n, NDEV - 1)),
            pltpu.SemaphoreType.DMA((n, NDEV - 1)),
            pltpu.SemaphoreType.DMA((n,)),
        ],
        compiler_params=pltpu.CompilerParams(has_side_effects=True),
        name=name,
    )(*arrs)


def _gather_two_level(arrs, name):
    n = len(arrs)

    def body(*refs):
        ins, outs = refs[:n], refs[n:2 * n]
        send_sems, recv_sems, local_sems = refs[2 * n:]
        x, y, c = lax.axis_index("x"), lax.axis_index("y"), lax.axis_index("c")
        me, sibling = (x, y, c), (x, y, 1 - c)
        xn, yn, dg = (1 - x, y, c), (x, 1 - y, c), (1 - x, 1 - y, c)
        relay_origin = (jnp.bitwise_xor(x, c), jnp.bitwise_xor(y, 1 - c), c)
        relay_target = (jnp.bitwise_xor(x, 1 - c), jnp.bitwise_xor(y, c), c)

        def copy(a, k, block, to, src=None):
            slot = outs[a].at[4 * block[0] + 2 * block[1] + block[2]]
            return pltpu.make_async_remote_copy(
                src_ref=slot if src is None else src, dst_ref=slot, send_sem=send_sems.at[a, k],
                recv_sem=recv_sems.at[a, k], device_id=to, device_id_type=pl.DeviceIdType.MESH)

        def other_core(block):
            return (block[0], block[1], 1 - c)

        mine, sent = [], []
        for a in range(n):
            lc = pltpu.make_async_copy(ins[a], outs[a].at[4 * x + 2 * y + c], local_sems.at[a])
            lc.start()
            mine.append(lc)
            sent += [copy(a, 0, me, sibling, src=ins[a]), copy(a, 1, me, xn, src=ins[a]),
                     copy(a, 2, me, yn, src=ins[a])]
        for cp in sent:
            cp.start()
        later = []
        for a in range(n):
            copy(a, 1, xn, me).wait_recv()
            copy(a, 2, yn, me).wait_recv()
            later += [copy(a, 3, relay_origin, relay_target), copy(a, 4, xn, sibling), copy(a, 5, yn, sibling)]
            for cp in later[-3:]:
                cp.start()
        for a in range(n):
            copy(a, 3, dg, me).wait_recv()
            later.append(copy(a, 6, dg, sibling))
            later[-1].start()
        for a in range(n):
            copy(a, 0, sibling, me).wait_recv()
            for k, block in ((4, xn), (5, yn), (6, dg)):
                copy(a, k, other_core(block), me).wait_recv()
        for cp in sent + later:
            cp.wait_send()
        for lc in mine:
            lc.wait()

    return pl.pallas_call(
        body,
        in_specs=[pl.BlockSpec(memory_space=pl.ANY)] * n,
        out_specs=[pl.BlockSpec(memory_space=pl.ANY)] * n,
        out_shape=[jax.ShapeDtypeStruct((NDEV,) + a.shape, a.dtype) for a in arrs],
        scratch_shapes=[
            pltpu.SemaphoreType.DMA((n, NDEV - 1)),
            pltpu.SemaphoreType.DMA((n, NDEV - 1)),
            pltpu.SemaphoreType.DMA((n,)),
        ],
        compiler_params=pltpu.CompilerParams(has_side_effects=True),
        name=name,
    )(*arrs)


_HBM = pl.BlockSpec(memory_space=pltpu.HBM)
_SEM = pl.BlockSpec(memory_space=pltpu.SEMAPHORE)
_EFFECT = pltpu.SideEffectType.DATAFLOW_SIDE_EFFECTING


def _comm_step(name, body_fn, lands, srcs=(), wait_sems=(), n_new=0, after=()):
    n, ns, nw, na = len(lands), len(srcs), len(wait_sems), len(after)

    def body(*refs):
        src, land = refs[:ns], refs[ns:ns + n]
        waits = refs[ns + n:ns + n + nw]
        new = refs[ns + n + nw + na:ns + n + nw + na + n_new]
        body_fn(src, land, waits, new)

    hbm = [pltpu.HBM(a.shape, a.dtype) for a in lands]
    ops = [pltpu.with_memory_space_constraint(a, pltpu.HBM) for a in list(srcs) + list(lands)]
    outs = pl.pallas_call(
        body,
        out_shape=tuple([pltpu.SemaphoreType.DMA(())] * n_new + hbm),
        in_specs=[_HBM] * (ns + n) + [_SEM] * nw + [pl.BlockSpec(memory_space=pl.ANY)] * na,
        out_specs=tuple([_SEM] * n_new + [_HBM] * n),
        input_output_aliases={ns + i: n_new + i for i in range(n)},
        compiler_params=pltpu.CompilerParams(has_side_effects=_EFFECT),
        name=name,
    )(*ops, *wait_sems, *after)
    return list(outs[:n_new]), list(outs[n_new:])


class _GatheredWeights:
    def __init__(self, shards):
        self.n = n = len(shards)
        x, y, c = lax.axis_index("x"), lax.axis_index("y"), lax.axis_index("c")
        self.x = x
        me = 4 * x + 2 * y + c
        lands = [lax.dynamic_update_slice(lax.empty((NDEV,) + s.shape, s.dtype), s[None], (me,) + (0,) * s.ndim)
                 for s in shards]

        def start_own(src, land, waits, new):
            p = self._peers()
            for a in range(n):
                for k, to in ((0, p["sibling"]), (1, p["xn"]), (2, p["yn"])):
                    self._copy(land[a], new, a, k, 3, p["me"], to).start()

        self.sems, self.lands = {}, None
        new, self.lands = _comm_step("gather_start", start_own, lands, n_new=6 * n)
        self._keep(new, (0, 1, 2))

    @staticmethod
    def _peers():
        x, y, c = lax.axis_index("x"), lax.axis_index("y"), lax.axis_index("c")
        return dict(
            me=(x, y, c), sibling=(x, y, 1 - c), xn=(1 - x, y, c), yn=(x, 1 - y, c), dg=(1 - x, 1 - y, c),
            relay_origin=(jnp.bitwise_xor(x, c), jnp.bitwise_xor(y, 1 - c), c),
            relay_target=(jnp.bitwise_xor(x, 1 - c), jnp.bitwise_xor(y, c), c))

    def _keep(self, new, ks):
        half = len(new) // 2
        i = 0
        for a in range(self.n):
            for k in ks:
                self.sems[a, k] = (new[i], new[half + i])
                i += 1

    @staticmethod
    def _copy(land, sem_refs, a, k, nk, block, to, src=None, ks=None):
        ks = tuple(range(nk)) if ks is None else ks
        half = len(sem_refs) // 2
        i = a * len(ks) + ks.index(k)
        slot = land.at[4 * block[0] + 2 * block[1] + block[2]]
        return pltpu.make_async_remote_copy(
            src_ref=slot if src is None else src, dst_ref=slot, send_sem=sem_refs[i], recv_sem=sem_refs[half + i],
            device_id=to, device_id_type=pl.DeviceIdType.MESH)

    def _sem_list(self, ks):
        return ([self.sems[a, k][0] for a in range(self.n) for k in ks]
                + [self.sems[a, k][1] for a in range(self.n) for k in ks])

    def first_half(self, after):
        n = self.n

        def relay(src, land, waits, new):
            p = self._peers()
            for a in range(n):
                self._copy(land[a], waits, a, 1, 0, p["xn"], p["me"], ks=(1, 2)).wait_recv()
                self._copy(land[a], waits, a, 2, 0, p["yn"], p["me"], ks=(1, 2)).wait_recv()
                self._copy(land[a], new, a, 3, 0, p["relay_origin"], p["relay_target"], ks=(3, 4, 5)).start()
                self._copy(land[a], new, a, 4, 0, p["xn"], p["sibling"], ks=(3, 4, 5)).start()
                self._copy(land[a], new, a, 5, 0, p["yn"], p["sibling"], ks=(3, 4, 5)).start()

        new, self.lands = _comm_step("gather_relay", relay, self.lands, wait_sems=self._sem_list((1, 2)),
                                     n_new=6 * n, after=after)
        self._keep(new, (3, 4, 5))

        def from_sibling(src, land, waits, new):
            p = self._peers()
            other = lambda b: (b[0], b[1], 1 - b[2])
            for a in range(n):
                self._copy(land[a], waits, a, 0, 0, other(p["me"]), p["me"], ks=(0, 4, 5)).wait_recv()
                self._copy(land[a], waits, a, 4, 0, other(p["xn"]), p["me"], ks=(0, 4, 5)).wait_recv()
                self._copy(land[a], waits, a, 5, 0, other(p["yn"]), p["me"], ks=(0, 4, 5)).wait_recv()

        _, self.lands = _comm_step("gather_wait_sibling", from_sibling, self.lands,
                                   wait_sems=self._sem_list((0, 4, 5)))
        return self.lands[0].reshape(NW, D), self.x.astype(jnp.int32).reshape(1)

    def second_half(self, after):
        n = self.n

        def forward_diagonal(src, land, waits, new):
            p = self._peers()
            for a in range(n):
                self._copy(land[a], waits, a, 3, 0, p["dg"], p["me"], ks=(3,)).wait_recv()
                self._copy(land[a], new, a, 6, 0, p["dg"], p["sibling"], ks=(6,)).start()

        new, self.lands = _comm_step("gather_forward_diagonal", forward_diagonal, self.lands,
                                     wait_sems=self._sem_list((3,)), n_new=2 * n, after=after)
        self._keep(new, (6,))

        def finish(src, land, waits, new):
            p = self._peers()
            ks = tuple(range(7))
            for a in range(n):
                self._copy(land[a], waits, a, 6, 0, (p["dg"][0], p["dg"][1], 1 - p["dg"][2]), p["me"], ks=ks).wait_recv()
                for k in ks:
                    self._copy(land[a], waits, a, k, 0, p["me"], p["me"], ks=ks).wait_send()

        _, self.lands = _comm_step("gather_finish", finish, self.lands, wait_sems=self._sem_list(tuple(range(7))))
        return self.lands[0].reshape(NW, D), (1 - self.x).astype(jnp.int32).reshape(1)

    def rest(self):
        g_a, g_b, g_o, g_bm = self.lands[1:]
        return (g_a.transpose(1, 0, 2).reshape(512, D), g_b.transpose(1, 0, 2).reshape(512, D),
                g_bm.transpose(1, 0, 2).reshape(2, D), g_o.reshape(D, D))


def _sibling_exchange(g, name):
    def body(in_ref, out_ref, send_sems, recv_sems):
        x, y, c = lax.axis_index("x"), lax.axis_index("y"), lax.axis_index("c")
        copies = []
        for q in range(4):
            cp = pltpu.make_async_remote_copy(
                src_ref=in_ref.at[2 * q + (1 - c)], dst_ref=out_ref.at[q], send_sem=send_sems.at[q],
                recv_sem=recv_sems.at[q], device_id=(x, y, 1 - c), device_id_type=pl.DeviceIdType.MESH)
            cp.start()
            copies.append(cp)
        for cp in copies:
            cp.wait_recv()
        for cp in copies:
            cp.wait_send()

    return pl.pallas_call(
        body,
        in_specs=[pl.BlockSpec(memory_space=pl.ANY)],
        out_specs=pl.BlockSpec(memory_space=pl.ANY),
        out_shape=jax.ShapeDtypeStruct((4,) + g.shape[1:], g.dtype),
        scratch_shapes=[pltpu.SemaphoreType.DMA((4,)), pltpu.SemaphoreType.DMA((4,))],
        compiler_params=pltpu.CompilerParams(has_side_effects=True),
        name=name,
    )(g)


def _row_tile(rows, limit=256):
    fits = [t for t in range(16, limit + 1, 16) if rows % t == 0]
    return fits[-1] if fits else rows


def _pair_sum(g, r, core, name):
    _, rows, cols = g.shape
    tr = _row_tile(rows)

    def body(c_ref, g_ref, r_ref, o_ref):
        del c_ref
        o_ref[...] = (g_ref[...].astype(f32) + r_ref[...].astype(f32)).astype(bf16)

    return pl.pallas_call(
        body,
        grid_spec=pltpu.PrefetchScalarGridSpec(
            num_scalar_prefetch=1,
            grid=(4, rows // tr),
            in_specs=[pl.BlockSpec((None, tr, cols), lambda q, i, c_ref: (2 * q + c_ref[0], i, 0)),
                      pl.BlockSpec((None, tr, cols), lambda q, i, c_ref: (q, i, 0))],
            out_specs=pl.BlockSpec((None, tr, cols), lambda q, i, c_ref: (q, i, 0)),
        ),
        out_shape=jax.ShapeDtypeStruct((4, rows, cols), bf16),
        compiler_params=_params(("arbitrary", "arbitrary")),
        name=name,
    )(core, g, r)


def _scatter_start(chip_arrs, all_arrs, name):
    arrs = list(chip_arrs) + list(all_arrs)
    n, nc = len(arrs), len(chip_arrs)
    lands = [lax.empty(((3 if i < nc else NDEV - 1),) + a.shape[1:], a.dtype) for i, a in enumerate(arrs)]

    def body(*refs):
        src, land = refs[:n], refs[n:2 * n]
        send_sems, recv_sems = refs[2 * n:3 * n], refs[3 * n:4 * n]
        token = refs[6 * n]
        x, y, c = lax.axis_index("x"), lax.axis_index("y"), lax.axis_index("c")
        for a in range(n):
            for r in range(1, 4 if a < nc else NDEV):
                if a < nc:
                    px, py, pc = (1 - x if r & 2 else x), (1 - y if r & 1 else y), c
                    block = 2 * px + py
                else:
                    px, py, pc = (1 - x if r & 4 else x), (1 - y if r & 2 else y), (1 - c if r & 1 else c)
                    block = 4 * px + 2 * py + pc
                pltpu.make_async_remote_copy(
                    src_ref=src[a].at[block], dst_ref=land[a].at[r - 1], send_sem=send_sems[a],
                    recv_sem=recv_sems[a], device_id=(px, py, pc), device_id_type=pl.DeviceIdType.MESH).start()
        token[...] = jnp.zeros_like(token)

    hbm = [pltpu.HBM(a.shape, a.dtype) for a in arrs + lands]
    ops = [pltpu.with_memory_space_constraint(a, pltpu.HBM) for a in arrs + lands]
    outs = pl.pallas_call(
        body,
        out_shape=tuple([pltpu.SemaphoreType.DMA(())] * (2 * n) + hbm + [jax.ShapeDtypeStruct((8, LANES), f32)]),
        in_specs=[_HBM] * (2 * n),
        out_specs=tuple([_SEM] * (2 * n) + [_HBM] * (2 * n) + [pl.BlockSpec(memory_space=pltpu.VMEM)]),
        input_output_aliases={i: 2 * n + i for i in range(2 * n)},
        compiler_params=pltpu.CompilerParams(has_side_effects=_EFFECT),
        name=name,
    )(*ops)
    return outs[:n], outs[n:2 * n], outs[2 * n:3 * n], outs[3 * n:4 * n], outs[4 * n]


def _scatter_wait(send_sems, recv_sems, srcs, lands, after, name):
    n = len(srcs)

    def body(*refs):
        land = refs[n:2 * n]
        ssem, rsem = refs[2 * n:3 * n], refs[3 * n:4 * n]
        x, y, c = lax.axis_index("x"), lax.axis_index("y"), lax.axis_index("c")
        for a in range(n):
            done = pltpu.make_async_remote_copy(
                src_ref=land[a], dst_ref=land[a], send_sem=ssem[a], recv_sem=rsem[a], device_id=(x, y, c),
                device_id_type=pl.DeviceIdType.MESH)
            done.wait_send()
            done.wait_recv()

    hbm = [pltpu.HBM(a.shape, a.dtype) for a in list(srcs) + list(lands)]
    outs = pl.pallas_call(
        body,
        out_shape=tuple(hbm),
        in_specs=[_HBM] * (2 * n) + [_SEM] * (2 * n) + [pl.BlockSpec(memory_space=pl.ANY)],
        out_specs=tuple([_HBM] * (2 * n)),
        input_output_aliases={i: i for i in range(2 * n)},
        compiler_params=pltpu.CompilerParams(has_side_effects=_EFFECT),
        name=name,
    )(*srcs, *lands, *send_sems, *recv_sems, after)
    return outs[:n], outs[n:]


def _adam_update(g, w_ref, m_ref, v_ref, g_ref, d_ref, nm_ref, nv_ref):
    mm = ADAM_B1 * m_ref[...] + (1.0 - ADAM_B1) * g
    vv = ADAM_B2 * v_ref[...] + (1.0 - ADAM_B2) * (g * g)
    m_hat = mm / (1.0 - ADAM_B1 ** ADAM_STEP)
    v_hat = vv / (1.0 - ADAM_B2 ** ADAM_STEP)
    g_ref[...] = g
    d_ref[...] = -ADAM_LR * (m_hat / (jnp.sqrt(v_hat) + ADAM_EPS) + ADAM_WD * w_ref[...])
    nm_ref[...] = mm
    nv_ref[...] = vv


def _adamw_own(w, own, own_idx, slots, m, v, name):
    r, c = w.shape[-2:]
    tr = _row_tile(r, 128)
    k = slots.shape[0]

    def body(i_ref, w_ref, o_ref, s_ref, m_ref, v_ref, g_ref, d_ref, nm_ref, nv_ref):
        del i_ref
        g = o_ref[...].astype(f32)
        for j in range(k):
            g = g + s_ref[j].astype(f32)
        _adam_update(g, w_ref, m_ref, v_ref, g_ref, d_ref, nm_ref, nv_ref)

    blk = pl.BlockSpec((None, tr, c), lambda i, ix: (0, i, 0))
    return pl.pallas_call(
        body,
        grid_spec=pltpu.PrefetchScalarGridSpec(
            num_scalar_prefetch=1,
            grid=(r // tr,),
            in_specs=[blk, pl.BlockSpec((None, tr, c), lambda i, ix: (ix[0], i, 0)),
                      pl.BlockSpec((k, tr, c), lambda i, ix: (0, i, 0)), blk, blk],
            out_specs=[blk] * 4,
        ),
        out_shape=[jax.ShapeDtypeStruct(w.shape, f32)] * 4,
        compiler_params=_params(("arbitrary",)),
        name=name,
    )(own_idx, w, own, slots, m, v)


def _adamw(w, slots, m, v, name):
    r, c = w.shape[-2:]
    tr = _row_tile(r, 128)

    def body(w_ref, s_ref, m_ref, v_ref, g_ref, d_ref, nm_ref, nv_ref):
        g = s_ref[0].astype(f32)
        for k in range(1, NDEV):
            g = g + s_ref[k].astype(f32)
        _adam_update(g, w_ref, m_ref, v_ref, g_ref, d_ref, nm_ref, nv_ref)

    if w.ndim == 3:
        blk = pl.BlockSpec((None, tr, c), lambda i: (0, i, 0))
    else:
        blk = pl.BlockSpec((tr, c), lambda i: (i, 0))
    return pl.pallas_call(
        body,
        grid=(r // tr,),
        in_specs=[blk, pl.BlockSpec((NDEV, tr, c), lambda i: (0, i, 0)), blk, blk],
        out_specs=[blk] * 4,
        out_shape=[jax.ShapeDtypeStruct(w.shape, f32)] * 4,
        compiler_params=_params(("arbitrary",)),
        name=name,
    )(w, slots, m, v)


class _Weights:
    def __init__(self, w_t, w_a, w_b, b_merge, w_o):
        self._w_t, self._rest = w_t, (w_a, w_b, b_merge, w_o)

    def first_half(self, after):
        del after
        return self._w_t, jnp.zeros((1,), jnp.int32)

    def second_half(self, after):
        del after
        return self._w_t, jnp.ones((1,), jnp.int32)

    def rest(self):
        return self._rest


def _local_step(x, tgt, norm_gain, weights, qn_a, kn_a, qn_b, kn_b, sink_a, rel_bias, on_weight_grads=None):
    two = lambda t: jnp.concatenate([t, t], axis=-1).reshape(1, LANES)
    ones = jnp.ones((1, LANES), f32)
    gains = jnp.stack([
        jnp.stack([two(qn_a), two(kn_a), ones]),
        jnp.stack([two(qn_b), two(kn_b), ones]),
        jnp.stack([two(qn_b), two(kn_b), ones]),
        jnp.stack([two(qn_b), two(kn_b), ones]),
    ])
    buckets = [jnp.asarray(_bucket_np(blk, d)) for blk, d, _ in GROUPS]
    bias = [_bias_expand(rel_bias, buckets[k], GROUPS[k][2], "bias_expand_%d" % k) for k in range(4)]

    hb, hbt, rstd = _rms(x, norm_gain)
    w_t, half = weights.first_half([hb] + bias)
    proj = _inproj_half(hb, w_t, half, None, "inproj_1")
    w_t, half = weights.second_half([proj])
    proj = _inproj_half(hb, w_t, half, proj, "inproj_2")
    w_a, w_b, b_merge, w_o = weights.rest()
    gl = _prep(proj, gains)
    o_a, l_a = _attn_fwd(gl, bias[0], sink_a.reshape(8), 0, 128, 1, "attn_fwd_a")
    fwd_b = [_attn_fwd(gl, bias[k], None, k, GROUPS[k][0], GROUPS[k][1], "attn_fwd_b%d" % k) for k in (1, 2, 3)]
    sink_b = jnp.repeat(sink_a.reshape(8), HD).reshape(1, 512)

    (dy, dyb, dproj, do_a, dd_a, do_b0, do_b1, do_b2, dd_b0, dd_b1, dd_b2, ya, yb, mg, dbr_a, dbr_b, loss, dbm,
     dsk) = _tail(x, tgt, o_a, l_a, [f[0] for f in fwd_b], [f[1] for f in fwd_b], proj, b_merge, w_a, w_b, w_o, sink_b)

    dw_o = _matmul_tokens(mg, dyb, "dw_out")
    dw_a = _matmul_tokens(ya, dbr_a, "dw_branch_a")
    dw_b = _matmul_tokens(yb, dbr_b, "dw_branch_b")
    if on_weight_grads is not None:
        early = on_weight_grads(dict(w_branch_a=dw_a, w_branch_b=dw_b, b_merge=dbm, w_out=dw_o))
        buckets = [buckets[0] + early.astype(jnp.int32)] + buckets[1:]

    dqkv_a, dbk_a = _attn_bwd(gl, bias[0], buckets[0], do_a, l_a, dd_a, 0, 128, 1, "attn_bwd_a")
    dproj, dg_a = _post_a(dqkv_a, proj, gains, dproj)
    dbk_b, dg_b = [], []
    for k, do_k, dd_k in ((1, do_b0, dd_b0), (2, do_b1, dd_b1), (3, do_b2, dd_b2)):
        dqkv, dbk = _attn_bwd(gl, bias[k], buckets[k], do_k, fwd_b[k - 1][1], dd_k, k, GROUPS[k][0], GROUPS[k][1],
                              "attn_bwd_b%d" % k)
        dproj, dg = _post_b(k, dqkv, proj, gains, dproj)
        dbk_b.append(dbk)
        dg_b.append(dg)
    dg_b = jnp.stack(dg_b)

    dw_in = _dw_in(hbt, dproj)
    token = jnp.zeros((), f32) if on_weight_grads is None else on_weight_grads(dict(w_in=dw_in))
    grad_x, d_norm_gain = _dh_norm_bwd(dproj, w_t, x, rstd, norm_gain + token, dy)

    fold = lambda t: t[..., :HD] + t[..., HD:]
    d_qn_a = fold(dg_a[0, 0])
    d_kn_a = fold(dg_a[1, 0])
    d_qn_b = fold(dg_b[:, 0, 0].sum(axis=0))
    d_kn_b = fold(dg_b[:, 1, 0].sum(axis=0))
    d_sink = dsk.reshape(8, HD)[:, 0]
    red = jnp.stack([dbk_a] + dbk_b)
    d_rel = red[:, :, 0, :32].reshape(32, 32).T
    return dict(loss=loss, grad_x=grad_x, norm_gain=d_norm_gain, w_in=dw_in, q_norm_a=d_qn_a, k_norm_a=d_kn_a,
                q_norm_b=d_qn_b, k_norm_b=d_kn_b, sink_a=d_sink, rel_bias=d_rel, w_branch_a=dw_a, w_branch_b=dw_b,
                b_merge=dbm, w_out=dw_o)


SMALL = (("norm_gain", D), ("q_norm_a", HD), ("k_norm_a", HD), ("q_norm_b", HD), ("k_norm_b", HD), ("sink_a", 8),
         ("rel_bias", 1024))
SMALL_PAD = 2432


SMALL_USED = sum(sz for _, sz in SMALL)


def _pack_small(parts, loss=None):
    tail = jnp.zeros((SMALL_PAD - SMALL_USED,), f32)
    if loss is not None:
        tail = tail.at[0].set(loss.reshape(()))
    return jnp.concatenate([parts[n].reshape(-1) for n, _ in SMALL] + [tail]).reshape(1, SMALL_PAD)


def _unpack_small(flat, shapes):
    out, off = {}, 0
    for n, sz in SMALL:
        out[n] = flat[0, off:off + sz].reshape(shapes[n])
        off += sz
    return out


def kernel(x, norm_gain, w_in, q_norm_a, k_norm_a, q_norm_b, k_norm_b, sink_a, rel_bias, w_branch_a, w_branch_b, b_merge, w_out, loss_target, m_norm_gain, m_w_in, m_q_norm_a, m_k_norm_a, m_q_norm_b, m_k_norm_b, m_sink_a, m_rel_bias, m_w_branch_a, m_w_branch_b, m_b_merge, m_w_out, v_norm_gain, v_w_in, v_q_norm_a, v_k_norm_a, v_q_norm_b, v_k_norm_b, v_sink_a, v_rel_bias, v_w_branch_a, v_w_branch_b, v_b_merge, v_w_out):
    csh = D // NDEV
    w_in_t, m_w_in_t, v_w_in_t = (jnp.swapaxes(t, 1, 2) for t in (w_in, m_w_in, v_w_in))
    weights = _GatheredWeights([w_in_t[0].astype(bf16), w_branch_a[0].astype(bf16), w_branch_b[0].astype(bf16),
                                w_out[0].astype(bf16), b_merge[0]])

    pending = {}
    core = lax.axis_index("c").astype(jnp.int32).reshape(1)
    chip = (2 * lax.axis_index("x") + lax.axis_index("y")).astype(jnp.int32).reshape(1)
    me = (2 * chip + core).astype(jnp.int32)

    def start_exchange(gw):
        if "w_in" in gw:
            from_sibling = _sibling_exchange(gw["w_in"], "grad_sibling_exchange")
            chip_sums = _pair_sum(gw["w_in"], from_sibling, core, "grad_pair_sum")
            pending["w_in"] = _scatter_start([chip_sums], [], "scatter_w_in_start")
            return pending["w_in"][4][0, 0]
        blocks = [gw["w_branch_a"].reshape(512, NDEV, csh).transpose(1, 0, 2).astype(bf16),
                  gw["w_branch_b"].reshape(512, NDEV, csh).transpose(1, 0, 2).astype(bf16),
                  gw["w_out"].reshape(NDEV, csh, D).astype(bf16),
                  gw["b_merge"].reshape(2, NDEV, csh).transpose(1, 0, 2)]
        pending["rest"] = _scatter_start([], blocks, "scatter_rest_start")
        return pending["rest"][4][0, 0]

    loc = _local_step(x[0], loss_target[0], norm_gain, weights, q_norm_a, k_norm_a, q_norm_b, k_norm_b, sink_a,
                      rel_bias, on_weight_grads=start_exchange)

    small_shapes = dict(norm_gain=(1, D), q_norm_a=(1, HD), k_norm_a=(1, HD), q_norm_b=(1, HD), k_norm_b=(1, HD),
                        sink_a=(1, 8), rel_bias=(32, 32))
    (r_small,) = _exchange([], [_pack_small(loc, loc["loss"])], "gather_small_grads")
    send_sems, recv_sems, srcs, lands, _ = pending["rest"]
    (s_a, s_b, s_o, s_bm), (r_a, r_b, r_o, r_bm) = _scatter_wait(
        send_sems, recv_sems, srcs, lands, r_small, "scatter_rest_wait")
    send_sems, recv_sems, srcs, lands, _ = pending["w_in"]
    (s_in,), (r_in,) = _scatter_wait(send_sems, recv_sems, srcs, lands, r_small, "scatter_w_in_wait")

    given = dict(norm_gain=norm_gain, q_norm_a=q_norm_a, k_norm_a=k_norm_a, q_norm_b=q_norm_b, k_norm_b=k_norm_b,
                 sink_a=sink_a, rel_bias=rel_bias)
    m_small = dict(norm_gain=m_norm_gain, q_norm_a=m_q_norm_a, k_norm_a=m_k_norm_a, q_norm_b=m_q_norm_b,
                   k_norm_b=m_k_norm_b, sink_a=m_sink_a, rel_bias=m_rel_bias)
    v_small = dict(norm_gain=v_norm_gain, q_norm_a=v_q_norm_a, k_norm_a=v_k_norm_a, q_norm_b=v_q_norm_b,
                   k_norm_b=v_k_norm_b, sink_a=v_sink_a, rel_bias=v_rel_bias)
    res = {
        "small": _adamw(_pack_small(given), r_small, _pack_small(m_small), _pack_small(v_small), "adamw_small"),
        "w_in": [jnp.swapaxes(t, 1, 2) for t in
                 _adamw_own(w_in_t, s_in, chip, r_in, m_w_in_t, v_w_in_t, "adamw_w_in")],
        "w_branch_a": _adamw_own(w_branch_a, s_a, me, r_a, m_w_branch_a, v_w_branch_a, "adamw_w_branch_a"),
        "w_branch_b": _adamw_own(w_branch_b, s_b, me, r_b, m_w_branch_b, v_w_branch_b, "adamw_w_branch_b"),
        "b_merge": _adamw_own(b_merge, s_bm, me, r_bm, m_b_merge, v_b_merge, "adamw_b_merge"),
        "w_out": _adamw_own(w_out, s_o, me, r_o, m_w_out, v_w_out, "adamw_w_out"),
    }
    order = ["norm_gain", "w_in", "q_norm_a", "k_norm_a", "q_norm_b", "k_norm_b", "sink_a", "rel_bias", "w_branch_a",
             "w_branch_b", "b_merge", "w_out"]
    outs = []
    for k in range(4):
        small = _unpack_small(res["small"][k], small_shapes)
        for n in order:
            outs.append(small[n] if n in small else res[n][k])
    loss = res["small"][0][0, SMALL_USED]
    return (loss, loc["grad_x"][None], *outs)
```

```python
import math

import numpy as np
import jax
import jax.numpy as jnp
from jax import lax
from jax.experimental import pallas as pl
from jax.experimental.pallas import tpu as pltpu

f32 = jnp.float32
bf16 = jnp.bfloat16

S = 4096
D = 1024
NA = 5376
NT = 3072
NW = NA + NT
WSH = NW // 8
HD = 64
LANES = 128
EPS = 1e-6
NEG = -1e30
SCALE = HD ** -0.5
TQ = 128
PAD = 128
SP = S + 2 * PAD
NDEV = 8
GROUPS = ((128, 1, 0), (64, 1, 8), (64, 4, 16), (64, 16, 24))
CHUNK = 256
PCHUNK = 128
RC = 64
TN = 768

ADAM_LR, ADAM_B1, ADAM_B2, ADAM_EPS, ADAM_WD, ADAM_STEP = 0.001, 0.9, 0.999, 1e-08, 0.01, 10

MIB = 1024 * 1024
NT_DIMS = (((1,), (1,)), ((), ()))
TN_DIMS = (((0,), (0,)), ((), ()))


def _params(sem=None, vmem_mib=48):
    return pltpu.CompilerParams(dimension_semantics=sem, vmem_limit_bytes=vmem_mib * MIB)


def _lo():
    return lax.broadcasted_iota(jnp.int32, (1, LANES), 1) < HD


def _head_ones():
    r = lax.broadcasted_iota(jnp.int32, (LANES, LANES), 0) // HD
    c = lax.broadcasted_iota(jnp.int32, (LANES, LANES), 1) // HD
    return jnp.where(r == c, 1.0, 0.0).astype(bf16)


def _half_sums(x, ones):
    hi = x.astype(bf16)
    mid = (x - hi.astype(f32)).astype(bf16)
    return (jnp.dot(hi, ones, preferred_element_type=f32) + jnp.dot(mid, ones, preferred_element_type=f32))


def _seg_sum(x, ones):
    outs = [_half_sums(x[:, b * LANES:(b + 1) * LANES], ones) for b in range(x.shape[1] // LANES)]
    return outs[0] if len(outs) == 1 else jnp.concatenate(outs, axis=1)


def _bucket_np(blk, stride):
    w = TQ + 2 * blk
    rel = np.arange(w)[None, :] - blk - np.arange(TQ)[:, None]
    band = np.abs(rel) <= blk
    r = rel * stride
    n = np.abs(r)
    nf = np.maximum(n, 8).astype(np.float32)
    large = 8 + (np.log(nf / np.float32(8)) / np.float32(math.log(128.0)) * np.float32(8)).astype(np.int32)
    large = np.minimum(large, 15)
    b = (r > 0).astype(np.int32) * 16 + np.where(n < 8, n, large)
    return np.where(band, b, -1).astype(np.int32)


def _rms(x, gain):
    ts = 512

    def body(x_ref, g_ref, h_ref, ht_ref, r_ref):
        xv = x_ref[...]
        r = lax.rsqrt(jnp.mean(xv * xv, axis=-1, keepdims=True) + EPS)
        h = (xv * r) * g_ref[...]
        h_ref[...] = h.astype(bf16)
        ht_ref[...] = h.T.astype(bf16)
        r_ref[...] = r

    return pl.pallas_call(
        body,
        grid=(S // ts,),
        in_specs=[pl.BlockSpec((ts, D), lambda i: (i, 0)), pl.BlockSpec((1, D), lambda i: (0, 0))],
        out_specs=[pl.BlockSpec((ts, D), lambda i: (i, 0)), pl.BlockSpec((D, ts), lambda i: (0, i)),
                   pl.BlockSpec((ts, 1), lambda i: (i, 0))],
        out_shape=[jax.ShapeDtypeStruct((S, D), bf16), jax.ShapeDtypeStruct((D, S), bf16),
                   jax.ShapeDtypeStruct((S, 1), f32)],
        compiler_params=_params(("arbitrary",)),
        name="rms",
    )(x, gain)


def _inproj_half(hb, w_t, half, proj, name):
    ts = 2048
    tn = NW // 6
    per = NW // 2 // tn

    def body(h_idx, h_ref, w_ref, *rest):
        del h_idx
        rest[-1][...] = lax.dot_general(h_ref[...], w_ref[...], NT_DIMS, preferred_element_type=f32)

    in_specs = [pl.BlockSpec((ts, D), lambda i, n, hf: (i, 0)),
                pl.BlockSpec((tn, D), lambda i, n, hf: (hf[0] * per + n, 0))]
    args = [half, hb, w_t]
    aliases = {}
    if proj is not None:
        in_specs.append(pl.BlockSpec(memory_space=pl.ANY))
        args.append(proj)
        aliases = {3: 0}
    return pl.pallas_call(
        body,
        grid_spec=pltpu.PrefetchScalarGridSpec(
            num_scalar_prefetch=1,
            grid=(S // ts, per),
            in_specs=in_specs,
            out_specs=pl.BlockSpec((ts, tn), lambda i, n, hf: (i, hf[0] * per + n)),
        ),
        out_shape=jax.ShapeDtypeStruct((S, NW), f32),
        input_output_aliases=aliases,
        compiler_params=_params(("arbitrary", "arbitrary")),
        name=name,
    )(*args)


def _bias_expand(table, bucket, c0, name):
    tq, w = bucket.shape
    blk = (w - tq) // 2

    def body(tab_ref, bk_ref, o_ref):
        h = pl.program_id(0)
        bk = bk_ref[...]

        def step(b, acc):
            return jnp.where(bk == b, tab_ref[b, c0 + h], acc)

        inner = lax.fori_loop(0, 32, step, jnp.full((tq, w), NEG, f32))
        col = lax.broadcasted_iota(jnp.int32, (1, w), 1)
        o_ref[0] = jnp.where(col < blk, NEG, inner)
        o_ref[1] = inner
        o_ref[2] = jnp.where(col >= tq + blk, NEG, inner)

    return pl.pallas_call(
        body,
        grid=(8,),
        in_specs=[pl.BlockSpec(memory_space=pltpu.SMEM), pl.BlockSpec((tq, w), lambda h: (0, 0))],
        out_specs=pl.BlockSpec((3, None, tq, w), lambda h: (0, h, 0, 0)),
        out_shape=jax.ShapeDtypeStruct((3, 8, tq, w), f32),
        compiler_params=_params(("arbitrary",)),
        name=name,
    )(table, bucket)


def _tile_kind(t, seq):
    m0 = jnp.bitwise_and(t * TQ, seq - 1)
    return jnp.where(m0 == 0, 0, jnp.where(m0 == seq - TQ, 2, 1))


def _col_block(g, j):
    kind = j // 4
    hp = j % 4
    a = jnp.where(kind == 0, hp, 3 + kind)
    b = 6 + 12 * kind + 4 * (g - 1) + hp
    return jnp.where(g == 0, a, b)


def _prep(proj_a, gains):
    def body(p_ref, g_ref, o_ref):
        g = pl.program_id(0)
        j = pl.program_id(1)
        kind = j // 4
        lo = _lo()
        ones = _head_ones()
        half = jnp.where(lo, 0, 1)
        take = (kind == 0) | (half == (j % 4) // 2)
        gain = g_ref[...]
        o_ref[0:PAD, :] = jnp.zeros((PAD, LANES), bf16)
        o_ref[PAD + S:SP, :] = jnp.zeros((PAD, LANES), bf16)

        def norm_store(xv, dst, dup):
            if dup:
                xv = jnp.where(take, xv, pltpu.roll(xv, HD, 1))
            r = lax.rsqrt(_half_sums(xv * xv, ones) * (1.0 / HD) + EPS)
            r = jnp.where(kind == 2, 1.0, r)
            yv = (xv * r) * gain
            yv = jnp.where(kind == 0, yv * SCALE, yv)
            o_ref[PAD + dst:PAD + dst + CHUNK, :] = yv.astype(bf16)

        for gi, (_, d, _) in enumerate(GROUPS):
            @pl.when(g == gi)
            def _():
                seq = S // d
                for c in range(d):
                    for i in range(seq // CHUNK):
                        if d == 1:
                            xv = p_ref[i * CHUNK:(i + 1) * CHUNK, :]
                        else:
                            xv = p_ref[pl.ds(c + i * CHUNK * d, CHUNK, stride=d), :]
                        norm_store(xv, c * seq + i * CHUNK, gi == 0)

    return pl.pallas_call(
        body,
        grid=(4, 12),
        in_specs=[
            pl.BlockSpec((S, LANES), lambda g, j: (0, _col_block(g, j))),
            pl.BlockSpec((None, None, 1, LANES), lambda g, j: (g, j // 4, 0, 0)),
        ],
        out_specs=pl.BlockSpec((None, None, SP, LANES), lambda g, j: (g, j, 0, 0)),
        out_shape=jax.ShapeDtypeStruct((4, 12, SP, LANES), bf16),
        compiler_params=_params(("arbitrary", "arbitrary")),
        name="prep",
    )(proj_a, gains)


def _token_rows(t, r0, n, d):
    if d == 1:
        return pl.ds(pl.multiple_of(t * TQ, TQ) + r0, n)
    per = S // d // TQ
    return pl.ds(((t % per) * TQ + r0) * d + t // per, n, stride=d)


def _stack_heads(t, lo):
    z = jnp.zeros_like(t)
    return jnp.concatenate([jnp.where(lo, t, z), jnp.where(lo, z, t)], axis=0)


def _unstack_heads(t2, lo):
    return jnp.where(lo, t2[:TQ], t2[TQ:])


def _attn_fwd(gl, bias, sink, g, blk, d, name):
    w = TQ + 2 * blk
    seq = S // d
    use_sink = sink is not None

    def body(*refs):
        if use_sink:
            sink_ref, q_ref, k_ref, v_ref, b_ref, o_ref, l_ref, s0, s1, p0, p1, lse_scr = refs
        else:
            q_ref, k_ref, v_ref, b_ref, o_ref, l_ref, s0, s1, p0, p1, lse_scr = refs
        hp = pl.program_id(0)
        lo = _lo()
        s_bufs, p_bufs = (s0, s1), (p0, p1)

        def scores(p, slot):
            for u in range(2):
                f0 = pl.multiple_of((2 * p + u) * TQ, TQ)
                q2 = _stack_heads(q_ref[pl.ds(PAD + f0, TQ), :], lo)
                kw = k_ref[pl.ds(PAD - blk + f0, w), :]
                s_bufs[slot][u] = lax.dot_general(q2, kw, NT_DIMS, preferred_element_type=f32)

        def softmax(p, slot):
            for u in range(2):
                t = 2 * p + u
                kind = _tile_kind(t, seq)
                for h in range(2):
                    for r in range(TQ // RC):
                        rows = slice(h * TQ + r * RC, h * TQ + (r + 1) * RC)
                        logit = s_bufs[slot][u, rows, :] + b_ref[kind, h, r * RC:(r + 1) * RC, :]
                        m = jnp.max(logit, axis=1, keepdims=True)
                        e = jnp.exp(logit - m)
                        lse = m + jnp.log(jnp.sum(e, axis=1, keepdims=True))
                        if use_sink:
                            sk = sink_ref[2 * hp + h]
                            mx = jnp.maximum(lse, sk)
                            lse = mx + jnp.log(jnp.exp(lse - mx) + jnp.exp(sk - mx))
                        p_bufs[slot][u, rows, :] = (e * jnp.exp(m - lse)).astype(bf16)
                        lse_scr[u, rows, :] = jnp.broadcast_to(lse, (RC, LANES))
                l_ref[_token_rows(t, 0, TQ, d), :] = jnp.where(lo, lse_scr[u, 0:TQ, :], lse_scr[u, TQ:2 * TQ, :])

        def values(p, slot):
            for u in range(2):
                t = 2 * p + u
                vw = v_ref[pl.ds(PAD - blk + pl.multiple_of(t * TQ, TQ), w), :]
                o2 = jnp.dot(p_bufs[slot][u], vw, preferred_element_type=f32)
                o_ref[_token_rows(t, 0, TQ, d), :] = _unstack_heads(o2, lo)

        npair = S // TQ // 2
        scores(0, 0)
        scores(1, 1)
        softmax(0, 0)

        def steady(k, carry):
            p = 2 * k + 2
            scores(p, 0)
            softmax(p - 1, 1)
            values(p - 2, 0)
            scores(p + 1, 1)
            softmax(p, 0)
            values(p - 1, 1)
            return carry

        lax.fori_loop(0, (npair - 2) // 2, steady, 0)
        softmax(npair - 1, 1)
        values(npair - 2, 0)
        values(npair - 1, 1)

    in_specs = [
        pl.BlockSpec((None, None, SP, LANES), lambda hp: (g, hp, 0, 0)),
        pl.BlockSpec((None, None, SP, LANES), lambda hp: (g, 4 + hp, 0, 0)),
        pl.BlockSpec((None, None, SP, LANES), lambda hp: (g, 8 + hp, 0, 0)),
        pl.BlockSpec((3, 2, TQ, w), lambda hp: (0, hp, 0, 0)),
    ]
    args = [gl, gl, gl, bias]
    if use_sink:
        in_specs = [pl.BlockSpec(memory_space=pltpu.SMEM)] + in_specs
        args = [sink] + args
    out = pl.BlockSpec((S, LANES), lambda hp: (0, hp))
    return pl.pallas_call(
        body,
        grid=(4,),
        in_specs=in_specs,
        out_specs=[out, out],
        out_shape=[jax.ShapeDtypeStruct((S, 4 * LANES), f32)] * 2,
        scratch_shapes=[pltpu.VMEM((2, 2 * TQ, w), f32), pltpu.VMEM((2, 2 * TQ, w), f32),
                        pltpu.VMEM((2, 2 * TQ, w), bf16), pltpu.VMEM((2, 2 * TQ, w), bf16),
                        pltpu.VMEM((2, 2 * TQ, LANES), f32)],
        compiler_params=_params(("arbitrary",)),
        name=name,
    )(*args)


def _attn_bwd(gl, bias, bucket, do, lse, dd, g, blk, d, name):
    w = TQ + 2 * blk
    seq = S // d

    def body(q_ref, k_ref, v_ref, b_ref, bk_ref, do_ref, l_ref, d_ref, dqkv_ref, dbk_ref,
             db_acc, s0, s1, dp0, dp1, pb0, pb1, ds0, ds1, dk_acc, dv_acc):
        lo = _lo()
        hi = jnp.logical_not(lo)
        dk_acc[...] = jnp.zeros((SP, LANES), f32)
        dv_acc[...] = jnp.zeros((SP, LANES), f32)
        db_acc[...] = jnp.zeros((2 * TQ, w), f32)
        s_bufs, dp_bufs, pb_bufs, ds_bufs = (s0, s1), (dp0, dp1), (pb0, pb1), (ds0, ds1)

        def stacked(t):
            f0 = pl.multiple_of(t * TQ, TQ)
            q2 = _stack_heads(q_ref[pl.ds(PAD + f0, TQ), :], lo)
            do2 = _stack_heads(do_ref[_token_rows(t, 0, TQ, d), :].astype(bf16), lo)
            return f0, q2, do2

        def scores(p, slot):
            for u in range(2):
                f0, q2, do2 = stacked(2 * p + u)
                win = pl.ds(PAD - blk + f0, w)
                s_bufs[slot][u] = lax.dot_general(q2, k_ref[win, :], NT_DIMS, preferred_element_type=f32)
                dp_bufs[slot][u] = lax.dot_general(do2, v_ref[win, :], NT_DIMS, preferred_element_type=f32)

        def grads(p, slot):
            for u in range(2):
                t = 2 * p + u
                kind = _tile_kind(t, seq)
                for h in range(2):
                    msk = lo if h == 0 else hi
                    for r in range(TQ // RC):
                        rows = slice(h * TQ + r * RC, h * TQ + (r + 1) * RC)
                        src = _token_rows(t, r * RC, RC, d)
                        lh = jnp.max(jnp.where(msk, l_ref[src, :], -jnp.inf), axis=1, keepdims=True)
                        dh = jnp.max(jnp.where(msk, d_ref[src, :], -jnp.inf), axis=1, keepdims=True)
                        logit = s_bufs[slot][u, rows, :] + b_ref[kind, h, r * RC:(r + 1) * RC, :]
                        pr = jnp.exp(logit - lh)
                        ds = pr * (dp_bufs[slot][u, rows, :] - dh)
                        db_acc[rows, :] += ds
                        pb_bufs[slot][u, rows, :] = pr.astype(bf16)
                        ds_bufs[slot][u, rows, :] = ds.astype(bf16)

        def accumulate(p, slot):
            for u in range(2):
                f0, q2, do2 = stacked(2 * p + u)
                win = pl.ds(PAD - blk + f0, w)
                dsb = ds_bufs[slot][u]
                dq2 = jnp.dot(dsb, k_ref[win, :], preferred_element_type=f32)
                dqkv_ref[0, pl.ds(PAD + f0, TQ), :] = _unstack_heads(dq2, lo).astype(bf16)
                dk_acc[win, :] += lax.dot_general(dsb, q2, TN_DIMS, preferred_element_type=f32)
                dv_acc[win, :] += lax.dot_general(pb_bufs[slot][u], do2, TN_DIMS, preferred_element_type=f32)

        npair = S // TQ // 2
        scores(0, 0)
        scores(1, 1)
        grads(0, 0)

        def steady(k, carry):
            p = 2 * k + 2
            scores(p, 0)
            grads(p - 1, 1)
            accumulate(p - 2, 0)
            scores(p + 1, 1)
            grads(p, 0)
            accumulate(p - 1, 1)
            return carry

        lax.fori_loop(0, (npair - 2) // 2, steady, 0)
        grads(npair - 1, 1)
        accumulate(npair - 2, 0)
        accumulate(npair - 1, 1)
        for i in range(SP // CHUNK):
            rows = slice(i * CHUNK, (i + 1) * CHUNK)
            dqkv_ref[1, rows, :] = dk_acc[rows, :].astype(bf16)
            dqkv_ref[2, rows, :] = dv_acc[rows, :].astype(bf16)

        bk = bk_ref[...]
        lane = lax.broadcasted_iota(jnp.int32, (8, LANES), 1)
        for h in range(2):
            db = db_acc[h * TQ:(h + 1) * TQ, :]
            acc = jnp.zeros((8, LANES), f32)
            for b in range(32):
                part = jnp.where(bk == b, db, 0.0).reshape(TQ // 8, 8, w).sum(axis=0)
                tot = jnp.sum(jnp.sum(part, axis=1, keepdims=True), axis=0, keepdims=True)
                acc = jnp.where(lane == b, tot, acc)
            dbk_ref[h] = acc

    def gcol(off):
        return pl.BlockSpec((None, None, SP, LANES), lambda hp: (g, off + hp, 0, 0))

    row = pl.BlockSpec((S, LANES), lambda hp: (0, hp))
    return pl.pallas_call(
        body,
        grid=(4,),
        in_specs=[gcol(0), gcol(4), gcol(8), pl.BlockSpec((3, 2, TQ, w), lambda hp: (0, hp, 0, 0)),
                  pl.BlockSpec((TQ, w), lambda hp: (0, 0)), row, row, row],
        out_specs=[pl.BlockSpec((3, None, SP, LANES), lambda hp: (0, hp, 0, 0)),
                   pl.BlockSpec((2, 8, LANES), lambda hp: (hp, 0, 0))],
        out_shape=[
            jax.ShapeDtypeStruct((3, 4, SP, LANES), bf16),
            jax.ShapeDtypeStruct((8, 8, LANES), f32),
        ],
        scratch_shapes=([pltpu.VMEM((2 * TQ, w), f32)] + [pltpu.VMEM((2, 2 * TQ, w), f32)] * 4
                        + [pltpu.VMEM((2, 2 * TQ, w), bf16)] * 4 + [pltpu.VMEM((SP, LANES), f32)] * 2),
        compiler_params=_params(("arbitrary",), vmem_mib=56),
        name=name,
    )(gl, gl, gl, bias, bucket, do, lse, dd)


def _sigmoid(z):
    return 1.0 / (1.0 + jnp.exp(-z))


def _tail(x, tgt, o_a, l_a, o_b, l_b, proj, bm, w_a, w_b, w_o, sink_b):
    ts = 256

    def body(x_ref, t_ref, oa_ref, la_ref, ob0_ref, ob1_ref, ob2_ref, lb0_ref, lb1_ref, lb2_ref,
             ga_ref, gb_ref, m0_ref, m1_ref, bm_ref, wa_ref, wb_ref, wo_ref, sk_ref,
             dy_ref, dyb_ref, dt_ref, doa_ref, dda_ref, dob0_ref, dob1_ref, dob2_ref, ddb0_ref, ddb1_ref, ddb2_ref,
             ya_ref, yb_ref, mg_ref, dbra_ref, dbrb_ref, loss_ref, dbm_ref, dsk_ref):
        i = pl.program_id(0)

        @pl.when(i == 0)
        def _():
            loss_ref[...] = jnp.zeros_like(loss_ref)
            dbm_ref[...] = jnp.zeros_like(dbm_ref)
            dsk_ref[...] = jnp.zeros_like(dsk_ref)

        ga = ga_ref[...]
        sa = _sigmoid(ga)
        silu_a = ga * sa
        oa = oa_ref[...]
        ya = oa * silu_a
        gb = gb_ref[...]
        sb = _sigmoid(gb)
        silu_b = gb * sb
        ob = [ob0_ref[...], ob1_ref[...], ob2_ref[...]]
        lb = [lb0_ref[...], lb1_ref[...], lb2_ref[...]]
        mx = jnp.maximum(jnp.maximum(lb[0], lb[1]), lb[2])
        ex = [jnp.exp(v - mx) for v in lb]
        den = ex[0] + ex[1] + ex[2]
        alpha = [e / den for e in ex]
        ybc = alpha[0] * ob[0] + alpha[1] * ob[1] + alpha[2] * ob[2]
        yb = ybc * silu_b
        yab = ya.astype(bf16)
        ybb = yb.astype(bf16)
        br_a = jnp.dot(yab, wa_ref[...], preferred_element_type=f32)
        br_b = jnp.dot(ybb, wb_ref[...], preferred_element_type=f32)
        g0 = _sigmoid(m0_ref[...] + bm_ref[0:1, :])
        g1 = _sigmoid(m1_ref[...] + bm_ref[1:2, :])
        merged = g0 * br_a + g1 * br_b
        mgb = merged.astype(bf16)
        y = x_ref[...] + jnp.dot(mgb, wo_ref[...], preferred_element_type=f32)
        err = y - t_ref[...]
        part = jnp.sum(jnp.sum(err * err, axis=1, keepdims=True), axis=0, keepdims=True)
        loss_ref[...] += part * (0.5 / D)
        dy = err * (1.0 / D)
        dyb = dy.astype(bf16)
        dmerged = lax.dot_general(dyb, wo_ref[...], NT_DIMS, preferred_element_type=f32)
        dbr_a = (dmerged * g0).astype(bf16)
        dbr_b = (dmerged * g1).astype(bf16)
        dm0 = dmerged * br_a * (g0 * (1.0 - g0))
        dm1 = dmerged * br_b * (g1 * (1.0 - g1))
        dbm_ref[0:1, :] += jnp.sum(dm0, axis=0, keepdims=True)
        dbm_ref[1:2, :] += jnp.sum(dm1, axis=0, keepdims=True)
        dya = lax.dot_general(dbr_a, wa_ref[...], NT_DIMS, preferred_element_type=f32)
        dyb2 = lax.dot_general(dbr_b, wb_ref[...], NT_DIMS, preferred_element_type=f32)
        do_a = dya * silu_a
        dga = dya * oa * (sa * (1.0 + ga * (1.0 - sa)))
        ones = _head_ones()
        delta_a = _seg_sum(do_a * oa, ones)
        dsk_ref[...] -= jnp.sum(delta_a * jnp.exp(sk_ref[...] - la_ref[...]), axis=0, keepdims=True)
        dybc = dyb2 * silu_b
        dgb = dyb2 * ybc * (sb * (1.0 + gb * (1.0 - sb)))
        dbar = _seg_sum(dybc * ybc, ones)
        dy_ref[...] = dy
        dyb_ref[...] = dyb
        dt_ref[:, 0:512] = dga.astype(bf16)
        dt_ref[:, 512:1024] = dgb.astype(bf16)
        dt_ref[:, 1024:2048] = dm0.astype(bf16)
        dt_ref[:, 2048:3072] = dm1.astype(bf16)
        doa_ref[...] = do_a.astype(bf16)
        dda_ref[...] = delta_a
        for k, (dob_ref, ddb_ref) in enumerate(((dob0_ref, ddb0_ref), (dob1_ref, ddb1_ref), (dob2_ref, ddb2_ref))):
            dob_ref[...] = alpha[k] * dybc
            ddb_ref[...] = alpha[k] * dbar
        ya_ref[...] = ya.T.astype(bf16)
        yb_ref[...] = yb.T.astype(bf16)
        mg_ref[...] = merged.T.astype(bf16)
        dbra_ref[...] = dbr_a
        dbrb_ref[...] = dbr_b

    def rows(n, blk=0):
        return pl.BlockSpec((ts, n), lambda i: (i, blk))

    def whole(r, c):
        return pl.BlockSpec((r, c), lambda i: (0, 0))

    def cols(n):
        return pl.BlockSpec((n, ts), lambda i: (0, i))

    def gate_cols(n, col):
        return pl.BlockSpec((pl.Element(ts), pl.Element(n)), lambda i: (i * ts, NA + col))

    outs = [
        ((S, D), f32, rows(D)), ((S, D), bf16, rows(D)), ((S, NW), bf16, gate_cols(NT, 0)),
        ((S, 512), bf16, rows(512)), ((S, 512), f32, rows(512)),
        ((S, 512), f32, rows(512)), ((S, 512), f32, rows(512)), ((S, 512), f32, rows(512)),
        ((S, 512), f32, rows(512)), ((S, 512), f32, rows(512)), ((S, 512), f32, rows(512)),
        ((512, S), bf16, cols(512)), ((512, S), bf16, cols(512)), ((D, S), bf16, cols(D)),
        ((S, D), bf16, rows(D)), ((S, D), bf16, rows(D)),
        ((1, 1), f32, whole(1, 1)), ((2, D), f32, whole(2, D)), ((1, 512), f32, whole(1, 512)),
    ]
    return pl.pallas_call(
        body,
        grid=(S // ts,),
        in_specs=[
            rows(D), rows(D), rows(512), rows(512), rows(512), rows(512), rows(512), rows(512), rows(512), rows(512),
            gate_cols(512, 0), gate_cols(512, 512), gate_cols(D, 1024), gate_cols(D, 2048), whole(2, D),
            whole(512, D), whole(512, D), whole(D, D), whole(1, 512),
        ],
        out_specs=[o[2] for o in outs],
        out_shape=[jax.ShapeDtypeStruct(o[0], o[1]) for o in outs],
        compiler_params=_params(("arbitrary",), vmem_mib=60),
        name="tail",
    )(x, tgt, o_a, l_a, *o_b, *l_b, proj, proj, proj, proj, bm, w_a, w_b, w_o, sink_b)


def _norm_bwd(xv, dyv, gain, kind, ones):
    r = lax.rsqrt(_half_sums(xv * xv, ones) * (1.0 / HD) + EPS)
    yv = xv * r
    up = jnp.where(kind == 0, dyv * SCALE, dyv)
    u = up * gain
    dxv = r * (u - yv * (_half_sums(u * yv, ones) * (1.0 / HD)))
    dxv = jnp.where(kind == 2, dyv, dxv)
    dg = jnp.where(kind == 2, 0.0, jnp.sum(up * yv, axis=0, keepdims=True))
    return dxv, dg


def _post_b(g, dqkv, proj_a, gains, dproj):
    d = GROUPS[g][1]
    seq = S // d

    def body(d_ref, p_ref, g_ref, alias_ref, o_ref, dg_ref, nat):
        del alias_ref
        j = pl.program_id(0)
        kind = j // 4
        gain = g_ref[...]
        ones = _head_ones()

        @pl.when(j % 4 == 0)
        def _():
            dg_ref[...] = jnp.zeros_like(dg_ref)

        for c in range(d):
            for i in range(seq // PCHUNK):
                src = c * seq + i * PCHUNK
                if d == 1:
                    idx = slice(src, src + PCHUNK)
                else:
                    idx = pl.ds(c + i * PCHUNK * d, PCHUNK, stride=d)
                dyv = d_ref[PAD + src:PAD + src + PCHUNK, :].astype(f32)
                dxv, dg = _norm_bwd(p_ref[idx, :], dyv, gain, kind, ones)
                nat[idx, :] = dxv
                dg_ref[...] += dg

        for i in range(S // CHUNK):
            o_ref[i * CHUNK:(i + 1) * CHUNK, :] = nat[i * CHUNK:(i + 1) * CHUNK, :].astype(bf16)

    return pl.pallas_call(
        body,
        grid=(12,),
        in_specs=[
            pl.BlockSpec((None, None, SP, LANES), lambda j: (j // 4, j % 4, 0, 0)),
            pl.BlockSpec((S, LANES), lambda j: (0, _col_block(g, jnp.minimum(j, 7)))),
            pl.BlockSpec((None, None, 1, LANES), lambda j: (g, j // 4, 0, 0)),
            pl.BlockSpec(memory_space=pl.ANY),
        ],
        out_specs=[
            pl.BlockSpec((S, LANES), lambda j: (0, _col_block(g, j))),
            pl.BlockSpec((None, 1, LANES), lambda j: (j // 4, 0, 0)),
        ],
        out_shape=[jax.ShapeDtypeStruct((S, NW), bf16), jax.ShapeDtypeStruct((3, 1, LANES), f32)],
        scratch_shapes=[pltpu.VMEM((S, LANES), f32)],
        input_output_aliases={3: 0},
        compiler_params=_params(("arbitrary",)),
        name="post_b%d" % g,
    )(dqkv, proj_a, gains, dproj)


def _post_a(dqkv, proj_a, gains, dproj):
    def body(q_ref, e_ref, p_ref, g_ref, alias_ref, o_ref, dg_ref):
        del alias_ref
        j = pl.program_id(0)
        kind = jnp.maximum(j - 3, 0)
        gain = g_ref[...]
        lo = _lo()
        ones = _head_ones()

        @pl.when((j == 0) | (j >= 4))
        def _():
            dg_ref[...] = jnp.zeros_like(dg_ref)

        for i in range(S // PCHUNK):
            r0 = i * PCHUNK
            rows = slice(PAD + r0, PAD + r0 + PCHUNK)
            t0 = e_ref[0, rows, :].astype(f32) + e_ref[1, rows, :].astype(f32)
            t1 = e_ref[2, rows, :].astype(f32) + e_ref[3, rows, :].astype(f32)
            folded = jnp.where(lo, t0 + pltpu.roll(t0, HD, 1), t1 + pltpu.roll(t1, HD, 1))
            dyv = jnp.where(kind == 0, q_ref[rows, :].astype(f32), folded)
            dxv, dg = _norm_bwd(p_ref[r0:r0 + PCHUNK, :], dyv, gain, kind, ones)
            o_ref[r0:r0 + PCHUNK, :] = dxv.astype(bf16)
            dg_ref[...] += dg

    return pl.pallas_call(
        body,
        grid=(6,),
        in_specs=[
            pl.BlockSpec((None, None, SP, LANES), lambda j: (0, jnp.minimum(j, 3), 0, 0)),
            pl.BlockSpec((None, 4, SP, LANES), lambda j: (jnp.clip(j - 3, 1, 2), 0, 0, 0)),
            pl.BlockSpec((S, LANES), lambda j: (0, jnp.minimum(j, 4))),
            pl.BlockSpec((None, None, 1, LANES), lambda j: (0, jnp.maximum(j - 3, 0), 0, 0)),
            pl.BlockSpec(memory_space=pl.ANY),
        ],
        out_specs=[
            pl.BlockSpec((S, LANES), lambda j: (0, j)),
            pl.BlockSpec((None, 1, LANES), lambda j: (jnp.maximum(j - 3, 0), 0, 0)),
        ],
        out_shape=[jax.ShapeDtypeStruct((S, NW), bf16), jax.ShapeDtypeStruct((3, 1, LANES), f32)],
        input_output_aliases={4: 0},
        compiler_params=_params(("arbitrary",)),
        name="post_a",
    )(dqkv, dqkv, proj_a, gains, dproj)


def _dh_norm_bwd(dproj, w, x, rstd, gain, dy):
    ts = 1024
    tk = NW // 6
    nk = NW // tk

    def body(d_ref, w_ref, x_ref, r_ref, g_ref, dy_ref, gx_ref, dgn_ref, acc):
        i = pl.program_id(0)
        k = pl.program_id(1)

        @pl.when((i == 0) & (k == 0))
        def _():
            dgn_ref[...] = jnp.zeros_like(dgn_ref)

        @pl.when(k == 0)
        def _():
            acc[...] = jnp.zeros_like(acc)

        acc[...] += jnp.dot(d_ref[...], w_ref[...], preferred_element_type=f32)

        @pl.when(k == nk - 1)
        def _():
            dh = acc[...]
            xh = x_ref[...] * r_ref[...]
            u = dh * g_ref[...]
            dx = r_ref[...] * (u - xh * jnp.mean(u * xh, axis=-1, keepdims=True))
            gx_ref[...] = dy_ref[...] + dx
            dgn_ref[...] += jnp.sum(dh * xh, axis=0, keepdims=True)

    return pl.pallas_call(
        body,
        grid=(S // ts, nk),
        in_specs=[
            pl.BlockSpec((ts, tk), lambda i, k: (i, k)),
            pl.BlockSpec((tk, D), lambda i, k: (k, 0)),
            pl.BlockSpec((ts, D), lambda i, k: (i, 0)),
            pl.BlockSpec((ts, 1), lambda i, k: (i, 0)),
            pl.BlockSpec((1, D), lambda i, k: (0, 0)),
            pl.BlockSpec((ts, D), lambda i, k: (i, 0)),
        ],
        out_specs=[pl.BlockSpec((ts, D), lambda i, k: (i, 0)), pl.BlockSpec((1, D), lambda i, k: (0, 0))],
        out_shape=[jax.ShapeDtypeStruct((S, D), f32), jax.ShapeDtypeStruct((1, D), f32)],
        scratch_shapes=[pltpu.VMEM((ts, D), f32)],
        compiler_params=_params(("arbitrary", "arbitrary"), vmem_mib=56),
        name="dh_norm_bwd",
    )(dproj, w, x, rstd, gain, dy)


def _dw_in(hbt, dproj, parity, name):
    tk = 1024
    win = WSH + 96

    def body(par_ref, a_ref, b_ref, o_ref, acc):
        p = 2 * pl.program_id(0) + par_ref[0]
        k = pl.program_id(1)

        @pl.when(k == 0)
        def _():
            acc[...] = jnp.zeros_like(acc)

        acc[...] += jnp.dot(a_ref[...], b_ref[...], preferred_element_type=f32)

        @pl.when(k == S // tk - 1)
        def _():
            acc_t = acc[...].T
            for pp in range(NDEV):
                off = (WSH * pp) % LANES

                @pl.when(p == pp)
                def _():
                    o_ref[...] = acc_t[off:off + WSH, :].astype(bf16)

    return pl.pallas_call(
        body,
        grid_spec=pltpu.PrefetchScalarGridSpec(
            num_scalar_prefetch=1,
            grid=(NDEV // 2, S // tk),
            in_specs=[
                pl.BlockSpec((D, tk), lambda q, k, par: (0, k)),
                pl.BlockSpec((pl.Element(tk), pl.Element(win)),
                             lambda q, k, par: (k * tk, (WSH * (2 * q + par[0])) // LANES * LANES)),
            ],
            out_specs=pl.BlockSpec((None, WSH, D), lambda q, k, par: (q, 0, 0)),
            scratch_shapes=[pltpu.VMEM((D, win), f32)],
        ),
        out_shape=jax.ShapeDtypeStruct((NDEV // 2, WSH, D), bf16),
        compiler_params=_params(("arbitrary", "arbitrary")),
        name=name,
    )(parity, hbt, dproj)


def _matmul_tokens(at, b, name):
    m, n = at.shape[0], b.shape[1]
    tn = 512
    tk = 1024

    def body(a_ref, b_ref, o_ref):
        @pl.when(pl.program_id(1) == 0)
        def _():
            o_ref[...] = jnp.zeros_like(o_ref)

        o_ref[...] += jnp.dot(a_ref[...], b_ref[...], preferred_element_type=f32)

    return pl.pallas_call(
        body,
        grid=(n // tn, S // tk),
        in_specs=[pl.BlockSpec((m, tk), lambda j, k: (0, k)), pl.BlockSpec((tk, tn), lambda j, k: (k, j))],
        out_specs=pl.BlockSpec((m, tn), lambda j, k: (0, j)),
        out_shape=jax.ShapeDtypeStruct((m, n), f32),
        compiler_params=_params(("arbitrary", "arbitrary")),
        name=name,
    )(at, b)


def _exchange(scatter, gather, name):
    arrs = list(scatter) + list(gather)
    n = len(arrs)
    ns = len(scatter)

    def body(*refs):
        ins, outs = refs[:n], refs[n:2 * n]
        send_sems, recv_sems, local_sems = refs[2 * n:]
        x, y, c = lax.axis_index("x"), lax.axis_index("y"), lax.axis_index("c")
        me = 4 * x + 2 * y + c
        local, remote = [], []
        for a in range(n):
            lc = pltpu.make_async_copy(ins[a].at[me] if a < ns else ins[a], outs[a].at[me], local_sems.at[a])
            lc.start()
            local.append(lc)
            for r in range(1, NDEV):
                px = 1 - x if r & 4 else x
                py = 1 - y if r & 2 else y
                pc = 1 - c if r & 1 else c
                cp = pltpu.make_async_remote_copy(
                    src_ref=ins[a].at[4 * px + 2 * py + pc] if a < ns else ins[a],
                    dst_ref=outs[a].at[me],
                    send_sem=send_sems.at[a, r - 1],
                    recv_sem=recv_sems.at[a, r - 1],
                    device_id=(px, py, pc),
                    device_id_type=pl.DeviceIdType.MESH,
                )
                cp.start()
                remote.append(cp)
        for cp in remote:
            cp.wait_recv()
        for cp in remote:
            cp.wait_send()
        for lc in local:
            lc.wait()

    out_shape = [jax.ShapeDtypeStruct(a.shape if i < ns else (NDEV,) + a.shape, a.dtype) for i, a in enumerate(arrs)]
    return pl.pallas_call(
        body,
        in_specs=[pl.BlockSpec(memory_space=pl.ANY)] * n,
        out_specs=[pl.BlockSpec(memory_space=pl.ANY)] * n,
        out_shape=out_shape,
        scratch_shapes=[
            pltpu.SemaphoreType.DMA((n, NDEV - 1)),
            pltpu.SemaphoreType.DMA((n, NDEV - 1)),
            pltpu.SemaphoreType.DMA((n,)),
        ],
        compiler_params=pltpu.CompilerParams(has_side_effects=True),
        name=name,
    )(*arrs)


def _gather_two_level(arrs, name):
    n = len(arrs)

    def body(*refs):
        ins, outs = refs[:n], refs[n:2 * n]
        send_sems, recv_sems, local_sems = refs[2 * n:]
        x, y, c = lax.axis_index("x"), lax.axis_index("y"), lax.axis_index("c")
        me, sibling = (x, y, c), (x, y, 1 - c)
        xn, yn, dg = (1 - x, y, c), (x, 1 - y, c), (1 - x, 1 - y, c)
        relay_origin = (jnp.bitwise_xor(x, c), jnp.bitwise_xor(y, 1 - c), c)
        relay_target = (jnp.bitwise_xor(x, 1 - c), jnp.bitwise_xor(y, c), c)

        def copy(a, k, block, to, src=None):
            slot = outs[a].at[4 * block[0] + 2 * block[1] + block[2]]
            return pltpu.make_async_remote_copy(
                src_ref=slot if src is None else src, dst_ref=slot, send_sem=send_sems.at[a, k],
                recv_sem=recv_sems.at[a, k], device_id=to, device_id_type=pl.DeviceIdType.MESH)

        def other_core(block):
            return (block[0], block[1], 1 - c)

        mine, sent = [], []
        for a in range(n):
            lc = pltpu.make_async_copy(ins[a], outs[a].at[4 * x + 2 * y + c], local_sems.at[a])
            lc.start()
            mine.append(lc)
            sent += [copy(a, 0, me, sibling, src=ins[a]), copy(a, 1, me, xn, src=ins[a]),
                     copy(a, 2, me, yn, src=ins[a])]
        for cp in sent:
            cp.start()
        later = []
        for a in range(n):
            copy(a, 1, xn, me).wait_recv()
            copy(a, 2, yn, me).wait_recv()
            later += [copy(a, 3, relay_origin, relay_target), copy(a, 4, xn, sibling), copy(a, 5, yn, sibling)]
            for cp in later[-3:]:
                cp.start()
        for a in range(n):
            copy(a, 3, dg, me).wait_recv()
            later.append(copy(a, 6, dg, sibling))
            later[-1].start()
        for a in range(n):
            copy(a, 0, sibling, me).wait_recv()
            for k, block in ((4, xn), (5, yn), (6, dg)):
                copy(a, k, other_core(block), me).wait_recv()
        for cp in sent + later:
            cp.wait_send()
        for lc in mine:
            lc.wait()

    return pl.pallas_call(
        body,
        in_specs=[pl.BlockSpec(memory_space=pl.ANY)] * n,
        out_specs=[pl.BlockSpec(memory_space=pl.ANY)] * n,
        out_shape=[jax.ShapeDtypeStruct((NDEV,) + a.shape, a.dtype) for a in arrs],
        scratch_shapes=[
            pltpu.SemaphoreType.DMA((n, NDEV - 1)),
            pltpu.SemaphoreType.DMA((n, NDEV - 1)),
            pltpu.SemaphoreType.DMA((n,)),
        ],
        compiler_params=pltpu.CompilerParams(has_side_effects=True),
        name=name,
    )(*arrs)


_HBM = pl.BlockSpec(memory_space=pltpu.HBM)
_SEM = pl.BlockSpec(memory_space=pltpu.SEMAPHORE)
_EFFECT = pltpu.SideEffectType.DATAFLOW_SIDE_EFFECTING


def _comm_step(name, body_fn, lands, srcs=(), wait_sems=(), n_new=0, after=(), token=False):
    n, ns, nw, na = len(lands), len(srcs), len(wait_sems), len(after)

    def body(*refs):
        src, land = refs[:ns], refs[ns:ns + n]
        waits = refs[ns + n:ns + n + nw]
        new = refs[ns + n + nw + na:ns + n + nw + na + n_new]
        body_fn(src, land, waits, new)
        if token:
            refs[-1][...] = jnp.zeros((8, LANES), f32)

    hbm = [pltpu.HBM(a.shape, a.dtype) for a in lands]
    ops = [pltpu.with_memory_space_constraint(a, pltpu.HBM) for a in list(srcs) + list(lands)]
    extra_shape = [jax.ShapeDtypeStruct((8, LANES), f32)] if token else []
    extra_spec = [pl.BlockSpec(memory_space=pltpu.VMEM)] if token else []
    outs = pl.pallas_call(
        body,
        out_shape=tuple([pltpu.SemaphoreType.DMA(())] * n_new + hbm + extra_shape),
        in_specs=[_HBM] * (ns + n) + [_SEM] * nw + [pl.BlockSpec(memory_space=pl.ANY)] * na,
        out_specs=tuple([_SEM] * n_new + [_HBM] * n + extra_spec),
        input_output_aliases={ns + i: n_new + i for i in range(n)},
        compiler_params=pltpu.CompilerParams(has_side_effects=_EFFECT),
        name=name,
    )(*ops, *wait_sems, *after)
    if token:
        return list(outs[:n_new]), list(outs[n_new:n_new + n]), outs[-1][0, 0]
    return list(outs[:n_new]), list(outs[n_new:])


class _GatheredWeights:
    def __init__(self, shards):
        self.n = n = len(shards)
        x, y, c = lax.axis_index("x"), lax.axis_index("y"), lax.axis_index("c")
        self.x = x
        me = 4 * x + 2 * y + c
        lands = [lax.dynamic_update_slice(lax.empty((NDEV,) + s.shape, s.dtype), s[None], (me,) + (0,) * s.ndim)
                 for s in shards]

        def start_own(src, land, waits, new):
            p = self._peers()
            for a in range(n):
                for k, to in ((0, p["sibling"]), (1, p["xn"]), (2, p["yn"])):
                    self._copy(land[a], new, a, k, 3, p["me"], to).start()

        self.sems, self.lands = {}, None
        new, self.lands = _comm_step("gather_start", start_own, lands, n_new=6 * n)
        self._keep(new, (0, 1, 2))

    @staticmethod
    def _peers():
        x, y, c = lax.axis_index("x"), lax.axis_index("y"), lax.axis_index("c")
        return dict(
            me=(x, y, c), sibling=(x, y, 1 - c), xn=(1 - x, y, c), yn=(x, 1 - y, c), dg=(1 - x, 1 - y, c),
            relay_origin=(jnp.bitwise_xor(x, c), jnp.bitwise_xor(y, 1 - c), c),
            relay_target=(jnp.bitwise_xor(x, 1 - c), jnp.bitwise_xor(y, c), c))

    def _keep(self, new, ks):
        half = len(new) // 2
        i = 0
        for a in range(self.n):
            for k in ks:
                self.sems[a, k] = (new[i], new[half + i])
                i += 1

    @staticmethod
    def _copy(land, sem_refs, a, k, nk, block, to, src=None, ks=None):
        ks = tuple(range(nk)) if ks is None else ks
        half = len(sem_refs) // 2
        i = a * len(ks) + ks.index(k)
        slot = land.at[4 * block[0] + 2 * block[1] + block[2]]
        return pltpu.make_async_remote_copy(
            src_ref=slot if src is None else src, dst_ref=slot, send_sem=sem_refs[i], recv_sem=sem_refs[half + i],
            device_id=to, device_id_type=pl.DeviceIdType.MESH)

    def _sem_list(self, ks):
        return ([self.sems[a, k][0] for a in range(self.n) for k in ks]
                + [self.sems[a, k][1] for a in range(self.n) for k in ks])

    def first_half(self, after):
        n = self.n

        def relay(src, land, waits, new):
            p = self._peers()
            for a in range(n):
                self._copy(land[a], waits, a, 1, 0, p["xn"], p["me"], ks=(1, 2)).wait_recv()
                self._copy(land[a], waits, a, 2, 0, p["yn"], p["me"], ks=(1, 2)).wait_recv()
                self._copy(land[a], new, a, 3, 0, p["relay_origin"], p["relay_target"], ks=(3, 4, 5)).start()
                self._copy(land[a], new, a, 4, 0, p["xn"], p["sibling"], ks=(3, 4, 5)).start()
                self._copy(land[a], new, a, 5, 0, p["yn"], p["sibling"], ks=(3, 4, 5)).start()

        new, self.lands = _comm_step("gather_relay", relay, self.lands, wait_sems=self._sem_list((1, 2)),
                                     n_new=6 * n, after=after)
        self._keep(new, (3, 4, 5))

        def from_sibling(src, land, waits, new):
            p = self._peers()
            other = lambda b: (b[0], b[1], 1 - b[2])
            for a in range(n):
                self._copy(land[a], waits, a, 0, 0, other(p["me"]), p["me"], ks=(0, 4, 5)).wait_recv()
                self._copy(land[a], waits, a, 4, 0, other(p["xn"]), p["me"], ks=(0, 4, 5)).wait_recv()
                self._copy(land[a], waits, a, 5, 0, other(p["yn"]), p["me"], ks=(0, 4, 5)).wait_recv()

        _, self.lands = _comm_step("gather_wait_sibling", from_sibling, self.lands,
                                   wait_sems=self._sem_list((0, 4, 5)))
        return self.lands[0].reshape(NW, D), self.x.astype(jnp.int32).reshape(1)

    def second_half(self, after):
        n = self.n

        def forward_diagonal(src, land, waits, new):
            p = self._peers()
            for a in range(n):
                self._copy(land[a], waits, a, 3, 0, p["dg"], p["me"], ks=(3,)).wait_recv()
                self._copy(land[a], new, a, 6, 0, p["dg"], p["sibling"], ks=(6,)).start()

        new, self.lands = _comm_step("gather_forward_diagonal", forward_diagonal, self.lands,
                                     wait_sems=self._sem_list((3,)), n_new=2 * n, after=after)
        self._keep(new, (6,))

        def finish(src, land, waits, new):
            p = self._peers()
            ks = tuple(range(7))
            for a in range(n):
                self._copy(land[a], waits, a, 6, 0, (p["dg"][0], p["dg"][1], 1 - p["dg"][2]), p["me"], ks=ks).wait_recv()
                for k in ks:
                    self._copy(land[a], waits, a, k, 0, p["me"], p["me"], ks=ks).wait_send()

        _, self.lands = _comm_step("gather_finish", finish, self.lands, wait_sems=self._sem_list(tuple(range(7))))
        return self.lands[0].reshape(NW, D), (1 - self.x).astype(jnp.int32).reshape(1)

    def rest(self):
        g_a, g_b, g_o, g_bm = self.lands[1:]
        return (g_a.transpose(1, 0, 2).reshape(512, D), g_b.transpose(1, 0, 2).reshape(512, D),
                g_bm.transpose(1, 0, 2).reshape(2, D), g_o.reshape(D, D))


def _sibling_send_start(shares):
    landing = lax.empty(shares.shape, shares.dtype)

    def start(src, land, waits, new):
        x, y, c = lax.axis_index("x"), lax.axis_index("y"), lax.axis_index("c")
        pltpu.make_async_remote_copy(src_ref=land[0], dst_ref=land[1], send_sem=new[0], recv_sem=new[1],
                                     device_id=(x, y, 1 - c), device_id_type=pl.DeviceIdType.MESH).start()

    return _comm_step("grad_sibling_start", start, [shares, landing], n_new=2, token=True)


def _sibling_send_wait(sems, lands, after):
    def wait(src, land, waits, new):
        x, y, c = lax.axis_index("x"), lax.axis_index("y"), lax.axis_index("c")
        done = pltpu.make_async_remote_copy(src_ref=land[0], dst_ref=land[1], send_sem=waits[0], recv_sem=waits[1],
                                            device_id=(x, y, c), device_id_type=pl.DeviceIdType.MESH)
        done.wait_send()
        done.wait_recv()

    _, lands = _comm_step("grad_sibling_wait", wait, lands, wait_sems=sems, after=after)
    return lands[1]


def _row_tile(rows, limit=256):
    fits = [t for t in range(16, limit + 1, 16) if rows % t == 0]
    return fits[-1] if fits else rows


def _pair_sum(mine, theirs, name):
    nb, rows, cols = mine.shape
    tr = _row_tile(rows)

    def body(a_ref, b_ref, o_ref):
        o_ref[...] = (a_ref[...].astype(f32) + b_ref[...].astype(f32)).astype(bf16)

    blk = pl.BlockSpec((None, tr, cols), lambda q, i: (q, i, 0))
    return pl.pallas_call(
        body,
        grid=(nb, rows // tr),
        in_specs=[blk, blk],
        out_specs=blk,
        out_shape=jax.ShapeDtypeStruct(mine.shape, bf16),
        compiler_params=_params(("arbitrary", "arbitrary")),
        name=name,
    )(mine, theirs)


def _scatter_start(chip_arrs, all_arrs, name):
    arrs = list(chip_arrs) + list(all_arrs)
    n, nc = len(arrs), len(chip_arrs)
    lands = [lax.empty(((3 if i < nc else NDEV - 1),) + a.shape[1:], a.dtype) for i, a in enumerate(arrs)]

    def body(*refs):
        src, land = refs[:n], refs[n:2 * n]
        send_sems, recv_sems = refs[2 * n:3 * n], refs[3 * n:4 * n]
        token = refs[6 * n]
        x, y, c = lax.axis_index("x"), lax.axis_index("y"), lax.axis_index("c")
        for a in range(n):
            for r in range(1, 4 if a < nc else NDEV):
                if a < nc:
                    px, py, pc = (1 - x if r & 2 else x), (1 - y if r & 1 else y), c
                    block = 2 * px + py
                else:
                    px, py, pc = (1 - x if r & 4 else x), (1 - y if r & 2 else y), (1 - c if r & 1 else c)
                    block = 4 * px + 2 * py + pc
                pltpu.make_async_remote_copy(
                    src_ref=src[a].at[block], dst_ref=land[a].at[r - 1], send_sem=send_sems[a],
                    recv_sem=recv_sems[a], device_id=(px, py, pc), device_id_type=pl.DeviceIdType.MESH).start()
        token[...] = jnp.zeros_like(token)

    hbm = [pltpu.HBM(a.shape, a.dtype) for a in arrs + lands]
    ops = [pltpu.with_memory_space_constraint(a, pltpu.HBM) for a in arrs + lands]
    outs = pl.pallas_call(
        body,
        out_shape=tuple([pltpu.SemaphoreType.DMA(())] * (2 * n) + hbm + [jax.ShapeDtypeStruct((8, LANES), f32)]),
        in_specs=[_HBM] * (2 * n),
        out_specs=tuple([_SEM] * (2 * n) + [_HBM] * (2 * n) + [pl.BlockSpec(memory_space=pltpu.VMEM)]),
        input_output_aliases={i: 2 * n + i for i in range(2 * n)},
        compiler_params=pltpu.CompilerParams(has_side_effects=_EFFECT),
        name=name,
    )(*ops)
    return outs[:n], outs[n:2 * n], outs[2 * n:3 * n], outs[3 * n:4 * n], outs[4 * n]


def _scatter_wait(send_sems, recv_sems, srcs, lands, after, name):
    n = len(srcs)

    def body(*refs):
        land = refs[n:2 * n]
        ssem, rsem = refs[2 * n:3 * n], refs[3 * n:4 * n]
        x, y, c = lax.axis_index("x"), lax.axis_index("y"), lax.axis_index("c")
        for a in range(n):
            done = pltpu.make_async_remote_copy(
                src_ref=land[a], dst_ref=land[a], send_sem=ssem[a], recv_sem=rsem[a], device_id=(x, y, c),
                device_id_type=pl.DeviceIdType.MESH)
            done.wait_send()
            done.wait_recv()

    hbm = [pltpu.HBM(a.shape, a.dtype) for a in list(srcs) + list(lands)]
    outs = pl.pallas_call(
        body,
        out_shape=tuple(hbm),
        in_specs=[_HBM] * (2 * n) + [_SEM] * (2 * n) + [pl.BlockSpec(memory_space=pl.ANY)],
        out_specs=tuple([_HBM] * (2 * n)),
        input_output_aliases={i: i for i in range(2 * n)},
        compiler_params=pltpu.CompilerParams(has_side_effects=_EFFECT),
        name=name,
    )(*srcs, *lands, *send_sems, *recv_sems, after)
    return outs[:n], outs[n:]


def _adam_update(g, w_ref, m_ref, v_ref, g_ref, d_ref, nm_ref, nv_ref):
    mm = ADAM_B1 * m_ref[...] + (1.0 - ADAM_B1) * g
    vv = ADAM_B2 * v_ref[...] + (1.0 - ADAM_B2) * (g * g)
    m_hat = mm / (1.0 - ADAM_B1 ** ADAM_STEP)
    v_hat = vv / (1.0 - ADAM_B2 ** ADAM_STEP)
    g_ref[...] = g
    d_ref[...] = -ADAM_LR * (m_hat / (jnp.sqrt(v_hat) + ADAM_EPS) + ADAM_WD * w_ref[...])
    nm_ref[...] = mm
    nv_ref[...] = vv


def _adamw_own(w, own, own_idx, slots, m, v, name):
    r, c = w.shape[-2:]
    tr = _row_tile(r, 128)
    k = slots.shape[0]

    def body(i_ref, w_ref, o_ref, s_ref, m_ref, v_ref, g_ref, d_ref, nm_ref, nv_ref):
        del i_ref
        g = o_ref[...].astype(f32)
        for j in range(k):
            g = g + s_ref[j].astype(f32)
        _adam_update(g, w_ref, m_ref, v_ref, g_ref, d_ref, nm_ref, nv_ref)

    blk = pl.BlockSpec((None, tr, c), lambda i, ix: (0, i, 0))
    return pl.pallas_call(
        body,
        grid_spec=pltpu.PrefetchScalarGridSpec(
            num_scalar_prefetch=1,
            grid=(r // tr,),
            in_specs=[blk, pl.BlockSpec((None, tr, c), lambda i, ix: (ix[0], i, 0)),
                      pl.BlockSpec((k, tr, c), lambda i, ix: (0, i, 0)), blk, blk],
            out_specs=[blk] * 4,
        ),
        out_shape=[jax.ShapeDtypeStruct(w.shape, f32)] * 4,
        compiler_params=_params(("arbitrary",)),
        name=name,
    )(own_idx, w, own, slots, m, v)


def _adamw(w, slots, m, v, name):
    r, c = w.shape[-2:]
    tr = _row_tile(r, 128)

    def body(w_ref, s_ref, m_ref, v_ref, g_ref, d_ref, nm_ref, nv_ref):
        g = s_ref[0].astype(f32)
        for k in range(1, NDEV):
            g = g + s_ref[k].astype(f32)
        _adam_update(g, w_ref, m_ref, v_ref, g_ref, d_ref, nm_ref, nv_ref)

    if w.ndim == 3:
        blk = pl.BlockSpec((None, tr, c), lambda i: (0, i, 0))
    else:
        blk = pl.BlockSpec((tr, c), lambda i: (i, 0))
    return pl.pallas_call(
        body,
        grid=(r // tr,),
        in_specs=[blk, pl.BlockSpec((NDEV, tr, c), lambda i: (0, i, 0)), blk, blk],
        out_specs=[blk] * 4,
        out_shape=[jax.ShapeDtypeStruct(w.shape, f32)] * 4,
        compiler_params=_params(("arbitrary",)),
        name=name,
    )(w, slots, m, v)


class _Weights:
    def __init__(self, w_t, w_a, w_b, b_merge, w_o):
        self._w_t, self._rest = w_t, (w_a, w_b, b_merge, w_o)

    def first_half(self, after):
        del after
        return self._w_t, jnp.zeros((1,), jnp.int32)

    def second_half(self, after):
        del after
        return self._w_t, jnp.ones((1,), jnp.int32)

    def rest(self):
        return self._rest


def _local_step(x, tgt, norm_gain, weights, qn_a, kn_a, qn_b, kn_b, sink_a, rel_bias, on_weight_grads=None,
                core=None):
    two = lambda t: jnp.concatenate([t, t], axis=-1).reshape(1, LANES)
    ones = jnp.ones((1, LANES), f32)
    gains = jnp.stack([
        jnp.stack([two(qn_a), two(kn_a), ones]),
        jnp.stack([two(qn_b), two(kn_b), ones]),
        jnp.stack([two(qn_b), two(kn_b), ones]),
        jnp.stack([two(qn_b), two(kn_b), ones]),
    ])
    buckets = [jnp.asarray(_bucket_np(blk, d)) for blk, d, _ in GROUPS]
    bias = [_bias_expand(rel_bias, buckets[k], GROUPS[k][2], "bias_expand_%d" % k) for k in range(4)]

    hb, hbt, rstd = _rms(x, norm_gain)
    w_t, half = weights.first_half([hb] + bias)
    proj = _inproj_half(hb, w_t, half, None, "inproj_1")
    w_t, half = weights.second_half([proj])
    proj = _inproj_half(hb, w_t, half, proj, "inproj_2")
    w_a, w_b, b_merge, w_o = weights.rest()
    gl = _prep(proj, gains)
    o_a, l_a = _attn_fwd(gl, bias[0], sink_a.reshape(8), 0, 128, 1, "attn_fwd_a")
    fwd_b = [_attn_fwd(gl, bias[k], None, k, GROUPS[k][0], GROUPS[k][1], "attn_fwd_b%d" % k) for k in (1, 2, 3)]
    sink_b = jnp.repeat(sink_a.reshape(8), HD).reshape(1, 512)

    (dy, dyb, dproj, do_a, dd_a, do_b0, do_b1, do_b2, dd_b0, dd_b1, dd_b2, ya, yb, mg, dbr_a, dbr_b, loss, dbm,
     dsk) = _tail(x, tgt, o_a, l_a, [f[0] for f in fwd_b], [f[1] for f in fwd_b], proj, b_merge, w_a, w_b, w_o, sink_b)

    dw_o = _matmul_tokens(mg, dyb, "dw_out")
    dw_a = _matmul_tokens(ya, dbr_a, "dw_branch_a")
    dw_b = _matmul_tokens(yb, dbr_b, "dw_branch_b")
    if on_weight_grads is not None:
        early = on_weight_grads(dict(w_branch_a=dw_a, w_branch_b=dw_b, b_merge=dbm, w_out=dw_o))
        buckets = [buckets[0] + early.astype(jnp.int32)] + buckets[1:]

    dqkv_a, dbk_a = _attn_bwd(gl, bias[0], buckets[0], do_a, l_a, dd_a, 0, 128, 1, "attn_bwd_a")
    dproj, dg_a = _post_a(dqkv_a, proj, gains, dproj)
    dbk_b, dg_b = [], []
    for k, do_k, dd_k in ((1, do_b0, dd_b0), (2, do_b1, dd_b1), (3, do_b2, dd_b2)):
        dqkv, dbk = _attn_bwd(gl, bias[k], buckets[k], do_k, fwd_b[k - 1][1], dd_k, k, GROUPS[k][0], GROUPS[k][1],
                              "attn_bwd_b%d" % k)
        dproj, dg = _post_b(k, dqkv, proj, gains, dproj)
        dbk_b.append(dbk)
        dg_b.append(dg)
    dg_b = jnp.stack(dg_b)

    core = jnp.zeros((1,), jnp.int32) if core is None else core
    dw_other = _dw_in(hbt, dproj, 1 - core, "dw_in_other")
    sent = jnp.zeros((), f32) if on_weight_grads is None else on_weight_grads(dict(w_in_other=dw_other))
    dw_in = _dw_in(hbt, dproj, core + sent.astype(jnp.int32), "dw_in_own")
    token = jnp.zeros((), f32) if on_weight_grads is None else on_weight_grads(dict(w_in=dw_in))
    grad_x, d_norm_gain = _dh_norm_bwd(dproj, w_t, x, rstd, norm_gain + token, dy)

    fold = lambda t: t[..., :HD] + t[..., HD:]
    d_qn_a = fold(dg_a[0, 0])
    d_kn_a = fold(dg_a[1, 0])
    d_qn_b = fold(dg_b[:, 0, 0].sum(axis=0))
    d_kn_b = fold(dg_b[:, 1, 0].sum(axis=0))
    d_sink = dsk.reshape(8, HD)[:, 0]
    red = jnp.stack([dbk_a] + dbk_b)
    d_rel = red[:, :, 0, :32].reshape(32, 32).T
    return dict(loss=loss, grad_x=grad_x, norm_gain=d_norm_gain, w_in=dw_in, w_in_other=dw_other, q_norm_a=d_qn_a,
                k_norm_a=d_kn_a,
                q_norm_b=d_qn_b, k_norm_b=d_kn_b, sink_a=d_sink, rel_bias=d_rel, w_branch_a=dw_a, w_branch_b=dw_b,
                b_merge=dbm, w_out=dw_o)


SMALL = (("norm_gain", D), ("q_norm_a", HD), ("k_norm_a", HD), ("q_norm_b", HD), ("k_norm_b", HD), ("sink_a", 8),
         ("rel_bias", 1024))
SMALL_PAD = 2432


SMALL_USED = sum(sz for _, sz in SMALL)


def _pack_small(parts, loss=None):
    tail = jnp.zeros((SMALL_PAD - SMALL_USED,), f32)
    if loss is not None:
        tail = tail.at[0].set(loss.reshape(()))
    return jnp.concatenate([parts[n].reshape(-1) for n, _ in SMALL] + [tail]).reshape(1, SMALL_PAD)


def _unpack_small(flat, shapes):
    out, off = {}, 0
    for n, sz in SMALL:
        out[n] = flat[0, off:off + sz].reshape(shapes[n])
        off += sz
    return out


def kernel(x, norm_gain, w_in, q_norm_a, k_norm_a, q_norm_b, k_norm_b, sink_a, rel_bias, w_branch_a, w_branch_b, b_merge, w_out, loss_target, m_norm_gain, m_w_in, m_q_norm_a, m_k_norm_a, m_q_norm_b, m_k_norm_b, m_sink_a, m_rel_bias, m_w_branch_a, m_w_branch_b, m_b_merge, m_w_out, v_norm_gain, v_w_in, v_q_norm_a, v_k_norm_a, v_q_norm_b, v_k_norm_b, v_sink_a, v_rel_bias, v_w_branch_a, v_w_branch_b, v_b_merge, v_w_out):
    csh = D // NDEV
    w_in_t, m_w_in_t, v_w_in_t = (jnp.swapaxes(t, 1, 2) for t in (w_in, m_w_in, v_w_in))
    weights = _GatheredWeights([w_in_t[0].astype(bf16), w_branch_a[0].astype(bf16), w_branch_b[0].astype(bf16),
                                w_out[0].astype(bf16), b_merge[0]])

    pending = {}
    core = lax.axis_index("c").astype(jnp.int32).reshape(1)
    chip = (2 * lax.axis_index("x") + lax.axis_index("y")).astype(jnp.int32).reshape(1)
    me = (2 * chip + core).astype(jnp.int32)

    def start_exchange(gw):
        if "w_in_other" in gw:
            sems, lands, sent = _sibling_send_start(gw["w_in_other"])
            pending["sibling"] = (sems, lands)
            return sent
        if "w_in" in gw:
            from_sibling = _sibling_send_wait(*pending["sibling"], after=[gw["w_in"]])
            chip_sums = _pair_sum(gw["w_in"], from_sibling, "grad_pair_sum")
            pending["w_in"] = _scatter_start([chip_sums], [], "scatter_w_in_start")
            return pending["w_in"][4][0, 0]
        blocks = [gw["w_branch_a"].reshape(512, NDEV, csh).transpose(1, 0, 2).astype(bf16),
                  gw["w_branch_b"].reshape(512, NDEV, csh).transpose(1, 0, 2).astype(bf16),
                  gw["w_out"].reshape(NDEV, csh, D).astype(bf16),
                  gw["b_merge"].reshape(2, NDEV, csh).transpose(1, 0, 2)]
        pending["rest"] = _scatter_start([], blocks, "scatter_rest_start")
        return pending["rest"][4][0, 0]

    loc = _local_step(x[0], loss_target[0], norm_gain, weights, q_norm_a, k_norm_a, q_norm_b, k_norm_b, sink_a,
                      rel_bias, on_weight_grads=start_exchange, core=core)

    small_shapes = dict(norm_gain=(1, D), q_norm_a=(1, HD), k_norm_a=(1, HD), q_norm_b=(1, HD), k_norm_b=(1, HD),
                        sink_a=(1, 8), rel_bias=(32, 32))
    (r_small,) = _exchange([], [_pack_small(loc, loc["loss"])], "gather_small_grads")
    send_sems, recv_sems, srcs, lands, _ = pending["rest"]
    (s_a, s_b, s_o, s_bm), (r_a, r_b, r_o, r_bm) = _scatter_wait(
        send_sems, recv_sems, srcs, lands, r_small, "scatter_rest_wait")
    send_sems, recv_sems, srcs, lands, _ = pending["w_in"]
    (s_in,), (r_in,) = _scatter_wait(send_sems, recv_sems, srcs, lands, r_small, "scatter_w_in_wait")

    given = dict(norm_gain=norm_gain, q_norm_a=q_norm_a, k_norm_a=k_norm_a, q_norm_b=q_norm_b, k_norm_b=k_norm_b,
                 sink_a=sink_a, rel_bias=rel_bias)
    m_small = dict(norm_gain=m_norm_gain, q_norm_a=m_q_norm_a, k_norm_a=m_k_norm_a, q_norm_b=m_q_norm_b,
                   k_norm_b=m_k_norm_b, sink_a=m_sink_a, rel_bias=m_rel_bias)
    v_small = dict(norm_gain=v_norm_gain, q_norm_a=v_q_norm_a, k_norm_a=v_k_norm_a, q_norm_b=v_q_norm_b,
                   k_norm_b=v_k_norm_b, sink_a=v_sink_a, rel_bias=v_rel_bias)
    res = {
        "small": _adamw(_pack_small(given), r_small, _pack_small(m_small), _pack_small(v_small), "adamw_small"),
        "w_in": [jnp.swapaxes(t, 1, 2) for t in
                 _adamw_own(w_in_t, s_in, chip, r_in, m_w_in_t, v_w_in_t, "adamw_w_in")],
        "w_branch_a": _adamw_own(w_branch_a, s_a, me, r_a, m_w_branch_a, v_w_branch_a, "adamw_w_branch_a"),
        "w_branch_b": _adamw_own(w_branch_b, s_b, me, r_b, m_w_branch_b, v_w_branch_b, "adamw_w_branch_b"),
        "b_merge": _adamw_own(b_merge, s_bm, me, r_bm, m_b_merge, v_b_merge, "adamw_b_merge"),
        "w_out": _adamw_own(w_out, s_o, me, r_o, m_w_out, v_w_out, "adamw_w_out"),
    }
    order = ["norm_gain", "w_in", "q_norm_a", "k_norm_a", "q_norm_b", "k_norm_b", "sink_a", "rel_bias", "w_branch_a",
             "w_branch_b", "b_merge", "w_out"]
    outs = []
    for k in range(4):
        small = _unpack_small(res["small"][k], small_shapes)
        for n in order:
            outs.append(small[n] if n in small else res[n][k])
    loss = res["small"][0][0, SMALL_USED]
    return (loss, loc["grad_x"][None], *outs)
```

```python
import math

import numpy as np
import jax
import jax.numpy as jnp
from jax import lax
from jax.experimental import pallas as pl
from jax.experimental.pallas import tpu as pltpu

f32 = jnp.float32
bf16 = jnp.bfloat16

S = 4096
D = 1024
NA = 5376
NT = 3072
NW = NA + NT
WSH = NW // 8
HD = 64
LANES = 128
EPS = 1e-6
NEG = -1e30
SCALE = HD ** -0.5
TQ = 128
PAD = 128
SP = S + 2 * PAD
NDEV = 8
GROUPS = ((128, 1, 0), (64, 1, 8), (64, 4, 16), (64, 16, 24))
CHUNK = 256
PCHUNK = 128
RC = 64

ADAM_LR, ADAM_B1, ADAM_B2, ADAM_EPS, ADAM_WD, ADAM_STEP = 0.001, 0.9, 0.999, 1e-08, 0.01, 10

MIB = 1024 * 1024
NT_DIMS = (((1,), (1,)), ((), ()))
TN_DIMS = (((0,), (0,)), ((), ()))


def _params(sem=None, vmem_mib=48):
    return pltpu.CompilerParams(dimension_semantics=sem, vmem_limit_bytes=vmem_mib * MIB)


def _lo():
    return lax.broadcasted_iota(jnp.int32, (1, LANES), 1) < HD


def _head_ones():
    r = lax.broadcasted_iota(jnp.int32, (LANES, LANES), 0) // HD
    c = lax.broadcasted_iota(jnp.int32, (LANES, LANES), 1) // HD
    return jnp.where(r == c, 1.0, 0.0).astype(bf16)


def _half_sums(x, ones):
    hi = x.astype(bf16)
    mid = (x - hi.astype(f32)).astype(bf16)
    return (jnp.dot(hi, ones, preferred_element_type=f32) + jnp.dot(mid, ones, preferred_element_type=f32))


def _seg_sum(x, ones):
    outs = [_half_sums(x[:, b * LANES:(b + 1) * LANES], ones) for b in range(x.shape[1] // LANES)]
    return outs[0] if len(outs) == 1 else jnp.concatenate(outs, axis=1)


def _bucket_np(blk, stride):
    w = TQ + 2 * blk
    rel = np.arange(w)[None, :] - blk - np.arange(TQ)[:, None]
    band = np.abs(rel) <= blk
    r = rel * stride
    n = np.abs(r)
    nf = np.maximum(n, 8).astype(np.float32)
    large = 8 + (np.log(nf / np.float32(8)) / np.float32(math.log(128.0)) * np.float32(8)).astype(np.int32)
    large = np.minimum(large, 15)
    b = (r > 0).astype(np.int32) * 16 + np.where(n < 8, n, large)
    return np.where(band, b, -1).astype(np.int32)


def _rms(x, gain):
    ts = 512

    def body(x_ref, g_ref, h_ref, ht_ref, r_ref):
        xv = x_ref[...]
        r = lax.rsqrt(jnp.mean(xv * xv, axis=-1, keepdims=True) + EPS)
        h = (xv * r) * g_ref[...]
        h_ref[...] = h.astype(bf16)
        ht_ref[...] = h.T.astype(bf16)
        r_ref[...] = r

    return pl.pallas_call(
        body,
        grid=(S // ts,),
        in_specs=[pl.BlockSpec((ts, D), lambda i: (i, 0)), pl.BlockSpec((1, D), lambda i: (0, 0))],
        out_specs=[pl.BlockSpec((ts, D), lambda i: (i, 0)), pl.BlockSpec((D, ts), lambda i: (0, i)),
                   pl.BlockSpec((ts, 1), lambda i: (i, 0))],
        out_shape=[jax.ShapeDtypeStruct((S, D), bf16), jax.ShapeDtypeStruct((D, S), bf16),
                   jax.ShapeDtypeStruct((S, 1), f32)],
        compiler_params=_params(("arbitrary",)),
        name="rms",
    )(x, gain)


def _inproj_half(hb, w_t, half, proj, name):
    ts = 512
    tn = NW // 2
    per = NW // 2 // tn

    def body(h_idx, h_ref, w_ref, *rest):
        del h_idx
        rest[-1][...] = lax.dot_general(h_ref[...], w_ref[...], NT_DIMS, preferred_element_type=f32)

    in_specs = [pl.BlockSpec((ts, D), lambda i, n, hf: (i, 0)),
                pl.BlockSpec((tn, D), lambda i, n, hf: (hf[0] * per + n, 0))]
    args = [half, hb, w_t]
    aliases = {}
    if proj is not None:
        in_specs.append(pl.BlockSpec(memory_space=pl.ANY))
        args.append(proj)
        aliases = {3: 0}
    return pl.pallas_call(
        body,
        grid_spec=pltpu.PrefetchScalarGridSpec(
            num_scalar_prefetch=1,
            grid=(S // ts, per),
            in_specs=in_specs,
            out_specs=pl.BlockSpec((ts, tn), lambda i, n, hf: (i, hf[0] * per + n)),
        ),
        out_shape=jax.ShapeDtypeStruct((S, NW), f32),
        input_output_aliases=aliases,
        compiler_params=_params(("arbitrary", "arbitrary")),
        name=name,
    )(*args)


def _bias_expand(table, bucket, c0, name):
    tq, w = bucket.shape
    blk = (w - tq) // 2

    def body(tab_ref, bk_ref, o_ref):
        h = pl.program_id(0)
        bk = bk_ref[...]

        def step(b, acc):
            return jnp.where(bk == b, tab_ref[b, c0 + h], acc)

        inner = lax.fori_loop(0, 32, step, jnp.full((tq, w), NEG, f32))
        col = lax.broadcasted_iota(jnp.int32, (1, w), 1)
        o_ref[0] = jnp.where(col < blk, NEG, inner)
        o_ref[1] = inner
        o_ref[2] = jnp.where(col >= tq + blk, NEG, inner)

    return pl.pallas_call(
        body,
        grid=(8,),
        in_specs=[pl.BlockSpec(memory_space=pltpu.SMEM), pl.BlockSpec((tq, w), lambda h: (0, 0))],
        out_specs=pl.BlockSpec((3, None, tq, w), lambda h: (0, h, 0, 0)),
        out_shape=jax.ShapeDtypeStruct((3, 8, tq, w), f32),
        compiler_params=_params(("arbitrary",)),
        name=name,
    )(table, bucket)


def _tile_kind(t, seq):
    m0 = jnp.bitwise_and(t * TQ, seq - 1)
    return jnp.where(m0 == 0, 0, jnp.where(m0 == seq - TQ, 2, 1))


def _col_block(g, j):
    kind = j // 4
    hp = j % 4
    a = jnp.where(kind == 0, hp, 3 + kind)
    b = 6 + 12 * kind + 4 * (g - 1) + hp
    return jnp.where(g == 0, a, b)


def _prep(proj_a, gains):
    def body(p_ref, g_ref, o_ref):
        g = pl.program_id(0)
        j = pl.program_id(1)
        kind = j // 4
        lo = _lo()
        ones = _head_ones()
        half = jnp.where(lo, 0, 1)
        take = (kind == 0) | (half == (j % 4) // 2)
        gain = g_ref[...]
        o_ref[0:PAD, :] = jnp.zeros((PAD, LANES), bf16)
        o_ref[PAD + S:SP, :] = jnp.zeros((PAD, LANES), bf16)

        def norm_store(xv, dst, dup):
            if dup:
                xv = jnp.where(take, xv, pltpu.roll(xv, HD, 1))
            r = lax.rsqrt(_half_sums(xv * xv, ones) * (1.0 / HD) + EPS)
            r = jnp.where(kind == 2, 1.0, r)
            yv = (xv * r) * gain
            yv = jnp.where(kind == 0, yv * SCALE, yv)
            o_ref[PAD + dst:PAD + dst + CHUNK, :] = yv.astype(bf16)

        for gi, (_, d, _) in enumerate(GROUPS):
            @pl.when(g == gi)
            def _():
                seq = S // d
                for c in range(d):
                    for i in range(seq // CHUNK):
                        if d == 1:
                            xv = p_ref[i * CHUNK:(i + 1) * CHUNK, :]
                        else:
                            xv = p_ref[pl.ds(c + i * CHUNK * d, CHUNK, stride=d), :]
                        norm_store(xv, c * seq + i * CHUNK, gi == 0)

    return pl.pallas_call(
        body,
        grid=(4, 12),
        in_specs=[
            pl.BlockSpec((S, LANES), lambda g, j: (0, _col_block(g, j))),
            pl.BlockSpec((None, None, 1, LANES), lambda g, j: (g, j // 4, 0, 0)),
        ],
        out_specs=pl.BlockSpec((None, None, SP, LANES), lambda g, j: (g, j, 0, 0)),
        out_shape=jax.ShapeDtypeStruct((4, 12, SP, LANES), bf16),
        compiler_params=_params(("arbitrary", "arbitrary")),
        name="prep",
    )(proj_a, gains)


def _token_rows(t, r0, n, d):
    if d == 1:
        return pl.ds(pl.multiple_of(t * TQ, TQ) + r0, n)
    per = S // d // TQ
    return pl.ds(((t % per) * TQ + r0) * d + t // per, n, stride=d)


def _stack_heads(t, lo):
    z = jnp.zeros_like(t)
    return jnp.concatenate([jnp.where(lo, t, z), jnp.where(lo, z, t)], axis=0)


def _unstack_heads(t2, lo):
    return jnp.where(lo, t2[:TQ], t2[TQ:])


def _attn_fwd(gl, bias, sink, g, blk, d, name):
    w = TQ + 2 * blk
    seq = S // d
    use_sink = sink is not None

    def body(*refs):
        if use_sink:
            sink_ref, q_ref, k_ref, v_ref, b_ref, o_ref, l_ref, s0, s1, p0, p1, lse_scr = refs
        else:
            q_ref, k_ref, v_ref, b_ref, o_ref, l_ref, s0, s1, p0, p1, lse_scr = refs
        hp = pl.program_id(0)
        lo = _lo()
        s_bufs, p_bufs = (s0, s1), (p0, p1)

        def scores(p, slot):
            for u in range(2):
                f0 = pl.multiple_of((2 * p + u) * TQ, TQ)
                q2 = _stack_heads(q_ref[pl.ds(PAD + f0, TQ), :], lo)
                kw = k_ref[pl.ds(PAD - blk + f0, w), :]
                s_bufs[slot][u] = lax.dot_general(q2, kw, NT_DIMS, preferred_element_type=f32)

        def softmax(p, slot):
            for u in range(2):
                t = 2 * p + u
                kind = _tile_kind(t, seq)
                for h in range(2):
                    for r in range(TQ // RC):
                        rows = slice(h * TQ + r * RC, h * TQ + (r + 1) * RC)
                        logit = s_bufs[slot][u, rows, :] + b_ref[kind, h, r * RC:(r + 1) * RC, :]
                        m = jnp.max(logit, axis=1, keepdims=True)
                        e = jnp.exp(logit - m)
                        lse = m + jnp.log(jnp.sum(e, axis=1, keepdims=True))
                        if use_sink:
                            sk = sink_ref[2 * hp + h]
                            mx = jnp.maximum(lse, sk)
                            lse = mx + jnp.log(jnp.exp(lse - mx) + jnp.exp(sk - mx))
                        p_bufs[slot][u, rows, :] = (e * jnp.exp(m - lse)).astype(bf16)
                        lse_scr[u, rows, :] = jnp.broadcast_to(lse, (RC, LANES))
                l_ref[_token_rows(t, 0, TQ, d), :] = jnp.where(lo, lse_scr[u, 0:TQ, :], lse_scr[u, TQ:2 * TQ, :])

        def values(p, slot):
            for u in range(2):
                t = 2 * p + u
                vw = v_ref[pl.ds(PAD - blk + pl.multiple_of(t * TQ, TQ), w), :]
                o2 = jnp.dot(p_bufs[slot][u], vw, preferred_element_type=f32)
                o_ref[_token_rows(t, 0, TQ, d), :] = _unstack_heads(o2, lo)

        npair = S // TQ // 2
        scores(0, 0)
        scores(1, 1)
        softmax(0, 0)

        def steady(k, carry):
            p = 2 * k + 2
            scores(p, 0)
            softmax(p - 1, 1)
            values(p - 2, 0)
            scores(p + 1, 1)
            softmax(p, 0)
            values(p - 1, 1)
            return carry

        lax.fori_loop(0, (npair - 2) // 2, steady, 0)
        softmax(npair - 1, 1)
        values(npair - 2, 0)
        values(npair - 1, 1)

    in_specs = [
        pl.BlockSpec((None, None, SP, LANES), lambda hp: (g, hp, 0, 0)),
        pl.BlockSpec((None, None, SP, LANES), lambda hp: (g, 4 + hp, 0, 0)),
        pl.BlockSpec((None, None, SP, LANES), lambda hp: (g, 8 + hp, 0, 0)),
        pl.BlockSpec((3, 2, TQ, w), lambda hp: (0, hp, 0, 0)),
    ]
    args = [gl, gl, gl, bias]
    if use_sink:
        in_specs = [pl.BlockSpec(memory_space=pltpu.SMEM)] + in_specs
        args = [sink] + args
    out = pl.BlockSpec((S, LANES), lambda hp: (0, hp))
    return pl.pallas_call(
        body,
        grid=(4,),
        in_specs=in_specs,
        out_specs=[out, out],
        out_shape=[jax.ShapeDtypeStruct((S, 4 * LANES), f32)] * 2,
        scratch_shapes=[pltpu.VMEM((2, 2 * TQ, w), f32), pltpu.VMEM((2, 2 * TQ, w), f32),
                        pltpu.VMEM((2, 2 * TQ, w), bf16), pltpu.VMEM((2, 2 * TQ, w), bf16),
                        pltpu.VMEM((2, 2 * TQ, LANES), f32)],
        compiler_params=_params(("arbitrary",)),
        name=name,
    )(*args)


def _attn_bwd(gl, bias, bucket, do, lse, dd, g, blk, d, name):
    w = TQ + 2 * blk
    seq = S // d

    def body(q_ref, k_ref, v_ref, b_ref, bk_ref, do_ref, l_ref, d_ref, dqkv_ref, dbk_ref,
             db_acc, s0, s1, dp0, dp1, pb0, pb1, ds0, ds1, dk_acc, dv_acc):
        lo = _lo()
        hi = jnp.logical_not(lo)
        dk_acc[...] = jnp.zeros((SP, LANES), f32)
        dv_acc[...] = jnp.zeros((SP, LANES), f32)
        db_acc[...] = jnp.zeros((2 * TQ, w), f32)
        s_bufs, dp_bufs, pb_bufs, ds_bufs = (s0, s1), (dp0, dp1), (pb0, pb1), (ds0, ds1)

        def stacked(t):
            f0 = pl.multiple_of(t * TQ, TQ)
            q2 = _stack_heads(q_ref[pl.ds(PAD + f0, TQ), :], lo)
            do2 = _stack_heads(do_ref[_token_rows(t, 0, TQ, d), :].astype(bf16), lo)
            return f0, q2, do2

        def scores(p, slot):
            for u in range(2):
                f0, q2, do2 = stacked(2 * p + u)
                win = pl.ds(PAD - blk + f0, w)
                s_bufs[slot][u] = lax.dot_general(q2, k_ref[win, :], NT_DIMS, preferred_element_type=f32)
                dp_bufs[slot][u] = lax.dot_general(do2, v_ref[win, :], NT_DIMS, preferred_element_type=f32)

        def grads(p, slot):
            for u in range(2):
                t = 2 * p + u
                kind = _tile_kind(t, seq)
                for h in range(2):
                    msk = lo if h == 0 else hi
                    for r in range(TQ // RC):
                        rows = slice(h * TQ + r * RC, h * TQ + (r + 1) * RC)
                        src = _token_rows(t, r * RC, RC, d)
                        lh = jnp.max(jnp.where(msk, l_ref[src, :], -jnp.inf), axis=1, keepdims=True)
                        dh = jnp.max(jnp.where(msk, d_ref[src, :], -jnp.inf), axis=1, keepdims=True)
                        logit = s_bufs[slot][u, rows, :] + b_ref[kind, h, r * RC:(r + 1) * RC, :]
                        pr = jnp.exp(logit - lh)
                        ds = pr * (dp_bufs[slot][u, rows, :] - dh)
                        db_acc[rows, :] += ds
                        pb_bufs[slot][u, rows, :] = pr.astype(bf16)
                        ds_bufs[slot][u, rows, :] = ds.astype(bf16)

        def accumulate(p, slot):
            for u in range(2):
                f0, q2, do2 = stacked(2 * p + u)
                win = pl.ds(PAD - blk + f0, w)
                dsb = ds_bufs[slot][u]
                dq2 = jnp.dot(dsb, k_ref[win, :], preferred_element_type=f32)
                dqkv_ref[0, pl.ds(PAD + f0, TQ), :] = _unstack_heads(dq2, lo).astype(bf16)
                dk_acc[win, :] += lax.dot_general(dsb, q2, TN_DIMS, preferred_element_type=f32)
                dv_acc[win, :] += lax.dot_general(pb_bufs[slot][u], do2, TN_DIMS, preferred_element_type=f32)

        npair = S // TQ // 2
        scores(0, 0)
        scores(1, 1)
        grads(0, 0)

        def steady(k, carry):
            p = 2 * k + 2
            scores(p, 0)
            grads(p - 1, 1)
            accumulate(p - 2, 0)
            scores(p + 1, 1)
            grads(p, 0)
            accumulate(p - 1, 1)
            return carry

        lax.fori_loop(0, (npair - 2) // 2, steady, 0)
        grads(npair - 1, 1)
        accumulate(npair - 2, 0)
        accumulate(npair - 1, 1)
        for i in range(SP // CHUNK):
            rows = slice(i * CHUNK, (i + 1) * CHUNK)
            dqkv_ref[1, rows, :] = dk_acc[rows, :].astype(bf16)
            dqkv_ref[2, rows, :] = dv_acc[rows, :].astype(bf16)

        bk = bk_ref[...]
        lane = lax.broadcasted_iota(jnp.int32, (8, LANES), 1)
        for h in range(2):
            db = db_acc[h * TQ:(h + 1) * TQ, :]
            acc = jnp.zeros((8, LANES), f32)
            for b in range(32):
                part = jnp.where(bk == b, db, 0.0).reshape(TQ // 8, 8, w).sum(axis=0)
                tot = jnp.sum(jnp.sum(part, axis=1, keepdims=True), axis=0, keepdims=True)
                acc = jnp.where(lane == b, tot, acc)
            dbk_ref[h] = acc

    def gcol(off):
        return pl.BlockSpec((None, None, SP, LANES), lambda hp: (g, off + hp, 0, 0))

    row = pl.BlockSpec((S, LANES), lambda hp: (0, hp))
    return pl.pallas_call(
        body,
        grid=(4,),
        in_specs=[gcol(0), gcol(4), gcol(8), pl.BlockSpec((3, 2, TQ, w), lambda hp: (0, hp, 0, 0)),
                  pl.BlockSpec((TQ, w), lambda hp: (0, 0)), row, row, row],
        out_specs=[pl.BlockSpec((3, None, SP, LANES), lambda hp: (0, hp, 0, 0)),
                   pl.BlockSpec((2, 8, LANES), lambda hp: (hp, 0, 0))],
        out_shape=[
            jax.ShapeDtypeStruct((3, 4, SP, LANES), bf16),
            jax.ShapeDtypeStruct((8, 8, LANES), f32),
        ],
        scratch_shapes=([pltpu.VMEM((2 * TQ, w), f32)] + [pltpu.VMEM((2, 2 * TQ, w), f32)] * 4
                        + [pltpu.VMEM((2, 2 * TQ, w), bf16)] * 4 + [pltpu.VMEM((SP, LANES), f32)] * 2),
        compiler_params=_params(("arbitrary",), vmem_mib=56),
        name=name,
    )(gl, gl, gl, bias, bucket, do, lse, dd)


def _sigmoid(z):
    return 1.0 / (1.0 + jnp.exp(-z))


def _tail(x, tgt, o_a, l_a, o_b, l_b, proj, bm, w_a, w_b, w_o, sink_b):
    ts = 256

    def body(x_ref, t_ref, oa_ref, la_ref, ob0_ref, ob1_ref, ob2_ref, lb0_ref, lb1_ref, lb2_ref,
             ga_ref, gb_ref, m0_ref, m1_ref, bm_ref, wa_ref, wb_ref, wo_ref, sk_ref,
             dy_ref, dyb_ref, dt_ref, doa_ref, dda_ref, dob0_ref, dob1_ref, dob2_ref, ddb0_ref, ddb1_ref, ddb2_ref,
             ya_ref, yb_ref, mg_ref, dbra_ref, dbrb_ref, loss_ref, dbm_ref, dsk_ref):
        i = pl.program_id(0)

        @pl.when(i == 0)
        def _():
            loss_ref[...] = jnp.zeros_like(loss_ref)
            dbm_ref[...] = jnp.zeros_like(dbm_ref)
            dsk_ref[...] = jnp.zeros_like(dsk_ref)

        ga = ga_ref[...]
        sa = _sigmoid(ga)
        silu_a = ga * sa
        oa = oa_ref[...]
        ya = oa * silu_a
        gb = gb_ref[...]
        sb = _sigmoid(gb)
        silu_b = gb * sb
        ob = [ob0_ref[...], ob1_ref[...], ob2_ref[...]]
        lb = [lb0_ref[...], lb1_ref[...], lb2_ref[...]]
        mx = jnp.maximum(jnp.maximum(lb[0], lb[1]), lb[2])
        ex = [jnp.exp(v - mx) for v in lb]
        den = ex[0] + ex[1] + ex[2]
        alpha = [e / den for e in ex]
        ybc = alpha[0] * ob[0] + alpha[1] * ob[1] + alpha[2] * ob[2]
        yb = ybc * silu_b
        yab = ya.astype(bf16)
        ybb = yb.astype(bf16)
        br_a = jnp.dot(yab, wa_ref[...], preferred_element_type=f32)
        br_b = jnp.dot(ybb, wb_ref[...], preferred_element_type=f32)
        g0 = _sigmoid(m0_ref[...] + bm_ref[0:1, :])
        g1 = _sigmoid(m1_ref[...] + bm_ref[1:2, :])
        merged = g0 * br_a + g1 * br_b
        mgb = merged.astype(bf16)
        y = x_ref[...] + jnp.dot(mgb, wo_ref[...], preferred_element_type=f32)
        err = y - t_ref[...]
        part = jnp.sum(jnp.sum(err * err, axis=1, keepdims=True), axis=0, keepdims=True)
        loss_ref[...] += part * (0.5 / D)
        dy = err * (1.0 / D)
        dyb = dy.astype(bf16)
        dmerged = lax.dot_general(dyb, wo_ref[...], NT_DIMS, preferred_element_type=f32)
        dbr_a = (dmerged * g0).astype(bf16)
        dbr_b = (dmerged * g1).astype(bf16)
        dm0 = dmerged * br_a * (g0 * (1.0 - g0))
        dm1 = dmerged * br_b * (g1 * (1.0 - g1))
        dbm_ref[0:1, :] += jnp.sum(dm0, axis=0, keepdims=True)
        dbm_ref[1:2, :] += jnp.sum(dm1, axis=0, keepdims=True)
        dya = lax.dot_general(dbr_a, wa_ref[...], NT_DIMS, preferred_element_type=f32)
        dyb2 = lax.dot_general(dbr_b, wb_ref[...], NT_DIMS, preferred_element_type=f32)
        do_a = dya * silu_a
        dga = dya * oa * (sa * (1.0 + ga * (1.0 - sa)))
        ones = _head_ones()
        delta_a = _seg_sum(do_a * oa, ones)
        dsk_ref[...] -= jnp.sum(delta_a * jnp.exp(sk_ref[...] - la_ref[...]), axis=0, keepdims=True)
        dybc = dyb2 * silu_b
        dgb = dyb2 * ybc * (sb * (1.0 + gb * (1.0 - sb)))
        dbar = _seg_sum(dybc * ybc, ones)
        dy_ref[...] = dy
        dyb_ref[...] = dyb
        dt_ref[:, 0:512] = dga.astype(bf16)
        dt_ref[:, 512:1024] = dgb.astype(bf16)
        dt_ref[:, 1024:2048] = dm0.astype(bf16)
        dt_ref[:, 2048:3072] = dm1.astype(bf16)
        doa_ref[...] = do_a.astype(bf16)
        dda_ref[...] = delta_a
        for k, (dob_ref, ddb_ref) in enumerate(((dob0_ref, ddb0_ref), (dob1_ref, ddb1_ref), (dob2_ref, ddb2_ref))):
            dob_ref[...] = alpha[k] * dybc
            ddb_ref[...] = alpha[k] * dbar
        ya_ref[...] = ya.T.astype(bf16)
        yb_ref[...] = yb.T.astype(bf16)
        mg_ref[...] = merged.T.astype(bf16)
        dbra_ref[...] = dbr_a
        dbrb_ref[...] = dbr_b

    def rows(n, blk=0):
        return pl.BlockSpec((ts, n), lambda i: (i, blk))

    def whole(r, c):
        return pl.BlockSpec((r, c), lambda i: (0, 0))

    def cols(n):
        return pl.BlockSpec((n, ts), lambda i: (0, i))

    def gate_cols(n, col):
        return pl.BlockSpec((pl.Element(ts), pl.Element(n)), lambda i: (i * ts, NA + col))

    outs = [
        ((S, D), f32, rows(D)), ((S, D), bf16, rows(D)), ((S, NW), bf16, gate_cols(NT, 0)),
        ((S, 512), bf16, rows(512)), ((S, 512), f32, rows(512)),
        ((S, 512), f32, rows(512)), ((S, 512), f32, rows(512)), ((S, 512), f32, rows(512)),
        ((S, 512), f32, rows(512)), ((S, 512), f32, rows(512)), ((S, 512), f32, rows(512)),
        ((512, S), bf16, cols(512)), ((512, S), bf16, cols(512)), ((D, S), bf16, cols(D)),
        ((S, D), bf16, rows(D)), ((S, D), bf16, rows(D)),
        ((1, 1), f32, whole(1, 1)), ((2, D), f32, whole(2, D)), ((1, 512), f32, whole(1, 512)),
    ]
    return pl.pallas_call(
        body,
        grid=(S // ts,),
        in_specs=[
            rows(D), rows(D), rows(512), rows(512), rows(512), rows(512), rows(512), rows(512), rows(512), rows(512),
            gate_cols(512, 0), gate_cols(512, 512), gate_cols(D, 1024), gate_cols(D, 2048), whole(2, D),
            whole(512, D), whole(512, D), whole(D, D), whole(1, 512),
        ],
        out_specs=[o[2] for o in outs],
        out_shape=[jax.ShapeDtypeStruct(o[0], o[1]) for o in outs],
        compiler_params=_params(("arbitrary",), vmem_mib=60),
        name="tail",
    )(x, tgt, o_a, l_a, *o_b, *l_b, proj, proj, proj, proj, bm, w_a, w_b, w_o, sink_b)


def _norm_bwd(xv, dyv, gain, kind, ones):
    r = lax.rsqrt(_half_sums(xv * xv, ones) * (1.0 / HD) + EPS)
    yv = xv * r
    up = jnp.where(kind == 0, dyv * SCALE, dyv)
    u = up * gain
    dxv = r * (u - yv * (_half_sums(u * yv, ones) * (1.0 / HD)))
    dxv = jnp.where(kind == 2, dyv, dxv)
    dg = jnp.where(kind == 2, 0.0, jnp.sum(up * yv, axis=0, keepdims=True))
    return dxv, dg


def _post_b(g, dqkv, proj_a, gains, dproj):
    d = GROUPS[g][1]
    seq = S // d

    def body(d_ref, p_ref, g_ref, alias_ref, o_ref, dg_ref, nat):
        del alias_ref
        j = pl.program_id(0)
        kind = j // 4
        gain = g_ref[...]
        ones = _head_ones()

        @pl.when(j % 4 == 0)
        def _():
            dg_ref[...] = jnp.zeros_like(dg_ref)

        for c in range(d):
            for i in range(seq // PCHUNK):
                src = c * seq + i * PCHUNK
                if d == 1:
                    idx = slice(src, src + PCHUNK)
                else:
                    idx = pl.ds(c + i * PCHUNK * d, PCHUNK, stride=d)
                dyv = d_ref[PAD + src:PAD + src + PCHUNK, :].astype(f32)
                dxv, dg = _norm_bwd(p_ref[idx, :], dyv, gain, kind, ones)
                nat[idx, :] = dxv
                dg_ref[...] += dg

        for i in range(S // CHUNK):
            o_ref[i * CHUNK:(i + 1) * CHUNK, :] = nat[i * CHUNK:(i + 1) * CHUNK, :].astype(bf16)

    return pl.pallas_call(
        body,
        grid=(12,),
        in_specs=[
            pl.BlockSpec((None, None, SP, LANES), lambda j: (j // 4, j % 4, 0, 0)),
            pl.BlockSpec((S, LANES), lambda j: (0, _col_block(g, jnp.minimum(j, 7)))),
            pl.BlockSpec((None, None, 1, LANES), lambda j: (g, j // 4, 0, 0)),
            pl.BlockSpec(memory_space=pl.ANY),
        ],
        out_specs=[
            pl.BlockSpec((S, LANES), lambda j: (0, _col_block(g, j))),
            pl.BlockSpec((None, 1, LANES), lambda j: (j // 4, 0, 0)),
        ],
        out_shape=[jax.ShapeDtypeStruct((S, NW), bf16), jax.ShapeDtypeStruct((3, 1, LANES), f32)],
        scratch_shapes=[pltpu.VMEM((S, LANES), f32)],
        input_output_aliases={3: 0},
        compiler_params=_params(("arbitrary",)),
        name="post_b%d" % g,
    )(dqkv, proj_a, gains, dproj)


def _post_a(dqkv, proj_a, gains, dproj):
    def body(q_ref, e_ref, p_ref, g_ref, alias_ref, o_ref, dg_ref):
        del alias_ref
        j = pl.program_id(0)
        kind = jnp.maximum(j - 3, 0)
        gain = g_ref[...]
        lo = _lo()
        ones = _head_ones()

        @pl.when((j == 0) | (j >= 4))
        def _():
            dg_ref[...] = jnp.zeros_like(dg_ref)

        for i in range(S // PCHUNK):
            r0 = i * PCHUNK
            rows = slice(PAD + r0, PAD + r0 + PCHUNK)
            t0 = e_ref[0, rows, :].astype(f32) + e_ref[1, rows, :].astype(f32)
            t1 = e_ref[2, rows, :].astype(f32) + e_ref[3, rows, :].astype(f32)
            folded = jnp.where(lo, t0 + pltpu.roll(t0, HD, 1), t1 + pltpu.roll(t1, HD, 1))
            dyv = jnp.where(kind == 0, q_ref[rows, :].astype(f32), folded)
            dxv, dg = _norm_bwd(p_ref[r0:r0 + PCHUNK, :], dyv, gain, kind, ones)
            o_ref[r0:r0 + PCHUNK, :] = dxv.astype(bf16)
            dg_ref[...] += dg

    return pl.pallas_call(
        body,
        grid=(6,),
        in_specs=[
            pl.BlockSpec((None, None, SP, LANES), lambda j: (0, jnp.minimum(j, 3), 0, 0)),
            pl.BlockSpec((None, 4, SP, LANES), lambda j: (jnp.clip(j - 3, 1, 2), 0, 0, 0)),
            pl.BlockSpec((S, LANES), lambda j: (0, jnp.minimum(j, 4))),
            pl.BlockSpec((None, None, 1, LANES), lambda j: (0, jnp.maximum(j - 3, 0), 0, 0)),
            pl.BlockSpec(memory_space=pl.ANY),
        ],
        out_specs=[
            pl.BlockSpec((S, LANES), lambda j: (0, j)),
            pl.BlockSpec((None, 1, LANES), lambda j: (jnp.maximum(j - 3, 0), 0, 0)),
        ],
        out_shape=[jax.ShapeDtypeStruct((S, NW), bf16), jax.ShapeDtypeStruct((3, 1, LANES), f32)],
        input_output_aliases={4: 0},
        compiler_params=_params(("arbitrary",)),
        name="post_a",
    )(dqkv, dqkv, proj_a, gains, dproj)


def _dh_norm_bwd(dproj, w, x, rstd, gain, dy):
    ts = 1024
    tk = NW // 6
    nk = NW // tk

    def body(d_ref, w_ref, x_ref, r_ref, g_ref, dy_ref, gx_ref, dgn_ref, acc):
        i = pl.program_id(0)
        k = pl.program_id(1)

        @pl.when((i == 0) & (k == 0))
        def _():
            dgn_ref[...] = jnp.zeros_like(dgn_ref)

        @pl.when(k == 0)
        def _():
            acc[...] = jnp.zeros_like(acc)

        acc[...] += jnp.dot(d_ref[...], w_ref[...], preferred_element_type=f32)

        @pl.when(k == nk - 1)
        def _():
            dh = acc[...]
            xh = x_ref[...] * r_ref[...]
            u = dh * g_ref[...]
            dx = r_ref[...] * (u - xh * jnp.mean(u * xh, axis=-1, keepdims=True))
            gx_ref[...] = dy_ref[...] + dx
            dgn_ref[...] += jnp.sum(dh * xh, axis=0, keepdims=True)

    return pl.pallas_call(
        body,
        grid=(S // ts, nk),
        in_specs=[
            pl.BlockSpec((ts, tk), lambda i, k: (i, k)),
            pl.BlockSpec((tk, D), lambda i, k: (k, 0)),
            pl.BlockSpec((ts, D), lambda i, k: (i, 0)),
            pl.BlockSpec((ts, 1), lambda i, k: (i, 0)),
            pl.BlockSpec((1, D), lambda i, k: (0, 0)),
            pl.BlockSpec((ts, D), lambda i, k: (i, 0)),
        ],
        out_specs=[pl.BlockSpec((ts, D), lambda i, k: (i, 0)), pl.BlockSpec((1, D), lambda i, k: (0, 0))],
        out_shape=[jax.ShapeDtypeStruct((S, D), f32), jax.ShapeDtypeStruct((1, D), f32)],
        scratch_shapes=[pltpu.VMEM((ts, D), f32)],
        compiler_params=_params(("arbitrary", "arbitrary"), vmem_mib=56),
        name="dh_norm_bwd",
    )(dproj, w, x, rstd, gain, dy)


def _dw_in(hbt, dproj, parity, name):
    tk = 1024
    win = WSH + 96

    def body(par_ref, a_ref, b_ref, o_ref, acc):
        p = 2 * pl.program_id(0) + par_ref[0]
        k = pl.program_id(1)

        @pl.when(k == 0)
        def _():
            acc[...] = jnp.zeros_like(acc)

        acc[...] += jnp.dot(a_ref[...], b_ref[...], preferred_element_type=f32)

        @pl.when(k == S // tk - 1)
        def _():
            acc_t = acc[...].T
            for pp in range(NDEV):
                off = (WSH * pp) % LANES

                @pl.when(p == pp)
                def _():
                    o_ref[...] = acc_t[off:off + WSH, :].astype(bf16)

    return pl.pallas_call(
        body,
        grid_spec=pltpu.PrefetchScalarGridSpec(
            num_scalar_prefetch=1,
            grid=(NDEV // 2, S // tk),
            in_specs=[
                pl.BlockSpec((D, tk), lambda q, k, par: (0, k)),
                pl.BlockSpec((pl.Element(tk), pl.Element(win)),
                             lambda q, k, par: (k * tk, (WSH * (2 * q + par[0])) // LANES * LANES)),
            ],
            out_specs=pl.BlockSpec((None, WSH, D), lambda q, k, par: (q, 0, 0)),
            scratch_shapes=[pltpu.VMEM((D, win), f32)],
        ),
        out_shape=jax.ShapeDtypeStruct((NDEV // 2, WSH, D), bf16),
        compiler_params=_params(("arbitrary", "arbitrary")),
        name=name,
    )(parity, hbt, dproj)


def _matmul_tokens(at, b, name):
    m, n = at.shape[0], b.shape[1]
    tn = 512
    tk = 1024

    def body(a_ref, b_ref, o_ref):
        @pl.when(pl.program_id(1) == 0)
        def _():
            o_ref[...] = jnp.zeros_like(o_ref)

        o_ref[...] += jnp.dot(a_ref[...], b_ref[...], preferred_element_type=f32)

    return pl.pallas_call(
        body,
        grid=(n // tn, S // tk),
        in_specs=[pl.BlockSpec((m, tk), lambda j, k: (0, k)), pl.BlockSpec((tk, tn), lambda j, k: (k, j))],
        out_specs=pl.BlockSpec((m, tn), lambda j, k: (0, j)),
        out_shape=jax.ShapeDtypeStruct((m, n), f32),
        compiler_params=_params(("arbitrary", "arbitrary")),
        name=name,
    )(at, b)


def _exchange(scatter, gather, name):
    arrs = list(scatter) + list(gather)
    n = len(arrs)
    ns = len(scatter)

    def body(*refs):
        ins, outs = refs[:n], refs[n:2 * n]
        send_sems, recv_sems, local_sems = refs[2 * n:]
        x, y, c = lax.axis_index("x"), lax.axis_index("y"), lax.axis_index("c")
        me = 4 * x + 2 * y + c
        local, remote = [], []
        for a in range(n):
            lc = pltpu.make_async_copy(ins[a].at[me] if a < ns else ins[a], outs[a].at[me], local_sems.at[a])
            lc.start()
            local.append(lc)
            for r in range(1, NDEV):
                px = 1 - x if r & 4 else x
                py = 1 - y if r & 2 else y
                pc = 1 - c if r & 1 else c
                cp = pltpu.make_async_remote_copy(
                    src_ref=ins[a].at[4 * px + 2 * py + pc] if a < ns else ins[a],
                    dst_ref=outs[a].at[me],
                    send_sem=send_sems.at[a, r - 1],
                    recv_sem=recv_sems.at[a, r - 1],
                    device_id=(px, py, pc),
                    device_id_type=pl.DeviceIdType.MESH,
                )
                cp.start()
                remote.append(cp)
        for cp in remote:
            cp.wait_recv()
        for cp in remote:
            cp.wait_send()
        for lc in local:
            lc.wait()

    out_shape = [jax.ShapeDtypeStruct(a.shape if i < ns else (NDEV,) + a.shape, a.dtype) for i, a in enumerate(arrs)]
    return pl.pallas_call(
        body,
        in_specs=[pl.BlockSpec(memory_space=pl.ANY)] * n,
        out_specs=[pl.BlockSpec(memory_space=pl.ANY)] * n,
        out_shape=out_shape,
        scratch_shapes=[
            pltpu.SemaphoreType.DMA((n, NDEV - 1)),
            pltpu.SemaphoreType.DMA((n, NDEV - 1)),
            pltpu.SemaphoreType.DMA((n,)),
        ],
        compiler_params=pltpu.CompilerParams(has_side_effects=True),
        name=name,
    )(*arrs)


_HBM = pl.BlockSpec(memory_space=pltpu.HBM)
_SEM = pl.BlockSpec(memory_space=pltpu.SEMAPHORE)
_EFFECT = pltpu.SideEffectType.DATAFLOW_SIDE_EFFECTING


def _comm_step(name, body_fn, lands, srcs=(), wait_sems=(), n_new=0, after=(), token=False):
    n, ns, nw, na = len(lands), len(srcs), len(wait_sems), len(after)

    def body(*refs):
        src, land = refs[:ns], refs[ns:ns + n]
        waits = refs[ns + n:ns + n + nw]
        new = refs[ns + n + nw + na:ns + n + nw + na + n_new]
        body_fn(src, land, waits, new)
        if token:
            refs[-1][...] = jnp.zeros((8, LANES), f32)

    hbm = [pltpu.HBM(a.shape, a.dtype) for a in lands]
    ops = [pltpu.with_memory_space_constraint(a, pltpu.HBM) for a in list(srcs) + list(lands)]
    extra_shape = [jax.ShapeDtypeStruct((8, LANES), f32)] if token else []
    extra_spec = [pl.BlockSpec(memory_space=pltpu.VMEM)] if token else []
    outs = pl.pallas_call(
        body,
        out_shape=tuple([pltpu.SemaphoreType.DMA(())] * n_new + hbm + extra_shape),
        in_specs=[_HBM] * (ns + n) + [_SEM] * nw + [pl.BlockSpec(memory_space=pl.ANY)] * na,
        out_specs=tuple([_SEM] * n_new + [_HBM] * n + extra_spec),
        input_output_aliases={ns + i: n_new + i for i in range(n)},
        compiler_params=pltpu.CompilerParams(has_side_effects=_EFFECT),
        name=name,
    )(*ops, *wait_sems, *after)
    if token:
        return list(outs[:n_new]), list(outs[n_new:n_new + n]), outs[-1][0, 0]
    return list(outs[:n_new]), list(outs[n_new:])


class _GatheredWeights:
    def __init__(self, shards):
        self.n = n = len(shards)
        x, y, c = lax.axis_index("x"), lax.axis_index("y"), lax.axis_index("c")
        self.x = x
        me = 4 * x + 2 * y + c
        lands = [lax.dynamic_update_slice(lax.empty((NDEV,) + s.shape, s.dtype), s[None], (me,) + (0,) * s.ndim)
                 for s in shards]

        def start_own(src, land, waits, new):
            p = self._peers()
            for a in range(n):
                for k, to in ((0, p["sibling"]), (1, p["xn"]), (2, p["yn"])):
                    self._copy(land[a], new, a, k, 3, p["me"], to).start()

        self.sems, self.lands = {}, None
        new, self.lands = _comm_step("gather_start", start_own, lands, n_new=6 * n)
        self._keep(new, (0, 1, 2))

    @staticmethod
    def _peers():
        x, y, c = lax.axis_index("x"), lax.axis_index("y"), lax.axis_index("c")
        return dict(
            me=(x, y, c), sibling=(x, y, 1 - c), xn=(1 - x, y, c), yn=(x, 1 - y, c), dg=(1 - x, 1 - y, c),
            relay_origin=(jnp.bitwise_xor(x, c), jnp.bitwise_xor(y, 1 - c), c),
            relay_target=(jnp.bitwise_xor(x, 1 - c), jnp.bitwise_xor(y, c), c))

    def _keep(self, new, ks):
        half = len(new) // 2
        i = 0
        for a in range(self.n):
            for k in ks:
                self.sems[a, k] = (new[i], new[half + i])
                i += 1

    @staticmethod
    def _copy(land, sem_refs, a, k, nk, block, to, src=None, ks=None):
        ks = tuple(range(nk)) if ks is None else ks
        half = len(sem_refs) // 2
        i = a * len(ks) + ks.index(k)
        slot = land.at[4 * block[0] + 2 * block[1] + block[2]]
        return pltpu.make_async_remote_copy(
            src_ref=slot if src is None else src, dst_ref=slot, send_sem=sem_refs[i], recv_sem=sem_refs[half + i],
            device_id=to, device_id_type=pl.DeviceIdType.MESH)

    def _sem_list(self, ks):
        return ([self.sems[a, k][0] for a in range(self.n) for k in ks]
                + [self.sems[a, k][1] for a in range(self.n) for k in ks])

    def first_half(self, after):
        n = self.n

        def relay(src, land, waits, new):
            p = self._peers()
            for a in range(n):
                self._copy(land[a], waits, a, 1, 0, p["xn"], p["me"], ks=(1, 2)).wait_recv()
                self._copy(land[a], waits, a, 2, 0, p["yn"], p["me"], ks=(1, 2)).wait_recv()
                self._copy(land[a], new, a, 3, 0, p["relay_origin"], p["relay_target"], ks=(3, 4, 5)).start()
                self._copy(land[a], new, a, 4, 0, p["xn"], p["sibling"], ks=(3, 4, 5)).start()
                self._copy(land[a], new, a, 5, 0, p["yn"], p["sibling"], ks=(3, 4, 5)).start()

        new, self.lands = _comm_step("gather_relay", relay, self.lands, wait_sems=self._sem_list((1, 2)),
                                     n_new=6 * n, after=after)
        self._keep(new, (3, 4, 5))

        def from_sibling(src, land, waits, new):
            p = self._peers()
            other = lambda b: (b[0], b[1], 1 - b[2])
            for a in range(n):
                self._copy(land[a], waits, a, 0, 0, other(p["me"]), p["me"], ks=(0, 4, 5)).wait_recv()
                self._copy(land[a], waits, a, 4, 0, other(p["xn"]), p["me"], ks=(0, 4, 5)).wait_recv()
                self._copy(land[a], waits, a, 5, 0, other(p["yn"]), p["me"], ks=(0, 4, 5)).wait_recv()

        _, self.lands = _comm_step("gather_wait_sibling", from_sibling, self.lands,
                                   wait_sems=self._sem_list((0, 4, 5)))
        return self.lands[0].reshape(NW, D), self.x.astype(jnp.int32).reshape(1)

    def second_half(self, after):
        n = self.n

        def forward_diagonal(src, land, waits, new):
            p = self._peers()
            for a in range(n):
                self._copy(land[a], waits, a, 3, 0, p["dg"], p["me"], ks=(3,)).wait_recv()
                self._copy(land[a], new, a, 6, 0, p["dg"], p["sibling"], ks=(6,)).start()

        new, self.lands = _comm_step("gather_forward_diagonal", forward_diagonal, self.lands,
                                     wait_sems=self._sem_list((3,)), n_new=2 * n, after=after)
        self._keep(new, (6,))

        def finish(src, land, waits, new):
            p = self._peers()
            ks = tuple(range(7))
            for a in range(n):
                self._copy(land[a], waits, a, 6, 0, (p["dg"][0], p["dg"][1], 1 - p["dg"][2]), p["me"], ks=ks).wait_recv()
                for k in ks:
                    self._copy(land[a], waits, a, k, 0, p["me"], p["me"], ks=ks).wait_send()

        _, self.lands = _comm_step("gather_finish", finish, self.lands, wait_sems=self._sem_list(tuple(range(7))))
        return self.lands[0].reshape(NW, D), (1 - self.x).astype(jnp.int32).reshape(1)

    def rest(self):
        g_a, g_b, g_o, g_bm = self.lands[1:]
        return (g_a.transpose(1, 0, 2).reshape(512, D), g_b.transpose(1, 0, 2).reshape(512, D),
                g_bm.transpose(1, 0, 2).reshape(2, D), g_o.reshape(D, D))


def _sibling_send_start(shares):
    landing = lax.empty(shares.shape, shares.dtype)

    def start(src, land, waits, new):
        x, y, c = lax.axis_index("x"), lax.axis_index("y"), lax.axis_index("c")
        pltpu.make_async_remote_copy(src_ref=land[0], dst_ref=land[1], send_sem=new[0], recv_sem=new[1],
                                     device_id=(x, y, 1 - c), device_id_type=pl.DeviceIdType.MESH).start()

    return _comm_step("grad_sibling_start", start, [shares, landing], n_new=2, token=True)


def _sibling_send_wait(sems, lands, after):
    def wait(src, land, waits, new):
        x, y, c = lax.axis_index("x"), lax.axis_index("y"), lax.axis_index("c")
        done = pltpu.make_async_remote_copy(src_ref=land[0], dst_ref=land[1], send_sem=waits[0], recv_sem=waits[1],
                                            device_id=(x, y, c), device_id_type=pl.DeviceIdType.MESH)
        done.wait_send()
        done.wait_recv()

    _, lands = _comm_step("grad_sibling_wait", wait, lands, wait_sems=sems, after=after)
    return lands[1]


def _row_tile(rows, limit=256):
    fits = [t for t in range(16, limit + 1, 16) if rows % t == 0]
    return fits[-1] if fits else rows


def _pair_sum(mine, theirs, name):
    nb, rows, cols = mine.shape
    tr = _row_tile(rows)

    def body(a_ref, b_ref, o_ref):
        o_ref[...] = (a_ref[...].astype(f32) + b_ref[...].astype(f32)).astype(bf16)

    blk = pl.BlockSpec((None, tr, cols), lambda q, i: (q, i, 0))
    return pl.pallas_call(
        body,
        grid=(nb, rows // tr),
        in_specs=[blk, blk],
        out_specs=blk,
        out_shape=jax.ShapeDtypeStruct(mine.shape, bf16),
        compiler_params=_params(("arbitrary", "arbitrary")),
        name=name,
    )(mine, theirs)


def _scatter_start(chip_arrs, all_arrs, name):
    arrs = list(chip_arrs) + list(all_arrs)
    n, nc = len(arrs), len(chip_arrs)
    lands = [lax.empty(((3 if i < nc else NDEV - 1),) + a.shape[1:], a.dtype) for i, a in enumerate(arrs)]

    def body(*refs):
        src, land = refs[:n], refs[n:2 * n]
        send_sems, recv_sems = refs[2 * n:3 * n], refs[3 * n:4 * n]
        token = refs[6 * n]
        x, y, c = lax.axis_index("x"), lax.axis_index("y"), lax.axis_index("c")
        for a in range(n):
            for r in range(1, 4 if a < nc else NDEV):
                if a < nc:
                    px, py, pc = (1 - x if r & 2 else x), (1 - y if r & 1 else y), c
                    block = 2 * px + py
                else:
                    px, py, pc = (1 - x if r & 4 else x), (1 - y if r & 2 else y), (1 - c if r & 1 else c)
                    block = 4 * px + 2 * py + pc
                pltpu.make_async_remote_copy(
                    src_ref=src[a].at[block], dst_ref=land[a].at[r - 1], send_sem=send_sems[a],
                    recv_sem=recv_sems[a], device_id=(px, py, pc), device_id_type=pl.DeviceIdType.MESH).start()
        token[...] = jnp.zeros_like(token)

    hbm = [pltpu.HBM(a.shape, a.dtype) for a in arrs + lands]
    ops = [pltpu.with_memory_space_constraint(a, pltpu.HBM) for a in arrs + lands]
    outs = pl.pallas_call(
        body,
        out_shape=tuple([pltpu.SemaphoreType.DMA(())] * (2 * n) + hbm + [jax.ShapeDtypeStruct((8, LANES), f32)]),
        in_specs=[_HBM] * (2 * n),
        out_specs=tuple([_SEM] * (2 * n) + [_HBM] * (2 * n) + [pl.BlockSpec(memory_space=pltpu.VMEM)]),
        input_output_aliases={i: 2 * n + i for i in range(2 * n)},
        compiler_params=pltpu.CompilerParams(has_side_effects=_EFFECT),
        name=name,
    )(*ops)
    return outs[:n], outs[n:2 * n], outs[2 * n:3 * n], outs[3 * n:4 * n], outs[4 * n]


def _scatter_wait(send_sems, recv_sems, srcs, lands, after, name):
    n = len(srcs)

    def body(*refs):
        land = refs[n:2 * n]
        ssem, rsem = refs[2 * n:3 * n], refs[3 * n:4 * n]
        x, y, c = lax.axis_index("x"), lax.axis_index("y"), lax.axis_index("c")
        for a in range(n):
            done = pltpu.make_async_remote_copy(
                src_ref=land[a], dst_ref=land[a], send_sem=ssem[a], recv_sem=rsem[a], device_id=(x, y, c),
                device_id_type=pl.DeviceIdType.MESH)
            done.wait_send()
            done.wait_recv()

    hbm = [pltpu.HBM(a.shape, a.dtype) for a in list(srcs) + list(lands)]
    outs = pl.pallas_call(
        body,
        out_shape=tuple(hbm),
        in_specs=[_HBM] * (2 * n) + [_SEM] * (2 * n) + [pl.BlockSpec(memory_space=pl.ANY)],
        out_specs=tuple([_HBM] * (2 * n)),
        input_output_aliases={i: i for i in range(2 * n)},
        compiler_params=pltpu.CompilerParams(has_side_effects=_EFFECT),
        name=name,
    )(*srcs, *lands, *send_sems, *recv_sems, after)
    return outs[:n], outs[n:]


def _adam_update(g, w_ref, m_ref, v_ref, g_ref, d_ref, nm_ref, nv_ref):
    mm = ADAM_B1 * m_ref[...] + (1.0 - ADAM_B1) * g
    vv = ADAM_B2 * v_ref[...] + (1.0 - ADAM_B2) * (g * g)
    m_hat = mm / (1.0 - ADAM_B1 ** ADAM_STEP)
    v_hat = vv / (1.0 - ADAM_B2 ** ADAM_STEP)
    g_ref[...] = g
    d_ref[...] = -ADAM_LR * (m_hat / (jnp.sqrt(v_hat) + ADAM_EPS) + ADAM_WD * w_ref[...])
    nm_ref[...] = mm
    nv_ref[...] = vv


def _adamw_own(w, own, own_idx, slots, m, v, name):
    r, c = w.shape[-2:]
    tr = _row_tile(r, 128)
    k = slots.shape[0]

    def body(i_ref, w_ref, o_ref, s_ref, m_ref, v_ref, g_ref, d_ref, nm_ref, nv_ref):
        del i_ref
        g = o_ref[...].astype(f32)
        for j in range(k):
            g = g + s_ref[j].astype(f32)
        _adam_update(g, w_ref, m_ref, v_ref, g_ref, d_ref, nm_ref, nv_ref)

    blk = pl.BlockSpec((None, tr, c), lambda i, ix: (0, i, 0))
    return pl.pallas_call(
        body,
        grid_spec=pltpu.PrefetchScalarGridSpec(
            num_scalar_prefetch=1,
            grid=(r // tr,),
            in_specs=[blk, pl.BlockSpec((None, tr, c), lambda i, ix: (ix[0], i, 0)),
                      pl.BlockSpec((k, tr, c), lambda i, ix: (0, i, 0)), blk, blk],
            out_specs=[blk] * 4,
        ),
        out_shape=[jax.ShapeDtypeStruct(w.shape, f32)] * 4,
        compiler_params=_params(("arbitrary",)),
        name=name,
    )(own_idx, w, own, slots, m, v)


def _adamw(w, slots, m, v, name):
    r, c = w.shape[-2:]
    tr = _row_tile(r, 128)

    def body(w_ref, s_ref, m_ref, v_ref, g_ref, d_ref, nm_ref, nv_ref):
        g = s_ref[0].astype(f32)
        for k in range(1, NDEV):
            g = g + s_ref[k].astype(f32)
        _adam_update(g, w_ref, m_ref, v_ref, g_ref, d_ref, nm_ref, nv_ref)

    if w.ndim == 3:
        blk = pl.BlockSpec((None, tr, c), lambda i: (0, i, 0))
    else:
        blk = pl.BlockSpec((tr, c), lambda i: (i, 0))
    return pl.pallas_call(
        body,
        grid=(r // tr,),
        in_specs=[blk, pl.BlockSpec((NDEV, tr, c), lambda i: (0, i, 0)), blk, blk],
        out_specs=[blk] * 4,
        out_shape=[jax.ShapeDtypeStruct(w.shape, f32)] * 4,
        compiler_params=_params(("arbitrary",)),
        name=name,
    )(w, slots, m, v)


class _Weights:
    def __init__(self, w_t, w_a, w_b, b_merge, w_o):
        self._w_t, self._rest = w_t, (w_a, w_b, b_merge, w_o)

    def first_half(self, after):
        del after
        return self._w_t, jnp.zeros((1,), jnp.int32)

    def second_half(self, after):
        del after
        return self._w_t, jnp.ones((1,), jnp.int32)

    def rest(self):
        return self._rest


def _local_step(x, tgt, norm_gain, weights, qn_a, kn_a, qn_b, kn_b, sink_a, rel_bias, on_weight_grads=None,
                core=None):
    two = lambda t: jnp.concatenate([t, t], axis=-1).reshape(1, LANES)
    ones = jnp.ones((1, LANES), f32)
    gains = jnp.stack([
        jnp.stack([two(qn_a), two(kn_a), ones]),
        jnp.stack([two(qn_b), two(kn_b), ones]),
        jnp.stack([two(qn_b), two(kn_b), ones]),
        jnp.stack([two(qn_b), two(kn_b), ones]),
    ])
    buckets = [jnp.asarray(_bucket_np(blk, d)) for blk, d, _ in GROUPS]
    bias = [_bias_expand(rel_bias, buckets[k], GROUPS[k][2], "bias_expand_%d" % k) for k in range(4)]

    hb, hbt, rstd = _rms(x, norm_gain)
    w_t, half = weights.first_half([hb] + bias)
    proj = _inproj_half(hb, w_t, half, None, "inproj_1")
    w_t, half = weights.second_half([proj])
    proj = _inproj_half(hb, w_t, half, proj, "inproj_2")
    w_a, w_b, b_merge, w_o = weights.rest()
    gl = _prep(proj, gains)
    o_a, l_a = _attn_fwd(gl, bias[0], sink_a.reshape(8), 0, 128, 1, "attn_fwd_a")
    fwd_b = [_attn_fwd(gl, bias[k], None, k, GROUPS[k][0], GROUPS[k][1], "attn_fwd_b%d" % k) for k in (1, 2, 3)]
    sink_b = jnp.repeat(sink_a.reshape(8), HD).reshape(1, 512)

    (dy, dyb, dproj, do_a, dd_a, do_b0, do_b1, do_b2, dd_b0, dd_b1, dd_b2, ya, yb, mg, dbr_a, dbr_b, loss, dbm,
     dsk) = _tail(x, tgt, o_a, l_a, [f[0] for f in fwd_b], [f[1] for f in fwd_b], proj, b_merge, w_a, w_b, w_o, sink_b)

    dw_o = _matmul_tokens(mg, dyb, "dw_out")
    dw_a = _matmul_tokens(ya, dbr_a, "dw_branch_a")
    dw_b = _matmul_tokens(yb, dbr_b, "dw_branch_b")
    if on_weight_grads is not None:
        early = on_weight_grads(dict(w_branch_a=dw_a, w_branch_b=dw_b, b_merge=dbm, w_out=dw_o))
        buckets = [buckets[0] + early.astype(jnp.int32)] + buckets[1:]

    dqkv_a, dbk_a = _attn_bwd(gl, bias[0], buckets[0], do_a, l_a, dd_a, 0, 128, 1, "attn_bwd_a")
    dproj, dg_a = _post_a(dqkv_a, proj, gains, dproj)
    dbk_b, dg_b = [], []
    for k, do_k, dd_k in ((1, do_b0, dd_b0), (2, do_b1, dd_b1), (3, do_b2, dd_b2)):
        dqkv, dbk = _attn_bwd(gl, bias[k], buckets[k], do_k, fwd_b[k - 1][1], dd_k, k, GROUPS[k][0], GROUPS[k][1],
                              "attn_bwd_b%d" % k)
        dproj, dg = _post_b(k, dqkv, proj, gains, dproj)
        dbk_b.append(dbk)
        dg_b.append(dg)
    dg_b = jnp.stack(dg_b)

    core = jnp.zeros((1,), jnp.int32) if core is None else core
    dw_other = _dw_in(hbt, dproj, 1 - core, "dw_in_other")
    sent = jnp.zeros((), f32) if on_weight_grads is None else on_weight_grads(dict(w_in_other=dw_other))
    dw_in = _dw_in(hbt, dproj, core + sent.astype(jnp.int32), "dw_in_own")
    token = jnp.zeros((), f32) if on_weight_grads is None else on_weight_grads(dict(w_in=dw_in))
    grad_x, d_norm_gain = _dh_norm_bwd(dproj, w_t, x, rstd, norm_gain + token, dy)

    fold = lambda t: t[..., :HD] + t[..., HD:]
    d_qn_a = fold(dg_a[0, 0])
    d_kn_a = fold(dg_a[1, 0])
    d_qn_b = fold(dg_b[:, 0, 0].sum(axis=0))
    d_kn_b = fold(dg_b[:, 1, 0].sum(axis=0))
    d_sink = dsk.reshape(8, HD)[:, 0]
    red = jnp.stack([dbk_a] + dbk_b)
    d_rel = red[:, :, 0, :32].reshape(32, 32).T
    return dict(loss=loss, grad_x=grad_x, norm_gain=d_norm_gain, w_in=dw_in, w_in_other=dw_other, q_norm_a=d_qn_a,
                k_norm_a=d_kn_a,
                q_norm_b=d_qn_b, k_norm_b=d_kn_b, sink_a=d_sink, rel_bias=d_rel, w_branch_a=dw_a, w_branch_b=dw_b,
                b_merge=dbm, w_out=dw_o)


SMALL = (("norm_gain", D), ("q_norm_a", HD), ("k_norm_a", HD), ("q_norm_b", HD), ("k_norm_b", HD), ("sink_a", 8),
         ("rel_bias", 1024))
SMALL_PAD = 2432


SMALL_USED = sum(sz for _, sz in SMALL)


def _pack_small(parts, loss=None):
    tail = jnp.zeros((SMALL_PAD - SMALL_USED,), f32)
    if loss is not None:
        tail = tail.at[0].set(loss.reshape(()))
    return jnp.concatenate([parts[n].reshape(-1) for n, _ in SMALL] + [tail]).reshape(1, SMALL_PAD)


def _unpack_small(flat, shapes):
    out, off = {}, 0
    for n, sz in SMALL:
        out[n] = flat[0, off:off + sz].reshape(shapes[n])
        off += sz
    return out


def kernel(x, norm_gain, w_in, q_norm_a, k_norm_a, q_norm_b, k_norm_b, sink_a, rel_bias, w_branch_a, w_branch_b, b_merge, w_out, loss_target, m_norm_gain, m_w_in, m_q_norm_a, m_k_norm_a, m_q_norm_b, m_k_norm_b, m_sink_a, m_rel_bias, m_w_branch_a, m_w_branch_b, m_b_merge, m_w_out, v_norm_gain, v_w_in, v_q_norm_a, v_k_norm_a, v_q_norm_b, v_k_norm_b, v_sink_a, v_rel_bias, v_w_branch_a, v_w_branch_b, v_b_merge, v_w_out):
    csh = D // NDEV
    w_in_t, m_w_in_t, v_w_in_t = (jnp.swapaxes(t, 1, 2) for t in (w_in, m_w_in, v_w_in))
    weights = _GatheredWeights([w_in_t[0].astype(bf16), w_branch_a[0].astype(bf16), w_branch_b[0].astype(bf16),
                                w_out[0].astype(bf16), b_merge[0]])

    pending = {}
    core = lax.axis_index("c").astype(jnp.int32).reshape(1)
    chip = (2 * lax.axis_index("x") + lax.axis_index("y")).astype(jnp.int32).reshape(1)
    me = (2 * chip + core).astype(jnp.int32)

    def start_exchange(gw):
        if "w_in_other" in gw:
            sems, lands, sent = _sibling_send_start(gw["w_in_other"])
            pending["sibling"] = (sems, lands)
            return sent
        if "w_in" in gw:
            from_sibling = _sibling_send_wait(*pending["sibling"], after=[gw["w_in"]])
            chip_sums = _pair_sum(gw["w_in"], from_sibling, "grad_pair_sum")
            pending["w_in"] = _scatter_start([chip_sums], [], "scatter_w_in_start")
            return pending["w_in"][4][0, 0]
        blocks = [gw["w_branch_a"].reshape(512, NDEV, csh).transpose(1, 0, 2).astype(bf16),
                  gw["w_branch_b"].reshape(512, NDEV, csh).transpose(1, 0, 2).astype(bf16),
                  gw["w_out"].reshape(NDEV, csh, D).astype(bf16),
                  gw["b_merge"].reshape(2, NDEV, csh).transpose(1, 0, 2)]
        pending["rest"] = _scatter_start([], blocks, "scatter_rest_start")
        return pending["rest"][4][0, 0]

    loc = _local_step(x[0], loss_target[0], norm_gain, weights, q_norm_a, k_norm_a, q_norm_b, k_norm_b, sink_a,
                      rel_bias, on_weight_grads=start_exchange, core=core)

    small_shapes = dict(norm_gain=(1, D), q_norm_a=(1, HD), k_norm_a=(1, HD), q_norm_b=(1, HD), k_norm_b=(1, HD),
                        sink_a=(1, 8), rel_bias=(32, 32))
    (r_small,) = _exchange([], [_pack_small(loc, loc["loss"])], "gather_small_grads")
    send_sems, recv_sems, srcs, lands, _ = pending["rest"]
    (s_a, s_b, s_o, s_bm), (r_a, r_b, r_o, r_bm) = _scatter_wait(
        send_sems, recv_sems, srcs, lands, r_small, "scatter_rest_wait")
    send_sems, recv_sems, srcs, lands, _ = pending["w_in"]
    (s_in,), (r_in,) = _scatter_wait(send_sems, recv_sems, srcs, lands, r_small, "scatter_w_in_wait")

    given = dict(norm_gain=norm_gain, q_norm_a=q_norm_a, k_norm_a=k_norm_a, q_norm_b=q_norm_b, k_norm_b=k_norm_b,
                 sink_a=sink_a, rel_bias=rel_bias)
    m_small = dict(norm_gain=m_norm_gain, q_norm_a=m_q_norm_a, k_norm_a=m_k_norm_a, q_norm_b=m_q_norm_b,
                   k_norm_b=m_k_norm_b, sink_a=m_sink_a, rel_bias=m_rel_bias)
    v_small = dict(norm_gain=v_norm_gain, q_norm_a=v_q_norm_a, k_norm_a=v_k_norm_a, q_norm_b=v_q_norm_b,
                   k_norm_b=v_k_norm_b, sink_a=v_sink_a, rel_bias=v_rel_bias)
    res = {
        "small": _adamw(_pack_small(given), r_small, _pack_small(m_small), _pack_small(v_small), "adamw_small"),
        "w_in": [jnp.swapaxes(t, 1, 2) for t in
                 _adamw_own(w_in_t, s_in, chip, r_in, m_w_in_t, v_w_in_t, "adamw_w_in")],
        "w_branch_a": _adamw_own(w_branch_a, s_a, me, r_a, m_w_branch_a, v_w_branch_a, "adamw_w_branch_a"),
        "w_branch_b": _adamw_own(w_branch_b, s_b, me, r_b, m_w_branch_b, v_w_branch_b, "adamw_w_branch_b"),
        "b_merge": _adamw_own(b_merge, s_bm, me, r_bm, m_b_merge, v_b_merge, "adamw_b_merge"),
        "w_out": _adamw_own(w_out, s_o, me, r_o, m_w_out, v_w_out, "adamw_w_out"),
    }
    order = ["norm_gain", "w_in", "q_norm_a", "k_norm_a", "q_norm_b", "k_norm_b", "sink_a", "rel_bias", "w_branch_a",
             "w_branch_b", "b_merge", "w_out"]
    outs = []
    for k in range(4):
        small = _unpack_small(res["small"][k], small_shapes)
        for n in order:
            outs.append(small[n] if n in small else res[n][k])
    loss = res["small"][0][0, SMALL_USED]
    return (loss, loc["grad_x"][None], *outs)
```

```python
import math

import numpy as np
import jax
import jax.numpy as jnp
from jax import lax
from jax.experimental import pallas as pl
from jax.experimental.pallas import tpu as pltpu

f32 = jnp.float32
bf16 = jnp.bfloat16

S = 4096
D = 1024
NA = 5376
NT = 3072
NW = NA + NT
WSH = NW // 8
HD = 64
LANES = 128
EPS = 1e-6
NEG = -1e30
SCALE = HD ** -0.5
TQ = 128
PAD = 128
SP = S + 2 * PAD
NDEV = 8
GROUPS = ((128, 1, 0), (64, 1, 8), (64, 4, 16), (64, 16, 24))
CHUNK = 256
PCHUNK = 128
RC = 64

ADAM_LR, ADAM_B1, ADAM_B2, ADAM_EPS, ADAM_WD, ADAM_STEP = 0.001, 0.9, 0.999, 1e-08, 0.01, 10

MIB = 1024 * 1024
NT_DIMS = (((1,), (1,)), ((), ()))
TN_DIMS = (((0,), (0,)), ((), ()))


def _params(sem=None, vmem_mib=48):
    return pltpu.CompilerParams(dimension_semantics=sem, vmem_limit_bytes=vmem_mib * MIB)


def _lo():
    return lax.broadcasted_iota(jnp.int32, (1, LANES), 1) < HD


def _head_ones():
    r = lax.broadcasted_iota(jnp.int32, (LANES, LANES), 0) // HD
    c = lax.broadcasted_iota(jnp.int32, (LANES, LANES), 1) // HD
    return jnp.where(r == c, 1.0, 0.0).astype(bf16)


def _half_sums(x, ones):
    hi = x.astype(bf16)
    mid = (x - hi.astype(f32)).astype(bf16)
    return (jnp.dot(hi, ones, preferred_element_type=f32) + jnp.dot(mid, ones, preferred_element_type=f32))


def _seg_sum(x, ones):
    outs = [_half_sums(x[:, b * LANES:(b + 1) * LANES], ones) for b in range(x.shape[1] // LANES)]
    return outs[0] if len(outs) == 1 else jnp.concatenate(outs, axis=1)


def _bucket_np(blk, stride):
    w = TQ + 2 * blk
    rel = np.arange(w)[None, :] - blk - np.arange(TQ)[:, None]
    band = np.abs(rel) <= blk
    r = rel * stride
    n = np.abs(r)
    nf = np.maximum(n, 8).astype(np.float32)
    large = 8 + (np.log(nf / np.float32(8)) / np.float32(math.log(128.0)) * np.float32(8)).astype(np.int32)
    large = np.minimum(large, 15)
    b = (r > 0).astype(np.int32) * 16 + np.where(n < 8, n, large)
    return np.where(band, b, -1).astype(np.int32)


def _rms(x, gain):
    ts = 512

    def body(x_ref, g_ref, h_ref, ht_ref, r_ref):
        xv = x_ref[...]
        r = lax.rsqrt(jnp.mean(xv * xv, axis=-1, keepdims=True) + EPS)
        h = (xv * r) * g_ref[...]
        h_ref[...] = h.astype(bf16)
        ht_ref[...] = h.T.astype(bf16)
        r_ref[...] = r

    return pl.pallas_call(
        body,
        grid=(S // ts,),
        in_specs=[pl.BlockSpec((ts, D), lambda i: (i, 0)), pl.BlockSpec((1, D), lambda i: (0, 0))],
        out_specs=[pl.BlockSpec((ts, D), lambda i: (i, 0)), pl.BlockSpec((D, ts), lambda i: (0, i)),
                   pl.BlockSpec((ts, 1), lambda i: (i, 0))],
        out_shape=[jax.ShapeDtypeStruct((S, D), bf16), jax.ShapeDtypeStruct((D, S), bf16),
                   jax.ShapeDtypeStruct((S, 1), f32)],
        compiler_params=_params(("arbitrary",)),
        name="rms",
    )(x, gain)


def _inproj_half(hb, w_t, half, proj, name):
    ts = 512
    tn = NW // 2
    per = NW // 2 // tn

    def body(h_idx, h_ref, w_ref, *rest):
        del h_idx
        rest[-1][...] = lax.dot_general(h_ref[...], w_ref[...], NT_DIMS, preferred_element_type=f32)

    in_specs = [pl.BlockSpec((ts, D), lambda i, n, hf: (i, 0)),
                pl.BlockSpec((tn, D), lambda i, n, hf: (hf[0] * per + n, 0))]
    args = [half, hb, w_t]
    aliases = {}
    if proj is not None:
        in_specs.append(pl.BlockSpec(memory_space=pl.ANY))
        args.append(proj)
        aliases = {3: 0}
    return pl.pallas_call(
        body,
        grid_spec=pltpu.PrefetchScalarGridSpec(
            num_scalar_prefetch=1,
            grid=(S // ts, per),
            in_specs=in_specs,
            out_specs=pl.BlockSpec((ts, tn), lambda i, n, hf: (i, hf[0] * per + n)),
        ),
        out_shape=jax.ShapeDtypeStruct((S, NW), f32),
        input_output_aliases=aliases,
        compiler_params=_params(("arbitrary", "arbitrary")),
        name=name,
    )(*args)


def _bias_expand(table, bucket, c0, name):
    tq, w = bucket.shape
    blk = (w - tq) // 2

    def body(tab_ref, bk_ref, o_ref):
        h = pl.program_id(0)
        bk = bk_ref[...]

        def step(b, acc):
            return jnp.where(bk == b, tab_ref[b, c0 + h], acc)

        inner = lax.fori_loop(0, 32, step, jnp.full((tq, w), NEG, f32))
        col = lax.broadcasted_iota(jnp.int32, (1, w), 1)
        o_ref[0] = jnp.where(col < blk, NEG, inner)
        o_ref[1] = inner
        o_ref[2] = jnp.where(col >= tq + blk, NEG, inner)

    return pl.pallas_call(
        body,
        grid=(8,),
        in_specs=[pl.BlockSpec(memory_space=pltpu.SMEM), pl.BlockSpec((tq, w), lambda h: (0, 0))],
        out_specs=pl.BlockSpec((3, None, tq, w), lambda h: (0, h, 0, 0)),
        out_shape=jax.ShapeDtypeStruct((3, 8, tq, w), f32),
        compiler_params=_params(("arbitrary",)),
        name=name,
    )(table, bucket)


def _tile_kind(t, seq):
    m0 = jnp.bitwise_and(t * TQ, seq - 1)
    return jnp.where(m0 == 0, 0, jnp.where(m0 == seq - TQ, 2, 1))


def _col_block(g, j):
    kind = j // 4
    hp = j % 4
    a = jnp.where(kind == 0, hp, 3 + kind)
    b = 6 + 12 * kind + 4 * (g - 1) + hp
    return jnp.where(g == 0, a, b)


def _prep(proj_a, gains):
    def body(p_ref, g_ref, o_ref):
        g = pl.program_id(0)
        j = pl.program_id(1)
        kind = j // 4
        lo = _lo()
        ones = _head_ones()
        half = jnp.where(lo, 0, 1)
        take = (kind == 0) | (half == (j % 4) // 2)
        gain = g_ref[...]
        o_ref[0:PAD, :] = jnp.zeros((PAD, LANES), bf16)
        o_ref[PAD + S:SP, :] = jnp.zeros((PAD, LANES), bf16)

        def norm_store(xv, dst, dup):
            if dup:
                xv = jnp.where(take, xv, pltpu.roll(xv, HD, 1))
            r = lax.rsqrt(_half_sums(xv * xv, ones) * (1.0 / HD) + EPS)
            r = jnp.where(kind == 2, 1.0, r)
            yv = (xv * r) * gain
            yv = jnp.where(kind == 0, yv * SCALE, yv)
            o_ref[PAD + dst:PAD + dst + CHUNK, :] = yv.astype(bf16)

        for gi, (_, d, _) in enumerate(GROUPS):
            @pl.when(g == gi)
            def _():
                seq = S // d
                for c in range(d):
                    for i in range(seq // CHUNK):
                        if d == 1:
                            xv = p_ref[i * CHUNK:(i + 1) * CHUNK, :]
                        else:
                            xv = p_ref[pl.ds(c + i * CHUNK * d, CHUNK, stride=d), :]
                        norm_store(xv, c * seq + i * CHUNK, gi == 0)

    return pl.pallas_call(
        body,
        grid=(4, 12),
        in_specs=[
            pl.BlockSpec((S, LANES), lambda g, j: (0, _col_block(g, j))),
            pl.BlockSpec((None, None, 1, LANES), lambda g, j: (g, j // 4, 0, 0)),
        ],
        out_specs=pl.BlockSpec((None, None, SP, LANES), lambda g, j: (g, j, 0, 0)),
        out_shape=jax.ShapeDtypeStruct((4, 12, SP, LANES), bf16),
        compiler_params=_params(("arbitrary", "arbitrary")),
        name="prep",
    )(proj_a, gains)


def _token_rows(t, r0, n, d):
    if d == 1:
        return pl.ds(pl.multiple_of(t * TQ, TQ) + r0, n)
    per = S // d // TQ
    return pl.ds(((t % per) * TQ + r0) * d + t // per, n, stride=d)


def _stack_heads(t, lo):
    z = jnp.zeros_like(t)
    return jnp.concatenate([jnp.where(lo, t, z), jnp.where(lo, z, t)], axis=0)


def _unstack_heads(t2, lo):
    return jnp.where(lo, t2[:TQ], t2[TQ:])


def _attn_fwd(gl, bias, sink, g, blk, d, name):
    w = TQ + 2 * blk
    seq = S // d
    use_sink = sink is not None

    def body(*refs):
        if use_sink:
            sink_ref, q_ref, k_ref, v_ref, b_ref, o_ref, l_ref, s0, s1, p0, p1, lse_scr = refs
        else:
            q_ref, k_ref, v_ref, b_ref, o_ref, l_ref, s0, s1, p0, p1, lse_scr = refs
        hp = pl.program_id(0)
        lo = _lo()
        s_bufs, p_bufs = (s0, s1), (p0, p1)

        def scores(p, slot):
            for u in range(2):
                f0 = pl.multiple_of((2 * p + u) * TQ, TQ)
                q2 = _stack_heads(q_ref[pl.ds(PAD + f0, TQ), :], lo)
                kw = k_ref[pl.ds(PAD - blk + f0, w), :]
                s_bufs[slot][u] = lax.dot_general(q2, kw, NT_DIMS, preferred_element_type=f32)

        def softmax(p, slot):
            for u in range(2):
                t = 2 * p + u
                kind = _tile_kind(t, seq)
                for h in range(2):
                    for r in range(TQ // RC):
                        rows = slice(h * TQ + r * RC, h * TQ + (r + 1) * RC)
                        logit = s_bufs[slot][u, rows, :] + b_ref[kind, h, r * RC:(r + 1) * RC, :]
                        m = jnp.max(logit, axis=1, keepdims=True)
                        e = jnp.exp(logit - m)
                        lse = m + jnp.log(jnp.sum(e, axis=1, keepdims=True))
                        if use_sink:
                            sk = sink_ref[2 * hp + h]
                            mx = jnp.maximum(lse, sk)
                            lse = mx + jnp.log(jnp.exp(lse - mx) + jnp.exp(sk - mx))
                        p_bufs[slot][u, rows, :] = (e * jnp.exp(m - lse)).astype(bf16)
                        lse_scr[u, rows, :] = jnp.broadcast_to(lse, (RC, LANES))
                l_ref[_token_rows(t, 0, TQ, d), :] = jnp.where(lo, lse_scr[u, 0:TQ, :], lse_scr[u, TQ:2 * TQ, :])

        def values(p, slot):
            for u in range(2):
                t = 2 * p + u
                vw = v_ref[pl.ds(PAD - blk + pl.multiple_of(t * TQ, TQ), w), :]
                o2 = jnp.dot(p_bufs[slot][u], vw, preferred_element_type=f32)
                o_ref[_token_rows(t, 0, TQ, d), :] = _unstack_heads(o2, lo)

        npair = S // TQ // 2
        scores(0, 0)
        scores(1, 1)
        softmax(0, 0)

        def steady(k, carry):
            p = 2 * k + 2
            scores(p, 0)
            softmax(p - 1, 1)
            values(p - 2, 0)
            scores(p + 1, 1)
            softmax(p, 0)
            values(p - 1, 1)
            return carry

        lax.fori_loop(0, (npair - 2) // 2, steady, 0)
        softmax(npair - 1, 1)
        values(npair - 2, 0)
        values(npair - 1, 1)

    in_specs = [
        pl.BlockSpec((None, None, SP, LANES), lambda hp: (g, hp, 0, 0)),
        pl.BlockSpec((None, None, SP, LANES), lambda hp: (g, 4 + hp, 0, 0)),
        pl.BlockSpec((None, None, SP, LANES), lambda hp: (g, 8 + hp, 0, 0)),
        pl.BlockSpec((3, 2, TQ, w), lambda hp: (0, hp, 0, 0)),
    ]
    args = [gl, gl, gl, bias]
    if use_sink:
        in_specs = [pl.BlockSpec(memory_space=pltpu.SMEM)] + in_specs
        args = [sink] + args
    out = pl.BlockSpec((S, LANES), lambda hp: (0, hp))
    return pl.pallas_call(
        body,
        grid=(4,),
        in_specs=in_specs,
        out_specs=[out, out],
        out_shape=[jax.ShapeDtypeStruct((S, 4 * LANES), f32)] * 2,
        scratch_shapes=[pltpu.VMEM((2, 2 * TQ, w), f32), pltpu.VMEM((2, 2 * TQ, w), f32),
                        pltpu.VMEM((2, 2 * TQ, w), bf16), pltpu.VMEM((2, 2 * TQ, w), bf16),
                        pltpu.VMEM((2, 2 * TQ, LANES), f32)],
        compiler_params=_params(("arbitrary",)),
        name=name,
    )(*args)


def _attn_bwd(gl, bias, bucket, do, lse, dd, g, blk, d, name):
    w = TQ + 2 * blk
    seq = S // d

    def body(q_ref, k_ref, v_ref, b_ref, bk_ref, do_ref, l_ref, d_ref, dqkv_ref, dbk_ref,
             db_acc, s0, s1, dp0, dp1, pb0, pb1, ds0, ds1, dk_acc, dv_acc):
        lo = _lo()
        hi = jnp.logical_not(lo)
        dk_acc[...] = jnp.zeros((SP, LANES), f32)
        dv_acc[...] = jnp.zeros((SP, LANES), f32)
        db_acc[...] = jnp.zeros((2 * TQ, w), f32)
        s_bufs, dp_bufs, pb_bufs, ds_bufs = (s0, s1), (dp0, dp1), (pb0, pb1), (ds0, ds1)

        def stacked(t):
            f0 = pl.multiple_of(t * TQ, TQ)
            q2 = _stack_heads(q_ref[pl.ds(PAD + f0, TQ), :], lo)
            do2 = _stack_heads(do_ref[_token_rows(t, 0, TQ, d), :].astype(bf16), lo)
            return f0, q2, do2

        def scores(p, slot):
            for u in range(2):
                f0, q2, do2 = stacked(2 * p + u)
                win = pl.ds(PAD - blk + f0, w)
                s_bufs[slot][u] = lax.dot_general(q2, k_ref[win, :], NT_DIMS, preferred_element_type=f32)
                dp_bufs[slot][u] = lax.dot_general(do2, v_ref[win, :], NT_DIMS, preferred_element_type=f32)

        def grads(p, slot):
            for u in range(2):
                t = 2 * p + u
                kind = _tile_kind(t, seq)
                for h in range(2):
                    msk = lo if h == 0 else hi
                    for r in range(TQ // RC):
                        rows = slice(h * TQ + r * RC, h * TQ + (r + 1) * RC)
                        src = _token_rows(t, r * RC, RC, d)
                        lh = jnp.max(jnp.where(msk, l_ref[src, :], -jnp.inf), axis=1, keepdims=True)
                        dh = jnp.max(jnp.where(msk, d_ref[src, :], -jnp.inf), axis=1, keepdims=True)
                        logit = s_bufs[slot][u, rows, :] + b_ref[kind, h, r * RC:(r + 1) * RC, :]
                        pr = jnp.exp(logit - lh)
                        ds = pr * (dp_bufs[slot][u, rows, :] - dh)
                        db_acc[rows, :] += ds
                        pb_bufs[slot][u, rows, :] = pr.astype(bf16)
                        ds_bufs[slot][u, rows, :] = ds.astype(bf16)

        def accumulate(p, slot):
            for u in range(2):
                f0, q2, do2 = stacked(2 * p + u)
                win = pl.ds(PAD - blk + f0, w)
                dsb = ds_bufs[slot][u]
                dq2 = jnp.dot(dsb, k_ref[win, :], preferred_element_type=f32)
                dqkv_ref[0, pl.ds(PAD + f0, TQ), :] = _unstack_heads(dq2, lo).astype(bf16)
                dk_acc[win, :] += lax.dot_general(dsb, q2, TN_DIMS, preferred_element_type=f32)
                dv_acc[win, :] += lax.dot_general(pb_bufs[slot][u], do2, TN_DIMS, preferred_element_type=f32)

        npair = S // TQ // 2
        scores(0, 0)
        scores(1, 1)
        grads(0, 0)

        def steady(k, carry):
            p = 2 * k + 2
            scores(p, 0)
            grads(p - 1, 1)
            accumulate(p - 2, 0)
            scores(p + 1, 1)
            grads(p, 0)
            accumulate(p - 1, 1)
            return carry

        lax.fori_loop(0, (npair - 2) // 2, steady, 0)
        grads(npair - 1, 1)
        accumulate(npair - 2, 0)
        accumulate(npair - 1, 1)
        for i in range(SP // CHUNK):
            rows = slice(i * CHUNK, (i + 1) * CHUNK)
            dqkv_ref[1, rows, :] = dk_acc[rows, :].astype(bf16)
            dqkv_ref[2, rows, :] = dv_acc[rows, :].astype(bf16)

        bk = bk_ref[...]
        lane = lax.broadcasted_iota(jnp.int32, (8, LANES), 1)
        for h in range(2):
            db = db_acc[h * TQ:(h + 1) * TQ, :]
            acc = jnp.zeros((8, LANES), f32)
            for b in range(32):
                part = jnp.where(bk == b, db, 0.0).reshape(TQ // 8, 8, w).sum(axis=0)
                tot = jnp.sum(jnp.sum(part, axis=1, keepdims=True), axis=0, keepdims=True)
                acc = jnp.where(lane == b, tot, acc)
            dbk_ref[h] = acc

    def gcol(off):
        return pl.BlockSpec((None, None, SP, LANES), lambda hp: (g, off + hp, 0, 0))

    row = pl.BlockSpec((S, LANES), lambda hp: (0, hp))
    return pl.pallas_call(
        body,
        grid=(4,),
        in_specs=[gcol(0), gcol(4), gcol(8), pl.BlockSpec((3, 2, TQ, w), lambda hp: (0, hp, 0, 0)),
                  pl.BlockSpec((TQ, w), lambda hp: (0, 0)), row, row, row],
        out_specs=[pl.BlockSpec((3, None, SP, LANES), lambda hp: (0, hp, 0, 0)),
                   pl.BlockSpec((2, 8, LANES), lambda hp: (hp, 0, 0))],
        out_shape=[
            jax.ShapeDtypeStruct((3, 4, SP, LANES), bf16),
            jax.ShapeDtypeStruct((8, 8, LANES), f32),
        ],
        scratch_shapes=([pltpu.VMEM((2 * TQ, w), f32)] + [pltpu.VMEM((2, 2 * TQ, w), f32)] * 4
                        + [pltpu.VMEM((2, 2 * TQ, w), bf16)] * 4 + [pltpu.VMEM((SP, LANES), f32)] * 2),
        compiler_params=_params(("arbitrary",), vmem_mib=56),
        name=name,
    )(gl, gl, gl, bias, bucket, do, lse, dd)


def _sigmoid(z):
    return 1.0 / (1.0 + jnp.exp(-z))


def _tail(x, tgt, o_a, l_a, o_b, l_b, proj, bm, w_a, w_b, w_o, sink_b):
    ts = 256

    def body(x_ref, t_ref, oa_ref, la_ref, ob0_ref, ob1_ref, ob2_ref, lb0_ref, lb1_ref, lb2_ref,
             ga_ref, gb_ref, m0_ref, m1_ref, bm_ref, wa_ref, wb_ref, wo_ref, sk_ref,
             dy_ref, dyb_ref, dt_ref, doa_ref, dda_ref, dob0_ref, dob1_ref, dob2_ref, ddb0_ref, ddb1_ref, ddb2_ref,
             ya_ref, yb_ref, mg_ref, dbra_ref, dbrb_ref, loss_ref, dbm_ref, dsk_ref):
        i = pl.program_id(0)

        @pl.when(i == 0)
        def _():
            loss_ref[...] = jnp.zeros_like(loss_ref)
            dbm_ref[...] = jnp.zeros_like(dbm_ref)
            dsk_ref[...] = jnp.zeros_like(dsk_ref)

        ga = ga_ref[...]
        sa = _sigmoid(ga)
        silu_a = ga * sa
        oa = oa_ref[...]
        ya = oa * silu_a
        gb = gb_ref[...]
        sb = _sigmoid(gb)
        silu_b = gb * sb
        ob = [ob0_ref[...], ob1_ref[...], ob2_ref[...]]
        lb = [lb0_ref[...], lb1_ref[...], lb2_ref[...]]
        mx = jnp.maximum(jnp.maximum(lb[0], lb[1]), lb[2])
        ex = [jnp.exp(v - mx) for v in lb]
        den = ex[0] + ex[1] + ex[2]
        alpha = [e / den for e in ex]
        ybc = alpha[0] * ob[0] + alpha[1] * ob[1] + alpha[2] * ob[2]
        yb = ybc * silu_b
        yab = ya.astype(bf16)
        ybb = yb.astype(bf16)
        br_a = jnp.dot(yab, wa_ref[...], preferred_element_type=f32)
        br_b = jnp.dot(ybb, wb_ref[...], preferred_element_type=f32)
        g0 = _sigmoid(m0_ref[...] + bm_ref[0:1, :])
        g1 = _sigmoid(m1_ref[...] + bm_ref[1:2, :])
        merged = g0 * br_a + g1 * br_b
        mgb = merged.astype(bf16)
        y = x_ref[...] + jnp.dot(mgb, wo_ref[...], preferred_element_type=f32)
        err = y - t_ref[...]
        part = jnp.sum(jnp.sum(err * err, axis=1, keepdims=True), axis=0, keepdims=True)
        loss_ref[...] += part * (0.5 / D)
        dy = err * (1.0 / D)
        dyb = dy.astype(bf16)
        dmerged = lax.dot_general(dyb, wo_ref[...], NT_DIMS, preferred_element_type=f32)
        dbr_a = (dmerged * g0).astype(bf16)
        dbr_b = (dmerged * g1).astype(bf16)
        dm0 = dmerged * br_a * (g0 * (1.0 - g0))
        dm1 = dmerged * br_b * (g1 * (1.0 - g1))
        dbm_ref[0:1, :] += jnp.sum(dm0, axis=0, keepdims=True)
        dbm_ref[1:2, :] += jnp.sum(dm1, axis=0, keepdims=True)
        dya = lax.dot_general(dbr_a, wa_ref[...], NT_DIMS, preferred_element_type=f32)
        dyb2 = lax.dot_general(dbr_b, wb_ref[...], NT_DIMS, preferred_element_type=f32)
        do_a = dya * silu_a
        dga = dya * oa * (sa * (1.0 + ga * (1.0 - sa)))
        ones = _head_ones()
        delta_a = _seg_sum(do_a * oa, ones)
        dsk_ref[...] -= jnp.sum(delta_a * jnp.exp(sk_ref[...] - la_ref[...]), axis=0, keepdims=True)
        dybc = dyb2 * silu_b
        dgb = dyb2 * ybc * (sb * (1.0 + gb * (1.0 - sb)))
        dbar = _seg_sum(dybc * ybc, ones)
        dy_ref[...] = dy
        dyb_ref[...] = dyb
        dt_ref[:, 0:512] = dga.astype(bf16)
        dt_ref[:, 512:1024] = dgb.astype(bf16)
        dt_ref[:, 1024:2048] = dm0.astype(bf16)
        dt_ref[:, 2048:3072] = dm1.astype(bf16)
        doa_ref[...] = do_a.astype(bf16)
        dda_ref[...] = delta_a
        for k, (dob_ref, ddb_ref) in enumerate(((dob0_ref, ddb0_ref), (dob1_ref, ddb1_ref), (dob2_ref, ddb2_ref))):
            dob_ref[...] = alpha[k] * dybc
            ddb_ref[...] = alpha[k] * dbar
        ya_ref[...] = ya.T.astype(bf16)
        yb_ref[...] = yb.T.astype(bf16)
        mg_ref[...] = merged.T.astype(bf16)
        dbra_ref[...] = dbr_a
        dbrb_ref[...] = dbr_b

    def rows(n, blk=0):
        return pl.BlockSpec((ts, n), lambda i: (i, blk))

    def whole(r, c):
        return pl.BlockSpec((r, c), lambda i: (0, 0))

    def cols(n):
        return pl.BlockSpec((n, ts), lambda i: (0, i))

    def gate_cols(n, col):
        return pl.BlockSpec((pl.Element(ts), pl.Element(n)), lambda i: (i * ts, NA + col))

    outs = [
        ((S, D), f32, rows(D)), ((S, D), bf16, rows(D)), ((S, NW), bf16, gate_cols(NT, 0)),
        ((S, 512), bf16, rows(512)), ((S, 512), f32, rows(512)),
        ((S, 512), f32, rows(512)), ((S, 512), f32, rows(512)), ((S, 512), f32, rows(512)),
        ((S, 512), f32, rows(512)), ((S, 512), f32, rows(512)), ((S, 512), f32, rows(512)),
        ((512, S), bf16, cols(512)), ((512, S), bf16, cols(512)), ((D, S), bf16, cols(D)),
        ((S, D), bf16, rows(D)), ((S, D), bf16, rows(D)),
        ((1, 1), f32, whole(1, 1)), ((2, D), f32, whole(2, D)), ((1, 512), f32, whole(1, 512)),
    ]
    return pl.pallas_call(
        body,
        grid=(S // ts,),
        in_specs=[
            rows(D), rows(D), rows(512), rows(512), rows(512), rows(512), rows(512), rows(512), rows(512), rows(512),
            gate_cols(512, 0), gate_cols(512, 512), gate_cols(D, 1024), gate_cols(D, 2048), whole(2, D),
            whole(512, D), whole(512, D), whole(D, D), whole(1, 512),
        ],
        out_specs=[o[2] for o in outs],
        out_shape=[jax.ShapeDtypeStruct(o[0], o[1]) for o in outs],
        compiler_params=_params(("arbitrary",), vmem_mib=60),
        name="tail",
    )(x, tgt, o_a, l_a, *o_b, *l_b, proj, proj, proj, proj, bm, w_a, w_b, w_o, sink_b)


def _norm_bwd(xv, dyv, gain, kind, ones):
    r = lax.rsqrt(_half_sums(xv * xv, ones) * (1.0 / HD) + EPS)
    yv = xv * r
    up = jnp.where(kind == 0, dyv * SCALE, dyv)
    u = up * gain
    dxv = r * (u - yv * (_half_sums(u * yv, ones) * (1.0 / HD)))
    dxv = jnp.where(kind == 2, dyv, dxv)
    dg = jnp.where(kind == 2, 0.0, jnp.sum(up * yv, axis=0, keepdims=True))
    return dxv, dg


def _post_b(g, dqkv, proj_a, gains, dproj):
    d = GROUPS[g][1]
    seq = S // d

    def body(d_ref, p_ref, g_ref, alias_ref, o_ref, dg_ref, nat):
        del alias_ref
        j = pl.program_id(0)
        kind = j // 4
        gain = g_ref[...]
        ones = _head_ones()

        @pl.when(j % 4 == 0)
        def _():
            dg_ref[...] = jnp.zeros_like(dg_ref)

        for c in range(d):
            for i in range(seq // PCHUNK):
                src = c * seq + i * PCHUNK
                if d == 1:
                    idx = slice(src, src + PCHUNK)
                else:
                    idx = pl.ds(c + i * PCHUNK * d, PCHUNK, stride=d)
                dyv = d_ref[PAD + src:PAD + src + PCHUNK, :].astype(f32)
                dxv, dg = _norm_bwd(p_ref[idx, :], dyv, gain, kind, ones)
                nat[idx, :] = dxv
                dg_ref[...] += dg

        for i in range(S // CHUNK):
            o_ref[i * CHUNK:(i + 1) * CHUNK, :] = nat[i * CHUNK:(i + 1) * CHUNK, :].astype(bf16)

    return pl.pallas_call(
        body,
        grid=(12,),
        in_specs=[
            pl.BlockSpec((None, None, SP, LANES), lambda j: (j // 4, j % 4, 0, 0)),
            pl.BlockSpec((S, LANES), lambda j: (0, _col_block(g, jnp.minimum(j, 7)))),
            pl.BlockSpec((None, None, 1, LANES), lambda j: (g, j // 4, 0, 0)),
            pl.BlockSpec(memory_space=pl.ANY),
        ],
        out_specs=[
            pl.BlockSpec((S, LANES), lambda j: (0, _col_block(g, j))),
            pl.BlockSpec((None, 1, LANES), lambda j: (j // 4, 0, 0)),
        ],
        out_shape=[jax.ShapeDtypeStruct((S, NW), bf16), jax.ShapeDtypeStruct((3, 1, LANES), f32)],
        scratch_shapes=[pltpu.VMEM((S, LANES), f32)],
        input_output_aliases={3: 0},
        compiler_params=_params(("arbitrary",)),
        name="post_b%d" % g,
    )(dqkv, proj_a, gains, dproj)


def _post_a(dqkv, proj_a, gains, dproj):
    def body(q_ref, e_ref, p_ref, g_ref, alias_ref, o_ref, dg_ref):
        del alias_ref
        j = pl.program_id(0)
        kind = jnp.maximum(j - 3, 0)
        gain = g_ref[...]
        lo = _lo()
        ones = _head_ones()

        @pl.when((j == 0) | (j >= 4))
        def _():
            dg_ref[...] = jnp.zeros_like(dg_ref)

        for i in range(S // PCHUNK):
            r0 = i * PCHUNK
            rows = slice(PAD + r0, PAD + r0 + PCHUNK)
            t0 = e_ref[0, rows, :].astype(f32) + e_ref[1, rows, :].astype(f32)
            t1 = e_ref[2, rows, :].astype(f32) + e_ref[3, rows, :].astype(f32)
            folded = jnp.where(lo, t0 + pltpu.roll(t0, HD, 1), t1 + pltpu.roll(t1, HD, 1))
            dyv = jnp.where(kind == 0, q_ref[rows, :].astype(f32), folded)
            dxv, dg = _norm_bwd(p_ref[r0:r0 + PCHUNK, :], dyv, gain, kind, ones)
            o_ref[r0:r0 + PCHUNK, :] = dxv.astype(bf16)
            dg_ref[...] += dg

    return pl.pallas_call(
        body,
        grid=(6,),
        in_specs=[
            pl.BlockSpec((None, None, SP, LANES), lambda j: (0, jnp.minimum(j, 3), 0, 0)),
            pl.BlockSpec((None, 4, SP, LANES), lambda j: (jnp.clip(j - 3, 1, 2), 0, 0, 0)),
            pl.BlockSpec((S, LANES), lambda j: (0, jnp.minimum(j, 4))),
            pl.BlockSpec((None, None, 1, LANES), lambda j: (0, jnp.maximum(j - 3, 0), 0, 0)),
            pl.BlockSpec(memory_space=pl.ANY),
        ],
        out_specs=[
            pl.BlockSpec((S, LANES), lambda j: (0, j)),
            pl.BlockSpec((None, 1, LANES), lambda j: (jnp.maximum(j - 3, 0), 0, 0)),
        ],
        out_shape=[jax.ShapeDtypeStruct((S, NW), bf16), jax.ShapeDtypeStruct((3, 1, LANES), f32)],
        input_output_aliases={4: 0},
        compiler_params=_params(("arbitrary",)),
        name="post_a",
    )(dqkv, dqkv, proj_a, gains, dproj)


def _dh_norm_bwd(dproj, w, x, rstd, gain, dy):
    ts = 1024
    tk = NW // 6
    nk = NW // tk

    def body(d_ref, w_ref, x_ref, r_ref, g_ref, dy_ref, gx_ref, dgn_ref, acc):
        i = pl.program_id(0)
        k = pl.program_id(1)

        @pl.when((i == 0) & (k == 0))
        def _():
            dgn_ref[...] = jnp.zeros_like(dgn_ref)

        @pl.when(k == 0)
        def _():
            acc[...] = jnp.zeros_like(acc)

        acc[...] += jnp.dot(d_ref[...], w_ref[...], preferred_element_type=f32)

        @pl.when(k == nk - 1)
        def _():
            dh = acc[...]
            xh = x_ref[...] * r_ref[...]
            u = dh * g_ref[...]
            dx = r_ref[...] * (u - xh * jnp.mean(u * xh, axis=-1, keepdims=True))
            gx_ref[...] = dy_ref[...] + dx
            dgn_ref[...] += jnp.sum(dh * xh, axis=0, keepdims=True)

    return pl.pallas_call(
        body,
        grid=(S // ts, nk),
        in_specs=[
            pl.BlockSpec((ts, tk), lambda i, k: (i, k)),
            pl.BlockSpec((tk, D), lambda i, k: (k, 0)),
            pl.BlockSpec((ts, D), lambda i, k: (i, 0)),
            pl.BlockSpec((ts, 1), lambda i, k: (i, 0)),
            pl.BlockSpec((1, D), lambda i, k: (0, 0)),
            pl.BlockSpec((ts, D), lambda i, k: (i, 0)),
        ],
        out_specs=[pl.BlockSpec((ts, D), lambda i, k: (i, 0)), pl.BlockSpec((1, D), lambda i, k: (0, 0))],
        out_shape=[jax.ShapeDtypeStruct((S, D), f32), jax.ShapeDtypeStruct((1, D), f32)],
        scratch_shapes=[pltpu.VMEM((ts, D), f32)],
        compiler_params=_params(("arbitrary", "arbitrary"), vmem_mib=56),
        name="dh_norm_bwd",
    )(dproj, w, x, rstd, gain, dy)


def _dw_in(hbt, dproj, parity, name):
    tk = 1024
    win = WSH + 96

    def body(par_ref, a_ref, b_ref, o_ref, acc):
        p = 2 * pl.program_id(0) + par_ref[0]
        k = pl.program_id(1)

        @pl.when(k == 0)
        def _():
            acc[...] = jnp.zeros_like(acc)

        acc[...] += jnp.dot(a_ref[...], b_ref[...], preferred_element_type=f32)

        @pl.when(k == S // tk - 1)
        def _():
            acc_t = acc[...].T
            for pp in range(NDEV):
                off = (WSH * pp) % LANES

                @pl.when(p == pp)
                def _():
                    o_ref[...] = acc_t[off:off + WSH, :].astype(bf16)

    return pl.pallas_call(
        body,
        grid_spec=pltpu.PrefetchScalarGridSpec(
            num_scalar_prefetch=1,
            grid=(NDEV // 2, S // tk),
            in_specs=[
                pl.BlockSpec((D, tk), lambda q, k, par: (0, k)),
                pl.BlockSpec((pl.Element(tk), pl.Element(win)),
                             lambda q, k, par: (k * tk, (WSH * (2 * q + par[0])) // LANES * LANES)),
            ],
            out_specs=pl.BlockSpec((None, WSH, D), lambda q, k, par: (q, 0, 0)),
            scratch_shapes=[pltpu.VMEM((D, win), f32)],
        ),
        out_shape=jax.ShapeDtypeStruct((NDEV // 2, WSH, D), bf16),
        compiler_params=_params(("arbitrary", "arbitrary")),
        name=name,
    )(parity, hbt, dproj)


def _matmul_tokens(at, b, name):
    m, n = at.shape[0], b.shape[1]
    tn = 512
    tk = 1024

    def body(a_ref, b_ref, o_ref):
        @pl.when(pl.program_id(1) == 0)
        def _():
            o_ref[...] = jnp.zeros_like(o_ref)

        o_ref[...] += jnp.dot(a_ref[...], b_ref[...], preferred_element_type=f32)

    return pl.pallas_call(
        body,
        grid=(n // tn, S // tk),
        in_specs=[pl.BlockSpec((m, tk), lambda j, k: (0, k)), pl.BlockSpec((tk, tn), lambda j, k: (k, j))],
        out_specs=pl.BlockSpec((m, tn), lambda j, k: (0, j)),
        out_shape=jax.ShapeDtypeStruct((m, n), f32),
        compiler_params=_params(("arbitrary", "arbitrary")),
        name=name,
    )(at, b)


def _exchange(scatter, gather, name):
    arrs = list(scatter) + list(gather)
    n = len(arrs)
    ns = len(scatter)

    def body(*refs):
        ins, outs = refs[:n], refs[n:2 * n]
        send_sems, recv_sems, local_sems = refs[2 * n:]
        x, y, c = lax.axis_index("x"), lax.axis_index("y"), lax.axis_index("c")
        me = 4 * x + 2 * y + c
        local, remote = [], []
        for a in range(n):
            lc = pltpu.make_async_copy(ins[a].at[me] if a < ns else ins[a], outs[a].at[me], local_sems.at[a])
            lc.start()
            local.append(lc)
            for r in range(1, NDEV):
                px = 1 - x if r & 4 else x
                py = 1 - y if r & 2 else y
                pc = 1 - c if r & 1 else c
                cp = pltpu.make_async_remote_copy(
                    src_ref=ins[a].at[4 * px + 2 * py + pc] if a < ns else ins[a],
                    dst_ref=outs[a].at[me],
                    send_sem=send_sems.at[a, r - 1],
                    recv_sem=recv_sems.at[a, r - 1],
                    device_id=(px, py, pc),
                    device_id_type=pl.DeviceIdType.MESH,
                )
                cp.start()
                remote.append(cp)
        for cp in remote:
            cp.wait_recv()
        for cp in remote:
            cp.wait_send()
        for lc in local:
            lc.wait()

    out_shape = [jax.ShapeDtypeStruct(a.shape if i < ns else (NDEV,) + a.shape, a.dtype) for i, a in enumerate(arrs)]
    return pl.pallas_call(
        body,
        in_specs=[pl.BlockSpec(memory_space=pl.ANY)] * n,
        out_specs=[pl.BlockSpec(memory_space=pl.ANY)] * n,
        out_shape=out_shape,
        scratch_shapes=[
            pltpu.SemaphoreType.DMA((n, NDEV - 1)),
            pltpu.SemaphoreType.DMA((n, NDEV - 1)),
            pltpu.SemaphoreType.DMA((n,)),
        ],
        compiler_params=pltpu.CompilerParams(has_side_effects=True),
        name=name,
    )(*arrs)


_HBM = pl.BlockSpec(memory_space=pltpu.HBM)
_SEM = pl.BlockSpec(memory_space=pltpu.SEMAPHORE)
_EFFECT = pltpu.SideEffectType.DATAFLOW_SIDE_EFFECTING


def _comm_step(name, body_fn, lands, srcs=(), wait_sems=(), n_new=0, after=(), token=False):
    n, ns, nw, na = len(lands), len(srcs), len(wait_sems), len(after)

    def body(*refs):
        src, land = refs[:ns], refs[ns:ns + n]
        waits = refs[ns + n:ns + n + nw]
        new = refs[ns + n + nw + na:ns + n + nw + na + n_new]
        body_fn(src, land, waits, new)
        if token:
            refs[-1][...] = jnp.zeros((8, LANES), f32)

    hbm = [pltpu.HBM(a.shape, a.dtype) for a in lands]
    ops = [pltpu.with_memory_space_constraint(a, pltpu.HBM) for a in list(srcs) + list(lands)]
    extra_shape = [jax.ShapeDtypeStruct((8, LANES), f32)] if token else []
    extra_spec = [pl.BlockSpec(memory_space=pltpu.VMEM)] if token else []
    outs = pl.pallas_call(
        body,
        out_shape=tuple([pltpu.SemaphoreType.DMA(())] * n_new + hbm + extra_shape),
        in_specs=[_HBM] * (ns + n) + [_SEM] * nw + [pl.BlockSpec(memory_space=pl.ANY)] * na,
        out_specs=tuple([_SEM] * n_new + [_HBM] * n + extra_spec),
        input_output_aliases={ns + i: n_new + i for i in range(n)},
        compiler_params=pltpu.CompilerParams(has_side_effects=_EFFECT),
        name=name,
    )(*ops, *wait_sems, *after)
    if token:
        return list(outs[:n_new]), list(outs[n_new:n_new + n]), outs[-1][0, 0]
    return list(outs[:n_new]), list(outs[n_new:])


class _GatheredWeights:
    def __init__(self, shards):
        self.n = n = len(shards)
        x, y, c = lax.axis_index("x"), lax.axis_index("y"), lax.axis_index("c")
        self.x = x
        me = 4 * x + 2 * y + c
        lands = [lax.dynamic_update_slice(lax.empty((NDEV,) + s.shape, s.dtype), s[None], (me,) + (0,) * s.ndim)
                 for s in shards]

        def start_own(src, land, waits, new):
            p = self._peers()
            for a in range(n):
                for k, to in ((0, p["sibling"]), (1, p["xn"]), (2, p["yn"])):
                    self._copy(land[a], new, a, k, 3, p["me"], to).start()

        self.sems, self.lands = {}, None
        new, self.lands = _comm_step("gather_start", start_own, lands, n_new=6 * n)
        self._keep(new, (0, 1, 2))

    @staticmethod
    def _peers():
        x, y, c = lax.axis_index("x"), lax.axis_index("y"), lax.axis_index("c")
        return dict(
            me=(x, y, c), sibling=(x, y, 1 - c), xn=(1 - x, y, c), yn=(x, 1 - y, c), dg=(1 - x, 1 - y, c),
            relay_origin=(jnp.bitwise_xor(x, c), jnp.bitwise_xor(y, 1 - c), c),
            relay_target=(jnp.bitwise_xor(x, 1 - c), jnp.bitwise_xor(y, c), c))

    def _keep(self, new, ks):
        half = len(new) // 2
        i = 0
        for a in range(self.n):
            for k in ks:
                self.sems[a, k] = (new[i], new[half + i])
                i += 1

    @staticmethod
    def _copy(land, sem_refs, a, k, nk, block, to, src=None, ks=None):
        ks = tuple(range(nk)) if ks is None else ks
        half = len(sem_refs) // 2
        i = a * len(ks) + ks.index(k)
        slot = land.at[4 * block[0] + 2 * block[1] + block[2]]
        return pltpu.make_async_remote_copy(
            src_ref=slot if src is None else src, dst_ref=slot, send_sem=sem_refs[i], recv_sem=sem_refs[half + i],
            device_id=to, device_id_type=pl.DeviceIdType.MESH)

    def _sem_list(self, ks):
        return ([self.sems[a, k][0] for a in range(self.n) for k in ks]
                + [self.sems[a, k][1] for a in range(self.n) for k in ks])

    def first_half(self, after):
        n = self.n

        def relay(src, land, waits, new):
            p = self._peers()
            for a in range(n):
                self._copy(land[a], waits, a, 1, 0, p["xn"], p["me"], ks=(1, 2)).wait_recv()
                self._copy(land[a], waits, a, 2, 0, p["yn"], p["me"], ks=(1, 2)).wait_recv()
                self._copy(land[a], new, a, 3, 0, p["relay_origin"], p["relay_target"], ks=(3, 4, 5)).start()
                self._copy(land[a], new, a, 4, 0, p["xn"], p["sibling"], ks=(3, 4, 5)).start()
                self._copy(land[a], new, a, 5, 0, p["yn"], p["sibling"], ks=(3, 4, 5)).start()

        new, self.lands = _comm_step("gather_relay", relay, self.lands, wait_sems=self._sem_list((1, 2)),
                                     n_new=6 * n, after=after)
        self._keep(new, (3, 4, 5))

        def from_sibling(src, land, waits, new):
            p = self._peers()
            other = lambda b: (b[0], b[1], 1 - b[2])
            for a in range(n):
                self._copy(land[a], waits, a, 0, 0, other(p["me"]), p["me"], ks=(0, 4, 5)).wait_recv()
                self._copy(land[a], waits, a, 4, 0, other(p["xn"]), p["me"], ks=(0, 4, 5)).wait_recv()
                self._copy(land[a], waits, a, 5, 0, other(p["yn"]), p["me"], ks=(0, 4, 5)).wait_recv()

        _, self.lands = _comm_step("gather_wait_sibling", from_sibling, self.lands,
                                   wait_sems=self._sem_list((0, 4, 5)))
        return self.lands[0].reshape(NW, D), self.x.astype(jnp.int32).reshape(1)

    def second_half(self, after):
        n = self.n

        def forward_diagonal(src, land, waits, new):
            p = self._peers()
            for a in range(n):
                self._copy(land[a], waits, a, 3, 0, p["dg"], p["me"], ks=(3,)).wait_recv()
                self._copy(land[a], new, a, 6, 0, p["dg"], p["sibling"], ks=(6,)).start()

        new, self.lands = _comm_step("gather_forward_diagonal", forward_diagonal, self.lands,
                                     wait_sems=self._sem_list((3,)), n_new=2 * n, after=after)
        self._keep(new, (6,))

        def finish(src, land, waits, new):
            p = self._peers()
            ks = tuple(range(7))
            for a in range(n):
                self._copy(land[a], waits, a, 6, 0, (p["dg"][0], p["dg"][1], 1 - p["dg"][2]), p["me"], ks=ks).wait_recv()
                for k in ks:
                    self._copy(land[a], waits, a, k, 0, p["me"], p["me"], ks=ks).wait_send()

        _, self.lands = _comm_step("gather_finish", finish, self.lands, wait_sems=self._sem_list(tuple(range(7))))
        return self.lands[0].reshape(NW, D), (1 - self.x).astype(jnp.int32).reshape(1)

    def rest(self):
        g_a, g_b, g_o, g_bm = self.lands[1:]
        return (g_a.transpose(1, 0, 2).reshape(512, D), g_b.transpose(1, 0, 2).reshape(512, D),
                g_bm.transpose(1, 0, 2).reshape(2, D), g_o.reshape(D, D))


def _sibling_send_start(shares):
    landing = lax.empty(shares.shape, shares.dtype)

    def start(src, land, waits, new):
        x, y, c = lax.axis_index("x"), lax.axis_index("y"), lax.axis_index("c")
        pltpu.make_async_remote_copy(src_ref=land[0], dst_ref=land[1], send_sem=new[0], recv_sem=new[1],
                                     device_id=(x, y, 1 - c), device_id_type=pl.DeviceIdType.MESH).start()

    return _comm_step("grad_sibling_start", start, [shares, landing], n_new=2, token=True)


def _sibling_send_wait(sems, lands, after):
    def wait(src, land, waits, new):
        x, y, c = lax.axis_index("x"), lax.axis_index("y"), lax.axis_index("c")
        done = pltpu.make_async_remote_copy(src_ref=land[0], dst_ref=land[1], send_sem=waits[0], recv_sem=waits[1],
                                            device_id=(x, y, c), device_id_type=pl.DeviceIdType.MESH)
        done.wait_send()
        done.wait_recv()

    _, lands = _comm_step("grad_sibling_wait", wait, lands, wait_sems=sems, after=after)
    return lands[1]


def _row_tile(rows, limit=256):
    fits = [t for t in range(16, limit + 1, 16) if rows % t == 0]
    return fits[-1] if fits else rows


def _pair_sum(mine, theirs, name):
    nb, rows, cols = mine.shape
    tr = _row_tile(rows, 528)

    def body(a_ref, b_ref, o_ref):
        o_ref[...] = (a_ref[...].astype(f32) + b_ref[...].astype(f32)).astype(bf16)

    blk = pl.BlockSpec((None, tr, cols), lambda q, i: (q, i, 0))
    return pl.pallas_call(
        body,
        grid=(nb, rows // tr),
        in_specs=[blk, blk],
        out_specs=blk,
        out_shape=jax.ShapeDtypeStruct(mine.shape, bf16),
        compiler_params=_params(("arbitrary", "arbitrary")),
        name=name,
    )(mine, theirs)


def _scatter_start(chip_arrs, all_arrs, name):
    arrs = list(chip_arrs) + list(all_arrs)
    n, nc = len(arrs), len(chip_arrs)
    lands = [lax.empty(((3 if i < nc else NDEV - 1),) + a.shape[1:], a.dtype) for i, a in enumerate(arrs)]

    def body(*refs):
        src, land = refs[:n], refs[n:2 * n]
        send_sems, recv_sems = refs[2 * n:3 * n], refs[3 * n:4 * n]
        token = refs[6 * n]
        x, y, c = lax.axis_index("x"), lax.axis_index("y"), lax.axis_index("c")
        for a in range(n):
            for r in range(1, 4 if a < nc else NDEV):
                if a < nc:
                    px, py, pc = (1 - x if r & 2 else x), (1 - y if r & 1 else y), c
                    block = 2 * px + py
                else:
                    px, py, pc = (1 - x if r & 4 else x), (1 - y if r & 2 else y), (1 - c if r & 1 else c)
                    block = 4 * px + 2 * py + pc
                pltpu.make_async_remote_copy(
                    src_ref=src[a].at[block], dst_ref=land[a].at[r - 1], send_sem=send_sems[a],
                    recv_sem=recv_sems[a], device_id=(px, py, pc), device_id_type=pl.DeviceIdType.MESH).start()
        token[...] = jnp.zeros_like(token)

    hbm = [pltpu.HBM(a.shape, a.dtype) for a in arrs + lands]
    ops = [pltpu.with_memory_space_constraint(a, pltpu.HBM) for a in arrs + lands]
    outs = pl.pallas_call(
        body,
        out_shape=tuple([pltpu.SemaphoreType.DMA(())] * (2 * n) + hbm + [jax.ShapeDtypeStruct((8, LANES), f32)]),
        in_specs=[_HBM] * (2 * n),
        out_specs=tuple([_SEM] * (2 * n) + [_HBM] * (2 * n) + [pl.BlockSpec(memory_space=pltpu.VMEM)]),
        input_output_aliases={i: 2 * n + i for i in range(2 * n)},
        compiler_params=pltpu.CompilerParams(has_side_effects=_EFFECT),
        name=name,
    )(*ops)
    return outs[:n], outs[n:2 * n], outs[2 * n:3 * n], outs[3 * n:4 * n], outs[4 * n]


def _scatter_wait(send_sems, recv_sems, srcs, lands, after, name):
    n = len(srcs)

    def body(*refs):
        land = refs[n:2 * n]
        ssem, rsem = refs[2 * n:3 * n], refs[3 * n:4 * n]
        x, y, c = lax.axis_index("x"), lax.axis_index("y"), lax.axis_index("c")
        for a in range(n):
            done = pltpu.make_async_remote_copy(
                src_ref=land[a], dst_ref=land[a], send_sem=ssem[a], recv_sem=rsem[a], device_id=(x, y, c),
                device_id_type=pl.DeviceIdType.MESH)
            done.wait_send()
            done.wait_recv()

    hbm = [pltpu.HBM(a.shape, a.dtype) for a in list(srcs) + list(lands)]
    outs = pl.pallas_call(
        body,
        out_shape=tuple(hbm),
        in_specs=[_HBM] * (2 * n) + [_SEM] * (2 * n) + [pl.BlockSpec(memory_space=pl.ANY)],
        out_specs=tuple([_HBM] * (2 * n)),
        input_output_aliases={i: i for i in range(2 * n)},
        compiler_params=pltpu.CompilerParams(has_side_effects=_EFFECT),
        name=name,
    )(*srcs, *lands, *send_sems, *recv_sems, after)
    return outs[:n], outs[n:]


def _adam_update(g, w_ref, m_ref, v_ref, g_ref, d_ref, nm_ref, nv_ref):
    mm = ADAM_B1 * m_ref[...] + (1.0 - ADAM_B1) * g
    vv = ADAM_B2 * v_ref[...] + (1.0 - ADAM_B2) * (g * g)
    m_hat = mm / (1.0 - ADAM_B1 ** ADAM_STEP)
    v_hat = vv / (1.0 - ADAM_B2 ** ADAM_STEP)
    g_ref[...] = g
    d_ref[...] = -ADAM_LR * (m_hat / (jnp.sqrt(v_hat) + ADAM_EPS) + ADAM_WD * w_ref[...])
    nm_ref[...] = mm
    nv_ref[...] = vv


def _adamw_own(w, own, own_idx, slots, m, v, name):
    r, c = w.shape[-2:]
    tr = _row_tile(r, 384)
    k = slots.shape[0]

    def body(i_ref, w_ref, o_ref, s_ref, m_ref, v_ref, g_ref, d_ref, nm_ref, nv_ref):
        del i_ref
        g = o_ref[...].astype(f32)
        for j in range(k):
            g = g + s_ref[j].astype(f32)
        _adam_update(g, w_ref, m_ref, v_ref, g_ref, d_ref, nm_ref, nv_ref)

    blk = pl.BlockSpec((None, tr, c), lambda i, ix: (0, i, 0))
    return pl.pallas_call(
        body,
        grid_spec=pltpu.PrefetchScalarGridSpec(
            num_scalar_prefetch=1,
            grid=(r // tr,),
            in_specs=[blk, pl.BlockSpec((None, tr, c), lambda i, ix: (ix[0], i, 0)),
                      pl.BlockSpec((k, tr, c), lambda i, ix: (0, i, 0)), blk, blk],
            out_specs=[blk] * 4,
        ),
        out_shape=[jax.ShapeDtypeStruct(w.shape, f32)] * 4,
        compiler_params=_params(("arbitrary",)),
        name=name,
    )(own_idx, w, own, slots, m, v)


def _adamw(w, slots, m, v, name):
    r, c = w.shape[-2:]
    tr = _row_tile(r, 128)

    def body(w_ref, s_ref, m_ref, v_ref, g_ref, d_ref, nm_ref, nv_ref):
        g = s_ref[0].astype(f32)
        for k in range(1, NDEV):
            g = g + s_ref[k].astype(f32)
        _adam_update(g, w_ref, m_ref, v_ref, g_ref, d_ref, nm_ref, nv_ref)

    if w.ndim == 3:
        blk = pl.BlockSpec((None, tr, c), lambda i: (0, i, 0))
    else:
        blk = pl.BlockSpec((tr, c), lambda i: (i, 0))
    return pl.pallas_call(
        body,
        grid=(r // tr,),
        in_specs=[blk, pl.BlockSpec((NDEV, tr, c), lambda i: (0, i, 0)), blk, blk],
        out_specs=[blk] * 4,
        out_shape=[jax.ShapeDtypeStruct(w.shape, f32)] * 4,
        compiler_params=_params(("arbitrary",)),
        name=name,
    )(w, slots, m, v)


class _Weights:
    def __init__(self, w_t, w_a, w_b, b_merge, w_o):
        self._w_t, self._rest = w_t, (w_a, w_b, b_merge, w_o)

    def first_half(self, after):
        del after
        return self._w_t, jnp.zeros((1,), jnp.int32)

    def second_half(self, after):
        del after
        return self._w_t, jnp.ones((1,), jnp.int32)

    def rest(self):
        return self._rest


def _local_step(x, tgt, norm_gain, weights, qn_a, kn_a, qn_b, kn_b, sink_a, rel_bias, on_weight_grads=None,
                core=None):
    two = lambda t: jnp.concatenate([t, t], axis=-1).reshape(1, LANES)
    ones = jnp.ones((1, LANES), f32)
    gains = jnp.stack([
        jnp.stack([two(qn_a), two(kn_a), ones]),
        jnp.stack([two(qn_b), two(kn_b), ones]),
        jnp.stack([two(qn_b), two(kn_b), ones]),
        jnp.stack([two(qn_b), two(kn_b), ones]),
    ])
    buckets = [jnp.asarray(_bucket_np(blk, d)) for blk, d, _ in GROUPS]
    bias = [_bias_expand(rel_bias, buckets[k], GROUPS[k][2], "bias_expand_%d" % k) for k in range(4)]

    hb, hbt, rstd = _rms(x, norm_gain)
    w_t, half = weights.first_half([hb] + bias)
    proj = _inproj_half(hb, w_t, half, None, "inproj_1")
    w_t, half = weights.second_half([proj])
    proj = _inproj_half(hb, w_t, half, proj, "inproj_2")
    w_a, w_b, b_merge, w_o = weights.rest()
    gl = _prep(proj, gains)
    o_a, l_a = _attn_fwd(gl, bias[0], sink_a.reshape(8), 0, 128, 1, "attn_fwd_a")
    fwd_b = [_attn_fwd(gl, bias[k], None, k, GROUPS[k][0], GROUPS[k][1], "attn_fwd_b%d" % k) for k in (1, 2, 3)]
    sink_b = jnp.repeat(sink_a.reshape(8), HD).reshape(1, 512)

    (dy, dyb, dproj, do_a, dd_a, do_b0, do_b1, do_b2, dd_b0, dd_b1, dd_b2, ya, yb, mg, dbr_a, dbr_b, loss, dbm,
     dsk) = _tail(x, tgt, o_a, l_a, [f[0] for f in fwd_b], [f[1] for f in fwd_b], proj, b_merge, w_a, w_b, w_o, sink_b)

    dw_o = _matmul_tokens(mg, dyb, "dw_out")
    dw_a = _matmul_tokens(ya, dbr_a, "dw_branch_a")
    dw_b = _matmul_tokens(yb, dbr_b, "dw_branch_b")
    if on_weight_grads is not None:
        early = on_weight_grads(dict(w_branch_a=dw_a, w_branch_b=dw_b, b_merge=dbm, w_out=dw_o))
        buckets = [buckets[0] + early.astype(jnp.int32)] + buckets[1:]

    dqkv_a, dbk_a = _attn_bwd(gl, bias[0], buckets[0], do_a, l_a, dd_a, 0, 128, 1, "attn_bwd_a")
    dproj, dg_a = _post_a(dqkv_a, proj, gains, dproj)
    dbk_b, dg_b = [], []
    for k, do_k, dd_k in ((1, do_b0, dd_b0), (2, do_b1, dd_b1), (3, do_b2, dd_b2)):
        dqkv, dbk = _attn_bwd(gl, bias[k], buckets[k], do_k, fwd_b[k - 1][1], dd_k, k, GROUPS[k][0], GROUPS[k][1],
                              "attn_bwd_b%d" % k)
        dproj, dg = _post_b(k, dqkv, proj, gains, dproj)
        dbk_b.append(dbk)
        dg_b.append(dg)
    dg_b = jnp.stack(dg_b)

    core = jnp.zeros((1,), jnp.int32) if core is None else core
    dw_other = _dw_in(hbt, dproj, 1 - core, "dw_in_other")
    sent = jnp.zeros((), f32) if on_weight_grads is None else on_weight_grads(dict(w_in_other=dw_other))
    dw_in = _dw_in(hbt, dproj, core + sent.astype(jnp.int32), "dw_in_own")
    token = jnp.zeros((), f32) if on_weight_grads is None else on_weight_grads(dict(w_in=dw_in))
    grad_x, d_norm_gain = _dh_norm_bwd(dproj, w_t, x, rstd, norm_gain + token, dy)

    fold = lambda t: t[..., :HD] + t[..., HD:]
    d_qn_a = fold(dg_a[0, 0])
    d_kn_a = fold(dg_a[1, 0])
    d_qn_b = fold(dg_b[:, 0, 0].sum(axis=0))
    d_kn_b = fold(dg_b[:, 1, 0].sum(axis=0))
    d_sink = dsk.reshape(8, HD)[:, 0]
    red = jnp.stack([dbk_a] + dbk_b)
    d_rel = red[:, :, 0, :32].reshape(32, 32).T
    return dict(loss=loss, grad_x=grad_x, norm_gain=d_norm_gain, w_in=dw_in, w_in_other=dw_other, q_norm_a=d_qn_a,
                k_norm_a=d_kn_a,
                q_norm_b=d_qn_b, k_norm_b=d_kn_b, sink_a=d_sink, rel_bias=d_rel, w_branch_a=dw_a, w_branch_b=dw_b,
                b_merge=dbm, w_out=dw_o)


SMALL = (("norm_gain", D), ("q_norm_a", HD), ("k_norm_a", HD), ("q_norm_b", HD), ("k_norm_b", HD), ("sink_a", 8),
         ("rel_bias", 1024))
SMALL_PAD = 2432


SMALL_USED = sum(sz for _, sz in SMALL)


def _pack_small(parts, loss=None):
    tail = jnp.zeros((SMALL_PAD - SMALL_USED,), f32)
    if loss is not None:
        tail = tail.at[0].set(loss.reshape(()))
    return jnp.concatenate([parts[n].reshape(-1) for n, _ in SMALL] + [tail]).reshape(1, SMALL_PAD)


def _unpack_small(flat, shapes):
    out, off = {}, 0
    for n, sz in SMALL:
        out[n] = flat[0, off:off + sz].reshape(shapes[n])
        off += sz
    return out


def kernel(x, norm_gain, w_in, q_norm_a, k_norm_a, q_norm_b, k_norm_b, sink_a, rel_bias, w_branch_a, w_branch_b, b_merge, w_out, loss_target, m_norm_gain, m_w_in, m_q_norm_a, m_k_norm_a, m_q_norm_b, m_k_norm_b, m_sink_a, m_rel_bias, m_w_branch_a, m_w_branch_b, m_b_merge, m_w_out, v_norm_gain, v_w_in, v_q_norm_a, v_k_norm_a, v_q_norm_b, v_k_norm_b, v_sink_a, v_rel_bias, v_w_branch_a, v_w_branch_b, v_b_merge, v_w_out):
    csh = D // NDEV
    w_in_t, m_w_in_t, v_w_in_t = (jnp.swapaxes(t, 1, 2) for t in (w_in, m_w_in, v_w_in))
    weights = _GatheredWeights([w_in_t[0].astype(bf16), w_branch_a[0].astype(bf16), w_branch_b[0].astype(bf16),
                                w_out[0].astype(bf16), b_merge[0]])

    pending = {}
    core = lax.axis_index("c").astype(jnp.int32).reshape(1)
    chip = (2 * lax.axis_index("x") + lax.axis_index("y")).astype(jnp.int32).reshape(1)
    me = (2 * chip + core).astype(jnp.int32)

    def start_exchange(gw):
        if "w_in_other" in gw:
            sems, lands, sent = _sibling_send_start(gw["w_in_other"])
            pending["sibling"] = (sems, lands)
            return sent
        if "w_in" in gw:
            from_sibling = _sibling_send_wait(*pending["sibling"], after=[gw["w_in"]])
            chip_sums = _pair_sum(gw["w_in"], from_sibling, "grad_pair_sum")
            pending["w_in"] = _scatter_start([chip_sums], [], "scatter_w_in_start")
            return pending["w_in"][4][0, 0]
        blocks = [gw["w_branch_a"].reshape(512, NDEV, csh).transpose(1, 0, 2).astype(bf16),
                  gw["w_branch_b"].reshape(512, NDEV, csh).transpose(1, 0, 2).astype(bf16),
                  gw["w_out"].reshape(NDEV, csh, D).astype(bf16),
                  gw["b_merge"].reshape(2, NDEV, csh).transpose(1, 0, 2)]
        pending["rest"] = _scatter_start([], blocks, "scatter_rest_start")
        return pending["rest"][4][0, 0]

    loc = _local_step(x[0], loss_target[0], norm_gain, weights, q_norm_a, k_norm_a, q_norm_b, k_norm_b, sink_a,
                      rel_bias, on_weight_grads=start_exchange, core=core)

    small_shapes = dict(norm_gain=(1, D), q_norm_a=(1, HD), k_norm_a=(1, HD), q_norm_b=(1, HD), k_norm_b=(1, HD),
                        sink_a=(1, 8), rel_bias=(32, 32))
    (r_small,) = _exchange([], [_pack_small(loc, loc["loss"])], "gather_small_grads")
    send_sems, recv_sems, srcs, lands, _ = pending["rest"]
    (s_a, s_b, s_o, s_bm), (r_a, r_b, r_o, r_bm) = _scatter_wait(
        send_sems, recv_sems, srcs, lands, r_small, "scatter_rest_wait")
    send_sems, recv_sems, srcs, lands, _ = pending["w_in"]
    (s_in,), (r_in,) = _scatter_wait(send_sems, recv_sems, srcs, lands, r_small, "scatter_w_in_wait")

    given = dict(norm_gain=norm_gain, q_norm_a=q_norm_a, k_norm_a=k_norm_a, q_norm_b=q_norm_b, k_norm_b=k_norm_b,
                 sink_a=sink_a, rel_bias=rel_bias)
    m_small = dict(norm_gain=m_norm_gain, q_norm_a=m_q_norm_a, k_norm_a=m_k_norm_a, q_norm_b=m_q_norm_b,
                   k_norm_b=m_k_norm_b, sink_a=m_sink_a, rel_bias=m_rel_bias)
    v_small = dict(norm_gain=v_norm_gain, q_norm_a=v_q_norm_a, k_norm_a=v_k_norm_a, q_norm_b=v_q_norm_b,
                   k_norm_b=v_k_norm_b, sink_a=v_sink_a, rel_bias=v_rel_bias)
    res = {
        "small": _adamw(_pack_small(given), r_small, _pack_small(m_small), _pack_small(v_small), "adamw_small"),
        "w_in": [jnp.swapaxes(t, 1, 2) for t in
                 _adamw_own(w_in_t, s_in, chip, r_in, m_w_in_t, v_w_in_t, "adamw_w_in")],
        "w_branch_a": _adamw_own(w_branch_a, s_a, me, r_a, m_w_branch_a, v_w_branch_a, "adamw_w_branch_a"),
        "w_branch_b": _adamw_own(w_branch_b, s_b, me, r_b, m_w_branch_b, v_w_branch_b, "adamw_w_branch_b"),
        "b_merge": _adamw_own(b_merge, s_bm, me, r_bm, m_b_merge, v_b_merge, "adamw_b_merge"),
        "w_out": _adamw_own(w_out, s_o, me, r_o, m_w_out, v_w_out, "adamw_w_out"),
    }
    order = ["norm_gain", "w_in", "q_norm_a", "k_norm_a", "q_norm_b", "k_norm_b", "sink_a", "rel_bias", "w_branch_a",
             "w_branch_b", "b_merge", "w_out"]
    outs = []
    for k in range(4):
        small = _unpack_small(res["small"][k], small_shapes)
        for n in order:
            outs.append(small[n] if n in small else res[n][k])
    loss = res["small"][0][0, SMALL_USED]
    return (loss, loc["grad_x"][None], *outs)
```

```python
import math

import numpy as np
import jax
import jax.numpy as jnp
from jax import lax
from jax.experimental import pallas as pl
from jax.experimental.pallas import tpu as pltpu

f32 = jnp.float32
bf16 = jnp.bfloat16

S = 4096
D = 1024
NA = 5376
NT = 3072
NW = NA + NT
WSH = NW // 8
HD = 64
LANES = 128
EPS = 1e-6
NEG = -1e30
SCALE = HD ** -0.5
TQ = 128
PAD = 128
SP = S + 2 * PAD
NDEV = 8
GROUPS = ((128, 1, 0), (64, 1, 8), (64, 4, 16), (64, 16, 24))
CHUNK = 256
PCHUNK = 128
RC = 64

ADAM_LR, ADAM_B1, ADAM_B2, ADAM_EPS, ADAM_WD, ADAM_STEP = 0.001, 0.9, 0.999, 1e-08, 0.01, 10

MIB = 1024 * 1024
NT_DIMS = (((1,), (1,)), ((), ()))
TN_DIMS = (((0,), (0,)), ((), ()))


def _params(sem=None, vmem_mib=48):
    return pltpu.CompilerParams(dimension_semantics=sem, vmem_limit_bytes=vmem_mib * MIB)


def _lo():
    return lax.broadcasted_iota(jnp.int32, (1, LANES), 1) < HD


def _head_ones():
    r = lax.broadcasted_iota(jnp.int32, (LANES, LANES), 0) // HD
    c = lax.broadcasted_iota(jnp.int32, (LANES, LANES), 1) // HD
    return jnp.where(r == c, 1.0, 0.0).astype(bf16)


def _half_sums(x, ones):
    hi = x.astype(bf16)
    mid = (x - hi.astype(f32)).astype(bf16)
    return (jnp.dot(hi, ones, preferred_element_type=f32) + jnp.dot(mid, ones, preferred_element_type=f32))


def _seg_sum(x, ones):
    outs = [_half_sums(x[:, b * LANES:(b + 1) * LANES], ones) for b in range(x.shape[1] // LANES)]
    return outs[0] if len(outs) == 1 else jnp.concatenate(outs, axis=1)


def _bucket_np(blk, stride):
    w = TQ + 2 * blk
    rel = np.arange(w)[None, :] - blk - np.arange(TQ)[:, None]
    band = np.abs(rel) <= blk
    r = rel * stride
    n = np.abs(r)
    nf = np.maximum(n, 8).astype(np.float32)
    large = 8 + (np.log(nf / np.float32(8)) / np.float32(math.log(128.0)) * np.float32(8)).astype(np.int32)
    large = np.minimum(large, 15)
    b = (r > 0).astype(np.int32) * 16 + np.where(n < 8, n, large)
    return np.where(band, b, -1).astype(np.int32)


def _rms(x, gain):
    ts = 512

    def body(x_ref, g_ref, h_ref, ht_ref, r_ref):
        xv = x_ref[...]
        r = lax.rsqrt(jnp.mean(xv * xv, axis=-1, keepdims=True) + EPS)
        h = (xv * r) * g_ref[...]
        h_ref[...] = h.astype(bf16)
        ht_ref[...] = h.T.astype(bf16)
        r_ref[...] = r

    return pl.pallas_call(
        body,
        grid=(S // ts,),
        in_specs=[pl.BlockSpec((ts, D), lambda i: (i, 0)), pl.BlockSpec((1, D), lambda i: (0, 0))],
        out_specs=[pl.BlockSpec((ts, D), lambda i: (i, 0)), pl.BlockSpec((D, ts), lambda i: (0, i)),
                   pl.BlockSpec((ts, 1), lambda i: (i, 0))],
        out_shape=[jax.ShapeDtypeStruct((S, D), bf16), jax.ShapeDtypeStruct((D, S), bf16),
                   jax.ShapeDtypeStruct((S, 1), f32)],
        compiler_params=_params(("arbitrary",)),
        name="rms",
    )(x, gain)


def _inproj_half(hb, w_t, half, proj, name):
    ts = 512
    tn = NW // 2
    per = NW // 2 // tn

    def body(h_idx, h_ref, w_ref, *rest):
        del h_idx
        rest[-1][...] = lax.dot_general(h_ref[...], w_ref[...], NT_DIMS, preferred_element_type=f32)

    in_specs = [pl.BlockSpec((ts, D), lambda i, n, hf: (i, 0)),
                pl.BlockSpec((tn, D), lambda i, n, hf: (hf[0] * per + n, 0))]
    args = [half, hb, w_t]
    aliases = {}
    if proj is not None:
        in_specs.append(pl.BlockSpec(memory_space=pl.ANY))
        args.append(proj)
        aliases = {3: 0}
    return pl.pallas_call(
        body,
        grid_spec=pltpu.PrefetchScalarGridSpec(
            num_scalar_prefetch=1,
            grid=(S // ts, per),
            in_specs=in_specs,
            out_specs=pl.BlockSpec((ts, tn), lambda i, n, hf: (i, hf[0] * per + n)),
        ),
        out_shape=jax.ShapeDtypeStruct((S, NW), f32),
        input_output_aliases=aliases,
        compiler_params=_params(("arbitrary", "arbitrary")),
        name=name,
    )(*args)


def _bias_expand(table, bucket, c0, name):
    tq, w = bucket.shape
    blk = (w - tq) // 2

    def body(tab_ref, bk_ref, o_ref):
        h = pl.program_id(0)
        bk = bk_ref[...]

        def step(b, acc):
            return jnp.where(bk == b, tab_ref[b, c0 + h], acc)

        inner = lax.fori_loop(0, 32, step, jnp.full((tq, w), NEG, f32))
        col = lax.broadcasted_iota(jnp.int32, (1, w), 1)
        o_ref[0] = jnp.where(col < blk, NEG, inner)
        o_ref[1] = inner
        o_ref[2] = jnp.where(col >= tq + blk, NEG, inner)

    return pl.pallas_call(
        body,
        grid=(8,),
        in_specs=[pl.BlockSpec(memory_space=pltpu.SMEM), pl.BlockSpec((tq, w), lambda h: (0, 0))],
        out_specs=pl.BlockSpec((3, None, tq, w), lambda h: (0, h, 0, 0)),
        out_shape=jax.ShapeDtypeStruct((3, 8, tq, w), f32),
        compiler_params=_params(("arbitrary",)),
        name=name,
    )(table, bucket)


def _tile_kind(t, seq):
    m0 = jnp.bitwise_and(t * TQ, seq - 1)
    return jnp.where(m0 == 0, 0, jnp.where(m0 == seq - TQ, 2, 1))


def _col_block(g, j):
    kind = j // 4
    hp = j % 4
    a = jnp.where(kind == 0, hp, 3 + kind)
    b = 6 + 12 * kind + 4 * (g - 1) + hp
    return jnp.where(g == 0, a, b)


def _prep(proj_a, gains):
    def body(pa_ref, pb_ref, g_ref, o_ref):
        g = pl.program_id(0)
        pj = pl.program_id(1)
        kind = pj // 2
        lo = _lo()
        ones = _head_ones()
        half = jnp.where(lo, 0, 1)
        gain = g_ref[...]

        def norm_store(xv, u, dst, dup):
            if dup:
                take = (kind == 0) | (half == ((2 * pj + u) % 4) // 2)
                xv = jnp.where(take, xv, pltpu.roll(xv, HD, 1))
            r = lax.rsqrt(_half_sums(xv * xv, ones) * (1.0 / HD) + EPS)
            r = jnp.where(kind == 2, 1.0, r)
            yv = (xv * r) * gain
            yv = jnp.where(kind == 0, yv * SCALE, yv)
            o_ref[u, PAD + dst:PAD + dst + CHUNK, :] = yv.astype(bf16)

        for u in range(2):
            o_ref[u, 0:PAD, :] = jnp.zeros((PAD, LANES), bf16)
            o_ref[u, PAD + S:SP, :] = jnp.zeros((PAD, LANES), bf16)
        for gi, (_, d, _) in enumerate(GROUPS):
            @pl.when(g == gi)
            def _():
                seq = S // d
                for u, p_ref in enumerate((pa_ref, pb_ref)):
                    for c in range(d):
                        for i in range(seq // CHUNK):
                            if d == 1:
                                xv = p_ref[i * CHUNK:(i + 1) * CHUNK, :]
                            else:
                                xv = p_ref[pl.ds(c + i * CHUNK * d, CHUNK, stride=d), :]
                            norm_store(xv, u, c * seq + i * CHUNK, gi == 0)

    return pl.pallas_call(
        body,
        grid=(4, 6),
        in_specs=[
            pl.BlockSpec((S, LANES), lambda g, pj: (0, _col_block(g, 2 * pj))),
            pl.BlockSpec((S, LANES), lambda g, pj: (0, _col_block(g, 2 * pj + 1))),
            pl.BlockSpec((None, None, 1, LANES), lambda g, pj: (g, pj // 2, 0, 0)),
        ],
        out_specs=pl.BlockSpec((None, 2, SP, LANES), lambda g, pj: (g, pj, 0, 0)),
        out_shape=jax.ShapeDtypeStruct((4, 12, SP, LANES), bf16),
        compiler_params=_params(("arbitrary", "arbitrary")),
        name="prep",
    )(proj_a, proj_a, gains)


def _token_rows(t, r0, n, d):
    if d == 1:
        return pl.ds(pl.multiple_of(t * TQ, TQ) + r0, n)
    per = S // d // TQ
    return pl.ds(((t % per) * TQ + r0) * d + t // per, n, stride=d)


def _stack_heads(t, lo):
    z = jnp.zeros_like(t)
    return jnp.concatenate([jnp.where(lo, t, z), jnp.where(lo, z, t)], axis=0)


def _unstack_heads(t2, lo):
    return jnp.where(lo, t2[:TQ], t2[TQ:])


def _attn_fwd(gl, bias, sink, g, blk, d, name):
    w = TQ + 2 * blk
    seq = S // d
    use_sink = sink is not None

    def body(*refs):
        if use_sink:
            sink_ref, q_ref, k_ref, v_ref, b_ref, o_ref, l_ref, s0, s1, p0, p1, lse_scr = refs
        else:
            q_ref, k_ref, v_ref, b_ref, o_ref, l_ref, s0, s1, p0, p1, lse_scr = refs
        hp = pl.program_id(0)
        lo = _lo()
        s_bufs, p_bufs = (s0, s1), (p0, p1)

        def scores(p, slot):
            for u in range(2):
                f0 = pl.multiple_of((2 * p + u) * TQ, TQ)
                q2 = _stack_heads(q_ref[pl.ds(PAD + f0, TQ), :], lo)
                kw = k_ref[pl.ds(PAD - blk + f0, w), :]
                s_bufs[slot][u] = lax.dot_general(q2, kw, NT_DIMS, preferred_element_type=f32)

        def softmax(p, slot):
            for u in range(2):
                t = 2 * p + u
                kind = _tile_kind(t, seq)
                for h in range(2):
                    for r in range(TQ // RC):
                        rows = slice(h * TQ + r * RC, h * TQ + (r + 1) * RC)
                        logit = s_bufs[slot][u, rows, :] + b_ref[kind, h, r * RC:(r + 1) * RC, :]
                        m = jnp.max(logit, axis=1, keepdims=True)
                        e = jnp.exp(logit - m)
                        lse = m + jnp.log(jnp.sum(e, axis=1, keepdims=True))
                        if use_sink:
                            sk = sink_ref[2 * hp + h]
                            mx = jnp.maximum(lse, sk)
                            lse = mx + jnp.log(jnp.exp(lse - mx) + jnp.exp(sk - mx))
                        p_bufs[slot][u, rows, :] = (e * jnp.exp(m - lse)).astype(bf16)
                        lse_scr[u, rows, :] = jnp.broadcast_to(lse, (RC, LANES))
                l_ref[_token_rows(t, 0, TQ, d), :] = jnp.where(lo, lse_scr[u, 0:TQ, :], lse_scr[u, TQ:2 * TQ, :])

        def values(p, slot):
            for u in range(2):
                t = 2 * p + u
                vw = v_ref[pl.ds(PAD - blk + pl.multiple_of(t * TQ, TQ), w), :]
                o2 = jnp.dot(p_bufs[slot][u], vw, preferred_element_type=f32)
                o_ref[_token_rows(t, 0, TQ, d), :] = _unstack_heads(o2, lo)

        npair = S // TQ // 2
        scores(0, 0)
        scores(1, 1)
        softmax(0, 0)

        def steady(k, carry):
            p = 2 * k + 2
            scores(p, 0)
            softmax(p - 1, 1)
            values(p - 2, 0)
            scores(p + 1, 1)
            softmax(p, 0)
            values(p - 1, 1)
            return carry

        lax.fori_loop(0, (npair - 2) // 2, steady, 0)
        softmax(npair - 1, 1)
        values(npair - 2, 0)
        values(npair - 1, 1)

    in_specs = [
        pl.BlockSpec((None, None, SP, LANES), lambda hp: (g, hp, 0, 0)),
        pl.BlockSpec((None, None, SP, LANES), lambda hp: (g, 4 + hp, 0, 0)),
        pl.BlockSpec((None, None, SP, LANES), lambda hp: (g, 8 + hp, 0, 0)),
        pl.BlockSpec((3, 2, TQ, w), lambda hp: (0, hp, 0, 0)),
    ]
    args = [gl, gl, gl, bias]
    if use_sink:
        in_specs = [pl.BlockSpec(memory_space=pltpu.SMEM)] + in_specs
        args = [sink] + args
    out = pl.BlockSpec((S, LANES), lambda hp: (0, hp))
    return pl.pallas_call(
        body,
        grid=(4,),
        in_specs=in_specs,
        out_specs=[out, out],
        out_shape=[jax.ShapeDtypeStruct((S, 4 * LANES), f32)] * 2,
        scratch_shapes=[pltpu.VMEM((2, 2 * TQ, w), f32), pltpu.VMEM((2, 2 * TQ, w), f32),
                        pltpu.VMEM((2, 2 * TQ, w), bf16), pltpu.VMEM((2, 2 * TQ, w), bf16),
                        pltpu.VMEM((2, 2 * TQ, LANES), f32)],
        compiler_params=_params(("arbitrary",)),
        name=name,
    )(*args)


def _attn_bwd(gl, bias, bucket, do, lse, dd, g, blk, d, name):
    w = TQ + 2 * blk
    seq = S // d

    def body(q_ref, k_ref, v_ref, b_ref, bk_ref, do_ref, l_ref, d_ref, dqkv_ref, dbk_ref,
             db_acc, s0, s1, dp0, dp1, pb0, pb1, ds0, ds1, dk_acc, dv_acc):
        lo = _lo()
        hi = jnp.logical_not(lo)
        dk_acc[...] = jnp.zeros((SP, LANES), f32)
        dv_acc[...] = jnp.zeros((SP, LANES), f32)
        db_acc[...] = jnp.zeros((2 * TQ, w), f32)
        s_bufs, dp_bufs, pb_bufs, ds_bufs = (s0, s1), (dp0, dp1), (pb0, pb1), (ds0, ds1)

        def stacked(t):
            f0 = pl.multiple_of(t * TQ, TQ)
            q2 = _stack_heads(q_ref[pl.ds(PAD + f0, TQ), :], lo)
            do2 = _stack_heads(do_ref[_token_rows(t, 0, TQ, d), :].astype(bf16), lo)
            return f0, q2, do2

        def scores(p, slot):
            for u in range(2):
                f0, q2, do2 = stacked(2 * p + u)
                win = pl.ds(PAD - blk + f0, w)
                s_bufs[slot][u] = lax.dot_general(q2, k_ref[win, :], NT_DIMS, preferred_element_type=f32)
                dp_bufs[slot][u] = lax.dot_general(do2, v_ref[win, :], NT_DIMS, preferred_element_type=f32)

        def grads(p, slot):
            for u in range(2):
                t = 2 * p + u
                kind = _tile_kind(t, seq)
                for h in range(2):
                    msk = lo if h == 0 else hi
                    for r in range(TQ // RC):
                        rows = slice(h * TQ + r * RC, h * TQ + (r + 1) * RC)
                        src = _token_rows(t, r * RC, RC, d)
                        lh = jnp.max(jnp.where(msk, l_ref[src, :], -jnp.inf), axis=1, keepdims=True)
                        dh = jnp.max(jnp.where(msk, d_ref[src, :], -jnp.inf), axis=1, keepdims=True)
                        logit = s_bufs[slot][u, rows, :] + b_ref[kind, h, r * RC:(r + 1) * RC, :]
                        pr = jnp.exp(logit - lh)
                        ds = pr * (dp_bufs[slot][u, rows, :] - dh)
                        db_acc[rows, :] += ds
                        pb_bufs[slot][u, rows, :] = pr.astype(bf16)
                        ds_bufs[slot][u, rows, :] = ds.astype(bf16)

        def accumulate(p, slot):
            for u in range(2):
                f0, q2, do2 = stacked(2 * p + u)
                win = pl.ds(PAD - blk + f0, w)
                dsb = ds_bufs[slot][u]
                dq2 = jnp.dot(dsb, k_ref[win, :], preferred_element_type=f32)
                dqkv_ref[0, pl.ds(PAD + f0, TQ), :] = _unstack_heads(dq2, lo).astype(bf16)
                dk_acc[win, :] += lax.dot_general(dsb, q2, TN_DIMS, preferred_element_type=f32)
                dv_acc[win, :] += lax.dot_general(pb_bufs[slot][u], do2, TN_DIMS, preferred_element_type=f32)

        npair = S // TQ // 2
        scores(0, 0)
        scores(1, 1)
        grads(0, 0)

        def steady(k, carry):
            p = 2 * k + 2
            scores(p, 0)
            grads(p - 1, 1)
            accumulate(p - 2, 0)
            scores(p + 1, 1)
            grads(p, 0)
            accumulate(p - 1, 1)
            return carry

        lax.fori_loop(0, (npair - 2) // 2, steady, 0)
        grads(npair - 1, 1)
        accumulate(npair - 2, 0)
        accumulate(npair - 1, 1)
        for i in range(SP // CHUNK):
            rows = slice(i * CHUNK, (i + 1) * CHUNK)
            dqkv_ref[1, rows, :] = dk_acc[rows, :].astype(bf16)
            dqkv_ref[2, rows, :] = dv_acc[rows, :].astype(bf16)

        bk = bk_ref[...]
        lane = lax.broadcasted_iota(jnp.int32, (8, LANES), 1)
        for h in range(2):
            db = db_acc[h * TQ:(h + 1) * TQ, :]
            acc = jnp.zeros((8, LANES), f32)
            for b in range(32):
                part = jnp.where(bk == b, db, 0.0).reshape(TQ // 8, 8, w).sum(axis=0)
                tot = jnp.sum(jnp.sum(part, axis=1, keepdims=True), axis=0, keepdims=True)
                acc = jnp.where(lane == b, tot, acc)
            dbk_ref[h] = acc

    def gcol(off):
        return pl.BlockSpec((None, None, SP, LANES), lambda hp: (g, off + hp, 0, 0))

    row = pl.BlockSpec((S, LANES), lambda hp: (0, hp))
    return pl.pallas_call(
        body,
        grid=(4,),
        in_specs=[gcol(0), gcol(4), gcol(8), pl.BlockSpec((3, 2, TQ, w), lambda hp: (0, hp, 0, 0)),
                  pl.BlockSpec((TQ, w), lambda hp: (0, 0)), row, row, row],
        out_specs=[pl.BlockSpec((3, None, SP, LANES), lambda hp: (0, hp, 0, 0)),
                   pl.BlockSpec((2, 8, LANES), lambda hp: (hp, 0, 0))],
        out_shape=[
            jax.ShapeDtypeStruct((3, 4, SP, LANES), bf16),
            jax.ShapeDtypeStruct((8, 8, LANES), f32),
        ],
        scratch_shapes=([pltpu.VMEM((2 * TQ, w), f32)] + [pltpu.VMEM((2, 2 * TQ, w), f32)] * 4
                        + [pltpu.VMEM((2, 2 * TQ, w), bf16)] * 4 + [pltpu.VMEM((SP, LANES), f32)] * 2),
        compiler_params=_params(("arbitrary",), vmem_mib=56),
        name=name,
    )(gl, gl, gl, bias, bucket, do, lse, dd)


def _sigmoid(z):
    return 1.0 / (1.0 + jnp.exp(-z))


def _tail(x, tgt, o_a, l_a, o_b, l_b, proj, bm, w_a, w_b, w_o, sink_b):
    ts = 256

    def body(x_ref, t_ref, oa_ref, la_ref, ob0_ref, ob1_ref, ob2_ref, lb0_ref, lb1_ref, lb2_ref,
             ga_ref, gb_ref, m0_ref, m1_ref, bm_ref, wa_ref, wb_ref, wo_ref, sk_ref,
             dy_ref, dyb_ref, dt_ref, doa_ref, dda_ref, dob0_ref, dob1_ref, dob2_ref, ddb0_ref, ddb1_ref, ddb2_ref,
             ya_ref, yb_ref, mg_ref, dbra_ref, dbrb_ref, loss_ref, dbm_ref, dsk_ref):
        i = pl.program_id(0)

        @pl.when(i == 0)
        def _():
            loss_ref[...] = jnp.zeros_like(loss_ref)
            dbm_ref[...] = jnp.zeros_like(dbm_ref)
            dsk_ref[...] = jnp.zeros_like(dsk_ref)

        ga = ga_ref[...]
        sa = _sigmoid(ga)
        silu_a = ga * sa
        oa = oa_ref[...]
        ya = oa * silu_a
        gb = gb_ref[...]
        sb = _sigmoid(gb)
        silu_b = gb * sb
        ob = [ob0_ref[...], ob1_ref[...], ob2_ref[...]]
        lb = [lb0_ref[...], lb1_ref[...], lb2_ref[...]]
        mx = jnp.maximum(jnp.maximum(lb[0], lb[1]), lb[2])
        ex = [jnp.exp(v - mx) for v in lb]
        den = ex[0] + ex[1] + ex[2]
        alpha = [e / den for e in ex]
        ybc = alpha[0] * ob[0] + alpha[1] * ob[1] + alpha[2] * ob[2]
        yb = ybc * silu_b
        yab = ya.astype(bf16)
        ybb = yb.astype(bf16)
        br_a = jnp.dot(yab, wa_ref[...], preferred_element_type=f32)
        br_b = jnp.dot(ybb, wb_ref[...], preferred_element_type=f32)
        g0 = _sigmoid(m0_ref[...] + bm_ref[0:1, :])
        g1 = _sigmoid(m1_ref[...] + bm_ref[1:2, :])
        merged = g0 * br_a + g1 * br_b
        mgb = merged.astype(bf16)
        y = x_ref[...] + jnp.dot(mgb, wo_ref[...], preferred_element_type=f32)
        err = y - t_ref[...]
        part = jnp.sum(jnp.sum(err * err, axis=1, keepdims=True), axis=0, keepdims=True)
        loss_ref[...] += part * (0.5 / D)
        dy = err * (1.0 / D)
        dyb = dy.astype(bf16)
        dmerged = lax.dot_general(dyb, wo_ref[...], NT_DIMS, preferred_element_type=f32)
        dbr_a = (dmerged * g0).astype(bf16)
        dbr_b = (dmerged * g1).astype(bf16)
        dm0 = dmerged * br_a * (g0 * (1.0 - g0))
        dm1 = dmerged * br_b * (g1 * (1.0 - g1))
        dbm_ref[0:1, :] += jnp.sum(dm0, axis=0, keepdims=True)
        dbm_ref[1:2, :] += jnp.sum(dm1, axis=0, keepdims=True)
        dya = lax.dot_general(dbr_a, wa_ref[...], NT_DIMS, preferred_element_type=f32)
        dyb2 = lax.dot_general(dbr_b, wb_ref[...], NT_DIMS, preferred_element_type=f32)
        do_a = dya * silu_a
        dga = dya * oa * (sa * (1.0 + ga * (1.0 - sa)))
        ones = _head_ones()
        delta_a = _seg_sum(do_a * oa, ones)
        dsk_ref[...] -= jnp.sum(delta_a * jnp.exp(sk_ref[...] - la_ref[...]), axis=0, keepdims=True)
        dybc = dyb2 * silu_b
        dgb = dyb2 * ybc * (sb * (1.0 + gb * (1.0 - sb)))
        dbar = _seg_sum(dybc * ybc, ones)
        dy_ref[...] = dy
        dyb_ref[...] = dyb
        dt_ref[:, 0:512] = dga.astype(bf16)
        dt_ref[:, 512:1024] = dgb.astype(bf16)
        dt_ref[:, 1024:2048] = dm0.astype(bf16)
        dt_ref[:, 2048:3072] = dm1.astype(bf16)
        doa_ref[...] = do_a.astype(bf16)
        dda_ref[...] = delta_a
        for k, (dob_ref, ddb_ref) in enumerate(((dob0_ref, ddb0_ref), (dob1_ref, ddb1_ref), (dob2_ref, ddb2_ref))):
            dob_ref[...] = alpha[k] * dybc
            ddb_ref[...] = alpha[k] * dbar
        ya_ref[...] = ya.T.astype(bf16)
        yb_ref[...] = yb.T.astype(bf16)
        mg_ref[...] = merged.T.astype(bf16)
        dbra_ref[...] = dbr_a
        dbrb_ref[...] = dbr_b

    def rows(n, blk=0):
        return pl.BlockSpec((ts, n), lambda i: (i, blk))

    def whole(r, c):
        return pl.BlockSpec((r, c), lambda i: (0, 0))

    def cols(n):
        return pl.BlockSpec((n, ts), lambda i: (0, i))

    def gate_cols(n, col):
        return pl.BlockSpec((pl.Element(ts), pl.Element(n)), lambda i: (i * ts, NA + col))

    outs = [
        ((S, D), f32, rows(D)), ((S, D), bf16, rows(D)), ((S, NW), bf16, gate_cols(NT, 0)),
        ((S, 512), bf16, rows(512)), ((S, 512), f32, rows(512)),
        ((S, 512), f32, rows(512)), ((S, 512), f32, rows(512)), ((S, 512), f32, rows(512)),
        ((S, 512), f32, rows(512)), ((S, 512), f32, rows(512)), ((S, 512), f32, rows(512)),
        ((512, S), bf16, cols(512)), ((512, S), bf16, cols(512)), ((D, S), bf16, cols(D)),
        ((S, D), bf16, rows(D)), ((S, D), bf16, rows(D)),
        ((1, 1), f32, whole(1, 1)), ((2, D), f32, whole(2, D)), ((1, 512), f32, whole(1, 512)),
    ]
    return pl.pallas_call(
        body,
        grid=(S // ts,),
        in_specs=[
            rows(D), rows(D), rows(512), rows(512), rows(512), rows(512), rows(512), rows(512), rows(512), rows(512),
            gate_cols(512, 0), gate_cols(512, 512), gate_cols(D, 1024), gate_cols(D, 2048), whole(2, D),
            whole(512, D), whole(512, D), whole(D, D), whole(1, 512),
        ],
        out_specs=[o[2] for o in outs],
        out_shape=[jax.ShapeDtypeStruct(o[0], o[1]) for o in outs],
        compiler_params=_params(("arbitrary",), vmem_mib=60),
        name="tail",
    )(x, tgt, o_a, l_a, *o_b, *l_b, proj, proj, proj, proj, bm, w_a, w_b, w_o, sink_b)


def _norm_bwd(xv, dyv, gain, kind, ones):
    r = lax.rsqrt(_half_sums(xv * xv, ones) * (1.0 / HD) + EPS)
    yv = xv * r
    up = jnp.where(kind == 0, dyv * SCALE, dyv)
    u = up * gain
    dxv = r * (u - yv * (_half_sums(u * yv, ones) * (1.0 / HD)))
    dxv = jnp.where(kind == 2, dyv, dxv)
    dg = jnp.where(kind == 2, 0.0, jnp.sum(up * yv, axis=0, keepdims=True))
    return dxv, dg


def _post_b(g, dqkv, proj_a, gains, dproj):
    d = GROUPS[g][1]
    seq = S // d

    def body(d_ref, p_ref, g_ref, alias_ref, o_ref, dg_ref, nat):
        del alias_ref
        j = pl.program_id(0)
        kind = j // 4
        gain = g_ref[...]
        ones = _head_ones()

        @pl.when(j % 4 == 0)
        def _():
            dg_ref[...] = jnp.zeros_like(dg_ref)

        for c in range(d):
            for i in range(seq // PCHUNK):
                src = c * seq + i * PCHUNK
                if d == 1:
                    idx = slice(src, src + PCHUNK)
                else:
                    idx = pl.ds(c + i * PCHUNK * d, PCHUNK, stride=d)
                dyv = d_ref[PAD + src:PAD + src + PCHUNK, :].astype(f32)
                dxv, dg = _norm_bwd(p_ref[idx, :], dyv, gain, kind, ones)
                nat[idx, :] = dxv
                dg_ref[...] += dg

        for i in range(S // CHUNK):
            o_ref[i * CHUNK:(i + 1) * CHUNK, :] = nat[i * CHUNK:(i + 1) * CHUNK, :].astype(bf16)

    return pl.pallas_call(
        body,
        grid=(12,),
        in_specs=[
            pl.BlockSpec((None, None, SP, LANES), lambda j: (j // 4, j % 4, 0, 0)),
            pl.BlockSpec((S, LANES), lambda j: (0, _col_block(g, jnp.minimum(j, 7)))),
            pl.BlockSpec((None, None, 1, LANES), lambda j: (g, j // 4, 0, 0)),
            pl.BlockSpec(memory_space=pl.ANY),
        ],
        out_specs=[
            pl.BlockSpec((S, LANES), lambda j: (0, _col_block(g, j))),
            pl.BlockSpec((None, 1, LANES), lambda j: (j // 4, 0, 0)),
        ],
        out_shape=[jax.ShapeDtypeStruct((S, NW), bf16), jax.ShapeDtypeStruct((3, 1, LANES), f32)],
        scratch_shapes=[pltpu.VMEM((S, LANES), f32)],
        input_output_aliases={3: 0},
        compiler_params=_params(("arbitrary",)),
        name="post_b%d" % g,
    )(dqkv, proj_a, gains, dproj)


def _post_a(dqkv, proj_a, gains, dproj):
    def body(q_ref, e_ref, p_ref, g_ref, alias_ref, o_ref, dg_ref):
        del alias_ref
        j = pl.program_id(0)
        kind = jnp.maximum(j - 3, 0)
        gain = g_ref[...]
        lo = _lo()
        ones = _head_ones()

        @pl.when((j == 0) | (j >= 4))
        def _():
            dg_ref[...] = jnp.zeros_like(dg_ref)

        for i in range(S // PCHUNK):
            r0 = i * PCHUNK
            rows = slice(PAD + r0, PAD + r0 + PCHUNK)
            t0 = e_ref[0, rows, :].astype(f32) + e_ref[1, rows, :].astype(f32)
            t1 = e_ref[2, rows, :].astype(f32) + e_ref[3, rows, :].astype(f32)
            folded = jnp.where(lo, t0 + pltpu.roll(t0, HD, 1), t1 + pltpu.roll(t1, HD, 1))
            dyv = jnp.where(kind == 0, q_ref[rows, :].astype(f32), folded)
            dxv, dg = _norm_bwd(p_ref[r0:r0 + PCHUNK, :], dyv, gain, kind, ones)
            o_ref[r0:r0 + PCHUNK, :] = dxv.astype(bf16)
            dg_ref[...] += dg

    return pl.pallas_call(
        body,
        grid=(6,),
        in_specs=[
            pl.BlockSpec((None, None, SP, LANES), lambda j: (0, jnp.minimum(j, 3), 0, 0)),
            pl.BlockSpec((None, 4, SP, LANES), lambda j: (jnp.clip(j - 3, 1, 2), 0, 0, 0)),
            pl.BlockSpec((S, LANES), lambda j: (0, jnp.minimum(j, 4))),
            pl.BlockSpec((None, None, 1, LANES), lambda j: (0, jnp.maximum(j - 3, 0), 0, 0)),
            pl.BlockSpec(memory_space=pl.ANY),
        ],
        out_specs=[
            pl.BlockSpec((S, LANES), lambda j: (0, j)),
            pl.BlockSpec((None, 1, LANES), lambda j: (jnp.maximum(j - 3, 0), 0, 0)),
        ],
        out_shape=[jax.ShapeDtypeStruct((S, NW), bf16), jax.ShapeDtypeStruct((3, 1, LANES), f32)],
        input_output_aliases={4: 0},
        compiler_params=_params(("arbitrary",)),
        name="post_a",
    )(dqkv, dqkv, proj_a, gains, dproj)


def _dh_norm_bwd(dproj, w, x, rstd, gain, dy):
    ts = 1024
    tk = NW // 6
    nk = NW // tk

    def body(d_ref, w_ref, x_ref, r_ref, g_ref, dy_ref, gx_ref, dgn_ref, acc):
        i = pl.program_id(0)
        k = pl.program_id(1)

        @pl.when((i == 0) & (k == 0))
        def _():
            dgn_ref[...] = jnp.zeros_like(dgn_ref)

        @pl.when(k == 0)
        def _():
            acc[...] = jnp.zeros_like(acc)

        acc[...] += jnp.dot(d_ref[...], w_ref[...], preferred_element_type=f32)

        @pl.when(k == nk - 1)
        def _():
            dh = acc[...]
            xh = x_ref[...] * r_ref[...]
            u = dh * g_ref[...]
            dx = r_ref[...] * (u - xh * jnp.mean(u * xh, axis=-1, keepdims=True))
            gx_ref[...] = dy_ref[...] + dx
            dgn_ref[...] += jnp.sum(dh * xh, axis=0, keepdims=True)

    return pl.pallas_call(
        body,
        grid=(S // ts, nk),
        in_specs=[
            pl.BlockSpec((ts, tk), lambda i, k: (i, k)),
            pl.BlockSpec((tk, D), lambda i, k: (k, 0)),
            pl.BlockSpec((ts, D), lambda i, k: (i, 0)),
            pl.BlockSpec((ts, 1), lambda i, k: (i, 0)),
            pl.BlockSpec((1, D), lambda i, k: (0, 0)),
            pl.BlockSpec((ts, D), lambda i, k: (i, 0)),
        ],
        out_specs=[pl.BlockSpec((ts, D), lambda i, k: (i, 0)), pl.BlockSpec((1, D), lambda i, k: (0, 0))],
        out_shape=[jax.ShapeDtypeStruct((S, D), f32), jax.ShapeDtypeStruct((1, D), f32)],
        scratch_shapes=[pltpu.VMEM((ts, D), f32)],
        compiler_params=_params(("arbitrary", "arbitrary"), vmem_mib=56),
        name="dh_norm_bwd",
    )(dproj, w, x, rstd, gain, dy)


def _dw_in(hbt, dproj, parity, name):
    tk = 1024
    win = WSH + 96

    def body(par_ref, a_ref, b_ref, o_ref, acc):
        p = 2 * pl.program_id(0) + par_ref[0]
        k = pl.program_id(1)

        @pl.when(k == 0)
        def _():
            acc[...] = jnp.zeros_like(acc)

        acc[...] += jnp.dot(a_ref[...], b_ref[...], preferred_element_type=f32)

        @pl.when(k == S // tk - 1)
        def _():
            acc_t = acc[...].T
            for pp in range(NDEV):
                off = (WSH * pp) % LANES

                @pl.when(p == pp)
                def _():
                    o_ref[...] = acc_t[off:off + WSH, :].astype(bf16)

    return pl.pallas_call(
        body,
        grid_spec=pltpu.PrefetchScalarGridSpec(
            num_scalar_prefetch=1,
            grid=(NDEV // 2, S // tk),
            in_specs=[
                pl.BlockSpec((D, tk), lambda q, k, par: (0, k)),
                pl.BlockSpec((pl.Element(tk), pl.Element(win)),
                             lambda q, k, par: (k * tk, (WSH * (2 * q + par[0])) // LANES * LANES)),
            ],
            out_specs=pl.BlockSpec((None, WSH, D), lambda q, k, par: (q, 0, 0)),
            scratch_shapes=[pltpu.VMEM((D, win), f32)],
        ),
        out_shape=jax.ShapeDtypeStruct((NDEV // 2, WSH, D), bf16),
        compiler_params=_params(("arbitrary", "arbitrary")),
        name=name,
    )(parity, hbt, dproj)


def _matmul_tokens(at, b, name):
    m, n = at.shape[0], b.shape[1]
    tn = 512
    tk = 1024

    def body(a_ref, b_ref, o_ref):
        @pl.when(pl.program_id(1) == 0)
        def _():
            o_ref[...] = jnp.zeros_like(o_ref)

        o_ref[...] += jnp.dot(a_ref[...], b_ref[...], preferred_element_type=f32)

    return pl.pallas_call(
        body,
        grid=(n // tn, S // tk),
        in_specs=[pl.BlockSpec((m, tk), lambda j, k: (0, k)), pl.BlockSpec((tk, tn), lambda j, k: (k, j))],
        out_specs=pl.BlockSpec((m, tn), lambda j, k: (0, j)),
        out_shape=jax.ShapeDtypeStruct((m, n), f32),
        compiler_params=_params(("arbitrary", "arbitrary")),
        name=name,
    )(at, b)


def _exchange(scatter, gather, name):
    arrs = list(scatter) + list(gather)
    n = len(arrs)
    ns = len(scatter)

    def body(*refs):
        ins, outs = refs[:n], refs[n:2 * n]
        send_sems, recv_sems, local_sems = refs[2 * n:]
        x, y, c = lax.axis_index("x"), lax.axis_index("y"), lax.axis_index("c")
        me = 4 * x + 2 * y + c
        local, remote = [], []
        for a in range(n):
            lc = pltpu.make_async_copy(ins[a].at[me] if a < ns else ins[a], outs[a].at[me], local_sems.at[a])
            lc.start()
            local.append(lc)
            for r in range(1, NDEV):
                px = 1 - x if r & 4 else x
                py = 1 - y if r & 2 else y
                pc = 1 - c if r & 1 else c
                cp = pltpu.make_async_remote_copy(
                    src_ref=ins[a].at[4 * px + 2 * py + pc] if a < ns else ins[a],
                    dst_ref=outs[a].at[me],
                    send_sem=send_sems.at[a, r - 1],
                    recv_sem=recv_sems.at[a, r - 1],
                    device_id=(px, py, pc),
                    device_id_type=pl.DeviceIdType.MESH,
                )
                cp.start()
                remote.append(cp)
        for cp in remote:
            cp.wait_recv()
        for cp in remote:
            cp.wait_send()
        for lc in local:
            lc.wait()

    out_shape = [jax.ShapeDtypeStruct(a.shape if i < ns else (NDEV,) + a.shape, a.dtype) for i, a in enumerate(arrs)]
    return pl.pallas_call(
        body,
        in_specs=[pl.BlockSpec(memory_space=pl.ANY)] * n,
        out_specs=[pl.BlockSpec(memory_space=pl.ANY)] * n,
        out_shape=out_shape,
        scratch_shapes=[
            pltpu.SemaphoreType.DMA((n, NDEV - 1)),
            pltpu.SemaphoreType.DMA((n, NDEV - 1)),
            pltpu.SemaphoreType.DMA((n,)),
        ],
        compiler_params=pltpu.CompilerParams(has_side_effects=True),
        name=name,
    )(*arrs)


_HBM = pl.BlockSpec(memory_space=pltpu.HBM)
_SEM = pl.BlockSpec(memory_space=pltpu.SEMAPHORE)
_EFFECT = pltpu.SideEffectType.DATAFLOW_SIDE_EFFECTING


def _comm_step(name, body_fn, lands, srcs=(), wait_sems=(), n_new=0, after=(), token=False):
    n, ns, nw, na = len(lands), len(srcs), len(wait_sems), len(after)

    def body(*refs):
        src, land = refs[:ns], refs[ns:ns + n]
        waits = refs[ns + n:ns + n + nw]
        new = refs[ns + n + nw + na:ns + n + nw + na + n_new]
        body_fn(src, land, waits, new)
        if token:
            refs[-1][...] = jnp.zeros((8, LANES), f32)

    hbm = [pltpu.HBM(a.shape, a.dtype) for a in lands]
    ops = [pltpu.with_memory_space_constraint(a, pltpu.HBM) for a in list(srcs) + list(lands)]
    extra_shape = [jax.ShapeDtypeStruct((8, LANES), f32)] if token else []
    extra_spec = [pl.BlockSpec(memory_space=pltpu.VMEM)] if token else []
    outs = pl.pallas_call(
        body,
        out_shape=tuple([pltpu.SemaphoreType.DMA(())] * n_new + hbm + extra_shape),
        in_specs=[_HBM] * (ns + n) + [_SEM] * nw + [pl.BlockSpec(memory_space=pl.ANY)] * na,
        out_specs=tuple([_SEM] * n_new + [_HBM] * n + extra_spec),
        input_output_aliases={ns + i: n_new + i for i in range(n)},
        compiler_params=pltpu.CompilerParams(has_side_effects=_EFFECT),
        name=name,
    )(*ops, *wait_sems, *after)
    if token:
        return list(outs[:n_new]), list(outs[n_new:n_new + n]), outs[-1][0, 0]
    return list(outs[:n_new]), list(outs[n_new:])


class _GatheredWeights:
    def __init__(self, shards):
        self.n = n = len(shards)
        x, y, c = lax.axis_index("x"), lax.axis_index("y"), lax.axis_index("c")
        self.x = x
        me = 4 * x + 2 * y + c
        lands = [lax.dynamic_update_slice(lax.empty((NDEV,) + s.shape, s.dtype), s[None], (me,) + (0,) * s.ndim)
                 for s in shards]

        def start_own(src, land, waits, new):
            p = self._peers()
            for a in range(n):
                for k, to in ((0, p["sibling"]), (1, p["xn"]), (2, p["yn"])):
                    self._copy(land[a], new, a, k, 3, p["me"], to).start()

        self.sems, self.lands = {}, None
        new, self.lands = _comm_step("gather_start", start_own, lands, n_new=6 * n)
        self._keep(new, (0, 1, 2))

    @staticmethod
    def _peers():
        x, y, c = lax.axis_index("x"), lax.axis_index("y"), lax.axis_index("c")
        return dict(
            me=(x, y, c), sibling=(x, y, 1 - c), xn=(1 - x, y, c), yn=(x, 1 - y, c), dg=(1 - x, 1 - y, c),
            relay_origin=(jnp.bitwise_xor(x, c), jnp.bitwise_xor(y, 1 - c), c),
            relay_target=(jnp.bitwise_xor(x, 1 - c), jnp.bitwise_xor(y, c), c))

    def _keep(self, new, ks):
        half = len(new) // 2
        i = 0
        for a in range(self.n):
            for k in ks:
                self.sems[a, k] = (new[i], new[half + i])
                i += 1

    @staticmethod
    def _copy(land, sem_refs, a, k, nk, block, to, src=None, ks=None):
        ks = tuple(range(nk)) if ks is None else ks
        half = len(sem_refs) // 2
        i = a * len(ks) + ks.index(k)
        slot = land.at[4 * block[0] + 2 * block[1] + block[2]]
        return pltpu.make_async_remote_copy(
            src_ref=slot if src is None else src, dst_ref=slot, send_sem=sem_refs[i], recv_sem=sem_refs[half + i],
            device_id=to, device_id_type=pl.DeviceIdType.MESH)

    def _sem_list(self, ks):
        return ([self.sems[a, k][0] for a in range(self.n) for k in ks]
                + [self.sems[a, k][1] for a in range(self.n) for k in ks])

    def first_half(self, after):
        n = self.n

        def relay(src, land, waits, new):
            p = self._peers()
            for a in range(n):
                self._copy(land[a], waits, a, 1, 0, p["xn"], p["me"], ks=(1, 2)).wait_recv()
                self._copy(land[a], waits, a, 2, 0, p["yn"], p["me"], ks=(1, 2)).wait_recv()
                self._copy(land[a], new, a, 3, 0, p["relay_origin"], p["relay_target"], ks=(3, 4, 5)).start()
                self._copy(land[a], new, a, 4, 0, p["xn"], p["sibling"], ks=(3, 4, 5)).start()
                self._copy(land[a], new, a, 5, 0, p["yn"], p["sibling"], ks=(3, 4, 5)).start()

        new, self.lands = _comm_step("gather_relay", relay, self.lands, wait_sems=self._sem_list((1, 2)),
                                     n_new=6 * n, after=after)
        self._keep(new, (3, 4, 5))

        def from_sibling(src, land, waits, new):
            p = self._peers()
            other = lambda b: (b[0], b[1], 1 - b[2])
            for a in range(n):
                self._copy(land[a], waits, a, 0, 0, other(p["me"]), p["me"], ks=(0, 4, 5)).wait_recv()
                self._copy(land[a], waits, a, 4, 0, other(p["xn"]), p["me"], ks=(0, 4, 5)).wait_recv()
                self._copy(land[a], waits, a, 5, 0, other(p["yn"]), p["me"], ks=(0, 4, 5)).wait_recv()

        _, self.lands = _comm_step("gather_wait_sibling", from_sibling, self.lands,
                                   wait_sems=self._sem_list((0, 4, 5)))
        return self.lands[0].reshape(NW, D), self.x.astype(jnp.int32).reshape(1)

    def second_half(self, after):
        n = self.n

        def forward_diagonal(src, land, waits, new):
            p = self._peers()
            for a in range(n):
                self._copy(land[a], waits, a, 3, 0, p["dg"], p["me"], ks=(3,)).wait_recv()
                self._copy(land[a], new, a, 6, 0, p["dg"], p["sibling"], ks=(6,)).start()

        new, self.lands = _comm_step("gather_forward_diagonal", forward_diagonal, self.lands,
                                     wait_sems=self._sem_list((3,)), n_new=2 * n, after=after)
        self._keep(new, (6,))

        def finish(src, land, waits, new):
            p = self._peers()
            ks = tuple(range(7))
            for a in range(n):
                self._copy(land[a], waits, a, 6, 0, (p["dg"][0], p["dg"][1], 1 - p["dg"][2]), p["me"], ks=ks).wait_recv()
                for k in ks:
                    self._copy(land[a], waits, a, k, 0, p["me"], p["me"], ks=ks).wait_send()

        _, self.lands = _comm_step("gather_finish", finish, self.lands, wait_sems=self._sem_list(tuple(range(7))))
        return self.lands[0].reshape(NW, D), (1 - self.x).astype(jnp.int32).reshape(1)

    def rest(self):
        g_a, g_b, g_o, g_bm = self.lands[1:]
        return (g_a.transpose(1, 0, 2).reshape(512, D), g_b.transpose(1, 0, 2).reshape(512, D),
                g_bm.transpose(1, 0, 2).reshape(2, D), g_o.reshape(D, D))


def _sibling_send_start(shares):
    landing = lax.empty(shares.shape, shares.dtype)

    def start(src, land, waits, new):
        x, y, c = lax.axis_index("x"), lax.axis_index("y"), lax.axis_index("c")
        pltpu.make_async_remote_copy(src_ref=land[0], dst_ref=land[1], send_sem=new[0], recv_sem=new[1],
                                     device_id=(x, y, 1 - c), device_id_type=pl.DeviceIdType.MESH).start()

    return _comm_step("grad_sibling_start", start, [shares, landing], n_new=2, token=True)


def _sibling_send_wait(sems, lands, after):
    def wait(src, land, waits, new):
        x, y, c = lax.axis_index("x"), lax.axis_index("y"), lax.axis_index("c")
        done = pltpu.make_async_remote_copy(src_ref=land[0], dst_ref=land[1], send_sem=waits[0], recv_sem=waits[1],
                                            device_id=(x, y, c), device_id_type=pl.DeviceIdType.MESH)
        done.wait_send()
        done.wait_recv()

    _, lands = _comm_step("grad_sibling_wait", wait, lands, wait_sems=sems, after=after)
    return lands[1]


def _row_tile(rows, limit=256):
    fits = [t for t in range(16, limit + 1, 16) if rows % t == 0]
    return fits[-1] if fits else rows


def _pair_sum(mine, theirs, name):
    nb, rows, cols = mine.shape
    tr = _row_tile(rows, 528)

    def body(a_ref, b_ref, o_ref):
        o_ref[...] = (a_ref[...].astype(f32) + b_ref[...].astype(f32)).astype(bf16)

    blk = pl.BlockSpec((None, tr, cols), lambda q, i: (q, i, 0))
    return pl.pallas_call(
        body,
        grid=(nb, rows // tr),
        in_specs=[blk, blk],
        out_specs=blk,
        out_shape=jax.ShapeDtypeStruct(mine.shape, bf16),
        compiler_params=_params(("arbitrary", "arbitrary")),
        name=name,
    )(mine, theirs)


def _scatter_start(chip_arrs, all_arrs, name):
    arrs = list(chip_arrs) + list(all_arrs)
    n, nc = len(arrs), len(chip_arrs)
    lands = [lax.empty(((3 if i < nc else NDEV - 1),) + a.shape[1:], a.dtype) for i, a in enumerate(arrs)]

    def body(*refs):
        src, land = refs[:n], refs[n:2 * n]
        send_sems, recv_sems = refs[2 * n:3 * n], refs[3 * n:4 * n]
        token = refs[6 * n]
        x, y, c = lax.axis_index("x"), lax.axis_index("y"), lax.axis_index("c")
        for a in range(n):
            for r in range(1, 4 if a < nc else NDEV):
                if a < nc:
                    px, py, pc = (1 - x if r & 2 else x), (1 - y if r & 1 else y), c
                    block = 2 * px + py
                else:
                    px, py, pc = (1 - x if r & 4 else x), (1 - y if r & 2 else y), (1 - c if r & 1 else c)
                    block = 4 * px + 2 * py + pc
                pltpu.make_async_remote_copy(
                    src_ref=src[a].at[block], dst_ref=land[a].at[r - 1], send_sem=send_sems[a],
                    recv_sem=recv_sems[a], device_id=(px, py, pc), device_id_type=pl.DeviceIdType.MESH).start()
        token[...] = jnp.zeros_like(token)

    hbm = [pltpu.HBM(a.shape, a.dtype) for a in arrs + lands]
    ops = [pltpu.with_memory_space_constraint(a, pltpu.HBM) for a in arrs + lands]
    outs = pl.pallas_call(
        body,
        out_shape=tuple([pltpu.SemaphoreType.DMA(())] * (2 * n) + hbm + [jax.ShapeDtypeStruct((8, LANES), f32)]),
        in_specs=[_HBM] * (2 * n),
        out_specs=tuple([_SEM] * (2 * n) + [_HBM] * (2 * n) + [pl.BlockSpec(memory_space=pltpu.VMEM)]),
        input_output_aliases={i: 2 * n + i for i in range(2 * n)},
        compiler_params=pltpu.CompilerParams(has_side_effects=_EFFECT),
        name=name,
    )(*ops)
    return outs[:n], outs[n:2 * n], outs[2 * n:3 * n], outs[3 * n:4 * n], outs[4 * n]


def _scatter_wait(send_sems, recv_sems, srcs, lands, after, name):
    n = len(srcs)

    def body(*refs):
        land = refs[n:2 * n]
        ssem, rsem = refs[2 * n:3 * n], refs[3 * n:4 * n]
        x, y, c = lax.axis_index("x"), lax.axis_index("y"), lax.axis_index("c")
        for a in range(n):
            done = pltpu.make_async_remote_copy(
                src_ref=land[a], dst_ref=land[a], send_sem=ssem[a], recv_sem=rsem[a], device_id=(x, y, c),
                device_id_type=pl.DeviceIdType.MESH)
            done.wait_send()
            done.wait_recv()

    hbm = [pltpu.HBM(a.shape, a.dtype) for a in list(srcs) + list(lands)]
    outs = pl.pallas_call(
        body,
        out_shape=tuple(hbm),
        in_specs=[_HBM] * (2 * n) + [_SEM] * (2 * n) + [pl.BlockSpec(memory_space=pl.ANY)],
        out_specs=tuple([_HBM] * (2 * n)),
        input_output_aliases={i: i for i in range(2 * n)},
        compiler_params=pltpu.CompilerParams(has_side_effects=_EFFECT),
        name=name,
    )(*srcs, *lands, *send_sems, *recv_sems, after)
    return outs[:n], outs[n:]


def _adam_update(g, w_ref, m_ref, v_ref, g_ref, d_ref, nm_ref, nv_ref):
    mm = ADAM_B1 * m_ref[...] + (1.0 - ADAM_B1) * g
    vv = ADAM_B2 * v_ref[...] + (1.0 - ADAM_B2) * (g * g)
    m_hat = mm / (1.0 - ADAM_B1 ** ADAM_STEP)
    v_hat = vv / (1.0 - ADAM_B2 ** ADAM_STEP)
    g_ref[...] = g
    d_ref[...] = -ADAM_LR * (m_hat / (jnp.sqrt(v_hat) + ADAM_EPS) + ADAM_WD * w_ref[...])
    nm_ref[...] = mm
    nv_ref[...] = vv


def _adamw_own(w, own, own_idx, slots, m, v, name):
    r, c = w.shape[-2:]
    tr = _row_tile(r, 384)
    k = slots.shape[0]

    def body(i_ref, w_ref, o_ref, s_ref, m_ref, v_ref, g_ref, d_ref, nm_ref, nv_ref):
        del i_ref
        g = o_ref[...].astype(f32)
        for j in range(k):
            g = g + s_ref[j].astype(f32)
        _adam_update(g, w_ref, m_ref, v_ref, g_ref, d_ref, nm_ref, nv_ref)

    blk = pl.BlockSpec((None, tr, c), lambda i, ix: (0, i, 0))
    return pl.pallas_call(
        body,
        grid_spec=pltpu.PrefetchScalarGridSpec(
            num_scalar_prefetch=1,
            grid=(r // tr,),
            in_specs=[blk, pl.BlockSpec((None, tr, c), lambda i, ix: (ix[0], i, 0)),
                      pl.BlockSpec((k, tr, c), lambda i, ix: (0, i, 0)), blk, blk],
            out_specs=[blk] * 4,
        ),
        out_shape=[jax.ShapeDtypeStruct(w.shape, f32)] * 4,
        compiler_params=_params(("arbitrary",)),
        name=name,
    )(own_idx, w, own, slots, m, v)


def _adamw(w, slots, m, v, name):
    r, c = w.shape[-2:]
    tr = _row_tile(r, 128)

    def body(w_ref, s_ref, m_ref, v_ref, g_ref, d_ref, nm_ref, nv_ref):
        g = s_ref[0].astype(f32)
        for k in range(1, NDEV):
            g = g + s_ref[k].astype(f32)
        _adam_update(g, w_ref, m_ref, v_ref, g_ref, d_ref, nm_ref, nv_ref)

    if w.ndim == 3:
        blk = pl.BlockSpec((None, tr, c), lambda i: (0, i, 0))
    else:
        blk = pl.BlockSpec((tr, c), lambda i: (i, 0))
    return pl.pallas_call(
        body,
        grid=(r // tr,),
        in_specs=[blk, pl.BlockSpec((NDEV, tr, c), lambda i: (0, i, 0)), blk, blk],
        out_specs=[blk] * 4,
        out_shape=[jax.ShapeDtypeStruct(w.shape, f32)] * 4,
        compiler_params=_params(("arbitrary",)),
        name=name,
    )(w, slots, m, v)


class _Weights:
    def __init__(self, w_t, w_a, w_b, b_merge, w_o):
        self._w_t, self._rest = w_t, (w_a, w_b, b_merge, w_o)

    def first_half(self, after):
        del after
        return self._w_t, jnp.zeros((1,), jnp.int32)

    def second_half(self, after):
        del after
        return self._w_t, jnp.ones((1,), jnp.int32)

    def rest(self):
        return self._rest


def _local_step(x, tgt, norm_gain, weights, qn_a, kn_a, qn_b, kn_b, sink_a, rel_bias, on_weight_grads=None,
                core=None):
    two = lambda t: jnp.concatenate([t, t], axis=-1).reshape(1, LANES)
    ones = jnp.ones((1, LANES), f32)
    gains = jnp.stack([
        jnp.stack([two(qn_a), two(kn_a), ones]),
        jnp.stack([two(qn_b), two(kn_b), ones]),
        jnp.stack([two(qn_b), two(kn_b), ones]),
        jnp.stack([two(qn_b), two(kn_b), ones]),
    ])
    buckets = [jnp.asarray(_bucket_np(blk, d)) for blk, d, _ in GROUPS]
    bias = [_bias_expand(rel_bias, buckets[k], GROUPS[k][2], "bias_expand_%d" % k) for k in range(4)]

    hb, hbt, rstd = _rms(x, norm_gain)
    w_t, half = weights.first_half([hb] + bias)
    proj = _inproj_half(hb, w_t, half, None, "inproj_1")
    w_t, half = weights.second_half([proj])
    proj = _inproj_half(hb, w_t, half, proj, "inproj_2")
    w_a, w_b, b_merge, w_o = weights.rest()
    gl = _prep(proj, gains)
    o_a, l_a = _attn_fwd(gl, bias[0], sink_a.reshape(8), 0, 128, 1, "attn_fwd_a")
    fwd_b = [_attn_fwd(gl, bias[k], None, k, GROUPS[k][0], GROUPS[k][1], "attn_fwd_b%d" % k) for k in (1, 2, 3)]
    sink_b = jnp.repeat(sink_a.reshape(8), HD).reshape(1, 512)

    (dy, dyb, dproj, do_a, dd_a, do_b0, do_b1, do_b2, dd_b0, dd_b1, dd_b2, ya, yb, mg, dbr_a, dbr_b, loss, dbm,
     dsk) = _tail(x, tgt, o_a, l_a, [f[0] for f in fwd_b], [f[1] for f in fwd_b], proj, b_merge, w_a, w_b, w_o, sink_b)

    dw_o = _matmul_tokens(mg, dyb, "dw_out")
    dw_a = _matmul_tokens(ya, dbr_a, "dw_branch_a")
    dw_b = _matmul_tokens(yb, dbr_b, "dw_branch_b")
    if on_weight_grads is not None:
        early = on_weight_grads(dict(w_branch_a=dw_a, w_branch_b=dw_b, b_merge=dbm, w_out=dw_o))
        buckets = [buckets[0] + early.astype(jnp.int32)] + buckets[1:]

    dqkv_a, dbk_a = _attn_bwd(gl, bias[0], buckets[0], do_a, l_a, dd_a, 0, 128, 1, "attn_bwd_a")
    dproj, dg_a = _post_a(dqkv_a, proj, gains, dproj)
    dbk_b, dg_b = [], []
    for k, do_k, dd_k in ((1, do_b0, dd_b0), (2, do_b1, dd_b1), (3, do_b2, dd_b2)):
        dqkv, dbk = _attn_bwd(gl, bias[k], buckets[k], do_k, fwd_b[k - 1][1], dd_k, k, GROUPS[k][0], GROUPS[k][1],
                              "attn_bwd_b%d" % k)
        dproj, dg = _post_b(k, dqkv, proj, gains, dproj)
        dbk_b.append(dbk)
        dg_b.append(dg)
    dg_b = jnp.stack(dg_b)

    core = jnp.zeros((1,), jnp.int32) if core is None else core
    dw_other = _dw_in(hbt, dproj, 1 - core, "dw_in_other")
    sent = jnp.zeros((), f32) if on_weight_grads is None else on_weight_grads(dict(w_in_other=dw_other))
    dw_in = _dw_in(hbt, dproj, core + sent.astype(jnp.int32), "dw_in_own")
    token = jnp.zeros((), f32) if on_weight_grads is None else on_weight_grads(dict(w_in=dw_in))
    grad_x, d_norm_gain = _dh_norm_bwd(dproj, w_t, x, rstd, norm_gain + token, dy)

    fold = lambda t: t[..., :HD] + t[..., HD:]
    d_qn_a = fold(dg_a[0, 0])
    d_kn_a = fold(dg_a[1, 0])
    d_qn_b = fold(dg_b[:, 0, 0].sum(axis=0))
    d_kn_b = fold(dg_b[:, 1, 0].sum(axis=0))
    d_sink = dsk.reshape(8, HD)[:, 0]
    red = jnp.stack([dbk_a] + dbk_b)
    d_rel = red[:, :, 0, :32].reshape(32, 32).T
    return dict(loss=loss, grad_x=grad_x, norm_gain=d_norm_gain, w_in=dw_in, w_in_other=dw_other, q_norm_a=d_qn_a,
                k_norm_a=d_kn_a,
                q_norm_b=d_qn_b, k_norm_b=d_kn_b, sink_a=d_sink, rel_bias=d_rel, w_branch_a=dw_a, w_branch_b=dw_b,
                b_merge=dbm, w_out=dw_o)


SMALL = (("norm_gain", D), ("q_norm_a", HD), ("k_norm_a", HD), ("q_norm_b", HD), ("k_norm_b", HD), ("sink_a", 8),
         ("rel_bias", 1024))
SMALL_PAD = 2432


SMALL_USED = sum(sz for _, sz in SMALL)


def _pack_small(parts, loss=None):
    tail = jnp.zeros((SMALL_PAD - SMALL_USED,), f32)
    if loss is not None:
        tail = tail.at[0].set(loss.reshape(()))
    return jnp.concatenate([parts[n].reshape(-1) for n, _ in SMALL] + [tail]).reshape(1, SMALL_PAD)


def _unpack_small(flat, shapes):
    out, off = {}, 0
    for n, sz in SMALL:
        out[n] = flat[0, off:off + sz].reshape(shapes[n])
        off += sz
    return out


def kernel(x, norm_gain, w_in, q_norm_a, k_norm_a, q_norm_b, k_norm_b, sink_a, rel_bias, w_branch_a, w_branch_b, b_merge, w_out, loss_target, m_norm_gain, m_w_in, m_q_norm_a, m_k_norm_a, m_q_norm_b, m_k_norm_b, m_sink_a, m_rel_bias, m_w_branch_a, m_w_branch_b, m_b_merge, m_w_out, v_norm_gain, v_w_in, v_q_norm_a, v_k_norm_a, v_q_norm_b, v_k_norm_b, v_sink_a, v_rel_bias, v_w_branch_a, v_w_branch_b, v_b_merge, v_w_out):
    csh = D // NDEV
    w_in_t, m_w_in_t, v_w_in_t = (jnp.swapaxes(t, 1, 2) for t in (w_in, m_w_in, v_w_in))
    weights = _GatheredWeights([w_in_t[0].astype(bf16), w_branch_a[0].astype(bf16), w_branch_b[0].astype(bf16),
                                w_out[0].astype(bf16), b_merge[0]])

    pending = {}
    core = lax.axis_index("c").astype(jnp.int32).reshape(1)
    chip = (2 * lax.axis_index("x") + lax.axis_index("y")).astype(jnp.int32).reshape(1)
    me = (2 * chip + core).astype(jnp.int32)

    def start_exchange(gw):
        if "w_in_other" in gw:
            sems, lands, sent = _sibling_send_start(gw["w_in_other"])
            pending["sibling"] = (sems, lands)
            return sent
        if "w_in" in gw:
            from_sibling = _sibling_send_wait(*pending["sibling"], after=[gw["w_in"]])
            chip_sums = _pair_sum(gw["w_in"], from_sibling, "grad_pair_sum")
            pending["w_in"] = _scatter_start([chip_sums], [], "scatter_w_in_start")
            return pending["w_in"][4][0, 0]
        blocks = [gw["w_branch_a"].reshape(512, NDEV, csh).transpose(1, 0, 2).astype(bf16),
                  gw["w_branch_b"].reshape(512, NDEV, csh).transpose(1, 0, 2).astype(bf16),
                  gw["w_out"].reshape(NDEV, csh, D).astype(bf16),
                  gw["b_merge"].reshape(2, NDEV, csh).transpose(1, 0, 2)]
        pending["rest"] = _scatter_start([], blocks, "scatter_rest_start")
        return pending["rest"][4][0, 0]

    loc = _local_step(x[0], loss_target[0], norm_gain, weights, q_norm_a, k_norm_a, q_norm_b, k_norm_b, sink_a,
                      rel_bias, on_weight_grads=start_exchange, core=core)

    small_shapes = dict(norm_gain=(1, D), q_norm_a=(1, HD), k_norm_a=(1, HD), q_norm_b=(1, HD), k_norm_b=(1, HD),
                        sink_a=(1, 8), rel_bias=(32, 32))
    (r_small,) = _exchange([], [_pack_small(loc, loc["loss"])], "gather_small_grads")
    send_sems, recv_sems, srcs, lands, _ = pending["rest"]
    (s_a, s_b, s_o, s_bm), (r_a, r_b, r_o, r_bm) = _scatter_wait(
        send_sems, recv_sems, srcs, lands, r_small, "scatter_rest_wait")
    send_sems, recv_sems, srcs, lands, _ = pending["w_in"]
    (s_in,), (r_in,) = _scatter_wait(send_sems, recv_sems, srcs, lands, r_small, "scatter_w_in_wait")

    given = dict(norm_gain=norm_gain, q_norm_a=q_norm_a, k_norm_a=k_norm_a, q_norm_b=q_norm_b, k_norm_b=k_norm_b,
                 sink_a=sink_a, rel_bias=rel_bias)
    m_small = dict(norm_gain=m_norm_gain, q_norm_a=m_q_norm_a, k_norm_a=m_k_norm_a, q_norm_b=m_q_norm_b,
                   k_norm_b=m_k_norm_b, sink_a=m_sink_a, rel_bias=m_rel_bias)
    v_small = dict(norm_gain=v_norm_gain, q_norm_a=v_q_norm_a, k_norm_a=v_k_norm_a, q_norm_b=v_q_norm_b,
                   k_norm_b=v_k_norm_b, sink_a=v_sink_a, rel_bias=v_rel_bias)
    res = {
        "small": _adamw(_pack_small(given), r_small, _pack_small(m_small), _pack_small(v_small), "adamw_small"),
        "w_in": [jnp.swapaxes(t, 1, 2) for t in
                 _adamw_own(w_in_t, s_in, chip, r_in, m_w_in_t, v_w_in_t, "adamw_w_in")],
        "w_branch_a": _adamw_own(w_branch_a, s_a, me, r_a, m_w_branch_a, v_w_branch_a, "adamw_w_branch_a"),
        "w_branch_b": _adamw_own(w_branch_b, s_b, me, r_b, m_w_branch_b, v_w_branch_b, "adamw_w_branch_b"),
        "b_merge": _adamw_own(b_merge, s_bm, me, r_bm, m_b_merge, v_b_merge, "adamw_b_merge"),
        "w_out": _adamw_own(w_out, s_o, me, r_o, m_w_out, v_w_out, "adamw_w_out"),
    }
    order = ["norm_gain", "w_in", "q_norm_a", "k_norm_a", "q_norm_b", "k_norm_b", "sink_a", "rel_bias", "w_branch_a",
             "w_branch_b", "b_merge", "w_out"]
    outs = []
    for k in range(4):
        small = _unpack_small(res["small"][k], small_shapes)
        for n in order:
            outs.append(small[n] if n in small else res[n][k])
    loss = res["small"][0][0, SMALL_USED]
    return (loss, loc["grad_x"][None], *outs)
```

```python
import math

import numpy as np
import jax
import jax.numpy as jnp
from jax import lax
from jax.experimental import pallas as pl
from jax.experimental.pallas import tpu as pltpu

f32 = jnp.float32
bf16 = jnp.bfloat16

S = 4096
D = 1024
NA = 5376
NT = 3072
NW = NA + NT
WSH = NW // 8
HD = 64
LANES = 128
EPS = 1e-6
NEG = -1e30
SCALE = HD ** -0.5
TQ = 128
PAD = 128
SP = S + 2 * PAD
NDEV = 8
GROUPS = ((128, 1, 0), (64, 1, 8), (64, 4, 16), (64, 16, 24))
CHUNK = 256
PCHUNK = 128
RC = 64

ADAM_LR, ADAM_B1, ADAM_B2, ADAM_EPS, ADAM_WD, ADAM_STEP = 0.001, 0.9, 0.999, 1e-08, 0.01, 10

MIB = 1024 * 1024
NT_DIMS = (((1,), (1,)), ((), ()))
TN_DIMS = (((0,), (0,)), ((), ()))


def _params(sem=None, vmem_mib=48):
    return pltpu.CompilerParams(dimension_semantics=sem, vmem_limit_bytes=vmem_mib * MIB)


def _lo():
    return lax.broadcasted_iota(jnp.int32, (1, LANES), 1) < HD


def _head_ones():
    r = lax.broadcasted_iota(jnp.int32, (LANES, LANES), 0) // HD
    c = lax.broadcasted_iota(jnp.int32, (LANES, LANES), 1) // HD
    return jnp.where(r == c, 1.0, 0.0).astype(bf16)


def _half_sums(x, ones):
    hi = x.astype(bf16)
    mid = (x - hi.astype(f32)).astype(bf16)
    return (jnp.dot(hi, ones, preferred_element_type=f32) + jnp.dot(mid, ones, preferred_element_type=f32))


def _seg_sum(x, ones):
    outs = [_half_sums(x[:, b * LANES:(b + 1) * LANES], ones) for b in range(x.shape[1] // LANES)]
    return outs[0] if len(outs) == 1 else jnp.concatenate(outs, axis=1)


def _bucket_np(blk, stride):
    w = TQ + 2 * blk
    rel = np.arange(w)[None, :] - blk - np.arange(TQ)[:, None]
    band = np.abs(rel) <= blk
    r = rel * stride
    n = np.abs(r)
    nf = np.maximum(n, 8).astype(np.float32)
    large = 8 + (np.log(nf / np.float32(8)) / np.float32(math.log(128.0)) * np.float32(8)).astype(np.int32)
    large = np.minimum(large, 15)
    b = (r > 0).astype(np.int32) * 16 + np.where(n < 8, n, large)
    return np.where(band, b, -1).astype(np.int32)


def _rms(x, gain):
    ts = 512

    def body(x_ref, g_ref, h_ref, ht_ref, r_ref):
        xv = x_ref[...]
        r = lax.rsqrt(jnp.mean(xv * xv, axis=-1, keepdims=True) + EPS)
        h = (xv * r) * g_ref[...]
        h_ref[...] = h.astype(bf16)
        ht_ref[...] = h.T.astype(bf16)
        r_ref[...] = r

    return pl.pallas_call(
        body,
        grid=(S // ts,),
        in_specs=[pl.BlockSpec((ts, D), lambda i: (i, 0)), pl.BlockSpec((1, D), lambda i: (0, 0))],
        out_specs=[pl.BlockSpec((ts, D), lambda i: (i, 0)), pl.BlockSpec((D, ts), lambda i: (0, i)),
                   pl.BlockSpec((ts, 1), lambda i: (i, 0))],
        out_shape=[jax.ShapeDtypeStruct((S, D), bf16), jax.ShapeDtypeStruct((D, S), bf16),
                   jax.ShapeDtypeStruct((S, 1), f32)],
        compiler_params=_params(("arbitrary",)),
        name="rms",
    )(x, gain)


def _inproj_half(hb, w_t, half, proj, name):
    ts = 512
    tn = NW // 2
    per = NW // 2 // tn

    def body(h_idx, h_ref, w_ref, *rest):
        del h_idx
        rest[-1][...] = lax.dot_general(h_ref[...], w_ref[...], NT_DIMS, preferred_element_type=f32)

    in_specs = [pl.BlockSpec((ts, D), lambda i, n, hf: (i, 0)),
                pl.BlockSpec((tn, D), lambda i, n, hf: (hf[0] * per + n, 0))]
    args = [half, hb, w_t]
    aliases = {}
    if proj is not None:
        in_specs.append(pl.BlockSpec(memory_space=pl.ANY))
        args.append(proj)
        aliases = {3: 0}
    return pl.pallas_call(
        body,
        grid_spec=pltpu.PrefetchScalarGridSpec(
            num_scalar_prefetch=1,
            grid=(S // ts, per),
            in_specs=in_specs,
            out_specs=pl.BlockSpec((ts, tn), lambda i, n, hf: (i, hf[0] * per + n)),
        ),
        out_shape=jax.ShapeDtypeStruct((S, NW), f32),
        input_output_aliases=aliases,
        compiler_params=_params(("arbitrary", "arbitrary")),
        name=name,
    )(*args)


def _bias_expand(table, bucket, c0, name):
    tq, w = bucket.shape
    blk = (w - tq) // 2

    def body(tab_ref, bk_ref, o_ref):
        h = pl.program_id(0)
        bk = bk_ref[...]

        def step(b, acc):
            return jnp.where(bk == b, tab_ref[b, c0 + h], acc)

        inner = lax.fori_loop(0, 32, step, jnp.full((tq, w), NEG, f32))
        col = lax.broadcasted_iota(jnp.int32, (1, w), 1)
        o_ref[0] = jnp.where(col < blk, NEG, inner)
        o_ref[1] = inner
        o_ref[2] = jnp.where(col >= tq + blk, NEG, inner)

    return pl.pallas_call(
        body,
        grid=(8,),
        in_specs=[pl.BlockSpec(memory_space=pltpu.SMEM), pl.BlockSpec((tq, w), lambda h: (0, 0))],
        out_specs=pl.BlockSpec((3, None, tq, w), lambda h: (0, h, 0, 0)),
        out_shape=jax.ShapeDtypeStruct((3, 8, tq, w), f32),
        compiler_params=_params(("arbitrary",)),
        name=name,
    )(table, bucket)


def _tile_kind(t, seq):
    m0 = jnp.bitwise_and(t * TQ, seq - 1)
    return jnp.where(m0 == 0, 0, jnp.where(m0 == seq - TQ, 2, 1))


def _col_block(g, j):
    kind = j // 4
    hp = j % 4
    a = jnp.where(kind == 0, hp, 3 + kind)
    b = 6 + 12 * kind + 4 * (g - 1) + hp
    return jnp.where(g == 0, a, b)


def _prep(proj_a, gains):
    def body(p0_ref, p1_ref, p2_ref, p3_ref, g_ref, o_ref):
        g = pl.program_id(0)
        kind = pl.program_id(1)
        lo = _lo()
        ones = _head_ones()
        half = jnp.where(lo, 0, 1)
        gain = g_ref[...]

        def norm_store(xv, u, dst, dup):
            if dup:
                take = (kind == 0) | (half == u // 2)
                xv = jnp.where(take, xv, pltpu.roll(xv, HD, 1))
            r = lax.rsqrt(_half_sums(xv * xv, ones) * (1.0 / HD) + EPS)
            r = jnp.where(kind == 2, 1.0, r)
            yv = (xv * r) * gain
            yv = jnp.where(kind == 0, yv * SCALE, yv)
            o_ref[u, PAD + dst:PAD + dst + CHUNK, :] = yv.astype(bf16)

        for u in range(4):
            o_ref[u, 0:PAD, :] = jnp.zeros((PAD, LANES), bf16)
            o_ref[u, PAD + S:SP, :] = jnp.zeros((PAD, LANES), bf16)
        for gi, (_, d, _) in enumerate(GROUPS):
            @pl.when(g == gi)
            def _():
                seq = S // d
                for u, p_ref in enumerate((p0_ref, p1_ref, p2_ref, p3_ref)):
                    for c in range(d):
                        for i in range(seq // CHUNK):
                            if d == 1:
                                xv = p_ref[i * CHUNK:(i + 1) * CHUNK, :]
                            else:
                                xv = p_ref[pl.ds(c + i * CHUNK * d, CHUNK, stride=d), :]
                            norm_store(xv, u, c * seq + i * CHUNK, gi == 0)

    return pl.pallas_call(
        body,
        grid=(4, 3),
        in_specs=[pl.BlockSpec((S, LANES), lambda g, kind, u=u: (0, _col_block(g, 4 * kind + u))) for u in range(4)] + [
            pl.BlockSpec((None, None, 1, LANES), lambda g, kind: (g, kind, 0, 0)),
        ],
        out_specs=pl.BlockSpec((None, 4, SP, LANES), lambda g, kind: (g, kind, 0, 0)),
        out_shape=jax.ShapeDtypeStruct((4, 12, SP, LANES), bf16),
        compiler_params=_params(("arbitrary", "arbitrary")),
        name="prep",
    )(proj_a, proj_a, proj_a, proj_a, gains)


def _token_rows(t, r0, n, d):
    if d == 1:
        return pl.ds(pl.multiple_of(t * TQ, TQ) + r0, n)
    per = S // d // TQ
    return pl.ds(((t % per) * TQ + r0) * d + t // per, n, stride=d)


def _stack_heads(t, lo):
    z = jnp.zeros_like(t)
    return jnp.concatenate([jnp.where(lo, t, z), jnp.where(lo, z, t)], axis=0)


def _unstack_heads(t2, lo):
    return jnp.where(lo, t2[:TQ], t2[TQ:])


def _attn_fwd(gl, bias, sink, g, blk, d, name):
    w = TQ + 2 * blk
    seq = S // d
    use_sink = sink is not None

    def body(*refs):
        if use_sink:
            sink_ref, q_ref, k_ref, v_ref, b_ref, o_ref, l_ref, s0, s1, p0, p1, lse_scr = refs
        else:
            q_ref, k_ref, v_ref, b_ref, o_ref, l_ref, s0, s1, p0, p1, lse_scr = refs
        hp = pl.program_id(0)
        lo = _lo()
        s_bufs, p_bufs = (s0, s1), (p0, p1)

        def scores(p, slot):
            for u in range(2):
                f0 = pl.multiple_of((2 * p + u) * TQ, TQ)
                q2 = _stack_heads(q_ref[pl.ds(PAD + f0, TQ), :], lo)
                kw = k_ref[pl.ds(PAD - blk + f0, w), :]
                s_bufs[slot][u] = lax.dot_general(q2, kw, NT_DIMS, preferred_element_type=f32)

        def softmax(p, slot):
            for u in range(2):
                t = 2 * p + u
                kind = _tile_kind(t, seq)
                for h in range(2):
                    for r in range(TQ // RC):
                        rows = slice(h * TQ + r * RC, h * TQ + (r + 1) * RC)
                        logit = s_bufs[slot][u, rows, :] + b_ref[kind, h, r * RC:(r + 1) * RC, :]
                        m = jnp.max(logit, axis=1, keepdims=True)
                        e = jnp.exp(logit - m)
                        lse = m + jnp.log(jnp.sum(e, axis=1, keepdims=True))
                        if use_sink:
                            sk = sink_ref[2 * hp + h]
                            mx = jnp.maximum(lse, sk)
                            lse = mx + jnp.log(jnp.exp(lse - mx) + jnp.exp(sk - mx))
                        p_bufs[slot][u, rows, :] = (e * jnp.exp(m - lse)).astype(bf16)
                        lse_scr[u, rows, :] = jnp.broadcast_to(lse, (RC, LANES))
                l_ref[_token_rows(t, 0, TQ, d), :] = jnp.where(lo, lse_scr[u, 0:TQ, :], lse_scr[u, TQ:2 * TQ, :])

        def values(p, slot):
            for u in range(2):
                t = 2 * p + u
                vw = v_ref[pl.ds(PAD - blk + pl.multiple_of(t * TQ, TQ), w), :]
                o2 = jnp.dot(p_bufs[slot][u], vw, preferred_element_type=f32)
                o_ref[_token_rows(t, 0, TQ, d), :] = _unstack_heads(o2, lo)

        npair = S // TQ // 2
        scores(0, 0)
        scores(1, 1)
        softmax(0, 0)

        def steady(k, carry):
            p = 2 * k + 2
            scores(p, 0)
            softmax(p - 1, 1)
            values(p - 2, 0)
            scores(p + 1, 1)
            softmax(p, 0)
            values(p - 1, 1)
            return carry

        lax.fori_loop(0, (npair - 2) // 2, steady, 0)
        softmax(npair - 1, 1)
        values(npair - 2, 0)
        values(npair - 1, 1)

    in_specs = [
        pl.BlockSpec((None, None, SP, LANES), lambda hp: (g, hp, 0, 0)),
        pl.BlockSpec((None, None, SP, LANES), lambda hp: (g, 4 + hp, 0, 0)),
        pl.BlockSpec((None, None, SP, LANES), lambda hp: (g, 8 + hp, 0, 0)),
        pl.BlockSpec((3, 2, TQ, w), lambda hp: (0, hp, 0, 0)),
    ]
    args = [gl, gl, gl, bias]
    if use_sink:
        in_specs = [pl.BlockSpec(memory_space=pltpu.SMEM)] + in_specs
        args = [sink] + args
    out = pl.BlockSpec((S, LANES), lambda hp: (0, hp))
    return pl.pallas_call(
        body,
        grid=(4,),
        in_specs=in_specs,
        out_specs=[out, out],
        out_shape=[jax.ShapeDtypeStruct((S, 4 * LANES), f32)] * 2,
        scratch_shapes=[pltpu.VMEM((2, 2 * TQ, w), f32), pltpu.VMEM((2, 2 * TQ, w), f32),
                        pltpu.VMEM((2, 2 * TQ, w), bf16), pltpu.VMEM((2, 2 * TQ, w), bf16),
                        pltpu.VMEM((2, 2 * TQ, LANES), f32)],
        compiler_params=_params(("arbitrary",)),
        name=name,
    )(*args)


def _attn_bwd(gl, bias, bucket, do, lse, dd, g, blk, d, name):
    w = TQ + 2 * blk
    seq = S // d

    def body(q_ref, k_ref, v_ref, b_ref, bk_ref, do_ref, l_ref, d_ref, dqkv_ref, dbk_ref,
             db_acc, s0, s1, dp0, dp1, pb0, pb1, ds0, ds1, dk_acc, dv_acc):
        lo = _lo()
        hi = jnp.logical_not(lo)
        dk_acc[...] = jnp.zeros((SP, LANES), f32)
        dv_acc[...] = jnp.zeros((SP, LANES), f32)
        db_acc[...] = jnp.zeros((2 * TQ, w), f32)
        s_bufs, dp_bufs, pb_bufs, ds_bufs = (s0, s1), (dp0, dp1), (pb0, pb1), (ds0, ds1)

        def stacked(t):
            f0 = pl.multiple_of(t * TQ, TQ)
            q2 = _stack_heads(q_ref[pl.ds(PAD + f0, TQ), :], lo)
            do2 = _stack_heads(do_ref[_token_rows(t, 0, TQ, d), :].astype(bf16), lo)
            return f0, q2, do2

        def scores(p, slot):
            for u in range(2):
                f0, q2, do2 = stacked(2 * p + u)
                win = pl.ds(PAD - blk + f0, w)
                s_bufs[slot][u] = lax.dot_general(q2, k_ref[win, :], NT_DIMS, preferred_element_type=f32)
                dp_bufs[slot][u] = lax.dot_general(do2, v_ref[win, :], NT_DIMS, preferred_element_type=f32)

        def grads(p, slot):
            for u in range(2):
                t = 2 * p + u
                kind = _tile_kind(t, seq)
                for h in range(2):
                    msk = lo if h == 0 else hi
                    for r in range(TQ // RC):
                        rows = slice(h * TQ + r * RC, h * TQ + (r + 1) * RC)
                        src = _token_rows(t, r * RC, RC, d)
                        lh = jnp.max(jnp.where(msk, l_ref[src, :], -jnp.inf), axis=1, keepdims=True)
                        dh = jnp.max(jnp.where(msk, d_ref[src, :], -jnp.inf), axis=1, keepdims=True)
                        logit = s_bufs[slot][u, rows, :] + b_ref[kind, h, r * RC:(r + 1) * RC, :]
                        pr = jnp.exp(logit - lh)
                        ds = pr * (dp_bufs[slot][u, rows, :] - dh)
                        db_acc[rows, :] += ds
                        pb_bufs[slot][u, rows, :] = pr.astype(bf16)
                        ds_bufs[slot][u, rows, :] = ds.astype(bf16)

        def accumulate(p, slot):
            for u in range(2):
                f0, q2, do2 = stacked(2 * p + u)
                win = pl.ds(PAD - blk + f0, w)
                dsb = ds_bufs[slot][u]
                dq2 = jnp.dot(dsb, k_ref[win, :], preferred_element_type=f32)
                dqkv_ref[0, pl.ds(PAD + f0, TQ), :] = _unstack_heads(dq2, lo).astype(bf16)
                dk_acc[win, :] += lax.dot_general(dsb, q2, TN_DIMS, preferred_element_type=f32)
                dv_acc[win, :] += lax.dot_general(pb_bufs[slot][u], do2, TN_DIMS, preferred_element_type=f32)

        npair = S // TQ // 2
        scores(0, 0)
        scores(1, 1)
        grads(0, 0)

        def steady(k, carry):
            p = 2 * k + 2
            scores(p, 0)
            grads(p - 1, 1)
            accumulate(p - 2, 0)
            scores(p + 1, 1)
            grads(p, 0)
            accumulate(p - 1, 1)
            return carry

        lax.fori_loop(0, (npair - 2) // 2, steady, 0)
        grads(npair - 1, 1)
        accumulate(npair - 2, 0)
        accumulate(npair - 1, 1)
        for i in range(SP // CHUNK):
            rows = slice(i * CHUNK, (i + 1) * CHUNK)
            dqkv_ref[1, rows, :] = dk_acc[rows, :].astype(bf16)
            dqkv_ref[2, rows, :] = dv_acc[rows, :].astype(bf16)

        bk = bk_ref[...]
        lane = lax.broadcasted_iota(jnp.int32, (8, LANES), 1)
        for h in range(2):
            db = db_acc[h * TQ:(h + 1) * TQ, :]
            acc = jnp.zeros((8, LANES), f32)
            for b in range(32):
                part = jnp.where(bk == b, db, 0.0).reshape(TQ // 8, 8, w).sum(axis=0)
                tot = jnp.sum(jnp.sum(part, axis=1, keepdims=True), axis=0, keepdims=True)
                acc = jnp.where(lane == b, tot, acc)
            dbk_ref[h] = acc

    def gcol(off):
        return pl.BlockSpec((None, None, SP, LANES), lambda hp: (g, off + hp, 0, 0))

    row = pl.BlockSpec((S, LANES), lambda hp: (0, hp))
    return pl.pallas_call(
        body,
        grid=(4,),
        in_specs=[gcol(0), gcol(4), gcol(8), pl.BlockSpec((3, 2, TQ, w), lambda hp: (0, hp, 0, 0)),
                  pl.BlockSpec((TQ, w), lambda hp: (0, 0)), row, row, row],
        out_specs=[pl.BlockSpec((3, None, SP, LANES), lambda hp: (0, hp, 0, 0)),
                   pl.BlockSpec((2, 8, LANES), lambda hp: (hp, 0, 0))],
        out_shape=[
            jax.ShapeDtypeStruct((3, 4, SP, LANES), bf16),
            jax.ShapeDtypeStruct((8, 8, LANES), f32),
        ],
        scratch_shapes=([pltpu.VMEM((2 * TQ, w), f32)] + [pltpu.VMEM((2, 2 * TQ, w), f32)] * 4
                        + [pltpu.VMEM((2, 2 * TQ, w), bf16)] * 4 + [pltpu.VMEM((SP, LANES), f32)] * 2),
        compiler_params=_params(("arbitrary",), vmem_mib=56),
        name=name,
    )(gl, gl, gl, bias, bucket, do, lse, dd)


def _sigmoid(z):
    return 1.0 / (1.0 + jnp.exp(-z))


def _tail(x, tgt, o_a, l_a, o_b, l_b, proj, bm, w_a, w_b, w_o, sink_b):
    ts = 256

    def body(x_ref, t_ref, oa_ref, la_ref, ob0_ref, ob1_ref, ob2_ref, lb0_ref, lb1_ref, lb2_ref,
             ga_ref, gb_ref, m0_ref, m1_ref, bm_ref, wa_ref, wb_ref, wo_ref, sk_ref,
             dy_ref, dyb_ref, dt_ref, doa_ref, dda_ref, dob0_ref, dob1_ref, dob2_ref, ddb0_ref, ddb1_ref, ddb2_ref,
             ya_ref, yb_ref, mg_ref, dbra_ref, dbrb_ref, loss_ref, dbm_ref, dsk_ref):
        i = pl.program_id(0)

        @pl.when(i == 0)
        def _():
            loss_ref[...] = jnp.zeros_like(loss_ref)
            dbm_ref[...] = jnp.zeros_like(dbm_ref)
            dsk_ref[...] = jnp.zeros_like(dsk_ref)

        ga = ga_ref[...]
        sa = _sigmoid(ga)
        silu_a = ga * sa
        oa = oa_ref[...]
        ya = oa * silu_a
        gb = gb_ref[...]
        sb = _sigmoid(gb)
        silu_b = gb * sb
        ob = [ob0_ref[...], ob1_ref[...], ob2_ref[...]]
        lb = [lb0_ref[...], lb1_ref[...], lb2_ref[...]]
        mx = jnp.maximum(jnp.maximum(lb[0], lb[1]), lb[2])
        ex = [jnp.exp(v - mx) for v in lb]
        den = ex[0] + ex[1] + ex[2]
        alpha = [e / den for e in ex]
        ybc = alpha[0] * ob[0] + alpha[1] * ob[1] + alpha[2] * ob[2]
        yb = ybc * silu_b
        yab = ya.astype(bf16)
        ybb = yb.astype(bf16)
        br_a = jnp.dot(yab, wa_ref[...], preferred_element_type=f32)
        br_b = jnp.dot(ybb, wb_ref[...], preferred_element_type=f32)
        g0 = _sigmoid(m0_ref[...] + bm_ref[0:1, :])
        g1 = _sigmoid(m1_ref[...] + bm_ref[1:2, :])
        merged = g0 * br_a + g1 * br_b
        mgb = merged.astype(bf16)
        y = x_ref[...] + jnp.dot(mgb, wo_ref[...], preferred_element_type=f32)
        err = y - t_ref[...]
        part = jnp.sum(jnp.sum(err * err, axis=1, keepdims=True), axis=0, keepdims=True)
        loss_ref[...] += part * (0.5 / D)
        dy = err * (1.0 / D)
        dyb = dy.astype(bf16)
        dmerged = lax.dot_general(dyb, wo_ref[...], NT_DIMS, preferred_element_type=f32)
        dbr_a = (dmerged * g0).astype(bf16)
        dbr_b = (dmerged * g1).astype(bf16)
        dm0 = dmerged * br_a * (g0 * (1.0 - g0))
        dm1 = dmerged * br_b * (g1 * (1.0 - g1))
        dbm_ref[0:1, :] += jnp.sum(dm0, axis=0, keepdims=True)
        dbm_ref[1:2, :] += jnp.sum(dm1, axis=0, keepdims=True)
        dya = lax.dot_general(dbr_a, wa_ref[...], NT_DIMS, preferred_element_type=f32)
        dyb2 = lax.dot_general(dbr_b, wb_ref[...], NT_DIMS, preferred_element_type=f32)
        do_a = dya * silu_a
        dga = dya * oa * (sa * (1.0 + ga * (1.0 - sa)))
        ones = _head_ones()
        delta_a = _seg_sum(do_a * oa, ones)
        dsk_ref[...] -= jnp.sum(delta_a * jnp.exp(sk_ref[...] - la_ref[...]), axis=0, keepdims=True)
        dybc = dyb2 * silu_b
        dgb = dyb2 * ybc * (sb * (1.0 + gb * (1.0 - sb)))
        dbar = _seg_sum(dybc * ybc, ones)
        dy_ref[...] = dy
        dyb_ref[...] = dyb
        dt_ref[:, 0:512] = dga.astype(bf16)
        dt_ref[:, 512:1024] = dgb.astype(bf16)
        dt_ref[:, 1024:2048] = dm0.astype(bf16)
        dt_ref[:, 2048:3072] = dm1.astype(bf16)
        doa_ref[...] = do_a.astype(bf16)
        dda_ref[...] = delta_a
        for k, (dob_ref, ddb_ref) in enumerate(((dob0_ref, ddb0_ref), (dob1_ref, ddb1_ref), (dob2_ref, ddb2_ref))):
            dob_ref[...] = alpha[k] * dybc
            ddb_ref[...] = alpha[k] * dbar
        ya_ref[...] = ya.T.astype(bf16)
        yb_ref[...] = yb.T.astype(bf16)
        mg_ref[...] = merged.T.astype(bf16)
        dbra_ref[...] = dbr_a
        dbrb_ref[...] = dbr_b

    def rows(n, blk=0):
        return pl.BlockSpec((ts, n), lambda i: (i, blk))

    def whole(r, c):
        return pl.BlockSpec((r, c), lambda i: (0, 0))

    def cols(n):
        return pl.BlockSpec((n, ts), lambda i: (0, i))

    def gate_cols(n, col):
        return pl.BlockSpec((pl.Element(ts), pl.Element(n)), lambda i: (i * ts, NA + col))

    outs = [
        ((S, D), f32, rows(D)), ((S, D), bf16, rows(D)), ((S, NW), bf16, gate_cols(NT, 0)),
        ((S, 512), bf16, rows(512)), ((S, 512), f32, rows(512)),
        ((S, 512), f32, rows(512)), ((S, 512), f32, rows(512)), ((S, 512), f32, rows(512)),
        ((S, 512), f32, rows(512)), ((S, 512), f32, rows(512)), ((S, 512), f32, rows(512)),
        ((512, S), bf16, cols(512)), ((512, S), bf16, cols(512)), ((D, S), bf16, cols(D)),
        ((S, D), bf16, rows(D)), ((S, D), bf16, rows(D)),
        ((1, 1), f32, whole(1, 1)), ((2, D), f32, whole(2, D)), ((1, 512), f32, whole(1, 512)),
    ]
    return pl.pallas_call(
        body,
        grid=(S // ts,),
        in_specs=[
            rows(D), rows(D), rows(512), rows(512), rows(512), rows(512), rows(512), rows(512), rows(512), rows(512),
            gate_cols(512, 0), gate_cols(512, 512), gate_cols(D, 1024), gate_cols(D, 2048), whole(2, D),
            whole(512, D), whole(512, D), whole(D, D), whole(1, 512),
        ],
        out_specs=[o[2] for o in outs],
        out_shape=[jax.ShapeDtypeStruct(o[0], o[1]) for o in outs],
        compiler_params=_params(("arbitrary",), vmem_mib=60),
        name="tail",
    )(x, tgt, o_a, l_a, *o_b, *l_b, proj, proj, proj, proj, bm, w_a, w_b, w_o, sink_b)


def _norm_bwd(xv, dyv, gain, kind, ones):
    r = lax.rsqrt(_half_sums(xv * xv, ones) * (1.0 / HD) + EPS)
    yv = xv * r
    up = jnp.where(kind == 0, dyv * SCALE, dyv)
    u = up * gain
    dxv = r * (u - yv * (_half_sums(u * yv, ones) * (1.0 / HD)))
    dxv = jnp.where(kind == 2, dyv, dxv)
    dg = jnp.where(kind == 2, 0.0, jnp.sum(up * yv, axis=0, keepdims=True))
    return dxv, dg


def _post_b(g, dqkv, proj_a, gains, dproj):
    d = GROUPS[g][1]
    seq = S // d

    def body(d_ref, pa_ref, pb_ref, g_ref, alias_ref, o_ref, dg_ref, nat_a, nat_b):
        del alias_ref
        pj = pl.program_id(0)
        kind = pj // 2
        gain = g_ref[...]
        ones = _head_ones()

        @pl.when(pj % 2 == 0)
        def _():
            dg_ref[...] = jnp.zeros_like(dg_ref)

        for u, (p_ref, nat) in enumerate(((pa_ref, nat_a), (pb_ref, nat_b))):
            for c in range(d):
                for i in range(seq // PCHUNK):
                    src = c * seq + i * PCHUNK
                    if d == 1:
                        idx = slice(src, src + PCHUNK)
                    else:
                        idx = pl.ds(c + i * PCHUNK * d, PCHUNK, stride=d)
                    dyv = d_ref[u, PAD + src:PAD + src + PCHUNK, :].astype(f32)
                    dxv, dg = _norm_bwd(p_ref[idx, :], dyv, gain, kind, ones)
                    nat[idx, :] = dxv
                    dg_ref[...] += dg
            for i in range(S // CHUNK):
                rows = slice(i * CHUNK, (i + 1) * CHUNK)
                o_ref[rows, u * LANES:(u + 1) * LANES] = nat[rows, :].astype(bf16)

    def pcol(u):
        return pl.BlockSpec((S, LANES), lambda pj: (0, _col_block(g, 2 * jnp.minimum(pj, 3) + u)))

    return pl.pallas_call(
        body,
        grid=(6,),
        in_specs=[
            pl.BlockSpec((None, 2, SP, LANES), lambda pj: (pj // 2, pj % 2, 0, 0)),
            pcol(0), pcol(1),
            pl.BlockSpec((None, None, 1, LANES), lambda pj: (g, pj // 2, 0, 0)),
            pl.BlockSpec(memory_space=pl.ANY),
        ],
        out_specs=[
            pl.BlockSpec((S, 2 * LANES), lambda pj: (0, _col_block(g, 2 * pj) // 2)),
            pl.BlockSpec((None, 1, LANES), lambda pj: (pj // 2, 0, 0)),
        ],
        out_shape=[jax.ShapeDtypeStruct((S, NW), bf16), jax.ShapeDtypeStruct((3, 1, LANES), f32)],
        scratch_shapes=[pltpu.VMEM((S, LANES), f32), pltpu.VMEM((S, LANES), f32)],
        input_output_aliases={4: 0},
        compiler_params=_params(("arbitrary",)),
        name="post_b%d" % g,
    )(dqkv, proj_a, proj_a, gains, dproj)


def _post_a(dqkv, proj_a, gains, dproj):
    def body(q_ref, e_ref, p_ref, g_ref, alias_ref, o_ref, dg_ref):
        del alias_ref
        j = pl.program_id(0)
        kind = jnp.maximum(j - 3, 0)
        gain = g_ref[...]
        lo = _lo()
        ones = _head_ones()

        @pl.when((j == 0) | (j >= 4))
        def _():
            dg_ref[...] = jnp.zeros_like(dg_ref)

        for i in range(S // PCHUNK):
            r0 = i * PCHUNK
            rows = slice(PAD + r0, PAD + r0 + PCHUNK)
            t0 = e_ref[0, rows, :].astype(f32) + e_ref[1, rows, :].astype(f32)
            t1 = e_ref[2, rows, :].astype(f32) + e_ref[3, rows, :].astype(f32)
            folded = jnp.where(lo, t0 + pltpu.roll(t0, HD, 1), t1 + pltpu.roll(t1, HD, 1))
            dyv = jnp.where(kind == 0, q_ref[rows, :].astype(f32), folded)
            dxv, dg = _norm_bwd(p_ref[r0:r0 + PCHUNK, :], dyv, gain, kind, ones)
            o_ref[r0:r0 + PCHUNK, :] = dxv.astype(bf16)
            dg_ref[...] += dg

    return pl.pallas_call(
        body,
        grid=(6,),
        in_specs=[
            pl.BlockSpec((None, None, SP, LANES), lambda j: (0, jnp.minimum(j, 3), 0, 0)),
            pl.BlockSpec((None, 4, SP, LANES), lambda j: (jnp.clip(j - 3, 1, 2), 0, 0, 0)),
            pl.BlockSpec((S, LANES), lambda j: (0, jnp.minimum(j, 4))),
            pl.BlockSpec((None, None, 1, LANES), lambda j: (0, jnp.maximum(j - 3, 0), 0, 0)),
            pl.BlockSpec(memory_space=pl.ANY),
        ],
        out_specs=[
            pl.BlockSpec((S, LANES), lambda j: (0, j)),
            pl.BlockSpec((None, 1, LANES), lambda j: (jnp.maximum(j - 3, 0), 0, 0)),
        ],
        out_shape=[jax.ShapeDtypeStruct((S, NW), bf16), jax.ShapeDtypeStruct((3, 1, LANES), f32)],
        input_output_aliases={4: 0},
        compiler_params=_params(("arbitrary",)),
        name="post_a",
    )(dqkv, dqkv, proj_a, gains, dproj)


def _dh_norm_bwd(dproj, w, x, rstd, gain, dy):
    ts = 1024
    tk = NW // 6
    nk = NW // tk

    def body(d_ref, w_ref, x_ref, r_ref, g_ref, dy_ref, gx_ref, dgn_ref, acc):
        i = pl.program_id(0)
        k = pl.program_id(1)

        @pl.when((i == 0) & (k == 0))
        def _():
            dgn_ref[...] = jnp.zeros_like(dgn_ref)

        @pl.when(k == 0)
        def _():
            acc[...] = jnp.zeros_like(acc)

        acc[...] += jnp.dot(d_ref[...], w_ref[...], preferred_element_type=f32)

        @pl.when(k == nk - 1)
        def _():
            dh = acc[...]
            xh = x_ref[...] * r_ref[...]
            u = dh * g_ref[...]
            dx = r_ref[...] * (u - xh * jnp.mean(u * xh, axis=-1, keepdims=True))
            gx_ref[...] = dy_ref[...] + dx
            dgn_ref[...] += jnp.sum(dh * xh, axis=0, keepdims=True)

    return pl.pallas_call(
        body,
        grid=(S // ts, nk),
        in_specs=[
            pl.BlockSpec((ts, tk), lambda i, k: (i, k)),
            pl.BlockSpec((tk, D), lambda i, k: (k, 0)),
            pl.BlockSpec((ts, D), lambda i, k: (i, 0)),
            pl.BlockSpec((ts, 1), lambda i, k: (i, 0)),
            pl.BlockSpec((1, D), lambda i, k: (0, 0)),
            pl.BlockSpec((ts, D), lambda i, k: (i, 0)),
        ],
        out_specs=[pl.BlockSpec((ts, D), lambda i, k: (i, 0)), pl.BlockSpec((1, D), lambda i, k: (0, 0))],
        out_shape=[jax.ShapeDtypeStruct((S, D), f32), jax.ShapeDtypeStruct((1, D), f32)],
        scratch_shapes=[pltpu.VMEM((ts, D), f32)],
        compiler_params=_params(("arbitrary", "arbitrary"), vmem_mib=56),
        name="dh_norm_bwd",
    )(dproj, w, x, rstd, gain, dy)


def _dw_in(hbt, dproj, parity, name):
    tk = 1024
    win = WSH + 96

    def body(par_ref, a_ref, b_ref, o_ref, acc):
        p = 2 * pl.program_id(0) + par_ref[0]
        k = pl.program_id(1)

        @pl.when(k == 0)
        def _():
            acc[...] = jnp.zeros_like(acc)

        acc[...] += jnp.dot(a_ref[...], b_ref[...], preferred_element_type=f32)

        @pl.when(k == S // tk - 1)
        def _():
            acc_t = acc[...].T
            for pp in range(NDEV):
                off = (WSH * pp) % LANES

                @pl.when(p == pp)
                def _():
                    o_ref[...] = acc_t[off:off + WSH, :].astype(bf16)

    return pl.pallas_call(
        body,
        grid_spec=pltpu.PrefetchScalarGridSpec(
            num_scalar_prefetch=1,
            grid=(NDEV // 2, S // tk),
            in_specs=[
                pl.BlockSpec((D, tk), lambda q, k, par: (0, k)),
                pl.BlockSpec((pl.Element(tk), pl.Element(win)),
                             lambda q, k, par: (k * tk, (WSH * (2 * q + par[0])) // LANES * LANES)),
            ],
            out_specs=pl.BlockSpec((None, WSH, D), lambda q, k, par: (q, 0, 0)),
            scratch_shapes=[pltpu.VMEM((D, win), f32)],
        ),
        out_shape=jax.ShapeDtypeStruct((NDEV // 2, WSH, D), bf16),
        compiler_params=_params(("arbitrary", "arbitrary")),
        name=name,
    )(parity, hbt, dproj)


def _matmul_tokens(at, b, name):
    m, n = at.shape[0], b.shape[1]
    tn = 512
    tk = 1024

    def body(a_ref, b_ref, o_ref):
        @pl.when(pl.program_id(1) == 0)
        def _():
            o_ref[...] = jnp.zeros_like(o_ref)

        o_ref[...] += jnp.dot(a_ref[...], b_ref[...], preferred_element_type=f32)

    return pl.pallas_call(
        body,
        grid=(n // tn, S // tk),
        in_specs=[pl.BlockSpec((m, tk), lambda j, k: (0, k)), pl.BlockSpec((tk, tn), lambda j, k: (k, j))],
        out_specs=pl.BlockSpec((m, tn), lambda j, k: (0, j)),
        out_shape=jax.ShapeDtypeStruct((m, n), f32),
        compiler_params=_params(("arbitrary", "arbitrary")),
        name=name,
    )(at, b)


def _exchange(scatter, gather, name):
    arrs = list(scatter) + list(gather)
    n = len(arrs)
    ns = len(scatter)

    def body(*refs):
        ins, outs = refs[:n], refs[n:2 * n]
        send_sems, recv_sems, local_sems = refs[2 * n:]
        x, y, c = lax.axis_index("x"), lax.axis_index("y"), lax.axis_index("c")
        me = 4 * x + 2 * y + c
        local, remote = [], []
        for a in range(n):
            lc = pltpu.make_async_copy(ins[a].at[me] if a < ns else ins[a], outs[a].at[me], local_sems.at[a])
            lc.start()
            local.append(lc)
            for r in range(1, NDEV):
                px = 1 - x if r & 4 else x
                py = 1 - y if r & 2 else y
                pc = 1 - c if r & 1 else c
                cp = pltpu.make_async_remote_copy(
                    src_ref=ins[a].at[4 * px + 2 * py + pc] if a < ns else ins[a],
                    dst_ref=outs[a].at[me],
                    send_sem=send_sems.at[a, r - 1],
                    recv_sem=recv_sems.at[a, r - 1],
                    device_id=(px, py, pc),
                    device_id_type=pl.DeviceIdType.MESH,
                )
                cp.start()
                remote.append(cp)
        for cp in remote:
            cp.wait_recv()
        for cp in remote:
            cp.wait_send()
        for lc in local:
            lc.wait()

    out_shape = [jax.ShapeDtypeStruct(a.shape if i < ns else (NDEV,) + a.shape, a.dtype) for i, a in enumerate(arrs)]
    return pl.pallas_call(
        body,
        in_specs=[pl.BlockSpec(memory_space=pl.ANY)] * n,
        out_specs=[pl.BlockSpec(memory_space=pl.ANY)] * n,
        out_shape=out_shape,
        scratch_shapes=[
            pltpu.SemaphoreType.DMA((n, NDEV - 1)),
            pltpu.SemaphoreType.DMA((n, NDEV - 1)),
            pltpu.SemaphoreType.DMA((n,)),
        ],
        compiler_params=pltpu.CompilerParams(has_side_effects=True),
        name=name,
    )(*arrs)


_HBM = pl.BlockSpec(memory_space=pltpu.HBM)
_SEM = pl.BlockSpec(memory_space=pltpu.SEMAPHORE)
_EFFECT = pltpu.SideEffectType.DATAFLOW_SIDE_EFFECTING


def _comm_step(name, body_fn, lands, srcs=(), wait_sems=(), n_new=0, after=(), token=False):
    n, ns, nw, na = len(lands), len(srcs), len(wait_sems), len(after)

    def body(*refs):
        src, land = refs[:ns], refs[ns:ns + n]
        waits = refs[ns + n:ns + n + nw]
        new = refs[ns + n + nw + na:ns + n + nw + na + n_new]
        body_fn(src, land, waits, new)
        if token:
            refs[-1][...] = jnp.zeros((8, LANES), f32)

    hbm = [pltpu.HBM(a.shape, a.dtype) for a in lands]
    ops = [pltpu.with_memory_space_constraint(a, pltpu.HBM) for a in list(srcs) + list(lands)]
    extra_shape = [jax.ShapeDtypeStruct((8, LANES), f32)] if token else []
    extra_spec = [pl.BlockSpec(memory_space=pltpu.VMEM)] if token else []
    outs = pl.pallas_call(
        body,
        out_shape=tuple([pltpu.SemaphoreType.DMA(())] * n_new + hbm + extra_shape),
        in_specs=[_HBM] * (ns + n) + [_SEM] * nw + [pl.BlockSpec(memory_space=pl.ANY)] * na,
        out_specs=tuple([_SEM] * n_new + [_HBM] * n + extra_spec),
        input_output_aliases={ns + i: n_new + i for i in range(n)},
        compiler_params=pltpu.CompilerParams(has_side_effects=_EFFECT),
        name=name,
    )(*ops, *wait_sems, *after)
    if token:
        return list(outs[:n_new]), list(outs[n_new:n_new + n]), outs[-1][0, 0]
    return list(outs[:n_new]), list(outs[n_new:])


class _GatheredWeights:
    def __init__(self, shards):
        self.n = n = len(shards)
        x, y, c = lax.axis_index("x"), lax.axis_index("y"), lax.axis_index("c")
        self.x = x
        me = 4 * x + 2 * y + c
        lands = [lax.dynamic_update_slice(lax.empty((NDEV,) + s.shape, s.dtype), s[None], (me,) + (0,) * s.ndim)
                 for s in shards]

        def start_own(src, land, waits, new):
            p = self._peers()
            for a in range(n):
                for k, to in ((0, p["sibling"]), (1, p["xn"]), (2, p["yn"])):
                    self._copy(land[a], new, a, k, 3, p["me"], to).start()

        self.sems, self.lands = {}, None
        new, self.lands = _comm_step("gather_start", start_own, lands, n_new=6 * n)
        self._keep(new, (0, 1, 2))

    @staticmethod
    def _peers():
        x, y, c = lax.axis_index("x"), lax.axis_index("y"), lax.axis_index("c")
        return dict(
            me=(x, y, c), sibling=(x, y, 1 - c), xn=(1 - x, y, c), yn=(x, 1 - y, c), dg=(1 - x, 1 - y, c),
            relay_origin=(jnp.bitwise_xor(x, c), jnp.bitwise_xor(y, 1 - c), c),
            relay_target=(jnp.bitwise_xor(x, 1 - c), jnp.bitwise_xor(y, c), c))

    def _keep(self, new, ks):
        half = len(new) // 2
        i = 0
        for a in range(self.n):
            for k in ks:
                self.sems[a, k] = (new[i], new[half + i])
                i += 1

    @staticmethod
    def _copy(land, sem_refs, a, k, nk, block, to, src=None, ks=None):
        ks = tuple(range(nk)) if ks is None else ks
        half = len(sem_refs) // 2
        i = a * len(ks) + ks.index(k)
        slot = land.at[4 * block[0] + 2 * block[1] + block[2]]
        return pltpu.make_async_remote_copy(
            src_ref=slot if src is None else src, dst_ref=slot, send_sem=sem_refs[i], recv_sem=sem_refs[half + i],
            device_id=to, device_id_type=pl.DeviceIdType.MESH)

    def _sem_list(self, ks):
        return ([self.sems[a, k][0] for a in range(self.n) for k in ks]
                + [self.sems[a, k][1] for a in range(self.n) for k in ks])

    def first_half(self, after):
        n = self.n

        def relay(src, land, waits, new):
            p = self._peers()
            for a in range(n):
                self._copy(land[a], waits, a, 1, 0, p["xn"], p["me"], ks=(1, 2)).wait_recv()
                self._copy(land[a], waits, a, 2, 0, p["yn"], p["me"], ks=(1, 2)).wait_recv()
                self._copy(land[a], new, a, 3, 0, p["relay_origin"], p["relay_target"], ks=(3, 4, 5)).start()
                self._copy(land[a], new, a, 4, 0, p["xn"], p["sibling"], ks=(3, 4, 5)).start()
                self._copy(land[a], new, a, 5, 0, p["yn"], p["sibling"], ks=(3, 4, 5)).start()

        new, self.lands = _comm_step("gather_relay", relay, self.lands, wait_sems=self._sem_list((1, 2)),
                                     n_new=6 * n, after=after)
        self._keep(new, (3, 4, 5))

        def from_sibling(src, land, waits, new):
            p = self._peers()
            other = lambda b: (b[0], b[1], 1 - b[2])
            for a in range(n):
                self._copy(land[a], waits, a, 0, 0, other(p["me"]), p["me"], ks=(0, 4, 5)).wait_recv()
                self._copy(land[a], waits, a, 4, 0, other(p["xn"]), p["me"], ks=(0, 4, 5)).wait_recv()
                self._copy(land[a], waits, a, 5, 0, other(p["yn"]), p["me"], ks=(0, 4, 5)).wait_recv()

        _, self.lands = _comm_step("gather_wait_sibling", from_sibling, self.lands,
                                   wait_sems=self._sem_list((0, 4, 5)))
        return self.lands[0].reshape(NW, D), self.x.astype(jnp.int32).reshape(1)

    def second_half(self, after):
        n = self.n

        def forward_diagonal(src, land, waits, new):
            p = self._peers()
            for a in range(n):
                self._copy(land[a], waits, a, 3, 0, p["dg"], p["me"], ks=(3,)).wait_recv()
                self._copy(land[a], new, a, 6, 0, p["dg"], p["sibling"], ks=(6,)).start()

        new, self.lands = _comm_step("gather_forward_diagonal", forward_diagonal, self.lands,
                                     wait_sems=self._sem_list((3,)), n_new=2 * n, after=after)
        self._keep(new, (6,))

        def finish(src, land, waits, new):
            p = self._peers()
            ks = tuple(range(7))
            for a in range(n):
                self._copy(land[a], waits, a, 6, 0, (p["dg"][0], p["dg"][1], 1 - p["dg"][2]), p["me"], ks=ks).wait_recv()
                for k in ks:
                    self._copy(land[a], waits, a, k, 0, p["me"], p["me"], ks=ks).wait_send()

        _, self.lands = _comm_step("gather_finish", finish, self.lands, wait_sems=self._sem_list(tuple(range(7))))
        return self.lands[0].reshape(NW, D), (1 - self.x).astype(jnp.int32).reshape(1)

    def rest(self):
        g_a, g_b, g_o, g_bm = self.lands[1:]
        return (g_a.transpose(1, 0, 2).reshape(512, D), g_b.transpose(1, 0, 2).reshape(512, D),
                g_bm.transpose(1, 0, 2).reshape(2, D), g_o.reshape(D, D))


def _sibling_send_start(shares):
    landing = lax.empty(shares.shape, shares.dtype)

    def start(src, land, waits, new):
        x, y, c = lax.axis_index("x"), lax.axis_index("y"), lax.axis_index("c")
        pltpu.make_async_remote_copy(src_ref=land[0], dst_ref=land[1], send_sem=new[0], recv_sem=new[1],
                                     device_id=(x, y, 1 - c), device_id_type=pl.DeviceIdType.MESH).start()

    return _comm_step("grad_sibling_start", start, [shares, landing], n_new=2, token=True)


def _sibling_send_wait(sems, lands, after):
    def wait(src, land, waits, new):
        x, y, c = lax.axis_index("x"), lax.axis_index("y"), lax.axis_index("c")
        done = pltpu.make_async_remote_copy(src_ref=land[0], dst_ref=land[1], send_sem=waits[0], recv_sem=waits[1],
                                            device_id=(x, y, c), device_id_type=pl.DeviceIdType.MESH)
        done.wait_send()
        done.wait_recv()

    _, lands = _comm_step("grad_sibling_wait", wait, lands, wait_sems=sems, after=after)
    return lands[1]


def _row_tile(rows, limit=256):
    fits = [t for t in range(16, limit + 1, 16) if rows % t == 0]
    return fits[-1] if fits else rows


def _pair_sum(mine, theirs, name):
    nb, rows, cols = mine.shape
    tr = _row_tile(rows, 528)

    def body(a_ref, b_ref, o_ref):
        o_ref[...] = (a_ref[...].astype(f32) + b_ref[...].astype(f32)).astype(bf16)

    blk = pl.BlockSpec((None, tr, cols), lambda q, i: (q, i, 0))
    return pl.pallas_call(
        body,
        grid=(nb, rows // tr),
        in_specs=[blk, blk],
        out_specs=blk,
        out_shape=jax.ShapeDtypeStruct(mine.shape, bf16),
        compiler_params=_params(("arbitrary", "arbitrary")),
        name=name,
    )(mine, theirs)


def _scatter_start(chip_arrs, all_arrs, name):
    arrs = list(chip_arrs) + list(all_arrs)
    n, nc = len(arrs), len(chip_arrs)
    lands = [lax.empty(((3 if i < nc else NDEV - 1),) + a.shape[1:], a.dtype) for i, a in enumerate(arrs)]

    def body(*refs):
        src, land = refs[:n], refs[n:2 * n]
        send_sems, recv_sems = refs[2 * n:3 * n], refs[3 * n:4 * n]
        token = refs[6 * n]
        x, y, c = lax.axis_index("x"), lax.axis_index("y"), lax.axis_index("c")
        for a in range(n):
            for r in range(1, 4 if a < nc else NDEV):
                if a < nc:
                    px, py, pc = (1 - x if r & 2 else x), (1 - y if r & 1 else y), c
                    block = 2 * px + py
                else:
                    px, py, pc = (1 - x if r & 4 else x), (1 - y if r & 2 else y), (1 - c if r & 1 else c)
                    block = 4 * px + 2 * py + pc
                pltpu.make_async_remote_copy(
                    src_ref=src[a].at[block], dst_ref=land[a].at[r - 1], send_sem=send_sems[a],
                    recv_sem=recv_sems[a], device_id=(px, py, pc), device_id_type=pl.DeviceIdType.MESH).start()
        token[...] = jnp.zeros_like(token)

    hbm = [pltpu.HBM(a.shape, a.dtype) for a in arrs + lands]
    ops = [pltpu.with_memory_space_constraint(a, pltpu.HBM) for a in arrs + lands]
    outs = pl.pallas_call(
        body,
        out_shape=tuple([pltpu.SemaphoreType.DMA(())] * (2 * n) + hbm + [jax.ShapeDtypeStruct((8, LANES), f32)]),
        in_specs=[_HBM] * (2 * n),
        out_specs=tuple([_SEM] * (2 * n) + [_HBM] * (2 * n) + [pl.BlockSpec(memory_space=pltpu.VMEM)]),
        input_output_aliases={i: 2 * n + i for i in range(2 * n)},
        compiler_params=pltpu.CompilerParams(has_side_effects=_EFFECT),
        name=name,
    )(*ops)
    return outs[:n], outs[n:2 * n], outs[2 * n:3 * n], outs[3 * n:4 * n], outs[4 * n]


def _scatter_wait(send_sems, recv_sems, srcs, lands, after, name):
    n = len(srcs)

    def body(*refs):
        land = refs[n:2 * n]
        ssem, rsem = refs[2 * n:3 * n], refs[3 * n:4 * n]
        x, y, c = lax.axis_index("x"), lax.axis_index("y"), lax.axis_index("c")
        for a in range(n):
            done = pltpu.make_async_remote_copy(
                src_ref=land[a], dst_ref=land[a], send_sem=ssem[a], recv_sem=rsem[a], device_id=(x, y, c),
                device_id_type=pl.DeviceIdType.MESH)
            done.wait_send()
            done.wait_recv()

    hbm = [pltpu.HBM(a.shape, a.dtype) for a in list(srcs) + list(lands)]
    outs = pl.pallas_call(
        body,
        out_shape=tuple(hbm),
        in_specs=[_HBM] * (2 * n) + [_SEM] * (2 * n) + [pl.BlockSpec(memory_space=pl.ANY)],
        out_specs=tuple([_HBM] * (2 * n)),
        input_output_aliases={i: i for i in range(2 * n)},
        compiler_params=pltpu.CompilerParams(has_side_effects=_EFFECT),
        name=name,
    )(*srcs, *lands, *send_sems, *recv_sems, after)
    return outs[:n], outs[n:]


def _adam_update(g, w_ref, m_ref, v_ref, g_ref, d_ref, nm_ref, nv_ref):
    mm = ADAM_B1 * m_ref[...] + (1.0 - ADAM_B1) * g
    vv = ADAM_B2 * v_ref[...] + (1.0 - ADAM_B2) * (g * g)
    m_hat = mm / (1.0 - ADAM_B1 ** ADAM_STEP)
    v_hat = vv / (1.0 - ADAM_B2 ** ADAM_STEP)
    g_ref[...] = g
    d_ref[...] = -ADAM_LR * (m_hat / (jnp.sqrt(v_hat) + ADAM_EPS) + ADAM_WD * w_ref[...])
    nm_ref[...] = mm
    nv_ref[...] = vv


def _adamw_own(w, own, own_idx, slots, m, v, name):
    r, c = w.shape[-2:]
    tr = _row_tile(r, 384)
    k = slots.shape[0]

    def body(i_ref, w_ref, o_ref, s_ref, m_ref, v_ref, g_ref, d_ref, nm_ref, nv_ref):
        del i_ref
        g = o_ref[...].astype(f32)
        for j in range(k):
            g = g + s_ref[j].astype(f32)
        _adam_update(g, w_ref, m_ref, v_ref, g_ref, d_ref, nm_ref, nv_ref)

    blk = pl.BlockSpec((None, tr, c), lambda i, ix: (0, i, 0))
    return pl.pallas_call(
        body,
        grid_spec=pltpu.PrefetchScalarGridSpec(
            num_scalar_prefetch=1,
            grid=(r // tr,),
            in_specs=[blk, pl.BlockSpec((None, tr, c), lambda i, ix: (ix[0], i, 0)),
                      pl.BlockSpec((k, tr, c), lambda i, ix: (0, i, 0)), blk, blk],
            out_specs=[blk] * 4,
        ),
        out_shape=[jax.ShapeDtypeStruct(w.shape, f32)] * 4,
        compiler_params=_params(("arbitrary",)),
        name=name,
    )(own_idx, w, own, slots, m, v)


def _adamw(w, slots, m, v, name):
    r, c = w.shape[-2:]
    tr = _row_tile(r, 128)

    def body(w_ref, s_ref, m_ref, v_ref, g_ref, d_ref, nm_ref, nv_ref):
        g = s_ref[0].astype(f32)
        for k in range(1, NDEV):
            g = g + s_ref[k].astype(f32)
        _adam_update(g, w_ref, m_ref, v_ref, g_ref, d_ref, nm_ref, nv_ref)

    if w.ndim == 3:
        blk = pl.BlockSpec((None, tr, c), lambda i: (0, i, 0))
    else:
        blk = pl.BlockSpec((tr, c), lambda i: (i, 0))
    return pl.pallas_call(
        body,
        grid=(r // tr,),
        in_specs=[blk, pl.BlockSpec((NDEV, tr, c), lambda i: (0, i, 0)), blk, blk],
        out_specs=[blk] * 4,
        out_shape=[jax.ShapeDtypeStruct(w.shape, f32)] * 4,
        compiler_params=_params(("arbitrary",)),
        name=name,
    )(w, slots, m, v)


class _Weights:
    def __init__(self, w_t, w_a, w_b, b_merge, w_o):
        self._w_t, self._rest = w_t, (w_a, w_b, b_merge, w_o)

    def first_half(self, after):
        del after
        return self._w_t, jnp.zeros((1,), jnp.int32)

    def second_half(self, after):
        del after
        return self._w_t, jnp.ones((1,), jnp.int32)

    def rest(self):
        return self._rest


def _local_step(x, tgt, norm_gain, weights, qn_a, kn_a, qn_b, kn_b, sink_a, rel_bias, on_weight_grads=None,
                core=None):
    two = lambda t: jnp.concatenate([t, t], axis=-1).reshape(1, LANES)
    ones = jnp.ones((1, LANES), f32)
    gains = jnp.stack([
        jnp.stack([two(qn_a), two(kn_a), ones]),
        jnp.stack([two(qn_b), two(kn_b), ones]),
        jnp.stack([two(qn_b), two(kn_b), ones]),
        jnp.stack([two(qn_b), two(kn_b), ones]),
    ])
    buckets = [jnp.asarray(_bucket_np(blk, d)) for blk, d, _ in GROUPS]
    bias = [_bias_expand(rel_bias, buckets[k], GROUPS[k][2], "bias_expand_%d" % k) for k in range(4)]

    hb, hbt, rstd = _rms(x, norm_gain)
    w_t, half = weights.first_half([hb] + bias)
    proj = _inproj_half(hb, w_t, half, None, "inproj_1")
    w_t, half = weights.second_half([proj])
    proj = _inproj_half(hb, w_t, half, proj, "inproj_2")
    w_a, w_b, b_merge, w_o = weights.rest()
    gl = _prep(proj, gains)
    o_a, l_a = _attn_fwd(gl, bias[0], sink_a.reshape(8), 0, 128, 1, "attn_fwd_a")
    fwd_b = [_attn_fwd(gl, bias[k], None, k, GROUPS[k][0], GROUPS[k][1], "attn_fwd_b%d" % k) for k in (1, 2, 3)]
    sink_b = jnp.repeat(sink_a.reshape(8), HD).reshape(1, 512)

    (dy, dyb, dproj, do_a, dd_a, do_b0, do_b1, do_b2, dd_b0, dd_b1, dd_b2, ya, yb, mg, dbr_a, dbr_b, loss, dbm,
     dsk) = _tail(x, tgt, o_a, l_a, [f[0] for f in fwd_b], [f[1] for f in fwd_b], proj, b_merge, w_a, w_b, w_o, sink_b)

    dw_o = _matmul_tokens(mg, dyb, "dw_out")
    dw_a = _matmul_tokens(ya, dbr_a, "dw_branch_a")
    dw_b = _matmul_tokens(yb, dbr_b, "dw_branch_b")
    if on_weight_grads is not None:
        early = on_weight_grads(dict(w_branch_a=dw_a, w_branch_b=dw_b, b_merge=dbm, w_out=dw_o))
        buckets = [buckets[0] + early.astype(jnp.int32)] + buckets[1:]

    dqkv_a, dbk_a = _attn_bwd(gl, bias[0], buckets[0], do_a, l_a, dd_a, 0, 128, 1, "attn_bwd_a")
    dproj, dg_a = _post_a(dqkv_a, proj, gains, dproj)
    dbk_b, dg_b = [], []
    for k, do_k, dd_k in ((1, do_b0, dd_b0), (2, do_b1, dd_b1), (3, do_b2, dd_b2)):
        dqkv, dbk = _attn_bwd(gl, bias[k], buckets[k], do_k, fwd_b[k - 1][1], dd_k, k, GROUPS[k][0], GROUPS[k][1],
                              "attn_bwd_b%d" % k)
        dproj, dg = _post_b(k, dqkv, proj, gains, dproj)
        dbk_b.append(dbk)
        dg_b.append(dg)
    dg_b = jnp.stack(dg_b)

    core = jnp.zeros((1,), jnp.int32) if core is None else core
    dw_other = _dw_in(hbt, dproj, 1 - core, "dw_in_other")
    sent = jnp.zeros((), f32) if on_weight_grads is None else on_weight_grads(dict(w_in_other=dw_other))
    dw_in = _dw_in(hbt, dproj, core + sent.astype(jnp.int32), "dw_in_own")
    token = jnp.zeros((), f32) if on_weight_grads is None else on_weight_grads(dict(w_in=dw_in))
    grad_x, d_norm_gain = _dh_norm_bwd(dproj, w_t, x, rstd, norm_gain + token, dy)

    fold = lambda t: t[..., :HD] + t[..., HD:]
    d_qn_a = fold(dg_a[0, 0])
    d_kn_a = fold(dg_a[1, 0])
    d_qn_b = fold(dg_b[:, 0, 0].sum(axis=0))
    d_kn_b = fold(dg_b[:, 1, 0].sum(axis=0))
    d_sink = dsk.reshape(8, HD)[:, 0]
    red = jnp.stack([dbk_a] + dbk_b)
    d_rel = red[:, :, 0, :32].reshape(32, 32).T
    return dict(loss=loss, grad_x=grad_x, norm_gain=d_norm_gain, w_in=dw_in, w_in_other=dw_other, q_norm_a=d_qn_a,
                k_norm_a=d_kn_a,
                q_norm_b=d_qn_b, k_norm_b=d_kn_b, sink_a=d_sink, rel_bias=d_rel, w_branch_a=dw_a, w_branch_b=dw_b,
                b_merge=dbm, w_out=dw_o)


SMALL = (("norm_gain", D), ("q_norm_a", HD), ("k_norm_a", HD), ("q_norm_b", HD), ("k_norm_b", HD), ("sink_a", 8),
         ("rel_bias", 1024))
SMALL_PAD = 2432


SMALL_USED = sum(sz for _, sz in SMALL)


def _pack_small(parts, loss=None):
    tail = jnp.zeros((SMALL_PAD - SMALL_USED,), f32)
    if loss is not None:
        tail = tail.at[0].set(loss.reshape(()))
    return jnp.concatenate([parts[n].reshape(-1) for n, _ in SMALL] + [tail]).reshape(1, SMALL_PAD)


def _unpack_small(flat, shapes):
    out, off = {}, 0
    for n, sz in SMALL:
        out[n] = flat[0, off:off + sz].reshape(shapes[n])
        off += sz
    return out


def kernel(x, norm_gain, w_in, q_norm_a, k_norm_a, q_norm_b, k_norm_b, sink_a, rel_bias, w_branch_a, w_branch_b, b_merge, w_out, loss_target, m_norm_gain, m_w_in, m_q_norm_a, m_k_norm_a, m_q_norm_b, m_k_norm_b, m_sink_a, m_rel_bias, m_w_branch_a, m_w_branch_b, m_b_merge, m_w_out, v_norm_gain, v_w_in, v_q_norm_a, v_k_norm_a, v_q_norm_b, v_k_norm_b, v_sink_a, v_rel_bias, v_w_branch_a, v_w_branch_b, v_b_merge, v_w_out):
    csh = D // NDEV
    w_in_t, m_w_in_t, v_w_in_t = (jnp.swapaxes(t, 1, 2) for t in (w_in, m_w_in, v_w_in))
    weights = _GatheredWeights([w_in_t[0].astype(bf16), w_branch_a[0].astype(bf16), w_branch_b[0].astype(bf16),
                                w_out[0].astype(bf16), b_merge[0]])

    pending = {}
    core = lax.axis_index("c").astype(jnp.int32).reshape(1)
    chip = (2 * lax.axis_index("x") + lax.axis_index("y")).astype(jnp.int32).reshape(1)
    me = (2 * chip + core).astype(jnp.int32)

    def start_exchange(gw):
        if "w_in_other" in gw:
            sems, lands, sent = _sibling_send_start(gw["w_in_other"])
            pending["sibling"] = (sems, lands)
            return sent
        if "w_in" in gw:
            from_sibling = _sibling_send_wait(*pending["sibling"], after=[gw["w_in"]])
            chip_sums = _pair_sum(gw["w_in"], from_sibling, "grad_pair_sum")
            pending["w_in"] = _scatter_start([chip_sums], [], "scatter_w_in_start")
            return pending["w_in"][4][0, 0]
        blocks = [gw["w_branch_a"].reshape(512, NDEV, csh).transpose(1, 0, 2).astype(bf16),
                  gw["w_branch_b"].reshape(512, NDEV, csh).transpose(1, 0, 2).astype(bf16),
                  gw["w_out"].reshape(NDEV, csh, D).astype(bf16),
                  gw["b_merge"].reshape(2, NDEV, csh).transpose(1, 0, 2)]
        pending["rest"] = _scatter_start([], blocks, "scatter_rest_start")
        return pending["rest"][4][0, 0]

    loc = _local_step(x[0], loss_target[0], norm_gain, weights, q_norm_a, k_norm_a, q_norm_b, k_norm_b, sink_a,
                      rel_bias, on_weight_grads=start_exchange, core=core)

    small_shapes = dict(norm_gain=(1, D), q_norm_a=(1, HD), k_norm_a=(1, HD), q_norm_b=(1, HD), k_norm_b=(1, HD),
                        sink_a=(1, 8), rel_bias=(32, 32))
    (r_small,) = _exchange([], [_pack_small(loc, loc["loss"])], "gather_small_grads")
    send_sems, recv_sems, srcs, lands, _ = pending["rest"]
    (s_a, s_b, s_o, s_bm), (r_a, r_b, r_o, r_bm) = _scatter_wait(
        send_sems, recv_sems, srcs, lands, r_small, "scatter_rest_wait")
    send_sems, recv_sems, srcs, lands, _ = pending["w_in"]
    (s_in,), (r_in,) = _scatter_wait(send_sems, recv_sems, srcs, lands, r_small, "scatter_w_in_wait")

    given = dict(norm_gain=norm_gain, q_norm_a=q_norm_a, k_norm_a=k_norm_a, q_norm_b=q_norm_b, k_norm_b=k_norm_b,
                 sink_a=sink_a, rel_bias=rel_bias)
    m_small = dict(norm_gain=m_norm_gain, q_norm_a=m_q_norm_a, k_norm_a=m_k_norm_a, q_norm_b=m_q_norm_b,
                   k_norm_b=m_k_norm_b, sink_a=m_sink_a, rel_bias=m_rel_bias)
    v_small = dict(norm_gain=v_norm_gain, q_norm_a=v_q_norm_a, k_norm_a=v_k_norm_a, q_norm_b=v_q_norm_b,
                   k_norm_b=v_k_norm_b, sink_a=v_sink_a, rel_bias=v_rel_bias)
    res = {
        "small": _adamw(_pack_small(given), r_small, _pack_small(m_small), _pack_small(v_small), "adamw_small"),
        "w_in": [jnp.swapaxes(t, 1, 2) for t in
                 _adamw_own(w_in_t, s_in, chip, r_in, m_w_in_t, v_w_in_t, "adamw_w_in")],
        "w_branch_a": _adamw_own(w_branch_a, s_a, me, r_a, m_w_branch_a, v_w_branch_a, "adamw_w_branch_a"),
        "w_branch_b": _adamw_own(w_branch_b, s_b, me, r_b, m_w_branch_b, v_w_branch_b, "adamw_w_branch_b"),
        "b_merge": _adamw_own(b_merge, s_bm, me, r_bm, m_b_merge, v_b_merge, "adamw_b_merge"),
        "w_out": _adamw_own(w_out, s_o, me, r_o, m_w_out, v_w_out, "adamw_w_out"),
    }
    order = ["norm_gain", "w_in", "q_norm_a", "k_norm_a", "q_norm_b", "k_norm_b", "sink_a", "rel_bias", "w_branch_a",
             "w_branch_b", "b_merge", "w_out"]
    outs = []
    for k in range(4):
        small = _unpack_small(res["small"][k], small_shapes)
        for n in order:
            outs.append(small[n] if n in small else res[n][k])
    loss = res["small"][0][0, SMALL_USED]
    return (loss, loc["grad_x"][None], *outs)
```

```python
import math

import numpy as np
import jax
import jax.numpy as jnp
from jax import lax
from jax.experimental import pallas as pl
from jax.experimental.pallas import tpu as pltpu

f32 = jnp.float32
bf16 = jnp.bfloat16

S = 4096
D = 1024
NA = 5376
NT = 3072
NW = NA + NT
WSH = NW // 8
HD = 64
LANES = 128
EPS = 1e-6
NEG = -1e30
SCALE = HD ** -0.5
TQ = 128
PAD = 128
SP = S + 2 * PAD
NDEV = 8
GROUPS = ((128, 1, 0), (64, 1, 8), (64, 4, 16), (64, 16, 24))
CHUNK = 256
PCHUNK = 128
RC = 64

ADAM_LR, ADAM_B1, ADAM_B2, ADAM_EPS, ADAM_WD, ADAM_STEP = 0.001, 0.9, 0.999, 1e-08, 0.01, 10

MIB = 1024 * 1024
NT_DIMS = (((1,), (1,)), ((), ()))
TN_DIMS = (((0,), (0,)), ((), ()))


def _params(sem=None, vmem_mib=48):
    return pltpu.CompilerParams(dimension_semantics=sem, vmem_limit_bytes=vmem_mib * MIB)


def _lo():
    return lax.broadcasted_iota(jnp.int32, (1, LANES), 1) < HD


def _head_ones():
    r = lax.broadcasted_iota(jnp.int32, (LANES, LANES), 0) // HD
    c = lax.broadcasted_iota(jnp.int32, (LANES, LANES), 1) // HD
    return jnp.where(r == c, 1.0, 0.0).astype(bf16)


def _half_sums(x, ones):
    hi = x.astype(bf16)
    mid = (x - hi.astype(f32)).astype(bf16)
    return (jnp.dot(hi, ones, preferred_element_type=f32) + jnp.dot(mid, ones, preferred_element_type=f32))


def _seg_sum(x, ones):
    outs = [_half_sums(x[:, b * LANES:(b + 1) * LANES], ones) for b in range(x.shape[1] // LANES)]
    return outs[0] if len(outs) == 1 else jnp.concatenate(outs, axis=1)


def _bucket_np(blk, stride):
    w = TQ + 2 * blk
    rel = np.arange(w)[None, :] - blk - np.arange(TQ)[:, None]
    band = np.abs(rel) <= blk
    r = rel * stride
    n = np.abs(r)
    nf = np.maximum(n, 8).astype(np.float32)
    large = 8 + (np.log(nf / np.float32(8)) / np.float32(math.log(128.0)) * np.float32(8)).astype(np.int32)
    large = np.minimum(large, 15)
    b = (r > 0).astype(np.int32) * 16 + np.where(n < 8, n, large)
    return np.where(band, b, -1).astype(np.int32)


def _rms(x, gain):
    ts = 512

    def body(x_ref, g_ref, h_ref, ht_ref, r_ref):
        xv = x_ref[...]
        r = lax.rsqrt(jnp.mean(xv * xv, axis=-1, keepdims=True) + EPS)
        h = (xv * r) * g_ref[...]
        h_ref[...] = h.astype(bf16)
        ht_ref[...] = h.T.astype(bf16)
        r_ref[...] = r

    return pl.pallas_call(
        body,
        grid=(S // ts,),
        in_specs=[pl.BlockSpec((ts, D), lambda i: (i, 0)), pl.BlockSpec((1, D), lambda i: (0, 0))],
        out_specs=[pl.BlockSpec((ts, D), lambda i: (i, 0)), pl.BlockSpec((D, ts), lambda i: (0, i)),
                   pl.BlockSpec((ts, 1), lambda i: (i, 0))],
        out_shape=[jax.ShapeDtypeStruct((S, D), bf16), jax.ShapeDtypeStruct((D, S), bf16),
                   jax.ShapeDtypeStruct((S, 1), f32)],
        compiler_params=_params(("arbitrary",)),
        name="rms",
    )(x, gain)


def _inproj_half(hb, w_t, half, proj, name):
    ts = 512
    tn = NW // 2
    per = NW // 2 // tn

    def body(h_idx, h_ref, w_ref, *rest):
        del h_idx
        rest[-1][...] = lax.dot_general(h_ref[...], w_ref[...], NT_DIMS, preferred_element_type=f32)

    in_specs = [pl.BlockSpec((ts, D), lambda i, n, hf: (i, 0)),
                pl.BlockSpec((tn, D), lambda i, n, hf: (hf[0] * per + n, 0))]
    args = [half, hb, w_t]
    aliases = {}
    if proj is not None:
        in_specs.append(pl.BlockSpec(memory_space=pl.ANY))
        args.append(proj)
        aliases = {3: 0}
    return pl.pallas_call(
        body,
        grid_spec=pltpu.PrefetchScalarGridSpec(
            num_scalar_prefetch=1,
            grid=(S // ts, per),
            in_specs=in_specs,
            out_specs=pl.BlockSpec((ts, tn), lambda i, n, hf: (i, hf[0] * per + n)),
        ),
        out_shape=jax.ShapeDtypeStruct((S, NW), f32),
        input_output_aliases=aliases,
        compiler_params=_params(("arbitrary", "arbitrary")),
        name=name,
    )(*args)


def _bias_expand(table, bucket, c0, name):
    tq, w = bucket.shape
    blk = (w - tq) // 2

    def body(tab_ref, bk_ref, o_ref):
        h = pl.program_id(0)
        bk = bk_ref[...]

        def step(b, acc):
            return jnp.where(bk == b, tab_ref[b, c0 + h], acc)

        inner = lax.fori_loop(0, 32, step, jnp.full((tq, w), NEG, f32))
        col = lax.broadcasted_iota(jnp.int32, (1, w), 1)
        o_ref[0] = jnp.where(col < blk, NEG, inner)
        o_ref[1] = inner
        o_ref[2] = jnp.where(col >= tq + blk, NEG, inner)

    return pl.pallas_call(
        body,
        grid=(8,),
        in_specs=[pl.BlockSpec(memory_space=pltpu.SMEM), pl.BlockSpec((tq, w), lambda h: (0, 0))],
        out_specs=pl.BlockSpec((3, None, tq, w), lambda h: (0, h, 0, 0)),
        out_shape=jax.ShapeDtypeStruct((3, 8, tq, w), f32),
        compiler_params=_params(("arbitrary",)),
        name=name,
    )(table, bucket)


def _tile_kind(t, seq):
    m0 = jnp.bitwise_and(t * TQ, seq - 1)
    return jnp.where(m0 == 0, 0, jnp.where(m0 == seq - TQ, 2, 1))


def _col_block(g, j):
    kind = j // 4
    hp = j % 4
    a = jnp.where(kind == 0, hp, 3 + kind)
    b = 6 + 12 * kind + 4 * (g - 1) + hp
    return jnp.where(g == 0, a, b)


def _prep(proj_a, gains):
    def body(p0_ref, p1_ref, p2_ref, p3_ref, g_ref, o_ref):
        g = pl.program_id(0)
        kind = pl.program_id(1)
        lo = _lo()
        ones = _head_ones()
        half = jnp.where(lo, 0, 1)
        gain = g_ref[...]

        def norm_store(xv, u, dst, dup):
            if dup:
                take = (kind == 0) | (half == u // 2)
                xv = jnp.where(take, xv, pltpu.roll(xv, HD, 1))
            r = lax.rsqrt(_half_sums(xv * xv, ones) * (1.0 / HD) + EPS)
            r = jnp.where(kind == 2, 1.0, r)
            yv = (xv * r) * gain
            yv = jnp.where(kind == 0, yv * SCALE, yv)
            o_ref[u, PAD + dst:PAD + dst + CHUNK, :] = yv.astype(bf16)

        for u in range(4):
            o_ref[u, 0:PAD, :] = jnp.zeros((PAD, LANES), bf16)
            o_ref[u, PAD + S:SP, :] = jnp.zeros((PAD, LANES), bf16)
        for gi, (_, d, _) in enumerate(GROUPS):
            @pl.when(g == gi)
            def _():
                seq = S // d
                for u, p_ref in enumerate((p0_ref, p1_ref, p2_ref, p3_ref)):
                    for c in range(d):
                        for i in range(seq // CHUNK):
                            if d == 1:
                                xv = p_ref[i * CHUNK:(i + 1) * CHUNK, :]
                            else:
                                xv = p_ref[pl.ds(c + i * CHUNK * d, CHUNK, stride=d), :]
                            norm_store(xv, u, c * seq + i * CHUNK, gi == 0)

    return pl.pallas_call(
        body,
        grid=(4, 3),
        in_specs=[pl.BlockSpec((S, LANES), lambda g, kind, u=u: (0, _col_block(g, 4 * kind + u))) for u in range(4)] + [
            pl.BlockSpec((None, None, 1, LANES), lambda g, kind: (g, kind, 0, 0)),
        ],
        out_specs=pl.BlockSpec((None, 4, SP, LANES), lambda g, kind: (g, kind, 0, 0)),
        out_shape=jax.ShapeDtypeStruct((4, 12, SP, LANES), bf16),
        compiler_params=_params(("arbitrary", "arbitrary")),
        name="prep",
    )(proj_a, proj_a, proj_a, proj_a, gains)


def _token_rows(t, r0, n, d):
    if d == 1:
        return pl.ds(pl.multiple_of(t * TQ, TQ) + r0, n)
    per = S // d // TQ
    return pl.ds(((t % per) * TQ + r0) * d + t // per, n, stride=d)


def _stack_heads(t, lo):
    z = jnp.zeros_like(t)
    return jnp.concatenate([jnp.where(lo, t, z), jnp.where(lo, z, t)], axis=0)


def _unstack_heads(t2, lo):
    return jnp.where(lo, t2[:TQ], t2[TQ:])


def _attn_fwd(gl, bias, sink, g, blk, d, name):
    w = TQ + 2 * blk
    seq = S // d
    use_sink = sink is not None

    def body(*refs):
        if use_sink:
            sink_ref, q_ref, k_ref, v_ref, b_ref, o_ref, l_ref, s0, s1, p0, p1, lse_scr = refs
        else:
            q_ref, k_ref, v_ref, b_ref, o_ref, l_ref, s0, s1, p0, p1, lse_scr = refs
        hp = pl.program_id(0)
        lo = _lo()
        s_bufs, p_bufs = (s0, s1), (p0, p1)

        def scores(p, slot):
            for u in range(2):
                f0 = pl.multiple_of((2 * p + u) * TQ, TQ)
                q2 = _stack_heads(q_ref[pl.ds(PAD + f0, TQ), :], lo)
                kw = k_ref[pl.ds(PAD - blk + f0, w), :]
                s_bufs[slot][u] = lax.dot_general(q2, kw, NT_DIMS, preferred_element_type=f32)

        def softmax(p, slot):
            for u in range(2):
                t = 2 * p + u
                kind = _tile_kind(t, seq)
                for h in range(2):
                    for r in range(TQ // RC):
                        rows = slice(h * TQ + r * RC, h * TQ + (r + 1) * RC)
                        logit = s_bufs[slot][u, rows, :] + b_ref[kind, h, r * RC:(r + 1) * RC, :]
                        m = jnp.max(logit, axis=1, keepdims=True)
                        e = jnp.exp(logit - m)
                        lse = m + jnp.log(jnp.sum(e, axis=1, keepdims=True))
                        if use_sink:
                            sk = sink_ref[2 * hp + h]
                            mx = jnp.maximum(lse, sk)
                            lse = mx + jnp.log(jnp.exp(lse - mx) + jnp.exp(sk - mx))
                        p_bufs[slot][u, rows, :] = (e * jnp.exp(m - lse)).astype(bf16)
                        lse_scr[u, rows, :] = jnp.broadcast_to(lse, (RC, LANES))
                l_ref[_token_rows(t, 0, TQ, d), :] = jnp.where(lo, lse_scr[u, 0:TQ, :], lse_scr[u, TQ:2 * TQ, :])

        def values(p, slot):
            for u in range(2):
                t = 2 * p + u
                vw = v_ref[pl.ds(PAD - blk + pl.multiple_of(t * TQ, TQ), w), :]
                o2 = jnp.dot(p_bufs[slot][u], vw, preferred_element_type=f32)
                o_ref[_token_rows(t, 0, TQ, d), :] = _unstack_heads(o2, lo)

        npair = S // TQ // 2
        scores(0, 0)
        scores(1, 1)
        softmax(0, 0)

        def steady(k, carry):
            p = 2 * k + 2
            scores(p, 0)
            softmax(p - 1, 1)
            values(p - 2, 0)
            scores(p + 1, 1)
            softmax(p, 0)
            values(p - 1, 1)
            return carry

        lax.fori_loop(0, (npair - 2) // 2, steady, 0)
        softmax(npair - 1, 1)
        values(npair - 2, 0)
        values(npair - 1, 1)

    in_specs = [
        pl.BlockSpec((None, None, SP, LANES), lambda hp: (g, hp, 0, 0)),
        pl.BlockSpec((None, None, SP, LANES), lambda hp: (g, 4 + hp, 0, 0)),
        pl.BlockSpec((None, None, SP, LANES), lambda hp: (g, 8 + hp, 0, 0)),
        pl.BlockSpec((3, 2, TQ, w), lambda hp: (0, hp, 0, 0)),
    ]
    args = [gl, gl, gl, bias]
    if use_sink:
        in_specs = [pl.BlockSpec(memory_space=pltpu.SMEM)] + in_specs
        args = [sink] + args
    out = pl.BlockSpec((S, LANES), lambda hp: (0, hp))
    return pl.pallas_call(
        body,
        grid=(4,),
        in_specs=in_specs,
        out_specs=[out, out],
        out_shape=[jax.ShapeDtypeStruct((S, 4 * LANES), f32)] * 2,
        scratch_shapes=[pltpu.VMEM((2, 2 * TQ, w), f32), pltpu.VMEM((2, 2 * TQ, w), f32),
                        pltpu.VMEM((2, 2 * TQ, w), bf16), pltpu.VMEM((2, 2 * TQ, w), bf16),
                        pltpu.VMEM((2, 2 * TQ, LANES), f32)],
        compiler_params=_params(("arbitrary",)),
        name=name,
    )(*args)


def _attn_bwd(gl, bias, bucket, do, lse, dd, g, blk, d, name):
    w = TQ + 2 * blk
    seq = S // d

    def body(q_ref, k_ref, v_ref, b_ref, bk_ref, do_ref, l_ref, d_ref, dqkv_ref, dbk_ref,
             db_acc, s0, s1, dp0, dp1, pb0, pb1, ds0, ds1, dk_acc, dv_acc):
        lo = _lo()
        hi = jnp.logical_not(lo)
        dk_acc[...] = jnp.zeros((SP, LANES), f32)
        dv_acc[...] = jnp.zeros((SP, LANES), f32)
        db_acc[...] = jnp.zeros((2 * TQ, w), f32)
        s_bufs, dp_bufs, pb_bufs, ds_bufs = (s0, s1), (dp0, dp1), (pb0, pb1), (ds0, ds1)

        def stacked(t):
            f0 = pl.multiple_of(t * TQ, TQ)
            q2 = _stack_heads(q_ref[pl.ds(PAD + f0, TQ), :], lo)
            do2 = _stack_heads(do_ref[_token_rows(t, 0, TQ, d), :].astype(bf16), lo)
            return f0, q2, do2

        def scores(p, slot):
            for u in range(2):
                f0, q2, do2 = stacked(2 * p + u)
                win = pl.ds(PAD - blk + f0, w)
                s_bufs[slot][u] = lax.dot_general(q2, k_ref[win, :], NT_DIMS, preferred_element_type=f32)
                dp_bufs[slot][u] = lax.dot_general(do2, v_ref[win, :], NT_DIMS, preferred_element_type=f32)

        def grads(p, slot):
            for u in range(2):
                t = 2 * p + u
                kind = _tile_kind(t, seq)
                for h in range(2):
                    msk = lo if h == 0 else hi
                    for r in range(TQ // RC):
                        rows = slice(h * TQ + r * RC, h * TQ + (r + 1) * RC)
                        src = _token_rows(t, r * RC, RC, d)
                        lh = jnp.max(jnp.where(msk, l_ref[src, :], -jnp.inf), axis=1, keepdims=True)
                        dh = jnp.max(jnp.where(msk, d_ref[src, :], -jnp.inf), axis=1, keepdims=True)
                        logit = s_bufs[slot][u, rows, :] + b_ref[kind, h, r * RC:(r + 1) * RC, :]
                        pr = jnp.exp(logit - lh)
                        ds = pr * (dp_bufs[slot][u, rows, :] - dh)
                        db_acc[rows, :] += ds
                        pb_bufs[slot][u, rows, :] = pr.astype(bf16)
                        ds_bufs[slot][u, rows, :] = ds.astype(bf16)

        def accumulate(p, slot):
            for u in range(2):
                f0, q2, do2 = stacked(2 * p + u)
                win = pl.ds(PAD - blk + f0, w)
                dsb = ds_bufs[slot][u]
                dq2 = jnp.dot(dsb, k_ref[win, :], preferred_element_type=f32)
                dqkv_ref[0, pl.ds(PAD + f0, TQ), :] = _unstack_heads(dq2, lo).astype(bf16)
                dk_acc[win, :] += lax.dot_general(dsb, q2, TN_DIMS, preferred_element_type=f32)
                dv_acc[win, :] += lax.dot_general(pb_bufs[slot][u], do2, TN_DIMS, preferred_element_type=f32)

        npair = S // TQ // 2
        scores(0, 0)
        scores(1, 1)
        grads(0, 0)

        def steady(k, carry):
            p = 2 * k + 2
            scores(p, 0)
            grads(p - 1, 1)
            accumulate(p - 2, 0)
            scores(p + 1, 1)
            grads(p, 0)
            accumulate(p - 1, 1)
            return carry

        lax.fori_loop(0, (npair - 2) // 2, steady, 0)
        grads(npair - 1, 1)
        accumulate(npair - 2, 0)
        accumulate(npair - 1, 1)
        for i in range(SP // CHUNK):
            rows = slice(i * CHUNK, (i + 1) * CHUNK)
            dqkv_ref[1, rows, :] = dk_acc[rows, :].astype(bf16)
            dqkv_ref[2, rows, :] = dv_acc[rows, :].astype(bf16)

        bk = bk_ref[...]
        lane = lax.broadcasted_iota(jnp.int32, (8, LANES), 1)
        for h in range(2):
            db = db_acc[h * TQ:(h + 1) * TQ, :]
            acc = jnp.zeros((8, LANES), f32)
            for b in range(32):
                part = jnp.where(bk == b, db, 0.0).reshape(TQ // 8, 8, w).sum(axis=0)
                tot = jnp.sum(jnp.sum(part, axis=1, keepdims=True), axis=0, keepdims=True)
                acc = jnp.where(lane == b, tot, acc)
            dbk_ref[h] = acc

    def gcol(off):
        return pl.BlockSpec((None, None, SP, LANES), lambda hp: (g, off + hp, 0, 0))

    row = pl.BlockSpec((S, LANES), lambda hp: (0, hp))
    return pl.pallas_call(
        body,
        grid=(4,),
        in_specs=[gcol(0), gcol(4), gcol(8), pl.BlockSpec((3, 2, TQ, w), lambda hp: (0, hp, 0, 0)),
                  pl.BlockSpec((TQ, w), lambda hp: (0, 0)), row, row, row],
        out_specs=[pl.BlockSpec((3, None, SP, LANES), lambda hp: (0, hp, 0, 0)),
                   pl.BlockSpec((2, 8, LANES), lambda hp: (hp, 0, 0))],
        out_shape=[
            jax.ShapeDtypeStruct((3, 4, SP, LANES), bf16),
            jax.ShapeDtypeStruct((8, 8, LANES), f32),
        ],
        scratch_shapes=([pltpu.VMEM((2 * TQ, w), f32)] + [pltpu.VMEM((2, 2 * TQ, w), f32)] * 4
                        + [pltpu.VMEM((2, 2 * TQ, w), bf16)] * 4 + [pltpu.VMEM((SP, LANES), f32)] * 2),
        compiler_params=_params(("arbitrary",), vmem_mib=56),
        name=name,
    )(gl, gl, gl, bias, bucket, do, lse, dd)


def _sigmoid(z):
    return 1.0 / (1.0 + jnp.exp(-z))


def _tail(x, tgt, o_a, l_a, o_b, l_b, proj, bm, w_a, w_b, w_o, sink_b):
    ts = 256

    def body(x_ref, t_ref, oa_ref, la_ref, ob0_ref, ob1_ref, ob2_ref, lb0_ref, lb1_ref, lb2_ref,
             ga_ref, gb_ref, m0_ref, m1_ref, bm_ref, wa_ref, wb_ref, wo_ref, sk_ref,
             dy_ref, dyb_ref, dt_ref, doa_ref, dda_ref, dob0_ref, dob1_ref, dob2_ref, ddb0_ref, ddb1_ref, ddb2_ref,
             ya_ref, yb_ref, mg_ref, dbra_ref, dbrb_ref, loss_ref, dbm_ref, dsk_ref):
        i = pl.program_id(0)

        @pl.when(i == 0)
        def _():
            loss_ref[...] = jnp.zeros_like(loss_ref)
            dbm_ref[...] = jnp.zeros_like(dbm_ref)
            dsk_ref[...] = jnp.zeros_like(dsk_ref)

        ga = ga_ref[...]
        sa = _sigmoid(ga)
        silu_a = ga * sa
        oa = oa_ref[...]
        ya = oa * silu_a
        gb = gb_ref[...]
        sb = _sigmoid(gb)
        silu_b = gb * sb
        ob = [ob0_ref[...], ob1_ref[...], ob2_ref[...]]
        lb = [lb0_ref[...], lb1_ref[...], lb2_ref[...]]
        mx = jnp.maximum(jnp.maximum(lb[0], lb[1]), lb[2])
        ex = [jnp.exp(v - mx) for v in lb]
        den = ex[0] + ex[1] + ex[2]
        alpha = [e / den for e in ex]
        ybc = alpha[0] * ob[0] + alpha[1] * ob[1] + alpha[2] * ob[2]
        yb = ybc * silu_b
        yab = ya.astype(bf16)
        ybb = yb.astype(bf16)
        br_a = jnp.dot(yab, wa_ref[...], preferred_element_type=f32)
        br_b = jnp.dot(ybb, wb_ref[...], preferred_element_type=f32)
        g0 = _sigmoid(m0_ref[...] + bm_ref[0:1, :])
        g1 = _sigmoid(m1_ref[...] + bm_ref[1:2, :])
        merged = g0 * br_a + g1 * br_b
        mgb = merged.astype(bf16)
        y = x_ref[...] + jnp.dot(mgb, wo_ref[...], preferred_element_type=f32)
        err = y - t_ref[...]
        part = jnp.sum(jnp.sum(err * err, axis=1, keepdims=True), axis=0, keepdims=True)
        loss_ref[...] += part * (0.5 / D)
        dy = err * (1.0 / D)
        dyb = dy.astype(bf16)
        dmerged = lax.dot_general(dyb, wo_ref[...], NT_DIMS, preferred_element_type=f32)
        dbr_a = (dmerged * g0).astype(bf16)
        dbr_b = (dmerged * g1).astype(bf16)
        dm0 = dmerged * br_a * (g0 * (1.0 - g0))
        dm1 = dmerged * br_b * (g1 * (1.0 - g1))
        dbm_ref[0:1, :] += jnp.sum(dm0, axis=0, keepdims=True)
        dbm_ref[1:2, :] += jnp.sum(dm1, axis=0, keepdims=True)
        dya = lax.dot_general(dbr_a, wa_ref[...], NT_DIMS, preferred_element_type=f32)
        dyb2 = lax.dot_general(dbr_b, wb_ref[...], NT_DIMS, preferred_element_type=f32)
        do_a = dya * silu_a
        dga = dya * oa * (sa * (1.0 + ga * (1.0 - sa)))
        ones = _head_ones()
        delta_a = _seg_sum(do_a * oa, ones)
        dsk_ref[...] -= jnp.sum(delta_a * jnp.exp(sk_ref[...] - la_ref[...]), axis=0, keepdims=True)
        dybc = dyb2 * silu_b
        dgb = dyb2 * ybc * (sb * (1.0 + gb * (1.0 - sb)))
        dbar = _seg_sum(dybc * ybc, ones)
        dy_ref[...] = dy
        dyb_ref[...] = dyb
        dt_ref[:, 0:512] = dga.astype(bf16)
        dt_ref[:, 512:1024] = dgb.astype(bf16)
        dt_ref[:, 1024:2048] = dm0.astype(bf16)
        dt_ref[:, 2048:3072] = dm1.astype(bf16)
        doa_ref[...] = do_a.astype(bf16)
        dda_ref[...] = delta_a
        for k, (dob_ref, ddb_ref) in enumerate(((dob0_ref, ddb0_ref), (dob1_ref, ddb1_ref), (dob2_ref, ddb2_ref))):
            dob_ref[...] = alpha[k] * dybc
            ddb_ref[...] = alpha[k] * dbar
        ya_ref[...] = ya.T.astype(bf16)
        yb_ref[...] = yb.T.astype(bf16)
        mg_ref[...] = merged.T.astype(bf16)
        dbra_ref[...] = dbr_a
        dbrb_ref[...] = dbr_b

    def rows(n, blk=0):
        return pl.BlockSpec((ts, n), lambda i: (i, blk))

    def whole(r, c):
        return pl.BlockSpec((r, c), lambda i: (0, 0))

    def cols(n):
        return pl.BlockSpec((n, ts), lambda i: (0, i))

    def gate_cols(n, col):
        return pl.BlockSpec((pl.Element(ts), pl.Element(n)), lambda i: (i * ts, NA + col))

    outs = [
        ((S, D), f32, rows(D)), ((S, D), bf16, rows(D)), ((S, NW), bf16, gate_cols(NT, 0)),
        ((S, 512), bf16, rows(512)), ((S, 512), f32, rows(512)),
        ((S, 512), f32, rows(512)), ((S, 512), f32, rows(512)), ((S, 512), f32, rows(512)),
        ((S, 512), f32, rows(512)), ((S, 512), f32, rows(512)), ((S, 512), f32, rows(512)),
        ((512, S), bf16, cols(512)), ((512, S), bf16, cols(512)), ((D, S), bf16, cols(D)),
        ((S, D), bf16, rows(D)), ((S, D), bf16, rows(D)),
        ((1, 1), f32, whole(1, 1)), ((2, D), f32, whole(2, D)), ((1, 512), f32, whole(1, 512)),
    ]
    return pl.pallas_call(
        body,
        grid=(S // ts,),
        in_specs=[
            rows(D), rows(D), rows(512), rows(512), rows(512), rows(512), rows(512), rows(512), rows(512), rows(512),
            gate_cols(512, 0), gate_cols(512, 512), gate_cols(D, 1024), gate_cols(D, 2048), whole(2, D),
            whole(512, D), whole(512, D), whole(D, D), whole(1, 512),
        ],
        out_specs=[o[2] for o in outs],
        out_shape=[jax.ShapeDtypeStruct(o[0], o[1]) for o in outs],
        compiler_params=_params(("arbitrary",), vmem_mib=60),
        name="tail",
    )(x, tgt, o_a, l_a, *o_b, *l_b, proj, proj, proj, proj, bm, w_a, w_b, w_o, sink_b)


def _norm_bwd(xv, dyv, gain, ones):
    r = lax.rsqrt(_half_sums(xv * xv, ones) * (1.0 / HD) + EPS)
    yv = xv * r
    u = dyv * gain
    dxv = r * (u - yv * (_half_sums(u * yv, ones) * (1.0 / HD)))
    return dxv, jnp.sum(dyv * yv, axis=0, keepdims=True)


def _post_b(g, dqkv, proj_a, gains, dproj):
    d = GROUPS[g][1]
    seq = S // d

    def body(d_ref, pa_ref, pb_ref, g_ref, alias_ref, o_ref, dg_ref, nat_a, nat_b):
        del alias_ref
        pj = pl.program_id(0)
        kind = pj // 2
        q_scale = jnp.where(kind == 0, SCALE, 1.0)
        gain = g_ref[...] * q_scale
        ones = _head_ones()

        @pl.when(pj % 2 == 0)
        def _():
            dg_ref[...] = jnp.zeros_like(dg_ref)

        def columns(with_norm):
            for u, (p_ref, nat) in enumerate(((pa_ref, nat_a), (pb_ref, nat_b))):
                for c in range(d):
                    for i in range(seq // PCHUNK):
                        src = c * seq + i * PCHUNK
                        if d == 1:
                            idx = slice(src, src + PCHUNK)
                        else:
                            idx = pl.ds(c + i * PCHUNK * d, PCHUNK, stride=d)
                        dyv = d_ref[u, PAD + src:PAD + src + PCHUNK, :].astype(f32)
                        if with_norm:
                            dyv, dg = _norm_bwd(p_ref[idx, :], dyv, gain, ones)
                            dg_ref[...] += dg * q_scale
                        nat[idx, :] = dyv
                for i in range(S // CHUNK):
                    rows = slice(i * CHUNK, (i + 1) * CHUNK)
                    o_ref[rows, u * LANES:(u + 1) * LANES] = nat[rows, :].astype(bf16)

        pl.when(kind < 2)(lambda: columns(True))
        pl.when(kind == 2)(lambda: columns(False))

    def pcol(u):
        return pl.BlockSpec((S, LANES), lambda pj: (0, _col_block(g, 2 * jnp.minimum(pj, 3) + u)))

    return pl.pallas_call(
        body,
        grid=(6,),
        in_specs=[
            pl.BlockSpec((None, 2, SP, LANES), lambda pj: (pj // 2, pj % 2, 0, 0)),
            pcol(0), pcol(1),
            pl.BlockSpec((None, None, 1, LANES), lambda pj: (g, pj // 2, 0, 0)),
            pl.BlockSpec(memory_space=pl.ANY),
        ],
        out_specs=[
            pl.BlockSpec((S, 2 * LANES), lambda pj: (0, _col_block(g, 2 * pj) // 2)),
            pl.BlockSpec((None, 1, LANES), lambda pj: (pj // 2, 0, 0)),
        ],
        out_shape=[jax.ShapeDtypeStruct((S, NW), bf16), jax.ShapeDtypeStruct((3, 1, LANES), f32)],
        scratch_shapes=[pltpu.VMEM((S, LANES), f32), pltpu.VMEM((S, LANES), f32)],
        input_output_aliases={4: 0},
        compiler_params=_params(("arbitrary",)),
        name="post_b%d" % g,
    )(dqkv, proj_a, proj_a, gains, dproj)


def _post_a(dqkv, proj_a, gains, dproj):
    def body(q_ref, e_ref, p_ref, g_ref, alias_ref, o_ref, dg_ref):
        del alias_ref
        j = pl.program_id(0)
        q_scale = jnp.where(j < 4, SCALE, 1.0)
        gain = g_ref[...] * q_scale
        lo = _lo()
        ones = _head_ones()

        @pl.when((j == 0) | (j >= 4))
        def _():
            dg_ref[...] = jnp.zeros_like(dg_ref)

        def column(folded, with_norm):
            for i in range(S // PCHUNK):
                r0 = i * PCHUNK
                rows = slice(PAD + r0, PAD + r0 + PCHUNK)
                if folded:
                    t0 = e_ref[0, rows, :].astype(f32) + e_ref[1, rows, :].astype(f32)
                    t1 = e_ref[2, rows, :].astype(f32) + e_ref[3, rows, :].astype(f32)
                    dyv = jnp.where(lo, t0 + pltpu.roll(t0, HD, 1), t1 + pltpu.roll(t1, HD, 1))
                else:
                    dyv = q_ref[rows, :].astype(f32)
                if with_norm:
                    dyv, dg = _norm_bwd(p_ref[r0:r0 + PCHUNK, :], dyv, gain, ones)
                    dg_ref[...] += dg * q_scale
                o_ref[r0:r0 + PCHUNK, :] = dyv.astype(bf16)

        pl.when(j < 4)(lambda: column(False, True))
        pl.when(j == 4)(lambda: column(True, True))
        pl.when(j == 5)(lambda: column(True, False))

    return pl.pallas_call(
        body,
        grid=(6,),
        in_specs=[
            pl.BlockSpec((None, None, SP, LANES), lambda j: (0, jnp.minimum(j, 3), 0, 0)),
            pl.BlockSpec((None, 4, SP, LANES), lambda j: (jnp.clip(j - 3, 1, 2), 0, 0, 0)),
            pl.BlockSpec((S, LANES), lambda j: (0, jnp.minimum(j, 4))),
            pl.BlockSpec((None, None, 1, LANES), lambda j: (0, jnp.maximum(j - 3, 0), 0, 0)),
            pl.BlockSpec(memory_space=pl.ANY),
        ],
        out_specs=[
            pl.BlockSpec((S, LANES), lambda j: (0, j)),
            pl.BlockSpec((None, 1, LANES), lambda j: (jnp.maximum(j - 3, 0), 0, 0)),
        ],
        out_shape=[jax.ShapeDtypeStruct((S, NW), bf16), jax.ShapeDtypeStruct((3, 1, LANES), f32)],
        input_output_aliases={4: 0},
        compiler_params=_params(("arbitrary",)),
        name="post_a",
    )(dqkv, dqkv, proj_a, gains, dproj)


def _dh_norm_bwd(dproj, w, x, rstd, gain, dy):
    ts = 1024
    tk = NW // 6
    nk = NW // tk

    def body(d_ref, w_ref, x_ref, r_ref, g_ref, dy_ref, gx_ref, dgn_ref, acc):
        i = pl.program_id(0)
        k = pl.program_id(1)

        @pl.when((i == 0) & (k == 0))
        def _():
            dgn_ref[...] = jnp.zeros_like(dgn_ref)

        @pl.when(k == 0)
        def _():
            acc[...] = jnp.zeros_like(acc)

        acc[...] += jnp.dot(d_ref[...], w_ref[...], preferred_element_type=f32)

        @pl.when(k == nk - 1)
        def _():
            dh = acc[...]
            xh = x_ref[...] * r_ref[...]
            u = dh * g_ref[...]
            dx = r_ref[...] * (u - xh * jnp.mean(u * xh, axis=-1, keepdims=True))
            gx_ref[...] = dy_ref[...] + dx
            dgn_ref[...] += jnp.sum(dh * xh, axis=0, keepdims=True)

    return pl.pallas_call(
        body,
        grid=(S // ts, nk),
        in_specs=[
            pl.BlockSpec((ts, tk), lambda i, k: (i, k)),
            pl.BlockSpec((tk, D), lambda i, k: (k, 0)),
            pl.BlockSpec((ts, D), lambda i, k: (i, 0)),
            pl.BlockSpec((ts, 1), lambda i, k: (i, 0)),
            pl.BlockSpec((1, D), lambda i, k: (0, 0)),
            pl.BlockSpec((ts, D), lambda i, k: (i, 0)),
        ],
        out_specs=[pl.BlockSpec((ts, D), lambda i, k: (i, 0)), pl.BlockSpec((1, D), lambda i, k: (0, 0))],
        out_shape=[jax.ShapeDtypeStruct((S, D), f32), jax.ShapeDtypeStruct((1, D), f32)],
        scratch_shapes=[pltpu.VMEM((ts, D), f32)],
        compiler_params=_params(("arbitrary", "arbitrary"), vmem_mib=56),
        name="dh_norm_bwd",
    )(dproj, w, x, rstd, gain, dy)


def _dw_in(hbt, dproj, parity, name):
    tk = 1024
    win = WSH + 96

    def body(par_ref, a_ref, b_ref, o_ref, acc):
        p = 2 * pl.program_id(0) + par_ref[0]
        k = pl.program_id(1)

        @pl.when(k == 0)
        def _():
            acc[...] = jnp.zeros_like(acc)

        acc[...] += jnp.dot(a_ref[...], b_ref[...], preferred_element_type=f32)

        @pl.when(k == S // tk - 1)
        def _():
            acc_t = acc[...].T
            for pp in range(NDEV):
                off = (WSH * pp) % LANES

                @pl.when(p == pp)
                def _():
                    o_ref[...] = acc_t[off:off + WSH, :].astype(bf16)

    return pl.pallas_call(
        body,
        grid_spec=pltpu.PrefetchScalarGridSpec(
            num_scalar_prefetch=1,
            grid=(NDEV // 2, S // tk),
            in_specs=[
                pl.BlockSpec((D, tk), lambda q, k, par: (0, k)),
                pl.BlockSpec((pl.Element(tk), pl.Element(win)),
                             lambda q, k, par: (k * tk, (WSH * (2 * q + par[0])) // LANES * LANES)),
            ],
            out_specs=pl.BlockSpec((None, WSH, D), lambda q, k, par: (q, 0, 0)),
            scratch_shapes=[pltpu.VMEM((D, win), f32)],
        ),
        out_shape=jax.ShapeDtypeStruct((NDEV // 2, WSH, D), bf16),
        compiler_params=_params(("arbitrary", "arbitrary")),
        name=name,
    )(parity, hbt, dproj)


def _matmul_tokens(at, b, name):
    m, n = at.shape[0], b.shape[1]
    tn = 512
    tk = 1024

    def body(a_ref, b_ref, o_ref):
        @pl.when(pl.program_id(1) == 0)
        def _():
            o_ref[...] = jnp.zeros_like(o_ref)

        o_ref[...] += jnp.dot(a_ref[...], b_ref[...], preferred_element_type=f32)

    return pl.pallas_call(
        body,
        grid=(n // tn, S // tk),
        in_specs=[pl.BlockSpec((m, tk), lambda j, k: (0, k)), pl.BlockSpec((tk, tn), lambda j, k: (k, j))],
        out_specs=pl.BlockSpec((m, tn), lambda j, k: (0, j)),
        out_shape=jax.ShapeDtypeStruct((m, n), f32),
        compiler_params=_params(("arbitrary", "arbitrary")),
        name=name,
    )(at, b)


def _exchange(scatter, gather, name):
    arrs = list(scatter) + list(gather)
    n = len(arrs)
    ns = len(scatter)

    def body(*refs):
        ins, outs = refs[:n], refs[n:2 * n]
        send_sems, recv_sems, local_sems = refs[2 * n:]
        x, y, c = lax.axis_index("x"), lax.axis_index("y"), lax.axis_index("c")
        me = 4 * x + 2 * y + c
        local, remote = [], []
        for a in range(n):
            lc = pltpu.make_async_copy(ins[a].at[me] if a < ns else ins[a], outs[a].at[me], local_sems.at[a])
            lc.start()
            local.append(lc)
            for r in range(1, NDEV):
                px = 1 - x if r & 4 else x
                py = 1 - y if r & 2 else y
                pc = 1 - c if r & 1 else c
                cp = pltpu.make_async_remote_copy(
                    src_ref=ins[a].at[4 * px + 2 * py + pc] if a < ns else ins[a],
                    dst_ref=outs[a].at[me],
                    send_sem=send_sems.at[a, r - 1],
                    recv_sem=recv_sems.at[a, r - 1],
                    device_id=(px, py, pc),
                    device_id_type=pl.DeviceIdType.MESH,
                )
                cp.start()
                remote.append(cp)
        for cp in remote:
            cp.wait_recv()
        for cp in remote:
            cp.wait_send()
        for lc in local:
            lc.wait()

    out_shape = [jax.ShapeDtypeStruct(a.shape if i < ns else (NDEV,) + a.shape, a.dtype) for i, a in enumerate(arrs)]
    return pl.pallas_call(
        body,
        in_specs=[pl.BlockSpec(memory_space=pl.ANY)] * n,
        out_specs=[pl.BlockSpec(memory_space=pl.ANY)] * n,
        out_shape=out_shape,
        scratch_shapes=[
            pltpu.SemaphoreType.DMA((n, NDEV - 1)),
            pltpu.SemaphoreType.DMA((n, NDEV - 1)),
            pltpu.SemaphoreType.DMA((n,)),
        ],
        compiler_params=pltpu.CompilerParams(has_side_effects=True),
        name=name,
    )(*arrs)


_HBM = pl.BlockSpec(memory_space=pltpu.HBM)
_SEM = pl.BlockSpec(memory_space=pltpu.SEMAPHORE)
_EFFECT = pltpu.SideEffectType.DATAFLOW_SIDE_EFFECTING


def _comm_step(name, body_fn, lands, srcs=(), wait_sems=(), n_new=0, after=(), token=False):
    n, ns, nw, na = len(lands), len(srcs), len(wait_sems), len(after)

    def body(*refs):
        src, land = refs[:ns], refs[ns:ns + n]
        waits = refs[ns + n:ns + n + nw]
        new = refs[ns + n + nw + na:ns + n + nw + na + n_new]
        body_fn(src, land, waits, new)
        if token:
            refs[-1][...] = jnp.zeros((8, LANES), f32)

    hbm = [pltpu.HBM(a.shape, a.dtype) for a in lands]
    ops = [pltpu.with_memory_space_constraint(a, pltpu.HBM) for a in list(srcs) + list(lands)]
    extra_shape = [jax.ShapeDtypeStruct((8, LANES), f32)] if token else []
    extra_spec = [pl.BlockSpec(memory_space=pltpu.VMEM)] if token else []
    outs = pl.pallas_call(
        body,
        out_shape=tuple([pltpu.SemaphoreType.DMA(())] * n_new + hbm + extra_shape),
        in_specs=[_HBM] * (ns + n) + [_SEM] * nw + [pl.BlockSpec(memory_space=pl.ANY)] * na,
        out_specs=tuple([_SEM] * n_new + [_HBM] * n + extra_spec),
        input_output_aliases={ns + i: n_new + i for i in range(n)},
        compiler_params=pltpu.CompilerParams(has_side_effects=_EFFECT),
        name=name,
    )(*ops, *wait_sems, *after)
    if token:
        return list(outs[:n_new]), list(outs[n_new:n_new + n]), outs[-1][0, 0]
    return list(outs[:n_new]), list(outs[n_new:])


class _GatheredWeights:
    def __init__(self, shards):
        self.n = n = len(shards)
        x, y, c = lax.axis_index("x"), lax.axis_index("y"), lax.axis_index("c")
        self.x = x
        me = 4 * x + 2 * y + c
        lands = [lax.dynamic_update_slice(lax.empty((NDEV,) + s.shape, s.dtype), s[None], (me,) + (0,) * s.ndim)
                 for s in shards]

        def start_own(src, land, waits, new):
            p = self._peers()
            for a in range(n):
                for k, to in ((0, p["sibling"]), (1, p["xn"]), (2, p["yn"])):
                    self._copy(land[a], new, a, k, 3, p["me"], to).start()

        self.sems, self.lands = {}, None
        new, self.lands = _comm_step("gather_start", start_own, lands, n_new=6 * n)
        self._keep(new, (0, 1, 2))

    @staticmethod
    def _peers():
        x, y, c = lax.axis_index("x"), lax.axis_index("y"), lax.axis_index("c")
        return dict(
            me=(x, y, c), sibling=(x, y, 1 - c), xn=(1 - x, y, c), yn=(x, 1 - y, c), dg=(1 - x, 1 - y, c),
            relay_origin=(jnp.bitwise_xor(x, c), jnp.bitwise_xor(y, 1 - c), c),
            relay_target=(jnp.bitwise_xor(x, 1 - c), jnp.bitwise_xor(y, c), c))

    def _keep(self, new, ks):
        half = len(new) // 2
        i = 0
        for a in range(self.n):
            for k in ks:
                self.sems[a, k] = (new[i], new[half + i])
                i += 1

    @staticmethod
    def _copy(land, sem_refs, a, k, nk, block, to, src=None, ks=None):
        ks = tuple(range(nk)) if ks is None else ks
        half = len(sem_refs) // 2
        i = a * len(ks) + ks.index(k)
        slot = land.at[4 * block[0] + 2 * block[1] + block[2]]
        return pltpu.make_async_remote_copy(
            src_ref=slot if src is None else src, dst_ref=slot, send_sem=sem_refs[i], recv_sem=sem_refs[half + i],
            device_id=to, device_id_type=pl.DeviceIdType.MESH)

    def _sem_list(self, ks):
        return ([self.sems[a, k][0] for a in range(self.n) for k in ks]
                + [self.sems[a, k][1] for a in range(self.n) for k in ks])

    def first_half(self, after):
        n = self.n

        def relay(src, land, waits, new):
            p = self._peers()
            for a in range(n):
                self._copy(land[a], waits, a, 1, 0, p["xn"], p["me"], ks=(1, 2)).wait_recv()
                self._copy(land[a], waits, a, 2, 0, p["yn"], p["me"], ks=(1, 2)).wait_recv()
                self._copy(land[a], new, a, 3, 0, p["relay_origin"], p["relay_target"], ks=(3, 4, 5)).start()
                self._copy(land[a], new, a, 4, 0, p["xn"], p["sibling"], ks=(3, 4, 5)).start()
                self._copy(land[a], new, a, 5, 0, p["yn"], p["sibling"], ks=(3, 4, 5)).start()

        new, self.lands = _comm_step("gather_relay", relay, self.lands, wait_sems=self._sem_list((1, 2)),
                                     n_new=6 * n, after=after)
        self._keep(new, (3, 4, 5))

        def from_sibling(src, land, waits, new):
            p = self._peers()
            other = lambda b: (b[0], b[1], 1 - b[2])
            for a in range(n):
                self._copy(land[a], waits, a, 0, 0, other(p["me"]), p["me"], ks=(0, 4, 5)).wait_recv()
                self._copy(land[a], waits, a, 4, 0, other(p["xn"]), p["me"], ks=(0, 4, 5)).wait_recv()
                self._copy(land[a], waits, a, 5, 0, other(p["yn"]), p["me"], ks=(0, 4, 5)).wait_recv()

        _, self.lands = _comm_step("gather_wait_sibling", from_sibling, self.lands,
                                   wait_sems=self._sem_list((0, 4, 5)))
        return self.lands[0].reshape(NW, D), self.x.astype(jnp.int32).reshape(1)

    def second_half(self, after):
        n = self.n

        def forward_diagonal(src, land, waits, new):
            p = self._peers()
            for a in range(n):
                self._copy(land[a], waits, a, 3, 0, p["dg"], p["me"], ks=(3,)).wait_recv()
                self._copy(land[a], new, a, 6, 0, p["dg"], p["sibling"], ks=(6,)).start()

        new, self.lands = _comm_step("gather_forward_diagonal", forward_diagonal, self.lands,
                                     wait_sems=self._sem_list((3,)), n_new=2 * n, after=after)
        self._keep(new, (6,))

        def finish(src, land, waits, new):
            p = self._peers()
            ks = tuple(range(7))
            for a in range(n):
                self._copy(land[a], waits, a, 6, 0, (p["dg"][0], p["dg"][1], 1 - p["dg"][2]), p["me"], ks=ks).wait_recv()
                for k in ks:
                    self._copy(land[a], waits, a, k, 0, p["me"], p["me"], ks=ks).wait_send()

        _, self.lands = _comm_step("gather_finish", finish, self.lands, wait_sems=self._sem_list(tuple(range(7))))
        return self.lands[0].reshape(NW, D), (1 - self.x).astype(jnp.int32).reshape(1)

    def rest(self):
        g_a, g_b, g_o, g_bm = self.lands[1:]
        return (g_a.transpose(1, 0, 2).reshape(512, D), g_b.transpose(1, 0, 2).reshape(512, D),
                g_bm.transpose(1, 0, 2).reshape(2, D), g_o.reshape(D, D))


def _sibling_send_start(shares):
    landing = lax.empty(shares.shape, shares.dtype)

    def start(src, land, waits, new):
        x, y, c = lax.axis_index("x"), lax.axis_index("y"), lax.axis_index("c")
        pltpu.make_async_remote_copy(src_ref=land[0], dst_ref=land[1], send_sem=new[0], recv_sem=new[1],
                                     device_id=(x, y, 1 - c), device_id_type=pl.DeviceIdType.MESH).start()

    return _comm_step("grad_sibling_start", start, [shares, landing], n_new=2, token=True)


def _sibling_send_wait(sems, lands, after):
    def wait(src, land, waits, new):
        x, y, c = lax.axis_index("x"), lax.axis_index("y"), lax.axis_index("c")
        done = pltpu.make_async_remote_copy(src_ref=land[0], dst_ref=land[1], send_sem=waits[0], recv_sem=waits[1],
                                            device_id=(x, y, c), device_id_type=pl.DeviceIdType.MESH)
        done.wait_send()
        done.wait_recv()

    _, lands = _comm_step("grad_sibling_wait", wait, lands, wait_sems=sems, after=after)
    return lands[1]


def _row_tile(rows, limit=256):
    fits = [t for t in range(16, limit + 1, 16) if rows % t == 0]
    return fits[-1] if fits else rows


def _pair_sum(mine, theirs, name):
    nb, rows, cols = mine.shape
    tr = _row_tile(rows, 528)

    def body(a_ref, b_ref, o_ref):
        o_ref[...] = (a_ref[...].astype(f32) + b_ref[...].astype(f32)).astype(bf16)

    blk = pl.BlockSpec((None, tr, cols), lambda q, i: (q, i, 0))
    return pl.pallas_call(
        body,
        grid=(nb, rows // tr),
        in_specs=[blk, blk],
        out_specs=blk,
        out_shape=jax.ShapeDtypeStruct(mine.shape, bf16),
        compiler_params=_params(("arbitrary", "arbitrary")),
        name=name,
    )(mine, theirs)


def _scatter_start(chip_arrs, all_arrs, name):
    arrs = list(chip_arrs) + list(all_arrs)
    n, nc = len(arrs), len(chip_arrs)
    lands = [lax.empty(((3 if i < nc else NDEV - 1),) + a.shape[1:], a.dtype) for i, a in enumerate(arrs)]

    def body(*refs):
        src, land = refs[:n], refs[n:2 * n]
        send_sems, recv_sems = refs[2 * n:3 * n], refs[3 * n:4 * n]
        token = refs[6 * n]
        x, y, c = lax.axis_index("x"), lax.axis_index("y"), lax.axis_index("c")
        for a in range(n):
            for r in range(1, 4 if a < nc else NDEV):
                if a < nc:
                    px, py, pc = (1 - x if r & 2 else x), (1 - y if r & 1 else y), c
                    block = 2 * px + py
                else:
                    px, py, pc = (1 - x if r & 4 else x), (1 - y if r & 2 else y), (1 - c if r & 1 else c)
                    block = 4 * px + 2 * py + pc
                pltpu.make_async_remote_copy(
                    src_ref=src[a].at[block], dst_ref=land[a].at[r - 1], send_sem=send_sems[a],
                    recv_sem=recv_sems[a], device_id=(px, py, pc), device_id_type=pl.DeviceIdType.MESH).start()
        token[...] = jnp.zeros_like(token)

    hbm = [pltpu.HBM(a.shape, a.dtype) for a in arrs + lands]
    ops = [pltpu.with_memory_space_constraint(a, pltpu.HBM) for a in arrs + lands]
    outs = pl.pallas_call(
        body,
        out_shape=tuple([pltpu.SemaphoreType.DMA(())] * (2 * n) + hbm + [jax.ShapeDtypeStruct((8, LANES), f32)]),
        in_specs=[_HBM] * (2 * n),
        out_specs=tuple([_SEM] * (2 * n) + [_HBM] * (2 * n) + [pl.BlockSpec(memory_space=pltpu.VMEM)]),
        input_output_aliases={i: 2 * n + i for i in range(2 * n)},
        compiler_params=pltpu.CompilerParams(has_side_effects=_EFFECT),
        name=name,
    )(*ops)
    return outs[:n], outs[n:2 * n], outs[2 * n:3 * n], outs[3 * n:4 * n], outs[4 * n]


def _scatter_wait(send_sems, recv_sems, srcs, lands, after, name):
    n = len(srcs)

    def body(*refs):
        land = refs[n:2 * n]
        ssem, rsem = refs[2 * n:3 * n], refs[3 * n:4 * n]
        x, y, c = lax.axis_index("x"), lax.axis_index("y"), lax.axis_index("c")
        for a in range(n):
            done = pltpu.make_async_remote_copy(
                src_ref=land[a], dst_ref=land[a], send_sem=ssem[a], recv_sem=rsem[a], device_id=(x, y, c),
                device_id_type=pl.DeviceIdType.MESH)
            done.wait_send()
            done.wait_recv()

    hbm = [pltpu.HBM(a.shape, a.dtype) for a in list(srcs) + list(lands)]
    outs = pl.pallas_call(
        body,
        out_shape=tuple(hbm),
        in_specs=[_HBM] * (2 * n) + [_SEM] * (2 * n) + [pl.BlockSpec(memory_space=pl.ANY)],
        out_specs=tuple([_HBM] * (2 * n)),
        input_output_aliases={i: i for i in range(2 * n)},
        compiler_params=pltpu.CompilerParams(has_side_effects=_EFFECT),
        name=name,
    )(*srcs, *lands, *send_sems, *recv_sems, after)
    return outs[:n], outs[n:]


def _adam_update(g, w_ref, m_ref, v_ref, g_ref, d_ref, nm_ref, nv_ref):
    mm = ADAM_B1 * m_ref[...] + (1.0 - ADAM_B1) * g
    vv = ADAM_B2 * v_ref[...] + (1.0 - ADAM_B2) * (g * g)
    m_hat = mm / (1.0 - ADAM_B1 ** ADAM_STEP)
    v_hat = vv / (1.0 - ADAM_B2 ** ADAM_STEP)
    g_ref[...] = g
    d_ref[...] = -ADAM_LR * (m_hat / (jnp.sqrt(v_hat) + ADAM_EPS) + ADAM_WD * w_ref[...])
    nm_ref[...] = mm
    nv_ref[...] = vv


def _adamw_own(w, own, own_idx, slots, m, v, name):
    r, c = w.shape[-2:]
    tr = _row_tile(r, 384)
    k = slots.shape[0]

    def body(i_ref, w_ref, o_ref, s_ref, m_ref, v_ref, g_ref, d_ref, nm_ref, nv_ref):
        del i_ref
        g = o_ref[...].astype(f32)
        for j in range(k):
            g = g + s_ref[j].astype(f32)
        _adam_update(g, w_ref, m_ref, v_ref, g_ref, d_ref, nm_ref, nv_ref)

    blk = pl.BlockSpec((None, tr, c), lambda i, ix: (0, i, 0))
    return pl.pallas_call(
        body,
        grid_spec=pltpu.PrefetchScalarGridSpec(
            num_scalar_prefetch=1,
            grid=(r // tr,),
            in_specs=[blk, pl.BlockSpec((None, tr, c), lambda i, ix: (ix[0], i, 0)),
                      pl.BlockSpec((k, tr, c), lambda i, ix: (0, i, 0)), blk, blk],
            out_specs=[blk] * 4,
        ),
        out_shape=[jax.ShapeDtypeStruct(w.shape, f32)] * 4,
        compiler_params=_params(("arbitrary",)),
        name=name,
    )(own_idx, w, own, slots, m, v)


def _adamw(w, slots, m, v, name):
    r, c = w.shape[-2:]
    tr = _row_tile(r, 128)

    def body(w_ref, s_ref, m_ref, v_ref, g_ref, d_ref, nm_ref, nv_ref):
        g = s_ref[0].astype(f32)
        for k in range(1, NDEV):
            g = g + s_ref[k].astype(f32)
        _adam_update(g, w_ref, m_ref, v_ref, g_ref, d_ref, nm_ref, nv_ref)

    if w.ndim == 3:
        blk = pl.BlockSpec((None, tr, c), lambda i: (0, i, 0))
    else:
        blk = pl.BlockSpec((tr, c), lambda i: (i, 0))
    return pl.pallas_call(
        body,
        grid=(r // tr,),
        in_specs=[blk, pl.BlockSpec((NDEV, tr, c), lambda i: (0, i, 0)), blk, blk],
        out_specs=[blk] * 4,
        out_shape=[jax.ShapeDtypeStruct(w.shape, f32)] * 4,
        compiler_params=_params(("arbitrary",)),
        name=name,
    )(w, slots, m, v)


class _Weights:
    def __init__(self, w_t, w_a, w_b, b_merge, w_o):
        self._w_t, self._rest = w_t, (w_a, w_b, b_merge, w_o)

    def first_half(self, after):
        del after
        return self._w_t, jnp.zeros((1,), jnp.int32)

    def second_half(self, after):
        del after
        return self._w_t, jnp.ones((1,), jnp.int32)

    def rest(self):
        return self._rest


def _local_step(x, tgt, norm_gain, weights, qn_a, kn_a, qn_b, kn_b, sink_a, rel_bias, on_weight_grads=None,
                core=None):
    two = lambda t: jnp.concatenate([t, t], axis=-1).reshape(1, LANES)
    ones = jnp.ones((1, LANES), f32)
    gains = jnp.stack([
        jnp.stack([two(qn_a), two(kn_a), ones]),
        jnp.stack([two(qn_b), two(kn_b), ones]),
        jnp.stack([two(qn_b), two(kn_b), ones]),
        jnp.stack([two(qn_b), two(kn_b), ones]),
    ])
    buckets = [jnp.asarray(_bucket_np(blk, d)) for blk, d, _ in GROUPS]
    bias = [_bias_expand(rel_bias, buckets[k], GROUPS[k][2], "bias_expand_%d" % k) for k in range(4)]

    hb, hbt, rstd = _rms(x, norm_gain)
    w_t, half = weights.first_half([hb] + bias)
    proj = _inproj_half(hb, w_t, half, None, "inproj_1")
    w_t, half = weights.second_half([proj])
    proj = _inproj_half(hb, w_t, half, proj, "inproj_2")
    w_a, w_b, b_merge, w_o = weights.rest()
    gl = _prep(proj, gains)
    o_a, l_a = _attn_fwd(gl, bias[0], sink_a.reshape(8), 0, 128, 1, "attn_fwd_a")
    fwd_b = [_attn_fwd(gl, bias[k], None, k, GROUPS[k][0], GROUPS[k][1], "attn_fwd_b%d" % k) for k in (1, 2, 3)]
    sink_b = jnp.repeat(sink_a.reshape(8), HD).reshape(1, 512)

    (dy, dyb, dproj, do_a, dd_a, do_b0, do_b1, do_b2, dd_b0, dd_b1, dd_b2, ya, yb, mg, dbr_a, dbr_b, loss, dbm,
     dsk) = _tail(x, tgt, o_a, l_a, [f[0] for f in fwd_b], [f[1] for f in fwd_b], proj, b_merge, w_a, w_b, w_o, sink_b)

    dw_o = _matmul_tokens(mg, dyb, "dw_out")
    dw_a = _matmul_tokens(ya, dbr_a, "dw_branch_a")
    dw_b = _matmul_tokens(yb, dbr_b, "dw_branch_b")
    if on_weight_grads is not None:
        early = on_weight_grads(dict(w_branch_a=dw_a, w_branch_b=dw_b, b_merge=dbm, w_out=dw_o))
        buckets = [buckets[0] + early.astype(jnp.int32)] + buckets[1:]

    dqkv_a, dbk_a = _attn_bwd(gl, bias[0], buckets[0], do_a, l_a, dd_a, 0, 128, 1, "attn_bwd_a")
    dproj, dg_a = _post_a(dqkv_a, proj, gains, dproj)
    dbk_b, dg_b = [], []
    for k, do_k, dd_k in ((1, do_b0, dd_b0), (2, do_b1, dd_b1), (3, do_b2, dd_b2)):
        dqkv, dbk = _attn_bwd(gl, bias[k], buckets[k], do_k, fwd_b[k - 1][1], dd_k, k, GROUPS[k][0], GROUPS[k][1],
                              "attn_bwd_b%d" % k)
        dproj, dg = _post_b(k, dqkv, proj, gains, dproj)
        dbk_b.append(dbk)
        dg_b.append(dg)
    dg_b = jnp.stack(dg_b)

    core = jnp.zeros((1,), jnp.int32) if core is None else core
    dw_other = _dw_in(hbt, dproj, 1 - core, "dw_in_other")
    sent = jnp.zeros((), f32) if on_weight_grads is None else on_weight_grads(dict(w_in_other=dw_other))
    dw_in = _dw_in(hbt, dproj, core + sent.astype(jnp.int32), "dw_in_own")
    token = jnp.zeros((), f32) if on_weight_grads is None else on_weight_grads(dict(w_in=dw_in))
    grad_x, d_norm_gain = _dh_norm_bwd(dproj, w_t, x, rstd, norm_gain + token, dy)

    fold = lambda t: t[..., :HD] + t[..., HD:]
    d_qn_a = fold(dg_a[0, 0])
    d_kn_a = fold(dg_a[1, 0])
    d_qn_b = fold(dg_b[:, 0, 0].sum(axis=0))
    d_kn_b = fold(dg_b[:, 1, 0].sum(axis=0))
    d_sink = dsk.reshape(8, HD)[:, 0]
    red = jnp.stack([dbk_a] + dbk_b)
    d_rel = red[:, :, 0, :32].reshape(32, 32).T
    return dict(loss=loss, grad_x=grad_x, norm_gain=d_norm_gain, w_in=dw_in, w_in_other=dw_other, q_norm_a=d_qn_a,
                k_norm_a=d_kn_a,
                q_norm_b=d_qn_b, k_norm_b=d_kn_b, sink_a=d_sink, rel_bias=d_rel, w_branch_a=dw_a, w_branch_b=dw_b,
                b_merge=dbm, w_out=dw_o)


SMALL = (("norm_gain", D), ("q_norm_a", HD), ("k_norm_a", HD), ("q_norm_b", HD), ("k_norm_b", HD), ("sink_a", 8),
         ("rel_bias", 1024))
SMALL_PAD = 2432


SMALL_USED = sum(sz for _, sz in SMALL)


def _pack_small(parts, loss=None):
    tail = jnp.zeros((SMALL_PAD - SMALL_USED,), f32)
    if loss is not None:
        tail = tail.at[0].set(loss.reshape(()))
    return jnp.concatenate([parts[n].reshape(-1) for n, _ in SMALL] + [tail]).reshape(1, SMALL_PAD)


def _unpack_small(flat, shapes):
    out, off = {}, 0
    for n, sz in SMALL:
        out[n] = flat[0, off:off + sz].reshape(shapes[n])
        off += sz
    return out


def kernel(x, norm_gain, w_in, q_norm_a, k_norm_a, q_norm_b, k_norm_b, sink_a, rel_bias, w_branch_a, w_branch_b, b_merge, w_out, loss_target, m_norm_gain, m_w_in, m_q_norm_a, m_k_norm_a, m_q_norm_b, m_k_norm_b, m_sink_a, m_rel_bias, m_w_branch_a, m_w_branch_b, m_b_merge, m_w_out, v_norm_gain, v_w_in, v_q_norm_a, v_k_norm_a, v_q_norm_b, v_k_norm_b, v_sink_a, v_rel_bias, v_w_branch_a, v_w_branch_b, v_b_merge, v_w_out):
    csh = D // NDEV
    w_in_t, m_w_in_t, v_w_in_t = (jnp.swapaxes(t, 1, 2) for t in (w_in, m_w_in, v_w_in))
    weights = _GatheredWeights([w_in_t[0].astype(bf16), w_branch_a[0].astype(bf16), w_branch_b[0].astype(bf16),
                                w_out[0].astype(bf16), b_merge[0]])

    pending = {}
    core = lax.axis_index("c").astype(jnp.int32).reshape(1)
    chip = (2 * lax.axis_index("x") + lax.axis_index("y")).astype(jnp.int32).reshape(1)
    me = (2 * chip + core).astype(jnp.int32)

    def start_exchange(gw):
        if "w_in_other" in gw:
            sems, lands, sent = _sibling_send_start(gw["w_in_other"])
            pending["sibling"] = (sems, lands)
            return sent
        if "w_in" in gw:
            from_sibling = _sibling_send_wait(*pending["sibling"], after=[gw["w_in"]])
            chip_sums = _pair_sum(gw["w_in"], from_sibling, "grad_pair_sum")
            pending["w_in"] = _scatter_start([chip_sums], [], "scatter_w_in_start")
            return pending["w_in"][4][0, 0]
        blocks = [gw["w_branch_a"].reshape(512, NDEV, csh).transpose(1, 0, 2).astype(bf16),
                  gw["w_branch_b"].reshape(512, NDEV, csh).transpose(1, 0, 2).astype(bf16),
                  gw["w_out"].reshape(NDEV, csh, D).astype(bf16),
                  gw["b_merge"].reshape(2, NDEV, csh).transpose(1, 0, 2)]
        pending["rest"] = _scatter_start([], blocks, "scatter_rest_start")
        return pending["rest"][4][0, 0]

    loc = _local_step(x[0], loss_target[0], norm_gain, weights, q_norm_a, k_norm_a, q_norm_b, k_norm_b, sink_a,
                      rel_bias, on_weight_grads=start_exchange, core=core)

    small_shapes = dict(norm_gain=(1, D), q_norm_a=(1, HD), k_norm_a=(1, HD), q_norm_b=(1, HD), k_norm_b=(1, HD),
                        sink_a=(1, 8), rel_bias=(32, 32))
    (r_small,) = _exchange([], [_pack_small(loc, loc["loss"])], "gather_small_grads")
    send_sems, recv_sems, srcs, lands, _ = pending["rest"]
    (s_a, s_b, s_o, s_bm), (r_a, r_b, r_o, r_bm) = _scatter_wait(
        send_sems, recv_sems, srcs, lands, r_small, "scatter_rest_wait")
    send_sems, recv_sems, srcs, lands, _ = pending["w_in"]
    (s_in,), (r_in,) = _scatter_wait(send_sems, recv_sems, srcs, lands, r_small, "scatter_w_in_wait")

    given = dict(norm_gain=norm_gain, q_norm_a=q_norm_a, k_norm_a=k_norm_a, q_norm_b=q_norm_b, k_norm_b=k_norm_b,
                 sink_a=sink_a, rel_bias=rel_bias)
    m_small = dict(norm_gain=m_norm_gain, q_norm_a=m_q_norm_a, k_norm_a=m_k_norm_a, q_norm_b=m_q_norm_b,
                   k_norm_b=m_k_norm_b, sink_a=m_sink_a, rel_bias=m_rel_bias)
    v_small = dict(norm_gain=v_norm_gain, q_norm_a=v_q_norm_a, k_norm_a=v_k_norm_a, q_norm_b=v_q_norm_b,
                   k_norm_b=v_k_norm_b, sink_a=v_sink_a, rel_bias=v_rel_bias)
    res = {
        "small": _adamw(_pack_small(given), r_small, _pack_small(m_small), _pack_small(v_small), "adamw_small"),
        "w_in": [jnp.swapaxes(t, 1, 2) for t in
                 _adamw_own(w_in_t, s_in, chip, r_in, m_w_in_t, v_w_in_t, "adamw_w_in")],
        "w_branch_a": _adamw_own(w_branch_a, s_a, me, r_a, m_w_branch_a, v_w_branch_a, "adamw_w_branch_a"),
        "w_branch_b": _adamw_own(w_branch_b, s_b, me, r_b, m_w_branch_b, v_w_branch_b, "adamw_w_branch_b"),
        "b_merge": _adamw_own(b_merge, s_bm, me, r_bm, m_b_merge, v_b_merge, "adamw_b_merge"),
        "w_out": _adamw_own(w_out, s_o, me, r_o, m_w_out, v_w_out, "adamw_w_out"),
    }
    order = ["norm_gain", "w_in", "q_norm_a", "k_norm_a", "q_norm_b", "k_norm_b", "sink_a", "rel_bias", "w_branch_a",
             "w_branch_b", "b_merge", "w_out"]
    outs = []
    for k in range(4):
        small = _unpack_small(res["small"][k], small_shapes)
        for n in order:
            outs.append(small[n] if n in small else res[n][k])
    loss = res["small"][0][0, SMALL_USED]
    return (loss, loc["grad_x"][None], *outs)
```

```python
import math

import numpy as np
import jax
import jax.numpy as jnp
from jax import lax
from jax.experimental import pallas as pl
from jax.experimental.pallas import tpu as pltpu

f32 = jnp.float32
bf16 = jnp.bfloat16

S = 4096
D = 1024
NA = 5376
NT = 3072
NW = NA + NT
WSH = NW // 8
HD = 64
LANES = 128
EPS = 1e-6
NEG = -1e30
SCALE = HD ** -0.5
TQ = 128
PAD = 128
SP = S + 2 * PAD
NDEV = 8
GROUPS = ((128, 1, 0), (64, 1, 8), (64, 4, 16), (64, 16, 24))
CHUNK = 256
PCHUNK = 128
RC = 64

ADAM_LR, ADAM_B1, ADAM_B2, ADAM_EPS, ADAM_WD, ADAM_STEP = 0.001, 0.9, 0.999, 1e-08, 0.01, 10

MIB = 1024 * 1024
NT_DIMS = (((1,), (1,)), ((), ()))
TN_DIMS = (((0,), (0,)), ((), ()))


def _params(sem=None, vmem_mib=48):
    return pltpu.CompilerParams(dimension_semantics=sem, vmem_limit_bytes=vmem_mib * MIB)


def _lo():
    return lax.broadcasted_iota(jnp.int32, (1, LANES), 1) < HD


def _head_ones():
    r = lax.broadcasted_iota(jnp.int32, (LANES, LANES), 0) // HD
    c = lax.broadcasted_iota(jnp.int32, (LANES, LANES), 1) // HD
    return jnp.where(r == c, 1.0, 0.0).astype(bf16)


def _half_sums(x, ones):
    hi = x.astype(bf16)
    mid = (x - hi.astype(f32)).astype(bf16)
    return (jnp.dot(hi, ones, preferred_element_type=f32) + jnp.dot(mid, ones, preferred_element_type=f32))


def _seg_sum(x, ones):
    outs = [_half_sums(x[:, b * LANES:(b + 1) * LANES], ones) for b in range(x.shape[1] // LANES)]
    return outs[0] if len(outs) == 1 else jnp.concatenate(outs, axis=1)


def _bucket_np(blk, stride):
    w = TQ + 2 * blk
    rel = np.arange(w)[None, :] - blk - np.arange(TQ)[:, None]
    band = np.abs(rel) <= blk
    r = rel * stride
    n = np.abs(r)
    nf = np.maximum(n, 8).astype(np.float32)
    large = 8 + (np.log(nf / np.float32(8)) / np.float32(math.log(128.0)) * np.float32(8)).astype(np.int32)
    large = np.minimum(large, 15)
    b = (r > 0).astype(np.int32) * 16 + np.where(n < 8, n, large)
    return np.where(band, b, -1).astype(np.int32)


def _rms(x, gain):
    ts = 512

    def body(x_ref, g_ref, h_ref, ht_ref, r_ref):
        xv = x_ref[...]
        r = lax.rsqrt(jnp.mean(xv * xv, axis=-1, keepdims=True) + EPS)
        h = (xv * r) * g_ref[...]
        h_ref[...] = h.astype(bf16)
        ht_ref[...] = h.T.astype(bf16)
        r_ref[...] = r

    return pl.pallas_call(
        body,
        grid=(S // ts,),
        in_specs=[pl.BlockSpec((ts, D), lambda i: (i, 0)), pl.BlockSpec((1, D), lambda i: (0, 0))],
        out_specs=[pl.BlockSpec((ts, D), lambda i: (i, 0)), pl.BlockSpec((D, ts), lambda i: (0, i)),
                   pl.BlockSpec((ts, 1), lambda i: (i, 0))],
        out_shape=[jax.ShapeDtypeStruct((S, D), bf16), jax.ShapeDtypeStruct((D, S), bf16),
                   jax.ShapeDtypeStruct((S, 1), f32)],
        compiler_params=_params(("arbitrary",)),
        name="rms",
    )(x, gain)


def _inproj_half(hb, w_t, half, proj, name):
    ts = 512
    tn = NW // 2
    per = NW // 2 // tn

    def body(h_idx, h_ref, w_ref, *rest):
        del h_idx
        rest[-1][...] = lax.dot_general(h_ref[...], w_ref[...], NT_DIMS, preferred_element_type=f32)

    in_specs = [pl.BlockSpec((ts, D), lambda i, n, hf: (i, 0)),
                pl.BlockSpec((tn, D), lambda i, n, hf: (hf[0] * per + n, 0))]
    args = [half, hb, w_t]
    aliases = {}
    if proj is not None:
        in_specs.append(pl.BlockSpec(memory_space=pl.ANY))
        args.append(proj)
        aliases = {3: 0}
    return pl.pallas_call(
        body,
        grid_spec=pltpu.PrefetchScalarGridSpec(
            num_scalar_prefetch=1,
            grid=(S // ts, per),
            in_specs=in_specs,
            out_specs=pl.BlockSpec((ts, tn), lambda i, n, hf: (i, hf[0] * per + n)),
        ),
        out_shape=jax.ShapeDtypeStruct((S, NW), f32),
        input_output_aliases=aliases,
        compiler_params=_params(("arbitrary", "arbitrary")),
        name=name,
    )(*args)


def _bias_expand(table, bucket, c0, name):
    tq, w = bucket.shape
    blk = (w - tq) // 2

    def body(tab_ref, bk_ref, o_ref):
        h = pl.program_id(0)
        bk = bk_ref[...]

        def step(b, acc):
            return jnp.where(bk == b, tab_ref[b, c0 + h], acc)

        inner = lax.fori_loop(0, 32, step, jnp.full((tq, w), NEG, f32))
        col = lax.broadcasted_iota(jnp.int32, (1, w), 1)
        o_ref[0] = jnp.where(col < blk, NEG, inner)
        o_ref[1] = inner
        o_ref[2] = jnp.where(col >= tq + blk, NEG, inner)

    return pl.pallas_call(
        body,
        grid=(8,),
        in_specs=[pl.BlockSpec(memory_space=pltpu.SMEM), pl.BlockSpec((tq, w), lambda h: (0, 0))],
        out_specs=pl.BlockSpec((3, None, tq, w), lambda h: (0, h, 0, 0)),
        out_shape=jax.ShapeDtypeStruct((3, 8, tq, w), f32),
        compiler_params=_params(("arbitrary",)),
        name=name,
    )(table, bucket)


def _tile_kind(t, seq):
    m0 = jnp.bitwise_and(t * TQ, seq - 1)
    return jnp.where(m0 == 0, 0, jnp.where(m0 == seq - TQ, 2, 1))


def _col_block(g, j):
    kind = j // 4
    hp = j % 4
    a = jnp.where(kind == 0, hp, 3 + kind)
    b = 6 + 12 * kind + 4 * (g - 1) + hp
    return jnp.where(g == 0, a, b)


def _prep(proj_a, gains):
    def body(p0_ref, p1_ref, p2_ref, p3_ref, g_ref, o_ref):
        g = pl.program_id(0)
        kind = pl.program_id(1)
        lo = _lo()
        ones = _head_ones()
        half = jnp.where(lo, 0, 1)
        gain = g_ref[...]

        def norm_store(xv, u, dst, dup, k):
            if dup and k > 0:
                xv = jnp.where(half == u // 2, xv, pltpu.roll(xv, HD, 1))
            if k < 2:
                r = lax.rsqrt(_half_sums(xv * xv, ones) * (1.0 / HD) + EPS)
                xv = (xv * r) * gain
            if k == 0:
                xv = xv * SCALE
            o_ref[u, PAD + dst:PAD + dst + CHUNK, :] = xv.astype(bf16)

        for u in range(4):
            o_ref[u, 0:PAD, :] = jnp.zeros((PAD, LANES), bf16)
            o_ref[u, PAD + S:SP, :] = jnp.zeros((PAD, LANES), bf16)
        for gi, (_, d, _) in enumerate(GROUPS):
            for k in range(3):
                @pl.when((g == gi) & (kind == k))
                def _():
                    seq = S // d
                    for u, p_ref in enumerate((p0_ref, p1_ref, p2_ref, p3_ref)):
                        for c in range(d):
                            for i in range(seq // CHUNK):
                                if d == 1:
                                    xv = p_ref[i * CHUNK:(i + 1) * CHUNK, :]
                                else:
                                    xv = p_ref[pl.ds(c + i * CHUNK * d, CHUNK, stride=d), :]
                                norm_store(xv, u, c * seq + i * CHUNK, gi == 0, k)

    return pl.pallas_call(
        body,
        grid=(4, 3),
        in_specs=[pl.BlockSpec((S, LANES), lambda g, kind, u=u: (0, _col_block(g, 4 * kind + u))) for u in range(4)] + [
            pl.BlockSpec((None, None, 1, LANES), lambda g, kind: (g, kind, 0, 0)),
        ],
        out_specs=pl.BlockSpec((None, 4, SP, LANES), lambda g, kind: (g, kind, 0, 0)),
        out_shape=jax.ShapeDtypeStruct((4, 12, SP, LANES), bf16),
        compiler_params=_params(("arbitrary", "arbitrary")),
        name="prep",
    )(proj_a, proj_a, proj_a, proj_a, gains)


def _token_rows(t, r0, n, d):
    if d == 1:
        return pl.ds(pl.multiple_of(t * TQ, TQ) + r0, n)
    per = S // d // TQ
    return pl.ds(((t % per) * TQ + r0) * d + t // per, n, stride=d)


def _stack_heads(t, lo):
    z = jnp.zeros_like(t)
    return jnp.concatenate([jnp.where(lo, t, z), jnp.where(lo, z, t)], axis=0)


def _unstack_heads(t2, lo):
    return jnp.where(lo, t2[:TQ], t2[TQ:])


def _attn_fwd(gl, bias, sink, g, blk, d, name):
    w = TQ + 2 * blk
    seq = S // d
    use_sink = sink is not None

    def body(*refs):
        if use_sink:
            sink_ref, q_ref, k_ref, v_ref, b_ref, o_ref, l_ref, s0, s1, p0, p1, lse_scr = refs
        else:
            q_ref, k_ref, v_ref, b_ref, o_ref, l_ref, s0, s1, p0, p1, lse_scr = refs
        hp = pl.program_id(0)
        lo = _lo()
        s_bufs, p_bufs = (s0, s1), (p0, p1)

        def scores(p, slot):
            for u in range(2):
                f0 = pl.multiple_of((2 * p + u) * TQ, TQ)
                q2 = _stack_heads(q_ref[pl.ds(PAD + f0, TQ), :], lo)
                kw = k_ref[pl.ds(PAD - blk + f0, w), :]
                s_bufs[slot][u] = lax.dot_general(q2, kw, NT_DIMS, preferred_element_type=f32)

        def softmax(p, slot):
            for u in range(2):
                t = 2 * p + u
                kind = _tile_kind(t, seq)
                for h in range(2):
                    for r in range(TQ // RC):
                        rows = slice(h * TQ + r * RC, h * TQ + (r + 1) * RC)
                        logit = s_bufs[slot][u, rows, :] + b_ref[kind, h, r * RC:(r + 1) * RC, :]
                        m = jnp.max(logit, axis=1, keepdims=True)
                        e = jnp.exp(logit - m)
                        lse = m + jnp.log(jnp.sum(e, axis=1, keepdims=True))
                        if use_sink:
                            sk = sink_ref[2 * hp + h]
                            mx = jnp.maximum(lse, sk)
                            lse = mx + jnp.log(jnp.exp(lse - mx) + jnp.exp(sk - mx))
                        p_bufs[slot][u, rows, :] = (e * jnp.exp(m - lse)).astype(bf16)
                        lse_scr[u, rows, :] = jnp.broadcast_to(lse, (RC, LANES))
                l_ref[_token_rows(t, 0, TQ, d), :] = jnp.where(lo, lse_scr[u, 0:TQ, :], lse_scr[u, TQ:2 * TQ, :])

        def values(p, slot):
            for u in range(2):
                t = 2 * p + u
                vw = v_ref[pl.ds(PAD - blk + pl.multiple_of(t * TQ, TQ), w), :]
                o2 = jnp.dot(p_bufs[slot][u], vw, preferred_element_type=f32)
                o_ref[_token_rows(t, 0, TQ, d), :] = _unstack_heads(o2, lo)

        npair = S // TQ // 2
        scores(0, 0)
        scores(1, 1)
        softmax(0, 0)

        def steady(k, carry):
            p = 2 * k + 2
            scores(p, 0)
            softmax(p - 1, 1)
            values(p - 2, 0)
            scores(p + 1, 1)
            softmax(p, 0)
            values(p - 1, 1)
            return carry

        lax.fori_loop(0, (npair - 2) // 2, steady, 0)
        softmax(npair - 1, 1)
        values(npair - 2, 0)
        values(npair - 1, 1)

    in_specs = [
        pl.BlockSpec((None, None, SP, LANES), lambda hp: (g, hp, 0, 0)),
        pl.BlockSpec((None, None, SP, LANES), lambda hp: (g, 4 + hp, 0, 0)),
        pl.BlockSpec((None, None, SP, LANES), lambda hp: (g, 8 + hp, 0, 0)),
        pl.BlockSpec((3, 2, TQ, w), lambda hp: (0, hp, 0, 0)),
    ]
    args = [gl, gl, gl, bias]
    if use_sink:
        in_specs = [pl.BlockSpec(memory_space=pltpu.SMEM)] + in_specs
        args = [sink] + args
    out = pl.BlockSpec((S, LANES), lambda hp: (0, hp))
    return pl.pallas_call(
        body,
        grid=(4,),
        in_specs=in_specs,
        out_specs=[out, out],
        out_shape=[jax.ShapeDtypeStruct((S, 4 * LANES), f32)] * 2,
        scratch_shapes=[pltpu.VMEM((2, 2 * TQ, w), f32), pltpu.VMEM((2, 2 * TQ, w), f32),
                        pltpu.VMEM((2, 2 * TQ, w), bf16), pltpu.VMEM((2, 2 * TQ, w), bf16),
                        pltpu.VMEM((2, 2 * TQ, LANES), f32)],
        compiler_params=_params(("arbitrary",)),
        name=name,
    )(*args)


def _attn_bwd(gl, bias, bucket, do, lse, dd, g, blk, d, name):
    w = TQ + 2 * blk
    seq = S // d

    def body(q_ref, k_ref, v_ref, b_ref, bk_ref, do_ref, l_ref, d_ref, dqkv_ref, dbk_ref,
             db_acc, s0, s1, dp0, dp1, pb0, pb1, ds0, ds1, dk_acc, dv_acc):
        lo = _lo()
        hi = jnp.logical_not(lo)
        dk_acc[...] = jnp.zeros((SP, LANES), f32)
        dv_acc[...] = jnp.zeros((SP, LANES), f32)
        db_acc[...] = jnp.zeros((2 * TQ, w), f32)
        s_bufs, dp_bufs, pb_bufs, ds_bufs = (s0, s1), (dp0, dp1), (pb0, pb1), (ds0, ds1)

        def stacked(t):
            f0 = pl.multiple_of(t * TQ, TQ)
            q2 = _stack_heads(q_ref[pl.ds(PAD + f0, TQ), :], lo)
            do2 = _stack_heads(do_ref[_token_rows(t, 0, TQ, d), :].astype(bf16), lo)
            return f0, q2, do2

        def scores(p, slot):
            for u in range(2):
                f0, q2, do2 = stacked(2 * p + u)
                win = pl.ds(PAD - blk + f0, w)
                s_bufs[slot][u] = lax.dot_general(q2, k_ref[win, :], NT_DIMS, preferred_element_type=f32)
                dp_bufs[slot][u] = lax.dot_general(do2, v_ref[win, :], NT_DIMS, preferred_element_type=f32)

        def grads(p, slot):
            for u in range(2):
                t = 2 * p + u
                kind = _tile_kind(t, seq)
                for h in range(2):
                    msk = lo if h == 0 else hi
                    for r in range(TQ // RC):
                        rows = slice(h * TQ + r * RC, h * TQ + (r + 1) * RC)
                        src = _token_rows(t, r * RC, RC, d)
                        lh = jnp.max(jnp.where(msk, l_ref[src, :], -jnp.inf), axis=1, keepdims=True)
                        dh = jnp.max(jnp.where(msk, d_ref[src, :], -jnp.inf), axis=1, keepdims=True)
                        logit = s_bufs[slot][u, rows, :] + b_ref[kind, h, r * RC:(r + 1) * RC, :]
                        pr = jnp.exp(logit - lh)
                        ds = pr * (dp_bufs[slot][u, rows, :] - dh)
                        db_acc[rows, :] += ds
                        pb_bufs[slot][u, rows, :] = pr.astype(bf16)
                        ds_bufs[slot][u, rows, :] = ds.astype(bf16)

        def accumulate(p, slot):
            for u in range(2):
                f0, q2, do2 = stacked(2 * p + u)
                win = pl.ds(PAD - blk + f0, w)
                dsb = ds_bufs[slot][u]
                dq2 = jnp.dot(dsb, k_ref[win, :], preferred_element_type=f32)
                dqkv_ref[0, pl.ds(PAD + f0, TQ), :] = _unstack_heads(dq2, lo).astype(bf16)
                dk_acc[win, :] += lax.dot_general(dsb, q2, TN_DIMS, preferred_element_type=f32)
                dv_acc[win, :] += lax.dot_general(pb_bufs[slot][u], do2, TN_DIMS, preferred_element_type=f32)

        npair = S // TQ // 2
        scores(0, 0)
        scores(1, 1)
        grads(0, 0)

        def steady(k, carry):
            p = 2 * k + 2
            scores(p, 0)
            grads(p - 1, 1)
            accumulate(p - 2, 0)
            scores(p + 1, 1)
            grads(p, 0)
            accumulate(p - 1, 1)
            return carry

        lax.fori_loop(0, (npair - 2) // 2, steady, 0)
        grads(npair - 1, 1)
        accumulate(npair - 2, 0)
        accumulate(npair - 1, 1)
        for i in range(SP // CHUNK):
            rows = slice(i * CHUNK, (i + 1) * CHUNK)
            dqkv_ref[1, rows, :] = dk_acc[rows, :].astype(bf16)
            dqkv_ref[2, rows, :] = dv_acc[rows, :].astype(bf16)

        bk = bk_ref[...]
        lane = lax.broadcasted_iota(jnp.int32, (8, LANES), 1)
        for h in range(2):
            db = db_acc[h * TQ:(h + 1) * TQ, :]
            acc = jnp.zeros((8, LANES), f32)
            for b in range(32):
                part = jnp.where(bk == b, db, 0.0).reshape(TQ // 8, 8, w).sum(axis=0)
                tot = jnp.sum(jnp.sum(part, axis=1, keepdims=True), axis=0, keepdims=True)
                acc = jnp.where(lane == b, tot, acc)
            dbk_ref[h] = acc

    def gcol(off):
        return pl.BlockSpec((None, None, SP, LANES), lambda hp: (g, off + hp, 0, 0))

    row = pl.BlockSpec((S, LANES), lambda hp: (0, hp))
    return pl.pallas_call(
        body,
        grid=(4,),
        in_specs=[gcol(0), gcol(4), gcol(8), pl.BlockSpec((3, 2, TQ, w), lambda hp: (0, hp, 0, 0)),
                  pl.BlockSpec((TQ, w), lambda hp: (0, 0)), row, row, row],
        out_specs=[pl.BlockSpec((3, None, SP, LANES), lambda hp: (0, hp, 0, 0)),
                   pl.BlockSpec((2, 8, LANES), lambda hp: (hp, 0, 0))],
        out_shape=[
            jax.ShapeDtypeStruct((3, 4, SP, LANES), bf16),
            jax.ShapeDtypeStruct((8, 8, LANES), f32),
        ],
        scratch_shapes=([pltpu.VMEM((2 * TQ, w), f32)] + [pltpu.VMEM((2, 2 * TQ, w), f32)] * 4
                        + [pltpu.VMEM((2, 2 * TQ, w), bf16)] * 4 + [pltpu.VMEM((SP, LANES), f32)] * 2),
        compiler_params=_params(("arbitrary",), vmem_mib=56),
        name=name,
    )(gl, gl, gl, bias, bucket, do, lse, dd)


def _sigmoid(z):
    return 1.0 / (1.0 + jnp.exp(-z))


def _tail(x, tgt, o_a, l_a, o_b, l_b, proj, bm, w_a, w_b, w_o, sink_b):
    ts = 256

    def body(x_ref, t_ref, oa_ref, la_ref, ob0_ref, ob1_ref, ob2_ref, lb0_ref, lb1_ref, lb2_ref,
             ga_ref, gb_ref, m0_ref, m1_ref, bm_ref, wa_ref, wb_ref, wo_ref, sk_ref,
             dy_ref, dyb_ref, dt_ref, doa_ref, dda_ref, dob0_ref, dob1_ref, dob2_ref, ddb0_ref, ddb1_ref, ddb2_ref,
             ya_ref, yb_ref, mg_ref, dbra_ref, dbrb_ref, loss_ref, dbm_ref, dsk_ref):
        i = pl.program_id(0)

        @pl.when(i == 0)
        def _():
            loss_ref[...] = jnp.zeros_like(loss_ref)
            dbm_ref[...] = jnp.zeros_like(dbm_ref)
            dsk_ref[...] = jnp.zeros_like(dsk_ref)

        ga = ga_ref[...]
        sa = _sigmoid(ga)
        silu_a = ga * sa
        oa = oa_ref[...]
        ya = oa * silu_a
        gb = gb_ref[...]
        sb = _sigmoid(gb)
        silu_b = gb * sb
        ob = [ob0_ref[...], ob1_ref[...], ob2_ref[...]]
        lb = [lb0_ref[...], lb1_ref[...], lb2_ref[...]]
        mx = jnp.maximum(jnp.maximum(lb[0], lb[1]), lb[2])
        ex = [jnp.exp(v - mx) for v in lb]
        den = ex[0] + ex[1] + ex[2]
        alpha = [e / den for e in ex]
        ybc = alpha[0] * ob[0] + alpha[1] * ob[1] + alpha[2] * ob[2]
        yb = ybc * silu_b
        yab = ya.astype(bf16)
        ybb = yb.astype(bf16)
        br_a = jnp.dot(yab, wa_ref[...], preferred_element_type=f32)
        br_b = jnp.dot(ybb, wb_ref[...], preferred_element_type=f32)
        g0 = _sigmoid(m0_ref[...] + bm_ref[0:1, :])
        g1 = _sigmoid(m1_ref[...] + bm_ref[1:2, :])
        merged = g0 * br_a + g1 * br_b
        mgb = merged.astype(bf16)
        y = x_ref[...] + jnp.dot(mgb, wo_ref[...], preferred_element_type=f32)
        err = y - t_ref[...]
        part = jnp.sum(jnp.sum(err * err, axis=1, keepdims=True), axis=0, keepdims=True)
        loss_ref[...] += part * (0.5 / D)
        dy = err * (1.0 / D)
        dyb = dy.astype(bf16)
        dmerged = lax.dot_general(dyb, wo_ref[...], NT_DIMS, preferred_element_type=f32)
        dbr_a = (dmerged * g0).astype(bf16)
        dbr_b = (dmerged * g1).astype(bf16)
        dm0 = dmerged * br_a * (g0 * (1.0 - g0))
        dm1 = dmerged * br_b * (g1 * (1.0 - g1))
        dbm_ref[0:1, :] += jnp.sum(dm0, axis=0, keepdims=True)
        dbm_ref[1:2, :] += jnp.sum(dm1, axis=0, keepdims=True)
        dya = lax.dot_general(dbr_a, wa_ref[...], NT_DIMS, preferred_element_type=f32)
        dyb2 = lax.dot_general(dbr_b, wb_ref[...], NT_DIMS, preferred_element_type=f32)
        do_a = dya * silu_a
        dga = dya * oa * (sa * (1.0 + ga * (1.0 - sa)))
        ones = _head_ones()
        delta_a = _seg_sum(do_a * oa, ones)
        dsk_ref[...] -= jnp.sum(delta_a * jnp.exp(sk_ref[...] - la_ref[...]), axis=0, keepdims=True)
        dybc = dyb2 * silu_b
        dgb = dyb2 * ybc * (sb * (1.0 + gb * (1.0 - sb)))
        dbar = _seg_sum(dybc * ybc, ones)
        dy_ref[...] = dy
        dyb_ref[...] = dyb
        dt_ref[:, 0:512] = dga.astype(bf16)
        dt_ref[:, 512:1024] = dgb.astype(bf16)
        dt_ref[:, 1024:2048] = dm0.astype(bf16)
        dt_ref[:, 2048:3072] = dm1.astype(bf16)
        doa_ref[...] = do_a.astype(bf16)
        dda_ref[...] = delta_a
        for k, (dob_ref, ddb_ref) in enumerate(((dob0_ref, ddb0_ref), (dob1_ref, ddb1_ref), (dob2_ref, ddb2_ref))):
            dob_ref[...] = alpha[k] * dybc
            ddb_ref[...] = alpha[k] * dbar
        ya_ref[...] = ya.T.astype(bf16)
        yb_ref[...] = yb.T.astype(bf16)
        mg_ref[...] = merged.T.astype(bf16)
        dbra_ref[...] = dbr_a
        dbrb_ref[...] = dbr_b

    def rows(n, blk=0):
        return pl.BlockSpec((ts, n), lambda i: (i, blk))

    def whole(r, c):
        return pl.BlockSpec((r, c), lambda i: (0, 0))

    def cols(n):
        return pl.BlockSpec((n, ts), lambda i: (0, i))

    def gate_cols(n, col):
        return pl.BlockSpec((pl.Element(ts), pl.Element(n)), lambda i: (i * ts, NA + col))

    outs = [
        ((S, D), f32, rows(D)), ((S, D), bf16, rows(D)), ((S, NW), bf16, gate_cols(NT, 0)),
        ((S, 512), bf16, rows(512)), ((S, 512), f32, rows(512)),
        ((S, 512), f32, rows(512)), ((S, 512), f32, rows(512)), ((S, 512), f32, rows(512)),
        ((S, 512), f32, rows(512)), ((S, 512), f32, rows(512)), ((S, 512), f32, rows(512)),
        ((512, S), bf16, cols(512)), ((512, S), bf16, cols(512)), ((D, S), bf16, cols(D)),
        ((S, D), bf16, rows(D)), ((S, D), bf16, rows(D)),
        ((1, 1), f32, whole(1, 1)), ((2, D), f32, whole(2, D)), ((1, 512), f32, whole(1, 512)),
    ]
    return pl.pallas_call(
        body,
        grid=(S // ts,),
        in_specs=[
            rows(D), rows(D), rows(512), rows(512), rows(512), rows(512), rows(512), rows(512), rows(512), rows(512),
            gate_cols(512, 0), gate_cols(512, 512), gate_cols(D, 1024), gate_cols(D, 2048), whole(2, D),
            whole(512, D), whole(512, D), whole(D, D), whole(1, 512),
        ],
        out_specs=[o[2] for o in outs],
        out_shape=[jax.ShapeDtypeStruct(o[0], o[1]) for o in outs],
        compiler_params=_params(("arbitrary",), vmem_mib=60),
        name="tail",
    )(x, tgt, o_a, l_a, *o_b, *l_b, proj, proj, proj, proj, bm, w_a, w_b, w_o, sink_b)


def _norm_bwd(xv, dyv, gain, ones):
    r = lax.rsqrt(_half_sums(xv * xv, ones) * (1.0 / HD) + EPS)
    yv = xv * r
    u = dyv * gain
    dxv = r * (u - yv * (_half_sums(u * yv, ones) * (1.0 / HD)))
    return dxv, jnp.sum(dyv * yv, axis=0, keepdims=True)


def _post_b(g, dqkv, proj_a, gains, dproj):
    d = GROUPS[g][1]
    seq = S // d

    def body(d_ref, pa_ref, pb_ref, g_ref, alias_ref, o_ref, dg_ref, nat_a, nat_b):
        del alias_ref
        pj = pl.program_id(0)
        kind = pj // 2
        q_scale = jnp.where(kind == 0, SCALE, 1.0)
        gain = g_ref[...] * q_scale
        ones = _head_ones()

        @pl.when(pj % 2 == 0)
        def _():
            dg_ref[...] = jnp.zeros_like(dg_ref)

        def columns(with_norm):
            for u, (p_ref, nat) in enumerate(((pa_ref, nat_a), (pb_ref, nat_b))):
                for c in range(d):
                    for i in range(seq // PCHUNK):
                        src = c * seq + i * PCHUNK
                        if d == 1:
                            idx = slice(src, src + PCHUNK)
                        else:
                            idx = pl.ds(c + i * PCHUNK * d, PCHUNK, stride=d)
                        dyv = d_ref[u, PAD + src:PAD + src + PCHUNK, :].astype(f32)
                        if with_norm:
                            dyv, dg = _norm_bwd(p_ref[idx, :], dyv, gain, ones)
                            dg_ref[...] += dg * q_scale
                        nat[idx, :] = dyv
                for i in range(S // CHUNK):
                    rows = slice(i * CHUNK, (i + 1) * CHUNK)
                    o_ref[rows, u * LANES:(u + 1) * LANES] = nat[rows, :].astype(bf16)

        pl.when(kind < 2)(lambda: columns(True))
        pl.when(kind == 2)(lambda: columns(False))

    def pcol(u):
        return pl.BlockSpec((S, LANES), lambda pj: (0, _col_block(g, 2 * jnp.minimum(pj, 3) + u)))

    return pl.pallas_call(
        body,
        grid=(6,),
        in_specs=[
            pl.BlockSpec((None, 2, SP, LANES), lambda pj: (pj // 2, pj % 2, 0, 0)),
            pcol(0), pcol(1),
            pl.BlockSpec((None, None, 1, LANES), lambda pj: (g, pj // 2, 0, 0)),
            pl.BlockSpec(memory_space=pl.ANY),
        ],
        out_specs=[
            pl.BlockSpec((S, 2 * LANES), lambda pj: (0, _col_block(g, 2 * pj) // 2)),
            pl.BlockSpec((None, 1, LANES), lambda pj: (pj // 2, 0, 0)),
        ],
        out_shape=[jax.ShapeDtypeStruct((S, NW), bf16), jax.ShapeDtypeStruct((3, 1, LANES), f32)],
        scratch_shapes=[pltpu.VMEM((S, LANES), f32), pltpu.VMEM((S, LANES), f32)],
        input_output_aliases={4: 0},
        compiler_params=_params(("arbitrary",)),
        name="post_b%d" % g,
    )(dqkv, proj_a, proj_a, gains, dproj)


def _post_a(dqkv, proj_a, gains, dproj):
    def body(q_ref, e_ref, p_ref, g_ref, alias_ref, o_ref, dg_ref):
        del alias_ref
        j = pl.program_id(0)
        q_scale = jnp.where(j < 4, SCALE, 1.0)
        gain = g_ref[...] * q_scale
        lo = _lo()
        ones = _head_ones()

        @pl.when((j == 0) | (j >= 4))
        def _():
            dg_ref[...] = jnp.zeros_like(dg_ref)

        def column(folded, with_norm):
            for i in range(S // PCHUNK):
                r0 = i * PCHUNK
                rows = slice(PAD + r0, PAD + r0 + PCHUNK)
                if folded:
                    t0 = e_ref[0, rows, :].astype(f32) + e_ref[1, rows, :].astype(f32)
                    t1 = e_ref[2, rows, :].astype(f32) + e_ref[3, rows, :].astype(f32)
                    dyv = jnp.where(lo, t0 + pltpu.roll(t0, HD, 1), t1 + pltpu.roll(t1, HD, 1))
                else:
                    dyv = q_ref[rows, :].astype(f32)
                if with_norm:
                    dyv, dg = _norm_bwd(p_ref[r0:r0 + PCHUNK, :], dyv, gain, ones)
                    dg_ref[...] += dg * q_scale
                o_ref[r0:r0 + PCHUNK, :] = dyv.astype(bf16)

        pl.when(j < 4)(lambda: column(False, True))
        pl.when(j == 4)(lambda: column(True, True))
        pl.when(j == 5)(lambda: column(True, False))

    return pl.pallas_call(
        body,
        grid=(6,),
        in_specs=[
            pl.BlockSpec((None, None, SP, LANES), lambda j: (0, jnp.minimum(j, 3), 0, 0)),
            pl.BlockSpec((None, 4, SP, LANES), lambda j: (jnp.clip(j - 3, 1, 2), 0, 0, 0)),
            pl.BlockSpec((S, LANES), lambda j: (0, jnp.minimum(j, 4))),
            pl.BlockSpec((None, None, 1, LANES), lambda j: (0, jnp.maximum(j - 3, 0), 0, 0)),
            pl.BlockSpec(memory_space=pl.ANY),
        ],
        out_specs=[
            pl.BlockSpec((S, LANES), lambda j: (0, j)),
            pl.BlockSpec((None, 1, LANES), lambda j: (jnp.maximum(j - 3, 0), 0, 0)),
        ],
        out_shape=[jax.ShapeDtypeStruct((S, NW), bf16), jax.ShapeDtypeStruct((3, 1, LANES), f32)],
        input_output_aliases={4: 0},
        compiler_params=_params(("arbitrary",)),
        name="post_a",
    )(dqkv, dqkv, proj_a, gains, dproj)


def _dh_norm_bwd(dproj, w, x, rstd, gain, dy):
    ts = 1024
    tk = NW // 6
    nk = NW // tk

    def body(d_ref, w_ref, x_ref, r_ref, g_ref, dy_ref, gx_ref, dgn_ref, acc):
        i = pl.program_id(0)
        k = pl.program_id(1)

        @pl.when((i == 0) & (k == 0))
        def _():
            dgn_ref[...] = jnp.zeros_like(dgn_ref)

        @pl.when(k == 0)
        def _():
            acc[...] = jnp.zeros_like(acc)

        acc[...] += jnp.dot(d_ref[...], w_ref[...], preferred_element_type=f32)

        @pl.when(k == nk - 1)
        def _():
            dh = acc[...]
            xh = x_ref[...] * r_ref[...]
            u = dh * g_ref[...]
            dx = r_ref[...] * (u - xh * jnp.mean(u * xh, axis=-1, keepdims=True))
            gx_ref[...] = dy_ref[...] + dx
            dgn_ref[...] += jnp.sum(dh * xh, axis=0, keepdims=True)

    return pl.pallas_call(
        body,
        grid=(S // ts, nk),
        in_specs=[
            pl.BlockSpec((ts, tk), lambda i, k: (i, k)),
            pl.BlockSpec((tk, D), lambda i, k: (k, 0)),
            pl.BlockSpec((ts, D), lambda i, k: (i, 0)),
            pl.BlockSpec((ts, 1), lambda i, k: (i, 0)),
            pl.BlockSpec((1, D), lambda i, k: (0, 0)),
            pl.BlockSpec((ts, D), lambda i, k: (i, 0)),
        ],
        out_specs=[pl.BlockSpec((ts, D), lambda i, k: (i, 0)), pl.BlockSpec((1, D), lambda i, k: (0, 0))],
        out_shape=[jax.ShapeDtypeStruct((S, D), f32), jax.ShapeDtypeStruct((1, D), f32)],
        scratch_shapes=[pltpu.VMEM((ts, D), f32)],
        compiler_params=_params(("arbitrary", "arbitrary"), vmem_mib=56),
        name="dh_norm_bwd",
    )(dproj, w, x, rstd, gain, dy)


def _dw_in(hbt, dproj, parity, name):
    tk = 1024
    win = WSH + 96

    def body(par_ref, a_ref, b_ref, o_ref, acc):
        p = 2 * pl.program_id(0) + par_ref[0]
        k = pl.program_id(1)

        @pl.when(k == 0)
        def _():
            acc[...] = jnp.zeros_like(acc)

        acc[...] += jnp.dot(a_ref[...], b_ref[...], preferred_element_type=f32)

        @pl.when(k == S // tk - 1)
        def _():
            acc_t = acc[...].T
            for pp in range(NDEV):
                off = (WSH * pp) % LANES

                @pl.when(p == pp)
                def _():
                    o_ref[...] = acc_t[off:off + WSH, :].astype(bf16)

    return pl.pallas_call(
        body,
        grid_spec=pltpu.PrefetchScalarGridSpec(
            num_scalar_prefetch=1,
            grid=(NDEV // 2, S // tk),
            in_specs=[
                pl.BlockSpec((D, tk), lambda q, k, par: (0, k)),
                pl.BlockSpec((pl.Element(tk), pl.Element(win)),
                             lambda q, k, par: (k * tk, (WSH * (2 * q + par[0])) // LANES * LANES)),
            ],
            out_specs=pl.BlockSpec((None, WSH, D), lambda q, k, par: (q, 0, 0)),
            scratch_shapes=[pltpu.VMEM((D, win), f32)],
        ),
        out_shape=jax.ShapeDtypeStruct((NDEV // 2, WSH, D), bf16),
        compiler_params=_params(("arbitrary", "arbitrary")),
        name=name,
    )(parity, hbt, dproj)


def _matmul_tokens(at, b, name):
    m, n = at.shape[0], b.shape[1]
    tn = 512
    tk = 1024

    def body(a_ref, b_ref, o_ref):
        @pl.when(pl.program_id(1) == 0)
        def _():
            o_ref[...] = jnp.zeros_like(o_ref)

        o_ref[...] += jnp.dot(a_ref[...], b_ref[...], preferred_element_type=f32)

    return pl.pallas_call(
        body,
        grid=(n // tn, S // tk),
        in_specs=[pl.BlockSpec((m, tk), lambda j, k: (0, k)), pl.BlockSpec((tk, tn), lambda j, k: (k, j))],
        out_specs=pl.BlockSpec((m, tn), lambda j, k: (0, j)),
        out_shape=jax.ShapeDtypeStruct((m, n), f32),
        compiler_params=_params(("arbitrary", "arbitrary")),
        name=name,
    )(at, b)


def _exchange(scatter, gather, name):
    arrs = list(scatter) + list(gather)
    n = len(arrs)
    ns = len(scatter)

    def body(*refs):
        ins, outs = refs[:n], refs[n:2 * n]
        send_sems, recv_sems, local_sems = refs[2 * n:]
        x, y, c = lax.axis_index("x"), lax.axis_index("y"), lax.axis_index("c")
        me = 4 * x + 2 * y + c
        local, remote = [], []
        for a in range(n):
            lc = pltpu.make_async_copy(ins[a].at[me] if a < ns else ins[a], outs[a].at[me], local_sems.at[a])
            lc.start()
            local.append(lc)
            for r in range(1, NDEV):
                px = 1 - x if r & 4 else x
                py = 1 - y if r & 2 else y
                pc = 1 - c if r & 1 else c
                cp = pltpu.make_async_remote_copy(
                    src_ref=ins[a].at[4 * px + 2 * py + pc] if a < ns else ins[a],
                    dst_ref=outs[a].at[me],
                    send_sem=send_sems.at[a, r - 1],
                    recv_sem=recv_sems.at[a, r - 1],
                    device_id=(px, py, pc),
                    device_id_type=pl.DeviceIdType.MESH,
                )
                cp.start()
                remote.append(cp)
        for cp in remote:
            cp.wait_recv()
        for cp in remote:
            cp.wait_send()
        for lc in local:
            lc.wait()

    out_shape = [jax.ShapeDtypeStruct(a.shape if i < ns else (NDEV,) + a.shape, a.dtype) for i, a in enumerate(arrs)]
    return pl.pallas_call(
        body,
        in_specs=[pl.BlockSpec(memory_space=pl.ANY)] * n,
        out_specs=[pl.BlockSpec(memory_space=pl.ANY)] * n,
        out_shape=out_shape,
        scratch_shapes=[
            pltpu.SemaphoreType.DMA((n, NDEV - 1)),
            pltpu.SemaphoreType.DMA((n, NDEV - 1)),
            pltpu.SemaphoreType.DMA((n,)),
        ],
        compiler_params=pltpu.CompilerParams(has_side_effects=True),
        name=name,
    )(*arrs)


_HBM = pl.BlockSpec(memory_space=pltpu.HBM)
_SEM = pl.BlockSpec(memory_space=pltpu.SEMAPHORE)
_EFFECT = pltpu.SideEffectType.DATAFLOW_SIDE_EFFECTING


def _comm_step(name, body_fn, lands, srcs=(), wait_sems=(), n_new=0, after=(), token=False):
    n, ns, nw, na = len(lands), len(srcs), len(wait_sems), len(after)

    def body(*refs):
        src, land = refs[:ns], refs[ns:ns + n]
        waits = refs[ns + n:ns + n + nw]
        new = refs[ns + n + nw + na:ns + n + nw + na + n_new]
        body_fn(src, land, waits, new)
        if token:
            refs[-1][...] = jnp.zeros((8, LANES), f32)

    hbm = [pltpu.HBM(a.shape, a.dtype) for a in lands]
    ops = [pltpu.with_memory_space_constraint(a, pltpu.HBM) for a in list(srcs) + list(lands)]
    extra_shape = [jax.ShapeDtypeStruct((8, LANES), f32)] if token else []
    extra_spec = [pl.BlockSpec(memory_space=pltpu.VMEM)] if token else []
    outs = pl.pallas_call(
        body,
        out_shape=tuple([pltpu.SemaphoreType.DMA(())] * n_new + hbm + extra_shape),
        in_specs=[_HBM] * (ns + n) + [_SEM] * nw + [pl.BlockSpec(memory_space=pl.ANY)] * na,
        out_specs=tuple([_SEM] * n_new + [_HBM] * n + extra_spec),
        input_output_aliases={ns + i: n_new + i for i in range(n)},
        compiler_params=pltpu.CompilerParams(has_side_effects=_EFFECT),
        name=name,
    )(*ops, *wait_sems, *after)
    if token:
        return list(outs[:n_new]), list(outs[n_new:n_new + n]), outs[-1][0, 0]
    return list(outs[:n_new]), list(outs[n_new:])


class _GatheredWeights:
    def __init__(self, shards):
        self.n = n = len(shards)
        x, y, c = lax.axis_index("x"), lax.axis_index("y"), lax.axis_index("c")
        self.x = x
        me = 4 * x + 2 * y + c
        lands = [lax.dynamic_update_slice(lax.empty((NDEV,) + s.shape, s.dtype), s[None], (me,) + (0,) * s.ndim)
                 for s in shards]

        def start_own(src, land, waits, new):
            p = self._peers()
            for a in range(n):
                for k, to in ((0, p["sibling"]), (1, p["xn"]), (2, p["yn"])):
                    self._copy(land[a], new, a, k, 3, p["me"], to).start()

        self.sems, self.lands = {}, None
        new, self.lands = _comm_step("gather_start", start_own, lands, n_new=6 * n)
        self._keep(new, (0, 1, 2))

    @staticmethod
    def _peers():
        x, y, c = lax.axis_index("x"), lax.axis_index("y"), lax.axis_index("c")
        return dict(
            me=(x, y, c), sibling=(x, y, 1 - c), xn=(1 - x, y, c), yn=(x, 1 - y, c), dg=(1 - x, 1 - y, c),
            relay_origin=(jnp.bitwise_xor(x, c), jnp.bitwise_xor(y, 1 - c), c),
            relay_target=(jnp.bitwise_xor(x, 1 - c), jnp.bitwise_xor(y, c), c))

    def _keep(self, new, ks):
        half = len(new) // 2
        i = 0
        for a in range(self.n):
            for k in ks:
                self.sems[a, k] = (new[i], new[half + i])
                i += 1

    @staticmethod
    def _copy(land, sem_refs, a, k, nk, block, to, src=None, ks=None):
        ks = tuple(range(nk)) if ks is None else ks
        half = len(sem_refs) // 2
        i = a * len(ks) + ks.index(k)
        slot = land.at[4 * block[0] + 2 * block[1] + block[2]]
        return pltpu.make_async_remote_copy(
            src_ref=slot if src is None else src, dst_ref=slot, send_sem=sem_refs[i], recv_sem=sem_refs[half + i],
            device_id=to, device_id_type=pl.DeviceIdType.MESH)

    def _sem_list(self, ks):
        return ([self.sems[a, k][0] for a in range(self.n) for k in ks]
                + [self.sems[a, k][1] for a in range(self.n) for k in ks])

    def first_half(self, after):
        n = self.n

        def relay(src, land, waits, new):
            p = self._peers()
            for a in range(n):
                self._copy(land[a], waits, a, 1, 0, p["xn"], p["me"], ks=(1, 2)).wait_recv()
                self._copy(land[a], waits, a, 2, 0, p["yn"], p["me"], ks=(1, 2)).wait_recv()
                self._copy(land[a], new, a, 3, 0, p["relay_origin"], p["relay_target"], ks=(3, 4, 5)).start()
                self._copy(land[a], new, a, 4, 0, p["xn"], p["sibling"], ks=(3, 4, 5)).start()
                self._copy(land[a], new, a, 5, 0, p["yn"], p["sibling"], ks=(3, 4, 5)).start()

        new, self.lands = _comm_step("gather_relay", relay, self.lands, wait_sems=self._sem_list((1, 2)),
                                     n_new=6 * n, after=after)
        self._keep(new, (3, 4, 5))

        def from_sibling(src, land, waits, new):
            p = self._peers()
            other = lambda b: (b[0], b[1], 1 - b[2])
            for a in range(n):
                self._copy(land[a], waits, a, 0, 0, other(p["me"]), p["me"], ks=(0, 4, 5)).wait_recv()
                self._copy(land[a], waits, a, 4, 0, other(p["xn"]), p["me"], ks=(0, 4, 5)).wait_recv()
                self._copy(land[a], waits, a, 5, 0, other(p["yn"]), p["me"], ks=(0, 4, 5)).wait_recv()

        _, self.lands = _comm_step("gather_wait_sibling", from_sibling, self.lands,
                                   wait_sems=self._sem_list((0, 4, 5)))
        return self.lands[0].reshape(NW, D), self.x.astype(jnp.int32).reshape(1)

    def second_half(self, after):
        n = self.n

        def forward_diagonal(src, land, waits, new):
            p = self._peers()
            for a in range(n):
                self._copy(land[a], waits, a, 3, 0, p["dg"], p["me"], ks=(3,)).wait_recv()
                self._copy(land[a], new, a, 6, 0, p["dg"], p["sibling"], ks=(6,)).start()

        new, self.lands = _comm_step("gather_forward_diagonal", forward_diagonal, self.lands,
                                     wait_sems=self._sem_list((3,)), n_new=2 * n, after=after)
        self._keep(new, (6,))

        def finish(src, land, waits, new):
            p = self._peers()
            ks = tuple(range(7))
            for a in range(n):
                self._copy(land[a], waits, a, 6, 0, (p["dg"][0], p["dg"][1], 1 - p["dg"][2]), p["me"], ks=ks).wait_recv()
                for k in ks:
                    self._copy(land[a], waits, a, k, 0, p["me"], p["me"], ks=ks).wait_send()

        _, self.lands = _comm_step("gather_finish", finish, self.lands, wait_sems=self._sem_list(tuple(range(7))))
        return self.lands[0].reshape(NW, D), (1 - self.x).astype(jnp.int32).reshape(1)

    def rest(self):
        g_a, g_b, g_o, g_bm = self.lands[1:]
        return (g_a.transpose(1, 0, 2).reshape(512, D), g_b.transpose(1, 0, 2).reshape(512, D),
                g_bm.transpose(1, 0, 2).reshape(2, D), g_o.reshape(D, D))


def _sibling_send_start(shares):
    landing = lax.empty(shares.shape, shares.dtype)

    def start(src, land, waits, new):
        x, y, c = lax.axis_index("x"), lax.axis_index("y"), lax.axis_index("c")
        pltpu.make_async_remote_copy(src_ref=land[0], dst_ref=land[1], send_sem=new[0], recv_sem=new[1],
                                     device_id=(x, y, 1 - c), device_id_type=pl.DeviceIdType.MESH).start()

    return _comm_step("grad_sibling_start", start, [shares, landing], n_new=2, token=True)


def _sibling_send_wait(sems, lands, after):
    def wait(src, land, waits, new):
        x, y, c = lax.axis_index("x"), lax.axis_index("y"), lax.axis_index("c")
        done = pltpu.make_async_remote_copy(src_ref=land[0], dst_ref=land[1], send_sem=waits[0], recv_sem=waits[1],
                                            device_id=(x, y, c), device_id_type=pl.DeviceIdType.MESH)
        done.wait_send()
        done.wait_recv()

    _, lands = _comm_step("grad_sibling_wait", wait, lands, wait_sems=sems, after=after)
    return lands[1]


def _row_tile(rows, limit=256):
    fits = [t for t in range(16, limit + 1, 16) if rows % t == 0]
    return fits[-1] if fits else rows


def _pair_sum(mine, theirs, name):
    nb, rows, cols = mine.shape
    tr = _row_tile(rows, 528)

    def body(a_ref, b_ref, o_ref):
        o_ref[...] = (a_ref[...].astype(f32) + b_ref[...].astype(f32)).astype(bf16)

    blk = pl.BlockSpec((None, tr, cols), lambda q, i: (q, i, 0))
    return pl.pallas_call(
        body,
        grid=(nb, rows // tr),
        in_specs=[blk, blk],
        out_specs=blk,
        out_shape=jax.ShapeDtypeStruct(mine.shape, bf16),
        compiler_params=_params(("arbitrary", "arbitrary")),
        name=name,
    )(mine, theirs)


def _scatter_start(chip_arrs, all_arrs, name):
    arrs = list(chip_arrs) + list(all_arrs)
    n, nc = len(arrs), len(chip_arrs)
    lands = [lax.empty(((3 if i < nc else NDEV - 1),) + a.shape[1:], a.dtype) for i, a in enumerate(arrs)]

    def body(*refs):
        src, land = refs[:n], refs[n:2 * n]
        send_sems, recv_sems = refs[2 * n:3 * n], refs[3 * n:4 * n]
        token = refs[6 * n]
        x, y, c = lax.axis_index("x"), lax.axis_index("y"), lax.axis_index("c")
        for a in range(n):
            for r in range(1, 4 if a < nc else NDEV):
                if a < nc:
                    px, py, pc = (1 - x if r & 2 else x), (1 - y if r & 1 else y), c
                    block = 2 * px + py
                else:
                    px, py, pc = (1 - x if r & 4 else x), (1 - y if r & 2 else y), (1 - c if r & 1 else c)
                    block = 4 * px + 2 * py + pc
                pltpu.make_async_remote_copy(
                    src_ref=src[a].at[block], dst_ref=land[a].at[r - 1], send_sem=send_sems[a],
                    recv_sem=recv_sems[a], device_id=(px, py, pc), device_id_type=pl.DeviceIdType.MESH).start()
        token[...] = jnp.zeros_like(token)

    hbm = [pltpu.HBM(a.shape, a.dtype) for a in arrs + lands]
    ops = [pltpu.with_memory_space_constraint(a, pltpu.HBM) for a in arrs + lands]
    outs = pl.pallas_call(
        body,
        out_shape=tuple([pltpu.SemaphoreType.DMA(())] * (2 * n) + hbm + [jax.ShapeDtypeStruct((8, LANES), f32)]),
        in_specs=[_HBM] * (2 * n),
        out_specs=tuple([_SEM] * (2 * n) + [_HBM] * (2 * n) + [pl.BlockSpec(memory_space=pltpu.VMEM)]),
        input_output_aliases={i: 2 * n + i for i in range(2 * n)},
        compiler_params=pltpu.CompilerParams(has_side_effects=_EFFECT),
        name=name,
    )(*ops)
    return outs[:n], outs[n:2 * n], outs[2 * n:3 * n], outs[3 * n:4 * n], outs[4 * n]


def _scatter_wait(send_sems, recv_sems, srcs, lands, after, name):
    n = len(srcs)

    def body(*refs):
        land = refs[n:2 * n]
        ssem, rsem = refs[2 * n:3 * n], refs[3 * n:4 * n]
        x, y, c = lax.axis_index("x"), lax.axis_index("y"), lax.axis_index("c")
        for a in range(n):
            done = pltpu.make_async_remote_copy(
                src_ref=land[a], dst_ref=land[a], send_sem=ssem[a], recv_sem=rsem[a], device_id=(x, y, c),
                device_id_type=pl.DeviceIdType.MESH)
            done.wait_send()
            done.wait_recv()

    hbm = [pltpu.HBM(a.shape, a.dtype) for a in list(srcs) + list(lands)]
    outs = pl.pallas_call(
        body,
        out_shape=tuple(hbm),
        in_specs=[_HBM] * (2 * n) + [_SEM] * (2 * n) + [pl.BlockSpec(memory_space=pl.ANY)],
        out_specs=tuple([_HBM] * (2 * n)),
        input_output_aliases={i: i for i in range(2 * n)},
        compiler_params=pltpu.CompilerParams(has_side_effects=_EFFECT),
        name=name,
    )(*srcs, *lands, *send_sems, *recv_sems, after)
    return outs[:n], outs[n:]


def _adam_update(g, w_ref, m_ref, v_ref, g_ref, d_ref, nm_ref, nv_ref):
    mm = ADAM_B1 * m_ref[...] + (1.0 - ADAM_B1) * g
    vv = ADAM_B2 * v_ref[...] + (1.0 - ADAM_B2) * (g * g)
    m_hat = mm / (1.0 - ADAM_B1 ** ADAM_STEP)
    v_hat = vv / (1.0 - ADAM_B2 ** ADAM_STEP)
    g_ref[...] = g
    d_ref[...] = -ADAM_LR * (m_hat / (jnp.sqrt(v_hat) + ADAM_EPS) + ADAM_WD * w_ref[...])
    nm_ref[...] = mm
    nv_ref[...] = vv


def _adamw_own(w, own, own_idx, slots, m, v, name):
    r, c = w.shape[-2:]
    tr = _row_tile(r, 384)
    k = slots.shape[0]

    def body(i_ref, w_ref, o_ref, s_ref, m_ref, v_ref, g_ref, d_ref, nm_ref, nv_ref):
        del i_ref
        g = o_ref[...].astype(f32)
        for j in range(k):
            g = g + s_ref[j].astype(f32)
        _adam_update(g, w_ref, m_ref, v_ref, g_ref, d_ref, nm_ref, nv_ref)

    blk = pl.BlockSpec((None, tr, c), lambda i, ix: (0, i, 0))
    return pl.pallas_call(
        body,
        grid_spec=pltpu.PrefetchScalarGridSpec(
            num_scalar_prefetch=1,
            grid=(r // tr,),
            in_specs=[blk, pl.BlockSpec((None, tr, c), lambda i, ix: (ix[0], i, 0)),
                      pl.BlockSpec((k, tr, c), lambda i, ix: (0, i, 0)), blk, blk],
            out_specs=[blk] * 4,
        ),
        out_shape=[jax.ShapeDtypeStruct(w.shape, f32)] * 4,
        compiler_params=_params(("arbitrary",)),
        name=name,
    )(own_idx, w, own, slots, m, v)


def _adamw(w, slots, m, v, name):
    r, c = w.shape[-2:]
    tr = _row_tile(r, 128)

    def body(w_ref, s_ref, m_ref, v_ref, g_ref, d_ref, nm_ref, nv_ref):
        g = s_ref[0].astype(f32)
        for k in range(1, NDEV):
            g = g + s_ref[k].astype(f32)
        _adam_update(g, w_ref, m_ref, v_ref, g_ref, d_ref, nm_ref, nv_ref)

    if w.ndim == 3:
        blk = pl.BlockSpec((None, tr, c), lambda i: (0, i, 0))
    else:
        blk = pl.BlockSpec((tr, c), lambda i: (i, 0))
    return pl.pallas_call(
        body,
        grid=(r // tr,),
        in_specs=[blk, pl.BlockSpec((NDEV, tr, c), lambda i: (0, i, 0)), blk, blk],
        out_specs=[blk] * 4,
        out_shape=[jax.ShapeDtypeStruct(w.shape, f32)] * 4,
        compiler_params=_params(("arbitrary",)),
        name=name,
    )(w, slots, m, v)


class _Weights:
    def __init__(self, w_t, w_a, w_b, b_merge, w_o):
        self._w_t, self._rest = w_t, (w_a, w_b, b_merge, w_o)

    def first_half(self, after):
        del after
        return self._w_t, jnp.zeros((1,), jnp.int32)

    def second_half(self, after):
        del after
        return self._w_t, jnp.ones((1,), jnp.int32)

    def rest(self):
        return self._rest


def _local_step(x, tgt, norm_gain, weights, qn_a, kn_a, qn_b, kn_b, sink_a, rel_bias, on_weight_grads=None,
                core=None):
    two = lambda t: jnp.concatenate([t, t], axis=-1).reshape(1, LANES)
    ones = jnp.ones((1, LANES), f32)
    gains = jnp.stack([
        jnp.stack([two(qn_a), two(kn_a), ones]),
        jnp.stack([two(qn_b), two(kn_b), ones]),
        jnp.stack([two(qn_b), two(kn_b), ones]),
        jnp.stack([two(qn_b), two(kn_b), ones]),
    ])
    buckets = [jnp.asarray(_bucket_np(blk, d)) for blk, d, _ in GROUPS]
    bias = [_bias_expand(rel_bias, buckets[k], GROUPS[k][2], "bias_expand_%d" % k) for k in range(4)]

    hb, hbt, rstd = _rms(x, norm_gain)
    w_t, half = weights.first_half([hb] + bias)
    proj = _inproj_half(hb, w_t, half, None, "inproj_1")
    w_t, half = weights.second_half([proj])
    proj = _inproj_half(hb, w_t, half, proj, "inproj_2")
    w_a, w_b, b_merge, w_o = weights.rest()
    gl = _prep(proj, gains)
    o_a, l_a = _attn_fwd(gl, bias[0], sink_a.reshape(8), 0, 128, 1, "attn_fwd_a")
    fwd_b = [_attn_fwd(gl, bias[k], None, k, GROUPS[k][0], GROUPS[k][1], "attn_fwd_b%d" % k) for k in (1, 2, 3)]
    sink_b = jnp.repeat(sink_a.reshape(8), HD).reshape(1, 512)

    (dy, dyb, dproj, do_a, dd_a, do_b0, do_b1, do_b2, dd_b0, dd_b1, dd_b2, ya, yb, mg, dbr_a, dbr_b, loss, dbm,
     dsk) = _tail(x, tgt, o_a, l_a, [f[0] for f in fwd_b], [f[1] for f in fwd_b], proj, b_merge, w_a, w_b, w_o, sink_b)

    dw_o = _matmul_tokens(mg, dyb, "dw_out")
    dw_a = _matmul_tokens(ya, dbr_a, "dw_branch_a")
    dw_b = _matmul_tokens(yb, dbr_b, "dw_branch_b")
    if on_weight_grads is not None:
        early = on_weight_grads(dict(w_branch_a=dw_a, w_branch_b=dw_b, b_merge=dbm, w_out=dw_o))
        buckets = [buckets[0] + early.astype(jnp.int32)] + buckets[1:]

    dqkv_a, dbk_a = _attn_bwd(gl, bias[0], buckets[0], do_a, l_a, dd_a, 0, 128, 1, "attn_bwd_a")
    dproj, dg_a = _post_a(dqkv_a, proj, gains, dproj)
    dbk_b, dg_b = [], []
    for k, do_k, dd_k in ((1, do_b0, dd_b0), (2, do_b1, dd_b1), (3, do_b2, dd_b2)):
        dqkv, dbk = _attn_bwd(gl, bias[k], buckets[k], do_k, fwd_b[k - 1][1], dd_k, k, GROUPS[k][0], GROUPS[k][1],
                              "attn_bwd_b%d" % k)
        dproj, dg = _post_b(k, dqkv, proj, gains, dproj)
        dbk_b.append(dbk)
        dg_b.append(dg)
    dg_b = jnp.stack(dg_b)

    core = jnp.zeros((1,), jnp.int32) if core is None else core
    dw_other = _dw_in(hbt, dproj, 1 - core, "dw_in_other")
    sent = jnp.zeros((), f32) if on_weight_grads is None else on_weight_grads(dict(w_in_other=dw_other))
    dw_in = _dw_in(hbt, dproj, core + sent.astype(jnp.int32), "dw_in_own")
    token = jnp.zeros((), f32) if on_weight_grads is None else on_weight_grads(dict(w_in=dw_in))
    grad_x, d_norm_gain = _dh_norm_bwd(dproj, w_t, x, rstd, norm_gain + token, dy)

    fold = lambda t: t[..., :HD] + t[..., HD:]
    d_qn_a = fold(dg_a[0, 0])
    d_kn_a = fold(dg_a[1, 0])
    d_qn_b = fold(dg_b[:, 0, 0].sum(axis=0))
    d_kn_b = fold(dg_b[:, 1, 0].sum(axis=0))
    d_sink = dsk.reshape(8, HD)[:, 0]
    red = jnp.stack([dbk_a] + dbk_b)
    d_rel = red[:, :, 0, :32].reshape(32, 32).T
    return dict(loss=loss, grad_x=grad_x, norm_gain=d_norm_gain, w_in=dw_in, w_in_other=dw_other, q_norm_a=d_qn_a,
                k_norm_a=d_kn_a,
                q_norm_b=d_qn_b, k_norm_b=d_kn_b, sink_a=d_sink, rel_bias=d_rel, w_branch_a=dw_a, w_branch_b=dw_b,
                b_merge=dbm, w_out=dw_o)


SMALL = (("norm_gain", D), ("q_norm_a", HD), ("k_norm_a", HD), ("q_norm_b", HD), ("k_norm_b", HD), ("sink_a", 8),
         ("rel_bias", 1024))
SMALL_PAD = 2432


SMALL_USED = sum(sz for _, sz in SMALL)


def _pack_small(parts, loss=None):
    tail = jnp.zeros((SMALL_PAD - SMALL_USED,), f32)
    if loss is not None:
        tail = tail.at[0].set(loss.reshape(()))
    return jnp.concatenate([parts[n].reshape(-1) for n, _ in SMALL] + [tail]).reshape(1, SMALL_PAD)


def _unpack_small(flat, shapes):
    out, off = {}, 0
    for n, sz in SMALL:
        out[n] = flat[0, off:off + sz].reshape(shapes[n])
        off += sz
    return out


def kernel(x, norm_gain, w_in, q_norm_a, k_norm_a, q_norm_b, k_norm_b, sink_a, rel_bias, w_branch_a, w_branch_b, b_merge, w_out, loss_target, m_norm_gain, m_w_in, m_q_norm_a, m_k_norm_a, m_q_norm_b, m_k_norm_b, m_sink_a, m_rel_bias, m_w_branch_a, m_w_branch_b, m_b_merge, m_w_out, v_norm_gain, v_w_in, v_q_norm_a, v_k_norm_a, v_q_norm_b, v_k_norm_b, v_sink_a, v_rel_bias, v_w_branch_a, v_w_branch_b, v_b_merge, v_w_out):
    csh = D // NDEV
    w_in_t, m_w_in_t, v_w_in_t = (jnp.swapaxes(t, 1, 2) for t in (w_in, m_w_in, v_w_in))
    weights = _GatheredWeights([w_in_t[0].astype(bf16), w_branch_a[0].astype(bf16), w_branch_b[0].astype(bf16),
                                w_out[0].astype(bf16), b_merge[0]])

    pending = {}
    core = lax.axis_index("c").astype(jnp.int32).reshape(1)
    chip = (2 * lax.axis_index("x") + lax.axis_index("y")).astype(jnp.int32).reshape(1)
    me = (2 * chip + core).astype(jnp.int32)

    def start_exchange(gw):
        if "w_in_other" in gw:
            sems, lands, sent = _sibling_send_start(gw["w_in_other"])
            pending["sibling"] = (sems, lands)
            return sent
        if "w_in" in gw:
            from_sibling = _sibling_send_wait(*pending["sibling"], after=[gw["w_in"]])
            chip_sums = _pair_sum(gw["w_in"], from_sibling, "grad_pair_sum")
            pending["w_in"] = _scatter_start([chip_sums], [], "scatter_w_in_start")
            return pending["w_in"][4][0, 0]
        blocks = [gw["w_branch_a"].reshape(512, NDEV, csh).transpose(1, 0, 2).astype(bf16),
                  gw["w_branch_b"].reshape(512, NDEV, csh).transpose(1, 0, 2).astype(bf16),
                  gw["w_out"].reshape(NDEV, csh, D).astype(bf16),
                  gw["b_merge"].reshape(2, NDEV, csh).transpose(1, 0, 2)]
        pending["rest"] = _scatter_start([], blocks, "scatter_rest_start")
        return pending["rest"][4][0, 0]

    loc = _local_step(x[0], loss_target[0], norm_gain, weights, q_norm_a, k_norm_a, q_norm_b, k_norm_b, sink_a,
                      rel_bias, on_weight_grads=start_exchange, core=core)

    small_shapes = dict(norm_gain=(1, D), q_norm_a=(1, HD), k_norm_a=(1, HD), q_norm_b=(1, HD), k_norm_b=(1, HD),
                        sink_a=(1, 8), rel_bias=(32, 32))
    (r_small,) = _exchange([], [_pack_small(loc, loc["loss"])], "gather_small_grads")
    send_sems, recv_sems, srcs, lands, _ = pending["rest"]
    (s_a, s_b, s_o, s_bm), (r_a, r_b, r_o, r_bm) = _scatter_wait(
        send_sems, recv_sems, srcs, lands, r_small, "scatter_rest_wait")
    send_sems, recv_sems, srcs, lands, _ = pending["w_in"]
    (s_in,), (r_in,) = _scatter_wait(send_sems, recv_sems, srcs, lands, r_small, "scatter_w_in_wait")

    given = dict(norm_gain=norm_gain, q_norm_a=q_norm_a, k_norm_a=k_norm_a, q_norm_b=q_norm_b, k_norm_b=k_norm_b,
                 sink_a=sink_a, rel_bias=rel_bias)
    m_small = dict(norm_gain=m_norm_gain, q_norm_a=m_q_norm_a, k_norm_a=m_k_norm_a, q_norm_b=m_q_norm_b,
                   k_norm_b=m_k_norm_b, sink_a=m_sink_a, rel_bias=m_rel_bias)
    v_small = dict(norm_gain=v_norm_gain, q_norm_a=v_q_norm_a, k_norm_a=v_k_norm_a, q_norm_b=v_q_norm_b,
                   k_norm_b=v_k_norm_b, sink_a=v_sink_a, rel_bias=v_rel_bias)
    res = {
        "small": _adamw(_pack_small(given), r_small, _pack_small(m_small), _pack_small(v_small), "adamw_small"),
        "w_in": [jnp.swapaxes(t, 1, 2) for t in
                 _adamw_own(w_in_t, s_in, chip, r_in, m_w_in_t, v_w_in_t, "adamw_w_in")],
        "w_branch_a": _adamw_own(w_branch_a, s_a, me, r_a, m_w_branch_a, v_w_branch_a, "adamw_w_branch_a"),
        "w_branch_b": _adamw_own(w_branch_b, s_b, me, r_b, m_w_branch_b, v_w_branch_b, "adamw_w_branch_b"),
        "b_merge": _adamw_own(b_merge, s_bm, me, r_bm, m_b_merge, v_b_merge, "adamw_b_merge"),
        "w_out": _adamw_own(w_out, s_o, me, r_o, m_w_out, v_w_out, "adamw_w_out"),
    }
    order = ["norm_gain", "w_in", "q_norm_a", "k_norm_a", "q_norm_b", "k_norm_b", "sink_a", "rel_bias", "w_branch_a",
             "w_branch_b", "b_merge", "w_out"]
    outs = []
    for k in range(4):
        small = _unpack_small(res["small"][k], small_shapes)
        for n in order:
            outs.append(small[n] if n in small else res[n][k])
    loss = res["small"][0][0, SMALL_USED]
    return (loss, loc["grad_x"][None], *outs)
```

```python
import math

import numpy as np
import jax
import jax.numpy as jnp
from jax import lax
from jax.experimental import pallas as pl
from jax.experimental.pallas import tpu as pltpu

f32 = jnp.float32
bf16 = jnp.bfloat16

S = 4096
D = 1024
NA = 5376
NT = 3072
NW = NA + NT
WSH = NW // 8
HD = 64
LANES = 128
EPS = 1e-6
NEG = -1e30
SCALE = HD ** -0.5
TQ = 128
PAD = 128
SP = S + 2 * PAD
NDEV = 8
GROUPS = ((128, 1, 0), (64, 1, 8), (64, 4, 16), (64, 16, 24))
CHUNK = 256
PCHUNK = 128
RC = 64

ADAM_LR, ADAM_B1, ADAM_B2, ADAM_EPS, ADAM_WD, ADAM_STEP = 0.001, 0.9, 0.999, 1e-08, 0.01, 10

MIB = 1024 * 1024
NT_DIMS = (((1,), (1,)), ((), ()))
TN_DIMS = (((0,), (0,)), ((), ()))


def _params(sem=None, vmem_mib=48):
    return pltpu.CompilerParams(dimension_semantics=sem, vmem_limit_bytes=vmem_mib * MIB)


def _lo():
    return lax.broadcasted_iota(jnp.int32, (1, LANES), 1) < HD


def _head_ones():
    r = lax.broadcasted_iota(jnp.int32, (LANES, LANES), 0) // HD
    c = lax.broadcasted_iota(jnp.int32, (LANES, LANES), 1) // HD
    return jnp.where(r == c, 1.0, 0.0).astype(bf16)


def _half_sums(x, ones):
    hi = x.astype(bf16)
    mid = (x - hi.astype(f32)).astype(bf16)
    return (jnp.dot(hi, ones, preferred_element_type=f32) + jnp.dot(mid, ones, preferred_element_type=f32))


def _seg_sum(x, ones):
    outs = [_half_sums(x[:, b * LANES:(b + 1) * LANES], ones) for b in range(x.shape[1] // LANES)]
    return outs[0] if len(outs) == 1 else jnp.concatenate(outs, axis=1)


def _bucket_np(blk, stride):
    w = TQ + 2 * blk
    rel = np.arange(w)[None, :] - blk - np.arange(TQ)[:, None]
    band = np.abs(rel) <= blk
    r = rel * stride
    n = np.abs(r)
    nf = np.maximum(n, 8).astype(np.float32)
    large = 8 + (np.log(nf / np.float32(8)) / np.float32(math.log(128.0)) * np.float32(8)).astype(np.int32)
    large = np.minimum(large, 15)
    b = (r > 0).astype(np.int32) * 16 + np.where(n < 8, n, large)
    return np.where(band, b, -1).astype(np.int32)


def _rms(x, gain):
    ts = 512

    def body(x_ref, g_ref, h_ref, ht_ref, r_ref):
        xv = x_ref[...]
        r = lax.rsqrt(jnp.mean(xv * xv, axis=-1, keepdims=True) + EPS)
        h = (xv * r) * g_ref[...]
        h_ref[...] = h.astype(bf16)
        ht_ref[...] = h.T.astype(bf16)
        r_ref[...] = r

    return pl.pallas_call(
        body,
        grid=(S // ts,),
        in_specs=[pl.BlockSpec((ts, D), lambda i: (i, 0)), pl.BlockSpec((1, D), lambda i: (0, 0))],
        out_specs=[pl.BlockSpec((ts, D), lambda i: (i, 0)), pl.BlockSpec((D, ts), lambda i: (0, i)),
                   pl.BlockSpec((ts, 1), lambda i: (i, 0))],
        out_shape=[jax.ShapeDtypeStruct((S, D), bf16), jax.ShapeDtypeStruct((D, S), bf16),
                   jax.ShapeDtypeStruct((S, 1), f32)],
        compiler_params=_params(("arbitrary",)),
        name="rms",
    )(x, gain)


def _inproj_half(hb, w_t, half, proj, name):
    ts = 512
    tn = NW // 2
    per = NW // 2 // tn

    def body(h_idx, h_ref, w_ref, *rest):
        del h_idx
        rest[-1][...] = lax.dot_general(h_ref[...], w_ref[...], NT_DIMS, preferred_element_type=f32)

    in_specs = [pl.BlockSpec((ts, D), lambda i, n, hf: (i, 0)),
                pl.BlockSpec((tn, D), lambda i, n, hf: (hf[0] * per + n, 0))]
    args = [half, hb, w_t]
    aliases = {}
    if proj is not None:
        in_specs.append(pl.BlockSpec(memory_space=pl.ANY))
        args.append(proj)
        aliases = {3: 0}
    return pl.pallas_call(
        body,
        grid_spec=pltpu.PrefetchScalarGridSpec(
            num_scalar_prefetch=1,
            grid=(S // ts, per),
            in_specs=in_specs,
            out_specs=pl.BlockSpec((ts, tn), lambda i, n, hf: (i, hf[0] * per + n)),
        ),
        out_shape=jax.ShapeDtypeStruct((S, NW), f32),
        input_output_aliases=aliases,
        compiler_params=_params(("arbitrary", "arbitrary")),
        name=name,
    )(*args)


def _bias_expand(table, bucket, c0, name):
    tq, w = bucket.shape
    blk = (w - tq) // 2

    def body(tab_ref, bk_ref, o_ref):
        h = pl.program_id(0)
        bk = bk_ref[...]

        def step(b, acc):
            return jnp.where(bk == b, tab_ref[b, c0 + h], acc)

        inner = lax.fori_loop(0, 32, step, jnp.full((tq, w), NEG, f32))
        col = lax.broadcasted_iota(jnp.int32, (1, w), 1)
        o_ref[0] = jnp.where(col < blk, NEG, inner)
        o_ref[1] = inner
        o_ref[2] = jnp.where(col >= tq + blk, NEG, inner)

    return pl.pallas_call(
        body,
        grid=(8,),
        in_specs=[pl.BlockSpec(memory_space=pltpu.SMEM), pl.BlockSpec((tq, w), lambda h: (0, 0))],
        out_specs=pl.BlockSpec((3, None, tq, w), lambda h: (0, h, 0, 0)),
        out_shape=jax.ShapeDtypeStruct((3, 8, tq, w), f32),
        compiler_params=_params(("arbitrary",)),
        name=name,
    )(table, bucket)


def _tile_kind(t, seq):
    m0 = jnp.bitwise_and(t * TQ, seq - 1)
    return jnp.where(m0 == 0, 0, jnp.where(m0 == seq - TQ, 2, 1))


def _col_block(g, j):
    kind = j // 4
    hp = j % 4
    a = jnp.where(kind == 0, hp, 3 + kind)
    b = 6 + 12 * kind + 4 * (g - 1) + hp
    return jnp.where(g == 0, a, b)


def _prep(proj_a, gains):
    def body(p0_ref, p1_ref, p2_ref, p3_ref, g_ref, o_ref):
        g = pl.program_id(0)
        kind = pl.program_id(1)
        lo = _lo()
        ones = _head_ones()
        half = jnp.where(lo, 0, 1)
        gain = g_ref[...]

        def norm_store(xv, u, dst, dup):
            if dup:
                take = (kind == 0) | (half == u // 2)
                xv = jnp.where(take, xv, pltpu.roll(xv, HD, 1))
            r = lax.rsqrt(_half_sums(xv * xv, ones) * (1.0 / HD) + EPS)
            r = jnp.where(kind == 2, 1.0, r)
            yv = (xv * r) * gain
            yv = jnp.where(kind == 0, yv * SCALE, yv)
            o_ref[u, PAD + dst:PAD + dst + CHUNK, :] = yv.astype(bf16)

        for u in range(4):
            o_ref[u, 0:PAD, :] = jnp.zeros((PAD, LANES), bf16)
            o_ref[u, PAD + S:SP, :] = jnp.zeros((PAD, LANES), bf16)
        for gi, (_, d, _) in enumerate(GROUPS):
            @pl.when(g == gi)
            def _():
                seq = S // d
                for u, p_ref in enumerate((p0_ref, p1_ref, p2_ref, p3_ref)):
                    for c in range(d):
                        for i in range(seq // CHUNK):
                            if d == 1:
                                xv = p_ref[i * CHUNK:(i + 1) * CHUNK, :]
                            else:
                                xv = p_ref[pl.ds(c + i * CHUNK * d, CHUNK, stride=d), :]
                            norm_store(xv, u, c * seq + i * CHUNK, gi == 0)

    return pl.pallas_call(
        body,
        grid=(4, 3),
        in_specs=[pl.BlockSpec((S, LANES), lambda g, kind, u=u: (0, _col_block(g, 4 * kind + u))) for u in range(4)] + [
            pl.BlockSpec((None, None, 1, LANES), lambda g, kind: (g, kind, 0, 0)),
        ],
        out_specs=pl.BlockSpec((None, 4, SP, LANES), lambda g, kind: (g, kind, 0, 0)),
        out_shape=jax.ShapeDtypeStruct((4, 12, SP, LANES), bf16),
        compiler_params=_params(("arbitrary", "arbitrary")),
        name="prep",
    )(proj_a, proj_a, proj_a, proj_a, gains)


def _token_rows(t, r0, n, d):
    if d == 1:
        return pl.ds(pl.multiple_of(t * TQ, TQ) + r0, n)
    per = S // d // TQ
    return pl.ds(((t % per) * TQ + r0) * d + t // per, n, stride=d)


def _stack_heads(t, lo):
    z = jnp.zeros_like(t)
    return jnp.concatenate([jnp.where(lo, t, z), jnp.where(lo, z, t)], axis=0)


def _unstack_heads(t2, lo):
    return jnp.where(lo, t2[:TQ], t2[TQ:])


def _attn_fwd(gl, bias, sink, g, blk, d, name):
    w = TQ + 2 * blk
    seq = S // d
    use_sink = sink is not None

    def body(*refs):
        if use_sink:
            sink_ref, q_ref, k_ref, v_ref, b_ref, o_ref, l_ref, s0, s1, p0, p1, lse_scr = refs
        else:
            q_ref, k_ref, v_ref, b_ref, o_ref, l_ref, s0, s1, p0, p1, lse_scr = refs
        hp = pl.program_id(0)
        lo = _lo()
        s_bufs, p_bufs = (s0, s1), (p0, p1)

        def scores(p, slot):
            for u in range(2):
                f0 = pl.multiple_of((2 * p + u) * TQ, TQ)
                q2 = _stack_heads(q_ref[pl.ds(PAD + f0, TQ), :], lo)
                kw = k_ref[pl.ds(PAD - blk + f0, w), :]
                s_bufs[slot][u] = lax.dot_general(q2, kw, NT_DIMS, preferred_element_type=f32)

        def softmax(p, slot):
            for u in range(2):
                t = 2 * p + u
                kind = _tile_kind(t, seq)
                for h in range(2):
                    for r in range(TQ // RC):
                        rows = slice(h * TQ + r * RC, h * TQ + (r + 1) * RC)
                        logit = s_bufs[slot][u, rows, :] + b_ref[kind, h, r * RC:(r + 1) * RC, :]
                        m = jnp.max(logit, axis=1, keepdims=True)
                        e = jnp.exp(logit - m)
                        lse = m + jnp.log(jnp.sum(e, axis=1, keepdims=True))
                        if use_sink:
                            sk = sink_ref[2 * hp + h]
                            mx = jnp.maximum(lse, sk)
                            lse = mx + jnp.log(jnp.exp(lse - mx) + jnp.exp(sk - mx))
                        p_bufs[slot][u, rows, :] = (e * jnp.exp(m - lse)).astype(bf16)
                        lse_scr[u, rows, :] = jnp.broadcast_to(lse, (RC, LANES))
                l_ref[_token_rows(t, 0, TQ, d), :] = jnp.where(lo, lse_scr[u, 0:TQ, :], lse_scr[u, TQ:2 * TQ, :])

        def values(p, slot):
            for u in range(2):
                t = 2 * p + u
                vw = v_ref[pl.ds(PAD - blk + pl.multiple_of(t * TQ, TQ), w), :]
                o2 = jnp.dot(p_bufs[slot][u], vw, preferred_element_type=f32)
                o_ref[_token_rows(t, 0, TQ, d), :] = _unstack_heads(o2, lo)

        npair = S // TQ // 2
        scores(0, 0)
        scores(1, 1)
        softmax(0, 0)

        def steady(k, carry):
            p = 2 * k + 2
            scores(p, 0)
            softmax(p - 1, 1)
            values(p - 2, 0)
            scores(p + 1, 1)
            softmax(p, 0)
            values(p - 1, 1)
            return carry

        lax.fori_loop(0, (npair - 2) // 2, steady, 0)
        softmax(npair - 1, 1)
        values(npair - 2, 0)
        values(npair - 1, 1)

    in_specs = [
        pl.BlockSpec((None, None, SP, LANES), lambda hp: (g, hp, 0, 0)),
        pl.BlockSpec((None, None, SP, LANES), lambda hp: (g, 4 + hp, 0, 0)),
        pl.BlockSpec((None, None, SP, LANES), lambda hp: (g, 8 + hp, 0, 0)),
        pl.BlockSpec((3, 2, TQ, w), lambda hp: (0, hp, 0, 0)),
    ]
    args = [gl, gl, gl, bias]
    if use_sink:
        in_specs = [pl.BlockSpec(memory_space=pltpu.SMEM)] + in_specs
        args = [sink] + args
    out = pl.BlockSpec((S, LANES), lambda hp: (0, hp))
    return pl.pallas_call(
        body,
        grid=(4,),
        in_specs=in_specs,
        out_specs=[out, out],
        out_shape=[jax.ShapeDtypeStruct((S, 4 * LANES), f32)] * 2,
        scratch_shapes=[pltpu.VMEM((2, 2 * TQ, w), f32), pltpu.VMEM((2, 2 * TQ, w), f32),
                        pltpu.VMEM((2, 2 * TQ, w), bf16), pltpu.VMEM((2, 2 * TQ, w), bf16),
                        pltpu.VMEM((2, 2 * TQ, LANES), f32)],
        compiler_params=_params(("arbitrary",)),
        name=name,
    )(*args)


def _attn_bwd(gl, bias, bucket, do, lse, dd, g, blk, d, name):
    w = TQ + 2 * blk
    seq = S // d

    def body(q_ref, k_ref, v_ref, b_ref, bk_ref, do_ref, l_ref, d_ref, dqkv_ref, dbk_ref,
             db_acc, s0, s1, dp0, dp1, pb0, pb1, ds0, ds1, dk_acc, dv_acc):
        lo = _lo()
        hi = jnp.logical_not(lo)
        dk_acc[...] = jnp.zeros((SP, LANES), f32)
        dv_acc[...] = jnp.zeros((SP, LANES), f32)
        db_acc[...] = jnp.zeros((2 * TQ, w), f32)
        s_bufs, dp_bufs, pb_bufs, ds_bufs = (s0, s1), (dp0, dp1), (pb0, pb1), (ds0, ds1)

        def stacked(t):
            f0 = pl.multiple_of(t * TQ, TQ)
            q2 = _stack_heads(q_ref[pl.ds(PAD + f0, TQ), :], lo)
            do2 = _stack_heads(do_ref[_token_rows(t, 0, TQ, d), :].astype(bf16), lo)
            return f0, q2, do2

        def scores(p, slot):
            for u in range(2):
                f0, q2, do2 = stacked(2 * p + u)
                win = pl.ds(PAD - blk + f0, w)
                s_bufs[slot][u] = lax.dot_general(q2, k_ref[win, :], NT_DIMS, preferred_element_type=f32)
                dp_bufs[slot][u] = lax.dot_general(do2, v_ref[win, :], NT_DIMS, preferred_element_type=f32)

        def grads(p, slot):
            for u in range(2):
                t = 2 * p + u
                kind = _tile_kind(t, seq)
                for h in range(2):
                    msk = lo if h == 0 else hi
                    for r in range(TQ // RC):
                        rows = slice(h * TQ + r * RC, h * TQ + (r + 1) * RC)
                        src = _token_rows(t, r * RC, RC, d)
                        lh = jnp.max(jnp.where(msk, l_ref[src, :], -jnp.inf), axis=1, keepdims=True)
                        dh = jnp.max(jnp.where(msk, d_ref[src, :], -jnp.inf), axis=1, keepdims=True)
                        logit = s_bufs[slot][u, rows, :] + b_ref[kind, h, r * RC:(r + 1) * RC, :]
                        pr = jnp.exp(logit - lh)
                        ds = pr * (dp_bufs[slot][u, rows, :] - dh)
                        db_acc[rows, :] += ds
                        pb_bufs[slot][u, rows, :] = pr.astype(bf16)
                        ds_bufs[slot][u, rows, :] = ds.astype(bf16)

        def accumulate(p, slot):
            for u in range(2):
                f0, q2, do2 = stacked(2 * p + u)
                win = pl.ds(PAD - blk + f0, w)
                dsb = ds_bufs[slot][u]
                dq2 = jnp.dot(dsb, k_ref[win, :], preferred_element_type=f32)
                dqkv_ref[0, pl.ds(PAD + f0, TQ), :] = _unstack_heads(dq2, lo).astype(bf16)
                dk_acc[win, :] += lax.dot_general(dsb, q2, TN_DIMS, preferred_element_type=f32)
                dv_acc[win, :] += lax.dot_general(pb_bufs[slot][u], do2, TN_DIMS, preferred_element_type=f32)

        npair = S // TQ // 2
        scores(0, 0)
        scores(1, 1)
        grads(0, 0)

        def steady(k, carry):
            p = 2 * k + 2
            scores(p, 0)
            grads(p - 1, 1)
            accumulate(p - 2, 0)
            scores(p + 1, 1)
            grads(p, 0)
            accumulate(p - 1, 1)
            return carry

        lax.fori_loop(0, (npair - 2) // 2, steady, 0)
        grads(npair - 1, 1)
        accumulate(npair - 2, 0)
        accumulate(npair - 1, 1)
        for i in range(SP // CHUNK):
            rows = slice(i * CHUNK, (i + 1) * CHUNK)
            dqkv_ref[1, rows, :] = dk_acc[rows, :].astype(bf16)
            dqkv_ref[2, rows, :] = dv_acc[rows, :].astype(bf16)

        bk = bk_ref[...]
        lane = lax.broadcasted_iota(jnp.int32, (8, LANES), 1)
        for h in range(2):
            db = db_acc[h * TQ:(h + 1) * TQ, :]
            acc = jnp.zeros((8, LANES), f32)
            for b in range(32):
                part = jnp.where(bk == b, db, 0.0).reshape(TQ // 8, 8, w).sum(axis=0)
                tot = jnp.sum(jnp.sum(part, axis=1, keepdims=True), axis=0, keepdims=True)
                acc = jnp.where(lane == b, tot, acc)
            dbk_ref[h] = acc

    def gcol(off):
        return pl.BlockSpec((None, None, SP, LANES), lambda hp: (g, off + hp, 0, 0))

    row = pl.BlockSpec((S, LANES), lambda hp: (0, hp))
    return pl.pallas_call(
        body,
        grid=(4,),
        in_specs=[gcol(0), gcol(4), gcol(8), pl.BlockSpec((3, 2, TQ, w), lambda hp: (0, hp, 0, 0)),
                  pl.BlockSpec((TQ, w), lambda hp: (0, 0)), row, row, row],
        out_specs=[pl.BlockSpec((3, None, SP, LANES), lambda hp: (0, hp, 0, 0)),
                   pl.BlockSpec((2, 8, LANES), lambda hp: (hp, 0, 0))],
        out_shape=[
            jax.ShapeDtypeStruct((3, 4, SP, LANES), bf16),
            jax.ShapeDtypeStruct((8, 8, LANES), f32),
        ],
        scratch_shapes=([pltpu.VMEM((2 * TQ, w), f32)] + [pltpu.VMEM((2, 2 * TQ, w), f32)] * 4
                        + [pltpu.VMEM((2, 2 * TQ, w), bf16)] * 4 + [pltpu.VMEM((SP, LANES), f32)] * 2),
        compiler_params=_params(("arbitrary",), vmem_mib=56),
        name=name,
    )(gl, gl, gl, bias, bucket, do, lse, dd)


def _sigmoid(z):
    return 1.0 / (1.0 + jnp.exp(-z))


def _tail(x, tgt, o_a, l_a, o_b, l_b, proj, bm, w_a, w_b, w_o, sink_b):
    ts = 256

    def body(x_ref, t_ref, oa_ref, la_ref, ob0_ref, ob1_ref, ob2_ref, lb0_ref, lb1_ref, lb2_ref,
             ga_ref, gb_ref, m0_ref, m1_ref, bm_ref, wa_ref, wb_ref, wo_ref, sk_ref,
             dy_ref, dyb_ref, dt_ref, doa_ref, dda_ref, dob0_ref, dob1_ref, dob2_ref, ddb0_ref, ddb1_ref, ddb2_ref,
             ya_ref, yb_ref, mg_ref, dbra_ref, dbrb_ref, loss_ref, dbm_ref, dsk_ref):
        i = pl.program_id(0)

        @pl.when(i == 0)
        def _():
            loss_ref[...] = jnp.zeros_like(loss_ref)
            dbm_ref[...] = jnp.zeros_like(dbm_ref)
            dsk_ref[...] = jnp.zeros_like(dsk_ref)

        ga = ga_ref[...]
        sa = _sigmoid(ga)
        silu_a = ga * sa
        oa = oa_ref[...]
        ya = oa * silu_a
        gb = gb_ref[...]
        sb = _sigmoid(gb)
        silu_b = gb * sb
        ob = [ob0_ref[...], ob1_ref[...], ob2_ref[...]]
        lb = [lb0_ref[...], lb1_ref[...], lb2_ref[...]]
        mx = jnp.maximum(jnp.maximum(lb[0], lb[1]), lb[2])
        ex = [jnp.exp(v - mx) for v in lb]
        den = ex[0] + ex[1] + ex[2]
        alpha = [e / den for e in ex]
        ybc = alpha[0] * ob[0] + alpha[1] * ob[1] + alpha[2] * ob[2]
        yb = ybc * silu_b
        yab = ya.astype(bf16)
        ybb = yb.astype(bf16)
        br_a = jnp.dot(yab, wa_ref[...], preferred_element_type=f32)
        br_b = jnp.dot(ybb, wb_ref[...], preferred_element_type=f32)
        g0 = _sigmoid(m0_ref[...] + bm_ref[0:1, :])
        g1 = _sigmoid(m1_ref[...] + bm_ref[1:2, :])
        merged = g0 * br_a + g1 * br_b
        mgb = merged.astype(bf16)
        y = x_ref[...] + jnp.dot(mgb, wo_ref[...], preferred_element_type=f32)
        err = y - t_ref[...]
        part = jnp.sum(jnp.sum(err * err, axis=1, keepdims=True), axis=0, keepdims=True)
        loss_ref[...] += part * (0.5 / D)
        dy = err * (1.0 / D)
        dyb = dy.astype(bf16)
        dmerged = lax.dot_general(dyb, wo_ref[...], NT_DIMS, preferred_element_type=f32)
        dbr_a = (dmerged * g0).astype(bf16)
        dbr_b = (dmerged * g1).astype(bf16)
        dm0 = dmerged * br_a * (g0 * (1.0 - g0))
        dm1 = dmerged * br_b * (g1 * (1.0 - g1))
        dbm_ref[0:1, :] += jnp.sum(dm0, axis=0, keepdims=True)
        dbm_ref[1:2, :] += jnp.sum(dm1, axis=0, keepdims=True)
        dya = lax.dot_general(dbr_a, wa_ref[...], NT_DIMS, preferred_element_type=f32)
        dyb2 = lax.dot_general(dbr_b, wb_ref[...], NT_DIMS, preferred_element_type=f32)
        do_a = dya * silu_a
        dga = dya * oa * (sa * (1.0 + ga * (1.0 - sa)))
        ones = _head_ones()
        delta_a = _seg_sum(do_a * oa, ones)
        dsk_ref[...] -= jnp.sum(delta_a * jnp.exp(sk_ref[...] - la_ref[...]), axis=0, keepdims=True)
        dybc = dyb2 * silu_b
        dgb = dyb2 * ybc * (sb * (1.0 + gb * (1.0 - sb)))
        dbar = _seg_sum(dybc * ybc, ones)
        dy_ref[...] = dy
        dyb_ref[...] = dyb
        dt_ref[:, 0:512] = dga.astype(bf16)
        dt_ref[:, 512:1024] = dgb.astype(bf16)
        dt_ref[:, 1024:2048] = dm0.astype(bf16)
        dt_ref[:, 2048:3072] = dm1.astype(bf16)
        doa_ref[...] = do_a.astype(bf16)
        dda_ref[...] = delta_a
        for k, (dob_ref, ddb_ref) in enumerate(((dob0_ref, ddb0_ref), (dob1_ref, ddb1_ref), (dob2_ref, ddb2_ref))):
            dob_ref[...] = alpha[k] * dybc
            ddb_ref[...] = alpha[k] * dbar
        ya_ref[...] = ya.T.astype(bf16)
        yb_ref[...] = yb.T.astype(bf16)
        mg_ref[...] = merged.T.astype(bf16)
        dbra_ref[...] = dbr_a
        dbrb_ref[...] = dbr_b

    def rows(n, blk=0):
        return pl.BlockSpec((ts, n), lambda i: (i, blk))

    def whole(r, c):
        return pl.BlockSpec((r, c), lambda i: (0, 0))

    def cols(n):
        return pl.BlockSpec((n, ts), lambda i: (0, i))

    def gate_cols(n, col):
        return pl.BlockSpec((pl.Element(ts), pl.Element(n)), lambda i: (i * ts, NA + col))

    outs = [
        ((S, D), f32, rows(D)), ((S, D), bf16, rows(D)), ((S, NW), bf16, gate_cols(NT, 0)),
        ((S, 512), bf16, rows(512)), ((S, 512), f32, rows(512)),
        ((S, 512), f32, rows(512)), ((S, 512), f32, rows(512)), ((S, 512), f32, rows(512)),
        ((S, 512), f32, rows(512)), ((S, 512), f32, rows(512)), ((S, 512), f32, rows(512)),
        ((512, S), bf16, cols(512)), ((512, S), bf16, cols(512)), ((D, S), bf16, cols(D)),
        ((S, D), bf16, rows(D)), ((S, D), bf16, rows(D)),
        ((1, 1), f32, whole(1, 1)), ((2, D), f32, whole(2, D)), ((1, 512), f32, whole(1, 512)),
    ]
    return pl.pallas_call(
        body,
        grid=(S // ts,),
        in_specs=[
            rows(D), rows(D), rows(512), rows(512), rows(512), rows(512), rows(512), rows(512), rows(512), rows(512),
            gate_cols(512, 0), gate_cols(512, 512), gate_cols(D, 1024), gate_cols(D, 2048), whole(2, D),
            whole(512, D), whole(512, D), whole(D, D), whole(1, 512),
        ],
        out_specs=[o[2] for o in outs],
        out_shape=[jax.ShapeDtypeStruct(o[0], o[1]) for o in outs],
        compiler_params=_params(("arbitrary",), vmem_mib=60),
        name="tail",
    )(x, tgt, o_a, l_a, *o_b, *l_b, proj, proj, proj, proj, bm, w_a, w_b, w_o, sink_b)


def _norm_bwd(xv, dyv, gain, ones):
    r = lax.rsqrt(_half_sums(xv * xv, ones) * (1.0 / HD) + EPS)
    yv = xv * r
    u = dyv * gain
    dxv = r * (u - yv * (_half_sums(u * yv, ones) * (1.0 / HD)))
    return dxv, jnp.sum(dyv * yv, axis=0, keepdims=True)


def _post_b(g, dqkv, proj_a, gains, dproj):
    d = GROUPS[g][1]
    seq = S // d

    def body(d_ref, pa_ref, pb_ref, g_ref, alias_ref, o_ref, dg_ref, nat_a, nat_b):
        del alias_ref
        pj = pl.program_id(0)
        kind = pj // 2
        q_scale = jnp.where(kind == 0, SCALE, 1.0)
        gain = g_ref[...] * q_scale
        ones = _head_ones()

        @pl.when(pj % 2 == 0)
        def _():
            dg_ref[...] = jnp.zeros_like(dg_ref)

        def columns(with_norm):
            per = seq // PCHUNK
            for u, (p_ref, nat) in enumerate(((pa_ref, nat_a), (pb_ref, nat_b))):
                def chunk(t, carry, u=u, p_ref=p_ref, nat=nat):
                    src = pl.multiple_of(t * PCHUNK, PCHUNK)
                    if d == 1:
                        idx = pl.ds(src, PCHUNK)
                    else:
                        idx = pl.ds((t % per) * (PCHUNK * d) + t // per, PCHUNK, stride=d)
                    dyv = d_ref[u, pl.ds(PAD + src, PCHUNK), :].astype(f32)
                    if with_norm:
                        dyv, dg = _norm_bwd(p_ref[idx, :], dyv, gain, ones)
                        dg_ref[...] += dg * q_scale
                    nat[idx, :] = dyv
                    return carry

                lax.fori_loop(0, S // PCHUNK, chunk, 0, unroll=8)

                def cast(i, carry, u=u, nat=nat):
                    rows = pl.ds(pl.multiple_of(i * CHUNK, CHUNK), CHUNK)
                    o_ref[rows, u * LANES:(u + 1) * LANES] = nat[rows, :].astype(bf16)
                    return carry

                lax.fori_loop(0, S // CHUNK, cast, 0)

        pl.when(kind < 2)(lambda: columns(True))
        pl.when(kind == 2)(lambda: columns(False))

    def pcol(u):
        return pl.BlockSpec((S, LANES), lambda pj: (0, _col_block(g, 2 * jnp.minimum(pj, 3) + u)))

    return pl.pallas_call(
        body,
        grid=(6,),
        in_specs=[
            pl.BlockSpec((None, 2, SP, LANES), lambda pj: (pj // 2, pj % 2, 0, 0)),
            pcol(0), pcol(1),
            pl.BlockSpec((None, None, 1, LANES), lambda pj: (g, pj // 2, 0, 0)),
            pl.BlockSpec(memory_space=pl.ANY),
        ],
        out_specs=[
            pl.BlockSpec((S, 2 * LANES), lambda pj: (0, _col_block(g, 2 * pj) // 2)),
            pl.BlockSpec((None, 1, LANES), lambda pj: (pj // 2, 0, 0)),
        ],
        out_shape=[jax.ShapeDtypeStruct((S, NW), bf16), jax.ShapeDtypeStruct((3, 1, LANES), f32)],
        scratch_shapes=[pltpu.VMEM((S, LANES), f32), pltpu.VMEM((S, LANES), f32)],
        input_output_aliases={4: 0},
        compiler_params=_params(("arbitrary",)),
        name="post_b%d" % g,
    )(dqkv, proj_a, proj_a, gains, dproj)


def _post_a(dqkv, proj_a, gains, dproj):
    def body(q_ref, e_ref, p_ref, g_ref, alias_ref, o_ref, dg_ref):
        del alias_ref
        j = pl.program_id(0)
        q_scale = jnp.where(j < 4, SCALE, 1.0)
        gain = g_ref[...] * q_scale
        lo = _lo()
        ones = _head_ones()

        @pl.when((j == 0) | (j >= 4))
        def _():
            dg_ref[...] = jnp.zeros_like(dg_ref)

        def column(folded, with_norm):
            for i in range(S // PCHUNK):
                r0 = i * PCHUNK
                rows = slice(PAD + r0, PAD + r0 + PCHUNK)
                if folded:
                    t0 = e_ref[0, rows, :].astype(f32) + e_ref[1, rows, :].astype(f32)
                    t1 = e_ref[2, rows, :].astype(f32) + e_ref[3, rows, :].astype(f32)
                    dyv = jnp.where(lo, t0 + pltpu.roll(t0, HD, 1), t1 + pltpu.roll(t1, HD, 1))
                else:
                    dyv = q_ref[rows, :].astype(f32)
                if with_norm:
                    dyv, dg = _norm_bwd(p_ref[r0:r0 + PCHUNK, :], dyv, gain, ones)
                    dg_ref[...] += dg * q_scale
                o_ref[r0:r0 + PCHUNK, :] = dyv.astype(bf16)

        pl.when(j < 4)(lambda: column(False, True))
        pl.when(j == 4)(lambda: column(True, True))
        pl.when(j == 5)(lambda: column(True, False))

    return pl.pallas_call(
        body,
        grid=(6,),
        in_specs=[
            pl.BlockSpec((None, None, SP, LANES), lambda j: (0, jnp.minimum(j, 3), 0, 0)),
            pl.BlockSpec((None, 4, SP, LANES), lambda j: (jnp.clip(j - 3, 1, 2), 0, 0, 0)),
            pl.BlockSpec((S, LANES), lambda j: (0, jnp.minimum(j, 4))),
            pl.BlockSpec((None, None, 1, LANES), lambda j: (0, jnp.maximum(j - 3, 0), 0, 0)),
            pl.BlockSpec(memory_space=pl.ANY),
        ],
        out_specs=[
            pl.BlockSpec((S, LANES), lambda j: (0, j)),
            pl.BlockSpec((None, 1, LANES), lambda j: (jnp.maximum(j - 3, 0), 0, 0)),
        ],
        out_shape=[jax.ShapeDtypeStruct((S, NW), bf16), jax.ShapeDtypeStruct((3, 1, LANES), f32)],
        input_output_aliases={4: 0},
        compiler_params=_params(("arbitrary",)),
        name="post_a",
    )(dqkv, dqkv, proj_a, gains, dproj)


def _dh_norm_bwd(dproj, w, x, rstd, gain, dy):
    ts = 1024
    tk = NW // 6
    nk = NW // tk

    def body(d_ref, w_ref, x_ref, r_ref, g_ref, dy_ref, gx_ref, dgn_ref, acc):
        i = pl.program_id(0)
        k = pl.program_id(1)

        @pl.when((i == 0) & (k == 0))
        def _():
            dgn_ref[...] = jnp.zeros_like(dgn_ref)

        @pl.when(k == 0)
        def _():
            acc[...] = jnp.zeros_like(acc)

        acc[...] += jnp.dot(d_ref[...], w_ref[...], preferred_element_type=f32)

        @pl.when(k == nk - 1)
        def _():
            dh = acc[...]
            xh = x_ref[...] * r_ref[...]
            u = dh * g_ref[...]
            dx = r_ref[...] * (u - xh * jnp.mean(u * xh, axis=-1, keepdims=True))
            gx_ref[...] = dy_ref[...] + dx
            dgn_ref[...] += jnp.sum(dh * xh, axis=0, keepdims=True)

    return pl.pallas_call(
        body,
        grid=(S // ts, nk),
        in_specs=[
            pl.BlockSpec((ts, tk), lambda i, k: (i, k)),
            pl.BlockSpec((tk, D), lambda i, k: (k, 0)),
            pl.BlockSpec((ts, D), lambda i, k: (i, 0)),
            pl.BlockSpec((ts, 1), lambda i, k: (i, 0)),
            pl.BlockSpec((1, D), lambda i, k: (0, 0)),
            pl.BlockSpec((ts, D), lambda i, k: (i, 0)),
        ],
        out_specs=[pl.BlockSpec((ts, D), lambda i, k: (i, 0)), pl.BlockSpec((1, D), lambda i, k: (0, 0))],
        out_shape=[jax.ShapeDtypeStruct((S, D), f32), jax.ShapeDtypeStruct((1, D), f32)],
        scratch_shapes=[pltpu.VMEM((ts, D), f32)],
        compiler_params=_params(("arbitrary", "arbitrary"), vmem_mib=56),
        name="dh_norm_bwd",
    )(dproj, w, x, rstd, gain, dy)


def _dw_in(hbt, dproj, parity, name):
    tk = 1024
    win = WSH + 96

    def body(par_ref, a_ref, b_ref, o_ref, acc):
        p = 2 * pl.program_id(0) + par_ref[0]
        k = pl.program_id(1)

        @pl.when(k == 0)
        def _():
            acc[...] = jnp.zeros_like(acc)

        acc[...] += jnp.dot(a_ref[...], b_ref[...], preferred_element_type=f32)

        @pl.when(k == S // tk - 1)
        def _():
            acc_t = acc[...].T
            for pp in range(NDEV):
                off = (WSH * pp) % LANES

                @pl.when(p == pp)
                def _():
                    o_ref[...] = acc_t[off:off + WSH, :].astype(bf16)

    return pl.pallas_call(
        body,
        grid_spec=pltpu.PrefetchScalarGridSpec(
            num_scalar_prefetch=1,
            grid=(NDEV // 2, S // tk),
            in_specs=[
                pl.BlockSpec((D, tk), lambda q, k, par: (0, k)),
                pl.BlockSpec((pl.Element(tk), pl.Element(win)),
                             lambda q, k, par: (k * tk, (WSH * (2 * q + par[0])) // LANES * LANES)),
            ],
            out_specs=pl.BlockSpec((None, WSH, D), lambda q, k, par: (q, 0, 0)),
            scratch_shapes=[pltpu.VMEM((D, win), f32)],
        ),
        out_shape=jax.ShapeDtypeStruct((NDEV // 2, WSH, D), bf16),
        compiler_params=_params(("arbitrary", "arbitrary")),
        name=name,
    )(parity, hbt, dproj)


def _matmul_tokens(at, b, name):
    m, n = at.shape[0], b.shape[1]
    tn = 512
    tk = 1024

    def body(a_ref, b_ref, o_ref):
        @pl.when(pl.program_id(1) == 0)
        def _():
            o_ref[...] = jnp.zeros_like(o_ref)

        o_ref[...] += jnp.dot(a_ref[...], b_ref[...], preferred_element_type=f32)

    return pl.pallas_call(
        body,
        grid=(n // tn, S // tk),
        in_specs=[pl.BlockSpec((m, tk), lambda j, k: (0, k)), pl.BlockSpec((tk, tn), lambda j, k: (k, j))],
        out_specs=pl.BlockSpec((m, tn), lambda j, k: (0, j)),
        out_shape=jax.ShapeDtypeStruct((m, n), f32),
        compiler_params=_params(("arbitrary", "arbitrary")),
        name=name,
    )(at, b)


def _exchange(scatter, gather, name):
    arrs = list(scatter) + list(gather)
    n = len(arrs)
    ns = len(scatter)

    def body(*refs):
        ins, outs = refs[:n], refs[n:2 * n]
        send_sems, recv_sems, local_sems = refs[2 * n:]
        x, y, c = lax.axis_index("x"), lax.axis_index("y"), lax.axis_index("c")
        me = 4 * x + 2 * y + c
        local, remote = [], []
        for a in range(n):
            lc = pltpu.make_async_copy(ins[a].at[me] if a < ns else ins[a], outs[a].at[me], local_sems.at[a])
            lc.start()
            local.append(lc)
            for r in range(1, NDEV):
                px = 1 - x if r & 4 else x
                py = 1 - y if r & 2 else y
                pc = 1 - c if r & 1 else c
                cp = pltpu.make_async_remote_copy(
                    src_ref=ins[a].at[4 * px + 2 * py + pc] if a < ns else ins[a],
                    dst_ref=outs[a].at[me],
                    send_sem=send_sems.at[a, r - 1],
                    recv_sem=recv_sems.at[a, r - 1],
                    device_id=(px, py, pc),
                    device_id_type=pl.DeviceIdType.MESH,
                )
                cp.start()
                remote.append(cp)
        for cp in remote:
            cp.wait_recv()
        for cp in remote:
            cp.wait_send()
        for lc in local:
            lc.wait()

    out_shape = [jax.ShapeDtypeStruct(a.shape if i < ns else (NDEV,) + a.shape, a.dtype) for i, a in enumerate(arrs)]
    return pl.pallas_call(
        body,
        in_specs=[pl.BlockSpec(memory_space=pl.ANY)] * n,
        out_specs=[pl.BlockSpec(memory_space=pl.ANY)] * n,
        out_shape=out_shape,
        scratch_shapes=[
            pltpu.SemaphoreType.DMA((n, NDEV - 1)),
            pltpu.SemaphoreType.DMA((n, NDEV - 1)),
            pltpu.SemaphoreType.DMA((n,)),
        ],
        compiler_params=pltpu.CompilerParams(has_side_effects=True),
        name=name,
    )(*arrs)


_HBM = pl.BlockSpec(memory_space=pltpu.HBM)
_SEM = pl.BlockSpec(memory_space=pltpu.SEMAPHORE)
_EFFECT = pltpu.SideEffectType.DATAFLOW_SIDE_EFFECTING


def _comm_step(name, body_fn, lands, srcs=(), wait_sems=(), n_new=0, after=(), token=False):
    n, ns, nw, na = len(lands), len(srcs), len(wait_sems), len(after)

    def body(*refs):
        src, land = refs[:ns], refs[ns:ns + n]
        waits = refs[ns + n:ns + n + nw]
        new = refs[ns + n + nw + na:ns + n + nw + na + n_new]
        body_fn(src, land, waits, new)
        if token:
            refs[-1][...] = jnp.zeros((8, LANES), f32)

    hbm = [pltpu.HBM(a.shape, a.dtype) for a in lands]
    ops = [pltpu.with_memory_space_constraint(a, pltpu.HBM) for a in list(srcs) + list(lands)]
    extra_shape = [jax.ShapeDtypeStruct((8, LANES), f32)] if token else []
    extra_spec = [pl.BlockSpec(memory_space=pltpu.VMEM)] if token else []
    outs = pl.pallas_call(
        body,
        out_shape=tuple([pltpu.SemaphoreType.DMA(())] * n_new + hbm + extra_shape),
        in_specs=[_HBM] * (ns + n) + [_SEM] * nw + [pl.BlockSpec(memory_space=pl.ANY)] * na,
        out_specs=tuple([_SEM] * n_new + [_HBM] * n + extra_spec),
        input_output_aliases={ns + i: n_new + i for i in range(n)},
        compiler_params=pltpu.CompilerParams(has_side_effects=_EFFECT),
        name=name,
    )(*ops, *wait_sems, *after)
    if token:
        return list(outs[:n_new]), list(outs[n_new:n_new + n]), outs[-1][0, 0]
    return list(outs[:n_new]), list(outs[n_new:])


class _GatheredWeights:
    def __init__(self, shards):
        self.n = n = len(shards)
        x, y, c = lax.axis_index("x"), lax.axis_index("y"), lax.axis_index("c")
        self.x = x
        me = 4 * x + 2 * y + c
        lands = [lax.dynamic_update_slice(lax.empty((NDEV,) + s.shape, s.dtype), s[None], (me,) + (0,) * s.ndim)
                 for s in shards]

        def start_own(src, land, waits, new):
            p = self._peers()
            for a in range(n):
                for k, to in ((0, p["sibling"]), (1, p["xn"]), (2, p["yn"])):
                    self._copy(land[a], new, a, k, 3, p["me"], to).start()

        self.sems, self.lands = {}, None
        new, self.lands = _comm_step("gather_start", start_own, lands, n_new=6 * n)
        self._keep(new, (0, 1, 2))

    @staticmethod
    def _peers():
        x, y, c = lax.axis_index("x"), lax.axis_index("y"), lax.axis_index("c")
        return dict(
            me=(x, y, c), sibling=(x, y, 1 - c), xn=(1 - x, y, c), yn=(x, 1 - y, c), dg=(1 - x, 1 - y, c),
            relay_origin=(jnp.bitwise_xor(x, c), jnp.bitwise_xor(y, 1 - c), c),
            relay_target=(jnp.bitwise_xor(x, 1 - c), jnp.bitwise_xor(y, c), c))

    def _keep(self, new, ks):
        half = len(new) // 2
        i = 0
        for a in range(self.n):
            for k in ks:
                self.sems[a, k] = (new[i], new[half + i])
                i += 1

    @staticmethod
    def _copy(land, sem_refs, a, k, nk, block, to, src=None, ks=None):
        ks = tuple(range(nk)) if ks is None else ks
        half = len(sem_refs) // 2
        i = a * len(ks) + ks.index(k)
        slot = land.at[4 * block[0] + 2 * block[1] + block[2]]
        return pltpu.make_async_remote_copy(
            src_ref=slot if src is None else src, dst_ref=slot, send_sem=sem_refs[i], recv_sem=sem_refs[half + i],
            device_id=to, device_id_type=pl.DeviceIdType.MESH)

    def _sem_list(self, ks):
        return ([self.sems[a, k][0] for a in range(self.n) for k in ks]
                + [self.sems[a, k][1] for a in range(self.n) for k in ks])

    def first_half(self, after):
        n = self.n

        def relay(src, land, waits, new):
            p = self._peers()
            for a in range(n):
                self._copy(land[a], waits, a, 1, 0, p["xn"], p["me"], ks=(1, 2)).wait_recv()
                self._copy(land[a], waits, a, 2, 0, p["yn"], p["me"], ks=(1, 2)).wait_recv()
                self._copy(land[a], new, a, 3, 0, p["relay_origin"], p["relay_target"], ks=(3, 4, 5)).start()
                self._copy(land[a], new, a, 4, 0, p["xn"], p["sibling"], ks=(3, 4, 5)).start()
                self._copy(land[a], new, a, 5, 0, p["yn"], p["sibling"], ks=(3, 4, 5)).start()

        new, self.lands = _comm_step("gather_relay", relay, self.lands, wait_sems=self._sem_list((1, 2)),
                                     n_new=6 * n, after=after)
        self._keep(new, (3, 4, 5))

        def from_sibling(src, land, waits, new):
            p = self._peers()
            other = lambda b: (b[0], b[1], 1 - b[2])
            for a in range(n):
                self._copy(land[a], waits, a, 0, 0, other(p["me"]), p["me"], ks=(0, 4, 5)).wait_recv()
                self._copy(land[a], waits, a, 4, 0, other(p["xn"]), p["me"], ks=(0, 4, 5)).wait_recv()
                self._copy(land[a], waits, a, 5, 0, other(p["yn"]), p["me"], ks=(0, 4, 5)).wait_recv()

        _, self.lands = _comm_step("gather_wait_sibling", from_sibling, self.lands,
                                   wait_sems=self._sem_list((0, 4, 5)))
        return self.lands[0].reshape(NW, D), self.x.astype(jnp.int32).reshape(1)

    def second_half(self, after):
        n = self.n

        def forward_diagonal(src, land, waits, new):
            p = self._peers()
            for a in range(n):
                self._copy(land[a], waits, a, 3, 0, p["dg"], p["me"], ks=(3,)).wait_recv()
                self._copy(land[a], new, a, 6, 0, p["dg"], p["sibling"], ks=(6,)).start()

        new, self.lands = _comm_step("gather_forward_diagonal", forward_diagonal, self.lands,
                                     wait_sems=self._sem_list((3,)), n_new=2 * n, after=after)
        self._keep(new, (6,))

        def finish(src, land, waits, new):
            p = self._peers()
            ks = tuple(range(7))
            for a in range(n):
                self._copy(land[a], waits, a, 6, 0, (p["dg"][0], p["dg"][1], 1 - p["dg"][2]), p["me"], ks=ks).wait_recv()
                for k in ks:
                    self._copy(land[a], waits, a, k, 0, p["me"], p["me"], ks=ks).wait_send()

        _, self.lands = _comm_step("gather_finish", finish, self.lands, wait_sems=self._sem_list(tuple(range(7))))
        return self.lands[0].reshape(NW, D), (1 - self.x).astype(jnp.int32).reshape(1)

    def rest(self):
        g_a, g_b, g_o, g_bm = self.lands[1:]
        return (g_a.transpose(1, 0, 2).reshape(512, D), g_b.transpose(1, 0, 2).reshape(512, D),
                g_bm.transpose(1, 0, 2).reshape(2, D), g_o.reshape(D, D))


def _sibling_send_start(shares):
    landing = lax.empty(shares.shape, shares.dtype)

    def start(src, land, waits, new):
        x, y, c = lax.axis_index("x"), lax.axis_index("y"), lax.axis_index("c")
        pltpu.make_async_remote_copy(src_ref=land[0], dst_ref=land[1], send_sem=new[0], recv_sem=new[1],
                                     device_id=(x, y, 1 - c), device_id_type=pl.DeviceIdType.MESH).start()

    return _comm_step("grad_sibling_start", start, [shares, landing], n_new=2, token=True)


def _sibling_send_wait(sems, lands, after):
    def wait(src, land, waits, new):
        x, y, c = lax.axis_index("x"), lax.axis_index("y"), lax.axis_index("c")
        done = pltpu.make_async_remote_copy(src_ref=land[0], dst_ref=land[1], send_sem=waits[0], recv_sem=waits[1],
                                            device_id=(x, y, c), device_id_type=pl.DeviceIdType.MESH)
        done.wait_send()
        done.wait_recv()

    _, lands = _comm_step("grad_sibling_wait", wait, lands, wait_sems=sems, after=after)
    return lands[1]


def _row_tile(rows, limit=256):
    fits = [t for t in range(16, limit + 1, 16) if rows % t == 0]
    return fits[-1] if fits else rows


def _pair_sum(mine, theirs, name):
    nb, rows, cols = mine.shape
    tr = _row_tile(rows, 528)

    def body(a_ref, b_ref, o_ref):
        o_ref[...] = (a_ref[...].astype(f32) + b_ref[...].astype(f32)).astype(bf16)

    blk = pl.BlockSpec((None, tr, cols), lambda q, i: (q, i, 0))
    return pl.pallas_call(
        body,
        grid=(nb, rows // tr),
        in_specs=[blk, blk],
        out_specs=blk,
        out_shape=jax.ShapeDtypeStruct(mine.shape, bf16),
        compiler_params=_params(("arbitrary", "arbitrary")),
        name=name,
    )(mine, theirs)


def _scatter_start(chip_arrs, all_arrs, name):
    arrs = list(chip_arrs) + list(all_arrs)
    n, nc = len(arrs), len(chip_arrs)
    lands = [lax.empty(((3 if i < nc else NDEV - 1),) + a.shape[1:], a.dtype) for i, a in enumerate(arrs)]

    def body(*refs):
        src, land = refs[:n], refs[n:2 * n]
        send_sems, recv_sems = refs[2 * n:3 * n], refs[3 * n:4 * n]
        token = refs[6 * n]
        x, y, c = lax.axis_index("x"), lax.axis_index("y"), lax.axis_index("c")
        for a in range(n):
            for r in range(1, 4 if a < nc else NDEV):
                if a < nc:
                    px, py, pc = (1 - x if r & 2 else x), (1 - y if r & 1 else y), c
                    block = 2 * px + py
                else:
                    px, py, pc = (1 - x if r & 4 else x), (1 - y if r & 2 else y), (1 - c if r & 1 else c)
                    block = 4 * px + 2 * py + pc
                pltpu.make_async_remote_copy(
                    src_ref=src[a].at[block], dst_ref=land[a].at[r - 1], send_sem=send_sems[a],
                    recv_sem=recv_sems[a], device_id=(px, py, pc), device_id_type=pl.DeviceIdType.MESH).start()
        token[...] = jnp.zeros_like(token)

    hbm = [pltpu.HBM(a.shape, a.dtype) for a in arrs + lands]
    ops = [pltpu.with_memory_space_constraint(a, pltpu.HBM) for a in arrs + lands]
    outs = pl.pallas_call(
        body,
        out_shape=tuple([pltpu.SemaphoreType.DMA(())] * (2 * n) + hbm + [jax.ShapeDtypeStruct((8, LANES), f32)]),
        in_specs=[_HBM] * (2 * n),
        out_specs=tuple([_SEM] * (2 * n) + [_HBM] * (2 * n) + [pl.BlockSpec(memory_space=pltpu.VMEM)]),
        input_output_aliases={i: 2 * n + i for i in range(2 * n)},
        compiler_params=pltpu.CompilerParams(has_side_effects=_EFFECT),
        name=name,
    )(*ops)
    return outs[:n], outs[n:2 * n], outs[2 * n:3 * n], outs[3 * n:4 * n], outs[4 * n]


def _scatter_wait(send_sems, recv_sems, srcs, lands, after, name):
    n = len(srcs)

    def body(*refs):
        land = refs[n:2 * n]
        ssem, rsem = refs[2 * n:3 * n], refs[3 * n:4 * n]
        x, y, c = lax.axis_index("x"), lax.axis_index("y"), lax.axis_index("c")
        for a in range(n):
            done = pltpu.make_async_remote_copy(
                src_ref=land[a], dst_ref=land[a], send_sem=ssem[a], recv_sem=rsem[a], device_id=(x, y, c),
                device_id_type=pl.DeviceIdType.MESH)
            done.wait_send()
            done.wait_recv()

    hbm = [pltpu.HBM(a.shape, a.dtype) for a in list(srcs) + list(lands)]
    outs = pl.pallas_call(
        body,
        out_shape=tuple(hbm),
        in_specs=[_HBM] * (2 * n) + [_SEM] * (2 * n) + [pl.BlockSpec(memory_space=pl.ANY)],
        out_specs=tuple([_HBM] * (2 * n)),
        input_output_aliases={i: i for i in range(2 * n)},
        compiler_params=pltpu.CompilerParams(has_side_effects=_EFFECT),
        name=name,
    )(*srcs, *lands, *send_sems, *recv_sems, after)
    return outs[:n], outs[n:]


def _adam_update(g, w_ref, m_ref, v_ref, g_ref, d_ref, nm_ref, nv_ref):
    mm = ADAM_B1 * m_ref[...] + (1.0 - ADAM_B1) * g
    vv = ADAM_B2 * v_ref[...] + (1.0 - ADAM_B2) * (g * g)
    m_hat = mm / (1.0 - ADAM_B1 ** ADAM_STEP)
    v_hat = vv / (1.0 - ADAM_B2 ** ADAM_STEP)
    g_ref[...] = g
    d_ref[...] = -ADAM_LR * (m_hat / (jnp.sqrt(v_hat) + ADAM_EPS) + ADAM_WD * w_ref[...])
    nm_ref[...] = mm
    nv_ref[...] = vv


def _adamw_own(w, own, own_idx, slots, m, v, name):
    r, c = w.shape[-2:]
    tr = _row_tile(r, 384)
    k = slots.shape[0]

    def body(i_ref, w_ref, o_ref, s_ref, m_ref, v_ref, g_ref, d_ref, nm_ref, nv_ref):
        del i_ref
        g = o_ref[...].astype(f32)
        for j in range(k):
            g = g + s_ref[j].astype(f32)
        _adam_update(g, w_ref, m_ref, v_ref, g_ref, d_ref, nm_ref, nv_ref)

    blk = pl.BlockSpec((None, tr, c), lambda i, ix: (0, i, 0))
    return pl.pallas_call(
        body,
        grid_spec=pltpu.PrefetchScalarGridSpec(
            num_scalar_prefetch=1,
            grid=(r // tr,),
            in_specs=[blk, pl.BlockSpec((None, tr, c), lambda i, ix: (ix[0], i, 0)),
                      pl.BlockSpec((k, tr, c), lambda i, ix: (0, i, 0)), blk, blk],
            out_specs=[blk] * 4,
        ),
        out_shape=[jax.ShapeDtypeStruct(w.shape, f32)] * 4,
        compiler_params=_params(("arbitrary",)),
        name=name,
    )(own_idx, w, own, slots, m, v)


def _adamw(w, slots, m, v, name):
    r, c = w.shape[-2:]
    tr = _row_tile(r, 128)

    def body(w_ref, s_ref, m_ref, v_ref, g_ref, d_ref, nm_ref, nv_ref):
        g = s_ref[0].astype(f32)
        for k in range(1, NDEV):
            g = g + s_ref[k].astype(f32)
        _adam_update(g, w_ref, m_ref, v_ref, g_ref, d_ref, nm_ref, nv_ref)

    if w.ndim == 3:
        blk = pl.BlockSpec((None, tr, c), lambda i: (0, i, 0))
    else:
        blk = pl.BlockSpec((tr, c), lambda i: (i, 0))
    return pl.pallas_call(
        body,
        grid=(r // tr,),
        in_specs=[blk, pl.BlockSpec((NDEV, tr, c), lambda i: (0, i, 0)), blk, blk],
        out_specs=[blk] * 4,
        out_shape=[jax.ShapeDtypeStruct(w.shape, f32)] * 4,
        compiler_params=_params(("arbitrary",)),
        name=name,
    )(w, slots, m, v)


class _Weights:
    def __init__(self, w_t, w_a, w_b, b_merge, w_o):
        self._w_t, self._rest = w_t, (w_a, w_b, b_merge, w_o)

    def first_half(self, after):
        del after
        return self._w_t, jnp.zeros((1,), jnp.int32)

    def second_half(self, after):
        del after
        return self._w_t, jnp.ones((1,), jnp.int32)

    def rest(self):
        return self._rest


def _local_step(x, tgt, norm_gain, weights, qn_a, kn_a, qn_b, kn_b, sink_a, rel_bias, on_weight_grads=None,
                core=None):
    two = lambda t: jnp.concatenate([t, t], axis=-1).reshape(1, LANES)
    ones = jnp.ones((1, LANES), f32)
    gains = jnp.stack([
        jnp.stack([two(qn_a), two(kn_a), ones]),
        jnp.stack([two(qn_b), two(kn_b), ones]),
        jnp.stack([two(qn_b), two(kn_b), ones]),
        jnp.stack([two(qn_b), two(kn_b), ones]),
    ])
    buckets = [jnp.asarray(_bucket_np(blk, d)) for blk, d, _ in GROUPS]
    bias = [_bias_expand(rel_bias, buckets[k], GROUPS[k][2], "bias_expand_%d" % k) for k in range(4)]

    hb, hbt, rstd = _rms(x, norm_gain)
    w_t, half = weights.first_half([hb] + bias)
    proj = _inproj_half(hb, w_t, half, None, "inproj_1")
    w_t, half = weights.second_half([proj])
    proj = _inproj_half(hb, w_t, half, proj, "inproj_2")
    w_a, w_b, b_merge, w_o = weights.rest()
    gl = _prep(proj, gains)
    o_a, l_a = _attn_fwd(gl, bias[0], sink_a.reshape(8), 0, 128, 1, "attn_fwd_a")
    fwd_b = [_attn_fwd(gl, bias[k], None, k, GROUPS[k][0], GROUPS[k][1], "attn_fwd_b%d" % k) for k in (1, 2, 3)]
    sink_b = jnp.repeat(sink_a.reshape(8), HD).reshape(1, 512)

    (dy, dyb, dproj, do_a, dd_a, do_b0, do_b1, do_b2, dd_b0, dd_b1, dd_b2, ya, yb, mg, dbr_a, dbr_b, loss, dbm,
     dsk) = _tail(x, tgt, o_a, l_a, [f[0] for f in fwd_b], [f[1] for f in fwd_b], proj, b_merge, w_a, w_b, w_o, sink_b)

    dw_o = _matmul_tokens(mg, dyb, "dw_out")
    dw_a = _matmul_tokens(ya, dbr_a, "dw_branch_a")
    dw_b = _matmul_tokens(yb, dbr_b, "dw_branch_b")
    if on_weight_grads is not None:
        early = on_weight_grads(dict(w_branch_a=dw_a, w_branch_b=dw_b, b_merge=dbm, w_out=dw_o))
        buckets = [buckets[0] + early.astype(jnp.int32)] + buckets[1:]

    dqkv_a, dbk_a = _attn_bwd(gl, bias[0], buckets[0], do_a, l_a, dd_a, 0, 128, 1, "attn_bwd_a")
    dproj, dg_a = _post_a(dqkv_a, proj, gains, dproj)
    dbk_b, dg_b = [], []
    for k, do_k, dd_k in ((1, do_b0, dd_b0), (2, do_b1, dd_b1), (3, do_b2, dd_b2)):
        dqkv, dbk = _attn_bwd(gl, bias[k], buckets[k], do_k, fwd_b[k - 1][1], dd_k, k, GROUPS[k][0], GROUPS[k][1],
                              "attn_bwd_b%d" % k)
        dproj, dg = _post_b(k, dqkv, proj, gains, dproj)
        dbk_b.append(dbk)
        dg_b.append(dg)
    dg_b = jnp.stack(dg_b)

    core = jnp.zeros((1,), jnp.int32) if core is None else core
    dw_other = _dw_in(hbt, dproj, 1 - core, "dw_in_other")
    sent = jnp.zeros((), f32) if on_weight_grads is None else on_weight_grads(dict(w_in_other=dw_other))
    dw_in = _dw_in(hbt, dproj, core + sent.astype(jnp.int32), "dw_in_own")
    token = jnp.zeros((), f32) if on_weight_grads is None else on_weight_grads(dict(w_in=dw_in))
    grad_x, d_norm_gain = _dh_norm_bwd(dproj, w_t, x, rstd, norm_gain + token, dy)

    fold = lambda t: t[..., :HD] + t[..., HD:]
    d_qn_a = fold(dg_a[0, 0])
    d_kn_a = fold(dg_a[1, 0])
    d_qn_b = fold(dg_b[:, 0, 0].sum(axis=0))
    d_kn_b = fold(dg_b[:, 1, 0].sum(axis=0))
    d_sink = dsk.reshape(8, HD)[:, 0]
    red = jnp.stack([dbk_a] + dbk_b)
    d_rel = red[:, :, 0, :32].reshape(32, 32).T
    return dict(loss=loss, grad_x=grad_x, norm_gain=d_norm_gain, w_in=dw_in, w_in_other=dw_other, q_norm_a=d_qn_a,
                k_norm_a=d_kn_a,
                q_norm_b=d_qn_b, k_norm_b=d_kn_b, sink_a=d_sink, rel_bias=d_rel, w_branch_a=dw_a, w_branch_b=dw_b,
                b_merge=dbm, w_out=dw_o)


SMALL = (("norm_gain", D), ("q_norm_a", HD), ("k_norm_a", HD), ("q_norm_b", HD), ("k_norm_b", HD), ("sink_a", 8),
         ("rel_bias", 1024))
SMALL_PAD = 2432


SMALL_USED = sum(sz for _, sz in SMALL)


def _pack_small(parts, loss=None):
    tail = jnp.zeros((SMALL_PAD - SMALL_USED,), f32)
    if loss is not None:
        tail = tail.at[0].set(loss.reshape(()))
    return jnp.concatenate([parts[n].reshape(-1) for n, _ in SMALL] + [tail]).reshape(1, SMALL_PAD)


def _unpack_small(flat, shapes):
    out, off = {}, 0
    for n, sz in SMALL:
        out[n] = flat[0, off:off + sz].reshape(shapes[n])
        off += sz
    return out


def kernel(x, norm_gain, w_in, q_norm_a, k_norm_a, q_norm_b, k_norm_b, sink_a, rel_bias, w_branch_a, w_branch_b, b_merge, w_out, loss_target, m_norm_gain, m_w_in, m_q_norm_a, m_k_norm_a, m_q_norm_b, m_k_norm_b, m_sink_a, m_rel_bias, m_w_branch_a, m_w_branch_b, m_b_merge, m_w_out, v_norm_gain, v_w_in, v_q_norm_a, v_k_norm_a, v_q_norm_b, v_k_norm_b, v_sink_a, v_rel_bias, v_w_branch_a, v_w_branch_b, v_b_merge, v_w_out):
    csh = D // NDEV
    w_in_t, m_w_in_t, v_w_in_t = (jnp.swapaxes(t, 1, 2) for t in (w_in, m_w_in, v_w_in))
    weights = _GatheredWeights([w_in_t[0].astype(bf16), w_branch_a[0].astype(bf16), w_branch_b[0].astype(bf16),
                                w_out[0].astype(bf16), b_merge[0]])

    pending = {}
    core = lax.axis_index("c").astype(jnp.int32).reshape(1)
    chip = (2 * lax.axis_index("x") + lax.axis_index("y")).astype(jnp.int32).reshape(1)
    me = (2 * chip + core).astype(jnp.int32)

    def start_exchange(gw):
        if "w_in_other" in gw:
            sems, lands, sent = _sibling_send_start(gw["w_in_other"])
            pending["sibling"] = (sems, lands)
            return sent
        if "w_in" in gw:
            from_sibling = _sibling_send_wait(*pending["sibling"], after=[gw["w_in"]])
            chip_sums = _pair_sum(gw["w_in"], from_sibling, "grad_pair_sum")
            pending["w_in"] = _scatter_start([chip_sums], [], "scatter_w_in_start")
            return pending["w_in"][4][0, 0]
        blocks = [gw["w_branch_a"].reshape(512, NDEV, csh).transpose(1, 0, 2).astype(bf16),
                  gw["w_branch_b"].reshape(512, NDEV, csh).transpose(1, 0, 2).astype(bf16),
                  gw["w_out"].reshape(NDEV, csh, D).astype(bf16),
                  gw["b_merge"].reshape(2, NDEV, csh).transpose(1, 0, 2)]
        pending["rest"] = _scatter_start([], blocks, "scatter_rest_start")
        return pending["rest"][4][0, 0]

    loc = _local_step(x[0], loss_target[0], norm_gain, weights, q_norm_a, k_norm_a, q_norm_b, k_norm_b, sink_a,
                      rel_bias, on_weight_grads=start_exchange, core=core)

    small_shapes = dict(norm_gain=(1, D), q_norm_a=(1, HD), k_norm_a=(1, HD), q_norm_b=(1, HD), k_norm_b=(1, HD),
                        sink_a=(1, 8), rel_bias=(32, 32))
    (r_small,) = _exchange([], [_pack_small(loc, loc["loss"])], "gather_small_grads")
    send_sems, recv_sems, srcs, lands, _ = pending["rest"]
    (s_a, s_b, s_o, s_bm), (r_a, r_b, r_o, r_bm) = _scatter_wait(
        send_sems, recv_sems, srcs, lands, r_small, "scatter_rest_wait")
    send_sems, recv_sems, srcs, lands, _ = pending["w_in"]
    (s_in,), (r_in,) = _scatter_wait(send_sems, recv_sems, srcs, lands, r_small, "scatter_w_in_wait")

    given = dict(norm_gain=norm_gain, q_norm_a=q_norm_a, k_norm_a=k_norm_a, q_norm_b=q_norm_b, k_norm_b=k_norm_b,
                 sink_a=sink_a, rel_bias=rel_bias)
    m_small = dict(norm_gain=m_norm_gain, q_norm_a=m_q_norm_a, k_norm_a=m_k_norm_a, q_norm_b=m_q_norm_b,
                   k_norm_b=m_k_norm_b, sink_a=m_sink_a, rel_bias=m_rel_bias)
    v_small = dict(norm_gain=v_norm_gain, q_norm_a=v_q_norm_a, k_norm_a=v_k_norm_a, q_norm_b=v_q_norm_b,
                   k_norm_b=v_k_norm_b, sink_a=v_sink_a, rel_bias=v_rel_bias)
    res = {
        "small": _adamw(_pack_small(given), r_small, _pack_small(m_small), _pack_small(v_small), "adamw_small"),
        "w_in": [jnp.swapaxes(t, 1, 2) for t in
                 _adamw_own(w_in_t, s_in, chip, r_in, m_w_in_t, v_w_in_t, "adamw_w_in")],
        "w_branch_a": _adamw_own(w_branch_a, s_a, me, r_a, m_w_branch_a, v_w_branch_a, "adamw_w_branch_a"),
        "w_branch_b": _adamw_own(w_branch_b, s_b, me, r_b, m_w_branch_b, v_w_branch_b, "adamw_w_branch_b"),
        "b_merge": _adamw_own(b_merge, s_bm, me, r_bm, m_b_merge, v_b_merge, "adamw_b_merge"),
        "w_out": _adamw_own(w_out, s_o, me, r_o, m_w_out, v_w_out, "adamw_w_out"),
    }
    order = ["norm_gain", "w_in", "q_norm_a", "k_norm_a", "q_norm_b", "k_norm_b", "sink_a", "rel_bias", "w_branch_a",
             "w_branch_b", "b_merge", "w_out"]
    outs = []
    for k in range(4):
        small = _unpack_small(res["small"][k], small_shapes)
        for n in order:
            outs.append(small[n] if n in small else res[n][k])
    loss = res["small"][0][0, SMALL_USED]
    return (loss, loc["grad_x"][None], *outs)
```

```python
import math

import numpy as np
import jax
import jax.numpy as jnp
from jax import lax
from jax.experimental import pallas as pl
from jax.experimental.pallas import tpu as pltpu

f32 = jnp.float32
bf16 = jnp.bfloat16

S = 4096
D = 1024
NA = 5376
NT = 3072
NW = NA + NT
WSH = NW // 8
HD = 64
LANES = 128
EPS = 1e-6
NEG = -1e30
SCALE = HD ** -0.5
TQ = 128
PAD = 128
SP = S + 2 * PAD
NDEV = 8
GROUPS = ((128, 1, 0), (64, 1, 8), (64, 4, 16), (64, 16, 24))
CHUNK = 256
PCHUNK = 128
RC = 64

ADAM_LR, ADAM_B1, ADAM_B2, ADAM_EPS, ADAM_WD, ADAM_STEP = 0.001, 0.9, 0.999, 1e-08, 0.01, 10

MIB = 1024 * 1024
NT_DIMS = (((1,), (1,)), ((), ()))
TN_DIMS = (((0,), (0,)), ((), ()))


def _params(sem=None, vmem_mib=48):
    return pltpu.CompilerParams(dimension_semantics=sem, vmem_limit_bytes=vmem_mib * MIB)


def _lo():
    return lax.broadcasted_iota(jnp.int32, (1, LANES), 1) < HD


def _head_ones():
    r = lax.broadcasted_iota(jnp.int32, (LANES, LANES), 0) // HD
    c = lax.broadcasted_iota(jnp.int32, (LANES, LANES), 1) // HD
    return jnp.where(r == c, 1.0, 0.0).astype(bf16)


def _half_sums(x, ones):
    hi = x.astype(bf16)
    mid = (x - hi.astype(f32)).astype(bf16)
    return (jnp.dot(hi, ones, preferred_element_type=f32) + jnp.dot(mid, ones, preferred_element_type=f32))


def _seg_sum(x, ones):
    outs = [_half_sums(x[:, b * LANES:(b + 1) * LANES], ones) for b in range(x.shape[1] // LANES)]
    return outs[0] if len(outs) == 1 else jnp.concatenate(outs, axis=1)


def _bucket_np(blk, stride):
    w = TQ + 2 * blk
    rel = np.arange(w)[None, :] - blk - np.arange(TQ)[:, None]
    band = np.abs(rel) <= blk
    r = rel * stride
    n = np.abs(r)
    nf = np.maximum(n, 8).astype(np.float32)
    large = 8 + (np.log(nf / np.float32(8)) / np.float32(math.log(128.0)) * np.float32(8)).astype(np.int32)
    large = np.minimum(large, 15)
    b = (r > 0).astype(np.int32) * 16 + np.where(n < 8, n, large)
    return np.where(band, b, -1).astype(np.int32)


def _rms(x, gain):
    ts = 512

    def body(x_ref, g_ref, h_ref, ht_ref, r_ref):
        xv = x_ref[...]
        r = lax.rsqrt(jnp.mean(xv * xv, axis=-1, keepdims=True) + EPS)
        h = (xv * r) * g_ref[...]
        h_ref[...] = h.astype(bf16)
        ht_ref[...] = h.T.astype(bf16)
        r_ref[...] = r

    return pl.pallas_call(
        body,
        grid=(S // ts,),
        in_specs=[pl.BlockSpec((ts, D), lambda i: (i, 0)), pl.BlockSpec((1, D), lambda i: (0, 0))],
        out_specs=[pl.BlockSpec((ts, D), lambda i: (i, 0)), pl.BlockSpec((D, ts), lambda i: (0, i)),
                   pl.BlockSpec((ts, 1), lambda i: (i, 0))],
        out_shape=[jax.ShapeDtypeStruct((S, D), bf16), jax.ShapeDtypeStruct((D, S), bf16),
                   jax.ShapeDtypeStruct((S, 1), f32)],
        compiler_params=_params(("arbitrary",)),
        name="rms",
    )(x, gain)


def _inproj_half(hb, w_t, half, proj, name):
    ts = 512
    tn = NW // 2
    per = NW // 2 // tn

    def body(h_idx, h_ref, w_ref, *rest):
        del h_idx
        rest[-1][...] = lax.dot_general(h_ref[...], w_ref[...], NT_DIMS, preferred_element_type=f32)

    in_specs = [pl.BlockSpec((ts, D), lambda i, n, hf: (i, 0)),
                pl.BlockSpec((tn, D), lambda i, n, hf: (hf[0] * per + n, 0))]
    args = [half, hb, w_t]
    aliases = {}
    if proj is not None:
        in_specs.append(pl.BlockSpec(memory_space=pl.ANY))
        args.append(proj)
        aliases = {3: 0}
    return pl.pallas_call(
        body,
        grid_spec=pltpu.PrefetchScalarGridSpec(
            num_scalar_prefetch=1,
            grid=(S // ts, per),
            in_specs=in_specs,
            out_specs=pl.BlockSpec((ts, tn), lambda i, n, hf: (i, hf[0] * per + n)),
        ),
        out_shape=jax.ShapeDtypeStruct((S, NW), f32),
        input_output_aliases=aliases,
        compiler_params=_params(("arbitrary", "arbitrary")),
        name=name,
    )(*args)


def _bias_expand(table, bucket, c0, name):
    tq, w = bucket.shape
    blk = (w - tq) // 2

    def body(tab_ref, bk_ref, o_ref):
        h = pl.program_id(0)
        bk = bk_ref[...]

        def step(b, acc):
            return jnp.where(bk == b, tab_ref[b, c0 + h], acc)

        inner = lax.fori_loop(0, 32, step, jnp.full((tq, w), NEG, f32))
        col = lax.broadcasted_iota(jnp.int32, (1, w), 1)
        o_ref[0] = jnp.where(col < blk, NEG, inner)
        o_ref[1] = inner
        o_ref[2] = jnp.where(col >= tq + blk, NEG, inner)

    return pl.pallas_call(
        body,
        grid=(8,),
        in_specs=[pl.BlockSpec(memory_space=pltpu.SMEM), pl.BlockSpec((tq, w), lambda h: (0, 0))],
        out_specs=pl.BlockSpec((3, None, tq, w), lambda h: (0, h, 0, 0)),
        out_shape=jax.ShapeDtypeStruct((3, 8, tq, w), f32),
        compiler_params=_params(("arbitrary",)),
        name=name,
    )(table, bucket)


def _tile_kind(t, seq):
    m0 = jnp.bitwise_and(t * TQ, seq - 1)
    return jnp.where(m0 == 0, 0, jnp.where(m0 == seq - TQ, 2, 1))


def _col_block(g, j):
    kind = j // 4
    hp = j % 4
    a = jnp.where(kind == 0, hp, 3 + kind)
    b = 6 + 12 * kind + 4 * (g - 1) + hp
    return jnp.where(g == 0, a, b)


def _prep(proj_a, gains):
    def body(p0_ref, p1_ref, p2_ref, p3_ref, g_ref, o_ref):
        g = pl.program_id(0)
        kind = pl.program_id(1)
        lo = _lo()
        ones = _head_ones()
        half = jnp.where(lo, 0, 1)
        gain = g_ref[...]

        def norm_store(xv, u, dst, dup):
            if dup:
                take = (kind == 0) | (half == u // 2)
                xv = jnp.where(take, xv, pltpu.roll(xv, HD, 1))
            r = lax.rsqrt(_half_sums(xv * xv, ones) * (1.0 / HD) + EPS)
            r = jnp.where(kind == 2, 1.0, r)
            yv = (xv * r) * gain
            yv = jnp.where(kind == 0, yv * SCALE, yv)
            o_ref[u, PAD + dst:PAD + dst + CHUNK, :] = yv.astype(bf16)

        for u in range(4):
            o_ref[u, 0:PAD, :] = jnp.zeros((PAD, LANES), bf16)
            o_ref[u, PAD + S:SP, :] = jnp.zeros((PAD, LANES), bf16)
        for gi, (_, d, _) in enumerate(GROUPS):
            @pl.when(g == gi)
            def _():
                seq = S // d
                for u, p_ref in enumerate((p0_ref, p1_ref, p2_ref, p3_ref)):
                    for c in range(d):
                        for i in range(seq // CHUNK):
                            if d == 1:
                                xv = p_ref[i * CHUNK:(i + 1) * CHUNK, :]
                            else:
                                xv = p_ref[pl.ds(c + i * CHUNK * d, CHUNK, stride=d), :]
                            norm_store(xv, u, c * seq + i * CHUNK, gi == 0)

    return pl.pallas_call(
        body,
        grid=(4, 3),
        in_specs=[pl.BlockSpec((S, LANES), lambda g, kind, u=u: (0, _col_block(g, 4 * kind + u))) for u in range(4)] + [
            pl.BlockSpec((None, None, 1, LANES), lambda g, kind: (g, kind, 0, 0)),
        ],
        out_specs=pl.BlockSpec((None, 4, SP, LANES), lambda g, kind: (g, kind, 0, 0)),
        out_shape=jax.ShapeDtypeStruct((4, 12, SP, LANES), bf16),
        compiler_params=_params(("arbitrary", "arbitrary")),
        name="prep",
    )(proj_a, proj_a, proj_a, proj_a, gains)


def _token_rows(t, r0, n, d):
    if d == 1:
        return pl.ds(pl.multiple_of(t * TQ, TQ) + r0, n)
    per = S // d // TQ
    return pl.ds(((t % per) * TQ + r0) * d + t // per, n, stride=d)


def _stack_heads(t, lo):
    z = jnp.zeros_like(t)
    return jnp.concatenate([jnp.where(lo, t, z), jnp.where(lo, z, t)], axis=0)


def _unstack_heads(t2, lo):
    return jnp.where(lo, t2[:TQ], t2[TQ:])


def _attn_fwd(gl, bias, sink, g, blk, d, name):
    w = TQ + 2 * blk
    seq = S // d
    use_sink = sink is not None

    def body(*refs):
        if use_sink:
            sink_ref, q_ref, k_ref, v_ref, b_ref, o_ref, l_ref, s0, s1, p0, p1, lse_scr = refs
        else:
            q_ref, k_ref, v_ref, b_ref, o_ref, l_ref, s0, s1, p0, p1, lse_scr = refs
        hp = pl.program_id(0)
        lo = _lo()
        s_bufs, p_bufs = (s0, s1), (p0, p1)

        def scores(p, slot):
            for u in range(2):
                f0 = pl.multiple_of((2 * p + u) * TQ, TQ)
                q2 = _stack_heads(q_ref[pl.ds(PAD + f0, TQ), :], lo)
                kw = k_ref[pl.ds(PAD - blk + f0, w), :]
                s_bufs[slot][u] = lax.dot_general(q2, kw, NT_DIMS, preferred_element_type=f32)

        def softmax(p, slot):
            for u in range(2):
                t = 2 * p + u
                kind = _tile_kind(t, seq)
                for h in range(2):
                    for r in range(TQ // RC):
                        rows = slice(h * TQ + r * RC, h * TQ + (r + 1) * RC)
                        logit = s_bufs[slot][u, rows, :] + b_ref[kind, h, r * RC:(r + 1) * RC, :]
                        m = jnp.max(logit, axis=1, keepdims=True)
                        e = jnp.exp(logit - m)
                        lse = m + jnp.log(jnp.sum(e, axis=1, keepdims=True))
                        if use_sink:
                            sk = sink_ref[2 * hp + h]
                            mx = jnp.maximum(lse, sk)
                            lse = mx + jnp.log(jnp.exp(lse - mx) + jnp.exp(sk - mx))
                        p_bufs[slot][u, rows, :] = (e * jnp.exp(m - lse)).astype(bf16)
                        lse_scr[u, rows, :] = jnp.broadcast_to(lse, (RC, LANES))
                l_ref[_token_rows(t, 0, TQ, d), :] = jnp.where(lo, lse_scr[u, 0:TQ, :], lse_scr[u, TQ:2 * TQ, :])

        def values(p, slot):
            for u in range(2):
                t = 2 * p + u
                vw = v_ref[pl.ds(PAD - blk + pl.multiple_of(t * TQ, TQ), w), :]
                o2 = jnp.dot(p_bufs[slot][u], vw, preferred_element_type=f32)
                o_ref[_token_rows(t, 0, TQ, d), :] = _unstack_heads(o2, lo)

        npair = S // TQ // 2
        scores(0, 0)
        scores(1, 1)
        softmax(0, 0)

        def steady(k, carry):
            p = 2 * k + 2
            scores(p, 0)
            softmax(p - 1, 1)
            values(p - 2, 0)
            scores(p + 1, 1)
            softmax(p, 0)
            values(p - 1, 1)
            return carry

        lax.fori_loop(0, (npair - 2) // 2, steady, 0)
        softmax(npair - 1, 1)
        values(npair - 2, 0)
        values(npair - 1, 1)

    in_specs = [
        pl.BlockSpec((None, None, SP, LANES), lambda hp: (g, hp, 0, 0)),
        pl.BlockSpec((None, None, SP, LANES), lambda hp: (g, 4 + hp, 0, 0)),
        pl.BlockSpec((None, None, SP, LANES), lambda hp: (g, 8 + hp, 0, 0)),
        pl.BlockSpec((3, 2, TQ, w), lambda hp: (0, hp, 0, 0)),
    ]
    args = [gl, gl, gl, bias]
    if use_sink:
        in_specs = [pl.BlockSpec(memory_space=pltpu.SMEM)] + in_specs
        args = [sink] + args
    out = pl.BlockSpec((S, LANES), lambda hp: (0, hp))
    return pl.pallas_call(
        body,
        grid=(4,),
        in_specs=in_specs,
        out_specs=[out, out],
        out_shape=[jax.ShapeDtypeStruct((S, 4 * LANES), f32)] * 2,
        scratch_shapes=[pltpu.VMEM((2, 2 * TQ, w), f32), pltpu.VMEM((2, 2 * TQ, w), f32),
                        pltpu.VMEM((2, 2 * TQ, w), bf16), pltpu.VMEM((2, 2 * TQ, w), bf16),
                        pltpu.VMEM((2, 2 * TQ, LANES), f32)],
        compiler_params=_params(("arbitrary",)),
        name=name,
    )(*args)


def _attn_bwd(gl, bias, bucket, do, lse, dd, g, blk, d, name):
    w = TQ + 2 * blk
    seq = S // d

    def body(q_ref, k_ref, v_ref, b_ref, bk_ref, do_ref, l_ref, d_ref, dqkv_ref, dbk_ref,
             db_acc, s0, s1, dp0, dp1, pb0, pb1, ds0, ds1, dk_acc, dv_acc):
        lo = _lo()
        hi = jnp.logical_not(lo)
        dk_acc[...] = jnp.zeros((SP, LANES), f32)
        dv_acc[...] = jnp.zeros((SP, LANES), f32)
        db_acc[...] = jnp.zeros((2 * TQ, w), f32)
        s_bufs, dp_bufs, pb_bufs, ds_bufs = (s0, s1), (dp0, dp1), (pb0, pb1), (ds0, ds1)

        def stacked(t):
            f0 = pl.multiple_of(t * TQ, TQ)
            q2 = _stack_heads(q_ref[pl.ds(PAD + f0, TQ), :], lo)
            do2 = _stack_heads(do_ref[_token_rows(t, 0, TQ, d), :].astype(bf16), lo)
            return f0, q2, do2

        def scores(p, slot):
            for u in range(2):
                f0, q2, do2 = stacked(2 * p + u)
                win = pl.ds(PAD - blk + f0, w)
                s_bufs[slot][u] = lax.dot_general(q2, k_ref[win, :], NT_DIMS, preferred_element_type=f32)
                dp_bufs[slot][u] = lax.dot_general(do2, v_ref[win, :], NT_DIMS, preferred_element_type=f32)

        def grads(p, slot):
            for u in range(2):
                t = 2 * p + u
                kind = _tile_kind(t, seq)
                for h in range(2):
                    msk = lo if h == 0 else hi
                    for r in range(TQ // RC):
                        rows = slice(h * TQ + r * RC, h * TQ + (r + 1) * RC)
                        src = _token_rows(t, r * RC, RC, d)
                        lh = jnp.max(jnp.where(msk, l_ref[src, :], -jnp.inf), axis=1, keepdims=True)
                        dh = jnp.max(jnp.where(msk, d_ref[src, :], -jnp.inf), axis=1, keepdims=True)
                        logit = s_bufs[slot][u, rows, :] + b_ref[kind, h, r * RC:(r + 1) * RC, :]
                        pr = jnp.exp(logit - lh)
                        ds = pr * (dp_bufs[slot][u, rows, :] - dh)
                        db_acc[rows, :] += ds
                        pb_bufs[slot][u, rows, :] = pr.astype(bf16)
                        ds_bufs[slot][u, rows, :] = ds.astype(bf16)

        def accumulate(p, slot):
            for u in range(2):
                f0, q2, do2 = stacked(2 * p + u)
                win = pl.ds(PAD - blk + f0, w)
                dsb = ds_bufs[slot][u]
                dq2 = jnp.dot(dsb, k_ref[win, :], preferred_element_type=f32)
                dqkv_ref[0, pl.ds(PAD + f0, TQ), :] = _unstack_heads(dq2, lo).astype(bf16)
                dk_acc[win, :] += lax.dot_general(dsb, q2, TN_DIMS, preferred_element_type=f32)
                dv_acc[win, :] += lax.dot_general(pb_bufs[slot][u], do2, TN_DIMS, preferred_element_type=f32)

        npair = S // TQ // 2
        scores(0, 0)
        scores(1, 1)
        grads(0, 0)

        def steady(k, carry):
            p = 2 * k + 2
            scores(p, 0)
            grads(p - 1, 1)
            accumulate(p - 2, 0)
            scores(p + 1, 1)
            grads(p, 0)
            accumulate(p - 1, 1)
            return carry

        lax.fori_loop(0, (npair - 2) // 2, steady, 0)
        grads(npair - 1, 1)
        accumulate(npair - 2, 0)
        accumulate(npair - 1, 1)
        for i in range(SP // CHUNK):
            rows = slice(i * CHUNK, (i + 1) * CHUNK)
            dqkv_ref[1, rows, :] = dk_acc[rows, :].astype(bf16)
            dqkv_ref[2, rows, :] = dv_acc[rows, :].astype(bf16)

        bk = bk_ref[...]
        lane = lax.broadcasted_iota(jnp.int32, (8, LANES), 1)
        for h in range(2):
            db = db_acc[h * TQ:(h + 1) * TQ, :]
            acc = jnp.zeros((8, LANES), f32)
            for b in range(32):
                part = jnp.where(bk == b, db, 0.0).reshape(TQ // 8, 8, w).sum(axis=0)
                tot = jnp.sum(jnp.sum(part, axis=1, keepdims=True), axis=0, keepdims=True)
                acc = jnp.where(lane == b, tot, acc)
            dbk_ref[h] = acc

    def gcol(off):
        return pl.BlockSpec((None, None, SP, LANES), lambda hp: (g, off + hp, 0, 0))

    row = pl.BlockSpec((S, LANES), lambda hp: (0, hp))
    return pl.pallas_call(
        body,
        grid=(4,),
        in_specs=[gcol(0), gcol(4), gcol(8), pl.BlockSpec((3, 2, TQ, w), lambda hp: (0, hp, 0, 0)),
                  pl.BlockSpec((TQ, w), lambda hp: (0, 0)), row, row, row],
        out_specs=[pl.BlockSpec((3, None, SP, LANES), lambda hp: (0, hp, 0, 0)),
                   pl.BlockSpec((2, 8, LANES), lambda hp: (hp, 0, 0))],
        out_shape=[
            jax.ShapeDtypeStruct((3, 4, SP, LANES), bf16),
            jax.ShapeDtypeStruct((8, 8, LANES), f32),
        ],
        scratch_shapes=([pltpu.VMEM((2 * TQ, w), f32)] + [pltpu.VMEM((2, 2 * TQ, w), f32)] * 4
                        + [pltpu.VMEM((2, 2 * TQ, w), bf16)] * 4 + [pltpu.VMEM((SP, LANES), f32)] * 2),
        compiler_params=_params(("arbitrary",), vmem_mib=56),
        name=name,
    )(gl, gl, gl, bias, bucket, do, lse, dd)


def _sigmoid(z):
    return 1.0 / (1.0 + jnp.exp(-z))


def _tail(x, tgt, o_a, l_a, o_b, l_b, proj, bm, w_a, w_b, w_o, sink_b):
    ts = 256

    def body(x_ref, t_ref, oa_ref, la_ref, ob0_ref, ob1_ref, ob2_ref, lb0_ref, lb1_ref, lb2_ref,
             ga_ref, gb_ref, m0_ref, m1_ref, bm_ref, wa_ref, wb_ref, wo_ref, sk_ref,
             dy_ref, dyb_ref, dt_ref, doa_ref, dda_ref, dob0_ref, dob1_ref, dob2_ref, ddb0_ref, ddb1_ref, ddb2_ref,
             ya_ref, yb_ref, mg_ref, dbra_ref, dbrb_ref, loss_ref, dbm_ref, dsk_ref):
        i = pl.program_id(0)

        @pl.when(i == 0)
        def _():
            loss_ref[...] = jnp.zeros_like(loss_ref)
            dbm_ref[...] = jnp.zeros_like(dbm_ref)
            dsk_ref[...] = jnp.zeros_like(dsk_ref)

        ga = ga_ref[...]
        sa = _sigmoid(ga)
        silu_a = ga * sa
        oa = oa_ref[...]
        ya = oa * silu_a
        gb = gb_ref[...]
        sb = _sigmoid(gb)
        silu_b = gb * sb
        ob = [ob0_ref[...], ob1_ref[...], ob2_ref[...]]
        lb = [lb0_ref[...], lb1_ref[...], lb2_ref[...]]
        mx = jnp.maximum(jnp.maximum(lb[0], lb[1]), lb[2])
        ex = [jnp.exp(v - mx) for v in lb]
        den = ex[0] + ex[1] + ex[2]
        alpha = [e / den for e in ex]
        ybc = alpha[0] * ob[0] + alpha[1] * ob[1] + alpha[2] * ob[2]
        yb = ybc * silu_b
        yab = ya.astype(bf16)
        ybb = yb.astype(bf16)
        br_a = jnp.dot(yab, wa_ref[...], preferred_element_type=f32)
        br_b = jnp.dot(ybb, wb_ref[...], preferred_element_type=f32)
        g0 = _sigmoid(m0_ref[...] + bm_ref[0:1, :])
        g1 = _sigmoid(m1_ref[...] + bm_ref[1:2, :])
        merged = g0 * br_a + g1 * br_b
        mgb = merged.astype(bf16)
        y = x_ref[...] + jnp.dot(mgb, wo_ref[...], preferred_element_type=f32)
        err = y - t_ref[...]
        part = jnp.sum(jnp.sum(err * err, axis=1, keepdims=True), axis=0, keepdims=True)
        loss_ref[...] += part * (0.5 / D)
        dy = err * (1.0 / D)
        dyb = dy.astype(bf16)
        dmerged = lax.dot_general(dyb, wo_ref[...], NT_DIMS, preferred_element_type=f32)
        dbr_a = (dmerged * g0).astype(bf16)
        dbr_b = (dmerged * g1).astype(bf16)
        dm0 = dmerged * br_a * (g0 * (1.0 - g0))
        dm1 = dmerged * br_b * (g1 * (1.0 - g1))
        dbm_ref[0:1, :] += jnp.sum(dm0, axis=0, keepdims=True)
        dbm_ref[1:2, :] += jnp.sum(dm1, axis=0, keepdims=True)
        dya = lax.dot_general(dbr_a, wa_ref[...], NT_DIMS, preferred_element_type=f32)
        dyb2 = lax.dot_general(dbr_b, wb_ref[...], NT_DIMS, preferred_element_type=f32)
        do_a = dya * silu_a
        dga = dya * oa * (sa * (1.0 + ga * (1.0 - sa)))
        ones = _head_ones()
        delta_a = _seg_sum(do_a * oa, ones)
        dsk_ref[...] -= jnp.sum(delta_a * jnp.exp(sk_ref[...] - la_ref[...]), axis=0, keepdims=True)
        dybc = dyb2 * silu_b
        dgb = dyb2 * ybc * (sb * (1.0 + gb * (1.0 - sb)))
        dbar = _seg_sum(dybc * ybc, ones)
        dy_ref[...] = dy
        dyb_ref[...] = dyb
        dt_ref[:, 0:512] = dga.astype(bf16)
        dt_ref[:, 512:1024] = dgb.astype(bf16)
        dt_ref[:, 1024:2048] = dm0.astype(bf16)
        dt_ref[:, 2048:3072] = dm1.astype(bf16)
        doa_ref[...] = do_a.astype(bf16)
        dda_ref[...] = delta_a
        for k, (dob_ref, ddb_ref) in enumerate(((dob0_ref, ddb0_ref), (dob1_ref, ddb1_ref), (dob2_ref, ddb2_ref))):
            dob_ref[...] = alpha[k] * dybc
            ddb_ref[...] = alpha[k] * dbar
        ya_ref[...] = ya.T.astype(bf16)
        yb_ref[...] = yb.T.astype(bf16)
        mg_ref[...] = merged.T.astype(bf16)
        dbra_ref[...] = dbr_a
        dbrb_ref[...] = dbr_b

    def rows(n, blk=0):
        return pl.BlockSpec((ts, n), lambda i: (i, blk))

    def whole(r, c):
        return pl.BlockSpec((r, c), lambda i: (0, 0))

    def cols(n):
        return pl.BlockSpec((n, ts), lambda i: (0, i))

    def gate_cols(n, col):
        return pl.BlockSpec((pl.Element(ts), pl.Element(n)), lambda i: (i * ts, NA + col))

    outs = [
        ((S, D), f32, rows(D)), ((S, D), bf16, rows(D)), ((S, NW), bf16, gate_cols(NT, 0)),
        ((S, 512), bf16, rows(512)), ((S, 512), f32, rows(512)),
        ((S, 512), f32, rows(512)), ((S, 512), f32, rows(512)), ((S, 512), f32, rows(512)),
        ((S, 512), f32, rows(512)), ((S, 512), f32, rows(512)), ((S, 512), f32, rows(512)),
        ((512, S), bf16, cols(512)), ((512, S), bf16, cols(512)), ((D, S), bf16, cols(D)),
        ((S, D), bf16, rows(D)), ((S, D), bf16, rows(D)),
        ((1, 1), f32, whole(1, 1)), ((2, D), f32, whole(2, D)), ((1, 512), f32, whole(1, 512)),
    ]
    return pl.pallas_call(
        body,
        grid=(S // ts,),
        in_specs=[
            rows(D), rows(D), rows(512), rows(512), rows(512), rows(512), rows(512), rows(512), rows(512), rows(512),
            gate_cols(512, 0), gate_cols(512, 512), gate_cols(D, 1024), gate_cols(D, 2048), whole(2, D),
            whole(512, D), whole(512, D), whole(D, D), whole(1, 512),
        ],
        out_specs=[o[2] for o in outs],
        out_shape=[jax.ShapeDtypeStruct(o[0], o[1]) for o in outs],
        compiler_params=_params(("arbitrary",), vmem_mib=60),
        name="tail",
    )(x, tgt, o_a, l_a, *o_b, *l_b, proj, proj, proj, proj, bm, w_a, w_b, w_o, sink_b)


def _norm_bwd(xv, dyv, gain, ones):
    r = lax.rsqrt(_half_sums(xv * xv, ones) * (1.0 / HD) + EPS)
    yv = xv * r
    u = dyv * gain
    dxv = r * (u - yv * (_half_sums(u * yv, ones) * (1.0 / HD)))
    return dxv, jnp.sum(dyv * yv, axis=0, keepdims=True)


def _post_b(g, dqkv, proj_a, gains, dproj):
    d = GROUPS[g][1]
    seq = S // d

    def body(d_ref, pa_ref, pb_ref, g_ref, alias_ref, o_ref, dg_ref, nat_a, nat_b):
        del alias_ref
        pj = pl.program_id(0)
        kind = pj // 2
        q_scale = jnp.where(kind == 0, SCALE, 1.0)
        gain = g_ref[...] * q_scale
        ones = _head_ones()

        @pl.when(pj % 2 == 0)
        def _():
            dg_ref[...] = jnp.zeros_like(dg_ref)

        def columns(with_norm):
            for u, (p_ref, nat) in enumerate(((pa_ref, nat_a), (pb_ref, nat_b))):
                for c in range(d):
                    for i in range(seq // PCHUNK):
                        src = c * seq + i * PCHUNK
                        if d == 1:
                            idx = slice(src, src + PCHUNK)
                        else:
                            idx = pl.ds(c + i * PCHUNK * d, PCHUNK, stride=d)
                        dyv = d_ref[u, PAD + src:PAD + src + PCHUNK, :].astype(f32)
                        if with_norm:
                            dyv, dg = _norm_bwd(p_ref[idx, :], dyv, gain, ones)
                            dg_ref[...] += dg * q_scale
                        nat[idx, :] = dyv
                for i in range(S // CHUNK):
                    rows = slice(i * CHUNK, (i + 1) * CHUNK)
                    o_ref[rows, u * LANES:(u + 1) * LANES] = nat[rows, :].astype(bf16)

        pl.when(kind < 2)(lambda: columns(True))
        pl.when(kind == 2)(lambda: columns(False))

    def pcol(u):
        return pl.BlockSpec((S, LANES), lambda pj: (0, _col_block(g, 2 * jnp.minimum(pj, 3) + u)))

    return pl.pallas_call(
        body,
        grid=(6,),
        in_specs=[
            pl.BlockSpec((None, 2, SP, LANES), lambda pj: (pj // 2, pj % 2, 0, 0)),
            pcol(0), pcol(1),
            pl.BlockSpec((None, None, 1, LANES), lambda pj: (g, pj // 2, 0, 0)),
            pl.BlockSpec(memory_space=pl.ANY),
        ],
        out_specs=[
            pl.BlockSpec((S, 2 * LANES), lambda pj: (0, _col_block(g, 2 * pj) // 2)),
            pl.BlockSpec((None, 1, LANES), lambda pj: (pj // 2, 0, 0)),
        ],
        out_shape=[jax.ShapeDtypeStruct((S, NW), bf16), jax.ShapeDtypeStruct((3, 1, LANES), f32)],
        scratch_shapes=[pltpu.VMEM((S, LANES), f32), pltpu.VMEM((S, LANES), f32)],
        input_output_aliases={4: 0},
        compiler_params=_params(("arbitrary",)),
        name="post_b%d" % g,
    )(dqkv, proj_a, proj_a, gains, dproj)


def _post_a(dqkv, proj_a, gains, dproj):
    def body(q_ref, e_ref, p_ref, g_ref, alias_ref, o_ref, dg_ref):
        del alias_ref
        j = pl.program_id(0)
        q_scale = jnp.where(j < 4, SCALE, 1.0)
        gain = g_ref[...] * q_scale
        lo = _lo()
        ones = _head_ones()

        @pl.when((j == 0) | (j >= 4))
        def _():
            dg_ref[...] = jnp.zeros_like(dg_ref)

        def column(folded, with_norm):
            for i in range(S // PCHUNK):
                r0 = i * PCHUNK
                rows = slice(PAD + r0, PAD + r0 + PCHUNK)
                if folded:
                    t0 = e_ref[0, rows, :].astype(f32) + e_ref[1, rows, :].astype(f32)
                    t1 = e_ref[2, rows, :].astype(f32) + e_ref[3, rows, :].astype(f32)
                    dyv = jnp.where(lo, t0 + pltpu.roll(t0, HD, 1), t1 + pltpu.roll(t1, HD, 1))
                else:
                    dyv = q_ref[rows, :].astype(f32)
                if with_norm:
                    dyv, dg = _norm_bwd(p_ref[r0:r0 + PCHUNK, :], dyv, gain, ones)
                    dg_ref[...] += dg * q_scale
                o_ref[r0:r0 + PCHUNK, :] = dyv.astype(bf16)

        pl.when(j < 4)(lambda: column(False, True))
        pl.when(j == 4)(lambda: column(True, True))
        pl.when(j == 5)(lambda: column(True, False))

    return pl.pallas_call(
        body,
        grid=(6,),
        in_specs=[
            pl.BlockSpec((None, None, SP, LANES), lambda j: (0, jnp.minimum(j, 3), 0, 0)),
            pl.BlockSpec((None, 4, SP, LANES), lambda j: (jnp.clip(j - 3, 1, 2), 0, 0, 0)),
            pl.BlockSpec((S, LANES), lambda j: (0, jnp.minimum(j, 4))),
            pl.BlockSpec((None, None, 1, LANES), lambda j: (0, jnp.maximum(j - 3, 0), 0, 0)),
            pl.BlockSpec(memory_space=pl.ANY),
        ],
        out_specs=[
            pl.BlockSpec((S, LANES), lambda j: (0, j)),
            pl.BlockSpec((None, 1, LANES), lambda j: (jnp.maximum(j - 3, 0), 0, 0)),
        ],
        out_shape=[jax.ShapeDtypeStruct((S, NW), bf16), jax.ShapeDtypeStruct((3, 1, LANES), f32)],
        input_output_aliases={4: 0},
        compiler_params=_params(("arbitrary",)),
        name="post_a",
    )(dqkv, dqkv, proj_a, gains, dproj)


def _dh_norm_bwd(dproj, w, x, rstd, gain, dy):
    ts = 1024
    tk = NW // 6
    nk = NW // tk

    def body(d_ref, w_ref, x_ref, r_ref, g_ref, dy_ref, gx_ref, dgn_ref, acc):
        i = pl.program_id(0)
        k = pl.program_id(1)

        @pl.when((i == 0) & (k == 0))
        def _():
            dgn_ref[...] = jnp.zeros_like(dgn_ref)

        @pl.when(k == 0)
        def _():
            acc[...] = jnp.zeros_like(acc)

        acc[...] += jnp.dot(d_ref[...], w_ref[...], preferred_element_type=f32)

        @pl.when(k == nk - 1)
        def _():
            dh = acc[...]
            xh = x_ref[...] * r_ref[...]
            u = dh * g_ref[...]
            dx = r_ref[...] * (u - xh * jnp.mean(u * xh, axis=-1, keepdims=True))
            gx_ref[...] = dy_ref[...] + dx
            dgn_ref[...] += jnp.sum(dh * xh, axis=0, keepdims=True)

    return pl.pallas_call(
        body,
        grid=(S // ts, nk),
        in_specs=[
            pl.BlockSpec((ts, tk), lambda i, k: (i, k)),
            pl.BlockSpec((tk, D), lambda i, k: (k, 0)),
            pl.BlockSpec((ts, D), lambda i, k: (i, 0)),
            pl.BlockSpec((ts, 1), lambda i, k: (i, 0)),
            pl.BlockSpec((1, D), lambda i, k: (0, 0)),
            pl.BlockSpec((ts, D), lambda i, k: (i, 0)),
        ],
        out_specs=[pl.BlockSpec((ts, D), lambda i, k: (i, 0)), pl.BlockSpec((1, D), lambda i, k: (0, 0))],
        out_shape=[jax.ShapeDtypeStruct((S, D), f32), jax.ShapeDtypeStruct((1, D), f32)],
        scratch_shapes=[pltpu.VMEM((ts, D), f32)],
        compiler_params=_params(("arbitrary", "arbitrary"), vmem_mib=56),
        name="dh_norm_bwd",
    )(dproj, w, x, rstd, gain, dy)


def _dw_in(hbt, dproj, parity, name):
    tk = 2048
    win = WSH + 96

    def body(par_ref, a_ref, b_ref, o_ref, acc):
        p = 2 * pl.program_id(0) + par_ref[0]
        k = pl.program_id(1)

        @pl.when(k == 0)
        def _():
            acc[...] = jnp.zeros_like(acc)

        acc[...] += jnp.dot(a_ref[...], b_ref[...], preferred_element_type=f32)

        @pl.when(k == S // tk - 1)
        def _():
            acc_t = acc[...].T
            for pp in range(NDEV):
                off = (WSH * pp) % LANES

                @pl.when(p == pp)
                def _():
                    o_ref[...] = acc_t[off:off + WSH, :].astype(bf16)

    return pl.pallas_call(
        body,
        grid_spec=pltpu.PrefetchScalarGridSpec(
            num_scalar_prefetch=1,
            grid=(NDEV // 2, S // tk),
            in_specs=[
                pl.BlockSpec((D, tk), lambda q, k, par: (0, k)),
                pl.BlockSpec((pl.Element(tk), pl.Element(win)),
                             lambda q, k, par: (k * tk, (WSH * (2 * q + par[0])) // LANES * LANES)),
            ],
            out_specs=pl.BlockSpec((None, WSH, D), lambda q, k, par: (q, 0, 0)),
            scratch_shapes=[pltpu.VMEM((D, win), f32)],
        ),
        out_shape=jax.ShapeDtypeStruct((NDEV // 2, WSH, D), bf16),
        compiler_params=_params(("arbitrary", "arbitrary")),
        name=name,
    )(parity, hbt, dproj)


def _matmul_tokens(at, b, name):
    m, n = at.shape[0], b.shape[1]
    tn = 1024
    tk = 1024

    def body(a_ref, b_ref, o_ref):
        @pl.when(pl.program_id(1) == 0)
        def _():
            o_ref[...] = jnp.zeros_like(o_ref)

        o_ref[...] += jnp.dot(a_ref[...], b_ref[...], preferred_element_type=f32)

    return pl.pallas_call(
        body,
        grid=(n // tn, S // tk),
        in_specs=[pl.BlockSpec((m, tk), lambda j, k: (0, k)), pl.BlockSpec((tk, tn), lambda j, k: (k, j))],
        out_specs=pl.BlockSpec((m, tn), lambda j, k: (0, j)),
        out_shape=jax.ShapeDtypeStruct((m, n), f32),
        compiler_params=_params(("arbitrary", "arbitrary")),
        name=name,
    )(at, b)


def _exchange(scatter, gather, name):
    arrs = list(scatter) + list(gather)
    n = len(arrs)
    ns = len(scatter)

    def body(*refs):
        ins, outs = refs[:n], refs[n:2 * n]
        send_sems, recv_sems, local_sems = refs[2 * n:]
        x, y, c = lax.axis_index("x"), lax.axis_index("y"), lax.axis_index("c")
        me = 4 * x + 2 * y + c
        local, remote = [], []
        for a in range(n):
            lc = pltpu.make_async_copy(ins[a].at[me] if a < ns else ins[a], outs[a].at[me], local_sems.at[a])
            lc.start()
            local.append(lc)
            for r in range(1, NDEV):
                px = 1 - x if r & 4 else x
                py = 1 - y if r & 2 else y
                pc = 1 - c if r & 1 else c
                cp = pltpu.make_async_remote_copy(
                    src_ref=ins[a].at[4 * px + 2 * py + pc] if a < ns else ins[a],
                    dst_ref=outs[a].at[me],
                    send_sem=send_sems.at[a, r - 1],
                    recv_sem=recv_sems.at[a, r - 1],
                    device_id=(px, py, pc),
                    device_id_type=pl.DeviceIdType.MESH,
                )
                cp.start()
                remote.append(cp)
        for cp in remote:
            cp.wait_recv()
        for cp in remote:
            cp.wait_send()
        for lc in local:
            lc.wait()

    out_shape = [jax.ShapeDtypeStruct(a.shape if i < ns else (NDEV,) + a.shape, a.dtype) for i, a in enumerate(arrs)]
    return pl.pallas_call(
        body,
        in_specs=[pl.BlockSpec(memory_space=pl.ANY)] * n,
        out_specs=[pl.BlockSpec(memory_space=pl.ANY)] * n,
        out_shape=out_shape,
        scratch_shapes=[
            pltpu.SemaphoreType.DMA((n, NDEV - 1)),
            pltpu.SemaphoreType.DMA((n, NDEV - 1)),
            pltpu.SemaphoreType.DMA((n,)),
        ],
        compiler_params=pltpu.CompilerParams(has_side_effects=True),
        name=name,
    )(*arrs)


_HBM = pl.BlockSpec(memory_space=pltpu.HBM)
_SEM = pl.BlockSpec(memory_space=pltpu.SEMAPHORE)
_EFFECT = pltpu.SideEffectType.DATAFLOW_SIDE_EFFECTING


def _comm_step(name, body_fn, lands, srcs=(), wait_sems=(), n_new=0, after=(), token=False):
    n, ns, nw, na = len(lands), len(srcs), len(wait_sems), len(after)

    def body(*refs):
        src, land = refs[:ns], refs[ns:ns + n]
        waits = refs[ns + n:ns + n + nw]
        new = refs[ns + n + nw + na:ns + n + nw + na + n_new]
        body_fn(src, land, waits, new)
        if token:
            refs[-1][...] = jnp.zeros((8, LANES), f32)

    hbm = [pltpu.HBM(a.shape, a.dtype) for a in lands]
    ops = [pltpu.with_memory_space_constraint(a, pltpu.HBM) for a in list(srcs) + list(lands)]
    extra_shape = [jax.ShapeDtypeStruct((8, LANES), f32)] if token else []
    extra_spec = [pl.BlockSpec(memory_space=pltpu.VMEM)] if token else []
    outs = pl.pallas_call(
        body,
        out_shape=tuple([pltpu.SemaphoreType.DMA(())] * n_new + hbm + extra_shape),
        in_specs=[_HBM] * (ns + n) + [_SEM] * nw + [pl.BlockSpec(memory_space=pl.ANY)] * na,
        out_specs=tuple([_SEM] * n_new + [_HBM] * n + extra_spec),
        input_output_aliases={ns + i: n_new + i for i in range(n)},
        compiler_params=pltpu.CompilerParams(has_side_effects=_EFFECT),
        name=name,
    )(*ops, *wait_sems, *after)
    if token:
        return list(outs[:n_new]), list(outs[n_new:n_new + n]), outs[-1][0, 0]
    return list(outs[:n_new]), list(outs[n_new:])


class _GatheredWeights:
    def __init__(self, shards):
        self.n = n = len(shards)
        x, y, c = lax.axis_index("x"), lax.axis_index("y"), lax.axis_index("c")
        self.x = x
        me = 4 * x + 2 * y + c
        lands = [lax.dynamic_update_slice(lax.empty((NDEV,) + s.shape, s.dtype), s[None], (me,) + (0,) * s.ndim)
                 for s in shards]

        def start_own(src, land, waits, new):
            p = self._peers()
            for a in range(n):
                for k, to in ((0, p["sibling"]), (1, p["xn"]), (2, p["yn"])):
                    self._copy(land[a], new, a, k, 3, p["me"], to).start()

        self.sems, self.lands = {}, None
        new, self.lands = _comm_step("gather_start", start_own, lands, n_new=6 * n)
        self._keep(new, (0, 1, 2))

    @staticmethod
    def _peers():
        x, y, c = lax.axis_index("x"), lax.axis_index("y"), lax.axis_index("c")
        return dict(
            me=(x, y, c), sibling=(x, y, 1 - c), xn=(1 - x, y, c), yn=(x, 1 - y, c), dg=(1 - x, 1 - y, c),
            relay_origin=(jnp.bitwise_xor(x, c), jnp.bitwise_xor(y, 1 - c), c),
            relay_target=(jnp.bitwise_xor(x, 1 - c), jnp.bitwise_xor(y, c), c))

    def _keep(self, new, ks):
        half = len(new) // 2
        i = 0
        for a in range(self.n):
            for k in ks:
                self.sems[a, k] = (new[i], new[half + i])
                i += 1

    @staticmethod
    def _copy(land, sem_refs, a, k, nk, block, to, src=None, ks=None):
        ks = tuple(range(nk)) if ks is None else ks
        half = len(sem_refs) // 2
        i = a * len(ks) + ks.index(k)
        slot = land.at[4 * block[0] + 2 * block[1] + block[2]]
        return pltpu.make_async_remote_copy(
            src_ref=slot if src is None else src, dst_ref=slot, send_sem=sem_refs[i], recv_sem=sem_refs[half + i],
            device_id=to, device_id_type=pl.DeviceIdType.MESH)

    def _sem_list(self, ks):
        return ([self.sems[a, k][0] for a in range(self.n) for k in ks]
                + [self.sems[a, k][1] for a in range(self.n) for k in ks])

    def first_half(self, after):
        n = self.n

        def relay(src, land, waits, new):
            p = self._peers()
            for a in range(n):
                self._copy(land[a], waits, a, 1, 0, p["xn"], p["me"], ks=(1, 2)).wait_recv()
                self._copy(land[a], waits, a, 2, 0, p["yn"], p["me"], ks=(1, 2)).wait_recv()
                self._copy(land[a], new, a, 3, 0, p["relay_origin"], p["relay_target"], ks=(3, 4, 5)).start()
                self._copy(land[a], new, a, 4, 0, p["xn"], p["sibling"], ks=(3, 4, 5)).start()
                self._copy(land[a], new, a, 5, 0, p["yn"], p["sibling"], ks=(3, 4, 5)).start()

        new, self.lands = _comm_step("gather_relay", relay, self.lands, wait_sems=self._sem_list((1, 2)),
                                     n_new=6 * n, after=after)
        self._keep(new, (3, 4, 5))

        def from_sibling(src, land, waits, new):
            p = self._peers()
            other = lambda b: (b[0], b[1], 1 - b[2])
            for a in range(n):
                self._copy(land[a], waits, a, 0, 0, other(p["me"]), p["me"], ks=(0, 4, 5)).wait_recv()
                self._copy(land[a], waits, a, 4, 0, other(p["xn"]), p["me"], ks=(0, 4, 5)).wait_recv()
                self._copy(land[a], waits, a, 5, 0, other(p["yn"]), p["me"], ks=(0, 4, 5)).wait_recv()

        _, self.lands = _comm_step("gather_wait_sibling", from_sibling, self.lands,
                                   wait_sems=self._sem_list((0, 4, 5)))
        return self.lands[0].reshape(NW, D), self.x.astype(jnp.int32).reshape(1)

    def second_half(self, after):
        n = self.n

        def forward_diagonal(src, land, waits, new):
            p = self._peers()
            for a in range(n):
                self._copy(land[a], waits, a, 3, 0, p["dg"], p["me"], ks=(3,)).wait_recv()
                self._copy(land[a], new, a, 6, 0, p["dg"], p["sibling"], ks=(6,)).start()

        new, self.lands = _comm_step("gather_forward_diagonal", forward_diagonal, self.lands,
                                     wait_sems=self._sem_list((3,)), n_new=2 * n, after=after)
        self._keep(new, (6,))

        def finish(src, land, waits, new):
            p = self._peers()
            ks = tuple(range(7))
            for a in range(n):
                self._copy(land[a], waits, a, 6, 0, (p["dg"][0], p["dg"][1], 1 - p["dg"][2]), p["me"], ks=ks).wait_recv()
                for k in ks:
                    self._copy(land[a], waits, a, k, 0, p["me"], p["me"], ks=ks).wait_send()

        _, self.lands = _comm_step("gather_finish", finish, self.lands, wait_sems=self._sem_list(tuple(range(7))))
        return self.lands[0].reshape(NW, D), (1 - self.x).astype(jnp.int32).reshape(1)

    def rest(self):
        g_a, g_b, g_o, g_bm = self.lands[1:]
        return (g_a.transpose(1, 0, 2).reshape(512, D), g_b.transpose(1, 0, 2).reshape(512, D),
                g_bm.transpose(1, 0, 2).reshape(2, D), g_o.reshape(D, D))


def _sibling_send_start(shares):
    landing = lax.empty(shares.shape, shares.dtype)

    def start(src, land, waits, new):
        x, y, c = lax.axis_index("x"), lax.axis_index("y"), lax.axis_index("c")
        pltpu.make_async_remote_copy(src_ref=land[0], dst_ref=land[1], send_sem=new[0], recv_sem=new[1],
                                     device_id=(x, y, 1 - c), device_id_type=pl.DeviceIdType.MESH).start()

    return _comm_step("grad_sibling_start", start, [shares, landing], n_new=2, token=True)


def _sibling_send_wait(sems, lands, after):
    def wait(src, land, waits, new):
        x, y, c = lax.axis_index("x"), lax.axis_index("y"), lax.axis_index("c")
        done = pltpu.make_async_remote_copy(src_ref=land[0], dst_ref=land[1], send_sem=waits[0], recv_sem=waits[1],
                                            device_id=(x, y, c), device_id_type=pl.DeviceIdType.MESH)
        done.wait_send()
        done.wait_recv()

    _, lands = _comm_step("grad_sibling_wait", wait, lands, wait_sems=sems, after=after)
    return lands[1]


def _row_tile(rows, limit=256):
    fits = [t for t in range(16, limit + 1, 16) if rows % t == 0]
    return fits[-1] if fits else rows


def _pair_sum(mine, theirs, name):
    nb, rows, cols = mine.shape
    tr = _row_tile(rows, 528)

    def body(a_ref, b_ref, o_ref):
        o_ref[...] = (a_ref[...].astype(f32) + b_ref[...].astype(f32)).astype(bf16)

    blk = pl.BlockSpec((None, tr, cols), lambda q, i: (q, i, 0))
    return pl.pallas_call(
        body,
        grid=(nb, rows // tr),
        in_specs=[blk, blk],
        out_specs=blk,
        out_shape=jax.ShapeDtypeStruct(mine.shape, bf16),
        compiler_params=_params(("arbitrary", "arbitrary")),
        name=name,
    )(mine, theirs)


def _scatter_start(chip_arrs, all_arrs, name):
    arrs = list(chip_arrs) + list(all_arrs)
    n, nc = len(arrs), len(chip_arrs)
    lands = [lax.empty(((3 if i < nc else NDEV - 1),) + a.shape[1:], a.dtype) for i, a in enumerate(arrs)]

    def body(*refs):
        src, land = refs[:n], refs[n:2 * n]
        send_sems, recv_sems = refs[2 * n:3 * n], refs[3 * n:4 * n]
        token = refs[6 * n]
        x, y, c = lax.axis_index("x"), lax.axis_index("y"), lax.axis_index("c")
        for a in range(n):
            for r in range(1, 4 if a < nc else NDEV):
                if a < nc:
                    px, py, pc = (1 - x if r & 2 else x), (1 - y if r & 1 else y), c
                    block = 2 * px + py
                else:
                    px, py, pc = (1 - x if r & 4 else x), (1 - y if r & 2 else y), (1 - c if r & 1 else c)
                    block = 4 * px + 2 * py + pc
                pltpu.make_async_remote_copy(
                    src_ref=src[a].at[block], dst_ref=land[a].at[r - 1], send_sem=send_sems[a],
                    recv_sem=recv_sems[a], device_id=(px, py, pc), device_id_type=pl.DeviceIdType.MESH).start()
        token[...] = jnp.zeros_like(token)

    hbm = [pltpu.HBM(a.shape, a.dtype) for a in arrs + lands]
    ops = [pltpu.with_memory_space_constraint(a, pltpu.HBM) for a in arrs + lands]
    outs = pl.pallas_call(
        body,
        out_shape=tuple([pltpu.SemaphoreType.DMA(())] * (2 * n) + hbm + [jax.ShapeDtypeStruct((8, LANES), f32)]),
        in_specs=[_HBM] * (2 * n),
        out_specs=tuple([_SEM] * (2 * n) + [_HBM] * (2 * n) + [pl.BlockSpec(memory_space=pltpu.VMEM)]),
        input_output_aliases={i: 2 * n + i for i in range(2 * n)},
        compiler_params=pltpu.CompilerParams(has_side_effects=_EFFECT),
        name=name,
    )(*ops)
    return outs[:n], outs[n:2 * n], outs[2 * n:3 * n], outs[3 * n:4 * n], outs[4 * n]


def _scatter_wait(send_sems, recv_sems, srcs, lands, after, name):
    n = len(srcs)

    def body(*refs):
        land = refs[n:2 * n]
        ssem, rsem = refs[2 * n:3 * n], refs[3 * n:4 * n]
        x, y, c = lax.axis_index("x"), lax.axis_index("y"), lax.axis_index("c")
        for a in range(n):
            done = pltpu.make_async_remote_copy(
                src_ref=land[a], dst_ref=land[a], send_sem=ssem[a], recv_sem=rsem[a], device_id=(x, y, c),
                device_id_type=pl.DeviceIdType.MESH)
            done.wait_send()
            done.wait_recv()

    hbm = [pltpu.HBM(a.shape, a.dtype) for a in list(srcs) + list(lands)]
    outs = pl.pallas_call(
        body,
        out_shape=tuple(hbm),
        in_specs=[_HBM] * (2 * n) + [_SEM] * (2 * n) + [pl.BlockSpec(memory_space=pl.ANY)],
        out_specs=tuple([_HBM] * (2 * n)),
        input_output_aliases={i: i for i in range(2 * n)},
        compiler_params=pltpu.CompilerParams(has_side_effects=_EFFECT),
        name=name,
    )(*srcs, *lands, *send_sems, *recv_sems, after)
    return outs[:n], outs[n:]


def _adam_update(g, w_ref, m_ref, v_ref, g_ref, d_ref, nm_ref, nv_ref):
    mm = ADAM_B1 * m_ref[...] + (1.0 - ADAM_B1) * g
    vv = ADAM_B2 * v_ref[...] + (1.0 - ADAM_B2) * (g * g)
    m_hat = mm / (1.0 - ADAM_B1 ** ADAM_STEP)
    v_hat = vv / (1.0 - ADAM_B2 ** ADAM_STEP)
    g_ref[...] = g
    d_ref[...] = -ADAM_LR * (m_hat / (jnp.sqrt(v_hat) + ADAM_EPS) + ADAM_WD * w_ref[...])
    nm_ref[...] = mm
    nv_ref[...] = vv


def _adamw_own(w, own, own_idx, slots, m, v, name):
    r, c = w.shape[-2:]
    tr = _row_tile(r, 384)
    k = slots.shape[0]

    def body(i_ref, w_ref, o_ref, s_ref, m_ref, v_ref, g_ref, d_ref, nm_ref, nv_ref):
        del i_ref
        g = o_ref[...].astype(f32)
        for j in range(k):
            g = g + s_ref[j].astype(f32)
        _adam_update(g, w_ref, m_ref, v_ref, g_ref, d_ref, nm_ref, nv_ref)

    blk = pl.BlockSpec((None, tr, c), lambda i, ix: (0, i, 0))
    return pl.pallas_call(
        body,
        grid_spec=pltpu.PrefetchScalarGridSpec(
            num_scalar_prefetch=1,
            grid=(r // tr,),
            in_specs=[blk, pl.BlockSpec((None, tr, c), lambda i, ix: (ix[0], i, 0)),
                      pl.BlockSpec((k, tr, c), lambda i, ix: (0, i, 0)), blk, blk],
            out_specs=[blk] * 4,
        ),
        out_shape=[jax.ShapeDtypeStruct(w.shape, f32)] * 4,
        compiler_params=_params(("arbitrary",)),
        name=name,
    )(own_idx, w, own, slots, m, v)


def _adamw(w, slots, m, v, name):
    r, c = w.shape[-2:]
    tr = _row_tile(r, 128)

    def body(w_ref, s_ref, m_ref, v_ref, g_ref, d_ref, nm_ref, nv_ref):
        g = s_ref[0].astype(f32)
        for k in range(1, NDEV):
            g = g + s_ref[k].astype(f32)
        _adam_update(g, w_ref, m_ref, v_ref, g_ref, d_ref, nm_ref, nv_ref)

    if w.ndim == 3:
        blk = pl.BlockSpec((None, tr, c), lambda i: (0, i, 0))
    else:
        blk = pl.BlockSpec((tr, c), lambda i: (i, 0))
    return pl.pallas_call(
        body,
        grid=(r // tr,),
        in_specs=[blk, pl.BlockSpec((NDEV, tr, c), lambda i: (0, i, 0)), blk, blk],
        out_specs=[blk] * 4,
        out_shape=[jax.ShapeDtypeStruct(w.shape, f32)] * 4,
        compiler_params=_params(("arbitrary",)),
        name=name,
    )(w, slots, m, v)


class _Weights:
    def __init__(self, w_t, w_a, w_b, b_merge, w_o):
        self._w_t, self._rest = w_t, (w_a, w_b, b_merge, w_o)

    def first_half(self, after):
        del after
        return self._w_t, jnp.zeros((1,), jnp.int32)

    def second_half(self, after):
        del after
        return self._w_t, jnp.ones((1,), jnp.int32)

    def rest(self):
        return self._rest


def _local_step(x, tgt, norm_gain, weights, qn_a, kn_a, qn_b, kn_b, sink_a, rel_bias, on_weight_grads=None,
                core=None):
    two = lambda t: jnp.concatenate([t, t], axis=-1).reshape(1, LANES)
    ones = jnp.ones((1, LANES), f32)
    gains = jnp.stack([
        jnp.stack([two(qn_a), two(kn_a), ones]),
        jnp.stack([two(qn_b), two(kn_b), ones]),
        jnp.stack([two(qn_b), two(kn_b), ones]),
        jnp.stack([two(qn_b), two(kn_b), ones]),
    ])
    buckets = [jnp.asarray(_bucket_np(blk, d)) for blk, d, _ in GROUPS]
    bias = [_bias_expand(rel_bias, buckets[k], GROUPS[k][2], "bias_expand_%d" % k) for k in range(4)]

    hb, hbt, rstd = _rms(x, norm_gain)
    w_t, half = weights.first_half([hb] + bias)
    proj = _inproj_half(hb, w_t, half, None, "inproj_1")
    w_t, half = weights.second_half([proj])
    proj = _inproj_half(hb, w_t, half, proj, "inproj_2")
    w_a, w_b, b_merge, w_o = weights.rest()
    gl = _prep(proj, gains)
    o_a, l_a = _attn_fwd(gl, bias[0], sink_a.reshape(8), 0, 128, 1, "attn_fwd_a")
    fwd_b = [_attn_fwd(gl, bias[k], None, k, GROUPS[k][0], GROUPS[k][1], "attn_fwd_b%d" % k) for k in (1, 2, 3)]
    sink_b = jnp.repeat(sink_a.reshape(8), HD).reshape(1, 512)

    (dy, dyb, dproj, do_a, dd_a, do_b0, do_b1, do_b2, dd_b0, dd_b1, dd_b2, ya, yb, mg, dbr_a, dbr_b, loss, dbm,
     dsk) = _tail(x, tgt, o_a, l_a, [f[0] for f in fwd_b], [f[1] for f in fwd_b], proj, b_merge, w_a, w_b, w_o, sink_b)

    dw_o = _matmul_tokens(mg, dyb, "dw_out")
    dw_a = _matmul_tokens(ya, dbr_a, "dw_branch_a")
    dw_b = _matmul_tokens(yb, dbr_b, "dw_branch_b")
    if on_weight_grads is not None:
        early = on_weight_grads(dict(w_branch_a=dw_a, w_branch_b=dw_b, b_merge=dbm, w_out=dw_o))
        buckets = [buckets[0] + early.astype(jnp.int32)] + buckets[1:]

    dqkv_a, dbk_a = _attn_bwd(gl, bias[0], buckets[0], do_a, l_a, dd_a, 0, 128, 1, "attn_bwd_a")
    dproj, dg_a = _post_a(dqkv_a, proj, gains, dproj)
    dbk_b, dg_b = [], []
    for k, do_k, dd_k in ((1, do_b0, dd_b0), (2, do_b1, dd_b1), (3, do_b2, dd_b2)):
        dqkv, dbk = _attn_bwd(gl, bias[k], buckets[k], do_k, fwd_b[k - 1][1], dd_k, k, GROUPS[k][0], GROUPS[k][1],
                              "attn_bwd_b%d" % k)
        dproj, dg = _post_b(k, dqkv, proj, gains, dproj)
        dbk_b.append(dbk)
        dg_b.append(dg)
    dg_b = jnp.stack(dg_b)

    core = jnp.zeros((1,), jnp.int32) if core is None else core
    dw_other = _dw_in(hbt, dproj, 1 - core, "dw_in_other")
    sent = jnp.zeros((), f32) if on_weight_grads is None else on_weight_grads(dict(w_in_other=dw_other))
    dw_in = _dw_in(hbt, dproj, core + sent.astype(jnp.int32), "dw_in_own")
    token = jnp.zeros((), f32) if on_weight_grads is None else on_weight_grads(dict(w_in=dw_in))
    grad_x, d_norm_gain = _dh_norm_bwd(dproj, w_t, x, rstd, norm_gain + token, dy)

    fold = lambda t: t[..., :HD] + t[..., HD:]
    d_qn_a = fold(dg_a[0, 0])
    d_kn_a = fold(dg_a[1, 0])
    d_qn_b = fold(dg_b[:, 0, 0].sum(axis=0))
    d_kn_b = fold(dg_b[:, 1, 0].sum(axis=0))
    d_sink = dsk.reshape(8, HD)[:, 0]
    red = jnp.stack([dbk_a] + dbk_b)
    d_rel = red[:, :, 0, :32].reshape(32, 32).T
    return dict(loss=loss, grad_x=grad_x, norm_gain=d_norm_gain, w_in=dw_in, w_in_other=dw_other, q_norm_a=d_qn_a,
                k_norm_a=d_kn_a,
                q_norm_b=d_qn_b, k_norm_b=d_kn_b, sink_a=d_sink, rel_bias=d_rel, w_branch_a=dw_a, w_branch_b=dw_b,
                b_merge=dbm, w_out=dw_o)


SMALL = (("norm_gain", D), ("q_norm_a", HD), ("k_norm_a", HD), ("q_norm_b", HD), ("k_norm_b", HD), ("sink_a", 8),
         ("rel_bias", 1024))
SMALL_PAD = 2432


SMALL_USED = sum(sz for _, sz in SMALL)


def _pack_small(parts, loss=None):
    tail = jnp.zeros((SMALL_PAD - SMALL_USED,), f32)
    if loss is not None:
        tail = tail.at[0].set(loss.reshape(()))
    return jnp.concatenate([parts[n].reshape(-1) for n, _ in SMALL] + [tail]).reshape(1, SMALL_PAD)


def _unpack_small(flat, shapes):
    out, off = {}, 0
    for n, sz in SMALL:
        out[n] = flat[0, off:off + sz].reshape(shapes[n])
        off += sz
    return out


def kernel(x, norm_gain, w_in, q_norm_a, k_norm_a, q_norm_b, k_norm_b, sink_a, rel_bias, w_branch_a, w_branch_b, b_merge, w_out, loss_target, m_norm_gain, m_w_in, m_q_norm_a, m_k_norm_a, m_q_norm_b, m_k_norm_b, m_sink_a, m_rel_bias, m_w_branch_a, m_w_branch_b, m_b_merge, m_w_out, v_norm_gain, v_w_in, v_q_norm_a, v_k_norm_a, v_q_norm_b, v_k_norm_b, v_sink_a, v_rel_bias, v_w_branch_a, v_w_branch_b, v_b_merge, v_w_out):
    csh = D // NDEV
    w_in_t, m_w_in_t, v_w_in_t = (jnp.swapaxes(t, 1, 2) for t in (w_in, m_w_in, v_w_in))
    weights = _GatheredWeights([w_in_t[0].astype(bf16), w_branch_a[0].astype(bf16), w_branch_b[0].astype(bf16),
                                w_out[0].astype(bf16), b_merge[0]])

    pending = {}
    core = lax.axis_index("c").astype(jnp.int32).reshape(1)
    chip = (2 * lax.axis_index("x") + lax.axis_index("y")).astype(jnp.int32).reshape(1)
    me = (2 * chip + core).astype(jnp.int32)

    def start_exchange(gw):
        if "w_in_other" in gw:
            sems, lands, sent = _sibling_send_start(gw["w_in_other"])
            pending["sibling"] = (sems, lands)
            return sent
        if "w_in" in gw:
            from_sibling = _sibling_send_wait(*pending["sibling"], after=[gw["w_in"]])
            chip_sums = _pair_sum(gw["w_in"], from_sibling, "grad_pair_sum")
            pending["w_in"] = _scatter_start([chip_sums], [], "scatter_w_in_start")
            return pending["w_in"][4][0, 0]
        blocks = [gw["w_branch_a"].reshape(512, NDEV, csh).transpose(1, 0, 2).astype(bf16),
                  gw["w_branch_b"].reshape(512, NDEV, csh).transpose(1, 0, 2).astype(bf16),
                  gw["w_out"].reshape(NDEV, csh, D).astype(bf16),
                  gw["b_merge"].reshape(2, NDEV, csh).transpose(1, 0, 2)]
        pending["rest"] = _scatter_start([], blocks, "scatter_rest_start")
        return pending["rest"][4][0, 0]

    loc = _local_step(x[0], loss_target[0], norm_gain, weights, q_norm_a, k_norm_a, q_norm_b, k_norm_b, sink_a,
                      rel_bias, on_weight_grads=start_exchange, core=core)

    small_shapes = dict(norm_gain=(1, D), q_norm_a=(1, HD), k_norm_a=(1, HD), q_norm_b=(1, HD), k_norm_b=(1, HD),
                        sink_a=(1, 8), rel_bias=(32, 32))
    (r_small,) = _exchange([], [_pack_small(loc, loc["loss"])], "gather_small_grads")
    send_sems, recv_sems, srcs, lands, _ = pending["rest"]
    (s_a, s_b, s_o, s_bm), (r_a, r_b, r_o, r_bm) = _scatter_wait(
        send_sems, recv_sems, srcs, lands, r_small, "scatter_rest_wait")
    send_sems, recv_sems, srcs, lands, _ = pending["w_in"]
    (s_in,), (r_in,) = _scatter_wait(send_sems, recv_sems, srcs, lands, r_small, "scatter_w_in_wait")

    given = dict(norm_gain=norm_gain, q_norm_a=q_norm_a, k_norm_a=k_norm_a, q_norm_b=q_norm_b, k_norm_b=k_norm_b,
                 sink_a=sink_a, rel_bias=rel_bias)
    m_small = dict(norm_gain=m_norm_gain, q_norm_a=m_q_norm_a, k_norm_a=m_k_norm_a, q_norm_b=m_q_norm_b,
                   k_norm_b=m_k_norm_b, sink_a=m_sink_a, rel_bias=m_rel_bias)
    v_small = dict(norm_gain=v_norm_gain, q_norm_a=v_q_norm_a, k_norm_a=v_k_norm_a, q_norm_b=v_q_norm_b,
                   k_norm_b=v_k_norm_b, sink_a=v_sink_a, rel_bias=v_rel_bias)
    res = {
        "small": _adamw(_pack_small(given), r_small, _pack_small(m_small), _pack_small(v_small), "adamw_small"),
        "w_in": [jnp.swapaxes(t, 1, 2) for t in
                 _adamw_own(w_in_t, s_in, chip, r_in, m_w_in_t, v_w_in_t, "adamw_w_in")],
        "w_branch_a": _adamw_own(w_branch_a, s_a, me, r_a, m_w_branch_a, v_w_branch_a, "adamw_w_branch_a"),
        "w_branch_b": _adamw_own(w_branch_b, s_b, me, r_b, m_w_branch_b, v_w_branch_b, "adamw_w_branch_b"),
        "b_merge": _adamw_own(b_merge, s_bm, me, r_bm, m_b_merge, v_b_merge, "adamw_b_merge"),
        "w_out": _adamw_own(w_out, s_o, me, r_o, m_w_out, v_w_out, "adamw_w_out"),
    }
    order = ["norm_gain", "w_in", "q_norm_a", "k_norm_a", "q_norm_b", "k_norm_b", "sink_a", "rel_bias", "w_branch_a",
             "w_branch_b", "b_merge", "w_out"]
    outs = []
    for k in range(4):
        small = _unpack_small(res["small"][k], small_shapes)
        for n in order:
            outs.append(small[n] if n in small else res[n][k])
    loss = res["small"][0][0, SMALL_USED]
    return (loss, loc["grad_x"][None], *outs)
```

```python
import math

import numpy as np
import jax
import jax.numpy as jnp
from jax import lax
from jax.experimental import pallas as pl
from jax.experimental.pallas import tpu as pltpu

f32 = jnp.float32
bf16 = jnp.bfloat16

S = 4096
D = 1024
NA = 5376
NT = 3072
NW = NA + NT
WSH = NW // 8
HD = 64
LANES = 128
EPS = 1e-6
NEG = -1e30
SCALE = HD ** -0.5
TQ = 128
PAD = 128
SP = S + 2 * PAD
NDEV = 8
GROUPS = ((128, 1, 0), (64, 1, 8), (64, 4, 16), (64, 16, 24))
CHUNK = 256
PCHUNK = 128
RC = 64

ADAM_LR, ADAM_B1, ADAM_B2, ADAM_EPS, ADAM_WD, ADAM_STEP = 0.001, 0.9, 0.999, 1e-08, 0.01, 10

MIB = 1024 * 1024
NT_DIMS = (((1,), (1,)), ((), ()))
TN_DIMS = (((0,), (0,)), ((), ()))


def _params(sem=None, vmem_mib=48):
    return pltpu.CompilerParams(dimension_semantics=sem, vmem_limit_bytes=vmem_mib * MIB)


def _lo():
    return lax.broadcasted_iota(jnp.int32, (1, LANES), 1) < HD


def _head_ones():
    r = lax.broadcasted_iota(jnp.int32, (LANES, LANES), 0) // HD
    c = lax.broadcasted_iota(jnp.int32, (LANES, LANES), 1) // HD
    return jnp.where(r == c, 1.0, 0.0).astype(bf16)


def _half_sums(x, ones):
    hi = x.astype(bf16)
    mid = (x - hi.astype(f32)).astype(bf16)
    return (jnp.dot(hi, ones, preferred_element_type=f32) + jnp.dot(mid, ones, preferred_element_type=f32))


def _seg_sum(x, ones):
    outs = [_half_sums(x[:, b * LANES:(b + 1) * LANES], ones) for b in range(x.shape[1] // LANES)]
    return outs[0] if len(outs) == 1 else jnp.concatenate(outs, axis=1)


def _bucket_np(blk, stride):
    w = TQ + 2 * blk
    rel = np.arange(w)[None, :] - blk - np.arange(TQ)[:, None]
    band = np.abs(rel) <= blk
    r = rel * stride
    n = np.abs(r)
    nf = np.maximum(n, 8).astype(np.float32)
    large = 8 + (np.log(nf / np.float32(8)) / np.float32(math.log(128.0)) * np.float32(8)).astype(np.int32)
    large = np.minimum(large, 15)
    b = (r > 0).astype(np.int32) * 16 + np.where(n < 8, n, large)
    return np.where(band, b, -1).astype(np.int32)


def _rms(x, gain):
    ts = 512

    def body(x_ref, g_ref, h_ref, ht_ref, r_ref):
        xv = x_ref[...]
        r = lax.rsqrt(jnp.mean(xv * xv, axis=-1, keepdims=True) + EPS)
        h = (xv * r) * g_ref[...]
        h_ref[...] = h.astype(bf16)
        ht_ref[...] = h.T.astype(bf16)
        r_ref[...] = r

    return pl.pallas_call(
        body,
        grid=(S // ts,),
        in_specs=[pl.BlockSpec((ts, D), lambda i: (i, 0)), pl.BlockSpec((1, D), lambda i: (0, 0))],
        out_specs=[pl.BlockSpec((ts, D), lambda i: (i, 0)), pl.BlockSpec((D, ts), lambda i: (0, i)),
                   pl.BlockSpec((ts, 1), lambda i: (i, 0))],
        out_shape=[jax.ShapeDtypeStruct((S, D), bf16), jax.ShapeDtypeStruct((D, S), bf16),
                   jax.ShapeDtypeStruct((S, 1), f32)],
        compiler_params=_params(("arbitrary",)),
        name="rms",
    )(x, gain)


def _inproj_half(hb, w_t, half, proj, name):
    ts = 512
    tn = NW // 2
    per = NW // 2 // tn

    def body(h_idx, h_ref, w_ref, *rest):
        del h_idx
        rest[-1][...] = lax.dot_general(h_ref[...], w_ref[...], NT_DIMS, preferred_element_type=f32)

    in_specs = [pl.BlockSpec((ts, D), lambda i, n, hf: (i, 0)),
                pl.BlockSpec((tn, D), lambda i, n, hf: (hf[0] * per + n, 0))]
    args = [half, hb, w_t]
    aliases = {}
    if proj is not None:
        in_specs.append(pl.BlockSpec(memory_space=pl.ANY))
        args.append(proj)
        aliases = {3: 0}
    return pl.pallas_call(
        body,
        grid_spec=pltpu.PrefetchScalarGridSpec(
            num_scalar_prefetch=1,
            grid=(S // ts, per),
            in_specs=in_specs,
            out_specs=pl.BlockSpec((ts, tn), lambda i, n, hf: (i, hf[0] * per + n)),
        ),
        out_shape=jax.ShapeDtypeStruct((S, NW), f32),
        input_output_aliases=aliases,
        compiler_params=_params(("arbitrary", "arbitrary")),
        name=name,
    )(*args)


def _bias_expand(table, bucket, c0, name):
    tq, w = bucket.shape
    blk = (w - tq) // 2

    def body(tab_ref, bk_ref, o_ref):
        h = pl.program_id(0)
        bk = bk_ref[...]

        def step(b, acc):
            return jnp.where(bk == b, tab_ref[b, c0 + h], acc)

        inner = lax.fori_loop(0, 32, step, jnp.full((tq, w), NEG, f32))
        col = lax.broadcasted_iota(jnp.int32, (1, w), 1)
        o_ref[0] = jnp.where(col < blk, NEG, inner)
        o_ref[1] = inner
        o_ref[2] = jnp.where(col >= tq + blk, NEG, inner)

    return pl.pallas_call(
        body,
        grid=(8,),
        in_specs=[pl.BlockSpec(memory_space=pltpu.SMEM), pl.BlockSpec((tq, w), lambda h: (0, 0))],
        out_specs=pl.BlockSpec((3, None, tq, w), lambda h: (0, h, 0, 0)),
        out_shape=jax.ShapeDtypeStruct((3, 8, tq, w), f32),
        compiler_params=_params(("arbitrary",)),
        name=name,
    )(table, bucket)


def _tile_kind(t, seq):
    m0 = jnp.bitwise_and(t * TQ, seq - 1)
    return jnp.where(m0 == 0, 0, jnp.where(m0 == seq - TQ, 2, 1))


def _col_block(g, j):
    kind = j // 4
    hp = j % 4
    a = jnp.where(kind == 0, hp, 3 + kind)
    b = 6 + 12 * kind + 4 * (g - 1) + hp
    return jnp.where(g == 0, a, b)


def _prep(proj_a, gains):
    def body(p0_ref, p1_ref, p2_ref, p3_ref, g_ref, o_ref):
        g = pl.program_id(0)
        kind = pl.program_id(1)
        lo = _lo()
        ones = _head_ones()
        half = jnp.where(lo, 0, 1)
        gain = g_ref[...]

        def norm_store(xv, u, dst, dup):
            if dup:
                take = (kind == 0) | (half == u // 2)
                xv = jnp.where(take, xv, pltpu.roll(xv, HD, 1))
            r = lax.rsqrt(_half_sums(xv * xv, ones) * (1.0 / HD) + EPS)
            r = jnp.where(kind == 2, 1.0, r)
            yv = (xv * r) * gain
            yv = jnp.where(kind == 0, yv * SCALE, yv)
            o_ref[u, PAD + dst:PAD + dst + CHUNK, :] = yv.astype(bf16)

        for u in range(4):
            o_ref[u, 0:PAD, :] = jnp.zeros((PAD, LANES), bf16)
            o_ref[u, PAD + S:SP, :] = jnp.zeros((PAD, LANES), bf16)
        for gi, (_, d, _) in enumerate(GROUPS):
            @pl.when(g == gi)
            def _():
                seq = S // d
                for u, p_ref in enumerate((p0_ref, p1_ref, p2_ref, p3_ref)):
                    for c in range(d):
                        for i in range(seq // CHUNK):
                            if d == 1:
                                xv = p_ref[i * CHUNK:(i + 1) * CHUNK, :]
                            else:
                                xv = p_ref[pl.ds(c + i * CHUNK * d, CHUNK, stride=d), :]
                            norm_store(xv, u, c * seq + i * CHUNK, gi == 0)

    return pl.pallas_call(
        body,
        grid=(4, 3),
        in_specs=[pl.BlockSpec((S, LANES), lambda g, kind, u=u: (0, _col_block(g, 4 * kind + u))) for u in range(4)] + [
            pl.BlockSpec((None, None, 1, LANES), lambda g, kind: (g, kind, 0, 0)),
        ],
        out_specs=pl.BlockSpec((None, 4, SP, LANES), lambda g, kind: (g, kind, 0, 0)),
        out_shape=jax.ShapeDtypeStruct((4, 12, SP, LANES), bf16),
        compiler_params=_params(("arbitrary", "arbitrary")),
        name="prep",
    )(proj_a, proj_a, proj_a, proj_a, gains)


def _token_rows(t, r0, n, d):
    if d == 1:
        return pl.ds(pl.multiple_of(t * TQ, TQ) + r0, n)
    per = S // d // TQ
    return pl.ds(((t % per) * TQ + r0) * d + t // per, n, stride=d)


def _stack_heads(t, lo):
    z = jnp.zeros_like(t)
    return jnp.concatenate([jnp.where(lo, t, z), jnp.where(lo, z, t)], axis=0)


def _unstack_heads(t2, lo):
    return jnp.where(lo, t2[:TQ], t2[TQ:])


def _attn_fwd(gl, bias, sink, g, blk, d, name):
    w = TQ + 2 * blk
    seq = S // d
    use_sink = sink is not None

    def body(*refs):
        if use_sink:
            sink_ref, q_ref, k_ref, v_ref, b_ref, o_ref, l_ref, s0, s1, p0, p1, lse_scr = refs
        else:
            q_ref, k_ref, v_ref, b_ref, o_ref, l_ref, s0, s1, p0, p1, lse_scr = refs
        hp = pl.program_id(0)
        lo = _lo()
        s_bufs, p_bufs = (s0, s1), (p0, p1)

        def scores(p, slot):
            for u in range(2):
                f0 = pl.multiple_of((2 * p + u) * TQ, TQ)
                q2 = _stack_heads(q_ref[pl.ds(PAD + f0, TQ), :], lo)
                kw = k_ref[pl.ds(PAD - blk + f0, w), :]
                s_bufs[slot][u] = lax.dot_general(q2, kw, NT_DIMS, preferred_element_type=f32)

        def softmax(p, slot):
            for u in range(2):
                t = 2 * p + u
                kind = _tile_kind(t, seq)
                for h in range(2):
                    for r in range(TQ // RC):
                        rows = slice(h * TQ + r * RC, h * TQ + (r + 1) * RC)
                        logit = s_bufs[slot][u, rows, :] + b_ref[kind, h, r * RC:(r + 1) * RC, :]
                        m = jnp.max(logit, axis=1, keepdims=True)
                        e = jnp.exp(logit - m)
                        lse = m + jnp.log(jnp.sum(e, axis=1, keepdims=True))
                        if use_sink:
                            sk = sink_ref[2 * hp + h]
                            mx = jnp.maximum(lse, sk)
                            lse = mx + jnp.log(jnp.exp(lse - mx) + jnp.exp(sk - mx))
                        p_bufs[slot][u, rows, :] = (e * jnp.exp(m - lse)).astype(bf16)
                        lse_scr[u, rows, :] = jnp.broadcast_to(lse, (RC, LANES))
                l_ref[_token_rows(t, 0, TQ, d), :] = jnp.where(lo, lse_scr[u, 0:TQ, :], lse_scr[u, TQ:2 * TQ, :])

        def values(p, slot):
            for u in range(2):
                t = 2 * p + u
                vw = v_ref[pl.ds(PAD - blk + pl.multiple_of(t * TQ, TQ), w), :]
                o2 = jnp.dot(p_bufs[slot][u], vw, preferred_element_type=f32)
                o_ref[_token_rows(t, 0, TQ, d), :] = _unstack_heads(o2, lo)

        npair = S // TQ // 2
        scores(0, 0)
        scores(1, 1)
        softmax(0, 0)

        def steady(k, carry):
            p = 2 * k + 2
            scores(p, 0)
            softmax(p - 1, 1)
            values(p - 2, 0)
            scores(p + 1, 1)
            softmax(p, 0)
            values(p - 1, 1)
            return carry

        lax.fori_loop(0, (npair - 2) // 2, steady, 0)
        softmax(npair - 1, 1)
        values(npair - 2, 0)
        values(npair - 1, 1)

    in_specs = [
        pl.BlockSpec((None, None, SP, LANES), lambda hp: (g, hp, 0, 0)),
        pl.BlockSpec((None, None, SP, LANES), lambda hp: (g, 4 + hp, 0, 0)),
        pl.BlockSpec((None, None, SP, LANES), lambda hp: (g, 8 + hp, 0, 0)),
        pl.BlockSpec((3, 2, TQ, w), lambda hp: (0, hp, 0, 0)),
    ]
    args = [gl, gl, gl, bias]
    if use_sink:
        in_specs = [pl.BlockSpec(memory_space=pltpu.SMEM)] + in_specs
        args = [sink] + args
    out = pl.BlockSpec((S, LANES), lambda hp: (0, hp))
    return pl.pallas_call(
        body,
        grid=(4,),
        in_specs=in_specs,
        out_specs=[out, out],
        out_shape=[jax.ShapeDtypeStruct((S, 4 * LANES), f32)] * 2,
        scratch_shapes=[pltpu.VMEM((2, 2 * TQ, w), f32), pltpu.VMEM((2, 2 * TQ, w), f32),
                        pltpu.VMEM((2, 2 * TQ, w), bf16), pltpu.VMEM((2, 2 * TQ, w), bf16),
                        pltpu.VMEM((2, 2 * TQ, LANES), f32)],
        compiler_params=_params(("arbitrary",)),
        name=name,
    )(*args)


def _attn_bwd(gl, bias, bucket, do, lse, dd, g, blk, d, name):
    w = TQ + 2 * blk
    seq = S // d

    def body(q_ref, k_ref, v_ref, b_ref, bk_ref, do_ref, l_ref, d_ref, dqkv_ref, dbk_ref,
             db_acc, s0, s1, dp0, dp1, pb0, pb1, ds0, ds1, dk_acc, dv_acc):
        lo = _lo()
        hi = jnp.logical_not(lo)
        dk_acc[...] = jnp.zeros((SP, LANES), f32)
        dv_acc[...] = jnp.zeros((SP, LANES), f32)
        db_acc[...] = jnp.zeros((2 * TQ, w), f32)
        s_bufs, dp_bufs, pb_bufs, ds_bufs = (s0, s1), (dp0, dp1), (pb0, pb1), (ds0, ds1)

        def stacked(t):
            f0 = pl.multiple_of(t * TQ, TQ)
            q2 = _stack_heads(q_ref[pl.ds(PAD + f0, TQ), :], lo)
            do2 = _stack_heads(do_ref[_token_rows(t, 0, TQ, d), :].astype(bf16), lo)
            return f0, q2, do2

        def scores(p, slot):
            for u in range(2):
                f0, q2, do2 = stacked(2 * p + u)
                win = pl.ds(PAD - blk + f0, w)
                s_bufs[slot][u] = lax.dot_general(q2, k_ref[win, :], NT_DIMS, preferred_element_type=f32)
                dp_bufs[slot][u] = lax.dot_general(do2, v_ref[win, :], NT_DIMS, preferred_element_type=f32)

        def grads(p, slot):
            for u in range(2):
                t = 2 * p + u
                kind = _tile_kind(t, seq)
                for h in range(2):
                    msk = lo if h == 0 else hi
                    for r in range(TQ // RC):
                        rows = slice(h * TQ + r * RC, h * TQ + (r + 1) * RC)
                        src = _token_rows(t, r * RC, RC, d)
                        lh = jnp.max(jnp.where(msk, l_ref[src, :], -jnp.inf), axis=1, keepdims=True)
                        dh = jnp.max(jnp.where(msk, d_ref[src, :], -jnp.inf), axis=1, keepdims=True)
                        logit = s_bufs[slot][u, rows, :] + b_ref[kind, h, r * RC:(r + 1) * RC, :]
                        pr = jnp.exp(logit - lh)
                        ds = pr * (dp_bufs[slot][u, rows, :] - dh)
                        db_acc[rows, :] += ds
                        pb_bufs[slot][u, rows, :] = pr.astype(bf16)
                        ds_bufs[slot][u, rows, :] = ds.astype(bf16)

        def accumulate(p, slot):
            for u in range(2):
                f0, q2, do2 = stacked(2 * p + u)
                win = pl.ds(PAD - blk + f0, w)
                dsb = ds_bufs[slot][u]
                dq2 = jnp.dot(dsb, k_ref[win, :], preferred_element_type=f32)
                dqkv_ref[0, pl.ds(PAD + f0, TQ), :] = _unstack_heads(dq2, lo).astype(bf16)
                dk_acc[win, :] += lax.dot_general(dsb, q2, TN_DIMS, preferred_element_type=f32)
                dv_acc[win, :] += lax.dot_general(pb_bufs[slot][u], do2, TN_DIMS, preferred_element_type=f32)

        npair = S // TQ // 2
        scores(0, 0)
        scores(1, 1)
        grads(0, 0)

        def steady(k, carry):
            p = 2 * k + 2
            scores(p, 0)
            grads(p - 1, 1)
            accumulate(p - 2, 0)
            scores(p + 1, 1)
            grads(p, 0)
            accumulate(p - 1, 1)
            return carry

        lax.fori_loop(0, (npair - 2) // 2, steady, 0)
        grads(npair - 1, 1)
        accumulate(npair - 2, 0)
        accumulate(npair - 1, 1)
        for i in range(SP // CHUNK):
            rows = slice(i * CHUNK, (i + 1) * CHUNK)
            dqkv_ref[1, rows, :] = dk_acc[rows, :].astype(bf16)
            dqkv_ref[2, rows, :] = dv_acc[rows, :].astype(bf16)

        bk = bk_ref[...]
        lane = lax.broadcasted_iota(jnp.int32, (8, LANES), 1)
        for h in range(2):
            db = db_acc[h * TQ:(h + 1) * TQ, :]
            acc = jnp.zeros((8, LANES), f32)
            for b in range(32):
                part = jnp.where(bk == b, db, 0.0).reshape(TQ // 8, 8, w).sum(axis=0)
                tot = jnp.sum(jnp.sum(part, axis=1, keepdims=True), axis=0, keepdims=True)
                acc = jnp.where(lane == b, tot, acc)
            dbk_ref[h] = acc

    def gcol(off):
        return pl.BlockSpec((None, None, SP, LANES), lambda hp: (g, off + hp, 0, 0))

    row = pl.BlockSpec((S, LANES), lambda hp: (0, hp))
    return pl.pallas_call(
        body,
        grid=(4,),
        in_specs=[gcol(0), gcol(4), gcol(8), pl.BlockSpec((3, 2, TQ, w), lambda hp: (0, hp, 0, 0)),
                  pl.BlockSpec((TQ, w), lambda hp: (0, 0)), row, row, row],
        out_specs=[pl.BlockSpec((3, None, SP, LANES), lambda hp: (0, hp, 0, 0)),
                   pl.BlockSpec((2, 8, LANES), lambda hp: (hp, 0, 0))],
        out_shape=[
            jax.ShapeDtypeStruct((3, 4, SP, LANES), bf16),
            jax.ShapeDtypeStruct((8, 8, LANES), f32),
        ],
        scratch_shapes=([pltpu.VMEM((2 * TQ, w), f32)] + [pltpu.VMEM((2, 2 * TQ, w), f32)] * 4
                        + [pltpu.VMEM((2, 2 * TQ, w), bf16)] * 4 + [pltpu.VMEM((SP, LANES), f32)] * 2),
        compiler_params=_params(("arbitrary",), vmem_mib=56),
        name=name,
    )(gl, gl, gl, bias, bucket, do, lse, dd)


def _sigmoid(z):
    return 1.0 / (1.0 + jnp.exp(-z))


def _tail(x, tgt, o_a, l_a, o_b, l_b, proj, bm, w_a, w_b, w_o, sink_b):
    ts = 256

    def body(x_ref, t_ref, oa_ref, la_ref, ob0_ref, ob1_ref, ob2_ref, lb0_ref, lb1_ref, lb2_ref,
             ga_ref, gb_ref, m0_ref, m1_ref, bm_ref, wa_ref, wb_ref, wo_ref, sk_ref,
             dy_ref, dyb_ref, dt_ref, doa_ref, dda_ref, dob0_ref, dob1_ref, dob2_ref, ddb0_ref, ddb1_ref, ddb2_ref,
             ya_ref, yb_ref, mg_ref, dbra_ref, dbrb_ref, loss_ref, dbm_ref, dsk_ref):
        i = pl.program_id(0)

        @pl.when(i == 0)
        def _():
            loss_ref[...] = jnp.zeros_like(loss_ref)
            dbm_ref[...] = jnp.zeros_like(dbm_ref)
            dsk_ref[...] = jnp.zeros_like(dsk_ref)

        ga = ga_ref[...]
        sa = _sigmoid(ga)
        silu_a = ga * sa
        oa = oa_ref[...]
        ya = oa * silu_a
        gb = gb_ref[...]
        sb = _sigmoid(gb)
        silu_b = gb * sb
        ob = [ob0_ref[...], ob1_ref[...], ob2_ref[...]]
        lb = [lb0_ref[...], lb1_ref[...], lb2_ref[...]]
        mx = jnp.maximum(jnp.maximum(lb[0], lb[1]), lb[2])
        ex = [jnp.exp(v - mx) for v in lb]
        den = ex[0] + ex[1] + ex[2]
        alpha = [e / den for e in ex]
        ybc = alpha[0] * ob[0] + alpha[1] * ob[1] + alpha[2] * ob[2]
        yb = ybc * silu_b
        yab = ya.astype(bf16)
        ybb = yb.astype(bf16)
        br_a = jnp.dot(yab, wa_ref[...], preferred_element_type=f32)
        br_b = jnp.dot(ybb, wb_ref[...], preferred_element_type=f32)
        g0 = _sigmoid(m0_ref[...] + bm_ref[0:1, :])
        g1 = _sigmoid(m1_ref[...] + bm_ref[1:2, :])
        merged = g0 * br_a + g1 * br_b
        mgb = merged.astype(bf16)
        y = x_ref[...] + jnp.dot(mgb, wo_ref[...], preferred_element_type=f32)
        err = y - t_ref[...]
        part = jnp.sum(jnp.sum(err * err, axis=1, keepdims=True), axis=0, keepdims=True)
        loss_ref[...] += part * (0.5 / D)
        dy = err * (1.0 / D)
        dyb = dy.astype(bf16)
        dmerged = lax.dot_general(dyb, wo_ref[...], NT_DIMS, preferred_element_type=f32)
        dbr_a = (dmerged * g0).astype(bf16)
        dbr_b = (dmerged * g1).astype(bf16)
        dm0 = dmerged * br_a * (g0 * (1.0 - g0))
        dm1 = dmerged * br_b * (g1 * (1.0 - g1))
        dbm_ref[0:1, :] += jnp.sum(dm0, axis=0, keepdims=True)
        dbm_ref[1:2, :] += jnp.sum(dm1, axis=0, keepdims=True)
        dya = lax.dot_general(dbr_a, wa_ref[...], NT_DIMS, preferred_element_type=f32)
        dyb2 = lax.dot_general(dbr_b, wb_ref[...], NT_DIMS, preferred_element_type=f32)
        do_a = dya * silu_a
        dga = dya * oa * (sa * (1.0 + ga * (1.0 - sa)))
        ones = _head_ones()
        delta_a = _seg_sum(do_a * oa, ones)
        dsk_ref[...] -= jnp.sum(delta_a * jnp.exp(sk_ref[...] - la_ref[...]), axis=0, keepdims=True)
        dybc = dyb2 * silu_b
        dgb = dyb2 * ybc * (sb * (1.0 + gb * (1.0 - sb)))
        dbar = _seg_sum(dybc * ybc, ones)
        dy_ref[...] = dy
        dyb_ref[...] = dyb
        dt_ref[:, 0:512] = dga.astype(bf16)
        dt_ref[:, 512:1024] = dgb.astype(bf16)
        dt_ref[:, 1024:2048] = dm0.astype(bf16)
        dt_ref[:, 2048:3072] = dm1.astype(bf16)
        doa_ref[...] = do_a.astype(bf16)
        dda_ref[...] = delta_a
        for k, (dob_ref, ddb_ref) in enumerate(((dob0_ref, ddb0_ref), (dob1_ref, ddb1_ref), (dob2_ref, ddb2_ref))):
            dob_ref[...] = alpha[k] * dybc
            ddb_ref[...] = alpha[k] * dbar
        ya_ref[...] = ya.T.astype(bf16)
        yb_ref[...] = yb.T.astype(bf16)
        mg_ref[...] = merged.T.astype(bf16)
        dbra_ref[...] = dbr_a
        dbrb_ref[...] = dbr_b

    def rows(n, blk=0):
        return pl.BlockSpec((ts, n), lambda i: (i, blk))

    def whole(r, c):
        return pl.BlockSpec((r, c), lambda i: (0, 0))

    def cols(n):
        return pl.BlockSpec((n, ts), lambda i: (0, i))

    def gate_cols(n, col):
        return pl.BlockSpec((pl.Element(ts), pl.Element(n)), lambda i: (i * ts, NA + col))

    outs = [
        ((S, D), f32, rows(D)), ((S, D), bf16, rows(D)), ((S, NW), bf16, gate_cols(NT, 0)),
        ((S, 512), bf16, rows(512)), ((S, 512), f32, rows(512)),
        ((S, 512), f32, rows(512)), ((S, 512), f32, rows(512)), ((S, 512), f32, rows(512)),
        ((S, 512), f32, rows(512)), ((S, 512), f32, rows(512)), ((S, 512), f32, rows(512)),
        ((512, S), bf16, cols(512)), ((512, S), bf16, cols(512)), ((D, S), bf16, cols(D)),
        ((S, D), bf16, rows(D)), ((S, D), bf16, rows(D)),
        ((1, 1), f32, whole(1, 1)), ((2, D), f32, whole(2, D)), ((1, 512), f32, whole(1, 512)),
    ]
    return pl.pallas_call(
        body,
        grid=(S // ts,),
        in_specs=[
            rows(D), rows(D), rows(512), rows(512), rows(512), rows(512), rows(512), rows(512), rows(512), rows(512),
            gate_cols(512, 0), gate_cols(512, 512), gate_cols(D, 1024), gate_cols(D, 2048), whole(2, D),
            whole(512, D), whole(512, D), whole(D, D), whole(1, 512),
        ],
        out_specs=[o[2] for o in outs],
        out_shape=[jax.ShapeDtypeStruct(o[0], o[1]) for o in outs],
        compiler_params=_params(("arbitrary",), vmem_mib=60),
        name="tail",
    )(x, tgt, o_a, l_a, *o_b, *l_b, proj, proj, proj, proj, bm, w_a, w_b, w_o, sink_b)


def _norm_bwd(xv, dyv, gain, ones):
    r = lax.rsqrt(_half_sums(xv * xv, ones) * (1.0 / HD) + EPS)
    yv = xv * r
    u = dyv * gain
    dxv = r * (u - yv * (_half_sums(u * yv, ones) * (1.0 / HD)))
    return dxv, jnp.sum(dyv * yv, axis=0, keepdims=True)


def _post_b(g, dqkv, proj_a, gains, dproj):
    d = GROUPS[g][1]
    seq = S // d

    def body(d_ref, pa_ref, pb_ref, g_ref, alias_ref, o_ref, dg_ref, nat_a, nat_b):
        del alias_ref
        pj = pl.program_id(0)
        kind = pj // 2
        q_scale = jnp.where(kind == 0, SCALE, 1.0)
        gain = g_ref[...] * q_scale
        ones = _head_ones()

        @pl.when(pj % 2 == 0)
        def _():
            dg_ref[...] = jnp.zeros_like(dg_ref)

        def columns(with_norm):
            for u, (p_ref, nat) in enumerate(((pa_ref, nat_a), (pb_ref, nat_b))):
                for c in range(d):
                    for i in range(seq // PCHUNK):
                        src = c * seq + i * PCHUNK
                        if d == 1:
                            idx = slice(src, src + PCHUNK)
                        else:
                            idx = pl.ds(c + i * PCHUNK * d, PCHUNK, stride=d)
                        dyv = d_ref[u, PAD + src:PAD + src + PCHUNK, :].astype(f32)
                        if with_norm:
                            dyv, dg = _norm_bwd(p_ref[idx, :], dyv, gain, ones)
                            dg_ref[...] += dg * q_scale
                        nat[idx, :] = dyv
                for i in range(S // CHUNK):
                    rows = slice(i * CHUNK, (i + 1) * CHUNK)
                    o_ref[rows, u * LANES:(u + 1) * LANES] = nat[rows, :].astype(bf16)

        pl.when(kind < 2)(lambda: columns(True))
        pl.when(kind == 2)(lambda: columns(False))

    def pcol(u):
        return pl.BlockSpec((S, LANES), lambda pj: (0, _col_block(g, 2 * jnp.minimum(pj, 3) + u)))

    return pl.pallas_call(
        body,
        grid=(6,),
        in_specs=[
            pl.BlockSpec((None, 2, SP, LANES), lambda pj: (pj // 2, pj % 2, 0, 0)),
            pcol(0), pcol(1),
            pl.BlockSpec((None, None, 1, LANES), lambda pj: (g, pj // 2, 0, 0)),
            pl.BlockSpec(memory_space=pl.ANY),
        ],
        out_specs=[
            pl.BlockSpec((S, 2 * LANES), lambda pj: (0, _col_block(g, 2 * pj) // 2)),
            pl.BlockSpec((None, 1, LANES), lambda pj: (pj // 2, 0, 0)),
        ],
        out_shape=[jax.ShapeDtypeStruct((S, NW), bf16), jax.ShapeDtypeStruct((3, 1, LANES), f32)],
        scratch_shapes=[pltpu.VMEM((S, LANES), f32), pltpu.VMEM((S, LANES), f32)],
        input_output_aliases={4: 0},
        compiler_params=_params(("arbitrary",)),
        name="post_b%d" % g,
    )(dqkv, proj_a, proj_a, gains, dproj)


def _post_a(dqkv, proj_a, gains, dproj):
    def body(q_ref, e_ref, p_ref, g_ref, alias_ref, o_ref, dg_ref):
        del alias_ref
        j = pl.program_id(0)
        q_scale = jnp.where(j < 4, SCALE, 1.0)
        gain = g_ref[...] * q_scale
        lo = _lo()
        ones = _head_ones()

        @pl.when((j == 0) | (j >= 4))
        def _():
            dg_ref[...] = jnp.zeros_like(dg_ref)

        def column(folded, with_norm):
            for i in range(S // PCHUNK):
                r0 = i * PCHUNK
                rows = slice(PAD + r0, PAD + r0 + PCHUNK)
                if folded:
                    t0 = e_ref[0, rows, :].astype(f32) + e_ref[1, rows, :].astype(f32)
                    t1 = e_ref[2, rows, :].astype(f32) + e_ref[3, rows, :].astype(f32)
                    dyv = jnp.where(lo, t0 + pltpu.roll(t0, HD, 1), t1 + pltpu.roll(t1, HD, 1))
                else:
                    dyv = q_ref[rows, :].astype(f32)
                if with_norm:
                    dyv, dg = _norm_bwd(p_ref[r0:r0 + PCHUNK, :], dyv, gain, ones)
                    dg_ref[...] += dg * q_scale
                o_ref[r0:r0 + PCHUNK, :] = dyv.astype(bf16)

        pl.when(j < 4)(lambda: column(False, True))
        pl.when(j == 4)(lambda: column(True, True))
        pl.when(j == 5)(lambda: column(True, False))

    return pl.pallas_call(
        body,
        grid=(6,),
        in_specs=[
            pl.BlockSpec((None, None, SP, LANES), lambda j: (0, jnp.minimum(j, 3), 0, 0)),
            pl.BlockSpec((None, 4, SP, LANES), lambda j: (jnp.clip(j - 3, 1, 2), 0, 0, 0)),
            pl.BlockSpec((S, LANES), lambda j: (0, jnp.minimum(j, 4))),
            pl.BlockSpec((None, None, 1, LANES), lambda j: (0, jnp.maximum(j - 3, 0), 0, 0)),
            pl.BlockSpec(memory_space=pl.ANY),
        ],
        out_specs=[
            pl.BlockSpec((S, LANES), lambda j: (0, j)),
            pl.BlockSpec((None, 1, LANES), lambda j: (jnp.maximum(j - 3, 0), 0, 0)),
        ],
        out_shape=[jax.ShapeDtypeStruct((S, NW), bf16), jax.ShapeDtypeStruct((3, 1, LANES), f32)],
        input_output_aliases={4: 0},
        compiler_params=_params(("arbitrary",)),
        name="post_a",
    )(dqkv, dqkv, proj_a, gains, dproj)


def _dh_norm_bwd(dproj, w, x, rstd, gain, dy):
    ts = 1024
    tk = NW // 6
    nk = NW // tk

    def body(d_ref, w_ref, x_ref, r_ref, g_ref, dy_ref, gx_ref, dgn_ref, acc):
        i = pl.program_id(0)
        k = pl.program_id(1)

        @pl.when((i == 0) & (k == 0))
        def _():
            dgn_ref[...] = jnp.zeros_like(dgn_ref)

        @pl.when(k == 0)
        def _():
            acc[...] = jnp.zeros_like(acc)

        acc[...] += jnp.dot(d_ref[...], w_ref[...], preferred_element_type=f32)

        @pl.when(k == nk - 1)
        def _():
            dh = acc[...]
            xh = x_ref[...] * r_ref[...]
            u = dh * g_ref[...]
            dx = r_ref[...] * (u - xh * jnp.mean(u * xh, axis=-1, keepdims=True))
            gx_ref[...] = dy_ref[...] + dx
            dgn_ref[...] += jnp.sum(dh * xh, axis=0, keepdims=True)

    return pl.pallas_call(
        body,
        grid=(S // ts, nk),
        in_specs=[
            pl.BlockSpec((ts, tk), lambda i, k: (i, k)),
            pl.BlockSpec((tk, D), lambda i, k: (k, 0)),
            pl.BlockSpec((ts, D), lambda i, k: (i, 0)),
            pl.BlockSpec((ts, 1), lambda i, k: (i, 0)),
            pl.BlockSpec((1, D), lambda i, k: (0, 0)),
            pl.BlockSpec((ts, D), lambda i, k: (i, 0)),
        ],
        out_specs=[pl.BlockSpec((ts, D), lambda i, k: (i, 0)), pl.BlockSpec((1, D), lambda i, k: (0, 0))],
        out_shape=[jax.ShapeDtypeStruct((S, D), f32), jax.ShapeDtypeStruct((1, D), f32)],
        scratch_shapes=[pltpu.VMEM((ts, D), f32)],
        compiler_params=_params(("arbitrary", "arbitrary"), vmem_mib=56),
        name="dh_norm_bwd",
    )(dproj, w, x, rstd, gain, dy)


def _dw_in(hbt, dproj, parity, name):
    tk = S
    win = WSH + 96

    def body(par_ref, a_ref, b_ref, o_ref, acc):
        p = 2 * pl.program_id(0) + par_ref[0]
        k = pl.program_id(1)

        @pl.when(k == 0)
        def _():
            acc[...] = jnp.zeros_like(acc)

        acc[...] += jnp.dot(a_ref[...], b_ref[...], preferred_element_type=f32)

        @pl.when(k == S // tk - 1)
        def _():
            acc_t = acc[...].T
            for pp in range(NDEV):
                off = (WSH * pp) % LANES

                @pl.when(p == pp)
                def _():
                    o_ref[...] = acc_t[off:off + WSH, :].astype(bf16)

    return pl.pallas_call(
        body,
        grid_spec=pltpu.PrefetchScalarGridSpec(
            num_scalar_prefetch=1,
            grid=(NDEV // 2, S // tk),
            in_specs=[
                pl.BlockSpec((D, tk), lambda q, k, par: (0, k)),
                pl.BlockSpec((pl.Element(tk), pl.Element(win)),
                             lambda q, k, par: (k * tk, (WSH * (2 * q + par[0])) // LANES * LANES)),
            ],
            out_specs=pl.BlockSpec((None, WSH, D), lambda q, k, par: (q, 0, 0)),
            scratch_shapes=[pltpu.VMEM((D, win), f32)],
        ),
        out_shape=jax.ShapeDtypeStruct((NDEV // 2, WSH, D), bf16),
        compiler_params=_params(("arbitrary", "arbitrary")),
        name=name,
    )(parity, hbt, dproj)


def _matmul_tokens(at, b, name):
    m, n = at.shape[0], b.shape[1]
    tn = 1024
    tk = 2048

    def body(a_ref, b_ref, o_ref):
        @pl.when(pl.program_id(1) == 0)
        def _():
            o_ref[...] = jnp.zeros_like(o_ref)

        o_ref[...] += jnp.dot(a_ref[...], b_ref[...], preferred_element_type=f32)

    return pl.pallas_call(
        body,
        grid=(n // tn, S // tk),
        in_specs=[pl.BlockSpec((m, tk), lambda j, k: (0, k)), pl.BlockSpec((tk, tn), lambda j, k: (k, j))],
        out_specs=pl.BlockSpec((m, tn), lambda j, k: (0, j)),
        out_shape=jax.ShapeDtypeStruct((m, n), f32),
        compiler_params=_params(("arbitrary", "arbitrary")),
        name=name,
    )(at, b)


def _exchange(scatter, gather, name):
    arrs = list(scatter) + list(gather)
    n = len(arrs)
    ns = len(scatter)

    def body(*refs):
        ins, outs = refs[:n], refs[n:2 * n]
        send_sems, recv_sems, local_sems = refs[2 * n:]
        x, y, c = lax.axis_index("x"), lax.axis_index("y"), lax.axis_index("c")
        me = 4 * x + 2 * y + c
        local, remote = [], []
        for a in range(n):
            lc = pltpu.make_async_copy(ins[a].at[me] if a < ns else ins[a], outs[a].at[me], local_sems.at[a])
            lc.start()
            local.append(lc)
            for r in range(1, NDEV):
                px = 1 - x if r & 4 else x
                py = 1 - y if r & 2 else y
                pc = 1 - c if r & 1 else c
                cp = pltpu.make_async_remote_copy(
                    src_ref=ins[a].at[4 * px + 2 * py + pc] if a < ns else ins[a],
                    dst_ref=outs[a].at[me],
                    send_sem=send_sems.at[a, r - 1],
                    recv_sem=recv_sems.at[a, r - 1],
                    device_id=(px, py, pc),
                    device_id_type=pl.DeviceIdType.MESH,
                )
                cp.start()
                remote.append(cp)
        for cp in remote:
            cp.wait_recv()
        for cp in remote:
            cp.wait_send()
        for lc in local:
            lc.wait()

    out_shape = [jax.ShapeDtypeStruct(a.shape if i < ns else (NDEV,) + a.shape, a.dtype) for i, a in enumerate(arrs)]
    return pl.pallas_call(
        body,
        in_specs=[pl.BlockSpec(memory_space=pl.ANY)] * n,
        out_specs=[pl.BlockSpec(memory_space=pl.ANY)] * n,
        out_shape=out_shape,
        scratch_shapes=[
            pltpu.SemaphoreType.DMA((n, NDEV - 1)),
            pltpu.SemaphoreType.DMA((n, NDEV - 1)),
            pltpu.SemaphoreType.DMA((n,)),
        ],
        compiler_params=pltpu.CompilerParams(has_side_effects=True),
        name=name,
    )(*arrs)


_HBM = pl.BlockSpec(memory_space=pltpu.HBM)
_SEM = pl.BlockSpec(memory_space=pltpu.SEMAPHORE)
_EFFECT = pltpu.SideEffectType.DATAFLOW_SIDE_EFFECTING


def _comm_step(name, body_fn, lands, srcs=(), wait_sems=(), n_new=0, after=(), token=False):
    n, ns, nw, na = len(lands), len(srcs), len(wait_sems), len(after)

    def body(*refs):
        src, land = refs[:ns], refs[ns:ns + n]
        waits = refs[ns + n:ns + n + nw]
        new = refs[ns + n + nw + na:ns + n + nw + na + n_new]
        body_fn(src, land, waits, new)
        if token:
            refs[-1][...] = jnp.zeros((8, LANES), f32)

    hbm = [pltpu.HBM(a.shape, a.dtype) for a in lands]
    ops = [pltpu.with_memory_space_constraint(a, pltpu.HBM) for a in list(srcs) + list(lands)]
    extra_shape = [jax.ShapeDtypeStruct((8, LANES), f32)] if token else []
    extra_spec = [pl.BlockSpec(memory_space=pltpu.VMEM)] if token else []
    outs = pl.pallas_call(
        body,
        out_shape=tuple([pltpu.SemaphoreType.DMA(())] * n_new + hbm + extra_shape),
        in_specs=[_HBM] * (ns + n) + [_SEM] * nw + [pl.BlockSpec(memory_space=pl.ANY)] * na,
        out_specs=tuple([_SEM] * n_new + [_HBM] * n + extra_spec),
        input_output_aliases={ns + i: n_new + i for i in range(n)},
        compiler_params=pltpu.CompilerParams(has_side_effects=_EFFECT),
        name=name,
    )(*ops, *wait_sems, *after)
    if token:
        return list(outs[:n_new]), list(outs[n_new:n_new + n]), outs[-1][0, 0]
    return list(outs[:n_new]), list(outs[n_new:])


class _GatheredWeights:
    def __init__(self, shards):
        self.n = n = len(shards)
        x, y, c = lax.axis_index("x"), lax.axis_index("y"), lax.axis_index("c")
        self.x = x
        me = 4 * x + 2 * y + c
        lands = [lax.dynamic_update_slice(lax.empty((NDEV,) + s.shape, s.dtype), s[None], (me,) + (0,) * s.ndim)
                 for s in shards]

        def start_own(src, land, waits, new):
            p = self._peers()
            for a in range(n):
                for k, to in ((0, p["sibling"]), (1, p["xn"]), (2, p["yn"])):
                    self._copy(land[a], new, a, k, 3, p["me"], to).start()

        self.sems, self.lands = {}, None
        new, self.lands = _comm_step("gather_start", start_own, lands, n_new=6 * n)
        self._keep(new, (0, 1, 2))

    @staticmethod
    def _peers():
        x, y, c = lax.axis_index("x"), lax.axis_index("y"), lax.axis_index("c")
        return dict(
            me=(x, y, c), sibling=(x, y, 1 - c), xn=(1 - x, y, c), yn=(x, 1 - y, c), dg=(1 - x, 1 - y, c),
            relay_origin=(jnp.bitwise_xor(x, c), jnp.bitwise_xor(y, 1 - c), c),
            relay_target=(jnp.bitwise_xor(x, 1 - c), jnp.bitwise_xor(y, c), c))

    def _keep(self, new, ks):
        half = len(new) // 2
        i = 0
        for a in range(self.n):
            for k in ks:
                self.sems[a, k] = (new[i], new[half + i])
                i += 1

    @staticmethod
    def _copy(land, sem_refs, a, k, nk, block, to, src=None, ks=None):
        ks = tuple(range(nk)) if ks is None else ks
        half = len(sem_refs) // 2
        i = a * len(ks) + ks.index(k)
        slot = land.at[4 * block[0] + 2 * block[1] + block[2]]
        return pltpu.make_async_remote_copy(
            src_ref=slot if src is None else src, dst_ref=slot, send_sem=sem_refs[i], recv_sem=sem_refs[half + i],
            device_id=to, device_id_type=pl.DeviceIdType.MESH)

    def _sem_list(self, ks):
        return ([self.sems[a, k][0] for a in range(self.n) for k in ks]
                + [self.sems[a, k][1] for a in range(self.n) for k in ks])

    def first_half(self, after):
        n = self.n

        def relay(src, land, waits, new):
            p = self._peers()
            for a in range(n):
                self._copy(land[a], waits, a, 1, 0, p["xn"], p["me"], ks=(1, 2)).wait_recv()
                self._copy(land[a], waits, a, 2, 0, p["yn"], p["me"], ks=(1, 2)).wait_recv()
                self._copy(land[a], new, a, 3, 0, p["relay_origin"], p["relay_target"], ks=(3, 4, 5)).start()
                self._copy(land[a], new, a, 4, 0, p["xn"], p["sibling"], ks=(3, 4, 5)).start()
                self._copy(land[a], new, a, 5, 0, p["yn"], p["sibling"], ks=(3, 4, 5)).start()

        new, self.lands = _comm_step("gather_relay", relay, self.lands, wait_sems=self._sem_list((1, 2)),
                                     n_new=6 * n, after=after)
        self._keep(new, (3, 4, 5))

        def from_sibling(src, land, waits, new):
            p = self._peers()
            other = lambda b: (b[0], b[1], 1 - b[2])
            for a in range(n):
                self._copy(land[a], waits, a, 0, 0, other(p["me"]), p["me"], ks=(0, 4, 5)).wait_recv()
                self._copy(land[a], waits, a, 4, 0, other(p["xn"]), p["me"], ks=(0, 4, 5)).wait_recv()
                self._copy(land[a], waits, a, 5, 0, other(p["yn"]), p["me"], ks=(0, 4, 5)).wait_recv()

        _, self.lands = _comm_step("gather_wait_sibling", from_sibling, self.lands,
                                   wait_sems=self._sem_list((0, 4, 5)))
        return self.lands[0].reshape(NW, D), self.x.astype(jnp.int32).reshape(1)

    def second_half(self, after):
        n = self.n

        def forward_diagonal(src, land, waits, new):
            p = self._peers()
            for a in range(n):
                self._copy(land[a], waits, a, 3, 0, p["dg"], p["me"], ks=(3,)).wait_recv()
                self._copy(land[a], new, a, 6, 0, p["dg"], p["sibling"], ks=(6,)).start()

        new, self.lands = _comm_step("gather_forward_diagonal", forward_diagonal, self.lands,
                                     wait_sems=self._sem_list((3,)), n_new=2 * n, after=after)
        self._keep(new, (6,))

        def finish(src, land, waits, new):
            p = self._peers()
            ks = tuple(range(7))
            for a in range(n):
                self._copy(land[a], waits, a, 6, 0, (p["dg"][0], p["dg"][1], 1 - p["dg"][2]), p["me"], ks=ks).wait_recv()
                for k in ks:
                    self._copy(land[a], waits, a, k, 0, p["me"], p["me"], ks=ks).wait_send()

        _, self.lands = _comm_step("gather_finish", finish, self.lands, wait_sems=self._sem_list(tuple(range(7))))
        return self.lands[0].reshape(NW, D), (1 - self.x).astype(jnp.int32).reshape(1)

    def rest(self):
        g_a, g_b, g_o, g_bm = self.lands[1:]
        return (g_a.transpose(1, 0, 2).reshape(512, D), g_b.transpose(1, 0, 2).reshape(512, D),
                g_bm.transpose(1, 0, 2).reshape(2, D), g_o.reshape(D, D))


def _sibling_send_start(shares):
    landing = lax.empty(shares.shape, shares.dtype)

    def start(src, land, waits, new):
        x, y, c = lax.axis_index("x"), lax.axis_index("y"), lax.axis_index("c")
        pltpu.make_async_remote_copy(src_ref=land[0], dst_ref=land[1], send_sem=new[0], recv_sem=new[1],
                                     device_id=(x, y, 1 - c), device_id_type=pl.DeviceIdType.MESH).start()

    return _comm_step("grad_sibling_start", start, [shares, landing], n_new=2, token=True)


def _sibling_send_wait(sems, lands, after):
    def wait(src, land, waits, new):
        x, y, c = lax.axis_index("x"), lax.axis_index("y"), lax.axis_index("c")
        done = pltpu.make_async_remote_copy(src_ref=land[0], dst_ref=land[1], send_sem=waits[0], recv_sem=waits[1],
                                            device_id=(x, y, c), device_id_type=pl.DeviceIdType.MESH)
        done.wait_send()
        done.wait_recv()

    _, lands = _comm_step("grad_sibling_wait", wait, lands, wait_sems=sems, after=after)
    return lands[1]


def _row_tile(rows, limit=256):
    fits = [t for t in range(16, limit + 1, 16) if rows % t == 0]
    return fits[-1] if fits else rows


def _pair_sum(mine, theirs, name):
    nb, rows, cols = mine.shape
    tr = _row_tile(rows, 528)

    def body(a_ref, b_ref, o_ref):
        o_ref[...] = (a_ref[...].astype(f32) + b_ref[...].astype(f32)).astype(bf16)

    blk = pl.BlockSpec((None, tr, cols), lambda q, i: (q, i, 0))
    return pl.pallas_call(
        body,
        grid=(nb, rows // tr),
        in_specs=[blk, blk],
        out_specs=blk,
        out_shape=jax.ShapeDtypeStruct(mine.shape, bf16),
        compiler_params=_params(("arbitrary", "arbitrary")),
        name=name,
    )(mine, theirs)


def _scatter_start(chip_arrs, all_arrs, name):
    arrs = list(chip_arrs) + list(all_arrs)
    n, nc = len(arrs), len(chip_arrs)
    lands = [lax.empty(((3 if i < nc else NDEV - 1),) + a.shape[1:], a.dtype) for i, a in enumerate(arrs)]

    def body(*refs):
        src, land = refs[:n], refs[n:2 * n]
        send_sems, recv_sems = refs[2 * n:3 * n], refs[3 * n:4 * n]
        token = refs[6 * n]
        x, y, c = lax.axis_index("x"), lax.axis_index("y"), lax.axis_index("c")
        for a in range(n):
            for r in range(1, 4 if a < nc else NDEV):
                if a < nc:
                    px, py, pc = (1 - x if r & 2 else x), (1 - y if r & 1 else y), c
                    block = 2 * px + py
                else:
                    px, py, pc = (1 - x if r & 4 else x), (1 - y if r & 2 else y), (1 - c if r & 1 else c)
                    block = 4 * px + 2 * py + pc
                pltpu.make_async_remote_copy(
                    src_ref=src[a].at[block], dst_ref=land[a].at[r - 1], send_sem=send_sems[a],
                    recv_sem=recv_sems[a], device_id=(px, py, pc), device_id_type=pl.DeviceIdType.MESH).start()
        token[...] = jnp.zeros_like(token)

    hbm = [pltpu.HBM(a.shape, a.dtype) for a in arrs + lands]
    ops = [pltpu.with_memory_space_constraint(a, pltpu.HBM) for a in arrs + lands]
    outs = pl.pallas_call(
        body,
        out_shape=tuple([pltpu.SemaphoreType.DMA(())] * (2 * n) + hbm + [jax.ShapeDtypeStruct((8, LANES), f32)]),
        in_specs=[_HBM] * (2 * n),
        out_specs=tuple([_SEM] * (2 * n) + [_HBM] * (2 * n) + [pl.BlockSpec(memory_space=pltpu.VMEM)]),
        input_output_aliases={i: 2 * n + i for i in range(2 * n)},
        compiler_params=pltpu.CompilerParams(has_side_effects=_EFFECT),
        name=name,
    )(*ops)
    return outs[:n], outs[n:2 * n], outs[2 * n:3 * n], outs[3 * n:4 * n], outs[4 * n]


def _scatter_wait(send_sems, recv_sems, srcs, lands, after, name):
    n = len(srcs)

    def body(*refs):
        land = refs[n:2 * n]
        ssem, rsem = refs[2 * n:3 * n], refs[3 * n:4 * n]
        x, y, c = lax.axis_index("x"), lax.axis_index("y"), lax.axis_index("c")
        for a in range(n):
            done = pltpu.make_async_remote_copy(
                src_ref=land[a], dst_ref=land[a], send_sem=ssem[a], recv_sem=rsem[a], device_id=(x, y, c),
                device_id_type=pl.DeviceIdType.MESH)
            done.wait_send()
            done.wait_recv()

    hbm = [pltpu.HBM(a.shape, a.dtype) for a in list(srcs) + list(lands)]
    outs = pl.pallas_call(
        body,
        out_shape=tuple(hbm),
        in_specs=[_HBM] * (2 * n) + [_SEM] * (2 * n) + [pl.BlockSpec(memory_space=pl.ANY)],
        out_specs=tuple([_HBM] * (2 * n)),
        input_output_aliases={i: i for i in range(2 * n)},
        compiler_params=pltpu.CompilerParams(has_side_effects=_EFFECT),
        name=name,
    )(*srcs, *lands, *send_sems, *recv_sems, after)
    return outs[:n], outs[n:]


def _adam_update(g, w_ref, m_ref, v_ref, g_ref, d_ref, nm_ref, nv_ref):
    mm = ADAM_B1 * m_ref[...] + (1.0 - ADAM_B1) * g
    vv = ADAM_B2 * v_ref[...] + (1.0 - ADAM_B2) * (g * g)
    m_hat = mm / (1.0 - ADAM_B1 ** ADAM_STEP)
    v_hat = vv / (1.0 - ADAM_B2 ** ADAM_STEP)
    g_ref[...] = g
    d_ref[...] = -ADAM_LR * (m_hat / (jnp.sqrt(v_hat) + ADAM_EPS) + ADAM_WD * w_ref[...])
    nm_ref[...] = mm
    nv_ref[...] = vv


def _adamw_own(w, own, own_idx, slots, m, v, name):
    r, c = w.shape[-2:]
    tr = _row_tile(r, 384)
    k = slots.shape[0]

    def body(i_ref, w_ref, o_ref, s_ref, m_ref, v_ref, g_ref, d_ref, nm_ref, nv_ref):
        del i_ref
        g = o_ref[...].astype(f32)
        for j in range(k):
            g = g + s_ref[j].astype(f32)
        _adam_update(g, w_ref, m_ref, v_ref, g_ref, d_ref, nm_ref, nv_ref)

    blk = pl.BlockSpec((None, tr, c), lambda i, ix: (0, i, 0))
    return pl.pallas_call(
        body,
        grid_spec=pltpu.PrefetchScalarGridSpec(
            num_scalar_prefetch=1,
            grid=(r // tr,),
            in_specs=[blk, pl.BlockSpec((None, tr, c), lambda i, ix: (ix[0], i, 0)),
                      pl.BlockSpec((k, tr, c), lambda i, ix: (0, i, 0)), blk, blk],
            out_specs=[blk] * 4,
        ),
        out_shape=[jax.ShapeDtypeStruct(w.shape, f32)] * 4,
        compiler_params=_params(("arbitrary",)),
        name=name,
    )(own_idx, w, own, slots, m, v)


def _adamw(w, slots, m, v, name):
    r, c = w.shape[-2:]
    tr = _row_tile(r, 128)

    def body(w_ref, s_ref, m_ref, v_ref, g_ref, d_ref, nm_ref, nv_ref):
        g = s_ref[0].astype(f32)
        for k in range(1, NDEV):
            g = g + s_ref[k].astype(f32)
        _adam_update(g, w_ref, m_ref, v_ref, g_ref, d_ref, nm_ref, nv_ref)

    if w.ndim == 3:
        blk = pl.BlockSpec((None, tr, c), lambda i: (0, i, 0))
    else:
        blk = pl.BlockSpec((tr, c), lambda i: (i, 0))
    return pl.pallas_call(
        body,
        grid=(r // tr,),
        in_specs=[blk, pl.BlockSpec((NDEV, tr, c), lambda i: (0, i, 0)), blk, blk],
        out_specs=[blk] * 4,
        out_shape=[jax.ShapeDtypeStruct(w.shape, f32)] * 4,
        compiler_params=_params(("arbitrary",)),
        name=name,
    )(w, slots, m, v)


class _Weights:
    def __init__(self, w_t, w_a, w_b, b_merge, w_o):
        self._w_t, self._rest = w_t, (w_a, w_b, b_merge, w_o)

    def first_half(self, after):
        del after
        return self._w_t, jnp.zeros((1,), jnp.int32)

    def second_half(self, after):
        del after
        return self._w_t, jnp.ones((1,), jnp.int32)

    def rest(self):
        return self._rest


def _local_step(x, tgt, norm_gain, weights, qn_a, kn_a, qn_b, kn_b, sink_a, rel_bias, on_weight_grads=None,
                core=None):
    two = lambda t: jnp.concatenate([t, t], axis=-1).reshape(1, LANES)
    ones = jnp.ones((1, LANES), f32)
    gains = jnp.stack([
        jnp.stack([two(qn_a), two(kn_a), ones]),
        jnp.stack([two(qn_b), two(kn_b), ones]),
        jnp.stack([two(qn_b), two(kn_b), ones]),
        jnp.stack([two(qn_b), two(kn_b), ones]),
    ])
    buckets = [jnp.asarray(_bucket_np(blk, d)) for blk, d, _ in GROUPS]
    bias = [_bias_expand(rel_bias, buckets[k], GROUPS[k][2], "bias_expand_%d" % k) for k in range(4)]

    hb, hbt, rstd = _rms(x, norm_gain)
    w_t, half = weights.first_half([hb] + bias)
    proj = _inproj_half(hb, w_t, half, None, "inproj_1")
    w_t, half = weights.second_half([proj])
    proj = _inproj_half(hb, w_t, half, proj, "inproj_2")
    w_a, w_b, b_merge, w_o = weights.rest()
    gl = _prep(proj, gains)
    o_a, l_a = _attn_fwd(gl, bias[0], sink_a.reshape(8), 0, 128, 1, "attn_fwd_a")
    fwd_b = [_attn_fwd(gl, bias[k], None, k, GROUPS[k][0], GROUPS[k][1], "attn_fwd_b%d" % k) for k in (1, 2, 3)]
    sink_b = jnp.repeat(sink_a.reshape(8), HD).reshape(1, 512)

    (dy, dyb, dproj, do_a, dd_a, do_b0, do_b1, do_b2, dd_b0, dd_b1, dd_b2, ya, yb, mg, dbr_a, dbr_b, loss, dbm,
     dsk) = _tail(x, tgt, o_a, l_a, [f[0] for f in fwd_b], [f[1] for f in fwd_b], proj, b_merge, w_a, w_b, w_o, sink_b)

    dw_o = _matmul_tokens(mg, dyb, "dw_out")
    dw_a = _matmul_tokens(ya, dbr_a, "dw_branch_a")
    dw_b = _matmul_tokens(yb, dbr_b, "dw_branch_b")
    if on_weight_grads is not None:
        early = on_weight_grads(dict(w_branch_a=dw_a, w_branch_b=dw_b, b_merge=dbm, w_out=dw_o))
        buckets = [buckets[0] + early.astype(jnp.int32)] + buckets[1:]

    dqkv_a, dbk_a = _attn_bwd(gl, bias[0], buckets[0], do_a, l_a, dd_a, 0, 128, 1, "attn_bwd_a")
    dproj, dg_a = _post_a(dqkv_a, proj, gains, dproj)
    dbk_b, dg_b = [], []
    for k, do_k, dd_k in ((1, do_b0, dd_b0), (2, do_b1, dd_b1), (3, do_b2, dd_b2)):
        dqkv, dbk = _attn_bwd(gl, bias[k], buckets[k], do_k, fwd_b[k - 1][1], dd_k, k, GROUPS[k][0], GROUPS[k][1],
                              "attn_bwd_b%d" % k)
        dproj, dg = _post_b(k, dqkv, proj, gains, dproj)
        dbk_b.append(dbk)
        dg_b.append(dg)
    dg_b = jnp.stack(dg_b)

    core = jnp.zeros((1,), jnp.int32) if core is None else core
    dw_other = _dw_in(hbt, dproj, 1 - core, "dw_in_other")
    sent = jnp.zeros((), f32) if on_weight_grads is None else on_weight_grads(dict(w_in_other=dw_other))
    dw_in = _dw_in(hbt, dproj, core + sent.astype(jnp.int32), "dw_in_own")
    token = jnp.zeros((), f32) if on_weight_grads is None else on_weight_grads(dict(w_in=dw_in))
    grad_x, d_norm_gain = _dh_norm_bwd(dproj, w_t, x, rstd, norm_gain + token, dy)

    fold = lambda t: t[..., :HD] + t[..., HD:]
    d_qn_a = fold(dg_a[0, 0])
    d_kn_a = fold(dg_a[1, 0])
    d_qn_b = fold(dg_b[:, 0, 0].sum(axis=0))
    d_kn_b = fold(dg_b[:, 1, 0].sum(axis=0))
    d_sink = dsk.reshape(8, HD)[:, 0]
    red = jnp.stack([dbk_a] + dbk_b)
    d_rel = red[:, :, 0, :32].reshape(32, 32).T
    return dict(loss=loss, grad_x=grad_x, norm_gain=d_norm_gain, w_in=dw_in, w_in_other=dw_other, q_norm_a=d_qn_a,
                k_norm_a=d_kn_a,
                q_norm_b=d_qn_b, k_norm_b=d_kn_b, sink_a=d_sink, rel_bias=d_rel, w_branch_a=dw_a, w_branch_b=dw_b,
                b_merge=dbm, w_out=dw_o)


SMALL = (("norm_gain", D), ("q_norm_a", HD), ("k_norm_a", HD), ("q_norm_b", HD), ("k_norm_b", HD), ("sink_a", 8),
         ("rel_bias", 1024))
SMALL_PAD = 2432


SMALL_USED = sum(sz for _, sz in SMALL)


def _pack_small(parts, loss=None):
    tail = jnp.zeros((SMALL_PAD - SMALL_USED,), f32)
    if loss is not None:
        tail = tail.at[0].set(loss.reshape(()))
    return jnp.concatenate([parts[n].reshape(-1) for n, _ in SMALL] + [tail]).reshape(1, SMALL_PAD)


def _unpack_small(flat, shapes):
    out, off = {}, 0
    for n, sz in SMALL:
        out[n] = flat[0, off:off + sz].reshape(shapes[n])
        off += sz
    return out


def kernel(x, norm_gain, w_in, q_norm_a, k_norm_a, q_norm_b, k_norm_b, sink_a, rel_bias, w_branch_a, w_branch_b, b_merge, w_out, loss_target, m_norm_gain, m_w_in, m_q_norm_a, m_k_norm_a, m_q_norm_b, m_k_norm_b, m_sink_a, m_rel_bias, m_w_branch_a, m_w_branch_b, m_b_merge, m_w_out, v_norm_gain, v_w_in, v_q_norm_a, v_k_norm_a, v_q_norm_b, v_k_norm_b, v_sink_a, v_rel_bias, v_w_branch_a, v_w_branch_b, v_b_merge, v_w_out):
    csh = D // NDEV
    w_in_t, m_w_in_t, v_w_in_t = (jnp.swapaxes(t, 1, 2) for t in (w_in, m_w_in, v_w_in))
    weights = _GatheredWeights([w_in_t[0].astype(bf16), w_branch_a[0].astype(bf16), w_branch_b[0].astype(bf16),
                                w_out[0].astype(bf16), b_merge[0]])

    pending = {}
    core = lax.axis_index("c").astype(jnp.int32).reshape(1)
    chip = (2 * lax.axis_index("x") + lax.axis_index("y")).astype(jnp.int32).reshape(1)
    me = (2 * chip + core).astype(jnp.int32)

    def start_exchange(gw):
        if "w_in_other" in gw:
            sems, lands, sent = _sibling_send_start(gw["w_in_other"])
            pending["sibling"] = (sems, lands)
            return sent
        if "w_in" in gw:
            from_sibling = _sibling_send_wait(*pending["sibling"], after=[gw["w_in"]])
            chip_sums = _pair_sum(gw["w_in"], from_sibling, "grad_pair_sum")
            pending["w_in"] = _scatter_start([chip_sums], [], "scatter_w_in_start")
            return pending["w_in"][4][0, 0]
        blocks = [gw["w_branch_a"].reshape(512, NDEV, csh).transpose(1, 0, 2).astype(bf16),
                  gw["w_branch_b"].reshape(512, NDEV, csh).transpose(1, 0, 2).astype(bf16),
                  gw["w_out"].reshape(NDEV, csh, D).astype(bf16),
                  gw["b_merge"].reshape(2, NDEV, csh).transpose(1, 0, 2)]
        pending["rest"] = _scatter_start([], blocks, "scatter_rest_start")
        return pending["rest"][4][0, 0]

    loc = _local_step(x[0], loss_target[0], norm_gain, weights, q_norm_a, k_norm_a, q_norm_b, k_norm_b, sink_a,
                      rel_bias, on_weight_grads=start_exchange, core=core)

    small_shapes = dict(norm_gain=(1, D), q_norm_a=(1, HD), k_norm_a=(1, HD), q_norm_b=(1, HD), k_norm_b=(1, HD),
                        sink_a=(1, 8), rel_bias=(32, 32))
    (r_small,) = _exchange([], [_pack_small(loc, loc["loss"])], "gather_small_grads")
    send_sems, recv_sems, srcs, lands, _ = pending["rest"]
    (s_a, s_b, s_o, s_bm), (r_a, r_b, r_o, r_bm) = _scatter_wait(
        send_sems, recv_sems, srcs, lands, r_small, "scatter_rest_wait")
    send_sems, recv_sems, srcs, lands, _ = pending["w_in"]
    (s_in,), (r_in,) = _scatter_wait(send_sems, recv_sems, srcs, lands, r_small, "scatter_w_in_wait")

    given = dict(norm_gain=norm_gain, q_norm_a=q_norm_a, k_norm_a=k_norm_a, q_norm_b=q_norm_b, k_norm_b=k_norm_b,
                 sink_a=sink_a, rel_bias=rel_bias)
    m_small = dict(norm_gain=m_norm_gain, q_norm_a=m_q_norm_a, k_norm_a=m_k_norm_a, q_norm_b=m_q_norm_b,
                   k_norm_b=m_k_norm_b, sink_a=m_sink_a, rel_bias=m_rel_bias)
    v_small = dict(norm_gain=v_norm_gain, q_norm_a=v_q_norm_a, k_norm_a=v_k_norm_a, q_norm_b=v_q_norm_b,
                   k_norm_b=v_k_norm_b, sink_a=v_sink_a, rel_bias=v_rel_bias)
    res = {
        "small": _adamw(_pack_small(given), r_small, _pack_small(m_small), _pack_small(v_small), "adamw_small"),
        "w_in": [jnp.swapaxes(t, 1, 2) for t in
                 _adamw_own(w_in_t, s_in, chip, r_in, m_w_in_t, v_w_in_t, "adamw_w_in")],
        "w_branch_a": _adamw_own(w_branch_a, s_a, me, r_a, m_w_branch_a, v_w_branch_a, "adamw_w_branch_a"),
        "w_branch_b": _adamw_own(w_branch_b, s_b, me, r_b, m_w_branch_b, v_w_branch_b, "adamw_w_branch_b"),
        "b_merge": _adamw_own(b_merge, s_bm, me, r_bm, m_b_merge, v_b_merge, "adamw_b_merge"),
        "w_out": _adamw_own(w_out, s_o, me, r_o, m_w_out, v_w_out, "adamw_w_out"),
    }
    order = ["norm_gain", "w_in", "q_norm_a", "k_norm_a", "q_norm_b", "k_norm_b", "sink_a", "rel_bias", "w_branch_a",
             "w_branch_b", "b_merge", "w_out"]
    outs = []
    for k in range(4):
        small = _unpack_small(res["small"][k], small_shapes)
        for n in order:
            outs.append(small[n] if n in small else res[n][k])
    loss = res["small"][0][0, SMALL_USED]
    return (loss, loc["grad_x"][None], *outs)
```

```python
import math

import numpy as np
import jax
import jax.numpy as jnp
from jax import lax
from jax.experimental import pallas as pl
from jax.experimental.pallas import tpu as pltpu

f32 = jnp.float32
bf16 = jnp.bfloat16

S = 4096
D = 1024
NA = 5376
NT = 3072
NW = NA + NT
WSH = NW // 8
HD = 64
LANES = 128
EPS = 1e-6
NEG = -1e30
SCALE = HD ** -0.5
TQ = 128
PAD = 128
SP = S + 2 * PAD
NDEV = 8
GROUPS = ((128, 1, 0), (64, 1, 8), (64, 4, 16), (64, 16, 24))
CHUNK = 256
PCHUNK = 128
RC = 64

ADAM_LR, ADAM_B1, ADAM_B2, ADAM_EPS, ADAM_WD, ADAM_STEP = 0.001, 0.9, 0.999, 1e-08, 0.01, 10

MIB = 1024 * 1024
NT_DIMS = (((1,), (1,)), ((), ()))
TN_DIMS = (((0,), (0,)), ((), ()))


def _params(sem=None, vmem_mib=48):
    return pltpu.CompilerParams(dimension_semantics=sem, vmem_limit_bytes=vmem_mib * MIB)


def _lo():
    return lax.broadcasted_iota(jnp.int32, (1, LANES), 1) < HD


def _head_ones():
    r = lax.broadcasted_iota(jnp.int32, (LANES, LANES), 0) // HD
    c = lax.broadcasted_iota(jnp.int32, (LANES, LANES), 1) // HD
    return jnp.where(r == c, 1.0, 0.0).astype(bf16)


def _half_sums(x, ones):
    hi = x.astype(bf16)
    mid = (x - hi.astype(f32)).astype(bf16)
    return (jnp.dot(hi, ones, preferred_element_type=f32) + jnp.dot(mid, ones, preferred_element_type=f32))


def _seg_sum(x, ones):
    outs = [_half_sums(x[:, b * LANES:(b + 1) * LANES], ones) for b in range(x.shape[1] // LANES)]
    return outs[0] if len(outs) == 1 else jnp.concatenate(outs, axis=1)


def _bucket_np(blk, stride):
    w = TQ + 2 * blk
    rel = np.arange(w)[None, :] - blk - np.arange(TQ)[:, None]
    band = np.abs(rel) <= blk
    r = rel * stride
    n = np.abs(r)
    nf = np.maximum(n, 8).astype(np.float32)
    large = 8 + (np.log(nf / np.float32(8)) / np.float32(math.log(128.0)) * np.float32(8)).astype(np.int32)
    large = np.minimum(large, 15)
    b = (r > 0).astype(np.int32) * 16 + np.where(n < 8, n, large)
    return np.where(band, b, -1).astype(np.int32)


def _rms(x, gain):
    ts = 512

    def body(x_ref, g_ref, h_ref, ht_ref, r_ref):
        xv = x_ref[...]
        r = lax.rsqrt(jnp.mean(xv * xv, axis=-1, keepdims=True) + EPS)
        h = (xv * r) * g_ref[...]
        h_ref[...] = h.astype(bf16)
        ht_ref[...] = h.T.astype(bf16)
        r_ref[...] = r

    return pl.pallas_call(
        body,
        grid=(S // ts,),
        in_specs=[pl.BlockSpec((ts, D), lambda i: (i, 0)), pl.BlockSpec((1, D), lambda i: (0, 0))],
        out_specs=[pl.BlockSpec((ts, D), lambda i: (i, 0)), pl.BlockSpec((D, ts), lambda i: (0, i)),
                   pl.BlockSpec((ts, 1), lambda i: (i, 0))],
        out_shape=[jax.ShapeDtypeStruct((S, D), bf16), jax.ShapeDtypeStruct((D, S), bf16),
                   jax.ShapeDtypeStruct((S, 1), f32)],
        compiler_params=_params(("arbitrary",)),
        name="rms",
    )(x, gain)


def _inproj_half(hb, w_t, half, proj, name):
    ts = 512
    tn = NW // 2
    per = NW // 2 // tn

    def body(h_idx, h_ref, w_ref, *rest):
        del h_idx
        rest[-1][...] = lax.dot_general(h_ref[...], w_ref[...], NT_DIMS, preferred_element_type=f32)

    in_specs = [pl.BlockSpec((ts, D), lambda i, n, hf: (i, 0)),
                pl.BlockSpec((tn, D), lambda i, n, hf: (hf[0] * per + n, 0))]
    args = [half, hb, w_t]
    aliases = {}
    if proj is not None:
        in_specs.append(pl.BlockSpec(memory_space=pl.ANY))
        args.append(proj)
        aliases = {3: 0}
    return pl.pallas_call(
        body,
        grid_spec=pltpu.PrefetchScalarGridSpec(
            num_scalar_prefetch=1,
            grid=(S // ts, per),
            in_specs=in_specs,
            out_specs=pl.BlockSpec((ts, tn), lambda i, n, hf: (i, hf[0] * per + n)),
        ),
        out_shape=jax.ShapeDtypeStruct((S, NW), f32),
        input_output_aliases=aliases,
        compiler_params=_params(("arbitrary", "arbitrary")),
        name=name,
    )(*args)


def _bias_expand(table, bucket, c0, name):
    tq, w = bucket.shape
    blk = (w - tq) // 2

    def body(tab_ref, bk_ref, o_ref):
        h = pl.program_id(0)
        bk = bk_ref[...]

        def step(b, acc):
            return jnp.where(bk == b, tab_ref[b, c0 + h], acc)

        inner = lax.fori_loop(0, 32, step, jnp.full((tq, w), NEG, f32))
        col = lax.broadcasted_iota(jnp.int32, (1, w), 1)
        o_ref[0] = jnp.where(col < blk, NEG, inner)
        o_ref[1] = inner
        o_ref[2] = jnp.where(col >= tq + blk, NEG, inner)

    return pl.pallas_call(
        body,
        grid=(8,),
        in_specs=[pl.BlockSpec(memory_space=pltpu.SMEM), pl.BlockSpec((tq, w), lambda h: (0, 0))],
        out_specs=pl.BlockSpec((3, None, tq, w), lambda h: (0, h, 0, 0)),
        out_shape=jax.ShapeDtypeStruct((3, 8, tq, w), f32),
        compiler_params=_params(("arbitrary",)),
        name=name,
    )(table, bucket)


def _tile_kind(t, seq):
    m0 = jnp.bitwise_and(t * TQ, seq - 1)
    return jnp.where(m0 == 0, 0, jnp.where(m0 == seq - TQ, 2, 1))


def _col_block(g, j):
    kind = j // 4
    hp = j % 4
    a = jnp.where(kind == 0, hp, 3 + kind)
    b = 6 + 12 * kind + 4 * (g - 1) + hp
    return jnp.where(g == 0, a, b)


def _prep(proj_a, gains):
    def body(p0_ref, p1_ref, p2_ref, p3_ref, g_ref, o_ref):
        g = pl.program_id(0)
        kind = pl.program_id(1)
        lo = _lo()
        ones = _head_ones()
        half = jnp.where(lo, 0, 1)
        gain = g_ref[...]

        def norm_store(xv, u, dst, dup):
            if dup:
                take = (kind == 0) | (half == u // 2)
                xv = jnp.where(take, xv, pltpu.roll(xv, HD, 1))
            r = lax.rsqrt(_half_sums(xv * xv, ones) * (1.0 / HD) + EPS)
            r = jnp.where(kind == 2, 1.0, r)
            yv = (xv * r) * gain
            yv = jnp.where(kind == 0, yv * SCALE, yv)
            o_ref[u, PAD + dst:PAD + dst + CHUNK, :] = yv.astype(bf16)

        for u in range(4):
            o_ref[u, 0:PAD, :] = jnp.zeros((PAD, LANES), bf16)
            o_ref[u, PAD + S:SP, :] = jnp.zeros((PAD, LANES), bf16)
        for gi, (_, d, _) in enumerate(GROUPS):
            @pl.when(g == gi)
            def _():
                seq = S // d
                for u, p_ref in enumerate((p0_ref, p1_ref, p2_ref, p3_ref)):
                    for c in range(d):
                        for i in range(seq // CHUNK):
                            if d == 1:
                                xv = p_ref[i * CHUNK:(i + 1) * CHUNK, :]
                            else:
                                xv = p_ref[pl.ds(c + i * CHUNK * d, CHUNK, stride=d), :]
                            norm_store(xv, u, c * seq + i * CHUNK, gi == 0)

    return pl.pallas_call(
        body,
        grid=(4, 3),
        in_specs=[pl.BlockSpec((S, LANES), lambda g, kind, u=u: (0, _col_block(g, 4 * kind + u))) for u in range(4)] + [
            pl.BlockSpec((None, None, 1, LANES), lambda g, kind: (g, kind, 0, 0)),
        ],
        out_specs=pl.BlockSpec((None, 4, SP, LANES), lambda g, kind: (g, kind, 0, 0)),
        out_shape=jax.ShapeDtypeStruct((4, 12, SP, LANES), bf16),
        compiler_params=_params(("arbitrary", "arbitrary")),
        name="prep",
    )(proj_a, proj_a, proj_a, proj_a, gains)


def _token_rows(t, r0, n, d):
    if d == 1:
        return pl.ds(pl.multiple_of(t * TQ, TQ) + r0, n)
    per = S // d // TQ
    return pl.ds(((t % per) * TQ + r0) * d + t // per, n, stride=d)


def _stack_heads(t, lo):
    z = jnp.zeros_like(t)
    return jnp.concatenate([jnp.where(lo, t, z), jnp.where(lo, z, t)], axis=0)


def _unstack_heads(t2, lo):
    return jnp.where(lo, t2[:TQ], t2[TQ:])


def _attn_fwd(gl, bias, sink, g, blk, d, name):
    w = TQ + 2 * blk
    seq = S // d
    use_sink = sink is not None

    def body(*refs):
        if use_sink:
            sink_ref, q_ref, k_ref, v_ref, b_ref, o_ref, l_ref, s0, s1, p0, p1, lse_scr = refs
        else:
            q_ref, k_ref, v_ref, b_ref, o_ref, l_ref, s0, s1, p0, p1, lse_scr = refs
        hp = pl.program_id(0)
        lo = _lo()
        s_bufs, p_bufs = (s0, s1), (p0, p1)

        def scores(p, slot):
            for u in range(2):
                f0 = pl.multiple_of((2 * p + u) * TQ, TQ)
                q2 = _stack_heads(q_ref[pl.ds(PAD + f0, TQ), :], lo)
                kw = k_ref[pl.ds(PAD - blk + f0, w), :]
                s_bufs[slot][u] = lax.dot_general(q2, kw, NT_DIMS, preferred_element_type=f32)

        def softmax(p, slot):
            for u in range(2):
                t = 2 * p + u
                kind = _tile_kind(t, seq)
                for h in range(2):
                    for r in range(TQ // RC):
                        rows = slice(h * TQ + r * RC, h * TQ + (r + 1) * RC)
                        logit = s_bufs[slot][u, rows, :] + b_ref[kind, h, r * RC:(r + 1) * RC, :]
                        m = jnp.max(logit, axis=1, keepdims=True)
                        e = jnp.exp(logit - m)
                        lse = m + jnp.log(jnp.sum(e, axis=1, keepdims=True))
                        if use_sink:
                            sk = sink_ref[2 * hp + h]
                            mx = jnp.maximum(lse, sk)
                            lse = mx + jnp.log(jnp.exp(lse - mx) + jnp.exp(sk - mx))
                        p_bufs[slot][u, rows, :] = (e * jnp.exp(m - lse)).astype(bf16)
                        lse_scr[u, rows, :] = jnp.broadcast_to(lse, (RC, LANES))
                l_ref[_token_rows(t, 0, TQ, d), :] = jnp.where(lo, lse_scr[u, 0:TQ, :], lse_scr[u, TQ:2 * TQ, :])

        def values(p, slot):
            for u in range(2):
                t = 2 * p + u
                vw = v_ref[pl.ds(PAD - blk + pl.multiple_of(t * TQ, TQ), w), :]
                o2 = jnp.dot(p_bufs[slot][u], vw, preferred_element_type=f32)
                o_ref[_token_rows(t, 0, TQ, d), :] = _unstack_heads(o2, lo)

        npair = S // TQ // 2
        scores(0, 0)
        scores(1, 1)
        softmax(0, 0)

        def steady(k, carry):
            p = 2 * k + 2
            scores(p, 0)
            softmax(p - 1, 1)
            values(p - 2, 0)
            scores(p + 1, 1)
            softmax(p, 0)
            values(p - 1, 1)
            return carry

        lax.fori_loop(0, (npair - 2) // 2, steady, 0)
        softmax(npair - 1, 1)
        values(npair - 2, 0)
        values(npair - 1, 1)

    in_specs = [
        pl.BlockSpec((None, None, SP, LANES), lambda hp: (g, hp, 0, 0)),
        pl.BlockSpec((None, None, SP, LANES), lambda hp: (g, 4 + hp, 0, 0)),
        pl.BlockSpec((None, None, SP, LANES), lambda hp: (g, 8 + hp, 0, 0)),
        pl.BlockSpec((3, 2, TQ, w), lambda hp: (0, hp, 0, 0)),
    ]
    args = [gl, gl, gl, bias]
    if use_sink:
        in_specs = [pl.BlockSpec(memory_space=pltpu.SMEM)] + in_specs
        args = [sink] + args
    out = pl.BlockSpec((S, LANES), lambda hp: (0, hp))
    return pl.pallas_call(
        body,
        grid=(4,),
        in_specs=in_specs,
        out_specs=[out, out],
        out_shape=[jax.ShapeDtypeStruct((S, 4 * LANES), f32)] * 2,
        scratch_shapes=[pltpu.VMEM((2, 2 * TQ, w), f32), pltpu.VMEM((2, 2 * TQ, w), f32),
                        pltpu.VMEM((2, 2 * TQ, w), bf16), pltpu.VMEM((2, 2 * TQ, w), bf16),
                        pltpu.VMEM((2, 2 * TQ, LANES), f32)],
        compiler_params=_params(("arbitrary",)),
        name=name,
    )(*args)


def _attn_bwd(gl, bias, bucket, do, lse, dd, g, blk, d, name):
    w = TQ + 2 * blk
    seq = S // d

    def body(q_ref, k_ref, v_ref, b_ref, bk_ref, do_ref, l_ref, d_ref, dqkv_ref, dbk_ref,
             db_acc, s0, s1, dp0, dp1, pb0, pb1, ds0, ds1, dk_acc, dv_acc):
        lo = _lo()
        hi = jnp.logical_not(lo)
        dk_acc[...] = jnp.zeros((SP, LANES), f32)
        dv_acc[...] = jnp.zeros((SP, LANES), f32)
        db_acc[...] = jnp.zeros((2 * TQ, w), f32)
        s_bufs, dp_bufs, pb_bufs, ds_bufs = (s0, s1), (dp0, dp1), (pb0, pb1), (ds0, ds1)

        def stacked(t):
            f0 = pl.multiple_of(t * TQ, TQ)
            q2 = _stack_heads(q_ref[pl.ds(PAD + f0, TQ), :], lo)
            do2 = _stack_heads(do_ref[_token_rows(t, 0, TQ, d), :].astype(bf16), lo)
            return f0, q2, do2

        def scores(p, slot):
            for u in range(2):
                f0, q2, do2 = stacked(2 * p + u)
                win = pl.ds(PAD - blk + f0, w)
                s_bufs[slot][u] = lax.dot_general(q2, k_ref[win, :], NT_DIMS, preferred_element_type=f32)
                dp_bufs[slot][u] = lax.dot_general(do2, v_ref[win, :], NT_DIMS, preferred_element_type=f32)

        def grads(p, slot):
            for u in range(2):
                t = 2 * p + u
                kind = _tile_kind(t, seq)
                for h in range(2):
                    msk = lo if h == 0 else hi
                    for r in range(TQ // RC):
                        rows = slice(h * TQ + r * RC, h * TQ + (r + 1) * RC)
                        src = _token_rows(t, r * RC, RC, d)
                        lh = jnp.max(jnp.where(msk, l_ref[src, :], -jnp.inf), axis=1, keepdims=True)
                        dh = jnp.max(jnp.where(msk, d_ref[src, :], -jnp.inf), axis=1, keepdims=True)
                        logit = s_bufs[slot][u, rows, :] + b_ref[kind, h, r * RC:(r + 1) * RC, :]
                        pr = jnp.exp(logit - lh)
                        ds = pr * (dp_bufs[slot][u, rows, :] - dh)
                        db_acc[rows, :] += ds
                        pb_bufs[slot][u, rows, :] = pr.astype(bf16)
                        ds_bufs[slot][u, rows, :] = ds.astype(bf16)

        def accumulate(p, slot):
            for u in range(2):
                f0, q2, do2 = stacked(2 * p + u)
                win = pl.ds(PAD - blk + f0, w)
                dsb = ds_bufs[slot][u]
                dq2 = jnp.dot(dsb, k_ref[win, :], preferred_element_type=f32)
                dqkv_ref[0, pl.ds(PAD + f0, TQ), :] = _unstack_heads(dq2, lo).astype(bf16)
                dk_acc[win, :] += lax.dot_general(dsb, q2, TN_DIMS, preferred_element_type=f32)
                dv_acc[win, :] += lax.dot_general(pb_bufs[slot][u], do2, TN_DIMS, preferred_element_type=f32)

        npair = S // TQ // 2
        scores(0, 0)
        scores(1, 1)
        grads(0, 0)

        def steady(k, carry):
            p = 2 * k + 2
            scores(p, 0)
            grads(p - 1, 1)
            accumulate(p - 2, 0)
            scores(p + 1, 1)
            grads(p, 0)
            accumulate(p - 1, 1)
            return carry

        lax.fori_loop(0, (npair - 2) // 2, steady, 0)
        grads(npair - 1, 1)
        accumulate(npair - 2, 0)
        accumulate(npair - 1, 1)
        for i in range(SP // CHUNK):
            rows = slice(i * CHUNK, (i + 1) * CHUNK)
            dqkv_ref[1, rows, :] = dk_acc[rows, :].astype(bf16)
            dqkv_ref[2, rows, :] = dv_acc[rows, :].astype(bf16)

        bk = bk_ref[...]
        lane = lax.broadcasted_iota(jnp.int32, (8, LANES), 1)
        for h in range(2):
            db = db_acc[h * TQ:(h + 1) * TQ, :]
            acc = jnp.zeros((8, LANES), f32)
            for b in range(32):
                part = jnp.where(bk == b, db, 0.0).reshape(TQ // 8, 8, w).sum(axis=0)
                tot = jnp.sum(jnp.sum(part, axis=1, keepdims=True), axis=0, keepdims=True)
                acc = jnp.where(lane == b, tot, acc)
            dbk_ref[h] = acc

    def gcol(off):
        return pl.BlockSpec((None, None, SP, LANES), lambda hp: (g, off + hp, 0, 0))

    row = pl.BlockSpec((S, LANES), lambda hp: (0, hp))
    return pl.pallas_call(
        body,
        grid=(4,),
        in_specs=[gcol(0), gcol(4), gcol(8), pl.BlockSpec((3, 2, TQ, w), lambda hp: (0, hp, 0, 0)),
                  pl.BlockSpec((TQ, w), lambda hp: (0, 0)), row, row, row],
        out_specs=[pl.BlockSpec((3, None, SP, LANES), lambda hp: (0, hp, 0, 0)),
                   pl.BlockSpec((2, 8, LANES), lambda hp: (hp, 0, 0))],
        out_shape=[
            jax.ShapeDtypeStruct((3, 4, SP, LANES), bf16),
            jax.ShapeDtypeStruct((8, 8, LANES), f32),
        ],
        scratch_shapes=([pltpu.VMEM((2 * TQ, w), f32)] + [pltpu.VMEM((2, 2 * TQ, w), f32)] * 4
                        + [pltpu.VMEM((2, 2 * TQ, w), bf16)] * 4 + [pltpu.VMEM((SP, LANES), f32)] * 2),
        compiler_params=_params(("arbitrary",), vmem_mib=56),
        name=name,
    )(gl, gl, gl, bias, bucket, do, lse, dd)


def _sigmoid(z):
    return 1.0 / (1.0 + jnp.exp(-z))


def _tail(x, tgt, o_a, l_a, o_b, l_b, proj, bm, w_a, w_b, w_o, sink_b):
    ts = 256

    def body(x_ref, t_ref, oa_ref, la_ref, ob0_ref, ob1_ref, ob2_ref, lb0_ref, lb1_ref, lb2_ref,
             ga_ref, gb_ref, m0_ref, m1_ref, bm_ref, wa_ref, wb_ref, wo_ref, sk_ref,
             dy_ref, dyb_ref, dt_ref, doa_ref, dda_ref, dob0_ref, dob1_ref, dob2_ref, ddb0_ref, ddb1_ref, ddb2_ref,
             ya_ref, yb_ref, mg_ref, dbra_ref, dbrb_ref, loss_ref, dbm_ref, dsk_ref):
        i = pl.program_id(0)

        @pl.when(i == 0)
        def _():
            loss_ref[...] = jnp.zeros_like(loss_ref)
            dbm_ref[...] = jnp.zeros_like(dbm_ref)
            dsk_ref[...] = jnp.zeros_like(dsk_ref)

        ga = ga_ref[...]
        sa = _sigmoid(ga)
        silu_a = ga * sa
        oa = oa_ref[...]
        ya = oa * silu_a
        gb = gb_ref[...]
        sb = _sigmoid(gb)
        silu_b = gb * sb
        ob = [ob0_ref[...], ob1_ref[...], ob2_ref[...]]
        lb = [lb0_ref[...], lb1_ref[...], lb2_ref[...]]
        mx = jnp.maximum(jnp.maximum(lb[0], lb[1]), lb[2])
        ex = [jnp.exp(v - mx) for v in lb]
        den = ex[0] + ex[1] + ex[2]
        alpha = [e / den for e in ex]
        ybc = alpha[0] * ob[0] + alpha[1] * ob[1] + alpha[2] * ob[2]
        yb = ybc * silu_b
        yab = ya.astype(bf16)
        ybb = yb.astype(bf16)
        br_a = jnp.dot(yab, wa_ref[...], preferred_element_type=f32)
        br_b = jnp.dot(ybb, wb_ref[...], preferred_element_type=f32)
        g0 = _sigmoid(m0_ref[...] + bm_ref[0:1, :])
        g1 = _sigmoid(m1_ref[...] + bm_ref[1:2, :])
        merged = g0 * br_a + g1 * br_b
        mgb = merged.astype(bf16)
        y = x_ref[...] + jnp.dot(mgb, wo_ref[...], preferred_element_type=f32)
        err = y - t_ref[...]
        part = jnp.sum(jnp.sum(err * err, axis=1, keepdims=True), axis=0, keepdims=True)
        loss_ref[...] += part * (0.5 / D)
        dy = err * (1.0 / D)
        dyb = dy.astype(bf16)
        dmerged = lax.dot_general(dyb, wo_ref[...], NT_DIMS, preferred_element_type=f32)
        dbr_a = (dmerged * g0).astype(bf16)
        dbr_b = (dmerged * g1).astype(bf16)
        dm0 = dmerged * br_a * (g0 * (1.0 - g0))
        dm1 = dmerged * br_b * (g1 * (1.0 - g1))
        dbm_ref[0:1, :] += jnp.sum(dm0, axis=0, keepdims=True)
        dbm_ref[1:2, :] += jnp.sum(dm1, axis=0, keepdims=True)
        dya = lax.dot_general(dbr_a, wa_ref[...], NT_DIMS, preferred_element_type=f32)
        dyb2 = lax.dot_general(dbr_b, wb_ref[...], NT_DIMS, preferred_element_type=f32)
        do_a = dya * silu_a
        dga = dya * oa * (sa * (1.0 + ga * (1.0 - sa)))
        ones = _head_ones()
        delta_a = _seg_sum(do_a * oa, ones)
        dsk_ref[...] -= jnp.sum(delta_a * jnp.exp(sk_ref[...] - la_ref[...]), axis=0, keepdims=True)
        dybc = dyb2 * silu_b
        dgb = dyb2 * ybc * (sb * (1.0 + gb * (1.0 - sb)))
        dbar = _seg_sum(dybc * ybc, ones)
        dy_ref[...] = dy
        dyb_ref[...] = dyb
        dt_ref[:, 0:512] = dga.astype(bf16)
        dt_ref[:, 512:1024] = dgb.astype(bf16)
        dt_ref[:, 1024:2048] = dm0.astype(bf16)
        dt_ref[:, 2048:3072] = dm1.astype(bf16)
        doa_ref[...] = do_a.astype(bf16)
        dda_ref[...] = delta_a
        for k, (dob_ref, ddb_ref) in enumerate(((dob0_ref, ddb0_ref), (dob1_ref, ddb1_ref), (dob2_ref, ddb2_ref))):
            dob_ref[...] = alpha[k] * dybc
            ddb_ref[...] = alpha[k] * dbar
        ya_ref[...] = ya.T.astype(bf16)
        yb_ref[...] = yb.T.astype(bf16)
        mg_ref[...] = merged.T.astype(bf16)
        dbra_ref[...] = dbr_a
        dbrb_ref[...] = dbr_b

    def rows(n, blk=0):
        return pl.BlockSpec((ts, n), lambda i: (i, blk))

    def whole(r, c):
        return pl.BlockSpec((r, c), lambda i: (0, 0))

    def cols(n):
        return pl.BlockSpec((n, ts), lambda i: (0, i))

    def gate_cols(n, col):
        return pl.BlockSpec((pl.Element(ts), pl.Element(n)), lambda i: (i * ts, NA + col))

    outs = [
        ((S, D), f32, rows(D)), ((S, D), bf16, rows(D)), ((S, NW), bf16, gate_cols(NT, 0)),
        ((S, 512), bf16, rows(512)), ((S, 512), f32, rows(512)),
        ((S, 512), f32, rows(512)), ((S, 512), f32, rows(512)), ((S, 512), f32, rows(512)),
        ((S, 512), f32, rows(512)), ((S, 512), f32, rows(512)), ((S, 512), f32, rows(512)),
        ((512, S), bf16, cols(512)), ((512, S), bf16, cols(512)), ((D, S), bf16, cols(D)),
        ((S, D), bf16, rows(D)), ((S, D), bf16, rows(D)),
        ((1, 1), f32, whole(1, 1)), ((2, D), f32, whole(2, D)), ((1, 512), f32, whole(1, 512)),
    ]
    return pl.pallas_call(
        body,
        grid=(S // ts,),
        in_specs=[
            rows(D), rows(D), rows(512), rows(512), rows(512), rows(512), rows(512), rows(512), rows(512), rows(512),
            gate_cols(512, 0), gate_cols(512, 512), gate_cols(D, 1024), gate_cols(D, 2048), whole(2, D),
            whole(512, D), whole(512, D), whole(D, D), whole(1, 512),
        ],
        out_specs=[o[2] for o in outs],
        out_shape=[jax.ShapeDtypeStruct(o[0], o[1]) for o in outs],
        compiler_params=_params(("arbitrary",), vmem_mib=60),
        name="tail",
    )(x, tgt, o_a, l_a, *o_b, *l_b, proj, proj, proj, proj, bm, w_a, w_b, w_o, sink_b)


def _norm_bwd(xv, dyv, gain, ones):
    r = lax.rsqrt(_half_sums(xv * xv, ones) * (1.0 / HD) + EPS)
    yv = xv * r
    u = dyv * gain
    dxv = r * (u - yv * (_half_sums(u * yv, ones) * (1.0 / HD)))
    return dxv, jnp.sum(dyv * yv, axis=0, keepdims=True)


def _post_b(g, dqkv, proj_a, gains, dproj):
    d = GROUPS[g][1]
    seq = S // d

    def body(d_ref, pa_ref, pb_ref, g_ref, alias_ref, o_ref, dg_ref, nat_a, nat_b):
        del alias_ref
        pj = pl.program_id(0)
        kind = pj // 2
        q_scale = jnp.where(kind == 0, SCALE, 1.0)
        gain = g_ref[...] * q_scale
        ones = _head_ones()

        @pl.when(pj % 2 == 0)
        def _():
            dg_ref[...] = jnp.zeros_like(dg_ref)

        def columns(with_norm):
            for u, (p_ref, nat) in enumerate(((pa_ref, nat_a), (pb_ref, nat_b))):
                for c in range(d):
                    for i in range(seq // PCHUNK):
                        src = c * seq + i * PCHUNK
                        if d == 1:
                            idx = slice(src, src + PCHUNK)
                        else:
                            idx = pl.ds(c + i * PCHUNK * d, PCHUNK, stride=d)
                        dyv = d_ref[u, PAD + src:PAD + src + PCHUNK, :].astype(f32)
                        if with_norm:
                            dyv, dg = _norm_bwd(p_ref[idx, :], dyv, gain, ones)
                            dg_ref[...] += dg * q_scale
                        nat[idx, :] = dyv
                for i in range(S // CHUNK):
                    rows = slice(i * CHUNK, (i + 1) * CHUNK)
                    o_ref[rows, u * LANES:(u + 1) * LANES] = nat[rows, :].astype(bf16)

        pl.when(kind < 2)(lambda: columns(True))
        pl.when(kind == 2)(lambda: columns(False))

    def pcol(u):
        return pl.BlockSpec((S, LANES), lambda pj: (0, _col_block(g, 2 * jnp.minimum(pj, 3) + u)))

    return pl.pallas_call(
        body,
        grid=(6,),
        in_specs=[
            pl.BlockSpec((None, 2, SP, LANES), lambda pj: (pj // 2, pj % 2, 0, 0)),
            pcol(0), pcol(1),
            pl.BlockSpec((None, None, 1, LANES), lambda pj: (g, pj // 2, 0, 0)),
            pl.BlockSpec(memory_space=pl.ANY),
        ],
        out_specs=[
            pl.BlockSpec((S, 2 * LANES), lambda pj: (0, _col_block(g, 2 * pj) // 2)),
            pl.BlockSpec((None, 1, LANES), lambda pj: (pj // 2, 0, 0)),
        ],
        out_shape=[jax.ShapeDtypeStruct((S, NW), bf16), jax.ShapeDtypeStruct((3, 1, LANES), f32)],
        scratch_shapes=[pltpu.VMEM((S, LANES), f32), pltpu.VMEM((S, LANES), f32)],
        input_output_aliases={4: 0},
        compiler_params=_params(("arbitrary",)),
        name="post_b%d" % g,
    )(dqkv, proj_a, proj_a, gains, dproj)


def _post_a(dqkv, proj_a, gains, dproj):
    def body(q_ref, e_ref, p_ref, g_ref, alias_ref, o_ref, dg_ref):
        del alias_ref
        j = pl.program_id(0)
        q_scale = jnp.where(j < 4, SCALE, 1.0)
        gain = g_ref[...] * q_scale
        lo = _lo()
        ones = _head_ones()

        @pl.when((j == 0) | (j >= 4))
        def _():
            dg_ref[...] = jnp.zeros_like(dg_ref)

        def column(folded, with_norm):
            for i in range(S // PCHUNK):
                r0 = i * PCHUNK
                rows = slice(PAD + r0, PAD + r0 + PCHUNK)
                if folded:
                    t0 = e_ref[0, rows, :].astype(f32) + e_ref[1, rows, :].astype(f32)
                    t1 = e_ref[2, rows, :].astype(f32) + e_ref[3, rows, :].astype(f32)
                    dyv = jnp.where(lo, t0 + pltpu.roll(t0, HD, 1), t1 + pltpu.roll(t1, HD, 1))
                else:
                    dyv = q_ref[rows, :].astype(f32)
                if with_norm:
                    dyv, dg = _norm_bwd(p_ref[r0:r0 + PCHUNK, :], dyv, gain, ones)
                    dg_ref[...] += dg * q_scale
                o_ref[r0:r0 + PCHUNK, :] = dyv.astype(bf16)

        pl.when(j < 4)(lambda: column(False, True))
        pl.when(j == 4)(lambda: column(True, True))
        pl.when(j == 5)(lambda: column(True, False))

    return pl.pallas_call(
        body,
        grid=(6,),
        in_specs=[
            pl.BlockSpec((None, None, SP, LANES), lambda j: (0, jnp.minimum(j, 3), 0, 0)),
            pl.BlockSpec((None, 4, SP, LANES), lambda j: (jnp.clip(j - 3, 1, 2), 0, 0, 0)),
            pl.BlockSpec((S, LANES), lambda j: (0, jnp.minimum(j, 4))),
            pl.BlockSpec((None, None, 1, LANES), lambda j: (0, jnp.maximum(j - 3, 0), 0, 0)),
            pl.BlockSpec(memory_space=pl.ANY),
        ],
        out_specs=[
            pl.BlockSpec((S, LANES), lambda j: (0, j)),
            pl.BlockSpec((None, 1, LANES), lambda j: (jnp.maximum(j - 3, 0), 0, 0)),
        ],
        out_shape=[jax.ShapeDtypeStruct((S, NW), bf16), jax.ShapeDtypeStruct((3, 1, LANES), f32)],
        input_output_aliases={4: 0},
        compiler_params=_params(("arbitrary",)),
        name="post_a",
    )(dqkv, dqkv, proj_a, gains, dproj)


def _dh_norm_bwd(dproj, w, x, rstd, gain, dy):
    ts = 1024
    tk = NW // 3
    nk = NW // tk

    def body(d_ref, w_ref, x_ref, r_ref, g_ref, dy_ref, gx_ref, dgn_ref, acc):
        i = pl.program_id(0)
        k = pl.program_id(1)

        @pl.when((i == 0) & (k == 0))
        def _():
            dgn_ref[...] = jnp.zeros_like(dgn_ref)

        @pl.when(k == 0)
        def _():
            acc[...] = jnp.zeros_like(acc)

        acc[...] += jnp.dot(d_ref[...], w_ref[...], preferred_element_type=f32)

        @pl.when(k == nk - 1)
        def _():
            dh = acc[...]
            xh = x_ref[...] * r_ref[...]
            u = dh * g_ref[...]
            dx = r_ref[...] * (u - xh * jnp.mean(u * xh, axis=-1, keepdims=True))
            gx_ref[...] = dy_ref[...] + dx
            dgn_ref[...] += jnp.sum(dh * xh, axis=0, keepdims=True)

    return pl.pallas_call(
        body,
        grid=(S // ts, nk),
        in_specs=[
            pl.BlockSpec((ts, tk), lambda i, k: (i, k)),
            pl.BlockSpec((tk, D), lambda i, k: (k, 0)),
            pl.BlockSpec((ts, D), lambda i, k: (i, 0)),
            pl.BlockSpec((ts, 1), lambda i, k: (i, 0)),
            pl.BlockSpec((1, D), lambda i, k: (0, 0)),
            pl.BlockSpec((ts, D), lambda i, k: (i, 0)),
        ],
        out_specs=[pl.BlockSpec((ts, D), lambda i, k: (i, 0)), pl.BlockSpec((1, D), lambda i, k: (0, 0))],
        out_shape=[jax.ShapeDtypeStruct((S, D), f32), jax.ShapeDtypeStruct((1, D), f32)],
        scratch_shapes=[pltpu.VMEM((ts, D), f32)],
        compiler_params=_params(("arbitrary", "arbitrary"), vmem_mib=60),
        name="dh_norm_bwd",
    )(dproj, w, x, rstd, gain, dy)


def _dw_in(hbt, dproj, parity, name):
    tk = S
    win = WSH + 96

    def body(par_ref, a_ref, b_ref, o_ref, acc):
        p = 2 * pl.program_id(0) + par_ref[0]
        k = pl.program_id(1)

        @pl.when(k == 0)
        def _():
            acc[...] = jnp.zeros_like(acc)

        acc[...] += jnp.dot(a_ref[...], b_ref[...], preferred_element_type=f32)

        @pl.when(k == S // tk - 1)
        def _():
            acc_t = acc[...].T
            for pp in range(NDEV):
                off = (WSH * pp) % LANES

                @pl.when(p == pp)
                def _():
                    o_ref[...] = acc_t[off:off + WSH, :].astype(bf16)

    return pl.pallas_call(
        body,
        grid_spec=pltpu.PrefetchScalarGridSpec(
            num_scalar_prefetch=1,
            grid=(NDEV // 2, S // tk),
            in_specs=[
                pl.BlockSpec((D, tk), lambda q, k, par: (0, k)),
                pl.BlockSpec((pl.Element(tk), pl.Element(win)),
                             lambda q, k, par: (k * tk, (WSH * (2 * q + par[0])) // LANES * LANES)),
            ],
            out_specs=pl.BlockSpec((None, WSH, D), lambda q, k, par: (q, 0, 0)),
            scratch_shapes=[pltpu.VMEM((D, win), f32)],
        ),
        out_shape=jax.ShapeDtypeStruct((NDEV // 2, WSH, D), bf16),
        compiler_params=_params(("arbitrary", "arbitrary")),
        name=name,
    )(parity, hbt, dproj)


def _matmul_tokens(at, b, name):
    m, n = at.shape[0], b.shape[1]
    tn = 1024
    tk = 2048

    def body(a_ref, b_ref, o_ref):
        @pl.when(pl.program_id(1) == 0)
        def _():
            o_ref[...] = jnp.zeros_like(o_ref)

        o_ref[...] += jnp.dot(a_ref[...], b_ref[...], preferred_element_type=f32)

    return pl.pallas_call(
        body,
        grid=(n // tn, S // tk),
        in_specs=[pl.BlockSpec((m, tk), lambda j, k: (0, k)), pl.BlockSpec((tk, tn), lambda j, k: (k, j))],
        out_specs=pl.BlockSpec((m, tn), lambda j, k: (0, j)),
        out_shape=jax.ShapeDtypeStruct((m, n), f32),
        compiler_params=_params(("arbitrary", "arbitrary")),
        name=name,
    )(at, b)


def _exchange(scatter, gather, name):
    arrs = list(scatter) + list(gather)
    n = len(arrs)
    ns = len(scatter)

    def body(*refs):
        ins, outs = refs[:n], refs[n:2 * n]
        send_sems, recv_sems, local_sems = refs[2 * n:]
        x, y, c = lax.axis_index("x"), lax.axis_index("y"), lax.axis_index("c")
        me = 4 * x + 2 * y + c
        local, remote = [], []
        for a in range(n):
            lc = pltpu.make_async_copy(ins[a].at[me] if a < ns else ins[a], outs[a].at[me], local_sems.at[a])
            lc.start()
            local.append(lc)
            for r in range(1, NDEV):
                px = 1 - x if r & 4 else x
                py = 1 - y if r & 2 else y
                pc = 1 - c if r & 1 else c
                cp = pltpu.make_async_remote_copy(
                    src_ref=ins[a].at[4 * px + 2 * py + pc] if a < ns else ins[a],
                    dst_ref=outs[a].at[me],
                    send_sem=send_sems.at[a, r - 1],
                    recv_sem=recv_sems.at[a, r - 1],
                    device_id=(px, py, pc),
                    device_id_type=pl.DeviceIdType.MESH,
                )
                cp.start()
                remote.append(cp)
        for cp in remote:
            cp.wait_recv()
        for cp in remote:
            cp.wait_send()
        for lc in local:
            lc.wait()

    out_shape = [jax.ShapeDtypeStruct(a.shape if i < ns else (NDEV,) + a.shape, a.dtype) for i, a in enumerate(arrs)]
    return pl.pallas_call(
        body,
        in_specs=[pl.BlockSpec(memory_space=pl.ANY)] * n,
        out_specs=[pl.BlockSpec(memory_space=pl.ANY)] * n,
        out_shape=out_shape,
        scratch_shapes=[
            pltpu.SemaphoreType.DMA((n, NDEV - 1)),
            pltpu.SemaphoreType.DMA((n, NDEV - 1)),
            pltpu.SemaphoreType.DMA((n,)),
        ],
        compiler_params=pltpu.CompilerParams(has_side_effects=True),
        name=name,
    )(*arrs)


_HBM = pl.BlockSpec(memory_space=pltpu.HBM)
_SEM = pl.BlockSpec(memory_space=pltpu.SEMAPHORE)
_EFFECT = pltpu.SideEffectType.DATAFLOW_SIDE_EFFECTING


def _comm_step(name, body_fn, lands, srcs=(), wait_sems=(), n_new=0, after=(), token=False):
    n, ns, nw, na = len(lands), len(srcs), len(wait_sems), len(after)

    def body(*refs):
        src, land = refs[:ns], refs[ns:ns + n]
        waits = refs[ns + n:ns + n + nw]
        new = refs[ns + n + nw + na:ns + n + nw + na + n_new]
        body_fn(src, land, waits, new)
        if token:
            refs[-1][...] = jnp.zeros((8, LANES), f32)

    hbm = [pltpu.HBM(a.shape, a.dtype) for a in lands]
    ops = [pltpu.with_memory_space_constraint(a, pltpu.HBM) for a in list(srcs) + list(lands)]
    extra_shape = [jax.ShapeDtypeStruct((8, LANES), f32)] if token else []
    extra_spec = [pl.BlockSpec(memory_space=pltpu.VMEM)] if token else []
    outs = pl.pallas_call(
        body,
        out_shape=tuple([pltpu.SemaphoreType.DMA(())] * n_new + hbm + extra_shape),
        in_specs=[_HBM] * (ns + n) + [_SEM] * nw + [pl.BlockSpec(memory_space=pl.ANY)] * na,
        out_specs=tuple([_SEM] * n_new + [_HBM] * n + extra_spec),
        input_output_aliases={ns + i: n_new + i for i in range(n)},
        compiler_params=pltpu.CompilerParams(has_side_effects=_EFFECT),
        name=name,
    )(*ops, *wait_sems, *after)
    if token:
        return list(outs[:n_new]), list(outs[n_new:n_new + n]), outs[-1][0, 0]
    return list(outs[:n_new]), list(outs[n_new:])


class _GatheredWeights:
    def __init__(self, shards):
        self.n = n = len(shards)
        x, y, c = lax.axis_index("x"), lax.axis_index("y"), lax.axis_index("c")
        self.x = x
        me = 4 * x + 2 * y + c
        lands = [lax.dynamic_update_slice(lax.empty((NDEV,) + s.shape, s.dtype), s[None], (me,) + (0,) * s.ndim)
                 for s in shards]

        def start_own(src, land, waits, new):
            p = self._peers()
            for a in range(n):
                for k, to in ((0, p["sibling"]), (1, p["xn"]), (2, p["yn"])):
                    self._copy(land[a], new, a, k, 3, p["me"], to).start()

        self.sems, self.lands = {}, None
        new, self.lands = _comm_step("gather_start", start_own, lands, n_new=6 * n)
        self._keep(new, (0, 1, 2))

    @staticmethod
    def _peers():
        x, y, c = lax.axis_index("x"), lax.axis_index("y"), lax.axis_index("c")
        return dict(
            me=(x, y, c), sibling=(x, y, 1 - c), xn=(1 - x, y, c), yn=(x, 1 - y, c), dg=(1 - x, 1 - y, c),
            relay_origin=(jnp.bitwise_xor(x, c), jnp.bitwise_xor(y, 1 - c), c),
            relay_target=(jnp.bitwise_xor(x, 1 - c), jnp.bitwise_xor(y, c), c))

    def _keep(self, new, ks):
        half = len(new) // 2
        i = 0
        for a in range(self.n):
            for k in ks:
                self.sems[a, k] = (new[i], new[half + i])
                i += 1

    @staticmethod
    def _copy(land, sem_refs, a, k, nk, block, to, src=None, ks=None):
        ks = tuple(range(nk)) if ks is None else ks
        half = len(sem_refs) // 2
        i = a * len(ks) + ks.index(k)
        slot = land.at[4 * block[0] + 2 * block[1] + block[2]]
        return pltpu.make_async_remote_copy(
            src_ref=slot if src is None else src, dst_ref=slot, send_sem=sem_refs[i], recv_sem=sem_refs[half + i],
            device_id=to, device_id_type=pl.DeviceIdType.MESH)

    def _sem_list(self, ks):
        return ([self.sems[a, k][0] for a in range(self.n) for k in ks]
                + [self.sems[a, k][1] for a in range(self.n) for k in ks])

    def first_half(self, after):
        n = self.n

        def relay(src, land, waits, new):
            p = self._peers()
            for a in range(n):
                self._copy(land[a], waits, a, 1, 0, p["xn"], p["me"], ks=(1, 2)).wait_recv()
                self._copy(land[a], waits, a, 2, 0, p["yn"], p["me"], ks=(1, 2)).wait_recv()
                self._copy(land[a], new, a, 3, 0, p["relay_origin"], p["relay_target"], ks=(3, 4, 5)).start()
                self._copy(land[a], new, a, 4, 0, p["xn"], p["sibling"], ks=(3, 4, 5)).start()
                self._copy(land[a], new, a, 5, 0, p["yn"], p["sibling"], ks=(3, 4, 5)).start()

        new, self.lands = _comm_step("gather_relay", relay, self.lands, wait_sems=self._sem_list((1, 2)),
                                     n_new=6 * n, after=after)
        self._keep(new, (3, 4, 5))

        def from_sibling(src, land, waits, new):
            p = self._peers()
            other = lambda b: (b[0], b[1], 1 - b[2])
            for a in range(n):
                self._copy(land[a], waits, a, 0, 0, other(p["me"]), p["me"], ks=(0, 4, 5)).wait_recv()
                self._copy(land[a], waits, a, 4, 0, other(p["xn"]), p["me"], ks=(0, 4, 5)).wait_recv()
                self._copy(land[a], waits, a, 5, 0, other(p["yn"]), p["me"], ks=(0, 4, 5)).wait_recv()

        _, self.lands = _comm_step("gather_wait_sibling", from_sibling, self.lands,
                                   wait_sems=self._sem_list((0, 4, 5)))
        return self.lands[0].reshape(NW, D), self.x.astype(jnp.int32).reshape(1)

    def second_half(self, after):
        n = self.n

        def forward_diagonal(src, land, waits, new):
            p = self._peers()
            for a in range(n):
                self._copy(land[a], waits, a, 3, 0, p["dg"], p["me"], ks=(3,)).wait_recv()
                self._copy(land[a], new, a, 6, 0, p["dg"], p["sibling"], ks=(6,)).start()

        new, self.lands = _comm_step("gather_forward_diagonal", forward_diagonal, self.lands,
                                     wait_sems=self._sem_list((3,)), n_new=2 * n, after=after)
        self._keep(new, (6,))

        def finish(src, land, waits, new):
            p = self._peers()
            ks = tuple(range(7))
            for a in range(n):
                self._copy(land[a], waits, a, 6, 0, (p["dg"][0], p["dg"][1], 1 - p["dg"][2]), p["me"], ks=ks).wait_recv()
                for k in ks:
                    self._copy(land[a], waits, a, k, 0, p["me"], p["me"], ks=ks).wait_send()

        _, self.lands = _comm_step("gather_finish", finish, self.lands, wait_sems=self._sem_list(tuple(range(7))))
        return self.lands[0].reshape(NW, D), (1 - self.x).astype(jnp.int32).reshape(1)

    def rest(self):
        g_a, g_b, g_o, g_bm = self.lands[1:]
        return (g_a.transpose(1, 0, 2).reshape(512, D), g_b.transpose(1, 0, 2).reshape(512, D),
                g_bm.transpose(1, 0, 2).reshape(2, D), g_o.reshape(D, D))


def _sibling_send_start(shares):
    landing = lax.empty(shares.shape, shares.dtype)

    def start(src, land, waits, new):
        x, y, c = lax.axis_index("x"), lax.axis_index("y"), lax.axis_index("c")
        pltpu.make_async_remote_copy(src_ref=land[0], dst_ref=land[1], send_sem=new[0], recv_sem=new[1],
                                     device_id=(x, y, 1 - c), device_id_type=pl.DeviceIdType.MESH).start()

    return _comm_step("grad_sibling_start", start, [shares, landing], n_new=2, token=True)


def _sibling_send_wait(sems, lands, after):
    def wait(src, land, waits, new):
        x, y, c = lax.axis_index("x"), lax.axis_index("y"), lax.axis_index("c")
        done = pltpu.make_async_remote_copy(src_ref=land[0], dst_ref=land[1], send_sem=waits[0], recv_sem=waits[1],
                                            device_id=(x, y, c), device_id_type=pl.DeviceIdType.MESH)
        done.wait_send()
        done.wait_recv()

    _, lands = _comm_step("grad_sibling_wait", wait, lands, wait_sems=sems, after=after)
    return lands[1]


def _row_tile(rows, limit=256):
    fits = [t for t in range(16, limit + 1, 16) if rows % t == 0]
    return fits[-1] if fits else rows


def _pair_sum(mine, theirs, name):
    nb, rows, cols = mine.shape
    tr = _row_tile(rows, 528)

    def body(a_ref, b_ref, o_ref):
        o_ref[...] = (a_ref[...].astype(f32) + b_ref[...].astype(f32)).astype(bf16)

    blk = pl.BlockSpec((None, tr, cols), lambda q, i: (q, i, 0))
    return pl.pallas_call(
        body,
        grid=(nb, rows // tr),
        in_specs=[blk, blk],
        out_specs=blk,
        out_shape=jax.ShapeDtypeStruct(mine.shape, bf16),
        compiler_params=_params(("arbitrary", "arbitrary")),
        name=name,
    )(mine, theirs)


def _scatter_start(chip_arrs, all_arrs, name):
    arrs = list(chip_arrs) + list(all_arrs)
    n, nc = len(arrs), len(chip_arrs)
    lands = [lax.empty(((3 if i < nc else NDEV - 1),) + a.shape[1:], a.dtype) for i, a in enumerate(arrs)]

    def body(*refs):
        src, land = refs[:n], refs[n:2 * n]
        send_sems, recv_sems = refs[2 * n:3 * n], refs[3 * n:4 * n]
        token = refs[6 * n]
        x, y, c = lax.axis_index("x"), lax.axis_index("y"), lax.axis_index("c")
        for a in range(n):
            for r in range(1, 4 if a < nc else NDEV):
                if a < nc:
                    px, py, pc = (1 - x if r & 2 else x), (1 - y if r & 1 else y), c
                    block = 2 * px + py
                else:
                    px, py, pc = (1 - x if r & 4 else x), (1 - y if r & 2 else y), (1 - c if r & 1 else c)
                    block = 4 * px + 2 * py + pc
                pltpu.make_async_remote_copy(
                    src_ref=src[a].at[block], dst_ref=land[a].at[r - 1], send_sem=send_sems[a],
                    recv_sem=recv_sems[a], device_id=(px, py, pc), device_id_type=pl.DeviceIdType.MESH).start()
        token[...] = jnp.zeros_like(token)

    hbm = [pltpu.HBM(a.shape, a.dtype) for a in arrs + lands]
    ops = [pltpu.with_memory_space_constraint(a, pltpu.HBM) for a in arrs + lands]
    outs = pl.pallas_call(
        body,
        out_shape=tuple([pltpu.SemaphoreType.DMA(())] * (2 * n) + hbm + [jax.ShapeDtypeStruct((8, LANES), f32)]),
        in_specs=[_HBM] * (2 * n),
        out_specs=tuple([_SEM] * (2 * n) + [_HBM] * (2 * n) + [pl.BlockSpec(memory_space=pltpu.VMEM)]),
        input_output_aliases={i: 2 * n + i for i in range(2 * n)},
        compiler_params=pltpu.CompilerParams(has_side_effects=_EFFECT),
        name=name,
    )(*ops)
    return outs[:n], outs[n:2 * n], outs[2 * n:3 * n], outs[3 * n:4 * n], outs[4 * n]


def _scatter_wait(send_sems, recv_sems, srcs, lands, after, name):
    n = len(srcs)

    def body(*refs):
        land = refs[n:2 * n]
        ssem, rsem = refs[2 * n:3 * n], refs[3 * n:4 * n]
        x, y, c = lax.axis_index("x"), lax.axis_index("y"), lax.axis_index("c")
        for a in range(n):
            done = pltpu.make_async_remote_copy(
                src_ref=land[a], dst_ref=land[a], send_sem=ssem[a], recv_sem=rsem[a], device_id=(x, y, c),
                device_id_type=pl.DeviceIdType.MESH)
            done.wait_send()
            done.wait_recv()

    hbm = [pltpu.HBM(a.shape, a.dtype) for a in list(srcs) + list(lands)]
    outs = pl.pallas_call(
        body,
        out_shape=tuple(hbm),
        in_specs=[_HBM] * (2 * n) + [_SEM] * (2 * n) + [pl.BlockSpec(memory_space=pl.ANY)],
        out_specs=tuple([_HBM] * (2 * n)),
        input_output_aliases={i: i for i in range(2 * n)},
        compiler_params=pltpu.CompilerParams(has_side_effects=_EFFECT),
        name=name,
    )(*srcs, *lands, *send_sems, *recv_sems, after)
    return outs[:n], outs[n:]


def _adam_update(g, w_ref, m_ref, v_ref, g_ref, d_ref, nm_ref, nv_ref):
    mm = ADAM_B1 * m_ref[...] + (1.0 - ADAM_B1) * g
    vv = ADAM_B2 * v_ref[...] + (1.0 - ADAM_B2) * (g * g)
    m_hat = mm / (1.0 - ADAM_B1 ** ADAM_STEP)
    v_hat = vv / (1.0 - ADAM_B2 ** ADAM_STEP)
    g_ref[...] = g
    d_ref[...] = -ADAM_LR * (m_hat / (jnp.sqrt(v_hat) + ADAM_EPS) + ADAM_WD * w_ref[...])
    nm_ref[...] = mm
    nv_ref[...] = vv


def _adamw_own(w, own, own_idx, slots, m, v, name):
    r, c = w.shape[-2:]
    tr = _row_tile(r, 384)
    k = slots.shape[0]

    def body(i_ref, w_ref, o_ref, s_ref, m_ref, v_ref, g_ref, d_ref, nm_ref, nv_ref):
        del i_ref
        g = o_ref[...].astype(f32)
        for j in range(k):
            g = g + s_ref[j].astype(f32)
        _adam_update(g, w_ref, m_ref, v_ref, g_ref, d_ref, nm_ref, nv_ref)

    blk = pl.BlockSpec((None, tr, c), lambda i, ix: (0, i, 0))
    return pl.pallas_call(
        body,
        grid_spec=pltpu.PrefetchScalarGridSpec(
            num_scalar_prefetch=1,
            grid=(r // tr,),
            in_specs=[blk, pl.BlockSpec((None, tr, c), lambda i, ix: (ix[0], i, 0)),
                      pl.BlockSpec((k, tr, c), lambda i, ix: (0, i, 0)), blk, blk],
            out_specs=[blk] * 4,
        ),
        out_shape=[jax.ShapeDtypeStruct(w.shape, f32)] * 4,
        compiler_params=_params(("arbitrary",)),
        name=name,
    )(own_idx, w, own, slots, m, v)


def _adamw(w, slots, m, v, name):
    r, c = w.shape[-2:]
    tr = _row_tile(r, 128)

    def body(w_ref, s_ref, m_ref, v_ref, g_ref, d_ref, nm_ref, nv_ref):
        g = s_ref[0].astype(f32)
        for k in range(1, NDEV):
            g = g + s_ref[k].astype(f32)
        _adam_update(g, w_ref, m_ref, v_ref, g_ref, d_ref, nm_ref, nv_ref)

    if w.ndim == 3:
        blk = pl.BlockSpec((None, tr, c), lambda i: (0, i, 0))
    else:
        blk = pl.BlockSpec((tr, c), lambda i: (i, 0))
    return pl.pallas_call(
        body,
        grid=(r // tr,),
        in_specs=[blk, pl.BlockSpec((NDEV, tr, c), lambda i: (0, i, 0)), blk, blk],
        out_specs=[blk] * 4,
        out_shape=[jax.ShapeDtypeStruct(w.shape, f32)] * 4,
        compiler_params=_params(("arbitrary",)),
        name=name,
    )(w, slots, m, v)


class _Weights:
    def __init__(self, w_t, w_a, w_b, b_merge, w_o):
        self._w_t, self._rest = w_t, (w_a, w_b, b_merge, w_o)

    def first_half(self, after):
        del after
        return self._w_t, jnp.zeros((1,), jnp.int32)

    def second_half(self, after):
        del after
        return self._w_t, jnp.ones((1,), jnp.int32)

    def rest(self):
        return self._rest


def _local_step(x, tgt, norm_gain, weights, qn_a, kn_a, qn_b, kn_b, sink_a, rel_bias, on_weight_grads=None,
                core=None):
    two = lambda t: jnp.concatenate([t, t], axis=-1).reshape(1, LANES)
    ones = jnp.ones((1, LANES), f32)
    gains = jnp.stack([
        jnp.stack([two(qn_a), two(kn_a), ones]),
        jnp.stack([two(qn_b), two(kn_b), ones]),
        jnp.stack([two(qn_b), two(kn_b), ones]),
        jnp.stack([two(qn_b), two(kn_b), ones]),
    ])
    buckets = [jnp.asarray(_bucket_np(blk, d)) for blk, d, _ in GROUPS]
    bias = [_bias_expand(rel_bias, buckets[k], GROUPS[k][2], "bias_expand_%d" % k) for k in range(4)]

    hb, hbt, rstd = _rms(x, norm_gain)
    w_t, half = weights.first_half([hb] + bias)
    proj = _inproj_half(hb, w_t, half, None, "inproj_1")
    w_t, half = weights.second_half([proj])
    proj = _inproj_half(hb, w_t, half, proj, "inproj_2")
    w_a, w_b, b_merge, w_o = weights.rest()
    gl = _prep(proj, gains)
    o_a, l_a = _attn_fwd(gl, bias[0], sink_a.reshape(8), 0, 128, 1, "attn_fwd_a")
    fwd_b = [_attn_fwd(gl, bias[k], None, k, GROUPS[k][0], GROUPS[k][1], "attn_fwd_b%d" % k) for k in (1, 2, 3)]
    sink_b = jnp.repeat(sink_a.reshape(8), HD).reshape(1, 512)

    (dy, dyb, dproj, do_a, dd_a, do_b0, do_b1, do_b2, dd_b0, dd_b1, dd_b2, ya, yb, mg, dbr_a, dbr_b, loss, dbm,
     dsk) = _tail(x, tgt, o_a, l_a, [f[0] for f in fwd_b], [f[1] for f in fwd_b], proj, b_merge, w_a, w_b, w_o, sink_b)

    dw_o = _matmul_tokens(mg, dyb, "dw_out")
    dw_a = _matmul_tokens(ya, dbr_a, "dw_branch_a")
    dw_b = _matmul_tokens(yb, dbr_b, "dw_branch_b")
    if on_weight_grads is not None:
        early = on_weight_grads(dict(w_branch_a=dw_a, w_branch_b=dw_b, b_merge=dbm, w_out=dw_o))
        buckets = [buckets[0] + early.astype(jnp.int32)] + buckets[1:]

    dqkv_a, dbk_a = _attn_bwd(gl, bias[0], buckets[0], do_a, l_a, dd_a, 0, 128, 1, "attn_bwd_a")
    dproj, dg_a = _post_a(dqkv_a, proj, gains, dproj)
    dbk_b, dg_b = [], []
    for k, do_k, dd_k in ((1, do_b0, dd_b0), (2, do_b1, dd_b1), (3, do_b2, dd_b2)):
        dqkv, dbk = _attn_bwd(gl, bias[k], buckets[k], do_k, fwd_b[k - 1][1], dd_k, k, GROUPS[k][0], GROUPS[k][1],
                              "attn_bwd_b%d" % k)
        dproj, dg = _post_b(k, dqkv, proj, gains, dproj)
        dbk_b.append(dbk)
        dg_b.append(dg)
    dg_b = jnp.stack(dg_b)

    core = jnp.zeros((1,), jnp.int32) if core is None else core
    dw_other = _dw_in(hbt, dproj, 1 - core, "dw_in_other")
    sent = jnp.zeros((), f32) if on_weight_grads is None else on_weight_grads(dict(w_in_other=dw_other))
    dw_in = _dw_in(hbt, dproj, core + sent.astype(jnp.int32), "dw_in_own")
    token = jnp.zeros((), f32) if on_weight_grads is None else on_weight_grads(dict(w_in=dw_in))
    grad_x, d_norm_gain = _dh_norm_bwd(dproj, w_t, x, rstd, norm_gain + token, dy)

    fold = lambda t: t[..., :HD] + t[..., HD:]
    d_qn_a = fold(dg_a[0, 0])
    d_kn_a = fold(dg_a[1, 0])
    d_qn_b = fold(dg_b[:, 0, 0].sum(axis=0))
    d_kn_b = fold(dg_b[:, 1, 0].sum(axis=0))
    d_sink = dsk.reshape(8, HD)[:, 0]
    red = jnp.stack([dbk_a] + dbk_b)
    d_rel = red[:, :, 0, :32].reshape(32, 32).T
    return dict(loss=loss, grad_x=grad_x, norm_gain=d_norm_gain, w_in=dw_in, w_in_other=dw_other, q_norm_a=d_qn_a,
                k_norm_a=d_kn_a,
                q_norm_b=d_qn_b, k_norm_b=d_kn_b, sink_a=d_sink, rel_bias=d_rel, w_branch_a=dw_a, w_branch_b=dw_b,
                b_merge=dbm, w_out=dw_o)


SMALL = (("norm_gain", D), ("q_norm_a", HD), ("k_norm_a", HD), ("q_norm_b", HD), ("k_norm_b", HD), ("sink_a", 8),
         ("rel_bias", 1024))
SMALL_PAD = 2432


SMALL_USED = sum(sz for _, sz in SMALL)


def _pack_small(parts, loss=None):
    tail = jnp.zeros((SMALL_PAD - SMALL_USED,), f32)
    if loss is not None:
        tail = tail.at[0].set(loss.reshape(()))
    return jnp.concatenate([parts[n].reshape(-1) for n, _ in SMALL] + [tail]).reshape(1, SMALL_PAD)


def _unpack_small(flat, shapes):
    out, off = {}, 0
    for n, sz in SMALL:
        out[n] = flat[0, off:off + sz].reshape(shapes[n])
        off += sz
    return out


def kernel(x, norm_gain, w_in, q_norm_a, k_norm_a, q_norm_b, k_norm_b, sink_a, rel_bias, w_branch_a, w_branch_b, b_merge, w_out, loss_target, m_norm_gain, m_w_in, m_q_norm_a, m_k_norm_a, m_q_norm_b, m_k_norm_b, m_sink_a, m_rel_bias, m_w_branch_a, m_w_branch_b, m_b_merge, m_w_out, v_norm_gain, v_w_in, v_q_norm_a, v_k_norm_a, v_q_norm_b, v_k_norm_b, v_sink_a, v_rel_bias, v_w_branch_a, v_w_branch_b, v_b_merge, v_w_out):
    csh = D // NDEV
    w_in_t, m_w_in_t, v_w_in_t = (jnp.swapaxes(t, 1, 2) for t in (w_in, m_w_in, v_w_in))
    weights = _GatheredWeights([w_in_t[0].astype(bf16), w_branch_a[0].astype(bf16), w_branch_b[0].astype(bf16),
                                w_out[0].astype(bf16), b_merge[0]])

    pending = {}
    core = lax.axis_index("c").astype(jnp.int32).reshape(1)
    chip = (2 * lax.axis_index("x") + lax.axis_index("y")).astype(jnp.int32).reshape(1)
    me = (2 * chip + core).astype(jnp.int32)

    def start_exchange(gw):
        if "w_in_other" in gw:
            sems, lands, sent = _sibling_send_start(gw["w_in_other"])
            pending["sibling"] = (sems, lands)
            return sent
        if "w_in" in gw:
            from_sibling = _sibling_send_wait(*pending["sibling"], after=[gw["w_in"]])
            chip_sums = _pair_sum(gw["w_in"], from_sibling, "grad_pair_sum")
            pending["w_in"] = _scatter_start([chip_sums], [], "scatter_w_in_start")
            return pending["w_in"][4][0, 0]
        blocks = [gw["w_branch_a"].reshape(512, NDEV, csh).transpose(1, 0, 2).astype(bf16),
                  gw["w_branch_b"].reshape(512, NDEV, csh).transpose(1, 0, 2).astype(bf16),
                  gw["w_out"].reshape(NDEV, csh, D).astype(bf16),
                  gw["b_merge"].reshape(2, NDEV, csh).transpose(1, 0, 2)]
        pending["rest"] = _scatter_start([], blocks, "scatter_rest_start")
        return pending["rest"][4][0, 0]

    loc = _local_step(x[0], loss_target[0], norm_gain, weights, q_norm_a, k_norm_a, q_norm_b, k_norm_b, sink_a,
                      rel_bias, on_weight_grads=start_exchange, core=core)

    small_shapes = dict(norm_gain=(1, D), q_norm_a=(1, HD), k_norm_a=(1, HD), q_norm_b=(1, HD), k_norm_b=(1, HD),
                        sink_a=(1, 8), rel_bias=(32, 32))
    (r_small,) = _exchange([], [_pack_small(loc, loc["loss"])], "gather_small_grads")
    send_sems, recv_sems, srcs, lands, _ = pending["rest"]
    (s_a, s_b, s_o, s_bm), (r_a, r_b, r_o, r_bm) = _scatter_wait(
        send_sems, recv_sems, srcs, lands, r_small, "scatter_rest_wait")
    send_sems, recv_sems, srcs, lands, _ = pending["w_in"]
    (s_in,), (r_in,) = _scatter_wait(send_sems, recv_sems, srcs, lands, r_small, "scatter_w_in_wait")

    given = dict(norm_gain=norm_gain, q_norm_a=q_norm_a, k_norm_a=k_norm_a, q_norm_b=q_norm_b, k_norm_b=k_norm_b,
                 sink_a=sink_a, rel_bias=rel_bias)
    m_small = dict(norm_gain=m_norm_gain, q_norm_a=m_q_norm_a, k_norm_a=m_k_norm_a, q_norm_b=m_q_norm_b,
                   k_norm_b=m_k_norm_b, sink_a=m_sink_a, rel_bias=m_rel_bias)
    v_small = dict(norm_gain=v_norm_gain, q_norm_a=v_q_norm_a, k_norm_a=v_k_norm_a, q_norm_b=v_q_norm_b,
                   k_norm_b=v_k_norm_b, sink_a=v_sink_a, rel_bias=v_rel_bias)
    res = {
        "small": _adamw(_pack_small(given), r_small, _pack_small(m_small), _pack_small(v_small), "adamw_small"),
        "w_in": [jnp.swapaxes(t, 1, 2) for t in
                 _adamw_own(w_in_t, s_in, chip, r_in, m_w_in_t, v_w_in_t, "adamw_w_in")],
        "w_branch_a": _adamw_own(w_branch_a, s_a, me, r_a, m_w_branch_a, v_w_branch_a, "adamw_w_branch_a"),
        "w_branch_b": _adamw_own(w_branch_b, s_b, me, r_b, m_w_branch_b, v_w_branch_b, "adamw_w_branch_b"),
        "b_merge": _adamw_own(b_merge, s_bm, me, r_bm, m_b_merge, v_b_merge, "adamw_b_merge"),
        "w_out": _adamw_own(w_out, s_o, me, r_o, m_w_out, v_w_out, "adamw_w_out"),
    }
    order = ["norm_gain", "w_in", "q_norm_a", "k_norm_a", "q_norm_b", "k_norm_b", "sink_a", "rel_bias", "w_branch_a",
             "w_branch_b", "b_merge", "w_out"]
    outs = []
    for k in range(4):
        small = _unpack_small(res["small"][k], small_shapes)
        for n in order:
            outs.append(small[n] if n in small else res[n][k])
    loss = res["small"][0][0, SMALL_USED]
    return (loss, loc["grad_x"][None], *outs)
```

```python
import math

import numpy as np
import jax
import jax.numpy as jnp
from jax import lax
from jax.experimental import pallas as pl
from jax.experimental.pallas import tpu as pltpu

f32 = jnp.float32
bf16 = jnp.bfloat16

S = 4096
D = 1024
NA = 5376
NT = 3072
NW = NA + NT
WSH = NW // 8
HD = 64
LANES = 128
EPS = 1e-6
NEG = -1e30
SCALE = HD ** -0.5
TQ = 128
PAD = 128
SP = S + 2 * PAD
NDEV = 8
GROUPS = ((128, 1, 0), (64, 1, 8), (64, 4, 16), (64, 16, 24))
CHUNK = 256
PCHUNK = 128
RC = 64

ADAM_LR, ADAM_B1, ADAM_B2, ADAM_EPS, ADAM_WD, ADAM_STEP = 0.001, 0.9, 0.999, 1e-08, 0.01, 10

MIB = 1024 * 1024
NT_DIMS = (((1,), (1,)), ((), ()))
TN_DIMS = (((0,), (0,)), ((), ()))


def _params(sem=None, vmem_mib=48):
    return pltpu.CompilerParams(dimension_semantics=sem, vmem_limit_bytes=vmem_mib * MIB)


def _lo():
    return lax.broadcasted_iota(jnp.int32, (1, LANES), 1) < HD


def _head_ones():
    r = lax.broadcasted_iota(jnp.int32, (LANES, LANES), 0) // HD
    c = lax.broadcasted_iota(jnp.int32, (LANES, LANES), 1) // HD
    return jnp.where(r == c, 1.0, 0.0).astype(bf16)


def _half_sums(x, ones):
    hi = x.astype(bf16)
    mid = (x - hi.astype(f32)).astype(bf16)
    return (jnp.dot(hi, ones, preferred_element_type=f32) + jnp.dot(mid, ones, preferred_element_type=f32))


def _seg_sum(x, ones):
    outs = [_half_sums(x[:, b * LANES:(b + 1) * LANES], ones) for b in range(x.shape[1] // LANES)]
    return outs[0] if len(outs) == 1 else jnp.concatenate(outs, axis=1)


def _bucket_np(blk, stride):
    w = TQ + 2 * blk
    rel = np.arange(w)[None, :] - blk - np.arange(TQ)[:, None]
    band = np.abs(rel) <= blk
    r = rel * stride
    n = np.abs(r)
    nf = np.maximum(n, 8).astype(np.float32)
    large = 8 + (np.log(nf / np.float32(8)) / np.float32(math.log(128.0)) * np.float32(8)).astype(np.int32)
    large = np.minimum(large, 15)
    b = (r > 0).astype(np.int32) * 16 + np.where(n < 8, n, large)
    return np.where(band, b, -1).astype(np.int32)


def _rms(x, gain):
    ts = 512

    def body(x_ref, g_ref, h_ref, ht_ref, r_ref):
        xv = x_ref[...]
        r = lax.rsqrt(jnp.mean(xv * xv, axis=-1, keepdims=True) + EPS)
        h = (xv * r) * g_ref[...]
        h_ref[...] = h.astype(bf16)
        ht_ref[...] = h.T.astype(bf16)
        r_ref[...] = r

    return pl.pallas_call(
        body,
        grid=(S // ts,),
        in_specs=[pl.BlockSpec((ts, D), lambda i: (i, 0)), pl.BlockSpec((1, D), lambda i: (0, 0))],
        out_specs=[pl.BlockSpec((ts, D), lambda i: (i, 0)), pl.BlockSpec((D, ts), lambda i: (0, i)),
                   pl.BlockSpec((ts, 1), lambda i: (i, 0))],
        out_shape=[jax.ShapeDtypeStruct((S, D), bf16), jax.ShapeDtypeStruct((D, S), bf16),
                   jax.ShapeDtypeStruct((S, 1), f32)],
        compiler_params=_params(("arbitrary",)),
        name="rms",
    )(x, gain)


def _inproj_half(hb, w_t, half, proj, name):
    ts = 512
    tn = NW // 2
    per = NW // 2 // tn

    def body(h_idx, h_ref, w_ref, *rest):
        del h_idx
        rest[-1][...] = lax.dot_general(h_ref[...], w_ref[...], NT_DIMS, preferred_element_type=f32)

    in_specs = [pl.BlockSpec((ts, D), lambda i, n, hf: (i, 0)),
                pl.BlockSpec((tn, D), lambda i, n, hf: (hf[0] * per + n, 0))]
    args = [half, hb, w_t]
    aliases = {}
    if proj is not None:
        in_specs.append(pl.BlockSpec(memory_space=pl.ANY))
        args.append(proj)
        aliases = {3: 0}
    return pl.pallas_call(
        body,
        grid_spec=pltpu.PrefetchScalarGridSpec(
            num_scalar_prefetch=1,
            grid=(S // ts, per),
            in_specs=in_specs,
            out_specs=pl.BlockSpec((ts, tn), lambda i, n, hf: (i, hf[0] * per + n)),
        ),
        out_shape=jax.ShapeDtypeStruct((S, NW), f32),
        input_output_aliases=aliases,
        compiler_params=_params(("arbitrary", "arbitrary")),
        name=name,
    )(*args)


def _bias_expand(table, bucket, c0, name):
    tq, w = bucket.shape
    blk = (w - tq) // 2

    def body(tab_ref, bk_ref, o_ref):
        h = pl.program_id(0)
        bk = bk_ref[...]

        def step(b, acc):
            return jnp.where(bk == b, tab_ref[b, c0 + h], acc)

        inner = lax.fori_loop(0, 32, step, jnp.full((tq, w), NEG, f32))
        col = lax.broadcasted_iota(jnp.int32, (1, w), 1)
        o_ref[0] = jnp.where(col < blk, NEG, inner)
        o_ref[1] = inner
        o_ref[2] = jnp.where(col >= tq + blk, NEG, inner)

    return pl.pallas_call(
        body,
        grid=(8,),
        in_specs=[pl.BlockSpec(memory_space=pltpu.SMEM), pl.BlockSpec((tq, w), lambda h: (0, 0))],
        out_specs=pl.BlockSpec((3, None, tq, w), lambda h: (0, h, 0, 0)),
        out_shape=jax.ShapeDtypeStruct((3, 8, tq, w), f32),
        compiler_params=_params(("arbitrary",)),
        name=name,
    )(table, bucket)


def _tile_kind(t, seq):
    m0 = jnp.bitwise_and(t * TQ, seq - 1)
    return jnp.where(m0 == 0, 0, jnp.where(m0 == seq - TQ, 2, 1))


def _col_block(g, j):
    kind = j // 4
    hp = j % 4
    a = jnp.where(kind == 0, hp, 3 + kind)
    b = 6 + 12 * kind + 4 * (g - 1) + hp
    return jnp.where(g == 0, a, b)


def _prep(proj_a, gains):
    nslot, nstep = 3, 12

    def body(p_hbm, g_ref, o_ref, bufs, sems):
        g = pl.program_id(0)
        kind = pl.program_id(1)
        step = 3 * g + kind

        def fetch(s, slot):
            return [pltpu.make_async_copy(
                p_hbm.at[:, pl.ds(pl.multiple_of(_col_block(s // 3, 4 * (s % 3) + u) * LANES, LANES), LANES)],
                bufs.at[slot, u], sems.at[slot, u]) for u in range(4)]

        @pl.when(step == 0)
        def _():
            for s in range(nslot - 1):
                for cp in fetch(s, s):
                    cp.start()

        @pl.when(step + nslot - 1 < nstep)
        def _():
            for cp in fetch(step + nslot - 1, (step + nslot - 1) % nslot):
                cp.start()

        slot = step % nslot
        for cp in fetch(step, slot):
            cp.wait()
        p0_ref, p1_ref, p2_ref, p3_ref = (bufs.at[slot, u] for u in range(4))
        lo = _lo()
        ones = _head_ones()
        half = jnp.where(lo, 0, 1)
        gain = g_ref[...]

        def norm_store(xv, u, dst, dup):
            if dup:
                take = (kind == 0) | (half == u // 2)
                xv = jnp.where(take, xv, pltpu.roll(xv, HD, 1))
            r = lax.rsqrt(_half_sums(xv * xv, ones) * (1.0 / HD) + EPS)
            r = jnp.where(kind == 2, 1.0, r)
            yv = (xv * r) * gain
            yv = jnp.where(kind == 0, yv * SCALE, yv)
            o_ref[u, PAD + dst:PAD + dst + CHUNK, :] = yv.astype(bf16)

        for u in range(4):
            o_ref[u, 0:PAD, :] = jnp.zeros((PAD, LANES), bf16)
            o_ref[u, PAD + S:SP, :] = jnp.zeros((PAD, LANES), bf16)
        for gi, (_, d, _) in enumerate(GROUPS):
            @pl.when(g == gi)
            def _():
                seq = S // d
                for u, p_ref in enumerate((p0_ref, p1_ref, p2_ref, p3_ref)):
                    for c in range(d):
                        for i in range(seq // CHUNK):
                            if d == 1:
                                xv = p_ref[i * CHUNK:(i + 1) * CHUNK, :]
                            else:
                                xv = p_ref[pl.ds(c + i * CHUNK * d, CHUNK, stride=d), :]
                            norm_store(xv, u, c * seq + i * CHUNK, gi == 0)

    return pl.pallas_call(
        body,
        grid=(4, 3),
        in_specs=[
            pl.BlockSpec(memory_space=pl.ANY),
            pl.BlockSpec((None, None, 1, LANES), lambda g, kind: (g, kind, 0, 0)),
        ],
        out_specs=pl.BlockSpec((None, 4, SP, LANES), lambda g, kind: (g, kind, 0, 0)),
        out_shape=jax.ShapeDtypeStruct((4, 12, SP, LANES), bf16),
        scratch_shapes=[pltpu.VMEM((nslot, 4, S, LANES), f32), pltpu.SemaphoreType.DMA((nslot, 4))],
        compiler_params=_params(("arbitrary", "arbitrary")),
        name="prep",
    )(proj_a, gains)


def _token_rows(t, r0, n, d):
    if d == 1:
        return pl.ds(pl.multiple_of(t * TQ, TQ) + r0, n)
    per = S // d // TQ
    return pl.ds(((t % per) * TQ + r0) * d + t // per, n, stride=d)


def _stack_heads(t, lo):
    z = jnp.zeros_like(t)
    return jnp.concatenate([jnp.where(lo, t, z), jnp.where(lo, z, t)], axis=0)


def _unstack_heads(t2, lo):
    return jnp.where(lo, t2[:TQ], t2[TQ:])


def _attn_fwd(gl, bias, sink, g, blk, d, name):
    w = TQ + 2 * blk
    seq = S // d
    use_sink = sink is not None

    def body(*refs):
        if use_sink:
            sink_ref, q_ref, k_ref, v_ref, b_ref, o_ref, l_ref, s0, s1, p0, p1, lse_scr = refs
        else:
            q_ref, k_ref, v_ref, b_ref, o_ref, l_ref, s0, s1, p0, p1, lse_scr = refs
        hp = pl.program_id(0)
        lo = _lo()
        s_bufs, p_bufs = (s0, s1), (p0, p1)

        def scores(p, slot):
            for u in range(2):
                f0 = pl.multiple_of((2 * p + u) * TQ, TQ)
                q2 = _stack_heads(q_ref[pl.ds(PAD + f0, TQ), :], lo)
                kw = k_ref[pl.ds(PAD - blk + f0, w), :]
                s_bufs[slot][u] = lax.dot_general(q2, kw, NT_DIMS, preferred_element_type=f32)

        def softmax(p, slot):
            for u in range(2):
                t = 2 * p + u
                kind = _tile_kind(t, seq)
                for h in range(2):
                    for r in range(TQ // RC):
                        rows = slice(h * TQ + r * RC, h * TQ + (r + 1) * RC)
                        logit = s_bufs[slot][u, rows, :] + b_ref[kind, h, r * RC:(r + 1) * RC, :]
                        m = jnp.max(logit, axis=1, keepdims=True)
                        e = jnp.exp(logit - m)
                        lse = m + jnp.log(jnp.sum(e, axis=1, keepdims=True))
                        if use_sink:
                            sk = sink_ref[2 * hp + h]
                            mx = jnp.maximum(lse, sk)
                            lse = mx + jnp.log(jnp.exp(lse - mx) + jnp.exp(sk - mx))
                        p_bufs[slot][u, rows, :] = (e * jnp.exp(m - lse)).astype(bf16)
                        lse_scr[u, rows, :] = jnp.broadcast_to(lse, (RC, LANES))
                l_ref[_token_rows(t, 0, TQ, d), :] = jnp.where(lo, lse_scr[u, 0:TQ, :], lse_scr[u, TQ:2 * TQ, :])

        def values(p, slot):
            for u in range(2):
                t = 2 * p + u
                vw = v_ref[pl.ds(PAD - blk + pl.multiple_of(t * TQ, TQ), w), :]
                o2 = jnp.dot(p_bufs[slot][u], vw, preferred_element_type=f32)
                o_ref[_token_rows(t, 0, TQ, d), :] = _unstack_heads(o2, lo)

        npair = S // TQ // 2
        scores(0, 0)
        scores(1, 1)
        softmax(0, 0)

        def steady(k, carry):
            p = 2 * k + 2
            scores(p, 0)
            softmax(p - 1, 1)
            values(p - 2, 0)
            scores(p + 1, 1)
            softmax(p, 0)
            values(p - 1, 1)
            return carry

        lax.fori_loop(0, (npair - 2) // 2, steady, 0)
        softmax(npair - 1, 1)
        values(npair - 2, 0)
        values(npair - 1, 1)

    in_specs = [
        pl.BlockSpec((None, None, SP, LANES), lambda hp: (g, hp, 0, 0)),
        pl.BlockSpec((None, None, SP, LANES), lambda hp: (g, 4 + hp, 0, 0)),
        pl.BlockSpec((None, None, SP, LANES), lambda hp: (g, 8 + hp, 0, 0)),
        pl.BlockSpec((3, 2, TQ, w), lambda hp: (0, hp, 0, 0)),
    ]
    args = [gl, gl, gl, bias]
    if use_sink:
        in_specs = [pl.BlockSpec(memory_space=pltpu.SMEM)] + in_specs
        args = [sink] + args
    out = pl.BlockSpec((S, LANES), lambda hp: (0, hp))
    return pl.pallas_call(
        body,
        grid=(4,),
        in_specs=in_specs,
        out_specs=[out, out],
        out_shape=[jax.ShapeDtypeStruct((S, 4 * LANES), f32)] * 2,
        scratch_shapes=[pltpu.VMEM((2, 2 * TQ, w), f32), pltpu.VMEM((2, 2 * TQ, w), f32),
                        pltpu.VMEM((2, 2 * TQ, w), bf16), pltpu.VMEM((2, 2 * TQ, w), bf16),
                        pltpu.VMEM((2, 2 * TQ, LANES), f32)],
        compiler_params=_params(("arbitrary",)),
        name=name,
    )(*args)


def _attn_bwd(gl, bias, bucket, do, lse, dd, g, blk, d, name):
    w = TQ + 2 * blk
    seq = S // d

    def body(q_ref, k_ref, v_ref, b_ref, bk_ref, do_ref, l_ref, d_ref, dqkv_ref, dbk_ref,
             db_acc, s0, s1, dp0, dp1, pb0, pb1, ds0, ds1, dk_acc, dv_acc):
        lo = _lo()
        hi = jnp.logical_not(lo)
        dk_acc[...] = jnp.zeros((SP, LANES), f32)
        dv_acc[...] = jnp.zeros((SP, LANES), f32)
        db_acc[...] = jnp.zeros((2 * TQ, w), f32)
        s_bufs, dp_bufs, pb_bufs, ds_bufs = (s0, s1), (dp0, dp1), (pb0, pb1), (ds0, ds1)

        def stacked(t):
            f0 = pl.multiple_of(t * TQ, TQ)
            q2 = _stack_heads(q_ref[pl.ds(PAD + f0, TQ), :], lo)
            do2 = _stack_heads(do_ref[_token_rows(t, 0, TQ, d), :].astype(bf16), lo)
            return f0, q2, do2

        def scores(p, slot):
            for u in range(2):
                f0, q2, do2 = stacked(2 * p + u)
                win = pl.ds(PAD - blk + f0, w)
                s_bufs[slot][u] = lax.dot_general(q2, k_ref[win, :], NT_DIMS, preferred_element_type=f32)
                dp_bufs[slot][u] = lax.dot_general(do2, v_ref[win, :], NT_DIMS, preferred_element_type=f32)

        def grads(p, slot):
            for u in range(2):
                t = 2 * p + u
                kind = _tile_kind(t, seq)
                for h in range(2):
                    msk = lo if h == 0 else hi
                    for r in range(TQ // RC):
                        rows = slice(h * TQ + r * RC, h * TQ + (r + 1) * RC)
                        src = _token_rows(t, r * RC, RC, d)
                        lh = jnp.max(jnp.where(msk, l_ref[src, :], -jnp.inf), axis=1, keepdims=True)
                        dh = jnp.max(jnp.where(msk, d_ref[src, :], -jnp.inf), axis=1, keepdims=True)
                        logit = s_bufs[slot][u, rows, :] + b_ref[kind, h, r * RC:(r + 1) * RC, :]
                        pr = jnp.exp(logit - lh)
                        ds = pr * (dp_bufs[slot][u, rows, :] - dh)
                        db_acc[rows, :] += ds
                        pb_bufs[slot][u, rows, :] = pr.astype(bf16)
                        ds_bufs[slot][u, rows, :] = ds.astype(bf16)

        def accumulate(p, slot):
            for u in range(2):
                f0, q2, do2 = stacked(2 * p + u)
                win = pl.ds(PAD - blk + f0, w)
                dsb = ds_bufs[slot][u]
                dq2 = jnp.dot(dsb, k_ref[win, :], preferred_element_type=f32)
                dqkv_ref[0, pl.ds(PAD + f0, TQ), :] = _unstack_heads(dq2, lo).astype(bf16)
                dk_acc[win, :] += lax.dot_general(dsb, q2, TN_DIMS, preferred_element_type=f32)
                dv_acc[win, :] += lax.dot_general(pb_bufs[slot][u], do2, TN_DIMS, preferred_element_type=f32)

        npair = S // TQ // 2
        scores(0, 0)
        scores(1, 1)
        grads(0, 0)

        def steady(k, carry):
            p = 2 * k + 2
            scores(p, 0)
            grads(p - 1, 1)
            accumulate(p - 2, 0)
            scores(p + 1, 1)
            grads(p, 0)
            accumulate(p - 1, 1)
            return carry

        lax.fori_loop(0, (npair - 2) // 2, steady, 0)
        grads(npair - 1, 1)
        accumulate(npair - 2, 0)
        accumulate(npair - 1, 1)
        for i in range(SP // CHUNK):
            rows = slice(i * CHUNK, (i + 1) * CHUNK)
            dqkv_ref[1, rows, :] = dk_acc[rows, :].astype(bf16)
            dqkv_ref[2, rows, :] = dv_acc[rows, :].astype(bf16)

        bk = bk_ref[...]
        lane = lax.broadcasted_iota(jnp.int32, (8, LANES), 1)
        for h in range(2):
            db = db_acc[h * TQ:(h + 1) * TQ, :]
            acc = jnp.zeros((8, LANES), f32)
            for b in range(32):
                part = jnp.where(bk == b, db, 0.0).reshape(TQ // 8, 8, w).sum(axis=0)
                tot = jnp.sum(jnp.sum(part, axis=1, keepdims=True), axis=0, keepdims=True)
                acc = jnp.where(lane == b, tot, acc)
            dbk_ref[h] = acc

    def gcol(off):
        return pl.BlockSpec((None, None, SP, LANES), lambda hp: (g, off + hp, 0, 0))

    row = pl.BlockSpec((S, LANES), lambda hp: (0, hp))
    return pl.pallas_call(
        body,
        grid=(4,),
        in_specs=[gcol(0), gcol(4), gcol(8), pl.BlockSpec((3, 2, TQ, w), lambda hp: (0, hp, 0, 0)),
                  pl.BlockSpec((TQ, w), lambda hp: (0, 0)), row, row, row],
        out_specs=[pl.BlockSpec((3, None, SP, LANES), lambda hp: (0, hp, 0, 0)),
                   pl.BlockSpec((2, 8, LANES), lambda hp: (hp, 0, 0))],
        out_shape=[
            jax.ShapeDtypeStruct((3, 4, SP, LANES), bf16),
            jax.ShapeDtypeStruct((8, 8, LANES), f32),
        ],
        scratch_shapes=([pltpu.VMEM((2 * TQ, w), f32)] + [pltpu.VMEM((2, 2 * TQ, w), f32)] * 4
                        + [pltpu.VMEM((2, 2 * TQ, w), bf16)] * 4 + [pltpu.VMEM((SP, LANES), f32)] * 2),
        compiler_params=_params(("arbitrary",), vmem_mib=56),
        name=name,
    )(gl, gl, gl, bias, bucket, do, lse, dd)


def _sigmoid(z):
    return 1.0 / (1.0 + jnp.exp(-z))


def _tail(x, tgt, o_a, l_a, o_b, l_b, proj, bm, w_a, w_b, w_o, sink_b):
    ts = 256

    def body(x_ref, t_ref, oa_ref, la_ref, ob0_ref, ob1_ref, ob2_ref, lb0_ref, lb1_ref, lb2_ref,
             ga_ref, gb_ref, m0_ref, m1_ref, bm_ref, wa_ref, wb_ref, wo_ref, sk_ref,
             dy_ref, dyb_ref, dt_ref, doa_ref, dda_ref, dob0_ref, dob1_ref, dob2_ref, ddb0_ref, ddb1_ref, ddb2_ref,
             ya_ref, yb_ref, mg_ref, dbra_ref, dbrb_ref, loss_ref, dbm_ref, dsk_ref):
        i = pl.program_id(0)

        @pl.when(i == 0)
        def _():
            loss_ref[...] = jnp.zeros_like(loss_ref)
            dbm_ref[...] = jnp.zeros_like(dbm_ref)
            dsk_ref[...] = jnp.zeros_like(dsk_ref)

        ga = ga_ref[...]
        sa = _sigmoid(ga)
        silu_a = ga * sa
        oa = oa_ref[...]
        ya = oa * silu_a
        gb = gb_ref[...]
        sb = _sigmoid(gb)
        silu_b = gb * sb
        ob = [ob0_ref[...], ob1_ref[...], ob2_ref[...]]
        lb = [lb0_ref[...], lb1_ref[...], lb2_ref[...]]
        mx = jnp.maximum(jnp.maximum(lb[0], lb[1]), lb[2])
        ex = [jnp.exp(v - mx) for v in lb]
        den = ex[0] + ex[1] + ex[2]
        alpha = [e / den for e in ex]
        ybc = alpha[0] * ob[0] + alpha[1] * ob[1] + alpha[2] * ob[2]
        yb = ybc * silu_b
        yab = ya.astype(bf16)
        ybb = yb.astype(bf16)
        br_a = jnp.dot(yab, wa_ref[...], preferred_element_type=f32)
        br_b = jnp.dot(ybb, wb_ref[...], preferred_element_type=f32)
        g0 = _sigmoid(m0_ref[...] + bm_ref[0:1, :])
        g1 = _sigmoid(m1_ref[...] + bm_ref[1:2, :])
        merged = g0 * br_a + g1 * br_b
        mgb = merged.astype(bf16)
        y = x_ref[...] + jnp.dot(mgb, wo_ref[...], preferred_element_type=f32)
        err = y - t_ref[...]
        part = jnp.sum(jnp.sum(err * err, axis=1, keepdims=True), axis=0, keepdims=True)
        loss_ref[...] += part * (0.5 / D)
        dy = err * (1.0 / D)
        dyb = dy.astype(bf16)
        dmerged = lax.dot_general(dyb, wo_ref[...], NT_DIMS, preferred_element_type=f32)
        dbr_a = (dmerged * g0).astype(bf16)
        dbr_b = (dmerged * g1).astype(bf16)
        dm0 = dmerged * br_a * (g0 * (1.0 - g0))
        dm1 = dmerged * br_b * (g1 * (1.0 - g1))
        dbm_ref[0:1, :] += jnp.sum(dm0, axis=0, keepdims=True)
        dbm_ref[1:2, :] += jnp.sum(dm1, axis=0, keepdims=True)
        dya = lax.dot_general(dbr_a, wa_ref[...], NT_DIMS, preferred_element_type=f32)
        dyb2 = lax.dot_general(dbr_b, wb_ref[...], NT_DIMS, preferred_element_type=f32)
        do_a = dya * silu_a
        dga = dya * oa * (sa * (1.0 + ga * (1.0 - sa)))
        ones = _head_ones()
        delta_a = _seg_sum(do_a * oa, ones)
        dsk_ref[...] -= jnp.sum(delta_a * jnp.exp(sk_ref[...] - la_ref[...]), axis=0, keepdims=True)
        dybc = dyb2 * silu_b
        dgb = dyb2 * ybc * (sb * (1.0 + gb * (1.0 - sb)))
        dbar = _seg_sum(dybc * ybc, ones)
        dy_ref[...] = dy
        dyb_ref[...] = dyb
        dt_ref[:, 0:512] = dga.astype(bf16)
        dt_ref[:, 512:1024] = dgb.astype(bf16)
        dt_ref[:, 1024:2048] = dm0.astype(bf16)
        dt_ref[:, 2048:3072] = dm1.astype(bf16)
        doa_ref[...] = do_a.astype(bf16)
        dda_ref[...] = delta_a
        for k, (dob_ref, ddb_ref) in enumerate(((dob0_ref, ddb0_ref), (dob1_ref, ddb1_ref), (dob2_ref, ddb2_ref))):
            dob_ref[...] = alpha[k] * dybc
            ddb_ref[...] = alpha[k] * dbar
        ya_ref[...] = ya.T.astype(bf16)
        yb_ref[...] = yb.T.astype(bf16)
        mg_ref[...] = merged.T.astype(bf16)
        dbra_ref[...] = dbr_a
        dbrb_ref[...] = dbr_b

    def rows(n, blk=0):
        return pl.BlockSpec((ts, n), lambda i: (i, blk))

    def whole(r, c):
        return pl.BlockSpec((r, c), lambda i: (0, 0))

    def cols(n):
        return pl.BlockSpec((n, ts), lambda i: (0, i))

    def gate_cols(n, col):
        return pl.BlockSpec((pl.Element(ts), pl.Element(n)), lambda i: (i * ts, NA + col))

    outs = [
        ((S, D), f32, rows(D)), ((S, D), bf16, rows(D)), ((S, NW), bf16, gate_cols(NT, 0)),
        ((S, 512), bf16, rows(512)), ((S, 512), f32, rows(512)),
        ((S, 512), f32, rows(512)), ((S, 512), f32, rows(512)), ((S, 512), f32, rows(512)),
        ((S, 512), f32, rows(512)), ((S, 512), f32, rows(512)), ((S, 512), f32, rows(512)),
        ((512, S), bf16, cols(512)), ((512, S), bf16, cols(512)), ((D, S), bf16, cols(D)),
        ((S, D), bf16, rows(D)), ((S, D), bf16, rows(D)),
        ((1, 1), f32, whole(1, 1)), ((2, D), f32, whole(2, D)), ((1, 512), f32, whole(1, 512)),
    ]
    return pl.pallas_call(
        body,
        grid=(S // ts,),
        in_specs=[
            rows(D), rows(D), rows(512), rows(512), rows(512), rows(512), rows(512), rows(512), rows(512), rows(512),
            gate_cols(512, 0), gate_cols(512, 512), gate_cols(D, 1024), gate_cols(D, 2048), whole(2, D),
            whole(512, D), whole(512, D), whole(D, D), whole(1, 512),
        ],
        out_specs=[o[2] for o in outs],
        out_shape=[jax.ShapeDtypeStruct(o[0], o[1]) for o in outs],
        compiler_params=_params(("arbitrary",), vmem_mib=60),
        name="tail",
    )(x, tgt, o_a, l_a, *o_b, *l_b, proj, proj, proj, proj, bm, w_a, w_b, w_o, sink_b)


def _norm_bwd(xv, dyv, gain, ones):
    r = lax.rsqrt(_half_sums(xv * xv, ones) * (1.0 / HD) + EPS)
    yv = xv * r
    u = dyv * gain
    dxv = r * (u - yv * (_half_sums(u * yv, ones) * (1.0 / HD)))
    return dxv, jnp.sum(dyv * yv, axis=0, keepdims=True)


def _post_b(g, dqkv, proj_a, gains, dproj):
    d = GROUPS[g][1]
    seq = S // d

    def body(d_ref, pa_ref, pb_ref, g_ref, alias_ref, o_ref, dg_ref, nat_a, nat_b):
        del alias_ref
        pj = pl.program_id(0)
        kind = pj // 2
        q_scale = jnp.where(kind == 0, SCALE, 1.0)
        gain = g_ref[...] * q_scale
        ones = _head_ones()

        @pl.when(pj % 2 == 0)
        def _():
            dg_ref[...] = jnp.zeros_like(dg_ref)

        def columns(with_norm):
            for u, (p_ref, nat) in enumerate(((pa_ref, nat_a), (pb_ref, nat_b))):
                for c in range(d):
                    for i in range(seq // PCHUNK):
                        src = c * seq + i * PCHUNK
                        if d == 1:
                            idx = slice(src, src + PCHUNK)
                        else:
                            idx = pl.ds(c + i * PCHUNK * d, PCHUNK, stride=d)
                        dyv = d_ref[u, PAD + src:PAD + src + PCHUNK, :].astype(f32)
                        if with_norm:
                            dyv, dg = _norm_bwd(p_ref[idx, :], dyv, gain, ones)
                            dg_ref[...] += dg * q_scale
                        nat[idx, :] = dyv
                for i in range(S // CHUNK):
                    rows = slice(i * CHUNK, (i + 1) * CHUNK)
                    o_ref[rows, u * LANES:(u + 1) * LANES] = nat[rows, :].astype(bf16)

        pl.when(kind < 2)(lambda: columns(True))
        pl.when(kind == 2)(lambda: columns(False))

    def pcol(u):
        return pl.BlockSpec((S, LANES), lambda pj: (0, _col_block(g, 2 * jnp.minimum(pj, 3) + u)))

    return pl.pallas_call(
        body,
        grid=(6,),
        in_specs=[
            pl.BlockSpec((None, 2, SP, LANES), lambda pj: (pj // 2, pj % 2, 0, 0)),
            pcol(0), pcol(1),
            pl.BlockSpec((None, None, 1, LANES), lambda pj: (g, pj // 2, 0, 0)),
            pl.BlockSpec(memory_space=pl.ANY),
        ],
        out_specs=[
            pl.BlockSpec((S, 2 * LANES), lambda pj: (0, _col_block(g, 2 * pj) // 2)),
            pl.BlockSpec((None, 1, LANES), lambda pj: (pj // 2, 0, 0)),
        ],
        out_shape=[jax.ShapeDtypeStruct((S, NW), bf16), jax.ShapeDtypeStruct((3, 1, LANES), f32)],
        scratch_shapes=[pltpu.VMEM((S, LANES), f32), pltpu.VMEM((S, LANES), f32)],
        input_output_aliases={4: 0},
        compiler_params=_params(("arbitrary",)),
        name="post_b%d" % g,
    )(dqkv, proj_a, proj_a, gains, dproj)


def _post_a(dqkv, proj_a, gains, dproj):
    def body(q_ref, e_ref, p_ref, g_ref, alias_ref, o_ref, dg_ref):
        del alias_ref
        j = pl.program_id(0)
        q_scale = jnp.where(j < 4, SCALE, 1.0)
        gain = g_ref[...] * q_scale
        lo = _lo()
        ones = _head_ones()

        @pl.when((j == 0) | (j >= 4))
        def _():
            dg_ref[...] = jnp.zeros_like(dg_ref)

        def column(folded, with_norm):
            for i in range(S // PCHUNK):
                r0 = i * PCHUNK
                rows = slice(PAD + r0, PAD + r0 + PCHUNK)
                if folded:
                    t0 = e_ref[0, rows, :].astype(f32) + e_ref[1, rows, :].astype(f32)
                    t1 = e_ref[2, rows, :].astype(f32) + e_ref[3, rows, :].astype(f32)
                    dyv = jnp.where(lo, t0 + pltpu.roll(t0, HD, 1), t1 + pltpu.roll(t1, HD, 1))
                else:
                    dyv = q_ref[rows, :].astype(f32)
                if with_norm:
                    dyv, dg = _norm_bwd(p_ref[r0:r0 + PCHUNK, :], dyv, gain, ones)
                    dg_ref[...] += dg * q_scale
                o_ref[r0:r0 + PCHUNK, :] = dyv.astype(bf16)

        pl.when(j < 4)(lambda: column(False, True))
        pl.when(j == 4)(lambda: column(True, True))
        pl.when(j == 5)(lambda: column(True, False))

    return pl.pallas_call(
        body,
        grid=(6,),
        in_specs=[
            pl.BlockSpec((None, None, SP, LANES), lambda j: (0, jnp.minimum(j, 3), 0, 0)),
            pl.BlockSpec((None, 4, SP, LANES), lambda j: (jnp.clip(j - 3, 1, 2), 0, 0, 0)),
            pl.BlockSpec((S, LANES), lambda j: (0, jnp.minimum(j, 4))),
            pl.BlockSpec((None, None, 1, LANES), lambda j: (0, jnp.maximum(j - 3, 0), 0, 0)),
            pl.BlockSpec(memory_space=pl.ANY),
        ],
        out_specs=[
            pl.BlockSpec((S, LANES), lambda j: (0, j)),
            pl.BlockSpec((None, 1, LANES), lambda j: (jnp.maximum(j - 3, 0), 0, 0)),
        ],
        out_shape=[jax.ShapeDtypeStruct((S, NW), bf16), jax.ShapeDtypeStruct((3, 1, LANES), f32)],
        input_output_aliases={4: 0},
        compiler_params=_params(("arbitrary",)),
        name="post_a",
    )(dqkv, dqkv, proj_a, gains, dproj)


def _dh_norm_bwd(dproj, w, x, rstd, gain, dy):
    ts = 1024
    tk = NW // 3
    nk = NW // tk

    def body(d_ref, w_ref, x_ref, r_ref, g_ref, dy_ref, gx_ref, dgn_ref, acc):
        i = pl.program_id(0)
        k = pl.program_id(1)

        @pl.when((i == 0) & (k == 0))
        def _():
            dgn_ref[...] = jnp.zeros_like(dgn_ref)

        @pl.when(k == 0)
        def _():
            acc[...] = jnp.zeros_like(acc)

        acc[...] += jnp.dot(d_ref[...], w_ref[...], preferred_element_type=f32)

        @pl.when(k == nk - 1)
        def _():
            dh = acc[...]
            xh = x_ref[...] * r_ref[...]
            u = dh * g_ref[...]
            dx = r_ref[...] * (u - xh * jnp.mean(u * xh, axis=-1, keepdims=True))
            gx_ref[...] = dy_ref[...] + dx
            dgn_ref[...] += jnp.sum(dh * xh, axis=0, keepdims=True)

    return pl.pallas_call(
        body,
        grid=(S // ts, nk),
        in_specs=[
            pl.BlockSpec((ts, tk), lambda i, k: (i, k)),
            pl.BlockSpec((tk, D), lambda i, k: (k, 0)),
            pl.BlockSpec((ts, D), lambda i, k: (i, 0)),
            pl.BlockSpec((ts, 1), lambda i, k: (i, 0)),
            pl.BlockSpec((1, D), lambda i, k: (0, 0)),
            pl.BlockSpec((ts, D), lambda i, k: (i, 0)),
        ],
        out_specs=[pl.BlockSpec((ts, D), lambda i, k: (i, 0)), pl.BlockSpec((1, D), lambda i, k: (0, 0))],
        out_shape=[jax.ShapeDtypeStruct((S, D), f32), jax.ShapeDtypeStruct((1, D), f32)],
        scratch_shapes=[pltpu.VMEM((ts, D), f32)],
        compiler_params=_params(("arbitrary", "arbitrary"), vmem_mib=60),
        name="dh_norm_bwd",
    )(dproj, w, x, rstd, gain, dy)


def _dw_in(hbt, dproj, parity, name):
    tk = S
    win = WSH + 96

    def body(par_ref, a_ref, b_ref, o_ref, acc):
        p = 2 * pl.program_id(0) + par_ref[0]
        k = pl.program_id(1)

        @pl.when(k == 0)
        def _():
            acc[...] = jnp.zeros_like(acc)

        acc[...] += jnp.dot(a_ref[...], b_ref[...], preferred_element_type=f32)

        @pl.when(k == S // tk - 1)
        def _():
            acc_t = acc[...].T
            for pp in range(NDEV):
                off = (WSH * pp) % LANES

                @pl.when(p == pp)
                def _():
                    o_ref[...] = acc_t[off:off + WSH, :].astype(bf16)

    return pl.pallas_call(
        body,
        grid_spec=pltpu.PrefetchScalarGridSpec(
            num_scalar_prefetch=1,
            grid=(NDEV // 2, S // tk),
            in_specs=[
                pl.BlockSpec((D, tk), lambda q, k, par: (0, k)),
                pl.BlockSpec((pl.Element(tk), pl.Element(win)),
                             lambda q, k, par: (k * tk, (WSH * (2 * q + par[0])) // LANES * LANES)),
            ],
            out_specs=pl.BlockSpec((None, WSH, D), lambda q, k, par: (q, 0, 0)),
            scratch_shapes=[pltpu.VMEM((D, win), f32)],
        ),
        out_shape=jax.ShapeDtypeStruct((NDEV // 2, WSH, D), bf16),
        compiler_params=_params(("arbitrary", "arbitrary")),
        name=name,
    )(parity, hbt, dproj)


def _matmul_tokens(at, b, name):
    m, n = at.shape[0], b.shape[1]
    tn = 1024
    tk = 2048

    def body(a_ref, b_ref, o_ref):
        @pl.when(pl.program_id(1) == 0)
        def _():
            o_ref[...] = jnp.zeros_like(o_ref)

        o_ref[...] += jnp.dot(a_ref[...], b_ref[...], preferred_element_type=f32)

    return pl.pallas_call(
        body,
        grid=(n // tn, S // tk),
        in_specs=[pl.BlockSpec((m, tk), lambda j, k: (0, k)), pl.BlockSpec((tk, tn), lambda j, k: (k, j))],
        out_specs=pl.BlockSpec((m, tn), lambda j, k: (0, j)),
        out_shape=jax.ShapeDtypeStruct((m, n), f32),
        compiler_params=_params(("arbitrary", "arbitrary")),
        name=name,
    )(at, b)


def _exchange(scatter, gather, name):
    arrs = list(scatter) + list(gather)
    n = len(arrs)
    ns = len(scatter)

    def body(*refs):
        ins, outs = refs[:n], refs[n:2 * n]
        send_sems, recv_sems, local_sems = refs[2 * n:]
        x, y, c = lax.axis_index("x"), lax.axis_index("y"), lax.axis_index("c")
        me = 4 * x + 2 * y + c
        local, remote = [], []
        for a in range(n):
            lc = pltpu.make_async_copy(ins[a].at[me] if a < ns else ins[a], outs[a].at[me], local_sems.at[a])
            lc.start()
            local.append(lc)
            for r in range(1, NDEV):
                px = 1 - x if r & 4 else x
                py = 1 - y if r & 2 else y
                pc = 1 - c if r & 1 else c
                cp = pltpu.make_async_remote_copy(
                    src_ref=ins[a].at[4 * px + 2 * py + pc] if a < ns else ins[a],
                    dst_ref=outs[a].at[me],
                    send_sem=send_sems.at[a, r - 1],
                    recv_sem=recv_sems.at[a, r - 1],
                    device_id=(px, py, pc),
                    device_id_type=pl.DeviceIdType.MESH,
                )
                cp.start()
                remote.append(cp)
        for cp in remote:
            cp.wait_recv()
        for cp in remote:
            cp.wait_send()
        for lc in local:
            lc.wait()

    out_shape = [jax.ShapeDtypeStruct(a.shape if i < ns else (NDEV,) + a.shape, a.dtype) for i, a in enumerate(arrs)]
    return pl.pallas_call(
        body,
        in_specs=[pl.BlockSpec(memory_space=pl.ANY)] * n,
        out_specs=[pl.BlockSpec(memory_space=pl.ANY)] * n,
        out_shape=out_shape,
        scratch_shapes=[
            pltpu.SemaphoreType.DMA((n, NDEV - 1)),
            pltpu.SemaphoreType.DMA((n, NDEV - 1)),
            pltpu.SemaphoreType.DMA((n,)),
        ],
        compiler_params=pltpu.CompilerParams(has_side_effects=True),
        name=name,
    )(*arrs)


_HBM = pl.BlockSpec(memory_space=pltpu.HBM)
_SEM = pl.BlockSpec(memory_space=pltpu.SEMAPHORE)
_EFFECT = pltpu.SideEffectType.DATAFLOW_SIDE_EFFECTING


def _comm_step(name, body_fn, lands, srcs=(), wait_sems=(), n_new=0, after=(), token=False):
    n, ns, nw, na = len(lands), len(srcs), len(wait_sems), len(after)

    def body(*refs):
        src, land = refs[:ns], refs[ns:ns + n]
        waits = refs[ns + n:ns + n + nw]
        new = refs[ns + n + nw + na:ns + n + nw + na + n_new]
        body_fn(src, land, waits, new)
        if token:
            refs[-1][...] = jnp.zeros((8, LANES), f32)

    hbm = [pltpu.HBM(a.shape, a.dtype) for a in lands]
    ops = [pltpu.with_memory_space_constraint(a, pltpu.HBM) for a in list(srcs) + list(lands)]
    extra_shape = [jax.ShapeDtypeStruct((8, LANES), f32)] if token else []
    extra_spec = [pl.BlockSpec(memory_space=pltpu.VMEM)] if token else []
    outs = pl.pallas_call(
        body,
        out_shape=tuple([pltpu.SemaphoreType.DMA(())] * n_new + hbm + extra_shape),
        in_specs=[_HBM] * (ns + n) + [_SEM] * nw + [pl.BlockSpec(memory_space=pl.ANY)] * na,
        out_specs=tuple([_SEM] * n_new + [_HBM] * n + extra_spec),
        input_output_aliases={ns + i: n_new + i for i in range(n)},
        compiler_params=pltpu.CompilerParams(has_side_effects=_EFFECT),
        name=name,
    )(*ops, *wait_sems, *after)
    if token:
        return list(outs[:n_new]), list(outs[n_new:n_new + n]), outs[-1][0, 0]
    return list(outs[:n_new]), list(outs[n_new:])


class _GatheredWeights:
    def __init__(self, shards):
        self.n = n = len(shards)
        x, y, c = lax.axis_index("x"), lax.axis_index("y"), lax.axis_index("c")
        self.x = x
        me = 4 * x + 2 * y + c
        lands = [lax.dynamic_update_slice(lax.empty((NDEV,) + s.shape, s.dtype), s[None], (me,) + (0,) * s.ndim)
                 for s in shards]

        def start_own(src, land, waits, new):
            p = self._peers()
            for a in range(n):
                for k, to in ((0, p["sibling"]), (1, p["xn"]), (2, p["yn"])):
                    self._copy(land[a], new, a, k, 3, p["me"], to).start()

        self.sems, self.lands = {}, None
        new, self.lands = _comm_step("gather_start", start_own, lands, n_new=6 * n)
        self._keep(new, (0, 1, 2))

    @staticmethod
    def _peers():
        x, y, c = lax.axis_index("x"), lax.axis_index("y"), lax.axis_index("c")
        return dict(
            me=(x, y, c), sibling=(x, y, 1 - c), xn=(1 - x, y, c), yn=(x, 1 - y, c), dg=(1 - x, 1 - y, c),
            relay_origin=(jnp.bitwise_xor(x, c), jnp.bitwise_xor(y, 1 - c), c),
            relay_target=(jnp.bitwise_xor(x, 1 - c), jnp.bitwise_xor(y, c), c))

    def _keep(self, new, ks):
        half = len(new) // 2
        i = 0
        for a in range(self.n):
            for k in ks:
                self.sems[a, k] = (new[i], new[half + i])
                i += 1

    @staticmethod
    def _copy(land, sem_refs, a, k, nk, block, to, src=None, ks=None):
        ks = tuple(range(nk)) if ks is None else ks
        half = len(sem_refs) // 2
        i = a * len(ks) + ks.index(k)
        slot = land.at[4 * block[0] + 2 * block[1] + block[2]]
        return pltpu.make_async_remote_copy(
            src_ref=slot if src is None else src, dst_ref=slot, send_sem=sem_refs[i], recv_sem=sem_refs[half + i],
            device_id=to, device_id_type=pl.DeviceIdType.MESH)

    def _sem_list(self, ks):
        return ([self.sems[a, k][0] for a in range(self.n) for k in ks]
                + [self.sems[a, k][1] for a in range(self.n) for k in ks])

    def first_half(self, after):
        n = self.n

        def relay(src, land, waits, new):
            p = self._peers()
            for a in range(n):
                self._copy(land[a], waits, a, 1, 0, p["xn"], p["me"], ks=(1, 2)).wait_recv()
                self._copy(land[a], waits, a, 2, 0, p["yn"], p["me"], ks=(1, 2)).wait_recv()
                self._copy(land[a], new, a, 3, 0, p["relay_origin"], p["relay_target"], ks=(3, 4, 5)).start()
                self._copy(land[a], new, a, 4, 0, p["xn"], p["sibling"], ks=(3, 4, 5)).start()
                self._copy(land[a], new, a, 5, 0, p["yn"], p["sibling"], ks=(3, 4, 5)).start()

        new, self.lands = _comm_step("gather_relay", relay, self.lands, wait_sems=self._sem_list((1, 2)),
                                     n_new=6 * n, after=after)
        self._keep(new, (3, 4, 5))

        def from_sibling(src, land, waits, new):
            p = self._peers()
            other = lambda b: (b[0], b[1], 1 - b[2])
            for a in range(n):
                self._copy(land[a], waits, a, 0, 0, other(p["me"]), p["me"], ks=(0, 4, 5)).wait_recv()
                self._copy(land[a], waits, a, 4, 0, other(p["xn"]), p["me"], ks=(0, 4, 5)).wait_recv()
                self._copy(land[a], waits, a, 5, 0, other(p["yn"]), p["me"], ks=(0, 4, 5)).wait_recv()

        _, self.lands = _comm_step("gather_wait_sibling", from_sibling, self.lands,
                                   wait_sems=self._sem_list((0, 4, 5)))
        return self.lands[0].reshape(NW, D), self.x.astype(jnp.int32).reshape(1)

    def second_half(self, after):
        n = self.n

        def forward_diagonal(src, land, waits, new):
            p = self._peers()
            for a in range(n):
                self._copy(land[a], waits, a, 3, 0, p["dg"], p["me"], ks=(3,)).wait_recv()
                self._copy(land[a], new, a, 6, 0, p["dg"], p["sibling"], ks=(6,)).start()

        new, self.lands = _comm_step("gather_forward_diagonal", forward_diagonal, self.lands,
                                     wait_sems=self._sem_list((3,)), n_new=2 * n, after=after)
        self._keep(new, (6,))

        def finish(src, land, waits, new):
            p = self._peers()
            ks = tuple(range(7))
            for a in range(n):
                self._copy(land[a], waits, a, 6, 0, (p["dg"][0], p["dg"][1], 1 - p["dg"][2]), p["me"], ks=ks).wait_recv()
                for k in ks:
                    self._copy(land[a], waits, a, k, 0, p["me"], p["me"], ks=ks).wait_send()

        _, self.lands = _comm_step("gather_finish", finish, self.lands, wait_sems=self._sem_list(tuple(range(7))))
        return self.lands[0].reshape(NW, D), (1 - self.x).astype(jnp.int32).reshape(1)

    def rest(self):
        g_a, g_b, g_o, g_bm = self.lands[1:]
        return (g_a.transpose(1, 0, 2).reshape(512, D), g_b.transpose(1, 0, 2).reshape(512, D),
                g_bm.transpose(1, 0, 2).reshape(2, D), g_o.reshape(D, D))


def _sibling_send_start(shares):
    landing = lax.empty(shares.shape, shares.dtype)

    def start(src, land, waits, new):
        x, y, c = lax.axis_index("x"), lax.axis_index("y"), lax.axis_index("c")
        pltpu.make_async_remote_copy(src_ref=land[0], dst_ref=land[1], send_sem=new[0], recv_sem=new[1],
                                     device_id=(x, y, 1 - c), device_id_type=pl.DeviceIdType.MESH).start()

    return _comm_step("grad_sibling_start", start, [shares, landing], n_new=2, token=True)


def _sibling_send_wait(sems, lands, after):
    def wait(src, land, waits, new):
        x, y, c = lax.axis_index("x"), lax.axis_index("y"), lax.axis_index("c")
        done = pltpu.make_async_remote_copy(src_ref=land[0], dst_ref=land[1], send_sem=waits[0], recv_sem=waits[1],
                                            device_id=(x, y, c), device_id_type=pl.DeviceIdType.MESH)
        done.wait_send()
        done.wait_recv()

    _, lands = _comm_step("grad_sibling_wait", wait, lands, wait_sems=sems, after=after)
    return lands[1]


def _row_tile(rows, limit=256):
    fits = [t for t in range(16, limit + 1, 16) if rows % t == 0]
    return fits[-1] if fits else rows


def _pair_sum(mine, theirs, name):
    nb, rows, cols = mine.shape
    tr = _row_tile(rows, 528)

    def body(a_ref, b_ref, o_ref):
        o_ref[...] = (a_ref[...].astype(f32) + b_ref[...].astype(f32)).astype(bf16)

    blk = pl.BlockSpec((None, tr, cols), lambda q, i: (q, i, 0))
    return pl.pallas_call(
        body,
        grid=(nb, rows // tr),
        in_specs=[blk, blk],
        out_specs=blk,
        out_shape=jax.ShapeDtypeStruct(mine.shape, bf16),
        compiler_params=_params(("arbitrary", "arbitrary")),
        name=name,
    )(mine, theirs)


def _scatter_start(chip_arrs, all_arrs, name):
    arrs = list(chip_arrs) + list(all_arrs)
    n, nc = len(arrs), len(chip_arrs)
    lands = [lax.empty(((3 if i < nc else NDEV - 1),) + a.shape[1:], a.dtype) for i, a in enumerate(arrs)]

    def body(*refs):
        src, land = refs[:n], refs[n:2 * n]
        send_sems, recv_sems = refs[2 * n:3 * n], refs[3 * n:4 * n]
        token = refs[6 * n]
        x, y, c = lax.axis_index("x"), lax.axis_index("y"), lax.axis_index("c")
        for a in range(n):
            for r in range(1, 4 if a < nc else NDEV):
                if a < nc:
                    px, py, pc = (1 - x if r & 2 else x), (1 - y if r & 1 else y), c
                    block = 2 * px + py
                else:
                    px, py, pc = (1 - x if r & 4 else x), (1 - y if r & 2 else y), (1 - c if r & 1 else c)
                    block = 4 * px + 2 * py + pc
                pltpu.make_async_remote_copy(
                    src_ref=src[a].at[block], dst_ref=land[a].at[r - 1], send_sem=send_sems[a],
                    recv_sem=recv_sems[a], device_id=(px, py, pc), device_id_type=pl.DeviceIdType.MESH).start()
        token[...] = jnp.zeros_like(token)

    hbm = [pltpu.HBM(a.shape, a.dtype) for a in arrs + lands]
    ops = [pltpu.with_memory_space_constraint(a, pltpu.HBM) for a in arrs + lands]
    outs = pl.pallas_call(
        body,
        out_shape=tuple([pltpu.SemaphoreType.DMA(())] * (2 * n) + hbm + [jax.ShapeDtypeStruct((8, LANES), f32)]),
        in_specs=[_HBM] * (2 * n),
        out_specs=tuple([_SEM] * (2 * n) + [_HBM] * (2 * n) + [pl.BlockSpec(memory_space=pltpu.VMEM)]),
        input_output_aliases={i: 2 * n + i for i in range(2 * n)},
        compiler_params=pltpu.CompilerParams(has_side_effects=_EFFECT),
        name=name,
    )(*ops)
    return outs[:n], outs[n:2 * n], outs[2 * n:3 * n], outs[3 * n:4 * n], outs[4 * n]


def _scatter_wait(send_sems, recv_sems, srcs, lands, after, name):
    n = len(srcs)

    def body(*refs):
        land = refs[n:2 * n]
        ssem, rsem = refs[2 * n:3 * n], refs[3 * n:4 * n]
        x, y, c = lax.axis_index("x"), lax.axis_index("y"), lax.axis_index("c")
        for a in range(n):
            done = pltpu.make_async_remote_copy(
                src_ref=land[a], dst_ref=land[a], send_sem=ssem[a], recv_sem=rsem[a], device_id=(x, y, c),
                device_id_type=pl.DeviceIdType.MESH)
            done.wait_send()
            done.wait_recv()

    hbm = [pltpu.HBM(a.shape, a.dtype) for a in list(srcs) + list(lands)]
    outs = pl.pallas_call(
        body,
        out_shape=tuple(hbm),
        in_specs=[_HBM] * (2 * n) + [_SEM] * (2 * n) + [pl.BlockSpec(memory_space=pl.ANY)],
        out_specs=tuple([_HBM] * (2 * n)),
        input_output_aliases={i: i for i in range(2 * n)},
        compiler_params=pltpu.CompilerParams(has_side_effects=_EFFECT),
        name=name,
    )(*srcs, *lands, *send_sems, *recv_sems, after)
    return outs[:n], outs[n:]


def _adam_update(g, w_ref, m_ref, v_ref, g_ref, d_ref, nm_ref, nv_ref):
    mm = ADAM_B1 * m_ref[...] + (1.0 - ADAM_B1) * g
    vv = ADAM_B2 * v_ref[...] + (1.0 - ADAM_B2) * (g * g)
    m_hat = mm / (1.0 - ADAM_B1 ** ADAM_STEP)
    v_hat = vv / (1.0 - ADAM_B2 ** ADAM_STEP)
    g_ref[...] = g
    d_ref[...] = -ADAM_LR * (m_hat / (jnp.sqrt(v_hat) + ADAM_EPS) + ADAM_WD * w_ref[...])
    nm_ref[...] = mm
    nv_ref[...] = vv


def _adamw_own(w, own, own_idx, slots, m, v, name):
    r, c = w.shape[-2:]
    tr = _row_tile(r, 384)
    k = slots.shape[0]

    def body(i_ref, w_ref, o_ref, s_ref, m_ref, v_ref, g_ref, d_ref, nm_ref, nv_ref):
        del i_ref
        g = o_ref[...].astype(f32)
        for j in range(k):
            g = g + s_ref[j].astype(f32)
        _adam_update(g, w_ref, m_ref, v_ref, g_ref, d_ref, nm_ref, nv_ref)

    blk = pl.BlockSpec((None, tr, c), lambda i, ix: (0, i, 0))
    return pl.pallas_call(
        body,
        grid_spec=pltpu.PrefetchScalarGridSpec(
            num_scalar_prefetch=1,
            grid=(r // tr,),
            in_specs=[blk, pl.BlockSpec((None, tr, c), lambda i, ix: (ix[0], i, 0)),
                      pl.BlockSpec((k, tr, c), lambda i, ix: (0, i, 0)), blk, blk],
            out_specs=[blk] * 4,
        ),
        out_shape=[jax.ShapeDtypeStruct(w.shape, f32)] * 4,
        compiler_params=_params(("arbitrary",)),
        name=name,
    )(own_idx, w, own, slots, m, v)


def _adamw(w, slots, m, v, name):
    r, c = w.shape[-2:]
    tr = _row_tile(r, 128)

    def body(w_ref, s_ref, m_ref, v_ref, g_ref, d_ref, nm_ref, nv_ref):
        g = s_ref[0].astype(f32)
        for k in range(1, NDEV):
            g = g + s_ref[k].astype(f32)
        _adam_update(g, w_ref, m_ref, v_ref, g_ref, d_ref, nm_ref, nv_ref)

    if w.ndim == 3:
        blk = pl.BlockSpec((None, tr, c), lambda i: (0, i, 0))
    else:
        blk = pl.BlockSpec((tr, c), lambda i: (i, 0))
    return pl.pallas_call(
        body,
        grid=(r // tr,),
        in_specs=[blk, pl.BlockSpec((NDEV, tr, c), lambda i: (0, i, 0)), blk, blk],
        out_specs=[blk] * 4,
        out_shape=[jax.ShapeDtypeStruct(w.shape, f32)] * 4,
        compiler_params=_params(("arbitrary",)),
        name=name,
    )(w, slots, m, v)


class _Weights:
    def __init__(self, w_t, w_a, w_b, b_merge, w_o):
        self._w_t, self._rest = w_t, (w_a, w_b, b_merge, w_o)

    def first_half(self, after):
        del after
        return self._w_t, jnp.zeros((1,), jnp.int32)

    def second_half(self, after):
        del after
        return self._w_t, jnp.ones((1,), jnp.int32)

    def rest(self):
        return self._rest


def _local_step(x, tgt, norm_gain, weights, qn_a, kn_a, qn_b, kn_b, sink_a, rel_bias, on_weight_grads=None,
                core=None):
    two = lambda t: jnp.concatenate([t, t], axis=-1).reshape(1, LANES)
    ones = jnp.ones((1, LANES), f32)
    gains = jnp.stack([
        jnp.stack([two(qn_a), two(kn_a), ones]),
        jnp.stack([two(qn_b), two(kn_b), ones]),
        jnp.stack([two(qn_b), two(kn_b), ones]),
        jnp.stack([two(qn_b), two(kn_b), ones]),
    ])
    buckets = [jnp.asarray(_bucket_np(blk, d)) for blk, d, _ in GROUPS]
    bias = [_bias_expand(rel_bias, buckets[k], GROUPS[k][2], "bias_expand_%d" % k) for k in range(4)]

    hb, hbt, rstd = _rms(x, norm_gain)
    w_t, half = weights.first_half([hb] + bias)
    proj = _inproj_half(hb, w_t, half, None, "inproj_1")
    w_t, half = weights.second_half([proj])
    proj = _inproj_half(hb, w_t, half, proj, "inproj_2")
    w_a, w_b, b_merge, w_o = weights.rest()
    gl = _prep(proj, gains)
    o_a, l_a = _attn_fwd(gl, bias[0], sink_a.reshape(8), 0, 128, 1, "attn_fwd_a")
    fwd_b = [_attn_fwd(gl, bias[k], None, k, GROUPS[k][0], GROUPS[k][1], "attn_fwd_b%d" % k) for k in (1, 2, 3)]
    sink_b = jnp.repeat(sink_a.reshape(8), HD).reshape(1, 512)

    (dy, dyb, dproj, do_a, dd_a, do_b0, do_b1, do_b2, dd_b0, dd_b1, dd_b2, ya, yb, mg, dbr_a, dbr_b, loss, dbm,
     dsk) = _tail(x, tgt, o_a, l_a, [f[0] for f in fwd_b], [f[1] for f in fwd_b], proj, b_merge, w_a, w_b, w_o, sink_b)

    dw_o = _matmul_tokens(mg, dyb, "dw_out")
    dw_a = _matmul_tokens(ya, dbr_a, "dw_branch_a")
    dw_b = _matmul_tokens(yb, dbr_b, "dw_branch_b")
    if on_weight_grads is not None:
        early = on_weight_grads(dict(w_branch_a=dw_a, w_branch_b=dw_b, b_merge=dbm, w_out=dw_o))
        buckets = [buckets[0] + early.astype(jnp.int32)] + buckets[1:]

    dqkv_a, dbk_a = _attn_bwd(gl, bias[0], buckets[0], do_a, l_a, dd_a, 0, 128, 1, "attn_bwd_a")
    dproj, dg_a = _post_a(dqkv_a, proj, gains, dproj)
    dbk_b, dg_b = [], []
    for k, do_k, dd_k in ((1, do_b0, dd_b0), (2, do_b1, dd_b1), (3, do_b2, dd_b2)):
        dqkv, dbk = _attn_bwd(gl, bias[k], buckets[k], do_k, fwd_b[k - 1][1], dd_k, k, GROUPS[k][0], GROUPS[k][1],
                              "attn_bwd_b%d" % k)
        dproj, dg = _post_b(k, dqkv, proj, gains, dproj)
        dbk_b.append(dbk)
        dg_b.append(dg)
    dg_b = jnp.stack(dg_b)

    core = jnp.zeros((1,), jnp.int32) if core is None else core
    dw_other = _dw_in(hbt, dproj, 1 - core, "dw_in_other")
    sent = jnp.zeros((), f32) if on_weight_grads is None else on_weight_grads(dict(w_in_other=dw_other))
    dw_in = _dw_in(hbt, dproj, core + sent.astype(jnp.int32), "dw_in_own")
    token = jnp.zeros((), f32) if on_weight_grads is None else on_weight_grads(dict(w_in=dw_in))
    grad_x, d_norm_gain = _dh_norm_bwd(dproj, w_t, x, rstd, norm_gain + token, dy)

    fold = lambda t: t[..., :HD] + t[..., HD:]
    d_qn_a = fold(dg_a[0, 0])
    d_kn_a = fold(dg_a[1, 0])
    d_qn_b = fold(dg_b[:, 0, 0].sum(axis=0))
    d_kn_b = fold(dg_b[:, 1, 0].sum(axis=0))
    d_sink = dsk.reshape(8, HD)[:, 0]
    red = jnp.stack([dbk_a] + dbk_b)
    d_rel = red[:, :, 0, :32].reshape(32, 32).T
    return dict(loss=loss, grad_x=grad_x, norm_gain=d_norm_gain, w_in=dw_in, w_in_other=dw_other, q_norm_a=d_qn_a,
                k_norm_a=d_kn_a,
                q_norm_b=d_qn_b, k_norm_b=d_kn_b, sink_a=d_sink, rel_bias=d_rel, w_branch_a=dw_a, w_branch_b=dw_b,
                b_merge=dbm, w_out=dw_o)


SMALL = (("norm_gain", D), ("q_norm_a", HD), ("k_norm_a", HD), ("q_norm_b", HD), ("k_norm_b", HD), ("sink_a", 8),
         ("rel_bias", 1024))
SMALL_PAD = 2432


SMALL_USED = sum(sz for _, sz in SMALL)


def _pack_small(parts, loss=None):
    tail = jnp.zeros((SMALL_PAD - SMALL_USED,), f32)
    if loss is not None:
        tail = tail.at[0].set(loss.reshape(()))
    return jnp.concatenate([parts[n].reshape(-1) for n, _ in SMALL] + [tail]).reshape(1, SMALL_PAD)


def _unpack_small(flat, shapes):
    out, off = {}, 0
    for n, sz in SMALL:
        out[n] = flat[0, off:off + sz].reshape(shapes[n])
        off += sz
    return out


def kernel(x, norm_gain, w_in, q_norm_a, k_norm_a, q_norm_b, k_norm_b, sink_a, rel_bias, w_branch_a, w_branch_b, b_merge, w_out, loss_target, m_norm_gain, m_w_in, m_q_norm_a, m_k_norm_a, m_q_norm_b, m_k_norm_b, m_sink_a, m_rel_bias, m_w_branch_a, m_w_branch_b, m_b_merge, m_w_out, v_norm_gain, v_w_in, v_q_norm_a, v_k_norm_a, v_q_norm_b, v_k_norm_b, v_sink_a, v_rel_bias, v_w_branch_a, v_w_branch_b, v_b_merge, v_w_out):
    csh = D // NDEV
    w_in_t, m_w_in_t, v_w_in_t = (jnp.swapaxes(t, 1, 2) for t in (w_in, m_w_in, v_w_in))
    weights = _GatheredWeights([w_in_t[0].astype(bf16), w_branch_a[0].astype(bf16), w_branch_b[0].astype(bf16),
                                w_out[0].astype(bf16), b_merge[0]])

    pending = {}
    core = lax.axis_index("c").astype(jnp.int32).reshape(1)
    chip = (2 * lax.axis_index("x") + lax.axis_index("y")).astype(jnp.int32).reshape(1)
    me = (2 * chip + core).astype(jnp.int32)

    def start_exchange(gw):
        if "w_in_other" in gw:
            sems, lands, sent = _sibling_send_start(gw["w_in_other"])
            pending["sibling"] = (sems, lands)
            return sent
        if "w_in" in gw:
            from_sibling = _sibling_send_wait(*pending["sibling"], after=[gw["w_in"]])
            chip_sums = _pair_sum(gw["w_in"], from_sibling, "grad_pair_sum")
            pending["w_in"] = _scatter_start([chip_sums], [], "scatter_w_in_start")
            return pending["w_in"][4][0, 0]
        blocks = [gw["w_branch_a"].reshape(512, NDEV, csh).transpose(1, 0, 2).astype(bf16),
                  gw["w_branch_b"].reshape(512, NDEV, csh).transpose(1, 0, 2).astype(bf16),
                  gw["w_out"].reshape(NDEV, csh, D).astype(bf16),
                  gw["b_merge"].reshape(2, NDEV, csh).transpose(1, 0, 2)]
        pending["rest"] = _scatter_start([], blocks, "scatter_rest_start")
        return pending["rest"][4][0, 0]

    loc = _local_step(x[0], loss_target[0], norm_gain, weights, q_norm_a, k_norm_a, q_norm_b, k_norm_b, sink_a,
                      rel_bias, on_weight_grads=start_exchange, core=core)

    small_shapes = dict(norm_gain=(1, D), q_norm_a=(1, HD), k_norm_a=(1, HD), q_norm_b=(1, HD), k_norm_b=(1, HD),
                        sink_a=(1, 8), rel_bias=(32, 32))
    (r_small,) = _exchange([], [_pack_small(loc, loc["loss"])], "gather_small_grads")
    send_sems, recv_sems, srcs, lands, _ = pending["rest"]
    (s_a, s_b, s_o, s_bm), (r_a, r_b, r_o, r_bm) = _scatter_wait(
        send_sems, recv_sems, srcs, lands, r_small, "scatter_rest_wait")
    send_sems, recv_sems, srcs, lands, _ = pending["w_in"]
    (s_in,), (r_in,) = _scatter_wait(send_sems, recv_sems, srcs, lands, r_small, "scatter_w_in_wait")

    given = dict(norm_gain=norm_gain, q_norm_a=q_norm_a, k_norm_a=k_norm_a, q_norm_b=q_norm_b, k_norm_b=k_norm_b,
                 sink_a=sink_a, rel_bias=rel_bias)
    m_small = dict(norm_gain=m_norm_gain, q_norm_a=m_q_norm_a, k_norm_a=m_k_norm_a, q_norm_b=m_q_norm_b,
                   k_norm_b=m_k_norm_b, sink_a=m_sink_a, rel_bias=m_rel_bias)
    v_small = dict(norm_gain=v_norm_gain, q_norm_a=v_q_norm_a, k_norm_a=v_k_norm_a, q_norm_b=v_q_norm_b,
                   k_norm_b=v_k_norm_b, sink_a=v_sink_a, rel_bias=v_rel_bias)
    res = {
        "small": _adamw(_pack_small(given), r_small, _pack_small(m_small), _pack_small(v_small), "adamw_small"),
        "w_in": [jnp.swapaxes(t, 1, 2) for t in
                 _adamw_own(w_in_t, s_in, chip, r_in, m_w_in_t, v_w_in_t, "adamw_w_in")],
        "w_branch_a": _adamw_own(w_branch_a, s_a, me, r_a, m_w_branch_a, v_w_branch_a, "adamw_w_branch_a"),
        "w_branch_b": _adamw_own(w_branch_b, s_b, me, r_b, m_w_branch_b, v_w_branch_b, "adamw_w_branch_b"),
        "b_merge": _adamw_own(b_merge, s_bm, me, r_bm, m_b_merge, v_b_merge, "adamw_b_merge"),
        "w_out": _adamw_own(w_out, s_o, me, r_o, m_w_out, v_w_out, "adamw_w_out"),
    }
    order = ["norm_gain", "w_in", "q_norm_a", "k_norm_a", "q_norm_b", "k_norm_b", "sink_a", "rel_bias", "w_branch_a",
             "w_branch_b", "b_merge", "w_out"]
    outs = []
    for k in range(4):
        small = _unpack_small(res["small"][k], small_shapes)
        for n in order:
            outs.append(small[n] if n in small else res[n][k])
    loss = res["small"][0][0, SMALL_USED]
    return (loss, loc["grad_x"][None], *outs)
```

```python
import math

import numpy as np
import jax
import jax.numpy as jnp
from jax import lax
from jax.experimental import pallas as pl
from jax.experimental.pallas import tpu as pltpu

f32 = jnp.float32
bf16 = jnp.bfloat16

S = 4096
D = 1024
NA = 5376
NT = 3072
NW = NA + NT
WSH = NW // 8
HD = 64
LANES = 128
EPS = 1e-6
NEG = -1e30
SCALE = HD ** -0.5
TQ = 128
PAD = 128
SP = S + 2 * PAD
NDEV = 8
GROUPS = ((128, 1, 0), (64, 1, 8), (64, 4, 16), (64, 16, 24))
CHUNK = 256
PCHUNK = 128
RC = 64

ADAM_LR, ADAM_B1, ADAM_B2, ADAM_EPS, ADAM_WD, ADAM_STEP = 0.001, 0.9, 0.999, 1e-08, 0.01, 10

MIB = 1024 * 1024
NT_DIMS = (((1,), (1,)), ((), ()))
TN_DIMS = (((0,), (0,)), ((), ()))


def _params(sem=None, vmem_mib=48):
    return pltpu.CompilerParams(dimension_semantics=sem, vmem_limit_bytes=vmem_mib * MIB)


def _lo():
    return lax.broadcasted_iota(jnp.int32, (1, LANES), 1) < HD


def _head_ones():
    r = lax.broadcasted_iota(jnp.int32, (LANES, LANES), 0) // HD
    c = lax.broadcasted_iota(jnp.int32, (LANES, LANES), 1) // HD
    return jnp.where(r == c, 1.0, 0.0).astype(bf16)


def _half_sums(x, ones):
    hi = x.astype(bf16)
    mid = (x - hi.astype(f32)).astype(bf16)
    return (jnp.dot(hi, ones, preferred_element_type=f32) + jnp.dot(mid, ones, preferred_element_type=f32))


def _seg_sum(x, ones):
    outs = [_half_sums(x[:, b * LANES:(b + 1) * LANES], ones) for b in range(x.shape[1] // LANES)]
    return outs[0] if len(outs) == 1 else jnp.concatenate(outs, axis=1)


def _bucket_np(blk, stride):
    w = TQ + 2 * blk
    rel = np.arange(w)[None, :] - blk - np.arange(TQ)[:, None]
    band = np.abs(rel) <= blk
    r = rel * stride
    n = np.abs(r)
    nf = np.maximum(n, 8).astype(np.float32)
    large = 8 + (np.log(nf / np.float32(8)) / np.float32(math.log(128.0)) * np.float32(8)).astype(np.int32)
    large = np.minimum(large, 15)
    b = (r > 0).astype(np.int32) * 16 + np.where(n < 8, n, large)
    return np.where(band, b, -1).astype(np.int32)


def _rms(x, gain):
    ts = 512

    def body(x_ref, g_ref, h_ref, ht_ref, r_ref):
        xv = x_ref[...]
        r = lax.rsqrt(jnp.mean(xv * xv, axis=-1, keepdims=True) + EPS)
        h = (xv * r) * g_ref[...]
        h_ref[...] = h.astype(bf16)
        ht_ref[...] = h.T.astype(bf16)
        r_ref[...] = r

    return pl.pallas_call(
        body,
        grid=(S // ts,),
        in_specs=[pl.BlockSpec((ts, D), lambda i: (i, 0)), pl.BlockSpec((1, D), lambda i: (0, 0))],
        out_specs=[pl.BlockSpec((ts, D), lambda i: (i, 0)), pl.BlockSpec((D, ts), lambda i: (0, i)),
                   pl.BlockSpec((ts, 1), lambda i: (i, 0))],
        out_shape=[jax.ShapeDtypeStruct((S, D), bf16), jax.ShapeDtypeStruct((D, S), bf16),
                   jax.ShapeDtypeStruct((S, 1), f32)],
        compiler_params=_params(("arbitrary",)),
        name="rms",
    )(x, gain)


def _inproj_half(hb, w_t, half, proj, name):
    ts = 512
    tn = NW // 2
    per = NW // 2 // tn

    def body(h_idx, h_ref, w_ref, *rest):
        del h_idx
        rest[-1][...] = lax.dot_general(h_ref[...], w_ref[...], NT_DIMS, preferred_element_type=f32)

    in_specs = [pl.BlockSpec((ts, D), lambda i, n, hf: (i, 0)),
                pl.BlockSpec((tn, D), lambda i, n, hf: (hf[0] * per + n, 0))]
    args = [half, hb, w_t]
    aliases = {}
    if proj is not None:
        in_specs.append(pl.BlockSpec(memory_space=pl.ANY))
        args.append(proj)
        aliases = {3: 0}
    return pl.pallas_call(
        body,
        grid_spec=pltpu.PrefetchScalarGridSpec(
            num_scalar_prefetch=1,
            grid=(S // ts, per),
            in_specs=in_specs,
            out_specs=pl.BlockSpec((ts, tn), lambda i, n, hf: (i, hf[0] * per + n)),
        ),
        out_shape=jax.ShapeDtypeStruct((S, NW), f32),
        input_output_aliases=aliases,
        compiler_params=_params(("arbitrary", "arbitrary")),
        name=name,
    )(*args)


def _bias_expand(table, bucket, c0, name):
    tq, w = bucket.shape
    blk = (w - tq) // 2

    def body(tab_ref, bk_ref, o_ref):
        h = pl.program_id(0)
        bk = bk_ref[...]

        def step(b, acc):
            return jnp.where(bk == b, tab_ref[b, c0 + h], acc)

        inner = lax.fori_loop(0, 32, step, jnp.full((tq, w), NEG, f32))
        col = lax.broadcasted_iota(jnp.int32, (1, w), 1)
        o_ref[0] = jnp.where(col < blk, NEG, inner)
        o_ref[1] = inner
        o_ref[2] = jnp.where(col >= tq + blk, NEG, inner)

    return pl.pallas_call(
        body,
        grid=(8,),
        in_specs=[pl.BlockSpec(memory_space=pltpu.SMEM), pl.BlockSpec((tq, w), lambda h: (0, 0))],
        out_specs=pl.BlockSpec((3, None, tq, w), lambda h: (0, h, 0, 0)),
        out_shape=jax.ShapeDtypeStruct((3, 8, tq, w), f32),
        compiler_params=_params(("arbitrary",)),
        name=name,
    )(table, bucket)


def _tile_kind(t, seq):
    m0 = jnp.bitwise_and(t * TQ, seq - 1)
    return jnp.where(m0 == 0, 0, jnp.where(m0 == seq - TQ, 2, 1))


def _col_block(g, j):
    kind = j // 4
    hp = j % 4
    a = jnp.where(kind == 0, hp, 3 + kind)
    b = 6 + 12 * kind + 4 * (g - 1) + hp
    return jnp.where(g == 0, a, b)


def _prep(proj_a, gains):
    nslot, nstep = 3, 12

    def body(p_hbm, g_ref, o_ref, bufs, sems):
        g = pl.program_id(0)
        kind = pl.program_id(1)
        step = 3 * g + kind

        def fetch(s, slot):
            return [pltpu.make_async_copy(
                p_hbm.at[:, pl.ds(pl.multiple_of(_col_block(s // 3, 4 * (s % 3) + u) * LANES, LANES), LANES)],
                bufs.at[slot, u], sems.at[slot, u]) for u in range(4)]

        @pl.when(step == 0)
        def _():
            for s in range(nslot - 1):
                for u, cp in enumerate(fetch(s, s)):
                    cp.start(priority=u % 2)

        @pl.when(step + nslot - 1 < nstep)
        def _():
            for u, cp in enumerate(fetch(step + nslot - 1, (step + nslot - 1) % nslot)):
                cp.start(priority=u % 2)

        slot = step % nslot
        for cp in fetch(step, slot):
            cp.wait()
        p0_ref, p1_ref, p2_ref, p3_ref = (bufs.at[slot, u] for u in range(4))
        lo = _lo()
        ones = _head_ones()
        half = jnp.where(lo, 0, 1)
        gain = g_ref[...]

        def norm_store(xv, u, dst, dup):
            if dup:
                take = (kind == 0) | (half == u // 2)
                xv = jnp.where(take, xv, pltpu.roll(xv, HD, 1))
            r = lax.rsqrt(_half_sums(xv * xv, ones) * (1.0 / HD) + EPS)
            r = jnp.where(kind == 2, 1.0, r)
            yv = (xv * r) * gain
            yv = jnp.where(kind == 0, yv * SCALE, yv)
            o_ref[u, PAD + dst:PAD + dst + CHUNK, :] = yv.astype(bf16)

        for u in range(4):
            o_ref[u, 0:PAD, :] = jnp.zeros((PAD, LANES), bf16)
            o_ref[u, PAD + S:SP, :] = jnp.zeros((PAD, LANES), bf16)
        for gi, (_, d, _) in enumerate(GROUPS):
            @pl.when(g == gi)
            def _():
                seq = S // d
                for u, p_ref in enumerate((p0_ref, p1_ref, p2_ref, p3_ref)):
                    for c in range(d):
                        for i in range(seq // CHUNK):
                            if d == 1:
                                xv = p_ref[i * CHUNK:(i + 1) * CHUNK, :]
                            else:
                                xv = p_ref[pl.ds(c + i * CHUNK * d, CHUNK, stride=d), :]
                            norm_store(xv, u, c * seq + i * CHUNK, gi == 0)

    return pl.pallas_call(
        body,
        grid=(4, 3),
        in_specs=[
            pl.BlockSpec(memory_space=pl.ANY),
            pl.BlockSpec((None, None, 1, LANES), lambda g, kind: (g, kind, 0, 0)),
        ],
        out_specs=pl.BlockSpec((None, 4, SP, LANES), lambda g, kind: (g, kind, 0, 0)),
        out_shape=jax.ShapeDtypeStruct((4, 12, SP, LANES), bf16),
        scratch_shapes=[pltpu.VMEM((nslot, 4, S, LANES), f32), pltpu.SemaphoreType.DMA((nslot, 4))],
        compiler_params=_params(("arbitrary", "arbitrary")),
        name="prep",
    )(proj_a, gains)


def _token_rows(t, r0, n, d):
    if d == 1:
        return pl.ds(pl.multiple_of(t * TQ, TQ) + r0, n)
    per = S // d // TQ
    return pl.ds(((t % per) * TQ + r0) * d + t // per, n, stride=d)


def _stack_heads(t, lo):
    z = jnp.zeros_like(t)
    return jnp.concatenate([jnp.where(lo, t, z), jnp.where(lo, z, t)], axis=0)


def _unstack_heads(t2, lo):
    return jnp.where(lo, t2[:TQ], t2[TQ:])


def _attn_fwd(gl, bias, sink, g, blk, d, name):
    w = TQ + 2 * blk
    seq = S // d
    use_sink = sink is not None

    def body(*refs):
        if use_sink:
            sink_ref, q_ref, k_ref, v_ref, b_ref, o_ref, l_ref, s0, s1, p0, p1, lse_scr = refs
        else:
            q_ref, k_ref, v_ref, b_ref, o_ref, l_ref, s0, s1, p0, p1, lse_scr = refs
        hp = pl.program_id(0)
        lo = _lo()
        s_bufs, p_bufs = (s0, s1), (p0, p1)

        def scores(p, slot):
            for u in range(2):
                f0 = pl.multiple_of((2 * p + u) * TQ, TQ)
                q2 = _stack_heads(q_ref[pl.ds(PAD + f0, TQ), :], lo)
                kw = k_ref[pl.ds(PAD - blk + f0, w), :]
                s_bufs[slot][u] = lax.dot_general(q2, kw, NT_DIMS, preferred_element_type=f32)

        def softmax(p, slot):
            for u in range(2):
                t = 2 * p + u
                kind = _tile_kind(t, seq)
                for h in range(2):
                    for r in range(TQ // RC):
                        rows = slice(h * TQ + r * RC, h * TQ + (r + 1) * RC)
                        logit = s_bufs[slot][u, rows, :] + b_ref[kind, h, r * RC:(r + 1) * RC, :]
                        m = jnp.max(logit, axis=1, keepdims=True)
                        e = jnp.exp(logit - m)
                        lse = m + jnp.log(jnp.sum(e, axis=1, keepdims=True))
                        if use_sink:
                            sk = sink_ref[2 * hp + h]
                            mx = jnp.maximum(lse, sk)
                            lse = mx + jnp.log(jnp.exp(lse - mx) + jnp.exp(sk - mx))
                        p_bufs[slot][u, rows, :] = (e * jnp.exp(m - lse)).astype(bf16)
                        lse_scr[u, rows, :] = jnp.broadcast_to(lse, (RC, LANES))
                l_ref[_token_rows(t, 0, TQ, d), :] = jnp.where(lo, lse_scr[u, 0:TQ, :], lse_scr[u, TQ:2 * TQ, :])

        def values(p, slot):
            for u in range(2):
                t = 2 * p + u
                vw = v_ref[pl.ds(PAD - blk + pl.multiple_of(t * TQ, TQ), w), :]
                o2 = jnp.dot(p_bufs[slot][u], vw, preferred_element_type=f32)
                o_ref[_token_rows(t, 0, TQ, d), :] = _unstack_heads(o2, lo)

        npair = S // TQ // 2
        scores(0, 0)
        scores(1, 1)
        softmax(0, 0)

        def steady(k, carry):
            p = 2 * k + 2
            scores(p, 0)
            softmax(p - 1, 1)
            values(p - 2, 0)
            scores(p + 1, 1)
            softmax(p, 0)
            values(p - 1, 1)
            return carry

        lax.fori_loop(0, (npair - 2) // 2, steady, 0)
        softmax(npair - 1, 1)
        values(npair - 2, 0)
        values(npair - 1, 1)

    in_specs = [
        pl.BlockSpec((None, None, SP, LANES), lambda hp: (g, hp, 0, 0)),
        pl.BlockSpec((None, None, SP, LANES), lambda hp: (g, 4 + hp, 0, 0)),
        pl.BlockSpec((None, None, SP, LANES), lambda hp: (g, 8 + hp, 0, 0)),
        pl.BlockSpec((3, 2, TQ, w), lambda hp: (0, hp, 0, 0)),
    ]
    args = [gl, gl, gl, bias]
    if use_sink:
        in_specs = [pl.BlockSpec(memory_space=pltpu.SMEM)] + in_specs
        args = [sink] + args
    out = pl.BlockSpec((S, LANES), lambda hp: (0, hp))
    return pl.pallas_call(
        body,
        grid=(4,),
        in_specs=in_specs,
        out_specs=[out, out],
        out_shape=[jax.ShapeDtypeStruct((S, 4 * LANES), f32)] * 2,
        scratch_shapes=[pltpu.VMEM((2, 2 * TQ, w), f32), pltpu.VMEM((2, 2 * TQ, w), f32),
                        pltpu.VMEM((2, 2 * TQ, w), bf16), pltpu.VMEM((2, 2 * TQ, w), bf16),
                        pltpu.VMEM((2, 2 * TQ, LANES), f32)],
        compiler_params=_params(("arbitrary",)),
        name=name,
    )(*args)


def _attn_bwd(gl, bias, bucket, do, lse, dd, g, blk, d, name):
    w = TQ + 2 * blk
    seq = S // d

    def body(q_ref, k_ref, v_ref, b_ref, bk_ref, do_ref, l_ref, d_ref, dqkv_ref, dbk_ref,
             db_acc, s0, s1, dp0, dp1, pb0, pb1, ds0, ds1, dk_acc, dv_acc):
        lo = _lo()
        hi = jnp.logical_not(lo)
        dk_acc[...] = jnp.zeros((SP, LANES), f32)
        dv_acc[...] = jnp.zeros((SP, LANES), f32)
        db_acc[...] = jnp.zeros((2 * TQ, w), f32)
        s_bufs, dp_bufs, pb_bufs, ds_bufs = (s0, s1), (dp0, dp1), (pb0, pb1), (ds0, ds1)

        def stacked(t):
            f0 = pl.multiple_of(t * TQ, TQ)
            q2 = _stack_heads(q_ref[pl.ds(PAD + f0, TQ), :], lo)
            do2 = _stack_heads(do_ref[_token_rows(t, 0, TQ, d), :].astype(bf16), lo)
            return f0, q2, do2

        def scores(p, slot):
            for u in range(2):
                f0, q2, do2 = stacked(2 * p + u)
                win = pl.ds(PAD - blk + f0, w)
                s_bufs[slot][u] = lax.dot_general(q2, k_ref[win, :], NT_DIMS, preferred_element_type=f32)
                dp_bufs[slot][u] = lax.dot_general(do2, v_ref[win, :], NT_DIMS, preferred_element_type=f32)

        def grads(p, slot):
            for u in range(2):
                t = 2 * p + u
                kind = _tile_kind(t, seq)
                for h in range(2):
                    msk = lo if h == 0 else hi
                    for r in range(TQ // RC):
                        rows = slice(h * TQ + r * RC, h * TQ + (r + 1) * RC)
                        src = _token_rows(t, r * RC, RC, d)
                        lh = jnp.max(jnp.where(msk, l_ref[src, :], -jnp.inf), axis=1, keepdims=True)
                        dh = jnp.max(jnp.where(msk, d_ref[src, :], -jnp.inf), axis=1, keepdims=True)
                        logit = s_bufs[slot][u, rows, :] + b_ref[kind, h, r * RC:(r + 1) * RC, :]
                        pr = jnp.exp(logit - lh)
                        ds = pr * (dp_bufs[slot][u, rows, :] - dh)
                        db_acc[rows, :] += ds
                        pb_bufs[slot][u, rows, :] = pr.astype(bf16)
                        ds_bufs[slot][u, rows, :] = ds.astype(bf16)

        def accumulate(p, slot):
            for u in range(2):
                f0, q2, do2 = stacked(2 * p + u)
                win = pl.ds(PAD - blk + f0, w)
                dsb = ds_bufs[slot][u]
                dq2 = jnp.dot(dsb, k_ref[win, :], preferred_element_type=f32)
                dqkv_ref[0, pl.ds(PAD + f0, TQ), :] = _unstack_heads(dq2, lo).astype(bf16)
                dk_acc[win, :] += lax.dot_general(dsb, q2, TN_DIMS, preferred_element_type=f32)
                dv_acc[win, :] += lax.dot_general(pb_bufs[slot][u], do2, TN_DIMS, preferred_element_type=f32)

        npair = S // TQ // 2
        scores(0, 0)
        scores(1, 1)
        grads(0, 0)

        def steady(k, carry):
            p = 2 * k + 2
            scores(p, 0)
            grads(p - 1, 1)
            accumulate(p - 2, 0)
            scores(p + 1, 1)
            grads(p, 0)
            accumulate(p - 1, 1)
            return carry

        lax.fori_loop(0, (npair - 2) // 2, steady, 0)
        grads(npair - 1, 1)
        accumulate(npair - 2, 0)
        accumulate(npair - 1, 1)
        for i in range(SP // CHUNK):
            rows = slice(i * CHUNK, (i + 1) * CHUNK)
            dqkv_ref[1, rows, :] = dk_acc[rows, :].astype(bf16)
            dqkv_ref[2, rows, :] = dv_acc[rows, :].astype(bf16)

        bk = bk_ref[...]
        lane = lax.broadcasted_iota(jnp.int32, (8, LANES), 1)
        for h in range(2):
            db = db_acc[h * TQ:(h + 1) * TQ, :]
            acc = jnp.zeros((8, LANES), f32)
            for b in range(32):
                part = jnp.where(bk == b, db, 0.0).reshape(TQ // 8, 8, w).sum(axis=0)
                tot = jnp.sum(jnp.sum(part, axis=1, keepdims=True), axis=0, keepdims=True)
                acc = jnp.where(lane == b, tot, acc)
            dbk_ref[h] = acc

    def gcol(off):
        return pl.BlockSpec((None, None, SP, LANES), lambda hp: (g, off + hp, 0, 0))

    row = pl.BlockSpec((S, LANES), lambda hp: (0, hp))
    return pl.pallas_call(
        body,
        grid=(4,),
        in_specs=[gcol(0), gcol(4), gcol(8), pl.BlockSpec((3, 2, TQ, w), lambda hp: (0, hp, 0, 0)),
                  pl.BlockSpec((TQ, w), lambda hp: (0, 0)), row, row, row],
        out_specs=[pl.BlockSpec((3, None, SP, LANES), lambda hp: (0, hp, 0, 0)),
                   pl.BlockSpec((2, 8, LANES), lambda hp: (hp, 0, 0))],
        out_shape=[
            jax.ShapeDtypeStruct((3, 4, SP, LANES), bf16),
            jax.ShapeDtypeStruct((8, 8, LANES), f32),
        ],
        scratch_shapes=([pltpu.VMEM((2 * TQ, w), f32)] + [pltpu.VMEM((2, 2 * TQ, w), f32)] * 4
                        + [pltpu.VMEM((2, 2 * TQ, w), bf16)] * 4 + [pltpu.VMEM((SP, LANES), f32)] * 2),
        compiler_params=_params(("arbitrary",), vmem_mib=56),
        name=name,
    )(gl, gl, gl, bias, bucket, do, lse, dd)


def _sigmoid(z):
    return 1.0 / (1.0 + jnp.exp(-z))


def _tail(x, tgt, o_a, l_a, o_b, l_b, proj, bm, w_a, w_b, w_o, sink_b):
    ts = 256

    def body(x_ref, t_ref, oa_ref, la_ref, ob0_ref, ob1_ref, ob2_ref, lb0_ref, lb1_ref, lb2_ref,
             ga_ref, gb_ref, m0_ref, m1_ref, bm_ref, wa_ref, wb_ref, wo_ref, sk_ref,
             dy_ref, dyb_ref, dt_ref, doa_ref, dda_ref, dob0_ref, dob1_ref, dob2_ref, ddb0_ref, ddb1_ref, ddb2_ref,
             ya_ref, yb_ref, mg_ref, dbra_ref, dbrb_ref, loss_ref, dbm_ref, dsk_ref):
        i = pl.program_id(0)

        @pl.when(i == 0)
        def _():
            loss_ref[...] = jnp.zeros_like(loss_ref)
            dbm_ref[...] = jnp.zeros_like(dbm_ref)
            dsk_ref[...] = jnp.zeros_like(dsk_ref)

        ga = ga_ref[...]
        sa = _sigmoid(ga)
        silu_a = ga * sa
        oa = oa_ref[...]
        ya = oa * silu_a
        gb = gb_ref[...]
        sb = _sigmoid(gb)
        silu_b = gb * sb
        ob = [ob0_ref[...], ob1_ref[...], ob2_ref[...]]
        lb = [lb0_ref[...], lb1_ref[...], lb2_ref[...]]
        mx = jnp.maximum(jnp.maximum(lb[0], lb[1]), lb[2])
        ex = [jnp.exp(v - mx) for v in lb]
        den = ex[0] + ex[1] + ex[2]
        alpha = [e / den for e in ex]
        ybc = alpha[0] * ob[0] + alpha[1] * ob[1] + alpha[2] * ob[2]
        yb = ybc * silu_b
        yab = ya.astype(bf16)
        ybb = yb.astype(bf16)
        br_a = jnp.dot(yab, wa_ref[...], preferred_element_type=f32)
        br_b = jnp.dot(ybb, wb_ref[...], preferred_element_type=f32)
        g0 = _sigmoid(m0_ref[...] + bm_ref[0:1, :])
        g1 = _sigmoid(m1_ref[...] + bm_ref[1:2, :])
        merged = g0 * br_a + g1 * br_b
        mgb = merged.astype(bf16)
        y = x_ref[...] + jnp.dot(mgb, wo_ref[...], preferred_element_type=f32)
        err = y - t_ref[...]
        part = jnp.sum(jnp.sum(err * err, axis=1, keepdims=True), axis=0, keepdims=True)
        loss_ref[...] += part * (0.5 / D)
        dy = err * (1.0 / D)
        dyb = dy.astype(bf16)
        dmerged = lax.dot_general(dyb, wo_ref[...], NT_DIMS, preferred_element_type=f32)
        dbr_a = (dmerged * g0).astype(bf16)
        dbr_b = (dmerged * g1).astype(bf16)
        dm0 = dmerged * br_a * (g0 * (1.0 - g0))
        dm1 = dmerged * br_b * (g1 * (1.0 - g1))
        dbm_ref[0:1, :] += jnp.sum(dm0, axis=0, keepdims=True)
        dbm_ref[1:2, :] += jnp.sum(dm1, axis=0, keepdims=True)
        dya = lax.dot_general(dbr_a, wa_ref[...], NT_DIMS, preferred_element_type=f32)
        dyb2 = lax.dot_general(dbr_b, wb_ref[...], NT_DIMS, preferred_element_type=f32)
        do_a = dya * silu_a
        dga = dya * oa * (sa * (1.0 + ga * (1.0 - sa)))
        ones = _head_ones()
        delta_a = _seg_sum(do_a * oa, ones)
        dsk_ref[...] -= jnp.sum(delta_a * jnp.exp(sk_ref[...] - la_ref[...]), axis=0, keepdims=True)
        dybc = dyb2 * silu_b
        dgb = dyb2 * ybc * (sb * (1.0 + gb * (1.0 - sb)))
        dbar = _seg_sum(dybc * ybc, ones)
        dy_ref[...] = dy
        dyb_ref[...] = dyb
        dt_ref[:, 0:512] = dga.astype(bf16)
        dt_ref[:, 512:1024] = dgb.astype(bf16)
        dt_ref[:, 1024:2048] = dm0.astype(bf16)
        dt_ref[:, 2048:3072] = dm1.astype(bf16)
        doa_ref[...] = do_a.astype(bf16)
        dda_ref[...] = delta_a
        for k, (dob_ref, ddb_ref) in enumerate(((dob0_ref, ddb0_ref), (dob1_ref, ddb1_ref), (dob2_ref, ddb2_ref))):
            dob_ref[...] = alpha[k] * dybc
            ddb_ref[...] = alpha[k] * dbar
        ya_ref[...] = ya.T.astype(bf16)
        yb_ref[...] = yb.T.astype(bf16)
        mg_ref[...] = merged.T.astype(bf16)
        dbra_ref[...] = dbr_a
        dbrb_ref[...] = dbr_b

    def rows(n, blk=0):
        return pl.BlockSpec((ts, n), lambda i: (i, blk))

    def whole(r, c):
        return pl.BlockSpec((r, c), lambda i: (0, 0))

    def cols(n):
        return pl.BlockSpec((n, ts), lambda i: (0, i))

    def gate_cols(n, col):
        return pl.BlockSpec((pl.Element(ts), pl.Element(n)), lambda i: (i * ts, NA + col))

    outs = [
        ((S, D), f32, rows(D)), ((S, D), bf16, rows(D)), ((S, NW), bf16, gate_cols(NT, 0)),
        ((S, 512), bf16, rows(512)), ((S, 512), f32, rows(512)),
        ((S, 512), f32, rows(512)), ((S, 512), f32, rows(512)), ((S, 512), f32, rows(512)),
        ((S, 512), f32, rows(512)), ((S, 512), f32, rows(512)), ((S, 512), f32, rows(512)),
        ((512, S), bf16, cols(512)), ((512, S), bf16, cols(512)), ((D, S), bf16, cols(D)),
        ((S, D), bf16, rows(D)), ((S, D), bf16, rows(D)),
        ((1, 1), f32, whole(1, 1)), ((2, D), f32, whole(2, D)), ((1, 512), f32, whole(1, 512)),
    ]
    return pl.pallas_call(
        body,
        grid=(S // ts,),
        in_specs=[
            rows(D), rows(D), rows(512), rows(512), rows(512), rows(512), rows(512), rows(512), rows(512), rows(512),
            gate_cols(512, 0), gate_cols(512, 512), gate_cols(D, 1024), gate_cols(D, 2048), whole(2, D),
            whole(512, D), whole(512, D), whole(D, D), whole(1, 512),
        ],
        out_specs=[o[2] for o in outs],
        out_shape=[jax.ShapeDtypeStruct(o[0], o[1]) for o in outs],
        compiler_params=_params(("arbitrary",), vmem_mib=60),
        name="tail",
    )(x, tgt, o_a, l_a, *o_b, *l_b, proj, proj, proj, proj, bm, w_a, w_b, w_o, sink_b)


def _norm_bwd(xv, dyv, gain, ones):
    r = lax.rsqrt(_half_sums(xv * xv, ones) * (1.0 / HD) + EPS)
    yv = xv * r
    u = dyv * gain
    dxv = r * (u - yv * (_half_sums(u * yv, ones) * (1.0 / HD)))
    return dxv, jnp.sum(dyv * yv, axis=0, keepdims=True)


def _post_b(g, dqkv, proj_a, gains, dproj):
    d = GROUPS[g][1]
    seq = S // d

    def body(d_ref, pa_ref, pb_ref, g_ref, alias_ref, o_ref, dg_ref, nat_a, nat_b):
        del alias_ref
        pj = pl.program_id(0)
        kind = pj // 2
        q_scale = jnp.where(kind == 0, SCALE, 1.0)
        gain = g_ref[...] * q_scale
        ones = _head_ones()

        @pl.when(pj % 2 == 0)
        def _():
            dg_ref[...] = jnp.zeros_like(dg_ref)

        def columns(with_norm):
            for u, (p_ref, nat) in enumerate(((pa_ref, nat_a), (pb_ref, nat_b))):
                for c in range(d):
                    for i in range(seq // PCHUNK):
                        src = c * seq + i * PCHUNK
                        if d == 1:
                            idx = slice(src, src + PCHUNK)
                        else:
                            idx = pl.ds(c + i * PCHUNK * d, PCHUNK, stride=d)
                        dyv = d_ref[u, PAD + src:PAD + src + PCHUNK, :].astype(f32)
                        if with_norm:
                            dyv, dg = _norm_bwd(p_ref[idx, :], dyv, gain, ones)
                            dg_ref[...] += dg * q_scale
                        nat[idx, :] = dyv
                for i in range(S // CHUNK):
                    rows = slice(i * CHUNK, (i + 1) * CHUNK)
                    o_ref[rows, u * LANES:(u + 1) * LANES] = nat[rows, :].astype(bf16)

        pl.when(kind < 2)(lambda: columns(True))
        pl.when(kind == 2)(lambda: columns(False))

    def pcol(u):
        return pl.BlockSpec((S, LANES), lambda pj: (0, _col_block(g, 2 * jnp.minimum(pj, 3) + u)))

    return pl.pallas_call(
        body,
        grid=(6,),
        in_specs=[
            pl.BlockSpec((None, 2, SP, LANES), lambda pj: (pj // 2, pj % 2, 0, 0)),
            pcol(0), pcol(1),
            pl.BlockSpec((None, None, 1, LANES), lambda pj: (g, pj // 2, 0, 0)),
            pl.BlockSpec(memory_space=pl.ANY),
        ],
        out_specs=[
            pl.BlockSpec((S, 2 * LANES), lambda pj: (0, _col_block(g, 2 * pj) // 2)),
            pl.BlockSpec((None, 1, LANES), lambda pj: (pj // 2, 0, 0)),
        ],
        out_shape=[jax.ShapeDtypeStruct((S, NW), bf16), jax.ShapeDtypeStruct((3, 1, LANES), f32)],
        scratch_shapes=[pltpu.VMEM((S, LANES), f32), pltpu.VMEM((S, LANES), f32)],
        input_output_aliases={4: 0},
        compiler_params=_params(("arbitrary",)),
        name="post_b%d" % g,
    )(dqkv, proj_a, proj_a, gains, dproj)


def _post_a(dqkv, proj_a, gains, dproj):
    def body(q_ref, e_ref, p_ref, g_ref, alias_ref, o_ref, dg_ref):
        del alias_ref
        j = pl.program_id(0)
        q_scale = jnp.where(j < 4, SCALE, 1.0)
        gain = g_ref[...] * q_scale
        lo = _lo()
        ones = _head_ones()

        @pl.when((j == 0) | (j >= 4))
        def _():
            dg_ref[...] = jnp.zeros_like(dg_ref)

        def column(folded, with_norm):
            for i in range(S // PCHUNK):
                r0 = i * PCHUNK
                rows = slice(PAD + r0, PAD + r0 + PCHUNK)
                if folded:
                    t0 = e_ref[0, rows, :].astype(f32) + e_ref[1, rows, :].astype(f32)
                    t1 = e_ref[2, rows, :].astype(f32) + e_ref[3, rows, :].astype(f32)
                    dyv = jnp.where(lo, t0 + pltpu.roll(t0, HD, 1), t1 + pltpu.roll(t1, HD, 1))
                else:
                    dyv = q_ref[rows, :].astype(f32)
                if with_norm:
                    dyv, dg = _norm_bwd(p_ref[r0:r0 + PCHUNK, :], dyv, gain, ones)
                    dg_ref[...] += dg * q_scale
                o_ref[r0:r0 + PCHUNK, :] = dyv.astype(bf16)

        pl.when(j < 4)(lambda: column(False, True))
        pl.when(j == 4)(lambda: column(True, True))
        pl.when(j == 5)(lambda: column(True, False))

    return pl.pallas_call(
        body,
        grid=(6,),
        in_specs=[
            pl.BlockSpec((None, None, SP, LANES), lambda j: (0, jnp.minimum(j, 3), 0, 0)),
            pl.BlockSpec((None, 4, SP, LANES), lambda j: (jnp.clip(j - 3, 1, 2), 0, 0, 0)),
            pl.BlockSpec((S, LANES), lambda j: (0, jnp.minimum(j, 4))),
            pl.BlockSpec((None, None, 1, LANES), lambda j: (0, jnp.maximum(j - 3, 0), 0, 0)),
            pl.BlockSpec(memory_space=pl.ANY),
        ],
        out_specs=[
            pl.BlockSpec((S, LANES), lambda j: (0, j)),
            pl.BlockSpec((None, 1, LANES), lambda j: (jnp.maximum(j - 3, 0), 0, 0)),
        ],
        out_shape=[jax.ShapeDtypeStruct((S, NW), bf16), jax.ShapeDtypeStruct((3, 1, LANES), f32)],
        input_output_aliases={4: 0},
        compiler_params=_params(("arbitrary",)),
        name="post_a",
    )(dqkv, dqkv, proj_a, gains, dproj)


def _dh_norm_bwd(dproj, w, x, rstd, gain, dy):
    ts = 1024
    tk = NW // 3
    nk = NW // tk

    def body(d_ref, w_ref, x_ref, r_ref, g_ref, dy_ref, gx_ref, dgn_ref, acc):
        i = pl.program_id(0)
        k = pl.program_id(1)

        @pl.when((i == 0) & (k == 0))
        def _():
            dgn_ref[...] = jnp.zeros_like(dgn_ref)

        @pl.when(k == 0)
        def _():
            acc[...] = jnp.zeros_like(acc)

        acc[...] += jnp.dot(d_ref[...], w_ref[...], preferred_element_type=f32)

        @pl.when(k == nk - 1)
        def _():
            dh = acc[...]
            xh = x_ref[...] * r_ref[...]
            u = dh * g_ref[...]
            dx = r_ref[...] * (u - xh * jnp.mean(u * xh, axis=-1, keepdims=True))
            gx_ref[...] = dy_ref[...] + dx
            dgn_ref[...] += jnp.sum(dh * xh, axis=0, keepdims=True)

    return pl.pallas_call(
        body,
        grid=(S // ts, nk),
        in_specs=[
            pl.BlockSpec((ts, tk), lambda i, k: (i, k)),
            pl.BlockSpec((tk, D), lambda i, k: (k, 0)),
            pl.BlockSpec((ts, D), lambda i, k: (i, 0)),
            pl.BlockSpec((ts, 1), lambda i, k: (i, 0)),
            pl.BlockSpec((1, D), lambda i, k: (0, 0)),
            pl.BlockSpec((ts, D), lambda i, k: (i, 0)),
        ],
        out_specs=[pl.BlockSpec((ts, D), lambda i, k: (i, 0)), pl.BlockSpec((1, D), lambda i, k: (0, 0))],
        out_shape=[jax.ShapeDtypeStruct((S, D), f32), jax.ShapeDtypeStruct((1, D), f32)],
        scratch_shapes=[pltpu.VMEM((ts, D), f32)],
        compiler_params=_params(("arbitrary", "arbitrary"), vmem_mib=60),
        name="dh_norm_bwd",
    )(dproj, w, x, rstd, gain, dy)


def _dw_in(hbt, dproj, parity, name):
    tk = S
    win = WSH + 96

    def body(par_ref, a_ref, b_ref, o_ref, acc):
        p = 2 * pl.program_id(0) + par_ref[0]
        k = pl.program_id(1)

        @pl.when(k == 0)
        def _():
            acc[...] = jnp.zeros_like(acc)

        acc[...] += jnp.dot(a_ref[...], b_ref[...], preferred_element_type=f32)

        @pl.when(k == S // tk - 1)
        def _():
            acc_t = acc[...].T
            for pp in range(NDEV):
                off = (WSH * pp) % LANES

                @pl.when(p == pp)
                def _():
                    o_ref[...] = acc_t[off:off + WSH, :].astype(bf16)

    return pl.pallas_call(
        body,
        grid_spec=pltpu.PrefetchScalarGridSpec(
            num_scalar_prefetch=1,
            grid=(NDEV // 2, S // tk),
            in_specs=[
                pl.BlockSpec((D, tk), lambda q, k, par: (0, k)),
                pl.BlockSpec((pl.Element(tk), pl.Element(win)),
                             lambda q, k, par: (k * tk, (WSH * (2 * q + par[0])) // LANES * LANES)),
            ],
            out_specs=pl.BlockSpec((None, WSH, D), lambda q, k, par: (q, 0, 0)),
            scratch_shapes=[pltpu.VMEM((D, win), f32)],
        ),
        out_shape=jax.ShapeDtypeStruct((NDEV // 2, WSH, D), bf16),
        compiler_params=_params(("arbitrary", "arbitrary")),
        name=name,
    )(parity, hbt, dproj)


def _matmul_tokens(at, b, name):
    m, n = at.shape[0], b.shape[1]
    tn = 1024
    tk = 2048

    def body(a_ref, b_ref, o_ref):
        @pl.when(pl.program_id(1) == 0)
        def _():
            o_ref[...] = jnp.zeros_like(o_ref)

        o_ref[...] += jnp.dot(a_ref[...], b_ref[...], preferred_element_type=f32)

    return pl.pallas_call(
        body,
        grid=(n // tn, S // tk),
        in_specs=[pl.BlockSpec((m, tk), lambda j, k: (0, k)), pl.BlockSpec((tk, tn), lambda j, k: (k, j))],
        out_specs=pl.BlockSpec((m, tn), lambda j, k: (0, j)),
        out_shape=jax.ShapeDtypeStruct((m, n), f32),
        compiler_params=_params(("arbitrary", "arbitrary")),
        name=name,
    )(at, b)


def _exchange(scatter, gather, name):
    arrs = list(scatter) + list(gather)
    n = len(arrs)
    ns = len(scatter)

    def body(*refs):
        ins, outs = refs[:n], refs[n:2 * n]
        send_sems, recv_sems, local_sems = refs[2 * n:]
        x, y, c = lax.axis_index("x"), lax.axis_index("y"), lax.axis_index("c")
        me = 4 * x + 2 * y + c
        local, remote = [], []
        for a in range(n):
            lc = pltpu.make_async_copy(ins[a].at[me] if a < ns else ins[a], outs[a].at[me], local_sems.at[a])
            lc.start()
            local.append(lc)
            for r in range(1, NDEV):
                px = 1 - x if r & 4 else x
                py = 1 - y if r & 2 else y
                pc = 1 - c if r & 1 else c
                cp = pltpu.make_async_remote_copy(
                    src_ref=ins[a].at[4 * px + 2 * py + pc] if a < ns else ins[a],
                    dst_ref=outs[a].at[me],
                    send_sem=send_sems.at[a, r - 1],
                    recv_sem=recv_sems.at[a, r - 1],
                    device_id=(px, py, pc),
                    device_id_type=pl.DeviceIdType.MESH,
                )
                cp.start()
                remote.append(cp)
        for cp in remote:
            cp.wait_recv()
        for cp in remote:
            cp.wait_send()
        for lc in local:
            lc.wait()

    out_shape = [jax.ShapeDtypeStruct(a.shape if i < ns else (NDEV,) + a.shape, a.dtype) for i, a in enumerate(arrs)]
    return pl.pallas_call(
        body,
        in_specs=[pl.BlockSpec(memory_space=pl.ANY)] * n,
        out_specs=[pl.BlockSpec(memory_space=pl.ANY)] * n,
        out_shape=out_shape,
        scratch_shapes=[
            pltpu.SemaphoreType.DMA((n, NDEV - 1)),
            pltpu.SemaphoreType.DMA((n, NDEV - 1)),
            pltpu.SemaphoreType.DMA((n,)),
        ],
        compiler_params=pltpu.CompilerParams(has_side_effects=True),
        name=name,
    )(*arrs)


_HBM = pl.BlockSpec(memory_space=pltpu.HBM)
_SEM = pl.BlockSpec(memory_space=pltpu.SEMAPHORE)
_EFFECT = pltpu.SideEffectType.DATAFLOW_SIDE_EFFECTING


def _comm_step(name, body_fn, lands, srcs=(), wait_sems=(), n_new=0, after=(), token=False):
    n, ns, nw, na = len(lands), len(srcs), len(wait_sems), len(after)

    def body(*refs):
        src, land = refs[:ns], refs[ns:ns + n]
        waits = refs[ns + n:ns + n + nw]
        new = refs[ns + n + nw + na:ns + n + nw + na + n_new]
        body_fn(src, land, waits, new)
        if token:
            refs[-1][...] = jnp.zeros((8, LANES), f32)

    hbm = [pltpu.HBM(a.shape, a.dtype) for a in lands]
    ops = [pltpu.with_memory_space_constraint(a, pltpu.HBM) for a in list(srcs) + list(lands)]
    extra_shape = [jax.ShapeDtypeStruct((8, LANES), f32)] if token else []
    extra_spec = [pl.BlockSpec(memory_space=pltpu.VMEM)] if token else []
    outs = pl.pallas_call(
        body,
        out_shape=tuple([pltpu.SemaphoreType.DMA(())] * n_new + hbm + extra_shape),
        in_specs=[_HBM] * (ns + n) + [_SEM] * nw + [pl.BlockSpec(memory_space=pl.ANY)] * na,
        out_specs=tuple([_SEM] * n_new + [_HBM] * n + extra_spec),
        input_output_aliases={ns + i: n_new + i for i in range(n)},
        compiler_params=pltpu.CompilerParams(has_side_effects=_EFFECT),
        name=name,
    )(*ops, *wait_sems, *after)
    if token:
        return list(outs[:n_new]), list(outs[n_new:n_new + n]), outs[-1][0, 0]
    return list(outs[:n_new]), list(outs[n_new:])


class _GatheredWeights:
    def __init__(self, shards):
        self.n = n = len(shards)
        x, y, c = lax.axis_index("x"), lax.axis_index("y"), lax.axis_index("c")
        self.x = x
        me = 4 * x + 2 * y + c
        lands = [lax.dynamic_update_slice(lax.empty((NDEV,) + s.shape, s.dtype), s[None], (me,) + (0,) * s.ndim)
                 for s in shards]

        def start_own(src, land, waits, new):
            p = self._peers()
            for a in range(n):
                for k, to in ((0, p["sibling"]), (1, p["xn"]), (2, p["yn"])):
                    self._copy(land[a], new, a, k, 3, p["me"], to).start()

        self.sems, self.lands = {}, None
        new, self.lands = _comm_step("gather_start", start_own, lands, n_new=6 * n)
        self._keep(new, (0, 1, 2))

    @staticmethod
    def _peers():
        x, y, c = lax.axis_index("x"), lax.axis_index("y"), lax.axis_index("c")
        return dict(
            me=(x, y, c), sibling=(x, y, 1 - c), xn=(1 - x, y, c), yn=(x, 1 - y, c), dg=(1 - x, 1 - y, c),
            relay_origin=(jnp.bitwise_xor(x, c), jnp.bitwise_xor(y, 1 - c), c),
            relay_target=(jnp.bitwise_xor(x, 1 - c), jnp.bitwise_xor(y, c), c))

    def _keep(self, new, ks):
        half = len(new) // 2
        i = 0
        for a in range(self.n):
            for k in ks:
                self.sems[a, k] = (new[i], new[half + i])
                i += 1

    @staticmethod
    def _copy(land, sem_refs, a, k, nk, block, to, src=None, ks=None):
        ks = tuple(range(nk)) if ks is None else ks
        half = len(sem_refs) // 2
        i = a * len(ks) + ks.index(k)
        slot = land.at[4 * block[0] + 2 * block[1] + block[2]]
        return pltpu.make_async_remote_copy(
            src_ref=slot if src is None else src, dst_ref=slot, send_sem=sem_refs[i], recv_sem=sem_refs[half + i],
            device_id=to, device_id_type=pl.DeviceIdType.MESH)

    def _sem_list(self, ks):
        return ([self.sems[a, k][0] for a in range(self.n) for k in ks]
                + [self.sems[a, k][1] for a in range(self.n) for k in ks])

    def first_half(self, after):
        n = self.n

        def relay(src, land, waits, new):
            p = self._peers()
            for a in range(n):
                self._copy(land[a], waits, a, 1, 0, p["xn"], p["me"], ks=(1, 2)).wait_recv()
                self._copy(land[a], waits, a, 2, 0, p["yn"], p["me"], ks=(1, 2)).wait_recv()
                self._copy(land[a], new, a, 3, 0, p["relay_origin"], p["relay_target"], ks=(3, 4, 5)).start()
                self._copy(land[a], new, a, 4, 0, p["xn"], p["sibling"], ks=(3, 4, 5)).start()
                self._copy(land[a], new, a, 5, 0, p["yn"], p["sibling"], ks=(3, 4, 5)).start()

        new, self.lands = _comm_step("gather_relay", relay, self.lands, wait_sems=self._sem_list((1, 2)),
                                     n_new=6 * n, after=after)
        self._keep(new, (3, 4, 5))

        def from_sibling(src, land, waits, new):
            p = self._peers()
            other = lambda b: (b[0], b[1], 1 - b[2])
            for a in range(n):
                self._copy(land[a], waits, a, 0, 0, other(p["me"]), p["me"], ks=(0, 4, 5)).wait_recv()
                self._copy(land[a], waits, a, 4, 0, other(p["xn"]), p["me"], ks=(0, 4, 5)).wait_recv()
                self._copy(land[a], waits, a, 5, 0, other(p["yn"]), p["me"], ks=(0, 4, 5)).wait_recv()

        _, self.lands = _comm_step("gather_wait_sibling", from_sibling, self.lands,
                                   wait_sems=self._sem_list((0, 4, 5)))
        return self.lands[0].reshape(NW, D), self.x.astype(jnp.int32).reshape(1)

    def second_half(self, after):
        n = self.n

        def forward_diagonal(src, land, waits, new):
            p = self._peers()
            for a in range(n):
                self._copy(land[a], waits, a, 3, 0, p["dg"], p["me"], ks=(3,)).wait_recv()
                self._copy(land[a], new, a, 6, 0, p["dg"], p["sibling"], ks=(6,)).start()

        new, self.lands = _comm_step("gather_forward_diagonal", forward_diagonal, self.lands,
                                     wait_sems=self._sem_list((3,)), n_new=2 * n, after=after)
        self._keep(new, (6,))

        def finish(src, land, waits, new):
            p = self._peers()
            ks = tuple(range(7))
            for a in range(n):
                self._copy(land[a], waits, a, 6, 0, (p["dg"][0], p["dg"][1], 1 - p["dg"][2]), p["me"], ks=ks).wait_recv()
                for k in ks:
                    self._copy(land[a], waits, a, k, 0, p["me"], p["me"], ks=ks).wait_send()

        _, self.lands = _comm_step("gather_finish", finish, self.lands, wait_sems=self._sem_list(tuple(range(7))))
        return self.lands[0].reshape(NW, D), (1 - self.x).astype(jnp.int32).reshape(1)

    def rest(self):
        g_a, g_b, g_o, g_bm = self.lands[1:]
        return (g_a.transpose(1, 0, 2).reshape(512, D), g_b.transpose(1, 0, 2).reshape(512, D),
                g_bm.transpose(1, 0, 2).reshape(2, D), g_o.reshape(D, D))


def _sibling_send_start(shares):
    landing = lax.empty(shares.shape, shares.dtype)

    def start(src, land, waits, new):
        x, y, c = lax.axis_index("x"), lax.axis_index("y"), lax.axis_index("c")
        pltpu.make_async_remote_copy(src_ref=land[0], dst_ref=land[1], send_sem=new[0], recv_sem=new[1],
                                     device_id=(x, y, 1 - c), device_id_type=pl.DeviceIdType.MESH).start()

    return _comm_step("grad_sibling_start", start, [shares, landing], n_new=2, token=True)


def _sibling_send_wait(sems, lands, after):
    def wait(src, land, waits, new):
        x, y, c = lax.axis_index("x"), lax.axis_index("y"), lax.axis_index("c")
        done = pltpu.make_async_remote_copy(src_ref=land[0], dst_ref=land[1], send_sem=waits[0], recv_sem=waits[1],
                                            device_id=(x, y, c), device_id_type=pl.DeviceIdType.MESH)
        done.wait_send()
        done.wait_recv()

    _, lands = _comm_step("grad_sibling_wait", wait, lands, wait_sems=sems, after=after)
    return lands[1]


def _row_tile(rows, limit=256):
    fits = [t for t in range(16, limit + 1, 16) if rows % t == 0]
    return fits[-1] if fits else rows


def _pair_sum(mine, theirs, name):
    nb, rows, cols = mine.shape
    tr = _row_tile(rows, 528)

    def body(a_ref, b_ref, o_ref):
        o_ref[...] = (a_ref[...].astype(f32) + b_ref[...].astype(f32)).astype(bf16)

    blk = pl.BlockSpec((None, tr, cols), lambda q, i: (q, i, 0))
    return pl.pallas_call(
        body,
        grid=(nb, rows // tr),
        in_specs=[blk, blk],
        out_specs=blk,
        out_shape=jax.ShapeDtypeStruct(mine.shape, bf16),
        compiler_params=_params(("arbitrary", "arbitrary")),
        name=name,
    )(mine, theirs)


def _scatter_start(chip_arrs, all_arrs, name):
    arrs = list(chip_arrs) + list(all_arrs)
    n, nc = len(arrs), len(chip_arrs)
    lands = [lax.empty(((3 if i < nc else NDEV - 1),) + a.shape[1:], a.dtype) for i, a in enumerate(arrs)]

    def body(*refs):
        src, land = refs[:n], refs[n:2 * n]
        send_sems, recv_sems = refs[2 * n:3 * n], refs[3 * n:4 * n]
        token = refs[6 * n]
        x, y, c = lax.axis_index("x"), lax.axis_index("y"), lax.axis_index("c")
        for a in range(n):
            for r in range(1, 4 if a < nc else NDEV):
                if a < nc:
                    px, py, pc = (1 - x if r & 2 else x), (1 - y if r & 1 else y), c
                    block = 2 * px + py
                else:
                    px, py, pc = (1 - x if r & 4 else x), (1 - y if r & 2 else y), (1 - c if r & 1 else c)
                    block = 4 * px + 2 * py + pc
                pltpu.make_async_remote_copy(
                    src_ref=src[a].at[block], dst_ref=land[a].at[r - 1], send_sem=send_sems[a],
                    recv_sem=recv_sems[a], device_id=(px, py, pc), device_id_type=pl.DeviceIdType.MESH).start()
        token[...] = jnp.zeros_like(token)

    hbm = [pltpu.HBM(a.shape, a.dtype) for a in arrs + lands]
    ops = [pltpu.with_memory_space_constraint(a, pltpu.HBM) for a in arrs + lands]
    outs = pl.pallas_call(
        body,
        out_shape=tuple([pltpu.SemaphoreType.DMA(())] * (2 * n) + hbm + [jax.ShapeDtypeStruct((8, LANES), f32)]),
        in_specs=[_HBM] * (2 * n),
        out_specs=tuple([_SEM] * (2 * n) + [_HBM] * (2 * n) + [pl.BlockSpec(memory_space=pltpu.VMEM)]),
        input_output_aliases={i: 2 * n + i for i in range(2 * n)},
        compiler_params=pltpu.CompilerParams(has_side_effects=_EFFECT),
        name=name,
    )(*ops)
    return outs[:n], outs[n:2 * n], outs[2 * n:3 * n], outs[3 * n:4 * n], outs[4 * n]


def _scatter_wait(send_sems, recv_sems, srcs, lands, after, name):
    n = len(srcs)

    def body(*refs):
        land = refs[n:2 * n]
        ssem, rsem = refs[2 * n:3 * n], refs[3 * n:4 * n]
        x, y, c = lax.axis_index("x"), lax.axis_index("y"), lax.axis_index("c")
        for a in range(n):
            done = pltpu.make_async_remote_copy(
                src_ref=land[a], dst_ref=land[a], send_sem=ssem[a], recv_sem=rsem[a], device_id=(x, y, c),
                device_id_type=pl.DeviceIdType.MESH)
            done.wait_send()
            done.wait_recv()

    hbm = [pltpu.HBM(a.shape, a.dtype) for a in list(srcs) + list(lands)]
    outs = pl.pallas_call(
        body,
        out_shape=tuple(hbm),
        in_specs=[_HBM] * (2 * n) + [_SEM] * (2 * n) + [pl.BlockSpec(memory_space=pl.ANY)],
        out_specs=tuple([_HBM] * (2 * n)),
        input_output_aliases={i: i for i in range(2 * n)},
        compiler_params=pltpu.CompilerParams(has_side_effects=_EFFECT),
        name=name,
    )(*srcs, *lands, *send_sems, *recv_sems, after)
    return outs[:n], outs[n:]


def _adam_update(g, w_ref, m_ref, v_ref, g_ref, d_ref, nm_ref, nv_ref):
    mm = ADAM_B1 * m_ref[...] + (1.0 - ADAM_B1) * g
    vv = ADAM_B2 * v_ref[...] + (1.0 - ADAM_B2) * (g * g)
    m_hat = mm / (1.0 - ADAM_B1 ** ADAM_STEP)
    v_hat = vv / (1.0 - ADAM_B2 ** ADAM_STEP)
    g_ref[...] = g
    d_ref[...] = -ADAM_LR * (m_hat / (jnp.sqrt(v_hat) + ADAM_EPS) + ADAM_WD * w_ref[...])
    nm_ref[...] = mm
    nv_ref[...] = vv


def _adamw_own(w, own, own_idx, slots, m, v, name):
    r, c = w.shape[-2:]
    tr = _row_tile(r, 384)
    k = slots.shape[0]

    def body(i_ref, w_ref, o_ref, s_ref, m_ref, v_ref, g_ref, d_ref, nm_ref, nv_ref):
        del i_ref
        g = o_ref[...].astype(f32)
        for j in range(k):
            g = g + s_ref[j].astype(f32)
        _adam_update(g, w_ref, m_ref, v_ref, g_ref, d_ref, nm_ref, nv_ref)

    blk = pl.BlockSpec((None, tr, c), lambda i, ix: (0, i, 0))
    return pl.pallas_call(
        body,
        grid_spec=pltpu.PrefetchScalarGridSpec(
            num_scalar_prefetch=1,
            grid=(r // tr,),
            in_specs=[blk, pl.BlockSpec((None, tr, c), lambda i, ix: (ix[0], i, 0)),
                      pl.BlockSpec((k, tr, c), lambda i, ix: (0, i, 0)), blk, blk],
            out_specs=[blk] * 4,
        ),
        out_shape=[jax.ShapeDtypeStruct(w.shape, f32)] * 4,
        compiler_params=_params(("arbitrary",)),
        name=name,
    )(own_idx, w, own, slots, m, v)


def _adamw(w, slots, m, v, name):
    r, c = w.shape[-2:]
    tr = _row_tile(r, 128)

    def body(w_ref, s_ref, m_ref, v_ref, g_ref, d_ref, nm_ref, nv_ref):
        g = s_ref[0].astype(f32)
        for k in range(1, NDEV):
            g = g + s_ref[k].astype(f32)
        _adam_update(g, w_ref, m_ref, v_ref, g_ref, d_ref, nm_ref, nv_ref)

    if w.ndim == 3:
        blk = pl.BlockSpec((None, tr, c), lambda i: (0, i, 0))
    else:
        blk = pl.BlockSpec((tr, c), lambda i: (i, 0))
    return pl.pallas_call(
        body,
        grid=(r // tr,),
        in_specs=[blk, pl.BlockSpec((NDEV, tr, c), lambda i: (0, i, 0)), blk, blk],
        out_specs=[blk] * 4,
        out_shape=[jax.ShapeDtypeStruct(w.shape, f32)] * 4,
        compiler_params=_params(("arbitrary",)),
        name=name,
    )(w, slots, m, v)


class _Weights:
    def __init__(self, w_t, w_a, w_b, b_merge, w_o):
        self._w_t, self._rest = w_t, (w_a, w_b, b_merge, w_o)

    def first_half(self, after):
        del after
        return self._w_t, jnp.zeros((1,), jnp.int32)

    def second_half(self, after):
        del after
        return self._w_t, jnp.ones((1,), jnp.int32)

    def rest(self):
        return self._rest


def _local_step(x, tgt, norm_gain, weights, qn_a, kn_a, qn_b, kn_b, sink_a, rel_bias, on_weight_grads=None,
                core=None):
    two = lambda t: jnp.concatenate([t, t], axis=-1).reshape(1, LANES)
    ones = jnp.ones((1, LANES), f32)
    gains = jnp.stack([
        jnp.stack([two(qn_a), two(kn_a), ones]),
        jnp.stack([two(qn_b), two(kn_b), ones]),
        jnp.stack([two(qn_b), two(kn_b), ones]),
        jnp.stack([two(qn_b), two(kn_b), ones]),
    ])
    buckets = [jnp.asarray(_bucket_np(blk, d)) for blk, d, _ in GROUPS]
    bias = [_bias_expand(rel_bias, buckets[k], GROUPS[k][2], "bias_expand_%d" % k) for k in range(4)]

    hb, hbt, rstd = _rms(x, norm_gain)
    w_t, half = weights.first_half([hb] + bias)
    proj = _inproj_half(hb, w_t, half, None, "inproj_1")
    w_t, half = weights.second_half([proj])
    proj = _inproj_half(hb, w_t, half, proj, "inproj_2")
    w_a, w_b, b_merge, w_o = weights.rest()
    gl = _prep(proj, gains)
    o_a, l_a = _attn_fwd(gl, bias[0], sink_a.reshape(8), 0, 128, 1, "attn_fwd_a")
    fwd_b = [_attn_fwd(gl, bias[k], None, k, GROUPS[k][0], GROUPS[k][1], "attn_fwd_b%d" % k) for k in (1, 2, 3)]
    sink_b = jnp.repeat(sink_a.reshape(8), HD).reshape(1, 512)

    (dy, dyb, dproj, do_a, dd_a, do_b0, do_b1, do_b2, dd_b0, dd_b1, dd_b2, ya, yb, mg, dbr_a, dbr_b, loss, dbm,
     dsk) = _tail(x, tgt, o_a, l_a, [f[0] for f in fwd_b], [f[1] for f in fwd_b], proj, b_merge, w_a, w_b, w_o, sink_b)

    dw_o = _matmul_tokens(mg, dyb, "dw_out")
    dw_a = _matmul_tokens(ya, dbr_a, "dw_branch_a")
    dw_b = _matmul_tokens(yb, dbr_b, "dw_branch_b")
    if on_weight_grads is not None:
        early = on_weight_grads(dict(w_branch_a=dw_a, w_branch_b=dw_b, b_merge=dbm, w_out=dw_o))
        buckets = [buckets[0] + early.astype(jnp.int32)] + buckets[1:]

    dqkv_a, dbk_a = _attn_bwd(gl, bias[0], buckets[0], do_a, l_a, dd_a, 0, 128, 1, "attn_bwd_a")
    dproj, dg_a = _post_a(dqkv_a, proj, gains, dproj)
    dbk_b, dg_b = [], []
    for k, do_k, dd_k in ((1, do_b0, dd_b0), (2, do_b1, dd_b1), (3, do_b2, dd_b2)):
        dqkv, dbk = _attn_bwd(gl, bias[k], buckets[k], do_k, fwd_b[k - 1][1], dd_k, k, GROUPS[k][0], GROUPS[k][1],
                              "attn_bwd_b%d" % k)
        dproj, dg = _post_b(k, dqkv, proj, gains, dproj)
        dbk_b.append(dbk)
        dg_b.append(dg)
    dg_b = jnp.stack(dg_b)

    core = jnp.zeros((1,), jnp.int32) if core is None else core
    dw_other = _dw_in(hbt, dproj, 1 - core, "dw_in_other")
    sent = jnp.zeros((), f32) if on_weight_grads is None else on_weight_grads(dict(w_in_other=dw_other))
    dw_in = _dw_in(hbt, dproj, core + sent.astype(jnp.int32), "dw_in_own")
    token = jnp.zeros((), f32) if on_weight_grads is None else on_weight_grads(dict(w_in=dw_in))
    grad_x, d_norm_gain = _dh_norm_bwd(dproj, w_t, x, rstd, norm_gain + token, dy)

    fold = lambda t: t[..., :HD] + t[..., HD:]
    d_qn_a = fold(dg_a[0, 0])
    d_kn_a = fold(dg_a[1, 0])
    d_qn_b = fold(dg_b[:, 0, 0].sum(axis=0))
    d_kn_b = fold(dg_b[:, 1, 0].sum(axis=0))
    d_sink = dsk.reshape(8, HD)[:, 0]
    red = jnp.stack([dbk_a] + dbk_b)
    d_rel = red[:, :, 0, :32].reshape(32, 32).T
    return dict(loss=loss, grad_x=grad_x, norm_gain=d_norm_gain, w_in=dw_in, w_in_other=dw_other, q_norm_a=d_qn_a,
                k_norm_a=d_kn_a,
                q_norm_b=d_qn_b, k_norm_b=d_kn_b, sink_a=d_sink, rel_bias=d_rel, w_branch_a=dw_a, w_branch_b=dw_b,
                b_merge=dbm, w_out=dw_o)


SMALL = (("norm_gain", D), ("q_norm_a", HD), ("k_norm_a", HD), ("q_norm_b", HD), ("k_norm_b", HD), ("sink_a", 8),
         ("rel_bias", 1024))
SMALL_PAD = 2432


SMALL_USED = sum(sz for _, sz in SMALL)


def _pack_small(parts, loss=None):
    tail = jnp.zeros((SMALL_PAD - SMALL_USED,), f32)
    if loss is not None:
        tail = tail.at[0].set(loss.reshape(()))
    return jnp.concatenate([parts[n].reshape(-1) for n, _ in SMALL] + [tail]).reshape(1, SMALL_PAD)


def _unpack_small(flat, shapes):
    out, off = {}, 0
    for n, sz in SMALL:
        out[n] = flat[0, off:off + sz].reshape(shapes[n])
        off += sz
    return out


def kernel(x, norm_gain, w_in, q_norm_a, k_norm_a, q_norm_b, k_norm_b, sink_a, rel_bias, w_branch_a, w_branch_b, b_merge, w_out, loss_target, m_norm_gain, m_w_in, m_q_norm_a, m_k_norm_a, m_q_norm_b, m_k_norm_b, m_sink_a, m_rel_bias, m_w_branch_a, m_w_branch_b, m_b_merge, m_w_out, v_norm_gain, v_w_in, v_q_norm_a, v_k_norm_a, v_q_norm_b, v_k_norm_b, v_sink_a, v_rel_bias, v_w_branch_a, v_w_branch_b, v_b_merge, v_w_out):
    csh = D // NDEV
    w_in_t, m_w_in_t, v_w_in_t = (jnp.swapaxes(t, 1, 2) for t in (w_in, m_w_in, v_w_in))
    weights = _GatheredWeights([w_in_t[0].astype(bf16), w_branch_a[0].astype(bf16), w_branch_b[0].astype(bf16),
                                w_out[0].astype(bf16), b_merge[0]])

    pending = {}
    core = lax.axis_index("c").astype(jnp.int32).reshape(1)
    chip = (2 * lax.axis_index("x") + lax.axis_index("y")).astype(jnp.int32).reshape(1)
    me = (2 * chip + core).astype(jnp.int32)

    def start_exchange(gw):
        if "w_in_other" in gw:
            sems, lands, sent = _sibling_send_start(gw["w_in_other"])
            pending["sibling"] = (sems, lands)
            return sent
        if "w_in" in gw:
            from_sibling = _sibling_send_wait(*pending["sibling"], after=[gw["w_in"]])
            chip_sums = _pair_sum(gw["w_in"], from_sibling, "grad_pair_sum")
            pending["w_in"] = _scatter_start([chip_sums], [], "scatter_w_in_start")
            return pending["w_in"][4][0, 0]
        blocks = [gw["w_branch_a"].reshape(512, NDEV, csh).transpose(1, 0, 2).astype(bf16),
                  gw["w_branch_b"].reshape(512, NDEV, csh).transpose(1, 0, 2).astype(bf16),
                  gw["w_out"].reshape(NDEV, csh, D).astype(bf16),
                  gw["b_merge"].reshape(2, NDEV, csh).transpose(1, 0, 2)]
        pending["rest"] = _scatter_start([], blocks, "scatter_rest_start")
        return pending["rest"][4][0, 0]

    loc = _local_step(x[0], loss_target[0], norm_gain, weights, q_norm_a, k_norm_a, q_norm_b, k_norm_b, sink_a,
                      rel_bias, on_weight_grads=start_exchange, core=core)

    small_shapes = dict(norm_gain=(1, D), q_norm_a=(1, HD), k_norm_a=(1, HD), q_norm_b=(1, HD), k_norm_b=(1, HD),
                        sink_a=(1, 8), rel_bias=(32, 32))
    (r_small,) = _exchange([], [_pack_small(loc, loc["loss"])], "gather_small_grads")
    send_sems, recv_sems, srcs, lands, _ = pending["rest"]
    (s_a, s_b, s_o, s_bm), (r_a, r_b, r_o, r_bm) = _scatter_wait(
        send_sems, recv_sems, srcs, lands, r_small, "scatter_rest_wait")
    send_sems, recv_sems, srcs, lands, _ = pending["w_in"]
    (s_in,), (r_in,) = _scatter_wait(send_sems, recv_sems, srcs, lands, r_small, "scatter_w_in_wait")

    given = dict(norm_gain=norm_gain, q_norm_a=q_norm_a, k_norm_a=k_norm_a, q_norm_b=q_norm_b, k_norm_b=k_norm_b,
                 sink_a=sink_a, rel_bias=rel_bias)
    m_small = dict(norm_gain=m_norm_gain, q_norm_a=m_q_norm_a, k_norm_a=m_k_norm_a, q_norm_b=m_q_norm_b,
                   k_norm_b=m_k_norm_b, sink_a=m_sink_a, rel_bias=m_rel_bias)
    v_small = dict(norm_gain=v_norm_gain, q_norm_a=v_q_norm_a, k_norm_a=v_k_norm_a, q_norm_b=v_q_norm_b,
                   k_norm_b=v_k_norm_b, sink_a=v_sink_a, rel_bias=v_rel_bias)
    res = {
        "small": _adamw(_pack_small(given), r_small, _pack_small(m_small), _pack_small(v_small), "adamw_small"),
        "w_in": [jnp.swapaxes(t, 1, 2) for t in
                 _adamw_own(w_in_t, s_in, chip, r_in, m_w_in_t, v_w_in_t, "adamw_w_in")],
        "w_branch_a": _adamw_own(w_branch_a, s_a, me, r_a, m_w_branch_a, v_w_branch_a, "adamw_w_branch_a"),
        "w_branch_b": _adamw_own(w_branch_b, s_b, me, r_b, m_w_branch_b, v_w_branch_b, "adamw_w_branch_b"),
        "b_merge": _adamw_own(b_merge, s_bm, me, r_bm, m_b_merge, v_b_merge, "adamw_b_merge"),
        "w_out": _adamw_own(w_out, s_o, me, r_o, m_w_out, v_w_out, "adamw_w_out"),
    }
    order = ["norm_gain", "w_in", "q_norm_a", "k_norm_a", "q_norm_b", "k_norm_b", "sink_a", "rel_bias", "w_branch_a",
             "w_branch_b", "b_merge", "w_out"]
    outs = []
    for k in range(4):
        small = _unpack_small(res["small"][k], small_shapes)
        for n in order:
            outs.append(small[n] if n in small else res[n][k])
    loss = res["small"][0][0, SMALL_USED]
    return (loss, loc["grad_x"][None], *outs)
```
